```python
import jax, jax.numpy as jnp
from jax import lax
import numpy as np

D_MODEL = 1024
BATCH = 8
SEQ = 4096
DEPTH = 1

D_MIX = D_MODEL
D_A = D_MIX // 2
D_B = D_MIX - D_A
HEADS_A = 4
HEAD_DIM_A = D_A // HEADS_A
HEADS_B = 8
CHUNK = 128
CONV_B_WIDTH = 31
D_FF = 2816
CONV_F_WIDTH = 3
D_IN = 2 * D_A + 2 * D_B
LN_EPS = 1e-5
DEEPNORM_ALPHA = (2.0 * DEPTH) ** 0.25
DEEPNORM_BETA = (8.0 * DEPTH) ** -0.25

kernel_name = "hybrid_sgu_conformer_convffn_deepnorm"


def _layernorm(x, g, b):
    xf = x.astype(jnp.float32)
    mu = jnp.mean(xf, axis=-1, keepdims=True)
    var = jnp.mean(jnp.square(xf - mu), axis=-1, keepdims=True)
    y = (xf - mu) * lax.rsqrt(var + LN_EPS)
    return (y * g.astype(jnp.float32) + b.astype(jnp.float32)).astype(x.dtype)


def _causal_depthwise_conv(x, w, b):
    k = w.shape[0]
    y = lax.conv_general_dilated(
        x, w[:, None, :].astype(x.dtype), window_strides=(1,),
        padding=[(k - 1, 0)], dimension_numbers=("NWC", "WIO", "NWC"),
        feature_group_count=x.shape[-1])
    return y + b


def _fwd_setup_inputs(seed: int = 0) -> dict:
    key = jax.random.key(seed)
    ks = jax.random.split(key, 24)
    f32 = jnp.float32
    nrm = lambda k, shape, s: jax.random.normal(k, shape, f32) * s
    gain = lambda k, shape: 1.0 + nrm(k, shape, 0.02)
    return {
        "x": jax.random.normal(ks[0], (BATCH, SEQ, D_MODEL), f32),
        "w_in": nrm(ks[1], (D_MODEL, D_IN), D_MODEL ** -0.5),
        "b_in": nrm(ks[2], (D_IN,), 0.02),
        "ln_a_g": gain(ks[3], (HEADS_A, HEAD_DIM_A)),
        "ln_a_b": nrm(ks[4], (HEADS_A, HEAD_DIM_A), 0.02),
        "w_spatial": nrm(ks[5], (HEADS_A, CHUNK, CHUNK), CHUNK ** -0.5),
        "b_spatial": gain(ks[6], (HEADS_A, CHUNK)),
        "conv_b_w": nrm(ks[7], (CONV_B_WIDTH, D_B), CONV_B_WIDTH ** -0.5),
        "conv_b_b": nrm(ks[8], (D_B,), 0.02),
        "ln_b_g": gain(ks[9], (D_B,)),
        "ln_b_b": nrm(ks[10], (D_B,), 0.02),
        "w_out": nrm(ks[11], (D_MIX, D_MODEL), D_MIX ** -0.5 * DEEPNORM_BETA),
        "b_out": nrm(ks[12], (D_MODEL,), 0.02),
        "ln1_g": gain(ks[13], (D_MODEL,)),
        "ln1_b": nrm(ks[14], (D_MODEL,), 0.02),
        "w_up": nrm(ks[15], (D_MODEL, 2 * D_FF), D_MODEL ** -0.5),
        "conv_f_w": nrm(ks[16], (CONV_F_WIDTH, 2 * D_FF), CONV_F_WIDTH ** -0.5),
        "conv_f_b": nrm(ks[17], (2 * D_FF,), 0.02),
        "w_down": nrm(ks[18], (D_FF, D_MODEL), D_FF ** -0.5 * DEEPNORM_BETA),
        "ln2_g": gain(ks[19], (D_MODEL,)),
        "ln2_b": nrm(ks[20], (D_MODEL,), 0.02),
    }


def _token_mixing(x, w_in, b_in, ln_a_g, ln_a_b, w_spatial, b_spatial,
                  conv_b_w, conv_b_b, ln_b_g, ln_b_b, w_out, b_out):
    bsz, seq, _ = x.shape
    n_chunks = seq // CHUNK
    h = x @ w_in + b_in
    za, zb = h[..., :2 * D_A], h[..., 2 * D_A:]

    za = jax.nn.gelu(za, approximate=False)
    u, v = za[..., :D_A], za[..., D_A:]
    v = _layernorm(v.reshape(bsz, seq, HEADS_A, HEAD_DIM_A), ln_a_g, ln_a_b)
    v = v.reshape(bsz, n_chunks, CHUNK, HEADS_A, HEAD_DIM_A)
    causal = jnp.tril(jnp.ones((CHUNK, CHUNK), dtype=w_spatial.dtype))
    ws = w_spatial * causal
    sv = jnp.einsum("hts,bcshd->bcthd", ws, v) + b_spatial.T[None, None, :, :, None]
    y_a = u * sv.reshape(bsz, seq, D_A)

    a_b, g_b = zb[..., :D_B], zb[..., D_B:]
    yb = a_b * jax.nn.sigmoid(g_b)
    yb = _causal_depthwise_conv(yb, conv_b_w, conv_b_b)
    y_b = jax.nn.silu(_layernorm(yb, ln_b_g, ln_b_b))

    y = jnp.concatenate([y_a, y_b], axis=-1)
    return y @ w_out + b_out


def _conv_ffn(x, w_up, conv_f_w, conv_f_b, w_down):
    h = _causal_depthwise_conv(x @ w_up, conv_f_w, conv_f_b)
    gate, val = h[..., :D_FF], h[..., D_FF:]
    return (jax.nn.silu(gate) * val) @ w_down


def _fwd_reference(x, w_in, b_in, ln_a_g, ln_a_b, w_spatial, b_spatial, conv_b_w, conv_b_b,
              ln_b_g, ln_b_b, w_out, b_out, ln1_g, ln1_b, w_up, conv_f_w, conv_f_b,
              w_down, ln2_g, ln2_b):
    for _ in range(DEPTH):
        mix = _token_mixing(x, w_in, b_in, ln_a_g, ln_a_b, w_spatial, b_spatial,
                            conv_b_w, conv_b_b, ln_b_g, ln_b_b, w_out, b_out)
        x = _layernorm(DEEPNORM_ALPHA * x + mix, ln1_g, ln1_b)
        ffn = _conv_ffn(x, w_up, conv_f_w, conv_f_b, w_down)
        x = _layernorm(DEEPNORM_ALPHA * x + ffn, ln2_g, ln2_b)
    return x


import jax as _jax
import jax.numpy as _jnp

TWIN_FORMAT = 'train_step'
FWD_PARAMS = ['x', 'w_in', 'b_in', 'ln_a_g', 'ln_a_b', 'w_spatial', 'b_spatial', 'conv_b_w', 'conv_b_b', 'ln_b_g', 'ln_b_b', 'w_out', 'b_out', 'ln1_g', 'ln1_b', 'w_up', 'conv_f_w', 'conv_f_b', 'w_down', 'ln2_g', 'ln2_b']
TWIN_WEIGHTS = ['w_in', 'b_in', 'ln_a_g', 'ln_a_b', 'w_spatial', 'b_spatial', 'conv_b_w', 'conv_b_b', 'ln_b_g', 'ln_b_b', 'w_out', 'b_out', 'ln1_g', 'ln1_b', 'w_up', 'conv_f_w', 'conv_f_b', 'w_down', 'ln2_g', 'ln2_b']
TWIN_DIFF_INPUT = 'x'
TWIN_INPUTS = ['x', 'w_in', 'b_in', 'ln_a_g', 'ln_a_b', 'w_spatial', 'b_spatial', 'conv_b_w', 'conv_b_b', 'ln_b_g', 'ln_b_b', 'w_out', 'b_out', 'ln1_g', 'ln1_b', 'w_up', 'conv_f_w', 'conv_f_b', 'w_down', 'ln2_g', 'ln2_b', 'loss_target', 'm_w_in', 'm_b_in', 'm_ln_a_g', 'm_ln_a_b', 'm_w_spatial', 'm_b_spatial', 'm_conv_b_w', 'm_conv_b_b', 'm_ln_b_g', 'm_ln_b_b', 'm_w_out', 'm_b_out', 'm_ln1_g', 'm_ln1_b', 'm_w_up', 'm_conv_f_w', 'm_conv_f_b', 'm_w_down', 'm_ln2_g', 'm_ln2_b', 'v_w_in', 'v_b_in', 'v_ln_a_g', 'v_ln_a_b', 'v_w_spatial', 'v_b_spatial', 'v_conv_b_w', 'v_conv_b_b', 'v_ln_b_g', 'v_ln_b_b', 'v_w_out', 'v_b_out', 'v_ln1_g', 'v_ln1_b', 'v_w_up', 'v_conv_f_w', 'v_conv_f_b', 'v_w_down', 'v_ln2_g', 'v_ln2_b']
TWIN_OUTPUTS = ['loss', 'grad_x', 'grad_w_in', 'grad_b_in', 'grad_ln_a_g', 'grad_ln_a_b', 'grad_w_spatial', 'grad_b_spatial', 'grad_conv_b_w', 'grad_conv_b_b', 'grad_ln_b_g', 'grad_ln_b_b', 'grad_w_out', 'grad_b_out', 'grad_ln1_g', 'grad_ln1_b', 'grad_w_up', 'grad_conv_f_w', 'grad_conv_f_b', 'grad_w_down', 'grad_ln2_g', 'grad_ln2_b', 'delta_w_in', 'delta_b_in', 'delta_ln_a_g', 'delta_ln_a_b', 'delta_w_spatial', 'delta_b_spatial', 'delta_conv_b_w', 'delta_conv_b_b', 'delta_ln_b_g', 'delta_ln_b_b', 'delta_w_out', 'delta_b_out', 'delta_ln1_g', 'delta_ln1_b', 'delta_w_up', 'delta_conv_f_w', 'delta_conv_f_b', 'delta_w_down', 'delta_ln2_g', 'delta_ln2_b', 'new_m_w_in', 'new_m_b_in', 'new_m_ln_a_g', 'new_m_ln_a_b', 'new_m_w_spatial', 'new_m_b_spatial', 'new_m_conv_b_w', 'new_m_conv_b_b', 'new_m_ln_b_g', 'new_m_ln_b_b', 'new_m_w_out', 'new_m_b_out', 'new_m_ln1_g', 'new_m_ln1_b', 'new_m_w_up', 'new_m_conv_f_w', 'new_m_conv_f_b', 'new_m_w_down', 'new_m_ln2_g', 'new_m_ln2_b', 'new_v_w_in', 'new_v_b_in', 'new_v_ln_a_g', 'new_v_ln_a_b', 'new_v_w_spatial', 'new_v_b_spatial', 'new_v_conv_b_w', 'new_v_conv_b_b', 'new_v_ln_b_g', 'new_v_ln_b_b', 'new_v_w_out', 'new_v_b_out', 'new_v_ln1_g', 'new_v_ln1_b', 'new_v_w_up', 'new_v_conv_f_w', 'new_v_conv_f_b', 'new_v_w_down', 'new_v_ln2_g', 'new_v_ln2_b']
TWIN_LEAF_KINDS = {'loss': 'loss', 'grad_x': 'grad_x', 'grad_w_in': 'grad_w', 'grad_b_in': 'grad_w', 'grad_ln_a_g': 'grad_w', 'grad_ln_a_b': 'grad_w', 'grad_w_spatial': 'grad_w', 'grad_b_spatial': 'grad_w', 'grad_conv_b_w': 'grad_w', 'grad_conv_b_b': 'grad_w', 'grad_ln_b_g': 'grad_w', 'grad_ln_b_b': 'grad_w', 'grad_w_out': 'grad_w', 'grad_b_out': 'grad_w', 'grad_ln1_g': 'grad_w', 'grad_ln1_b': 'grad_w', 'grad_w_up': 'grad_w', 'grad_conv_f_w': 'grad_w', 'grad_conv_f_b': 'grad_w', 'grad_w_down': 'grad_w', 'grad_ln2_g': 'grad_w', 'grad_ln2_b': 'grad_w', 'delta_w_in': 'delta_w', 'delta_b_in': 'delta_w', 'delta_ln_a_g': 'delta_w', 'delta_ln_a_b': 'delta_w', 'delta_w_spatial': 'delta_w', 'delta_b_spatial': 'delta_w', 'delta_conv_b_w': 'delta_w', 'delta_conv_b_b': 'delta_w', 'delta_ln_b_g': 'delta_w', 'delta_ln_b_b': 'delta_w', 'delta_w_out': 'delta_w', 'delta_b_out': 'delta_w', 'delta_ln1_g': 'delta_w', 'delta_ln1_b': 'delta_w', 'delta_w_up': 'delta_w', 'delta_conv_f_w': 'delta_w', 'delta_conv_f_b': 'delta_w', 'delta_w_down': 'delta_w', 'delta_ln2_g': 'delta_w', 'delta_ln2_b': 'delta_w', 'new_m_w_in': 'new_m', 'new_m_b_in': 'new_m', 'new_m_ln_a_g': 'new_m', 'new_m_ln_a_b': 'new_m', 'new_m_w_spatial': 'new_m', 'new_m_b_spatial': 'new_m', 'new_m_conv_b_w': 'new_m', 'new_m_conv_b_b': 'new_m', 'new_m_ln_b_g': 'new_m', 'new_m_ln_b_b': 'new_m', 'new_m_w_out': 'new_m', 'new_m_b_out': 'new_m', 'new_m_ln1_g': 'new_m', 'new_m_ln1_b': 'new_m', 'new_m_w_up': 'new_m', 'new_m_conv_f_w': 'new_m', 'new_m_conv_f_b': 'new_m', 'new_m_w_down': 'new_m', 'new_m_ln2_g': 'new_m', 'new_m_ln2_b': 'new_m', 'new_v_w_in': 'new_v', 'new_v_b_in': 'new_v', 'new_v_ln_a_g': 'new_v', 'new_v_ln_a_b': 'new_v', 'new_v_w_spatial': 'new_v', 'new_v_b_spatial': 'new_v', 'new_v_conv_b_w': 'new_v', 'new_v_conv_b_b': 'new_v', 'new_v_ln_b_g': 'new_v', 'new_v_ln_b_b': 'new_v', 'new_v_w_out': 'new_v', 'new_v_b_out': 'new_v', 'new_v_ln1_g': 'new_v', 'new_v_ln1_b': 'new_v', 'new_v_w_up': 'new_v', 'new_v_conv_f_w': 'new_v', 'new_v_conv_f_b': 'new_v', 'new_v_w_down': 'new_v', 'new_v_ln2_g': 'new_v', 'new_v_ln2_b': 'new_v'}


def _forward(args):
    return _fwd_reference(*[args[k] for k in FWD_PARAMS])


def _output_shape():
    out = _jax.eval_shape(lambda: _forward(_fwd_setup_inputs(0)))
    return out.shape, out.dtype

N_MICROBATCH = 1
ADAM_LR = 0.001
ADAM_B1 = 0.9
ADAM_B2 = 0.999
ADAM_EPS = 1e-08
ADAM_WD = 0.01
ADAM_STEP = 10
PER_EXAMPLE_BATCH_AXIS = {'x': 0, 'loss_target': 0}
SHARED_INPUTS = []
_WEIGHT_DTYPES = {'w_in': _jnp.float32, 'b_in': _jnp.float32, 'ln_a_g': _jnp.float32, 'ln_a_b': _jnp.float32, 'w_spatial': _jnp.float32, 'b_spatial': _jnp.float32, 'conv_b_w': _jnp.float32, 'conv_b_b': _jnp.float32, 'ln_b_g': _jnp.float32, 'ln_b_b': _jnp.float32, 'w_out': _jnp.float32, 'b_out': _jnp.float32, 'ln1_g': _jnp.float32, 'ln1_b': _jnp.float32, 'w_up': _jnp.float32, 'conv_f_w': _jnp.float32, 'conv_f_b': _jnp.float32, 'w_down': _jnp.float32, 'ln2_g': _jnp.float32, 'ln2_b': _jnp.float32}
MOMENT_SCALE = {'w_in': 5.178467e-02, 'b_in': 8.376163e-02, 'ln_a_g': 3.933489e-02, 'ln_a_b': 3.683024e-02, 'w_spatial': 4.066288e-02, 'b_spatial': 5.828886e-02, 'conv_b_w': 5.662041e-02, 'conv_b_b': 1.715033e-01, 'ln_b_g': 8.900666e-02, 'ln_b_b': 1.101465e-01, 'w_out': 1.224991e-01, 'b_out': 3.554174e-01, 'ln1_g': 9.572686e-01, 'ln1_b': 4.471837e-01, 'w_up': 3.170837e-02, 'conv_f_w': 3.225046e-02, 'conv_f_b': 3.762487e-02, 'w_down': 8.708554e-02, 'ln2_g': 3.200013e+01, 'ln2_b': 3.728561e+00}


def _to_microbatches(a, axis):
    t = _jnp.moveaxis(a, axis, 0)
    t = t.reshape((N_MICROBATCH, t.shape[0] // N_MICROBATCH) + t.shape[1:])
    return _jnp.moveaxis(t, 1, axis + 1)


def setup_inputs(seed: int = 0) -> dict:
    inp = _fwd_setup_inputs(seed)
    key = _jax.random.fold_in(_jax.random.key(seed), 7919)
    shape, _ = _output_shape()
    out = dict(inp)
    out["loss_target"] = _jax.random.normal(_jax.random.fold_in(key, 0), shape, _jnp.float32)
    for i, name in enumerate(TWIN_WEIGHTS):
        w = inp[name].astype(_jnp.float32)
        if MOMENT_SCALE is None:
            s = _jnp.sqrt(_jnp.mean(_jnp.square(w)) + 1e-30)
        else:
            s = MOMENT_SCALE[name]
        km, kv = _jax.random.split(_jax.random.fold_in(key, i + 1))
        out[name] = w
        out["m_" + name] = s * _jax.random.normal(km, w.shape, _jnp.float32)
        out["v_" + name] = (s * s) * _jax.random.uniform(kv, w.shape, _jnp.float32, 0.5, 1.5)
    if N_MICROBATCH > 1:
        for name, axis in PER_EXAMPLE_BATCH_AXIS.items():
            out[name] = _to_microbatches(out[name], axis)
    return {'x': out['x'], 'w_in': out['w_in'], 'b_in': out['b_in'], 'ln_a_g': out['ln_a_g'], 'ln_a_b': out['ln_a_b'], 'w_spatial': out['w_spatial'], 'b_spatial': out['b_spatial'], 'conv_b_w': out['conv_b_w'], 'conv_b_b': out['conv_b_b'], 'ln_b_g': out['ln_b_g'], 'ln_b_b': out['ln_b_b'], 'w_out': out['w_out'], 'b_out': out['b_out'], 'ln1_g': out['ln1_g'], 'ln1_b': out['ln1_b'], 'w_up': out['w_up'], 'conv_f_w': out['conv_f_w'], 'conv_f_b': out['conv_f_b'], 'w_down': out['w_down'], 'ln2_g': out['ln2_g'], 'ln2_b': out['ln2_b'], 'loss_target': out['loss_target'], 'm_w_in': out['m_w_in'], 'm_b_in': out['m_b_in'], 'm_ln_a_g': out['m_ln_a_g'], 'm_ln_a_b': out['m_ln_a_b'], 'm_w_spatial': out['m_w_spatial'], 'm_b_spatial': out['m_b_spatial'], 'm_conv_b_w': out['m_conv_b_w'], 'm_conv_b_b': out['m_conv_b_b'], 'm_ln_b_g': out['m_ln_b_g'], 'm_ln_b_b': out['m_ln_b_b'], 'm_w_out': out['m_w_out'], 'm_b_out': out['m_b_out'], 'm_ln1_g': out['m_ln1_g'], 'm_ln1_b': out['m_ln1_b'], 'm_w_up': out['m_w_up'], 'm_conv_f_w': out['m_conv_f_w'], 'm_conv_f_b': out['m_conv_f_b'], 'm_w_down': out['m_w_down'], 'm_ln2_g': out['m_ln2_g'], 'm_ln2_b': out['m_ln2_b'], 'v_w_in': out['v_w_in'], 'v_b_in': out['v_b_in'], 'v_ln_a_g': out['v_ln_a_g'], 'v_ln_a_b': out['v_ln_a_b'], 'v_w_spatial': out['v_w_spatial'], 'v_b_spatial': out['v_b_spatial'], 'v_conv_b_w': out['v_conv_b_w'], 'v_conv_b_b': out['v_conv_b_b'], 'v_ln_b_g': out['v_ln_b_g'], 'v_ln_b_b': out['v_ln_b_b'], 'v_w_out': out['v_w_out'], 'v_b_out': out['v_b_out'], 'v_ln1_g': out['v_ln1_g'], 'v_ln1_b': out['v_ln1_b'], 'v_w_up': out['v_w_up'], 'v_conv_f_w': out['v_conv_f_w'], 'v_conv_f_b': out['v_conv_f_b'], 'v_w_down': out['v_w_down'], 'v_ln2_g': out['v_ln2_g'], 'v_ln2_b': out['v_ln2_b']}


def _loss(weights, diff, rest, loss_target):
    with _jax.named_scope("forward"):
        args = {**rest, TWIN_DIFF_INPUT: diff, **{k: w.astype(_WEIGHT_DTYPES[k]) for k, w in weights.items()}}
        y = _forward(args)
    with _jax.named_scope("loss_head"):
        err = _jnp.square(y.astype(_jnp.float32) - loss_target)
        return 0.5 * _jnp.sum(_jnp.mean(err, axis=-1)) if err.ndim else 0.5 * err


def _adamw(w, g, m, v):
    m = ADAM_B1 * m + (1.0 - ADAM_B1) * g
    v = ADAM_B2 * v + (1.0 - ADAM_B2) * _jnp.square(g)
    m_hat = m / (1.0 - ADAM_B1 ** ADAM_STEP)
    v_hat = v / (1.0 - ADAM_B2 ** ADAM_STEP)
    delta = -ADAM_LR * (m_hat / (_jnp.sqrt(v_hat) + ADAM_EPS) + ADAM_WD * w)
    return delta, m, v


def reference(x, w_in, b_in, ln_a_g, ln_a_b, w_spatial, b_spatial, conv_b_w, conv_b_b, ln_b_g, ln_b_b, w_out, b_out, ln1_g, ln1_b, w_up, conv_f_w, conv_f_b, w_down, ln2_g, ln2_b, loss_target, m_w_in, m_b_in, m_ln_a_g, m_ln_a_b, m_w_spatial, m_b_spatial, m_conv_b_w, m_conv_b_b, m_ln_b_g, m_ln_b_b, m_w_out, m_b_out, m_ln1_g, m_ln1_b, m_w_up, m_conv_f_w, m_conv_f_b, m_w_down, m_ln2_g, m_ln2_b, v_w_in, v_b_in, v_ln_a_g, v_ln_a_b, v_w_spatial, v_b_spatial, v_conv_b_w, v_conv_b_b, v_ln_b_g, v_ln_b_b, v_w_out, v_b_out, v_ln1_g, v_ln1_b, v_w_up, v_conv_f_w, v_conv_f_b, v_w_down, v_ln2_g, v_ln2_b):
    given = dict(x=x, w_in=w_in, b_in=b_in, ln_a_g=ln_a_g, ln_a_b=ln_a_b, w_spatial=w_spatial, b_spatial=b_spatial, conv_b_w=conv_b_w, conv_b_b=conv_b_b, ln_b_g=ln_b_g, ln_b_b=ln_b_b, w_out=w_out, b_out=b_out, ln1_g=ln1_g, ln1_b=ln1_b, w_up=w_up, conv_f_w=conv_f_w, conv_f_b=conv_f_b, w_down=w_down, ln2_g=ln2_g, ln2_b=ln2_b, loss_target=loss_target, m_w_in=m_w_in, m_b_in=m_b_in, m_ln_a_g=m_ln_a_g, m_ln_a_b=m_ln_a_b, m_w_spatial=m_w_spatial, m_b_spatial=m_b_spatial, m_conv_b_w=m_conv_b_w, m_conv_b_b=m_conv_b_b, m_ln_b_g=m_ln_b_g, m_ln_b_b=m_ln_b_b, m_w_out=m_w_out, m_b_out=m_b_out, m_ln1_g=m_ln1_g, m_ln1_b=m_ln1_b, m_w_up=m_w_up, m_conv_f_w=m_conv_f_w, m_conv_f_b=m_conv_f_b, m_w_down=m_w_down, m_ln2_g=m_ln2_g, m_ln2_b=m_ln2_b, v_w_in=v_w_in, v_b_in=v_b_in, v_ln_a_g=v_ln_a_g, v_ln_a_b=v_ln_a_b, v_w_spatial=v_w_spatial, v_b_spatial=v_b_spatial, v_conv_b_w=v_conv_b_w, v_conv_b_b=v_conv_b_b, v_ln_b_g=v_ln_b_g, v_ln_b_b=v_ln_b_b, v_w_out=v_w_out, v_b_out=v_b_out, v_ln1_g=v_ln1_g, v_ln1_b=v_ln1_b, v_w_up=v_w_up, v_conv_f_w=v_conv_f_w, v_conv_f_b=v_conv_f_b, v_w_down=v_w_down, v_ln2_g=v_ln2_g, v_ln2_b=v_ln2_b)
    weights = {n: given[n] for n in TWIN_WEIGHTS}
    shared = {n: given[n] for n in SHARED_INPUTS}
    per_example = {n: given[n] for n in ['x']}
    grad_fn = _jax.value_and_grad(_loss, argnums=(0, 1))

    def one_microbatch(ex, loss_target):
        ex = dict(ex)
        diff = ex.pop(TWIN_DIFF_INPUT)
        return grad_fn(weights, diff, {**shared, **ex}, loss_target)

    if N_MICROBATCH == 1:
        loss, (grad_w, grad_x) = one_microbatch(per_example, given["loss_target"])
    else:
        def body(carry, xs):
            loss_sum, grad_sum = carry
            l_k, (gw_k, gx_k) = one_microbatch(xs[0], xs[1])
            with _jax.named_scope("update"):
                return (loss_sum + l_k, _jax.tree.map(_jnp.add, grad_sum, gw_k)), gx_k

        init = (_jnp.zeros((), _jnp.float32), _jax.tree.map(_jnp.zeros_like, weights))
        (loss, grad_w), grad_x = _jax.lax.scan(body, init, (per_example, given["loss_target"]))
    with _jax.named_scope("update"):
        delta_w, new_m, new_v = {}, {}, {}
        for n in TWIN_WEIGHTS:
            delta_w[n], new_m[n], new_v[n] = _adamw(weights[n], grad_w[n], given["m_" + n], given["v_" + n])
    return (loss, grad_x, *[grad_w[n] for n in TWIN_WEIGHTS], *[delta_w[n] for n in TWIN_WEIGHTS],
            *[new_m[n] for n in TWIN_WEIGHTS], *[new_v[n] for n in TWIN_WEIGHTS])
```

```python
import functools
import math

import jax
import jax.numpy as jnp
from jax import lax
from jax.experimental import pallas as pl
from jax.experimental.pallas import tpu as pltpu

F32 = jnp.float32
BF16 = jnp.bfloat16

D = 1024
D_A = 512
D_B = 512
HEADS = 4
HEAD_DIM = 128
CHUNK = 128
KB = 31
KF = 3
D_FF = 2816
D_IN = 2048
N_DEV = 8
W_IN_BLK = D_IN // N_DEV
W_UP_BLK = 2 * D_FF // N_DEV
N_F = 4
LN_EPS = 1e-5
ALPHA = 2.0 ** 0.25

ADAM_LR = 0.001
ADAM_B1 = 0.9
ADAM_B2 = 0.999
ADAM_EPS = 1e-08
ADAM_WD = 0.01
ADAM_STEP = 10

INV_SQRT2 = 1.0 / math.sqrt(2.0)
INV_SQRT_2PI = 1.0 / math.sqrt(2.0 * math.pi)

HALO_B = 32
HALO_F = 8
ROWS = 64
VMEM_LIMIT = 58 * 1024 * 1024

MESH = pl.DeviceIdType.MESH
ANY = pl.BlockSpec(memory_space=pl.ANY)
VMEM = pl.BlockSpec(memory_space=pltpu.VMEM)

S_BIN, S_LNAG, S_LNAB, S_WS, S_BS, S_CBB, S_LNBG, S_LNBB, S_BOUT, S_LN1G, S_LN1B = (
    0, 16, 24, 32, 544, 552, 560, 568, 576, 584, 592)
S_MIX_ROWS = 600
S_CFB = 600
S_LN2G = 648
S_LN2B = 656
S_ROWS = 664


def _tn(a, b):
    return lax.dot_general(a, b, (((0,), (0,)), ((), ())), preferred_element_type=F32)


def _nt(a, b):
    return lax.dot_general(a, b, (((1,), (1,)), ((), ())), preferred_element_type=F32)


def _nn(a, b):
    return jnp.dot(a, b, preferred_element_type=F32)


def _sigmoid(x):
    return 1.0 / (1.0 + jnp.exp(-x))


def _ln_stats(x):
    mu = jnp.mean(x, axis=-1, keepdims=True)
    xc = x - mu
    var = jnp.mean(xc * xc, axis=-1, keepdims=True)
    rstd = lax.rsqrt(var + LN_EPS)
    return xc * rstd, rstd


def _ln_bwd(dxhat, xhat, rstd):
    m1 = jnp.mean(dxhat, axis=-1, keepdims=True)
    m2 = jnp.mean(dxhat * xhat, axis=-1, keepdims=True)
    return rstd * (dxhat - m1 - xhat * m2)


def _rsum8(x):
    r, n = x.shape
    return x.reshape(r // 8, 8, n).sum(axis=0)


def _rows(i, n=ROWS):
    return pl.ds(pl.multiple_of(i * n, n), n)


def _tril_mask():
    r = lax.broadcasted_iota(jnp.int32, (CHUNK, CHUNK), 0)
    c = lax.broadcasted_iota(jnp.int32, (CHUNK, CHUNK), 1)
    return c <= r


def _mixer_a_fwd(hu, hv, ga_ref, ba_ref, wsm_ref, bst_ref):
    cdf_u = 0.5 * (1.0 + lax.erf(hu * INV_SQRT2))
    cdf_v = 0.5 * (1.0 + lax.erf(hv * INV_SQRT2))
    u = hu * cdf_u
    v = hv * cdf_v
    xhats, rstds, vns, svs = [], [], [], []
    for hd in range(HEADS):
        sl = slice(hd * HEAD_DIM, (hd + 1) * HEAD_DIM)
        xhat, rstd = _ln_stats(v[:, sl])
        vn = (xhat * ga_ref[hd:hd + 1, :] + ba_ref[hd:hd + 1, :]).astype(BF16)
        sv = _nn(wsm_ref[hd], vn) + bst_ref[:, hd:hd + 1]
        xhats.append(xhat)
        rstds.append(rstd)
        vns.append(vn)
        svs.append(sv)
    return u, cdf_u, cdf_v, xhats, rstds, vns, svs


def _conv_b_block(ext_ref, base, cw_ref):
    win = ext_ref[pl.ds(base, ROWS + HALO_B), :]
    acc = jnp.zeros((ROWS, D_B), F32)
    for k in range(KB):
        acc = acc + win[2 + k:2 + k + ROWS, :] * cw_ref[k:k + 1, :]
    return acc, win


def _conv_f_block(ext_ref, base, w):
    win = ext_ref[pl.ds(base, ROWS + HALO_F), :]
    out = (win[6:6 + ROWS, :] * w[0:1, :] + win[7:7 + ROWS, :] * w[1:2, :]
           + win[8:8 + ROWS, :] * w[2:3, :])
    return out, win


def _params(sem, **kw):
    return pltpu.CompilerParams(dimension_semantics=sem, vmem_limit_bytes=VMEM_LIMIT, **kw)


def _resident(shape):
    zeros = (0,) * len(shape)
    return pl.BlockSpec(shape, lambda *_: zeros, pipeline_mode=pl.Buffered(1))


def _full(shape):
    zeros = (0,) * len(shape)
    return pl.BlockSpec(shape, lambda *_: zeros)


def _mesh_pos():
    return lax.axis_index("x"), lax.axis_index("y"), lax.axis_index("c")


def _chip_patterns(x, y):
    return [(x, y), (1 - x, y), (x, 1 - y), (1 - x, 1 - y)]


def _lid(x, y, c):
    return 4 * x + 2 * y + c


def all_gather_weights(w_in, w_out, w_up, w_down, convp):
    srcs = [w_in, w_out, w_up, w_down, convp]
    n = len(srcs)
    out_dt = [BF16, BF16, BF16, BF16, F32]

    def body(win_ref, wout_ref, wup_ref, wdown_ref, convp_ref,
             gin_ref, gout_ref, gup_ref, gdown_ref, gconv_ref,
             sin_ref, sout_ref, sup_ref, sdown_ref, send_sems, recv_sems, local_sems):
        x, y, c = _mesh_pos()
        me, sib = (x, y, c), (x, y, 1 - c)
        chips = _chip_patterns(x, y)[1:]
        sin_ref[...] = win_ref[...].astype(BF16)
        sout_ref[...] = wout_ref[...].astype(BF16)
        sup_ref[...] = wup_ref[...].astype(BF16)
        sdown_ref[...] = wdown_ref[...].astype(BF16)
        mine = [sin_ref, sout_ref, sup_ref, sdown_ref, convp_ref]
        outs = [gin_ref, gout_ref, gup_ref, gdown_ref, gconv_ref]

        def copy(a, k, block, to, src=None):
            blk = outs[a].at[_lid(*block)]
            return pltpu.make_async_remote_copy(
                src_ref=blk if src is None else src, dst_ref=blk,
                send_sem=send_sems.at[a, k], recv_sem=recv_sems.at[a, k],
                device_id=to, device_id_type=MESH)

        local = [pltpu.make_async_copy(mine[a], outs[a].at[_lid(*me)], local_sems.at[a])
                 for a in range(n)]
        for cp in local:
            cp.start()
        first = []
        for a in range(n):
            first.append(copy(a, 0, me, sib, src=mine[a]))
            for j, chip in enumerate(chips):
                first.append(copy(a, 1 + j, me, (*chip, c), src=mine[a]))
        for cp in first:
            cp.start()
        passed = []
        for j, chip in enumerate(chips):
            for a in range(n):
                copy(a, 1 + j, (*chip, c), me).wait_recv()
                cp = copy(a, 4 + j, (*chip, c), sib)
                cp.start()
                passed.append(cp)
        for a in range(n):
            copy(a, 0, sib, me).wait_recv()
            for j, chip in enumerate(chips):
                copy(a, 4 + j, (*chip, 1 - c), me).wait_recv()
        for cp in first + passed:
            cp.wait_send()
        for cp in local:
            cp.wait()

    return pl.pallas_call(
        body, name="all_gather_weights",
        out_shape=[jax.ShapeDtypeStruct((N_DEV,) + s.shape, dt) for s, dt in zip(srcs, out_dt)],
        in_specs=[VMEM] * n, out_specs=[ANY] * n,
        scratch_shapes=[pltpu.VMEM(s.shape, BF16) for s in srcs[:4]] + [
            pltpu.SemaphoreType.DMA((n, 7)), pltpu.SemaphoreType.DMA((n, 7)),
            pltpu.SemaphoreType.DMA((n,))],
        compiler_params=pltpu.CompilerParams(vmem_limit_bytes=VMEM_LIMIT),
    )(*srcs)


def exchange_with_sibling(grads, svec):
    n = len(grads)

    def body(*refs):
        g = refs[:n]
        sv = refs[n]
        land = refs[n + 1:2 * n + 1]
        sv_land = refs[2 * n + 1]
        send_sems, recv_sems, sv_sems = refs[2 * n + 2:]
        x, y, c = _mesh_pos()
        sib = (x, y, 1 - c)
        copies = []
        for a in range(n):
            for k, (px, py) in enumerate(_chip_patterns(x, y)):
                copies.append(pltpu.make_async_remote_copy(
                    src_ref=g[a].at[_lid(px, py, 1 - c)], dst_ref=land[a].at[k],
                    send_sem=send_sems.at[a, k], recv_sem=recv_sems.at[a, k],
                    device_id=sib, device_id_type=MESH))
        copies.append(pltpu.make_async_remote_copy(
            src_ref=sv, dst_ref=sv_land, send_sem=sv_sems.at[0], recv_sem=sv_sems.at[1],
            device_id=sib, device_id_type=MESH))
        for cp in copies:
            cp.start()
        for cp in copies:
            cp.wait()

    return pl.pallas_call(
        body, name="exchange_with_sibling",
        out_shape=[jax.ShapeDtypeStruct((4,) + g.shape[1:], F32) for g in grads]
        + [jax.ShapeDtypeStruct(svec.shape, F32)],
        in_specs=[ANY] * (n + 1), out_specs=[ANY] * (n + 1),
        scratch_shapes=[pltpu.SemaphoreType.DMA((n, 4)), pltpu.SemaphoreType.DMA((n, 4)),
                        pltpu.SemaphoreType.DMA((2,))],
    )(*grads, svec)


def exchange_between_chips(partials, svec, sv_land):
    n = len(partials)

    def body(*refs):
        p = refs[:n]
        sv_ref, svl_ref = refs[n], refs[n + 1]
        land = refs[n + 2:2 * n + 2]
        sv_slots = refs[2 * n + 2]
        chip_sv, send_sems, recv_sems, sv_send, sv_recv, local_sem = refs[2 * n + 3:]
        x, y, c = _mesh_pos()
        q = 2 * x + y
        chip_sv[...] = sv_ref[...] + svl_ref[...]
        local = pltpu.make_async_copy(chip_sv, sv_slots.at[q], local_sem)
        local.start()
        copies = []
        for k, (px, py) in enumerate(_chip_patterns(x, y)[1:]):
            to = (px, py, c)
            for a in range(n):
                copies.append(pltpu.make_async_remote_copy(
                    src_ref=p[a].at[k], dst_ref=land[a].at[k],
                    send_sem=send_sems.at[a, k], recv_sem=recv_sems.at[a, k],
                    device_id=to, device_id_type=MESH))
        for cp in copies:
            cp.start()
        sv_copies = []
        for k, (px, py) in enumerate(_chip_patterns(x, y)[1:]):
            cp = pltpu.make_async_remote_copy(
                src_ref=chip_sv, dst_ref=sv_slots.at[q],
                send_sem=sv_send.at[k], recv_sem=sv_recv.at[k],
                device_id=(px, py, c), device_id_type=MESH)
            cp.start()
            sv_copies.append(cp)
        for cp in copies:
            cp.wait()
        for k, (px, py) in enumerate(_chip_patterns(x, y)[1:]):
            sv_copies[k].wait_send()
            pltpu.make_async_remote_copy(
                src_ref=chip_sv, dst_ref=sv_slots.at[2 * px + py],
                send_sem=sv_send.at[k], recv_sem=sv_recv.at[k],
                device_id=(px, py, c), device_id_type=MESH).wait_recv()
        local.wait()

    return pl.pallas_call(
        body, name="exchange_between_chips",
        out_shape=[jax.ShapeDtypeStruct(p.shape, BF16) for p in partials]
        + [jax.ShapeDtypeStruct((4,) + svec.shape, F32)],
        in_specs=[ANY] * n + [VMEM, VMEM], out_specs=[ANY] * (n + 1),
        scratch_shapes=[pltpu.VMEM(svec.shape, F32),
                        pltpu.SemaphoreType.DMA((n, 3)), pltpu.SemaphoreType.DMA((n, 3)),
                        pltpu.SemaphoreType.DMA((3,)), pltpu.SemaphoreType.DMA((3,)),
                        pltpu.SemaphoreType.DMA],
    )(*partials, svec, sv_land)


def chip_partials(name, g, land, jidx, rb):
    _, r, c = g.shape

    def body(j_ref, g_ref, l_ref, o_ref):
        o_ref[...] = (g_ref[...] + l_ref[...]).astype(BF16)

    return pl.pallas_call(
        body, name=name,
        out_shape=jax.ShapeDtypeStruct((3, r, c), BF16),
        grid_spec=pltpu.PrefetchScalarGridSpec(
            num_scalar_prefetch=1, grid=(3, r // rb),
            in_specs=[pl.BlockSpec((1, rb, c), lambda k, i, j: (j[1 + k], i, 0)),
                      pl.BlockSpec((1, rb, c), lambda k, i, j: (1 + k, i, 0))],
            out_specs=pl.BlockSpec((1, rb, c), lambda k, i, j: (k, i, 0))),
        compiler_params=_params(("arbitrary", "arbitrary")),
    )(jidx, g, land)


def _adamw(w, g, m, v):
    m2 = ADAM_B1 * m + (1.0 - ADAM_B1) * g
    v2 = ADAM_B2 * v + (1.0 - ADAM_B2) * (g * g)
    m_hat = m2 / (1.0 - ADAM_B1 ** ADAM_STEP)
    v_hat = v2 / (1.0 - ADAM_B2 ** ADAM_STEP)
    delta = -ADAM_LR * (m_hat / (jnp.sqrt(v_hat) + ADAM_EPS) + ADAM_WD * w)
    return delta, m2, v2


def reduce_and_adamw(name, g, land, recv, w, m, v, jidx, rb):
    _, r, c = g.shape

    def body(j_ref, g_ref, l_ref, r_ref, w_ref, m_ref, v_ref, go_ref, do_ref, mo_ref, vo_ref):
        grad = (g_ref[0] + l_ref[0]) + r_ref[0].astype(F32) + r_ref[1].astype(F32) + r_ref[2].astype(F32)
        delta, m2, v2 = _adamw(w_ref[...], grad, m_ref[...], v_ref[...])
        go_ref[...] = grad
        do_ref[...] = delta
        mo_ref[...] = m2
        vo_ref[...] = v2

    blk = pl.BlockSpec((rb, c), lambda i, j: (i, 0))
    return pl.pallas_call(
        body, name=name,
        out_shape=[jax.ShapeDtypeStruct((r, c), F32)] * 4,
        grid_spec=pltpu.PrefetchScalarGridSpec(
            num_scalar_prefetch=1, grid=(r // rb,),
            in_specs=[pl.BlockSpec((1, rb, c), lambda i, j: (j[0], i, 0)),
                      pl.BlockSpec((1, rb, c), lambda i, j: (0, i, 0)),
                      pl.BlockSpec((3, rb, c), lambda i, j: (0, i, 0)),
                      blk, blk, blk],
            out_specs=[blk] * 4),
        compiler_params=_params(("arbitrary",)),
    )(jidx, g, land, recv, w, m, v)


SMALL_LAYOUT = [
    ("b_in", S_BIN, 16), ("ln_a_g", S_LNAG, 4), ("ln_a_b", S_LNAB, 4), ("w_spatial", S_WS, 512),
    ("b_spatial", S_BS, 4), ("conv_b_b", S_CBB, 4), ("ln_b_g", S_LNBG, 4), ("ln_b_b", S_LNBB, 4),
    ("b_out", S_BOUT, 8), ("ln1_g", S_LN1G, 8), ("ln1_b", S_LN1B, 8), ("conv_f_b", S_CFB, 44),
    ("ln2_g", S_LN2G, 8), ("ln2_b", S_LN2B, 8),
]


def small_adamw(sv_slots, ws, ms, vs):
    n = len(SMALL_LAYOUT)

    def body(*refs):
        s_ref = refs[0]
        w_refs, m_refs, v_refs = refs[1:1 + n], refs[1 + n:1 + 2 * n], refs[1 + 2 * n:1 + 3 * n]
        outs = refs[1 + 3 * n:]
        for p, (_, row0, rows) in enumerate(SMALL_LAYOUT):
            sl = pl.ds(row0, rows)
            grad = ((s_ref[0, sl, :] + s_ref[1, sl, :]) + s_ref[2, sl, :]) + s_ref[3, sl, :]
            delta, m2, v2 = _adamw(w_refs[p][...], grad, m_refs[p][...], v_refs[p][...])
            outs[p][...] = grad
            outs[n + p][...] = delta
            outs[2 * n + p][...] = m2
            outs[3 * n + p][...] = v2

    shapes = [jax.ShapeDtypeStruct((rows, 128), F32) for _, _, rows in SMALL_LAYOUT]
    return pl.pallas_call(
        body, name="small_adamw", out_shape=shapes * 4,
        in_specs=[VMEM] * (1 + 3 * n), out_specs=[VMEM] * (4 * n),
    )(sv_slots, *ws, *ms, *vs)


def mix_forward(x, win_g, b_in, ln_a_g, ln_a_b, w_spatial, bst, conv_b_w, conv_b_b, ln_b_g, ln_b_b,
                wout, b_out, ln1_g, ln1_b, tm):
    t = x.shape[0]
    n_chunks = tm // CHUNK

    def body(x_ref, win_ref, bin_ref, ga_ref, ba_ref, ws_ref, bst_ref, cw_ref, cb_ref, gb_ref,
             bb_ref, wout_ref, bout_ref, g1_ref, b1_ref,
             h_ref, xhat1_ref, rstd1_ref, ext_ref, y_ref, wsm_ref):
        i = pl.program_id(0)

        @pl.when(i == 0)
        def _():
            ext_ref[0:HALO_B, :] = jnp.zeros((HALO_B, D_B), F32)
            mask = _tril_mask()
            for hd in range(HEADS):
                wsm_ref[hd] = jnp.where(mask, ws_ref[hd], 0.0).astype(BF16)

        xb = x_ref[...].astype(BF16)
        for j in range(N_DEV):
            cols = slice(j * W_IN_BLK, (j + 1) * W_IN_BLK)
            h_ref[:, cols] = _nn(xb, win_ref[j]) + bin_ref[:, cols]

        def chunk(ci, carry):
            r = _rows(ci, CHUNK)
            u, _, _, _, _, _, svs = _mixer_a_fwd(h_ref[r, 0:D_A], h_ref[r, D_A:2 * D_A],
                                                 ga_ref, ba_ref, wsm_ref, bst_ref)
            for hd in range(HEADS):
                sl = slice(hd * HEAD_DIM, (hd + 1) * HEAD_DIM)
                y_ref[r, sl] = (u[:, sl] * svs[hd]).astype(BF16)
            a_b = h_ref[r, 2 * D_A:2 * D_A + D_B]
            g_b = h_ref[r, 2 * D_A + D_B:D_IN]
            ext_ref[pl.ds(pl.multiple_of(HALO_B + ci * CHUNK, 32), CHUNK), :] = a_b * _sigmoid(g_b)
            return carry

        lax.fori_loop(0, n_chunks, chunk, 0)

        def conv_rows(bi, carry):
            base = pl.multiple_of(bi * ROWS, ROWS)
            acc, _ = _conv_b_block(ext_ref, base, cw_ref)
            xhat, _ = _ln_stats(acc + cb_ref[...])
            yb2 = xhat * gb_ref[...] + bb_ref[...]
            y_ref[pl.ds(base, ROWS), D_A:D] = (yb2 * _sigmoid(yb2)).astype(BF16)
            return carry

        lax.fori_loop(0, tm // ROWS, conv_rows, 0)
        ext_ref[0:HALO_B, :] = ext_ref[tm:tm + HALO_B, :]

        mix = _nn(y_ref[...], wout_ref[...]) + bout_ref[...]
        xhat1, rstd1 = _ln_stats(ALPHA * x_ref[...] + mix)
        xhat1_ref[...] = xhat1
        rstd1_ref[...] = jnp.broadcast_to(rstd1, (tm, 128))

    row = lambda w: pl.BlockSpec((tm, w), lambda i: (i, 0))
    return pl.pallas_call(
        body, name="mix_forward", grid=(t // tm,),
        in_specs=[row(D), _resident(win_g.shape), _full(b_in.shape), _full(ln_a_g.shape),
                  _full(ln_a_b.shape), _full(w_spatial.shape), _full(bst.shape),
                  _full(conv_b_w.shape), _full(conv_b_b.shape), _full(ln_b_g.shape),
                  _full(ln_b_b.shape), _resident(wout.shape), _full(b_out.shape),
                  _full(ln1_g.shape), _full(ln1_b.shape)],
        out_specs=[row(D_IN), row(D), row(128)],
        out_shape=[jax.ShapeDtypeStruct((t, D_IN), F32), jax.ShapeDtypeStruct((t, D), F32),
                   jax.ShapeDtypeStruct((t, 128), F32)],
        scratch_shapes=[pltpu.VMEM((tm + HALO_B, D_B), F32), pltpu.VMEM((tm, D), BF16),
                        pltpu.VMEM((HEADS, CHUNK, CHUNK), BF16)],
        compiler_params=_params(("arbitrary",)),
    )(x, win_g, b_in, ln_a_g, ln_a_b, w_spatial, bst, conv_b_w, conv_b_b, ln_b_g, ln_b_b,
      wout, b_out, ln1_g, ln1_b)


def ffn_forward(xhat1, ln1_g, ln1_b, wup_g, cfw, cfb, wdown, ln2_g, ln2_b, target, tm):
    t = xhat1.shape[0]
    nt = t // tm

    def body(xh_ref, g1_ref, b1_ref, wup_ref, cfw_ref, cfb_ref, wdown_ref, g2_ref, b2_ref, tgt_ref,
             hu_ref, dr2_ref, loss_ref, sln2_ref,
             x1_ref, x1b_ref, extg_ref, extv_ref, carry_ref, gbuf_ref, ffn_ref, acc_loss, acc_g2, acc_b2):
        i = pl.program_id(0)

        @pl.when(i == 0)
        def _():
            carry_ref[...] = jnp.zeros(carry_ref.shape, F32)
            acc_loss[...] = jnp.zeros(acc_loss.shape, F32)
            acc_g2[...] = jnp.zeros(acc_g2.shape, F32)
            acc_b2[...] = jnp.zeros(acc_b2.shape, F32)

        x1 = xh_ref[...] * g1_ref[...] + b1_ref[...]
        x1_ref[...] = x1
        x1b_ref[...] = x1.astype(BF16)

        for f in range(N_F):
            hg = _nn(x1b_ref[...], wup_ref[f])
            hv = _nn(x1b_ref[...], wup_ref[N_F + f])
            hu_ref[f] = hg
            hu_ref[N_F + f] = hv
            extg_ref[0:HALO_F, :] = carry_ref[f]
            extv_ref[0:HALO_F, :] = carry_ref[N_F + f]
            extg_ref[HALO_F:HALO_F + tm, :] = hg
            extv_ref[HALO_F:HALO_F + tm, :] = hv
            carry_ref[f] = extg_ref[tm:tm + HALO_F, :]
            carry_ref[N_F + f] = extv_ref[tm:tm + HALO_F, :]
            wg, wv = cfw_ref[f], cfw_ref[N_F + f]
            bg, bv = cfb_ref[f:f + 1, :], cfb_ref[N_F + f:N_F + f + 1, :]

            def rows(bi, carry, wg=wg, wv=wv, bg=bg, bv=bv):
                base = pl.multiple_of(bi * ROWS, ROWS)
                gate = _conv_f_block(extg_ref, base, wg)[0] + bg
                val = _conv_f_block(extv_ref, base, wv)[0] + bv
                gbuf_ref[pl.ds(base, ROWS), :] = (gate * _sigmoid(gate) * val).astype(BF16)
                return carry

            lax.fori_loop(0, tm // ROWS, rows, 0)
            part = _nn(gbuf_ref[...], wdown_ref[f])
            if f == 0:
                ffn_ref[...] = part
            else:
                ffn_ref[...] += part

        def tail(bi, carry):
            r = _rows(bi)
            xhat2, rstd2 = _ln_stats(ALPHA * x1_ref[r, :] + ffn_ref[r, :])
            err = xhat2 * g2_ref[...] + b2_ref[...] - tgt_ref[r, :]
            e2 = _rsum8(err * err)
            acc_loss[...] += sum(e2[:, k * 128:(k + 1) * 128] for k in range(D // 128))
            dy = err * (1.0 / D)
            acc_g2[...] += _rsum8(dy * xhat2)
            acc_b2[...] += _rsum8(dy)
            dr2_ref[r, :] = _ln_bwd(dy * g2_ref[...], xhat2, rstd2)
            return carry

        lax.fori_loop(0, tm // ROWS, tail, 0)
        loss_ref[...] = acc_loss[...]

        @pl.when(i == nt - 1)
        def _():
            dg = jnp.sum(acc_g2[...], axis=0, keepdims=True)
            db = jnp.sum(acc_b2[...], axis=0, keepdims=True)
            for k in range(D // 128):
                sln2_ref[k:k + 1, :] = dg[:, k * 128:(k + 1) * 128]
                sln2_ref[8 + k:9 + k, :] = db[:, k * 128:(k + 1) * 128]

    row = pl.BlockSpec((tm, D), lambda i: (i, 0))
    return pl.pallas_call(
        body, name="ffn_forward", grid=(nt,),
        in_specs=[row, _full(ln1_g.shape), _full(ln1_b.shape), _resident(wup_g.shape),
                  _full(cfw.shape), _full(cfb.shape), _resident(wdown.shape),
                  _full(ln2_g.shape), _full(ln2_b.shape), row],
        out_specs=[pl.BlockSpec((N_DEV, tm, W_UP_BLK), lambda i: (0, i, 0)), row,
                   _full((8, 128)), _full((16, 128))],
        out_shape=[jax.ShapeDtypeStruct((N_DEV, t, W_UP_BLK), F32), jax.ShapeDtypeStruct((t, D), F32),
                   jax.ShapeDtypeStruct((8, 128), F32), jax.ShapeDtypeStruct((16, 128), F32)],
        scratch_shapes=[pltpu.VMEM((tm, D), F32), pltpu.VMEM((tm, D), BF16),
                        pltpu.VMEM((tm + HALO_F, W_UP_BLK), F32), pltpu.VMEM((tm + HALO_F, W_UP_BLK), F32),
                        pltpu.VMEM((N_DEV, HALO_F, W_UP_BLK), F32), pltpu.VMEM((tm, W_UP_BLK), BF16),
                        pltpu.VMEM((tm, D), F32), pltpu.VMEM((8, 128), F32),
                        pltpu.VMEM((8, D), F32), pltpu.VMEM((8, D), F32)],
        compiler_params=_params(("arbitrary",)),
    )(xhat1, ln1_g, ln1_b, wup_g, cfw, cfb, wdown, ln2_g, ln2_b, target)


def ffn_backward(dr2, xhat1, ln1_g, ln1_b, hu, wup_g, cfw, cfb, wdown, tm):
    t = dr2.shape[0]
    nt = t // tm
    hu4 = hu.reshape(2, N_F, t, W_UP_BLK)
    wup4 = wup_g.reshape(2, N_F, D, W_UP_BLK)
    cfw4 = cfw.reshape(2, N_F, KF, W_UP_BLK)
    cfb4 = cfb.reshape(2, N_F, 1, W_UP_BLK)
    halo_blocks = tm // HALO_F

    def body(dr2_ref, xh_ref, g1_ref, b1_ref, hu_ref, halo_ref, wup_ref, cfw_ref, cfb_ref, wdown_ref,
             dwup_ref, dwdown_ref, dcfw_ref, dcfb_ref, dx1_ref,
             x1b_ref, drb_ref, dg_ref, extg_ref, extv_ref, dextg_ref, dextv_ref, gbuf_ref,
             dhug_ref, dhuv_ref, acc_wup, acc_wdown, acc_cfw, acc_cfb, sem):
        f = pl.program_id(0)
        i = pl.program_id(1)
        first_tile = i == nt - 1

        @pl.when(i == 0)
        def _():
            acc_wup[...] = jnp.zeros(acc_wup.shape, F32)
            acc_wdown[...] = jnp.zeros(acc_wdown.shape, F32)
            acc_cfw[...] = jnp.zeros(acc_cfw.shape, F32)
            acc_cfb[...] = jnp.zeros(acc_cfb.shape, F32)
            dextg_ref[tm:tm + HALO_F, :] = jnp.zeros((HALO_F, W_UP_BLK), F32)
            dextv_ref[tm:tm + HALO_F, :] = jnp.zeros((HALO_F, W_UP_BLK), F32)

        x1b_ref[...] = (xh_ref[...] * g1_ref[...] + b1_ref[...]).astype(BF16)
        drb_ref[...] = dr2_ref[...].astype(BF16)
        dg_ref[...] = _nt(drb_ref[...], wdown_ref[0])

        keep = jnp.where(first_tile, 0.0, 1.0)
        extg_ref[0:HALO_F, :] = halo_ref[0, 0] * keep
        extv_ref[0:HALO_F, :] = halo_ref[1, 0] * keep
        extg_ref[HALO_F:HALO_F + tm, :] = hu_ref[0, 0]
        extv_ref[HALO_F:HALO_F + tm, :] = hu_ref[1, 0]
        wg, wv = cfw_ref[0, 0], cfw_ref[1, 0]
        bg, bv = cfb_ref[0, 0], cfb_ref[1, 0]

        def rows1(bi, carry):
            base = pl.multiple_of(bi * ROWS, ROWS)
            r = pl.ds(base, ROWS)
            gate, wing = _conv_f_block(extg_ref, base, wg)
            val, winv = _conv_f_block(extv_ref, base, wv)
            gate = gate + bg
            val = val + bv
            sg = _sigmoid(gate)
            silu = gate * sg
            gbuf_ref[r, :] = (silu * val).astype(BF16)
            dg = dg_ref[r, :]
            dgate = dg * val * (sg * (1.0 + gate * (1.0 - sg)))
            dval = dg * silu
            dextg_ref[r, :] = dgate
            dextv_ref[r, :] = dval
            acc_cfb[0:8, :] += _rsum8(dgate)
            acc_cfb[8:16, :] += _rsum8(dval)
            for k in range(KF):
                acc_cfw[8 * k:8 * k + 8, :] += _rsum8(dgate * wing[6 + k:6 + k + ROWS, :])
                acc_cfw[8 * (KF + k):8 * (KF + k) + 8, :] += _rsum8(dval * winv[6 + k:6 + k + ROWS, :])
            return carry

        lax.fori_loop(0, tm // ROWS, rows1, 0)

        def rows2(bi, carry):
            base = pl.multiple_of(bi * ROWS, ROWS)
            r = pl.ds(base, ROWS)
            wing = dextg_ref[pl.ds(base, ROWS + HALO_F), :]
            winv = dextv_ref[pl.ds(base, ROWS + HALO_F), :]
            dhug = sum(wing[2 - k:2 - k + ROWS, :] * wg[k:k + 1, :] for k in range(KF))
            dhuv = sum(winv[2 - k:2 - k + ROWS, :] * wv[k:k + 1, :] for k in range(KF))
            dhug_ref[r, :] = dhug.astype(BF16)
            dhuv_ref[r, :] = dhuv.astype(BF16)
            return carry

        lax.fori_loop(0, tm // ROWS, rows2, 0)
        dextg_ref[tm:tm + HALO_F, :] = dextg_ref[0:HALO_F, :]
        dextv_ref[tm:tm + HALO_F, :] = dextv_ref[0:HALO_F, :]

        acc_wdown[...] += _tn(gbuf_ref[...], drb_ref[...])
        x1t = x1b_ref[...].T
        acc_wup[0] += _nn(x1t, dhug_ref[...])
        acc_wup[1] += _nn(x1t, dhuv_ref[...])
        dx1_ref[0] = _nt(dhug_ref[...], wup_ref[0, 0]) + _nt(dhuv_ref[...], wup_ref[1, 0])

        @pl.when(i == nt - 1)
        def _():
            for g in range(2):
                dcfb_ref[g, 0] = jnp.sum(acc_cfb[8 * g:8 * g + 8, :], axis=0, keepdims=True)
                for k in range(KF):
                    r0 = 8 * (g * KF + k)
                    dcfw_ref[g, 0, k:k + 1, :] = jnp.sum(acc_cfw[r0:r0 + 8, :], axis=0, keepdims=True)
            cps = [pltpu.make_async_copy(acc_wup.at[0], dwup_ref.at[0, f], sem.at[0]),
                   pltpu.make_async_copy(acc_wup.at[1], dwup_ref.at[1, f], sem.at[1]),
                   pltpu.make_async_copy(acc_wdown, dwdown_ref.at[f], sem.at[2])]
            for cp in cps:
                cp.start()
            for cp in cps:
                cp.wait()

    rev = lambda i: nt - 1 - i
    row = pl.BlockSpec((tm, D), lambda f, i: (rev(i), 0))
    pair = lambda r, c: pl.BlockSpec((2, 1, r, c), lambda f, i: (0, f, 0, 0))
    return pl.pallas_call(
        body, name="ffn_backward", grid=(N_F, nt),
        in_specs=[row, row, _full(ln1_g.shape), _full(ln1_b.shape),
                  pl.BlockSpec((2, 1, tm, W_UP_BLK), lambda f, i: (0, f, rev(i), 0)),
                  pl.BlockSpec((2, 1, HALO_F, W_UP_BLK),
                               lambda f, i: (0, f, jnp.maximum(rev(i) * halo_blocks - 1, 0), 0)),
                  pair(D, W_UP_BLK), pair(KF, W_UP_BLK), pair(1, W_UP_BLK),
                  pl.BlockSpec((1, W_UP_BLK, D), lambda f, i: (f, 0, 0))],
        out_specs=[ANY, ANY, pair(KF, W_UP_BLK), pair(1, W_UP_BLK),
                   pl.BlockSpec((1, tm, D), lambda f, i: (f, rev(i), 0))],
        out_shape=[jax.ShapeDtypeStruct((2, N_F, D, W_UP_BLK), F32),
                   jax.ShapeDtypeStruct((N_F, W_UP_BLK, D), F32),
                   jax.ShapeDtypeStruct((2, N_F, KF, W_UP_BLK), F32),
                   jax.ShapeDtypeStruct((2, N_F, 1, W_UP_BLK), F32),
                   jax.ShapeDtypeStruct((N_F, t, D), F32)],
        scratch_shapes=[pltpu.VMEM((tm, D), BF16), pltpu.VMEM((tm, D), BF16),
                        pltpu.VMEM((tm, W_UP_BLK), F32),
                        pltpu.VMEM((tm + HALO_F, W_UP_BLK), F32), pltpu.VMEM((tm + HALO_F, W_UP_BLK), F32),
                        pltpu.VMEM((tm + HALO_F, W_UP_BLK), F32), pltpu.VMEM((tm + HALO_F, W_UP_BLK), F32),
                        pltpu.VMEM((tm, W_UP_BLK), BF16), pltpu.VMEM((tm, W_UP_BLK), BF16),
                        pltpu.VMEM((tm, W_UP_BLK), BF16),
                        pltpu.VMEM((2, D, W_UP_BLK), F32), pltpu.VMEM((W_UP_BLK, D), F32),
                        pltpu.VMEM((2 * KF * 8, W_UP_BLK), F32), pltpu.VMEM((16, W_UP_BLK), F32),
                        pltpu.SemaphoreType.DMA((3,))],
        compiler_params=_params(("arbitrary", "arbitrary")),
    )(dr2, xhat1, ln1_g, ln1_b, hu4, hu4, wup4, cfw4, cfb4, wdown)


def mix_backward(x, h, dx1p, dr2, xhat1, rstd1, win_g, b_in, ln_a_g, ln_a_b, w_spatial, bst,
                 conv_b_w, conv_b_b, ln_b_g, ln_b_b, wout, ln1_g, tm):
    t = x.shape[0]
    nt = t // tm
    n_chunks = tm // CHUNK
    halo_blocks = tm // HALO_B

    def body(x_ref, h_ref, halo_ref, dx1p_ref, dr2_ref, xh1_ref, rstd1_ref, win_ref, ga_ref, ba_ref,
             ws_ref, bst_ref, cw_ref, cb_ref, gb_ref, bb_ref, wout_ref, g1_ref,
             gx_ref, dwin_ref, dwout_ref, dcw_ref, small_ref,
             ext_ref, dext_ref, y_ref, dy_ref, dh_ref, dr1_ref, dmb_ref, wsm_ref,
             acc_win, acc_wout, acc_bin, acc_lnag, acc_lnab, acc_ws, acc_bs, acc_cbb, acc_lnbg,
             acc_lnbb, acc_bout, acc_ln1g, acc_ln1b, acc_cw, sem):
        i = pl.program_id(0)
        first_tile = i == nt - 1
        accs = [acc_win, acc_wout, acc_bin, acc_lnag, acc_lnab, acc_ws, acc_bs, acc_cbb, acc_lnbg,
                acc_lnbb, acc_bout, acc_ln1g, acc_ln1b, acc_cw]

        @pl.when(i == 0)
        def _():
            for acc in accs:
                acc[...] = jnp.zeros(acc.shape, F32)
            dext_ref[tm:tm + HALO_B, :] = jnp.zeros((HALO_B, D_B), F32)
            mask = _tril_mask()
            for hd in range(HEADS):
                wsm_ref[hd] = jnp.where(mask, ws_ref[hd], 0.0).astype(BF16)

        def ln1_rows(bi, carry):
            r = _rows(bi)
            dx1 = ALPHA * dr2_ref[r, :] + ((dx1p_ref[0, r, :] + dx1p_ref[1, r, :])
                                          + (dx1p_ref[2, r, :] + dx1p_ref[3, r, :]))
            xhat = xh1_ref[r, :]
            acc_ln1g[...] += _rsum8(dx1 * xhat)
            acc_ln1b[...] += _rsum8(dx1)
            dr1 = _ln_bwd(dx1 * g1_ref[...], xhat, rstd1_ref[r, 0:1])
            acc_bout[...] += _rsum8(dr1)
            dr1_ref[r, :] = dr1
            dmb_ref[r, :] = dr1.astype(BF16)
            return carry

        lax.fori_loop(0, tm // ROWS, ln1_rows, 0)
        dy_ref[...] = _nt(dmb_ref[...], wout_ref[...])

        ha = halo_ref[:, 0:D_B]
        hg = halo_ref[:, D_B:2 * D_B]
        ext_ref[0:HALO_B, :] = jnp.where(first_tile, 0.0, 1.0) * (ha * _sigmoid(hg))

        def chunk(ci, carry):
            r = _rows(ci, CHUNK)
            hu, hv = h_ref[r, 0:D_A], h_ref[r, D_A:2 * D_A]
            u, cdf_u, cdf_v, xhats, rstds, vns, svs = _mixer_a_fwd(hu, hv, ga_ref, ba_ref, wsm_ref, bst_ref)
            for hd in range(HEADS):
                sl = slice(hd * HEAD_DIM, (hd + 1) * HEAD_DIM)
                rows8 = slice(8 * hd, 8 * hd + 8)
                dy_a = dy_ref[r, sl]
                y_ref[r, sl] = (u[:, sl] * svs[hd]).astype(BF16)
                du = dy_a * svs[hd]
                dsv = dy_a * u[:, sl]
                dsvb = dsv.astype(BF16)
                acc_bs[hd] += dsv
                acc_ws[hd] += _nt(dsvb, vns[hd])
                dvn = _tn(wsm_ref[hd], dsvb)
                acc_lnag[rows8, :] += _rsum8(dvn * xhats[hd])
                acc_lnab[rows8, :] += _rsum8(dvn)
                dv = _ln_bwd(dvn * ga_ref[hd:hd + 1, :], xhats[hd], rstds[hd])
                hus, hvs = hu[:, sl], hv[:, sl]
                dh_ref[r, sl] = du * (cdf_u[:, sl] + hus * jnp.exp(-0.5 * hus * hus) * INV_SQRT_2PI)
                dh_ref[r, D_A + hd * HEAD_DIM:D_A + (hd + 1) * HEAD_DIM] = dv * (
                    cdf_v[:, sl] + hvs * jnp.exp(-0.5 * hvs * hvs) * INV_SQRT_2PI)
            a_b = h_ref[r, 2 * D_A:2 * D_A + D_B]
            g_b = h_ref[r, 2 * D_A + D_B:D_IN]
            ext_ref[pl.ds(pl.multiple_of(HALO_B + ci * CHUNK, 32), CHUNK), :] = a_b * _sigmoid(g_b)
            return carry

        lax.fori_loop(0, n_chunks, chunk, 0)

        def conv_rows(bi, carry):
            base = pl.multiple_of(bi * ROWS, ROWS)
            r = pl.ds(base, ROWS)
            acc, _ = _conv_b_block(ext_ref, base, cw_ref)
            xhat, rstd = _ln_stats(acc + cb_ref[...])
            yb2 = xhat * gb_ref[...] + bb_ref[...]
            sg = _sigmoid(yb2)
            y_ref[r, D_A:D] = (yb2 * sg).astype(BF16)
            dyb2 = dy_ref[r, D_A:D] * (sg * (1.0 + yb2 * (1.0 - sg)))
            acc_lnbg[...] += _rsum8(dyb2 * xhat)
            acc_lnbb[...] += _rsum8(dyb2)
            dyb1 = _ln_bwd(dyb2 * gb_ref[...], xhat, rstd)
            acc_cbb[...] += _rsum8(dyb1)
            dext_ref[r, :] = dyb1
            return carry

        lax.fori_loop(0, tm // ROWS, conv_rows, 0)

        def convt_rows(bi, carry):
            base = pl.multiple_of(bi * ROWS, ROWS)
            r = pl.ds(base, ROWS)
            dwin = dext_ref[pl.ds(base, ROWS + HALO_B), :]
            win = ext_ref[pl.ds(base, ROWS + HALO_B), :]
            dyb1 = dwin[0:ROWS, :]
            dyb0 = jnp.zeros((ROWS, D_B), F32)
            for k in range(KB):
                dyb0 = dyb0 + dwin[30 - k:30 - k + ROWS, :] * cw_ref[k:k + 1, :]
                acc_cw[8 * k:8 * k + 8, :] += _rsum8(dyb1 * win[2 + k:2 + k + ROWS, :])
            a_b = h_ref[r, 2 * D_A:2 * D_A + D_B]
            sg = _sigmoid(h_ref[r, 2 * D_A + D_B:D_IN])
            dh_ref[r, 2 * D_A:2 * D_A + D_B] = dyb0 * sg
            dh_ref[r, 2 * D_A + D_B:D_IN] = dyb0 * a_b * sg * (1.0 - sg)
            return carry

        lax.fori_loop(0, tm // ROWS, convt_rows, 0)
        dext_ref[tm:tm + HALO_B, :] = dext_ref[0:HALO_B, :]

        acc_wout[...] += _tn(y_ref[...], dmb_ref[...])
        xt = x_ref[...].T.astype(BF16)
        dx = ALPHA * dr1_ref[...]
        for j in range(N_DEV):
            cols = slice(j * W_IN_BLK, (j + 1) * W_IN_BLK)
            dhj = dh_ref[:, cols]
            acc_bin[:, cols] += _rsum8(dhj)
            dhb = dhj.astype(BF16)
            acc_win[j] += _nn(xt, dhb)
            dx = dx + _nt(dhb, win_ref[j])
        gx_ref[...] = dx

        @pl.when(i == nt - 1)
        def _():
            cps = [pltpu.make_async_copy(acc_win, dwin_ref, sem.at[0]),
                   pltpu.make_async_copy(acc_wout, dwout_ref, sem.at[1])]
            for cp in cps:
                cp.start()
            small_ref[...] = jnp.zeros(small_ref.shape, F32)

            def put_row_vector(row0, acc):
                vec = jnp.sum(acc[...], axis=0, keepdims=True)
                for k in range(vec.shape[1] // 128):
                    small_ref[row0 + k:row0 + k + 1, :] = vec[:, k * 128:(k + 1) * 128]

            put_row_vector(S_BIN, acc_bin)
            put_row_vector(S_CBB, acc_cbb)
            put_row_vector(S_LNBG, acc_lnbg)
            put_row_vector(S_LNBB, acc_lnbb)
            put_row_vector(S_BOUT, acc_bout)
            put_row_vector(S_LN1G, acc_ln1g)
            put_row_vector(S_LN1B, acc_ln1b)
            mask = _tril_mask()
            for hd in range(HEADS):
                rows8 = slice(8 * hd, 8 * hd + 8)
                small_ref[S_LNAG + hd:S_LNAG + hd + 1, :] = jnp.sum(acc_lnag[rows8, :], axis=0, keepdims=True)
                small_ref[S_LNAB + hd:S_LNAB + hd + 1, :] = jnp.sum(acc_lnab[rows8, :], axis=0, keepdims=True)
                small_ref[S_WS + hd * CHUNK:S_WS + (hd + 1) * CHUNK, :] = jnp.where(mask, acc_ws[hd], 0.0)
                small_ref[S_BS + hd:S_BS + hd + 1, :] = jnp.sum(acc_bs[hd].T, axis=0, keepdims=True)
            for k in range(KB):
                dcw_ref[k:k + 1, :] = jnp.sum(acc_cw[8 * k:8 * k + 8, :], axis=0, keepdims=True)
            for cp in cps:
                cp.wait()

    rev = lambda i: nt - 1 - i
    row = lambda w: pl.BlockSpec((tm, w), lambda i: (rev(i), 0))
    return pl.pallas_call(
        body, name="mix_backward", grid=(nt,),
        in_specs=[row(D), row(D_IN),
                  pl.BlockSpec((HALO_B, 2 * D_B), lambda i: (jnp.maximum(rev(i) * halo_blocks - 1, 0), 1)),
                  pl.BlockSpec((N_F, tm, D), lambda i: (0, rev(i), 0)),
                  row(D), row(D), row(128), _resident(win_g.shape), _full(ln_a_g.shape),
                  _full(ln_a_b.shape), _full(w_spatial.shape), _full(bst.shape), _full(conv_b_w.shape),
                  _full(conv_b_b.shape), _full(ln_b_g.shape), _full(ln_b_b.shape),
                  _resident(wout.shape), _full(ln1_g.shape)],
        out_specs=[row(D), ANY, ANY, _full((KB, D_B)), _full((S_MIX_ROWS, 128))],
        out_shape=[jax.ShapeDtypeStruct((t, D), F32), jax.ShapeDtypeStruct((N_DEV, D, W_IN_BLK), F32),
                   jax.ShapeDtypeStruct((D, D), F32), jax.ShapeDtypeStruct((KB, D_B), F32),
                   jax.ShapeDtypeStruct((S_MIX_ROWS, 128), F32)],
        scratch_shapes=[pltpu.VMEM((tm + HALO_B, D_B), F32), pltpu.VMEM((tm + HALO_B, D_B), F32),
                        pltpu.VMEM((tm, D), BF16), pltpu.VMEM((tm, D), F32), pltpu.VMEM((tm, D_IN), F32),
                        pltpu.VMEM((tm, D), F32), pltpu.VMEM((tm, D), BF16),
                        pltpu.VMEM((HEADS, CHUNK, CHUNK), BF16),
                        pltpu.VMEM((N_DEV, D, W_IN_BLK), F32), pltpu.VMEM((D, D), F32),
                        pltpu.VMEM((8, D_IN), F32), pltpu.VMEM((8 * HEADS, HEAD_DIM), F32),
                        pltpu.VMEM((8 * HEADS, HEAD_DIM), F32), pltpu.VMEM((HEADS, CHUNK, CHUNK), F32),
                        pltpu.VMEM((HEADS, CHUNK, CHUNK), F32), pltpu.VMEM((8, D_B), F32),
                        pltpu.VMEM((8, D_B), F32), pltpu.VMEM((8, D_B), F32), pltpu.VMEM((8, D), F32),
                        pltpu.VMEM((8, D), F32), pltpu.VMEM((8, D), F32), pltpu.VMEM((8 * KB, D_B), F32),
                        pltpu.SemaphoreType.DMA((2,))],
        compiler_params=_params(("arbitrary",)),
    )(x, h, h, dx1p, dr2, xhat1, rstd1, win_g, ln_a_g, ln_a_b, w_spatial, bst, conv_b_w, conv_b_b,
      ln_b_g, ln_b_b, wout, ln1_g)


def _rows128(a):
    return a.reshape(-1, 128)


def _pack_conv(cb, cf):
    out = jnp.zeros((40, 768), F32)
    out = out.at[0:KB, 0:64].set(cb)
    return out.at[32:32 + KF, 0:W_UP_BLK].set(cf)


def kernel(x, w_in, b_in, ln_a_g, ln_a_b, w_spatial, b_spatial, conv_b_w, conv_b_b, ln_b_g, ln_b_b, w_out, b_out, ln1_g, ln1_b, w_up, conv_f_w, conv_f_b, w_down, ln2_g, ln2_b, loss_target, m_w_in, m_b_in, m_ln_a_g, m_ln_a_b, m_w_spatial, m_b_spatial, m_conv_b_w, m_conv_b_b, m_ln_b_g, m_ln_b_b, m_w_out, m_b_out, m_ln1_g, m_ln1_b, m_w_up, m_conv_f_w, m_conv_f_b, m_w_down, m_ln2_g, m_ln2_b, v_w_in, v_b_in, v_ln_a_g, v_ln_a_b, v_w_spatial, v_b_spatial, v_conv_b_w, v_conv_b_b, v_ln_b_g, v_ln_b_b, v_w_out, v_b_out, v_ln1_g, v_ln1_b, v_w_up, v_conv_f_w, v_conv_f_b, v_w_down, v_ln2_g, v_ln2_b):
    t = x.shape[1]
    x2 = x.reshape(t, D)
    target = loss_target.reshape(t, D)
    tm_fwd = min(t, 512)
    tm_bwd = min(t, 256)

    xi, yi, ci = _mesh_pos()
    jidx = jnp.stack([_lid(px, py, ci) for px, py in _chip_patterns(xi, yi)]).astype(jnp.int32)

    win_g, wout_g, wup_g, wdown_g, conv_g = all_gather_weights(
        w_in, w_out, w_up, w_down, _pack_conv(conv_b_w, conv_f_w))
    wout_full = wout_g.reshape(D, D)
    wdown4 = wdown_g.reshape(N_F, W_UP_BLK, D)
    conv_b_full = conv_g[:, 0:KB, 0:64].transpose(1, 0, 2).reshape(KB, D_B)
    cfw = conv_g[:, 32:32 + KF, 0:W_UP_BLK]
    cfb = conv_f_b.reshape(N_DEV, W_UP_BLK)
    row = lambda a: a.reshape(1, -1)
    bst = b_spatial.T

    h, xhat1, rstd1 = mix_forward(
        x2, win_g, row(b_in), ln_a_g, ln_a_b, w_spatial, bst, conv_b_full, row(conv_b_b),
        row(ln_b_g), row(ln_b_b), wout_full, row(b_out), row(ln1_g), row(ln1_b), tm_fwd)
    hu, dr2, loss_part, s_ln2 = ffn_forward(
        xhat1, row(ln1_g), row(ln1_b), wup_g, cfw, cfb, wdown4, row(ln2_g), row(ln2_b), target, tm_bwd)
    loss = lax.psum(jnp.sum(loss_part), ("x", "y", "c")) * (0.5 / D)

    dwup, dwdown, dcfw, dcfb, dx1p = ffn_backward(
        dr2, xhat1, row(ln1_g), row(ln1_b), hu, wup_g, cfw, cfb, wdown4, tm_bwd)
    grad_x, dwin, dwout, dcw, s_mix = mix_backward(
        x2, h, dx1p, dr2, xhat1, rstd1, win_g, row(b_in), ln_a_g, ln_a_b, w_spatial, bst,
        conv_b_full, row(conv_b_b), row(ln_b_g), row(ln_b_b), wout_full, row(ln1_g), tm_bwd)

    dcfb_rows = jnp.pad(dcfb.reshape(-1, 128), ((0, 4), (0, 0)))
    svec = jnp.concatenate([s_mix, dcfb_rows, s_ln2], axis=0)
    dconv = jnp.zeros((N_DEV, 40, 768), F32)
    dconv = dconv.at[:, 0:KB, 0:64].set(dcw.reshape(KB, N_DEV, 64).transpose(1, 0, 2))
    dconv = dconv.at[:, 32:32 + KF, 0:W_UP_BLK].set(dcfw.reshape(N_DEV, KF, W_UP_BLK))
    grads = [dwin, dwout.reshape(N_DEV, D // N_DEV, D), dwup.reshape(N_DEV, D, W_UP_BLK),
             dwdown.reshape(N_DEV, D_FF // N_DEV, D), dconv]
    names = ["w_in", "w_out", "w_up", "w_down", "conv"]
    row_blocks = [512, 128, 256, 352, 40]

    *lands, sv_land = exchange_with_sibling(grads, svec)
    partials = [chip_partials("chip_partials_" + nm, g, l, jidx, rb)
                for nm, g, l, rb in zip(names, grads, lands, row_blocks)]
    *recvs, sv_slots = exchange_between_chips(partials, svec, sv_land)

    shard_w = [w_in, w_out, w_up, w_down, _pack_conv(conv_b_w, conv_f_w)]
    shard_m = [m_w_in, m_w_out, m_w_up, m_w_down, _pack_conv(m_conv_b_w, m_conv_f_w)]
    shard_v = [v_w_in, v_w_out, v_w_up, v_w_down, _pack_conv(v_conv_b_w, v_conv_f_w)]
    big = {}
    for nm, g, l, r, w, m, v, rb in zip(names, grads, lands, recvs, shard_w, shard_m, shard_v, row_blocks):
        big[nm] = reduce_and_adamw("reduce_adamw_" + nm, g, l, r, w, m, v, jidx, rb)
    for k in range(4):
        packed = big["conv"][k]
        big.setdefault("conv_b_w", []).append(packed[0:KB, 0:64])
        big.setdefault("conv_f_w", []).append(packed[32:32 + KF, 0:W_UP_BLK])

    small_w = dict(b_in=b_in, ln_a_g=ln_a_g, ln_a_b=ln_a_b, w_spatial=w_spatial, b_spatial=b_spatial,
                   conv_b_b=conv_b_b, ln_b_g=ln_b_g, ln_b_b=ln_b_b, b_out=b_out, ln1_g=ln1_g,
                   ln1_b=ln1_b, conv_f_b=conv_f_b, ln2_g=ln2_g, ln2_b=ln2_b)
    small_m = dict(b_in=m_b_in, ln_a_g=m_ln_a_g, ln_a_b=m_ln_a_b, w_spatial=m_w_spatial,
                   b_spatial=m_b_spatial, conv_b_b=m_conv_b_b, ln_b_g=m_ln_b_g, ln_b_b=m_ln_b_b,
                   b_out=m_b_out, ln1_g=m_ln1_g, ln1_b=m_ln1_b, conv_f_b=m_conv_f_b, ln2_g=m_ln2_g,
                   ln2_b=m_ln2_b)
    small_v = dict(b_in=v_b_in, ln_a_g=v_ln_a_g, ln_a_b=v_ln_a_b, w_spatial=v_w_spatial,
                   b_spatial=v_b_spatial, conv_b_b=v_conv_b_b, ln_b_g=v_ln_b_g, ln_b_b=v_ln_b_b,
                   b_out=v_b_out, ln1_g=v_ln1_g, ln1_b=v_ln1_b, conv_f_b=v_conv_f_b, ln2_g=v_ln2_g,
                   ln2_b=v_ln2_b)
    order = [nm for nm, _, _ in SMALL_LAYOUT]
    small_out = small_adamw(sv_slots, [_rows128(small_w[nm]) for nm in order],
                            [_rows128(small_m[nm]) for nm in order], [_rows128(small_v[nm]) for nm in order])
    n_small = len(order)
    small = {nm: [small_out[k * n_small + p].reshape(small_w[nm].shape) for k in range(4)]
             for p, nm in enumerate(order)}

    weights = ["w_in", "b_in", "ln_a_g", "ln_a_b", "w_spatial", "b_spatial", "conv_b_w", "conv_b_b",
               "ln_b_g", "ln_b_b", "w_out", "b_out", "ln1_g", "ln1_b", "w_up", "conv_f_w", "conv_f_b",
               "w_down", "ln2_g", "ln2_b"]
    result = lambda nm, k: big[nm][k] if nm in big else small[nm][k]
    return (loss, grad_x.reshape(x.shape), *[result(nm, 0) for nm in weights],
            *[result(nm, 1) for nm in weights], *[result(nm, 2) for nm in weights],
            *[result(nm, 3) for nm in weights])
```

```python
import functools
import math

import jax
import jax.numpy as jnp
from jax import lax
from jax.experimental import pallas as pl
from jax.experimental.pallas import tpu as pltpu

F32 = jnp.float32
BF16 = jnp.bfloat16

D = 1024
D_A = 512
D_B = 512
HEADS = 4
HEAD_DIM = 128
CHUNK = 128
KB = 31
KF = 3
D_FF = 2816
D_IN = 2048
N_DEV = 8
W_IN_BLK = D_IN // N_DEV
W_UP_BLK = 2 * D_FF // N_DEV
N_F = 4
LN_EPS = 1e-5
ALPHA = 2.0 ** 0.25

ADAM_LR = 0.001
ADAM_B1 = 0.9
ADAM_B2 = 0.999
ADAM_EPS = 1e-08
ADAM_WD = 0.01
ADAM_STEP = 10

INV_SQRT2 = 1.0 / math.sqrt(2.0)
INV_SQRT_2PI = 1.0 / math.sqrt(2.0 * math.pi)

HALO_B = 32
HALO_F = 8
ROWS = 64
VMEM_LIMIT = 58 * 1024 * 1024

MESH = pl.DeviceIdType.MESH
ANY = pl.BlockSpec(memory_space=pl.ANY)
VMEM = pl.BlockSpec(memory_space=pltpu.VMEM)

S_BIN, S_LNAG, S_LNAB, S_WS, S_BS, S_CBB, S_LNBG, S_LNBB, S_BOUT, S_LN1G, S_LN1B = (
    0, 16, 24, 32, 544, 552, 560, 568, 576, 584, 592)
S_MIX_ROWS = 600
S_CFB = 600
S_LN2G = 648
S_LN2B = 656
S_ROWS = 664


def _tn(a, b):
    return lax.dot_general(a, b, (((0,), (0,)), ((), ())), preferred_element_type=F32)


def _nt(a, b):
    return lax.dot_general(a, b, (((1,), (1,)), ((), ())), preferred_element_type=F32)


def _nn(a, b):
    return jnp.dot(a, b, preferred_element_type=F32)


def _sigmoid(x):
    return 1.0 / (1.0 + jnp.exp(-x))


def _ln_stats(x):
    mu = jnp.mean(x, axis=-1, keepdims=True)
    xc = x - mu
    var = jnp.mean(xc * xc, axis=-1, keepdims=True)
    rstd = lax.rsqrt(var + LN_EPS)
    return xc * rstd, rstd


def _ln_bwd(dxhat, xhat, rstd):
    m1 = jnp.mean(dxhat, axis=-1, keepdims=True)
    m2 = jnp.mean(dxhat * xhat, axis=-1, keepdims=True)
    return rstd * (dxhat - m1 - xhat * m2)


def _rsum8(x):
    r, n = x.shape
    return x.reshape(r // 8, 8, n).sum(axis=0)


def _rows(i, n=ROWS):
    return pl.ds(i * n, n)


def _loop(n, body):
    for i in range(n):
        body(i)


def _tril_mask():
    r = lax.broadcasted_iota(jnp.int32, (CHUNK, CHUNK), 0)
    c = lax.broadcasted_iota(jnp.int32, (CHUNK, CHUNK), 1)
    return c <= r


def _mixer_a_fwd(hu, hv, ga_ref, ba_ref, wsm_ref, bst_ref):
    cdf_u = 0.5 * (1.0 + lax.erf(hu * INV_SQRT2))
    cdf_v = 0.5 * (1.0 + lax.erf(hv * INV_SQRT2))
    u = hu * cdf_u
    v = hv * cdf_v
    xhats, rstds, vns, svs = [], [], [], []
    for hd in range(HEADS):
        sl = slice(hd * HEAD_DIM, (hd + 1) * HEAD_DIM)
        xhat, rstd = _ln_stats(v[:, sl])
        vn = (xhat * ga_ref[hd:hd + 1, :] + ba_ref[hd:hd + 1, :]).astype(BF16)
        sv = _nn(wsm_ref[hd], vn) + bst_ref[:, hd:hd + 1]
        xhats.append(xhat)
        rstds.append(rstd)
        vns.append(vn)
        svs.append(sv)
    return u, cdf_u, cdf_v, xhats, rstds, vns, svs


def _shifted(win):
    n = win.shape[0]
    return [win] + [pltpu.roll(win, n - s, 0) for s in range(1, 8)]


def _tap(shifted, offset):
    s = offset % 8
    return shifted[s][offset - s:offset - s + ROWS, :]


def _conv_b_block(ext_ref, base, cw_ref):
    win = _shifted(ext_ref[pl.ds(base, ROWS + HALO_B), :])
    acc = jnp.zeros((ROWS, D_B), F32)
    for k in range(KB):
        acc = acc + _tap(win, 2 + k) * cw_ref[k:k + 1, :]
    return acc, win


def _conv_f_block(ext_ref, base, w):
    win = ext_ref[pl.ds(base, ROWS + HALO_F), :]
    out = (win[6:6 + ROWS, :] * w[0:1, :] + win[7:7 + ROWS, :] * w[1:2, :]
           + win[8:8 + ROWS, :] * w[2:3, :])
    return out, win


def _params(sem, **kw):
    return pltpu.CompilerParams(dimension_semantics=sem, vmem_limit_bytes=VMEM_LIMIT, **kw)


def _resident(shape):
    zeros = (0,) * len(shape)
    return pl.BlockSpec(shape, lambda *_: zeros, pipeline_mode=pl.Buffered(1))


def _full(shape):
    zeros = (0,) * len(shape)
    return pl.BlockSpec(shape, lambda *_: zeros)


def _mesh_pos():
    return lax.axis_index("x"), lax.axis_index("y"), lax.axis_index("c")


def _chip_patterns(x, y):
    return [(x, y), (1 - x, y), (x, 1 - y), (1 - x, 1 - y)]


def _lid(x, y, c):
    return 4 * x + 2 * y + c


def all_gather_weights(w_in, w_out, w_up, w_down, convp):
    srcs = [w_in, w_out, w_up, w_down, convp]
    n = len(srcs)
    out_dt = [BF16, BF16, BF16, BF16, F32]

    def body(win_ref, wout_ref, wup_ref, wdown_ref, convp_ref,
             gin_ref, gout_ref, gup_ref, gdown_ref, gconv_ref,
             sin_ref, sout_ref, sup_ref, sdown_ref, send_sems, recv_sems, local_sems):
        x, y, c = _mesh_pos()
        me, sib = (x, y, c), (x, y, 1 - c)
        chips = _chip_patterns(x, y)[1:]
        sin_ref[...] = win_ref[...].astype(BF16)
        sout_ref[...] = wout_ref[...].astype(BF16)
        sup_ref[...] = wup_ref[...].astype(BF16)
        sdown_ref[...] = wdown_ref[...].astype(BF16)
        mine = [sin_ref, sout_ref, sup_ref, sdown_ref, convp_ref]
        outs = [gin_ref, gout_ref, gup_ref, gdown_ref, gconv_ref]

        def copy(a, k, block, to, src=None):
            blk = outs[a].at[_lid(*block)]
            return pltpu.make_async_remote_copy(
                src_ref=blk if src is None else src, dst_ref=blk,
                send_sem=send_sems.at[a, k], recv_sem=recv_sems.at[a, k],
                device_id=to, device_id_type=MESH)

        local = [pltpu.make_async_copy(mine[a], outs[a].at[_lid(*me)], local_sems.at[a])
                 for a in range(n)]
        for cp in local:
            cp.start()
        first = []
        for a in range(n):
            first.append(copy(a, 0, me, sib, src=mine[a]))
            for j, chip in enumerate(chips):
                first.append(copy(a, 1 + j, me, (*chip, c), src=mine[a]))
        for cp in first:
            cp.start()
        passed = []
        for j, chip in enumerate(chips):
            for a in range(n):
                copy(a, 1 + j, (*chip, c), me).wait_recv()
                cp = copy(a, 4 + j, (*chip, c), sib)
                cp.start()
                passed.append(cp)
        for a in range(n):
            copy(a, 0, sib, me).wait_recv()
            for j, chip in enumerate(chips):
                copy(a, 4 + j, (*chip, 1 - c), me).wait_recv()
        for cp in first + passed:
            cp.wait_send()
        for cp in local:
            cp.wait()

    return pl.pallas_call(
        body, name="all_gather_weights",
        out_shape=[jax.ShapeDtypeStruct((N_DEV,) + s.shape, dt) for s, dt in zip(srcs, out_dt)],
        in_specs=[VMEM] * n, out_specs=[ANY] * n,
        scratch_shapes=[pltpu.VMEM(s.shape, BF16) for s in srcs[:4]] + [
            pltpu.SemaphoreType.DMA((n, 7)), pltpu.SemaphoreType.DMA((n, 7)),
            pltpu.SemaphoreType.DMA((n,))],
        compiler_params=pltpu.CompilerParams(vmem_limit_bytes=VMEM_LIMIT),
    )(*srcs)


def exchange_with_sibling(grads, svec):
    n = len(grads)

    def body(*refs):
        g = refs[:n]
        sv = refs[n]
        land = refs[n + 1:2 * n + 1]
        sv_land = refs[2 * n + 1]
        send_sems, recv_sems, sv_sems = refs[2 * n + 2:]
        x, y, c = _mesh_pos()
        sib = (x, y, 1 - c)
        copies = []
        for a in range(n):
            for k, (px, py) in enumerate(_chip_patterns(x, y)):
                copies.append(pltpu.make_async_remote_copy(
                    src_ref=g[a].at[_lid(px, py, 1 - c)], dst_ref=land[a].at[k],
                    send_sem=send_sems.at[a, k], recv_sem=recv_sems.at[a, k],
                    device_id=sib, device_id_type=MESH))
        copies.append(pltpu.make_async_remote_copy(
            src_ref=sv, dst_ref=sv_land, send_sem=sv_sems.at[0], recv_sem=sv_sems.at[1],
            device_id=sib, device_id_type=MESH))
        for cp in copies:
            cp.start()
        for cp in copies:
            cp.wait()

    return pl.pallas_call(
        body, name="exchange_with_sibling",
        out_shape=[jax.ShapeDtypeStruct((4,) + g.shape[1:], F32) for g in grads]
        + [jax.ShapeDtypeStruct(svec.shape, F32)],
        in_specs=[ANY] * (n + 1), out_specs=[ANY] * (n + 1),
        scratch_shapes=[pltpu.SemaphoreType.DMA((n, 4)), pltpu.SemaphoreType.DMA((n, 4)),
                        pltpu.SemaphoreType.DMA((2,))],
    )(*grads, svec)


def exchange_between_chips(partials, svec, sv_land):
    n = len(partials)

    def body(*refs):
        p = refs[:n]
        sv_ref, svl_ref = refs[n], refs[n + 1]
        land = refs[n + 2:2 * n + 2]
        sv_slots = refs[2 * n + 2]
        chip_sv, send_sems, recv_sems, sv_send, sv_recv, local_sem = refs[2 * n + 3:]
        x, y, c = _mesh_pos()
        q = 2 * x + y
        chip_sv[...] = sv_ref[...] + svl_ref[...]
        local = pltpu.make_async_copy(chip_sv, sv_slots.at[q], local_sem)
        local.start()
        copies = []
        for k, (px, py) in enumerate(_chip_patterns(x, y)[1:]):
            to = (px, py, c)
            for a in range(n):
                copies.append(pltpu.make_async_remote_copy(
                    src_ref=p[a].at[k], dst_ref=land[a].at[k],
                    send_sem=send_sems.at[a, k], recv_sem=recv_sems.at[a, k],
                    device_id=to, device_id_type=MESH))
        for cp in copies:
            cp.start()
        sv_copies = []
        for k, (px, py) in enumerate(_chip_patterns(x, y)[1:]):
            cp = pltpu.make_async_remote_copy(
                src_ref=chip_sv, dst_ref=sv_slots.at[q],
                send_sem=sv_send.at[k], recv_sem=sv_recv.at[k],
                device_id=(px, py, c), device_id_type=MESH)
            cp.start()
            sv_copies.append(cp)
        for cp in copies:
            cp.wait()
        for k, (px, py) in enumerate(_chip_patterns(x, y)[1:]):
            sv_copies[k].wait_send()
            pltpu.make_async_remote_copy(
                src_ref=chip_sv, dst_ref=sv_slots.at[2 * px + py],
                send_sem=sv_send.at[k], recv_sem=sv_recv.at[k],
                device_id=(px, py, c), device_id_type=MESH).wait_recv()
        local.wait()

    return pl.pallas_call(
        body, name="exchange_between_chips",
        out_shape=[jax.ShapeDtypeStruct(p.shape, BF16) for p in partials]
        + [jax.ShapeDtypeStruct((4,) + svec.shape, F32)],
        in_specs=[ANY] * n + [VMEM, VMEM], out_specs=[ANY] * (n + 1),
        scratch_shapes=[pltpu.VMEM(svec.shape, F32),
                        pltpu.SemaphoreType.DMA((n, 3)), pltpu.SemaphoreType.DMA((n, 3)),
                        pltpu.SemaphoreType.DMA((3,)), pltpu.SemaphoreType.DMA((3,)),
                        pltpu.SemaphoreType.DMA],
    )(*partials, svec, sv_land)


def chip_partials(name, g, land, jidx, rb):
    _, r, c = g.shape

    def body(j_ref, g_ref, l_ref, o_ref):
        o_ref[...] = (g_ref[...] + l_ref[...]).astype(BF16)

    return pl.pallas_call(
        body, name=name,
        out_shape=jax.ShapeDtypeStruct((3, r, c), BF16),
        grid_spec=pltpu.PrefetchScalarGridSpec(
            num_scalar_prefetch=1, grid=(3, r // rb),
            in_specs=[pl.BlockSpec((1, rb, c), lambda k, i, j: (j[1 + k], i, 0)),
                      pl.BlockSpec((1, rb, c), lambda k, i, j: (1 + k, i, 0))],
            out_specs=pl.BlockSpec((1, rb, c), lambda k, i, j: (k, i, 0))),
        compiler_params=_params(("arbitrary", "arbitrary")),
    )(jidx, g, land)


def _adamw(w, g, m, v):
    m2 = ADAM_B1 * m + (1.0 - ADAM_B1) * g
    v2 = ADAM_B2 * v + (1.0 - ADAM_B2) * (g * g)
    m_hat = m2 / (1.0 - ADAM_B1 ** ADAM_STEP)
    v_hat = v2 / (1.0 - ADAM_B2 ** ADAM_STEP)
    delta = -ADAM_LR * (m_hat / (jnp.sqrt(v_hat) + ADAM_EPS) + ADAM_WD * w)
    return delta, m2, v2


def reduce_and_adamw(name, g, land, recv, w, m, v, jidx, rb):
    _, r, c = g.shape

    def body(j_ref, g_ref, l_ref, r_ref, w_ref, m_ref, v_ref, go_ref, do_ref, mo_ref, vo_ref):
        grad = (g_ref[0] + l_ref[0]) + r_ref[0].astype(F32) + r_ref[1].astype(F32) + r_ref[2].astype(F32)
        delta, m2, v2 = _adamw(w_ref[...], grad, m_ref[...], v_ref[...])
        go_ref[...] = grad
        do_ref[...] = delta
        mo_ref[...] = m2
        vo_ref[...] = v2

    blk = pl.BlockSpec((rb, c), lambda i, j: (i, 0))
    return pl.pallas_call(
        body, name=name,
        out_shape=[jax.ShapeDtypeStruct((r, c), F32)] * 4,
        grid_spec=pltpu.PrefetchScalarGridSpec(
            num_scalar_prefetch=1, grid=(r // rb,),
            in_specs=[pl.BlockSpec((1, rb, c), lambda i, j: (j[0], i, 0)),
                      pl.BlockSpec((1, rb, c), lambda i, j: (0, i, 0)),
                      pl.BlockSpec((3, rb, c), lambda i, j: (0, i, 0)),
                      blk, blk, blk],
            out_specs=[blk] * 4),
        compiler_params=_params(("arbitrary",)),
    )(jidx, g, land, recv, w, m, v)


SMALL_LAYOUT = [
    ("b_in", S_BIN, 16), ("ln_a_g", S_LNAG, 4), ("ln_a_b", S_LNAB, 4), ("w_spatial", S_WS, 512),
    ("b_spatial", S_BS, 4), ("conv_b_b", S_CBB, 4), ("ln_b_g", S_LNBG, 4), ("ln_b_b", S_LNBB, 4),
    ("b_out", S_BOUT, 8), ("ln1_g", S_LN1G, 8), ("ln1_b", S_LN1B, 8), ("conv_f_b", S_CFB, 44),
    ("ln2_g", S_LN2G, 8), ("ln2_b", S_LN2B, 8),
]


def small_adamw(sv_slots, ws, ms, vs):
    n = len(SMALL_LAYOUT)

    def body(*refs):
        s_ref = refs[0]
        w_refs, m_refs, v_refs = refs[1:1 + n], refs[1 + n:1 + 2 * n], refs[1 + 2 * n:1 + 3 * n]
        outs = refs[1 + 3 * n:]
        for p, (_, row0, rows) in enumerate(SMALL_LAYOUT):
            sl = pl.ds(row0, rows)
            grad = ((s_ref[0, sl, :] + s_ref[1, sl, :]) + s_ref[2, sl, :]) + s_ref[3, sl, :]
            delta, m2, v2 = _adamw(w_refs[p][...], grad, m_refs[p][...], v_refs[p][...])
            outs[p][...] = grad
            outs[n + p][...] = delta
            outs[2 * n + p][...] = m2
            outs[3 * n + p][...] = v2

    shapes = [jax.ShapeDtypeStruct((rows, 128), F32) for _, _, rows in SMALL_LAYOUT]
    return pl.pallas_call(
        body, name="small_adamw", out_shape=shapes * 4,
        in_specs=[VMEM] * (1 + 3 * n), out_specs=[VMEM] * (4 * n),
    )(sv_slots, *ws, *ms, *vs)


def mix_forward(x, win_g, b_in, ln_a_g, ln_a_b, w_spatial, bst, conv_b_w, conv_b_b, ln_b_g, ln_b_b,
                wout, b_out, ln1_g, ln1_b, tm):
    t = x.shape[0]
    n_chunks = tm // CHUNK

    def body(x_ref, win_ref, bin_ref, ga_ref, ba_ref, ws_ref, bst_ref, cw_ref, cb_ref, gb_ref,
             bb_ref, wout_ref, bout_ref, g1_ref, b1_ref,
             h_ref, xhat1_ref, rstd1_ref, ext_ref, y_ref, wsm_ref):
        i = pl.program_id(0)

        @pl.when(i == 0)
        def _():
            ext_ref[0:HALO_B, :] = jnp.zeros((HALO_B, D_B), F32)
            mask = _tril_mask()
            for hd in range(HEADS):
                wsm_ref[hd] = jnp.where(mask, ws_ref[hd], 0.0).astype(BF16)

        xb = x_ref[...].astype(BF16)
        for j in range(N_DEV):
            cols = slice(j * W_IN_BLK, (j + 1) * W_IN_BLK)
            h_ref[:, cols] = _nn(xb, win_ref[j]) + bin_ref[:, cols]

        def chunk(ci):
            r = _rows(ci, CHUNK)
            u, _, _, _, _, _, svs = _mixer_a_fwd(h_ref[r, 0:D_A], h_ref[r, D_A:2 * D_A],
                                                 ga_ref, ba_ref, wsm_ref, bst_ref)
            for hd in range(HEADS):
                sl = slice(hd * HEAD_DIM, (hd + 1) * HEAD_DIM)
                y_ref[r, sl] = (u[:, sl] * svs[hd]).astype(BF16)
            a_b = h_ref[r, 2 * D_A:2 * D_A + D_B]
            g_b = h_ref[r, 2 * D_A + D_B:D_IN]
            ext_ref[pl.ds(HALO_B + ci * CHUNK, CHUNK), :] = a_b * _sigmoid(g_b)

        _loop(n_chunks, chunk)

        def conv_rows(bi):
            base = bi * ROWS
            acc, _ = _conv_b_block(ext_ref, base, cw_ref)
            xhat, _ = _ln_stats(acc + cb_ref[...])
            yb2 = xhat * gb_ref[...] + bb_ref[...]
            y_ref[pl.ds(base, ROWS), D_A:D] = (yb2 * _sigmoid(yb2)).astype(BF16)

        _loop(tm // ROWS, conv_rows)
        ext_ref[0:HALO_B, :] = ext_ref[tm:tm + HALO_B, :]

        mix = _nn(y_ref[...], wout_ref[...]) + bout_ref[...]
        xhat1, rstd1 = _ln_stats(ALPHA * x_ref[...] + mix)
        xhat1_ref[...] = xhat1
        rstd1_ref[...] = jnp.broadcast_to(rstd1, (tm, 128))

    row = lambda w: pl.BlockSpec((tm, w), lambda i: (i, 0))
    return pl.pallas_call(
        body, name="mix_forward", grid=(t // tm,),
        in_specs=[row(D), _resident(win_g.shape), _full(b_in.shape), _full(ln_a_g.shape),
                  _full(ln_a_b.shape), _full(w_spatial.shape), _full(bst.shape),
                  _full(conv_b_w.shape), _full(conv_b_b.shape), _full(ln_b_g.shape),
                  _full(ln_b_b.shape), _resident(wout.shape), _full(b_out.shape),
                  _full(ln1_g.shape), _full(ln1_b.shape)],
        out_specs=[row(D_IN), row(D), row(128)],
        out_shape=[jax.ShapeDtypeStruct((t, D_IN), F32), jax.ShapeDtypeStruct((t, D), F32),
                   jax.ShapeDtypeStruct((t, 128), F32)],
        scratch_shapes=[pltpu.VMEM((tm + HALO_B, D_B), F32), pltpu.VMEM((tm, D), BF16),
                        pltpu.VMEM((HEADS, CHUNK, CHUNK), BF16)],
        compiler_params=_params(("arbitrary",)),
    )(x, win_g, b_in, ln_a_g, ln_a_b, w_spatial, bst, conv_b_w, conv_b_b, ln_b_g, ln_b_b,
      wout, b_out, ln1_g, ln1_b)


def ffn_forward(xhat1, ln1_g, ln1_b, wup_g, cfw, cfb, wdown, ln2_g, ln2_b, target, tm):
    t = xhat1.shape[0]
    nt = t // tm

    def body(xh_ref, g1_ref, b1_ref, wup_ref, cfw_ref, cfb_ref, wdown_ref, g2_ref, b2_ref, tgt_ref,
             hu_ref, dr2_ref, loss_ref, sln2_ref,
             x1_ref, x1b_ref, extg_ref, extv_ref, carry_ref, gbuf_ref, ffn_ref, acc_loss, acc_g2, acc_b2):
        i = pl.program_id(0)

        @pl.when(i == 0)
        def _():
            carry_ref[...] = jnp.zeros(carry_ref.shape, F32)
            acc_loss[...] = jnp.zeros(acc_loss.shape, F32)
            acc_g2[...] = jnp.zeros(acc_g2.shape, F32)
            acc_b2[...] = jnp.zeros(acc_b2.shape, F32)

        x1 = xh_ref[...] * g1_ref[...] + b1_ref[...]
        x1_ref[...] = x1
        x1b_ref[...] = x1.astype(BF16)

        for f in range(N_F):
            hg = _nn(x1b_ref[...], wup_ref[f])
            hv = _nn(x1b_ref[...], wup_ref[N_F + f])
            hu_ref[f] = hg
            hu_ref[N_F + f] = hv
            extg_ref[0:HALO_F, :] = carry_ref[f]
            extv_ref[0:HALO_F, :] = carry_ref[N_F + f]
            extg_ref[HALO_F:HALO_F + tm, :] = hg
            extv_ref[HALO_F:HALO_F + tm, :] = hv
            carry_ref[f] = extg_ref[tm:tm + HALO_F, :]
            carry_ref[N_F + f] = extv_ref[tm:tm + HALO_F, :]
            wg, wv = cfw_ref[f], cfw_ref[N_F + f]
            bg, bv = cfb_ref[f:f + 1, :], cfb_ref[N_F + f:N_F + f + 1, :]

            def rows(bi, wg=wg, wv=wv, bg=bg, bv=bv):
                base = bi * ROWS
                gate = _conv_f_block(extg_ref, base, wg)[0] + bg
                val = _conv_f_block(extv_ref, base, wv)[0] + bv
                gbuf_ref[pl.ds(base, ROWS), :] = (gate * _sigmoid(gate) * val).astype(BF16)

            _loop(tm // ROWS, rows)
            part = _nn(gbuf_ref[...], wdown_ref[f])
            if f == 0:
                ffn_ref[...] = part
            else:
                ffn_ref[...] += part

        def tail(bi):
            r = _rows(bi)
            xhat2, rstd2 = _ln_stats(ALPHA * x1_ref[r, :] + ffn_ref[r, :])
            err = xhat2 * g2_ref[...] + b2_ref[...] - tgt_ref[r, :]
            e2 = _rsum8(err * err)
            acc_loss[...] += sum(e2[:, k * 128:(k + 1) * 128] for k in range(D // 128))
            dy = err * (1.0 / D)
            acc_g2[...] += _rsum8(dy * xhat2)
            acc_b2[...] += _rsum8(dy)
            dr2_ref[r, :] = _ln_bwd(dy * g2_ref[...], xhat2, rstd2)

        _loop(tm // ROWS, tail)
        loss_ref[...] = acc_loss[...]

        @pl.when(i == nt - 1)
        def _():
            dg = jnp.sum(acc_g2[...], axis=0, keepdims=True)
            db = jnp.sum(acc_b2[...], axis=0, keepdims=True)
            for k in range(D // 128):
                sln2_ref[k:k + 1, :] = dg[:, k * 128:(k + 1) * 128]
                sln2_ref[8 + k:9 + k, :] = db[:, k * 128:(k + 1) * 128]

    row = pl.BlockSpec((tm, D), lambda i: (i, 0))
    return pl.pallas_call(
        body, name="ffn_forward", grid=(nt,),
        in_specs=[row, _full(ln1_g.shape), _full(ln1_b.shape), _resident(wup_g.shape),
                  _full(cfw.shape), _full(cfb.shape), _resident(wdown.shape),
                  _full(ln2_g.shape), _full(ln2_b.shape), row],
        out_specs=[pl.BlockSpec((N_DEV, tm, W_UP_BLK), lambda i: (0, i, 0)), row,
                   _full((8, 128)), _full((16, 128))],
        out_shape=[jax.ShapeDtypeStruct((N_DEV, t, W_UP_BLK), F32), jax.ShapeDtypeStruct((t, D), F32),
                   jax.ShapeDtypeStruct((8, 128), F32), jax.ShapeDtypeStruct((16, 128), F32)],
        scratch_shapes=[pltpu.VMEM((tm, D), F32), pltpu.VMEM((tm, D), BF16),
                        pltpu.VMEM((tm + HALO_F, W_UP_BLK), F32), pltpu.VMEM((tm + HALO_F, W_UP_BLK), F32),
                        pltpu.VMEM((N_DEV, HALO_F, W_UP_BLK), F32), pltpu.VMEM((tm, W_UP_BLK), BF16),
                        pltpu.VMEM((tm, D), F32), pltpu.VMEM((8, 128), F32),
                        pltpu.VMEM((8, D), F32), pltpu.VMEM((8, D), F32)],
        compiler_params=_params(("arbitrary",)),
    )(xhat1, ln1_g, ln1_b, wup_g, cfw, cfb, wdown, ln2_g, ln2_b, target)


def ffn_backward(dr2, xhat1, ln1_g, ln1_b, hu, wup_g, cfw, cfb, wdown, tm):
    t = dr2.shape[0]
    nt = t // tm
    hu4 = hu.reshape(2, N_F, t, W_UP_BLK)
    wup4 = wup_g.reshape(2, N_F, D, W_UP_BLK)
    cfw4 = cfw.reshape(2, N_F, KF, W_UP_BLK)
    cfb4 = cfb.reshape(2, N_F, 1, W_UP_BLK)
    halo_blocks = tm // HALO_F

    def body(dr2_ref, xh_ref, g1_ref, b1_ref, hu_ref, halo_ref, wup_ref, cfw_ref, cfb_ref, wdown_ref,
             dwup_ref, dwdown_ref, dcfw_ref, dcfb_ref, dx1_ref,
             x1b_ref, drb_ref, dg_ref, extg_ref, extv_ref, dextg_ref, dextv_ref, gbuf_ref,
             dhug_ref, dhuv_ref, acc_wup, acc_wdown, acc_cfw, acc_cfb, sem):
        f = pl.program_id(0)
        i = pl.program_id(1)
        first_tile = i == nt - 1

        @pl.when(i == 0)
        def _():
            acc_wup[...] = jnp.zeros(acc_wup.shape, F32)
            acc_wdown[...] = jnp.zeros(acc_wdown.shape, F32)
            acc_cfw[...] = jnp.zeros(acc_cfw.shape, F32)
            acc_cfb[...] = jnp.zeros(acc_cfb.shape, F32)
            dextg_ref[tm:tm + HALO_F, :] = jnp.zeros((HALO_F, W_UP_BLK), F32)
            dextv_ref[tm:tm + HALO_F, :] = jnp.zeros((HALO_F, W_UP_BLK), F32)

        x1b_ref[...] = (xh_ref[...] * g1_ref[...] + b1_ref[...]).astype(BF16)
        drb_ref[...] = dr2_ref[...].astype(BF16)
        dg_ref[...] = _nt(drb_ref[...], wdown_ref[0])

        keep = jnp.where(first_tile, 0.0, 1.0)
        extg_ref[0:HALO_F, :] = halo_ref[0, 0] * keep
        extv_ref[0:HALO_F, :] = halo_ref[1, 0] * keep
        extg_ref[HALO_F:HALO_F + tm, :] = hu_ref[0, 0]
        extv_ref[HALO_F:HALO_F + tm, :] = hu_ref[1, 0]
        wg, wv = cfw_ref[0, 0], cfw_ref[1, 0]
        bg, bv = cfb_ref[0, 0], cfb_ref[1, 0]

        def rows1(bi):
            base = bi * ROWS
            r = pl.ds(base, ROWS)
            gate, wing = _conv_f_block(extg_ref, base, wg)
            val, winv = _conv_f_block(extv_ref, base, wv)
            gate = gate + bg
            val = val + bv
            sg = _sigmoid(gate)
            silu = gate * sg
            gbuf_ref[r, :] = (silu * val).astype(BF16)
            dg = dg_ref[r, :]
            dgate = dg * val * (sg * (1.0 + gate * (1.0 - sg)))
            dval = dg * silu
            dextg_ref[r, :] = dgate
            dextv_ref[r, :] = dval
            acc_cfb[0:8, :] += _rsum8(dgate)
            acc_cfb[8:16, :] += _rsum8(dval)
            for k in range(KF):
                acc_cfw[8 * k:8 * k + 8, :] += _rsum8(dgate * wing[6 + k:6 + k + ROWS, :])
                acc_cfw[8 * (KF + k):8 * (KF + k) + 8, :] += _rsum8(dval * winv[6 + k:6 + k + ROWS, :])

        _loop(tm // ROWS, rows1)

        def rows2(bi):
            base = bi * ROWS
            r = pl.ds(base, ROWS)
            wing = dextg_ref[pl.ds(base, ROWS + HALO_F), :]
            winv = dextv_ref[pl.ds(base, ROWS + HALO_F), :]
            dhug = sum(wing[2 - k:2 - k + ROWS, :] * wg[k:k + 1, :] for k in range(KF))
            dhuv = sum(winv[2 - k:2 - k + ROWS, :] * wv[k:k + 1, :] for k in range(KF))
            dhug_ref[r, :] = dhug.astype(BF16)
            dhuv_ref[r, :] = dhuv.astype(BF16)

        _loop(tm // ROWS, rows2)
        dextg_ref[tm:tm + HALO_F, :] = dextg_ref[0:HALO_F, :]
        dextv_ref[tm:tm + HALO_F, :] = dextv_ref[0:HALO_F, :]

        acc_wdown[...] += _tn(gbuf_ref[...], drb_ref[...])
        x1t = x1b_ref[...].T
        acc_wup[0] += _nn(x1t, dhug_ref[...])
        acc_wup[1] += _nn(x1t, dhuv_ref[...])
        dx1_ref[0] = (_nt(dhug_ref[...], wup_ref[0, 0]) + _nt(dhuv_ref[...], wup_ref[1, 0])).astype(BF16)

        @pl.when(i == nt - 1)
        def _():
            for g in range(2):
                dcfb_ref[g, 0] = jnp.sum(acc_cfb[8 * g:8 * g + 8, :], axis=0, keepdims=True)
                for k in range(KF):
                    r0 = 8 * (g * KF + k)
                    dcfw_ref[g, 0, k:k + 1, :] = jnp.sum(acc_cfw[r0:r0 + 8, :], axis=0, keepdims=True)
            cps = [pltpu.make_async_copy(acc_wup.at[0], dwup_ref.at[0, f], sem.at[0]),
                   pltpu.make_async_copy(acc_wup.at[1], dwup_ref.at[1, f], sem.at[1]),
                   pltpu.make_async_copy(acc_wdown, dwdown_ref.at[f], sem.at[2])]
            for cp in cps:
                cp.start()
            for cp in cps:
                cp.wait()

    rev = lambda i: nt - 1 - i
    row = pl.BlockSpec((tm, D), lambda f, i: (rev(i), 0))
    pair = lambda r, c: pl.BlockSpec((2, 1, r, c), lambda f, i: (0, f, 0, 0))
    return pl.pallas_call(
        body, name="ffn_backward", grid=(N_F, nt),
        in_specs=[row, row, _full(ln1_g.shape), _full(ln1_b.shape),
                  pl.BlockSpec((2, 1, tm, W_UP_BLK), lambda f, i: (0, f, rev(i), 0)),
                  pl.BlockSpec((2, 1, HALO_F, W_UP_BLK),
                               lambda f, i: (0, f, jnp.maximum(rev(i) * halo_blocks - 1, 0), 0)),
                  pair(D, W_UP_BLK), pair(KF, W_UP_BLK), pair(1, W_UP_BLK),
                  pl.BlockSpec((1, W_UP_BLK, D), lambda f, i: (f, 0, 0))],
        out_specs=[ANY, ANY, pair(KF, W_UP_BLK), pair(1, W_UP_BLK),
                   pl.BlockSpec((1, tm, D), lambda f, i: (f, rev(i), 0))],
        out_shape=[jax.ShapeDtypeStruct((2, N_F, D, W_UP_BLK), F32),
                   jax.ShapeDtypeStruct((N_F, W_UP_BLK, D), F32),
                   jax.ShapeDtypeStruct((2, N_F, KF, W_UP_BLK), F32),
                   jax.ShapeDtypeStruct((2, N_F, 1, W_UP_BLK), F32),
                   jax.ShapeDtypeStruct((N_F, t, D), BF16)],
        scratch_shapes=[pltpu.VMEM((tm, D), BF16), pltpu.VMEM((tm, D), BF16),
                        pltpu.VMEM((tm, W_UP_BLK), F32),
                        pltpu.VMEM((tm + HALO_F, W_UP_BLK), F32), pltpu.VMEM((tm + HALO_F, W_UP_BLK), F32),
                        pltpu.VMEM((tm + HALO_F, W_UP_BLK), F32), pltpu.VMEM((tm + HALO_F, W_UP_BLK), F32),
                        pltpu.VMEM((tm, W_UP_BLK), BF16), pltpu.VMEM((tm, W_UP_BLK), BF16),
                        pltpu.VMEM((tm, W_UP_BLK), BF16),
                        pltpu.VMEM((2, D, W_UP_BLK), F32), pltpu.VMEM((W_UP_BLK, D), F32),
                        pltpu.VMEM((2 * KF * 8, W_UP_BLK), F32), pltpu.VMEM((16, W_UP_BLK), F32),
                        pltpu.SemaphoreType.DMA((3,))],
        compiler_params=_params(("arbitrary", "arbitrary")),
    )(dr2, xhat1, ln1_g, ln1_b, hu4, hu4, wup4, cfw4, cfb4, wdown)


def mix_backward(x, h, dx1p, dr2, xhat1, rstd1, win_g, ln_a_g, ln_a_b, w_spatial, bst,
                 conv_b_w, conv_b_b, ln_b_g, ln_b_b, wout, ln1_g, tm):
    t = x.shape[0]
    nt = t // tm
    n_chunks = tm // CHUNK
    halo_blocks = tm // HALO_B

    def body(x_ref, h_ref, halo_ref, dx1p_ref, dr2_ref, xh1_ref, rstd1_ref, win_ref, ga_ref, ba_ref,
             ws_ref, bst_ref, cw_ref, cb_ref, gb_ref, bb_ref, wout_ref, g1_ref,
             gx_ref, dwin_ref, dwout_ref, dcw_ref, small_ref,
             ext_ref, dext_ref, y_ref, dy_ref, dh_ref, dmb_ref, wsm_ref,
             acc_win, acc_wout, acc_bin, acc_lnag, acc_lnab, acc_ws, acc_bs, acc_cbb, acc_lnbg,
             acc_lnbb, acc_bout, acc_ln1g, acc_ln1b, acc_cw, sem):
        i = pl.program_id(0)
        first_tile = i == nt - 1
        accs = [acc_win, acc_wout, acc_bin, acc_lnag, acc_lnab, acc_ws, acc_bs, acc_cbb, acc_lnbg,
                acc_lnbb, acc_bout, acc_ln1g, acc_ln1b, acc_cw]

        @pl.when(i == 0)
        def _():
            for acc in accs:
                acc[...] = jnp.zeros(acc.shape, F32)
            dext_ref[tm:tm + HALO_B, :] = jnp.zeros((HALO_B, D_B), F32)
            mask = _tril_mask()
            for hd in range(HEADS):
                wsm_ref[hd] = jnp.where(mask, ws_ref[hd], 0.0).astype(BF16)

        def ln1_rows(bi):
            r = _rows(bi)
            part = [dx1p_ref[f, r, :].astype(F32) for f in range(N_F)]
            dx1 = ALPHA * dr2_ref[r, :] + ((part[0] + part[1]) + (part[2] + part[3]))
            xhat = xh1_ref[r, :]
            acc_ln1g[...] += _rsum8(dx1 * xhat)
            acc_ln1b[...] += _rsum8(dx1)
            dr1 = _ln_bwd(dx1 * g1_ref[...], xhat, rstd1_ref[r, 0:1])
            acc_bout[...] += _rsum8(dr1)
            gx_ref[r, :] = ALPHA * dr1
            dmb_ref[r, :] = dr1.astype(BF16)

        _loop(tm // ROWS, ln1_rows)
        dy_ref[...] = _nt(dmb_ref[...], wout_ref[...])

        ha = halo_ref[:, 0:D_B]
        hg = halo_ref[:, D_B:2 * D_B]
        ext_ref[0:HALO_B, :] = jnp.where(first_tile, 0.0, 1.0) * (ha * _sigmoid(hg))

        def chunk(ci):
            r = _rows(ci, CHUNK)
            hu, hv = h_ref[r, 0:D_A], h_ref[r, D_A:2 * D_A]
            u, cdf_u, cdf_v, xhats, rstds, vns, svs = _mixer_a_fwd(hu, hv, ga_ref, ba_ref, wsm_ref, bst_ref)
            for hd in range(HEADS):
                sl = slice(hd * HEAD_DIM, (hd + 1) * HEAD_DIM)
                rows8 = slice(8 * hd, 8 * hd + 8)
                dy_a = dy_ref[r, sl]
                y_ref[r, sl] = (u[:, sl] * svs[hd]).astype(BF16)
                du = dy_a * svs[hd]
                dsv = dy_a * u[:, sl]
                dsvb = dsv.astype(BF16)
                acc_bs[hd] += dsv
                acc_ws[hd] += _nt(dsvb, vns[hd])
                dvn = _tn(wsm_ref[hd], dsvb)
                acc_lnag[rows8, :] += _rsum8(dvn * xhats[hd])
                acc_lnab[rows8, :] += _rsum8(dvn)
                dv = _ln_bwd(dvn * ga_ref[hd:hd + 1, :], xhats[hd], rstds[hd])
                hus, hvs = hu[:, sl], hv[:, sl]
                slv = slice(D_A + hd * HEAD_DIM, D_A + (hd + 1) * HEAD_DIM)
                dhu = du * (cdf_u[:, sl] + hus * jnp.exp(-0.5 * hus * hus) * INV_SQRT_2PI)
                dhv = dv * (cdf_v[:, sl] + hvs * jnp.exp(-0.5 * hvs * hvs) * INV_SQRT_2PI)
                acc_bin[:, sl] += _rsum8(dhu)
                acc_bin[:, slv] += _rsum8(dhv)
                dh_ref[r, sl] = dhu.astype(BF16)
                dh_ref[r, slv] = dhv.astype(BF16)
            a_b = h_ref[r, 2 * D_A:2 * D_A + D_B]
            g_b = h_ref[r, 2 * D_A + D_B:D_IN]
            ext_ref[pl.ds(HALO_B + ci * CHUNK, CHUNK), :] = a_b * _sigmoid(g_b)

        _loop(n_chunks, chunk)

        def conv_rows(bi):
            base = bi * ROWS
            r = pl.ds(base, ROWS)
            acc, win = _conv_b_block(ext_ref, base, cw_ref)
            xhat, rstd = _ln_stats(acc + cb_ref[...])
            yb2 = xhat * gb_ref[...] + bb_ref[...]
            sg = _sigmoid(yb2)
            y_ref[r, D_A:D] = (yb2 * sg).astype(BF16)
            dyb2 = dy_ref[r, D_A:D] * (sg * (1.0 + yb2 * (1.0 - sg)))
            acc_lnbg[...] += _rsum8(dyb2 * xhat)
            acc_lnbb[...] += _rsum8(dyb2)
            dyb1 = _ln_bwd(dyb2 * gb_ref[...], xhat, rstd)
            acc_cbb[...] += _rsum8(dyb1)
            dext_ref[r, :] = dyb1
            for k in range(KB):
                acc_cw[8 * k:8 * k + 8, :] += _rsum8(dyb1 * _tap(win, 2 + k))

        _loop(tm // ROWS, conv_rows)

        def convt_rows(bi):
            base = bi * ROWS
            r = pl.ds(base, ROWS)
            dwin = _shifted(dext_ref[pl.ds(base, ROWS + HALO_B), :])
            dyb0 = jnp.zeros((ROWS, D_B), F32)
            for k in range(KB):
                dyb0 = dyb0 + _tap(dwin, 30 - k) * cw_ref[k:k + 1, :]
            a_b = h_ref[r, 2 * D_A:2 * D_A + D_B]
            sg = _sigmoid(h_ref[r, 2 * D_A + D_B:D_IN])
            da_b = dyb0 * sg
            dg_b = dyb0 * a_b * sg * (1.0 - sg)
            acc_bin[:, 2 * D_A:2 * D_A + D_B] += _rsum8(da_b)
            acc_bin[:, 2 * D_A + D_B:D_IN] += _rsum8(dg_b)
            dh_ref[r, 2 * D_A:2 * D_A + D_B] = da_b.astype(BF16)
            dh_ref[r, 2 * D_A + D_B:D_IN] = dg_b.astype(BF16)

        _loop(tm // ROWS, convt_rows)
        dext_ref[tm:tm + HALO_B, :] = dext_ref[0:HALO_B, :]

        acc_wout[...] += _tn(y_ref[...], dmb_ref[...])
        xt = x_ref[...].T.astype(BF16)
        dh_blocks = [dh_ref[:, j * W_IN_BLK:(j + 1) * W_IN_BLK] for j in range(N_DEV)]
        for j in range(N_DEV):
            acc_win[j] += _nn(xt, dh_blocks[j])
        gx_ref[...] += sum(_nt(dh_blocks[j], win_ref[j]) for j in range(N_DEV))

        @pl.when(i == nt - 1)
        def _():
            cps = [pltpu.make_async_copy(acc_win, dwin_ref, sem.at[0]),
                   pltpu.make_async_copy(acc_wout, dwout_ref, sem.at[1])]
            for cp in cps:
                cp.start()
            small_ref[...] = jnp.zeros(small_ref.shape, F32)

            def put_row_vector(row0, acc):
                vec = jnp.sum(acc[...], axis=0, keepdims=True)
                for k in range(vec.shape[1] // 128):
                    small_ref[row0 + k:row0 + k + 1, :] = vec[:, k * 128:(k + 1) * 128]

            put_row_vector(S_BIN, acc_bin)
            put_row_vector(S_CBB, acc_cbb)
            put_row_vector(S_LNBG, acc_lnbg)
            put_row_vector(S_LNBB, acc_lnbb)
            put_row_vector(S_BOUT, acc_bout)
            put_row_vector(S_LN1G, acc_ln1g)
            put_row_vector(S_LN1B, acc_ln1b)
            mask = _tril_mask()
            for hd in range(HEADS):
                rows8 = slice(8 * hd, 8 * hd + 8)
                small_ref[S_LNAG + hd:S_LNAG + hd + 1, :] = jnp.sum(acc_lnag[rows8, :], axis=0, keepdims=True)
                small_ref[S_LNAB + hd:S_LNAB + hd + 1, :] = jnp.sum(acc_lnab[rows8, :], axis=0, keepdims=True)
                small_ref[S_WS + hd * CHUNK:S_WS + (hd + 1) * CHUNK, :] = jnp.where(mask, acc_ws[hd], 0.0)
                small_ref[S_BS + hd:S_BS + hd + 1, :] = jnp.sum(acc_bs[hd].T, axis=0, keepdims=True)
            for k in range(KB):
                dcw_ref[k:k + 1, :] = jnp.sum(acc_cw[8 * k:8 * k + 8, :], axis=0, keepdims=True)
            for cp in cps:
                cp.wait()

    rev = lambda i: nt - 1 - i
    row = lambda w: pl.BlockSpec((tm, w), lambda i: (rev(i), 0))
    return pl.pallas_call(
        body, name="mix_backward", grid=(nt,),
        in_specs=[row(D), row(D_IN),
                  pl.BlockSpec((HALO_B, 2 * D_B), lambda i: (jnp.maximum(rev(i) * halo_blocks - 1, 0), 1)),
                  pl.BlockSpec((N_F, tm, D), lambda i: (0, rev(i), 0)),
                  row(D), row(D), row(128), _resident(win_g.shape), _full(ln_a_g.shape),
                  _full(ln_a_b.shape), _full(w_spatial.shape), _full(bst.shape), _full(conv_b_w.shape),
                  _full(conv_b_b.shape), _full(ln_b_g.shape), _full(ln_b_b.shape),
                  _resident(wout.shape), _full(ln1_g.shape)],
        out_specs=[row(D), ANY, ANY, _full((KB, D_B)), _full((S_MIX_ROWS, 128))],
        out_shape=[jax.ShapeDtypeStruct((t, D), F32), jax.ShapeDtypeStruct((N_DEV, D, W_IN_BLK), F32),
                   jax.ShapeDtypeStruct((D, D), F32), jax.ShapeDtypeStruct((KB, D_B), F32),
                   jax.ShapeDtypeStruct((S_MIX_ROWS, 128), F32)],
        scratch_shapes=[pltpu.VMEM((tm + HALO_B, D_B), F32), pltpu.VMEM((tm + HALO_B, D_B), F32),
                        pltpu.VMEM((tm, D), BF16), pltpu.VMEM((tm, D), F32), pltpu.VMEM((tm, D_IN), BF16),
                        pltpu.VMEM((tm, D), BF16),
                        pltpu.VMEM((HEADS, CHUNK, CHUNK), BF16),
                        pltpu.VMEM((N_DEV, D, W_IN_BLK), F32), pltpu.VMEM((D, D), F32),
                        pltpu.VMEM((8, D_IN), F32), pltpu.VMEM((8 * HEADS, HEAD_DIM), F32),
                        pltpu.VMEM((8 * HEADS, HEAD_DIM), F32), pltpu.VMEM((HEADS, CHUNK, CHUNK), F32),
                        pltpu.VMEM((HEADS, CHUNK, CHUNK), F32), pltpu.VMEM((8, D_B), F32),
                        pltpu.VMEM((8, D_B), F32), pltpu.VMEM((8, D_B), F32), pltpu.VMEM((8, D), F32),
                        pltpu.VMEM((8, D), F32), pltpu.VMEM((8, D), F32), pltpu.VMEM((8 * KB, D_B), F32),
                        pltpu.SemaphoreType.DMA((2,))],
        compiler_params=_params(("arbitrary",)),
    )(x, h, h, dx1p, dr2, xhat1, rstd1, win_g, ln_a_g, ln_a_b, w_spatial, bst, conv_b_w, conv_b_b,
      ln_b_g, ln_b_b, wout, ln1_g)


def _rows128(a):
    return a.reshape(-1, 128)


def _pack_conv(cb, cf):
    out = jnp.zeros((40, 768), F32)
    out = out.at[0:KB, 0:64].set(cb)
    return out.at[32:32 + KF, 0:W_UP_BLK].set(cf)


def kernel(x, w_in, b_in, ln_a_g, ln_a_b, w_spatial, b_spatial, conv_b_w, conv_b_b, ln_b_g, ln_b_b, w_out, b_out, ln1_g, ln1_b, w_up, conv_f_w, conv_f_b, w_down, ln2_g, ln2_b, loss_target, m_w_in, m_b_in, m_ln_a_g, m_ln_a_b, m_w_spatial, m_b_spatial, m_conv_b_w, m_conv_b_b, m_ln_b_g, m_ln_b_b, m_w_out, m_b_out, m_ln1_g, m_ln1_b, m_w_up, m_conv_f_w, m_conv_f_b, m_w_down, m_ln2_g, m_ln2_b, v_w_in, v_b_in, v_ln_a_g, v_ln_a_b, v_w_spatial, v_b_spatial, v_conv_b_w, v_conv_b_b, v_ln_b_g, v_ln_b_b, v_w_out, v_b_out, v_ln1_g, v_ln1_b, v_w_up, v_conv_f_w, v_conv_f_b, v_w_down, v_ln2_g, v_ln2_b):
    t = x.shape[1]
    x2 = x.reshape(t, D)
    target = loss_target.reshape(t, D)
    tm_fwd = min(t, 512)
    tm_bwd = min(t, 256)

    xi, yi, ci = _mesh_pos()
    jidx = jnp.stack([_lid(px, py, ci) for px, py in _chip_patterns(xi, yi)]).astype(jnp.int32)

    win_g, wout_g, wup_g, wdown_g, conv_g = all_gather_weights(
        w_in, w_out, w_up, w_down, _pack_conv(conv_b_w, conv_f_w))
    wout_full = wout_g.reshape(D, D)
    wdown4 = wdown_g.reshape(N_F, W_UP_BLK, D)
    conv_b_full = conv_g[:, 0:KB, 0:64].transpose(1, 0, 2).reshape(KB, D_B)
    cfw = conv_g[:, 32:32 + KF, 0:W_UP_BLK]
    cfb = conv_f_b.reshape(N_DEV, W_UP_BLK)
    row = lambda a: a.reshape(1, -1)
    bst = b_spatial.T

    h, xhat1, rstd1 = mix_forward(
        x2, win_g, row(b_in), ln_a_g, ln_a_b, w_spatial, bst, conv_b_full, row(conv_b_b),
        row(ln_b_g), row(ln_b_b), wout_full, row(b_out), row(ln1_g), row(ln1_b), tm_fwd)
    hu, dr2, loss_part, s_ln2 = ffn_forward(
        xhat1, row(ln1_g), row(ln1_b), wup_g, cfw, cfb, wdown4, row(ln2_g), row(ln2_b), target, tm_bwd)
    loss = lax.psum(jnp.sum(loss_part), ("x", "y", "c")) * (0.5 / D)

    dwup, dwdown, dcfw, dcfb, dx1p = ffn_backward(
        dr2, xhat1, row(ln1_g), row(ln1_b), hu, wup_g, cfw, cfb, wdown4, tm_bwd)
    grad_x, dwin, dwout, dcw, s_mix = mix_backward(
        x2, h, dx1p, dr2, xhat1, rstd1, win_g, ln_a_g, ln_a_b, w_spatial, bst,
        conv_b_full, row(conv_b_b), row(ln_b_g), row(ln_b_b), wout_full, row(ln1_g), tm_bwd)

    dcfb_rows = jnp.pad(dcfb.reshape(-1, 128), ((0, 4), (0, 0)))
    svec = jnp.concatenate([s_mix, dcfb_rows, s_ln2], axis=0)
    dconv = jnp.zeros((N_DEV, 40, 768), F32)
    dconv = dconv.at[:, 0:KB, 0:64].set(dcw.reshape(KB, N_DEV, 64).transpose(1, 0, 2))
    dconv = dconv.at[:, 32:32 + KF, 0:W_UP_BLK].set(dcfw.reshape(N_DEV, KF, W_UP_BLK))
    grads = [dwin, dwout.reshape(N_DEV, D // N_DEV, D), dwup.reshape(N_DEV, D, W_UP_BLK),
             dwdown.reshape(N_DEV, D_FF // N_DEV, D), dconv]
    names = ["w_in", "w_out", "w_up", "w_down", "conv"]
    row_blocks = [512, 128, 256, 352, 40]

    *lands, sv_land = exchange_with_sibling(grads, svec)
    partials = [chip_partials("chip_partials_" + nm, g, l, jidx, rb)
                for nm, g, l, rb in zip(names, grads, lands, row_blocks)]
    *recvs, sv_slots = exchange_between_chips(partials, svec, sv_land)

    shard_w = [w_in, w_out, w_up, w_down, _pack_conv(conv_b_w, conv_f_w)]
    shard_m = [m_w_in, m_w_out, m_w_up, m_w_down, _pack_conv(m_conv_b_w, m_conv_f_w)]
    shard_v = [v_w_in, v_w_out, v_w_up, v_w_down, _pack_conv(v_conv_b_w, v_conv_f_w)]
    big = {}
    for nm, g, l, r, w, m, v, rb in zip(names, grads, lands, recvs, shard_w, shard_m, shard_v, row_blocks):
        big[nm] = reduce_and_adamw("reduce_adamw_" + nm, g, l, r, w, m, v, jidx, rb)
    for k in range(4):
        packed = big["conv"][k]
        big.setdefault("conv_b_w", []).append(packed[0:KB, 0:64])
        big.setdefault("conv_f_w", []).append(packed[32:32 + KF, 0:W_UP_BLK])

    small_w = dict(b_in=b_in, ln_a_g=ln_a_g, ln_a_b=ln_a_b, w_spatial=w_spatial, b_spatial=b_spatial,
                   conv_b_b=conv_b_b, ln_b_g=ln_b_g, ln_b_b=ln_b_b, b_out=b_out, ln1_g=ln1_g,
                   ln1_b=ln1_b, conv_f_b=conv_f_b, ln2_g=ln2_g, ln2_b=ln2_b)
    small_m = dict(b_in=m_b_in, ln_a_g=m_ln_a_g, ln_a_b=m_ln_a_b, w_spatial=m_w_spatial,
                   b_spatial=m_b_spatial, conv_b_b=m_conv_b_b, ln_b_g=m_ln_b_g, ln_b_b=m_ln_b_b,
                   b_out=m_b_out, ln1_g=m_ln1_g, ln1_b=m_ln1_b, conv_f_b=m_conv_f_b, ln2_g=m_ln2_g,
                   ln2_b=m_ln2_b)
    small_v = dict(b_in=v_b_in, ln_a_g=v_ln_a_g, ln_a_b=v_ln_a_b, w_spatial=v_w_spatial,
                   b_spatial=v_b_spatial, conv_b_b=v_conv_b_b, ln_b_g=v_ln_b_g, ln_b_b=v_ln_b_b,
                   b_out=v_b_out, ln1_g=v_ln1_g, ln1_b=v_ln1_b, conv_f_b=v_conv_f_b, ln2_g=v_ln2_g,
                   ln2_b=v_ln2_b)
    order = [nm for nm, _, _ in SMALL_LAYOUT]
    small_out = small_adamw(sv_slots, [_rows128(small_w[nm]) for nm in order],
                            [_rows128(small_m[nm]) for nm in order], [_rows128(small_v[nm]) for nm in order])
    n_small = len(order)
    small = {nm: [small_out[k * n_small + p].reshape(small_w[nm].shape) for k in range(4)]
             for p, nm in enumerate(order)}

    weights = ["w_in", "b_in", "ln_a_g", "ln_a_b", "w_spatial", "b_spatial", "conv_b_w", "conv_b_b",
               "ln_b_g", "ln_b_b", "w_out", "b_out", "ln1_g", "ln1_b", "w_up", "conv_f_w", "conv_f_b",
               "w_down", "ln2_g", "ln2_b"]
    result = lambda nm, k: big[nm][k] if nm in big else small[nm][k]
    return (loss, grad_x.reshape(x.shape), *[result(nm, 0) for nm in weights],
            *[result(nm, 1) for nm in weights], *[result(nm, 2) for nm in weights],
            *[result(nm, 3) for nm in weights])
```

```python
import functools
import math

import jax
import jax.numpy as jnp
from jax import lax
from jax.experimental import pallas as pl
from jax.experimental.pallas import tpu as pltpu

F32 = jnp.float32
BF16 = jnp.bfloat16

D = 1024
D_A = 512
D_B = 512
HEADS = 4
HEAD_DIM = 128
CHUNK = 128
KB = 31
KF = 3
D_FF = 2816
D_IN = 2048
N_DEV = 8
W_IN_BLK = D_IN // N_DEV
W_UP_BLK = 2 * D_FF // N_DEV
N_F = 4
LN_EPS = 1e-5
ALPHA = 2.0 ** 0.25

ADAM_LR = 0.001
ADAM_B1 = 0.9
ADAM_B2 = 0.999
ADAM_EPS = 1e-08
ADAM_WD = 0.01
ADAM_STEP = 10

INV_SQRT2 = 1.0 / math.sqrt(2.0)
INV_SQRT_2PI = 1.0 / math.sqrt(2.0 * math.pi)

HALO_B = 32
HALO_F = 8
ROWS = 64
VMEM_LIMIT = 58 * 1024 * 1024

MESH = pl.DeviceIdType.MESH
ANY = pl.BlockSpec(memory_space=pl.ANY)
VMEM = pl.BlockSpec(memory_space=pltpu.VMEM)

S_BIN, S_LNAG, S_LNAB, S_WS, S_BS, S_CBB, S_LNBG, S_LNBB, S_BOUT, S_LN1G, S_LN1B = (
    0, 16, 24, 32, 544, 552, 560, 568, 576, 584, 592)
S_MIX_ROWS = 600
S_CFB = 600
S_LN2G = 648
S_LN2B = 656
S_ROWS = 664


def _tn(a, b):
    return lax.dot_general(a, b, (((0,), (0,)), ((), ())), preferred_element_type=F32)


def _nt(a, b):
    return lax.dot_general(a, b, (((1,), (1,)), ((), ())), preferred_element_type=F32)


def _nn(a, b):
    return jnp.dot(a, b, preferred_element_type=F32)


def _sigmoid(x):
    return 1.0 / (1.0 + jnp.exp(-x))


def _ln_stats(x):
    mu = jnp.mean(x, axis=-1, keepdims=True)
    xc = x - mu
    var = jnp.mean(xc * xc, axis=-1, keepdims=True)
    rstd = lax.rsqrt(var + LN_EPS)
    return xc * rstd, rstd


def _ln_bwd(dxhat, xhat, rstd):
    m1 = jnp.mean(dxhat, axis=-1, keepdims=True)
    m2 = jnp.mean(dxhat * xhat, axis=-1, keepdims=True)
    return rstd * (dxhat - m1 - xhat * m2)


def _rsum8(x):
    r, n = x.shape
    return x.reshape(r // 8, 8, n).sum(axis=0)


def _rows(i, n=ROWS):
    return pl.ds(i * n, n)


def _loop(n, body):
    for i in range(n):
        body(i)


def _tril_mask():
    r = lax.broadcasted_iota(jnp.int32, (CHUNK, CHUNK), 0)
    c = lax.broadcasted_iota(jnp.int32, (CHUNK, CHUNK), 1)
    return c <= r


def _mixer_a_fwd(hu, hv, ga_ref, ba_ref, wsm_ref, bst_ref):
    cdf_u = 0.5 * (1.0 + lax.erf(hu * INV_SQRT2))
    cdf_v = 0.5 * (1.0 + lax.erf(hv * INV_SQRT2))
    u = hu * cdf_u
    v = hv * cdf_v
    xhats, rstds, vns, svs = [], [], [], []
    for hd in range(HEADS):
        sl = slice(hd * HEAD_DIM, (hd + 1) * HEAD_DIM)
        xhat, rstd = _ln_stats(v[:, sl])
        vn = (xhat * ga_ref[hd:hd + 1, :] + ba_ref[hd:hd + 1, :]).astype(BF16)
        sv = _nn(wsm_ref[hd], vn) + bst_ref[:, hd:hd + 1]
        xhats.append(xhat)
        rstds.append(rstd)
        vns.append(vn)
        svs.append(sv)
    return u, cdf_u, cdf_v, xhats, rstds, vns, svs


def _shifted(win):
    n = win.shape[0]
    return [win] + [pltpu.roll(win, n - s, 0) for s in range(1, 8)]


def _tap(shifted, offset):
    s = offset % 8
    return shifted[s][offset - s:offset - s + ROWS, :]


def _conv_b_block(ext_ref, base, cw_ref):
    win = _shifted(ext_ref[pl.ds(base, ROWS + HALO_B), :])
    acc = jnp.zeros((ROWS, D_B), F32)
    for k in range(KB):
        acc = acc + _tap(win, 2 + k) * cw_ref[k:k + 1, :]
    return acc, win


def _conv_f_block(ext_ref, base, w):
    win = ext_ref[pl.ds(base, ROWS + HALO_F), :]
    out = (win[6:6 + ROWS, :] * w[0:1, :] + win[7:7 + ROWS, :] * w[1:2, :]
           + win[8:8 + ROWS, :] * w[2:3, :])
    return out, win


def _params(sem, **kw):
    return pltpu.CompilerParams(dimension_semantics=sem, vmem_limit_bytes=VMEM_LIMIT, **kw)


def _resident(shape):
    zeros = (0,) * len(shape)
    return pl.BlockSpec(shape, lambda *_: zeros, pipeline_mode=pl.Buffered(1))


def _full(shape):
    zeros = (0,) * len(shape)
    return pl.BlockSpec(shape, lambda *_: zeros)


def _mesh_pos():
    return lax.axis_index("x"), lax.axis_index("y"), lax.axis_index("c")


def _chip_patterns(x, y):
    return [(x, y), (1 - x, y), (x, 1 - y), (1 - x, 1 - y)]


def _lid(x, y, c):
    return 4 * x + 2 * y + c


def _gather_copy(outs, send_sems, recv_sems, a, k, block, to, src=None):
    blk = outs[a].at[_lid(*block)]
    return pltpu.make_async_remote_copy(
        src_ref=blk if src is None else src, dst_ref=blk,
        send_sem=send_sems.at[a, k], recv_sem=recv_sems.at[a, k], device_id=to, device_id_type=MESH)


def _gather_start(mine, outs, send_sems, recv_sems, local_sems):
    x, y, c = _mesh_pos()
    me, sib = (x, y, c), (x, y, 1 - c)
    for a in range(len(mine)):
        pltpu.make_async_copy(mine[a], outs[a].at[_lid(*me)], local_sems.at[a]).start()
        _gather_copy(outs, send_sems, recv_sems, a, 0, me, sib, src=mine[a]).start()
        for j, chip in enumerate(_chip_patterns(x, y)[1:]):
            _gather_copy(outs, send_sems, recv_sems, a, 1 + j, me, (*chip, c), src=mine[a]).start()


def _gather_finish(mine, outs, send_sems, recv_sems, local_sems):
    x, y, c = _mesh_pos()
    me, sib = (x, y, c), (x, y, 1 - c)
    chips = _chip_patterns(x, y)[1:]
    n = len(mine)
    copy = functools.partial(_gather_copy, outs, send_sems, recv_sems)
    passed = []
    for j, chip in enumerate(chips):
        for a in range(n):
            copy(a, 1 + j, (*chip, c), me).wait_recv()
            cp = copy(a, 4 + j, (*chip, c), sib)
            cp.start()
            passed.append(cp)
    for a in range(n):
        copy(a, 0, sib, me).wait_recv()
        for j, chip in enumerate(chips):
            copy(a, 4 + j, (*chip, 1 - c), me).wait_recv()
        for k in range(4):
            copy(a, k, me, sib, src=mine[a]).wait_send()
        pltpu.make_async_copy(mine[a], outs[a].at[_lid(*me)], local_sems.at[a]).wait()
    for cp in passed:
        cp.wait_send()


def _gather_scratch(n):
    return [pltpu.SemaphoreType.DMA((n, 7)), pltpu.SemaphoreType.DMA((n, 7)), pltpu.SemaphoreType.DMA((n,))]


def all_gather_mixer_weights(w_in, w_out, w_up, w_down, convp):
    srcs = [w_in, w_out, convp]
    n = len(srcs)

    def body(win_ref, wout_ref, convp_ref, wup_ref, wdown_ref,
             gin_ref, gout_ref, gconv_ref, sup_ref, sdown_ref,
             sin_ref, sout_ref, send_sems, recv_sems, local_sems):
        sin_ref[...] = win_ref[...].astype(BF16)
        sout_ref[...] = wout_ref[...].astype(BF16)
        mine = [sin_ref, sout_ref, convp_ref]
        outs = [gin_ref, gout_ref, gconv_ref]
        _gather_start(mine, outs, send_sems, recv_sems, local_sems)
        sup_ref[...] = wup_ref[...].astype(BF16)
        sdown_ref[...] = wdown_ref[...].astype(BF16)
        _gather_finish(mine, outs, send_sems, recv_sems, local_sems)

    return pl.pallas_call(
        body, name="all_gather_mixer_weights",
        out_shape=[jax.ShapeDtypeStruct((N_DEV,) + w_in.shape, BF16),
                   jax.ShapeDtypeStruct((N_DEV,) + w_out.shape, BF16),
                   jax.ShapeDtypeStruct((N_DEV,) + convp.shape, F32),
                   jax.ShapeDtypeStruct(w_up.shape, BF16), jax.ShapeDtypeStruct(w_down.shape, BF16)],
        in_specs=[VMEM] * 5, out_specs=[ANY] * n + [VMEM, VMEM],
        scratch_shapes=[pltpu.VMEM(w_in.shape, BF16), pltpu.VMEM(w_out.shape, BF16)] + _gather_scratch(n),
        compiler_params=pltpu.CompilerParams(vmem_limit_bytes=VMEM_LIMIT),
    )(w_in, w_out, convp, w_up, w_down)


def exchange_with_sibling(name, grads, svec=None):
    srcs = list(grads) + ([] if svec is None else [svec])
    n, n_all = len(grads), len(srcs)

    def body(*refs):
        src, land = refs[:n_all], refs[n_all:2 * n_all]
        send_sems, recv_sems = refs[2 * n_all:]
        x, y, c = _mesh_pos()
        sib = (x, y, 1 - c)
        copies = []
        for a in range(n):
            for k, (px, py) in enumerate(_chip_patterns(x, y)):
                copies.append(pltpu.make_async_remote_copy(
                    src_ref=src[a].at[_lid(px, py, 1 - c)], dst_ref=land[a].at[k],
                    send_sem=send_sems.at[a, k], recv_sem=recv_sems.at[a, k],
                    device_id=sib, device_id_type=MESH))
        if svec is not None:
            copies.append(pltpu.make_async_remote_copy(
                src_ref=src[n], dst_ref=land[n], send_sem=send_sems.at[n, 0], recv_sem=recv_sems.at[n, 0],
                device_id=sib, device_id_type=MESH))
        for cp in copies:
            cp.start()
        for cp in copies:
            cp.wait()

    return pl.pallas_call(
        body, name=name,
        out_shape=[jax.ShapeDtypeStruct((4,) + g.shape[1:], F32) for g in grads]
        + ([] if svec is None else [jax.ShapeDtypeStruct(svec.shape, F32)]),
        in_specs=[ANY] * n_all, out_specs=[ANY] * n_all,
        scratch_shapes=[pltpu.SemaphoreType.DMA((n_all, 4)), pltpu.SemaphoreType.DMA((n_all, 4))],
    )(*srcs)


def _chip_copies(p, land, send_sems, recv_sems):
    x, y, c = _mesh_pos()
    return [pltpu.make_async_remote_copy(
        src_ref=p[a].at[k], dst_ref=land[a].at[k], send_sem=send_sems.at[a, k], recv_sem=recv_sems.at[a, k],
        device_id=(px, py, c), device_id_type=MESH)
        for k, (px, py) in enumerate(_chip_patterns(x, y)[1:]) for a in range(len(p))]


def exchange_between_chips(partials, svec, sv_land):
    n = len(partials)

    def body(*refs):
        p = refs[:n]
        sv_ref, svl_ref = refs[n], refs[n + 1]
        land = refs[n + 2:2 * n + 2]
        sv_slots = refs[2 * n + 2]
        chip_sv, send_sems, recv_sems, sv_send, sv_recv, local_sem = refs[2 * n + 3:]
        x, y, c = _mesh_pos()
        q = 2 * x + y
        chip_sv[...] = sv_ref[...] + svl_ref[...]
        local = pltpu.make_async_copy(chip_sv, sv_slots.at[q], local_sem)
        local.start()
        copies = _chip_copies(p, land, send_sems, recv_sems)
        for cp in copies:
            cp.start()
        sv_copies = []
        for k, (px, py) in enumerate(_chip_patterns(x, y)[1:]):
            cp = pltpu.make_async_remote_copy(
                src_ref=chip_sv, dst_ref=sv_slots.at[q],
                send_sem=sv_send.at[k], recv_sem=sv_recv.at[k],
                device_id=(px, py, c), device_id_type=MESH)
            cp.start()
            sv_copies.append(cp)
        for cp in copies:
            cp.wait()
        for k, (px, py) in enumerate(_chip_patterns(x, y)[1:]):
            sv_copies[k].wait_send()
            pltpu.make_async_remote_copy(
                src_ref=chip_sv, dst_ref=sv_slots.at[2 * px + py],
                send_sem=sv_send.at[k], recv_sem=sv_recv.at[k],
                device_id=(px, py, c), device_id_type=MESH).wait_recv()
        local.wait()

    return pl.pallas_call(
        body, name="exchange_between_chips",
        out_shape=[jax.ShapeDtypeStruct(p.shape, BF16) for p in partials]
        + [jax.ShapeDtypeStruct((4,) + svec.shape, F32)],
        in_specs=[ANY] * n + [VMEM, VMEM], out_specs=[ANY] * (n + 1),
        scratch_shapes=[pltpu.VMEM(svec.shape, F32),
                        pltpu.SemaphoreType.DMA((n, 3)), pltpu.SemaphoreType.DMA((n, 3)),
                        pltpu.SemaphoreType.DMA((3,)), pltpu.SemaphoreType.DMA((3,)),
                        pltpu.SemaphoreType.DMA],
    )(*partials, svec, sv_land)


def chip_partials(name, g, land, jidx, rb):
    _, r, c = g.shape

    def body(j_ref, g_ref, l_ref, o_ref):
        o_ref[...] = (g_ref[...] + l_ref[...]).astype(BF16)

    return pl.pallas_call(
        body, name=name,
        out_shape=jax.ShapeDtypeStruct((3, r, c), BF16),
        grid_spec=pltpu.PrefetchScalarGridSpec(
            num_scalar_prefetch=1, grid=(3, r // rb),
            in_specs=[pl.BlockSpec((1, rb, c), lambda k, i, j: (j[1 + k], i, 0)),
                      pl.BlockSpec((1, rb, c), lambda k, i, j: (1 + k, i, 0))],
            out_specs=pl.BlockSpec((1, rb, c), lambda k, i, j: (k, i, 0))),
        compiler_params=_params(("arbitrary", "arbitrary")),
    )(jidx, g, land)


def _adamw(w, g, m, v):
    m2 = ADAM_B1 * m + (1.0 - ADAM_B1) * g
    v2 = ADAM_B2 * v + (1.0 - ADAM_B2) * (g * g)
    m_hat = m2 / (1.0 - ADAM_B1 ** ADAM_STEP)
    v_hat = v2 / (1.0 - ADAM_B2 ** ADAM_STEP)
    delta = -ADAM_LR * (m_hat / (jnp.sqrt(v_hat) + ADAM_EPS) + ADAM_WD * w)
    return delta, m2, v2


def reduce_and_adamw(name, g, land, recv, w, m, v, jidx, rb):
    _, r, c = g.shape

    def body(j_ref, g_ref, l_ref, r_ref, w_ref, m_ref, v_ref, go_ref, do_ref, mo_ref, vo_ref):
        grad = (g_ref[0] + l_ref[0]) + r_ref[0].astype(F32) + r_ref[1].astype(F32) + r_ref[2].astype(F32)
        delta, m2, v2 = _adamw(w_ref[...], grad, m_ref[...], v_ref[...])
        go_ref[...] = grad
        do_ref[...] = delta
        mo_ref[...] = m2
        vo_ref[...] = v2

    blk = pl.BlockSpec((rb, c), lambda i, j: (i, 0))
    return pl.pallas_call(
        body, name=name,
        out_shape=[jax.ShapeDtypeStruct((r, c), F32)] * 4,
        grid_spec=pltpu.PrefetchScalarGridSpec(
            num_scalar_prefetch=1, grid=(r // rb,),
            in_specs=[pl.BlockSpec((1, rb, c), lambda i, j: (j[0], i, 0)),
                      pl.BlockSpec((1, rb, c), lambda i, j: (0, i, 0)),
                      pl.BlockSpec((3, rb, c), lambda i, j: (0, i, 0)),
                      blk, blk, blk],
            out_specs=[blk] * 4),
        compiler_params=_params(("arbitrary",)),
    )(jidx, g, land, recv, w, m, v)


SMALL_LAYOUT = [
    ("b_in", S_BIN, 16), ("ln_a_g", S_LNAG, 4), ("ln_a_b", S_LNAB, 4), ("w_spatial", S_WS, 512),
    ("b_spatial", S_BS, 4), ("conv_b_b", S_CBB, 4), ("ln_b_g", S_LNBG, 4), ("ln_b_b", S_LNBB, 4),
    ("b_out", S_BOUT, 8), ("ln1_g", S_LN1G, 8), ("ln1_b", S_LN1B, 8), ("conv_f_b", S_CFB, 44),
    ("ln2_g", S_LN2G, 8), ("ln2_b", S_LN2B, 8),
]


def small_adamw(sv_slots, ws, ms, vs):
    n = len(SMALL_LAYOUT)

    def body(*refs):
        s_ref = refs[0]
        w_refs, m_refs, v_refs = refs[1:1 + n], refs[1 + n:1 + 2 * n], refs[1 + 2 * n:1 + 3 * n]
        outs = refs[1 + 3 * n:]
        for p, (_, row0, rows) in enumerate(SMALL_LAYOUT):
            sl = pl.ds(row0, rows)
            grad = ((s_ref[0, sl, :] + s_ref[1, sl, :]) + s_ref[2, sl, :]) + s_ref[3, sl, :]
            delta, m2, v2 = _adamw(w_refs[p][...], grad, m_refs[p][...], v_refs[p][...])
            outs[p][...] = grad
            outs[n + p][...] = delta
            outs[2 * n + p][...] = m2
            outs[3 * n + p][...] = v2

    shapes = [jax.ShapeDtypeStruct((rows, 128), F32) for _, _, rows in SMALL_LAYOUT]
    return pl.pallas_call(
        body, name="small_adamw", out_shape=shapes * 4,
        in_specs=[VMEM] * (1 + 3 * n), out_specs=[VMEM] * (4 * n),
    )(sv_slots, *ws, *ms, *vs)


def mix_forward(x, win_g, b_in, ln_a_g, ln_a_b, w_spatial, bst, conv_b_w, conv_b_b, ln_b_g, ln_b_b,
                wout, b_out, ln1_g, ln1_b, sup, sdown, tm):
    t = x.shape[0]
    nt = t // tm
    n_chunks = tm // CHUNK

    def body(x_ref, win_ref, bin_ref, ga_ref, ba_ref, ws_ref, bst_ref, cw_ref, cb_ref, gb_ref,
             bb_ref, wout_ref, bout_ref, g1_ref, b1_ref, sup_ref, sdown_ref,
             h_ref, xhat1_ref, rstd1_ref, gup_ref, gdown_ref,
             ext_ref, y_ref, wsm_ref, send_sems, recv_sems, local_sems):
        i = pl.program_id(0)
        gather = ([sup_ref, sdown_ref], [gup_ref, gdown_ref], send_sems, recv_sems, local_sems)

        @pl.when(i == 0)
        def _():
            _gather_start(*gather)
            ext_ref[0:HALO_B, :] = jnp.zeros((HALO_B, D_B), F32)
            mask = _tril_mask()
            for hd in range(HEADS):
                wsm_ref[hd] = jnp.where(mask, ws_ref[hd], 0.0).astype(BF16)

        xb = x_ref[...].astype(BF16)
        for j in range(N_DEV):
            cols = slice(j * W_IN_BLK, (j + 1) * W_IN_BLK)
            h_ref[:, cols] = _nn(xb, win_ref[j]) + bin_ref[:, cols]

        def chunk(ci):
            r = _rows(ci, CHUNK)
            u, _, _, _, _, _, svs = _mixer_a_fwd(h_ref[r, 0:D_A], h_ref[r, D_A:2 * D_A],
                                                 ga_ref, ba_ref, wsm_ref, bst_ref)
            for hd in range(HEADS):
                sl = slice(hd * HEAD_DIM, (hd + 1) * HEAD_DIM)
                y_ref[r, sl] = (u[:, sl] * svs[hd]).astype(BF16)
            a_b = h_ref[r, 2 * D_A:2 * D_A + D_B]
            g_b = h_ref[r, 2 * D_A + D_B:D_IN]
            ext_ref[pl.ds(HALO_B + ci * CHUNK, CHUNK), :] = a_b * _sigmoid(g_b)

        _loop(n_chunks, chunk)

        def conv_rows(bi):
            base = bi * ROWS
            acc, _ = _conv_b_block(ext_ref, base, cw_ref)
            xhat, _ = _ln_stats(acc + cb_ref[...])
            yb2 = xhat * gb_ref[...] + bb_ref[...]
            y_ref[pl.ds(base, ROWS), D_A:D] = (yb2 * _sigmoid(yb2)).astype(BF16)

        _loop(tm // ROWS, conv_rows)
        ext_ref[0:HALO_B, :] = ext_ref[tm:tm + HALO_B, :]

        mix = _nn(y_ref[...], wout_ref[...]) + bout_ref[...]
        xhat1, rstd1 = _ln_stats(ALPHA * x_ref[...] + mix)
        xhat1_ref[...] = xhat1
        rstd1_ref[...] = jnp.broadcast_to(rstd1, (tm, 128))

        @pl.when(i == nt - 1)
        def _():
            _gather_finish(*gather)

    row = lambda w: pl.BlockSpec((tm, w), lambda i: (i, 0))
    return pl.pallas_call(
        body, name="mix_forward", grid=(nt,),
        in_specs=[row(D), _resident(win_g.shape), _full(b_in.shape), _full(ln_a_g.shape),
                  _full(ln_a_b.shape), _full(w_spatial.shape), _full(bst.shape),
                  _full(conv_b_w.shape), _full(conv_b_b.shape), _full(ln_b_g.shape),
                  _full(ln_b_b.shape), _resident(wout.shape), _full(b_out.shape),
                  _full(ln1_g.shape), _full(ln1_b.shape), ANY, ANY],
        out_specs=[row(D_IN), row(D), row(128), ANY, ANY],
        out_shape=[jax.ShapeDtypeStruct((t, D_IN), F32), jax.ShapeDtypeStruct((t, D), F32),
                   jax.ShapeDtypeStruct((t, 128), F32),
                   jax.ShapeDtypeStruct((N_DEV,) + sup.shape, BF16),
                   jax.ShapeDtypeStruct((N_DEV,) + sdown.shape, BF16)],
        scratch_shapes=[pltpu.VMEM((tm + HALO_B, D_B), F32), pltpu.VMEM((tm, D), BF16),
                        pltpu.VMEM((HEADS, CHUNK, CHUNK), BF16)] + _gather_scratch(2),
        compiler_params=_params(("arbitrary",)),
    )(x, win_g, b_in, ln_a_g, ln_a_b, w_spatial, bst, conv_b_w, conv_b_b, ln_b_g, ln_b_b,
      wout, b_out, ln1_g, ln1_b, sup, sdown)


def ffn_forward(xhat1, ln1_g, ln1_b, wup_g, cfw, cfb, wdown, ln2_g, ln2_b, target, tm):
    t = xhat1.shape[0]
    nt = t // tm

    def body(xh_ref, g1_ref, b1_ref, wup_ref, cfw_ref, cfb_ref, wdown_ref, g2_ref, b2_ref, tgt_ref,
             hu_ref, dr2_ref, loss_ref, sln2_ref,
             x1_ref, x1b_ref, extg_ref, extv_ref, carry_ref, gbuf_ref, ffn_ref, acc_loss, acc_g2, acc_b2):
        i = pl.program_id(0)

        @pl.when(i == 0)
        def _():
            carry_ref[...] = jnp.zeros(carry_ref.shape, F32)
            acc_loss[...] = jnp.zeros(acc_loss.shape, F32)
            acc_g2[...] = jnp.zeros(acc_g2.shape, F32)
            acc_b2[...] = jnp.zeros(acc_b2.shape, F32)

        x1 = xh_ref[...] * g1_ref[...] + b1_ref[...]
        x1_ref[...] = x1
        x1b_ref[...] = x1.astype(BF16)

        for f in range(N_F):
            hg = _nn(x1b_ref[...], wup_ref[f])
            hv = _nn(x1b_ref[...], wup_ref[N_F + f])
            hu_ref[f] = hg
            hu_ref[N_F + f] = hv
            extg_ref[0:HALO_F, :] = carry_ref[f]
            extv_ref[0:HALO_F, :] = carry_ref[N_F + f]
            extg_ref[HALO_F:HALO_F + tm, :] = hg
            extv_ref[HALO_F:HALO_F + tm, :] = hv
            carry_ref[f] = extg_ref[tm:tm + HALO_F, :]
            carry_ref[N_F + f] = extv_ref[tm:tm + HALO_F, :]
            wg, wv = cfw_ref[f], cfw_ref[N_F + f]
            bg, bv = cfb_ref[f:f + 1, :], cfb_ref[N_F + f:N_F + f + 1, :]

            def rows(bi, wg=wg, wv=wv, bg=bg, bv=bv):
                base = bi * ROWS
                gate = _conv_f_block(extg_ref, base, wg)[0] + bg
                val = _conv_f_block(extv_ref, base, wv)[0] + bv
                gbuf_ref[pl.ds(base, ROWS), :] = (gate * _sigmoid(gate) * val).astype(BF16)

            _loop(tm // ROWS, rows)
            part = _nn(gbuf_ref[...], wdown_ref[f])
            if f == 0:
                ffn_ref[...] = part
            else:
                ffn_ref[...] += part

        def tail(bi):
            r = _rows(bi)
            xhat2, rstd2 = _ln_stats(ALPHA * x1_ref[r, :] + ffn_ref[r, :])
            err = xhat2 * g2_ref[...] + b2_ref[...] - tgt_ref[r, :]
            e2 = _rsum8(err * err)
            acc_loss[...] += sum(e2[:, k * 128:(k + 1) * 128] for k in range(D // 128))
            dy = err * (1.0 / D)
            acc_g2[...] += _rsum8(dy * xhat2)
            acc_b2[...] += _rsum8(dy)
            dr2_ref[r, :] = _ln_bwd(dy * g2_ref[...], xhat2, rstd2)

        _loop(tm // ROWS, tail)
        loss_ref[...] = acc_loss[...]

        @pl.when(i == nt - 1)
        def _():
            dg = jnp.sum(acc_g2[...], axis=0, keepdims=True)
            db = jnp.sum(acc_b2[...], axis=0, keepdims=True)
            for k in range(D // 128):
                sln2_ref[k:k + 1, :] = dg[:, k * 128:(k + 1) * 128]
                sln2_ref[8 + k:9 + k, :] = db[:, k * 128:(k + 1) * 128]

    row = pl.BlockSpec((tm, D), lambda i: (i, 0))
    return pl.pallas_call(
        body, name="ffn_forward", grid=(nt,),
        in_specs=[row, _full(ln1_g.shape), _full(ln1_b.shape), _resident(wup_g.shape),
                  _full(cfw.shape), _full(cfb.shape), _resident(wdown.shape),
                  _full(ln2_g.shape), _full(ln2_b.shape), row],
        out_specs=[pl.BlockSpec((N_DEV, tm, W_UP_BLK), lambda i: (0, i, 0)), row,
                   _full((8, 128)), _full((16, 128))],
        out_shape=[jax.ShapeDtypeStruct((N_DEV, t, W_UP_BLK), F32), jax.ShapeDtypeStruct((t, D), F32),
                   jax.ShapeDtypeStruct((8, 128), F32), jax.ShapeDtypeStruct((16, 128), F32)],
        scratch_shapes=[pltpu.VMEM((tm, D), F32), pltpu.VMEM((tm, D), BF16),
                        pltpu.VMEM((tm + HALO_F, W_UP_BLK), F32), pltpu.VMEM((tm + HALO_F, W_UP_BLK), F32),
                        pltpu.VMEM((N_DEV, HALO_F, W_UP_BLK), F32), pltpu.VMEM((tm, W_UP_BLK), BF16),
                        pltpu.VMEM((tm, D), F32), pltpu.VMEM((8, 128), F32),
                        pltpu.VMEM((8, D), F32), pltpu.VMEM((8, D), F32)],
        compiler_params=_params(("arbitrary",)),
    )(xhat1, ln1_g, ln1_b, wup_g, cfw, cfb, wdown, ln2_g, ln2_b, target)


def ffn_backward(dr2, xhat1, ln1_g, ln1_b, hu, wup_g, cfw, cfb, wdown, tm):
    t = dr2.shape[0]
    nt = t // tm
    hu4 = hu.reshape(2, N_F, t, W_UP_BLK)
    wup4 = wup_g.reshape(2, N_F, D, W_UP_BLK)
    cfw4 = cfw.reshape(2, N_F, KF, W_UP_BLK)
    cfb4 = cfb.reshape(2, N_F, 1, W_UP_BLK)
    halo_blocks = tm // HALO_F

    def body(dr2_ref, xh_ref, g1_ref, b1_ref, hu_ref, halo_ref, wup_ref, cfw_ref, cfb_ref, wdown_ref,
             dwup_ref, dwdown_ref, dcfw_ref, dcfb_ref, dx1_ref,
             x1b_ref, drb_ref, dg_ref, extg_ref, extv_ref, dextg_ref, dextv_ref, gbuf_ref,
             dhug_ref, dhuv_ref, acc_wup, acc_wdown, acc_cfw, acc_cfb, sem):
        f = pl.program_id(0)
        i = pl.program_id(1)
        first_tile = i == nt - 1

        @pl.when(i == 0)
        def _():
            acc_wup[...] = jnp.zeros(acc_wup.shape, F32)
            acc_wdown[...] = jnp.zeros(acc_wdown.shape, F32)
            acc_cfw[...] = jnp.zeros(acc_cfw.shape, F32)
            acc_cfb[...] = jnp.zeros(acc_cfb.shape, F32)
            dextg_ref[tm:tm + HALO_F, :] = jnp.zeros((HALO_F, W_UP_BLK), F32)
            dextv_ref[tm:tm + HALO_F, :] = jnp.zeros((HALO_F, W_UP_BLK), F32)

        x1b_ref[...] = (xh_ref[...] * g1_ref[...] + b1_ref[...]).astype(BF16)
        drb_ref[...] = dr2_ref[...].astype(BF16)
        dg_ref[...] = _nt(drb_ref[...], wdown_ref[0])

        keep = jnp.where(first_tile, 0.0, 1.0)
        extg_ref[0:HALO_F, :] = halo_ref[0, 0] * keep
        extv_ref[0:HALO_F, :] = halo_ref[1, 0] * keep
        extg_ref[HALO_F:HALO_F + tm, :] = hu_ref[0, 0]
        extv_ref[HALO_F:HALO_F + tm, :] = hu_ref[1, 0]
        wg, wv = cfw_ref[0, 0], cfw_ref[1, 0]
        bg, bv = cfb_ref[0, 0], cfb_ref[1, 0]

        def rows1(bi):
            base = bi * ROWS
            r = pl.ds(base, ROWS)
            gate, wing = _conv_f_block(extg_ref, base, wg)
            val, winv = _conv_f_block(extv_ref, base, wv)
            gate = gate + bg
            val = val + bv
            sg = _sigmoid(gate)
            silu = gate * sg
            gbuf_ref[r, :] = (silu * val).astype(BF16)
            dg = dg_ref[r, :]
            dgate = dg * val * (sg * (1.0 + gate * (1.0 - sg)))
            dval = dg * silu
            dextg_ref[r, :] = dgate
            dextv_ref[r, :] = dval
            acc_cfb[0:8, :] += _rsum8(dgate)
            acc_cfb[8:16, :] += _rsum8(dval)
            for k in range(KF):
                acc_cfw[8 * k:8 * k + 8, :] += _rsum8(dgate * wing[6 + k:6 + k + ROWS, :])
                acc_cfw[8 * (KF + k):8 * (KF + k) + 8, :] += _rsum8(dval * winv[6 + k:6 + k + ROWS, :])

        _loop(tm // ROWS, rows1)

        def rows2(bi):
            base = bi * ROWS
            r = pl.ds(base, ROWS)
            wing = dextg_ref[pl.ds(base, ROWS + HALO_F), :]
            winv = dextv_ref[pl.ds(base, ROWS + HALO_F), :]
            dhug = sum(wing[2 - k:2 - k + ROWS, :] * wg[k:k + 1, :] for k in range(KF))
            dhuv = sum(winv[2 - k:2 - k + ROWS, :] * wv[k:k + 1, :] for k in range(KF))
            dhug_ref[r, :] = dhug.astype(BF16)
            dhuv_ref[r, :] = dhuv.astype(BF16)

        _loop(tm // ROWS, rows2)
        dextg_ref[tm:tm + HALO_F, :] = dextg_ref[0:HALO_F, :]
        dextv_ref[tm:tm + HALO_F, :] = dextv_ref[0:HALO_F, :]

        acc_wdown[...] += _tn(gbuf_ref[...], drb_ref[...])
        x1t = x1b_ref[...].T
        acc_wup[0] += _nn(x1t, dhug_ref[...])
        acc_wup[1] += _nn(x1t, dhuv_ref[...])
        dx1_ref[0] = (_nt(dhug_ref[...], wup_ref[0, 0]) + _nt(dhuv_ref[...], wup_ref[1, 0])).astype(BF16)

        @pl.when(i == nt - 1)
        def _():
            for g in range(2):
                dcfb_ref[g, 0] = jnp.sum(acc_cfb[8 * g:8 * g + 8, :], axis=0, keepdims=True)
                for k in range(KF):
                    r0 = 8 * (g * KF + k)
                    dcfw_ref[g, 0, k:k + 1, :] = jnp.sum(acc_cfw[r0:r0 + 8, :], axis=0, keepdims=True)
            cps = [pltpu.make_async_copy(acc_wup.at[0], dwup_ref.at[0, f], sem.at[0]),
                   pltpu.make_async_copy(acc_wup.at[1], dwup_ref.at[1, f], sem.at[1]),
                   pltpu.make_async_copy(acc_wdown, dwdown_ref.at[f], sem.at[2])]
            for cp in cps:
                cp.start()
            for cp in cps:
                cp.wait()

    rev = lambda i: nt - 1 - i
    row = pl.BlockSpec((tm, D), lambda f, i: (rev(i), 0))
    pair = lambda r, c: pl.BlockSpec((2, 1, r, c), lambda f, i: (0, f, 0, 0))
    return pl.pallas_call(
        body, name="ffn_backward", grid=(N_F, nt),
        in_specs=[row, row, _full(ln1_g.shape), _full(ln1_b.shape),
                  pl.BlockSpec((2, 1, tm, W_UP_BLK), lambda f, i: (0, f, rev(i), 0)),
                  pl.BlockSpec((2, 1, HALO_F, W_UP_BLK),
                               lambda f, i: (0, f, jnp.maximum(rev(i) * halo_blocks - 1, 0), 0)),
                  pair(D, W_UP_BLK), pair(KF, W_UP_BLK), pair(1, W_UP_BLK),
                  pl.BlockSpec((1, W_UP_BLK, D), lambda f, i: (f, 0, 0))],
        out_specs=[ANY, ANY, pair(KF, W_UP_BLK), pair(1, W_UP_BLK),
                   pl.BlockSpec((1, tm, D), lambda f, i: (f, rev(i), 0))],
        out_shape=[jax.ShapeDtypeStruct((2, N_F, D, W_UP_BLK), F32),
                   jax.ShapeDtypeStruct((N_F, W_UP_BLK, D), F32),
                   jax.ShapeDtypeStruct((2, N_F, KF, W_UP_BLK), F32),
                   jax.ShapeDtypeStruct((2, N_F, 1, W_UP_BLK), F32),
                   jax.ShapeDtypeStruct((N_F, t, D), BF16)],
        scratch_shapes=[pltpu.VMEM((tm, D), BF16), pltpu.VMEM((tm, D), BF16),
                        pltpu.VMEM((tm, W_UP_BLK), F32),
                        pltpu.VMEM((tm + HALO_F, W_UP_BLK), F32), pltpu.VMEM((tm + HALO_F, W_UP_BLK), F32),
                        pltpu.VMEM((tm + HALO_F, W_UP_BLK), F32), pltpu.VMEM((tm + HALO_F, W_UP_BLK), F32),
                        pltpu.VMEM((tm, W_UP_BLK), BF16), pltpu.VMEM((tm, W_UP_BLK), BF16),
                        pltpu.VMEM((tm, W_UP_BLK), BF16),
                        pltpu.VMEM((2, D, W_UP_BLK), F32), pltpu.VMEM((W_UP_BLK, D), F32),
                        pltpu.VMEM((2 * KF * 8, W_UP_BLK), F32), pltpu.VMEM((16, W_UP_BLK), F32),
                        pltpu.SemaphoreType.DMA((3,))],
        compiler_params=_params(("arbitrary", "arbitrary")),
    )(dr2, xhat1, ln1_g, ln1_b, hu4, hu4, wup4, cfw4, cfb4, wdown)


def mix_backward(x, h, dx1p, dr2, xhat1, rstd1, win_g, ln_a_g, ln_a_b, w_spatial, bst,
                 conv_b_w, conv_b_b, ln_b_g, ln_b_b, wout, ln1_g, ffn_partials, tm):
    t = x.shape[0]
    n_p = len(ffn_partials)
    nt = t // tm
    n_chunks = tm // CHUNK
    halo_blocks = tm // HALO_B

    def body(x_ref, h_ref, halo_ref, dx1p_ref, dr2_ref, xh1_ref, rstd1_ref, win_ref, ga_ref, ba_ref,
             ws_ref, bst_ref, cw_ref, cb_ref, gb_ref, bb_ref, wout_ref, g1_ref, *rest):
        p_refs, rest = rest[:n_p], rest[n_p:]
        gx_ref, dwin_ref, dwout_ref, dcw_ref, small_ref = rest[:5]
        land_refs, rest = rest[5:5 + n_p], rest[5 + n_p:]
        (ext_ref, dext_ref, y_ref, dy_ref, dh_ref, dmb_ref, wsm_ref,
         acc_win, acc_wout, acc_bin, acc_lnag, acc_lnab, acc_ws, acc_bs, acc_cbb, acc_lnbg,
         acc_lnbb, acc_bout, acc_ln1g, acc_ln1b, acc_cw, sem, send_sems, recv_sems) = rest
        i = pl.program_id(0)

        @pl.when(i == 0)
        def _():
            for cp in _chip_copies(p_refs, land_refs, send_sems, recv_sems):
                cp.start()

        first_tile = i == nt - 1
        accs = [acc_win, acc_wout, acc_bin, acc_lnag, acc_lnab, acc_ws, acc_bs, acc_cbb, acc_lnbg,
                acc_lnbb, acc_bout, acc_ln1g, acc_ln1b, acc_cw]

        @pl.when(i == 0)
        def _():
            for acc in accs:
                acc[...] = jnp.zeros(acc.shape, F32)
            dext_ref[tm:tm + HALO_B, :] = jnp.zeros((HALO_B, D_B), F32)
            mask = _tril_mask()
            for hd in range(HEADS):
                wsm_ref[hd] = jnp.where(mask, ws_ref[hd], 0.0).astype(BF16)

        def ln1_rows(bi):
            r = _rows(bi)
            part = [dx1p_ref[f, r, :].astype(F32) for f in range(N_F)]
            dx1 = ALPHA * dr2_ref[r, :] + ((part[0] + part[1]) + (part[2] + part[3]))
            xhat = xh1_ref[r, :]
            acc_ln1g[...] += _rsum8(dx1 * xhat)
            acc_ln1b[...] += _rsum8(dx1)
            dr1 = _ln_bwd(dx1 * g1_ref[...], xhat, rstd1_ref[r, 0:1])
            acc_bout[...] += _rsum8(dr1)
            gx_ref[r, :] = ALPHA * dr1
            dmb_ref[r, :] = dr1.astype(BF16)

        _loop(tm // ROWS, ln1_rows)
        dy_ref[...] = _nt(dmb_ref[...], wout_ref[...])

        ha = halo_ref[:, 0:D_B]
        hg = halo_ref[:, D_B:2 * D_B]
        ext_ref[0:HALO_B, :] = jnp.where(first_tile, 0.0, 1.0) * (ha * _sigmoid(hg))

        def chunk(ci):
            r = _rows(ci, CHUNK)
            hu, hv = h_ref[r, 0:D_A], h_ref[r, D_A:2 * D_A]
            u, cdf_u, cdf_v, xhats, rstds, vns, svs = _mixer_a_fwd(hu, hv, ga_ref, ba_ref, wsm_ref, bst_ref)
            for hd in range(HEADS):
                sl = slice(hd * HEAD_DIM, (hd + 1) * HEAD_DIM)
                rows8 = slice(8 * hd, 8 * hd + 8)
                dy_a = dy_ref[r, sl]
                y_ref[r, sl] = (u[:, sl] * svs[hd]).astype(BF16)
                du = dy_a * svs[hd]
                dsv = dy_a * u[:, sl]
                dsvb = dsv.astype(BF16)
                acc_bs[hd] += dsv
                acc_ws[hd] += _nt(dsvb, vns[hd])
                dvn = _tn(wsm_ref[hd], dsvb)
                acc_lnag[rows8, :] += _rsum8(dvn * xhats[hd])
                acc_lnab[rows8, :] += _rsum8(dvn)
                dv = _ln_bwd(dvn * ga_ref[hd:hd + 1, :], xhats[hd], rstds[hd])
                hus, hvs = hu[:, sl], hv[:, sl]
                slv = slice(D_A + hd * HEAD_DIM, D_A + (hd + 1) * HEAD_DIM)
                dhu = du * (cdf_u[:, sl] + hus * jnp.exp(-0.5 * hus * hus) * INV_SQRT_2PI)
                dhv = dv * (cdf_v[:, sl] + hvs * jnp.exp(-0.5 * hvs * hvs) * INV_SQRT_2PI)
                acc_bin[:, sl] += _rsum8(dhu)
                acc_bin[:, slv] += _rsum8(dhv)
                dh_ref[r, sl] = dhu.astype(BF16)
                dh_ref[r, slv] = dhv.astype(BF16)
            a_b = h_ref[r, 2 * D_A:2 * D_A + D_B]
            g_b = h_ref[r, 2 * D_A + D_B:D_IN]
            ext_ref[pl.ds(HALO_B + ci * CHUNK, CHUNK), :] = a_b * _sigmoid(g_b)

        _loop(n_chunks, chunk)

        def conv_rows(bi):
            base = bi * ROWS
            r = pl.ds(base, ROWS)
            acc, win = _conv_b_block(ext_ref, base, cw_ref)
            xhat, rstd = _ln_stats(acc + cb_ref[...])
            yb2 = xhat * gb_ref[...] + bb_ref[...]
            sg = _sigmoid(yb2)
            y_ref[r, D_A:D] = (yb2 * sg).astype(BF16)
            dyb2 = dy_ref[r, D_A:D] * (sg * (1.0 + yb2 * (1.0 - sg)))
            acc_lnbg[...] += _rsum8(dyb2 * xhat)
            acc_lnbb[...] += _rsum8(dyb2)
            dyb1 = _ln_bwd(dyb2 * gb_ref[...], xhat, rstd)
            acc_cbb[...] += _rsum8(dyb1)
            dext_ref[r, :] = dyb1
            for k in range(KB):
                acc_cw[8 * k:8 * k + 8, :] += _rsum8(dyb1 * _tap(win, 2 + k))

        _loop(tm // ROWS, conv_rows)

        def convt_rows(bi):
            base = bi * ROWS
            r = pl.ds(base, ROWS)
            dwin = _shifted(dext_ref[pl.ds(base, ROWS + HALO_B), :])
            dyb0 = jnp.zeros((ROWS, D_B), F32)
            for k in range(KB):
                dyb0 = dyb0 + _tap(dwin, 30 - k) * cw_ref[k:k + 1, :]
            a_b = h_ref[r, 2 * D_A:2 * D_A + D_B]
            sg = _sigmoid(h_ref[r, 2 * D_A + D_B:D_IN])
            da_b = dyb0 * sg
            dg_b = dyb0 * a_b * sg * (1.0 - sg)
            acc_bin[:, 2 * D_A:2 * D_A + D_B] += _rsum8(da_b)
            acc_bin[:, 2 * D_A + D_B:D_IN] += _rsum8(dg_b)
            dh_ref[r, 2 * D_A:2 * D_A + D_B] = da_b.astype(BF16)
            dh_ref[r, 2 * D_A + D_B:D_IN] = dg_b.astype(BF16)

        _loop(tm // ROWS, convt_rows)
        dext_ref[tm:tm + HALO_B, :] = dext_ref[0:HALO_B, :]

        acc_wout[...] += _tn(y_ref[...], dmb_ref[...])
        xt = x_ref[...].T.astype(BF16)
        dh_blocks = [dh_ref[:, j * W_IN_BLK:(j + 1) * W_IN_BLK] for j in range(N_DEV)]
        for j in range(N_DEV):
            acc_win[j] += _nn(xt, dh_blocks[j])
        gx_ref[...] += sum(_nt(dh_blocks[j], win_ref[j]) for j in range(N_DEV))

        @pl.when(i == nt - 1)
        def _():
            cps = [pltpu.make_async_copy(acc_win, dwin_ref, sem.at[0]),
                   pltpu.make_async_copy(acc_wout, dwout_ref, sem.at[1])]
            for cp in cps:
                cp.start()
            small_ref[...] = jnp.zeros(small_ref.shape, F32)

            def put_row_vector(row0, acc):
                vec = jnp.sum(acc[...], axis=0, keepdims=True)
                for k in range(vec.shape[1] // 128):
                    small_ref[row0 + k:row0 + k + 1, :] = vec[:, k * 128:(k + 1) * 128]

            put_row_vector(S_BIN, acc_bin)
            put_row_vector(S_CBB, acc_cbb)
            put_row_vector(S_LNBG, acc_lnbg)
            put_row_vector(S_LNBB, acc_lnbb)
            put_row_vector(S_BOUT, acc_bout)
            put_row_vector(S_LN1G, acc_ln1g)
            put_row_vector(S_LN1B, acc_ln1b)
            mask = _tril_mask()
            for hd in range(HEADS):
                rows8 = slice(8 * hd, 8 * hd + 8)
                small_ref[S_LNAG + hd:S_LNAG + hd + 1, :] = jnp.sum(acc_lnag[rows8, :], axis=0, keepdims=True)
                small_ref[S_LNAB + hd:S_LNAB + hd + 1, :] = jnp.sum(acc_lnab[rows8, :], axis=0, keepdims=True)
                small_ref[S_WS + hd * CHUNK:S_WS + (hd + 1) * CHUNK, :] = jnp.where(mask, acc_ws[hd], 0.0)
                small_ref[S_BS + hd:S_BS + hd + 1, :] = jnp.sum(acc_bs[hd].T, axis=0, keepdims=True)
            for k in range(KB):
                dcw_ref[k:k + 1, :] = jnp.sum(acc_cw[8 * k:8 * k + 8, :], axis=0, keepdims=True)
            for cp in cps:
                cp.wait()
            for cp in _chip_copies(p_refs, land_refs, send_sems, recv_sems):
                cp.wait()

    rev = lambda i: nt - 1 - i
    row = lambda w: pl.BlockSpec((tm, w), lambda i: (rev(i), 0))
    return pl.pallas_call(
        body, name="mix_backward", grid=(nt,),
        in_specs=[row(D), row(D_IN),
                  pl.BlockSpec((HALO_B, 2 * D_B), lambda i: (jnp.maximum(rev(i) * halo_blocks - 1, 0), 1)),
                  pl.BlockSpec((N_F, tm, D), lambda i: (0, rev(i), 0)),
                  row(D), row(D), row(128), _resident(win_g.shape), _full(ln_a_g.shape),
                  _full(ln_a_b.shape), _full(w_spatial.shape), _full(bst.shape), _full(conv_b_w.shape),
                  _full(conv_b_b.shape), _full(ln_b_g.shape), _full(ln_b_b.shape),
                  _resident(wout.shape), _full(ln1_g.shape)] + [ANY] * n_p,
        out_specs=[row(D), ANY, ANY, _full((KB, D_B)), _full((S_MIX_ROWS, 128))] + [ANY] * n_p,
        out_shape=[jax.ShapeDtypeStruct((t, D), F32), jax.ShapeDtypeStruct((N_DEV, D, W_IN_BLK), F32),
                   jax.ShapeDtypeStruct((D, D), F32), jax.ShapeDtypeStruct((KB, D_B), F32),
                   jax.ShapeDtypeStruct((S_MIX_ROWS, 128), F32)]
        + [jax.ShapeDtypeStruct(p.shape, BF16) for p in ffn_partials],
        scratch_shapes=[pltpu.VMEM((tm + HALO_B, D_B), F32), pltpu.VMEM((tm + HALO_B, D_B), F32),
                        pltpu.VMEM((tm, D), BF16), pltpu.VMEM((tm, D), F32), pltpu.VMEM((tm, D_IN), BF16),
                        pltpu.VMEM((tm, D), BF16),
                        pltpu.VMEM((HEADS, CHUNK, CHUNK), BF16),
                        pltpu.VMEM((N_DEV, D, W_IN_BLK), F32), pltpu.VMEM((D, D), F32),
                        pltpu.VMEM((8, D_IN), F32), pltpu.VMEM((8 * HEADS, HEAD_DIM), F32),
                        pltpu.VMEM((8 * HEADS, HEAD_DIM), F32), pltpu.VMEM((HEADS, CHUNK, CHUNK), F32),
                        pltpu.VMEM((HEADS, CHUNK, CHUNK), F32), pltpu.VMEM((8, D_B), F32),
                        pltpu.VMEM((8, D_B), F32), pltpu.VMEM((8, D_B), F32), pltpu.VMEM((8, D), F32),
                        pltpu.VMEM((8, D), F32), pltpu.VMEM((8, D), F32), pltpu.VMEM((8 * KB, D_B), F32),
                        pltpu.SemaphoreType.DMA((2,)),
                        pltpu.SemaphoreType.DMA((n_p, 3)), pltpu.SemaphoreType.DMA((n_p, 3))],
        compiler_params=_params(("arbitrary",)),
    )(x, h, h, dx1p, dr2, xhat1, rstd1, win_g, ln_a_g, ln_a_b, w_spatial, bst, conv_b_w, conv_b_b,
      ln_b_g, ln_b_b, wout, ln1_g, *ffn_partials)


def _rows128(a):
    return a.reshape(-1, 128)


def _pack_conv(cb, cf):
    out = jnp.zeros((40, 768), F32)
    out = out.at[0:KB, 0:64].set(cb)
    return out.at[32:32 + KF, 0:W_UP_BLK].set(cf)


def kernel(x, w_in, b_in, ln_a_g, ln_a_b, w_spatial, b_spatial, conv_b_w, conv_b_b, ln_b_g, ln_b_b, w_out, b_out, ln1_g, ln1_b, w_up, conv_f_w, conv_f_b, w_down, ln2_g, ln2_b, loss_target, m_w_in, m_b_in, m_ln_a_g, m_ln_a_b, m_w_spatial, m_b_spatial, m_conv_b_w, m_conv_b_b, m_ln_b_g, m_ln_b_b, m_w_out, m_b_out, m_ln1_g, m_ln1_b, m_w_up, m_conv_f_w, m_conv_f_b, m_w_down, m_ln2_g, m_ln2_b, v_w_in, v_b_in, v_ln_a_g, v_ln_a_b, v_w_spatial, v_b_spatial, v_conv_b_w, v_conv_b_b, v_ln_b_g, v_ln_b_b, v_w_out, v_b_out, v_ln1_g, v_ln1_b, v_w_up, v_conv_f_w, v_conv_f_b, v_w_down, v_ln2_g, v_ln2_b):
    t = x.shape[1]
    x2 = x.reshape(t, D)
    target = loss_target.reshape(t, D)
    tm_fwd = min(t, 512)
    tm_bwd = min(t, 256)

    xi, yi, ci = _mesh_pos()
    jidx = jnp.stack([_lid(px, py, ci) for px, py in _chip_patterns(xi, yi)]).astype(jnp.int32)

    win_g, wout_g, conv_g, sup, sdown = all_gather_mixer_weights(
        w_in, w_out, w_up, w_down, _pack_conv(conv_b_w, conv_f_w))
    wout_full = wout_g.reshape(D, D)
    conv_b_full = conv_g[:, 0:KB, 0:64].transpose(1, 0, 2).reshape(KB, D_B)
    cfw = conv_g[:, 32:32 + KF, 0:W_UP_BLK]
    cfb = conv_f_b.reshape(N_DEV, W_UP_BLK)
    row = lambda a: a.reshape(1, -1)
    bst = b_spatial.T

    h, xhat1, rstd1, wup_g, wdown_g = mix_forward(
        x2, win_g, row(b_in), ln_a_g, ln_a_b, w_spatial, bst, conv_b_full, row(conv_b_b),
        row(ln_b_g), row(ln_b_b), wout_full, row(b_out), row(ln1_g), row(ln1_b), sup, sdown, tm_fwd)
    wdown4 = wdown_g.reshape(N_F, W_UP_BLK, D)
    hu, dr2, loss_part, s_ln2 = ffn_forward(
        xhat1, row(ln1_g), row(ln1_b), wup_g, cfw, cfb, wdown4, row(ln2_g), row(ln2_b), target, tm_bwd)
    loss = lax.psum(jnp.sum(loss_part), ("x", "y", "c")) * (0.5 / D)

    dwup, dwdown, dcfw, dcfb, dx1p = ffn_backward(
        dr2, xhat1, row(ln1_g), row(ln1_b), hu, wup_g, cfw, cfb, wdown4, tm_bwd)
    ffn_grads = [dwup.reshape(N_DEV, D, W_UP_BLK), dwdown.reshape(N_DEV, D_FF // N_DEV, D)]
    ffn_lands = exchange_with_sibling("exchange_with_sibling_ffn", ffn_grads)
    ffn_partials = [chip_partials("chip_partials_" + nm, g, l, jidx, rb)
                    for nm, g, l, rb in zip(["w_up", "w_down"], ffn_grads, ffn_lands, [256, 352])]
    grad_x, dwin, dwout, dcw, s_mix, *ffn_recvs = mix_backward(
        x2, h, dx1p, dr2, xhat1, rstd1, win_g, ln_a_g, ln_a_b, w_spatial, bst,
        conv_b_full, row(conv_b_b), row(ln_b_g), row(ln_b_b), wout_full, row(ln1_g), ffn_partials, tm_bwd)

    dcfb_rows = jnp.pad(dcfb.reshape(-1, 128), ((0, 4), (0, 0)))
    svec = jnp.concatenate([s_mix, dcfb_rows, s_ln2], axis=0)
    dconv = jnp.zeros((N_DEV, 40, 768), F32)
    dconv = dconv.at[:, 0:KB, 0:64].set(dcw.reshape(KB, N_DEV, 64).transpose(1, 0, 2))
    dconv = dconv.at[:, 32:32 + KF, 0:W_UP_BLK].set(dcfw.reshape(N_DEV, KF, W_UP_BLK))
    mix_grads = [dwin, dwout.reshape(N_DEV, D // N_DEV, D), dconv]
    *mix_lands, sv_land = exchange_with_sibling("exchange_with_sibling_mixer", mix_grads, svec)
    mix_partials = [chip_partials("chip_partials_" + nm, g, l, jidx, rb)
                    for nm, g, l, rb in zip(["w_in", "w_out", "conv"], mix_grads, mix_lands, [512, 128, 40])]
    *mix_recvs, sv_slots = exchange_between_chips(mix_partials, svec, sv_land)
    names = ["w_in", "w_out", "conv", "w_up", "w_down"]
    grads = mix_grads + ffn_grads
    lands = mix_lands + list(ffn_lands)
    recvs = mix_recvs + ffn_recvs
    row_blocks = [512, 128, 40, 256, 352]

    shard_w = [w_in, w_out, _pack_conv(conv_b_w, conv_f_w), w_up, w_down]
    shard_m = [m_w_in, m_w_out, _pack_conv(m_conv_b_w, m_conv_f_w), m_w_up, m_w_down]
    shard_v = [v_w_in, v_w_out, _pack_conv(v_conv_b_w, v_conv_f_w), v_w_up, v_w_down]
    big = {}
    for nm, g, l, r, w, m, v, rb in zip(names, grads, lands, recvs, shard_w, shard_m, shard_v, row_blocks):
        big[nm] = reduce_and_adamw("reduce_adamw_" + nm, g, l, r, w, m, v, jidx, rb)
    for k in range(4):
        packed = big["conv"][k]
        big.setdefault("conv_b_w", []).append(packed[0:KB, 0:64])
        big.setdefault("conv_f_w", []).append(packed[32:32 + KF, 0:W_UP_BLK])

    small_w = dict(b_in=b_in, ln_a_g=ln_a_g, ln_a_b=ln_a_b, w_spatial=w_spatial, b_spatial=b_spatial,
                   conv_b_b=conv_b_b, ln_b_g=ln_b_g, ln_b_b=ln_b_b, b_out=b_out, ln1_g=ln1_g,
                   ln1_b=ln1_b, conv_f_b=conv_f_b, ln2_g=ln2_g, ln2_b=ln2_b)
    small_m = dict(b_in=m_b_in, ln_a_g=m_ln_a_g, ln_a_b=m_ln_a_b, w_spatial=m_w_spatial,
                   b_spatial=m_b_spatial, conv_b_b=m_conv_b_b, ln_b_g=m_ln_b_g, ln_b_b=m_ln_b_b,
                   b_out=m_b_out, ln1_g=m_ln1_g, ln1_b=m_ln1_b, conv_f_b=m_conv_f_b, ln2_g=m_ln2_g,
                   ln2_b=m_ln2_b)
    small_v = dict(b_in=v_b_in, ln_a_g=v_ln_a_g, ln_a_b=v_ln_a_b, w_spatial=v_w_spatial,
                   b_spatial=v_b_spatial, conv_b_b=v_conv_b_b, ln_b_g=v_ln_b_g, ln_b_b=v_ln_b_b,
                   b_out=v_b_out, ln1_g=v_ln1_g, ln1_b=v_ln1_b, conv_f_b=v_conv_f_b, ln2_g=v_ln2_g,
                   ln2_b=v_ln2_b)
    order = [nm for nm, _, _ in SMALL_LAYOUT]
    small_out = small_adamw(sv_slots, [_rows128(small_w[nm]) for nm in order],
                            [_rows128(small_m[nm]) for nm in order], [_rows128(small_v[nm]) for nm in order])
    n_small = len(order)
    small = {nm: [small_out[k * n_small + p].reshape(small_w[nm].shape) for k in range(4)]
             for p, nm in enumerate(order)}

    weights = ["w_in", "b_in", "ln_a_g", "ln_a_b", "w_spatial", "b_spatial", "conv_b_w", "conv_b_b",
               "ln_b_g", "ln_b_b", "w_out", "b_out", "ln1_g", "ln1_b", "w_up", "conv_f_w", "conv_f_b",
               "w_down", "ln2_g", "ln2_b"]
    result = lambda nm, k: big[nm][k] if nm in big else small[nm][k]
    return (loss, grad_x.reshape(x.shape), *[result(nm, 0) for nm in weights],
            *[result(nm, 1) for nm in weights], *[result(nm, 2) for nm in weights],
            *[result(nm, 3) for nm in weights])
```

```python
import functools
import math

import jax
import jax.numpy as jnp
from jax import lax
from jax.experimental import pallas as pl
from jax.experimental.pallas import tpu as pltpu

F32 = jnp.float32
BF16 = jnp.bfloat16

D = 1024
D_A = 512
D_B = 512
HEADS = 4
HEAD_DIM = 128
CHUNK = 128
KB = 31
KF = 3
D_FF = 2816
D_IN = 2048
N_DEV = 8
W_IN_BLK = D_IN // N_DEV
W_UP_BLK = 2 * D_FF // N_DEV
N_F = 4
LN_EPS = 1e-5
ALPHA = 2.0 ** 0.25

ADAM_LR = 0.001
ADAM_B1 = 0.9
ADAM_B2 = 0.999
ADAM_EPS = 1e-08
ADAM_WD = 0.01
ADAM_STEP = 10

INV_SQRT2 = 1.0 / math.sqrt(2.0)
INV_SQRT_2PI = 1.0 / math.sqrt(2.0 * math.pi)

HALO_B = 32
HALO_F = 8
ROWS = 64
VMEM_LIMIT = 58 * 1024 * 1024

MESH = pl.DeviceIdType.MESH
ANY = pl.BlockSpec(memory_space=pl.ANY)
VMEM = pl.BlockSpec(memory_space=pltpu.VMEM)

S_BIN, S_LNAG, S_LNAB, S_WS, S_BS, S_CBB, S_LNBG, S_LNBB, S_BOUT, S_LN1G, S_LN1B = (
    0, 16, 24, 32, 544, 552, 560, 568, 576, 584, 592)
S_MIX_ROWS = 600
S_CFB = 600
S_LN2G = 648
S_LN2B = 656
S_ROWS = 664


def _tn(a, b):
    return lax.dot_general(a, b, (((0,), (0,)), ((), ())), preferred_element_type=F32)


def _nt(a, b):
    return lax.dot_general(a, b, (((1,), (1,)), ((), ())), preferred_element_type=F32)


def _nn(a, b):
    return jnp.dot(a, b, preferred_element_type=F32)


def _sigmoid(x):
    return 1.0 / (1.0 + jnp.exp(-x))


def _ln_stats(x):
    mu = jnp.mean(x, axis=-1, keepdims=True)
    xc = x - mu
    var = jnp.mean(xc * xc, axis=-1, keepdims=True)
    rstd = lax.rsqrt(var + LN_EPS)
    return xc * rstd, rstd


def _ln_bwd(dxhat, xhat, rstd):
    m1 = jnp.mean(dxhat, axis=-1, keepdims=True)
    m2 = jnp.mean(dxhat * xhat, axis=-1, keepdims=True)
    return rstd * (dxhat - m1 - xhat * m2)


def _rsum8(x):
    r, n = x.shape
    return x.reshape(r // 8, 8, n).sum(axis=0)


def _rows(i, n=ROWS):
    return pl.ds(i * n, n)


def _loop(n, body):
    for i in range(n):
        body(i)


def _tril_mask():
    r = lax.broadcasted_iota(jnp.int32, (CHUNK, CHUNK), 0)
    c = lax.broadcasted_iota(jnp.int32, (CHUNK, CHUNK), 1)
    return c <= r


def _mixer_a_fwd(hu, hv, ga_ref, ba_ref, wsm_ref, bst_ref):
    cdf_u = 0.5 * (1.0 + lax.erf(hu * INV_SQRT2))
    cdf_v = 0.5 * (1.0 + lax.erf(hv * INV_SQRT2))
    u = hu * cdf_u
    v = hv * cdf_v
    xhats, rstds, vns, svs = [], [], [], []
    for hd in range(HEADS):
        sl = slice(hd * HEAD_DIM, (hd + 1) * HEAD_DIM)
        xhat, rstd = _ln_stats(v[:, sl])
        vn = (xhat * ga_ref[hd:hd + 1, :] + ba_ref[hd:hd + 1, :]).astype(BF16)
        sv = _nn(wsm_ref[hd], vn) + bst_ref[:, hd:hd + 1]
        xhats.append(xhat)
        rstds.append(rstd)
        vns.append(vn)
        svs.append(sv)
    return u, cdf_u, cdf_v, xhats, rstds, vns, svs


def _shifted(win):
    n = win.shape[0]
    return [win] + [pltpu.roll(win, n - s, 0) for s in range(1, 8)]


def _tap(shifted, offset):
    s = offset % 8
    return shifted[s][offset - s:offset - s + ROWS, :]


def _conv_b_block(ext_ref, base, cw_ref):
    win = _shifted(ext_ref[pl.ds(base, ROWS + HALO_B), :])
    acc = jnp.zeros((ROWS, D_B), F32)
    for k in range(KB):
        acc = acc + _tap(win, 2 + k) * cw_ref[k:k + 1, :]
    return acc, win


def _taps_f(win):
    n = ROWS + HALO_F
    return [pltpu.roll(win, n - 6, 0)[0:ROWS, :], pltpu.roll(win, n - 7, 0)[0:ROWS, :], win[8:n, :]]


def _params(sem, **kw):
    return pltpu.CompilerParams(dimension_semantics=sem, vmem_limit_bytes=VMEM_LIMIT, **kw)


def _resident(shape):
    zeros = (0,) * len(shape)
    return pl.BlockSpec(shape, lambda *_: zeros, pipeline_mode=pl.Buffered(1))


def _full(shape):
    zeros = (0,) * len(shape)
    return pl.BlockSpec(shape, lambda *_: zeros)


def _mesh_pos():
    return lax.axis_index("x"), lax.axis_index("y"), lax.axis_index("c")


def _chip_patterns(x, y):
    return [(x, y), (1 - x, y), (x, 1 - y), (1 - x, 1 - y)]


def _lid(x, y, c):
    return 4 * x + 2 * y + c


def _gather_copy(outs, send_sems, recv_sems, a, k, block, to, src=None):
    blk = outs[a].at[_lid(*block)]
    return pltpu.make_async_remote_copy(
        src_ref=blk if src is None else src, dst_ref=blk,
        send_sem=send_sems.at[a, k], recv_sem=recv_sems.at[a, k], device_id=to, device_id_type=MESH)


def _gather_start(mine, outs, send_sems, recv_sems, local_sems):
    x, y, c = _mesh_pos()
    me, sib = (x, y, c), (x, y, 1 - c)
    for a in range(len(mine)):
        pltpu.make_async_copy(mine[a], outs[a].at[_lid(*me)], local_sems.at[a]).start()
        _gather_copy(outs, send_sems, recv_sems, a, 0, me, sib, src=mine[a]).start()
        for j, chip in enumerate(_chip_patterns(x, y)[1:]):
            _gather_copy(outs, send_sems, recv_sems, a, 1 + j, me, (*chip, c), src=mine[a]).start()


def _gather_finish(mine, outs, send_sems, recv_sems, local_sems):
    x, y, c = _mesh_pos()
    me, sib = (x, y, c), (x, y, 1 - c)
    chips = _chip_patterns(x, y)[1:]
    n = len(mine)
    copy = functools.partial(_gather_copy, outs, send_sems, recv_sems)
    passed = []
    for j, chip in enumerate(chips):
        for a in range(n):
            copy(a, 1 + j, (*chip, c), me).wait_recv()
            cp = copy(a, 4 + j, (*chip, c), sib)
            cp.start()
            passed.append(cp)
    for a in range(n):
        copy(a, 0, sib, me).wait_recv()
        for j, chip in enumerate(chips):
            copy(a, 4 + j, (*chip, 1 - c), me).wait_recv()
        for k in range(4):
            copy(a, k, me, sib, src=mine[a]).wait_send()
        pltpu.make_async_copy(mine[a], outs[a].at[_lid(*me)], local_sems.at[a]).wait()
    for cp in passed:
        cp.wait_send()


def _gather_scratch(n):
    return [pltpu.SemaphoreType.DMA((n, 7)), pltpu.SemaphoreType.DMA((n, 7)), pltpu.SemaphoreType.DMA((n,))]


def all_gather_mixer_weights(w_in, w_out, w_up, w_down, convp):
    srcs = [w_in, w_out, convp]
    n = len(srcs)

    def body(win_ref, wout_ref, convp_ref, wup_ref, wdown_ref,
             gin_ref, gout_ref, gconv_ref, sup_ref, sdown_ref,
             sin_ref, sout_ref, send_sems, recv_sems, local_sems):
        sin_ref[...] = win_ref[...].astype(BF16)
        sout_ref[...] = wout_ref[...].astype(BF16)
        mine = [sin_ref, sout_ref, convp_ref]
        outs = [gin_ref, gout_ref, gconv_ref]
        _gather_start(mine, outs, send_sems, recv_sems, local_sems)
        sup_ref[...] = wup_ref[...].astype(BF16)
        sdown_ref[...] = wdown_ref[...].astype(BF16)
        _gather_finish(mine, outs, send_sems, recv_sems, local_sems)

    return pl.pallas_call(
        body, name="all_gather_mixer_weights",
        out_shape=[jax.ShapeDtypeStruct((N_DEV,) + w_in.shape, BF16),
                   jax.ShapeDtypeStruct((N_DEV,) + w_out.shape, BF16),
                   jax.ShapeDtypeStruct((N_DEV,) + convp.shape, F32),
                   jax.ShapeDtypeStruct(w_up.shape, BF16), jax.ShapeDtypeStruct(w_down.shape, BF16)],
        in_specs=[VMEM] * 5, out_specs=[ANY] * n + [VMEM, VMEM],
        scratch_shapes=[pltpu.VMEM(w_in.shape, BF16), pltpu.VMEM(w_out.shape, BF16)] + _gather_scratch(n),
        compiler_params=pltpu.CompilerParams(vmem_limit_bytes=VMEM_LIMIT),
    )(w_in, w_out, convp, w_up, w_down)


def exchange_with_sibling(name, grads, svec=None):
    srcs = list(grads) + ([] if svec is None else [svec])
    n, n_all = len(grads), len(srcs)

    def body(*refs):
        src, land = refs[:n_all], refs[n_all:2 * n_all]
        send_sems, recv_sems = refs[2 * n_all:]
        x, y, c = _mesh_pos()
        sib = (x, y, 1 - c)
        copies = []
        for a in range(n):
            for k, (px, py) in enumerate(_chip_patterns(x, y)):
                copies.append(pltpu.make_async_remote_copy(
                    src_ref=src[a].at[_lid(px, py, 1 - c)], dst_ref=land[a].at[k],
                    send_sem=send_sems.at[a, k], recv_sem=recv_sems.at[a, k],
                    device_id=sib, device_id_type=MESH))
        if svec is not None:
            copies.append(pltpu.make_async_remote_copy(
                src_ref=src[n], dst_ref=land[n], send_sem=send_sems.at[n, 0], recv_sem=recv_sems.at[n, 0],
                device_id=sib, device_id_type=MESH))
        for cp in copies:
            cp.start()
        for cp in copies:
            cp.wait()

    return pl.pallas_call(
        body, name=name,
        out_shape=[jax.ShapeDtypeStruct((4,) + g.shape[1:], F32) for g in grads]
        + ([] if svec is None else [jax.ShapeDtypeStruct(svec.shape, F32)]),
        in_specs=[ANY] * n_all, out_specs=[ANY] * n_all,
        scratch_shapes=[pltpu.SemaphoreType.DMA((n_all, 4)), pltpu.SemaphoreType.DMA((n_all, 4))],
    )(*srcs)


def _chip_copies(p, land, send_sems, recv_sems):
    x, y, c = _mesh_pos()
    return [pltpu.make_async_remote_copy(
        src_ref=p[a].at[k], dst_ref=land[a].at[k], send_sem=send_sems.at[a, k], recv_sem=recv_sems.at[a, k],
        device_id=(px, py, c), device_id_type=MESH)
        for k, (px, py) in enumerate(_chip_patterns(x, y)[1:]) for a in range(len(p))]


def exchange_between_chips(partials, svec, sv_land):
    n = len(partials)

    def body(*refs):
        p = refs[:n]
        sv_ref, svl_ref = refs[n], refs[n + 1]
        land = refs[n + 2:2 * n + 2]
        sv_slots = refs[2 * n + 2]
        chip_sv, send_sems, recv_sems, sv_send, sv_recv, local_sem = refs[2 * n + 3:]
        x, y, c = _mesh_pos()
        q = 2 * x + y
        chip_sv[...] = sv_ref[...] + svl_ref[...]
        local = pltpu.make_async_copy(chip_sv, sv_slots.at[q], local_sem)
        local.start()
        copies = _chip_copies(p, land, send_sems, recv_sems)
        for cp in copies:
            cp.start()
        sv_copies = []
        for k, (px, py) in enumerate(_chip_patterns(x, y)[1:]):
            cp = pltpu.make_async_remote_copy(
                src_ref=chip_sv, dst_ref=sv_slots.at[q],
                send_sem=sv_send.at[k], recv_sem=sv_recv.at[k],
                device_id=(px, py, c), device_id_type=MESH)
            cp.start()
            sv_copies.append(cp)
        for cp in copies:
            cp.wait()
        for k, (px, py) in enumerate(_chip_patterns(x, y)[1:]):
            sv_copies[k].wait_send()
            pltpu.make_async_remote_copy(
                src_ref=chip_sv, dst_ref=sv_slots.at[2 * px + py],
                send_sem=sv_send.at[k], recv_sem=sv_recv.at[k],
                device_id=(px, py, c), device_id_type=MESH).wait_recv()
        local.wait()

    return pl.pallas_call(
        body, name="exchange_between_chips",
        out_shape=[jax.ShapeDtypeStruct(p.shape, BF16) for p in partials]
        + [jax.ShapeDtypeStruct((4,) + svec.shape, F32)],
        in_specs=[ANY] * n + [VMEM, VMEM], out_specs=[ANY] * (n + 1),
        scratch_shapes=[pltpu.VMEM(svec.shape, F32),
                        pltpu.SemaphoreType.DMA((n, 3)), pltpu.SemaphoreType.DMA((n, 3)),
                        pltpu.SemaphoreType.DMA((3,)), pltpu.SemaphoreType.DMA((3,)),
                        pltpu.SemaphoreType.DMA],
    )(*partials, svec, sv_land)


def chip_partials(name, g, land, jidx, rb):
    _, r, c = g.shape

    def body(j_ref, g_ref, l_ref, o_ref):
        o_ref[...] = (g_ref[...] + l_ref[...]).astype(BF16)

    return pl.pallas_call(
        body, name=name,
        out_shape=jax.ShapeDtypeStruct((3, r, c), BF16),
        grid_spec=pltpu.PrefetchScalarGridSpec(
            num_scalar_prefetch=1, grid=(3, r // rb),
            in_specs=[pl.BlockSpec((1, rb, c), lambda k, i, j: (j[1 + k], i, 0)),
                      pl.BlockSpec((1, rb, c), lambda k, i, j: (1 + k, i, 0))],
            out_specs=pl.BlockSpec((1, rb, c), lambda k, i, j: (k, i, 0))),
        compiler_params=_params(("arbitrary", "arbitrary")),
    )(jidx, g, land)


def _adamw(w, g, m, v):
    m2 = ADAM_B1 * m + (1.0 - ADAM_B1) * g
    v2 = ADAM_B2 * v + (1.0 - ADAM_B2) * (g * g)
    m_hat = m2 / (1.0 - ADAM_B1 ** ADAM_STEP)
    v_hat = v2 / (1.0 - ADAM_B2 ** ADAM_STEP)
    delta = -ADAM_LR * (m_hat / (jnp.sqrt(v_hat) + ADAM_EPS) + ADAM_WD * w)
    return delta, m2, v2


def reduce_and_adamw(name, g, land, recv, w, m, v, jidx, rb):
    _, r, c = g.shape

    def body(j_ref, g_ref, l_ref, r_ref, w_ref, m_ref, v_ref, go_ref, do_ref, mo_ref, vo_ref):
        grad = (g_ref[0] + l_ref[0]) + r_ref[0].astype(F32) + r_ref[1].astype(F32) + r_ref[2].astype(F32)
        delta, m2, v2 = _adamw(w_ref[...], grad, m_ref[...], v_ref[...])
        go_ref[...] = grad
        do_ref[...] = delta
        mo_ref[...] = m2
        vo_ref[...] = v2

    blk = pl.BlockSpec((rb, c), lambda i, j: (i, 0))
    return pl.pallas_call(
        body, name=name,
        out_shape=[jax.ShapeDtypeStruct((r, c), F32)] * 4,
        grid_spec=pltpu.PrefetchScalarGridSpec(
            num_scalar_prefetch=1, grid=(r // rb,),
            in_specs=[pl.BlockSpec((1, rb, c), lambda i, j: (j[0], i, 0)),
                      pl.BlockSpec((1, rb, c), lambda i, j: (0, i, 0)),
                      pl.BlockSpec((3, rb, c), lambda i, j: (0, i, 0)),
                      blk, blk, blk],
            out_specs=[blk] * 4),
        compiler_params=_params(("arbitrary",)),
    )(jidx, g, land, recv, w, m, v)


SMALL_LAYOUT = [
    ("b_in", S_BIN, 16), ("ln_a_g", S_LNAG, 4), ("ln_a_b", S_LNAB, 4), ("w_spatial", S_WS, 512),
    ("b_spatial", S_BS, 4), ("conv_b_b", S_CBB, 4), ("ln_b_g", S_LNBG, 4), ("ln_b_b", S_LNBB, 4),
    ("b_out", S_BOUT, 8), ("ln1_g", S_LN1G, 8), ("ln1_b", S_LN1B, 8), ("conv_f_b", S_CFB, 44),
    ("ln2_g", S_LN2G, 8), ("ln2_b", S_LN2B, 8),
]


def small_adamw(sv_slots, ws, ms, vs):
    n = len(SMALL_LAYOUT)

    def body(*refs):
        s_ref = refs[0]
        w_refs, m_refs, v_refs = refs[1:1 + n], refs[1 + n:1 + 2 * n], refs[1 + 2 * n:1 + 3 * n]
        outs = refs[1 + 3 * n:]
        for p, (_, row0, rows) in enumerate(SMALL_LAYOUT):
            sl = pl.ds(row0, rows)
            grad = ((s_ref[0, sl, :] + s_ref[1, sl, :]) + s_ref[2, sl, :]) + s_ref[3, sl, :]
            delta, m2, v2 = _adamw(w_refs[p][...], grad, m_refs[p][...], v_refs[p][...])
            outs[p][...] = grad
            outs[n + p][...] = delta
            outs[2 * n + p][...] = m2
            outs[3 * n + p][...] = v2

    shapes = [jax.ShapeDtypeStruct((rows, 128), F32) for _, _, rows in SMALL_LAYOUT]
    return pl.pallas_call(
        body, name="small_adamw", out_shape=shapes * 4,
        in_specs=[VMEM] * (1 + 3 * n), out_specs=[VMEM] * (4 * n),
    )(sv_slots, *ws, *ms, *vs)


def mix_forward(x, win_g, b_in, ln_a_g, ln_a_b, w_spatial, bst, conv_b_w, conv_b_b, ln_b_g, ln_b_b,
                wout, b_out, ln1_g, ln1_b, sup, sdown, tm):
    t = x.shape[0]
    nt = t // tm
    n_chunks = tm // CHUNK

    def body(x_ref, win_ref, bin_ref, ga_ref, ba_ref, ws_ref, bst_ref, cw_ref, cb_ref, gb_ref,
             bb_ref, wout_ref, bout_ref, g1_ref, b1_ref, sup_ref, sdown_ref,
             h_ref, xhat1_ref, rstd1_ref, gup_ref, gdown_ref,
             ext_ref, y_ref, wsm_ref, send_sems, recv_sems, local_sems):
        i = pl.program_id(0)
        gather = ([sup_ref, sdown_ref], [gup_ref, gdown_ref], send_sems, recv_sems, local_sems)

        @pl.when(i == 0)
        def _():
            _gather_start(*gather)
            ext_ref[0:HALO_B, :] = jnp.zeros((HALO_B, D_B), F32)
            mask = _tril_mask()
            for hd in range(HEADS):
                wsm_ref[hd] = jnp.where(mask, ws_ref[hd], 0.0).astype(BF16)

        xb = x_ref[...].astype(BF16)
        for j in range(N_DEV):
            cols = slice(j * W_IN_BLK, (j + 1) * W_IN_BLK)
            h_ref[:, cols] = _nn(xb, win_ref[j]) + bin_ref[:, cols]

        def chunk(ci):
            r = _rows(ci, CHUNK)
            u, _, _, _, _, _, svs = _mixer_a_fwd(h_ref[r, 0:D_A], h_ref[r, D_A:2 * D_A],
                                                 ga_ref, ba_ref, wsm_ref, bst_ref)
            for hd in range(HEADS):
                sl = slice(hd * HEAD_DIM, (hd + 1) * HEAD_DIM)
                y_ref[r, sl] = (u[:, sl] * svs[hd]).astype(BF16)
            a_b = h_ref[r, 2 * D_A:2 * D_A + D_B]
            g_b = h_ref[r, 2 * D_A + D_B:D_IN]
            ext_ref[pl.ds(HALO_B + ci * CHUNK, CHUNK), :] = a_b * _sigmoid(g_b)

        _loop(n_chunks, chunk)

        def conv_rows(bi):
            base = bi * ROWS
            acc, _ = _conv_b_block(ext_ref, base, cw_ref)
            xhat, _ = _ln_stats(acc + cb_ref[...])
            yb2 = xhat * gb_ref[...] + bb_ref[...]
            y_ref[pl.ds(base, ROWS), D_A:D] = (yb2 * _sigmoid(yb2)).astype(BF16)

        _loop(tm // ROWS, conv_rows)
        ext_ref[0:HALO_B, :] = ext_ref[tm:tm + HALO_B, :]

        mix = _nn(y_ref[...], wout_ref[...]) + bout_ref[...]
        xhat1, rstd1 = _ln_stats(ALPHA * x_ref[...] + mix)
        xhat1_ref[...] = xhat1
        rstd1_ref[...] = jnp.broadcast_to(rstd1, (tm, 128))

        @pl.when(i == nt - 1)
        def _():
            _gather_finish(*gather)

    row = lambda w: pl.BlockSpec((tm, w), lambda i: (i, 0))
    return pl.pallas_call(
        body, name="mix_forward", grid=(nt,),
        in_specs=[row(D), _resident(win_g.shape), _full(b_in.shape), _full(ln_a_g.shape),
                  _full(ln_a_b.shape), _full(w_spatial.shape), _full(bst.shape),
                  _full(conv_b_w.shape), _full(conv_b_b.shape), _full(ln_b_g.shape),
                  _full(ln_b_b.shape), _resident(wout.shape), _full(b_out.shape),
                  _full(ln1_g.shape), _full(ln1_b.shape), ANY, ANY],
        out_specs=[row(D_IN), row(D), row(128), ANY, ANY],
        out_shape=[jax.ShapeDtypeStruct((t, D_IN), F32), jax.ShapeDtypeStruct((t, D), F32),
                   jax.ShapeDtypeStruct((t, 128), F32),
                   jax.ShapeDtypeStruct((N_DEV,) + sup.shape, BF16),
                   jax.ShapeDtypeStruct((N_DEV,) + sdown.shape, BF16)],
        scratch_shapes=[pltpu.VMEM((tm + HALO_B, D_B), F32), pltpu.VMEM((tm, D), BF16),
                        pltpu.VMEM((HEADS, CHUNK, CHUNK), BF16)] + _gather_scratch(2),
        compiler_params=_params(("arbitrary",)),
    )(x, win_g, b_in, ln_a_g, ln_a_b, w_spatial, bst, conv_b_w, conv_b_b, ln_b_g, ln_b_b,
      wout, b_out, ln1_g, ln1_b, sup, sdown)


def ffn_forward(xhat1, ln1_g, ln1_b, wup_g, cfw, cfb, wdown, ln2_g, ln2_b, target, tm):
    t = xhat1.shape[0]
    nt = t // tm

    def body(xh_ref, g1_ref, b1_ref, wup_ref, cfw_ref, cfb_ref, wdown_ref, g2_ref, b2_ref, tgt_ref,
             hu_ref, dr2_ref, loss_ref, sln2_ref,
             x1_ref, x1b_ref, carry_ref, gbuf_ref, ffn_ref, acc_loss, acc_g2, acc_b2):
        i = pl.program_id(0)

        @pl.when(i == 0)
        def _():
            carry_ref[...] = jnp.zeros(carry_ref.shape, F32)
            acc_loss[...] = jnp.zeros(acc_loss.shape, F32)
            acc_g2[...] = jnp.zeros(acc_g2.shape, F32)
            acc_b2[...] = jnp.zeros(acc_b2.shape, F32)

        x1 = xh_ref[...] * g1_ref[...] + b1_ref[...]
        x1_ref[...] = x1
        x1b_ref[...] = x1.astype(BF16)

        def conv(j, base):
            if base == 0:
                win = jnp.concatenate([carry_ref[j], hu_ref[j, 0:ROWS, :]], axis=0)
            else:
                win = hu_ref[j, base - HALO_F:base + ROWS, :]
            taps = _taps_f(win)
            w = cfw_ref[j]
            return sum(taps[k] * w[k:k + 1, :] for k in range(KF)) + cfb_ref[j:j + 1, :]

        for f in range(N_F):
            hu_ref[f] = _nn(x1b_ref[...], wup_ref[f])
            hu_ref[N_F + f] = _nn(x1b_ref[...], wup_ref[N_F + f])

            def rows(bi, f=f):
                gate = conv(f, bi * ROWS)
                val = conv(N_F + f, bi * ROWS)
                gbuf_ref[_rows(bi), :] = (gate * _sigmoid(gate) * val).astype(BF16)

            _loop(tm // ROWS, rows)
            carry_ref[f] = hu_ref[f, tm - HALO_F:tm, :]
            carry_ref[N_F + f] = hu_ref[N_F + f, tm - HALO_F:tm, :]
            part = _nn(gbuf_ref[...], wdown_ref[f])
            if f == 0:
                ffn_ref[...] = part
            else:
                ffn_ref[...] += part

        def tail(bi):
            r = _rows(bi)
            xhat2, rstd2 = _ln_stats(ALPHA * x1_ref[r, :] + ffn_ref[r, :])
            err = xhat2 * g2_ref[...] + b2_ref[...] - tgt_ref[r, :]
            e2 = _rsum8(err * err)
            acc_loss[...] += sum(e2[:, k * 128:(k + 1) * 128] for k in range(D // 128))
            dy = err * (1.0 / D)
            acc_g2[...] += _rsum8(dy * xhat2)
            acc_b2[...] += _rsum8(dy)
            dr2_ref[r, :] = _ln_bwd(dy * g2_ref[...], xhat2, rstd2)

        _loop(tm // ROWS, tail)
        loss_ref[...] = acc_loss[...]

        @pl.when(i == nt - 1)
        def _():
            dg = jnp.sum(acc_g2[...], axis=0, keepdims=True)
            db = jnp.sum(acc_b2[...], axis=0, keepdims=True)
            for k in range(D // 128):
                sln2_ref[k:k + 1, :] = dg[:, k * 128:(k + 1) * 128]
                sln2_ref[8 + k:9 + k, :] = db[:, k * 128:(k + 1) * 128]

    row = pl.BlockSpec((tm, D), lambda i: (i, 0))
    return pl.pallas_call(
        body, name="ffn_forward", grid=(nt,),
        in_specs=[row, _full(ln1_g.shape), _full(ln1_b.shape), _resident(wup_g.shape),
                  _full(cfw.shape), _full(cfb.shape), _resident(wdown.shape),
                  _full(ln2_g.shape), _full(ln2_b.shape), row],
        out_specs=[pl.BlockSpec((N_DEV, tm, W_UP_BLK), lambda i: (0, i, 0)), row,
                   _full((8, 128)), _full((16, 128))],
        out_shape=[jax.ShapeDtypeStruct((N_DEV, t, W_UP_BLK), F32), jax.ShapeDtypeStruct((t, D), F32),
                   jax.ShapeDtypeStruct((8, 128), F32), jax.ShapeDtypeStruct((16, 128), F32)],
        scratch_shapes=[pltpu.VMEM((tm, D), F32), pltpu.VMEM((tm, D), BF16),
                        pltpu.VMEM((N_DEV, HALO_F, W_UP_BLK), F32), pltpu.VMEM((tm, W_UP_BLK), BF16),
                        pltpu.VMEM((tm, D), F32), pltpu.VMEM((8, 128), F32),
                        pltpu.VMEM((8, D), F32), pltpu.VMEM((8, D), F32)],
        compiler_params=_params(("arbitrary",)),
    )(xhat1, ln1_g, ln1_b, wup_g, cfw, cfb, wdown, ln2_g, ln2_b, target)


def ffn_backward(dr2, xhat1, ln1_g, ln1_b, hu, wup_g, cfw, cfb, wdown, tm):
    t = dr2.shape[0]
    nt = t // tm
    sub_rows = tm
    hu4 = hu.reshape(2, N_F, t, W_UP_BLK)
    wup4 = wup_g.reshape(2, N_F, D, W_UP_BLK)
    cfw4 = cfw.reshape(2, N_F, KF, W_UP_BLK)
    cfb4 = cfb.reshape(2, N_F, 1, W_UP_BLK)
    halo_blocks = tm // HALO_F

    def body(dr2_ref, xh_ref, g1_ref, b1_ref, hu_ref, halo_ref, wup_ref, cfw_ref, cfb_ref, wdown_ref,
             dwup_ref, dwdown_ref, dcfw_ref, dcfb_ref, dx1_ref,
             x1b_ref, drb_ref, dg_ref, dextg_ref, dextv_ref, gbuf_ref,
             dhug_ref, dhuv_ref, acc_wup, acc_wdown, acc_cfw, acc_cfb, sem):
        f = pl.program_id(0)
        i = pl.program_id(1)
        first_tile = i == nt - 1

        @pl.when(i == 0)
        def _():
            acc_wup[...] = jnp.zeros(acc_wup.shape, F32)
            acc_wdown[...] = jnp.zeros(acc_wdown.shape, F32)
            acc_cfw[...] = jnp.zeros(acc_cfw.shape, F32)
            acc_cfb[...] = jnp.zeros(acc_cfb.shape, F32)
            dextg_ref[tm:tm + HALO_F, :] = jnp.zeros((HALO_F, W_UP_BLK), F32)
            dextv_ref[tm:tm + HALO_F, :] = jnp.zeros((HALO_F, W_UP_BLK), F32)

        keep = jnp.where(first_tile, 0.0, 1.0)
        halo = [halo_ref[0, 0] * keep, halo_ref[1, 0] * keep]
        w = [cfw_ref[0, 0], cfw_ref[1, 0]]
        b = [cfb_ref[0, 0], cfb_ref[1, 0]]
        dext = [dextg_ref, dextv_ref]
        dhu = [dhug_ref, dhuv_ref]

        def taps(g, base):
            if base == 0:
                win = jnp.concatenate([halo[g], hu_ref[g, 0, 0:ROWS, :]], axis=0)
            else:
                win = hu_ref[g, 0, base - HALO_F:base + ROWS, :]
            return _taps_f(win)

        def rows1(bi):
            r = _rows(bi)
            tg, tv = taps(0, bi * ROWS), taps(1, bi * ROWS)
            gate = sum(tg[k] * w[0][k:k + 1, :] for k in range(KF)) + b[0]
            val = sum(tv[k] * w[1][k:k + 1, :] for k in range(KF)) + b[1]
            sg = _sigmoid(gate)
            silu = gate * sg
            gbuf_ref[r, :] = (silu * val).astype(BF16)
            dg = dg_ref[r, :]
            dgate = dg * val * (sg * (1.0 + gate * (1.0 - sg)))
            dval = dg * silu
            dextg_ref[r, :] = dgate
            dextv_ref[r, :] = dval
            acc_cfb[0:8, :] += _rsum8(dgate)
            acc_cfb[8:16, :] += _rsum8(dval)
            for k in range(KF):
                acc_cfw[8 * k:8 * k + 8, :] += _rsum8(dgate * tg[k])
                acc_cfw[8 * (KF + k):8 * (KF + k) + 8, :] += _rsum8(dval * tv[k])

        def rows2(bi):
            r = _rows(bi)
            for g in range(2):
                win = dext[g][pl.ds(bi * ROWS, ROWS + HALO_F), :]
                n = ROWS + HALO_F
                later = [pltpu.roll(win, n - 2, 0)[0:ROWS, :], pltpu.roll(win, n - 1, 0)[0:ROWS, :],
                         win[0:ROWS, :]]
                d = sum(later[k] * w[g][k:k + 1, :] for k in range(KF))
                dhu[g][r, :] = d.astype(BF16)

        for sub in reversed(range(tm // sub_rows)):
            rs = slice(sub * sub_rows, (sub + 1) * sub_rows)
            blocks = range(sub * sub_rows // ROWS, (sub + 1) * sub_rows // ROWS)
            x1b_ref[rs, :] = (xh_ref[rs, :] * g1_ref[...] + b1_ref[...]).astype(BF16)
            drb_ref[rs, :] = dr2_ref[rs, :].astype(BF16)
            dg_ref[rs, :] = _nt(drb_ref[rs, :], wdown_ref[0])
            for bi in blocks:
                rows1(bi)
            for bi in blocks:
                rows2(bi)
            acc_wdown[...] += _tn(gbuf_ref[rs, :], drb_ref[rs, :])
            x1t = x1b_ref[rs, :].T
            acc_wup[0] += _nn(x1t, dhug_ref[rs, :])
            acc_wup[1] += _nn(x1t, dhuv_ref[rs, :])
            dx1_ref[0, rs, :] = (_nt(dhug_ref[rs, :], wup_ref[0, 0])
                                 + _nt(dhuv_ref[rs, :], wup_ref[1, 0])).astype(BF16)
        dextg_ref[tm:tm + HALO_F, :] = dextg_ref[0:HALO_F, :]
        dextv_ref[tm:tm + HALO_F, :] = dextv_ref[0:HALO_F, :]

        @pl.when(i == nt - 1)
        def _():
            for g in range(2):
                dcfb_ref[g, 0] = jnp.sum(acc_cfb[8 * g:8 * g + 8, :], axis=0, keepdims=True)
                for k in range(KF):
                    r0 = 8 * (g * KF + k)
                    dcfw_ref[g, 0, k:k + 1, :] = jnp.sum(acc_cfw[r0:r0 + 8, :], axis=0, keepdims=True)
            cps = [pltpu.make_async_copy(acc_wup.at[0], dwup_ref.at[0, f], sem.at[0]),
                   pltpu.make_async_copy(acc_wup.at[1], dwup_ref.at[1, f], sem.at[1]),
                   pltpu.make_async_copy(acc_wdown, dwdown_ref.at[f], sem.at[2])]
            for cp in cps:
                cp.start()
            for cp in cps:
                cp.wait()

    rev = lambda i: nt - 1 - i
    row = pl.BlockSpec((tm, D), lambda f, i: (rev(i), 0))
    pair = lambda r, c: pl.BlockSpec((2, 1, r, c), lambda f, i: (0, f, 0, 0))
    return pl.pallas_call(
        body, name="ffn_backward", grid=(N_F, nt),
        in_specs=[row, row, _full(ln1_g.shape), _full(ln1_b.shape),
                  pl.BlockSpec((2, 1, tm, W_UP_BLK), lambda f, i: (0, f, rev(i), 0)),
                  pl.BlockSpec((2, 1, HALO_F, W_UP_BLK),
                               lambda f, i: (0, f, jnp.maximum(rev(i) * halo_blocks - 1, 0), 0)),
                  pair(D, W_UP_BLK), pair(KF, W_UP_BLK), pair(1, W_UP_BLK),
                  pl.BlockSpec((1, W_UP_BLK, D), lambda f, i: (f, 0, 0))],
        out_specs=[ANY, ANY, pair(KF, W_UP_BLK), pair(1, W_UP_BLK),
                   pl.BlockSpec((1, tm, D), lambda f, i: (f, rev(i), 0))],
        out_shape=[jax.ShapeDtypeStruct((2, N_F, D, W_UP_BLK), F32),
                   jax.ShapeDtypeStruct((N_F, W_UP_BLK, D), F32),
                   jax.ShapeDtypeStruct((2, N_F, KF, W_UP_BLK), F32),
                   jax.ShapeDtypeStruct((2, N_F, 1, W_UP_BLK), F32),
                   jax.ShapeDtypeStruct((N_F, t, D), BF16)],
        scratch_shapes=[pltpu.VMEM((tm, D), BF16), pltpu.VMEM((tm, D), BF16),
                        pltpu.VMEM((tm, W_UP_BLK), F32),
                        pltpu.VMEM((tm + HALO_F, W_UP_BLK), F32), pltpu.VMEM((tm + HALO_F, W_UP_BLK), F32),
                        pltpu.VMEM((tm, W_UP_BLK), BF16), pltpu.VMEM((tm, W_UP_BLK), BF16),
                        pltpu.VMEM((tm, W_UP_BLK), BF16),
                        pltpu.VMEM((2, D, W_UP_BLK), F32), pltpu.VMEM((W_UP_BLK, D), F32),
                        pltpu.VMEM((2 * KF * 8, W_UP_BLK), F32), pltpu.VMEM((16, W_UP_BLK), F32),
                        pltpu.SemaphoreType.DMA((3,))],
        compiler_params=_params(("arbitrary", "arbitrary")),
    )(dr2, xhat1, ln1_g, ln1_b, hu4, hu4, wup4, cfw4, cfb4, wdown)


def mix_backward(x, h, dx1p, dr2, xhat1, rstd1, win_g, ln_a_g, ln_a_b, w_spatial, bst,
                 conv_b_w, conv_b_b, ln_b_g, ln_b_b, wout, ln1_g, ffn_partials, tm):
    t = x.shape[0]
    n_p = len(ffn_partials)
    nt = t // tm
    n_chunks = tm // CHUNK
    halo_blocks = tm // HALO_B

    def body(x_ref, h_ref, halo_ref, dx1p_ref, dr2_ref, xh1_ref, rstd1_ref, win_ref, ga_ref, ba_ref,
             ws_ref, bst_ref, cw_ref, cb_ref, gb_ref, bb_ref, wout_ref, g1_ref, *rest):
        p_refs, rest = rest[:n_p], rest[n_p:]
        gx_ref, dwin_ref, dwout_ref, dcw_ref, small_ref = rest[:5]
        land_refs, rest = rest[5:5 + n_p], rest[5 + n_p:]
        (ext_ref, dext_ref, y_ref, dy_ref, dh_ref, dmb_ref, wsm_ref,
         acc_win, acc_wout, acc_bin, acc_lnag, acc_lnab, acc_ws, acc_bs, acc_cbb, acc_lnbg,
         acc_lnbb, acc_bout, acc_ln1g, acc_ln1b, acc_cw, sem, send_sems, recv_sems) = rest
        i = pl.program_id(0)

        @pl.when(i == 0)
        def _():
            for cp in _chip_copies(p_refs, land_refs, send_sems, recv_sems):
                cp.start()

        first_tile = i == nt - 1
        accs = [acc_win, acc_wout, acc_bin, acc_lnag, acc_lnab, acc_ws, acc_bs, acc_cbb, acc_lnbg,
                acc_lnbb, acc_bout, acc_ln1g, acc_ln1b, acc_cw]

        @pl.when(i == 0)
        def _():
            for acc in accs:
                acc[...] = jnp.zeros(acc.shape, F32)
            dext_ref[tm:tm + HALO_B, :] = jnp.zeros((HALO_B, D_B), F32)
            mask = _tril_mask()
            for hd in range(HEADS):
                wsm_ref[hd] = jnp.where(mask, ws_ref[hd], 0.0).astype(BF16)

        def ln1_rows(bi):
            r = _rows(bi)
            part = [dx1p_ref[f, r, :].astype(F32) for f in range(N_F)]
            dx1 = ALPHA * dr2_ref[r, :] + ((part[0] + part[1]) + (part[2] + part[3]))
            xhat = xh1_ref[r, :]
            acc_ln1g[...] += _rsum8(dx1 * xhat)
            acc_ln1b[...] += _rsum8(dx1)
            dr1 = _ln_bwd(dx1 * g1_ref[...], xhat, rstd1_ref[r, 0:1])
            acc_bout[...] += _rsum8(dr1)
            gx_ref[r, :] = ALPHA * dr1
            dmb_ref[r, :] = dr1.astype(BF16)

        _loop(tm // ROWS, ln1_rows)
        dy_ref[...] = _nt(dmb_ref[...], wout_ref[...])

        ha = halo_ref[:, 0:D_B]
        hg = halo_ref[:, D_B:2 * D_B]
        ext_ref[0:HALO_B, :] = jnp.where(first_tile, 0.0, 1.0) * (ha * _sigmoid(hg))

        def chunk(ci):
            r = _rows(ci, CHUNK)
            hu, hv = h_ref[r, 0:D_A], h_ref[r, D_A:2 * D_A]
            u, cdf_u, cdf_v, xhats, rstds, vns, svs = _mixer_a_fwd(hu, hv, ga_ref, ba_ref, wsm_ref, bst_ref)
            for hd in range(HEADS):
                sl = slice(hd * HEAD_DIM, (hd + 1) * HEAD_DIM)
                rows8 = slice(8 * hd, 8 * hd + 8)
                dy_a = dy_ref[r, sl]
                y_ref[r, sl] = (u[:, sl] * svs[hd]).astype(BF16)
                du = dy_a * svs[hd]
                dsv = dy_a * u[:, sl]
                dsvb = dsv.astype(BF16)
                acc_bs[hd] += dsv
                acc_ws[hd] += _nt(dsvb, vns[hd])
                dvn = _tn(wsm_ref[hd], dsvb)
                acc_lnag[rows8, :] += _rsum8(dvn * xhats[hd])
                acc_lnab[rows8, :] += _rsum8(dvn)
                dv = _ln_bwd(dvn * ga_ref[hd:hd + 1, :], xhats[hd], rstds[hd])
                hus, hvs = hu[:, sl], hv[:, sl]
                slv = slice(D_A + hd * HEAD_DIM, D_A + (hd + 1) * HEAD_DIM)
                dhu = du * (cdf_u[:, sl] + hus * jnp.exp(-0.5 * hus * hus) * INV_SQRT_2PI)
                dhv = dv * (cdf_v[:, sl] + hvs * jnp.exp(-0.5 * hvs * hvs) * INV_SQRT_2PI)
                acc_bin[:, sl] += _rsum8(dhu)
                acc_bin[:, slv] += _rsum8(dhv)
                dh_ref[r, sl] = dhu.astype(BF16)
                dh_ref[r, slv] = dhv.astype(BF16)
            a_b = h_ref[r, 2 * D_A:2 * D_A + D_B]
            g_b = h_ref[r, 2 * D_A + D_B:D_IN]
            ext_ref[pl.ds(HALO_B + ci * CHUNK, CHUNK), :] = a_b * _sigmoid(g_b)

        _loop(n_chunks, chunk)

        def conv_rows(bi):
            base = bi * ROWS
            r = pl.ds(base, ROWS)
            acc, win = _conv_b_block(ext_ref, base, cw_ref)
            xhat, rstd = _ln_stats(acc + cb_ref[...])
            yb2 = xhat * gb_ref[...] + bb_ref[...]
            sg = _sigmoid(yb2)
            y_ref[r, D_A:D] = (yb2 * sg).astype(BF16)
            dyb2 = dy_ref[r, D_A:D] * (sg * (1.0 + yb2 * (1.0 - sg)))
            acc_lnbg[...] += _rsum8(dyb2 * xhat)
            acc_lnbb[...] += _rsum8(dyb2)
            dyb1 = _ln_bwd(dyb2 * gb_ref[...], xhat, rstd)
            acc_cbb[...] += _rsum8(dyb1)
            dext_ref[r, :] = dyb1
            for k in range(KB):
                acc_cw[8 * k:8 * k + 8, :] += _rsum8(dyb1 * _tap(win, 2 + k))

        _loop(tm // ROWS, conv_rows)

        def convt_rows(bi):
            base = bi * ROWS
            r = pl.ds(base, ROWS)
            dwin = _shifted(dext_ref[pl.ds(base, ROWS + HALO_B), :])
            dyb0 = jnp.zeros((ROWS, D_B), F32)
            for k in range(KB):
                dyb0 = dyb0 + _tap(dwin, 30 - k) * cw_ref[k:k + 1, :]
            a_b = h_ref[r, 2 * D_A:2 * D_A + D_B]
            sg = _sigmoid(h_ref[r, 2 * D_A + D_B:D_IN])
            da_b = dyb0 * sg
            dg_b = dyb0 * a_b * sg * (1.0 - sg)
            acc_bin[:, 2 * D_A:2 * D_A + D_B] += _rsum8(da_b)
            acc_bin[:, 2 * D_A + D_B:D_IN] += _rsum8(dg_b)
            dh_ref[r, 2 * D_A:2 * D_A + D_B] = da_b.astype(BF16)
            dh_ref[r, 2 * D_A + D_B:D_IN] = dg_b.astype(BF16)

        _loop(tm // ROWS, convt_rows)
        dext_ref[tm:tm + HALO_B, :] = dext_ref[0:HALO_B, :]

        acc_wout[...] += _tn(y_ref[...], dmb_ref[...])
        xt = x_ref[...].T.astype(BF16)
        dh_blocks = [dh_ref[:, j * W_IN_BLK:(j + 1) * W_IN_BLK] for j in range(N_DEV)]
        for j in range(N_DEV):
            acc_win[j] += _nn(xt, dh_blocks[j])
        gx_ref[...] += sum(_nt(dh_blocks[j], win_ref[j]) for j in range(N_DEV))

        @pl.when(i == nt - 1)
        def _():
            cps = [pltpu.make_async_copy(acc_win, dwin_ref, sem.at[0]),
                   pltpu.make_async_copy(acc_wout, dwout_ref, sem.at[1])]
            for cp in cps:
                cp.start()
            small_ref[...] = jnp.zeros(small_ref.shape, F32)

            def put_row_vector(row0, acc):
                vec = jnp.sum(acc[...], axis=0, keepdims=True)
                for k in range(vec.shape[1] // 128):
                    small_ref[row0 + k:row0 + k + 1, :] = vec[:, k * 128:(k + 1) * 128]

            put_row_vector(S_BIN, acc_bin)
            put_row_vector(S_CBB, acc_cbb)
            put_row_vector(S_LNBG, acc_lnbg)
            put_row_vector(S_LNBB, acc_lnbb)
            put_row_vector(S_BOUT, acc_bout)
            put_row_vector(S_LN1G, acc_ln1g)
            put_row_vector(S_LN1B, acc_ln1b)
            mask = _tril_mask()
            for hd in range(HEADS):
                rows8 = slice(8 * hd, 8 * hd + 8)
                small_ref[S_LNAG + hd:S_LNAG + hd + 1, :] = jnp.sum(acc_lnag[rows8, :], axis=0, keepdims=True)
                small_ref[S_LNAB + hd:S_LNAB + hd + 1, :] = jnp.sum(acc_lnab[rows8, :], axis=0, keepdims=True)
                small_ref[S_WS + hd * CHUNK:S_WS + (hd + 1) * CHUNK, :] = jnp.where(mask, acc_ws[hd], 0.0)
                small_ref[S_BS + hd:S_BS + hd + 1, :] = jnp.sum(acc_bs[hd].T, axis=0, keepdims=True)
            for k in range(KB):
                dcw_ref[k:k + 1, :] = jnp.sum(acc_cw[8 * k:8 * k + 8, :], axis=0, keepdims=True)
            for cp in cps:
                cp.wait()
            for cp in _chip_copies(p_refs, land_refs, send_sems, recv_sems):
                cp.wait()

    rev = lambda i: nt - 1 - i
    row = lambda w: pl.BlockSpec((tm, w), lambda i: (rev(i), 0))
    return pl.pallas_call(
        body, name="mix_backward", grid=(nt,),
        in_specs=[row(D), row(D_IN),
                  pl.BlockSpec((HALO_B, 2 * D_B), lambda i: (jnp.maximum(rev(i) * halo_blocks - 1, 0), 1)),
                  pl.BlockSpec((N_F, tm, D), lambda i: (0, rev(i), 0)),
                  row(D), row(D), row(128), _resident(win_g.shape), _full(ln_a_g.shape),
                  _full(ln_a_b.shape), _full(w_spatial.shape), _full(bst.shape), _full(conv_b_w.shape),
                  _full(conv_b_b.shape), _full(ln_b_g.shape), _full(ln_b_b.shape),
                  _resident(wout.shape), _full(ln1_g.shape)] + [ANY] * n_p,
        out_specs=[row(D), ANY, ANY, _full((KB, D_B)), _full((S_MIX_ROWS, 128))] + [ANY] * n_p,
        out_shape=[jax.ShapeDtypeStruct((t, D), F32), jax.ShapeDtypeStruct((N_DEV, D, W_IN_BLK), F32),
                   jax.ShapeDtypeStruct((D, D), F32), jax.ShapeDtypeStruct((KB, D_B), F32),
                   jax.ShapeDtypeStruct((S_MIX_ROWS, 128), F32)]
        + [jax.ShapeDtypeStruct(p.shape, BF16) for p in ffn_partials],
        scratch_shapes=[pltpu.VMEM((tm + HALO_B, D_B), F32), pltpu.VMEM((tm + HALO_B, D_B), F32),
                        pltpu.VMEM((tm, D), BF16), pltpu.VMEM((tm, D), F32), pltpu.VMEM((tm, D_IN), BF16),
                        pltpu.VMEM((tm, D), BF16),
                        pltpu.VMEM((HEADS, CHUNK, CHUNK), BF16),
                        pltpu.VMEM((N_DEV, D, W_IN_BLK), F32), pltpu.VMEM((D, D), F32),
                        pltpu.VMEM((8, D_IN), F32), pltpu.VMEM((8 * HEADS, HEAD_DIM), F32),
                        pltpu.VMEM((8 * HEADS, HEAD_DIM), F32), pltpu.VMEM((HEADS, CHUNK, CHUNK), F32),
                        pltpu.VMEM((HEADS, CHUNK, CHUNK), F32), pltpu.VMEM((8, D_B), F32),
                        pltpu.VMEM((8, D_B), F32), pltpu.VMEM((8, D_B), F32), pltpu.VMEM((8, D), F32),
                        pltpu.VMEM((8, D), F32), pltpu.VMEM((8, D), F32), pltpu.VMEM((8 * KB, D_B), F32),
                        pltpu.SemaphoreType.DMA((2,)),
                        pltpu.SemaphoreType.DMA((n_p, 3)), pltpu.SemaphoreType.DMA((n_p, 3))],
        compiler_params=_params(("arbitrary",)),
    )(x, h, h, dx1p, dr2, xhat1, rstd1, win_g, ln_a_g, ln_a_b, w_spatial, bst, conv_b_w, conv_b_b,
      ln_b_g, ln_b_b, wout, ln1_g, *ffn_partials)


def _rows128(a):
    return a.reshape(-1, 128)


def _pack_conv(cb, cf):
    out = jnp.zeros((40, 768), F32)
    out = out.at[0:KB, 0:64].set(cb)
    return out.at[32:32 + KF, 0:W_UP_BLK].set(cf)


def kernel(x, w_in, b_in, ln_a_g, ln_a_b, w_spatial, b_spatial, conv_b_w, conv_b_b, ln_b_g, ln_b_b, w_out, b_out, ln1_g, ln1_b, w_up, conv_f_w, conv_f_b, w_down, ln2_g, ln2_b, loss_target, m_w_in, m_b_in, m_ln_a_g, m_ln_a_b, m_w_spatial, m_b_spatial, m_conv_b_w, m_conv_b_b, m_ln_b_g, m_ln_b_b, m_w_out, m_b_out, m_ln1_g, m_ln1_b, m_w_up, m_conv_f_w, m_conv_f_b, m_w_down, m_ln2_g, m_ln2_b, v_w_in, v_b_in, v_ln_a_g, v_ln_a_b, v_w_spatial, v_b_spatial, v_conv_b_w, v_conv_b_b, v_ln_b_g, v_ln_b_b, v_w_out, v_b_out, v_ln1_g, v_ln1_b, v_w_up, v_conv_f_w, v_conv_f_b, v_w_down, v_ln2_g, v_ln2_b):
    t = x.shape[1]
    x2 = x.reshape(t, D)
    target = loss_target.reshape(t, D)
    tm_fwd = min(t, 512)
    tm_bwd = min(t, 256)
    tm_ffn_bwd = min(t, 512)

    xi, yi, ci = _mesh_pos()
    jidx = jnp.stack([_lid(px, py, ci) for px, py in _chip_patterns(xi, yi)]).astype(jnp.int32)

    win_g, wout_g, conv_g, sup, sdown = all_gather_mixer_weights(
        w_in, w_out, w_up, w_down, _pack_conv(conv_b_w, conv_f_w))
    wout_full = wout_g.reshape(D, D)
    conv_b_full = conv_g[:, 0:KB, 0:64].transpose(1, 0, 2).reshape(KB, D_B)
    cfw = conv_g[:, 32:32 + KF, 0:W_UP_BLK]
    cfb = conv_f_b.reshape(N_DEV, W_UP_BLK)
    row = lambda a: a.reshape(1, -1)
    bst = b_spatial.T

    h, xhat1, rstd1, wup_g, wdown_g = mix_forward(
        x2, win_g, row(b_in), ln_a_g, ln_a_b, w_spatial, bst, conv_b_full, row(conv_b_b),
        row(ln_b_g), row(ln_b_b), wout_full, row(b_out), row(ln1_g), row(ln1_b), sup, sdown, tm_fwd)
    wdown4 = wdown_g.reshape(N_F, W_UP_BLK, D)
    hu, dr2, loss_part, s_ln2 = ffn_forward(
        xhat1, row(ln1_g), row(ln1_b), wup_g, cfw, cfb, wdown4, row(ln2_g), row(ln2_b), target, tm_bwd)
    loss = lax.psum(jnp.sum(loss_part), ("x", "y", "c")) * (0.5 / D)

    dwup, dwdown, dcfw, dcfb, dx1p = ffn_backward(
        dr2, xhat1, row(ln1_g), row(ln1_b), hu, wup_g, cfw, cfb, wdown4, tm_ffn_bwd)
    ffn_grads = [dwup.reshape(N_DEV, D, W_UP_BLK), dwdown.reshape(N_DEV, D_FF // N_DEV, D)]
    ffn_lands = exchange_with_sibling("exchange_with_sibling_ffn", ffn_grads)
    ffn_partials = [chip_partials("chip_partials_" + nm, g, l, jidx, rb)
                    for nm, g, l, rb in zip(["w_up", "w_down"], ffn_grads, ffn_lands, [256, 352])]
    grad_x, dwin, dwout, dcw, s_mix, *ffn_recvs = mix_backward(
        x2, h, dx1p, dr2, xhat1, rstd1, win_g, ln_a_g, ln_a_b, w_spatial, bst,
        conv_b_full, row(conv_b_b), row(ln_b_g), row(ln_b_b), wout_full, row(ln1_g), ffn_partials, tm_bwd)

    dcfb_rows = jnp.pad(dcfb.reshape(-1, 128), ((0, 4), (0, 0)))
    svec = jnp.concatenate([s_mix, dcfb_rows, s_ln2], axis=0)
    dconv = jnp.zeros((N_DEV, 40, 768), F32)
    dconv = dconv.at[:, 0:KB, 0:64].set(dcw.reshape(KB, N_DEV, 64).transpose(1, 0, 2))
    dconv = dconv.at[:, 32:32 + KF, 0:W_UP_BLK].set(dcfw.reshape(N_DEV, KF, W_UP_BLK))
    mix_grads = [dwin, dwout.reshape(N_DEV, D // N_DEV, D), dconv]
    *mix_lands, sv_land = exchange_with_sibling("exchange_with_sibling_mixer", mix_grads, svec)
    mix_partials = [chip_partials("chip_partials_" + nm, g, l, jidx, rb)
                    for nm, g, l, rb in zip(["w_in", "w_out", "conv"], mix_grads, mix_lands, [512, 128, 40])]
    *mix_recvs, sv_slots = exchange_between_chips(mix_partials, svec, sv_land)
    names = ["w_in", "w_out", "conv", "w_up", "w_down"]
    grads = mix_grads + ffn_grads
    lands = mix_lands + list(ffn_lands)
    recvs = mix_recvs + ffn_recvs
    row_blocks = [512, 128, 40, 256, 352]

    shard_w = [w_in, w_out, _pack_conv(conv_b_w, conv_f_w), w_up, w_down]
    shard_m = [m_w_in, m_w_out, _pack_conv(m_conv_b_w, m_conv_f_w), m_w_up, m_w_down]
    shard_v = [v_w_in, v_w_out, _pack_conv(v_conv_b_w, v_conv_f_w), v_w_up, v_w_down]
    big = {}
    for nm, g, l, r, w, m, v, rb in zip(names, grads, lands, recvs, shard_w, shard_m, shard_v, row_blocks):
        big[nm] = reduce_and_adamw("reduce_adamw_" + nm, g, l, r, w, m, v, jidx, rb)
    for k in range(4):
        packed = big["conv"][k]
        big.setdefault("conv_b_w", []).append(packed[0:KB, 0:64])
        big.setdefault("conv_f_w", []).append(packed[32:32 + KF, 0:W_UP_BLK])

    small_w = dict(b_in=b_in, ln_a_g=ln_a_g, ln_a_b=ln_a_b, w_spatial=w_spatial, b_spatial=b_spatial,
                   conv_b_b=conv_b_b, ln_b_g=ln_b_g, ln_b_b=ln_b_b, b_out=b_out, ln1_g=ln1_g,
                   ln1_b=ln1_b, conv_f_b=conv_f_b, ln2_g=ln2_g, ln2_b=ln2_b)
    small_m = dict(b_in=m_b_in, ln_a_g=m_ln_a_g, ln_a_b=m_ln_a_b, w_spatial=m_w_spatial,
                   b_spatial=m_b_spatial, conv_b_b=m_conv_b_b, ln_b_g=m_ln_b_g, ln_b_b=m_ln_b_b,
                   b_out=m_b_out, ln1_g=m_ln1_g, ln1_b=m_ln1_b, conv_f_b=m_conv_f_b, ln2_g=m_ln2_g,
                   ln2_b=m_ln2_b)
    small_v = dict(b_in=v_b_in, ln_a_g=v_ln_a_g, ln_a_b=v_ln_a_b, w_spatial=v_w_spatial,
                   b_spatial=v_b_spatial, conv_b_b=v_conv_b_b, ln_b_g=v_ln_b_g, ln_b_b=v_ln_b_b,
                   b_out=v_b_out, ln1_g=v_ln1_g, ln1_b=v_ln1_b, conv_f_b=v_conv_f_b, ln2_g=v_ln2_g,
                   ln2_b=v_ln2_b)
    order = [nm for nm, _, _ in SMALL_LAYOUT]
    small_out = small_adamw(sv_slots, [_rows128(small_w[nm]) for nm in order],
                            [_rows128(small_m[nm]) for nm in order], [_rows128(small_v[nm]) for nm in order])
    n_small = len(order)
    small = {nm: [small_out[k * n_small + p].reshape(small_w[nm].shape) for k in range(4)]
             for p, nm in enumerate(order)}

    weights = ["w_in", "b_in", "ln_a_g", "ln_a_b", "w_spatial", "b_spatial", "conv_b_w", "conv_b_b",
               "ln_b_g", "ln_b_b", "w_out", "b_out", "ln1_g", "ln1_b", "w_up", "conv_f_w", "conv_f_b",
               "w_down", "ln2_g", "ln2_b"]
    result = lambda nm, k: big[nm][k] if nm in big else small[nm][k]
    return (loss, grad_x.reshape(x.shape), *[result(nm, 0) for nm in weights],
            *[result(nm, 1) for nm in weights], *[result(nm, 2) for nm in weights],
            *[result(nm, 3) for nm in weights])
```

```python
import functools
import math

import jax
import jax.numpy as jnp
from jax import lax
from jax.experimental import pallas as pl
from jax.experimental.pallas import tpu as pltpu

F32 = jnp.float32
BF16 = jnp.bfloat16

D = 1024
D_A = 512
D_B = 512
HEADS = 4
HEAD_DIM = 128
CHUNK = 128
KB = 31
KF = 3
D_FF = 2816
D_IN = 2048
N_DEV = 8
W_IN_BLK = D_IN // N_DEV
W_UP_BLK = 2 * D_FF // N_DEV
N_F = 4
LN_EPS = 1e-5
ALPHA = 2.0 ** 0.25

ADAM_LR = 0.001
ADAM_B1 = 0.9
ADAM_B2 = 0.999
ADAM_EPS = 1e-08
ADAM_WD = 0.01
ADAM_STEP = 10

INV_SQRT2 = 1.0 / math.sqrt(2.0)
INV_SQRT_2PI = 1.0 / math.sqrt(2.0 * math.pi)

HALO_B = 32
HALO_F = 8
ROWS = 64
VMEM_LIMIT = 58 * 1024 * 1024

MESH = pl.DeviceIdType.MESH
ANY = pl.BlockSpec(memory_space=pl.ANY)
VMEM = pl.BlockSpec(memory_space=pltpu.VMEM)

S_BIN, S_LNAG, S_LNAB, S_WS, S_BS, S_CBB, S_LNBG, S_LNBB, S_BOUT, S_LN1G, S_LN1B = (
    0, 16, 24, 32, 544, 552, 560, 568, 576, 584, 592)
S_MIX_ROWS = 600
S_CFB = 600
S_LN2G = 648
S_LN2B = 656
S_LOSS = 664
S_ROWS = 672


def _tn(a, b):
    return lax.dot_general(a, b, (((0,), (0,)), ((), ())), preferred_element_type=F32)


def _nt(a, b):
    return lax.dot_general(a, b, (((1,), (1,)), ((), ())), preferred_element_type=F32)


def _nn(a, b):
    return jnp.dot(a, b, preferred_element_type=F32)


def _sigmoid(x):
    return 1.0 / (1.0 + jnp.exp(-x))


def _ln_stats(x):
    mu = jnp.mean(x, axis=-1, keepdims=True)
    xc = x - mu
    var = jnp.mean(xc * xc, axis=-1, keepdims=True)
    rstd = lax.rsqrt(var + LN_EPS)
    return xc * rstd, rstd


def _ln_bwd(dxhat, xhat, rstd):
    m1 = jnp.mean(dxhat, axis=-1, keepdims=True)
    m2 = jnp.mean(dxhat * xhat, axis=-1, keepdims=True)
    return rstd * (dxhat - m1 - xhat * m2)


def _rsum8(x):
    r, n = x.shape
    return x.reshape(r // 8, 8, n).sum(axis=0)


def _rows(i, n=ROWS):
    return pl.ds(i * n, n)


def _loop(n, body):
    for i in range(n):
        body(i)


def _tril_mask():
    r = lax.broadcasted_iota(jnp.int32, (CHUNK, CHUNK), 0)
    c = lax.broadcasted_iota(jnp.int32, (CHUNK, CHUNK), 1)
    return c <= r


def _mixer_a_fwd(hu, hv, ga_ref, ba_ref, wsm_ref, bst_ref):
    cdf_u = 0.5 * (1.0 + lax.erf(hu * INV_SQRT2))
    cdf_v = 0.5 * (1.0 + lax.erf(hv * INV_SQRT2))
    u = hu * cdf_u
    v = hv * cdf_v
    xhats, rstds, vns, svs = [], [], [], []
    for hd in range(HEADS):
        sl = slice(hd * HEAD_DIM, (hd + 1) * HEAD_DIM)
        xhat, rstd = _ln_stats(v[:, sl])
        vn = (xhat * ga_ref[hd:hd + 1, :] + ba_ref[hd:hd + 1, :]).astype(BF16)
        sv = _nn(wsm_ref[hd], vn) + bst_ref[:, hd:hd + 1]
        xhats.append(xhat)
        rstds.append(rstd)
        vns.append(vn)
        svs.append(sv)
    return u, cdf_u, cdf_v, xhats, rstds, vns, svs


def _shifted(win):
    n = win.shape[0]
    return [win] + [pltpu.roll(win, n - s, 0) for s in range(1, 8)]


def _tap(shifted, offset):
    s = offset % 8
    return shifted[s][offset - s:offset - s + ROWS, :]


def _conv_b_block(ext_ref, base, cw_ref):
    win = _shifted(ext_ref[pl.ds(base, ROWS + HALO_B), :])
    acc = jnp.zeros((ROWS, D_B), F32)
    for k in range(KB):
        acc = acc + _tap(win, 2 + k) * cw_ref[k:k + 1, :]
    return acc, win


def _taps_f(win):
    n = ROWS + HALO_F
    return [pltpu.roll(win, n - 6, 0)[0:ROWS, :], pltpu.roll(win, n - 7, 0)[0:ROWS, :], win[8:n, :]]


def _params(sem, **kw):
    return pltpu.CompilerParams(dimension_semantics=sem, vmem_limit_bytes=VMEM_LIMIT, **kw)


def _resident(shape):
    zeros = (0,) * len(shape)
    return pl.BlockSpec(shape, lambda *_: zeros, pipeline_mode=pl.Buffered(1))


def _full(shape):
    zeros = (0,) * len(shape)
    return pl.BlockSpec(shape, lambda *_: zeros)


def _mesh_pos():
    return lax.axis_index("x"), lax.axis_index("y"), lax.axis_index("c")


def _chip_patterns(x, y):
    return [(x, y), (1 - x, y), (x, 1 - y), (1 - x, 1 - y)]


def _lid(x, y, c):
    return 4 * x + 2 * y + c


def _gather_copy(outs, send_sems, recv_sems, a, k, block, to, src=None):
    blk = outs[a].at[_lid(*block)]
    return pltpu.make_async_remote_copy(
        src_ref=blk if src is None else src, dst_ref=blk,
        send_sem=send_sems.at[a, k], recv_sem=recv_sems.at[a, k], device_id=to, device_id_type=MESH)


def _gather_start(mine, outs, send_sems, recv_sems, local_sems):
    x, y, c = _mesh_pos()
    me, sib = (x, y, c), (x, y, 1 - c)
    for a in range(len(mine)):
        pltpu.make_async_copy(mine[a], outs[a].at[_lid(*me)], local_sems.at[a]).start()
        _gather_copy(outs, send_sems, recv_sems, a, 0, me, sib, src=mine[a]).start()
        for j, chip in enumerate(_chip_patterns(x, y)[1:]):
            _gather_copy(outs, send_sems, recv_sems, a, 1 + j, me, (*chip, c), src=mine[a]).start()


def _gather_finish(mine, outs, send_sems, recv_sems, local_sems):
    x, y, c = _mesh_pos()
    me, sib = (x, y, c), (x, y, 1 - c)
    chips = _chip_patterns(x, y)[1:]
    n = len(mine)
    copy = functools.partial(_gather_copy, outs, send_sems, recv_sems)
    passed = []
    for j, chip in enumerate(chips):
        for a in range(n):
            copy(a, 1 + j, (*chip, c), me).wait_recv()
            cp = copy(a, 4 + j, (*chip, c), sib)
            cp.start()
            passed.append(cp)
    for a in range(n):
        copy(a, 0, sib, me).wait_recv()
        for j, chip in enumerate(chips):
            copy(a, 4 + j, (*chip, 1 - c), me).wait_recv()
        for k in range(4):
            copy(a, k, me, sib, src=mine[a]).wait_send()
        pltpu.make_async_copy(mine[a], outs[a].at[_lid(*me)], local_sems.at[a]).wait()
    for cp in passed:
        cp.wait_send()


def _gather_scratch(n):
    return [pltpu.SemaphoreType.DMA((n, 7)), pltpu.SemaphoreType.DMA((n, 7)), pltpu.SemaphoreType.DMA((n,))]


def all_gather_mixer_weights(w_in, w_out, w_up, w_down, convp):
    srcs = [w_in, w_out, convp]
    n = len(srcs)

    def body(win_ref, wout_ref, convp_ref, wup_ref, wdown_ref,
             gin_ref, gout_ref, gconv_ref, sup_ref, sdown_ref,
             sin_ref, sout_ref, send_sems, recv_sems, local_sems):
        sin_ref[...] = win_ref[...].astype(BF16)
        sout_ref[...] = wout_ref[...].astype(BF16)
        mine = [sin_ref, sout_ref, convp_ref]
        outs = [gin_ref, gout_ref, gconv_ref]
        _gather_start(mine, outs, send_sems, recv_sems, local_sems)
        sup_ref[...] = wup_ref[...].astype(BF16)
        sdown_ref[...] = wdown_ref[...].astype(BF16)
        _gather_finish(mine, outs, send_sems, recv_sems, local_sems)

    return pl.pallas_call(
        body, name="all_gather_mixer_weights",
        out_shape=[jax.ShapeDtypeStruct((N_DEV,) + w_in.shape, BF16),
                   jax.ShapeDtypeStruct((N_DEV,) + w_out.shape, BF16),
                   jax.ShapeDtypeStruct((N_DEV,) + convp.shape, F32),
                   jax.ShapeDtypeStruct(w_up.shape, BF16), jax.ShapeDtypeStruct(w_down.shape, BF16)],
        in_specs=[VMEM] * 5, out_specs=[ANY] * n + [VMEM, VMEM],
        scratch_shapes=[pltpu.VMEM(w_in.shape, BF16), pltpu.VMEM(w_out.shape, BF16)] + _gather_scratch(n),
        compiler_params=pltpu.CompilerParams(vmem_limit_bytes=VMEM_LIMIT),
    )(w_in, w_out, convp, w_up, w_down)


def exchange_with_sibling(name, grads, svec=None):
    srcs = list(grads) + ([] if svec is None else [svec])
    n, n_all = len(grads), len(srcs)

    def body(*refs):
        src, land = refs[:n_all], refs[n_all:2 * n_all]
        send_sems, recv_sems = refs[2 * n_all:]
        x, y, c = _mesh_pos()
        sib = (x, y, 1 - c)
        copies = []
        for a in range(n):
            for k, (px, py) in enumerate(_chip_patterns(x, y)):
                copies.append(pltpu.make_async_remote_copy(
                    src_ref=src[a].at[_lid(px, py, 1 - c)], dst_ref=land[a].at[k],
                    send_sem=send_sems.at[a, k], recv_sem=recv_sems.at[a, k],
                    device_id=sib, device_id_type=MESH))
        if svec is not None:
            copies.append(pltpu.make_async_remote_copy(
                src_ref=src[n], dst_ref=land[n], send_sem=send_sems.at[n, 0], recv_sem=recv_sems.at[n, 0],
                device_id=sib, device_id_type=MESH))
        for cp in copies:
            cp.start()
        for cp in copies:
            cp.wait()

    return pl.pallas_call(
        body, name=name,
        out_shape=[jax.ShapeDtypeStruct((4,) + g.shape[1:], F32) for g in grads]
        + ([] if svec is None else [jax.ShapeDtypeStruct(svec.shape, F32)]),
        in_specs=[ANY] * n_all, out_specs=[ANY] * n_all,
        scratch_shapes=[pltpu.SemaphoreType.DMA((n_all, 4)), pltpu.SemaphoreType.DMA((n_all, 4))],
    )(*srcs)


def _chip_copies(p, land, send_sems, recv_sems):
    x, y, c = _mesh_pos()
    return [pltpu.make_async_remote_copy(
        src_ref=p[a].at[k], dst_ref=land[a].at[k], send_sem=send_sems.at[a, k], recv_sem=recv_sems.at[a, k],
        device_id=(px, py, c), device_id_type=MESH)
        for k, (px, py) in enumerate(_chip_patterns(x, y)[1:]) for a in range(len(p))]


def exchange_between_chips(partials, svec, sv_land):
    n = len(partials)

    def body(*refs):
        p = refs[:n]
        sv_ref, svl_ref = refs[n], refs[n + 1]
        land = refs[n + 2:2 * n + 2]
        sv_slots = refs[2 * n + 2]
        chip_sv, send_sems, recv_sems, sv_send, sv_recv, local_sem = refs[2 * n + 3:]
        x, y, c = _mesh_pos()
        q = 2 * x + y
        chip_sv[...] = sv_ref[...] + svl_ref[...]
        local = pltpu.make_async_copy(chip_sv, sv_slots.at[q], local_sem)
        local.start()
        copies = _chip_copies(p, land, send_sems, recv_sems)
        for cp in copies:
            cp.start()
        sv_copies = []
        for k, (px, py) in enumerate(_chip_patterns(x, y)[1:]):
            cp = pltpu.make_async_remote_copy(
                src_ref=chip_sv, dst_ref=sv_slots.at[q],
                send_sem=sv_send.at[k], recv_sem=sv_recv.at[k],
                device_id=(px, py, c), device_id_type=MESH)
            cp.start()
            sv_copies.append(cp)
        for cp in copies:
            cp.wait()
        for k, (px, py) in enumerate(_chip_patterns(x, y)[1:]):
            sv_copies[k].wait_send()
            pltpu.make_async_remote_copy(
                src_ref=chip_sv, dst_ref=sv_slots.at[2 * px + py],
                send_sem=sv_send.at[k], recv_sem=sv_recv.at[k],
                device_id=(px, py, c), device_id_type=MESH).wait_recv()
        local.wait()

    return pl.pallas_call(
        body, name="exchange_between_chips",
        out_shape=[jax.ShapeDtypeStruct(p.shape, BF16) for p in partials]
        + [jax.ShapeDtypeStruct((4,) + svec.shape, F32)],
        in_specs=[ANY] * n + [VMEM, VMEM], out_specs=[ANY] * (n + 1),
        scratch_shapes=[pltpu.VMEM(svec.shape, F32),
                        pltpu.SemaphoreType.DMA((n, 3)), pltpu.SemaphoreType.DMA((n, 3)),
                        pltpu.SemaphoreType.DMA((3,)), pltpu.SemaphoreType.DMA((3,)),
                        pltpu.SemaphoreType.DMA],
    )(*partials, svec, sv_land)


def chip_partials(name, g, land, jidx, rb):
    _, r, c = g.shape

    def body(j_ref, g_ref, l_ref, o_ref):
        o_ref[...] = (g_ref[...] + l_ref[...]).astype(BF16)

    return pl.pallas_call(
        body, name=name,
        out_shape=jax.ShapeDtypeStruct((3, r, c), BF16),
        grid_spec=pltpu.PrefetchScalarGridSpec(
            num_scalar_prefetch=1, grid=(3, r // rb),
            in_specs=[pl.BlockSpec((1, rb, c), lambda k, i, j: (j[1 + k], i, 0)),
                      pl.BlockSpec((1, rb, c), lambda k, i, j: (1 + k, i, 0))],
            out_specs=pl.BlockSpec((1, rb, c), lambda k, i, j: (k, i, 0))),
        compiler_params=_params(("arbitrary", "arbitrary")),
    )(jidx, g, land)


def _adamw(w, g, m, v):
    m2 = ADAM_B1 * m + (1.0 - ADAM_B1) * g
    v2 = ADAM_B2 * v + (1.0 - ADAM_B2) * (g * g)
    m_hat = m2 / (1.0 - ADAM_B1 ** ADAM_STEP)
    v_hat = v2 / (1.0 - ADAM_B2 ** ADAM_STEP)
    delta = -ADAM_LR * (m_hat / (jnp.sqrt(v_hat) + ADAM_EPS) + ADAM_WD * w)
    return delta, m2, v2


def reduce_and_adamw(name, g, land, recv, w, m, v, jidx, rb):
    _, r, c = g.shape

    def body(j_ref, g_ref, l_ref, r_ref, w_ref, m_ref, v_ref, go_ref, do_ref, mo_ref, vo_ref):
        grad = (g_ref[0] + l_ref[0]) + r_ref[0].astype(F32) + r_ref[1].astype(F32) + r_ref[2].astype(F32)
        delta, m2, v2 = _adamw(w_ref[...], grad, m_ref[...], v_ref[...])
        go_ref[...] = grad
        do_ref[...] = delta
        mo_ref[...] = m2
        vo_ref[...] = v2

    blk = pl.BlockSpec((rb, c), lambda i, j: (i, 0))
    return pl.pallas_call(
        body, name=name,
        out_shape=[jax.ShapeDtypeStruct((r, c), F32)] * 4,
        grid_spec=pltpu.PrefetchScalarGridSpec(
            num_scalar_prefetch=1, grid=(r // rb,),
            in_specs=[pl.BlockSpec((1, rb, c), lambda i, j: (j[0], i, 0)),
                      pl.BlockSpec((1, rb, c), lambda i, j: (0, i, 0)),
                      pl.BlockSpec((3, rb, c), lambda i, j: (0, i, 0)),
                      blk, blk, blk],
            out_specs=[blk] * 4),
        compiler_params=_params(("arbitrary",)),
    )(jidx, g, land, recv, w, m, v)


SMALL_LAYOUT = [
    ("b_in", S_BIN, 16), ("ln_a_g", S_LNAG, 4), ("ln_a_b", S_LNAB, 4), ("w_spatial", S_WS, 512),
    ("b_spatial", S_BS, 4), ("conv_b_b", S_CBB, 4), ("ln_b_g", S_LNBG, 4), ("ln_b_b", S_LNBB, 4),
    ("b_out", S_BOUT, 8), ("ln1_g", S_LN1G, 8), ("ln1_b", S_LN1B, 8), ("conv_f_b", S_CFB, 44),
    ("ln2_g", S_LN2G, 8), ("ln2_b", S_LN2B, 8),
]


def small_adamw(sv_slots, ws, ms, vs):
    n = len(SMALL_LAYOUT)

    def body(*refs):
        s_ref = refs[0]
        w_refs, m_refs, v_refs = refs[1:1 + n], refs[1 + n:1 + 2 * n], refs[1 + 2 * n:1 + 3 * n]
        outs = refs[1 + 3 * n:]
        for p, (_, row0, rows) in enumerate(SMALL_LAYOUT):
            sl = pl.ds(row0, rows)
            grad = ((s_ref[0, sl, :] + s_ref[1, sl, :]) + s_ref[2, sl, :]) + s_ref[3, sl, :]
            delta, m2, v2 = _adamw(w_refs[p][...], grad, m_refs[p][...], v_refs[p][...])
            outs[p][...] = grad
            outs[n + p][...] = delta
            outs[2 * n + p][...] = m2
            outs[3 * n + p][...] = v2
        sl = pl.ds(S_LOSS, 8)
        outs[4 * n][...] = ((s_ref[0, sl, :] + s_ref[1, sl, :]) + s_ref[2, sl, :]) + s_ref[3, sl, :]

    shapes = [jax.ShapeDtypeStruct((rows, 128), F32) for _, _, rows in SMALL_LAYOUT]
    return pl.pallas_call(
        body, name="small_adamw", out_shape=shapes * 4 + [jax.ShapeDtypeStruct((8, 128), F32)],
        in_specs=[VMEM] * (1 + 3 * n), out_specs=[VMEM] * (4 * n + 1),
    )(sv_slots, *ws, *ms, *vs)


def mix_forward(x, win_g, b_in, ln_a_g, ln_a_b, w_spatial, bst, conv_b_w, conv_b_b, ln_b_g, ln_b_b,
                wout, b_out, ln1_g, ln1_b, sup, sdown, tm):
    t = x.shape[0]
    nt = t // tm
    n_chunks = tm // CHUNK

    def body(x_ref, win_ref, bin_ref, ga_ref, ba_ref, ws_ref, bst_ref, cw_ref, cb_ref, gb_ref,
             bb_ref, wout_ref, bout_ref, g1_ref, b1_ref, sup_ref, sdown_ref,
             h_ref, xhat1_ref, rstd1_ref, yb1_ref, gup_ref, gdown_ref,
             ext_ref, y_ref, wsm_ref, send_sems, recv_sems, local_sems):
        i = pl.program_id(0)
        gather = ([sup_ref, sdown_ref], [gup_ref, gdown_ref], send_sems, recv_sems, local_sems)

        @pl.when(i == 0)
        def _():
            _gather_start(*gather)
            ext_ref[0:HALO_B, :] = jnp.zeros((HALO_B, D_B), F32)
            mask = _tril_mask()
            for hd in range(HEADS):
                wsm_ref[hd] = jnp.where(mask, ws_ref[hd], 0.0).astype(BF16)

        xb = x_ref[...].astype(BF16)
        for j in range(N_DEV):
            cols = slice(j * W_IN_BLK, (j + 1) * W_IN_BLK)
            h_ref[:, cols] = _nn(xb, win_ref[j]) + bin_ref[:, cols]

        def chunk(ci):
            r = _rows(ci, CHUNK)
            u, _, _, _, _, _, svs = _mixer_a_fwd(h_ref[r, 0:D_A], h_ref[r, D_A:2 * D_A],
                                                 ga_ref, ba_ref, wsm_ref, bst_ref)
            for hd in range(HEADS):
                sl = slice(hd * HEAD_DIM, (hd + 1) * HEAD_DIM)
                y_ref[r, sl] = (u[:, sl] * svs[hd]).astype(BF16)
            a_b = h_ref[r, 2 * D_A:2 * D_A + D_B]
            g_b = h_ref[r, 2 * D_A + D_B:D_IN]
            ext_ref[pl.ds(HALO_B + ci * CHUNK, CHUNK), :] = a_b * _sigmoid(g_b)

        _loop(n_chunks, chunk)

        def conv_rows(bi):
            base = bi * ROWS
            acc, _ = _conv_b_block(ext_ref, base, cw_ref)
            yb1 = acc + cb_ref[...]
            yb1_ref[pl.ds(base, ROWS), :] = yb1
            xhat, _ = _ln_stats(yb1)
            yb2 = xhat * gb_ref[...] + bb_ref[...]
            y_ref[pl.ds(base, ROWS), D_A:D] = (yb2 * _sigmoid(yb2)).astype(BF16)

        _loop(tm // ROWS, conv_rows)
        ext_ref[0:HALO_B, :] = ext_ref[tm:tm + HALO_B, :]

        mix = _nn(y_ref[...], wout_ref[...]) + bout_ref[...]
        xhat1, rstd1 = _ln_stats(ALPHA * x_ref[...] + mix)
        xhat1_ref[...] = xhat1
        rstd1_ref[...] = jnp.broadcast_to(rstd1, (tm, 128))

        @pl.when(i == nt - 1)
        def _():
            _gather_finish(*gather)

    row = lambda w: pl.BlockSpec((tm, w), lambda i: (i, 0))
    return pl.pallas_call(
        body, name="mix_forward", grid=(nt,),
        in_specs=[row(D), _resident(win_g.shape), _full(b_in.shape), _full(ln_a_g.shape),
                  _full(ln_a_b.shape), _full(w_spatial.shape), _full(bst.shape),
                  _full(conv_b_w.shape), _full(conv_b_b.shape), _full(ln_b_g.shape),
                  _full(ln_b_b.shape), _resident(wout.shape), _full(b_out.shape),
                  _full(ln1_g.shape), _full(ln1_b.shape), ANY, ANY],
        out_specs=[row(D_IN), row(D), row(128), row(D_B), ANY, ANY],
        out_shape=[jax.ShapeDtypeStruct((t, D_IN), F32), jax.ShapeDtypeStruct((t, D), F32),
                   jax.ShapeDtypeStruct((t, 128), F32), jax.ShapeDtypeStruct((t, D_B), F32),
                   jax.ShapeDtypeStruct((N_DEV,) + sup.shape, BF16),
                   jax.ShapeDtypeStruct((N_DEV,) + sdown.shape, BF16)],
        scratch_shapes=[pltpu.VMEM((tm + HALO_B, D_B), F32), pltpu.VMEM((tm, D), BF16),
                        pltpu.VMEM((HEADS, CHUNK, CHUNK), BF16)] + _gather_scratch(2),
        compiler_params=_params(("arbitrary",)),
    )(x, win_g, b_in, ln_a_g, ln_a_b, w_spatial, bst, conv_b_w, conv_b_b, ln_b_g, ln_b_b,
      wout, b_out, ln1_g, ln1_b, sup, sdown)


def ffn_forward(xhat1, ln1_g, ln1_b, wup_g, cfw, cfb, wdown, ln2_g, ln2_b, target, tm):
    t = xhat1.shape[0]
    nt = t // tm

    def body(xh_ref, g1_ref, b1_ref, wup_ref, cfw_ref, cfb_ref, wdown_ref, g2_ref, b2_ref, tgt_ref,
             hu_ref, dr2_ref, loss_ref, sln2_ref,
             x1_ref, x1b_ref, carry_ref, gbuf_ref, ffn_ref, acc_loss, acc_g2, acc_b2):
        i = pl.program_id(0)

        @pl.when(i == 0)
        def _():
            carry_ref[...] = jnp.zeros(carry_ref.shape, F32)
            acc_loss[...] = jnp.zeros(acc_loss.shape, F32)
            acc_g2[...] = jnp.zeros(acc_g2.shape, F32)
            acc_b2[...] = jnp.zeros(acc_b2.shape, F32)

        x1 = xh_ref[...] * g1_ref[...] + b1_ref[...]
        x1_ref[...] = x1
        x1b_ref[...] = x1.astype(BF16)

        def conv(j, base):
            if base == 0:
                win = jnp.concatenate([carry_ref[j], hu_ref[j, 0:ROWS, :]], axis=0)
            else:
                win = hu_ref[j, base - HALO_F:base + ROWS, :]
            taps = _taps_f(win)
            w = cfw_ref[j]
            return sum(taps[k] * w[k:k + 1, :] for k in range(KF)) + cfb_ref[j:j + 1, :]

        for f in range(N_F):
            hu_ref[f] = _nn(x1b_ref[...], wup_ref[f])
            hu_ref[N_F + f] = _nn(x1b_ref[...], wup_ref[N_F + f])

            def rows(bi, f=f):
                gate = conv(f, bi * ROWS)
                val = conv(N_F + f, bi * ROWS)
                gbuf_ref[_rows(bi), :] = (gate * _sigmoid(gate) * val).astype(BF16)

            _loop(tm // ROWS, rows)
            carry_ref[f] = hu_ref[f, tm - HALO_F:tm, :]
            carry_ref[N_F + f] = hu_ref[N_F + f, tm - HALO_F:tm, :]
            part = _nn(gbuf_ref[...], wdown_ref[f])
            if f == 0:
                ffn_ref[...] = part
            else:
                ffn_ref[...] += part

        def tail(bi):
            r = _rows(bi)
            xhat2, rstd2 = _ln_stats(ALPHA * x1_ref[r, :] + ffn_ref[r, :])
            err = xhat2 * g2_ref[...] + b2_ref[...] - tgt_ref[r, :]
            e2 = _rsum8(err * err)
            acc_loss[...] += sum(e2[:, k * 128:(k + 1) * 128] for k in range(D // 128))
            dy = err * (1.0 / D)
            acc_g2[...] += _rsum8(dy * xhat2)
            acc_b2[...] += _rsum8(dy)
            dr2_ref[r, :] = _ln_bwd(dy * g2_ref[...], xhat2, rstd2)

        _loop(tm // ROWS, tail)
        loss_ref[...] = acc_loss[...]

        @pl.when(i == nt - 1)
        def _():
            dg = jnp.sum(acc_g2[...], axis=0, keepdims=True)
            db = jnp.sum(acc_b2[...], axis=0, keepdims=True)
            for k in range(D // 128):
                sln2_ref[k:k + 1, :] = dg[:, k * 128:(k + 1) * 128]
                sln2_ref[8 + k:9 + k, :] = db[:, k * 128:(k + 1) * 128]

    row = pl.BlockSpec((tm, D), lambda i: (i, 0))
    return pl.pallas_call(
        body, name="ffn_forward", grid=(nt,),
        in_specs=[row, _full(ln1_g.shape), _full(ln1_b.shape), _resident(wup_g.shape),
                  _full(cfw.shape), _full(cfb.shape), _resident(wdown.shape),
                  _full(ln2_g.shape), _full(ln2_b.shape), row],
        out_specs=[pl.BlockSpec((N_DEV, tm, W_UP_BLK), lambda i: (0, i, 0)), row,
                   _full((8, 128)), _full((16, 128))],
        out_shape=[jax.ShapeDtypeStruct((N_DEV, t, W_UP_BLK), F32), jax.ShapeDtypeStruct((t, D), F32),
                   jax.ShapeDtypeStruct((8, 128), F32), jax.ShapeDtypeStruct((16, 128), F32)],
        scratch_shapes=[pltpu.VMEM((tm, D), F32), pltpu.VMEM((tm, D), BF16),
                        pltpu.VMEM((N_DEV, HALO_F, W_UP_BLK), F32), pltpu.VMEM((tm, W_UP_BLK), BF16),
                        pltpu.VMEM((tm, D), F32), pltpu.VMEM((8, 128), F32),
                        pltpu.VMEM((8, D), F32), pltpu.VMEM((8, D), F32)],
        compiler_params=_params(("arbitrary",)),
    )(xhat1, ln1_g, ln1_b, wup_g, cfw, cfb, wdown, ln2_g, ln2_b, target)


def ffn_backward(dr2, xhat1, ln1_g, ln1_b, hu, wup_g, cfw, cfb, wdown, tm):
    t = dr2.shape[0]
    nt = t // tm
    sub_rows = tm
    hu4 = hu.reshape(2, N_F, t, W_UP_BLK)
    wup4 = wup_g.reshape(2, N_F, D, W_UP_BLK)
    cfw4 = cfw.reshape(2, N_F, KF, W_UP_BLK)
    cfb4 = cfb.reshape(2, N_F, 1, W_UP_BLK)
    halo_blocks = tm // HALO_F

    def body(dr2_ref, xh_ref, g1_ref, b1_ref, hu_ref, halo_ref, wup_ref, cfw_ref, cfb_ref, wdown_ref,
             dwup_ref, dwdown_ref, dcfw_ref, dcfb_ref, dx1_ref,
             x1b_ref, drb_ref, dg_ref, dextg_ref, dextv_ref, gbuf_ref,
             dhug_ref, dhuv_ref, acc_wup, acc_wdown, acc_cfw, acc_cfb, sem):
        f = pl.program_id(0)
        i = pl.program_id(1)
        first_tile = i == nt - 1

        @pl.when(i == 0)
        def _():
            acc_wup[...] = jnp.zeros(acc_wup.shape, F32)
            acc_wdown[...] = jnp.zeros(acc_wdown.shape, F32)
            acc_cfw[...] = jnp.zeros(acc_cfw.shape, F32)
            acc_cfb[...] = jnp.zeros(acc_cfb.shape, F32)
            dextg_ref[tm:tm + HALO_F, :] = jnp.zeros((HALO_F, W_UP_BLK), F32)
            dextv_ref[tm:tm + HALO_F, :] = jnp.zeros((HALO_F, W_UP_BLK), F32)

        keep = jnp.where(first_tile, 0.0, 1.0)
        halo = [halo_ref[0, 0] * keep, halo_ref[1, 0] * keep]
        w = [cfw_ref[0, 0], cfw_ref[1, 0]]
        b = [cfb_ref[0, 0], cfb_ref[1, 0]]
        dext = [dextg_ref, dextv_ref]
        dhu = [dhug_ref, dhuv_ref]

        def taps(g, base):
            if base == 0:
                win = jnp.concatenate([halo[g], hu_ref[g, 0, 0:ROWS, :]], axis=0)
            else:
                win = hu_ref[g, 0, base - HALO_F:base + ROWS, :]
            return _taps_f(win)

        def rows1(bi):
            r = _rows(bi)
            tg, tv = taps(0, bi * ROWS), taps(1, bi * ROWS)
            gate = sum(tg[k] * w[0][k:k + 1, :] for k in range(KF)) + b[0]
            val = sum(tv[k] * w[1][k:k + 1, :] for k in range(KF)) + b[1]
            sg = _sigmoid(gate)
            silu = gate * sg
            gbuf_ref[r, :] = (silu * val).astype(BF16)
            dg = dg_ref[r, :]
            dgate = dg * val * (sg * (1.0 + gate * (1.0 - sg)))
            dval = dg * silu
            dextg_ref[r, :] = dgate
            dextv_ref[r, :] = dval
            acc_cfb[0:8, :] += _rsum8(dgate)
            acc_cfb[8:16, :] += _rsum8(dval)
            for k in range(KF):
                acc_cfw[8 * k:8 * k + 8, :] += _rsum8(dgate * tg[k])
                acc_cfw[8 * (KF + k):8 * (KF + k) + 8, :] += _rsum8(dval * tv[k])

        def rows2(bi):
            r = _rows(bi)
            for g in range(2):
                win = dext[g][pl.ds(bi * ROWS, ROWS + HALO_F), :]
                n = ROWS + HALO_F
                later = [pltpu.roll(win, n - 2, 0)[0:ROWS, :], pltpu.roll(win, n - 1, 0)[0:ROWS, :],
                         win[0:ROWS, :]]
                d = sum(later[k] * w[g][k:k + 1, :] for k in range(KF))
                dhu[g][r, :] = d.astype(BF16)

        for sub in reversed(range(tm // sub_rows)):
            rs = slice(sub * sub_rows, (sub + 1) * sub_rows)
            blocks = range(sub * sub_rows // ROWS, (sub + 1) * sub_rows // ROWS)
            x1b_ref[rs, :] = (xh_ref[rs, :] * g1_ref[...] + b1_ref[...]).astype(BF16)
            drb_ref[rs, :] = dr2_ref[rs, :].astype(BF16)
            dg_ref[rs, :] = _nt(drb_ref[rs, :], wdown_ref[0])
            for bi in blocks:
                rows1(bi)
            for bi in blocks:
                rows2(bi)
            acc_wdown[...] += _tn(gbuf_ref[rs, :], drb_ref[rs, :])
            x1t = x1b_ref[rs, :].T
            acc_wup[0] += _nn(x1t, dhug_ref[rs, :])
            acc_wup[1] += _nn(x1t, dhuv_ref[rs, :])
            dx1_ref[0, rs, :] = (_nt(dhug_ref[rs, :], wup_ref[0, 0])
                                 + _nt(dhuv_ref[rs, :], wup_ref[1, 0])).astype(BF16)
        dextg_ref[tm:tm + HALO_F, :] = dextg_ref[0:HALO_F, :]
        dextv_ref[tm:tm + HALO_F, :] = dextv_ref[0:HALO_F, :]

        @pl.when(i == nt - 1)
        def _():
            for g in range(2):
                dcfb_ref[g, 0] = jnp.sum(acc_cfb[8 * g:8 * g + 8, :], axis=0, keepdims=True)
                for k in range(KF):
                    r0 = 8 * (g * KF + k)
                    dcfw_ref[g, 0, k:k + 1, :] = jnp.sum(acc_cfw[r0:r0 + 8, :], axis=0, keepdims=True)
            cps = [pltpu.make_async_copy(acc_wup.at[0], dwup_ref.at[0, f], sem.at[0]),
                   pltpu.make_async_copy(acc_wup.at[1], dwup_ref.at[1, f], sem.at[1]),
                   pltpu.make_async_copy(acc_wdown, dwdown_ref.at[f], sem.at[2])]
            for cp in cps:
                cp.start()
            for cp in cps:
                cp.wait()

    rev = lambda i: nt - 1 - i
    row = pl.BlockSpec((tm, D), lambda f, i: (rev(i), 0))
    pair = lambda r, c: pl.BlockSpec((2, 1, r, c), lambda f, i: (0, f, 0, 0))
    return pl.pallas_call(
        body, name="ffn_backward", grid=(N_F, nt),
        in_specs=[row, row, _full(ln1_g.shape), _full(ln1_b.shape),
                  pl.BlockSpec((2, 1, tm, W_UP_BLK), lambda f, i: (0, f, rev(i), 0)),
                  pl.BlockSpec((2, 1, HALO_F, W_UP_BLK),
                               lambda f, i: (0, f, jnp.maximum(rev(i) * halo_blocks - 1, 0), 0)),
                  pair(D, W_UP_BLK), pair(KF, W_UP_BLK), pair(1, W_UP_BLK),
                  pl.BlockSpec((1, W_UP_BLK, D), lambda f, i: (f, 0, 0))],
        out_specs=[ANY, ANY, pair(KF, W_UP_BLK), pair(1, W_UP_BLK),
                   pl.BlockSpec((1, tm, D), lambda f, i: (f, rev(i), 0))],
        out_shape=[jax.ShapeDtypeStruct((2, N_F, D, W_UP_BLK), F32),
                   jax.ShapeDtypeStruct((N_F, W_UP_BLK, D), F32),
                   jax.ShapeDtypeStruct((2, N_F, KF, W_UP_BLK), F32),
                   jax.ShapeDtypeStruct((2, N_F, 1, W_UP_BLK), F32),
                   jax.ShapeDtypeStruct((N_F, t, D), BF16)],
        scratch_shapes=[pltpu.VMEM((tm, D), BF16), pltpu.VMEM((tm, D), BF16),
                        pltpu.VMEM((tm, W_UP_BLK), F32),
                        pltpu.VMEM((tm + HALO_F, W_UP_BLK), F32), pltpu.VMEM((tm + HALO_F, W_UP_BLK), F32),
                        pltpu.VMEM((tm, W_UP_BLK), BF16), pltpu.VMEM((tm, W_UP_BLK), BF16),
                        pltpu.VMEM((tm, W_UP_BLK), BF16),
                        pltpu.VMEM((2, D, W_UP_BLK), F32), pltpu.VMEM((W_UP_BLK, D), F32),
                        pltpu.VMEM((2 * KF * 8, W_UP_BLK), F32), pltpu.VMEM((16, W_UP_BLK), F32),
                        pltpu.SemaphoreType.DMA((3,))],
        compiler_params=_params(("arbitrary", "arbitrary")),
    )(dr2, xhat1, ln1_g, ln1_b, hu4, hu4, wup4, cfw4, cfb4, wdown)


def mix_backward(x, h, yb1, dx1p, dr2, xhat1, rstd1, win_g, ln_a_g, ln_a_b, w_spatial, bst,
                 conv_b_w, conv_b_b, ln_b_g, ln_b_b, wout, ln1_g, ffn_partials, tm):
    t = x.shape[0]
    n_p = len(ffn_partials)
    nt = t // tm
    n_chunks = tm // CHUNK
    halo_blocks = tm // HALO_B

    def body(x_ref, h_ref, halo_ref, yb1_ref, dx1p_ref, dr2_ref, xh1_ref, rstd1_ref, win_ref, ga_ref, ba_ref,
             ws_ref, bst_ref, cw_ref, cb_ref, gb_ref, bb_ref, wout_ref, g1_ref, *rest):
        p_refs, rest = rest[:n_p], rest[n_p:]
        gx_ref, dwin_ref, dwout_ref, dcw_ref, small_ref = rest[:5]
        land_refs, rest = rest[5:5 + n_p], rest[5 + n_p:]
        (ext_ref, dext_ref, y_ref, dy_ref, dh_ref, dmb_ref, wsm_ref,
         acc_win, acc_wout, acc_bin, acc_lnag, acc_lnab, acc_ws, acc_bs, acc_cbb, acc_lnbg,
         acc_lnbb, acc_bout, acc_ln1g, acc_ln1b, acc_cw, sem, send_sems, recv_sems) = rest
        i = pl.program_id(0)

        @pl.when(i == 0)
        def _():
            for cp in _chip_copies(p_refs, land_refs, send_sems, recv_sems):
                cp.start()

        first_tile = i == nt - 1
        accs = [acc_win, acc_wout, acc_bin, acc_lnag, acc_lnab, acc_ws, acc_bs, acc_cbb, acc_lnbg,
                acc_lnbb, acc_bout, acc_ln1g, acc_ln1b, acc_cw]

        @pl.when(i == 0)
        def _():
            for acc in accs:
                acc[...] = jnp.zeros(acc.shape, F32)
            dext_ref[tm:tm + HALO_B, :] = jnp.zeros((HALO_B, D_B), F32)
            mask = _tril_mask()
            for hd in range(HEADS):
                wsm_ref[hd] = jnp.where(mask, ws_ref[hd], 0.0).astype(BF16)

        def ln1_rows(bi):
            r = _rows(bi)
            part = [dx1p_ref[f, r, :].astype(F32) for f in range(N_F)]
            dx1 = ALPHA * dr2_ref[r, :] + ((part[0] + part[1]) + (part[2] + part[3]))
            xhat = xh1_ref[r, :]
            acc_ln1g[...] += _rsum8(dx1 * xhat)
            acc_ln1b[...] += _rsum8(dx1)
            dr1 = _ln_bwd(dx1 * g1_ref[...], xhat, rstd1_ref[r, 0:1])
            acc_bout[...] += _rsum8(dr1)
            gx_ref[r, :] = ALPHA * dr1
            dmb_ref[r, :] = dr1.astype(BF16)

        _loop(tm // ROWS, ln1_rows)
        dy_ref[...] = _nt(dmb_ref[...], wout_ref[...])

        ha = halo_ref[:, 0:D_B]
        hg = halo_ref[:, D_B:2 * D_B]
        ext_ref[0:HALO_B, :] = jnp.where(first_tile, 0.0, 1.0) * (ha * _sigmoid(hg))

        def chunk(ci):
            r = _rows(ci, CHUNK)
            hu, hv = h_ref[r, 0:D_A], h_ref[r, D_A:2 * D_A]
            u, cdf_u, cdf_v, xhats, rstds, vns, svs = _mixer_a_fwd(hu, hv, ga_ref, ba_ref, wsm_ref, bst_ref)
            for hd in range(HEADS):
                sl = slice(hd * HEAD_DIM, (hd + 1) * HEAD_DIM)
                rows8 = slice(8 * hd, 8 * hd + 8)
                dy_a = dy_ref[r, sl]
                y_ref[r, sl] = (u[:, sl] * svs[hd]).astype(BF16)
                du = dy_a * svs[hd]
                dsv = dy_a * u[:, sl]
                dsvb = dsv.astype(BF16)
                acc_bs[hd] += dsv
                acc_ws[hd] += _nt(dsvb, vns[hd])
                dvn = _tn(wsm_ref[hd], dsvb)
                acc_lnag[rows8, :] += _rsum8(dvn * xhats[hd])
                acc_lnab[rows8, :] += _rsum8(dvn)
                dv = _ln_bwd(dvn * ga_ref[hd:hd + 1, :], xhats[hd], rstds[hd])
                hus, hvs = hu[:, sl], hv[:, sl]
                slv = slice(D_A + hd * HEAD_DIM, D_A + (hd + 1) * HEAD_DIM)
                dhu = du * (cdf_u[:, sl] + hus * jnp.exp(-0.5 * hus * hus) * INV_SQRT_2PI)
                dhv = dv * (cdf_v[:, sl] + hvs * jnp.exp(-0.5 * hvs * hvs) * INV_SQRT_2PI)
                acc_bin[:, sl] += _rsum8(dhu)
                acc_bin[:, slv] += _rsum8(dhv)
                dh_ref[r, sl] = dhu.astype(BF16)
                dh_ref[r, slv] = dhv.astype(BF16)
            a_b = h_ref[r, 2 * D_A:2 * D_A + D_B]
            g_b = h_ref[r, 2 * D_A + D_B:D_IN]
            ext_ref[pl.ds(HALO_B + ci * CHUNK, CHUNK), :] = a_b * _sigmoid(g_b)

        _loop(n_chunks, chunk)

        def conv_rows(bi):
            base = bi * ROWS
            r = pl.ds(base, ROWS)
            win = _shifted(ext_ref[pl.ds(base, ROWS + HALO_B), :])
            xhat, rstd = _ln_stats(yb1_ref[r, :])
            yb2 = xhat * gb_ref[...] + bb_ref[...]
            sg = _sigmoid(yb2)
            y_ref[r, D_A:D] = (yb2 * sg).astype(BF16)
            dyb2 = dy_ref[r, D_A:D] * (sg * (1.0 + yb2 * (1.0 - sg)))
            acc_lnbg[...] += _rsum8(dyb2 * xhat)
            acc_lnbb[...] += _rsum8(dyb2)
            dyb1 = _ln_bwd(dyb2 * gb_ref[...], xhat, rstd)
            acc_cbb[...] += _rsum8(dyb1)
            dext_ref[r, :] = dyb1
            for k in range(KB):
                acc_cw[8 * k:8 * k + 8, :] += _rsum8(dyb1 * _tap(win, 2 + k))

        _loop(tm // ROWS, conv_rows)

        def convt_rows(bi):
            base = bi * ROWS
            r = pl.ds(base, ROWS)
            dwin = _shifted(dext_ref[pl.ds(base, ROWS + HALO_B), :])
            dyb0 = jnp.zeros((ROWS, D_B), F32)
            for k in range(KB):
                dyb0 = dyb0 + _tap(dwin, 30 - k) * cw_ref[k:k + 1, :]
            a_b = h_ref[r, 2 * D_A:2 * D_A + D_B]
            sg = _sigmoid(h_ref[r, 2 * D_A + D_B:D_IN])
            da_b = dyb0 * sg
            dg_b = dyb0 * a_b * sg * (1.0 - sg)
            acc_bin[:, 2 * D_A:2 * D_A + D_B] += _rsum8(da_b)
            acc_bin[:, 2 * D_A + D_B:D_IN] += _rsum8(dg_b)
            dh_ref[r, 2 * D_A:2 * D_A + D_B] = da_b.astype(BF16)
            dh_ref[r, 2 * D_A + D_B:D_IN] = dg_b.astype(BF16)

        _loop(tm // ROWS, convt_rows)
        dext_ref[tm:tm + HALO_B, :] = dext_ref[0:HALO_B, :]

        acc_wout[...] += _tn(y_ref[...], dmb_ref[...])
        xt = x_ref[...].T.astype(BF16)
        dh_blocks = [dh_ref[:, j * W_IN_BLK:(j + 1) * W_IN_BLK] for j in range(N_DEV)]
        for j in range(N_DEV):
            acc_win[j] += _nn(xt, dh_blocks[j])
        gx_ref[...] += sum(_nt(dh_blocks[j], win_ref[j]) for j in range(N_DEV))

        @pl.when(i == nt - 1)
        def _():
            cps = [pltpu.make_async_copy(acc_win, dwin_ref, sem.at[0]),
                   pltpu.make_async_copy(acc_wout, dwout_ref, sem.at[1])]
            for cp in cps:
                cp.start()
            small_ref[...] = jnp.zeros(small_ref.shape, F32)

            def put_row_vector(row0, acc):
                vec = jnp.sum(acc[...], axis=0, keepdims=True)
                for k in range(vec.shape[1] // 128):
                    small_ref[row0 + k:row0 + k + 1, :] = vec[:, k * 128:(k + 1) * 128]

            put_row_vector(S_BIN, acc_bin)
            put_row_vector(S_CBB, acc_cbb)
            put_row_vector(S_LNBG, acc_lnbg)
            put_row_vector(S_LNBB, acc_lnbb)
            put_row_vector(S_BOUT, acc_bout)
            put_row_vector(S_LN1G, acc_ln1g)
            put_row_vector(S_LN1B, acc_ln1b)
            mask = _tril_mask()
            for hd in range(HEADS):
                rows8 = slice(8 * hd, 8 * hd + 8)
                small_ref[S_LNAG + hd:S_LNAG + hd + 1, :] = jnp.sum(acc_lnag[rows8, :], axis=0, keepdims=True)
                small_ref[S_LNAB + hd:S_LNAB + hd + 1, :] = jnp.sum(acc_lnab[rows8, :], axis=0, keepdims=True)
                small_ref[S_WS + hd * CHUNK:S_WS + (hd + 1) * CHUNK, :] = jnp.where(mask, acc_ws[hd], 0.0)
                small_ref[S_BS + hd:S_BS + hd + 1, :] = jnp.sum(acc_bs[hd].T, axis=0, keepdims=True)
            for k in range(KB):
                dcw_ref[k:k + 1, :] = jnp.sum(acc_cw[8 * k:8 * k + 8, :], axis=0, keepdims=True)
            for cp in cps:
                cp.wait()
            for cp in _chip_copies(p_refs, land_refs, send_sems, recv_sems):
                cp.wait()

    rev = lambda i: nt - 1 - i
    row = lambda w: pl.BlockSpec((tm, w), lambda i: (rev(i), 0))
    return pl.pallas_call(
        body, name="mix_backward", grid=(nt,),
        in_specs=[row(D), row(D_IN),
                  pl.BlockSpec((HALO_B, 2 * D_B), lambda i: (jnp.maximum(rev(i) * halo_blocks - 1, 0), 1)),
                  row(D_B), pl.BlockSpec((N_F, tm, D), lambda i: (0, rev(i), 0)),
                  row(D), row(D), row(128), _resident(win_g.shape), _full(ln_a_g.shape),
                  _full(ln_a_b.shape), _full(w_spatial.shape), _full(bst.shape), _full(conv_b_w.shape),
                  _full(conv_b_b.shape), _full(ln_b_g.shape), _full(ln_b_b.shape),
                  _resident(wout.shape), _full(ln1_g.shape)] + [ANY] * n_p,
        out_specs=[row(D), ANY, ANY, _full((KB, D_B)), _full((S_MIX_ROWS, 128))] + [ANY] * n_p,
        out_shape=[jax.ShapeDtypeStruct((t, D), F32), jax.ShapeDtypeStruct((N_DEV, D, W_IN_BLK), F32),
                   jax.ShapeDtypeStruct((D, D), F32), jax.ShapeDtypeStruct((KB, D_B), F32),
                   jax.ShapeDtypeStruct((S_MIX_ROWS, 128), F32)]
        + [jax.ShapeDtypeStruct(p.shape, BF16) for p in ffn_partials],
        scratch_shapes=[pltpu.VMEM((tm + HALO_B, D_B), F32), pltpu.VMEM((tm + HALO_B, D_B), F32),
                        pltpu.VMEM((tm, D), BF16), pltpu.VMEM((tm, D), F32), pltpu.VMEM((tm, D_IN), BF16),
                        pltpu.VMEM((tm, D), BF16),
                        pltpu.VMEM((HEADS, CHUNK, CHUNK), BF16),
                        pltpu.VMEM((N_DEV, D, W_IN_BLK), F32), pltpu.VMEM((D, D), F32),
                        pltpu.VMEM((8, D_IN), F32), pltpu.VMEM((8 * HEADS, HEAD_DIM), F32),
                        pltpu.VMEM((8 * HEADS, HEAD_DIM), F32), pltpu.VMEM((HEADS, CHUNK, CHUNK), F32),
                        pltpu.VMEM((HEADS, CHUNK, CHUNK), F32), pltpu.VMEM((8, D_B), F32),
                        pltpu.VMEM((8, D_B), F32), pltpu.VMEM((8, D_B), F32), pltpu.VMEM((8, D), F32),
                        pltpu.VMEM((8, D), F32), pltpu.VMEM((8, D), F32), pltpu.VMEM((8 * KB, D_B), F32),
                        pltpu.SemaphoreType.DMA((2,)),
                        pltpu.SemaphoreType.DMA((n_p, 3)), pltpu.SemaphoreType.DMA((n_p, 3))],
        compiler_params=_params(("arbitrary",)),
    )(x, h, h, yb1, dx1p, dr2, xhat1, rstd1, win_g, ln_a_g, ln_a_b, w_spatial, bst, conv_b_w, conv_b_b,
      ln_b_g, ln_b_b, wout, ln1_g, *ffn_partials)


def _rows128(a):
    return a.reshape(-1, 128)


def _pack_conv(cb, cf):
    out = jnp.zeros((40, 768), F32)
    out = out.at[0:KB, 0:64].set(cb)
    return out.at[32:32 + KF, 0:W_UP_BLK].set(cf)


def kernel(x, w_in, b_in, ln_a_g, ln_a_b, w_spatial, b_spatial, conv_b_w, conv_b_b, ln_b_g, ln_b_b, w_out, b_out, ln1_g, ln1_b, w_up, conv_f_w, conv_f_b, w_down, ln2_g, ln2_b, loss_target, m_w_in, m_b_in, m_ln_a_g, m_ln_a_b, m_w_spatial, m_b_spatial, m_conv_b_w, m_conv_b_b, m_ln_b_g, m_ln_b_b, m_w_out, m_b_out, m_ln1_g, m_ln1_b, m_w_up, m_conv_f_w, m_conv_f_b, m_w_down, m_ln2_g, m_ln2_b, v_w_in, v_b_in, v_ln_a_g, v_ln_a_b, v_w_spatial, v_b_spatial, v_conv_b_w, v_conv_b_b, v_ln_b_g, v_ln_b_b, v_w_out, v_b_out, v_ln1_g, v_ln1_b, v_w_up, v_conv_f_w, v_conv_f_b, v_w_down, v_ln2_g, v_ln2_b):
    t = x.shape[1]
    x2 = x.reshape(t, D)
    target = loss_target.reshape(t, D)
    tm_fwd = min(t, 512)
    tm_bwd = min(t, 256)
    tm_ffn_bwd = min(t, 512)

    xi, yi, ci = _mesh_pos()
    jidx = jnp.stack([_lid(px, py, ci) for px, py in _chip_patterns(xi, yi)]).astype(jnp.int32)

    win_g, wout_g, conv_g, sup, sdown = all_gather_mixer_weights(
        w_in, w_out, w_up, w_down, _pack_conv(conv_b_w, conv_f_w))
    wout_full = wout_g.reshape(D, D)
    conv_b_full = conv_g[:, 0:KB, 0:64].transpose(1, 0, 2).reshape(KB, D_B)
    cfw = conv_g[:, 32:32 + KF, 0:W_UP_BLK]
    cfb = conv_f_b.reshape(N_DEV, W_UP_BLK)
    row = lambda a: a.reshape(1, -1)
    bst = b_spatial.T

    h, xhat1, rstd1, yb1, wup_g, wdown_g = mix_forward(
        x2, win_g, row(b_in), ln_a_g, ln_a_b, w_spatial, bst, conv_b_full, row(conv_b_b),
        row(ln_b_g), row(ln_b_b), wout_full, row(b_out), row(ln1_g), row(ln1_b), sup, sdown, tm_fwd)
    wdown4 = wdown_g.reshape(N_F, W_UP_BLK, D)
    hu, dr2, loss_part, s_ln2 = ffn_forward(
        xhat1, row(ln1_g), row(ln1_b), wup_g, cfw, cfb, wdown4, row(ln2_g), row(ln2_b), target, tm_bwd)

    dwup, dwdown, dcfw, dcfb, dx1p = ffn_backward(
        dr2, xhat1, row(ln1_g), row(ln1_b), hu, wup_g, cfw, cfb, wdown4, tm_ffn_bwd)
    ffn_grads = [dwup.reshape(N_DEV, D, W_UP_BLK), dwdown.reshape(N_DEV, D_FF // N_DEV, D)]
    ffn_lands = exchange_with_sibling("exchange_with_sibling_ffn", ffn_grads)
    ffn_partials = [chip_partials("chip_partials_" + nm, g, l, jidx, rb)
                    for nm, g, l, rb in zip(["w_up", "w_down"], ffn_grads, ffn_lands, [256, 352])]
    grad_x, dwin, dwout, dcw, s_mix, *ffn_recvs = mix_backward(
        x2, h, yb1, dx1p, dr2, xhat1, rstd1, win_g, ln_a_g, ln_a_b, w_spatial, bst,
        conv_b_full, row(conv_b_b), row(ln_b_g), row(ln_b_b), wout_full, row(ln1_g), ffn_partials, tm_bwd)

    dcfb_rows = jnp.pad(dcfb.reshape(-1, 128), ((0, 4), (0, 0)))
    svec = jnp.concatenate([s_mix, dcfb_rows, s_ln2, loss_part], axis=0)
    dconv = jnp.zeros((N_DEV, 40, 768), F32)
    dconv = dconv.at[:, 0:KB, 0:64].set(dcw.reshape(KB, N_DEV, 64).transpose(1, 0, 2))
    dconv = dconv.at[:, 32:32 + KF, 0:W_UP_BLK].set(dcfw.reshape(N_DEV, KF, W_UP_BLK))
    mix_grads = [dwin, dwout.reshape(N_DEV, D // N_DEV, D), dconv]
    *mix_lands, sv_land = exchange_with_sibling("exchange_with_sibling_mixer", mix_grads, svec)
    mix_partials = [chip_partials("chip_partials_" + nm, g, l, jidx, rb)
                    for nm, g, l, rb in zip(["w_in", "w_out", "conv"], mix_grads, mix_lands, [512, 128, 40])]
    *mix_recvs, sv_slots = exchange_between_chips(mix_partials, svec, sv_land)
    names = ["w_in", "w_out", "conv", "w_up", "w_down"]
    grads = mix_grads + ffn_grads
    lands = mix_lands + list(ffn_lands)
    recvs = mix_recvs + ffn_recvs
    row_blocks = [512, 128, 40, 256, 352]

    shard_w = [w_in, w_out, _pack_conv(conv_b_w, conv_f_w), w_up, w_down]
    shard_m = [m_w_in, m_w_out, _pack_conv(m_conv_b_w, m_conv_f_w), m_w_up, m_w_down]
    shard_v = [v_w_in, v_w_out, _pack_conv(v_conv_b_w, v_conv_f_w), v_w_up, v_w_down]
    big = {}
    for nm, g, l, r, w, m, v, rb in zip(names, grads, lands, recvs, shard_w, shard_m, shard_v, row_blocks):
        big[nm] = reduce_and_adamw("reduce_adamw_" + nm, g, l, r, w, m, v, jidx, rb)
    for k in range(4):
        packed = big["conv"][k]
        big.setdefault("conv_b_w", []).append(packed[0:KB, 0:64])
        big.setdefault("conv_f_w", []).append(packed[32:32 + KF, 0:W_UP_BLK])

    small_w = dict(b_in=b_in, ln_a_g=ln_a_g, ln_a_b=ln_a_b, w_spatial=w_spatial, b_spatial=b_spatial,
                   conv_b_b=conv_b_b, ln_b_g=ln_b_g, ln_b_b=ln_b_b, b_out=b_out, ln1_g=ln1_g,
                   ln1_b=ln1_b, conv_f_b=conv_f_b, ln2_g=ln2_g, ln2_b=ln2_b)
    small_m = dict(b_in=m_b_in, ln_a_g=m_ln_a_g, ln_a_b=m_ln_a_b, w_spatial=m_w_spatial,
                   b_spatial=m_b_spatial, conv_b_b=m_conv_b_b, ln_b_g=m_ln_b_g, ln_b_b=m_ln_b_b,
                   b_out=m_b_out, ln1_g=m_ln1_g, ln1_b=m_ln1_b, conv_f_b=m_conv_f_b, ln2_g=m_ln2_g,
                   ln2_b=m_ln2_b)
    small_v = dict(b_in=v_b_in, ln_a_g=v_ln_a_g, ln_a_b=v_ln_a_b, w_spatial=v_w_spatial,
                   b_spatial=v_b_spatial, conv_b_b=v_conv_b_b, ln_b_g=v_ln_b_g, ln_b_b=v_ln_b_b,
                   b_out=v_b_out, ln1_g=v_ln1_g, ln1_b=v_ln1_b, conv_f_b=v_conv_f_b, ln2_g=v_ln2_g,
                   ln2_b=v_ln2_b)
    order = [nm for nm, _, _ in SMALL_LAYOUT]
    small_out = small_adamw(sv_slots, [_rows128(small_w[nm]) for nm in order],
                            [_rows128(small_m[nm]) for nm in order], [_rows128(small_v[nm]) for nm in order])
    n_small = len(order)
    small = {nm: [small_out[k * n_small + p].reshape(small_w[nm].shape) for k in range(4)]
             for p, nm in enumerate(order)}
    loss = jnp.sum(small_out[4 * n_small]) * (0.5 / D)

    weights = ["w_in", "b_in", "ln_a_g", "ln_a_b", "w_spatial", "b_spatial", "conv_b_w", "conv_b_b",
               "ln_b_g", "ln_b_b", "w_out", "b_out", "ln1_g", "ln1_b", "w_up", "conv_f_w", "conv_f_b",
               "w_down", "ln2_g", "ln2_b"]
    result = lambda nm, k: big[nm][k] if nm in big else small[nm][k]
    return (loss, grad_x.reshape(x.shape), *[result(nm, 0) for nm in weights],
            *[result(nm, 1) for nm in weights], *[result(nm, 2) for nm in weights],
            *[result(nm, 3) for nm in weights])
```

```python
import functools
import math

import jax
import jax.numpy as jnp
from jax import lax
from jax.experimental import pallas as pl
from jax.experimental.pallas import tpu as pltpu

F32 = jnp.float32
BF16 = jnp.bfloat16

D = 1024
D_A = 512
D_B = 512
HEADS = 4
HEAD_DIM = 128
CHUNK = 128
KB = 31
KF = 3
D_FF = 2816
D_IN = 2048
N_DEV = 8
W_IN_BLK = D_IN // N_DEV
W_UP_BLK = 2 * D_FF // N_DEV
N_F = 4
LN_EPS = 1e-5
ALPHA = 2.0 ** 0.25

ADAM_LR = 0.001
ADAM_B1 = 0.9
ADAM_B2 = 0.999
ADAM_EPS = 1e-08
ADAM_WD = 0.01
ADAM_STEP = 10

INV_SQRT2 = 1.0 / math.sqrt(2.0)
INV_SQRT_2PI = 1.0 / math.sqrt(2.0 * math.pi)

HALO_B = 32
HALO_F = 8
ROWS = 64
VMEM_LIMIT = 58 * 1024 * 1024

MESH = pl.DeviceIdType.MESH
ANY = pl.BlockSpec(memory_space=pl.ANY)
VMEM = pl.BlockSpec(memory_space=pltpu.VMEM)

S_BIN, S_LNAG, S_LNAB, S_WS, S_BS, S_CBB, S_LNBG, S_LNBB, S_BOUT, S_LN1G, S_LN1B = (
    0, 16, 24, 32, 544, 552, 560, 568, 576, 584, 592)
S_MIX_ROWS = 600
S_CFB = 600
S_LN2G = 648
S_LN2B = 656
S_LOSS = 664
S_ROWS = 672


def _tn(a, b):
    return lax.dot_general(a, b, (((0,), (0,)), ((), ())), preferred_element_type=F32)


def _nt(a, b):
    return lax.dot_general(a, b, (((1,), (1,)), ((), ())), preferred_element_type=F32)


def _nn(a, b):
    return jnp.dot(a, b, preferred_element_type=F32)


def _sigmoid(x):
    return 1.0 / (1.0 + jnp.exp(-x))


def _ln_stats(x):
    mu = jnp.mean(x, axis=-1, keepdims=True)
    xc = x - mu
    var = jnp.mean(xc * xc, axis=-1, keepdims=True)
    rstd = lax.rsqrt(var + LN_EPS)
    return xc * rstd, rstd


def _ln_bwd(dxhat, xhat, rstd):
    m1 = jnp.mean(dxhat, axis=-1, keepdims=True)
    m2 = jnp.mean(dxhat * xhat, axis=-1, keepdims=True)
    return rstd * (dxhat - m1 - xhat * m2)


def _rsum8(x):
    r, n = x.shape
    return x.reshape(r // 8, 8, n).sum(axis=0)


def _rows(i, n=ROWS):
    return pl.ds(i * n, n)


def _loop(n, body):
    for i in range(n):
        body(i)


def _tril_mask():
    r = lax.broadcasted_iota(jnp.int32, (CHUNK, CHUNK), 0)
    c = lax.broadcasted_iota(jnp.int32, (CHUNK, CHUNK), 1)
    return c <= r


def _mixer_a_fwd(hu, hv, ga_ref, ba_ref, wsm_ref, bst_ref):
    cdf_u = 0.5 * (1.0 + lax.erf(hu * INV_SQRT2))
    cdf_v = 0.5 * (1.0 + lax.erf(hv * INV_SQRT2))
    u = hu * cdf_u
    v = hv * cdf_v
    xhats, rstds, vns, svs = [], [], [], []
    for hd in range(HEADS):
        sl = slice(hd * HEAD_DIM, (hd + 1) * HEAD_DIM)
        xhat, rstd = _ln_stats(v[:, sl])
        vn = (xhat * ga_ref[hd:hd + 1, :] + ba_ref[hd:hd + 1, :]).astype(BF16)
        sv = _nn(wsm_ref[hd], vn) + bst_ref[:, hd:hd + 1]
        xhats.append(xhat)
        rstds.append(rstd)
        vns.append(vn)
        svs.append(sv)
    return u, cdf_u, cdf_v, xhats, rstds, vns, svs


def _shifted(win):
    n = win.shape[0]
    return [win] + [pltpu.roll(win, n - s, 0) for s in range(1, 8)]


def _tap(shifted, offset):
    s = offset % 8
    return shifted[s][offset - s:offset - s + ROWS, :]


def _conv_b_block(ext_ref, base, cw_ref):
    win = _shifted(ext_ref[pl.ds(base, ROWS + HALO_B), :])
    acc = jnp.zeros((ROWS, D_B), F32)
    for k in range(KB):
        acc = acc + _tap(win, 2 + k) * cw_ref[k:k + 1, :]
    return acc, win


def _taps_f(win):
    n = ROWS + HALO_F
    return [pltpu.roll(win, n - 6, 0)[0:ROWS, :], pltpu.roll(win, n - 7, 0)[0:ROWS, :], win[8:n, :]]


def _params(sem, **kw):
    return pltpu.CompilerParams(dimension_semantics=sem, vmem_limit_bytes=VMEM_LIMIT, **kw)


def _resident(shape):
    zeros = (0,) * len(shape)
    return pl.BlockSpec(shape, lambda *_: zeros, pipeline_mode=pl.Buffered(1))


def _full(shape):
    zeros = (0,) * len(shape)
    return pl.BlockSpec(shape, lambda *_: zeros)


def _mesh_pos():
    return lax.axis_index("x"), lax.axis_index("y"), lax.axis_index("c")


def _chip_patterns(x, y):
    return [(x, y), (1 - x, y), (x, 1 - y), (1 - x, 1 - y)]


def _lid(x, y, c):
    return 4 * x + 2 * y + c


def _gather_copy(outs, send_sems, recv_sems, a, k, block, to, src=None):
    blk = outs[a].at[_lid(*block)]
    return pltpu.make_async_remote_copy(
        src_ref=blk if src is None else src, dst_ref=blk,
        send_sem=send_sems.at[a, k], recv_sem=recv_sems.at[a, k], device_id=to, device_id_type=MESH)


def _gather_start(mine, outs, send_sems, recv_sems, local_sems):
    x, y, c = _mesh_pos()
    me, sib = (x, y, c), (x, y, 1 - c)
    for a in range(len(mine)):
        pltpu.make_async_copy(mine[a], outs[a].at[_lid(*me)], local_sems.at[a]).start()
        _gather_copy(outs, send_sems, recv_sems, a, 0, me, sib, src=mine[a]).start()
        for j, chip in enumerate(_chip_patterns(x, y)[1:]):
            _gather_copy(outs, send_sems, recv_sems, a, 1 + j, me, (*chip, c), src=mine[a]).start()


def _gather_finish(mine, outs, send_sems, recv_sems, local_sems):
    x, y, c = _mesh_pos()
    me, sib = (x, y, c), (x, y, 1 - c)
    chips = _chip_patterns(x, y)[1:]
    n = len(mine)
    copy = functools.partial(_gather_copy, outs, send_sems, recv_sems)
    passed = []
    for j, chip in enumerate(chips):
        for a in range(n):
            copy(a, 1 + j, (*chip, c), me).wait_recv()
            cp = copy(a, 4 + j, (*chip, c), sib)
            cp.start()
            passed.append(cp)
    for a in range(n):
        copy(a, 0, sib, me).wait_recv()
        for j, chip in enumerate(chips):
            copy(a, 4 + j, (*chip, 1 - c), me).wait_recv()
        for k in range(4):
            copy(a, k, me, sib, src=mine[a]).wait_send()
        pltpu.make_async_copy(mine[a], outs[a].at[_lid(*me)], local_sems.at[a]).wait()
    for cp in passed:
        cp.wait_send()


def _gather_scratch(n):
    return [pltpu.SemaphoreType.DMA((n, 7)), pltpu.SemaphoreType.DMA((n, 7)), pltpu.SemaphoreType.DMA((n,))]


def all_gather_mixer_weights(w_in, w_out, w_up, w_down, convp):
    srcs = [w_in, w_out, convp]
    n = len(srcs)

    def body(win_ref, wout_ref, convp_ref, wup_ref, wdown_ref,
             gin_ref, gout_ref, gconv_ref, sup_ref, sdown_ref,
             sin_ref, sout_ref, send_sems, recv_sems, local_sems):
        sin_ref[...] = win_ref[...].astype(BF16)
        sout_ref[...] = wout_ref[...].astype(BF16)
        mine = [sin_ref, sout_ref, convp_ref]
        outs = [gin_ref, gout_ref, gconv_ref]
        _gather_start(mine, outs, send_sems, recv_sems, local_sems)
        sup_ref[...] = wup_ref[...].astype(BF16)
        sdown_ref[...] = wdown_ref[...].astype(BF16)
        _gather_finish(mine, outs, send_sems, recv_sems, local_sems)

    return pl.pallas_call(
        body, name="all_gather_mixer_weights",
        out_shape=[jax.ShapeDtypeStruct((N_DEV,) + w_in.shape, BF16),
                   jax.ShapeDtypeStruct((N_DEV,) + w_out.shape, BF16),
                   jax.ShapeDtypeStruct((N_DEV,) + convp.shape, F32),
                   jax.ShapeDtypeStruct(w_up.shape, BF16), jax.ShapeDtypeStruct(w_down.shape, BF16)],
        in_specs=[VMEM] * 5, out_specs=[ANY] * n + [VMEM, VMEM],
        scratch_shapes=[pltpu.VMEM(w_in.shape, BF16), pltpu.VMEM(w_out.shape, BF16)] + _gather_scratch(n),
        compiler_params=pltpu.CompilerParams(vmem_limit_bytes=VMEM_LIMIT),
    )(w_in, w_out, convp, w_up, w_down)


def exchange_with_sibling(name, grads, svec=None):
    srcs = list(grads) + ([] if svec is None else [svec])
    n, n_all = len(grads), len(srcs)

    def body(*refs):
        src, land = refs[:n_all], refs[n_all:2 * n_all]
        send_sems, recv_sems = refs[2 * n_all:]
        x, y, c = _mesh_pos()
        sib = (x, y, 1 - c)
        copies = []
        for a in range(n):
            for k, (px, py) in enumerate(_chip_patterns(x, y)):
                copies.append(pltpu.make_async_remote_copy(
                    src_ref=src[a].at[_lid(px, py, 1 - c)], dst_ref=land[a].at[k],
                    send_sem=send_sems.at[a, k], recv_sem=recv_sems.at[a, k],
                    device_id=sib, device_id_type=MESH))
        if svec is not None:
            copies.append(pltpu.make_async_remote_copy(
                src_ref=src[n], dst_ref=land[n], send_sem=send_sems.at[n, 0], recv_sem=recv_sems.at[n, 0],
                device_id=sib, device_id_type=MESH))
        for cp in copies:
            cp.start()
        for cp in copies:
            cp.wait()

    return pl.pallas_call(
        body, name=name,
        out_shape=[jax.ShapeDtypeStruct((4,) + g.shape[1:], F32) for g in grads]
        + ([] if svec is None else [jax.ShapeDtypeStruct(svec.shape, F32)]),
        in_specs=[ANY] * n_all, out_specs=[ANY] * n_all,
        scratch_shapes=[pltpu.SemaphoreType.DMA((n_all, 4)), pltpu.SemaphoreType.DMA((n_all, 4))],
    )(*srcs)


def _chip_copies(p, land, send_sems, recv_sems):
    x, y, c = _mesh_pos()
    return [pltpu.make_async_remote_copy(
        src_ref=p[a].at[k], dst_ref=land[a].at[k], send_sem=send_sems.at[a, k], recv_sem=recv_sems.at[a, k],
        device_id=(px, py, c), device_id_type=MESH)
        for k, (px, py) in enumerate(_chip_patterns(x, y)[1:]) for a in range(len(p))]


def exchange_between_chips(partials, svec, sv_land):
    n = len(partials)

    def body(*refs):
        p = refs[:n]
        sv_ref, svl_ref = refs[n], refs[n + 1]
        land = refs[n + 2:2 * n + 2]
        sv_slots = refs[2 * n + 2]
        chip_sv, send_sems, recv_sems, sv_send, sv_recv, local_sem = refs[2 * n + 3:]
        x, y, c = _mesh_pos()
        q = 2 * x + y
        chip_sv[...] = sv_ref[...] + svl_ref[...]
        local = pltpu.make_async_copy(chip_sv, sv_slots.at[q], local_sem)
        local.start()
        copies = _chip_copies(p, land, send_sems, recv_sems)
        for cp in copies:
            cp.start()
        sv_copies = []
        for k, (px, py) in enumerate(_chip_patterns(x, y)[1:]):
            cp = pltpu.make_async_remote_copy(
                src_ref=chip_sv, dst_ref=sv_slots.at[q],
                send_sem=sv_send.at[k], recv_sem=sv_recv.at[k],
                device_id=(px, py, c), device_id_type=MESH)
            cp.start()
            sv_copies.append(cp)
        for cp in copies:
            cp.wait()
        for k, (px, py) in enumerate(_chip_patterns(x, y)[1:]):
            sv_copies[k].wait_send()
            pltpu.make_async_remote_copy(
                src_ref=chip_sv, dst_ref=sv_slots.at[2 * px + py],
                send_sem=sv_send.at[k], recv_sem=sv_recv.at[k],
                device_id=(px, py, c), device_id_type=MESH).wait_recv()
        local.wait()

    return pl.pallas_call(
        body, name="exchange_between_chips",
        out_shape=[jax.ShapeDtypeStruct(p.shape, BF16) for p in partials]
        + [jax.ShapeDtypeStruct((4,) + svec.shape, F32)],
        in_specs=[ANY] * n + [VMEM, VMEM], out_specs=[ANY] * (n + 1),
        scratch_shapes=[pltpu.VMEM(svec.shape, F32),
                        pltpu.SemaphoreType.DMA((n, 3)), pltpu.SemaphoreType.DMA((n, 3)),
                        pltpu.SemaphoreType.DMA((3,)), pltpu.SemaphoreType.DMA((3,)),
                        pltpu.SemaphoreType.DMA],
    )(*partials, svec, sv_land)


def chip_partials(name, g, land, jidx, rb):
    _, r, c = g.shape

    def body(j_ref, g_ref, l_ref, o_ref):
        o_ref[...] = (g_ref[...] + l_ref[...]).astype(BF16)

    return pl.pallas_call(
        body, name=name,
        out_shape=jax.ShapeDtypeStruct((3, r, c), BF16),
        grid_spec=pltpu.PrefetchScalarGridSpec(
            num_scalar_prefetch=1, grid=(3, r // rb),
            in_specs=[pl.BlockSpec((1, rb, c), lambda k, i, j: (j[1 + k], i, 0)),
                      pl.BlockSpec((1, rb, c), lambda k, i, j: (1 + k, i, 0))],
            out_specs=pl.BlockSpec((1, rb, c), lambda k, i, j: (k, i, 0))),
        compiler_params=_params(("arbitrary", "arbitrary")),
    )(jidx, g, land)


def _adamw(w, g, m, v):
    m2 = ADAM_B1 * m + (1.0 - ADAM_B1) * g
    v2 = ADAM_B2 * v + (1.0 - ADAM_B2) * (g * g)
    m_hat = m2 / (1.0 - ADAM_B1 ** ADAM_STEP)
    v_hat = v2 / (1.0 - ADAM_B2 ** ADAM_STEP)
    delta = -ADAM_LR * (m_hat / (jnp.sqrt(v_hat) + ADAM_EPS) + ADAM_WD * w)
    return delta, m2, v2


def reduce_and_adamw(name, g, land, recv, w, m, v, jidx, rb):
    _, r, c = g.shape

    def body(j_ref, g_ref, l_ref, r_ref, w_ref, m_ref, v_ref, go_ref, do_ref, mo_ref, vo_ref):
        grad = (g_ref[0] + l_ref[0]) + r_ref[0].astype(F32) + r_ref[1].astype(F32) + r_ref[2].astype(F32)
        delta, m2, v2 = _adamw(w_ref[...], grad, m_ref[...], v_ref[...])
        go_ref[...] = grad
        do_ref[...] = delta
        mo_ref[...] = m2
        vo_ref[...] = v2

    blk = pl.BlockSpec((rb, c), lambda i, j: (i, 0))
    return pl.pallas_call(
        body, name=name,
        out_shape=[jax.ShapeDtypeStruct((r, c), F32)] * 4,
        grid_spec=pltpu.PrefetchScalarGridSpec(
            num_scalar_prefetch=1, grid=(r // rb,),
            in_specs=[pl.BlockSpec((1, rb, c), lambda i, j: (j[0], i, 0)),
                      pl.BlockSpec((1, rb, c), lambda i, j: (0, i, 0)),
                      pl.BlockSpec((3, rb, c), lambda i, j: (0, i, 0)),
                      blk, blk, blk],
            out_specs=[blk] * 4),
        compiler_params=_params(("arbitrary",)),
    )(jidx, g, land, recv, w, m, v)


SMALL_LAYOUT = [
    ("b_in", S_BIN, 16), ("ln_a_g", S_LNAG, 4), ("ln_a_b", S_LNAB, 4), ("w_spatial", S_WS, 512),
    ("b_spatial", S_BS, 4), ("conv_b_b", S_CBB, 4), ("ln_b_g", S_LNBG, 4), ("ln_b_b", S_LNBB, 4),
    ("b_out", S_BOUT, 8), ("ln1_g", S_LN1G, 8), ("ln1_b", S_LN1B, 8), ("conv_f_b", S_CFB, 44),
    ("ln2_g", S_LN2G, 8), ("ln2_b", S_LN2B, 8),
]


def small_adamw(sv_slots, ws, ms, vs):
    n = len(SMALL_LAYOUT)

    def body(*refs):
        s_ref = refs[0]
        w_refs, m_refs, v_refs = refs[1:1 + n], refs[1 + n:1 + 2 * n], refs[1 + 2 * n:1 + 3 * n]
        outs = refs[1 + 3 * n:]
        for p, (_, row0, rows) in enumerate(SMALL_LAYOUT):
            sl = pl.ds(row0, rows)
            grad = ((s_ref[0, sl, :] + s_ref[1, sl, :]) + s_ref[2, sl, :]) + s_ref[3, sl, :]
            delta, m2, v2 = _adamw(w_refs[p][...], grad, m_refs[p][...], v_refs[p][...])
            outs[p][...] = grad
            outs[n + p][...] = delta
            outs[2 * n + p][...] = m2
            outs[3 * n + p][...] = v2
        sl = pl.ds(S_LOSS, 8)
        outs[4 * n][...] = ((s_ref[0, sl, :] + s_ref[1, sl, :]) + s_ref[2, sl, :]) + s_ref[3, sl, :]

    shapes = [jax.ShapeDtypeStruct((rows, 128), F32) for _, _, rows in SMALL_LAYOUT]
    return pl.pallas_call(
        body, name="small_adamw", out_shape=shapes * 4 + [jax.ShapeDtypeStruct((8, 128), F32)],
        in_specs=[VMEM] * (1 + 3 * n), out_specs=[VMEM] * (4 * n + 1),
    )(sv_slots, *ws, *ms, *vs)


def mix_forward(x, win_g, b_in, ln_a_g, ln_a_b, w_spatial, bst, conv_b_w, conv_b_b, ln_b_g, ln_b_b,
                wout, b_out, ln1_g, ln1_b, sup, sdown, tm):
    t = x.shape[0]
    nt = t // tm
    n_chunks = tm // CHUNK

    def body(x_ref, win_ref, bin_ref, ga_ref, ba_ref, ws_ref, bst_ref, cw_ref, cb_ref, gb_ref,
             bb_ref, wout_ref, bout_ref, g1_ref, b1_ref, sup_ref, sdown_ref,
             h_ref, xhat1_ref, rstd1_ref, yb1_ref, gup_ref, gdown_ref,
             ext_ref, y_ref, wsm_ref, send_sems, recv_sems, local_sems):
        i = pl.program_id(0)
        gather = ([sup_ref, sdown_ref], [gup_ref, gdown_ref], send_sems, recv_sems, local_sems)

        @pl.when(i == 0)
        def _():
            _gather_start(*gather)
            ext_ref[0:HALO_B, :] = jnp.zeros((HALO_B, D_B), F32)
            mask = _tril_mask()
            for hd in range(HEADS):
                wsm_ref[hd] = jnp.where(mask, ws_ref[hd], 0.0).astype(BF16)

        xb = x_ref[...].astype(BF16)
        for j in range(N_DEV):
            cols = slice(j * W_IN_BLK, (j + 1) * W_IN_BLK)
            h_ref[:, cols] = _nn(xb, win_ref[j]) + bin_ref[:, cols]

        def chunk(ci):
            r = _rows(ci, CHUNK)
            u, _, _, _, _, _, svs = _mixer_a_fwd(h_ref[r, 0:D_A], h_ref[r, D_A:2 * D_A],
                                                 ga_ref, ba_ref, wsm_ref, bst_ref)
            for hd in range(HEADS):
                sl = slice(hd * HEAD_DIM, (hd + 1) * HEAD_DIM)
                y_ref[r, sl] = (u[:, sl] * svs[hd]).astype(BF16)
            a_b = h_ref[r, 2 * D_A:2 * D_A + D_B]
            g_b = h_ref[r, 2 * D_A + D_B:D_IN]
            ext_ref[pl.ds(HALO_B + ci * CHUNK, CHUNK), :] = a_b * _sigmoid(g_b)

        _loop(n_chunks, chunk)

        def conv_rows(bi):
            base = bi * ROWS
            acc, _ = _conv_b_block(ext_ref, base, cw_ref)
            yb1 = acc + cb_ref[...]
            yb1_ref[pl.ds(base, ROWS), :] = yb1
            xhat, _ = _ln_stats(yb1)
            yb2 = xhat * gb_ref[...] + bb_ref[...]
            y_ref[pl.ds(base, ROWS), D_A:D] = (yb2 * _sigmoid(yb2)).astype(BF16)

        _loop(tm // ROWS, conv_rows)
        ext_ref[0:HALO_B, :] = ext_ref[tm:tm + HALO_B, :]

        mix = _nn(y_ref[...], wout_ref[...]) + bout_ref[...]
        xhat1, rstd1 = _ln_stats(ALPHA * x_ref[...] + mix)
        xhat1_ref[...] = xhat1
        rstd1_ref[...] = jnp.broadcast_to(rstd1, (tm, 128))

        @pl.when(i == nt - 1)
        def _():
            _gather_finish(*gather)

    row = lambda w: pl.BlockSpec((tm, w), lambda i: (i, 0))
    return pl.pallas_call(
        body, name="mix_forward", grid=(nt,),
        in_specs=[row(D), _resident(win_g.shape), _full(b_in.shape), _full(ln_a_g.shape),
                  _full(ln_a_b.shape), _full(w_spatial.shape), _full(bst.shape),
                  _full(conv_b_w.shape), _full(conv_b_b.shape), _full(ln_b_g.shape),
                  _full(ln_b_b.shape), _resident(wout.shape), _full(b_out.shape),
                  _full(ln1_g.shape), _full(ln1_b.shape), ANY, ANY],
        out_specs=[row(D_IN), row(D), row(128), row(D_B), ANY, ANY],
        out_shape=[jax.ShapeDtypeStruct((t, D_IN), F32), jax.ShapeDtypeStruct((t, D), F32),
                   jax.ShapeDtypeStruct((t, 128), F32), jax.ShapeDtypeStruct((t, D_B), F32),
                   jax.ShapeDtypeStruct((N_DEV,) + sup.shape, BF16),
                   jax.ShapeDtypeStruct((N_DEV,) + sdown.shape, BF16)],
        scratch_shapes=[pltpu.VMEM((tm + HALO_B, D_B), F32), pltpu.VMEM((tm, D), BF16),
                        pltpu.VMEM((HEADS, CHUNK, CHUNK), BF16)] + _gather_scratch(2),
        compiler_params=_params(("arbitrary",)),
    )(x, win_g, b_in, ln_a_g, ln_a_b, w_spatial, bst, conv_b_w, conv_b_b, ln_b_g, ln_b_b,
      wout, b_out, ln1_g, ln1_b, sup, sdown)


def ffn_forward(xhat1, ln1_g, ln1_b, wup_g, cfw, cfb, wdown, ln2_g, ln2_b, target, tm):
    t = xhat1.shape[0]
    nt = t // tm

    def body(xh_ref, g1_ref, b1_ref, wup_ref, cfw_ref, cfb_ref, wdown_ref, g2_ref, b2_ref, tgt_ref,
             hu_ref, gv_ref, dr2_ref, loss_ref, sln2_ref,
             x1_ref, x1b_ref, carry_ref, gbuf_ref, ffn_ref, acc_loss, acc_g2, acc_b2):
        i = pl.program_id(0)

        @pl.when(i == 0)
        def _():
            carry_ref[...] = jnp.zeros(carry_ref.shape, F32)
            acc_loss[...] = jnp.zeros(acc_loss.shape, F32)
            acc_g2[...] = jnp.zeros(acc_g2.shape, F32)
            acc_b2[...] = jnp.zeros(acc_b2.shape, F32)

        x1 = xh_ref[...] * g1_ref[...] + b1_ref[...]
        x1_ref[...] = x1
        x1b_ref[...] = x1.astype(BF16)

        def conv(j, base):
            if base == 0:
                win = jnp.concatenate([carry_ref[j], hu_ref[j, 0:ROWS, :]], axis=0)
            else:
                win = hu_ref[j, base - HALO_F:base + ROWS, :]
            taps = _taps_f(win)
            w = cfw_ref[j]
            return sum(taps[k] * w[k:k + 1, :] for k in range(KF)) + cfb_ref[j:j + 1, :]

        for f in range(N_F):
            hu_ref[f] = _nn(x1b_ref[...], wup_ref[f])
            hu_ref[N_F + f] = _nn(x1b_ref[...], wup_ref[N_F + f])

            def rows(bi, f=f):
                gate = conv(f, bi * ROWS)
                val = conv(N_F + f, bi * ROWS)
                gbuf_ref[_rows(bi), :] = (gate * _sigmoid(gate) * val).astype(BF16)
                gv_ref[f, _rows(bi), :] = gate.astype(BF16)
                gv_ref[N_F + f, _rows(bi), :] = val.astype(BF16)

            _loop(tm // ROWS, rows)
            carry_ref[f] = hu_ref[f, tm - HALO_F:tm, :]
            carry_ref[N_F + f] = hu_ref[N_F + f, tm - HALO_F:tm, :]
            part = _nn(gbuf_ref[...], wdown_ref[f])
            if f == 0:
                ffn_ref[...] = part
            else:
                ffn_ref[...] += part

        def tail(bi):
            r = _rows(bi)
            xhat2, rstd2 = _ln_stats(ALPHA * x1_ref[r, :] + ffn_ref[r, :])
            err = xhat2 * g2_ref[...] + b2_ref[...] - tgt_ref[r, :]
            e2 = _rsum8(err * err)
            acc_loss[...] += sum(e2[:, k * 128:(k + 1) * 128] for k in range(D // 128))
            dy = err * (1.0 / D)
            acc_g2[...] += _rsum8(dy * xhat2)
            acc_b2[...] += _rsum8(dy)
            dr2_ref[r, :] = _ln_bwd(dy * g2_ref[...], xhat2, rstd2)

        _loop(tm // ROWS, tail)
        loss_ref[...] = acc_loss[...]

        @pl.when(i == nt - 1)
        def _():
            dg = jnp.sum(acc_g2[...], axis=0, keepdims=True)
            db = jnp.sum(acc_b2[...], axis=0, keepdims=True)
            for k in range(D // 128):
                sln2_ref[k:k + 1, :] = dg[:, k * 128:(k + 1) * 128]
                sln2_ref[8 + k:9 + k, :] = db[:, k * 128:(k + 1) * 128]

    row = pl.BlockSpec((tm, D), lambda i: (i, 0))
    return pl.pallas_call(
        body, name="ffn_forward", grid=(nt,),
        in_specs=[row, _full(ln1_g.shape), _full(ln1_b.shape), _resident(wup_g.shape),
                  _full(cfw.shape), _full(cfb.shape), _resident(wdown.shape),
                  _full(ln2_g.shape), _full(ln2_b.shape), row],
        out_specs=[pl.BlockSpec((N_DEV, tm, W_UP_BLK), lambda i: (0, i, 0)),
                   pl.BlockSpec((N_DEV, tm, W_UP_BLK), lambda i: (0, i, 0)), row,
                   _full((8, 128)), _full((16, 128))],
        out_shape=[jax.ShapeDtypeStruct((N_DEV, t, W_UP_BLK), F32),
                   jax.ShapeDtypeStruct((N_DEV, t, W_UP_BLK), BF16), jax.ShapeDtypeStruct((t, D), F32),
                   jax.ShapeDtypeStruct((8, 128), F32), jax.ShapeDtypeStruct((16, 128), F32)],
        scratch_shapes=[pltpu.VMEM((tm, D), F32), pltpu.VMEM((tm, D), BF16),
                        pltpu.VMEM((N_DEV, HALO_F, W_UP_BLK), F32), pltpu.VMEM((tm, W_UP_BLK), BF16),
                        pltpu.VMEM((tm, D), F32), pltpu.VMEM((8, 128), F32),
                        pltpu.VMEM((8, D), F32), pltpu.VMEM((8, D), F32)],
        compiler_params=_params(("arbitrary",)),
    )(xhat1, ln1_g, ln1_b, wup_g, cfw, cfb, wdown, ln2_g, ln2_b, target)


def ffn_backward(dr2, xhat1, ln1_g, ln1_b, hu, gv, wup_g, cfw, wdown, tm):
    t = dr2.shape[0]
    nt = t // tm
    sub_rows = tm
    hu4 = hu.reshape(2, N_F, t, W_UP_BLK)
    gv4 = gv.reshape(2, N_F, t, W_UP_BLK)
    wup4 = wup_g.reshape(2, N_F, D, W_UP_BLK)
    cfw4 = cfw.reshape(2, N_F, KF, W_UP_BLK)

    def body(dr2_ref, xh_ref, g1_ref, b1_ref, hu_ref, gv_ref, wup_ref, cfw_ref, wdown_ref,
             dwup_ref, dwdown_ref, dcfw_ref, dcfb_ref, dx1_ref,
             x1b_ref, drb_ref, dg_ref, dextg_ref, dextv_ref, gbuf_ref,
             dhug_ref, dhuv_ref, acc_wup, acc_wdown, acc_cfw, acc_cfb, sem):
        f = pl.program_id(0)
        i = pl.program_id(1)

        @pl.when(i == 0)
        def _():
            acc_wup[...] = jnp.zeros(acc_wup.shape, F32)
            acc_wdown[...] = jnp.zeros(acc_wdown.shape, F32)
            acc_cfw[...] = jnp.zeros(acc_cfw.shape, F32)
            acc_cfb[...] = jnp.zeros(acc_cfb.shape, F32)
            dextg_ref[tm:tm + HALO_F, :] = jnp.zeros((HALO_F, W_UP_BLK), F32)
            dextv_ref[tm:tm + HALO_F, :] = jnp.zeros((HALO_F, W_UP_BLK), F32)

        w = [cfw_ref[0, 0], cfw_ref[1, 0]]
        dext = [dextg_ref, dextv_ref]
        dhu = [dhug_ref, dhuv_ref]

        def rows1(bi):
            r = _rows(bi)
            gate = gv_ref[0, 0, r, :].astype(F32)
            val = gv_ref[1, 0, r, :].astype(F32)
            sg = _sigmoid(gate)
            silu = gate * sg
            gbuf_ref[r, :] = (silu * val).astype(BF16)
            dg = dg_ref[r, :]
            dgate = dg * val * (sg * (1.0 + gate * (1.0 - sg)))
            dval = dg * silu
            dextg_ref[r, :] = dgate
            dextv_ref[r, :] = dval
            acc_cfb[0:8, :] += _rsum8(dgate)
            acc_cfb[8:16, :] += _rsum8(dval)

        def rows2(bi):
            r = _rows(bi)
            for g in range(2):
                win = dext[g][pl.ds(bi * ROWS, ROWS + HALO_F), :]
                n = ROWS + HALO_F
                later = [pltpu.roll(win, n - 2, 0)[0:ROWS, :], pltpu.roll(win, n - 1, 0)[0:ROWS, :],
                         win[0:ROWS, :]]
                d = sum(later[k] * w[g][k:k + 1, :] for k in range(KF))
                dhu[g][r, :] = d.astype(BF16)
                pre = hu_ref[g, 0, r, :]
                for k in range(KF):
                    r0 = 8 * (g * KF + k)
                    acc_cfw[r0:r0 + 8, :] += _rsum8(later[k] * pre)

        for sub in reversed(range(tm // sub_rows)):
            rs = slice(sub * sub_rows, (sub + 1) * sub_rows)
            blocks = range(sub * sub_rows // ROWS, (sub + 1) * sub_rows // ROWS)
            x1b_ref[rs, :] = (xh_ref[rs, :] * g1_ref[...] + b1_ref[...]).astype(BF16)
            drb_ref[rs, :] = dr2_ref[rs, :].astype(BF16)
            dg_ref[rs, :] = _nt(drb_ref[rs, :], wdown_ref[0])
            for bi in blocks:
                rows1(bi)
            for bi in blocks:
                rows2(bi)
            acc_wdown[...] += _tn(gbuf_ref[rs, :], drb_ref[rs, :])
            x1t = x1b_ref[rs, :].T
            acc_wup[0] += _nn(x1t, dhug_ref[rs, :])
            acc_wup[1] += _nn(x1t, dhuv_ref[rs, :])
            dx1_ref[0, rs, :] = (_nt(dhug_ref[rs, :], wup_ref[0, 0])
                                 + _nt(dhuv_ref[rs, :], wup_ref[1, 0])).astype(BF16)
        dextg_ref[tm:tm + HALO_F, :] = dextg_ref[0:HALO_F, :]
        dextv_ref[tm:tm + HALO_F, :] = dextv_ref[0:HALO_F, :]

        @pl.when(i == nt - 1)
        def _():
            for g in range(2):
                dcfb_ref[g, 0] = jnp.sum(acc_cfb[8 * g:8 * g + 8, :], axis=0, keepdims=True)
                for k in range(KF):
                    r0 = 8 * (g * KF + k)
                    dcfw_ref[g, 0, k:k + 1, :] = jnp.sum(acc_cfw[r0:r0 + 8, :], axis=0, keepdims=True)
            cps = [pltpu.make_async_copy(acc_wup.at[0], dwup_ref.at[0, f], sem.at[0]),
                   pltpu.make_async_copy(acc_wup.at[1], dwup_ref.at[1, f], sem.at[1]),
                   pltpu.make_async_copy(acc_wdown, dwdown_ref.at[f], sem.at[2])]
            for cp in cps:
                cp.start()
            for cp in cps:
                cp.wait()

    rev = lambda i: nt - 1 - i
    row = pl.BlockSpec((tm, D), lambda f, i: (rev(i), 0))
    pair = lambda r, c: pl.BlockSpec((2, 1, r, c), lambda f, i: (0, f, 0, 0))
    return pl.pallas_call(
        body, name="ffn_backward", grid=(N_F, nt),
        in_specs=[row, row, _full(ln1_g.shape), _full(ln1_b.shape),
                  pl.BlockSpec((2, 1, tm, W_UP_BLK), lambda f, i: (0, f, rev(i), 0)),
                  pl.BlockSpec((2, 1, tm, W_UP_BLK), lambda f, i: (0, f, rev(i), 0)),
                  pair(D, W_UP_BLK), pair(KF, W_UP_BLK),
                  pl.BlockSpec((1, W_UP_BLK, D), lambda f, i: (f, 0, 0))],
        out_specs=[ANY, ANY, pair(KF, W_UP_BLK), pair(1, W_UP_BLK),
                   pl.BlockSpec((1, tm, D), lambda f, i: (f, rev(i), 0))],
        out_shape=[jax.ShapeDtypeStruct((2, N_F, D, W_UP_BLK), F32),
                   jax.ShapeDtypeStruct((N_F, W_UP_BLK, D), F32),
                   jax.ShapeDtypeStruct((2, N_F, KF, W_UP_BLK), F32),
                   jax.ShapeDtypeStruct((2, N_F, 1, W_UP_BLK), F32),
                   jax.ShapeDtypeStruct((N_F, t, D), BF16)],
        scratch_shapes=[pltpu.VMEM((tm, D), BF16), pltpu.VMEM((tm, D), BF16),
                        pltpu.VMEM((tm, W_UP_BLK), F32),
                        pltpu.VMEM((tm + HALO_F, W_UP_BLK), F32), pltpu.VMEM((tm + HALO_F, W_UP_BLK), F32),
                        pltpu.VMEM((tm, W_UP_BLK), BF16), pltpu.VMEM((tm, W_UP_BLK), BF16),
                        pltpu.VMEM((tm, W_UP_BLK), BF16),
                        pltpu.VMEM((2, D, W_UP_BLK), F32), pltpu.VMEM((W_UP_BLK, D), F32),
                        pltpu.VMEM((2 * KF * 8, W_UP_BLK), F32), pltpu.VMEM((16, W_UP_BLK), F32),
                        pltpu.SemaphoreType.DMA((3,))],
        compiler_params=_params(("arbitrary", "arbitrary")),
    )(dr2, xhat1, ln1_g, ln1_b, hu4, gv4, wup4, cfw4, wdown)


def mix_backward(x, h, yb1, dx1p, dr2, xhat1, rstd1, win_g, ln_a_g, ln_a_b, w_spatial, bst,
                 conv_b_w, conv_b_b, ln_b_g, ln_b_b, wout, ln1_g, ffn_partials, tm):
    t = x.shape[0]
    n_p = len(ffn_partials)
    nt = t // tm
    n_chunks = tm // CHUNK
    halo_blocks = tm // HALO_B

    def body(x_ref, h_ref, halo_ref, yb1_ref, dx1p_ref, dr2_ref, xh1_ref, rstd1_ref, win_ref, ga_ref, ba_ref,
             ws_ref, bst_ref, cw_ref, cb_ref, gb_ref, bb_ref, wout_ref, g1_ref, *rest):
        p_refs, rest = rest[:n_p], rest[n_p:]
        gx_ref, dwin_ref, dwout_ref, dcw_ref, small_ref = rest[:5]
        land_refs, rest = rest[5:5 + n_p], rest[5 + n_p:]
        (ext_ref, dext_ref, y_ref, dy_ref, dh_ref, dmb_ref, wsm_ref,
         acc_win, acc_wout, acc_bin, acc_lnag, acc_lnab, acc_ws, acc_bs, acc_cbb, acc_lnbg,
         acc_lnbb, acc_bout, acc_ln1g, acc_ln1b, acc_cw, sem, send_sems, recv_sems) = rest
        i = pl.program_id(0)

        @pl.when(i == 0)
        def _():
            for cp in _chip_copies(p_refs, land_refs, send_sems, recv_sems):
                cp.start()

        first_tile = i == nt - 1
        accs = [acc_win, acc_wout, acc_bin, acc_lnag, acc_lnab, acc_ws, acc_bs, acc_cbb, acc_lnbg,
                acc_lnbb, acc_bout, acc_ln1g, acc_ln1b, acc_cw]

        @pl.when(i == 0)
        def _():
            for acc in accs:
                acc[...] = jnp.zeros(acc.shape, F32)
            dext_ref[tm:tm + HALO_B, :] = jnp.zeros((HALO_B, D_B), F32)
            mask = _tril_mask()
            for hd in range(HEADS):
                wsm_ref[hd] = jnp.where(mask, ws_ref[hd], 0.0).astype(BF16)

        def ln1_rows(bi):
            r = _rows(bi)
            part = [dx1p_ref[f, r, :].astype(F32) for f in range(N_F)]
            dx1 = ALPHA * dr2_ref[r, :] + ((part[0] + part[1]) + (part[2] + part[3]))
            xhat = xh1_ref[r, :]
            acc_ln1g[...] += _rsum8(dx1 * xhat)
            acc_ln1b[...] += _rsum8(dx1)
            dr1 = _ln_bwd(dx1 * g1_ref[...], xhat, rstd1_ref[r, 0:1])
            acc_bout[...] += _rsum8(dr1)
            gx_ref[r, :] = ALPHA * dr1
            dmb_ref[r, :] = dr1.astype(BF16)

        _loop(tm // ROWS, ln1_rows)
        dy_ref[...] = _nt(dmb_ref[...], wout_ref[...])

        ha = halo_ref[:, 0:D_B]
        hg = halo_ref[:, D_B:2 * D_B]
        ext_ref[0:HALO_B, :] = jnp.where(first_tile, 0.0, 1.0) * (ha * _sigmoid(hg))

        def chunk(ci):
            r = _rows(ci, CHUNK)
            hu, hv = h_ref[r, 0:D_A], h_ref[r, D_A:2 * D_A]
            u, cdf_u, cdf_v, xhats, rstds, vns, svs = _mixer_a_fwd(hu, hv, ga_ref, ba_ref, wsm_ref, bst_ref)
            for hd in range(HEADS):
                sl = slice(hd * HEAD_DIM, (hd + 1) * HEAD_DIM)
                rows8 = slice(8 * hd, 8 * hd + 8)
                dy_a = dy_ref[r, sl]
                y_ref[r, sl] = (u[:, sl] * svs[hd]).astype(BF16)
                du = dy_a * svs[hd]
                dsv = dy_a * u[:, sl]
                dsvb = dsv.astype(BF16)
                acc_bs[hd] += dsv
                acc_ws[hd] += _nt(dsvb, vns[hd])
                dvn = _tn(wsm_ref[hd], dsvb)
                acc_lnag[rows8, :] += _rsum8(dvn * xhats[hd])
                acc_lnab[rows8, :] += _rsum8(dvn)
                dv = _ln_bwd(dvn * ga_ref[hd:hd + 1, :], xhats[hd], rstds[hd])
                hus, hvs = hu[:, sl], hv[:, sl]
                slv = slice(D_A + hd * HEAD_DIM, D_A + (hd + 1) * HEAD_DIM)
                dhu = du * (cdf_u[:, sl] + hus * jnp.exp(-0.5 * hus * hus) * INV_SQRT_2PI)
                dhv = dv * (cdf_v[:, sl] + hvs * jnp.exp(-0.5 * hvs * hvs) * INV_SQRT_2PI)
                acc_bin[:, sl] += _rsum8(dhu)
                acc_bin[:, slv] += _rsum8(dhv)
                dh_ref[r, sl] = dhu.astype(BF16)
                dh_ref[r, slv] = dhv.astype(BF16)
            a_b = h_ref[r, 2 * D_A:2 * D_A + D_B]
            g_b = h_ref[r, 2 * D_A + D_B:D_IN]
            ext_ref[pl.ds(HALO_B + ci * CHUNK, CHUNK), :] = a_b * _sigmoid(g_b)

        _loop(n_chunks, chunk)

        def conv_rows(bi):
            base = bi * ROWS
            r = pl.ds(base, ROWS)
            win = _shifted(ext_ref[pl.ds(base, ROWS + HALO_B), :])
            xhat, rstd = _ln_stats(yb1_ref[r, :])
            yb2 = xhat * gb_ref[...] + bb_ref[...]
            sg = _sigmoid(yb2)
            y_ref[r, D_A:D] = (yb2 * sg).astype(BF16)
            dyb2 = dy_ref[r, D_A:D] * (sg * (1.0 + yb2 * (1.0 - sg)))
            acc_lnbg[...] += _rsum8(dyb2 * xhat)
            acc_lnbb[...] += _rsum8(dyb2)
            dyb1 = _ln_bwd(dyb2 * gb_ref[...], xhat, rstd)
            acc_cbb[...] += _rsum8(dyb1)
            dext_ref[r, :] = dyb1
            for k in range(KB):
                acc_cw[8 * k:8 * k + 8, :] += _rsum8(dyb1 * _tap(win, 2 + k))

        _loop(tm // ROWS, conv_rows)

        def convt_rows(bi):
            base = bi * ROWS
            r = pl.ds(base, ROWS)
            dwin = _shifted(dext_ref[pl.ds(base, ROWS + HALO_B), :])
            dyb0 = jnp.zeros((ROWS, D_B), F32)
            for k in range(KB):
                dyb0 = dyb0 + _tap(dwin, 30 - k) * cw_ref[k:k + 1, :]
            a_b = h_ref[r, 2 * D_A:2 * D_A + D_B]
            sg = _sigmoid(h_ref[r, 2 * D_A + D_B:D_IN])
            da_b = dyb0 * sg
            dg_b = dyb0 * a_b * sg * (1.0 - sg)
            acc_bin[:, 2 * D_A:2 * D_A + D_B] += _rsum8(da_b)
            acc_bin[:, 2 * D_A + D_B:D_IN] += _rsum8(dg_b)
            dh_ref[r, 2 * D_A:2 * D_A + D_B] = da_b.astype(BF16)
            dh_ref[r, 2 * D_A + D_B:D_IN] = dg_b.astype(BF16)

        _loop(tm // ROWS, convt_rows)
        dext_ref[tm:tm + HALO_B, :] = dext_ref[0:HALO_B, :]

        acc_wout[...] += _tn(y_ref[...], dmb_ref[...])
        xt = x_ref[...].T.astype(BF16)
        dh_blocks = [dh_ref[:, j * W_IN_BLK:(j + 1) * W_IN_BLK] for j in range(N_DEV)]
        for j in range(N_DEV):
            acc_win[j] += _nn(xt, dh_blocks[j])
        gx_ref[...] += sum(_nt(dh_blocks[j], win_ref[j]) for j in range(N_DEV))

        @pl.when(i == nt - 1)
        def _():
            cps = [pltpu.make_async_copy(acc_win, dwin_ref, sem.at[0]),
                   pltpu.make_async_copy(acc_wout, dwout_ref, sem.at[1])]
            for cp in cps:
                cp.start()
            small_ref[...] = jnp.zeros(small_ref.shape, F32)

            def put_row_vector(row0, acc):
                vec = jnp.sum(acc[...], axis=0, keepdims=True)
                for k in range(vec.shape[1] // 128):
                    small_ref[row0 + k:row0 + k + 1, :] = vec[:, k * 128:(k + 1) * 128]

            put_row_vector(S_BIN, acc_bin)
            put_row_vector(S_CBB, acc_cbb)
            put_row_vector(S_LNBG, acc_lnbg)
            put_row_vector(S_LNBB, acc_lnbb)
            put_row_vector(S_BOUT, acc_bout)
            put_row_vector(S_LN1G, acc_ln1g)
            put_row_vector(S_LN1B, acc_ln1b)
            mask = _tril_mask()
            for hd in range(HEADS):
                rows8 = slice(8 * hd, 8 * hd + 8)
                small_ref[S_LNAG + hd:S_LNAG + hd + 1, :] = jnp.sum(acc_lnag[rows8, :], axis=0, keepdims=True)
                small_ref[S_LNAB + hd:S_LNAB + hd + 1, :] = jnp.sum(acc_lnab[rows8, :], axis=0, keepdims=True)
                small_ref[S_WS + hd * CHUNK:S_WS + (hd + 1) * CHUNK, :] = jnp.where(mask, acc_ws[hd], 0.0)
                small_ref[S_BS + hd:S_BS + hd + 1, :] = jnp.sum(acc_bs[hd].T, axis=0, keepdims=True)
            for k in range(KB):
                dcw_ref[k:k + 1, :] = jnp.sum(acc_cw[8 * k:8 * k + 8, :], axis=0, keepdims=True)
            for cp in cps:
                cp.wait()
            for cp in _chip_copies(p_refs, land_refs, send_sems, recv_sems):
                cp.wait()

    rev = lambda i: nt - 1 - i
    row = lambda w: pl.BlockSpec((tm, w), lambda i: (rev(i), 0))
    return pl.pallas_call(
        body, name="mix_backward", grid=(nt,),
        in_specs=[row(D), row(D_IN),
                  pl.BlockSpec((HALO_B, 2 * D_B), lambda i: (jnp.maximum(rev(i) * halo_blocks - 1, 0), 1)),
                  row(D_B), pl.BlockSpec((N_F, tm, D), lambda i: (0, rev(i), 0)),
                  row(D), row(D), row(128), _resident(win_g.shape), _full(ln_a_g.shape),
                  _full(ln_a_b.shape), _full(w_spatial.shape), _full(bst.shape), _full(conv_b_w.shape),
                  _full(conv_b_b.shape), _full(ln_b_g.shape), _full(ln_b_b.shape),
                  _resident(wout.shape), _full(ln1_g.shape)] + [ANY] * n_p,
        out_specs=[row(D), ANY, ANY, _full((KB, D_B)), _full((S_MIX_ROWS, 128))] + [ANY] * n_p,
        out_shape=[jax.ShapeDtypeStruct((t, D), F32), jax.ShapeDtypeStruct((N_DEV, D, W_IN_BLK), F32),
                   jax.ShapeDtypeStruct((D, D), F32), jax.ShapeDtypeStruct((KB, D_B), F32),
                   jax.ShapeDtypeStruct((S_MIX_ROWS, 128), F32)]
        + [jax.ShapeDtypeStruct(p.shape, BF16) for p in ffn_partials],
        scratch_shapes=[pltpu.VMEM((tm + HALO_B, D_B), F32), pltpu.VMEM((tm + HALO_B, D_B), F32),
                        pltpu.VMEM((tm, D), BF16), pltpu.VMEM((tm, D), F32), pltpu.VMEM((tm, D_IN), BF16),
                        pltpu.VMEM((tm, D), BF16),
                        pltpu.VMEM((HEADS, CHUNK, CHUNK), BF16),
                        pltpu.VMEM((N_DEV, D, W_IN_BLK), F32), pltpu.VMEM((D, D), F32),
                        pltpu.VMEM((8, D_IN), F32), pltpu.VMEM((8 * HEADS, HEAD_DIM), F32),
                        pltpu.VMEM((8 * HEADS, HEAD_DIM), F32), pltpu.VMEM((HEADS, CHUNK, CHUNK), F32),
                        pltpu.VMEM((HEADS, CHUNK, CHUNK), F32), pltpu.VMEM((8, D_B), F32),
                        pltpu.VMEM((8, D_B), F32), pltpu.VMEM((8, D_B), F32), pltpu.VMEM((8, D), F32),
                        pltpu.VMEM((8, D), F32), pltpu.VMEM((8, D), F32), pltpu.VMEM((8 * KB, D_B), F32),
                        pltpu.SemaphoreType.DMA((2,)),
                        pltpu.SemaphoreType.DMA((n_p, 3)), pltpu.SemaphoreType.DMA((n_p, 3))],
        compiler_params=_params(("arbitrary",)),
    )(x, h, h, yb1, dx1p, dr2, xhat1, rstd1, win_g, ln_a_g, ln_a_b, w_spatial, bst, conv_b_w, conv_b_b,
      ln_b_g, ln_b_b, wout, ln1_g, *ffn_partials)


def _rows128(a):
    return a.reshape(-1, 128)


def _pack_conv(cb, cf):
    out = jnp.zeros((40, 768), F32)
    out = out.at[0:KB, 0:64].set(cb)
    return out.at[32:32 + KF, 0:W_UP_BLK].set(cf)


def kernel(x, w_in, b_in, ln_a_g, ln_a_b, w_spatial, b_spatial, conv_b_w, conv_b_b, ln_b_g, ln_b_b, w_out, b_out, ln1_g, ln1_b, w_up, conv_f_w, conv_f_b, w_down, ln2_g, ln2_b, loss_target, m_w_in, m_b_in, m_ln_a_g, m_ln_a_b, m_w_spatial, m_b_spatial, m_conv_b_w, m_conv_b_b, m_ln_b_g, m_ln_b_b, m_w_out, m_b_out, m_ln1_g, m_ln1_b, m_w_up, m_conv_f_w, m_conv_f_b, m_w_down, m_ln2_g, m_ln2_b, v_w_in, v_b_in, v_ln_a_g, v_ln_a_b, v_w_spatial, v_b_spatial, v_conv_b_w, v_conv_b_b, v_ln_b_g, v_ln_b_b, v_w_out, v_b_out, v_ln1_g, v_ln1_b, v_w_up, v_conv_f_w, v_conv_f_b, v_w_down, v_ln2_g, v_ln2_b):
    t = x.shape[1]
    x2 = x.reshape(t, D)
    target = loss_target.reshape(t, D)
    tm_fwd = min(t, 512)
    tm_bwd = min(t, 256)
    tm_ffn_bwd = min(t, 512)

    xi, yi, ci = _mesh_pos()
    jidx = jnp.stack([_lid(px, py, ci) for px, py in _chip_patterns(xi, yi)]).astype(jnp.int32)

    win_g, wout_g, conv_g, sup, sdown = all_gather_mixer_weights(
        w_in, w_out, w_up, w_down, _pack_conv(conv_b_w, conv_f_w))
    wout_full = wout_g.reshape(D, D)
    conv_b_full = conv_g[:, 0:KB, 0:64].transpose(1, 0, 2).reshape(KB, D_B)
    cfw = conv_g[:, 32:32 + KF, 0:W_UP_BLK]
    cfb = conv_f_b.reshape(N_DEV, W_UP_BLK)
    row = lambda a: a.reshape(1, -1)
    bst = b_spatial.T

    h, xhat1, rstd1, yb1, wup_g, wdown_g = mix_forward(
        x2, win_g, row(b_in), ln_a_g, ln_a_b, w_spatial, bst, conv_b_full, row(conv_b_b),
        row(ln_b_g), row(ln_b_b), wout_full, row(b_out), row(ln1_g), row(ln1_b), sup, sdown, tm_fwd)
    wdown4 = wdown_g.reshape(N_F, W_UP_BLK, D)
    hu, gv, dr2, loss_part, s_ln2 = ffn_forward(
        xhat1, row(ln1_g), row(ln1_b), wup_g, cfw, cfb, wdown4, row(ln2_g), row(ln2_b), target, tm_bwd)

    dwup, dwdown, dcfw, dcfb, dx1p = ffn_backward(
        dr2, xhat1, row(ln1_g), row(ln1_b), hu, gv, wup_g, cfw, wdown4, tm_ffn_bwd)
    ffn_grads = [dwup.reshape(N_DEV, D, W_UP_BLK), dwdown.reshape(N_DEV, D_FF // N_DEV, D)]
    ffn_lands = exchange_with_sibling("exchange_with_sibling_ffn", ffn_grads)
    ffn_partials = [chip_partials("chip_partials_" + nm, g, l, jidx, rb)
                    for nm, g, l, rb in zip(["w_up", "w_down"], ffn_grads, ffn_lands, [256, 352])]
    grad_x, dwin, dwout, dcw, s_mix, *ffn_recvs = mix_backward(
        x2, h, yb1, dx1p, dr2, xhat1, rstd1, win_g, ln_a_g, ln_a_b, w_spatial, bst,
        conv_b_full, row(conv_b_b), row(ln_b_g), row(ln_b_b), wout_full, row(ln1_g), ffn_partials, tm_bwd)

    dcfb_rows = jnp.pad(dcfb.reshape(-1, 128), ((0, 4), (0, 0)))
    svec = jnp.concatenate([s_mix, dcfb_rows, s_ln2, loss_part], axis=0)
    dconv = jnp.zeros((N_DEV, 40, 768), F32)
    dconv = dconv.at[:, 0:KB, 0:64].set(dcw.reshape(KB, N_DEV, 64).transpose(1, 0, 2))
    dconv = dconv.at[:, 32:32 + KF, 0:W_UP_BLK].set(dcfw.reshape(N_DEV, KF, W_UP_BLK))
    mix_grads = [dwin, dwout.reshape(N_DEV, D // N_DEV, D), dconv]
    *mix_lands, sv_land = exchange_with_sibling("exchange_with_sibling_mixer", mix_grads, svec)
    mix_partials = [chip_partials("chip_partials_" + nm, g, l, jidx, rb)
                    for nm, g, l, rb in zip(["w_in", "w_out", "conv"], mix_grads, mix_lands, [512, 128, 40])]
    *mix_recvs, sv_slots = exchange_between_chips(mix_partials, svec, sv_land)
    names = ["w_in", "w_out", "conv", "w_up", "w_down"]
    grads = mix_grads + ffn_grads
    lands = mix_lands + list(ffn_lands)
    recvs = mix_recvs + ffn_recvs
    row_blocks = [512, 128, 40, 256, 352]

    shard_w = [w_in, w_out, _pack_conv(conv_b_w, conv_f_w), w_up, w_down]
    shard_m = [m_w_in, m_w_out, _pack_conv(m_conv_b_w, m_conv_f_w), m_w_up, m_w_down]
    shard_v = [v_w_in, v_w_out, _pack_conv(v_conv_b_w, v_conv_f_w), v_w_up, v_w_down]
    big = {}
    for nm, g, l, r, w, m, v, rb in zip(names, grads, lands, recvs, shard_w, shard_m, shard_v, row_blocks):
        big[nm] = reduce_and_adamw("reduce_adamw_" + nm, g, l, r, w, m, v, jidx, rb)
    for k in range(4):
        packed = big["conv"][k]
        big.setdefault("conv_b_w", []).append(packed[0:KB, 0:64])
        big.setdefault("conv_f_w", []).append(packed[32:32 + KF, 0:W_UP_BLK])

    small_w = dict(b_in=b_in, ln_a_g=ln_a_g, ln_a_b=ln_a_b, w_spatial=w_spatial, b_spatial=b_spatial,
                   conv_b_b=conv_b_b, ln_b_g=ln_b_g, ln_b_b=ln_b_b, b_out=b_out, ln1_g=ln1_g,
                   ln1_b=ln1_b, conv_f_b=conv_f_b, ln2_g=ln2_g, ln2_b=ln2_b)
    small_m = dict(b_in=m_b_in, ln_a_g=m_ln_a_g, ln_a_b=m_ln_a_b, w_spatial=m_w_spatial,
                   b_spatial=m_b_spatial, conv_b_b=m_conv_b_b, ln_b_g=m_ln_b_g, ln_b_b=m_ln_b_b,
                   b_out=m_b_out, ln1_g=m_ln1_g, ln1_b=m_ln1_b, conv_f_b=m_conv_f_b, ln2_g=m_ln2_g,
                   ln2_b=m_ln2_b)
    small_v = dict(b_in=v_b_in, ln_a_g=v_ln_a_g, ln_a_b=v_ln_a_b, w_spatial=v_w_spatial,
                   b_spatial=v_b_spatial, conv_b_b=v_conv_b_b, ln_b_g=v_ln_b_g, ln_b_b=v_ln_b_b,
                   b_out=v_b_out, ln1_g=v_ln1_g, ln1_b=v_ln1_b, conv_f_b=v_conv_f_b, ln2_g=v_ln2_g,
                   ln2_b=v_ln2_b)
    order = [nm for nm, _, _ in SMALL_LAYOUT]
    small_out = small_adamw(sv_slots, [_rows128(small_w[nm]) for nm in order],
                            [_rows128(small_m[nm]) for nm in order], [_rows128(small_v[nm]) for nm in order])
    n_small = len(order)
    small = {nm: [small_out[k * n_small + p].reshape(small_w[nm].shape) for k in range(4)]
             for p, nm in enumerate(order)}
    loss = jnp.sum(small_out[4 * n_small]) * (0.5 / D)

    weights = ["w_in", "b_in", "ln_a_g", "ln_a_b", "w_spatial", "b_spatial", "conv_b_w", "conv_b_b",
               "ln_b_g", "ln_b_b", "w_out", "b_out", "ln1_g", "ln1_b", "w_up", "conv_f_w", "conv_f_b",
               "w_down", "ln2_g", "ln2_b"]
    result = lambda nm, k: big[nm][k] if nm in big else small[nm][k]
    return (loss, grad_x.reshape(x.shape), *[result(nm, 0) for nm in weights],
            *[result(nm, 1) for nm in weights], *[result(nm, 2) for nm in weights],
            *[result(nm, 3) for nm in weights])
```

```python
import functools
import math

import jax
import jax.numpy as jnp
from jax import lax
from jax.experimental import pallas as pl
from jax.experimental.pallas import tpu as pltpu

F32 = jnp.float32
BF16 = jnp.bfloat16

D = 1024
D_A = 512
D_B = 512
HEADS = 4
HEAD_DIM = 128
CHUNK = 128
KB = 31
KF = 3
D_FF = 2816
D_IN = 2048
N_DEV = 8
W_IN_BLK = D_IN // N_DEV
W_UP_BLK = 2 * D_FF // N_DEV
N_F = 4
LN_EPS = 1e-5
ALPHA = 2.0 ** 0.25

ADAM_LR = 0.001
ADAM_B1 = 0.9
ADAM_B2 = 0.999
ADAM_EPS = 1e-08
ADAM_WD = 0.01
ADAM_STEP = 10

INV_SQRT2 = 1.0 / math.sqrt(2.0)
INV_SQRT_2PI = 1.0 / math.sqrt(2.0 * math.pi)

HALO_B = 32
HALO_F = 8
ROWS = 64
VMEM_LIMIT = 58 * 1024 * 1024

MESH = pl.DeviceIdType.MESH
ANY = pl.BlockSpec(memory_space=pl.ANY)
VMEM = pl.BlockSpec(memory_space=pltpu.VMEM)

S_BIN, S_LNAG, S_LNAB, S_WS, S_BS, S_CBB, S_LNBG, S_LNBB, S_BOUT, S_LN1G, S_LN1B = (
    0, 16, 24, 32, 544, 552, 560, 568, 576, 584, 592)
S_MIX_ROWS = 600
S_CFB = 600
S_LN2G = 648
S_LN2B = 656
S_LOSS = 664
S_ROWS = 672


def _tn(a, b):
    return lax.dot_general(a, b, (((0,), (0,)), ((), ())), preferred_element_type=F32)


def _nt(a, b):
    return lax.dot_general(a, b, (((1,), (1,)), ((), ())), preferred_element_type=F32)


def _nn(a, b):
    return jnp.dot(a, b, preferred_element_type=F32)


def _sigmoid(x):
    return 1.0 / (1.0 + jnp.exp(-x))


def _ln_stats(x):
    mu = jnp.mean(x, axis=-1, keepdims=True)
    xc = x - mu
    var = jnp.mean(xc * xc, axis=-1, keepdims=True)
    rstd = lax.rsqrt(var + LN_EPS)
    return xc * rstd, rstd


def _ln_bwd(dxhat, xhat, rstd):
    m1 = jnp.mean(dxhat, axis=-1, keepdims=True)
    m2 = jnp.mean(dxhat * xhat, axis=-1, keepdims=True)
    return rstd * (dxhat - m1 - xhat * m2)


def _rsum8(x):
    r, n = x.shape
    return x.reshape(r // 8, 8, n).sum(axis=0)


def _rows(i, n=ROWS):
    return pl.ds(i * n, n)


def _loop(n, body):
    for i in range(n):
        body(i)


def _tril_mask():
    r = lax.broadcasted_iota(jnp.int32, (CHUNK, CHUNK), 0)
    c = lax.broadcasted_iota(jnp.int32, (CHUNK, CHUNK), 1)
    return c <= r


def _mixer_a_fwd(hu, hv, ga_ref, ba_ref, wsm_ref, bst_ref):
    cdf_u = 0.5 * (1.0 + lax.erf(hu * INV_SQRT2))
    cdf_v = 0.5 * (1.0 + lax.erf(hv * INV_SQRT2))
    u = hu * cdf_u
    v = hv * cdf_v
    xhats, rstds, vns, svs = [], [], [], []
    for hd in range(HEADS):
        sl = slice(hd * HEAD_DIM, (hd + 1) * HEAD_DIM)
        xhat, rstd = _ln_stats(v[:, sl])
        vn = (xhat * ga_ref[hd:hd + 1, :] + ba_ref[hd:hd + 1, :]).astype(BF16)
        sv = _nn(wsm_ref[hd], vn) + bst_ref[:, hd:hd + 1]
        xhats.append(xhat)
        rstds.append(rstd)
        vns.append(vn)
        svs.append(sv)
    return u, cdf_u, cdf_v, xhats, rstds, vns, svs


def _shifted(win):
    n = win.shape[0]
    return [win] + [pltpu.roll(win, n - s, 0) for s in range(1, 8)]


def _tap(shifted, offset):
    s = offset % 8
    return shifted[s][offset - s:offset - s + ROWS, :]


def _conv_b_block(ext_ref, base, cw_ref):
    win = _shifted(ext_ref[pl.ds(base, ROWS + HALO_B), :])
    acc = jnp.zeros((ROWS, D_B), F32)
    for k in range(KB):
        acc = acc + _tap(win, 2 + k) * cw_ref[k:k + 1, :]
    return acc, win


def _taps_f(win):
    n = ROWS + HALO_F
    return [pltpu.roll(win, n - 6, 0)[0:ROWS, :], pltpu.roll(win, n - 7, 0)[0:ROWS, :], win[8:n, :]]


def _params(sem, **kw):
    return pltpu.CompilerParams(dimension_semantics=sem, vmem_limit_bytes=VMEM_LIMIT, **kw)


def _resident(shape):
    zeros = (0,) * len(shape)
    return pl.BlockSpec(shape, lambda *_: zeros, pipeline_mode=pl.Buffered(1))


def _full(shape):
    zeros = (0,) * len(shape)
    return pl.BlockSpec(shape, lambda *_: zeros)


def _mesh_pos():
    return lax.axis_index("x"), lax.axis_index("y"), lax.axis_index("c")


def _chip_patterns(x, y):
    return [(x, y), (1 - x, y), (x, 1 - y), (1 - x, 1 - y)]


def _lid(x, y, c):
    return 4 * x + 2 * y + c


def _gather_copy(outs, send_sems, recv_sems, a, k, block, to, src=None):
    blk = outs[a].at[_lid(*block)]
    return pltpu.make_async_remote_copy(
        src_ref=blk if src is None else src, dst_ref=blk,
        send_sem=send_sems.at[a, k], recv_sem=recv_sems.at[a, k], device_id=to, device_id_type=MESH)


def _gather_start(mine, outs, send_sems, recv_sems, local_sems):
    x, y, c = _mesh_pos()
    me, sib = (x, y, c), (x, y, 1 - c)
    for a in range(len(mine)):
        pltpu.make_async_copy(mine[a], outs[a].at[_lid(*me)], local_sems.at[a]).start()
        _gather_copy(outs, send_sems, recv_sems, a, 0, me, sib, src=mine[a]).start()
        for j, chip in enumerate(_chip_patterns(x, y)[1:]):
            _gather_copy(outs, send_sems, recv_sems, a, 1 + j, me, (*chip, c), src=mine[a]).start()


def _gather_finish(mine, outs, send_sems, recv_sems, local_sems):
    x, y, c = _mesh_pos()
    me, sib = (x, y, c), (x, y, 1 - c)
    chips = _chip_patterns(x, y)[1:]
    n = len(mine)
    copy = functools.partial(_gather_copy, outs, send_sems, recv_sems)
    passed = []
    for j, chip in enumerate(chips):
        for a in range(n):
            copy(a, 1 + j, (*chip, c), me).wait_recv()
            cp = copy(a, 4 + j, (*chip, c), sib)
            cp.start()
            passed.append(cp)
    for a in range(n):
        copy(a, 0, sib, me).wait_recv()
        for j, chip in enumerate(chips):
            copy(a, 4 + j, (*chip, 1 - c), me).wait_recv()
        for k in range(4):
            copy(a, k, me, sib, src=mine[a]).wait_send()
        pltpu.make_async_copy(mine[a], outs[a].at[_lid(*me)], local_sems.at[a]).wait()
    for cp in passed:
        cp.wait_send()


def _gather_scratch(n):
    return [pltpu.SemaphoreType.DMA((n, 7)), pltpu.SemaphoreType.DMA((n, 7)), pltpu.SemaphoreType.DMA((n,))]


def all_gather_mixer_weights(w_in, w_out, w_up, w_down, convp):
    srcs = [w_in, w_out, convp]
    n = len(srcs)

    def body(win_ref, wout_ref, convp_ref, wup_ref, wdown_ref,
             gin_ref, gout_ref, gconv_ref, sup_ref, sdown_ref,
             sin_ref, sout_ref, send_sems, recv_sems, local_sems):
        sin_ref[...] = win_ref[...].astype(BF16)
        sout_ref[...] = wout_ref[...].astype(BF16)
        mine = [sin_ref, sout_ref, convp_ref]
        outs = [gin_ref, gout_ref, gconv_ref]
        _gather_start(mine, outs, send_sems, recv_sems, local_sems)
        sup_ref[...] = wup_ref[...].astype(BF16)
        sdown_ref[...] = wdown_ref[...].astype(BF16)
        _gather_finish(mine, outs, send_sems, recv_sems, local_sems)

    return pl.pallas_call(
        body, name="all_gather_mixer_weights",
        out_shape=[jax.ShapeDtypeStruct((N_DEV,) + w_in.shape, BF16),
                   jax.ShapeDtypeStruct((N_DEV,) + w_out.shape, BF16),
                   jax.ShapeDtypeStruct((N_DEV,) + convp.shape, F32),
                   jax.ShapeDtypeStruct(w_up.shape, BF16), jax.ShapeDtypeStruct(w_down.shape, BF16)],
        in_specs=[VMEM] * 5, out_specs=[ANY] * n + [VMEM, VMEM],
        scratch_shapes=[pltpu.VMEM(w_in.shape, BF16), pltpu.VMEM(w_out.shape, BF16)] + _gather_scratch(n),
        compiler_params=pltpu.CompilerParams(vmem_limit_bytes=VMEM_LIMIT),
    )(w_in, w_out, convp, w_up, w_down)


def exchange_with_sibling(name, grads, svec=None):
    srcs = list(grads) + ([] if svec is None else [svec])
    n, n_all = len(grads), len(srcs)

    def body(*refs):
        src, land = refs[:n_all], refs[n_all:2 * n_all]
        send_sems, recv_sems = refs[2 * n_all:]
        x, y, c = _mesh_pos()
        sib = (x, y, 1 - c)
        copies = []
        for a in range(n):
            for k, (px, py) in enumerate(_chip_patterns(x, y)):
                copies.append(pltpu.make_async_remote_copy(
                    src_ref=src[a].at[_lid(px, py, 1 - c)], dst_ref=land[a].at[k],
                    send_sem=send_sems.at[a, k], recv_sem=recv_sems.at[a, k],
                    device_id=sib, device_id_type=MESH))
        if svec is not None:
            copies.append(pltpu.make_async_remote_copy(
                src_ref=src[n], dst_ref=land[n], send_sem=send_sems.at[n, 0], recv_sem=recv_sems.at[n, 0],
                device_id=sib, device_id_type=MESH))
        for cp in copies:
            cp.start()
        for cp in copies:
            cp.wait()

    return pl.pallas_call(
        body, name=name,
        out_shape=[jax.ShapeDtypeStruct((4,) + g.shape[1:], F32) for g in grads]
        + ([] if svec is None else [jax.ShapeDtypeStruct(svec.shape, F32)]),
        in_specs=[ANY] * n_all, out_specs=[ANY] * n_all,
        scratch_shapes=[pltpu.SemaphoreType.DMA((n_all, 4)), pltpu.SemaphoreType.DMA((n_all, 4))],
    )(*srcs)


def _chip_copies(p, land, send_sems, recv_sems):
    x, y, c = _mesh_pos()
    return [pltpu.make_async_remote_copy(
        src_ref=p[a].at[k], dst_ref=land[a].at[k], send_sem=send_sems.at[a, k], recv_sem=recv_sems.at[a, k],
        device_id=(px, py, c), device_id_type=MESH)
        for k, (px, py) in enumerate(_chip_patterns(x, y)[1:]) for a in range(len(p))]


def exchange_between_chips(partials, svec, sv_land):
    n = len(partials)

    def body(*refs):
        p = refs[:n]
        sv_ref, svl_ref = refs[n], refs[n + 1]
        land = refs[n + 2:2 * n + 2]
        sv_slots = refs[2 * n + 2]
        chip_sv, send_sems, recv_sems, sv_send, sv_recv, local_sem = refs[2 * n + 3:]
        x, y, c = _mesh_pos()
        q = 2 * x + y
        chip_sv[...] = sv_ref[...] + svl_ref[...]
        local = pltpu.make_async_copy(chip_sv, sv_slots.at[q], local_sem)
        local.start()
        copies = _chip_copies(p, land, send_sems, recv_sems)
        for cp in copies:
            cp.start()
        sv_copies = []
        for k, (px, py) in enumerate(_chip_patterns(x, y)[1:]):
            cp = pltpu.make_async_remote_copy(
                src_ref=chip_sv, dst_ref=sv_slots.at[q],
                send_sem=sv_send.at[k], recv_sem=sv_recv.at[k],
                device_id=(px, py, c), device_id_type=MESH)
            cp.start()
            sv_copies.append(cp)
        for cp in copies:
            cp.wait()
        for k, (px, py) in enumerate(_chip_patterns(x, y)[1:]):
            sv_copies[k].wait_send()
            pltpu.make_async_remote_copy(
                src_ref=chip_sv, dst_ref=sv_slots.at[2 * px + py],
                send_sem=sv_send.at[k], recv_sem=sv_recv.at[k],
                device_id=(px, py, c), device_id_type=MESH).wait_recv()
        local.wait()

    return pl.pallas_call(
        body, name="exchange_between_chips",
        out_shape=[jax.ShapeDtypeStruct(p.shape, BF16) for p in partials]
        + [jax.ShapeDtypeStruct((4,) + svec.shape, F32)],
        in_specs=[ANY] * n + [VMEM, VMEM], out_specs=[ANY] * (n + 1),
        scratch_shapes=[pltpu.VMEM(svec.shape, F32),
                        pltpu.SemaphoreType.DMA((n, 3)), pltpu.SemaphoreType.DMA((n, 3)),
                        pltpu.SemaphoreType.DMA((3,)), pltpu.SemaphoreType.DMA((3,)),
                        pltpu.SemaphoreType.DMA],
    )(*partials, svec, sv_land)


def chip_partials(name, g, land, jidx, rb):
    _, r, c = g.shape

    def body(j_ref, g_ref, l_ref, o_ref):
        o_ref[...] = (g_ref[...] + l_ref[...]).astype(BF16)

    return pl.pallas_call(
        body, name=name,
        out_shape=jax.ShapeDtypeStruct((3, r, c), BF16),
        grid_spec=pltpu.PrefetchScalarGridSpec(
            num_scalar_prefetch=1, grid=(3, r // rb),
            in_specs=[pl.BlockSpec((1, rb, c), lambda k, i, j: (j[1 + k], i, 0)),
                      pl.BlockSpec((1, rb, c), lambda k, i, j: (1 + k, i, 0))],
            out_specs=pl.BlockSpec((1, rb, c), lambda k, i, j: (k, i, 0))),
        compiler_params=_params(("arbitrary", "arbitrary")),
    )(jidx, g, land)


def _adamw(w, g, m, v):
    m2 = ADAM_B1 * m + (1.0 - ADAM_B1) * g
    v2 = ADAM_B2 * v + (1.0 - ADAM_B2) * (g * g)
    m_hat = m2 / (1.0 - ADAM_B1 ** ADAM_STEP)
    v_hat = v2 / (1.0 - ADAM_B2 ** ADAM_STEP)
    delta = -ADAM_LR * (m_hat / (jnp.sqrt(v_hat) + ADAM_EPS) + ADAM_WD * w)
    return delta, m2, v2


def reduce_and_adamw(name, g, land, recv, w, m, v, jidx, rb):
    _, r, c = g.shape

    def body(j_ref, g_ref, l_ref, r_ref, w_ref, m_ref, v_ref, go_ref, do_ref, mo_ref, vo_ref):
        grad = (g_ref[0] + l_ref[0]) + r_ref[0].astype(F32) + r_ref[1].astype(F32) + r_ref[2].astype(F32)
        delta, m2, v2 = _adamw(w_ref[...], grad, m_ref[...], v_ref[...])
        go_ref[...] = grad
        do_ref[...] = delta
        mo_ref[...] = m2
        vo_ref[...] = v2

    blk = pl.BlockSpec((rb, c), lambda i, j: (i, 0))
    return pl.pallas_call(
        body, name=name,
        out_shape=[jax.ShapeDtypeStruct((r, c), F32)] * 4,
        grid_spec=pltpu.PrefetchScalarGridSpec(
            num_scalar_prefetch=1, grid=(r // rb,),
            in_specs=[pl.BlockSpec((1, rb, c), lambda i, j: (j[0], i, 0)),
                      pl.BlockSpec((1, rb, c), lambda i, j: (0, i, 0)),
                      pl.BlockSpec((3, rb, c), lambda i, j: (0, i, 0)),
                      blk, blk, blk],
            out_specs=[blk] * 4),
        compiler_params=_params(("arbitrary",)),
    )(jidx, g, land, recv, w, m, v)


SMALL_LAYOUT = [
    ("b_in", S_BIN, 16), ("ln_a_g", S_LNAG, 4), ("ln_a_b", S_LNAB, 4), ("w_spatial", S_WS, 512),
    ("b_spatial", S_BS, 4), ("conv_b_b", S_CBB, 4), ("ln_b_g", S_LNBG, 4), ("ln_b_b", S_LNBB, 4),
    ("b_out", S_BOUT, 8), ("ln1_g", S_LN1G, 8), ("ln1_b", S_LN1B, 8), ("conv_f_b", S_CFB, 44),
    ("ln2_g", S_LN2G, 8), ("ln2_b", S_LN2B, 8),
]


def small_adamw(sv_slots, ws, ms, vs):
    n = len(SMALL_LAYOUT)

    def body(*refs):
        s_ref = refs[0]
        w_refs, m_refs, v_refs = refs[1:1 + n], refs[1 + n:1 + 2 * n], refs[1 + 2 * n:1 + 3 * n]
        outs = refs[1 + 3 * n:]
        for p, (_, row0, rows) in enumerate(SMALL_LAYOUT):
            sl = pl.ds(row0, rows)
            grad = ((s_ref[0, sl, :] + s_ref[1, sl, :]) + s_ref[2, sl, :]) + s_ref[3, sl, :]
            delta, m2, v2 = _adamw(w_refs[p][...], grad, m_refs[p][...], v_refs[p][...])
            outs[p][...] = grad
            outs[n + p][...] = delta
            outs[2 * n + p][...] = m2
            outs[3 * n + p][...] = v2
        sl = pl.ds(S_LOSS, 8)
        outs[4 * n][...] = ((s_ref[0, sl, :] + s_ref[1, sl, :]) + s_ref[2, sl, :]) + s_ref[3, sl, :]

    shapes = [jax.ShapeDtypeStruct((rows, 128), F32) for _, _, rows in SMALL_LAYOUT]
    return pl.pallas_call(
        body, name="small_adamw", out_shape=shapes * 4 + [jax.ShapeDtypeStruct((8, 128), F32)],
        in_specs=[VMEM] * (1 + 3 * n), out_specs=[VMEM] * (4 * n + 1),
    )(sv_slots, *ws, *ms, *vs)


def mix_forward(x, win_g, b_in, ln_a_g, ln_a_b, w_spatial, bst, conv_b_w, conv_b_b, ln_b_g, ln_b_b,
                wout, b_out, ln1_g, ln1_b, sup, sdown, tm):
    t = x.shape[0]
    nt = t // tm
    n_chunks = tm // CHUNK

    def body(x_ref, win_ref, bin_ref, ga_ref, ba_ref, ws_ref, bst_ref, cw_ref, cb_ref, gb_ref,
             bb_ref, wout_ref, bout_ref, g1_ref, b1_ref, sup_ref, sdown_ref,
             h_ref, xhat1_ref, rstd1_ref, yb1_ref, gup_ref, gdown_ref,
             ext_ref, y_ref, wsm_ref, send_sems, recv_sems, local_sems):
        i = pl.program_id(0)
        gather = ([sup_ref, sdown_ref], [gup_ref, gdown_ref], send_sems, recv_sems, local_sems)

        @pl.when(i == 0)
        def _():
            _gather_start(*gather)
            ext_ref[0:HALO_B, :] = jnp.zeros((HALO_B, D_B), F32)
            mask = _tril_mask()
            for hd in range(HEADS):
                wsm_ref[hd] = jnp.where(mask, ws_ref[hd], 0.0).astype(BF16)

        xb = x_ref[...].astype(BF16)
        for j in range(N_DEV):
            cols = slice(j * W_IN_BLK, (j + 1) * W_IN_BLK)
            h_ref[:, cols] = _nn(xb, win_ref[j]) + bin_ref[:, cols]

        def chunk(ci):
            r = _rows(ci, CHUNK)
            u, _, _, _, _, _, svs = _mixer_a_fwd(h_ref[r, 0:D_A], h_ref[r, D_A:2 * D_A],
                                                 ga_ref, ba_ref, wsm_ref, bst_ref)
            for hd in range(HEADS):
                sl = slice(hd * HEAD_DIM, (hd + 1) * HEAD_DIM)
                y_ref[r, sl] = (u[:, sl] * svs[hd]).astype(BF16)
            a_b = h_ref[r, 2 * D_A:2 * D_A + D_B]
            g_b = h_ref[r, 2 * D_A + D_B:D_IN]
            ext_ref[pl.ds(HALO_B + ci * CHUNK, CHUNK), :] = a_b * _sigmoid(g_b)

        _loop(n_chunks, chunk)

        def conv_rows(bi):
            base = bi * ROWS
            acc, _ = _conv_b_block(ext_ref, base, cw_ref)
            yb1 = acc + cb_ref[...]
            yb1_ref[pl.ds(base, ROWS), :] = yb1
            xhat, _ = _ln_stats(yb1)
            yb2 = xhat * gb_ref[...] + bb_ref[...]
            y_ref[pl.ds(base, ROWS), D_A:D] = (yb2 * _sigmoid(yb2)).astype(BF16)

        _loop(tm // ROWS, conv_rows)
        ext_ref[0:HALO_B, :] = ext_ref[tm:tm + HALO_B, :]

        mix = _nn(y_ref[...], wout_ref[...]) + bout_ref[...]
        xhat1, rstd1 = _ln_stats(ALPHA * x_ref[...] + mix)
        xhat1_ref[...] = xhat1
        rstd1_ref[...] = jnp.broadcast_to(rstd1, (tm, 128))

        @pl.when(i == nt - 1)
        def _():
            _gather_finish(*gather)

    row = lambda w: pl.BlockSpec((tm, w), lambda i: (i, 0))
    return pl.pallas_call(
        body, name="mix_forward", grid=(nt,),
        in_specs=[row(D), _resident(win_g.shape), _full(b_in.shape), _full(ln_a_g.shape),
                  _full(ln_a_b.shape), _full(w_spatial.shape), _full(bst.shape),
                  _full(conv_b_w.shape), _full(conv_b_b.shape), _full(ln_b_g.shape),
                  _full(ln_b_b.shape), _resident(wout.shape), _full(b_out.shape),
                  _full(ln1_g.shape), _full(ln1_b.shape), ANY, ANY],
        out_specs=[row(D_IN), row(D), row(128), row(D_B), ANY, ANY],
        out_shape=[jax.ShapeDtypeStruct((t, D_IN), F32), jax.ShapeDtypeStruct((t, D), F32),
                   jax.ShapeDtypeStruct((t, 128), F32), jax.ShapeDtypeStruct((t, D_B), F32),
                   jax.ShapeDtypeStruct((N_DEV,) + sup.shape, BF16),
                   jax.ShapeDtypeStruct((N_DEV,) + sdown.shape, BF16)],
        scratch_shapes=[pltpu.VMEM((tm + HALO_B, D_B), F32), pltpu.VMEM((tm, D), BF16),
                        pltpu.VMEM((HEADS, CHUNK, CHUNK), BF16)] + _gather_scratch(2),
        compiler_params=_params(("arbitrary",)),
    )(x, win_g, b_in, ln_a_g, ln_a_b, w_spatial, bst, conv_b_w, conv_b_b, ln_b_g, ln_b_b,
      wout, b_out, ln1_g, ln1_b, sup, sdown)


def ffn_forward(xhat1, ln1_g, ln1_b, wup_g, cfw, cfb, wdown, ln2_g, ln2_b, target, tm):
    t = xhat1.shape[0]
    nt = t // tm

    def body(xh_ref, g1_ref, b1_ref, wup_ref, cfw_ref, cfb_ref, wdown_ref, g2_ref, b2_ref, tgt_ref,
             hu_ref, gv_ref, dr2_ref, loss_ref, sln2_ref,
             x1_ref, x1b_ref, carry_ref, gbuf_ref, ffn_ref, acc_loss, acc_g2, acc_b2):
        i = pl.program_id(0)

        @pl.when(i == 0)
        def _():
            carry_ref[...] = jnp.zeros(carry_ref.shape, F32)
            acc_loss[...] = jnp.zeros(acc_loss.shape, F32)
            acc_g2[...] = jnp.zeros(acc_g2.shape, F32)
            acc_b2[...] = jnp.zeros(acc_b2.shape, F32)

        x1 = xh_ref[...] * g1_ref[...] + b1_ref[...]
        x1_ref[...] = x1
        x1b_ref[...] = x1.astype(BF16)

        def conv(j, base):
            if base == 0:
                win = jnp.concatenate([carry_ref[j], hu_ref[j, 0:ROWS, :]], axis=0)
            else:
                win = hu_ref[j, base - HALO_F:base + ROWS, :]
            taps = _taps_f(win)
            w = cfw_ref[j]
            return sum(taps[k] * w[k:k + 1, :] for k in range(KF)) + cfb_ref[j:j + 1, :]

        for f in range(N_F):
            hu_ref[f] = _nt(x1b_ref[...], wup_ref[f])
            hu_ref[N_F + f] = _nt(x1b_ref[...], wup_ref[N_F + f])

            def rows(bi, f=f):
                gate = conv(f, bi * ROWS)
                val = conv(N_F + f, bi * ROWS)
                gbuf_ref[_rows(bi), :] = (gate * _sigmoid(gate) * val).astype(BF16)
                gv_ref[f, _rows(bi), :] = gate.astype(BF16)
                gv_ref[N_F + f, _rows(bi), :] = val.astype(BF16)

            _loop(tm // ROWS, rows)
            carry_ref[f] = hu_ref[f, tm - HALO_F:tm, :]
            carry_ref[N_F + f] = hu_ref[N_F + f, tm - HALO_F:tm, :]
            part = _nn(gbuf_ref[...], wdown_ref[f])
            if f == 0:
                ffn_ref[...] = part
            else:
                ffn_ref[...] += part

        def tail(bi):
            r = _rows(bi)
            xhat2, rstd2 = _ln_stats(ALPHA * x1_ref[r, :] + ffn_ref[r, :])
            err = xhat2 * g2_ref[...] + b2_ref[...] - tgt_ref[r, :]
            e2 = _rsum8(err * err)
            acc_loss[...] += sum(e2[:, k * 128:(k + 1) * 128] for k in range(D // 128))
            dy = err * (1.0 / D)
            acc_g2[...] += _rsum8(dy * xhat2)
            acc_b2[...] += _rsum8(dy)
            dr2_ref[r, :] = _ln_bwd(dy * g2_ref[...], xhat2, rstd2)

        _loop(tm // ROWS, tail)
        loss_ref[...] = acc_loss[...]

        @pl.when(i == nt - 1)
        def _():
            dg = jnp.sum(acc_g2[...], axis=0, keepdims=True)
            db = jnp.sum(acc_b2[...], axis=0, keepdims=True)
            for k in range(D // 128):
                sln2_ref[k:k + 1, :] = dg[:, k * 128:(k + 1) * 128]
                sln2_ref[8 + k:9 + k, :] = db[:, k * 128:(k + 1) * 128]

    row = pl.BlockSpec((tm, D), lambda i: (i, 0))
    return pl.pallas_call(
        body, name="ffn_forward", grid=(nt,),
        in_specs=[row, _full(ln1_g.shape), _full(ln1_b.shape), _resident(wup_g.shape),
                  _full(cfw.shape), _full(cfb.shape), _resident(wdown.shape),
                  _full(ln2_g.shape), _full(ln2_b.shape), row],
        out_specs=[pl.BlockSpec((N_DEV, tm, W_UP_BLK), lambda i: (0, i, 0)),
                   pl.BlockSpec((N_DEV, tm, W_UP_BLK), lambda i: (0, i, 0)), row,
                   _full((8, 128)), _full((16, 128))],
        out_shape=[jax.ShapeDtypeStruct((N_DEV, t, W_UP_BLK), F32),
                   jax.ShapeDtypeStruct((N_DEV, t, W_UP_BLK), BF16), jax.ShapeDtypeStruct((t, D), F32),
                   jax.ShapeDtypeStruct((8, 128), F32), jax.ShapeDtypeStruct((16, 128), F32)],
        scratch_shapes=[pltpu.VMEM((tm, D), F32), pltpu.VMEM((tm, D), BF16),
                        pltpu.VMEM((N_DEV, HALO_F, W_UP_BLK), F32), pltpu.VMEM((tm, W_UP_BLK), BF16),
                        pltpu.VMEM((tm, D), F32), pltpu.VMEM((8, 128), F32),
                        pltpu.VMEM((8, D), F32), pltpu.VMEM((8, D), F32)],
        compiler_params=_params(("arbitrary",)),
    )(xhat1, ln1_g, ln1_b, wup_g, cfw, cfb, wdown, ln2_g, ln2_b, target)


def ffn_backward(dr2, xhat1, ln1_g, ln1_b, hu, gv, wup_g, cfw, wdown, tm):
    t = dr2.shape[0]
    nt = t // tm
    sub_rows = tm
    hu4 = hu.reshape(2, N_F, t, W_UP_BLK)
    gv4 = gv.reshape(2, N_F, t, W_UP_BLK)
    wup4 = wup_g.reshape(2, N_F, W_UP_BLK, D)
    cfw4 = cfw.reshape(2, N_F, KF, W_UP_BLK)

    def body(dr2_ref, xh_ref, g1_ref, b1_ref, hu_ref, gv_ref, wup_ref, cfw_ref, wdown_ref,
             dwup_ref, dwdown_ref, dcfw_ref, dcfb_ref, dx1_ref,
             x1b_ref, drb_ref, dg_ref, dextg_ref, dextv_ref, gbuf_ref,
             dhug_ref, dhuv_ref, acc_wup, acc_wdown, acc_cfw, acc_cfb, sem):
        f = pl.program_id(0)
        i = pl.program_id(1)

        @pl.when(i == 0)
        def _():
            acc_wup[...] = jnp.zeros(acc_wup.shape, F32)
            acc_wdown[...] = jnp.zeros(acc_wdown.shape, F32)
            acc_cfw[...] = jnp.zeros(acc_cfw.shape, F32)
            acc_cfb[...] = jnp.zeros(acc_cfb.shape, F32)
            dextg_ref[tm:tm + HALO_F, :] = jnp.zeros((HALO_F, W_UP_BLK), F32)
            dextv_ref[tm:tm + HALO_F, :] = jnp.zeros((HALO_F, W_UP_BLK), F32)

        w = [cfw_ref[0, 0], cfw_ref[1, 0]]
        dext = [dextg_ref, dextv_ref]
        dhu = [dhug_ref, dhuv_ref]

        def rows1(bi):
            r = _rows(bi)
            gate = gv_ref[0, 0, r, :].astype(F32)
            val = gv_ref[1, 0, r, :].astype(F32)
            sg = _sigmoid(gate)
            silu = gate * sg
            gbuf_ref[r, :] = (silu * val).astype(BF16)
            dg = dg_ref[r, :]
            dgate = dg * val * (sg * (1.0 + gate * (1.0 - sg)))
            dval = dg * silu
            dextg_ref[r, :] = dgate
            dextv_ref[r, :] = dval
            acc_cfb[0:8, :] += _rsum8(dgate)
            acc_cfb[8:16, :] += _rsum8(dval)

        def rows2(bi):
            r = _rows(bi)
            for g in range(2):
                win = dext[g][pl.ds(bi * ROWS, ROWS + HALO_F), :]
                n = ROWS + HALO_F
                later = [pltpu.roll(win, n - 2, 0)[0:ROWS, :], pltpu.roll(win, n - 1, 0)[0:ROWS, :],
                         win[0:ROWS, :]]
                d = sum(later[k] * w[g][k:k + 1, :] for k in range(KF))
                dhu[g][r, :] = d.astype(BF16)
                pre = hu_ref[g, 0, r, :]
                for k in range(KF):
                    r0 = 8 * (g * KF + k)
                    acc_cfw[r0:r0 + 8, :] += _rsum8(later[k] * pre)

        for sub in reversed(range(tm // sub_rows)):
            rs = slice(sub * sub_rows, (sub + 1) * sub_rows)
            blocks = range(sub * sub_rows // ROWS, (sub + 1) * sub_rows // ROWS)
            x1b_ref[rs, :] = (xh_ref[rs, :] * g1_ref[...] + b1_ref[...]).astype(BF16)
            drb_ref[rs, :] = dr2_ref[rs, :].astype(BF16)
            dg_ref[rs, :] = _nt(drb_ref[rs, :], wdown_ref[0])
            for bi in blocks:
                rows1(bi)
            for bi in blocks:
                rows2(bi)
            acc_wdown[...] += _tn(gbuf_ref[rs, :], drb_ref[rs, :])
            acc_wup[0] += _tn(dhug_ref[rs, :], x1b_ref[rs, :])
            acc_wup[1] += _tn(dhuv_ref[rs, :], x1b_ref[rs, :])
            dx1_ref[0, rs, :] = (_nn(dhug_ref[rs, :], wup_ref[0, 0])
                                 + _nn(dhuv_ref[rs, :], wup_ref[1, 0])).astype(BF16)
        dextg_ref[tm:tm + HALO_F, :] = dextg_ref[0:HALO_F, :]
        dextv_ref[tm:tm + HALO_F, :] = dextv_ref[0:HALO_F, :]

        @pl.when(i == nt - 1)
        def _():
            for g in range(2):
                dcfb_ref[g, 0] = jnp.sum(acc_cfb[8 * g:8 * g + 8, :], axis=0, keepdims=True)
                for k in range(KF):
                    r0 = 8 * (g * KF + k)
                    dcfw_ref[g, 0, k:k + 1, :] = jnp.sum(acc_cfw[r0:r0 + 8, :], axis=0, keepdims=True)
            cps = [pltpu.make_async_copy(acc_wup.at[0], dwup_ref.at[0, f], sem.at[0]),
                   pltpu.make_async_copy(acc_wup.at[1], dwup_ref.at[1, f], sem.at[1]),
                   pltpu.make_async_copy(acc_wdown, dwdown_ref.at[f], sem.at[2])]
            for cp in cps:
                cp.start()
            for cp in cps:
                cp.wait()

    rev = lambda i: nt - 1 - i
    row = pl.BlockSpec((tm, D), lambda f, i: (rev(i), 0))
    pair = lambda r, c: pl.BlockSpec((2, 1, r, c), lambda f, i: (0, f, 0, 0))
    return pl.pallas_call(
        body, name="ffn_backward", grid=(N_F, nt),
        in_specs=[row, row, _full(ln1_g.shape), _full(ln1_b.shape),
                  pl.BlockSpec((2, 1, tm, W_UP_BLK), lambda f, i: (0, f, rev(i), 0)),
                  pl.BlockSpec((2, 1, tm, W_UP_BLK), lambda f, i: (0, f, rev(i), 0)),
                  pair(W_UP_BLK, D), pair(KF, W_UP_BLK),
                  pl.BlockSpec((1, W_UP_BLK, D), lambda f, i: (f, 0, 0))],
        out_specs=[ANY, ANY, pair(KF, W_UP_BLK), pair(1, W_UP_BLK),
                   pl.BlockSpec((1, tm, D), lambda f, i: (f, rev(i), 0))],
        out_shape=[jax.ShapeDtypeStruct((2, N_F, W_UP_BLK, D), F32),
                   jax.ShapeDtypeStruct((N_F, W_UP_BLK, D), F32),
                   jax.ShapeDtypeStruct((2, N_F, KF, W_UP_BLK), F32),
                   jax.ShapeDtypeStruct((2, N_F, 1, W_UP_BLK), F32),
                   jax.ShapeDtypeStruct((N_F, t, D), BF16)],
        scratch_shapes=[pltpu.VMEM((tm, D), BF16), pltpu.VMEM((tm, D), BF16),
                        pltpu.VMEM((tm, W_UP_BLK), F32),
                        pltpu.VMEM((tm + HALO_F, W_UP_BLK), F32), pltpu.VMEM((tm + HALO_F, W_UP_BLK), F32),
                        pltpu.VMEM((tm, W_UP_BLK), BF16), pltpu.VMEM((tm, W_UP_BLK), BF16),
                        pltpu.VMEM((tm, W_UP_BLK), BF16),
                        pltpu.VMEM((2, W_UP_BLK, D), F32), pltpu.VMEM((W_UP_BLK, D), F32),
                        pltpu.VMEM((2 * KF * 8, W_UP_BLK), F32), pltpu.VMEM((16, W_UP_BLK), F32),
                        pltpu.SemaphoreType.DMA((3,))],
        compiler_params=_params(("arbitrary", "arbitrary")),
    )(dr2, xhat1, ln1_g, ln1_b, hu4, gv4, wup4, cfw4, wdown)


def mix_backward(x, h, yb1, dx1p, dr2, xhat1, rstd1, win_g, ln_a_g, ln_a_b, w_spatial, bst,
                 conv_b_w, ln_b_g, ln_b_b, wout, ln1_g, ffn_partials, tm):
    t = x.shape[0]
    n_p = len(ffn_partials)
    nt = t // tm
    n_chunks = tm // CHUNK
    halo_blocks = tm // HALO_B

    def body(x_ref, h_ref, halo_ref, yb1_ref, dx1p_ref, dr2_ref, xh1_ref, rstd1_ref, win_ref, ga_ref, ba_ref,
             ws_ref, bst_ref, cw_ref, gb_ref, bb_ref, wout_ref, g1_ref, *rest):
        p_refs, rest = rest[:n_p], rest[n_p:]
        gx_ref, dwin_ref, dwout_ref, dcw_ref, small_ref = rest[:5]
        land_refs, rest = rest[5:5 + n_p], rest[5 + n_p:]
        (ext_ref, dext_ref, y_ref, dy_ref, dh_ref, dmb_ref, wsm_ref,
         acc_win, acc_wout, acc_bin, acc_lnag, acc_lnab, acc_ws, acc_bs, acc_cbb, acc_lnbg,
         acc_lnbb, acc_bout, acc_ln1g, acc_ln1b, acc_cw, sem, send_sems, recv_sems) = rest
        i = pl.program_id(0)

        @pl.when(i == 0)
        def _():
            for cp in _chip_copies(p_refs, land_refs, send_sems, recv_sems):
                cp.start()

        first_tile = i == nt - 1
        accs = [acc_win, acc_wout, acc_bin, acc_lnag, acc_lnab, acc_ws, acc_bs, acc_cbb, acc_lnbg,
                acc_lnbb, acc_bout, acc_ln1g, acc_ln1b, acc_cw]

        @pl.when(i == 0)
        def _():
            for acc in accs:
                acc[...] = jnp.zeros(acc.shape, F32)
            dext_ref[tm:tm + HALO_B, :] = jnp.zeros((HALO_B, D_B), F32)
            mask = _tril_mask()
            for hd in range(HEADS):
                wsm_ref[hd] = jnp.where(mask, ws_ref[hd], 0.0).astype(BF16)

        def ln1_rows(bi):
            r = _rows(bi)
            part = [dx1p_ref[f, r, :].astype(F32) for f in range(N_F)]
            dx1 = ALPHA * dr2_ref[r, :] + ((part[0] + part[1]) + (part[2] + part[3]))
            xhat = xh1_ref[r, :]
            acc_ln1g[...] += _rsum8(dx1 * xhat)
            acc_ln1b[...] += _rsum8(dx1)
            dr1 = _ln_bwd(dx1 * g1_ref[...], xhat, rstd1_ref[r, 0:1])
            acc_bout[...] += _rsum8(dr1)
            gx_ref[r, :] = ALPHA * dr1
            dmb_ref[r, :] = dr1.astype(BF16)

        _loop(tm // ROWS, ln1_rows)
        dy_ref[...] = _nt(dmb_ref[...], wout_ref[...])

        ha = halo_ref[:, 0:D_B]
        hg = halo_ref[:, D_B:2 * D_B]
        ext_ref[0:HALO_B, :] = jnp.where(first_tile, 0.0, 1.0) * (ha * _sigmoid(hg))

        def chunk(ci):
            r = _rows(ci, CHUNK)
            hu, hv = h_ref[r, 0:D_A], h_ref[r, D_A:2 * D_A]
            u, cdf_u, cdf_v, xhats, rstds, vns, svs = _mixer_a_fwd(hu, hv, ga_ref, ba_ref, wsm_ref, bst_ref)
            for hd in range(HEADS):
                sl = slice(hd * HEAD_DIM, (hd + 1) * HEAD_DIM)
                rows8 = slice(8 * hd, 8 * hd + 8)
                dy_a = dy_ref[r, sl]
                y_ref[r, sl] = (u[:, sl] * svs[hd]).astype(BF16)
                du = dy_a * svs[hd]
                dsv = dy_a * u[:, sl]
                dsvb = dsv.astype(BF16)
                acc_bs[hd] += dsv
                acc_ws[hd] += _nt(dsvb, vns[hd])
                dvn = _tn(wsm_ref[hd], dsvb)
                acc_lnag[rows8, :] += _rsum8(dvn * xhats[hd])
                acc_lnab[rows8, :] += _rsum8(dvn)
                dv = _ln_bwd(dvn * ga_ref[hd:hd + 1, :], xhats[hd], rstds[hd])
                hus, hvs = hu[:, sl], hv[:, sl]
                slv = slice(D_A + hd * HEAD_DIM, D_A + (hd + 1) * HEAD_DIM)
                dhu = du * (cdf_u[:, sl] + hus * jnp.exp(-0.5 * hus * hus) * INV_SQRT_2PI)
                dhv = dv * (cdf_v[:, sl] + hvs * jnp.exp(-0.5 * hvs * hvs) * INV_SQRT_2PI)
                acc_bin[:, sl] += _rsum8(dhu)
                acc_bin[:, slv] += _rsum8(dhv)
                dh_ref[r, sl] = dhu.astype(BF16)
                dh_ref[r, slv] = dhv.astype(BF16)
            a_b = h_ref[r, 2 * D_A:2 * D_A + D_B]
            g_b = h_ref[r, 2 * D_A + D_B:D_IN]
            ext_ref[pl.ds(HALO_B + ci * CHUNK, CHUNK), :] = a_b * _sigmoid(g_b)

        _loop(n_chunks, chunk)

        def conv_rows(bi):
            base = bi * ROWS
            r = pl.ds(base, ROWS)
            win = _shifted(ext_ref[pl.ds(base, ROWS + HALO_B), :])
            xhat, rstd = _ln_stats(yb1_ref[r, :])
            yb2 = xhat * gb_ref[...] + bb_ref[...]
            sg = _sigmoid(yb2)
            y_ref[r, D_A:D] = (yb2 * sg).astype(BF16)
            dyb2 = dy_ref[r, D_A:D] * (sg * (1.0 + yb2 * (1.0 - sg)))
            acc_lnbg[...] += _rsum8(dyb2 * xhat)
            acc_lnbb[...] += _rsum8(dyb2)
            dyb1 = _ln_bwd(dyb2 * gb_ref[...], xhat, rstd)
            acc_cbb[...] += _rsum8(dyb1)
            dext_ref[r, :] = dyb1
            for k in range(KB):
                acc_cw[8 * k:8 * k + 8, :] += _rsum8(dyb1 * _tap(win, 2 + k))

        _loop(tm // ROWS, conv_rows)

        def convt_rows(bi):
            base = bi * ROWS
            r = pl.ds(base, ROWS)
            dwin = _shifted(dext_ref[pl.ds(base, ROWS + HALO_B), :])
            dyb0 = jnp.zeros((ROWS, D_B), F32)
            for k in range(KB):
                dyb0 = dyb0 + _tap(dwin, 30 - k) * cw_ref[k:k + 1, :]
            a_b = h_ref[r, 2 * D_A:2 * D_A + D_B]
            sg = _sigmoid(h_ref[r, 2 * D_A + D_B:D_IN])
            da_b = dyb0 * sg
            dg_b = dyb0 * a_b * sg * (1.0 - sg)
            acc_bin[:, 2 * D_A:2 * D_A + D_B] += _rsum8(da_b)
            acc_bin[:, 2 * D_A + D_B:D_IN] += _rsum8(dg_b)
            dh_ref[r, 2 * D_A:2 * D_A + D_B] = da_b.astype(BF16)
            dh_ref[r, 2 * D_A + D_B:D_IN] = dg_b.astype(BF16)

        _loop(tm // ROWS, convt_rows)
        dext_ref[tm:tm + HALO_B, :] = dext_ref[0:HALO_B, :]

        acc_wout[...] += _tn(y_ref[...], dmb_ref[...])
        xt = x_ref[...].T.astype(BF16)
        dh_blocks = [dh_ref[:, j * W_IN_BLK:(j + 1) * W_IN_BLK] for j in range(N_DEV)]
        for j in range(N_DEV):
            acc_win[j] += _nn(xt, dh_blocks[j])
        gx_ref[...] += sum(_nt(dh_blocks[j], win_ref[j]) for j in range(N_DEV))

        @pl.when(i == nt - 1)
        def _():
            cps = [pltpu.make_async_copy(acc_win, dwin_ref, sem.at[0]),
                   pltpu.make_async_copy(acc_wout, dwout_ref, sem.at[1])]
            for cp in cps:
                cp.start()
            small_ref[...] = jnp.zeros(small_ref.shape, F32)

            def put_row_vector(row0, acc):
                vec = jnp.sum(acc[...], axis=0, keepdims=True)
                for k in range(vec.shape[1] // 128):
                    small_ref[row0 + k:row0 + k + 1, :] = vec[:, k * 128:(k + 1) * 128]

            put_row_vector(S_BIN, acc_bin)
            put_row_vector(S_CBB, acc_cbb)
            put_row_vector(S_LNBG, acc_lnbg)
            put_row_vector(S_LNBB, acc_lnbb)
            put_row_vector(S_BOUT, acc_bout)
            put_row_vector(S_LN1G, acc_ln1g)
            put_row_vector(S_LN1B, acc_ln1b)
            mask = _tril_mask()
            for hd in range(HEADS):
                rows8 = slice(8 * hd, 8 * hd + 8)
                small_ref[S_LNAG + hd:S_LNAG + hd + 1, :] = jnp.sum(acc_lnag[rows8, :], axis=0, keepdims=True)
                small_ref[S_LNAB + hd:S_LNAB + hd + 1, :] = jnp.sum(acc_lnab[rows8, :], axis=0, keepdims=True)
                small_ref[S_WS + hd * CHUNK:S_WS + (hd + 1) * CHUNK, :] = jnp.where(mask, acc_ws[hd], 0.0)
                small_ref[S_BS + hd:S_BS + hd + 1, :] = jnp.sum(acc_bs[hd].T, axis=0, keepdims=True)
            for k in range(KB):
                dcw_ref[k:k + 1, :] = jnp.sum(acc_cw[8 * k:8 * k + 8, :], axis=0, keepdims=True)
            for cp in cps:
                cp.wait()
            for cp in _chip_copies(p_refs, land_refs, send_sems, recv_sems):
                cp.wait()

    rev = lambda i: nt - 1 - i
    row = lambda w: pl.BlockSpec((tm, w), lambda i: (rev(i), 0))
    return pl.pallas_call(
        body, name="mix_backward", grid=(nt,),
        in_specs=[row(D), row(D_IN),
                  pl.BlockSpec((HALO_B, 2 * D_B), lambda i: (jnp.maximum(rev(i) * halo_blocks - 1, 0), 1)),
                  row(D_B), pl.BlockSpec((N_F, tm, D), lambda i: (0, rev(i), 0)),
                  row(D), row(D), row(128), _resident(win_g.shape), _full(ln_a_g.shape),
                  _full(ln_a_b.shape), _full(w_spatial.shape), _full(bst.shape), _full(conv_b_w.shape),
                  _full(ln_b_g.shape), _full(ln_b_b.shape),
                  _resident(wout.shape), _full(ln1_g.shape)] + [ANY] * n_p,
        out_specs=[row(D), ANY, ANY, _full((KB, D_B)), _full((S_MIX_ROWS, 128))] + [ANY] * n_p,
        out_shape=[jax.ShapeDtypeStruct((t, D), F32), jax.ShapeDtypeStruct((N_DEV, D, W_IN_BLK), F32),
                   jax.ShapeDtypeStruct((D, D), F32), jax.ShapeDtypeStruct((KB, D_B), F32),
                   jax.ShapeDtypeStruct((S_MIX_ROWS, 128), F32)]
        + [jax.ShapeDtypeStruct(p.shape, BF16) for p in ffn_partials],
        scratch_shapes=[pltpu.VMEM((tm + HALO_B, D_B), F32), pltpu.VMEM((tm + HALO_B, D_B), F32),
                        pltpu.VMEM((tm, D), BF16), pltpu.VMEM((tm, D), F32), pltpu.VMEM((tm, D_IN), BF16),
                        pltpu.VMEM((tm, D), BF16),
                        pltpu.VMEM((HEADS, CHUNK, CHUNK), BF16),
                        pltpu.VMEM((N_DEV, D, W_IN_BLK), F32), pltpu.VMEM((D, D), F32),
                        pltpu.VMEM((8, D_IN), F32), pltpu.VMEM((8 * HEADS, HEAD_DIM), F32),
                        pltpu.VMEM((8 * HEADS, HEAD_DIM), F32), pltpu.VMEM((HEADS, CHUNK, CHUNK), F32),
                        pltpu.VMEM((HEADS, CHUNK, CHUNK), F32), pltpu.VMEM((8, D_B), F32),
                        pltpu.VMEM((8, D_B), F32), pltpu.VMEM((8, D_B), F32), pltpu.VMEM((8, D), F32),
                        pltpu.VMEM((8, D), F32), pltpu.VMEM((8, D), F32), pltpu.VMEM((8 * KB, D_B), F32),
                        pltpu.SemaphoreType.DMA((2,)),
                        pltpu.SemaphoreType.DMA((n_p, 3)), pltpu.SemaphoreType.DMA((n_p, 3))],
        compiler_params=_params(("arbitrary",)),
    )(x, h, h, yb1, dx1p, dr2, xhat1, rstd1, win_g, ln_a_g, ln_a_b, w_spatial, bst, conv_b_w,
      ln_b_g, ln_b_b, wout, ln1_g, *ffn_partials)


def _rows128(a):
    return a.reshape(-1, 128)


def _pack_conv(cb, cf):
    out = jnp.zeros((40, 768), F32)
    out = out.at[0:KB, 0:64].set(cb)
    return out.at[32:32 + KF, 0:W_UP_BLK].set(cf)


def kernel(x, w_in, b_in, ln_a_g, ln_a_b, w_spatial, b_spatial, conv_b_w, conv_b_b, ln_b_g, ln_b_b, w_out, b_out, ln1_g, ln1_b, w_up, conv_f_w, conv_f_b, w_down, ln2_g, ln2_b, loss_target, m_w_in, m_b_in, m_ln_a_g, m_ln_a_b, m_w_spatial, m_b_spatial, m_conv_b_w, m_conv_b_b, m_ln_b_g, m_ln_b_b, m_w_out, m_b_out, m_ln1_g, m_ln1_b, m_w_up, m_conv_f_w, m_conv_f_b, m_w_down, m_ln2_g, m_ln2_b, v_w_in, v_b_in, v_ln_a_g, v_ln_a_b, v_w_spatial, v_b_spatial, v_conv_b_w, v_conv_b_b, v_ln_b_g, v_ln_b_b, v_w_out, v_b_out, v_ln1_g, v_ln1_b, v_w_up, v_conv_f_w, v_conv_f_b, v_w_down, v_ln2_g, v_ln2_b):
    t = x.shape[1]
    x2 = x.reshape(t, D)
    target = loss_target.reshape(t, D)
    tm_fwd = min(t, 512)
    tm_bwd = min(t, 256)
    tm_ffn_bwd = min(t, 512)

    xi, yi, ci = _mesh_pos()
    jidx = jnp.stack([_lid(px, py, ci) for px, py in _chip_patterns(xi, yi)]).astype(jnp.int32)

    win_g, wout_g, conv_g, sup, sdown = all_gather_mixer_weights(
        w_in, w_out, w_up.T, w_down, _pack_conv(conv_b_w, conv_f_w))
    wout_full = wout_g.reshape(D, D)
    conv_b_full = conv_g[:, 0:KB, 0:64].transpose(1, 0, 2).reshape(KB, D_B)
    cfw = conv_g[:, 32:32 + KF, 0:W_UP_BLK]
    cfb = conv_f_b.reshape(N_DEV, W_UP_BLK)
    row = lambda a: a.reshape(1, -1)
    bst = b_spatial.T

    h, xhat1, rstd1, yb1, wup_g, wdown_g = mix_forward(
        x2, win_g, row(b_in), ln_a_g, ln_a_b, w_spatial, bst, conv_b_full, row(conv_b_b),
        row(ln_b_g), row(ln_b_b), wout_full, row(b_out), row(ln1_g), row(ln1_b), sup, sdown, tm_fwd)
    wdown4 = wdown_g.reshape(N_F, W_UP_BLK, D)
    hu, gv, dr2, loss_part, s_ln2 = ffn_forward(
        xhat1, row(ln1_g), row(ln1_b), wup_g, cfw, cfb, wdown4, row(ln2_g), row(ln2_b), target, tm_bwd)

    dwup, dwdown, dcfw, dcfb, dx1p = ffn_backward(
        dr2, xhat1, row(ln1_g), row(ln1_b), hu, gv, wup_g, cfw, wdown4, tm_ffn_bwd)
    ffn_grads = [dwup.reshape(N_DEV, W_UP_BLK, D), dwdown.reshape(N_DEV, D_FF // N_DEV, D)]
    ffn_lands = exchange_with_sibling("exchange_with_sibling_ffn", ffn_grads)
    ffn_partials = [chip_partials("chip_partials_" + nm, g, l, jidx, rb)
                    for nm, g, l, rb in zip(["w_up", "w_down"], ffn_grads, ffn_lands, [352, 352])]
    grad_x, dwin, dwout, dcw, s_mix, *ffn_recvs = mix_backward(
        x2, h, yb1, dx1p, dr2, xhat1, rstd1, win_g, ln_a_g, ln_a_b, w_spatial, bst,
        conv_b_full, row(ln_b_g), row(ln_b_b), wout_full, row(ln1_g), ffn_partials, tm_bwd)

    dcfb_rows = jnp.pad(dcfb.reshape(-1, 128), ((0, 4), (0, 0)))
    svec = jnp.concatenate([s_mix, dcfb_rows, s_ln2, loss_part], axis=0)
    dconv = jnp.zeros((N_DEV, 40, 768), F32)
    dconv = dconv.at[:, 0:KB, 0:64].set(dcw.reshape(KB, N_DEV, 64).transpose(1, 0, 2))
    dconv = dconv.at[:, 32:32 + KF, 0:W_UP_BLK].set(dcfw.reshape(N_DEV, KF, W_UP_BLK))
    mix_grads = [dwin, dwout.reshape(N_DEV, D // N_DEV, D), dconv]
    *mix_lands, sv_land = exchange_with_sibling("exchange_with_sibling_mixer", mix_grads, svec)
    mix_partials = [chip_partials("chip_partials_" + nm, g, l, jidx, rb)
                    for nm, g, l, rb in zip(["w_in", "w_out", "conv"], mix_grads, mix_lands, [512, 128, 40])]
    *mix_recvs, sv_slots = exchange_between_chips(mix_partials, svec, sv_land)
    names = ["w_in", "w_out", "conv", "w_up", "w_down"]
    grads = mix_grads + ffn_grads
    lands = mix_lands + list(ffn_lands)
    recvs = mix_recvs + ffn_recvs
    row_blocks = [512, 128, 40, 352, 352]

    shard_w = [w_in, w_out, _pack_conv(conv_b_w, conv_f_w), w_up.T, w_down]
    shard_m = [m_w_in, m_w_out, _pack_conv(m_conv_b_w, m_conv_f_w), m_w_up.T, m_w_down]
    shard_v = [v_w_in, v_w_out, _pack_conv(v_conv_b_w, v_conv_f_w), v_w_up.T, v_w_down]
    big = {}
    for nm, g, l, r, w, m, v, rb in zip(names, grads, lands, recvs, shard_w, shard_m, shard_v, row_blocks):
        big[nm] = reduce_and_adamw("reduce_adamw_" + nm, g, l, r, w, m, v, jidx, rb)
    big["w_up"] = [o.T for o in big["w_up"]]
    for k in range(4):
        packed = big["conv"][k]
        big.setdefault("conv_b_w", []).append(packed[0:KB, 0:64])
        big.setdefault("conv_f_w", []).append(packed[32:32 + KF, 0:W_UP_BLK])

    small_w = dict(b_in=b_in, ln_a_g=ln_a_g, ln_a_b=ln_a_b, w_spatial=w_spatial, b_spatial=b_spatial,
                   conv_b_b=conv_b_b, ln_b_g=ln_b_g, ln_b_b=ln_b_b, b_out=b_out, ln1_g=ln1_g,
                   ln1_b=ln1_b, conv_f_b=conv_f_b, ln2_g=ln2_g, ln2_b=ln2_b)
    small_m = dict(b_in=m_b_in, ln_a_g=m_ln_a_g, ln_a_b=m_ln_a_b, w_spatial=m_w_spatial,
                   b_spatial=m_b_spatial, conv_b_b=m_conv_b_b, ln_b_g=m_ln_b_g, ln_b_b=m_ln_b_b,
                   b_out=m_b_out, ln1_g=m_ln1_g, ln1_b=m_ln1_b, conv_f_b=m_conv_f_b, ln2_g=m_ln2_g,
                   ln2_b=m_ln2_b)
    small_v = dict(b_in=v_b_in, ln_a_g=v_ln_a_g, ln_a_b=v_ln_a_b, w_spatial=v_w_spatial,
                   b_spatial=v_b_spatial, conv_b_b=v_conv_b_b, ln_b_g=v_ln_b_g, ln_b_b=v_ln_b_b,
                   b_out=v_b_out, ln1_g=v_ln1_g, ln1_b=v_ln1_b, conv_f_b=v_conv_f_b, ln2_g=v_ln2_g,
                   ln2_b=v_ln2_b)
    order = [nm for nm, _, _ in SMALL_LAYOUT]
    small_out = small_adamw(sv_slots, [_rows128(small_w[nm]) for nm in order],
                            [_rows128(small_m[nm]) for nm in order], [_rows128(small_v[nm]) for nm in order])
    n_small = len(order)
    small = {nm: [small_out[k * n_small + p].reshape(small_w[nm].shape) for k in range(4)]
             for p, nm in enumerate(order)}
    loss = jnp.sum(small_out[4 * n_small]) * (0.5 / D)

    weights = ["w_in", "b_in", "ln_a_g", "ln_a_b", "w_spatial", "b_spatial", "conv_b_w", "conv_b_b",
               "ln_b_g", "ln_b_b", "w_out", "b_out", "ln1_g", "ln1_b", "w_up", "conv_f_w", "conv_f_b",
               "w_down", "ln2_g", "ln2_b"]
    result = lambda nm, k: big[nm][k] if nm in big else small[nm][k]
    return (loss, grad_x.reshape(x.shape), *[result(nm, 0) for nm in weights],
            *[result(nm, 1) for nm in weights], *[result(nm, 2) for nm in weights],
            *[result(nm, 3) for nm in weights])
```

```python
import functools
import math

import jax
import jax.numpy as jnp
from jax import lax
from jax.experimental import pallas as pl
from jax.experimental.pallas import tpu as pltpu

F32 = jnp.float32
BF16 = jnp.bfloat16

D = 1024
D_A = 512
D_B = 512
HEADS = 4
HEAD_DIM = 128
CHUNK = 128
KB = 31
KF = 3
D_FF = 2816
D_IN = 2048
N_DEV = 8
W_IN_BLK = D_IN // N_DEV
W_UP_BLK = 2 * D_FF // N_DEV
N_F = 4
LN_EPS = 1e-5
ALPHA = 2.0 ** 0.25

ADAM_LR = 0.001
ADAM_B1 = 0.9
ADAM_B2 = 0.999
ADAM_EPS = 1e-08
ADAM_WD = 0.01
ADAM_STEP = 10

INV_SQRT2 = 1.0 / math.sqrt(2.0)
INV_SQRT_2PI = 1.0 / math.sqrt(2.0 * math.pi)

HALO_B = 32
HALO_F = 8
ROWS = 64
VMEM_LIMIT = 58 * 1024 * 1024

MESH = pl.DeviceIdType.MESH
ANY = pl.BlockSpec(memory_space=pl.ANY)
VMEM = pl.BlockSpec(memory_space=pltpu.VMEM)

S_BIN, S_LNAG, S_LNAB, S_WS, S_BS, S_CBB, S_LNBG, S_LNBB, S_BOUT, S_LN1G, S_LN1B = (
    0, 16, 24, 32, 544, 552, 560, 568, 576, 584, 592)
S_MIX_ROWS = 600
S_CFB = 600
S_LN2G = 648
S_LN2B = 656
S_LOSS = 664
S_ROWS = 672


def _tn(a, b):
    return lax.dot_general(a, b, (((0,), (0,)), ((), ())), preferred_element_type=F32)


def _nt(a, b):
    return lax.dot_general(a, b, (((1,), (1,)), ((), ())), preferred_element_type=F32)


def _nn(a, b):
    return jnp.dot(a, b, preferred_element_type=F32)


def _sigmoid(x):
    return 1.0 / (1.0 + jnp.exp(-x))


def _ln_stats(x):
    mu = jnp.mean(x, axis=-1, keepdims=True)
    xc = x - mu
    var = jnp.mean(xc * xc, axis=-1, keepdims=True)
    rstd = lax.rsqrt(var + LN_EPS)
    return xc * rstd, rstd


def _ln_bwd(dxhat, xhat, rstd):
    m1 = jnp.mean(dxhat, axis=-1, keepdims=True)
    m2 = jnp.mean(dxhat * xhat, axis=-1, keepdims=True)
    return rstd * (dxhat - m1 - xhat * m2)


def _rsum8(x):
    r, n = x.shape
    return x.reshape(r // 8, 8, n).sum(axis=0)


def _rows(i, n=ROWS):
    return pl.ds(i * n, n)


def _loop(n, body):
    for i in range(n):
        body(i)


def _tril_mask():
    r = lax.broadcasted_iota(jnp.int32, (CHUNK, CHUNK), 0)
    c = lax.broadcasted_iota(jnp.int32, (CHUNK, CHUNK), 1)
    return c <= r


def _mixer_a_fwd(hu, hv, ga_ref, ba_ref, wsm_ref, bst_ref):
    cdf_u = 0.5 * (1.0 + lax.erf(hu * INV_SQRT2))
    cdf_v = 0.5 * (1.0 + lax.erf(hv * INV_SQRT2))
    u = hu * cdf_u
    v = hv * cdf_v
    xhats, rstds, vns, svs = [], [], [], []
    for hd in range(HEADS):
        sl = slice(hd * HEAD_DIM, (hd + 1) * HEAD_DIM)
        xhat, rstd = _ln_stats(v[:, sl])
        vn = (xhat * ga_ref[hd:hd + 1, :] + ba_ref[hd:hd + 1, :]).astype(BF16)
        sv = _nn(wsm_ref[hd], vn) + bst_ref[:, hd:hd + 1]
        xhats.append(xhat)
        rstds.append(rstd)
        vns.append(vn)
        svs.append(sv)
    return u, cdf_u, cdf_v, xhats, rstds, vns, svs


def _shifted(win):
    n = win.shape[0]
    return [win] + [pltpu.roll(win, n - s, 0) for s in range(1, 8)]


def _tap(shifted, offset):
    s = offset % 8
    return shifted[s][offset - s:offset - s + ROWS, :]


def _conv_b_block(ext_ref, base, cw_ref):
    win = _shifted(ext_ref[pl.ds(base, ROWS + HALO_B), :])
    acc = jnp.zeros((ROWS, D_B), F32)
    for k in range(KB):
        acc = acc + _tap(win, 2 + k) * cw_ref[k:k + 1, :]
    return acc, win


def _taps_f(win):
    n = ROWS + HALO_F
    return [pltpu.roll(win, n - 6, 0)[0:ROWS, :], pltpu.roll(win, n - 7, 0)[0:ROWS, :], win[8:n, :]]


def _params(sem, **kw):
    return pltpu.CompilerParams(dimension_semantics=sem, vmem_limit_bytes=VMEM_LIMIT, **kw)


def _resident(shape):
    zeros = (0,) * len(shape)
    return pl.BlockSpec(shape, lambda *_: zeros, pipeline_mode=pl.Buffered(1))


def _full(shape):
    zeros = (0,) * len(shape)
    return pl.BlockSpec(shape, lambda *_: zeros)


def _mesh_pos():
    return lax.axis_index("x"), lax.axis_index("y"), lax.axis_index("c")


def _chip_patterns(x, y):
    return [(x, y), (1 - x, y), (x, 1 - y), (1 - x, 1 - y)]


def _lid(x, y, c):
    return 4 * x + 2 * y + c


def _gather_copy(outs, send_sems, recv_sems, a, k, block, to, src=None):
    blk = outs[a].at[_lid(*block)]
    return pltpu.make_async_remote_copy(
        src_ref=blk if src is None else src, dst_ref=blk,
        send_sem=send_sems.at[a, k], recv_sem=recv_sems.at[a, k], device_id=to, device_id_type=MESH)


def _gather_start(mine, outs, send_sems, recv_sems, local_sems):
    x, y, c = _mesh_pos()
    me, sib = (x, y, c), (x, y, 1 - c)
    for a in range(len(mine)):
        pltpu.make_async_copy(mine[a], outs[a].at[_lid(*me)], local_sems.at[a]).start()
        _gather_copy(outs, send_sems, recv_sems, a, 0, me, sib, src=mine[a]).start()
        for j, chip in enumerate(_chip_patterns(x, y)[1:]):
            _gather_copy(outs, send_sems, recv_sems, a, 1 + j, me, (*chip, c), src=mine[a]).start()


def _gather_finish(mine, outs, send_sems, recv_sems, local_sems):
    x, y, c = _mesh_pos()
    me, sib = (x, y, c), (x, y, 1 - c)
    chips = _chip_patterns(x, y)[1:]
    n = len(mine)
    copy = functools.partial(_gather_copy, outs, send_sems, recv_sems)
    passed = []
    for j, chip in enumerate(chips):
        for a in range(n):
            copy(a, 1 + j, (*chip, c), me).wait_recv()
            cp = copy(a, 4 + j, (*chip, c), sib)
            cp.start()
            passed.append(cp)
    for a in range(n):
        copy(a, 0, sib, me).wait_recv()
        for j, chip in enumerate(chips):
            copy(a, 4 + j, (*chip, 1 - c), me).wait_recv()
        for k in range(4):
            copy(a, k, me, sib, src=mine[a]).wait_send()
        pltpu.make_async_copy(mine[a], outs[a].at[_lid(*me)], local_sems.at[a]).wait()
    for cp in passed:
        cp.wait_send()


def _gather_scratch(n):
    return [pltpu.SemaphoreType.DMA((n, 7)), pltpu.SemaphoreType.DMA((n, 7)), pltpu.SemaphoreType.DMA((n,))]


def all_gather_mixer_weights(w_in, w_out, w_up, w_down, convp):
    srcs = [w_in, w_out, convp]
    n = len(srcs)

    def body(win_ref, wout_ref, convp_ref, wup_ref, wdown_ref,
             gin_ref, gout_ref, gconv_ref, sup_ref, sdown_ref,
             sin_ref, sout_ref, send_sems, recv_sems, local_sems):
        sin_ref[...] = win_ref[...].astype(BF16)
        sout_ref[...] = wout_ref[...].astype(BF16)
        mine = [sin_ref, sout_ref, convp_ref]
        outs = [gin_ref, gout_ref, gconv_ref]
        _gather_start(mine, outs, send_sems, recv_sems, local_sems)
        sup_ref[...] = wup_ref[...].astype(BF16)
        sdown_ref[...] = wdown_ref[...].astype(BF16)
        _gather_finish(mine, outs, send_sems, recv_sems, local_sems)

    return pl.pallas_call(
        body, name="all_gather_mixer_weights",
        out_shape=[jax.ShapeDtypeStruct((N_DEV,) + w_in.shape, BF16),
                   jax.ShapeDtypeStruct((N_DEV,) + w_out.shape, BF16),
                   jax.ShapeDtypeStruct((N_DEV,) + convp.shape, F32),
                   jax.ShapeDtypeStruct(w_up.shape, BF16), jax.ShapeDtypeStruct(w_down.shape, BF16)],
        in_specs=[VMEM] * 5, out_specs=[ANY] * n + [VMEM, VMEM],
        scratch_shapes=[pltpu.VMEM(w_in.shape, BF16), pltpu.VMEM(w_out.shape, BF16)] + _gather_scratch(n),
        compiler_params=pltpu.CompilerParams(vmem_limit_bytes=VMEM_LIMIT),
    )(w_in, w_out, convp, w_up, w_down)


def exchange_with_sibling(name, grads, svec=None):
    srcs = list(grads) + ([] if svec is None else [svec])
    n, n_all = len(grads), len(srcs)

    def body(*refs):
        src, land = refs[:n_all], refs[n_all:2 * n_all]
        send_sems, recv_sems = refs[2 * n_all:]
        x, y, c = _mesh_pos()
        sib = (x, y, 1 - c)
        copies = []
        for a in range(n):
            for k, (px, py) in enumerate(_chip_patterns(x, y)):
                copies.append(pltpu.make_async_remote_copy(
                    src_ref=src[a].at[_lid(px, py, 1 - c)], dst_ref=land[a].at[k],
                    send_sem=send_sems.at[a, k], recv_sem=recv_sems.at[a, k],
                    device_id=sib, device_id_type=MESH))
        if svec is not None:
            copies.append(pltpu.make_async_remote_copy(
                src_ref=src[n], dst_ref=land[n], send_sem=send_sems.at[n, 0], recv_sem=recv_sems.at[n, 0],
                device_id=sib, device_id_type=MESH))
        for cp in copies:
            cp.start()
        for cp in copies:
            cp.wait()

    return pl.pallas_call(
        body, name=name,
        out_shape=[jax.ShapeDtypeStruct((4,) + g.shape[1:], F32) for g in grads]
        + ([] if svec is None else [jax.ShapeDtypeStruct(svec.shape, F32)]),
        in_specs=[ANY] * n_all, out_specs=[ANY] * n_all,
        scratch_shapes=[pltpu.SemaphoreType.DMA((n_all, 4)), pltpu.SemaphoreType.DMA((n_all, 4))],
    )(*srcs)


def _chip_copies(p, land, send_sems, recv_sems):
    x, y, c = _mesh_pos()
    return [pltpu.make_async_remote_copy(
        src_ref=p[a].at[k], dst_ref=land[a].at[k], send_sem=send_sems.at[a, k], recv_sem=recv_sems.at[a, k],
        device_id=(px, py, c), device_id_type=MESH)
        for k, (px, py) in enumerate(_chip_patterns(x, y)[1:]) for a in range(len(p))]


def exchange_between_chips(partials, svec, sv_land):
    n = len(partials)

    def body(*refs):
        p = refs[:n]
        sv_ref, svl_ref = refs[n], refs[n + 1]
        land = refs[n + 2:2 * n + 2]
        sv_slots = refs[2 * n + 2]
        chip_sv, send_sems, recv_sems, sv_send, sv_recv, local_sem = refs[2 * n + 3:]
        x, y, c = _mesh_pos()
        q = 2 * x + y
        chip_sv[...] = sv_ref[...] + svl_ref[...]
        local = pltpu.make_async_copy(chip_sv, sv_slots.at[q], local_sem)
        local.start()
        copies = _chip_copies(p, land, send_sems, recv_sems)
        for cp in copies:
            cp.start()
        sv_copies = []
        for k, (px, py) in enumerate(_chip_patterns(x, y)[1:]):
            cp = pltpu.make_async_remote_copy(
                src_ref=chip_sv, dst_ref=sv_slots.at[q],
                send_sem=sv_send.at[k], recv_sem=sv_recv.at[k],
                device_id=(px, py, c), device_id_type=MESH)
            cp.start()
            sv_copies.append(cp)
        for cp in copies:
            cp.wait()
        for k, (px, py) in enumerate(_chip_patterns(x, y)[1:]):
            sv_copies[k].wait_send()
            pltpu.make_async_remote_copy(
                src_ref=chip_sv, dst_ref=sv_slots.at[2 * px + py],
                send_sem=sv_send.at[k], recv_sem=sv_recv.at[k],
                device_id=(px, py, c), device_id_type=MESH).wait_recv()
        local.wait()

    return pl.pallas_call(
        body, name="exchange_between_chips",
        out_shape=[jax.ShapeDtypeStruct(p.shape, BF16) for p in partials]
        + [jax.ShapeDtypeStruct((4,) + svec.shape, F32)],
        in_specs=[ANY] * n + [VMEM, VMEM], out_specs=[ANY] * (n + 1),
        scratch_shapes=[pltpu.VMEM(svec.shape, F32),
                        pltpu.SemaphoreType.DMA((n, 3)), pltpu.SemaphoreType.DMA((n, 3)),
                        pltpu.SemaphoreType.DMA((3,)), pltpu.SemaphoreType.DMA((3,)),
                        pltpu.SemaphoreType.DMA],
    )(*partials, svec, sv_land)


def chip_partials(name, g, land, jidx, rb):
    _, r, c = g.shape

    def body(j_ref, g_ref, l_ref, o_ref):
        o_ref[...] = (g_ref[...] + l_ref[...]).astype(BF16)

    return pl.pallas_call(
        body, name=name,
        out_shape=jax.ShapeDtypeStruct((3, r, c), BF16),
        grid_spec=pltpu.PrefetchScalarGridSpec(
            num_scalar_prefetch=1, grid=(3, r // rb),
            in_specs=[pl.BlockSpec((1, rb, c), lambda k, i, j: (j[1 + k], i, 0)),
                      pl.BlockSpec((1, rb, c), lambda k, i, j: (1 + k, i, 0))],
            out_specs=pl.BlockSpec((1, rb, c), lambda k, i, j: (k, i, 0))),
        compiler_params=_params(("arbitrary", "arbitrary")),
    )(jidx, g, land)


def _adamw(w, g, m, v):
    m2 = ADAM_B1 * m + (1.0 - ADAM_B1) * g
    v2 = ADAM_B2 * v + (1.0 - ADAM_B2) * (g * g)
    m_hat = m2 / (1.0 - ADAM_B1 ** ADAM_STEP)
    v_hat = v2 / (1.0 - ADAM_B2 ** ADAM_STEP)
    delta = -ADAM_LR * (m_hat / (jnp.sqrt(v_hat) + ADAM_EPS) + ADAM_WD * w)
    return delta, m2, v2


def reduce_and_adamw(name, g, land, recv, w, m, v, jidx, rb):
    _, r, c = g.shape

    def body(j_ref, g_ref, l_ref, r_ref, w_ref, m_ref, v_ref, go_ref, do_ref, mo_ref, vo_ref):
        grad = (g_ref[0] + l_ref[0]) + r_ref[0].astype(F32) + r_ref[1].astype(F32) + r_ref[2].astype(F32)
        delta, m2, v2 = _adamw(w_ref[...], grad, m_ref[...], v_ref[...])
        go_ref[...] = grad
        do_ref[...] = delta
        mo_ref[...] = m2
        vo_ref[...] = v2

    blk = pl.BlockSpec((rb, c), lambda i, j: (i, 0))
    return pl.pallas_call(
        body, name=name,
        out_shape=[jax.ShapeDtypeStruct((r, c), F32)] * 4,
        grid_spec=pltpu.PrefetchScalarGridSpec(
            num_scalar_prefetch=1, grid=(r // rb,),
            in_specs=[pl.BlockSpec((1, rb, c), lambda i, j: (j[0], i, 0)),
                      pl.BlockSpec((1, rb, c), lambda i, j: (0, i, 0)),
                      pl.BlockSpec((3, rb, c), lambda i, j: (0, i, 0)),
                      blk, blk, blk],
            out_specs=[blk] * 4),
        compiler_params=_params(("arbitrary",)),
    )(jidx, g, land, recv, w, m, v)


SMALL_LAYOUT = [
    ("b_in", S_BIN, 16), ("ln_a_g", S_LNAG, 4), ("ln_a_b", S_LNAB, 4), ("w_spatial", S_WS, 512),
    ("b_spatial", S_BS, 4), ("conv_b_b", S_CBB, 4), ("ln_b_g", S_LNBG, 4), ("ln_b_b", S_LNBB, 4),
    ("b_out", S_BOUT, 8), ("ln1_g", S_LN1G, 8), ("ln1_b", S_LN1B, 8), ("conv_f_b", S_CFB, 44),
    ("ln2_g", S_LN2G, 8), ("ln2_b", S_LN2B, 8),
]


def small_adamw(sv_slots, ws, ms, vs):
    n = len(SMALL_LAYOUT)

    def body(*refs):
        s_ref = refs[0]
        w_refs, m_refs, v_refs = refs[1:1 + n], refs[1 + n:1 + 2 * n], refs[1 + 2 * n:1 + 3 * n]
        outs = refs[1 + 3 * n:]
        for p, (_, row0, rows) in enumerate(SMALL_LAYOUT):
            sl = pl.ds(row0, rows)
            grad = ((s_ref[0, sl, :] + s_ref[1, sl, :]) + s_ref[2, sl, :]) + s_ref[3, sl, :]
            delta, m2, v2 = _adamw(w_refs[p][...], grad, m_refs[p][...], v_refs[p][...])
            outs[p][...] = grad
            outs[n + p][...] = delta
            outs[2 * n + p][...] = m2
            outs[3 * n + p][...] = v2
        sl = pl.ds(S_LOSS, 8)
        outs[4 * n][...] = ((s_ref[0, sl, :] + s_ref[1, sl, :]) + s_ref[2, sl, :]) + s_ref[3, sl, :]

    shapes = [jax.ShapeDtypeStruct((rows, 128), F32) for _, _, rows in SMALL_LAYOUT]
    return pl.pallas_call(
        body, name="small_adamw", out_shape=shapes * 4 + [jax.ShapeDtypeStruct((8, 128), F32)],
        in_specs=[VMEM] * (1 + 3 * n), out_specs=[VMEM] * (4 * n + 1),
    )(sv_slots, *ws, *ms, *vs)


def mix_forward(x, win_g, b_in, ln_a_g, ln_a_b, w_spatial, bst, conv_b_w, conv_b_b, ln_b_g, ln_b_b,
                wout, b_out, ln1_g, ln1_b, sup, sdown, tm):
    t = x.shape[0]
    nt = t // tm
    n_chunks = tm // CHUNK

    def body(x_ref, win_ref, bin_ref, ga_ref, ba_ref, ws_ref, bst_ref, cw_ref, cb_ref, gb_ref,
             bb_ref, wout_ref, bout_ref, g1_ref, b1_ref, sup_ref, sdown_ref,
             h_ref, xhat1_ref, rstd1_ref, yb1_ref, gup_ref, gdown_ref,
             ext_ref, y_ref, wsm_ref, send_sems, recv_sems, local_sems):
        i = pl.program_id(0)
        gather = ([sup_ref, sdown_ref], [gup_ref, gdown_ref], send_sems, recv_sems, local_sems)

        @pl.when(i == 0)
        def _():
            _gather_start(*gather)
            ext_ref[0:HALO_B, :] = jnp.zeros((HALO_B, D_B), F32)
            mask = _tril_mask()
            for hd in range(HEADS):
                wsm_ref[hd] = jnp.where(mask, ws_ref[hd], 0.0).astype(BF16)

        xb = x_ref[...].astype(BF16)
        for j in range(N_DEV):
            cols = slice(j * W_IN_BLK, (j + 1) * W_IN_BLK)
            h_ref[:, cols] = _nn(xb, win_ref[j]) + bin_ref[:, cols]

        def chunk(ci):
            r = _rows(ci, CHUNK)
            u, _, _, _, _, _, svs = _mixer_a_fwd(h_ref[r, 0:D_A], h_ref[r, D_A:2 * D_A],
                                                 ga_ref, ba_ref, wsm_ref, bst_ref)
            for hd in range(HEADS):
                sl = slice(hd * HEAD_DIM, (hd + 1) * HEAD_DIM)
                y_ref[r, sl] = (u[:, sl] * svs[hd]).astype(BF16)
            a_b = h_ref[r, 2 * D_A:2 * D_A + D_B]
            g_b = h_ref[r, 2 * D_A + D_B:D_IN]
            ext_ref[pl.ds(HALO_B + ci * CHUNK, CHUNK), :] = a_b * _sigmoid(g_b)

        _loop(n_chunks, chunk)

        def conv_rows(bi):
            base = bi * ROWS
            acc, _ = _conv_b_block(ext_ref, base, cw_ref)
            yb1 = acc + cb_ref[...]
            yb1_ref[pl.ds(base, ROWS), :] = yb1
            xhat, _ = _ln_stats(yb1)
            yb2 = xhat * gb_ref[...] + bb_ref[...]
            y_ref[pl.ds(base, ROWS), D_A:D] = (yb2 * _sigmoid(yb2)).astype(BF16)

        _loop(tm // ROWS, conv_rows)
        ext_ref[0:HALO_B, :] = ext_ref[tm:tm + HALO_B, :]

        mix = _nn(y_ref[...], wout_ref[...]) + bout_ref[...]
        xhat1, rstd1 = _ln_stats(ALPHA * x_ref[...] + mix)
        xhat1_ref[...] = xhat1
        rstd1_ref[...] = jnp.broadcast_to(rstd1, (tm, 128))

        @pl.when(i == nt - 1)
        def _():
            _gather_finish(*gather)

    row = lambda w: pl.BlockSpec((tm, w), lambda i: (i, 0))
    return pl.pallas_call(
        body, name="mix_forward", grid=(nt,),
        in_specs=[row(D), _resident(win_g.shape), _full(b_in.shape), _full(ln_a_g.shape),
                  _full(ln_a_b.shape), _full(w_spatial.shape), _full(bst.shape),
                  _full(conv_b_w.shape), _full(conv_b_b.shape), _full(ln_b_g.shape),
                  _full(ln_b_b.shape), _resident(wout.shape), _full(b_out.shape),
                  _full(ln1_g.shape), _full(ln1_b.shape), ANY, ANY],
        out_specs=[row(D_IN), row(D), row(128), row(D_B), ANY, ANY],
        out_shape=[jax.ShapeDtypeStruct((t, D_IN), F32), jax.ShapeDtypeStruct((t, D), F32),
                   jax.ShapeDtypeStruct((t, 128), F32), jax.ShapeDtypeStruct((t, D_B), F32),
                   jax.ShapeDtypeStruct((N_DEV,) + sup.shape, BF16),
                   jax.ShapeDtypeStruct((N_DEV,) + sdown.shape, BF16)],
        scratch_shapes=[pltpu.VMEM((tm + HALO_B, D_B), F32), pltpu.VMEM((tm, D), BF16),
                        pltpu.VMEM((HEADS, CHUNK, CHUNK), BF16)] + _gather_scratch(2),
        compiler_params=_params(("arbitrary",)),
    )(x, win_g, b_in, ln_a_g, ln_a_b, w_spatial, bst, conv_b_w, conv_b_b, ln_b_g, ln_b_b,
      wout, b_out, ln1_g, ln1_b, sup, sdown)


def ffn_forward(xhat1, ln1_g, ln1_b, wup_g, cfw, cfb, wdown, ln2_g, ln2_b, target, tm):
    t = xhat1.shape[0]
    nt = t // tm

    def body(xh_ref, g1_ref, b1_ref, wup_ref, cfw_ref, cfb_ref, wdown_ref, g2_ref, b2_ref, tgt_ref,
             hu_ref, gv_ref, dr2_ref, loss_ref, sln2_ref,
             x1_ref, x1b_ref, carry_ref, gbuf_ref, ffn_ref, acc_loss, acc_g2, acc_b2):
        i = pl.program_id(0)

        @pl.when(i == 0)
        def _():
            carry_ref[...] = jnp.zeros(carry_ref.shape, F32)
            acc_loss[...] = jnp.zeros(acc_loss.shape, F32)
            acc_g2[...] = jnp.zeros(acc_g2.shape, F32)
            acc_b2[...] = jnp.zeros(acc_b2.shape, F32)

        x1 = xh_ref[...] * g1_ref[...] + b1_ref[...]
        x1_ref[...] = x1
        x1b_ref[...] = x1.astype(BF16)

        def conv(j, base):
            if base == 0:
                win = jnp.concatenate([carry_ref[j], hu_ref[j, 0:ROWS, :]], axis=0)
            else:
                win = hu_ref[j, base - HALO_F:base + ROWS, :]
            taps = _taps_f(win)
            w = cfw_ref[j]
            return sum(taps[k] * w[k:k + 1, :] for k in range(KF)) + cfb_ref[j:j + 1, :]

        for f in range(N_F):
            hu_ref[f] = _nt(x1b_ref[...], wup_ref[f])
            hu_ref[N_F + f] = _nt(x1b_ref[...], wup_ref[N_F + f])

            def rows(bi, f=f):
                gate = conv(f, bi * ROWS)
                val = conv(N_F + f, bi * ROWS)
                gbuf_ref[_rows(bi), :] = (gate * _sigmoid(gate) * val).astype(BF16)
                gv_ref[f, _rows(bi), :] = gate.astype(BF16)
                gv_ref[N_F + f, _rows(bi), :] = val.astype(BF16)

            _loop(tm // ROWS, rows)
            carry_ref[f] = hu_ref[f, tm - HALO_F:tm, :]
            carry_ref[N_F + f] = hu_ref[N_F + f, tm - HALO_F:tm, :]
            part = _nn(gbuf_ref[...], wdown_ref[f])
            if f == 0:
                ffn_ref[...] = part
            else:
                ffn_ref[...] += part

        def tail(bi):
            r = _rows(bi)
            xhat2, rstd2 = _ln_stats(ALPHA * x1_ref[r, :] + ffn_ref[r, :])
            err = xhat2 * g2_ref[...] + b2_ref[...] - tgt_ref[r, :]
            e2 = _rsum8(err * err)
            acc_loss[...] += sum(e2[:, k * 128:(k + 1) * 128] for k in range(D // 128))
            dy = err * (1.0 / D)
            acc_g2[...] += _rsum8(dy * xhat2)
            acc_b2[...] += _rsum8(dy)
            dr2_ref[r, :] = _ln_bwd(dy * g2_ref[...], xhat2, rstd2)

        _loop(tm // ROWS, tail)
        loss_ref[...] = acc_loss[...]

        @pl.when(i == nt - 1)
        def _():
            dg = jnp.sum(acc_g2[...], axis=0, keepdims=True)
            db = jnp.sum(acc_b2[...], axis=0, keepdims=True)
            for k in range(D // 128):
                sln2_ref[k:k + 1, :] = dg[:, k * 128:(k + 1) * 128]
                sln2_ref[8 + k:9 + k, :] = db[:, k * 128:(k + 1) * 128]

    row = pl.BlockSpec((tm, D), lambda i: (i, 0))
    return pl.pallas_call(
        body, name="ffn_forward", grid=(nt,),
        in_specs=[row, _full(ln1_g.shape), _full(ln1_b.shape), _resident(wup_g.shape),
                  _full(cfw.shape), _full(cfb.shape), _resident(wdown.shape),
                  _full(ln2_g.shape), _full(ln2_b.shape), row],
        out_specs=[pl.BlockSpec((N_DEV, tm, W_UP_BLK), lambda i: (0, i, 0)),
                   pl.BlockSpec((N_DEV, tm, W_UP_BLK), lambda i: (0, i, 0)), row,
                   _full((8, 128)), _full((16, 128))],
        out_shape=[jax.ShapeDtypeStruct((N_DEV, t, W_UP_BLK), F32),
                   jax.ShapeDtypeStruct((N_DEV, t, W_UP_BLK), BF16), jax.ShapeDtypeStruct((t, D), F32),
                   jax.ShapeDtypeStruct((8, 128), F32), jax.ShapeDtypeStruct((16, 128), F32)],
        scratch_shapes=[pltpu.VMEM((tm, D), F32), pltpu.VMEM((tm, D), BF16),
                        pltpu.VMEM((N_DEV, HALO_F, W_UP_BLK), F32), pltpu.VMEM((tm, W_UP_BLK), BF16),
                        pltpu.VMEM((tm, D), F32), pltpu.VMEM((8, 128), F32),
                        pltpu.VMEM((8, D), F32), pltpu.VMEM((8, D), F32)],
        compiler_params=_params(("arbitrary",)),
    )(xhat1, ln1_g, ln1_b, wup_g, cfw, cfb, wdown, ln2_g, ln2_b, target)


def ffn_backward(dr2, xhat1, ln1_g, ln1_b, hu, gv, wup_g, cfw, wdown, tm):
    t = dr2.shape[0]
    nt = t // tm
    sub_rows = tm
    hu4 = hu.reshape(2, N_F, t, W_UP_BLK)
    gv4 = gv.reshape(2, N_F, t, W_UP_BLK)
    wup4 = wup_g.reshape(2, N_F, W_UP_BLK, D)
    cfw4 = cfw.reshape(2, N_F, KF, W_UP_BLK)

    def body(dr2_ref, xh_ref, g1_ref, b1_ref, hu_ref, gv_ref, wup_ref, cfw_ref, wdown_ref,
             dwup_ref, dwdown_ref, dcfw_ref, dcfb_ref, dx1_ref, land_up_ref, land_down_ref,
             x1b_ref, drb_ref, dg_ref, dextg_ref, dextv_ref, gbuf_ref,
             dhug_ref, dhuv_ref, acc_wup, acc_wdown, acc_cfw, acc_cfb, sem, send_sems, recv_sems):
        f = pl.program_id(0)
        i = pl.program_id(1)
        x, y, c = _mesh_pos()
        half = D_FF // N_DEV

        def to_sibling(fi, k, src, land_ref, shard_chip):
            d = jnp.bitwise_xor(shard_chip, 2 * x + y)
            slot = jnp.where(d == 1, 2, jnp.where(d == 2, 1, d))
            return pltpu.make_async_remote_copy(
                src_ref=src, dst_ref=land_ref.at[slot], send_sem=send_sems.at[fi, k], recv_sem=recv_sems.at[fi, k],
                device_id=(x, y, 1 - c), device_id_type=MESH)

        def up_copy(fi, g):
            return to_sibling(fi, g, dwup_ref.at[g, fi], land_up_ref, 2 * g + fi // 2)

        def down_copy(fi):
            return to_sibling(fi, 2, dwdown_ref.at[fi, pl.ds((1 - c) * half, half)], land_down_ref, fi)

        @pl.when(i == 0)
        def _():
            acc_wup[...] = jnp.zeros(acc_wup.shape, F32)
            acc_wdown[...] = jnp.zeros(acc_wdown.shape, F32)
            acc_cfw[...] = jnp.zeros(acc_cfw.shape, F32)
            acc_cfb[...] = jnp.zeros(acc_cfb.shape, F32)
            dextg_ref[tm:tm + HALO_F, :] = jnp.zeros((HALO_F, W_UP_BLK), F32)
            dextv_ref[tm:tm + HALO_F, :] = jnp.zeros((HALO_F, W_UP_BLK), F32)

        w = [cfw_ref[0, 0], cfw_ref[1, 0]]
        dext = [dextg_ref, dextv_ref]
        dhu = [dhug_ref, dhuv_ref]

        def rows1(bi):
            r = _rows(bi)
            gate = gv_ref[0, 0, r, :].astype(F32)
            val = gv_ref[1, 0, r, :].astype(F32)
            sg = _sigmoid(gate)
            silu = gate * sg
            gbuf_ref[r, :] = (silu * val).astype(BF16)
            dg = dg_ref[r, :]
            dgate = dg * val * (sg * (1.0 + gate * (1.0 - sg)))
            dval = dg * silu
            dextg_ref[r, :] = dgate
            dextv_ref[r, :] = dval
            acc_cfb[0:8, :] += _rsum8(dgate)
            acc_cfb[8:16, :] += _rsum8(dval)

        def rows2(bi):
            r = _rows(bi)
            for g in range(2):
                win = dext[g][pl.ds(bi * ROWS, ROWS + HALO_F), :]
                n = ROWS + HALO_F
                later = [pltpu.roll(win, n - 2, 0)[0:ROWS, :], pltpu.roll(win, n - 1, 0)[0:ROWS, :],
                         win[0:ROWS, :]]
                d = sum(later[k] * w[g][k:k + 1, :] for k in range(KF))
                dhu[g][r, :] = d.astype(BF16)
                pre = hu_ref[g, 0, r, :]
                for k in range(KF):
                    r0 = 8 * (g * KF + k)
                    acc_cfw[r0:r0 + 8, :] += _rsum8(later[k] * pre)

        for sub in reversed(range(tm // sub_rows)):
            rs = slice(sub * sub_rows, (sub + 1) * sub_rows)
            blocks = range(sub * sub_rows // ROWS, (sub + 1) * sub_rows // ROWS)
            x1b_ref[rs, :] = (xh_ref[rs, :] * g1_ref[...] + b1_ref[...]).astype(BF16)
            drb_ref[rs, :] = dr2_ref[rs, :].astype(BF16)
            dg_ref[rs, :] = _nt(drb_ref[rs, :], wdown_ref[0])
            for bi in blocks:
                rows1(bi)
            for bi in blocks:
                rows2(bi)
            acc_wdown[...] += _tn(gbuf_ref[rs, :], drb_ref[rs, :])
            acc_wup[0] += _tn(dhug_ref[rs, :], x1b_ref[rs, :])
            acc_wup[1] += _tn(dhuv_ref[rs, :], x1b_ref[rs, :])
            dx1_ref[0, rs, :] = (_nn(dhug_ref[rs, :], wup_ref[0, 0])
                                 + _nn(dhuv_ref[rs, :], wup_ref[1, 0])).astype(BF16)
        dextg_ref[tm:tm + HALO_F, :] = dextg_ref[0:HALO_F, :]
        dextv_ref[tm:tm + HALO_F, :] = dextv_ref[0:HALO_F, :]

        @pl.when(i == nt - 1)
        def _():
            for g in range(2):
                dcfb_ref[g, 0] = jnp.sum(acc_cfb[8 * g:8 * g + 8, :], axis=0, keepdims=True)
                for k in range(KF):
                    r0 = 8 * (g * KF + k)
                    dcfw_ref[g, 0, k:k + 1, :] = jnp.sum(acc_cfw[r0:r0 + 8, :], axis=0, keepdims=True)
            cps = [pltpu.make_async_copy(acc_wup.at[0], dwup_ref.at[0, f], sem.at[0]),
                   pltpu.make_async_copy(acc_wup.at[1], dwup_ref.at[1, f], sem.at[1]),
                   pltpu.make_async_copy(acc_wdown, dwdown_ref.at[f], sem.at[2])]
            for cp in cps:
                cp.start()
            for cp in cps:
                cp.wait()
            down_copy(f).start()

            @pl.when(f % 2 != c)
            def _():
                up_copy(f, 0).start()
                up_copy(f, 1).start()

        @pl.when((i == nt - 1) & (f == N_F - 1))
        def _():
            for fi in range(N_F):
                down_copy(fi).wait()
                for g in range(2):
                    @pl.when(fi % 2 != c)
                    def _():
                        up_copy(fi, g).wait_send()

                    @pl.when(fi % 2 == c)
                    def _():
                        up_copy(fi, g).wait_recv()

    rev = lambda i: nt - 1 - i
    row = pl.BlockSpec((tm, D), lambda f, i: (rev(i), 0))
    pair = lambda r, c: pl.BlockSpec((2, 1, r, c), lambda f, i: (0, f, 0, 0))
    return pl.pallas_call(
        body, name="ffn_backward", grid=(N_F, nt),
        in_specs=[row, row, _full(ln1_g.shape), _full(ln1_b.shape),
                  pl.BlockSpec((2, 1, tm, W_UP_BLK), lambda f, i: (0, f, rev(i), 0)),
                  pl.BlockSpec((2, 1, tm, W_UP_BLK), lambda f, i: (0, f, rev(i), 0)),
                  pair(W_UP_BLK, D), pair(KF, W_UP_BLK),
                  pl.BlockSpec((1, W_UP_BLK, D), lambda f, i: (f, 0, 0))],
        out_specs=[ANY, ANY, pair(KF, W_UP_BLK), pair(1, W_UP_BLK),
                   pl.BlockSpec((1, tm, D), lambda f, i: (f, rev(i), 0)), ANY, ANY],
        out_shape=[jax.ShapeDtypeStruct((2, N_F, W_UP_BLK, D), F32),
                   jax.ShapeDtypeStruct((N_F, W_UP_BLK, D), F32),
                   jax.ShapeDtypeStruct((2, N_F, KF, W_UP_BLK), F32),
                   jax.ShapeDtypeStruct((2, N_F, 1, W_UP_BLK), F32),
                   jax.ShapeDtypeStruct((N_F, t, D), BF16),
                   jax.ShapeDtypeStruct((4, W_UP_BLK, D), F32),
                   jax.ShapeDtypeStruct((4, D_FF // N_DEV, D), F32)],
        scratch_shapes=[pltpu.VMEM((tm, D), BF16), pltpu.VMEM((tm, D), BF16),
                        pltpu.VMEM((tm, W_UP_BLK), F32),
                        pltpu.VMEM((tm + HALO_F, W_UP_BLK), F32), pltpu.VMEM((tm + HALO_F, W_UP_BLK), F32),
                        pltpu.VMEM((tm, W_UP_BLK), BF16), pltpu.VMEM((tm, W_UP_BLK), BF16),
                        pltpu.VMEM((tm, W_UP_BLK), BF16),
                        pltpu.VMEM((2, W_UP_BLK, D), F32), pltpu.VMEM((W_UP_BLK, D), F32),
                        pltpu.VMEM((2 * KF * 8, W_UP_BLK), F32), pltpu.VMEM((16, W_UP_BLK), F32),
                        pltpu.SemaphoreType.DMA((3,)),
                        pltpu.SemaphoreType.DMA((N_F, 3)), pltpu.SemaphoreType.DMA((N_F, 3))],
        compiler_params=_params(("arbitrary", "arbitrary")),
    )(dr2, xhat1, ln1_g, ln1_b, hu4, gv4, wup4, cfw4, wdown)


def mix_backward(x, h, yb1, dx1p, dr2, xhat1, rstd1, win_g, ln_a_g, ln_a_b, w_spatial, bst,
                 conv_b_w, ln_b_g, ln_b_b, wout, ln1_g, ffn_partials, tm):
    t = x.shape[0]
    n_p = len(ffn_partials)
    nt = t // tm
    n_chunks = tm // CHUNK
    halo_blocks = tm // HALO_B

    def body(x_ref, h_ref, halo_ref, yb1_ref, dx1p_ref, dr2_ref, xh1_ref, rstd1_ref, win_ref, ga_ref, ba_ref,
             ws_ref, bst_ref, cw_ref, gb_ref, bb_ref, wout_ref, g1_ref, *rest):
        p_refs, rest = rest[:n_p], rest[n_p:]
        gx_ref, dwin_ref, dwout_ref, dcw_ref, small_ref = rest[:5]
        land_refs, rest = rest[5:5 + n_p], rest[5 + n_p:]
        (ext_ref, dext_ref, y_ref, dy_ref, dh_ref, dmb_ref, wsm_ref,
         acc_win, acc_wout, acc_bin, acc_lnag, acc_lnab, acc_ws, acc_bs, acc_cbb, acc_lnbg,
         acc_lnbb, acc_bout, acc_ln1g, acc_ln1b, acc_cw, sem, send_sems, recv_sems) = rest
        i = pl.program_id(0)

        @pl.when(i == 0)
        def _():
            for cp in _chip_copies(p_refs, land_refs, send_sems, recv_sems):
                cp.start()

        first_tile = i == nt - 1
        accs = [acc_win, acc_wout, acc_bin, acc_lnag, acc_lnab, acc_ws, acc_bs, acc_cbb, acc_lnbg,
                acc_lnbb, acc_bout, acc_ln1g, acc_ln1b, acc_cw]

        @pl.when(i == 0)
        def _():
            for acc in accs:
                acc[...] = jnp.zeros(acc.shape, F32)
            dext_ref[tm:tm + HALO_B, :] = jnp.zeros((HALO_B, D_B), F32)
            mask = _tril_mask()
            for hd in range(HEADS):
                wsm_ref[hd] = jnp.where(mask, ws_ref[hd], 0.0).astype(BF16)

        def ln1_rows(bi):
            r = _rows(bi)
            part = [dx1p_ref[f, r, :].astype(F32) for f in range(N_F)]
            dx1 = ALPHA * dr2_ref[r, :] + ((part[0] + part[1]) + (part[2] + part[3]))
            xhat = xh1_ref[r, :]
            acc_ln1g[...] += _rsum8(dx1 * xhat)
            acc_ln1b[...] += _rsum8(dx1)
            dr1 = _ln_bwd(dx1 * g1_ref[...], xhat, rstd1_ref[r, 0:1])
            acc_bout[...] += _rsum8(dr1)
            gx_ref[r, :] = ALPHA * dr1
            dmb_ref[r, :] = dr1.astype(BF16)

        _loop(tm // ROWS, ln1_rows)
        dy_ref[...] = _nt(dmb_ref[...], wout_ref[...])

        ha = halo_ref[:, 0:D_B]
        hg = halo_ref[:, D_B:2 * D_B]
        ext_ref[0:HALO_B, :] = jnp.where(first_tile, 0.0, 1.0) * (ha * _sigmoid(hg))

        def chunk(ci):
            r = _rows(ci, CHUNK)
            hu, hv = h_ref[r, 0:D_A], h_ref[r, D_A:2 * D_A]
            u, cdf_u, cdf_v, xhats, rstds, vns, svs = _mixer_a_fwd(hu, hv, ga_ref, ba_ref, wsm_ref, bst_ref)
            for hd in range(HEADS):
                sl = slice(hd * HEAD_DIM, (hd + 1) * HEAD_DIM)
                rows8 = slice(8 * hd, 8 * hd + 8)
                dy_a = dy_ref[r, sl]
                y_ref[r, sl] = (u[:, sl] * svs[hd]).astype(BF16)
                du = dy_a * svs[hd]
                dsv = dy_a * u[:, sl]
                dsvb = dsv.astype(BF16)
                acc_bs[hd] += dsv
                acc_ws[hd] += _nt(dsvb, vns[hd])
                dvn = _tn(wsm_ref[hd], dsvb)
                acc_lnag[rows8, :] += _rsum8(dvn * xhats[hd])
                acc_lnab[rows8, :] += _rsum8(dvn)
                dv = _ln_bwd(dvn * ga_ref[hd:hd + 1, :], xhats[hd], rstds[hd])
                hus, hvs = hu[:, sl], hv[:, sl]
                slv = slice(D_A + hd * HEAD_DIM, D_A + (hd + 1) * HEAD_DIM)
                dhu = du * (cdf_u[:, sl] + hus * jnp.exp(-0.5 * hus * hus) * INV_SQRT_2PI)
                dhv = dv * (cdf_v[:, sl] + hvs * jnp.exp(-0.5 * hvs * hvs) * INV_SQRT_2PI)
                acc_bin[:, sl] += _rsum8(dhu)
                acc_bin[:, slv] += _rsum8(dhv)
                dh_ref[r, sl] = dhu.astype(BF16)
                dh_ref[r, slv] = dhv.astype(BF16)
            a_b = h_ref[r, 2 * D_A:2 * D_A + D_B]
            g_b = h_ref[r, 2 * D_A + D_B:D_IN]
            ext_ref[pl.ds(HALO_B + ci * CHUNK, CHUNK), :] = a_b * _sigmoid(g_b)

        _loop(n_chunks, chunk)

        def conv_rows(bi):
            base = bi * ROWS
            r = pl.ds(base, ROWS)
            win = _shifted(ext_ref[pl.ds(base, ROWS + HALO_B), :])
            xhat, rstd = _ln_stats(yb1_ref[r, :])
            yb2 = xhat * gb_ref[...] + bb_ref[...]
            sg = _sigmoid(yb2)
            y_ref[r, D_A:D] = (yb2 * sg).astype(BF16)
            dyb2 = dy_ref[r, D_A:D] * (sg * (1.0 + yb2 * (1.0 - sg)))
            acc_lnbg[...] += _rsum8(dyb2 * xhat)
            acc_lnbb[...] += _rsum8(dyb2)
            dyb1 = _ln_bwd(dyb2 * gb_ref[...], xhat, rstd)
            acc_cbb[...] += _rsum8(dyb1)
            dext_ref[r, :] = dyb1
            for k in range(KB):
                acc_cw[8 * k:8 * k + 8, :] += _rsum8(dyb1 * _tap(win, 2 + k))

        _loop(tm // ROWS, conv_rows)

        def convt_rows(bi):
            base = bi * ROWS
            r = pl.ds(base, ROWS)
            dwin = _shifted(dext_ref[pl.ds(base, ROWS + HALO_B), :])
            dyb0 = jnp.zeros((ROWS, D_B), F32)
            for k in range(KB):
                dyb0 = dyb0 + _tap(dwin, 30 - k) * cw_ref[k:k + 1, :]
            a_b = h_ref[r, 2 * D_A:2 * D_A + D_B]
            sg = _sigmoid(h_ref[r, 2 * D_A + D_B:D_IN])
            da_b = dyb0 * sg
            dg_b = dyb0 * a_b * sg * (1.0 - sg)
            acc_bin[:, 2 * D_A:2 * D_A + D_B] += _rsum8(da_b)
            acc_bin[:, 2 * D_A + D_B:D_IN] += _rsum8(dg_b)
            dh_ref[r, 2 * D_A:2 * D_A + D_B] = da_b.astype(BF16)
            dh_ref[r, 2 * D_A + D_B:D_IN] = dg_b.astype(BF16)

        _loop(tm // ROWS, convt_rows)
        dext_ref[tm:tm + HALO_B, :] = dext_ref[0:HALO_B, :]

        acc_wout[...] += _tn(y_ref[...], dmb_ref[...])
        xt = x_ref[...].T.astype(BF16)
        dh_blocks = [dh_ref[:, j * W_IN_BLK:(j + 1) * W_IN_BLK] for j in range(N_DEV)]
        for j in range(N_DEV):
            acc_win[j] += _nn(xt, dh_blocks[j])
        gx_ref[...] += sum(_nt(dh_blocks[j], win_ref[j]) for j in range(N_DEV))

        @pl.when(i == nt - 1)
        def _():
            cps = [pltpu.make_async_copy(acc_win, dwin_ref, sem.at[0]),
                   pltpu.make_async_copy(acc_wout, dwout_ref, sem.at[1])]
            for cp in cps:
                cp.start()
            small_ref[...] = jnp.zeros(small_ref.shape, F32)

            def put_row_vector(row0, acc):
                vec = jnp.sum(acc[...], axis=0, keepdims=True)
                for k in range(vec.shape[1] // 128):
                    small_ref[row0 + k:row0 + k + 1, :] = vec[:, k * 128:(k + 1) * 128]

            put_row_vector(S_BIN, acc_bin)
            put_row_vector(S_CBB, acc_cbb)
            put_row_vector(S_LNBG, acc_lnbg)
            put_row_vector(S_LNBB, acc_lnbb)
            put_row_vector(S_BOUT, acc_bout)
            put_row_vector(S_LN1G, acc_ln1g)
            put_row_vector(S_LN1B, acc_ln1b)
            mask = _tril_mask()
            for hd in range(HEADS):
                rows8 = slice(8 * hd, 8 * hd + 8)
                small_ref[S_LNAG + hd:S_LNAG + hd + 1, :] = jnp.sum(acc_lnag[rows8, :], axis=0, keepdims=True)
                small_ref[S_LNAB + hd:S_LNAB + hd + 1, :] = jnp.sum(acc_lnab[rows8, :], axis=0, keepdims=True)
                small_ref[S_WS + hd * CHUNK:S_WS + (hd + 1) * CHUNK, :] = jnp.where(mask, acc_ws[hd], 0.0)
                small_ref[S_BS + hd:S_BS + hd + 1, :] = jnp.sum(acc_bs[hd].T, axis=0, keepdims=True)
            for k in range(KB):
                dcw_ref[k:k + 1, :] = jnp.sum(acc_cw[8 * k:8 * k + 8, :], axis=0, keepdims=True)
            for cp in cps:
                cp.wait()
            for cp in _chip_copies(p_refs, land_refs, send_sems, recv_sems):
                cp.wait()

    rev = lambda i: nt - 1 - i
    row = lambda w: pl.BlockSpec((tm, w), lambda i: (rev(i), 0))
    return pl.pallas_call(
        body, name="mix_backward", grid=(nt,),
        in_specs=[row(D), row(D_IN),
                  pl.BlockSpec((HALO_B, 2 * D_B), lambda i: (jnp.maximum(rev(i) * halo_blocks - 1, 0), 1)),
                  row(D_B), pl.BlockSpec((N_F, tm, D), lambda i: (0, rev(i), 0)),
                  row(D), row(D), row(128), _resident(win_g.shape), _full(ln_a_g.shape),
                  _full(ln_a_b.shape), _full(w_spatial.shape), _full(bst.shape), _full(conv_b_w.shape),
                  _full(ln_b_g.shape), _full(ln_b_b.shape),
                  _resident(wout.shape), _full(ln1_g.shape)] + [ANY] * n_p,
        out_specs=[row(D), ANY, ANY, _full((KB, D_B)), _full((S_MIX_ROWS, 128))] + [ANY] * n_p,
        out_shape=[jax.ShapeDtypeStruct((t, D), F32), jax.ShapeDtypeStruct((N_DEV, D, W_IN_BLK), F32),
                   jax.ShapeDtypeStruct((D, D), F32), jax.ShapeDtypeStruct((KB, D_B), F32),
                   jax.ShapeDtypeStruct((S_MIX_ROWS, 128), F32)]
        + [jax.ShapeDtypeStruct(p.shape, BF16) for p in ffn_partials],
        scratch_shapes=[pltpu.VMEM((tm + HALO_B, D_B), F32), pltpu.VMEM((tm + HALO_B, D_B), F32),
                        pltpu.VMEM((tm, D), BF16), pltpu.VMEM((tm, D), F32), pltpu.VMEM((tm, D_IN), BF16),
                        pltpu.VMEM((tm, D), BF16),
                        pltpu.VMEM((HEADS, CHUNK, CHUNK), BF16),
                        pltpu.VMEM((N_DEV, D, W_IN_BLK), F32), pltpu.VMEM((D, D), F32),
                        pltpu.VMEM((8, D_IN), F32), pltpu.VMEM((8 * HEADS, HEAD_DIM), F32),
                        pltpu.VMEM((8 * HEADS, HEAD_DIM), F32), pltpu.VMEM((HEADS, CHUNK, CHUNK), F32),
                        pltpu.VMEM((HEADS, CHUNK, CHUNK), F32), pltpu.VMEM((8, D_B), F32),
                        pltpu.VMEM((8, D_B), F32), pltpu.VMEM((8, D_B), F32), pltpu.VMEM((8, D), F32),
                        pltpu.VMEM((8, D), F32), pltpu.VMEM((8, D), F32), pltpu.VMEM((8 * KB, D_B), F32),
                        pltpu.SemaphoreType.DMA((2,)),
                        pltpu.SemaphoreType.DMA((n_p, 3)), pltpu.SemaphoreType.DMA((n_p, 3))],
        compiler_params=_params(("arbitrary",)),
    )(x, h, h, yb1, dx1p, dr2, xhat1, rstd1, win_g, ln_a_g, ln_a_b, w_spatial, bst, conv_b_w,
      ln_b_g, ln_b_b, wout, ln1_g, *ffn_partials)


def _rows128(a):
    return a.reshape(-1, 128)


def _pack_conv(cb, cf):
    out = jnp.zeros((40, 768), F32)
    out = out.at[0:KB, 0:64].set(cb)
    return out.at[32:32 + KF, 0:W_UP_BLK].set(cf)


def kernel(x, w_in, b_in, ln_a_g, ln_a_b, w_spatial, b_spatial, conv_b_w, conv_b_b, ln_b_g, ln_b_b, w_out, b_out, ln1_g, ln1_b, w_up, conv_f_w, conv_f_b, w_down, ln2_g, ln2_b, loss_target, m_w_in, m_b_in, m_ln_a_g, m_ln_a_b, m_w_spatial, m_b_spatial, m_conv_b_w, m_conv_b_b, m_ln_b_g, m_ln_b_b, m_w_out, m_b_out, m_ln1_g, m_ln1_b, m_w_up, m_conv_f_w, m_conv_f_b, m_w_down, m_ln2_g, m_ln2_b, v_w_in, v_b_in, v_ln_a_g, v_ln_a_b, v_w_spatial, v_b_spatial, v_conv_b_w, v_conv_b_b, v_ln_b_g, v_ln_b_b, v_w_out, v_b_out, v_ln1_g, v_ln1_b, v_w_up, v_conv_f_w, v_conv_f_b, v_w_down, v_ln2_g, v_ln2_b):
    t = x.shape[1]
    x2 = x.reshape(t, D)
    target = loss_target.reshape(t, D)
    tm_fwd = min(t, 512)
    tm_bwd = min(t, 256)
    tm_ffn_bwd = min(t, 512)

    xi, yi, ci = _mesh_pos()
    jidx = jnp.stack([_lid(px, py, ci) for px, py in _chip_patterns(xi, yi)]).astype(jnp.int32)

    win_g, wout_g, conv_g, sup, sdown = all_gather_mixer_weights(
        w_in, w_out, w_up.T, w_down, _pack_conv(conv_b_w, conv_f_w))
    wout_full = wout_g.reshape(D, D)
    conv_b_full = conv_g[:, 0:KB, 0:64].transpose(1, 0, 2).reshape(KB, D_B)
    cfw = conv_g[:, 32:32 + KF, 0:W_UP_BLK]
    cfb = conv_f_b.reshape(N_DEV, W_UP_BLK)
    row = lambda a: a.reshape(1, -1)
    bst = b_spatial.T

    h, xhat1, rstd1, yb1, wup_g, wdown_g = mix_forward(
        x2, win_g, row(b_in), ln_a_g, ln_a_b, w_spatial, bst, conv_b_full, row(conv_b_b),
        row(ln_b_g), row(ln_b_b), wout_full, row(b_out), row(ln1_g), row(ln1_b), sup, sdown, tm_fwd)
    wdown4 = wdown_g.reshape(N_F, W_UP_BLK, D)
    hu, gv, dr2, loss_part, s_ln2 = ffn_forward(
        xhat1, row(ln1_g), row(ln1_b), wup_g, cfw, cfb, wdown4, row(ln2_g), row(ln2_b), target, tm_bwd)

    dwup, dwdown, dcfw, dcfb, dx1p, *ffn_lands = ffn_backward(
        dr2, xhat1, row(ln1_g), row(ln1_b), hu, gv, wup_g, cfw, wdown4, tm_ffn_bwd)
    ffn_grads = [dwup.reshape(N_DEV, W_UP_BLK, D), dwdown.reshape(N_DEV, D_FF // N_DEV, D)]
    ffn_partials = [chip_partials("chip_partials_" + nm, g, l, jidx, rb)
                    for nm, g, l, rb in zip(["w_up", "w_down"], ffn_grads, ffn_lands, [352, 352])]
    grad_x, dwin, dwout, dcw, s_mix, *ffn_recvs = mix_backward(
        x2, h, yb1, dx1p, dr2, xhat1, rstd1, win_g, ln_a_g, ln_a_b, w_spatial, bst,
        conv_b_full, row(ln_b_g), row(ln_b_b), wout_full, row(ln1_g), ffn_partials, tm_bwd)

    dcfb_rows = jnp.pad(dcfb.reshape(-1, 128), ((0, 4), (0, 0)))
    svec = jnp.concatenate([s_mix, dcfb_rows, s_ln2, loss_part], axis=0)
    dconv = jnp.zeros((N_DEV, 40, 768), F32)
    dconv = dconv.at[:, 0:KB, 0:64].set(dcw.reshape(KB, N_DEV, 64).transpose(1, 0, 2))
    dconv = dconv.at[:, 32:32 + KF, 0:W_UP_BLK].set(dcfw.reshape(N_DEV, KF, W_UP_BLK))
    mix_grads = [dwin, dwout.reshape(N_DEV, D // N_DEV, D), dconv]
    *mix_lands, sv_land = exchange_with_sibling("exchange_with_sibling_mixer", mix_grads, svec)
    mix_partials = [chip_partials("chip_partials_" + nm, g, l, jidx, rb)
                    for nm, g, l, rb in zip(["w_in", "w_out", "conv"], mix_grads, mix_lands, [512, 128, 40])]
    *mix_recvs, sv_slots = exchange_between_chips(mix_partials, svec, sv_land)
    names = ["w_in", "w_out", "conv", "w_up", "w_down"]
    grads = mix_grads + ffn_grads
    lands = mix_lands + list(ffn_lands)
    recvs = mix_recvs + ffn_recvs
    row_blocks = [512, 128, 40, 352, 352]

    shard_w = [w_in, w_out, _pack_conv(conv_b_w, conv_f_w), w_up.T, w_down]
    shard_m = [m_w_in, m_w_out, _pack_conv(m_conv_b_w, m_conv_f_w), m_w_up.T, m_w_down]
    shard_v = [v_w_in, v_w_out, _pack_conv(v_conv_b_w, v_conv_f_w), v_w_up.T, v_w_down]
    big = {}
    for nm, g, l, r, w, m, v, rb in zip(names, grads, lands, recvs, shard_w, shard_m, shard_v, row_blocks):
        big[nm] = reduce_and_adamw("reduce_adamw_" + nm, g, l, r, w, m, v, jidx, rb)
    big["w_up"] = [o.T for o in big["w_up"]]
    for k in range(4):
        packed = big["conv"][k]
        big.setdefault("conv_b_w", []).append(packed[0:KB, 0:64])
        big.setdefault("conv_f_w", []).append(packed[32:32 + KF, 0:W_UP_BLK])

    small_w = dict(b_in=b_in, ln_a_g=ln_a_g, ln_a_b=ln_a_b, w_spatial=w_spatial, b_spatial=b_spatial,
                   conv_b_b=conv_b_b, ln_b_g=ln_b_g, ln_b_b=ln_b_b, b_out=b_out, ln1_g=ln1_g,
                   ln1_b=ln1_b, conv_f_b=conv_f_b, ln2_g=ln2_g, ln2_b=ln2_b)
    small_m = dict(b_in=m_b_in, ln_a_g=m_ln_a_g, ln_a_b=m_ln_a_b, w_spatial=m_w_spatial,
                   b_spatial=m_b_spatial, conv_b_b=m_conv_b_b, ln_b_g=m_ln_b_g, ln_b_b=m_ln_b_b,
                   b_out=m_b_out, ln1_g=m_ln1_g, ln1_b=m_ln1_b, conv_f_b=m_conv_f_b, ln2_g=m_ln2_g,
                   ln2_b=m_ln2_b)
    small_v = dict(b_in=v_b_in, ln_a_g=v_ln_a_g, ln_a_b=v_ln_a_b, w_spatial=v_w_spatial,
                   b_spatial=v_b_spatial, conv_b_b=v_conv_b_b, ln_b_g=v_ln_b_g, ln_b_b=v_ln_b_b,
                   b_out=v_b_out, ln1_g=v_ln1_g, ln1_b=v_ln1_b, conv_f_b=v_conv_f_b, ln2_g=v_ln2_g,
                   ln2_b=v_ln2_b)
    order = [nm for nm, _, _ in SMALL_LAYOUT]
    small_out = small_adamw(sv_slots, [_rows128(small_w[nm]) for nm in order],
                            [_rows128(small_m[nm]) for nm in order], [_rows128(small_v[nm]) for nm in order])
    n_small = len(order)
    small = {nm: [small_out[k * n_small + p].reshape(small_w[nm].shape) for k in range(4)]
             for p, nm in enumerate(order)}
    loss = jnp.sum(small_out[4 * n_small]) * (0.5 / D)

    weights = ["w_in", "b_in", "ln_a_g", "ln_a_b", "w_spatial", "b_spatial", "conv_b_w", "conv_b_b",
               "ln_b_g", "ln_b_b", "w_out", "b_out", "ln1_g", "ln1_b", "w_up", "conv_f_w", "conv_f_b",
               "w_down", "ln2_g", "ln2_b"]
    result = lambda nm, k: big[nm][k] if nm in big else small[nm][k]
    return (loss, grad_x.reshape(x.shape), *[result(nm, 0) for nm in weights],
            *[result(nm, 1) for nm in weights], *[result(nm, 2) for nm in weights],
            *[result(nm, 3) for nm in weights])
```

```python
import functools
import math

import jax
import jax.numpy as jnp
from jax import lax
from jax.experimental import pallas as pl
from jax.experimental.pallas import tpu as pltpu

F32 = jnp.float32
BF16 = jnp.bfloat16

D = 1024
D_A = 512
D_B = 512
HEADS = 4
HEAD_DIM = 128
CHUNK = 128
KB = 31
KF = 3
D_FF = 2816
D_IN = 2048
N_DEV = 8
W_IN_BLK = D_IN // N_DEV
W_UP_BLK = 2 * D_FF // N_DEV
N_F = 4
LN_EPS = 1e-5
ALPHA = 2.0 ** 0.25

ADAM_LR = 0.001
ADAM_B1 = 0.9
ADAM_B2 = 0.999
ADAM_EPS = 1e-08
ADAM_WD = 0.01
ADAM_STEP = 10

INV_SQRT2 = 1.0 / math.sqrt(2.0)
INV_SQRT_2PI = 1.0 / math.sqrt(2.0 * math.pi)

HALO_B = 32
HALO_F = 8
ROWS = 64
VMEM_LIMIT = 58 * 1024 * 1024

MESH = pl.DeviceIdType.MESH
ANY = pl.BlockSpec(memory_space=pl.ANY)
VMEM = pl.BlockSpec(memory_space=pltpu.VMEM)

S_BIN, S_LNAG, S_LNAB, S_WS, S_BS, S_CBB, S_LNBG, S_LNBB, S_BOUT, S_LN1G, S_LN1B = (
    0, 16, 24, 32, 544, 552, 560, 568, 576, 584, 592)
S_MIX_ROWS = 600
S_CFB = 600
S_LN2G = 648
S_LN2B = 656
S_LOSS = 664
S_ROWS = 672


def _tn(a, b):
    return lax.dot_general(a, b, (((0,), (0,)), ((), ())), preferred_element_type=F32)


def _nt(a, b):
    return lax.dot_general(a, b, (((1,), (1,)), ((), ())), preferred_element_type=F32)


def _nn(a, b):
    return jnp.dot(a, b, preferred_element_type=F32)


def _sigmoid(x):
    return 1.0 / (1.0 + jnp.exp(-x))


def _ln_stats(x):
    mu = jnp.mean(x, axis=-1, keepdims=True)
    xc = x - mu
    var = jnp.mean(xc * xc, axis=-1, keepdims=True)
    rstd = lax.rsqrt(var + LN_EPS)
    return xc * rstd, rstd


def _ln_bwd(dxhat, xhat, rstd):
    m1 = jnp.mean(dxhat, axis=-1, keepdims=True)
    m2 = jnp.mean(dxhat * xhat, axis=-1, keepdims=True)
    return rstd * (dxhat - m1 - xhat * m2)


def _rsum8(x):
    r, n = x.shape
    return x.reshape(r // 8, 8, n).sum(axis=0)


def _rows(i, n=ROWS):
    return pl.ds(i * n, n)


def _loop(n, body):
    for i in range(n):
        body(i)


def _tril_mask():
    r = lax.broadcasted_iota(jnp.int32, (CHUNK, CHUNK), 0)
    c = lax.broadcasted_iota(jnp.int32, (CHUNK, CHUNK), 1)
    return c <= r


def _mixer_a_fwd(hu, hv, ga_ref, ba_ref, wsm_ref, bst_ref):
    cdf_u = 0.5 * (1.0 + lax.erf(hu * INV_SQRT2))
    cdf_v = 0.5 * (1.0 + lax.erf(hv * INV_SQRT2))
    u = hu * cdf_u
    v = hv * cdf_v
    xhats, rstds, vns, svs = [], [], [], []
    for hd in range(HEADS):
        sl = slice(hd * HEAD_DIM, (hd + 1) * HEAD_DIM)
        xhat, rstd = _ln_stats(v[:, sl])
        vn = (xhat * ga_ref[hd:hd + 1, :] + ba_ref[hd:hd + 1, :]).astype(BF16)
        sv = _nn(wsm_ref[hd], vn) + bst_ref[:, hd:hd + 1]
        xhats.append(xhat)
        rstds.append(rstd)
        vns.append(vn)
        svs.append(sv)
    return u, cdf_u, cdf_v, xhats, rstds, vns, svs


def _shifted(win):
    n = win.shape[0]
    return [win] + [pltpu.roll(win, n - s, 0) for s in range(1, 8)]


def _tap(shifted, offset):
    s = offset % 8
    return shifted[s][offset - s:offset - s + ROWS, :]


def _conv_b_block(ext_ref, base, cw_ref):
    win = _shifted(ext_ref[pl.ds(base, ROWS + HALO_B), :])
    acc = jnp.zeros((ROWS, D_B), F32)
    for k in range(KB):
        acc = acc + _tap(win, 2 + k) * cw_ref[k:k + 1, :]
    return acc, win


def _taps_f(win):
    n = ROWS + HALO_F
    return [pltpu.roll(win, n - 6, 0)[0:ROWS, :], pltpu.roll(win, n - 7, 0)[0:ROWS, :], win[8:n, :]]


def _params(sem, **kw):
    return pltpu.CompilerParams(dimension_semantics=sem, vmem_limit_bytes=VMEM_LIMIT, **kw)


def _resident(shape):
    zeros = (0,) * len(shape)
    return pl.BlockSpec(shape, lambda *_: zeros, pipeline_mode=pl.Buffered(1))


def _full(shape):
    zeros = (0,) * len(shape)
    return pl.BlockSpec(shape, lambda *_: zeros)


def _mesh_pos():
    return lax.axis_index("x"), lax.axis_index("y"), lax.axis_index("c")


def _chip_patterns(x, y):
    return [(x, y), (1 - x, y), (x, 1 - y), (1 - x, 1 - y)]


def _lid(x, y, c):
    return 4 * x + 2 * y + c


def _gather_copy(outs, send_sems, recv_sems, a, k, block, to, src=None):
    blk = outs[a].at[_lid(*block)]
    return pltpu.make_async_remote_copy(
        src_ref=blk if src is None else src, dst_ref=blk,
        send_sem=send_sems.at[a, k], recv_sem=recv_sems.at[a, k], device_id=to, device_id_type=MESH)


def _gather_start(mine, outs, send_sems, recv_sems, local_sems):
    x, y, c = _mesh_pos()
    me, sib = (x, y, c), (x, y, 1 - c)
    for a in range(len(mine)):
        pltpu.make_async_copy(mine[a], outs[a].at[_lid(*me)], local_sems.at[a]).start()
        _gather_copy(outs, send_sems, recv_sems, a, 0, me, sib, src=mine[a]).start()
        for j, chip in enumerate(_chip_patterns(x, y)[1:]):
            _gather_copy(outs, send_sems, recv_sems, a, 1 + j, me, (*chip, c), src=mine[a]).start()


def _gather_finish(mine, outs, send_sems, recv_sems, local_sems):
    x, y, c = _mesh_pos()
    me, sib = (x, y, c), (x, y, 1 - c)
    chips = _chip_patterns(x, y)[1:]
    n = len(mine)
    copy = functools.partial(_gather_copy, outs, send_sems, recv_sems)
    passed = []
    for j, chip in enumerate(chips):
        for a in range(n):
            copy(a, 1 + j, (*chip, c), me).wait_recv()
            cp = copy(a, 4 + j, (*chip, c), sib)
            cp.start()
            passed.append(cp)
    for a in range(n):
        copy(a, 0, sib, me).wait_recv()
        for j, chip in enumerate(chips):
            copy(a, 4 + j, (*chip, 1 - c), me).wait_recv()
        for k in range(4):
            copy(a, k, me, sib, src=mine[a]).wait_send()
        pltpu.make_async_copy(mine[a], outs[a].at[_lid(*me)], local_sems.at[a]).wait()
    for cp in passed:
        cp.wait_send()


def _gather_scratch(n):
    return [pltpu.SemaphoreType.DMA((n, 7)), pltpu.SemaphoreType.DMA((n, 7)), pltpu.SemaphoreType.DMA((n,))]


def all_gather_mixer_weights(w_in, w_out, w_up, w_down, convp):
    srcs = [w_in, w_out, convp]
    n = len(srcs)

    def body(win_ref, wout_ref, convp_ref, wup_ref, wdown_ref,
             gin_ref, gout_ref, gconv_ref, sup_ref, sdown_ref,
             sin_ref, sout_ref, send_sems, recv_sems, local_sems):
        sin_ref[...] = win_ref[...].astype(BF16)
        sout_ref[...] = wout_ref[...].astype(BF16)
        mine = [sin_ref, sout_ref, convp_ref]
        outs = [gin_ref, gout_ref, gconv_ref]
        _gather_start(mine, outs, send_sems, recv_sems, local_sems)
        sup_ref[...] = wup_ref[...].astype(BF16)
        sdown_ref[...] = wdown_ref[...].astype(BF16)
        _gather_finish(mine, outs, send_sems, recv_sems, local_sems)

    return pl.pallas_call(
        body, name="all_gather_mixer_weights",
        out_shape=[jax.ShapeDtypeStruct((N_DEV,) + w_in.shape, BF16),
                   jax.ShapeDtypeStruct((N_DEV,) + w_out.shape, BF16),
                   jax.ShapeDtypeStruct((N_DEV,) + convp.shape, F32),
                   jax.ShapeDtypeStruct(w_up.shape, BF16), jax.ShapeDtypeStruct(w_down.shape, BF16)],
        in_specs=[VMEM] * 5, out_specs=[ANY] * n + [VMEM, VMEM],
        scratch_shapes=[pltpu.VMEM(w_in.shape, BF16), pltpu.VMEM(w_out.shape, BF16)] + _gather_scratch(n),
        compiler_params=pltpu.CompilerParams(vmem_limit_bytes=VMEM_LIMIT),
    )(w_in, w_out, convp, w_up, w_down)


def exchange_with_sibling(name, grads, svec=None):
    srcs = list(grads) + ([] if svec is None else [svec])
    n, n_all = len(grads), len(srcs)

    def body(*refs):
        src, land = refs[:n_all], refs[n_all:2 * n_all]
        send_sems, recv_sems = refs[2 * n_all:]
        x, y, c = _mesh_pos()
        sib = (x, y, 1 - c)
        copies = []
        for a in range(n):
            for k, (px, py) in enumerate(_chip_patterns(x, y)):
                copies.append(pltpu.make_async_remote_copy(
                    src_ref=src[a].at[_lid(px, py, 1 - c)], dst_ref=land[a].at[k],
                    send_sem=send_sems.at[a, k], recv_sem=recv_sems.at[a, k],
                    device_id=sib, device_id_type=MESH))
        if svec is not None:
            copies.append(pltpu.make_async_remote_copy(
                src_ref=src[n], dst_ref=land[n], send_sem=send_sems.at[n, 0], recv_sem=recv_sems.at[n, 0],
                device_id=sib, device_id_type=MESH))
        for cp in copies:
            cp.start()
        for cp in copies:
            cp.wait()

    return pl.pallas_call(
        body, name=name,
        out_shape=[jax.ShapeDtypeStruct((4,) + g.shape[1:], F32) for g in grads]
        + ([] if svec is None else [jax.ShapeDtypeStruct(svec.shape, F32)]),
        in_specs=[ANY] * n_all, out_specs=[ANY] * n_all,
        scratch_shapes=[pltpu.SemaphoreType.DMA((n_all, 4)), pltpu.SemaphoreType.DMA((n_all, 4))],
    )(*srcs)


def _chip_copies(p, land, send_sems, recv_sems):
    x, y, c = _mesh_pos()
    return [pltpu.make_async_remote_copy(
        src_ref=p[a].at[k], dst_ref=land[a].at[k], send_sem=send_sems.at[a, k], recv_sem=recv_sems.at[a, k],
        device_id=(px, py, c), device_id_type=MESH)
        for k, (px, py) in enumerate(_chip_patterns(x, y)[1:]) for a in range(len(p))]


def exchange_between_chips(partials, svec, sv_land):
    n = len(partials)

    def body(*refs):
        p = refs[:n]
        sv_ref, svl_ref = refs[n], refs[n + 1]
        land = refs[n + 2:2 * n + 2]
        sv_slots = refs[2 * n + 2]
        chip_sv, send_sems, recv_sems, sv_send, sv_recv, local_sem = refs[2 * n + 3:]
        x, y, c = _mesh_pos()
        q = 2 * x + y
        chip_sv[...] = sv_ref[...] + svl_ref[...]
        local = pltpu.make_async_copy(chip_sv, sv_slots.at[q], local_sem)
        local.start()
        copies = _chip_copies(p, land, send_sems, recv_sems)
        for cp in copies:
            cp.start()
        sv_copies = []
        for k, (px, py) in enumerate(_chip_patterns(x, y)[1:]):
            cp = pltpu.make_async_remote_copy(
                src_ref=chip_sv, dst_ref=sv_slots.at[q],
                send_sem=sv_send.at[k], recv_sem=sv_recv.at[k],
                device_id=(px, py, c), device_id_type=MESH)
            cp.start()
            sv_copies.append(cp)
        for cp in copies:
            cp.wait()
        for k, (px, py) in enumerate(_chip_patterns(x, y)[1:]):
            sv_copies[k].wait_send()
            pltpu.make_async_remote_copy(
                src_ref=chip_sv, dst_ref=sv_slots.at[2 * px + py],
                send_sem=sv_send.at[k], recv_sem=sv_recv.at[k],
                device_id=(px, py, c), device_id_type=MESH).wait_recv()
        local.wait()

    return pl.pallas_call(
        body, name="exchange_between_chips",
        out_shape=[jax.ShapeDtypeStruct(p.shape, BF16) for p in partials]
        + [jax.ShapeDtypeStruct((4,) + svec.shape, F32)],
        in_specs=[ANY] * n + [VMEM, VMEM], out_specs=[ANY] * (n + 1),
        scratch_shapes=[pltpu.VMEM(svec.shape, F32),
                        pltpu.SemaphoreType.DMA((n, 3)), pltpu.SemaphoreType.DMA((n, 3)),
                        pltpu.SemaphoreType.DMA((3,)), pltpu.SemaphoreType.DMA((3,)),
                        pltpu.SemaphoreType.DMA],
    )(*partials, svec, sv_land)


def chip_partials(name, g, land, jidx, rb):
    _, r, c = g.shape

    def body(j_ref, g_ref, l_ref, o_ref):
        o_ref[...] = (g_ref[...] + l_ref[...]).astype(BF16)

    return pl.pallas_call(
        body, name=name,
        out_shape=jax.ShapeDtypeStruct((3, r, c), BF16),
        grid_spec=pltpu.PrefetchScalarGridSpec(
            num_scalar_prefetch=1, grid=(3, r // rb),
            in_specs=[pl.BlockSpec((1, rb, c), lambda k, i, j: (j[1 + k], i, 0)),
                      pl.BlockSpec((1, rb, c), lambda k, i, j: (1 + k, i, 0))],
            out_specs=pl.BlockSpec((1, rb, c), lambda k, i, j: (k, i, 0))),
        compiler_params=_params(("arbitrary", "arbitrary")),
    )(jidx, g, land)


def _adamw(w, g, m, v):
    m2 = ADAM_B1 * m + (1.0 - ADAM_B1) * g
    v2 = ADAM_B2 * v + (1.0 - ADAM_B2) * (g * g)
    m_hat = m2 / (1.0 - ADAM_B1 ** ADAM_STEP)
    v_hat = v2 / (1.0 - ADAM_B2 ** ADAM_STEP)
    delta = -ADAM_LR * (m_hat / (jnp.sqrt(v_hat) + ADAM_EPS) + ADAM_WD * w)
    return delta, m2, v2


def reduce_and_adamw(name, g, land, recv, w, m, v, jidx, rb):
    _, r, c = g.shape

    def body(j_ref, g_ref, l_ref, r_ref, w_ref, m_ref, v_ref, go_ref, do_ref, mo_ref, vo_ref):
        grad = (g_ref[0] + l_ref[0]) + r_ref[0].astype(F32) + r_ref[1].astype(F32) + r_ref[2].astype(F32)
        delta, m2, v2 = _adamw(w_ref[...], grad, m_ref[...], v_ref[...])
        go_ref[...] = grad
        do_ref[...] = delta
        mo_ref[...] = m2
        vo_ref[...] = v2

    blk = pl.BlockSpec((rb, c), lambda i, j: (i, 0))
    return pl.pallas_call(
        body, name=name,
        out_shape=[jax.ShapeDtypeStruct((r, c), F32)] * 4,
        grid_spec=pltpu.PrefetchScalarGridSpec(
            num_scalar_prefetch=1, grid=(r // rb,),
            in_specs=[pl.BlockSpec((1, rb, c), lambda i, j: (j[0], i, 0)),
                      pl.BlockSpec((1, rb, c), lambda i, j: (0, i, 0)),
                      pl.BlockSpec((3, rb, c), lambda i, j: (0, i, 0)),
                      blk, blk, blk],
            out_specs=[blk] * 4),
        compiler_params=_params(("arbitrary",)),
    )(jidx, g, land, recv, w, m, v)


SMALL_LAYOUT = [
    ("b_in", S_BIN, 16), ("ln_a_g", S_LNAG, 4), ("ln_a_b", S_LNAB, 4), ("w_spatial", S_WS, 512),
    ("b_spatial", S_BS, 4), ("conv_b_b", S_CBB, 4), ("ln_b_g", S_LNBG, 4), ("ln_b_b", S_LNBB, 4),
    ("b_out", S_BOUT, 8), ("ln1_g", S_LN1G, 8), ("ln1_b", S_LN1B, 8), ("conv_f_b", S_CFB, 44),
    ("ln2_g", S_LN2G, 8), ("ln2_b", S_LN2B, 8),
]


def small_adamw(sv_slots, ws, ms, vs):
    n = len(SMALL_LAYOUT)

    def body(*refs):
        s_ref = refs[0]
        w_refs, m_refs, v_refs = refs[1:1 + n], refs[1 + n:1 + 2 * n], refs[1 + 2 * n:1 + 3 * n]
        outs = refs[1 + 3 * n:]
        for p, (_, row0, rows) in enumerate(SMALL_LAYOUT):
            sl = pl.ds(row0, rows)
            grad = ((s_ref[0, sl, :] + s_ref[1, sl, :]) + s_ref[2, sl, :]) + s_ref[3, sl, :]
            delta, m2, v2 = _adamw(w_refs[p][...], grad, m_refs[p][...], v_refs[p][...])
            outs[p][...] = grad
            outs[n + p][...] = delta
            outs[2 * n + p][...] = m2
            outs[3 * n + p][...] = v2
        sl = pl.ds(S_LOSS, 8)
        outs[4 * n][...] = ((s_ref[0, sl, :] + s_ref[1, sl, :]) + s_ref[2, sl, :]) + s_ref[3, sl, :]

    shapes = [jax.ShapeDtypeStruct((rows, 128), F32) for _, _, rows in SMALL_LAYOUT]
    return pl.pallas_call(
        body, name="small_adamw", out_shape=shapes * 4 + [jax.ShapeDtypeStruct((8, 128), F32)],
        in_specs=[VMEM] * (1 + 3 * n), out_specs=[VMEM] * (4 * n + 1),
    )(sv_slots, *ws, *ms, *vs)


def mix_forward(x, win_g, b_in, ln_a_g, ln_a_b, w_spatial, bst, conv_b_w, conv_b_b, ln_b_g, ln_b_b,
                wout, b_out, ln1_g, ln1_b, sup, sdown, tm):
    t = x.shape[0]
    nt = t // tm
    n_chunks = tm // CHUNK

    def body(x_ref, win_ref, bin_ref, ga_ref, ba_ref, ws_ref, bst_ref, cw_ref, cb_ref, gb_ref,
             bb_ref, wout_ref, bout_ref, g1_ref, b1_ref, sup_ref, sdown_ref,
             h_ref, xhat1_ref, rstd1_ref, yb1_ref, gup_ref, gdown_ref,
             ext_ref, y_ref, wsm_ref, send_sems, recv_sems, local_sems):
        i = pl.program_id(0)
        gather = ([sup_ref, sdown_ref], [gup_ref, gdown_ref], send_sems, recv_sems, local_sems)

        @pl.when(i == 0)
        def _():
            _gather_start(*gather)
            ext_ref[0:HALO_B, :] = jnp.zeros((HALO_B, D_B), F32)
            mask = _tril_mask()
            for hd in range(HEADS):
                wsm_ref[hd] = jnp.where(mask, ws_ref[hd], 0.0).astype(BF16)

        xb = x_ref[...].astype(BF16)
        for j in range(N_DEV):
            cols = slice(j * W_IN_BLK, (j + 1) * W_IN_BLK)
            h_ref[:, cols] = _nn(xb, win_ref[j]) + bin_ref[:, cols]

        def chunk(ci):
            r = _rows(ci, CHUNK)
            u, _, _, _, _, _, svs = _mixer_a_fwd(h_ref[r, 0:D_A], h_ref[r, D_A:2 * D_A],
                                                 ga_ref, ba_ref, wsm_ref, bst_ref)
            for hd in range(HEADS):
                sl = slice(hd * HEAD_DIM, (hd + 1) * HEAD_DIM)
                y_ref[r, sl] = (u[:, sl] * svs[hd]).astype(BF16)
            a_b = h_ref[r, 2 * D_A:2 * D_A + D_B]
            g_b = h_ref[r, 2 * D_A + D_B:D_IN]
            ext_ref[pl.ds(HALO_B + ci * CHUNK, CHUNK), :] = a_b * _sigmoid(g_b)

        _loop(n_chunks, chunk)

        def conv_rows(bi):
            base = bi * ROWS
            acc, _ = _conv_b_block(ext_ref, base, cw_ref)
            yb1 = acc + cb_ref[...]
            yb1_ref[pl.ds(base, ROWS), :] = yb1
            xhat, _ = _ln_stats(yb1)
            yb2 = xhat * gb_ref[...] + bb_ref[...]
            y_ref[pl.ds(base, ROWS), D_A:D] = (yb2 * _sigmoid(yb2)).astype(BF16)

        _loop(tm // ROWS, conv_rows)
        ext_ref[0:HALO_B, :] = ext_ref[tm:tm + HALO_B, :]

        mix = _nn(y_ref[...], wout_ref[...]) + bout_ref[...]
        xhat1, rstd1 = _ln_stats(ALPHA * x_ref[...] + mix)
        xhat1_ref[...] = xhat1
        rstd1_ref[...] = jnp.broadcast_to(rstd1, (tm, 128))

        @pl.when(i == nt - 1)
        def _():
            _gather_finish(*gather)

    row = lambda w: pl.BlockSpec((tm, w), lambda i: (i, 0))
    return pl.pallas_call(
        body, name="mix_forward", grid=(nt,),
        in_specs=[row(D), _resident(win_g.shape), _full(b_in.shape), _full(ln_a_g.shape),
                  _full(ln_a_b.shape), _full(w_spatial.shape), _full(bst.shape),
                  _full(conv_b_w.shape), _full(conv_b_b.shape), _full(ln_b_g.shape),
                  _full(ln_b_b.shape), _resident(wout.shape), _full(b_out.shape),
                  _full(ln1_g.shape), _full(ln1_b.shape), ANY, ANY],
        out_specs=[row(D_IN), row(D), row(128), row(D_B), ANY, ANY],
        out_shape=[jax.ShapeDtypeStruct((t, D_IN), F32), jax.ShapeDtypeStruct((t, D), F32),
                   jax.ShapeDtypeStruct((t, 128), F32), jax.ShapeDtypeStruct((t, D_B), F32),
                   jax.ShapeDtypeStruct((N_DEV,) + sup.shape, BF16),
                   jax.ShapeDtypeStruct((N_DEV,) + sdown.shape, BF16)],
        scratch_shapes=[pltpu.VMEM((tm + HALO_B, D_B), F32), pltpu.VMEM((tm, D), BF16),
                        pltpu.VMEM((HEADS, CHUNK, CHUNK), BF16)] + _gather_scratch(2),
        compiler_params=_params(("arbitrary",)),
    )(x, win_g, b_in, ln_a_g, ln_a_b, w_spatial, bst, conv_b_w, conv_b_b, ln_b_g, ln_b_b,
      wout, b_out, ln1_g, ln1_b, sup, sdown)


def ffn_forward(xhat1, ln1_g, ln1_b, wup_g, cfw, cfb, wdown, ln2_g, ln2_b, target, tm):
    t = xhat1.shape[0]
    nt = t // tm

    def body(xh_ref, g1_ref, b1_ref, wup_ref, cfw_ref, cfb_ref, wdown_ref, g2_ref, b2_ref, tgt_ref,
             hu_ref, gv_ref, dr2_ref, loss_ref, sln2_ref,
             x1_ref, x1b_ref, carry_ref, gbuf_ref, ffn_ref, acc_loss, acc_g2, acc_b2):
        i = pl.program_id(0)

        @pl.when(i == 0)
        def _():
            carry_ref[...] = jnp.zeros(carry_ref.shape, F32)
            acc_loss[...] = jnp.zeros(acc_loss.shape, F32)
            acc_g2[...] = jnp.zeros(acc_g2.shape, F32)
            acc_b2[...] = jnp.zeros(acc_b2.shape, F32)

        x1 = xh_ref[...] * g1_ref[...] + b1_ref[...]
        x1_ref[...] = x1
        x1b_ref[...] = x1.astype(BF16)

        def conv(j, base):
            if base == 0:
                win = jnp.concatenate([carry_ref[j], hu_ref[j, 0:ROWS, :]], axis=0)
            else:
                win = hu_ref[j, base - HALO_F:base + ROWS, :]
            taps = _taps_f(win)
            w = cfw_ref[j]
            return sum(taps[k] * w[k:k + 1, :] for k in range(KF)) + cfb_ref[j:j + 1, :]

        for f in range(N_F):
            hu_ref[f] = _nt(x1b_ref[...], wup_ref[f])
            hu_ref[N_F + f] = _nt(x1b_ref[...], wup_ref[N_F + f])

            def rows(bi, f=f):
                gate = conv(f, bi * ROWS)
                val = conv(N_F + f, bi * ROWS)
                gbuf_ref[_rows(bi), :] = (gate * _sigmoid(gate) * val).astype(BF16)
                gv_ref[f, _rows(bi), :] = gate.astype(BF16)
                gv_ref[N_F + f, _rows(bi), :] = val.astype(BF16)

            _loop(tm // ROWS, rows)
            carry_ref[f] = hu_ref[f, tm - HALO_F:tm, :]
            carry_ref[N_F + f] = hu_ref[N_F + f, tm - HALO_F:tm, :]
            part = _nn(gbuf_ref[...], wdown_ref[f])
            if f == 0:
                ffn_ref[...] = part
            else:
                ffn_ref[...] += part

        def tail(bi):
            r = _rows(bi)
            xhat2, rstd2 = _ln_stats(ALPHA * x1_ref[r, :] + ffn_ref[r, :])
            err = xhat2 * g2_ref[...] + b2_ref[...] - tgt_ref[r, :]
            e2 = _rsum8(err * err)
            acc_loss[...] += sum(e2[:, k * 128:(k + 1) * 128] for k in range(D // 128))
            dy = err * (1.0 / D)
            acc_g2[...] += _rsum8(dy * xhat2)
            acc_b2[...] += _rsum8(dy)
            dr2_ref[r, :] = _ln_bwd(dy * g2_ref[...], xhat2, rstd2)

        _loop(tm // ROWS, tail)
        loss_ref[...] = acc_loss[...]

        @pl.when(i == nt - 1)
        def _():
            dg = jnp.sum(acc_g2[...], axis=0, keepdims=True)
            db = jnp.sum(acc_b2[...], axis=0, keepdims=True)
            for k in range(D // 128):
                sln2_ref[k:k + 1, :] = dg[:, k * 128:(k + 1) * 128]
                sln2_ref[8 + k:9 + k, :] = db[:, k * 128:(k + 1) * 128]

    row = pl.BlockSpec((tm, D), lambda i: (i, 0))
    return pl.pallas_call(
        body, name="ffn_forward", grid=(nt,),
        in_specs=[row, _full(ln1_g.shape), _full(ln1_b.shape), _resident(wup_g.shape),
                  _full(cfw.shape), _full(cfb.shape), _resident(wdown.shape),
                  _full(ln2_g.shape), _full(ln2_b.shape), row],
        out_specs=[pl.BlockSpec((N_DEV, tm, W_UP_BLK), lambda i: (0, i, 0)),
                   pl.BlockSpec((N_DEV, tm, W_UP_BLK), lambda i: (0, i, 0)), row,
                   _full((8, 128)), _full((16, 128))],
        out_shape=[jax.ShapeDtypeStruct((N_DEV, t, W_UP_BLK), F32),
                   jax.ShapeDtypeStruct((N_DEV, t, W_UP_BLK), BF16), jax.ShapeDtypeStruct((t, D), F32),
                   jax.ShapeDtypeStruct((8, 128), F32), jax.ShapeDtypeStruct((16, 128), F32)],
        scratch_shapes=[pltpu.VMEM((tm, D), F32), pltpu.VMEM((tm, D), BF16),
                        pltpu.VMEM((N_DEV, HALO_F, W_UP_BLK), F32), pltpu.VMEM((tm, W_UP_BLK), BF16),
                        pltpu.VMEM((tm, D), F32), pltpu.VMEM((8, 128), F32),
                        pltpu.VMEM((8, D), F32), pltpu.VMEM((8, D), F32)],
        compiler_params=_params(("arbitrary",)),
    )(xhat1, ln1_g, ln1_b, wup_g, cfw, cfb, wdown, ln2_g, ln2_b, target)


def ffn_backward(order, dr2, xhat1, ln1_g, ln1_b, hu, gv, wup_g, cfw, wdown, tm):
    t = dr2.shape[0]
    nt = t // tm
    sub_rows = tm
    hu4 = hu.reshape(2, N_F, t, W_UP_BLK)
    gv4 = gv.reshape(2, N_F, t, W_UP_BLK)
    wup4 = wup_g.reshape(2, N_F, W_UP_BLK, D)
    cfw4 = cfw.reshape(2, N_F, KF, W_UP_BLK)

    def body(order_ref, dr2_ref, xh_ref, g1_ref, b1_ref, hu_ref, gv_ref, wup_ref, cfw_ref, wdown_ref,
             dwup_ref, dwdown_ref, dcfw_ref, dcfb_ref, dx1_ref, land_up_ref, land_down_ref,
             x1b_ref, drb_ref, dg_ref, dextg_ref, dextv_ref, gbuf_ref,
             dhug_ref, dhuv_ref, acc_wup, acc_wdown, acc_cfw, acc_cfb, sem, send_sems, recv_sems):
        f = order_ref[pl.program_id(0)]
        i = pl.program_id(1)
        x, y, c = _mesh_pos()
        half = D_FF // N_DEV

        def to_sibling(fi, k, src, land_ref, shard_chip):
            d = jnp.bitwise_xor(shard_chip, 2 * x + y)
            slot = jnp.where(d == 1, 2, jnp.where(d == 2, 1, d))
            return pltpu.make_async_remote_copy(
                src_ref=src, dst_ref=land_ref.at[slot], send_sem=send_sems.at[fi, k], recv_sem=recv_sems.at[fi, k],
                device_id=(x, y, 1 - c), device_id_type=MESH)

        def up_copy(fi, g):
            return to_sibling(fi, g, dwup_ref.at[g, fi], land_up_ref, 2 * g + fi // 2)

        def down_copy(fi):
            return to_sibling(fi, 2, dwdown_ref.at[fi, pl.ds((1 - c) * half, half)], land_down_ref, fi)

        @pl.when(i == 0)
        def _():
            acc_wup[...] = jnp.zeros(acc_wup.shape, F32)
            acc_wdown[...] = jnp.zeros(acc_wdown.shape, F32)
            acc_cfw[...] = jnp.zeros(acc_cfw.shape, F32)
            acc_cfb[...] = jnp.zeros(acc_cfb.shape, F32)
            dextg_ref[tm:tm + HALO_F, :] = jnp.zeros((HALO_F, W_UP_BLK), F32)
            dextv_ref[tm:tm + HALO_F, :] = jnp.zeros((HALO_F, W_UP_BLK), F32)

        w = [cfw_ref[0, 0], cfw_ref[1, 0]]
        dext = [dextg_ref, dextv_ref]
        dhu = [dhug_ref, dhuv_ref]

        def rows1(bi):
            r = _rows(bi)
            gate = gv_ref[0, 0, r, :].astype(F32)
            val = gv_ref[1, 0, r, :].astype(F32)
            sg = _sigmoid(gate)
            silu = gate * sg
            gbuf_ref[r, :] = (silu * val).astype(BF16)
            dg = dg_ref[r, :]
            dgate = dg * val * (sg * (1.0 + gate * (1.0 - sg)))
            dval = dg * silu
            dextg_ref[r, :] = dgate
            dextv_ref[r, :] = dval
            acc_cfb[0:8, :] += _rsum8(dgate)
            acc_cfb[8:16, :] += _rsum8(dval)

        def rows2(bi):
            r = _rows(bi)
            for g in range(2):
                win = dext[g][pl.ds(bi * ROWS, ROWS + HALO_F), :]
                n = ROWS + HALO_F
                later = [pltpu.roll(win, n - 2, 0)[0:ROWS, :], pltpu.roll(win, n - 1, 0)[0:ROWS, :],
                         win[0:ROWS, :]]
                d = sum(later[k] * w[g][k:k + 1, :] for k in range(KF))
                dhu[g][r, :] = d.astype(BF16)
                pre = hu_ref[g, 0, r, :]
                for k in range(KF):
                    r0 = 8 * (g * KF + k)
                    acc_cfw[r0:r0 + 8, :] += _rsum8(later[k] * pre)

        for sub in reversed(range(tm // sub_rows)):
            rs = slice(sub * sub_rows, (sub + 1) * sub_rows)
            blocks = range(sub * sub_rows // ROWS, (sub + 1) * sub_rows // ROWS)
            x1b_ref[rs, :] = (xh_ref[rs, :] * g1_ref[...] + b1_ref[...]).astype(BF16)
            drb_ref[rs, :] = dr2_ref[rs, :].astype(BF16)
            dg_ref[rs, :] = _nt(drb_ref[rs, :], wdown_ref[0])
            for bi in blocks:
                rows1(bi)
            for bi in blocks:
                rows2(bi)
            acc_wdown[...] += _tn(gbuf_ref[rs, :], drb_ref[rs, :])
            acc_wup[0] += _tn(dhug_ref[rs, :], x1b_ref[rs, :])
            acc_wup[1] += _tn(dhuv_ref[rs, :], x1b_ref[rs, :])
            dx1_ref[0, rs, :] = (_nn(dhug_ref[rs, :], wup_ref[0, 0])
                                 + _nn(dhuv_ref[rs, :], wup_ref[1, 0])).astype(BF16)
        dextg_ref[tm:tm + HALO_F, :] = dextg_ref[0:HALO_F, :]
        dextv_ref[tm:tm + HALO_F, :] = dextv_ref[0:HALO_F, :]

        @pl.when(i == nt - 1)
        def _():
            for g in range(2):
                dcfb_ref[g, 0] = jnp.sum(acc_cfb[8 * g:8 * g + 8, :], axis=0, keepdims=True)
                for k in range(KF):
                    r0 = 8 * (g * KF + k)
                    dcfw_ref[g, 0, k:k + 1, :] = jnp.sum(acc_cfw[r0:r0 + 8, :], axis=0, keepdims=True)
            cps = [pltpu.make_async_copy(acc_wup.at[0], dwup_ref.at[0, f], sem.at[0]),
                   pltpu.make_async_copy(acc_wup.at[1], dwup_ref.at[1, f], sem.at[1]),
                   pltpu.make_async_copy(acc_wdown, dwdown_ref.at[f], sem.at[2])]
            for cp in cps:
                cp.start()
            for cp in cps:
                cp.wait()
            down_copy(f).start()

            @pl.when(f % 2 != c)
            def _():
                up_copy(f, 0).start()
                up_copy(f, 1).start()

        @pl.when((i == nt - 1) & (pl.program_id(0) == N_F - 1))
        def _():
            for fi in range(N_F):
                down_copy(fi).wait()
                for g in range(2):
                    @pl.when(fi % 2 != c)
                    def _():
                        up_copy(fi, g).wait_send()

                    @pl.when(fi % 2 == c)
                    def _():
                        up_copy(fi, g).wait_recv()

    rev = lambda i: nt - 1 - i
    row = pl.BlockSpec((tm, D), lambda fo, i, o: (rev(i), 0))
    pair = lambda r, c: pl.BlockSpec((2, 1, r, c), lambda fo, i, o: (0, o[fo], 0, 0))
    tile = pl.BlockSpec((2, 1, tm, W_UP_BLK), lambda fo, i, o: (0, o[fo], rev(i), 0))
    return pl.pallas_call(
        body, name="ffn_backward",
        grid_spec=pltpu.PrefetchScalarGridSpec(
            num_scalar_prefetch=1, grid=(N_F, nt),
            in_specs=[row, row, _full(ln1_g.shape), _full(ln1_b.shape), tile, tile,
                      pair(W_UP_BLK, D), pair(KF, W_UP_BLK),
                      pl.BlockSpec((1, W_UP_BLK, D), lambda fo, i, o: (o[fo], 0, 0))],
            out_specs=[ANY, ANY, pair(KF, W_UP_BLK), pair(1, W_UP_BLK),
                       pl.BlockSpec((1, tm, D), lambda fo, i, o: (o[fo], rev(i), 0)), ANY, ANY],
            scratch_shapes=[pltpu.VMEM((tm, D), BF16), pltpu.VMEM((tm, D), BF16),
                            pltpu.VMEM((tm, W_UP_BLK), F32),
                            pltpu.VMEM((tm + HALO_F, W_UP_BLK), F32), pltpu.VMEM((tm + HALO_F, W_UP_BLK), F32),
                            pltpu.VMEM((tm, W_UP_BLK), BF16), pltpu.VMEM((tm, W_UP_BLK), BF16),
                            pltpu.VMEM((tm, W_UP_BLK), BF16),
                            pltpu.VMEM((2, W_UP_BLK, D), F32), pltpu.VMEM((W_UP_BLK, D), F32),
                            pltpu.VMEM((2 * KF * 8, W_UP_BLK), F32), pltpu.VMEM((16, W_UP_BLK), F32),
                            pltpu.SemaphoreType.DMA((3,)),
                            pltpu.SemaphoreType.DMA((N_F, 3)), pltpu.SemaphoreType.DMA((N_F, 3))]),
        out_shape=[jax.ShapeDtypeStruct((2, N_F, W_UP_BLK, D), F32),
                   jax.ShapeDtypeStruct((N_F, W_UP_BLK, D), F32),
                   jax.ShapeDtypeStruct((2, N_F, KF, W_UP_BLK), F32),
                   jax.ShapeDtypeStruct((2, N_F, 1, W_UP_BLK), F32),
                   jax.ShapeDtypeStruct((N_F, t, D), BF16),
                   jax.ShapeDtypeStruct((4, W_UP_BLK, D), F32),
                   jax.ShapeDtypeStruct((4, D_FF // N_DEV, D), F32)],
        compiler_params=_params(("arbitrary", "arbitrary")),
    )(order, dr2, xhat1, ln1_g, ln1_b, hu4, gv4, wup4, cfw4, wdown)


def mix_backward(x, h, yb1, dx1p, dr2, xhat1, rstd1, win_g, ln_a_g, ln_a_b, w_spatial, bst,
                 conv_b_w, ln_b_g, ln_b_b, wout, ln1_g, ffn_partials, tm):
    t = x.shape[0]
    n_p = len(ffn_partials)
    nt = t // tm
    n_chunks = tm // CHUNK
    halo_blocks = tm // HALO_B

    def body(x_ref, h_ref, halo_ref, yb1_ref, dx1p_ref, dr2_ref, xh1_ref, rstd1_ref, win_ref, ga_ref, ba_ref,
             ws_ref, bst_ref, cw_ref, gb_ref, bb_ref, wout_ref, g1_ref, *rest):
        p_refs, rest = rest[:n_p], rest[n_p:]
        gx_ref, dwin_ref, dwout_ref, dcw_ref, small_ref = rest[:5]
        land_refs, rest = rest[5:5 + n_p], rest[5 + n_p:]
        (ext_ref, dext_ref, y_ref, dy_ref, dh_ref, dmb_ref, wsm_ref,
         acc_win, acc_wout, acc_bin, acc_lnag, acc_lnab, acc_ws, acc_bs, acc_cbb, acc_lnbg,
         acc_lnbb, acc_bout, acc_ln1g, acc_ln1b, acc_cw, sem, send_sems, recv_sems) = rest
        i = pl.program_id(0)

        @pl.when(i == 0)
        def _():
            for cp in _chip_copies(p_refs, land_refs, send_sems, recv_sems):
                cp.start()

        first_tile = i == nt - 1
        accs = [acc_win, acc_wout, acc_bin, acc_lnag, acc_lnab, acc_ws, acc_bs, acc_cbb, acc_lnbg,
                acc_lnbb, acc_bout, acc_ln1g, acc_ln1b, acc_cw]

        @pl.when(i == 0)
        def _():
            for acc in accs:
                acc[...] = jnp.zeros(acc.shape, F32)
            dext_ref[tm:tm + HALO_B, :] = jnp.zeros((HALO_B, D_B), F32)
            mask = _tril_mask()
            for hd in range(HEADS):
                wsm_ref[hd] = jnp.where(mask, ws_ref[hd], 0.0).astype(BF16)

        def ln1_rows(bi):
            r = _rows(bi)
            part = [dx1p_ref[f, r, :].astype(F32) for f in range(N_F)]
            dx1 = ALPHA * dr2_ref[r, :] + ((part[0] + part[1]) + (part[2] + part[3]))
            xhat = xh1_ref[r, :]
            acc_ln1g[...] += _rsum8(dx1 * xhat)
            acc_ln1b[...] += _rsum8(dx1)
            dr1 = _ln_bwd(dx1 * g1_ref[...], xhat, rstd1_ref[r, 0:1])
            acc_bout[...] += _rsum8(dr1)
            gx_ref[r, :] = ALPHA * dr1
            dmb_ref[r, :] = dr1.astype(BF16)

        _loop(tm // ROWS, ln1_rows)
        dy_ref[...] = _nt(dmb_ref[...], wout_ref[...])

        ha = halo_ref[:, 0:D_B]
        hg = halo_ref[:, D_B:2 * D_B]
        ext_ref[0:HALO_B, :] = jnp.where(first_tile, 0.0, 1.0) * (ha * _sigmoid(hg))

        def chunk(ci):
            r = _rows(ci, CHUNK)
            hu, hv = h_ref[r, 0:D_A], h_ref[r, D_A:2 * D_A]
            u, cdf_u, cdf_v, xhats, rstds, vns, svs = _mixer_a_fwd(hu, hv, ga_ref, ba_ref, wsm_ref, bst_ref)
            for hd in range(HEADS):
                sl = slice(hd * HEAD_DIM, (hd + 1) * HEAD_DIM)
                rows8 = slice(8 * hd, 8 * hd + 8)
                dy_a = dy_ref[r, sl]
                y_ref[r, sl] = (u[:, sl] * svs[hd]).astype(BF16)
                du = dy_a * svs[hd]
                dsv = dy_a * u[:, sl]
                dsvb = dsv.astype(BF16)
                acc_bs[hd] += dsv
                acc_ws[hd] += _nt(dsvb, vns[hd])
                dvn = _tn(wsm_ref[hd], dsvb)
                acc_lnag[rows8, :] += _rsum8(dvn * xhats[hd])
                acc_lnab[rows8, :] += _rsum8(dvn)
                dv = _ln_bwd(dvn * ga_ref[hd:hd + 1, :], xhats[hd], rstds[hd])
                hus, hvs = hu[:, sl], hv[:, sl]
                slv = slice(D_A + hd * HEAD_DIM, D_A + (hd + 1) * HEAD_DIM)
                dhu = du * (cdf_u[:, sl] + hus * jnp.exp(-0.5 * hus * hus) * INV_SQRT_2PI)
                dhv = dv * (cdf_v[:, sl] + hvs * jnp.exp(-0.5 * hvs * hvs) * INV_SQRT_2PI)
                acc_bin[:, sl] += _rsum8(dhu)
                acc_bin[:, slv] += _rsum8(dhv)
                dh_ref[r, sl] = dhu.astype(BF16)
                dh_ref[r, slv] = dhv.astype(BF16)
            a_b = h_ref[r, 2 * D_A:2 * D_A + D_B]
            g_b = h_ref[r, 2 * D_A + D_B:D_IN]
            ext_ref[pl.ds(HALO_B + ci * CHUNK, CHUNK), :] = a_b * _sigmoid(g_b)

        _loop(n_chunks, chunk)

        def conv_rows(bi):
            base = bi * ROWS
            r = pl.ds(base, ROWS)
            win = _shifted(ext_ref[pl.ds(base, ROWS + HALO_B), :])
            xhat, rstd = _ln_stats(yb1_ref[r, :])
            yb2 = xhat * gb_ref[...] + bb_ref[...]
            sg = _sigmoid(yb2)
            y_ref[r, D_A:D] = (yb2 * sg).astype(BF16)
            dyb2 = dy_ref[r, D_A:D] * (sg * (1.0 + yb2 * (1.0 - sg)))
            acc_lnbg[...] += _rsum8(dyb2 * xhat)
            acc_lnbb[...] += _rsum8(dyb2)
            dyb1 = _ln_bwd(dyb2 * gb_ref[...], xhat, rstd)
            acc_cbb[...] += _rsum8(dyb1)
            dext_ref[r, :] = dyb1
            for k in range(KB):
                acc_cw[8 * k:8 * k + 8, :] += _rsum8(dyb1 * _tap(win, 2 + k))

        _loop(tm // ROWS, conv_rows)

        def convt_rows(bi):
            base = bi * ROWS
            r = pl.ds(base, ROWS)
            dwin = _shifted(dext_ref[pl.ds(base, ROWS + HALO_B), :])
            dyb0 = jnp.zeros((ROWS, D_B), F32)
            for k in range(KB):
                dyb0 = dyb0 + _tap(dwin, 30 - k) * cw_ref[k:k + 1, :]
            a_b = h_ref[r, 2 * D_A:2 * D_A + D_B]
            sg = _sigmoid(h_ref[r, 2 * D_A + D_B:D_IN])
            da_b = dyb0 * sg
            dg_b = dyb0 * a_b * sg * (1.0 - sg)
            acc_bin[:, 2 * D_A:2 * D_A + D_B] += _rsum8(da_b)
            acc_bin[:, 2 * D_A + D_B:D_IN] += _rsum8(dg_b)
            dh_ref[r, 2 * D_A:2 * D_A + D_B] = da_b.astype(BF16)
            dh_ref[r, 2 * D_A + D_B:D_IN] = dg_b.astype(BF16)

        _loop(tm // ROWS, convt_rows)
        dext_ref[tm:tm + HALO_B, :] = dext_ref[0:HALO_B, :]

        acc_wout[...] += _tn(y_ref[...], dmb_ref[...])
        xt = x_ref[...].T.astype(BF16)
        dh_blocks = [dh_ref[:, j * W_IN_BLK:(j + 1) * W_IN_BLK] for j in range(N_DEV)]
        for j in range(N_DEV):
            acc_win[j] += _nn(xt, dh_blocks[j])
        gx_ref[...] += sum(_nt(dh_blocks[j], win_ref[j]) for j in range(N_DEV))

        @pl.when(i == nt - 1)
        def _():
            cps = [pltpu.make_async_copy(acc_win, dwin_ref, sem.at[0]),
                   pltpu.make_async_copy(acc_wout, dwout_ref, sem.at[1])]
            for cp in cps:
                cp.start()
            small_ref[...] = jnp.zeros(small_ref.shape, F32)

            def put_row_vector(row0, acc):
                vec = jnp.sum(acc[...], axis=0, keepdims=True)
                for k in range(vec.shape[1] // 128):
                    small_ref[row0 + k:row0 + k + 1, :] = vec[:, k * 128:(k + 1) * 128]

            put_row_vector(S_BIN, acc_bin)
            put_row_vector(S_CBB, acc_cbb)
            put_row_vector(S_LNBG, acc_lnbg)
            put_row_vector(S_LNBB, acc_lnbb)
            put_row_vector(S_BOUT, acc_bout)
            put_row_vector(S_LN1G, acc_ln1g)
            put_row_vector(S_LN1B, acc_ln1b)
            mask = _tril_mask()
            for hd in range(HEADS):
                rows8 = slice(8 * hd, 8 * hd + 8)
                small_ref[S_LNAG + hd:S_LNAG + hd + 1, :] = jnp.sum(acc_lnag[rows8, :], axis=0, keepdims=True)
                small_ref[S_LNAB + hd:S_LNAB + hd + 1, :] = jnp.sum(acc_lnab[rows8, :], axis=0, keepdims=True)
                small_ref[S_WS + hd * CHUNK:S_WS + (hd + 1) * CHUNK, :] = jnp.where(mask, acc_ws[hd], 0.0)
                small_ref[S_BS + hd:S_BS + hd + 1, :] = jnp.sum(acc_bs[hd].T, axis=0, keepdims=True)
            for k in range(KB):
                dcw_ref[k:k + 1, :] = jnp.sum(acc_cw[8 * k:8 * k + 8, :], axis=0, keepdims=True)
            for cp in cps:
                cp.wait()
            for cp in _chip_copies(p_refs, land_refs, send_sems, recv_sems):
                cp.wait()

    rev = lambda i: nt - 1 - i
    row = lambda w: pl.BlockSpec((tm, w), lambda i: (rev(i), 0))
    return pl.pallas_call(
        body, name="mix_backward", grid=(nt,),
        in_specs=[row(D), row(D_IN),
                  pl.BlockSpec((HALO_B, 2 * D_B), lambda i: (jnp.maximum(rev(i) * halo_blocks - 1, 0), 1)),
                  row(D_B), pl.BlockSpec((N_F, tm, D), lambda i: (0, rev(i), 0)),
                  row(D), row(D), row(128), _resident(win_g.shape), _full(ln_a_g.shape),
                  _full(ln_a_b.shape), _full(w_spatial.shape), _full(bst.shape), _full(conv_b_w.shape),
                  _full(ln_b_g.shape), _full(ln_b_b.shape),
                  _resident(wout.shape), _full(ln1_g.shape)] + [ANY] * n_p,
        out_specs=[row(D), ANY, ANY, _full((KB, D_B)), _full((S_MIX_ROWS, 128))] + [ANY] * n_p,
        out_shape=[jax.ShapeDtypeStruct((t, D), F32), jax.ShapeDtypeStruct((N_DEV, D, W_IN_BLK), F32),
                   jax.ShapeDtypeStruct((D, D), F32), jax.ShapeDtypeStruct((KB, D_B), F32),
                   jax.ShapeDtypeStruct((S_MIX_ROWS, 128), F32)]
        + [jax.ShapeDtypeStruct(p.shape, BF16) for p in ffn_partials],
        scratch_shapes=[pltpu.VMEM((tm + HALO_B, D_B), F32), pltpu.VMEM((tm + HALO_B, D_B), F32),
                        pltpu.VMEM((tm, D), BF16), pltpu.VMEM((tm, D), F32), pltpu.VMEM((tm, D_IN), BF16),
                        pltpu.VMEM((tm, D), BF16),
                        pltpu.VMEM((HEADS, CHUNK, CHUNK), BF16),
                        pltpu.VMEM((N_DEV, D, W_IN_BLK), F32), pltpu.VMEM((D, D), F32),
                        pltpu.VMEM((8, D_IN), F32), pltpu.VMEM((8 * HEADS, HEAD_DIM), F32),
                        pltpu.VMEM((8 * HEADS, HEAD_DIM), F32), pltpu.VMEM((HEADS, CHUNK, CHUNK), F32),
                        pltpu.VMEM((HEADS, CHUNK, CHUNK), F32), pltpu.VMEM((8, D_B), F32),
                        pltpu.VMEM((8, D_B), F32), pltpu.VMEM((8, D_B), F32), pltpu.VMEM((8, D), F32),
                        pltpu.VMEM((8, D), F32), pltpu.VMEM((8, D), F32), pltpu.VMEM((8 * KB, D_B), F32),
                        pltpu.SemaphoreType.DMA((2,)),
                        pltpu.SemaphoreType.DMA((n_p, 3)), pltpu.SemaphoreType.DMA((n_p, 3))],
        compiler_params=_params(("arbitrary",)),
    )(x, h, h, yb1, dx1p, dr2, xhat1, rstd1, win_g, ln_a_g, ln_a_b, w_spatial, bst, conv_b_w,
      ln_b_g, ln_b_b, wout, ln1_g, *ffn_partials)


def _rows128(a):
    return a.reshape(-1, 128)


def _pack_conv(cb, cf):
    lead = cb.shape[:-2]
    pad = [(0, 0)] * len(lead)
    flat = jnp.pad(cb.reshape(lead + (KB * 64,)), pad + [(0, 3 * W_UP_BLK - KB * 64)])
    rows = jnp.concatenate([cf, flat.reshape(lead + (3, W_UP_BLK))], axis=-2)
    return jnp.pad(rows, pad + [(0, 2), (0, 768 - W_UP_BLK)])


def _unpack_conv(p):
    lead = p.shape[:-2]
    cf = p[..., 0:KF, 0:W_UP_BLK]
    cb = p[..., 3:6, 0:W_UP_BLK].reshape(lead + (3 * W_UP_BLK,))[..., :KB * 64].reshape(lead + (KB, 64))
    return cb, cf


def kernel(x, w_in, b_in, ln_a_g, ln_a_b, w_spatial, b_spatial, conv_b_w, conv_b_b, ln_b_g, ln_b_b, w_out, b_out, ln1_g, ln1_b, w_up, conv_f_w, conv_f_b, w_down, ln2_g, ln2_b, loss_target, m_w_in, m_b_in, m_ln_a_g, m_ln_a_b, m_w_spatial, m_b_spatial, m_conv_b_w, m_conv_b_b, m_ln_b_g, m_ln_b_b, m_w_out, m_b_out, m_ln1_g, m_ln1_b, m_w_up, m_conv_f_w, m_conv_f_b, m_w_down, m_ln2_g, m_ln2_b, v_w_in, v_b_in, v_ln_a_g, v_ln_a_b, v_w_spatial, v_b_spatial, v_conv_b_w, v_conv_b_b, v_ln_b_g, v_ln_b_b, v_w_out, v_b_out, v_ln1_g, v_ln1_b, v_w_up, v_conv_f_w, v_conv_f_b, v_w_down, v_ln2_g, v_ln2_b):
    t = x.shape[1]
    x2 = x.reshape(t, D)
    target = loss_target.reshape(t, D)
    tm_fwd = min(t, 512)
    tm_bwd = min(t, 256)
    tm_ffn_bwd = min(t, 512)

    xi, yi, ci = _mesh_pos()
    jidx = jnp.stack([_lid(px, py, ci) for px, py in _chip_patterns(xi, yi)]).astype(jnp.int32)

    win_g, wout_g, conv_g, sup, sdown = all_gather_mixer_weights(
        w_in, w_out, w_up.T, w_down, _pack_conv(conv_b_w, conv_f_w))
    wout_full = wout_g.reshape(D, D)
    conv_b_all, cfw = _unpack_conv(conv_g)
    conv_b_full = conv_b_all.transpose(1, 0, 2).reshape(KB, D_B)
    cfb = conv_f_b.reshape(N_DEV, W_UP_BLK)
    row = lambda a: a.reshape(1, -1)
    bst = b_spatial.T

    h, xhat1, rstd1, yb1, wup_g, wdown_g = mix_forward(
        x2, win_g, row(b_in), ln_a_g, ln_a_b, w_spatial, bst, conv_b_full, row(conv_b_b),
        row(ln_b_g), row(ln_b_b), wout_full, row(b_out), row(ln1_g), row(ln1_b), sup, sdown, tm_fwd)
    wdown4 = wdown_g.reshape(N_F, W_UP_BLK, D)
    hu, gv, dr2, loss_part, s_ln2 = ffn_forward(
        xhat1, row(ln1_g), row(ln1_b), wup_g, cfw, cfb, wdown4, row(ln2_g), row(ln2_b), target, tm_bwd)

    order = jnp.where(ci == 0, jnp.array([1, 3, 0, 2], jnp.int32), jnp.array([0, 2, 1, 3], jnp.int32))
    dwup, dwdown, dcfw, dcfb, dx1p, *ffn_lands = ffn_backward(
        order, dr2, xhat1, row(ln1_g), row(ln1_b), hu, gv, wup_g, cfw, wdown4, tm_ffn_bwd)
    ffn_grads = [dwup.reshape(N_DEV, W_UP_BLK, D), dwdown.reshape(N_DEV, D_FF // N_DEV, D)]
    ffn_partials = [chip_partials("chip_partials_" + nm, g, l, jidx, rb)
                    for nm, g, l, rb in zip(["w_up", "w_down"], ffn_grads, ffn_lands, [352, 352])]
    grad_x, dwin, dwout, dcw, s_mix, *ffn_recvs = mix_backward(
        x2, h, yb1, dx1p, dr2, xhat1, rstd1, win_g, ln_a_g, ln_a_b, w_spatial, bst,
        conv_b_full, row(ln_b_g), row(ln_b_b), wout_full, row(ln1_g), ffn_partials, tm_bwd)

    dcfb_rows = jnp.pad(dcfb.reshape(-1, 128), ((0, 4), (0, 0)))
    svec = jnp.concatenate([s_mix, dcfb_rows, s_ln2, loss_part], axis=0)
    dconv = _pack_conv(dcw.reshape(KB, N_DEV, 64).transpose(1, 0, 2), dcfw.reshape(N_DEV, KF, W_UP_BLK))
    mix_grads = [dwin, dwout.reshape(N_DEV, D // N_DEV, D), dconv]
    *mix_lands, sv_land = exchange_with_sibling("exchange_with_sibling_mixer", mix_grads, svec)
    mix_partials = [chip_partials("chip_partials_" + nm, g, l, jidx, rb)
                    for nm, g, l, rb in zip(["w_in", "w_out", "conv"], mix_grads, mix_lands, [512, 128, 8])]
    *mix_recvs, sv_slots = exchange_between_chips(mix_partials, svec, sv_land)
    names = ["w_in", "w_out", "conv", "w_up", "w_down"]
    grads = mix_grads + ffn_grads
    lands = mix_lands + list(ffn_lands)
    recvs = mix_recvs + ffn_recvs
    row_blocks = [512, 128, 8, 352, 352]

    shard_w = [w_in, w_out, _pack_conv(conv_b_w, conv_f_w), w_up.T, w_down]
    shard_m = [m_w_in, m_w_out, _pack_conv(m_conv_b_w, m_conv_f_w), m_w_up.T, m_w_down]
    shard_v = [v_w_in, v_w_out, _pack_conv(v_conv_b_w, v_conv_f_w), v_w_up.T, v_w_down]
    big = {}
    for nm, g, l, r, w, m, v, rb in zip(names, grads, lands, recvs, shard_w, shard_m, shard_v, row_blocks):
        big[nm] = reduce_and_adamw("reduce_adamw_" + nm, g, l, r, w, m, v, jidx, rb)
    big["w_up"] = [o.T for o in big["w_up"]]
    for k in range(4):
        packed = big["conv"][k]
        cb_k, cf_k = _unpack_conv(packed)
        big.setdefault("conv_b_w", []).append(cb_k)
        big.setdefault("conv_f_w", []).append(cf_k)

    small_w = dict(b_in=b_in, ln_a_g=ln_a_g, ln_a_b=ln_a_b, w_spatial=w_spatial, b_spatial=b_spatial,
                   conv_b_b=conv_b_b, ln_b_g=ln_b_g, ln_b_b=ln_b_b, b_out=b_out, ln1_g=ln1_g,
                   ln1_b=ln1_b, conv_f_b=conv_f_b, ln2_g=ln2_g, ln2_b=ln2_b)
    small_m = dict(b_in=m_b_in, ln_a_g=m_ln_a_g, ln_a_b=m_ln_a_b, w_spatial=m_w_spatial,
                   b_spatial=m_b_spatial, conv_b_b=m_conv_b_b, ln_b_g=m_ln_b_g, ln_b_b=m_ln_b_b,
                   b_out=m_b_out, ln1_g=m_ln1_g, ln1_b=m_ln1_b, conv_f_b=m_conv_f_b, ln2_g=m_ln2_g,
                   ln2_b=m_ln2_b)
    small_v = dict(b_in=v_b_in, ln_a_g=v_ln_a_g, ln_a_b=v_ln_a_b, w_spatial=v_w_spatial,
                   b_spatial=v_b_spatial, conv_b_b=v_conv_b_b, ln_b_g=v_ln_b_g, ln_b_b=v_ln_b_b,
                   b_out=v_b_out, ln1_g=v_ln1_g, ln1_b=v_ln1_b, conv_f_b=v_conv_f_b, ln2_g=v_ln2_g,
                   ln2_b=v_ln2_b)
    order = [nm for nm, _, _ in SMALL_LAYOUT]
    small_out = small_adamw(sv_slots, [_rows128(small_w[nm]) for nm in order],
                            [_rows128(small_m[nm]) for nm in order], [_rows128(small_v[nm]) for nm in order])
    n_small = len(order)
    small = {nm: [small_out[k * n_small + p].reshape(small_w[nm].shape) for k in range(4)]
             for p, nm in enumerate(order)}
    loss = jnp.sum(small_out[4 * n_small]) * (0.5 / D)

    weights = ["w_in", "b_in", "ln_a_g", "ln_a_b", "w_spatial", "b_spatial", "conv_b_w", "conv_b_b",
               "ln_b_g", "ln_b_b", "w_out", "b_out", "ln1_g", "ln1_b", "w_up", "conv_f_w", "conv_f_b",
               "w_down", "ln2_g", "ln2_b"]
    result = lambda nm, k: big[nm][k] if nm in big else small[nm][k]
    return (loss, grad_x.reshape(x.shape), *[result(nm, 0) for nm in weights],
            *[result(nm, 1) for nm in weights], *[result(nm, 2) for nm in weights],
            *[result(nm, 3) for nm in weights])
```

```python
import functools
import math

import jax
import jax.numpy as jnp
from jax import lax
from jax.experimental import pallas as pl
from jax.experimental.pallas import tpu as pltpu

F32 = jnp.float32
BF16 = jnp.bfloat16

D = 1024
D_A = 512
D_B = 512
HEADS = 4
HEAD_DIM = 128
CHUNK = 128
KB = 31
KF = 3
D_FF = 2816
D_IN = 2048
N_DEV = 8
W_IN_BLK = D_IN // N_DEV
W_UP_BLK = 2 * D_FF // N_DEV
N_F = 4
LN_EPS = 1e-5
ALPHA = 2.0 ** 0.25

ADAM_LR = 0.001
ADAM_B1 = 0.9
ADAM_B2 = 0.999
ADAM_EPS = 1e-08
ADAM_WD = 0.01
ADAM_STEP = 10

INV_SQRT2 = 1.0 / math.sqrt(2.0)
INV_SQRT_2PI = 1.0 / math.sqrt(2.0 * math.pi)

HALO_B = 32
HALO_F = 8
ROWS = 64
VMEM_LIMIT = 58 * 1024 * 1024

MESH = pl.DeviceIdType.MESH
ANY = pl.BlockSpec(memory_space=pl.ANY)
VMEM = pl.BlockSpec(memory_space=pltpu.VMEM)

S_BIN, S_LNAG, S_LNAB, S_WS, S_BS, S_CBB, S_LNBG, S_LNBB, S_BOUT, S_LN1G, S_LN1B = (
    0, 16, 24, 32, 544, 552, 560, 568, 576, 584, 592)
S_MIX_ROWS = 600
S_CFB = 600
S_LN2G = 648
S_LN2B = 656
S_LOSS = 664
S_ROWS = 672


def _tn(a, b):
    return lax.dot_general(a, b, (((0,), (0,)), ((), ())), preferred_element_type=F32)


def _nt(a, b):
    return lax.dot_general(a, b, (((1,), (1,)), ((), ())), preferred_element_type=F32)


def _nn(a, b):
    return jnp.dot(a, b, preferred_element_type=F32)


def _sigmoid(x):
    return 1.0 / (1.0 + jnp.exp(-x))


def _ln_stats(x):
    mu = jnp.mean(x, axis=-1, keepdims=True)
    xc = x - mu
    var = jnp.mean(xc * xc, axis=-1, keepdims=True)
    rstd = lax.rsqrt(var + LN_EPS)
    return xc * rstd, rstd


def _ln_bwd(dxhat, xhat, rstd):
    m1 = jnp.mean(dxhat, axis=-1, keepdims=True)
    m2 = jnp.mean(dxhat * xhat, axis=-1, keepdims=True)
    return rstd * (dxhat - m1 - xhat * m2)


def _rsum8(x):
    r, n = x.shape
    return x.reshape(r // 8, 8, n).sum(axis=0)


def _rows(i, n=ROWS):
    return pl.ds(i * n, n)


def _loop(n, body):
    for i in range(n):
        body(i)


def _tril_mask():
    r = lax.broadcasted_iota(jnp.int32, (CHUNK, CHUNK), 0)
    c = lax.broadcasted_iota(jnp.int32, (CHUNK, CHUNK), 1)
    return c <= r


def _mixer_a_fwd(hu, hv, ga_ref, ba_ref, wsm_ref, bst_ref):
    cdf_u = 0.5 * (1.0 + lax.erf(hu * INV_SQRT2))
    cdf_v = 0.5 * (1.0 + lax.erf(hv * INV_SQRT2))
    u = hu * cdf_u
    v = hv * cdf_v
    xhats, rstds, vns, svs = [], [], [], []
    for hd in range(HEADS):
        sl = slice(hd * HEAD_DIM, (hd + 1) * HEAD_DIM)
        xhat, rstd = _ln_stats(v[:, sl])
        vn = (xhat * ga_ref[hd:hd + 1, :] + ba_ref[hd:hd + 1, :]).astype(BF16)
        sv = _nn(wsm_ref[hd], vn) + bst_ref[:, hd:hd + 1]
        xhats.append(xhat)
        rstds.append(rstd)
        vns.append(vn)
        svs.append(sv)
    return u, cdf_u, cdf_v, xhats, rstds, vns, svs


def _shifted(win):
    n = win.shape[0]
    return [win] + [pltpu.roll(win, n - s, 0) for s in range(1, 8)]


def _tap(shifted, offset):
    s = offset % 8
    return shifted[s][offset - s:offset - s + ROWS, :]


def _conv_b_block(ext_ref, base, cw_ref):
    win = _shifted(ext_ref[pl.ds(base, ROWS + HALO_B), :])
    acc = jnp.zeros((ROWS, D_B), F32)
    for k in range(KB):
        acc = acc + _tap(win, 2 + k) * cw_ref[k:k + 1, :]
    return acc, win


def _taps_f(win):
    n = ROWS + HALO_F
    return [pltpu.roll(win, n - 6, 0)[0:ROWS, :], pltpu.roll(win, n - 7, 0)[0:ROWS, :], win[8:n, :]]


def _params(sem, **kw):
    return pltpu.CompilerParams(dimension_semantics=sem, vmem_limit_bytes=VMEM_LIMIT, **kw)


def _resident(shape):
    zeros = (0,) * len(shape)
    return pl.BlockSpec(shape, lambda *_: zeros, pipeline_mode=pl.Buffered(1))


def _full(shape):
    zeros = (0,) * len(shape)
    return pl.BlockSpec(shape, lambda *_: zeros)


def _mesh_pos():
    return lax.axis_index("x"), lax.axis_index("y"), lax.axis_index("c")


def _chip_patterns(x, y):
    return [(x, y), (1 - x, y), (x, 1 - y), (1 - x, 1 - y)]


def _lid(x, y, c):
    return 4 * x + 2 * y + c


def _gather_copy(outs, send_sems, recv_sems, a, k, block, to, src=None):
    blk = outs[a].at[_lid(*block)]
    return pltpu.make_async_remote_copy(
        src_ref=blk if src is None else src, dst_ref=blk,
        send_sem=send_sems.at[a, k], recv_sem=recv_sems.at[a, k], device_id=to, device_id_type=MESH)


def _gather_start(mine, outs, send_sems, recv_sems, local_sems):
    x, y, c = _mesh_pos()
    me, sib = (x, y, c), (x, y, 1 - c)
    for a in range(len(mine)):
        pltpu.make_async_copy(mine[a], outs[a].at[_lid(*me)], local_sems.at[a]).start()
        _gather_copy(outs, send_sems, recv_sems, a, 0, me, sib, src=mine[a]).start()
        for j, chip in enumerate(_chip_patterns(x, y)[1:]):
            _gather_copy(outs, send_sems, recv_sems, a, 1 + j, me, (*chip, c), src=mine[a]).start()


def _gather_finish(mine, outs, send_sems, recv_sems, local_sems):
    x, y, c = _mesh_pos()
    me, sib = (x, y, c), (x, y, 1 - c)
    chips = _chip_patterns(x, y)[1:]
    n = len(mine)
    copy = functools.partial(_gather_copy, outs, send_sems, recv_sems)
    passed = []
    for j, chip in enumerate(chips):
        for a in range(n):
            copy(a, 1 + j, (*chip, c), me).wait_recv()
            cp = copy(a, 4 + j, (*chip, c), sib)
            cp.start()
            passed.append(cp)
    for a in range(n):
        copy(a, 0, sib, me).wait_recv()
        for j, chip in enumerate(chips):
            copy(a, 4 + j, (*chip, 1 - c), me).wait_recv()
        for k in range(4):
            copy(a, k, me, sib, src=mine[a]).wait_send()
        pltpu.make_async_copy(mine[a], outs[a].at[_lid(*me)], local_sems.at[a]).wait()
    for cp in passed:
        cp.wait_send()


def _gather_scratch(n):
    return [pltpu.SemaphoreType.DMA((n, 7)), pltpu.SemaphoreType.DMA((n, 7)), pltpu.SemaphoreType.DMA((n,))]


def all_gather_mixer_weights(w_in, w_out, w_up, w_down, convp):
    srcs = [w_in, w_out, convp]
    n = len(srcs)

    def body(win_ref, wout_ref, convp_ref, wup_ref, wdown_ref,
             gin_ref, gout_ref, gconv_ref, sup_ref, sdown_ref,
             sin_ref, sout_ref, send_sems, recv_sems, local_sems):
        sin_ref[...] = win_ref[...].astype(BF16)
        sout_ref[...] = wout_ref[...].astype(BF16)
        mine = [sin_ref, sout_ref, convp_ref]
        outs = [gin_ref, gout_ref, gconv_ref]
        _gather_start(mine, outs, send_sems, recv_sems, local_sems)
        sup_ref[...] = wup_ref[...].astype(BF16)
        sdown_ref[...] = wdown_ref[...].astype(BF16)
        _gather_finish(mine, outs, send_sems, recv_sems, local_sems)

    return pl.pallas_call(
        body, name="all_gather_mixer_weights",
        out_shape=[jax.ShapeDtypeStruct((N_DEV,) + w_in.shape, BF16),
                   jax.ShapeDtypeStruct((N_DEV,) + w_out.shape, BF16),
                   jax.ShapeDtypeStruct((N_DEV,) + convp.shape, F32),
                   jax.ShapeDtypeStruct(w_up.shape, BF16), jax.ShapeDtypeStruct(w_down.shape, BF16)],
        in_specs=[VMEM] * 5, out_specs=[ANY] * n + [VMEM, VMEM],
        scratch_shapes=[pltpu.VMEM(w_in.shape, BF16), pltpu.VMEM(w_out.shape, BF16)] + _gather_scratch(n),
        compiler_params=pltpu.CompilerParams(vmem_limit_bytes=VMEM_LIMIT),
    )(w_in, w_out, convp, w_up, w_down)


def _chip_copies(p, land, send_sems, recv_sems):
    x, y, c = _mesh_pos()
    return [pltpu.make_async_remote_copy(
        src_ref=p[a].at[k], dst_ref=land[a].at[k], send_sem=send_sems.at[a, k], recv_sem=recv_sems.at[a, k],
        device_id=(px, py, c), device_id_type=MESH)
        for k, (px, py) in enumerate(_chip_patterns(x, y)[1:]) for a in range(len(p))]


def chip_partials(name, g, land, jidx, rb):
    _, r, c = g.shape

    def body(j_ref, g_ref, l_ref, o_ref):
        o_ref[...] = (g_ref[...] + l_ref[...]).astype(BF16)

    return pl.pallas_call(
        body, name=name,
        out_shape=jax.ShapeDtypeStruct((3, r, c), BF16),
        grid_spec=pltpu.PrefetchScalarGridSpec(
            num_scalar_prefetch=1, grid=(3, r // rb),
            in_specs=[pl.BlockSpec((1, rb, c), lambda k, i, j: (j[1 + k], i, 0)),
                      pl.BlockSpec((1, rb, c), lambda k, i, j: (1 + k, i, 0))],
            out_specs=pl.BlockSpec((1, rb, c), lambda k, i, j: (k, i, 0))),
        compiler_params=_params(("arbitrary", "arbitrary")),
    )(jidx, g, land)


def _adamw(w, g, m, v):
    m2 = ADAM_B1 * m + (1.0 - ADAM_B1) * g
    v2 = ADAM_B2 * v + (1.0 - ADAM_B2) * (g * g)
    m_hat = m2 / (1.0 - ADAM_B1 ** ADAM_STEP)
    v_hat = v2 / (1.0 - ADAM_B2 ** ADAM_STEP)
    delta = -ADAM_LR * (m_hat / (jnp.sqrt(v_hat) + ADAM_EPS) + ADAM_WD * w)
    return delta, m2, v2


def reduce_and_adamw(name, g, land, recv, w, m, v, jidx, rb):
    _, r, c = g.shape

    def body(j_ref, g_ref, l_ref, r_ref, w_ref, m_ref, v_ref, go_ref, do_ref, mo_ref, vo_ref):
        grad = (g_ref[0] + l_ref[0]) + r_ref[0].astype(F32) + r_ref[1].astype(F32) + r_ref[2].astype(F32)
        delta, m2, v2 = _adamw(w_ref[...], grad, m_ref[...], v_ref[...])
        go_ref[...] = grad
        do_ref[...] = delta
        mo_ref[...] = m2
        vo_ref[...] = v2

    blk = pl.BlockSpec((rb, c), lambda i, j: (i, 0))
    return pl.pallas_call(
        body, name=name,
        out_shape=[jax.ShapeDtypeStruct((r, c), F32)] * 4,
        grid_spec=pltpu.PrefetchScalarGridSpec(
            num_scalar_prefetch=1, grid=(r // rb,),
            in_specs=[pl.BlockSpec((1, rb, c), lambda i, j: (j[0], i, 0)),
                      pl.BlockSpec((1, rb, c), lambda i, j: (0, i, 0)),
                      pl.BlockSpec((3, rb, c), lambda i, j: (0, i, 0)),
                      blk, blk, blk],
            out_specs=[blk] * 4),
        compiler_params=_params(("arbitrary",)),
    )(jidx, g, land, recv, w, m, v)


def mixer_reduce_adamw(grads, svec, ws, ms, vs):
    n = len(grads)
    shard = [g.shape[1:] for g in grads]

    def body(*refs):
        g = refs[:n]
        sv_ref = refs[n]
        w, m, v = refs[n + 1:2 * n + 1], refs[2 * n + 1:3 * n + 1], refs[3 * n + 1:4 * n + 1]
        outs = refs[4 * n + 1:8 * n + 1]
        sv_slots = refs[8 * n + 1]
        rest = refs[8 * n + 2:]
        own, land, sendb, recvb = rest[:n], rest[n:2 * n], rest[2 * n:3 * n], rest[3 * n:4 * n]
        sv_land, chip_sv, d2d_send, d2d_recv, ici_send, ici_recv, local_sems, sv_sems = rest[4 * n:]
        x, y, c = _mesh_pos()
        sib = (x, y, 1 - c)
        pats = _chip_patterns(x, y)
        q = 2 * x + y

        d2d, local = [], []
        for a in range(n):
            for k, (px, py) in enumerate(pats):
                d2d.append(pltpu.make_async_remote_copy(
                    src_ref=g[a].at[_lid(px, py, 1 - c)], dst_ref=land[a].at[k],
                    send_sem=d2d_send.at[a, k], recv_sem=d2d_recv.at[a, k], device_id=sib, device_id_type=MESH))
                local.append(pltpu.make_async_copy(g[a].at[_lid(px, py, c)], own[a].at[k], local_sems.at[a, k]))
        d2d.append(pltpu.make_async_remote_copy(
            src_ref=sv_ref, dst_ref=sv_land, send_sem=d2d_send.at[n, 0], recv_sem=d2d_recv.at[n, 0],
            device_id=sib, device_id_type=MESH))
        for cp in d2d + local:
            cp.start()
        for cp in local + d2d:
            cp.wait()

        for a in range(n):
            for k in range(3):
                sendb[a][k] = (own[a][1 + k] + land[a][1 + k]).astype(BF16)
        chip_sv[...] = sv_ref[...] + sv_land[...]
        ici = _chip_copies(sendb, recvb, ici_send, ici_recv)
        sv_local = pltpu.make_async_copy(chip_sv, sv_slots.at[q], sv_sems.at[0])
        sv_out = [pltpu.make_async_remote_copy(
            src_ref=chip_sv, dst_ref=sv_slots.at[q], send_sem=sv_sems.at[1 + k], recv_sem=sv_sems.at[4 + k],
            device_id=(px, py, c), device_id_type=MESH) for k, (px, py) in enumerate(pats[1:])]
        for cp in ici + sv_out + [sv_local]:
            cp.start()
        for cp in ici:
            cp.wait()
        for k, (px, py) in enumerate(pats[1:]):
            sv_out[k].wait_send()
            pltpu.make_async_remote_copy(
                src_ref=chip_sv, dst_ref=sv_slots.at[2 * px + py], send_sem=sv_sems.at[1 + k],
                recv_sem=sv_sems.at[4 + k], device_id=(px, py, c), device_id_type=MESH).wait_recv()
        sv_local.wait()

        for a in range(n):
            grad = ((own[a][0] + land[a][0]) + recvb[a][0].astype(F32) + recvb[a][1].astype(F32)
                    + recvb[a][2].astype(F32))
            delta, m2, v2 = _adamw(w[a][...], grad, m[a][...], v[a][...])
            outs[a][...] = grad
            outs[n + a][...] = delta
            outs[2 * n + a][...] = m2
            outs[3 * n + a][...] = v2

    shard_out = [jax.ShapeDtypeStruct(s, F32) for s in shard]
    return pl.pallas_call(
        body, name="mixer_reduce_adamw",
        out_shape=shard_out * 4 + [jax.ShapeDtypeStruct((4,) + svec.shape, F32)],
        in_specs=[ANY] * n + [VMEM] * (1 + 3 * n), out_specs=[VMEM] * (4 * n) + [ANY],
        scratch_shapes=[pltpu.VMEM((4,) + s, F32) for s in shard] + [pltpu.VMEM((4,) + s, F32) for s in shard]
        + [pltpu.VMEM((3,) + s, BF16) for s in shard] + [pltpu.VMEM((3,) + s, BF16) for s in shard]
        + [pltpu.VMEM(svec.shape, F32), pltpu.VMEM(svec.shape, F32),
           pltpu.SemaphoreType.DMA((n + 1, 4)), pltpu.SemaphoreType.DMA((n + 1, 4)),
           pltpu.SemaphoreType.DMA((n, 3)), pltpu.SemaphoreType.DMA((n, 3)),
           pltpu.SemaphoreType.DMA((n, 4)), pltpu.SemaphoreType.DMA((7,))],
        compiler_params=pltpu.CompilerParams(vmem_limit_bytes=VMEM_LIMIT),
    )(*grads, svec, *ws, *ms, *vs)


SMALL_LAYOUT = [
    ("b_in", S_BIN, 16), ("ln_a_g", S_LNAG, 4), ("ln_a_b", S_LNAB, 4), ("w_spatial", S_WS, 512),
    ("b_spatial", S_BS, 4), ("conv_b_b", S_CBB, 4), ("ln_b_g", S_LNBG, 4), ("ln_b_b", S_LNBB, 4),
    ("b_out", S_BOUT, 8), ("ln1_g", S_LN1G, 8), ("ln1_b", S_LN1B, 8), ("conv_f_b", S_CFB, 44),
    ("ln2_g", S_LN2G, 8), ("ln2_b", S_LN2B, 8),
]


def small_adamw(sv_slots, ws, ms, vs):
    n = len(SMALL_LAYOUT)

    def body(*refs):
        s_ref = refs[0]
        w_refs, m_refs, v_refs = refs[1:1 + n], refs[1 + n:1 + 2 * n], refs[1 + 2 * n:1 + 3 * n]
        outs = refs[1 + 3 * n:]
        for p, (_, row0, rows) in enumerate(SMALL_LAYOUT):
            sl = pl.ds(row0, rows)
            grad = ((s_ref[0, sl, :] + s_ref[1, sl, :]) + s_ref[2, sl, :]) + s_ref[3, sl, :]
            delta, m2, v2 = _adamw(w_refs[p][...], grad, m_refs[p][...], v_refs[p][...])
            outs[p][...] = grad
            outs[n + p][...] = delta
            outs[2 * n + p][...] = m2
            outs[3 * n + p][...] = v2
        sl = pl.ds(S_LOSS, 8)
        outs[4 * n][...] = ((s_ref[0, sl, :] + s_ref[1, sl, :]) + s_ref[2, sl, :]) + s_ref[3, sl, :]

    shapes = [jax.ShapeDtypeStruct((rows, 128), F32) for _, _, rows in SMALL_LAYOUT]
    return pl.pallas_call(
        body, name="small_adamw", out_shape=shapes * 4 + [jax.ShapeDtypeStruct((8, 128), F32)],
        in_specs=[VMEM] * (1 + 3 * n), out_specs=[VMEM] * (4 * n + 1),
    )(sv_slots, *ws, *ms, *vs)


def mix_forward(x, win_g, b_in, ln_a_g, ln_a_b, w_spatial, bst, conv_b_w, conv_b_b, ln_b_g, ln_b_b,
                wout, b_out, ln1_g, ln1_b, sup, sdown, tm):
    t = x.shape[0]
    nt = t // tm
    n_chunks = tm // CHUNK

    def body(x_ref, win_ref, bin_ref, ga_ref, ba_ref, ws_ref, bst_ref, cw_ref, cb_ref, gb_ref,
             bb_ref, wout_ref, bout_ref, g1_ref, b1_ref, sup_ref, sdown_ref,
             h_ref, xhat1_ref, rstd1_ref, yb1_ref, gup_ref, gdown_ref,
             ext_ref, y_ref, wsm_ref, send_sems, recv_sems, local_sems):
        i = pl.program_id(0)
        gather = ([sup_ref, sdown_ref], [gup_ref, gdown_ref], send_sems, recv_sems, local_sems)

        @pl.when(i == 0)
        def _():
            _gather_start(*gather)
            ext_ref[0:HALO_B, :] = jnp.zeros((HALO_B, D_B), F32)
            mask = _tril_mask()
            for hd in range(HEADS):
                wsm_ref[hd] = jnp.where(mask, ws_ref[hd], 0.0).astype(BF16)

        xb = x_ref[...].astype(BF16)
        for j in range(N_DEV):
            cols = slice(j * W_IN_BLK, (j + 1) * W_IN_BLK)
            h_ref[:, cols] = _nn(xb, win_ref[j]) + bin_ref[:, cols]

        def chunk(ci):
            r = _rows(ci, CHUNK)
            u, _, _, _, _, _, svs = _mixer_a_fwd(h_ref[r, 0:D_A], h_ref[r, D_A:2 * D_A],
                                                 ga_ref, ba_ref, wsm_ref, bst_ref)
            for hd in range(HEADS):
                sl = slice(hd * HEAD_DIM, (hd + 1) * HEAD_DIM)
                y_ref[r, sl] = (u[:, sl] * svs[hd]).astype(BF16)
            a_b = h_ref[r, 2 * D_A:2 * D_A + D_B]
            g_b = h_ref[r, 2 * D_A + D_B:D_IN]
            ext_ref[pl.ds(HALO_B + ci * CHUNK, CHUNK), :] = a_b * _sigmoid(g_b)

        _loop(n_chunks, chunk)

        def conv_rows(bi):
            base = bi * ROWS
            acc, _ = _conv_b_block(ext_ref, base, cw_ref)
            yb1 = acc + cb_ref[...]
            yb1_ref[pl.ds(base, ROWS), :] = yb1
            xhat, _ = _ln_stats(yb1)
            yb2 = xhat * gb_ref[...] + bb_ref[...]
            y_ref[pl.ds(base, ROWS), D_A:D] = (yb2 * _sigmoid(yb2)).astype(BF16)

        _loop(tm // ROWS, conv_rows)
        ext_ref[0:HALO_B, :] = ext_ref[tm:tm + HALO_B, :]

        mix = _nn(y_ref[...], wout_ref[...]) + bout_ref[...]
        xhat1, rstd1 = _ln_stats(ALPHA * x_ref[...] + mix)
        xhat1_ref[...] = xhat1
        rstd1_ref[...] = jnp.broadcast_to(rstd1, (tm, 128))

        @pl.when(i == nt - 1)
        def _():
            _gather_finish(*gather)

    row = lambda w: pl.BlockSpec((tm, w), lambda i: (i, 0))
    return pl.pallas_call(
        body, name="mix_forward", grid=(nt,),
        in_specs=[row(D), _resident(win_g.shape), _full(b_in.shape), _full(ln_a_g.shape),
                  _full(ln_a_b.shape), _full(w_spatial.shape), _full(bst.shape),
                  _full(conv_b_w.shape), _full(conv_b_b.shape), _full(ln_b_g.shape),
                  _full(ln_b_b.shape), _resident(wout.shape), _full(b_out.shape),
                  _full(ln1_g.shape), _full(ln1_b.shape), ANY, ANY],
        out_specs=[row(D_IN), row(D), row(128), row(D_B), ANY, ANY],
        out_shape=[jax.ShapeDtypeStruct((t, D_IN), F32), jax.ShapeDtypeStruct((t, D), F32),
                   jax.ShapeDtypeStruct((t, 128), F32), jax.ShapeDtypeStruct((t, D_B), F32),
                   jax.ShapeDtypeStruct((N_DEV,) + sup.shape, BF16),
                   jax.ShapeDtypeStruct((N_DEV,) + sdown.shape, BF16)],
        scratch_shapes=[pltpu.VMEM((tm + HALO_B, D_B), F32), pltpu.VMEM((tm, D), BF16),
                        pltpu.VMEM((HEADS, CHUNK, CHUNK), BF16)] + _gather_scratch(2),
        compiler_params=_params(("arbitrary",)),
    )(x, win_g, b_in, ln_a_g, ln_a_b, w_spatial, bst, conv_b_w, conv_b_b, ln_b_g, ln_b_b,
      wout, b_out, ln1_g, ln1_b, sup, sdown)


def ffn_forward(xhat1, ln1_g, ln1_b, wup_g, cfw, cfb, wdown, ln2_g, ln2_b, target, tm):
    t = xhat1.shape[0]
    nt = t // tm

    def body(xh_ref, g1_ref, b1_ref, wup_ref, cfw_ref, cfb_ref, wdown_ref, g2_ref, b2_ref, tgt_ref,
             hu_ref, gv_ref, dr2_ref, loss_ref, sln2_ref,
             x1_ref, x1b_ref, carry_ref, gbuf_ref, ffn_ref, acc_loss, acc_g2, acc_b2):
        i = pl.program_id(0)

        @pl.when(i == 0)
        def _():
            carry_ref[...] = jnp.zeros(carry_ref.shape, F32)
            acc_loss[...] = jnp.zeros(acc_loss.shape, F32)
            acc_g2[...] = jnp.zeros(acc_g2.shape, F32)
            acc_b2[...] = jnp.zeros(acc_b2.shape, F32)

        x1 = xh_ref[...] * g1_ref[...] + b1_ref[...]
        x1_ref[...] = x1
        x1b_ref[...] = x1.astype(BF16)

        def conv(j, base):
            if base == 0:
                win = jnp.concatenate([carry_ref[j], hu_ref[j, 0:ROWS, :]], axis=0)
            else:
                win = hu_ref[j, base - HALO_F:base + ROWS, :]
            taps = _taps_f(win)
            w = cfw_ref[j]
            return sum(taps[k] * w[k:k + 1, :] for k in range(KF)) + cfb_ref[j:j + 1, :]

        for f in range(N_F):
            hu_ref[f] = _nt(x1b_ref[...], wup_ref[f])
            hu_ref[N_F + f] = _nt(x1b_ref[...], wup_ref[N_F + f])

            def rows(bi, f=f):
                gate = conv(f, bi * ROWS)
                val = conv(N_F + f, bi * ROWS)
                gbuf_ref[_rows(bi), :] = (gate * _sigmoid(gate) * val).astype(BF16)
                gv_ref[f, _rows(bi), :] = gate.astype(BF16)
                gv_ref[N_F + f, _rows(bi), :] = val.astype(BF16)

            _loop(tm // ROWS, rows)
            carry_ref[f] = hu_ref[f, tm - HALO_F:tm, :]
            carry_ref[N_F + f] = hu_ref[N_F + f, tm - HALO_F:tm, :]
            part = _nn(gbuf_ref[...], wdown_ref[f])
            if f == 0:
                ffn_ref[...] = part
            else:
                ffn_ref[...] += part

        def tail(bi):
            r = _rows(bi)
            xhat2, rstd2 = _ln_stats(ALPHA * x1_ref[r, :] + ffn_ref[r, :])
            err = xhat2 * g2_ref[...] + b2_ref[...] - tgt_ref[r, :]
            e2 = _rsum8(err * err)
            acc_loss[...] += sum(e2[:, k * 128:(k + 1) * 128] for k in range(D // 128))
            dy = err * (1.0 / D)
            acc_g2[...] += _rsum8(dy * xhat2)
            acc_b2[...] += _rsum8(dy)
            dr2_ref[r, :] = _ln_bwd(dy * g2_ref[...], xhat2, rstd2)

        _loop(tm // ROWS, tail)
        loss_ref[...] = acc_loss[...]

        @pl.when(i == nt - 1)
        def _():
            dg = jnp.sum(acc_g2[...], axis=0, keepdims=True)
            db = jnp.sum(acc_b2[...], axis=0, keepdims=True)
            for k in range(D // 128):
                sln2_ref[k:k + 1, :] = dg[:, k * 128:(k + 1) * 128]
                sln2_ref[8 + k:9 + k, :] = db[:, k * 128:(k + 1) * 128]

    row = pl.BlockSpec((tm, D), lambda i: (i, 0))
    return pl.pallas_call(
        body, name="ffn_forward", grid=(nt,),
        in_specs=[row, _full(ln1_g.shape), _full(ln1_b.shape), _resident(wup_g.shape),
                  _full(cfw.shape), _full(cfb.shape), _resident(wdown.shape),
                  _full(ln2_g.shape), _full(ln2_b.shape), row],
        out_specs=[pl.BlockSpec((N_DEV, tm, W_UP_BLK), lambda i: (0, i, 0)),
                   pl.BlockSpec((N_DEV, tm, W_UP_BLK), lambda i: (0, i, 0)), row,
                   _full((8, 128)), _full((16, 128))],
        out_shape=[jax.ShapeDtypeStruct((N_DEV, t, W_UP_BLK), F32),
                   jax.ShapeDtypeStruct((N_DEV, t, W_UP_BLK), BF16), jax.ShapeDtypeStruct((t, D), F32),
                   jax.ShapeDtypeStruct((8, 128), F32), jax.ShapeDtypeStruct((16, 128), F32)],
        scratch_shapes=[pltpu.VMEM((tm, D), F32), pltpu.VMEM((tm, D), BF16),
                        pltpu.VMEM((N_DEV, HALO_F, W_UP_BLK), F32), pltpu.VMEM((tm, W_UP_BLK), BF16),
                        pltpu.VMEM((tm, D), F32), pltpu.VMEM((8, 128), F32),
                        pltpu.VMEM((8, D), F32), pltpu.VMEM((8, D), F32)],
        compiler_params=_params(("arbitrary",)),
    )(xhat1, ln1_g, ln1_b, wup_g, cfw, cfb, wdown, ln2_g, ln2_b, target)


def ffn_backward(order, dr2, xhat1, ln1_g, ln1_b, hu, gv, wup_g, cfw, wdown, tm):
    t = dr2.shape[0]
    nt = t // tm
    sub_rows = tm
    hu4 = hu.reshape(2, N_F, t, W_UP_BLK)
    gv4 = gv.reshape(2, N_F, t, W_UP_BLK)
    wup4 = wup_g.reshape(2, N_F, W_UP_BLK, D)
    cfw4 = cfw.reshape(2, N_F, KF, W_UP_BLK)

    def body(order_ref, dr2_ref, xh_ref, g1_ref, b1_ref, hu_ref, gv_ref, wup_ref, cfw_ref, wdown_ref,
             dwup_ref, dwdown_ref, dcfw_ref, dcfb_ref, dx1_ref, land_up_ref, land_down_ref,
             x1b_ref, drb_ref, dg_ref, dextg_ref, dextv_ref, gbuf_ref,
             dhug_ref, dhuv_ref, acc_wup, acc_wdown, acc_cfw, acc_cfb, sem, send_sems, recv_sems):
        f = order_ref[pl.program_id(0)]
        i = pl.program_id(1)
        x, y, c = _mesh_pos()
        half = D_FF // N_DEV

        def to_sibling(fi, k, src, land_ref, shard_chip):
            d = jnp.bitwise_xor(shard_chip, 2 * x + y)
            slot = jnp.where(d == 1, 2, jnp.where(d == 2, 1, d))
            return pltpu.make_async_remote_copy(
                src_ref=src, dst_ref=land_ref.at[slot], send_sem=send_sems.at[fi, k], recv_sem=recv_sems.at[fi, k],
                device_id=(x, y, 1 - c), device_id_type=MESH)

        def up_copy(fi, g):
            return to_sibling(fi, g, dwup_ref.at[g, fi], land_up_ref, 2 * g + fi // 2)

        def down_copy(fi):
            return to_sibling(fi, 2, dwdown_ref.at[fi, pl.ds((1 - c) * half, half)], land_down_ref, fi)

        @pl.when(i == 0)
        def _():
            acc_wup[...] = jnp.zeros(acc_wup.shape, F32)
            acc_wdown[...] = jnp.zeros(acc_wdown.shape, F32)
            acc_cfw[...] = jnp.zeros(acc_cfw.shape, F32)
            acc_cfb[...] = jnp.zeros(acc_cfb.shape, F32)
            dextg_ref[tm:tm + HALO_F, :] = jnp.zeros((HALO_F, W_UP_BLK), F32)
            dextv_ref[tm:tm + HALO_F, :] = jnp.zeros((HALO_F, W_UP_BLK), F32)

        w = [cfw_ref[0, 0], cfw_ref[1, 0]]
        dext = [dextg_ref, dextv_ref]
        dhu = [dhug_ref, dhuv_ref]

        def rows1(bi):
            r = _rows(bi)
            gate = gv_ref[0, 0, r, :].astype(F32)
            val = gv_ref[1, 0, r, :].astype(F32)
            sg = _sigmoid(gate)
            silu = gate * sg
            gbuf_ref[r, :] = (silu * val).astype(BF16)
            dg = dg_ref[r, :]
            dgate = dg * val * (sg * (1.0 + gate * (1.0 - sg)))
            dval = dg * silu
            dextg_ref[r, :] = dgate
            dextv_ref[r, :] = dval
            acc_cfb[0:8, :] += _rsum8(dgate)
            acc_cfb[8:16, :] += _rsum8(dval)

        def rows2(bi):
            r = _rows(bi)
            for g in range(2):
                win = dext[g][pl.ds(bi * ROWS, ROWS + HALO_F), :]
                n = ROWS + HALO_F
                later = [pltpu.roll(win, n - 2, 0)[0:ROWS, :], pltpu.roll(win, n - 1, 0)[0:ROWS, :],
                         win[0:ROWS, :]]
                d = sum(later[k] * w[g][k:k + 1, :] for k in range(KF))
                dhu[g][r, :] = d.astype(BF16)
                pre = hu_ref[g, 0, r, :]
                for k in range(KF):
                    r0 = 8 * (g * KF + k)
                    acc_cfw[r0:r0 + 8, :] += _rsum8(later[k] * pre)

        for sub in reversed(range(tm // sub_rows)):
            rs = slice(sub * sub_rows, (sub + 1) * sub_rows)
            blocks = range(sub * sub_rows // ROWS, (sub + 1) * sub_rows // ROWS)
            x1b_ref[rs, :] = (xh_ref[rs, :] * g1_ref[...] + b1_ref[...]).astype(BF16)
            drb_ref[rs, :] = dr2_ref[rs, :].astype(BF16)
            dg_ref[rs, :] = _nt(drb_ref[rs, :], wdown_ref[0])
            for bi in blocks:
                rows1(bi)
            for bi in blocks:
                rows2(bi)
            acc_wdown[...] += _tn(gbuf_ref[rs, :], drb_ref[rs, :])
            acc_wup[0] += _tn(dhug_ref[rs, :], x1b_ref[rs, :])
            acc_wup[1] += _tn(dhuv_ref[rs, :], x1b_ref[rs, :])
            dx1_ref[0, rs, :] = (_nn(dhug_ref[rs, :], wup_ref[0, 0])
                                 + _nn(dhuv_ref[rs, :], wup_ref[1, 0])).astype(BF16)
        dextg_ref[tm:tm + HALO_F, :] = dextg_ref[0:HALO_F, :]
        dextv_ref[tm:tm + HALO_F, :] = dextv_ref[0:HALO_F, :]

        @pl.when(i == nt - 1)
        def _():
            for g in range(2):
                dcfb_ref[g, 0] = jnp.sum(acc_cfb[8 * g:8 * g + 8, :], axis=0, keepdims=True)
                for k in range(KF):
                    r0 = 8 * (g * KF + k)
                    dcfw_ref[g, 0, k:k + 1, :] = jnp.sum(acc_cfw[r0:r0 + 8, :], axis=0, keepdims=True)
            cps = [pltpu.make_async_copy(acc_wup.at[0], dwup_ref.at[0, f], sem.at[0]),
                   pltpu.make_async_copy(acc_wup.at[1], dwup_ref.at[1, f], sem.at[1]),
                   pltpu.make_async_copy(acc_wdown, dwdown_ref.at[f], sem.at[2])]
            for cp in cps:
                cp.start()
            for cp in cps:
                cp.wait()
            down_copy(f).start()

            @pl.when(f % 2 != c)
            def _():
                up_copy(f, 0).start()
                up_copy(f, 1).start()

        @pl.when((i == nt - 1) & (pl.program_id(0) == N_F - 1))
        def _():
            for fi in range(N_F):
                down_copy(fi).wait()
                for g in range(2):
                    @pl.when(fi % 2 != c)
                    def _():
                        up_copy(fi, g).wait_send()

                    @pl.when(fi % 2 == c)
                    def _():
                        up_copy(fi, g).wait_recv()

    rev = lambda i: nt - 1 - i
    row = pl.BlockSpec((tm, D), lambda fo, i, o: (rev(i), 0))
    pair = lambda r, c: pl.BlockSpec((2, 1, r, c), lambda fo, i, o: (0, o[fo], 0, 0))
    tile = pl.BlockSpec((2, 1, tm, W_UP_BLK), lambda fo, i, o: (0, o[fo], rev(i), 0))
    return pl.pallas_call(
        body, name="ffn_backward",
        grid_spec=pltpu.PrefetchScalarGridSpec(
            num_scalar_prefetch=1, grid=(N_F, nt),
            in_specs=[row, row, _full(ln1_g.shape), _full(ln1_b.shape), tile, tile,
                      pair(W_UP_BLK, D), pair(KF, W_UP_BLK),
                      pl.BlockSpec((1, W_UP_BLK, D), lambda fo, i, o: (o[fo], 0, 0))],
            out_specs=[ANY, ANY, pair(KF, W_UP_BLK), pair(1, W_UP_BLK),
                       pl.BlockSpec((1, tm, D), lambda fo, i, o: (o[fo], rev(i), 0)), ANY, ANY],
            scratch_shapes=[pltpu.VMEM((tm, D), BF16), pltpu.VMEM((tm, D), BF16),
                            pltpu.VMEM((tm, W_UP_BLK), F32),
                            pltpu.VMEM((tm + HALO_F, W_UP_BLK), F32), pltpu.VMEM((tm + HALO_F, W_UP_BLK), F32),
                            pltpu.VMEM((tm, W_UP_BLK), BF16), pltpu.VMEM((tm, W_UP_BLK), BF16),
                            pltpu.VMEM((tm, W_UP_BLK), BF16),
                            pltpu.VMEM((2, W_UP_BLK, D), F32), pltpu.VMEM((W_UP_BLK, D), F32),
                            pltpu.VMEM((2 * KF * 8, W_UP_BLK), F32), pltpu.VMEM((16, W_UP_BLK), F32),
                            pltpu.SemaphoreType.DMA((3,)),
                            pltpu.SemaphoreType.DMA((N_F, 3)), pltpu.SemaphoreType.DMA((N_F, 3))]),
        out_shape=[jax.ShapeDtypeStruct((2, N_F, W_UP_BLK, D), F32),
                   jax.ShapeDtypeStruct((N_F, W_UP_BLK, D), F32),
                   jax.ShapeDtypeStruct((2, N_F, KF, W_UP_BLK), F32),
                   jax.ShapeDtypeStruct((2, N_F, 1, W_UP_BLK), F32),
                   jax.ShapeDtypeStruct((N_F, t, D), BF16),
                   jax.ShapeDtypeStruct((4, W_UP_BLK, D), F32),
                   jax.ShapeDtypeStruct((4, D_FF // N_DEV, D), F32)],
        compiler_params=_params(("arbitrary", "arbitrary")),
    )(order, dr2, xhat1, ln1_g, ln1_b, hu4, gv4, wup4, cfw4, wdown)


def mix_backward(x, h, yb1, dx1p, dr2, xhat1, rstd1, win_g, ln_a_g, ln_a_b, w_spatial, bst,
                 conv_b_w, ln_b_g, ln_b_b, wout, ln1_g, ffn_partials, tm):
    t = x.shape[0]
    n_p = len(ffn_partials)
    nt = t // tm
    n_chunks = tm // CHUNK
    halo_blocks = tm // HALO_B

    def body(x_ref, h_ref, halo_ref, yb1_ref, dx1p_ref, dr2_ref, xh1_ref, rstd1_ref, win_ref, ga_ref, ba_ref,
             ws_ref, bst_ref, cw_ref, gb_ref, bb_ref, wout_ref, g1_ref, *rest):
        p_refs, rest = rest[:n_p], rest[n_p:]
        gx_ref, dwin_ref, dwout_ref, dcw_ref, small_ref = rest[:5]
        land_refs, rest = rest[5:5 + n_p], rest[5 + n_p:]
        (ext_ref, dext_ref, y_ref, dy_ref, dh_ref, dmb_ref, wsm_ref,
         acc_win, acc_wout, acc_bin, acc_lnag, acc_lnab, acc_ws, acc_bs, acc_cbb, acc_lnbg,
         acc_lnbb, acc_bout, acc_ln1g, acc_ln1b, acc_cw, sem, send_sems, recv_sems) = rest
        i = pl.program_id(0)

        @pl.when(i == 0)
        def _():
            for cp in _chip_copies(p_refs, land_refs, send_sems, recv_sems):
                cp.start()

        first_tile = i == nt - 1
        accs = [acc_win, acc_wout, acc_bin, acc_lnag, acc_lnab, acc_ws, acc_bs, acc_cbb, acc_lnbg,
                acc_lnbb, acc_bout, acc_ln1g, acc_ln1b, acc_cw]

        @pl.when(i == 0)
        def _():
            for acc in accs:
                acc[...] = jnp.zeros(acc.shape, F32)
            dext_ref[tm:tm + HALO_B, :] = jnp.zeros((HALO_B, D_B), F32)
            mask = _tril_mask()
            for hd in range(HEADS):
                wsm_ref[hd] = jnp.where(mask, ws_ref[hd], 0.0).astype(BF16)

        def ln1_rows(bi):
            r = _rows(bi)
            part = [dx1p_ref[f, r, :].astype(F32) for f in range(N_F)]
            dx1 = ALPHA * dr2_ref[r, :] + ((part[0] + part[1]) + (part[2] + part[3]))
            xhat = xh1_ref[r, :]
            acc_ln1g[...] += _rsum8(dx1 * xhat)
            acc_ln1b[...] += _rsum8(dx1)
            dr1 = _ln_bwd(dx1 * g1_ref[...], xhat, rstd1_ref[r, 0:1])
            acc_bout[...] += _rsum8(dr1)
            gx_ref[r, :] = ALPHA * dr1
            dmb_ref[r, :] = dr1.astype(BF16)

        _loop(tm // ROWS, ln1_rows)
        dy_ref[...] = _nt(dmb_ref[...], wout_ref[...])

        ha = halo_ref[:, 0:D_B]
        hg = halo_ref[:, D_B:2 * D_B]
        ext_ref[0:HALO_B, :] = jnp.where(first_tile, 0.0, 1.0) * (ha * _sigmoid(hg))

        def chunk(ci):
            r = _rows(ci, CHUNK)
            hu, hv = h_ref[r, 0:D_A], h_ref[r, D_A:2 * D_A]
            u, cdf_u, cdf_v, xhats, rstds, vns, svs = _mixer_a_fwd(hu, hv, ga_ref, ba_ref, wsm_ref, bst_ref)
            for hd in range(HEADS):
                sl = slice(hd * HEAD_DIM, (hd + 1) * HEAD_DIM)
                rows8 = slice(8 * hd, 8 * hd + 8)
                dy_a = dy_ref[r, sl]
                y_ref[r, sl] = (u[:, sl] * svs[hd]).astype(BF16)
                du = dy_a * svs[hd]
                dsv = dy_a * u[:, sl]
                dsvb = dsv.astype(BF16)
                acc_bs[hd] += dsv
                acc_ws[hd] += _nt(dsvb, vns[hd])
                dvn = _tn(wsm_ref[hd], dsvb)
                acc_lnag[rows8, :] += _rsum8(dvn * xhats[hd])
                acc_lnab[rows8, :] += _rsum8(dvn)
                dv = _ln_bwd(dvn * ga_ref[hd:hd + 1, :], xhats[hd], rstds[hd])
                hus, hvs = hu[:, sl], hv[:, sl]
                slv = slice(D_A + hd * HEAD_DIM, D_A + (hd + 1) * HEAD_DIM)
                dhu = du * (cdf_u[:, sl] + hus * jnp.exp(-0.5 * hus * hus) * INV_SQRT_2PI)
                dhv = dv * (cdf_v[:, sl] + hvs * jnp.exp(-0.5 * hvs * hvs) * INV_SQRT_2PI)
                acc_bin[:, sl] += _rsum8(dhu)
                acc_bin[:, slv] += _rsum8(dhv)
                dh_ref[r, sl] = dhu.astype(BF16)
                dh_ref[r, slv] = dhv.astype(BF16)
            a_b = h_ref[r, 2 * D_A:2 * D_A + D_B]
            g_b = h_ref[r, 2 * D_A + D_B:D_IN]
            ext_ref[pl.ds(HALO_B + ci * CHUNK, CHUNK), :] = a_b * _sigmoid(g_b)

        _loop(n_chunks, chunk)

        def conv_rows(bi):
            base = bi * ROWS
            r = pl.ds(base, ROWS)
            win = _shifted(ext_ref[pl.ds(base, ROWS + HALO_B), :])
            xhat, rstd = _ln_stats(yb1_ref[r, :])
            yb2 = xhat * gb_ref[...] + bb_ref[...]
            sg = _sigmoid(yb2)
            y_ref[r, D_A:D] = (yb2 * sg).astype(BF16)
            dyb2 = dy_ref[r, D_A:D] * (sg * (1.0 + yb2 * (1.0 - sg)))
            acc_lnbg[...] += _rsum8(dyb2 * xhat)
            acc_lnbb[...] += _rsum8(dyb2)
            dyb1 = _ln_bwd(dyb2 * gb_ref[...], xhat, rstd)
            acc_cbb[...] += _rsum8(dyb1)
            dext_ref[r, :] = dyb1
            for k in range(KB):
                acc_cw[8 * k:8 * k + 8, :] += _rsum8(dyb1 * _tap(win, 2 + k))

        _loop(tm // ROWS, conv_rows)

        def convt_rows(bi):
            base = bi * ROWS
            r = pl.ds(base, ROWS)
            dwin = _shifted(dext_ref[pl.ds(base, ROWS + HALO_B), :])
            dyb0 = jnp.zeros((ROWS, D_B), F32)
            for k in range(KB):
                dyb0 = dyb0 + _tap(dwin, 30 - k) * cw_ref[k:k + 1, :]
            a_b = h_ref[r, 2 * D_A:2 * D_A + D_B]
            sg = _sigmoid(h_ref[r, 2 * D_A + D_B:D_IN])
            da_b = dyb0 * sg
            dg_b = dyb0 * a_b * sg * (1.0 - sg)
            acc_bin[:, 2 * D_A:2 * D_A + D_B] += _rsum8(da_b)
            acc_bin[:, 2 * D_A + D_B:D_IN] += _rsum8(dg_b)
            dh_ref[r, 2 * D_A:2 * D_A + D_B] = da_b.astype(BF16)
            dh_ref[r, 2 * D_A + D_B:D_IN] = dg_b.astype(BF16)

        _loop(tm // ROWS, convt_rows)
        dext_ref[tm:tm + HALO_B, :] = dext_ref[0:HALO_B, :]

        acc_wout[...] += _tn(y_ref[...], dmb_ref[...])
        xt = x_ref[...].T.astype(BF16)
        dh_blocks = [dh_ref[:, j * W_IN_BLK:(j + 1) * W_IN_BLK] for j in range(N_DEV)]
        for j in range(N_DEV):
            acc_win[j] += _nn(xt, dh_blocks[j])
        gx_ref[...] += sum(_nt(dh_blocks[j], win_ref[j]) for j in range(N_DEV))

        @pl.when(i == nt - 1)
        def _():
            cps = [pltpu.make_async_copy(acc_win, dwin_ref, sem.at[0]),
                   pltpu.make_async_copy(acc_wout, dwout_ref, sem.at[1])]
            for cp in cps:
                cp.start()
            small_ref[...] = jnp.zeros(small_ref.shape, F32)

            def put_row_vector(row0, acc):
                vec = jnp.sum(acc[...], axis=0, keepdims=True)
                for k in range(vec.shape[1] // 128):
                    small_ref[row0 + k:row0 + k + 1, :] = vec[:, k * 128:(k + 1) * 128]

            put_row_vector(S_BIN, acc_bin)
            put_row_vector(S_CBB, acc_cbb)
            put_row_vector(S_LNBG, acc_lnbg)
            put_row_vector(S_LNBB, acc_lnbb)
            put_row_vector(S_BOUT, acc_bout)
            put_row_vector(S_LN1G, acc_ln1g)
            put_row_vector(S_LN1B, acc_ln1b)
            mask = _tril_mask()
            for hd in range(HEADS):
                rows8 = slice(8 * hd, 8 * hd + 8)
                small_ref[S_LNAG + hd:S_LNAG + hd + 1, :] = jnp.sum(acc_lnag[rows8, :], axis=0, keepdims=True)
                small_ref[S_LNAB + hd:S_LNAB + hd + 1, :] = jnp.sum(acc_lnab[rows8, :], axis=0, keepdims=True)
                small_ref[S_WS + hd * CHUNK:S_WS + (hd + 1) * CHUNK, :] = jnp.where(mask, acc_ws[hd], 0.0)
                small_ref[S_BS + hd:S_BS + hd + 1, :] = jnp.sum(acc_bs[hd].T, axis=0, keepdims=True)
            for k in range(KB):
                dcw_ref[k:k + 1, :] = jnp.sum(acc_cw[8 * k:8 * k + 8, :], axis=0, keepdims=True)
            for cp in cps:
                cp.wait()
            for cp in _chip_copies(p_refs, land_refs, send_sems, recv_sems):
                cp.wait()

    rev = lambda i: nt - 1 - i
    row = lambda w: pl.BlockSpec((tm, w), lambda i: (rev(i), 0))
    return pl.pallas_call(
        body, name="mix_backward", grid=(nt,),
        in_specs=[row(D), row(D_IN),
                  pl.BlockSpec((HALO_B, 2 * D_B), lambda i: (jnp.maximum(rev(i) * halo_blocks - 1, 0), 1)),
                  row(D_B), pl.BlockSpec((N_F, tm, D), lambda i: (0, rev(i), 0)),
                  row(D), row(D), row(128), _resident(win_g.shape), _full(ln_a_g.shape),
                  _full(ln_a_b.shape), _full(w_spatial.shape), _full(bst.shape), _full(conv_b_w.shape),
                  _full(ln_b_g.shape), _full(ln_b_b.shape),
                  _resident(wout.shape), _full(ln1_g.shape)] + [ANY] * n_p,
        out_specs=[row(D), ANY, ANY, _full((KB, D_B)), _full((S_MIX_ROWS, 128))] + [ANY] * n_p,
        out_shape=[jax.ShapeDtypeStruct((t, D), F32), jax.ShapeDtypeStruct((N_DEV, D, W_IN_BLK), F32),
                   jax.ShapeDtypeStruct((D, D), F32), jax.ShapeDtypeStruct((KB, D_B), F32),
                   jax.ShapeDtypeStruct((S_MIX_ROWS, 128), F32)]
        + [jax.ShapeDtypeStruct(p.shape, BF16) for p in ffn_partials],
        scratch_shapes=[pltpu.VMEM((tm + HALO_B, D_B), F32), pltpu.VMEM((tm + HALO_B, D_B), F32),
                        pltpu.VMEM((tm, D), BF16), pltpu.VMEM((tm, D), F32), pltpu.VMEM((tm, D_IN), BF16),
                        pltpu.VMEM((tm, D), BF16),
                        pltpu.VMEM((HEADS, CHUNK, CHUNK), BF16),
                        pltpu.VMEM((N_DEV, D, W_IN_BLK), F32), pltpu.VMEM((D, D), F32),
                        pltpu.VMEM((8, D_IN), F32), pltpu.VMEM((8 * HEADS, HEAD_DIM), F32),
                        pltpu.VMEM((8 * HEADS, HEAD_DIM), F32), pltpu.VMEM((HEADS, CHUNK, CHUNK), F32),
                        pltpu.VMEM((HEADS, CHUNK, CHUNK), F32), pltpu.VMEM((8, D_B), F32),
                        pltpu.VMEM((8, D_B), F32), pltpu.VMEM((8, D_B), F32), pltpu.VMEM((8, D), F32),
                        pltpu.VMEM((8, D), F32), pltpu.VMEM((8, D), F32), pltpu.VMEM((8 * KB, D_B), F32),
                        pltpu.SemaphoreType.DMA((2,)),
                        pltpu.SemaphoreType.DMA((n_p, 3)), pltpu.SemaphoreType.DMA((n_p, 3))],
        compiler_params=_params(("arbitrary",)),
    )(x, h, h, yb1, dx1p, dr2, xhat1, rstd1, win_g, ln_a_g, ln_a_b, w_spatial, bst, conv_b_w,
      ln_b_g, ln_b_b, wout, ln1_g, *ffn_partials)


def _rows128(a):
    return a.reshape(-1, 128)


def _pack_conv(cb, cf):
    lead = cb.shape[:-2]
    pad = [(0, 0)] * len(lead)
    flat = jnp.pad(cb.reshape(lead + (KB * 64,)), pad + [(0, 3 * W_UP_BLK - KB * 64)])
    rows = jnp.concatenate([cf, flat.reshape(lead + (3, W_UP_BLK))], axis=-2)
    return jnp.pad(rows, pad + [(0, 2), (0, 768 - W_UP_BLK)])


def _unpack_conv(p):
    lead = p.shape[:-2]
    cf = p[..., 0:KF, 0:W_UP_BLK]
    cb = p[..., 3:6, 0:W_UP_BLK].reshape(lead + (3 * W_UP_BLK,))[..., :KB * 64].reshape(lead + (KB, 64))
    return cb, cf


def kernel(x, w_in, b_in, ln_a_g, ln_a_b, w_spatial, b_spatial, conv_b_w, conv_b_b, ln_b_g, ln_b_b, w_out, b_out, ln1_g, ln1_b, w_up, conv_f_w, conv_f_b, w_down, ln2_g, ln2_b, loss_target, m_w_in, m_b_in, m_ln_a_g, m_ln_a_b, m_w_spatial, m_b_spatial, m_conv_b_w, m_conv_b_b, m_ln_b_g, m_ln_b_b, m_w_out, m_b_out, m_ln1_g, m_ln1_b, m_w_up, m_conv_f_w, m_conv_f_b, m_w_down, m_ln2_g, m_ln2_b, v_w_in, v_b_in, v_ln_a_g, v_ln_a_b, v_w_spatial, v_b_spatial, v_conv_b_w, v_conv_b_b, v_ln_b_g, v_ln_b_b, v_w_out, v_b_out, v_ln1_g, v_ln1_b, v_w_up, v_conv_f_w, v_conv_f_b, v_w_down, v_ln2_g, v_ln2_b):
    t = x.shape[1]
    x2 = x.reshape(t, D)
    target = loss_target.reshape(t, D)
    tm_fwd = min(t, 512)
    tm_bwd = min(t, 256)
    tm_ffn_bwd = min(t, 512)

    xi, yi, ci = _mesh_pos()
    jidx = jnp.stack([_lid(px, py, ci) for px, py in _chip_patterns(xi, yi)]).astype(jnp.int32)

    win_g, wout_g, conv_g, sup, sdown = all_gather_mixer_weights(
        w_in, w_out, w_up.T, w_down, _pack_conv(conv_b_w, conv_f_w))
    wout_full = wout_g.reshape(D, D)
    conv_b_all, cfw = _unpack_conv(conv_g)
    conv_b_full = conv_b_all.transpose(1, 0, 2).reshape(KB, D_B)
    cfb = conv_f_b.reshape(N_DEV, W_UP_BLK)
    row = lambda a: a.reshape(1, -1)
    bst = b_spatial.T

    h, xhat1, rstd1, yb1, wup_g, wdown_g = mix_forward(
        x2, win_g, row(b_in), ln_a_g, ln_a_b, w_spatial, bst, conv_b_full, row(conv_b_b),
        row(ln_b_g), row(ln_b_b), wout_full, row(b_out), row(ln1_g), row(ln1_b), sup, sdown, tm_fwd)
    wdown4 = wdown_g.reshape(N_F, W_UP_BLK, D)
    hu, gv, dr2, loss_part, s_ln2 = ffn_forward(
        xhat1, row(ln1_g), row(ln1_b), wup_g, cfw, cfb, wdown4, row(ln2_g), row(ln2_b), target, tm_bwd)

    order = jnp.where(ci == 0, jnp.array([1, 3, 0, 2], jnp.int32), jnp.array([0, 2, 1, 3], jnp.int32))
    dwup, dwdown, dcfw, dcfb, dx1p, *ffn_lands = ffn_backward(
        order, dr2, xhat1, row(ln1_g), row(ln1_b), hu, gv, wup_g, cfw, wdown4, tm_ffn_bwd)
    ffn_grads = [dwup.reshape(N_DEV, W_UP_BLK, D), dwdown.reshape(N_DEV, D_FF // N_DEV, D)]
    ffn_partials = [chip_partials("chip_partials_" + nm, g, l, jidx, rb)
                    for nm, g, l, rb in zip(["w_up", "w_down"], ffn_grads, ffn_lands, [352, 352])]
    grad_x, dwin, dwout, dcw, s_mix, *ffn_recvs = mix_backward(
        x2, h, yb1, dx1p, dr2, xhat1, rstd1, win_g, ln_a_g, ln_a_b, w_spatial, bst,
        conv_b_full, row(ln_b_g), row(ln_b_b), wout_full, row(ln1_g), ffn_partials, tm_bwd)

    dcfb_rows = jnp.pad(dcfb.reshape(-1, 128), ((0, 4), (0, 0)))
    svec = jnp.concatenate([s_mix, dcfb_rows, s_ln2, loss_part], axis=0)
    dconv = _pack_conv(dcw.reshape(KB, N_DEV, 64).transpose(1, 0, 2), dcfw.reshape(N_DEV, KF, W_UP_BLK))
    mix_grads = [dwin, dwout.reshape(N_DEV, D // N_DEV, D), dconv]
    mix_w = [w_in, w_out, _pack_conv(conv_b_w, conv_f_w)]
    mix_m = [m_w_in, m_w_out, _pack_conv(m_conv_b_w, m_conv_f_w)]
    mix_v = [v_w_in, v_w_out, _pack_conv(v_conv_b_w, v_conv_f_w)]
    *mix_out, sv_slots = mixer_reduce_adamw(mix_grads, svec, mix_w, mix_m, mix_v)
    big = {nm: [mix_out[k * 3 + p] for k in range(4)] for p, nm in enumerate(["w_in", "w_out", "conv"])}

    ffn_w = [(w_up.T, m_w_up.T, v_w_up.T), (w_down, m_w_down, v_w_down)]
    for nm, g, l, r, (w, m, v) in zip(["w_up", "w_down"], ffn_grads, ffn_lands, ffn_recvs, ffn_w):
        big[nm] = reduce_and_adamw("reduce_adamw_" + nm, g, l, r, w, m, v, jidx, 352)
    big["w_up"] = [o.T for o in big["w_up"]]
    for k in range(4):
        cb_k, cf_k = _unpack_conv(big["conv"][k])
        big.setdefault("conv_b_w", []).append(cb_k)
        big.setdefault("conv_f_w", []).append(cf_k)

    small_w = dict(b_in=b_in, ln_a_g=ln_a_g, ln_a_b=ln_a_b, w_spatial=w_spatial, b_spatial=b_spatial,
                   conv_b_b=conv_b_b, ln_b_g=ln_b_g, ln_b_b=ln_b_b, b_out=b_out, ln1_g=ln1_g,
                   ln1_b=ln1_b, conv_f_b=conv_f_b, ln2_g=ln2_g, ln2_b=ln2_b)
    small_m = dict(b_in=m_b_in, ln_a_g=m_ln_a_g, ln_a_b=m_ln_a_b, w_spatial=m_w_spatial,
                   b_spatial=m_b_spatial, conv_b_b=m_conv_b_b, ln_b_g=m_ln_b_g, ln_b_b=m_ln_b_b,
                   b_out=m_b_out, ln1_g=m_ln1_g, ln1_b=m_ln1_b, conv_f_b=m_conv_f_b, ln2_g=m_ln2_g,
                   ln2_b=m_ln2_b)
    small_v = dict(b_in=v_b_in, ln_a_g=v_ln_a_g, ln_a_b=v_ln_a_b, w_spatial=v_w_spatial,
                   b_spatial=v_b_spatial, conv_b_b=v_conv_b_b, ln_b_g=v_ln_b_g, ln_b_b=v_ln_b_b,
                   b_out=v_b_out, ln1_g=v_ln1_g, ln1_b=v_ln1_b, conv_f_b=v_conv_f_b, ln2_g=v_ln2_g,
                   ln2_b=v_ln2_b)
    order = [nm for nm, _, _ in SMALL_LAYOUT]
    small_out = small_adamw(sv_slots, [_rows128(small_w[nm]) for nm in order],
                            [_rows128(small_m[nm]) for nm in order], [_rows128(small_v[nm]) for nm in order])
    n_small = len(order)
    small = {nm: [small_out[k * n_small + p].reshape(small_w[nm].shape) for k in range(4)]
             for p, nm in enumerate(order)}
    loss = jnp.sum(small_out[4 * n_small]) * (0.5 / D)

    weights = ["w_in", "b_in", "ln_a_g", "ln_a_b", "w_spatial", "b_spatial", "conv_b_w", "conv_b_b",
               "ln_b_g", "ln_b_b", "w_out", "b_out", "ln1_g", "ln1_b", "w_up", "conv_f_w", "conv_f_b",
               "w_down", "ln2_g", "ln2_b"]
    result = lambda nm, k: big[nm][k] if nm in big else small[nm][k]
    return (loss, grad_x.reshape(x.shape), *[result(nm, 0) for nm in weights],
            *[result(nm, 1) for nm in weights], *[result(nm, 2) for nm in weights],
            *[result(nm, 3) for nm in weights])
```

```python
import functools
import math

import jax
import jax.numpy as jnp
from jax import lax
from jax.experimental import pallas as pl
from jax.experimental.pallas import tpu as pltpu

F32 = jnp.float32
BF16 = jnp.bfloat16

D = 1024
D_A = 512
D_B = 512
HEADS = 4
HEAD_DIM = 128
CHUNK = 128
KB = 31
KF = 3
D_FF = 2816
D_IN = 2048
N_DEV = 8
W_IN_BLK = D_IN // N_DEV
W_UP_BLK = 2 * D_FF // N_DEV
N_F = 4
LN_EPS = 1e-5
ALPHA = 2.0 ** 0.25

ADAM_LR = 0.001
ADAM_B1 = 0.9
ADAM_B2 = 0.999
ADAM_EPS = 1e-08
ADAM_WD = 0.01
ADAM_STEP = 10

INV_SQRT2 = 1.0 / math.sqrt(2.0)
INV_SQRT_2PI = 1.0 / math.sqrt(2.0 * math.pi)

HALO_B = 32
HALO_F = 8
ROWS = 64
VMEM_LIMIT = 58 * 1024 * 1024

MESH = pl.DeviceIdType.MESH
ANY = pl.BlockSpec(memory_space=pl.ANY)
VMEM = pl.BlockSpec(memory_space=pltpu.VMEM)

S_BIN, S_LNAG, S_LNAB, S_WS, S_BS, S_CBB, S_LNBG, S_LNBB, S_BOUT, S_LN1G, S_LN1B = (
    0, 16, 24, 32, 544, 552, 560, 568, 576, 584, 592)
S_MIX_ROWS = 600
S_CFB = 600
S_LN2G = 648
S_LN2B = 656
S_LOSS = 664
S_ROWS = 672


def _tn(a, b):
    return lax.dot_general(a, b, (((0,), (0,)), ((), ())), preferred_element_type=F32)


def _nt(a, b):
    return lax.dot_general(a, b, (((1,), (1,)), ((), ())), preferred_element_type=F32)


def _nn(a, b):
    return jnp.dot(a, b, preferred_element_type=F32)


def _sigmoid(x):
    return 1.0 / (1.0 + jnp.exp(-x))


def _ln_stats(x):
    mu = jnp.mean(x, axis=-1, keepdims=True)
    xc = x - mu
    var = jnp.mean(xc * xc, axis=-1, keepdims=True)
    rstd = lax.rsqrt(var + LN_EPS)
    return xc * rstd, rstd


def _ln_bwd(dxhat, xhat, rstd):
    m1 = jnp.mean(dxhat, axis=-1, keepdims=True)
    m2 = jnp.mean(dxhat * xhat, axis=-1, keepdims=True)
    return rstd * (dxhat - m1 - xhat * m2)


def _rsum8(x):
    r, n = x.shape
    return x.reshape(r // 8, 8, n).sum(axis=0)


def _rows(i, n=ROWS):
    return pl.ds(i * n, n)


def _loop(n, body):
    for i in range(n):
        body(i)


def _tril_mask():
    r = lax.broadcasted_iota(jnp.int32, (CHUNK, CHUNK), 0)
    c = lax.broadcasted_iota(jnp.int32, (CHUNK, CHUNK), 1)
    return c <= r


def _mixer_a_fwd(hu, hv, ga_ref, ba_ref, wsm_ref, bst_ref):
    cdf_u = 0.5 * (1.0 + lax.erf(hu * INV_SQRT2))
    cdf_v = 0.5 * (1.0 + lax.erf(hv * INV_SQRT2))
    u = hu * cdf_u
    v = hv * cdf_v
    xhats, rstds, vns, svs = [], [], [], []
    for hd in range(HEADS):
        sl = slice(hd * HEAD_DIM, (hd + 1) * HEAD_DIM)
        xhat, rstd = _ln_stats(v[:, sl])
        vn = (xhat * ga_ref[hd:hd + 1, :] + ba_ref[hd:hd + 1, :]).astype(BF16)
        sv = _nn(wsm_ref[hd], vn) + bst_ref[:, hd:hd + 1]
        xhats.append(xhat)
        rstds.append(rstd)
        vns.append(vn)
        svs.append(sv)
    return u, cdf_u, cdf_v, xhats, rstds, vns, svs


def _shifted(win):
    n = win.shape[0]
    return [win] + [pltpu.roll(win, n - s, 0) for s in range(1, 8)]


def _tap(shifted, offset):
    s = offset % 8
    return shifted[s][offset - s:offset - s + ROWS, :]


def _conv_b_block(ext_ref, base, cw_ref):
    win = _shifted(ext_ref[pl.ds(base, ROWS + HALO_B), :])
    acc = jnp.zeros((ROWS, D_B), F32)
    for k in range(KB):
        acc = acc + _tap(win, 2 + k) * cw_ref[k:k + 1, :]
    return acc, win


def _taps_f(win):
    n = ROWS + HALO_F
    return [pltpu.roll(win, n - 6, 0)[0:ROWS, :], pltpu.roll(win, n - 7, 0)[0:ROWS, :], win[8:n, :]]


def _params(sem, **kw):
    return pltpu.CompilerParams(dimension_semantics=sem, vmem_limit_bytes=VMEM_LIMIT, **kw)


def _resident(shape):
    zeros = (0,) * len(shape)
    return pl.BlockSpec(shape, lambda *_: zeros, pipeline_mode=pl.Buffered(1))


def _full(shape):
    zeros = (0,) * len(shape)
    return pl.BlockSpec(shape, lambda *_: zeros)


def _mesh_pos():
    return lax.axis_index("x"), lax.axis_index("y"), lax.axis_index("c")


def _chip_patterns(x, y):
    return [(x, y), (1 - x, y), (x, 1 - y), (1 - x, 1 - y)]


def _lid(x, y, c):
    return 4 * x + 2 * y + c


def _gather_copy(outs, send_sems, recv_sems, a, k, block, to, src=None):
    blk = outs[a].at[_lid(*block)]
    return pltpu.make_async_remote_copy(
        src_ref=blk if src is None else src, dst_ref=blk,
        send_sem=send_sems.at[a, k], recv_sem=recv_sems.at[a, k], device_id=to, device_id_type=MESH)


def _gather_start(mine, outs, send_sems, recv_sems, local_sems):
    x, y, c = _mesh_pos()
    me, sib = (x, y, c), (x, y, 1 - c)
    for a in range(len(mine)):
        pltpu.make_async_copy(mine[a], outs[a].at[_lid(*me)], local_sems.at[a]).start()
        _gather_copy(outs, send_sems, recv_sems, a, 0, me, sib, src=mine[a]).start()
        for j, chip in enumerate(_chip_patterns(x, y)[1:]):
            _gather_copy(outs, send_sems, recv_sems, a, 1 + j, me, (*chip, c), src=mine[a]).start()


def _gather_finish(mine, outs, send_sems, recv_sems, local_sems):
    x, y, c = _mesh_pos()
    me, sib = (x, y, c), (x, y, 1 - c)
    chips = _chip_patterns(x, y)[1:]
    n = len(mine)
    copy = functools.partial(_gather_copy, outs, send_sems, recv_sems)
    passed = []
    for j, chip in enumerate(chips):
        for a in range(n):
            copy(a, 1 + j, (*chip, c), me).wait_recv()
            cp = copy(a, 4 + j, (*chip, c), sib)
            cp.start()
            passed.append(cp)
    for a in range(n):
        copy(a, 0, sib, me).wait_recv()
        for j, chip in enumerate(chips):
            copy(a, 4 + j, (*chip, 1 - c), me).wait_recv()
        for k in range(4):
            copy(a, k, me, sib, src=mine[a]).wait_send()
        pltpu.make_async_copy(mine[a], outs[a].at[_lid(*me)], local_sems.at[a]).wait()
    for cp in passed:
        cp.wait_send()


def _gather_scratch(n):
    return [pltpu.SemaphoreType.DMA((n, 7)), pltpu.SemaphoreType.DMA((n, 7)), pltpu.SemaphoreType.DMA((n,))]


def all_gather_mixer_weights(w_in, w_out, w_up, w_down, convp):
    srcs = [w_in, w_out, convp]
    n = len(srcs)

    def body(win_ref, wout_ref, convp_ref, wup_ref, wdown_ref,
             gin_ref, gout_ref, gconv_ref, sup_ref, sdown_ref,
             sin_ref, sout_ref, send_sems, recv_sems, local_sems):
        sin_ref[...] = win_ref[...].astype(BF16)
        sout_ref[...] = wout_ref[...].astype(BF16)
        mine = [sin_ref, sout_ref, convp_ref]
        outs = [gin_ref, gout_ref, gconv_ref]
        _gather_start(mine, outs, send_sems, recv_sems, local_sems)
        sup_ref[...] = wup_ref[...].T.astype(BF16)
        sdown_ref[...] = wdown_ref[...].astype(BF16)
        _gather_finish(mine, outs, send_sems, recv_sems, local_sems)

    return pl.pallas_call(
        body, name="all_gather_mixer_weights",
        out_shape=[jax.ShapeDtypeStruct((N_DEV,) + w_in.shape, BF16),
                   jax.ShapeDtypeStruct((N_DEV,) + w_out.shape, BF16),
                   jax.ShapeDtypeStruct((N_DEV,) + convp.shape, F32),
                   jax.ShapeDtypeStruct(w_up.shape[::-1], BF16), jax.ShapeDtypeStruct(w_down.shape, BF16)],
        in_specs=[VMEM] * 5, out_specs=[ANY] * n + [VMEM, VMEM],
        scratch_shapes=[pltpu.VMEM(w_in.shape, BF16), pltpu.VMEM(w_out.shape, BF16)] + _gather_scratch(n),
        compiler_params=pltpu.CompilerParams(vmem_limit_bytes=VMEM_LIMIT),
    )(w_in, w_out, convp, w_up, w_down)


def _chip_copies(p, land, send_sems, recv_sems):
    x, y, c = _mesh_pos()
    return [pltpu.make_async_remote_copy(
        src_ref=p[a].at[k], dst_ref=land[a].at[k], send_sem=send_sems.at[a, k], recv_sem=recv_sems.at[a, k],
        device_id=(px, py, c), device_id_type=MESH)
        for k, (px, py) in enumerate(_chip_patterns(x, y)[1:]) for a in range(len(p))]


def chip_partials(name, g, land, jidx, rb):
    _, r, c = g.shape

    def body(j_ref, g_ref, l_ref, o_ref):
        o_ref[...] = (g_ref[...] + l_ref[...]).astype(BF16)

    return pl.pallas_call(
        body, name=name,
        out_shape=jax.ShapeDtypeStruct((3, r, c), BF16),
        grid_spec=pltpu.PrefetchScalarGridSpec(
            num_scalar_prefetch=1, grid=(3, r // rb),
            in_specs=[pl.BlockSpec((1, rb, c), lambda k, i, j: (j[1 + k], i, 0)),
                      pl.BlockSpec((1, rb, c), lambda k, i, j: (1 + k, i, 0))],
            out_specs=pl.BlockSpec((1, rb, c), lambda k, i, j: (k, i, 0))),
        compiler_params=_params(("arbitrary", "arbitrary")),
    )(jidx, g, land)


def _adamw(w, g, m, v):
    m2 = ADAM_B1 * m + (1.0 - ADAM_B1) * g
    v2 = ADAM_B2 * v + (1.0 - ADAM_B2) * (g * g)
    m_hat = m2 / (1.0 - ADAM_B1 ** ADAM_STEP)
    v_hat = v2 / (1.0 - ADAM_B2 ** ADAM_STEP)
    delta = -ADAM_LR * (m_hat / (jnp.sqrt(v_hat) + ADAM_EPS) + ADAM_WD * w)
    return delta, m2, v2


def reduce_and_adamw(name, g, land, recv, w, m, v, jidx, rb):
    _, r, c = g.shape

    def body(j_ref, g_ref, l_ref, r_ref, w_ref, m_ref, v_ref, go_ref, do_ref, mo_ref, vo_ref):
        grad = (g_ref[0] + l_ref[0]) + r_ref[0].astype(F32) + r_ref[1].astype(F32) + r_ref[2].astype(F32)
        delta, m2, v2 = _adamw(w_ref[...], grad, m_ref[...], v_ref[...])
        go_ref[...] = grad
        do_ref[...] = delta
        mo_ref[...] = m2
        vo_ref[...] = v2

    blk = pl.BlockSpec((rb, c), lambda i, j: (i, 0))
    return pl.pallas_call(
        body, name=name,
        out_shape=[jax.ShapeDtypeStruct((r, c), F32)] * 4,
        grid_spec=pltpu.PrefetchScalarGridSpec(
            num_scalar_prefetch=1, grid=(r // rb,),
            in_specs=[pl.BlockSpec((1, rb, c), lambda i, j: (j[0], i, 0)),
                      pl.BlockSpec((1, rb, c), lambda i, j: (0, i, 0)),
                      pl.BlockSpec((3, rb, c), lambda i, j: (0, i, 0)),
                      blk, blk, blk],
            out_specs=[blk] * 4),
        compiler_params=_params(("arbitrary",)),
    )(jidx, g, land, recv, w, m, v)


def mixer_reduce_adamw(grads, svec, ws, ms, vs):
    n = len(grads)
    shard = [g.shape[1:] for g in grads]

    def body(*refs):
        g = refs[:n]
        sv_ref = refs[n]
        w, m, v = refs[n + 1:2 * n + 1], refs[2 * n + 1:3 * n + 1], refs[3 * n + 1:4 * n + 1]
        outs = refs[4 * n + 1:8 * n + 1]
        sv_slots = refs[8 * n + 1]
        rest = refs[8 * n + 2:]
        own, land, sendb, recvb = rest[:n], rest[n:2 * n], rest[2 * n:3 * n], rest[3 * n:4 * n]
        sv_land, chip_sv, d2d_send, d2d_recv, ici_send, ici_recv, local_sems, sv_sems = rest[4 * n:]
        x, y, c = _mesh_pos()
        sib = (x, y, 1 - c)
        pats = _chip_patterns(x, y)
        q = 2 * x + y

        d2d, local = [], []
        for a in range(n):
            for k, (px, py) in enumerate(pats):
                d2d.append(pltpu.make_async_remote_copy(
                    src_ref=g[a].at[_lid(px, py, 1 - c)], dst_ref=land[a].at[k],
                    send_sem=d2d_send.at[a, k], recv_sem=d2d_recv.at[a, k], device_id=sib, device_id_type=MESH))
                local.append(pltpu.make_async_copy(g[a].at[_lid(px, py, c)], own[a].at[k], local_sems.at[a, k]))
        d2d.append(pltpu.make_async_remote_copy(
            src_ref=sv_ref, dst_ref=sv_land, send_sem=d2d_send.at[n, 0], recv_sem=d2d_recv.at[n, 0],
            device_id=sib, device_id_type=MESH))
        for cp in d2d + local:
            cp.start()
        for cp in local + d2d:
            cp.wait()

        for a in range(n):
            for k in range(3):
                sendb[a][k] = (own[a][1 + k] + land[a][1 + k]).astype(BF16)
        chip_sv[...] = sv_ref[...] + sv_land[...]
        ici = _chip_copies(sendb, recvb, ici_send, ici_recv)
        sv_local = pltpu.make_async_copy(chip_sv, sv_slots.at[q], sv_sems.at[0])
        sv_out = [pltpu.make_async_remote_copy(
            src_ref=chip_sv, dst_ref=sv_slots.at[q], send_sem=sv_sems.at[1 + k], recv_sem=sv_sems.at[4 + k],
            device_id=(px, py, c), device_id_type=MESH) for k, (px, py) in enumerate(pats[1:])]
        for cp in ici + sv_out + [sv_local]:
            cp.start()
        for cp in ici:
            cp.wait()
        for k, (px, py) in enumerate(pats[1:]):
            sv_out[k].wait_send()
            pltpu.make_async_remote_copy(
                src_ref=chip_sv, dst_ref=sv_slots.at[2 * px + py], send_sem=sv_sems.at[1 + k],
                recv_sem=sv_sems.at[4 + k], device_id=(px, py, c), device_id_type=MESH).wait_recv()
        sv_local.wait()

        for a in range(n):
            grad = ((own[a][0] + land[a][0]) + recvb[a][0].astype(F32) + recvb[a][1].astype(F32)
                    + recvb[a][2].astype(F32))
            delta, m2, v2 = _adamw(w[a][...], grad, m[a][...], v[a][...])
            outs[a][...] = grad
            outs[n + a][...] = delta
            outs[2 * n + a][...] = m2
            outs[3 * n + a][...] = v2

    shard_out = [jax.ShapeDtypeStruct(s, F32) for s in shard]
    return pl.pallas_call(
        body, name="mixer_reduce_adamw",
        out_shape=shard_out * 4 + [jax.ShapeDtypeStruct((4,) + svec.shape, F32)],
        in_specs=[ANY] * n + [VMEM] * (1 + 3 * n), out_specs=[VMEM] * (4 * n) + [ANY],
        scratch_shapes=[pltpu.VMEM((4,) + s, F32) for s in shard] + [pltpu.VMEM((4,) + s, F32) for s in shard]
        + [pltpu.VMEM((3,) + s, BF16) for s in shard] + [pltpu.VMEM((3,) + s, BF16) for s in shard]
        + [pltpu.VMEM(svec.shape, F32), pltpu.VMEM(svec.shape, F32),
           pltpu.SemaphoreType.DMA((n + 1, 4)), pltpu.SemaphoreType.DMA((n + 1, 4)),
           pltpu.SemaphoreType.DMA((n, 3)), pltpu.SemaphoreType.DMA((n, 3)),
           pltpu.SemaphoreType.DMA((n, 4)), pltpu.SemaphoreType.DMA((7,))],
        compiler_params=pltpu.CompilerParams(vmem_limit_bytes=VMEM_LIMIT),
    )(*grads, svec, *ws, *ms, *vs)


SMALL_LAYOUT = [
    ("b_in", S_BIN, 16), ("ln_a_g", S_LNAG, 4), ("ln_a_b", S_LNAB, 4), ("w_spatial", S_WS, 512),
    ("b_spatial", S_BS, 4), ("conv_b_b", S_CBB, 4), ("ln_b_g", S_LNBG, 4), ("ln_b_b", S_LNBB, 4),
    ("b_out", S_BOUT, 8), ("ln1_g", S_LN1G, 8), ("ln1_b", S_LN1B, 8), ("conv_f_b", S_CFB, 44),
    ("ln2_g", S_LN2G, 8), ("ln2_b", S_LN2B, 8),
]


def small_adamw(sv_slots, ws, ms, vs):
    n = len(SMALL_LAYOUT)

    def body(*refs):
        s_ref = refs[0]
        w_refs, m_refs, v_refs = refs[1:1 + n], refs[1 + n:1 + 2 * n], refs[1 + 2 * n:1 + 3 * n]
        outs = refs[1 + 3 * n:]
        for p, (_, row0, rows) in enumerate(SMALL_LAYOUT):
            sl = pl.ds(row0, rows)
            grad = ((s_ref[0, sl, :] + s_ref[1, sl, :]) + s_ref[2, sl, :]) + s_ref[3, sl, :]
            delta, m2, v2 = _adamw(w_refs[p][...], grad, m_refs[p][...], v_refs[p][...])
            outs[p][...] = grad
            outs[n + p][...] = delta
            outs[2 * n + p][...] = m2
            outs[3 * n + p][...] = v2
        sl = pl.ds(S_LOSS, 8)
        outs[4 * n][...] = ((s_ref[0, sl, :] + s_ref[1, sl, :]) + s_ref[2, sl, :]) + s_ref[3, sl, :]

    shapes = [jax.ShapeDtypeStruct((rows, 128), F32) for _, _, rows in SMALL_LAYOUT]
    return pl.pallas_call(
        body, name="small_adamw", out_shape=shapes * 4 + [jax.ShapeDtypeStruct((8, 128), F32)],
        in_specs=[VMEM] * (1 + 3 * n), out_specs=[VMEM] * (4 * n + 1),
    )(sv_slots, *ws, *ms, *vs)


def mix_forward(x, win_g, b_in, ln_a_g, ln_a_b, w_spatial, bst, conv_b_w, conv_b_b, ln_b_g, ln_b_b,
                wout, b_out, ln1_g, ln1_b, sup, sdown, tm):
    t = x.shape[0]
    nt = t // tm
    n_chunks = tm // CHUNK

    def body(x_ref, win_ref, bin_ref, ga_ref, ba_ref, ws_ref, bst_ref, cw_ref, cb_ref, gb_ref,
             bb_ref, wout_ref, bout_ref, g1_ref, b1_ref, sup_ref, sdown_ref,
             h_ref, xhat1_ref, rstd1_ref, yb1_ref, gup_ref, gdown_ref,
             ext_ref, y_ref, wsm_ref, send_sems, recv_sems, local_sems):
        i = pl.program_id(0)
        gather = ([sup_ref, sdown_ref], [gup_ref, gdown_ref], send_sems, recv_sems, local_sems)

        @pl.when(i == 0)
        def _():
            _gather_start(*gather)
            ext_ref[0:HALO_B, :] = jnp.zeros((HALO_B, D_B), F32)
            mask = _tril_mask()
            for hd in range(HEADS):
                wsm_ref[hd] = jnp.where(mask, ws_ref[hd], 0.0).astype(BF16)

        xb = x_ref[...].astype(BF16)
        for j in range(N_DEV):
            cols = slice(j * W_IN_BLK, (j + 1) * W_IN_BLK)
            h_ref[:, cols] = _nn(xb, win_ref[j]) + bin_ref[:, cols]

        def chunk(ci):
            r = _rows(ci, CHUNK)
            u, _, _, _, _, _, svs = _mixer_a_fwd(h_ref[r, 0:D_A], h_ref[r, D_A:2 * D_A],
                                                 ga_ref, ba_ref, wsm_ref, bst_ref)
            for hd in range(HEADS):
                sl = slice(hd * HEAD_DIM, (hd + 1) * HEAD_DIM)
                y_ref[r, sl] = (u[:, sl] * svs[hd]).astype(BF16)
            a_b = h_ref[r, 2 * D_A:2 * D_A + D_B]
            g_b = h_ref[r, 2 * D_A + D_B:D_IN]
            ext_ref[pl.ds(HALO_B + ci * CHUNK, CHUNK), :] = a_b * _sigmoid(g_b)

        _loop(n_chunks, chunk)

        def conv_rows(bi):
            base = bi * ROWS
            acc, _ = _conv_b_block(ext_ref, base, cw_ref)
            yb1 = acc + cb_ref[...]
            yb1_ref[pl.ds(base, ROWS), :] = yb1
            xhat, _ = _ln_stats(yb1)
            yb2 = xhat * gb_ref[...] + bb_ref[...]
            y_ref[pl.ds(base, ROWS), D_A:D] = (yb2 * _sigmoid(yb2)).astype(BF16)

        _loop(tm // ROWS, conv_rows)
        ext_ref[0:HALO_B, :] = ext_ref[tm:tm + HALO_B, :]

        mix = _nn(y_ref[...], wout_ref[...]) + bout_ref[...]
        xhat1, rstd1 = _ln_stats(ALPHA * x_ref[...] + mix)
        xhat1_ref[...] = xhat1
        rstd1_ref[...] = jnp.broadcast_to(rstd1, (tm, 128))

        @pl.when(i == nt - 1)
        def _():
            _gather_finish(*gather)

    row = lambda w: pl.BlockSpec((tm, w), lambda i: (i, 0))
    return pl.pallas_call(
        body, name="mix_forward", grid=(nt,),
        in_specs=[row(D), _resident(win_g.shape), _full(b_in.shape), _full(ln_a_g.shape),
                  _full(ln_a_b.shape), _full(w_spatial.shape), _full(bst.shape),
                  _full(conv_b_w.shape), _full(conv_b_b.shape), _full(ln_b_g.shape),
                  _full(ln_b_b.shape), _resident(wout.shape), _full(b_out.shape),
                  _full(ln1_g.shape), _full(ln1_b.shape), ANY, ANY],
        out_specs=[row(D_IN), row(D), row(128), row(D_B), ANY, ANY],
        out_shape=[jax.ShapeDtypeStruct((t, D_IN), F32), jax.ShapeDtypeStruct((t, D), F32),
                   jax.ShapeDtypeStruct((t, 128), F32), jax.ShapeDtypeStruct((t, D_B), F32),
                   jax.ShapeDtypeStruct((N_DEV,) + sup.shape, BF16),
                   jax.ShapeDtypeStruct((N_DEV,) + sdown.shape, BF16)],
        scratch_shapes=[pltpu.VMEM((tm + HALO_B, D_B), F32), pltpu.VMEM((tm, D), BF16),
                        pltpu.VMEM((HEADS, CHUNK, CHUNK), BF16)] + _gather_scratch(2),
        compiler_params=_params(("arbitrary",)),
    )(x, win_g, b_in, ln_a_g, ln_a_b, w_spatial, bst, conv_b_w, conv_b_b, ln_b_g, ln_b_b,
      wout, b_out, ln1_g, ln1_b, sup, sdown)


def ffn_forward(xhat1, ln1_g, ln1_b, wup_g, cfw, cfb, wdown, ln2_g, ln2_b, target, tm):
    t = xhat1.shape[0]
    nt = t // tm

    def body(xh_ref, g1_ref, b1_ref, wup_ref, cfw_ref, cfb_ref, wdown_ref, g2_ref, b2_ref, tgt_ref,
             hu_ref, gv_ref, dr2_ref, loss_ref, sln2_ref,
             x1_ref, x1b_ref, carry_ref, gbuf_ref, ffn_ref, acc_loss, acc_g2, acc_b2):
        i = pl.program_id(0)

        @pl.when(i == 0)
        def _():
            carry_ref[...] = jnp.zeros(carry_ref.shape, F32)
            acc_loss[...] = jnp.zeros(acc_loss.shape, F32)
            acc_g2[...] = jnp.zeros(acc_g2.shape, F32)
            acc_b2[...] = jnp.zeros(acc_b2.shape, F32)

        x1 = xh_ref[...] * g1_ref[...] + b1_ref[...]
        x1_ref[...] = x1
        x1b_ref[...] = x1.astype(BF16)

        def conv(j, base):
            if base == 0:
                win = jnp.concatenate([carry_ref[j], hu_ref[j, 0:ROWS, :]], axis=0)
            else:
                win = hu_ref[j, base - HALO_F:base + ROWS, :]
            taps = _taps_f(win)
            w = cfw_ref[j]
            return sum(taps[k] * w[k:k + 1, :] for k in range(KF)) + cfb_ref[j:j + 1, :]

        for f in range(N_F):
            hu_ref[f] = _nn(x1b_ref[...], wup_ref[f])
            hu_ref[N_F + f] = _nn(x1b_ref[...], wup_ref[N_F + f])

            def rows(bi, f=f):
                gate = conv(f, bi * ROWS)
                val = conv(N_F + f, bi * ROWS)
                gbuf_ref[_rows(bi), :] = (gate * _sigmoid(gate) * val).astype(BF16)
                gv_ref[f, _rows(bi), :] = gate.astype(BF16)
                gv_ref[N_F + f, _rows(bi), :] = val.astype(BF16)

            _loop(tm // ROWS, rows)
            carry_ref[f] = hu_ref[f, tm - HALO_F:tm, :]
            carry_ref[N_F + f] = hu_ref[N_F + f, tm - HALO_F:tm, :]
            part = _nn(gbuf_ref[...], wdown_ref[f])
            if f == 0:
                ffn_ref[...] = part
            else:
                ffn_ref[...] += part

        def tail(bi):
            r = _rows(bi)
            xhat2, rstd2 = _ln_stats(ALPHA * x1_ref[r, :] + ffn_ref[r, :])
            err = xhat2 * g2_ref[...] + b2_ref[...] - tgt_ref[r, :]
            e2 = _rsum8(err * err)
            acc_loss[...] += sum(e2[:, k * 128:(k + 1) * 128] for k in range(D // 128))
            dy = err * (1.0 / D)
            acc_g2[...] += _rsum8(dy * xhat2)
            acc_b2[...] += _rsum8(dy)
            dr2_ref[r, :] = _ln_bwd(dy * g2_ref[...], xhat2, rstd2)

        _loop(tm // ROWS, tail)
        loss_ref[...] = acc_loss[...]

        @pl.when(i == nt - 1)
        def _():
            dg = jnp.sum(acc_g2[...], axis=0, keepdims=True)
            db = jnp.sum(acc_b2[...], axis=0, keepdims=True)
            for k in range(D // 128):
                sln2_ref[k:k + 1, :] = dg[:, k * 128:(k + 1) * 128]
                sln2_ref[8 + k:9 + k, :] = db[:, k * 128:(k + 1) * 128]

    row = pl.BlockSpec((tm, D), lambda i: (i, 0))
    return pl.pallas_call(
        body, name="ffn_forward", grid=(nt,),
        in_specs=[row, _full(ln1_g.shape), _full(ln1_b.shape), _resident(wup_g.shape),
                  _full(cfw.shape), _full(cfb.shape), _resident(wdown.shape),
                  _full(ln2_g.shape), _full(ln2_b.shape), row],
        out_specs=[pl.BlockSpec((N_DEV, tm, W_UP_BLK), lambda i: (0, i, 0)),
                   pl.BlockSpec((N_DEV, tm, W_UP_BLK), lambda i: (0, i, 0)), row,
                   _full((8, 128)), _full((16, 128))],
        out_shape=[jax.ShapeDtypeStruct((N_DEV, t, W_UP_BLK), F32),
                   jax.ShapeDtypeStruct((N_DEV, t, W_UP_BLK), BF16), jax.ShapeDtypeStruct((t, D), F32),
                   jax.ShapeDtypeStruct((8, 128), F32), jax.ShapeDtypeStruct((16, 128), F32)],
        scratch_shapes=[pltpu.VMEM((tm, D), F32), pltpu.VMEM((tm, D), BF16),
                        pltpu.VMEM((N_DEV, HALO_F, W_UP_BLK), F32), pltpu.VMEM((tm, W_UP_BLK), BF16),
                        pltpu.VMEM((tm, D), F32), pltpu.VMEM((8, 128), F32),
                        pltpu.VMEM((8, D), F32), pltpu.VMEM((8, D), F32)],
        compiler_params=_params(("arbitrary",)),
    )(xhat1, ln1_g, ln1_b, wup_g, cfw, cfb, wdown, ln2_g, ln2_b, target)


def ffn_backward(order, dr2, xhat1, ln1_g, ln1_b, hu, gv, wup_g, cfw, wdown, tm):
    t = dr2.shape[0]
    nt = t // tm
    sub_rows = tm
    hu4 = hu.reshape(2, N_F, t, W_UP_BLK)
    gv4 = gv.reshape(2, N_F, t, W_UP_BLK)
    wup4 = wup_g.reshape(2, N_F, D, W_UP_BLK)
    cfw4 = cfw.reshape(2, N_F, KF, W_UP_BLK)

    def body(order_ref, dr2_ref, xh_ref, g1_ref, b1_ref, hu_ref, gv_ref, wup_ref, cfw_ref, wdown_ref,
             dwup_ref, dwdown_ref, dcfw_ref, dcfb_ref, dx1_ref, land_up_ref, land_down_ref,
             x1b_ref, drb_ref, dg_ref, dextg_ref, dextv_ref, gbuf_ref,
             dhug_ref, dhuv_ref, acc_wup, acc_wdown, acc_cfw, acc_cfb, sem, send_sems, recv_sems):
        f = order_ref[pl.program_id(0)]
        i = pl.program_id(1)
        x, y, c = _mesh_pos()
        half = D_FF // N_DEV

        def to_sibling(fi, k, src, land_ref, shard_chip):
            d = jnp.bitwise_xor(shard_chip, 2 * x + y)
            slot = jnp.where(d == 1, 2, jnp.where(d == 2, 1, d))
            return pltpu.make_async_remote_copy(
                src_ref=src, dst_ref=land_ref.at[slot], send_sem=send_sems.at[fi, k], recv_sem=recv_sems.at[fi, k],
                device_id=(x, y, 1 - c), device_id_type=MESH)

        def up_copy(fi, g):
            return to_sibling(fi, g, dwup_ref.at[g, fi], land_up_ref, 2 * g + fi // 2)

        def down_copy(fi):
            return to_sibling(fi, 2, dwdown_ref.at[fi, pl.ds((1 - c) * half, half)], land_down_ref, fi)

        @pl.when(i == 0)
        def _():
            acc_wup[...] = jnp.zeros(acc_wup.shape, F32)
            acc_wdown[...] = jnp.zeros(acc_wdown.shape, F32)
            acc_cfw[...] = jnp.zeros(acc_cfw.shape, F32)
            acc_cfb[...] = jnp.zeros(acc_cfb.shape, F32)
            dextg_ref[tm:tm + HALO_F, :] = jnp.zeros((HALO_F, W_UP_BLK), F32)
            dextv_ref[tm:tm + HALO_F, :] = jnp.zeros((HALO_F, W_UP_BLK), F32)

        w = [cfw_ref[0, 0], cfw_ref[1, 0]]
        dext = [dextg_ref, dextv_ref]
        dhu = [dhug_ref, dhuv_ref]

        def rows1(bi):
            r = _rows(bi)
            gate = gv_ref[0, 0, r, :].astype(F32)
            val = gv_ref[1, 0, r, :].astype(F32)
            sg = _sigmoid(gate)
            silu = gate * sg
            gbuf_ref[r, :] = (silu * val).astype(BF16)
            dg = dg_ref[r, :]
            dgate = dg * val * (sg * (1.0 + gate * (1.0 - sg)))
            dval = dg * silu
            dextg_ref[r, :] = dgate
            dextv_ref[r, :] = dval
            acc_cfb[0:8, :] += _rsum8(dgate)
            acc_cfb[8:16, :] += _rsum8(dval)

        def rows2(bi):
            r = _rows(bi)
            for g in range(2):
                win = dext[g][pl.ds(bi * ROWS, ROWS + HALO_F), :]
                n = ROWS + HALO_F
                later = [pltpu.roll(win, n - 2, 0)[0:ROWS, :], pltpu.roll(win, n - 1, 0)[0:ROWS, :],
                         win[0:ROWS, :]]
                d = sum(later[k] * w[g][k:k + 1, :] for k in range(KF))
                dhu[g][r, :] = d.astype(BF16)
                pre = hu_ref[g, 0, r, :]
                for k in range(KF):
                    r0 = 8 * (g * KF + k)
                    acc_cfw[r0:r0 + 8, :] += _rsum8(later[k] * pre)

        for sub in reversed(range(tm // sub_rows)):
            rs = slice(sub * sub_rows, (sub + 1) * sub_rows)
            blocks = range(sub * sub_rows // ROWS, (sub + 1) * sub_rows // ROWS)
            x1b_ref[rs, :] = (xh_ref[rs, :] * g1_ref[...] + b1_ref[...]).astype(BF16)
            drb_ref[rs, :] = dr2_ref[rs, :].astype(BF16)
            dg_ref[rs, :] = _nt(drb_ref[rs, :], wdown_ref[0])
            for bi in blocks:
                rows1(bi)
            for bi in blocks:
                rows2(bi)
            acc_wdown[...] += _tn(gbuf_ref[rs, :], drb_ref[rs, :])
            acc_wup[0] += _tn(dhug_ref[rs, :], x1b_ref[rs, :])
            acc_wup[1] += _tn(dhuv_ref[rs, :], x1b_ref[rs, :])
            dx1_ref[0, rs, :] = (_nt(dhug_ref[rs, :], wup_ref[0, 0])
                                 + _nt(dhuv_ref[rs, :], wup_ref[1, 0])).astype(BF16)
        dextg_ref[tm:tm + HALO_F, :] = dextg_ref[0:HALO_F, :]
        dextv_ref[tm:tm + HALO_F, :] = dextv_ref[0:HALO_F, :]

        @pl.when(i == nt - 1)
        def _():
            for g in range(2):
                dcfb_ref[g, 0] = jnp.sum(acc_cfb[8 * g:8 * g + 8, :], axis=0, keepdims=True)
                for k in range(KF):
                    r0 = 8 * (g * KF + k)
                    dcfw_ref[g, 0, k:k + 1, :] = jnp.sum(acc_cfw[r0:r0 + 8, :], axis=0, keepdims=True)
            cps = [pltpu.make_async_copy(acc_wup.at[0], dwup_ref.at[0, f], sem.at[0]),
                   pltpu.make_async_copy(acc_wup.at[1], dwup_ref.at[1, f], sem.at[1]),
                   pltpu.make_async_copy(acc_wdown, dwdown_ref.at[f], sem.at[2])]
            for cp in cps:
                cp.start()
            for cp in cps:
                cp.wait()
            down_copy(f).start()

            @pl.when(f % 2 != c)
            def _():
                up_copy(f, 0).start()
                up_copy(f, 1).start()

        @pl.when((i == nt - 1) & (pl.program_id(0) == N_F - 1))
        def _():
            for fi in range(N_F):
                down_copy(fi).wait()
                for g in range(2):
                    @pl.when(fi % 2 != c)
                    def _():
                        up_copy(fi, g).wait_send()

                    @pl.when(fi % 2 == c)
                    def _():
                        up_copy(fi, g).wait_recv()

    rev = lambda i: nt - 1 - i
    row = pl.BlockSpec((tm, D), lambda fo, i, o: (rev(i), 0))
    pair = lambda r, c: pl.BlockSpec((2, 1, r, c), lambda fo, i, o: (0, o[fo], 0, 0))
    tile = pl.BlockSpec((2, 1, tm, W_UP_BLK), lambda fo, i, o: (0, o[fo], rev(i), 0))
    return pl.pallas_call(
        body, name="ffn_backward",
        grid_spec=pltpu.PrefetchScalarGridSpec(
            num_scalar_prefetch=1, grid=(N_F, nt),
            in_specs=[row, row, _full(ln1_g.shape), _full(ln1_b.shape), tile, tile,
                      pair(D, W_UP_BLK), pair(KF, W_UP_BLK),
                      pl.BlockSpec((1, W_UP_BLK, D), lambda fo, i, o: (o[fo], 0, 0))],
            out_specs=[ANY, ANY, pair(KF, W_UP_BLK), pair(1, W_UP_BLK),
                       pl.BlockSpec((1, tm, D), lambda fo, i, o: (o[fo], rev(i), 0)), ANY, ANY],
            scratch_shapes=[pltpu.VMEM((tm, D), BF16), pltpu.VMEM((tm, D), BF16),
                            pltpu.VMEM((tm, W_UP_BLK), F32),
                            pltpu.VMEM((tm + HALO_F, W_UP_BLK), F32), pltpu.VMEM((tm + HALO_F, W_UP_BLK), F32),
                            pltpu.VMEM((tm, W_UP_BLK), BF16), pltpu.VMEM((tm, W_UP_BLK), BF16),
                            pltpu.VMEM((tm, W_UP_BLK), BF16),
                            pltpu.VMEM((2, W_UP_BLK, D), F32), pltpu.VMEM((W_UP_BLK, D), F32),
                            pltpu.VMEM((2 * KF * 8, W_UP_BLK), F32), pltpu.VMEM((16, W_UP_BLK), F32),
                            pltpu.SemaphoreType.DMA((3,)),
                            pltpu.SemaphoreType.DMA((N_F, 3)), pltpu.SemaphoreType.DMA((N_F, 3))]),
        out_shape=[jax.ShapeDtypeStruct((2, N_F, W_UP_BLK, D), F32),
                   jax.ShapeDtypeStruct((N_F, W_UP_BLK, D), F32),
                   jax.ShapeDtypeStruct((2, N_F, KF, W_UP_BLK), F32),
                   jax.ShapeDtypeStruct((2, N_F, 1, W_UP_BLK), F32),
                   jax.ShapeDtypeStruct((N_F, t, D), BF16),
                   jax.ShapeDtypeStruct((4, W_UP_BLK, D), F32),
                   jax.ShapeDtypeStruct((4, D_FF // N_DEV, D), F32)],
        compiler_params=_params(("arbitrary", "arbitrary")),
    )(order, dr2, xhat1, ln1_g, ln1_b, hu4, gv4, wup4, cfw4, wdown)


def mix_backward(x, h, yb1, dx1p, dr2, xhat1, rstd1, win_g, ln_a_g, ln_a_b, w_spatial, bst,
                 conv_b_w, ln_b_g, ln_b_b, wout, ln1_g, ffn_partials, tm):
    t = x.shape[0]
    n_p = len(ffn_partials)
    nt = t // tm
    n_chunks = tm // CHUNK
    halo_blocks = tm // HALO_B

    def body(x_ref, h_ref, halo_ref, yb1_ref, dx1p_ref, dr2_ref, xh1_ref, rstd1_ref, win_ref, ga_ref, ba_ref,
             ws_ref, bst_ref, cw_ref, gb_ref, bb_ref, wout_ref, g1_ref, *rest):
        p_refs, rest = rest[:n_p], rest[n_p:]
        gx_ref, dwin_ref, dwout_ref, dcw_ref, small_ref = rest[:5]
        land_refs, rest = rest[5:5 + n_p], rest[5 + n_p:]
        (ext_ref, dext_ref, y_ref, dy_ref, dh_ref, dmb_ref, wsm_ref,
         acc_win, acc_wout, acc_bin, acc_lnag, acc_lnab, acc_ws, acc_bs, acc_cbb, acc_lnbg,
         acc_lnbb, acc_bout, acc_ln1g, acc_ln1b, acc_cw, sem, send_sems, recv_sems) = rest
        i = pl.program_id(0)

        @pl.when(i == 0)
        def _():
            for cp in _chip_copies(p_refs, land_refs, send_sems, recv_sems):
                cp.start()

        first_tile = i == nt - 1
        accs = [acc_win, acc_wout, acc_bin, acc_lnag, acc_lnab, acc_ws, acc_bs, acc_cbb, acc_lnbg,
                acc_lnbb, acc_bout, acc_ln1g, acc_ln1b, acc_cw]

        @pl.when(i == 0)
        def _():
            for acc in accs:
                acc[...] = jnp.zeros(acc.shape, F32)
            dext_ref[tm:tm + HALO_B, :] = jnp.zeros((HALO_B, D_B), F32)
            mask = _tril_mask()
            for hd in range(HEADS):
                wsm_ref[hd] = jnp.where(mask, ws_ref[hd], 0.0).astype(BF16)

        def ln1_rows(bi):
            r = _rows(bi)
            part = [dx1p_ref[f, r, :].astype(F32) for f in range(N_F)]
            dx1 = ALPHA * dr2_ref[r, :] + ((part[0] + part[1]) + (part[2] + part[3]))
            xhat = xh1_ref[r, :]
            acc_ln1g[...] += _rsum8(dx1 * xhat)
            acc_ln1b[...] += _rsum8(dx1)
            dr1 = _ln_bwd(dx1 * g1_ref[...], xhat, rstd1_ref[r, 0:1])
            acc_bout[...] += _rsum8(dr1)
            gx_ref[r, :] = ALPHA * dr1
            dmb_ref[r, :] = dr1.astype(BF16)

        _loop(tm // ROWS, ln1_rows)
        dy_ref[...] = _nt(dmb_ref[...], wout_ref[...])

        ha = halo_ref[:, 0:D_B]
        hg = halo_ref[:, D_B:2 * D_B]
        ext_ref[0:HALO_B, :] = jnp.where(first_tile, 0.0, 1.0) * (ha * _sigmoid(hg))

        def chunk(ci):
            r = _rows(ci, CHUNK)
            hu, hv = h_ref[r, 0:D_A], h_ref[r, D_A:2 * D_A]
            u, cdf_u, cdf_v, xhats, rstds, vns, svs = _mixer_a_fwd(hu, hv, ga_ref, ba_ref, wsm_ref, bst_ref)
            for hd in range(HEADS):
                sl = slice(hd * HEAD_DIM, (hd + 1) * HEAD_DIM)
                rows8 = slice(8 * hd, 8 * hd + 8)
                dy_a = dy_ref[r, sl]
                y_ref[r, sl] = (u[:, sl] * svs[hd]).astype(BF16)
                du = dy_a * svs[hd]
                dsv = dy_a * u[:, sl]
                dsvb = dsv.astype(BF16)
                acc_bs[hd] += dsv
                acc_ws[hd] += _nt(dsvb, vns[hd])
                dvn = _tn(wsm_ref[hd], dsvb)
                acc_lnag[rows8, :] += _rsum8(dvn * xhats[hd])
                acc_lnab[rows8, :] += _rsum8(dvn)
                dv = _ln_bwd(dvn * ga_ref[hd:hd + 1, :], xhats[hd], rstds[hd])
                hus, hvs = hu[:, sl], hv[:, sl]
                slv = slice(D_A + hd * HEAD_DIM, D_A + (hd + 1) * HEAD_DIM)
                dhu = du * (cdf_u[:, sl] + hus * jnp.exp(-0.5 * hus * hus) * INV_SQRT_2PI)
                dhv = dv * (cdf_v[:, sl] + hvs * jnp.exp(-0.5 * hvs * hvs) * INV_SQRT_2PI)
                acc_bin[:, sl] += _rsum8(dhu)
                acc_bin[:, slv] += _rsum8(dhv)
                dh_ref[r, sl] = dhu.astype(BF16)
                dh_ref[r, slv] = dhv.astype(BF16)
            a_b = h_ref[r, 2 * D_A:2 * D_A + D_B]
            g_b = h_ref[r, 2 * D_A + D_B:D_IN]
            ext_ref[pl.ds(HALO_B + ci * CHUNK, CHUNK), :] = a_b * _sigmoid(g_b)

        _loop(n_chunks, chunk)

        def conv_rows(bi):
            base = bi * ROWS
            r = pl.ds(base, ROWS)
            win = _shifted(ext_ref[pl.ds(base, ROWS + HALO_B), :])
            xhat, rstd = _ln_stats(yb1_ref[r, :])
            yb2 = xhat * gb_ref[...] + bb_ref[...]
            sg = _sigmoid(yb2)
            y_ref[r, D_A:D] = (yb2 * sg).astype(BF16)
            dyb2 = dy_ref[r, D_A:D] * (sg * (1.0 + yb2 * (1.0 - sg)))
            acc_lnbg[...] += _rsum8(dyb2 * xhat)
            acc_lnbb[...] += _rsum8(dyb2)
            dyb1 = _ln_bwd(dyb2 * gb_ref[...], xhat, rstd)
            acc_cbb[...] += _rsum8(dyb1)
            dext_ref[r, :] = dyb1
            for k in range(KB):
                acc_cw[8 * k:8 * k + 8, :] += _rsum8(dyb1 * _tap(win, 2 + k))

        _loop(tm // ROWS, conv_rows)

        def convt_rows(bi):
            base = bi * ROWS
            r = pl.ds(base, ROWS)
            dwin = _shifted(dext_ref[pl.ds(base, ROWS + HALO_B), :])
            dyb0 = jnp.zeros((ROWS, D_B), F32)
            for k in range(KB):
                dyb0 = dyb0 + _tap(dwin, 30 - k) * cw_ref[k:k + 1, :]
            a_b = h_ref[r, 2 * D_A:2 * D_A + D_B]
            sg = _sigmoid(h_ref[r, 2 * D_A + D_B:D_IN])
            da_b = dyb0 * sg
            dg_b = dyb0 * a_b * sg * (1.0 - sg)
            acc_bin[:, 2 * D_A:2 * D_A + D_B] += _rsum8(da_b)
            acc_bin[:, 2 * D_A + D_B:D_IN] += _rsum8(dg_b)
            dh_ref[r, 2 * D_A:2 * D_A + D_B] = da_b.astype(BF16)
            dh_ref[r, 2 * D_A + D_B:D_IN] = dg_b.astype(BF16)

        _loop(tm // ROWS, convt_rows)
        dext_ref[tm:tm + HALO_B, :] = dext_ref[0:HALO_B, :]

        acc_wout[...] += _tn(y_ref[...], dmb_ref[...])
        xt = x_ref[...].T.astype(BF16)
        dh_blocks = [dh_ref[:, j * W_IN_BLK:(j + 1) * W_IN_BLK] for j in range(N_DEV)]
        for j in range(N_DEV):
            acc_win[j] += _nn(xt, dh_blocks[j])
        gx_ref[...] += sum(_nt(dh_blocks[j], win_ref[j]) for j in range(N_DEV))

        @pl.when(i == nt - 1)
        def _():
            cps = [pltpu.make_async_copy(acc_win, dwin_ref, sem.at[0]),
                   pltpu.make_async_copy(acc_wout, dwout_ref, sem.at[1])]
            for cp in cps:
                cp.start()
            small_ref[...] = jnp.zeros(small_ref.shape, F32)

            def put_row_vector(row0, acc):
                vec = jnp.sum(acc[...], axis=0, keepdims=True)
                for k in range(vec.shape[1] // 128):
                    small_ref[row0 + k:row0 + k + 1, :] = vec[:, k * 128:(k + 1) * 128]

            put_row_vector(S_BIN, acc_bin)
            put_row_vector(S_CBB, acc_cbb)
            put_row_vector(S_LNBG, acc_lnbg)
            put_row_vector(S_LNBB, acc_lnbb)
            put_row_vector(S_BOUT, acc_bout)
            put_row_vector(S_LN1G, acc_ln1g)
            put_row_vector(S_LN1B, acc_ln1b)
            mask = _tril_mask()
            for hd in range(HEADS):
                rows8 = slice(8 * hd, 8 * hd + 8)
                small_ref[S_LNAG + hd:S_LNAG + hd + 1, :] = jnp.sum(acc_lnag[rows8, :], axis=0, keepdims=True)
                small_ref[S_LNAB + hd:S_LNAB + hd + 1, :] = jnp.sum(acc_lnab[rows8, :], axis=0, keepdims=True)
                small_ref[S_WS + hd * CHUNK:S_WS + (hd + 1) * CHUNK, :] = jnp.where(mask, acc_ws[hd], 0.0)
                small_ref[S_BS + hd:S_BS + hd + 1, :] = jnp.sum(acc_bs[hd].T, axis=0, keepdims=True)
            for k in range(KB):
                dcw_ref[k:k + 1, :] = jnp.sum(acc_cw[8 * k:8 * k + 8, :], axis=0, keepdims=True)
            for cp in cps:
                cp.wait()
            for cp in _chip_copies(p_refs, land_refs, send_sems, recv_sems):
                cp.wait()

    rev = lambda i: nt - 1 - i
    row = lambda w: pl.BlockSpec((tm, w), lambda i: (rev(i), 0))
    return pl.pallas_call(
        body, name="mix_backward", grid=(nt,),
        in_specs=[row(D), row(D_IN),
                  pl.BlockSpec((HALO_B, 2 * D_B), lambda i: (jnp.maximum(rev(i) * halo_blocks - 1, 0), 1)),
                  row(D_B), pl.BlockSpec((N_F, tm, D), lambda i: (0, rev(i), 0)),
                  row(D), row(D), row(128), _resident(win_g.shape), _full(ln_a_g.shape),
                  _full(ln_a_b.shape), _full(w_spatial.shape), _full(bst.shape), _full(conv_b_w.shape),
                  _full(ln_b_g.shape), _full(ln_b_b.shape),
                  _resident(wout.shape), _full(ln1_g.shape)] + [ANY] * n_p,
        out_specs=[row(D), ANY, ANY, _full((KB, D_B)), _full((S_MIX_ROWS, 128))] + [ANY] * n_p,
        out_shape=[jax.ShapeDtypeStruct((t, D), F32), jax.ShapeDtypeStruct((N_DEV, D, W_IN_BLK), F32),
                   jax.ShapeDtypeStruct((D, D), F32), jax.ShapeDtypeStruct((KB, D_B), F32),
                   jax.ShapeDtypeStruct((S_MIX_ROWS, 128), F32)]
        + [jax.ShapeDtypeStruct(p.shape, BF16) for p in ffn_partials],
        scratch_shapes=[pltpu.VMEM((tm + HALO_B, D_B), F32), pltpu.VMEM((tm + HALO_B, D_B), F32),
                        pltpu.VMEM((tm, D), BF16), pltpu.VMEM((tm, D), F32), pltpu.VMEM((tm, D_IN), BF16),
                        pltpu.VMEM((tm, D), BF16),
                        pltpu.VMEM((HEADS, CHUNK, CHUNK), BF16),
                        pltpu.VMEM((N_DEV, D, W_IN_BLK), F32), pltpu.VMEM((D, D), F32),
                        pltpu.VMEM((8, D_IN), F32), pltpu.VMEM((8 * HEADS, HEAD_DIM), F32),
                        pltpu.VMEM((8 * HEADS, HEAD_DIM), F32), pltpu.VMEM((HEADS, CHUNK, CHUNK), F32),
                        pltpu.VMEM((HEADS, CHUNK, CHUNK), F32), pltpu.VMEM((8, D_B), F32),
                        pltpu.VMEM((8, D_B), F32), pltpu.VMEM((8, D_B), F32), pltpu.VMEM((8, D), F32),
                        pltpu.VMEM((8, D), F32), pltpu.VMEM((8, D), F32), pltpu.VMEM((8 * KB, D_B), F32),
                        pltpu.SemaphoreType.DMA((2,)),
                        pltpu.SemaphoreType.DMA((n_p, 3)), pltpu.SemaphoreType.DMA((n_p, 3))],
        compiler_params=_params(("arbitrary",)),
    )(x, h, h, yb1, dx1p, dr2, xhat1, rstd1, win_g, ln_a_g, ln_a_b, w_spatial, bst, conv_b_w,
      ln_b_g, ln_b_b, wout, ln1_g, *ffn_partials)


def _rows128(a):
    return a.reshape(-1, 128)


def _pack_conv(cb, cf):
    lead = cb.shape[:-2]
    pad = [(0, 0)] * len(lead)
    flat = jnp.pad(cb.reshape(lead + (KB * 64,)), pad + [(0, 3 * W_UP_BLK - KB * 64)])
    rows = jnp.concatenate([cf, flat.reshape(lead + (3, W_UP_BLK))], axis=-2)
    return jnp.pad(rows, pad + [(0, 2), (0, 768 - W_UP_BLK)])


def _unpack_conv(p):
    lead = p.shape[:-2]
    cf = p[..., 0:KF, 0:W_UP_BLK]
    cb = p[..., 3:6, 0:W_UP_BLK].reshape(lead + (3 * W_UP_BLK,))[..., :KB * 64].reshape(lead + (KB, 64))
    return cb, cf


def kernel(x, w_in, b_in, ln_a_g, ln_a_b, w_spatial, b_spatial, conv_b_w, conv_b_b, ln_b_g, ln_b_b, w_out, b_out, ln1_g, ln1_b, w_up, conv_f_w, conv_f_b, w_down, ln2_g, ln2_b, loss_target, m_w_in, m_b_in, m_ln_a_g, m_ln_a_b, m_w_spatial, m_b_spatial, m_conv_b_w, m_conv_b_b, m_ln_b_g, m_ln_b_b, m_w_out, m_b_out, m_ln1_g, m_ln1_b, m_w_up, m_conv_f_w, m_conv_f_b, m_w_down, m_ln2_g, m_ln2_b, v_w_in, v_b_in, v_ln_a_g, v_ln_a_b, v_w_spatial, v_b_spatial, v_conv_b_w, v_conv_b_b, v_ln_b_g, v_ln_b_b, v_w_out, v_b_out, v_ln1_g, v_ln1_b, v_w_up, v_conv_f_w, v_conv_f_b, v_w_down, v_ln2_g, v_ln2_b):
    t = x.shape[1]
    x2 = x.reshape(t, D)
    target = loss_target.reshape(t, D)
    tm_fwd = min(t, 512)
    tm_bwd = min(t, 256)
    tm_ffn_bwd = min(t, 512)

    xi, yi, ci = _mesh_pos()
    jidx = jnp.stack([_lid(px, py, ci) for px, py in _chip_patterns(xi, yi)]).astype(jnp.int32)

    win_g, wout_g, conv_g, sup, sdown = all_gather_mixer_weights(
        w_in, w_out, w_up.T, w_down, _pack_conv(conv_b_w, conv_f_w))
    wout_full = wout_g.reshape(D, D)
    conv_b_all, cfw = _unpack_conv(conv_g)
    conv_b_full = conv_b_all.transpose(1, 0, 2).reshape(KB, D_B)
    cfb = conv_f_b.reshape(N_DEV, W_UP_BLK)
    row = lambda a: a.reshape(1, -1)
    bst = b_spatial.T

    h, xhat1, rstd1, yb1, wup_g, wdown_g = mix_forward(
        x2, win_g, row(b_in), ln_a_g, ln_a_b, w_spatial, bst, conv_b_full, row(conv_b_b),
        row(ln_b_g), row(ln_b_b), wout_full, row(b_out), row(ln1_g), row(ln1_b), sup, sdown, tm_fwd)
    wdown4 = wdown_g.reshape(N_F, W_UP_BLK, D)
    hu, gv, dr2, loss_part, s_ln2 = ffn_forward(
        xhat1, row(ln1_g), row(ln1_b), wup_g, cfw, cfb, wdown4, row(ln2_g), row(ln2_b), target, tm_bwd)

    order = jnp.where(ci == 0, jnp.array([1, 3, 0, 2], jnp.int32), jnp.array([0, 2, 1, 3], jnp.int32))
    dwup, dwdown, dcfw, dcfb, dx1p, *ffn_lands = ffn_backward(
        order, dr2, xhat1, row(ln1_g), row(ln1_b), hu, gv, wup_g, cfw, wdown4, tm_ffn_bwd)
    ffn_grads = [dwup.reshape(N_DEV, W_UP_BLK, D), dwdown.reshape(N_DEV, D_FF // N_DEV, D)]
    ffn_partials = [chip_partials("chip_partials_" + nm, g, l, jidx, rb)
                    for nm, g, l, rb in zip(["w_up", "w_down"], ffn_grads, ffn_lands, [352, 352])]
    grad_x, dwin, dwout, dcw, s_mix, *ffn_recvs = mix_backward(
        x2, h, yb1, dx1p, dr2, xhat1, rstd1, win_g, ln_a_g, ln_a_b, w_spatial, bst,
        conv_b_full, row(ln_b_g), row(ln_b_b), wout_full, row(ln1_g), ffn_partials, tm_bwd)

    dcfb_rows = jnp.pad(dcfb.reshape(-1, 128), ((0, 4), (0, 0)))
    svec = jnp.concatenate([s_mix, dcfb_rows, s_ln2, loss_part], axis=0)
    dconv = _pack_conv(dcw.reshape(KB, N_DEV, 64).transpose(1, 0, 2), dcfw.reshape(N_DEV, KF, W_UP_BLK))
    mix_grads = [dwin, dwout.reshape(N_DEV, D // N_DEV, D), dconv]
    mix_w = [w_in, w_out, _pack_conv(conv_b_w, conv_f_w)]
    mix_m = [m_w_in, m_w_out, _pack_conv(m_conv_b_w, m_conv_f_w)]
    mix_v = [v_w_in, v_w_out, _pack_conv(v_conv_b_w, v_conv_f_w)]
    *mix_out, sv_slots = mixer_reduce_adamw(mix_grads, svec, mix_w, mix_m, mix_v)
    big = {nm: [mix_out[k * 3 + p] for k in range(4)] for p, nm in enumerate(["w_in", "w_out", "conv"])}

    ffn_w = [(w_up.T, m_w_up.T, v_w_up.T), (w_down, m_w_down, v_w_down)]
    for nm, g, l, r, (w, m, v) in zip(["w_up", "w_down"], ffn_grads, ffn_lands, ffn_recvs, ffn_w):
        big[nm] = reduce_and_adamw("reduce_adamw_" + nm, g, l, r, w, m, v, jidx, 352)
    big["w_up"] = [o.T for o in big["w_up"]]
    for k in range(4):
        cb_k, cf_k = _unpack_conv(big["conv"][k])
        big.setdefault("conv_b_w", []).append(cb_k)
        big.setdefault("conv_f_w", []).append(cf_k)

    small_w = dict(b_in=b_in, ln_a_g=ln_a_g, ln_a_b=ln_a_b, w_spatial=w_spatial, b_spatial=b_spatial,
                   conv_b_b=conv_b_b, ln_b_g=ln_b_g, ln_b_b=ln_b_b, b_out=b_out, ln1_g=ln1_g,
                   ln1_b=ln1_b, conv_f_b=conv_f_b, ln2_g=ln2_g, ln2_b=ln2_b)
    small_m = dict(b_in=m_b_in, ln_a_g=m_ln_a_g, ln_a_b=m_ln_a_b, w_spatial=m_w_spatial,
                   b_spatial=m_b_spatial, conv_b_b=m_conv_b_b, ln_b_g=m_ln_b_g, ln_b_b=m_ln_b_b,
                   b_out=m_b_out, ln1_g=m_ln1_g, ln1_b=m_ln1_b, conv_f_b=m_conv_f_b, ln2_g=m_ln2_g,
                   ln2_b=m_ln2_b)
    small_v = dict(b_in=v_b_in, ln_a_g=v_ln_a_g, ln_a_b=v_ln_a_b, w_spatial=v_w_spatial,
                   b_spatial=v_b_spatial, conv_b_b=v_conv_b_b, ln_b_g=v_ln_b_g, ln_b_b=v_ln_b_b,
                   b_out=v_b_out, ln1_g=v_ln1_g, ln1_b=v_ln1_b, conv_f_b=v_conv_f_b, ln2_g=v_ln2_g,
                   ln2_b=v_ln2_b)
    order = [nm for nm, _, _ in SMALL_LAYOUT]
    small_out = small_adamw(sv_slots, [_rows128(small_w[nm]) for nm in order],
                            [_rows128(small_m[nm]) for nm in order], [_rows128(small_v[nm]) for nm in order])
    n_small = len(order)
    small = {nm: [small_out[k * n_small + p].reshape(small_w[nm].shape) for k in range(4)]
             for p, nm in enumerate(order)}
    loss = jnp.sum(small_out[4 * n_small]) * (0.5 / D)

    weights = ["w_in", "b_in", "ln_a_g", "ln_a_b", "w_spatial", "b_spatial", "conv_b_w", "conv_b_b",
               "ln_b_g", "ln_b_b", "w_out", "b_out", "ln1_g", "ln1_b", "w_up", "conv_f_w", "conv_f_b",
               "w_down", "ln2_g", "ln2_b"]
    result = lambda nm, k: big[nm][k] if nm in big else small[nm][k]
    return (loss, grad_x.reshape(x.shape), *[result(nm, 0) for nm in weights],
            *[result(nm, 1) for nm in weights], *[result(nm, 2) for nm in weights],
            *[result(nm, 3) for nm in weights])
```

```python
import functools
import math

import jax
import jax.numpy as jnp
from jax import lax
from jax.experimental import pallas as pl
from jax.experimental.pallas import tpu as pltpu

F32 = jnp.float32
BF16 = jnp.bfloat16

D = 1024
D_A = 512
D_B = 512
HEADS = 4
HEAD_DIM = 128
CHUNK = 128
KB = 31
KF = 3
D_FF = 2816
D_IN = 2048
N_DEV = 8
W_IN_BLK = D_IN // N_DEV
W_UP_BLK = 2 * D_FF // N_DEV
N_F = 4
LN_EPS = 1e-5
ALPHA = 2.0 ** 0.25

ADAM_LR = 0.001
ADAM_B1 = 0.9
ADAM_B2 = 0.999
ADAM_EPS = 1e-08
ADAM_WD = 0.01
ADAM_STEP = 10

INV_SQRT2 = 1.0 / math.sqrt(2.0)
INV_SQRT_2PI = 1.0 / math.sqrt(2.0 * math.pi)

HALO_B = 32
HALO_F = 8
ROWS = 64
LN_ROWS = 32
VMEM_LIMIT = 58 * 1024 * 1024

MESH = pl.DeviceIdType.MESH
ANY = pl.BlockSpec(memory_space=pl.ANY)
VMEM = pl.BlockSpec(memory_space=pltpu.VMEM)

S_BIN, S_LNAG, S_LNAB, S_WS, S_BS, S_CBB, S_LNBG, S_LNBB, S_BOUT, S_LN1G, S_LN1B = (
    0, 16, 24, 32, 544, 552, 560, 568, 576, 584, 592)
S_MIX_ROWS = 600
S_CFB = 600
S_LN2G = 648
S_LN2B = 656
S_LOSS = 664
S_ROWS = 672


def _tn(a, b):
    return lax.dot_general(a, b, (((0,), (0,)), ((), ())), preferred_element_type=F32)


def _nt(a, b):
    return lax.dot_general(a, b, (((1,), (1,)), ((), ())), preferred_element_type=F32)


def _nn(a, b):
    return jnp.dot(a, b, preferred_element_type=F32)


def _sigmoid(x):
    return 1.0 / (1.0 + jnp.exp(-x))


def _ln_stats(x):
    mu = jnp.mean(x, axis=-1, keepdims=True)
    xc = x - mu
    var = jnp.mean(xc * xc, axis=-1, keepdims=True)
    rstd = lax.rsqrt(var + LN_EPS)
    return xc * rstd, rstd


def _ln_bwd(dxhat, xhat, rstd):
    m1 = jnp.mean(dxhat, axis=-1, keepdims=True)
    m2 = jnp.mean(dxhat * xhat, axis=-1, keepdims=True)
    return rstd * (dxhat - m1 - xhat * m2)


def _rsum8(x):
    r, n = x.shape
    return x.reshape(r // 8, 8, n).sum(axis=0)


def _rows(i, n=ROWS):
    return pl.ds(i * n, n)


def _loop(n, body):
    for i in range(n):
        body(i)


def _tril_mask():
    r = lax.broadcasted_iota(jnp.int32, (CHUNK, CHUNK), 0)
    c = lax.broadcasted_iota(jnp.int32, (CHUNK, CHUNK), 1)
    return c <= r


def _mixer_a_head(h_ref, r, hd, ga_ref, ba_ref, wsm_ref, bst_ref):
    sl = slice(hd * HEAD_DIM, (hd + 1) * HEAD_DIM)
    hu = h_ref[r, sl]
    hv = h_ref[r, D_A + hd * HEAD_DIM:D_A + (hd + 1) * HEAD_DIM]
    cdf_u = 0.5 * (1.0 + lax.erf(hu * INV_SQRT2))
    cdf_v = 0.5 * (1.0 + lax.erf(hv * INV_SQRT2))
    u = hu * cdf_u
    xhat, rstd = _ln_stats(hv * cdf_v)
    vn = (xhat * ga_ref[hd:hd + 1, :] + ba_ref[hd:hd + 1, :]).astype(BF16)
    sv = _nn(wsm_ref[hd], vn) + bst_ref[:, hd:hd + 1]
    return hu, hv, u, cdf_u, cdf_v, xhat, rstd, vn, sv


def _taps(win, offsets):
    n = win.shape[0]
    for s in range(8):
        ks = [k for k, o in enumerate(offsets) if o % 8 == s]
        if ks:
            moved = win if s == 0 else pltpu.roll(win, n - s, 0)
            for k in ks:
                yield k, moved[offsets[k] - s:offsets[k] - s + ROWS, :]


CONV_B_OFFSETS = [2 + k for k in range(KB)]
CONV_B_T_OFFSETS = [30 - k for k in range(KB)]


def _conv_b_block(ext_ref, base, cw_ref):
    acc = jnp.zeros((ROWS, D_B), F32)
    for k, tap in _taps(ext_ref[pl.ds(base, ROWS + HALO_B), :], CONV_B_OFFSETS):
        acc = acc + tap * cw_ref[k:k + 1, :]
    return acc


def _taps_f(win):
    n = ROWS + HALO_F
    return [pltpu.roll(win, n - 6, 0)[0:ROWS, :], pltpu.roll(win, n - 7, 0)[0:ROWS, :], win[8:n, :]]


def _params(sem, **kw):
    return pltpu.CompilerParams(dimension_semantics=sem, vmem_limit_bytes=VMEM_LIMIT, **kw)


def _resident(shape):
    zeros = (0,) * len(shape)
    return pl.BlockSpec(shape, lambda *_: zeros, pipeline_mode=pl.Buffered(1))


def _full(shape):
    zeros = (0,) * len(shape)
    return pl.BlockSpec(shape, lambda *_: zeros)


def _mesh_pos():
    return lax.axis_index("x"), lax.axis_index("y"), lax.axis_index("c")


def _chip_patterns(x, y):
    return [(x, y), (1 - x, y), (x, 1 - y), (1 - x, 1 - y)]


def _lid(x, y, c):
    return 4 * x + 2 * y + c


def _gather_copy(outs, send_sems, recv_sems, a, k, block, to, src=None):
    blk = outs[a].at[_lid(*block)]
    return pltpu.make_async_remote_copy(
        src_ref=blk if src is None else src, dst_ref=blk,
        send_sem=send_sems.at[a, k], recv_sem=recv_sems.at[a, k], device_id=to, device_id_type=MESH)


def _gather_start(mine, outs, send_sems, recv_sems, local_sems):
    x, y, c = _mesh_pos()
    me, sib = (x, y, c), (x, y, 1 - c)
    for a in range(len(mine)):
        pltpu.make_async_copy(mine[a], outs[a].at[_lid(*me)], local_sems.at[a]).start()
        _gather_copy(outs, send_sems, recv_sems, a, 0, me, sib, src=mine[a]).start()
        for j, chip in enumerate(_chip_patterns(x, y)[1:]):
            _gather_copy(outs, send_sems, recv_sems, a, 1 + j, me, (*chip, c), src=mine[a]).start()


def _gather_finish(mine, outs, send_sems, recv_sems, local_sems):
    x, y, c = _mesh_pos()
    me, sib = (x, y, c), (x, y, 1 - c)
    chips = _chip_patterns(x, y)[1:]
    n = len(mine)
    copy = functools.partial(_gather_copy, outs, send_sems, recv_sems)
    passed = []
    for j, chip in enumerate(chips):
        for a in range(n):
            copy(a, 1 + j, (*chip, c), me).wait_recv()
            cp = copy(a, 4 + j, (*chip, c), sib)
            cp.start()
            passed.append(cp)
    for a in range(n):
        copy(a, 0, sib, me).wait_recv()
        for j, chip in enumerate(chips):
            copy(a, 4 + j, (*chip, 1 - c), me).wait_recv()
        for k in range(4):
            copy(a, k, me, sib, src=mine[a]).wait_send()
        pltpu.make_async_copy(mine[a], outs[a].at[_lid(*me)], local_sems.at[a]).wait()
    for cp in passed:
        cp.wait_send()


def _gather_scratch(n):
    return [pltpu.SemaphoreType.DMA((n, 7)), pltpu.SemaphoreType.DMA((n, 7)), pltpu.SemaphoreType.DMA((n,))]


def all_gather_mixer_weights(w_in, w_out, w_up, w_down, convp):
    srcs = [w_in, w_out, convp]
    n = len(srcs)

    def body(win_ref, wout_ref, convp_ref, wup_ref, wdown_ref,
             gin_ref, gout_ref, gconv_ref, sup_ref, sdown_ref,
             sin_ref, sout_ref, send_sems, recv_sems, local_sems):
        sin_ref[...] = win_ref[...].astype(BF16)
        sout_ref[...] = wout_ref[...].astype(BF16)
        mine = [sin_ref, sout_ref, convp_ref]
        outs = [gin_ref, gout_ref, gconv_ref]
        _gather_start(mine, outs, send_sems, recv_sems, local_sems)
        sup_ref[...] = wup_ref[...].T.astype(BF16)
        sdown_ref[...] = wdown_ref[...].astype(BF16)
        _gather_finish(mine, outs, send_sems, recv_sems, local_sems)

    return pl.pallas_call(
        body, name="all_gather_mixer_weights",
        out_shape=[jax.ShapeDtypeStruct((N_DEV,) + w_in.shape, BF16),
                   jax.ShapeDtypeStruct((N_DEV,) + w_out.shape, BF16),
                   jax.ShapeDtypeStruct((N_DEV,) + convp.shape, F32),
                   jax.ShapeDtypeStruct(w_up.shape[::-1], BF16), jax.ShapeDtypeStruct(w_down.shape, BF16)],
        in_specs=[VMEM] * 5, out_specs=[ANY] * n + [VMEM, VMEM],
        scratch_shapes=[pltpu.VMEM(w_in.shape, BF16), pltpu.VMEM(w_out.shape, BF16)] + _gather_scratch(n),
        compiler_params=pltpu.CompilerParams(vmem_limit_bytes=VMEM_LIMIT),
    )(w_in, w_out, convp, w_up, w_down)


def _chip_copies(p, land, send_sems, recv_sems):
    x, y, c = _mesh_pos()
    return [pltpu.make_async_remote_copy(
        src_ref=p[a].at[k], dst_ref=land[a].at[k], send_sem=send_sems.at[a, k], recv_sem=recv_sems.at[a, k],
        device_id=(px, py, c), device_id_type=MESH)
        for k, (px, py) in enumerate(_chip_patterns(x, y)[1:]) for a in range(len(p))]


def chip_partials(name, g, land, jidx, rb):
    _, r, c = g.shape

    def body(j_ref, g_ref, l_ref, o_ref):
        o_ref[...] = (g_ref[...] + l_ref[...]).astype(BF16)

    return pl.pallas_call(
        body, name=name,
        out_shape=jax.ShapeDtypeStruct((3, r, c), BF16),
        grid_spec=pltpu.PrefetchScalarGridSpec(
            num_scalar_prefetch=1, grid=(3, r // rb),
            in_specs=[pl.BlockSpec((1, rb, c), lambda k, i, j: (j[1 + k], i, 0)),
                      pl.BlockSpec((1, rb, c), lambda k, i, j: (1 + k, i, 0))],
            out_specs=pl.BlockSpec((1, rb, c), lambda k, i, j: (k, i, 0))),
        compiler_params=_params(("arbitrary", "arbitrary")),
    )(jidx, g, land)


def _adamw(w, g, m, v):
    m2 = ADAM_B1 * m + (1.0 - ADAM_B1) * g
    v2 = ADAM_B2 * v + (1.0 - ADAM_B2) * (g * g)
    m_hat = m2 / (1.0 - ADAM_B1 ** ADAM_STEP)
    v_hat = v2 / (1.0 - ADAM_B2 ** ADAM_STEP)
    delta = -ADAM_LR * (m_hat / (jnp.sqrt(v_hat) + ADAM_EPS) + ADAM_WD * w)
    return delta, m2, v2


def reduce_and_adamw(name, g, land, recv, w, m, v, jidx, rb):
    _, r, c = g.shape

    def body(j_ref, g_ref, l_ref, r_ref, w_ref, m_ref, v_ref, go_ref, do_ref, mo_ref, vo_ref):
        grad = (g_ref[0] + l_ref[0]) + r_ref[0].astype(F32) + r_ref[1].astype(F32) + r_ref[2].astype(F32)
        delta, m2, v2 = _adamw(w_ref[...], grad, m_ref[...], v_ref[...])
        go_ref[...] = grad
        do_ref[...] = delta
        mo_ref[...] = m2
        vo_ref[...] = v2

    blk = pl.BlockSpec((rb, c), lambda i, j: (i, 0))
    return pl.pallas_call(
        body, name=name,
        out_shape=[jax.ShapeDtypeStruct((r, c), F32)] * 4,
        grid_spec=pltpu.PrefetchScalarGridSpec(
            num_scalar_prefetch=1, grid=(r // rb,),
            in_specs=[pl.BlockSpec((1, rb, c), lambda i, j: (j[0], i, 0)),
                      pl.BlockSpec((1, rb, c), lambda i, j: (0, i, 0)),
                      pl.BlockSpec((3, rb, c), lambda i, j: (0, i, 0)),
                      blk, blk, blk],
            out_specs=[blk] * 4),
        compiler_params=_params(("arbitrary",)),
    )(jidx, g, land, recv, w, m, v)


def mixer_reduce_adamw(grads, svec, ws, ms, vs):
    n = len(grads)
    shard = [g.shape[1:] for g in grads]

    def body(*refs):
        g = refs[:n]
        sv_ref = refs[n]
        w, m, v = refs[n + 1:2 * n + 1], refs[2 * n + 1:3 * n + 1], refs[3 * n + 1:4 * n + 1]
        outs = refs[4 * n + 1:8 * n + 1]
        sv_slots = refs[8 * n + 1]
        rest = refs[8 * n + 2:]
        own, land, sendb, recvb = rest[:n], rest[n:2 * n], rest[2 * n:3 * n], rest[3 * n:4 * n]
        sv_land, chip_sv, d2d_send, d2d_recv, ici_send, ici_recv, local_sems, sv_sems = rest[4 * n:]
        x, y, c = _mesh_pos()
        sib = (x, y, 1 - c)
        pats = _chip_patterns(x, y)
        q = 2 * x + y

        d2d, local = [], []
        for a in range(n):
            for k, (px, py) in enumerate(pats):
                d2d.append(pltpu.make_async_remote_copy(
                    src_ref=g[a].at[_lid(px, py, 1 - c)], dst_ref=land[a].at[k],
                    send_sem=d2d_send.at[a, k], recv_sem=d2d_recv.at[a, k], device_id=sib, device_id_type=MESH))
                local.append(pltpu.make_async_copy(g[a].at[_lid(px, py, c)], own[a].at[k], local_sems.at[a, k]))
        d2d.append(pltpu.make_async_remote_copy(
            src_ref=sv_ref, dst_ref=sv_land, send_sem=d2d_send.at[n, 0], recv_sem=d2d_recv.at[n, 0],
            device_id=sib, device_id_type=MESH))
        for cp in d2d + local:
            cp.start()
        for cp in local + d2d:
            cp.wait()

        for a in range(n):
            for k in range(3):
                sendb[a][k] = (own[a][1 + k] + land[a][1 + k]).astype(BF16)
        chip_sv[...] = sv_ref[...] + sv_land[...]
        ici = _chip_copies(sendb, recvb, ici_send, ici_recv)
        sv_local = pltpu.make_async_copy(chip_sv, sv_slots.at[q], sv_sems.at[0])
        sv_out = [pltpu.make_async_remote_copy(
            src_ref=chip_sv, dst_ref=sv_slots.at[q], send_sem=sv_sems.at[1 + k], recv_sem=sv_sems.at[4 + k],
            device_id=(px, py, c), device_id_type=MESH) for k, (px, py) in enumerate(pats[1:])]
        for cp in ici + sv_out + [sv_local]:
            cp.start()
        for cp in ici:
            cp.wait()
        for k, (px, py) in enumerate(pats[1:]):
            sv_out[k].wait_send()
            pltpu.make_async_remote_copy(
                src_ref=chip_sv, dst_ref=sv_slots.at[2 * px + py], send_sem=sv_sems.at[1 + k],
                recv_sem=sv_sems.at[4 + k], device_id=(px, py, c), device_id_type=MESH).wait_recv()
        sv_local.wait()

        for a in range(n):
            grad = ((own[a][0] + land[a][0]) + recvb[a][0].astype(F32) + recvb[a][1].astype(F32)
                    + recvb[a][2].astype(F32))
            delta, m2, v2 = _adamw(w[a][...], grad, m[a][...], v[a][...])
            outs[a][...] = grad
            outs[n + a][...] = delta
            outs[2 * n + a][...] = m2
            outs[3 * n + a][...] = v2

    shard_out = [jax.ShapeDtypeStruct(s, F32) for s in shard]
    return pl.pallas_call(
        body, name="mixer_reduce_adamw",
        out_shape=shard_out * 4 + [jax.ShapeDtypeStruct((4,) + svec.shape, F32)],
        in_specs=[ANY] * n + [VMEM] * (1 + 3 * n), out_specs=[VMEM] * (4 * n) + [ANY],
        scratch_shapes=[pltpu.VMEM((4,) + s, F32) for s in shard] + [pltpu.VMEM((4,) + s, F32) for s in shard]
        + [pltpu.VMEM((3,) + s, BF16) for s in shard] + [pltpu.VMEM((3,) + s, BF16) for s in shard]
        + [pltpu.VMEM(svec.shape, F32), pltpu.VMEM(svec.shape, F32),
           pltpu.SemaphoreType.DMA((n + 1, 4)), pltpu.SemaphoreType.DMA((n + 1, 4)),
           pltpu.SemaphoreType.DMA((n, 3)), pltpu.SemaphoreType.DMA((n, 3)),
           pltpu.SemaphoreType.DMA((n, 4)), pltpu.SemaphoreType.DMA((7,))],
        compiler_params=pltpu.CompilerParams(vmem_limit_bytes=VMEM_LIMIT),
    )(*grads, svec, *ws, *ms, *vs)


SMALL_LAYOUT = [
    ("b_in", S_BIN, 16), ("ln_a_g", S_LNAG, 4), ("ln_a_b", S_LNAB, 4), ("w_spatial", S_WS, 512),
    ("b_spatial", S_BS, 4), ("conv_b_b", S_CBB, 4), ("ln_b_g", S_LNBG, 4), ("ln_b_b", S_LNBB, 4),
    ("b_out", S_BOUT, 8), ("ln1_g", S_LN1G, 8), ("ln1_b", S_LN1B, 8), ("conv_f_b", S_CFB, 44),
    ("ln2_g", S_LN2G, 8), ("ln2_b", S_LN2B, 8),
]


def small_adamw(sv_slots, ws, ms, vs):
    n = len(SMALL_LAYOUT)

    def body(*refs):
        s_ref = refs[0]
        w_refs, m_refs, v_refs = refs[1:1 + n], refs[1 + n:1 + 2 * n], refs[1 + 2 * n:1 + 3 * n]
        outs = refs[1 + 3 * n:]
        for p, (_, row0, rows) in enumerate(SMALL_LAYOUT):
            sl = pl.ds(row0, rows)
            grad = ((s_ref[0, sl, :] + s_ref[1, sl, :]) + s_ref[2, sl, :]) + s_ref[3, sl, :]
            delta, m2, v2 = _adamw(w_refs[p][...], grad, m_refs[p][...], v_refs[p][...])
            outs[p][...] = grad
            outs[n + p][...] = delta
            outs[2 * n + p][...] = m2
            outs[3 * n + p][...] = v2
        sl = pl.ds(S_LOSS, 8)
        outs[4 * n][...] = ((s_ref[0, sl, :] + s_ref[1, sl, :]) + s_ref[2, sl, :]) + s_ref[3, sl, :]

    shapes = [jax.ShapeDtypeStruct((rows, 128), F32) for _, _, rows in SMALL_LAYOUT]
    return pl.pallas_call(
        body, name="small_adamw", out_shape=shapes * 4 + [jax.ShapeDtypeStruct((8, 128), F32)],
        in_specs=[VMEM] * (1 + 3 * n), out_specs=[VMEM] * (4 * n + 1),
    )(sv_slots, *ws, *ms, *vs)


def mix_forward(x, win_g, b_in, ln_a_g, ln_a_b, w_spatial, bst, conv_b_w, conv_b_b, ln_b_g, ln_b_b,
                wout, b_out, ln1_g, ln1_b, sup, sdown, tm):
    t = x.shape[0]
    nt = t // tm
    n_chunks = tm // CHUNK

    def body(x_ref, win_ref, bin_ref, ga_ref, ba_ref, ws_ref, bst_ref, cw_ref, cb_ref, gb_ref,
             bb_ref, wout_ref, bout_ref, g1_ref, b1_ref, sup_ref, sdown_ref,
             h_ref, xhat1_ref, rstd1_ref, yb1_ref, gup_ref, gdown_ref,
             ext_ref, y_ref, wsm_ref, send_sems, recv_sems, local_sems):
        i = pl.program_id(0)
        gather = ([sup_ref, sdown_ref], [gup_ref, gdown_ref], send_sems, recv_sems, local_sems)

        @pl.when(i == 0)
        def _():
            _gather_start(*gather)
            ext_ref[0:HALO_B, :] = jnp.zeros((HALO_B, D_B), F32)
            mask = _tril_mask()
            for hd in range(HEADS):
                wsm_ref[hd] = jnp.where(mask, ws_ref[hd], 0.0).astype(BF16)

        xb = x_ref[...].astype(BF16)
        for j in range(N_DEV):
            cols = slice(j * W_IN_BLK, (j + 1) * W_IN_BLK)
            h_ref[:, cols] = _nn(xb, win_ref[j]) + bin_ref[:, cols]

        def chunk(ci):
            r = _rows(ci, CHUNK)
            for hd in range(HEADS):
                _, _, u, _, _, _, _, _, sv = _mixer_a_head(h_ref, r, hd, ga_ref, ba_ref, wsm_ref, bst_ref)
                y_ref[r, hd * HEAD_DIM:(hd + 1) * HEAD_DIM] = (u * sv).astype(BF16)
            a_b = h_ref[r, 2 * D_A:2 * D_A + D_B]
            g_b = h_ref[r, 2 * D_A + D_B:D_IN]
            ext_ref[pl.ds(HALO_B + ci * CHUNK, CHUNK), :] = a_b * _sigmoid(g_b)

        _loop(n_chunks, chunk)

        def conv_rows(bi):
            base = bi * ROWS
            yb1 = _conv_b_block(ext_ref, base, cw_ref) + cb_ref[...]
            yb1_ref[pl.ds(base, ROWS), :] = yb1
            xhat, _ = _ln_stats(yb1)
            yb2 = xhat * gb_ref[...] + bb_ref[...]
            y_ref[pl.ds(base, ROWS), D_A:D] = (yb2 * _sigmoid(yb2)).astype(BF16)

        _loop(tm // ROWS, conv_rows)
        ext_ref[0:HALO_B, :] = ext_ref[tm:tm + HALO_B, :]

        mix = _nn(y_ref[...], wout_ref[...]) + bout_ref[...]
        xhat1, rstd1 = _ln_stats(ALPHA * x_ref[...] + mix)
        xhat1_ref[...] = xhat1
        rstd1_ref[...] = jnp.broadcast_to(rstd1, (tm, 128))

        @pl.when(i == nt - 1)
        def _():
            _gather_finish(*gather)

    row = lambda w: pl.BlockSpec((tm, w), lambda i: (i, 0))
    return pl.pallas_call(
        body, name="mix_forward", grid=(nt,),
        in_specs=[row(D), _resident(win_g.shape), _full(b_in.shape), _full(ln_a_g.shape),
                  _full(ln_a_b.shape), _full(w_spatial.shape), _full(bst.shape),
                  _full(conv_b_w.shape), _full(conv_b_b.shape), _full(ln_b_g.shape),
                  _full(ln_b_b.shape), _resident(wout.shape), _full(b_out.shape),
                  _full(ln1_g.shape), _full(ln1_b.shape), ANY, ANY],
        out_specs=[row(D_IN), row(D), row(128), row(D_B), ANY, ANY],
        out_shape=[jax.ShapeDtypeStruct((t, D_IN), F32), jax.ShapeDtypeStruct((t, D), F32),
                   jax.ShapeDtypeStruct((t, 128), F32), jax.ShapeDtypeStruct((t, D_B), F32),
                   jax.ShapeDtypeStruct((N_DEV,) + sup.shape, BF16),
                   jax.ShapeDtypeStruct((N_DEV,) + sdown.shape, BF16)],
        scratch_shapes=[pltpu.VMEM((tm + HALO_B, D_B), F32), pltpu.VMEM((tm, D), BF16),
                        pltpu.VMEM((HEADS, CHUNK, CHUNK), BF16)] + _gather_scratch(2),
        compiler_params=_params(("arbitrary",)),
    )(x, win_g, b_in, ln_a_g, ln_a_b, w_spatial, bst, conv_b_w, conv_b_b, ln_b_g, ln_b_b,
      wout, b_out, ln1_g, ln1_b, sup, sdown)


def ffn_forward(xhat1, ln1_g, ln1_b, wup_g, cfw, cfb, wdown, ln2_g, ln2_b, target, tm):
    t = xhat1.shape[0]
    nt = t // tm

    def body(xh_ref, g1_ref, b1_ref, wup_ref, cfw_ref, cfb_ref, wdown_ref, g2_ref, b2_ref, tgt_ref,
             hu_ref, gv_ref, dr2_ref, loss_ref, sln2_ref,
             x1_ref, x1b_ref, carry_ref, gbuf_ref, ffn_ref, acc_loss, acc_g2, acc_b2):
        i = pl.program_id(0)

        @pl.when(i == 0)
        def _():
            carry_ref[...] = jnp.zeros(carry_ref.shape, F32)
            acc_loss[...] = jnp.zeros(acc_loss.shape, F32)
            acc_g2[...] = jnp.zeros(acc_g2.shape, F32)
            acc_b2[...] = jnp.zeros(acc_b2.shape, F32)

        x1 = xh_ref[...] * g1_ref[...] + b1_ref[...]
        x1_ref[...] = x1
        x1b_ref[...] = x1.astype(BF16)

        def conv(j, base):
            if base == 0:
                win = jnp.concatenate([carry_ref[j], hu_ref[j, 0:ROWS, :]], axis=0)
            else:
                win = hu_ref[j, base - HALO_F:base + ROWS, :]
            taps = _taps_f(win)
            w = cfw_ref[j]
            return sum(taps[k] * w[k:k + 1, :] for k in range(KF)) + cfb_ref[j:j + 1, :]

        for f in range(N_F):
            hu_ref[f] = _nn(x1b_ref[...], wup_ref[f])
            hu_ref[N_F + f] = _nn(x1b_ref[...], wup_ref[N_F + f])

            def rows(bi, f=f):
                gate = conv(f, bi * ROWS)
                val = conv(N_F + f, bi * ROWS)
                gbuf_ref[_rows(bi), :] = (gate * _sigmoid(gate) * val).astype(BF16)
                gv_ref[f, _rows(bi), :] = gate.astype(BF16)
                gv_ref[N_F + f, _rows(bi), :] = val.astype(BF16)

            _loop(tm // ROWS, rows)
            carry_ref[f] = hu_ref[f, tm - HALO_F:tm, :]
            carry_ref[N_F + f] = hu_ref[N_F + f, tm - HALO_F:tm, :]
            part = _nn(gbuf_ref[...], wdown_ref[f])
            if f == 0:
                ffn_ref[...] = part
            else:
                ffn_ref[...] += part

        def tail(bi):
            r = _rows(bi, LN_ROWS)
            xhat2, rstd2 = _ln_stats(ALPHA * x1_ref[r, :] + ffn_ref[r, :])
            err = xhat2 * g2_ref[...] + b2_ref[...] - tgt_ref[r, :]
            e2 = _rsum8(err * err)
            acc_loss[...] += sum(e2[:, k * 128:(k + 1) * 128] for k in range(D // 128))
            dy = err * (1.0 / D)
            acc_g2[...] += _rsum8(dy * xhat2)
            acc_b2[...] += _rsum8(dy)
            dr2_ref[r, :] = _ln_bwd(dy * g2_ref[...], xhat2, rstd2)

        _loop(tm // LN_ROWS, tail)
        loss_ref[...] = acc_loss[...]

        @pl.when(i == nt - 1)
        def _():
            dg = jnp.sum(acc_g2[...], axis=0, keepdims=True)
            db = jnp.sum(acc_b2[...], axis=0, keepdims=True)
            for k in range(D // 128):
                sln2_ref[k:k + 1, :] = dg[:, k * 128:(k + 1) * 128]
                sln2_ref[8 + k:9 + k, :] = db[:, k * 128:(k + 1) * 128]

    row = pl.BlockSpec((tm, D), lambda i: (i, 0))
    return pl.pallas_call(
        body, name="ffn_forward", grid=(nt,),
        in_specs=[row, _full(ln1_g.shape), _full(ln1_b.shape), _resident(wup_g.shape),
                  _full(cfw.shape), _full(cfb.shape), _resident(wdown.shape),
                  _full(ln2_g.shape), _full(ln2_b.shape), row],
        out_specs=[pl.BlockSpec((N_DEV, tm, W_UP_BLK), lambda i: (0, i, 0)),
                   pl.BlockSpec((N_DEV, tm, W_UP_BLK), lambda i: (0, i, 0)), row,
                   _full((8, 128)), _full((16, 128))],
        out_shape=[jax.ShapeDtypeStruct((N_DEV, t, W_UP_BLK), F32),
                   jax.ShapeDtypeStruct((N_DEV, t, W_UP_BLK), BF16), jax.ShapeDtypeStruct((t, D), F32),
                   jax.ShapeDtypeStruct((8, 128), F32), jax.ShapeDtypeStruct((16, 128), F32)],
        scratch_shapes=[pltpu.VMEM((tm, D), F32), pltpu.VMEM((tm, D), BF16),
                        pltpu.VMEM((N_DEV, HALO_F, W_UP_BLK), F32), pltpu.VMEM((tm, W_UP_BLK), BF16),
                        pltpu.VMEM((tm, D), F32), pltpu.VMEM((8, 128), F32),
                        pltpu.VMEM((8, D), F32), pltpu.VMEM((8, D), F32)],
        compiler_params=_params(("arbitrary",)),
    )(xhat1, ln1_g, ln1_b, wup_g, cfw, cfb, wdown, ln2_g, ln2_b, target)


def ffn_backward(order, dr2, xhat1, ln1_g, ln1_b, hu, gv, wup_g, cfw, wdown, tm):
    t = dr2.shape[0]
    nt = t // tm
    sub_rows = tm
    hu4 = hu.reshape(2, N_F, t, W_UP_BLK)
    gv4 = gv.reshape(2, N_F, t, W_UP_BLK)
    wup4 = wup_g.reshape(2, N_F, D, W_UP_BLK)
    cfw4 = cfw.reshape(2, N_F, KF, W_UP_BLK)

    def body(order_ref, dr2_ref, xh_ref, g1_ref, b1_ref, hu_ref, gv_ref, wup_ref, cfw_ref, wdown_ref,
             dwup_ref, dwdown_ref, dcfw_ref, dcfb_ref, dx1_ref, land_up_ref, land_down_ref,
             x1b_ref, drb_ref, dg_ref, dextg_ref, dextv_ref, gbuf_ref,
             dhug_ref, dhuv_ref, acc_wup, acc_wdown, acc_cfw, acc_cfb, sem, send_sems, recv_sems):
        f = order_ref[pl.program_id(0)]
        i = pl.program_id(1)
        x, y, c = _mesh_pos()
        half = D_FF // N_DEV

        def to_sibling(fi, k, src, land_ref, shard_chip):
            d = jnp.bitwise_xor(shard_chip, 2 * x + y)
            slot = jnp.where(d == 1, 2, jnp.where(d == 2, 1, d))
            return pltpu.make_async_remote_copy(
                src_ref=src, dst_ref=land_ref.at[slot], send_sem=send_sems.at[fi, k], recv_sem=recv_sems.at[fi, k],
                device_id=(x, y, 1 - c), device_id_type=MESH)

        def up_copy(fi, g):
            return to_sibling(fi, g, dwup_ref.at[g, fi], land_up_ref, 2 * g + fi // 2)

        def down_copy(fi):
            return to_sibling(fi, 2, dwdown_ref.at[fi, pl.ds((1 - c) * half, half)], land_down_ref, fi)

        @pl.when(i == 0)
        def _():
            acc_wup[...] = jnp.zeros(acc_wup.shape, F32)
            acc_wdown[...] = jnp.zeros(acc_wdown.shape, F32)
            acc_cfw[...] = jnp.zeros(acc_cfw.shape, F32)
            acc_cfb[...] = jnp.zeros(acc_cfb.shape, F32)
            dextg_ref[tm:tm + HALO_F, :] = jnp.zeros((HALO_F, W_UP_BLK), F32)
            dextv_ref[tm:tm + HALO_F, :] = jnp.zeros((HALO_F, W_UP_BLK), F32)

        w = [cfw_ref[0, 0], cfw_ref[1, 0]]
        dext = [dextg_ref, dextv_ref]
        dhu = [dhug_ref, dhuv_ref]

        def rows1(bi):
            r = _rows(bi)
            gate = gv_ref[0, 0, r, :].astype(F32)
            val = gv_ref[1, 0, r, :].astype(F32)
            sg = _sigmoid(gate)
            silu = gate * sg
            gbuf_ref[r, :] = (silu * val).astype(BF16)
            dg = dg_ref[r, :]
            dgate = dg * val * (sg * (1.0 + gate * (1.0 - sg)))
            dval = dg * silu
            dextg_ref[r, :] = dgate
            dextv_ref[r, :] = dval
            acc_cfb[0:8, :] += _rsum8(dgate)
            acc_cfb[8:16, :] += _rsum8(dval)

        def rows2(bi):
            r = _rows(bi)
            for g in range(2):
                win = dext[g][pl.ds(bi * ROWS, ROWS + HALO_F), :]
                n = ROWS + HALO_F
                later = [pltpu.roll(win, n - 2, 0)[0:ROWS, :], pltpu.roll(win, n - 1, 0)[0:ROWS, :],
                         win[0:ROWS, :]]
                d = sum(later[k] * w[g][k:k + 1, :] for k in range(KF))
                dhu[g][r, :] = d.astype(BF16)
                pre = hu_ref[g, 0, r, :]
                for k in range(KF):
                    r0 = 8 * (g * KF + k)
                    acc_cfw[r0:r0 + 8, :] += _rsum8(later[k] * pre)

        for sub in reversed(range(tm // sub_rows)):
            rs = slice(sub * sub_rows, (sub + 1) * sub_rows)
            blocks = range(sub * sub_rows // ROWS, (sub + 1) * sub_rows // ROWS)
            x1b_ref[rs, :] = (xh_ref[rs, :] * g1_ref[...] + b1_ref[...]).astype(BF16)
            drb_ref[rs, :] = dr2_ref[rs, :].astype(BF16)
            dg_ref[rs, :] = _nt(drb_ref[rs, :], wdown_ref[0])
            for bi in blocks:
                rows1(bi)
            for bi in blocks:
                rows2(bi)
            acc_wdown[...] += _tn(gbuf_ref[rs, :], drb_ref[rs, :])
            acc_wup[0] += _tn(dhug_ref[rs, :], x1b_ref[rs, :])
            acc_wup[1] += _tn(dhuv_ref[rs, :], x1b_ref[rs, :])
            dx1_ref[0, rs, :] = (_nt(dhug_ref[rs, :], wup_ref[0, 0])
                                 + _nt(dhuv_ref[rs, :], wup_ref[1, 0])).astype(BF16)
        dextg_ref[tm:tm + HALO_F, :] = dextg_ref[0:HALO_F, :]
        dextv_ref[tm:tm + HALO_F, :] = dextv_ref[0:HALO_F, :]

        @pl.when(i == nt - 1)
        def _():
            for g in range(2):
                dcfb_ref[g, 0] = jnp.sum(acc_cfb[8 * g:8 * g + 8, :], axis=0, keepdims=True)
                for k in range(KF):
                    r0 = 8 * (g * KF + k)
                    dcfw_ref[g, 0, k:k + 1, :] = jnp.sum(acc_cfw[r0:r0 + 8, :], axis=0, keepdims=True)
            cps = [pltpu.make_async_copy(acc_wup.at[0], dwup_ref.at[0, f], sem.at[0]),
                   pltpu.make_async_copy(acc_wup.at[1], dwup_ref.at[1, f], sem.at[1]),
                   pltpu.make_async_copy(acc_wdown, dwdown_ref.at[f], sem.at[2])]
            for cp in cps:
                cp.start()
            for cp in cps:
                cp.wait()
            down_copy(f).start()

            @pl.when(f % 2 != c)
            def _():
                up_copy(f, 0).start()
                up_copy(f, 1).start()

        @pl.when((i == nt - 1) & (pl.program_id(0) == N_F - 1))
        def _():
            for fi in range(N_F):
                down_copy(fi).wait()
                for g in range(2):
                    @pl.when(fi % 2 != c)
                    def _():
                        up_copy(fi, g).wait_send()

                    @pl.when(fi % 2 == c)
                    def _():
                        up_copy(fi, g).wait_recv()

    rev = lambda i: nt - 1 - i
    row = pl.BlockSpec((tm, D), lambda fo, i, o: (rev(i), 0))
    pair = lambda r, c: pl.BlockSpec((2, 1, r, c), lambda fo, i, o: (0, o[fo], 0, 0))
    tile = pl.BlockSpec((2, 1, tm, W_UP_BLK), lambda fo, i, o: (0, o[fo], rev(i), 0))
    return pl.pallas_call(
        body, name="ffn_backward",
        grid_spec=pltpu.PrefetchScalarGridSpec(
            num_scalar_prefetch=1, grid=(N_F, nt),
            in_specs=[row, row, _full(ln1_g.shape), _full(ln1_b.shape), tile, tile,
                      pair(D, W_UP_BLK), pair(KF, W_UP_BLK),
                      pl.BlockSpec((1, W_UP_BLK, D), lambda fo, i, o: (o[fo], 0, 0))],
            out_specs=[ANY, ANY, pair(KF, W_UP_BLK), pair(1, W_UP_BLK),
                       pl.BlockSpec((1, tm, D), lambda fo, i, o: (o[fo], rev(i), 0)), ANY, ANY],
            scratch_shapes=[pltpu.VMEM((tm, D), BF16), pltpu.VMEM((tm, D), BF16),
                            pltpu.VMEM((tm, W_UP_BLK), F32),
                            pltpu.VMEM((tm + HALO_F, W_UP_BLK), F32), pltpu.VMEM((tm + HALO_F, W_UP_BLK), F32),
                            pltpu.VMEM((tm, W_UP_BLK), BF16), pltpu.VMEM((tm, W_UP_BLK), BF16),
                            pltpu.VMEM((tm, W_UP_BLK), BF16),
                            pltpu.VMEM((2, W_UP_BLK, D), F32), pltpu.VMEM((W_UP_BLK, D), F32),
                            pltpu.VMEM((2 * KF * 8, W_UP_BLK), F32), pltpu.VMEM((16, W_UP_BLK), F32),
                            pltpu.SemaphoreType.DMA((3,)),
                            pltpu.SemaphoreType.DMA((N_F, 3)), pltpu.SemaphoreType.DMA((N_F, 3))]),
        out_shape=[jax.ShapeDtypeStruct((2, N_F, W_UP_BLK, D), F32),
                   jax.ShapeDtypeStruct((N_F, W_UP_BLK, D), F32),
                   jax.ShapeDtypeStruct((2, N_F, KF, W_UP_BLK), F32),
                   jax.ShapeDtypeStruct((2, N_F, 1, W_UP_BLK), F32),
                   jax.ShapeDtypeStruct((N_F, t, D), BF16),
                   jax.ShapeDtypeStruct((4, W_UP_BLK, D), F32),
                   jax.ShapeDtypeStruct((4, D_FF // N_DEV, D), F32)],
        compiler_params=_params(("arbitrary", "arbitrary")),
    )(order, dr2, xhat1, ln1_g, ln1_b, hu4, gv4, wup4, cfw4, wdown)


def mix_backward(x, h, yb1, dx1p, dr2, xhat1, rstd1, win_g, ln_a_g, ln_a_b, w_spatial, bst,
                 conv_b_w, ln_b_g, ln_b_b, wout, ln1_g, ffn_partials, tm):
    t = x.shape[0]
    n_p = len(ffn_partials)
    nt = t // tm
    n_chunks = tm // CHUNK
    halo_blocks = tm // HALO_B

    def body(x_ref, h_ref, halo_ref, yb1_ref, dx1p_ref, dr2_ref, xh1_ref, rstd1_ref, win_ref, ga_ref, ba_ref,
             ws_ref, bst_ref, cw_ref, gb_ref, bb_ref, wout_ref, g1_ref, *rest):
        p_refs, rest = rest[:n_p], rest[n_p:]
        gx_ref, dwin_ref, dwout_ref, dcw_ref, small_ref = rest[:5]
        land_refs, rest = rest[5:5 + n_p], rest[5 + n_p:]
        (ext_ref, dext_ref, y_ref, dy_ref, dh_ref, dmb_ref, wsm_ref,
         acc_win, acc_wout, acc_bin, acc_lnag, acc_lnab, acc_ws, acc_bs, acc_cbb, acc_lnbg,
         acc_lnbb, acc_bout, acc_ln1g, acc_ln1b, acc_cw, sem, send_sems, recv_sems) = rest
        i = pl.program_id(0)

        @pl.when(i == 0)
        def _():
            for cp in _chip_copies(p_refs, land_refs, send_sems, recv_sems):
                cp.start()

        first_tile = i == nt - 1
        accs = [acc_win, acc_wout, acc_bin, acc_lnag, acc_lnab, acc_ws, acc_bs, acc_cbb, acc_lnbg,
                acc_lnbb, acc_bout, acc_ln1g, acc_ln1b, acc_cw]

        @pl.when(i == 0)
        def _():
            for acc in accs:
                acc[...] = jnp.zeros(acc.shape, F32)
            dext_ref[tm:tm + HALO_B, :] = jnp.zeros((HALO_B, D_B), F32)
            mask = _tril_mask()
            for hd in range(HEADS):
                wsm_ref[hd] = jnp.where(mask, ws_ref[hd], 0.0).astype(BF16)

        def ln1_rows(bi):
            r = _rows(bi, LN_ROWS)
            part = [dx1p_ref[f, r, :].astype(F32) for f in range(N_F)]
            dx1 = ALPHA * dr2_ref[r, :] + ((part[0] + part[1]) + (part[2] + part[3]))
            xhat = xh1_ref[r, :]
            acc_ln1g[...] += _rsum8(dx1 * xhat)
            acc_ln1b[...] += _rsum8(dx1)
            dr1 = _ln_bwd(dx1 * g1_ref[...], xhat, rstd1_ref[r, 0:1])
            acc_bout[...] += _rsum8(dr1)
            gx_ref[r, :] = ALPHA * dr1
            dmb_ref[r, :] = dr1.astype(BF16)

        _loop(tm // LN_ROWS, ln1_rows)
        dy_ref[...] = _nt(dmb_ref[...], wout_ref[...])

        ha = halo_ref[:, 0:D_B]
        hg = halo_ref[:, D_B:2 * D_B]
        ext_ref[0:HALO_B, :] = jnp.where(first_tile, 0.0, 1.0) * (ha * _sigmoid(hg))

        def chunk(ci):
            r = _rows(ci, CHUNK)
            for hd in range(HEADS):
                sl = slice(hd * HEAD_DIM, (hd + 1) * HEAD_DIM)
                rows8 = slice(8 * hd, 8 * hd + 8)
                hus, hvs, u, cdf_u, cdf_v, xhat, rstd, vn, sv = _mixer_a_head(
                    h_ref, r, hd, ga_ref, ba_ref, wsm_ref, bst_ref)
                dy_a = dy_ref[r, sl]
                y_ref[r, sl] = (u * sv).astype(BF16)
                du = dy_a * sv
                dsv = dy_a * u
                dsvb = dsv.astype(BF16)
                acc_bs[hd] += dsv
                acc_ws[hd] += _nt(dsvb, vn)
                dvn = _tn(wsm_ref[hd], dsvb)
                acc_lnag[rows8, :] += _rsum8(dvn * xhat)
                acc_lnab[rows8, :] += _rsum8(dvn)
                dv = _ln_bwd(dvn * ga_ref[hd:hd + 1, :], xhat, rstd)
                slv = slice(D_A + hd * HEAD_DIM, D_A + (hd + 1) * HEAD_DIM)
                dhu = du * (cdf_u + hus * jnp.exp(-0.5 * hus * hus) * INV_SQRT_2PI)
                dhv = dv * (cdf_v + hvs * jnp.exp(-0.5 * hvs * hvs) * INV_SQRT_2PI)
                acc_bin[:, sl] += _rsum8(dhu)
                acc_bin[:, slv] += _rsum8(dhv)
                dh_ref[r, sl] = dhu.astype(BF16)
                dh_ref[r, slv] = dhv.astype(BF16)
            a_b = h_ref[r, 2 * D_A:2 * D_A + D_B]
            g_b = h_ref[r, 2 * D_A + D_B:D_IN]
            ext_ref[pl.ds(HALO_B + ci * CHUNK, CHUNK), :] = a_b * _sigmoid(g_b)

        _loop(n_chunks, chunk)

        def conv_rows(bi):
            base = bi * ROWS
            r = pl.ds(base, ROWS)
            xhat, rstd = _ln_stats(yb1_ref[r, :])
            yb2 = xhat * gb_ref[...] + bb_ref[...]
            sg = _sigmoid(yb2)
            y_ref[r, D_A:D] = (yb2 * sg).astype(BF16)
            dyb2 = dy_ref[r, D_A:D] * (sg * (1.0 + yb2 * (1.0 - sg)))
            acc_lnbg[...] += _rsum8(dyb2 * xhat)
            acc_lnbb[...] += _rsum8(dyb2)
            dyb1 = _ln_bwd(dyb2 * gb_ref[...], xhat, rstd)
            acc_cbb[...] += _rsum8(dyb1)
            dext_ref[r, :] = dyb1
            for k, tap in _taps(ext_ref[pl.ds(base, ROWS + HALO_B), :], CONV_B_OFFSETS):
                acc_cw[8 * k:8 * k + 8, :] += _rsum8(dyb1 * tap)

        _loop(tm // ROWS, conv_rows)

        def convt_rows(bi):
            base = bi * ROWS
            r = pl.ds(base, ROWS)
            dyb0 = jnp.zeros((ROWS, D_B), F32)
            for k, tap in _taps(dext_ref[pl.ds(base, ROWS + HALO_B), :], CONV_B_T_OFFSETS):
                dyb0 = dyb0 + tap * cw_ref[k:k + 1, :]
            a_b = h_ref[r, 2 * D_A:2 * D_A + D_B]
            sg = _sigmoid(h_ref[r, 2 * D_A + D_B:D_IN])
            da_b = dyb0 * sg
            dg_b = dyb0 * a_b * sg * (1.0 - sg)
            acc_bin[:, 2 * D_A:2 * D_A + D_B] += _rsum8(da_b)
            acc_bin[:, 2 * D_A + D_B:D_IN] += _rsum8(dg_b)
            dh_ref[r, 2 * D_A:2 * D_A + D_B] = da_b.astype(BF16)
            dh_ref[r, 2 * D_A + D_B:D_IN] = dg_b.astype(BF16)

        _loop(tm // ROWS, convt_rows)
        dext_ref[tm:tm + HALO_B, :] = dext_ref[0:HALO_B, :]

        acc_wout[...] += _tn(y_ref[...], dmb_ref[...])
        xt = x_ref[...].T.astype(BF16)
        dh_blocks = [dh_ref[:, j * W_IN_BLK:(j + 1) * W_IN_BLK] for j in range(N_DEV)]
        for j in range(N_DEV):
            acc_win[j] += _nn(xt, dh_blocks[j])
        gx_ref[...] += sum(_nt(dh_blocks[j], win_ref[j]) for j in range(N_DEV))

        @pl.when(i == nt - 1)
        def _():
            cps = [pltpu.make_async_copy(acc_win, dwin_ref, sem.at[0]),
                   pltpu.make_async_copy(acc_wout, dwout_ref, sem.at[1])]
            for cp in cps:
                cp.start()
            small_ref[...] = jnp.zeros(small_ref.shape, F32)

            def put_row_vector(row0, acc):
                vec = jnp.sum(acc[...], axis=0, keepdims=True)
                for k in range(vec.shape[1] // 128):
                    small_ref[row0 + k:row0 + k + 1, :] = vec[:, k * 128:(k + 1) * 128]

            put_row_vector(S_BIN, acc_bin)
            put_row_vector(S_CBB, acc_cbb)
            put_row_vector(S_LNBG, acc_lnbg)
            put_row_vector(S_LNBB, acc_lnbb)
            put_row_vector(S_BOUT, acc_bout)
            put_row_vector(S_LN1G, acc_ln1g)
            put_row_vector(S_LN1B, acc_ln1b)
            mask = _tril_mask()
            for hd in range(HEADS):
                rows8 = slice(8 * hd, 8 * hd + 8)
                small_ref[S_LNAG + hd:S_LNAG + hd + 1, :] = jnp.sum(acc_lnag[rows8, :], axis=0, keepdims=True)
                small_ref[S_LNAB + hd:S_LNAB + hd + 1, :] = jnp.sum(acc_lnab[rows8, :], axis=0, keepdims=True)
                small_ref[S_WS + hd * CHUNK:S_WS + (hd + 1) * CHUNK, :] = jnp.where(mask, acc_ws[hd], 0.0)
                small_ref[S_BS + hd:S_BS + hd + 1, :] = jnp.sum(acc_bs[hd].T, axis=0, keepdims=True)
            for k in range(KB):
                dcw_ref[k:k + 1, :] = jnp.sum(acc_cw[8 * k:8 * k + 8, :], axis=0, keepdims=True)
            for cp in cps:
                cp.wait()
            for cp in _chip_copies(p_refs, land_refs, send_sems, recv_sems):
                cp.wait()

    rev = lambda i: nt - 1 - i
    row = lambda w: pl.BlockSpec((tm, w), lambda i: (rev(i), 0))
    return pl.pallas_call(
        body, name="mix_backward", grid=(nt,),
        in_specs=[row(D), row(D_IN),
                  pl.BlockSpec((HALO_B, 2 * D_B), lambda i: (jnp.maximum(rev(i) * halo_blocks - 1, 0), 1)),
                  row(D_B), pl.BlockSpec((N_F, tm, D), lambda i: (0, rev(i), 0)),
                  row(D), row(D), row(128), _resident(win_g.shape), _full(ln_a_g.shape),
                  _full(ln_a_b.shape), _full(w_spatial.shape), _full(bst.shape), _full(conv_b_w.shape),
                  _full(ln_b_g.shape), _full(ln_b_b.shape),
                  _resident(wout.shape), _full(ln1_g.shape)] + [ANY] * n_p,
        out_specs=[row(D), ANY, ANY, _full((KB, D_B)), _full((S_MIX_ROWS, 128))] + [ANY] * n_p,
        out_shape=[jax.ShapeDtypeStruct((t, D), F32), jax.ShapeDtypeStruct((N_DEV, D, W_IN_BLK), F32),
                   jax.ShapeDtypeStruct((D, D), F32), jax.ShapeDtypeStruct((KB, D_B), F32),
                   jax.ShapeDtypeStruct((S_MIX_ROWS, 128), F32)]
        + [jax.ShapeDtypeStruct(p.shape, BF16) for p in ffn_partials],
        scratch_shapes=[pltpu.VMEM((tm + HALO_B, D_B), F32), pltpu.VMEM((tm + HALO_B, D_B), F32),
                        pltpu.VMEM((tm, D), BF16), pltpu.VMEM((tm, D), F32), pltpu.VMEM((tm, D_IN), BF16),
                        pltpu.VMEM((tm, D), BF16),
                        pltpu.VMEM((HEADS, CHUNK, CHUNK), BF16),
                        pltpu.VMEM((N_DEV, D, W_IN_BLK), F32), pltpu.VMEM((D, D), F32),
                        pltpu.VMEM((8, D_IN), F32), pltpu.VMEM((8 * HEADS, HEAD_DIM), F32),
                        pltpu.VMEM((8 * HEADS, HEAD_DIM), F32), pltpu.VMEM((HEADS, CHUNK, CHUNK), F32),
                        pltpu.VMEM((HEADS, CHUNK, CHUNK), F32), pltpu.VMEM((8, D_B), F32),
                        pltpu.VMEM((8, D_B), F32), pltpu.VMEM((8, D_B), F32), pltpu.VMEM((8, D), F32),
                        pltpu.VMEM((8, D), F32), pltpu.VMEM((8, D), F32), pltpu.VMEM((8 * KB, D_B), F32),
                        pltpu.SemaphoreType.DMA((2,)),
                        pltpu.SemaphoreType.DMA((n_p, 3)), pltpu.SemaphoreType.DMA((n_p, 3))],
        compiler_params=_params(("arbitrary",)),
    )(x, h, h, yb1, dx1p, dr2, xhat1, rstd1, win_g, ln_a_g, ln_a_b, w_spatial, bst, conv_b_w,
      ln_b_g, ln_b_b, wout, ln1_g, *ffn_partials)


def _rows128(a):
    return a.reshape(-1, 128)


def _pack_conv(cb, cf):
    lead = cb.shape[:-2]
    pad = [(0, 0)] * len(lead)
    flat = jnp.pad(cb.reshape(lead + (KB * 64,)), pad + [(0, 3 * W_UP_BLK - KB * 64)])
    rows = jnp.concatenate([cf, flat.reshape(lead + (3, W_UP_BLK))], axis=-2)
    return jnp.pad(rows, pad + [(0, 2), (0, 768 - W_UP_BLK)])


def _unpack_conv(p):
    lead = p.shape[:-2]
    cf = p[..., 0:KF, 0:W_UP_BLK]
    cb = p[..., 3:6, 0:W_UP_BLK].reshape(lead + (3 * W_UP_BLK,))[..., :KB * 64].reshape(lead + (KB, 64))
    return cb, cf


def kernel(x, w_in, b_in, ln_a_g, ln_a_b, w_spatial, b_spatial, conv_b_w, conv_b_b, ln_b_g, ln_b_b, w_out, b_out, ln1_g, ln1_b, w_up, conv_f_w, conv_f_b, w_down, ln2_g, ln2_b, loss_target, m_w_in, m_b_in, m_ln_a_g, m_ln_a_b, m_w_spatial, m_b_spatial, m_conv_b_w, m_conv_b_b, m_ln_b_g, m_ln_b_b, m_w_out, m_b_out, m_ln1_g, m_ln1_b, m_w_up, m_conv_f_w, m_conv_f_b, m_w_down, m_ln2_g, m_ln2_b, v_w_in, v_b_in, v_ln_a_g, v_ln_a_b, v_w_spatial, v_b_spatial, v_conv_b_w, v_conv_b_b, v_ln_b_g, v_ln_b_b, v_w_out, v_b_out, v_ln1_g, v_ln1_b, v_w_up, v_conv_f_w, v_conv_f_b, v_w_down, v_ln2_g, v_ln2_b):
    t = x.shape[1]
    x2 = x.reshape(t, D)
    target = loss_target.reshape(t, D)
    tm_fwd = min(t, 512)
    tm_bwd = min(t, 256)
    tm_ffn_bwd = min(t, 512)

    xi, yi, ci = _mesh_pos()
    jidx = jnp.stack([_lid(px, py, ci) for px, py in _chip_patterns(xi, yi)]).astype(jnp.int32)

    win_g, wout_g, conv_g, sup, sdown = all_gather_mixer_weights(
        w_in, w_out, w_up.T, w_down, _pack_conv(conv_b_w, conv_f_w))
    wout_full = wout_g.reshape(D, D)
    conv_b_all, cfw = _unpack_conv(conv_g)
    conv_b_full = conv_b_all.transpose(1, 0, 2).reshape(KB, D_B)
    cfb = conv_f_b.reshape(N_DEV, W_UP_BLK)
    row = lambda a: a.reshape(1, -1)
    bst = b_spatial.T

    h, xhat1, rstd1, yb1, wup_g, wdown_g = mix_forward(
        x2, win_g, row(b_in), ln_a_g, ln_a_b, w_spatial, bst, conv_b_full, row(conv_b_b),
        row(ln_b_g), row(ln_b_b), wout_full, row(b_out), row(ln1_g), row(ln1_b), sup, sdown, tm_fwd)
    wdown4 = wdown_g.reshape(N_F, W_UP_BLK, D)
    hu, gv, dr2, loss_part, s_ln2 = ffn_forward(
        xhat1, row(ln1_g), row(ln1_b), wup_g, cfw, cfb, wdown4, row(ln2_g), row(ln2_b), target, tm_bwd)

    order = jnp.where(ci == 0, jnp.array([1, 3, 0, 2], jnp.int32), jnp.array([0, 2, 1, 3], jnp.int32))
    dwup, dwdown, dcfw, dcfb, dx1p, *ffn_lands = ffn_backward(
        order, dr2, xhat1, row(ln1_g), row(ln1_b), hu, gv, wup_g, cfw, wdown4, tm_ffn_bwd)
    ffn_grads = [dwup.reshape(N_DEV, W_UP_BLK, D), dwdown.reshape(N_DEV, D_FF // N_DEV, D)]
    ffn_partials = [chip_partials("chip_partials_" + nm, g, l, jidx, rb)
                    for nm, g, l, rb in zip(["w_up", "w_down"], ffn_grads, ffn_lands, [352, 352])]
    grad_x, dwin, dwout, dcw, s_mix, *ffn_recvs = mix_backward(
        x2, h, yb1, dx1p, dr2, xhat1, rstd1, win_g, ln_a_g, ln_a_b, w_spatial, bst,
        conv_b_full, row(ln_b_g), row(ln_b_b), wout_full, row(ln1_g), ffn_partials, tm_bwd)

    dcfb_rows = jnp.pad(dcfb.reshape(-1, 128), ((0, 4), (0, 0)))
    svec = jnp.concatenate([s_mix, dcfb_rows, s_ln2, loss_part], axis=0)
    dconv = _pack_conv(dcw.reshape(KB, N_DEV, 64).transpose(1, 0, 2), dcfw.reshape(N_DEV, KF, W_UP_BLK))
    mix_grads = [dwin, dwout.reshape(N_DEV, D // N_DEV, D), dconv]
    mix_w = [w_in, w_out, _pack_conv(conv_b_w, conv_f_w)]
    mix_m = [m_w_in, m_w_out, _pack_conv(m_conv_b_w, m_conv_f_w)]
    mix_v = [v_w_in, v_w_out, _pack_conv(v_conv_b_w, v_conv_f_w)]
    *mix_out, sv_slots = mixer_reduce_adamw(mix_grads, svec, mix_w, mix_m, mix_v)
    big = {nm: [mix_out[k * 3 + p] for k in range(4)] for p, nm in enumerate(["w_in", "w_out", "conv"])}

    ffn_w = [(w_up.T, m_w_up.T, v_w_up.T), (w_down, m_w_down, v_w_down)]
    for nm, g, l, r, (w, m, v) in zip(["w_up", "w_down"], ffn_grads, ffn_lands, ffn_recvs, ffn_w):
        big[nm] = reduce_and_adamw("reduce_adamw_" + nm, g, l, r, w, m, v, jidx, 352)
    big["w_up"] = [o.T for o in big["w_up"]]
    for k in range(4):
        cb_k, cf_k = _unpack_conv(big["conv"][k])
        big.setdefault("conv_b_w", []).append(cb_k)
        big.setdefault("conv_f_w", []).append(cf_k)

    small_w = dict(b_in=b_in, ln_a_g=ln_a_g, ln_a_b=ln_a_b, w_spatial=w_spatial, b_spatial=b_spatial,
                   conv_b_b=conv_b_b, ln_b_g=ln_b_g, ln_b_b=ln_b_b, b_out=b_out, ln1_g=ln1_g,
                   ln1_b=ln1_b, conv_f_b=conv_f_b, ln2_g=ln2_g, ln2_b=ln2_b)
    small_m = dict(b_in=m_b_in, ln_a_g=m_ln_a_g, ln_a_b=m_ln_a_b, w_spatial=m_w_spatial,
                   b_spatial=m_b_spatial, conv_b_b=m_conv_b_b, ln_b_g=m_ln_b_g, ln_b_b=m_ln_b_b,
                   b_out=m_b_out, ln1_g=m_ln1_g, ln1_b=m_ln1_b, conv_f_b=m_conv_f_b, ln2_g=m_ln2_g,
                   ln2_b=m_ln2_b)
    small_v = dict(b_in=v_b_in, ln_a_g=v_ln_a_g, ln_a_b=v_ln_a_b, w_spatial=v_w_spatial,
                   b_spatial=v_b_spatial, conv_b_b=v_conv_b_b, ln_b_g=v_ln_b_g, ln_b_b=v_ln_b_b,
                   b_out=v_b_out, ln1_g=v_ln1_g, ln1_b=v_ln1_b, conv_f_b=v_conv_f_b, ln2_g=v_ln2_g,
                   ln2_b=v_ln2_b)
    order = [nm for nm, _, _ in SMALL_LAYOUT]
    small_out = small_adamw(sv_slots, [_rows128(small_w[nm]) for nm in order],
                            [_rows128(small_m[nm]) for nm in order], [_rows128(small_v[nm]) for nm in order])
    n_small = len(order)
    small = {nm: [small_out[k * n_small + p].reshape(small_w[nm].shape) for k in range(4)]
             for p, nm in enumerate(order)}
    loss = jnp.sum(small_out[4 * n_small]) * (0.5 / D)

    weights = ["w_in", "b_in", "ln_a_g", "ln_a_b", "w_spatial", "b_spatial", "conv_b_w", "conv_b_b",
               "ln_b_g", "ln_b_b", "w_out", "b_out", "ln1_g", "ln1_b", "w_up", "conv_f_w", "conv_f_b",
               "w_down", "ln2_g", "ln2_b"]
    result = lambda nm, k: big[nm][k] if nm in big else small[nm][k]
    return (loss, grad_x.reshape(x.shape), *[result(nm, 0) for nm in weights],
            *[result(nm, 1) for nm in weights], *[result(nm, 2) for nm in weights],
            *[result(nm, 3) for nm in weights])
```

```python
import functools
import math

import jax
import jax.numpy as jnp
from jax import lax
from jax.experimental import pallas as pl
from jax.experimental.pallas import tpu as pltpu

F32 = jnp.float32
BF16 = jnp.bfloat16

D = 1024
D_A = 512
D_B = 512
HEADS = 4
HEAD_DIM = 128
CHUNK = 128
KB = 31
KF = 3
D_FF = 2816
D_IN = 2048
N_DEV = 8
W_IN_BLK = D_IN // N_DEV
W_UP_BLK = 2 * D_FF // N_DEV
N_F = 4
LN_EPS = 1e-5
ALPHA = 2.0 ** 0.25

ADAM_LR = 0.001
ADAM_B1 = 0.9
ADAM_B2 = 0.999
ADAM_EPS = 1e-08
ADAM_WD = 0.01
ADAM_STEP = 10

INV_SQRT2 = 1.0 / math.sqrt(2.0)
INV_SQRT_2PI = 1.0 / math.sqrt(2.0 * math.pi)

HALO_B = 32
HALO_F = 8
ROWS = 64
LN_ROWS = 32
VMEM_LIMIT = 58 * 1024 * 1024

MESH = pl.DeviceIdType.MESH
ANY = pl.BlockSpec(memory_space=pl.ANY)
VMEM = pl.BlockSpec(memory_space=pltpu.VMEM)

S_BIN, S_LNAG, S_LNAB, S_WS, S_BS, S_CBB, S_LNBG, S_LNBB, S_BOUT, S_LN1G, S_LN1B = (
    0, 16, 24, 32, 544, 552, 560, 568, 576, 584, 592)
S_MIX_ROWS = 600
S_CFB = 600
S_LN2G = 648
S_LN2B = 656
S_LOSS = 664
S_ROWS = 672


def _tn(a, b):
    return lax.dot_general(a, b, (((0,), (0,)), ((), ())), preferred_element_type=F32)


def _nt(a, b):
    return lax.dot_general(a, b, (((1,), (1,)), ((), ())), preferred_element_type=F32)


def _nn(a, b):
    return jnp.dot(a, b, preferred_element_type=F32)


def _sigmoid(x):
    return 1.0 / (1.0 + jnp.exp(-x))


def _ln_stats(x):
    mu = jnp.mean(x, axis=-1, keepdims=True)
    xc = x - mu
    var = jnp.mean(xc * xc, axis=-1, keepdims=True)
    rstd = lax.rsqrt(var + LN_EPS)
    return xc * rstd, rstd


def _ln_bwd(dxhat, xhat, rstd):
    m1 = jnp.mean(dxhat, axis=-1, keepdims=True)
    m2 = jnp.mean(dxhat * xhat, axis=-1, keepdims=True)
    return rstd * (dxhat - m1 - xhat * m2)


def _rsum8(x):
    r, n = x.shape
    return x.reshape(r // 8, 8, n).sum(axis=0)


def _rows(i, n=ROWS):
    return pl.ds(i * n, n)


def _loop(n, body):
    for i in range(n):
        body(i)


def _tril_mask():
    r = lax.broadcasted_iota(jnp.int32, (CHUNK, CHUNK), 0)
    c = lax.broadcasted_iota(jnp.int32, (CHUNK, CHUNK), 1)
    return c <= r


def _mixer_a_head(h_ref, r, hd, ga_ref, ba_ref, wsm_ref, bst_ref):
    sl = slice(hd * HEAD_DIM, (hd + 1) * HEAD_DIM)
    hu = h_ref[r, sl]
    hv = h_ref[r, D_A + hd * HEAD_DIM:D_A + (hd + 1) * HEAD_DIM]
    cdf_u = 0.5 * (1.0 + lax.erf(hu * INV_SQRT2))
    cdf_v = 0.5 * (1.0 + lax.erf(hv * INV_SQRT2))
    u = hu * cdf_u
    xhat, rstd = _ln_stats(hv * cdf_v)
    vn = (xhat * ga_ref[hd:hd + 1, :] + ba_ref[hd:hd + 1, :]).astype(BF16)
    sv = _nn(wsm_ref[hd], vn) + bst_ref[:, hd:hd + 1]
    return hu, hv, u, cdf_u, cdf_v, xhat, rstd, vn, sv


def _taps(win, offsets):
    n = win.shape[0]
    for s in range(8):
        ks = [k for k, o in enumerate(offsets) if o % 8 == s]
        if ks:
            moved = win if s == 0 else pltpu.roll(win, n - s, 0)
            for k in ks:
                yield k, moved[offsets[k] - s:offsets[k] - s + ROWS, :]


CONV_B_OFFSETS = [2 + k for k in range(KB)]
CONV_B_T_OFFSETS = [30 - k for k in range(KB)]


def _conv_b_block(ext_ref, base, cw_ref):
    acc = jnp.zeros((ROWS, D_B), F32)
    for k, tap in _taps(ext_ref[pl.ds(base, ROWS + HALO_B), :], CONV_B_OFFSETS):
        acc = acc + tap * cw_ref[k:k + 1, :]
    return acc


def _taps_f(win):
    n = ROWS + HALO_F
    return [pltpu.roll(win, n - 6, 0)[0:ROWS, :], pltpu.roll(win, n - 7, 0)[0:ROWS, :], win[8:n, :]]


def _params(sem, **kw):
    return pltpu.CompilerParams(dimension_semantics=sem, vmem_limit_bytes=VMEM_LIMIT, **kw)


def _resident(shape):
    zeros = (0,) * len(shape)
    return pl.BlockSpec(shape, lambda *_: zeros, pipeline_mode=pl.Buffered(1))


def _full(shape):
    zeros = (0,) * len(shape)
    return pl.BlockSpec(shape, lambda *_: zeros)


def _mesh_pos():
    return lax.axis_index("x"), lax.axis_index("y"), lax.axis_index("c")


def _chip_patterns(x, y):
    return [(x, y), (1 - x, y), (x, 1 - y), (1 - x, 1 - y)]


def _lid(x, y, c):
    return 4 * x + 2 * y + c


def _gather_copy(outs, send_sems, recv_sems, a, k, block, to, src=None):
    blk = outs[a].at[_lid(*block)]
    return pltpu.make_async_remote_copy(
        src_ref=blk if src is None else src, dst_ref=blk,
        send_sem=send_sems.at[a, k], recv_sem=recv_sems.at[a, k], device_id=to, device_id_type=MESH)


def _gather_start(mine, outs, send_sems, recv_sems, local_sems):
    x, y, c = _mesh_pos()
    me, sib = (x, y, c), (x, y, 1 - c)
    for a in range(len(mine)):
        pltpu.make_async_copy(mine[a], outs[a].at[_lid(*me)], local_sems.at[a]).start()
        _gather_copy(outs, send_sems, recv_sems, a, 0, me, sib, src=mine[a]).start()
        for j, chip in enumerate(_chip_patterns(x, y)[1:]):
            _gather_copy(outs, send_sems, recv_sems, a, 1 + j, me, (*chip, c), src=mine[a]).start()


def _gather_finish(mine, outs, send_sems, recv_sems, local_sems):
    x, y, c = _mesh_pos()
    me, sib = (x, y, c), (x, y, 1 - c)
    chips = _chip_patterns(x, y)[1:]
    n = len(mine)
    copy = functools.partial(_gather_copy, outs, send_sems, recv_sems)
    passed = []
    for j, chip in enumerate(chips):
        for a in range(n):
            copy(a, 1 + j, (*chip, c), me).wait_recv()
            cp = copy(a, 4 + j, (*chip, c), sib)
            cp.start()
            passed.append(cp)
    for a in range(n):
        copy(a, 0, sib, me).wait_recv()
        for j, chip in enumerate(chips):
            copy(a, 4 + j, (*chip, 1 - c), me).wait_recv()
        for k in range(4):
            copy(a, k, me, sib, src=mine[a]).wait_send()
        pltpu.make_async_copy(mine[a], outs[a].at[_lid(*me)], local_sems.at[a]).wait()
    for cp in passed:
        cp.wait_send()


def _gather_scratch(n):
    return [pltpu.SemaphoreType.DMA((n, 7)), pltpu.SemaphoreType.DMA((n, 7)), pltpu.SemaphoreType.DMA((n,))]


def all_gather_mixer_weights(w_in, w_out, w_up, w_down, convp):
    srcs = [w_in, w_out, convp]
    n = len(srcs)

    def body(win_ref, wout_ref, convp_ref, wup_ref, wdown_ref,
             gin_ref, gout_ref, gconv_ref, sup_ref, sdown_ref,
             sin_ref, sout_ref, send_sems, recv_sems, local_sems):
        sin_ref[...] = win_ref[...].astype(BF16)
        sout_ref[...] = wout_ref[...].astype(BF16)
        mine = [sin_ref, sout_ref, convp_ref]
        outs = [gin_ref, gout_ref, gconv_ref]
        _gather_start(mine, outs, send_sems, recv_sems, local_sems)
        sup_ref[...] = wup_ref[...].T.astype(BF16)
        sdown_ref[...] = wdown_ref[...].astype(BF16)
        _gather_finish(mine, outs, send_sems, recv_sems, local_sems)

    return pl.pallas_call(
        body, name="all_gather_mixer_weights",
        out_shape=[jax.ShapeDtypeStruct((N_DEV,) + w_in.shape, BF16),
                   jax.ShapeDtypeStruct((N_DEV,) + w_out.shape, BF16),
                   jax.ShapeDtypeStruct((N_DEV,) + convp.shape, F32),
                   jax.ShapeDtypeStruct(w_up.shape[::-1], BF16), jax.ShapeDtypeStruct(w_down.shape, BF16)],
        in_specs=[VMEM] * 5, out_specs=[ANY] * n + [VMEM, VMEM],
        scratch_shapes=[pltpu.VMEM(w_in.shape, BF16), pltpu.VMEM(w_out.shape, BF16)] + _gather_scratch(n),
        compiler_params=pltpu.CompilerParams(vmem_limit_bytes=VMEM_LIMIT),
    )(w_in, w_out, convp, w_up, w_down)


def _chip_copies(p, land, send_sems, recv_sems):
    x, y, c = _mesh_pos()
    return [pltpu.make_async_remote_copy(
        src_ref=p[a].at[k], dst_ref=land[a].at[k], send_sem=send_sems.at[a, k], recv_sem=recv_sems.at[a, k],
        device_id=(px, py, c), device_id_type=MESH)
        for k, (px, py) in enumerate(_chip_patterns(x, y)[1:]) for a in range(len(p))]


def chip_partials(name, g, land, jidx, rb):
    _, r, c = g.shape

    def body(j_ref, g_ref, l_ref, o_ref):
        o_ref[...] = (g_ref[...] + l_ref[...]).astype(BF16)

    return pl.pallas_call(
        body, name=name,
        out_shape=jax.ShapeDtypeStruct((3, r, c), BF16),
        grid_spec=pltpu.PrefetchScalarGridSpec(
            num_scalar_prefetch=1, grid=(3, r // rb),
            in_specs=[pl.BlockSpec((1, rb, c), lambda k, i, j: (j[1 + k], i, 0)),
                      pl.BlockSpec((1, rb, c), lambda k, i, j: (1 + k, i, 0))],
            out_specs=pl.BlockSpec((1, rb, c), lambda k, i, j: (k, i, 0))),
        compiler_params=_params(("arbitrary", "arbitrary")),
    )(jidx, g, land)


def _adamw(w, g, m, v):
    m2 = ADAM_B1 * m + (1.0 - ADAM_B1) * g
    v2 = ADAM_B2 * v + (1.0 - ADAM_B2) * (g * g)
    m_hat = m2 / (1.0 - ADAM_B1 ** ADAM_STEP)
    v_hat = v2 / (1.0 - ADAM_B2 ** ADAM_STEP)
    delta = -ADAM_LR * (m_hat / (jnp.sqrt(v_hat) + ADAM_EPS) + ADAM_WD * w)
    return delta, m2, v2


def reduce_and_adamw(name, g, land, recv, w, m, v, jidx, rb):
    _, r, c = g.shape

    def body(j_ref, g_ref, l_ref, r_ref, w_ref, m_ref, v_ref, go_ref, do_ref, mo_ref, vo_ref):
        grad = (g_ref[0] + l_ref[0]) + r_ref[0].astype(F32) + r_ref[1].astype(F32) + r_ref[2].astype(F32)
        delta, m2, v2 = _adamw(w_ref[...], grad, m_ref[...], v_ref[...])
        go_ref[...] = grad
        do_ref[...] = delta
        mo_ref[...] = m2
        vo_ref[...] = v2

    blk = pl.BlockSpec((rb, c), lambda i, j: (i, 0))
    return pl.pallas_call(
        body, name=name,
        out_shape=[jax.ShapeDtypeStruct((r, c), F32)] * 4,
        grid_spec=pltpu.PrefetchScalarGridSpec(
            num_scalar_prefetch=1, grid=(r // rb,),
            in_specs=[pl.BlockSpec((1, rb, c), lambda i, j: (j[0], i, 0)),
                      pl.BlockSpec((1, rb, c), lambda i, j: (0, i, 0)),
                      pl.BlockSpec((3, rb, c), lambda i, j: (0, i, 0)),
                      blk, blk, blk],
            out_specs=[blk] * 4),
        compiler_params=_params(("arbitrary",)),
    )(jidx, g, land, recv, w, m, v)


def mixer_reduce_adamw(grads, svec, ws, ms, vs):
    n = len(grads)
    shard = [g.shape[1:] for g in grads]

    def body(*refs):
        g = refs[:n]
        sv_ref = refs[n]
        w, m, v = refs[n + 1:2 * n + 1], refs[2 * n + 1:3 * n + 1], refs[3 * n + 1:4 * n + 1]
        outs = refs[4 * n + 1:8 * n + 1]
        sv_slots = refs[8 * n + 1]
        rest = refs[8 * n + 2:]
        own, land, sendb, recvb = rest[:n], rest[n:2 * n], rest[2 * n:3 * n], rest[3 * n:4 * n]
        sv_land, chip_sv, d2d_send, d2d_recv, ici_send, ici_recv, local_sems, sv_sems = rest[4 * n:]
        x, y, c = _mesh_pos()
        sib = (x, y, 1 - c)
        pats = _chip_patterns(x, y)
        q = 2 * x + y

        d2d, local = [], []
        for a in range(n):
            for k, (px, py) in enumerate(pats):
                d2d.append(pltpu.make_async_remote_copy(
                    src_ref=g[a].at[_lid(px, py, 1 - c)], dst_ref=land[a].at[k],
                    send_sem=d2d_send.at[a, k], recv_sem=d2d_recv.at[a, k], device_id=sib, device_id_type=MESH))
                local.append(pltpu.make_async_copy(g[a].at[_lid(px, py, c)], own[a].at[k], local_sems.at[a, k]))
        d2d.append(pltpu.make_async_remote_copy(
            src_ref=sv_ref, dst_ref=sv_land, send_sem=d2d_send.at[n, 0], recv_sem=d2d_recv.at[n, 0],
            device_id=sib, device_id_type=MESH))
        for cp in d2d + local:
            cp.start()
        for cp in local + d2d:
            cp.wait()

        for a in range(n):
            for k in range(3):
                sendb[a][k] = (own[a][1 + k] + land[a][1 + k]).astype(BF16)
        chip_sv[...] = sv_ref[...] + sv_land[...]
        ici = _chip_copies(sendb, recvb, ici_send, ici_recv)
        sv_local = pltpu.make_async_copy(chip_sv, sv_slots.at[q], sv_sems.at[0])
        sv_out = [pltpu.make_async_remote_copy(
            src_ref=chip_sv, dst_ref=sv_slots.at[q], send_sem=sv_sems.at[1 + k], recv_sem=sv_sems.at[4 + k],
            device_id=(px, py, c), device_id_type=MESH) for k, (px, py) in enumerate(pats[1:])]
        for cp in ici + sv_out + [sv_local]:
            cp.start()
        for cp in ici:
            cp.wait()
        for k, (px, py) in enumerate(pats[1:]):
            sv_out[k].wait_send()
            pltpu.make_async_remote_copy(
                src_ref=chip_sv, dst_ref=sv_slots.at[2 * px + py], send_sem=sv_sems.at[1 + k],
                recv_sem=sv_sems.at[4 + k], device_id=(px, py, c), device_id_type=MESH).wait_recv()
        sv_local.wait()

        for a in range(n):
            grad = ((own[a][0] + land[a][0]) + recvb[a][0].astype(F32) + recvb[a][1].astype(F32)
                    + recvb[a][2].astype(F32))
            delta, m2, v2 = _adamw(w[a][...], grad, m[a][...], v[a][...])
            outs[a][...] = grad
            outs[n + a][...] = delta
            outs[2 * n + a][...] = m2
            outs[3 * n + a][...] = v2

    shard_out = [jax.ShapeDtypeStruct(s, F32) for s in shard]
    return pl.pallas_call(
        body, name="mixer_reduce_adamw",
        out_shape=shard_out * 4 + [jax.ShapeDtypeStruct((4,) + svec.shape, F32)],
        in_specs=[ANY] * n + [VMEM] * (1 + 3 * n), out_specs=[VMEM] * (4 * n) + [ANY],
        scratch_shapes=[pltpu.VMEM((4,) + s, F32) for s in shard] + [pltpu.VMEM((4,) + s, F32) for s in shard]
        + [pltpu.VMEM((3,) + s, BF16) for s in shard] + [pltpu.VMEM((3,) + s, BF16) for s in shard]
        + [pltpu.VMEM(svec.shape, F32), pltpu.VMEM(svec.shape, F32),
           pltpu.SemaphoreType.DMA((n + 1, 4)), pltpu.SemaphoreType.DMA((n + 1, 4)),
           pltpu.SemaphoreType.DMA((n, 3)), pltpu.SemaphoreType.DMA((n, 3)),
           pltpu.SemaphoreType.DMA((n, 4)), pltpu.SemaphoreType.DMA((7,))],
        compiler_params=pltpu.CompilerParams(vmem_limit_bytes=VMEM_LIMIT),
    )(*grads, svec, *ws, *ms, *vs)


SMALL_LAYOUT = [
    ("b_in", S_BIN, 16), ("ln_a_g", S_LNAG, 4), ("ln_a_b", S_LNAB, 4), ("w_spatial", S_WS, 512),
    ("b_spatial", S_BS, 4), ("conv_b_b", S_CBB, 4), ("ln_b_g", S_LNBG, 4), ("ln_b_b", S_LNBB, 4),
    ("b_out", S_BOUT, 8), ("ln1_g", S_LN1G, 8), ("ln1_b", S_LN1B, 8), ("conv_f_b", S_CFB, 44),
    ("ln2_g", S_LN2G, 8), ("ln2_b", S_LN2B, 8),
]


def small_adamw(sv_slots, ws, ms, vs):
    n = len(SMALL_LAYOUT)

    def body(*refs):
        s_ref = refs[0]
        w_refs, m_refs, v_refs = refs[1:1 + n], refs[1 + n:1 + 2 * n], refs[1 + 2 * n:1 + 3 * n]
        outs = refs[1 + 3 * n:]
        for p, (_, row0, rows) in enumerate(SMALL_LAYOUT):
            sl = pl.ds(row0, rows)
            grad = ((s_ref[0, sl, :] + s_ref[1, sl, :]) + s_ref[2, sl, :]) + s_ref[3, sl, :]
            delta, m2, v2 = _adamw(w_refs[p][...], grad, m_refs[p][...], v_refs[p][...])
            outs[p][...] = grad
            outs[n + p][...] = delta
            outs[2 * n + p][...] = m2
            outs[3 * n + p][...] = v2
        sl = pl.ds(S_LOSS, 8)
        outs[4 * n][...] = ((s_ref[0, sl, :] + s_ref[1, sl, :]) + s_ref[2, sl, :]) + s_ref[3, sl, :]

    shapes = [jax.ShapeDtypeStruct((rows, 128), F32) for _, _, rows in SMALL_LAYOUT]
    return pl.pallas_call(
        body, name="small_adamw", out_shape=shapes * 4 + [jax.ShapeDtypeStruct((8, 128), F32)],
        in_specs=[VMEM] * (1 + 3 * n), out_specs=[VMEM] * (4 * n + 1),
    )(sv_slots, *ws, *ms, *vs)


def mix_forward(x, win_g, b_in, ln_a_g, ln_a_b, w_spatial, bst, conv_b_w, conv_b_b, ln_b_g, ln_b_b,
                wout, b_out, ln1_g, ln1_b, sup, sdown, tm):
    t = x.shape[0]
    nt = t // tm
    n_chunks = tm // CHUNK

    def body(x_ref, win_ref, bin_ref, ga_ref, ba_ref, ws_ref, bst_ref, cw_ref, cb_ref, gb_ref,
             bb_ref, wout_ref, bout_ref, g1_ref, b1_ref, sup_ref, sdown_ref,
             h_ref, xhat1_ref, rstd1_ref, yb1_ref, gup_ref, gdown_ref,
             ext_ref, y_ref, wsm_ref, send_sems, recv_sems, local_sems):
        i = pl.program_id(0)
        gather = ([sup_ref, sdown_ref], [gup_ref, gdown_ref], send_sems, recv_sems, local_sems)

        @pl.when(i == 0)
        def _():
            _gather_start(*gather)
            ext_ref[0:HALO_B, :] = jnp.zeros((HALO_B, D_B), F32)
            mask = _tril_mask()
            for hd in range(HEADS):
                wsm_ref[hd] = jnp.where(mask, ws_ref[hd], 0.0).astype(BF16)

        xb = x_ref[...].astype(BF16)
        for j in range(N_DEV):
            cols = slice(j * W_IN_BLK, (j + 1) * W_IN_BLK)
            h_ref[:, cols] = _nn(xb, win_ref[j]) + bin_ref[:, cols]

        def chunk(ci):
            r = _rows(ci, CHUNK)
            for hd in range(HEADS):
                _, _, u, _, _, _, _, _, sv = _mixer_a_head(h_ref, r, hd, ga_ref, ba_ref, wsm_ref, bst_ref)
                y_ref[r, hd * HEAD_DIM:(hd + 1) * HEAD_DIM] = (u * sv).astype(BF16)
            a_b = h_ref[r, 2 * D_A:2 * D_A + D_B]
            g_b = h_ref[r, 2 * D_A + D_B:D_IN]
            ext_ref[pl.ds(HALO_B + ci * CHUNK, CHUNK), :] = a_b * _sigmoid(g_b)

        _loop(n_chunks, chunk)

        def conv_rows(bi):
            base = bi * ROWS
            yb1 = _conv_b_block(ext_ref, base, cw_ref) + cb_ref[...]
            yb1_ref[pl.ds(base, ROWS), :] = yb1
            xhat, _ = _ln_stats(yb1)
            yb2 = xhat * gb_ref[...] + bb_ref[...]
            y_ref[pl.ds(base, ROWS), D_A:D] = (yb2 * _sigmoid(yb2)).astype(BF16)

        _loop(tm // ROWS, conv_rows)
        ext_ref[0:HALO_B, :] = ext_ref[tm:tm + HALO_B, :]

        mix = _nn(y_ref[...], wout_ref[...]) + bout_ref[...]
        xhat1, rstd1 = _ln_stats(ALPHA * x_ref[...] + mix)
        xhat1_ref[...] = xhat1
        rstd1_ref[...] = jnp.broadcast_to(rstd1, (tm, 128))

        @pl.when(i == nt - 1)
        def _():
            _gather_finish(*gather)

    row = lambda w: pl.BlockSpec((tm, w), lambda i: (i, 0))
    return pl.pallas_call(
        body, name="mix_forward", grid=(nt,),
        in_specs=[row(D), _resident(win_g.shape), _full(b_in.shape), _full(ln_a_g.shape),
                  _full(ln_a_b.shape), _full(w_spatial.shape), _full(bst.shape),
                  _full(conv_b_w.shape), _full(conv_b_b.shape), _full(ln_b_g.shape),
                  _full(ln_b_b.shape), _resident(wout.shape), _full(b_out.shape),
                  _full(ln1_g.shape), _full(ln1_b.shape), ANY, ANY],
        out_specs=[row(D_IN), row(D), row(128), row(D_B), ANY, ANY],
        out_shape=[jax.ShapeDtypeStruct((t, D_IN), F32), jax.ShapeDtypeStruct((t, D), F32),
                   jax.ShapeDtypeStruct((t, 128), F32), jax.ShapeDtypeStruct((t, D_B), F32),
                   jax.ShapeDtypeStruct((N_DEV,) + sup.shape, BF16),
                   jax.ShapeDtypeStruct((N_DEV,) + sdown.shape, BF16)],
        scratch_shapes=[pltpu.VMEM((tm + HALO_B, D_B), F32), pltpu.VMEM((tm, D), BF16),
                        pltpu.VMEM((HEADS, CHUNK, CHUNK), BF16)] + _gather_scratch(2),
        compiler_params=_params(("arbitrary",)),
    )(x, win_g, b_in, ln_a_g, ln_a_b, w_spatial, bst, conv_b_w, conv_b_b, ln_b_g, ln_b_b,
      wout, b_out, ln1_g, ln1_b, sup, sdown)


def ffn_forward(xhat1, ln1_g, ln1_b, wup_g, cfw, cfb, wdown, ln2_g, ln2_b, target, tm):
    t = xhat1.shape[0]
    nt = t // tm

    def body(xh_ref, g1_ref, b1_ref, wup_ref, cfw_ref, cfb_ref, wdown_ref, g2_ref, b2_ref, tgt_ref,
             hu_ref, gv_ref, dr2_ref, loss_ref, sln2_ref,
             x1_ref, x1b_ref, hu32_ref, carry_ref, gbuf_ref, ffn_ref, acc_loss, acc_g2, acc_b2):
        i = pl.program_id(0)

        @pl.when(i == 0)
        def _():
            carry_ref[...] = jnp.zeros(carry_ref.shape, F32)
            acc_loss[...] = jnp.zeros(acc_loss.shape, F32)
            acc_g2[...] = jnp.zeros(acc_g2.shape, F32)
            acc_b2[...] = jnp.zeros(acc_b2.shape, F32)

        x1 = xh_ref[...] * g1_ref[...] + b1_ref[...]
        x1_ref[...] = x1
        x1b_ref[...] = x1.astype(BF16)

        def conv(g, j, base):
            if base == 0:
                win = jnp.concatenate([carry_ref[j], hu32_ref[g, 0:ROWS, :]], axis=0)
            else:
                win = hu32_ref[g, base - HALO_F:base + ROWS, :]
            taps = _taps_f(win)
            w = cfw_ref[j]
            return sum(taps[k] * w[k:k + 1, :] for k in range(KF)) + cfb_ref[j:j + 1, :]

        for f in range(N_F):
            hu32_ref[0] = _nn(x1b_ref[...], wup_ref[f])
            hu32_ref[1] = _nn(x1b_ref[...], wup_ref[N_F + f])

            def rows(bi, f=f):
                r = _rows(bi)
                gate = conv(0, f, bi * ROWS)
                val = conv(1, N_F + f, bi * ROWS)
                gbuf_ref[r, :] = (gate * _sigmoid(gate) * val).astype(BF16)
                gv_ref[f, r, :] = gate.astype(BF16)
                gv_ref[N_F + f, r, :] = val.astype(BF16)
                hu_ref[f, r, :] = hu32_ref[0, r, :].astype(BF16)
                hu_ref[N_F + f, r, :] = hu32_ref[1, r, :].astype(BF16)

            _loop(tm // ROWS, rows)
            carry_ref[f] = hu32_ref[0, tm - HALO_F:tm, :]
            carry_ref[N_F + f] = hu32_ref[1, tm - HALO_F:tm, :]
            part = _nn(gbuf_ref[...], wdown_ref[f])
            if f == 0:
                ffn_ref[...] = part
            else:
                ffn_ref[...] += part

        def tail(bi):
            r = _rows(bi, LN_ROWS)
            xhat2, rstd2 = _ln_stats(ALPHA * x1_ref[r, :] + ffn_ref[r, :])
            err = xhat2 * g2_ref[...] + b2_ref[...] - tgt_ref[r, :]
            e2 = _rsum8(err * err)
            acc_loss[...] += sum(e2[:, k * 128:(k + 1) * 128] for k in range(D // 128))
            dy = err * (1.0 / D)
            acc_g2[...] += _rsum8(dy * xhat2)
            acc_b2[...] += _rsum8(dy)
            dr2_ref[r, :] = _ln_bwd(dy * g2_ref[...], xhat2, rstd2)

        _loop(tm // LN_ROWS, tail)
        loss_ref[...] = acc_loss[...]

        @pl.when(i == nt - 1)
        def _():
            dg = jnp.sum(acc_g2[...], axis=0, keepdims=True)
            db = jnp.sum(acc_b2[...], axis=0, keepdims=True)
            for k in range(D // 128):
                sln2_ref[k:k + 1, :] = dg[:, k * 128:(k + 1) * 128]
                sln2_ref[8 + k:9 + k, :] = db[:, k * 128:(k + 1) * 128]

    row = pl.BlockSpec((tm, D), lambda i: (i, 0))
    return pl.pallas_call(
        body, name="ffn_forward", grid=(nt,),
        in_specs=[row, _full(ln1_g.shape), _full(ln1_b.shape), _resident(wup_g.shape),
                  _full(cfw.shape), _full(cfb.shape), _resident(wdown.shape),
                  _full(ln2_g.shape), _full(ln2_b.shape), row],
        out_specs=[pl.BlockSpec((N_DEV, tm, W_UP_BLK), lambda i: (0, i, 0)),
                   pl.BlockSpec((N_DEV, tm, W_UP_BLK), lambda i: (0, i, 0)), row,
                   _full((8, 128)), _full((16, 128))],
        out_shape=[jax.ShapeDtypeStruct((N_DEV, t, W_UP_BLK), BF16),
                   jax.ShapeDtypeStruct((N_DEV, t, W_UP_BLK), BF16), jax.ShapeDtypeStruct((t, D), F32),
                   jax.ShapeDtypeStruct((8, 128), F32), jax.ShapeDtypeStruct((16, 128), F32)],
        scratch_shapes=[pltpu.VMEM((tm, D), F32), pltpu.VMEM((tm, D), BF16),
                        pltpu.VMEM((2, tm, W_UP_BLK), F32),
                        pltpu.VMEM((N_DEV, HALO_F, W_UP_BLK), F32), pltpu.VMEM((tm, W_UP_BLK), BF16),
                        pltpu.VMEM((tm, D), F32), pltpu.VMEM((8, 128), F32),
                        pltpu.VMEM((8, D), F32), pltpu.VMEM((8, D), F32)],
        compiler_params=_params(("arbitrary",)),
    )(xhat1, ln1_g, ln1_b, wup_g, cfw, cfb, wdown, ln2_g, ln2_b, target)


def ffn_backward(order, dr2, xhat1, ln1_g, ln1_b, hu, gv, wup_g, cfw, wdown, tm):
    t = dr2.shape[0]
    nt = t // tm
    sub_rows = tm
    hu4 = hu.reshape(2, N_F, t, W_UP_BLK)
    gv4 = gv.reshape(2, N_F, t, W_UP_BLK)
    wup4 = wup_g.reshape(2, N_F, D, W_UP_BLK)
    cfw4 = cfw.reshape(2, N_F, KF, W_UP_BLK)

    def body(order_ref, dr2_ref, xh_ref, g1_ref, b1_ref, hu_ref, gv_ref, wup_ref, cfw_ref, wdown_ref,
             dwup_ref, dwdown_ref, dcfw_ref, dcfb_ref, dx1_ref, land_up_ref, land_down_ref,
             x1b_ref, drb_ref, dg_ref, dextg_ref, dextv_ref, gbuf_ref,
             dhug_ref, dhuv_ref, acc_wup, acc_wdown, acc_cfw, acc_cfb, sem, send_sems, recv_sems):
        f = order_ref[pl.program_id(0)]
        i = pl.program_id(1)
        x, y, c = _mesh_pos()
        half = D_FF // N_DEV

        def to_sibling(fi, k, src, land_ref, shard_chip):
            d = jnp.bitwise_xor(shard_chip, 2 * x + y)
            slot = jnp.where(d == 1, 2, jnp.where(d == 2, 1, d))
            return pltpu.make_async_remote_copy(
                src_ref=src, dst_ref=land_ref.at[slot], send_sem=send_sems.at[fi, k], recv_sem=recv_sems.at[fi, k],
                device_id=(x, y, 1 - c), device_id_type=MESH)

        def up_copy(fi, g):
            return to_sibling(fi, g, dwup_ref.at[g, fi], land_up_ref, 2 * g + fi // 2)

        def down_copy(fi):
            return to_sibling(fi, 2, dwdown_ref.at[fi, pl.ds((1 - c) * half, half)], land_down_ref, fi)

        @pl.when(i == 0)
        def _():
            acc_wup[...] = jnp.zeros(acc_wup.shape, F32)
            acc_wdown[...] = jnp.zeros(acc_wdown.shape, F32)
            acc_cfw[...] = jnp.zeros(acc_cfw.shape, F32)
            acc_cfb[...] = jnp.zeros(acc_cfb.shape, F32)
            dextg_ref[tm:tm + HALO_F, :] = jnp.zeros((HALO_F, W_UP_BLK), F32)
            dextv_ref[tm:tm + HALO_F, :] = jnp.zeros((HALO_F, W_UP_BLK), F32)

        w = [cfw_ref[0, 0], cfw_ref[1, 0]]
        dext = [dextg_ref, dextv_ref]
        dhu = [dhug_ref, dhuv_ref]

        def rows1(bi):
            r = _rows(bi)
            gate = gv_ref[0, 0, r, :].astype(F32)
            val = gv_ref[1, 0, r, :].astype(F32)
            sg = _sigmoid(gate)
            silu = gate * sg
            gbuf_ref[r, :] = (silu * val).astype(BF16)
            dg = dg_ref[r, :]
            dgate = dg * val * (sg * (1.0 + gate * (1.0 - sg)))
            dval = dg * silu
            dextg_ref[r, :] = dgate
            dextv_ref[r, :] = dval
            acc_cfb[0:8, :] += _rsum8(dgate)
            acc_cfb[8:16, :] += _rsum8(dval)

        def rows2(bi):
            r = _rows(bi)
            for g in range(2):
                win = dext[g][pl.ds(bi * ROWS, ROWS + HALO_F), :]
                n = ROWS + HALO_F
                later = [pltpu.roll(win, n - 2, 0)[0:ROWS, :], pltpu.roll(win, n - 1, 0)[0:ROWS, :],
                         win[0:ROWS, :]]
                d = sum(later[k] * w[g][k:k + 1, :] for k in range(KF))
                dhu[g][r, :] = d.astype(BF16)
                pre = hu_ref[g, 0, r, :].astype(F32)
                for k in range(KF):
                    r0 = 8 * (g * KF + k)
                    acc_cfw[r0:r0 + 8, :] += _rsum8(later[k] * pre)

        for sub in reversed(range(tm // sub_rows)):
            rs = slice(sub * sub_rows, (sub + 1) * sub_rows)
            blocks = range(sub * sub_rows // ROWS, (sub + 1) * sub_rows // ROWS)
            x1b_ref[rs, :] = (xh_ref[rs, :] * g1_ref[...] + b1_ref[...]).astype(BF16)
            drb_ref[rs, :] = dr2_ref[rs, :].astype(BF16)
            dg_ref[rs, :] = _nt(drb_ref[rs, :], wdown_ref[0])
            for bi in blocks:
                rows1(bi)
            for bi in blocks:
                rows2(bi)
            acc_wdown[...] += _tn(gbuf_ref[rs, :], drb_ref[rs, :])
            acc_wup[0] += _tn(dhug_ref[rs, :], x1b_ref[rs, :])
            acc_wup[1] += _tn(dhuv_ref[rs, :], x1b_ref[rs, :])
            dx1_ref[0, rs, :] = (_nt(dhug_ref[rs, :], wup_ref[0, 0])
                                 + _nt(dhuv_ref[rs, :], wup_ref[1, 0])).astype(BF16)
        dextg_ref[tm:tm + HALO_F, :] = dextg_ref[0:HALO_F, :]
        dextv_ref[tm:tm + HALO_F, :] = dextv_ref[0:HALO_F, :]

        @pl.when(i == nt - 1)
        def _():
            for g in range(2):
                dcfb_ref[g, 0] = jnp.sum(acc_cfb[8 * g:8 * g + 8, :], axis=0, keepdims=True)
                for k in range(KF):
                    r0 = 8 * (g * KF + k)
                    dcfw_ref[g, 0, k:k + 1, :] = jnp.sum(acc_cfw[r0:r0 + 8, :], axis=0, keepdims=True)
            cps = [pltpu.make_async_copy(acc_wup.at[0], dwup_ref.at[0, f], sem.at[0]),
                   pltpu.make_async_copy(acc_wup.at[1], dwup_ref.at[1, f], sem.at[1]),
                   pltpu.make_async_copy(acc_wdown, dwdown_ref.at[f], sem.at[2])]
            for cp in cps:
                cp.start()
            for cp in cps:
                cp.wait()
            down_copy(f).start()

            @pl.when(f % 2 != c)
            def _():
                up_copy(f, 0).start()
                up_copy(f, 1).start()

        @pl.when((i == nt - 1) & (pl.program_id(0) == N_F - 1))
        def _():
            for fi in range(N_F):
                down_copy(fi).wait()
                for g in range(2):
                    @pl.when(fi % 2 != c)
                    def _():
                        up_copy(fi, g).wait_send()

                    @pl.when(fi % 2 == c)
                    def _():
                        up_copy(fi, g).wait_recv()

    rev = lambda i: nt - 1 - i
    row = pl.BlockSpec((tm, D), lambda fo, i, o: (rev(i), 0))
    pair = lambda r, c: pl.BlockSpec((2, 1, r, c), lambda fo, i, o: (0, o[fo], 0, 0))
    tile = pl.BlockSpec((2, 1, tm, W_UP_BLK), lambda fo, i, o: (0, o[fo], rev(i), 0))
    return pl.pallas_call(
        body, name="ffn_backward",
        grid_spec=pltpu.PrefetchScalarGridSpec(
            num_scalar_prefetch=1, grid=(N_F, nt),
            in_specs=[row, row, _full(ln1_g.shape), _full(ln1_b.shape), tile, tile,
                      pair(D, W_UP_BLK), pair(KF, W_UP_BLK),
                      pl.BlockSpec((1, W_UP_BLK, D), lambda fo, i, o: (o[fo], 0, 0))],
            out_specs=[ANY, ANY, pair(KF, W_UP_BLK), pair(1, W_UP_BLK),
                       pl.BlockSpec((1, tm, D), lambda fo, i, o: (o[fo], rev(i), 0)), ANY, ANY],
            scratch_shapes=[pltpu.VMEM((tm, D), BF16), pltpu.VMEM((tm, D), BF16),
                            pltpu.VMEM((tm, W_UP_BLK), F32),
                            pltpu.VMEM((tm + HALO_F, W_UP_BLK), F32), pltpu.VMEM((tm + HALO_F, W_UP_BLK), F32),
                            pltpu.VMEM((tm, W_UP_BLK), BF16), pltpu.VMEM((tm, W_UP_BLK), BF16),
                            pltpu.VMEM((tm, W_UP_BLK), BF16),
                            pltpu.VMEM((2, W_UP_BLK, D), F32), pltpu.VMEM((W_UP_BLK, D), F32),
                            pltpu.VMEM((2 * KF * 8, W_UP_BLK), F32), pltpu.VMEM((16, W_UP_BLK), F32),
                            pltpu.SemaphoreType.DMA((3,)),
                            pltpu.SemaphoreType.DMA((N_F, 3)), pltpu.SemaphoreType.DMA((N_F, 3))]),
        out_shape=[jax.ShapeDtypeStruct((2, N_F, W_UP_BLK, D), F32),
                   jax.ShapeDtypeStruct((N_F, W_UP_BLK, D), F32),
                   jax.ShapeDtypeStruct((2, N_F, KF, W_UP_BLK), F32),
                   jax.ShapeDtypeStruct((2, N_F, 1, W_UP_BLK), F32),
                   jax.ShapeDtypeStruct((N_F, t, D), BF16),
                   jax.ShapeDtypeStruct((4, W_UP_BLK, D), F32),
                   jax.ShapeDtypeStruct((4, D_FF // N_DEV, D), F32)],
        compiler_params=_params(("arbitrary", "arbitrary")),
    )(order, dr2, xhat1, ln1_g, ln1_b, hu4, gv4, wup4, cfw4, wdown)


def mix_backward(x, h, yb1, dx1p, dr2, xhat1, rstd1, win_g, ln_a_g, ln_a_b, w_spatial, bst,
                 conv_b_w, ln_b_g, ln_b_b, wout, ln1_g, ffn_partials, tm):
    t = x.shape[0]
    n_p = len(ffn_partials)
    nt = t // tm
    n_chunks = tm // CHUNK
    halo_blocks = tm // HALO_B

    def body(x_ref, h_ref, halo_ref, yb1_ref, dx1p_ref, dr2_ref, xh1_ref, rstd1_ref, win_ref, ga_ref, ba_ref,
             ws_ref, bst_ref, cw_ref, gb_ref, bb_ref, wout_ref, g1_ref, *rest):
        p_refs, rest = rest[:n_p], rest[n_p:]
        gx_ref, dwin_ref, dwout_ref, dcw_ref, small_ref = rest[:5]
        land_refs, rest = rest[5:5 + n_p], rest[5 + n_p:]
        (ext_ref, dext_ref, y_ref, dy_ref, dh_ref, dmb_ref, wsm_ref,
         acc_win, acc_wout, acc_bin, acc_lnag, acc_lnab, acc_ws, acc_bs, acc_cbb, acc_lnbg,
         acc_lnbb, acc_bout, acc_ln1g, acc_ln1b, acc_cw, sem, send_sems, recv_sems) = rest
        i = pl.program_id(0)

        @pl.when(i == 0)
        def _():
            for cp in _chip_copies(p_refs, land_refs, send_sems, recv_sems):
                cp.start()

        first_tile = i == nt - 1
        accs = [acc_win, acc_wout, acc_bin, acc_lnag, acc_lnab, acc_ws, acc_bs, acc_cbb, acc_lnbg,
                acc_lnbb, acc_bout, acc_ln1g, acc_ln1b, acc_cw]

        @pl.when(i == 0)
        def _():
            for acc in accs:
                acc[...] = jnp.zeros(acc.shape, F32)
            dext_ref[tm:tm + HALO_B, :] = jnp.zeros((HALO_B, D_B), F32)
            mask = _tril_mask()
            for hd in range(HEADS):
                wsm_ref[hd] = jnp.where(mask, ws_ref[hd], 0.0).astype(BF16)

        def ln1_rows(bi):
            r = _rows(bi, LN_ROWS)
            part = [dx1p_ref[f, r, :].astype(F32) for f in range(N_F)]
            dx1 = ALPHA * dr2_ref[r, :] + ((part[0] + part[1]) + (part[2] + part[3]))
            xhat = xh1_ref[r, :]
            acc_ln1g[...] += _rsum8(dx1 * xhat)
            acc_ln1b[...] += _rsum8(dx1)
            dr1 = _ln_bwd(dx1 * g1_ref[...], xhat, rstd1_ref[r, 0:1])
            acc_bout[...] += _rsum8(dr1)
            gx_ref[r, :] = ALPHA * dr1
            dmb_ref[r, :] = dr1.astype(BF16)

        _loop(tm // LN_ROWS, ln1_rows)
        dy_ref[...] = _nt(dmb_ref[...], wout_ref[...])

        ha = halo_ref[:, 0:D_B]
        hg = halo_ref[:, D_B:2 * D_B]
        ext_ref[0:HALO_B, :] = jnp.where(first_tile, 0.0, 1.0) * (ha * _sigmoid(hg))

        def chunk(ci):
            r = _rows(ci, CHUNK)
            for hd in range(HEADS):
                sl = slice(hd * HEAD_DIM, (hd + 1) * HEAD_DIM)
                rows8 = slice(8 * hd, 8 * hd + 8)
                hus, hvs, u, cdf_u, cdf_v, xhat, rstd, vn, sv = _mixer_a_head(
                    h_ref, r, hd, ga_ref, ba_ref, wsm_ref, bst_ref)
                dy_a = dy_ref[r, sl]
                y_ref[r, sl] = (u * sv).astype(BF16)
                du = dy_a * sv
                dsv = dy_a * u
                dsvb = dsv.astype(BF16)
                acc_bs[hd] += dsv
                acc_ws[hd] += _nt(dsvb, vn)
                dvn = _tn(wsm_ref[hd], dsvb)
                acc_lnag[rows8, :] += _rsum8(dvn * xhat)
                acc_lnab[rows8, :] += _rsum8(dvn)
                dv = _ln_bwd(dvn * ga_ref[hd:hd + 1, :], xhat, rstd)
                slv = slice(D_A + hd * HEAD_DIM, D_A + (hd + 1) * HEAD_DIM)
                dhu = du * (cdf_u + hus * jnp.exp(-0.5 * hus * hus) * INV_SQRT_2PI)
                dhv = dv * (cdf_v + hvs * jnp.exp(-0.5 * hvs * hvs) * INV_SQRT_2PI)
                acc_bin[:, sl] += _rsum8(dhu)
                acc_bin[:, slv] += _rsum8(dhv)
                dh_ref[r, sl] = dhu.astype(BF16)
                dh_ref[r, slv] = dhv.astype(BF16)
            a_b = h_ref[r, 2 * D_A:2 * D_A + D_B]
            g_b = h_ref[r, 2 * D_A + D_B:D_IN]
            ext_ref[pl.ds(HALO_B + ci * CHUNK, CHUNK), :] = a_b * _sigmoid(g_b)

        _loop(n_chunks, chunk)

        def conv_rows(bi):
            base = bi * ROWS
            r = pl.ds(base, ROWS)
            xhat, rstd = _ln_stats(yb1_ref[r, :])
            yb2 = xhat * gb_ref[...] + bb_ref[...]
            sg = _sigmoid(yb2)
            y_ref[r, D_A:D] = (yb2 * sg).astype(BF16)
            dyb2 = dy_ref[r, D_A:D] * (sg * (1.0 + yb2 * (1.0 - sg)))
            acc_lnbg[...] += _rsum8(dyb2 * xhat)
            acc_lnbb[...] += _rsum8(dyb2)
            dyb1 = _ln_bwd(dyb2 * gb_ref[...], xhat, rstd)
            acc_cbb[...] += _rsum8(dyb1)
            dext_ref[r, :] = dyb1
            for k, tap in _taps(ext_ref[pl.ds(base, ROWS + HALO_B), :], CONV_B_OFFSETS):
                acc_cw[8 * k:8 * k + 8, :] += _rsum8(dyb1 * tap)

        _loop(tm // ROWS, conv_rows)

        def convt_rows(bi):
            base = bi * ROWS
            r = pl.ds(base, ROWS)
            dyb0 = jnp.zeros((ROWS, D_B), F32)
            for k, tap in _taps(dext_ref[pl.ds(base, ROWS + HALO_B), :], CONV_B_T_OFFSETS):
                dyb0 = dyb0 + tap * cw_ref[k:k + 1, :]
            a_b = h_ref[r, 2 * D_A:2 * D_A + D_B]
            sg = _sigmoid(h_ref[r, 2 * D_A + D_B:D_IN])
            da_b = dyb0 * sg
            dg_b = dyb0 * a_b * sg * (1.0 - sg)
            acc_bin[:, 2 * D_A:2 * D_A + D_B] += _rsum8(da_b)
            acc_bin[:, 2 * D_A + D_B:D_IN] += _rsum8(dg_b)
            dh_ref[r, 2 * D_A:2 * D_A + D_B] = da_b.astype(BF16)
            dh_ref[r, 2 * D_A + D_B:D_IN] = dg_b.astype(BF16)

        _loop(tm // ROWS, convt_rows)
        dext_ref[tm:tm + HALO_B, :] = dext_ref[0:HALO_B, :]

        acc_wout[...] += _tn(y_ref[...], dmb_ref[...])
        xt = x_ref[...].T.astype(BF16)
        dh_blocks = [dh_ref[:, j * W_IN_BLK:(j + 1) * W_IN_BLK] for j in range(N_DEV)]
        for j in range(N_DEV):
            acc_win[j] += _nn(xt, dh_blocks[j])
        gx_ref[...] += sum(_nt(dh_blocks[j], win_ref[j]) for j in range(N_DEV))

        @pl.when(i == nt - 1)
        def _():
            cps = [pltpu.make_async_copy(acc_win, dwin_ref, sem.at[0]),
                   pltpu.make_async_copy(acc_wout, dwout_ref, sem.at[1])]
            for cp in cps:
                cp.start()
            small_ref[...] = jnp.zeros(small_ref.shape, F32)

            def put_row_vector(row0, acc):
                vec = jnp.sum(acc[...], axis=0, keepdims=True)
                for k in range(vec.shape[1] // 128):
                    small_ref[row0 + k:row0 + k + 1, :] = vec[:, k * 128:(k + 1) * 128]

            put_row_vector(S_BIN, acc_bin)
            put_row_vector(S_CBB, acc_cbb)
            put_row_vector(S_LNBG, acc_lnbg)
            put_row_vector(S_LNBB, acc_lnbb)
            put_row_vector(S_BOUT, acc_bout)
            put_row_vector(S_LN1G, acc_ln1g)
            put_row_vector(S_LN1B, acc_ln1b)
            mask = _tril_mask()
            for hd in range(HEADS):
                rows8 = slice(8 * hd, 8 * hd + 8)
                small_ref[S_LNAG + hd:S_LNAG + hd + 1, :] = jnp.sum(acc_lnag[rows8, :], axis=0, keepdims=True)
                small_ref[S_LNAB + hd:S_LNAB + hd + 1, :] = jnp.sum(acc_lnab[rows8, :], axis=0, keepdims=True)
                small_ref[S_WS + hd * CHUNK:S_WS + (hd + 1) * CHUNK, :] = jnp.where(mask, acc_ws[hd], 0.0)
                small_ref[S_BS + hd:S_BS + hd + 1, :] = jnp.sum(acc_bs[hd].T, axis=0, keepdims=True)
            for k in range(KB):
                dcw_ref[k:k + 1, :] = jnp.sum(acc_cw[8 * k:8 * k + 8, :], axis=0, keepdims=True)
            for cp in cps:
                cp.wait()
            for cp in _chip_copies(p_refs, land_refs, send_sems, recv_sems):
                cp.wait()

    rev = lambda i: nt - 1 - i
    row = lambda w: pl.BlockSpec((tm, w), lambda i: (rev(i), 0))
    return pl.pallas_call(
        body, name="mix_backward", grid=(nt,),
        in_specs=[row(D), row(D_IN),
                  pl.BlockSpec((HALO_B, 2 * D_B), lambda i: (jnp.maximum(rev(i) * halo_blocks - 1, 0), 1)),
                  row(D_B), pl.BlockSpec((N_F, tm, D), lambda i: (0, rev(i), 0)),
                  row(D), row(D), row(128), _resident(win_g.shape), _full(ln_a_g.shape),
                  _full(ln_a_b.shape), _full(w_spatial.shape), _full(bst.shape), _full(conv_b_w.shape),
                  _full(ln_b_g.shape), _full(ln_b_b.shape),
                  _resident(wout.shape), _full(ln1_g.shape)] + [ANY] * n_p,
        out_specs=[row(D), ANY, ANY, _full((KB, D_B)), _full((S_MIX_ROWS, 128))] + [ANY] * n_p,
        out_shape=[jax.ShapeDtypeStruct((t, D), F32), jax.ShapeDtypeStruct((N_DEV, D, W_IN_BLK), F32),
                   jax.ShapeDtypeStruct((D, D), F32), jax.ShapeDtypeStruct((KB, D_B), F32),
                   jax.ShapeDtypeStruct((S_MIX_ROWS, 128), F32)]
        + [jax.ShapeDtypeStruct(p.shape, BF16) for p in ffn_partials],
        scratch_shapes=[pltpu.VMEM((tm + HALO_B, D_B), F32), pltpu.VMEM((tm + HALO_B, D_B), F32),
                        pltpu.VMEM((tm, D), BF16), pltpu.VMEM((tm, D), F32), pltpu.VMEM((tm, D_IN), BF16),
                        pltpu.VMEM((tm, D), BF16),
                        pltpu.VMEM((HEADS, CHUNK, CHUNK), BF16),
                        pltpu.VMEM((N_DEV, D, W_IN_BLK), F32), pltpu.VMEM((D, D), F32),
                        pltpu.VMEM((8, D_IN), F32), pltpu.VMEM((8 * HEADS, HEAD_DIM), F32),
                        pltpu.VMEM((8 * HEADS, HEAD_DIM), F32), pltpu.VMEM((HEADS, CHUNK, CHUNK), F32),
                        pltpu.VMEM((HEADS, CHUNK, CHUNK), F32), pltpu.VMEM((8, D_B), F32),
                        pltpu.VMEM((8, D_B), F32), pltpu.VMEM((8, D_B), F32), pltpu.VMEM((8, D), F32),
                        pltpu.VMEM((8, D), F32), pltpu.VMEM((8, D), F32), pltpu.VMEM((8 * KB, D_B), F32),
                        pltpu.SemaphoreType.DMA((2,)),
                        pltpu.SemaphoreType.DMA((n_p, 3)), pltpu.SemaphoreType.DMA((n_p, 3))],
        compiler_params=_params(("arbitrary",)),
    )(x, h, h, yb1, dx1p, dr2, xhat1, rstd1, win_g, ln_a_g, ln_a_b, w_spatial, bst, conv_b_w,
      ln_b_g, ln_b_b, wout, ln1_g, *ffn_partials)


def _rows128(a):
    return a.reshape(-1, 128)


def _pack_conv(cb, cf):
    lead = cb.shape[:-2]
    pad = [(0, 0)] * len(lead)
    flat = jnp.pad(cb.reshape(lead + (KB * 64,)), pad + [(0, 3 * W_UP_BLK - KB * 64)])
    rows = jnp.concatenate([cf, flat.reshape(lead + (3, W_UP_BLK))], axis=-2)
    return jnp.pad(rows, pad + [(0, 2), (0, 768 - W_UP_BLK)])


def _unpack_conv(p):
    lead = p.shape[:-2]
    cf = p[..., 0:KF, 0:W_UP_BLK]
    cb = p[..., 3:6, 0:W_UP_BLK].reshape(lead + (3 * W_UP_BLK,))[..., :KB * 64].reshape(lead + (KB, 64))
    return cb, cf


def kernel(x, w_in, b_in, ln_a_g, ln_a_b, w_spatial, b_spatial, conv_b_w, conv_b_b, ln_b_g, ln_b_b, w_out, b_out, ln1_g, ln1_b, w_up, conv_f_w, conv_f_b, w_down, ln2_g, ln2_b, loss_target, m_w_in, m_b_in, m_ln_a_g, m_ln_a_b, m_w_spatial, m_b_spatial, m_conv_b_w, m_conv_b_b, m_ln_b_g, m_ln_b_b, m_w_out, m_b_out, m_ln1_g, m_ln1_b, m_w_up, m_conv_f_w, m_conv_f_b, m_w_down, m_ln2_g, m_ln2_b, v_w_in, v_b_in, v_ln_a_g, v_ln_a_b, v_w_spatial, v_b_spatial, v_conv_b_w, v_conv_b_b, v_ln_b_g, v_ln_b_b, v_w_out, v_b_out, v_ln1_g, v_ln1_b, v_w_up, v_conv_f_w, v_conv_f_b, v_w_down, v_ln2_g, v_ln2_b):
    t = x.shape[1]
    x2 = x.reshape(t, D)
    target = loss_target.reshape(t, D)
    tm_fwd = min(t, 512)
    tm_bwd = min(t, 256)
    tm_ffn_bwd = min(t, 512)

    xi, yi, ci = _mesh_pos()
    jidx = jnp.stack([_lid(px, py, ci) for px, py in _chip_patterns(xi, yi)]).astype(jnp.int32)

    win_g, wout_g, conv_g, sup, sdown = all_gather_mixer_weights(
        w_in, w_out, w_up.T, w_down, _pack_conv(conv_b_w, conv_f_w))
    wout_full = wout_g.reshape(D, D)
    conv_b_all, cfw = _unpack_conv(conv_g)
    conv_b_full = conv_b_all.transpose(1, 0, 2).reshape(KB, D_B)
    cfb = conv_f_b.reshape(N_DEV, W_UP_BLK)
    row = lambda a: a.reshape(1, -1)
    bst = b_spatial.T

    h, xhat1, rstd1, yb1, wup_g, wdown_g = mix_forward(
        x2, win_g, row(b_in), ln_a_g, ln_a_b, w_spatial, bst, conv_b_full, row(conv_b_b),
        row(ln_b_g), row(ln_b_b), wout_full, row(b_out), row(ln1_g), row(ln1_b), sup, sdown, tm_fwd)
    wdown4 = wdown_g.reshape(N_F, W_UP_BLK, D)
    hu, gv, dr2, loss_part, s_ln2 = ffn_forward(
        xhat1, row(ln1_g), row(ln1_b), wup_g, cfw, cfb, wdown4, row(ln2_g), row(ln2_b), target, tm_bwd)

    order = jnp.where(ci == 0, jnp.array([1, 3, 0, 2], jnp.int32), jnp.array([0, 2, 1, 3], jnp.int32))
    dwup, dwdown, dcfw, dcfb, dx1p, *ffn_lands = ffn_backward(
        order, dr2, xhat1, row(ln1_g), row(ln1_b), hu, gv, wup_g, cfw, wdown4, tm_ffn_bwd)
    ffn_grads = [dwup.reshape(N_DEV, W_UP_BLK, D), dwdown.reshape(N_DEV, D_FF // N_DEV, D)]
    ffn_partials = [chip_partials("chip_partials_" + nm, g, l, jidx, rb)
                    for nm, g, l, rb in zip(["w_up", "w_down"], ffn_grads, ffn_lands, [352, 352])]
    grad_x, dwin, dwout, dcw, s_mix, *ffn_recvs = mix_backward(
        x2, h, yb1, dx1p, dr2, xhat1, rstd1, win_g, ln_a_g, ln_a_b, w_spatial, bst,
        conv_b_full, row(ln_b_g), row(ln_b_b), wout_full, row(ln1_g), ffn_partials, tm_bwd)

    dcfb_rows = jnp.pad(dcfb.reshape(-1, 128), ((0, 4), (0, 0)))
    svec = jnp.concatenate([s_mix, dcfb_rows, s_ln2, loss_part], axis=0)
    dconv = _pack_conv(dcw.reshape(KB, N_DEV, 64).transpose(1, 0, 2), dcfw.reshape(N_DEV, KF, W_UP_BLK))
    mix_grads = [dwin, dwout.reshape(N_DEV, D // N_DEV, D), dconv]
    mix_w = [w_in, w_out, _pack_conv(conv_b_w, conv_f_w)]
    mix_m = [m_w_in, m_w_out, _pack_conv(m_conv_b_w, m_conv_f_w)]
    mix_v = [v_w_in, v_w_out, _pack_conv(v_conv_b_w, v_conv_f_w)]
    *mix_out, sv_slots = mixer_reduce_adamw(mix_grads, svec, mix_w, mix_m, mix_v)
    big = {nm: [mix_out[k * 3 + p] for k in range(4)] for p, nm in enumerate(["w_in", "w_out", "conv"])}

    ffn_w = [(w_up.T, m_w_up.T, v_w_up.T), (w_down, m_w_down, v_w_down)]
    for nm, g, l, r, (w, m, v) in zip(["w_up", "w_down"], ffn_grads, ffn_lands, ffn_recvs, ffn_w):
        big[nm] = reduce_and_adamw("reduce_adamw_" + nm, g, l, r, w, m, v, jidx, 352)
    big["w_up"] = [o.T for o in big["w_up"]]
    for k in range(4):
        cb_k, cf_k = _unpack_conv(big["conv"][k])
        big.setdefault("conv_b_w", []).append(cb_k)
        big.setdefault("conv_f_w", []).append(cf_k)

    small_w = dict(b_in=b_in, ln_a_g=ln_a_g, ln_a_b=ln_a_b, w_spatial=w_spatial, b_spatial=b_spatial,
                   conv_b_b=conv_b_b, ln_b_g=ln_b_g, ln_b_b=ln_b_b, b_out=b_out, ln1_g=ln1_g,
                   ln1_b=ln1_b, conv_f_b=conv_f_b, ln2_g=ln2_g, ln2_b=ln2_b)
    small_m = dict(b_in=m_b_in, ln_a_g=m_ln_a_g, ln_a_b=m_ln_a_b, w_spatial=m_w_spatial,
                   b_spatial=m_b_spatial, conv_b_b=m_conv_b_b, ln_b_g=m_ln_b_g, ln_b_b=m_ln_b_b,
                   b_out=m_b_out, ln1_g=m_ln1_g, ln1_b=m_ln1_b, conv_f_b=m_conv_f_b, ln2_g=m_ln2_g,
                   ln2_b=m_ln2_b)
    small_v = dict(b_in=v_b_in, ln_a_g=v_ln_a_g, ln_a_b=v_ln_a_b, w_spatial=v_w_spatial,
                   b_spatial=v_b_spatial, conv_b_b=v_conv_b_b, ln_b_g=v_ln_b_g, ln_b_b=v_ln_b_b,
                   b_out=v_b_out, ln1_g=v_ln1_g, ln1_b=v_ln1_b, conv_f_b=v_conv_f_b, ln2_g=v_ln2_g,
                   ln2_b=v_ln2_b)
    order = [nm for nm, _, _ in SMALL_LAYOUT]
    small_out = small_adamw(sv_slots, [_rows128(small_w[nm]) for nm in order],
                            [_rows128(small_m[nm]) for nm in order], [_rows128(small_v[nm]) for nm in order])
    n_small = len(order)
    small = {nm: [small_out[k * n_small + p].reshape(small_w[nm].shape) for k in range(4)]
             for p, nm in enumerate(order)}
    loss = jnp.sum(small_out[4 * n_small]) * (0.5 / D)

    weights = ["w_in", "b_in", "ln_a_g", "ln_a_b", "w_spatial", "b_spatial", "conv_b_w", "conv_b_b",
               "ln_b_g", "ln_b_b", "w_out", "b_out", "ln1_g", "ln1_b", "w_up", "conv_f_w", "conv_f_b",
               "w_down", "ln2_g", "ln2_b"]
    result = lambda nm, k: big[nm][k] if nm in big else small[nm][k]
    return (loss, grad_x.reshape(x.shape), *[result(nm, 0) for nm in weights],
            *[result(nm, 1) for nm in weights], *[result(nm, 2) for nm in weights],
            *[result(nm, 3) for nm in weights])
```

```python
import functools
import math

import jax
import jax.numpy as jnp
from jax import lax
from jax.experimental import pallas as pl
from jax.experimental.pallas import tpu as pltpu

F32 = jnp.float32
BF16 = jnp.bfloat16

D = 1024
D_A = 512
D_B = 512
HEADS = 4
HEAD_DIM = 128
CHUNK = 128
KB = 31
KF = 3
D_FF = 2816
D_IN = 2048
N_DEV = 8
W_IN_BLK = D_IN // N_DEV
W_UP_BLK = 2 * D_FF // N_DEV
N_F = 4
LN_EPS = 1e-5
ALPHA = 2.0 ** 0.25

ADAM_LR = 0.001
ADAM_B1 = 0.9
ADAM_B2 = 0.999
ADAM_EPS = 1e-08
ADAM_WD = 0.01
ADAM_STEP = 10

INV_SQRT2 = 1.0 / math.sqrt(2.0)
INV_SQRT_2PI = 1.0 / math.sqrt(2.0 * math.pi)

HALO_B = 32
HALO_F = 8
ROWS = 64
LN_ROWS = 32
VMEM_LIMIT = 58 * 1024 * 1024

MESH = pl.DeviceIdType.MESH
ANY = pl.BlockSpec(memory_space=pl.ANY)
VMEM = pl.BlockSpec(memory_space=pltpu.VMEM)

S_BIN, S_LNAG, S_LNAB, S_WS, S_BS, S_CBB, S_LNBG, S_LNBB, S_BOUT, S_LN1G, S_LN1B = (
    0, 16, 24, 32, 544, 552, 560, 568, 576, 584, 592)
S_MIX_ROWS = 600
S_CFB = 600
S_LN2G = 648
S_LN2B = 656
S_LOSS = 664
S_ROWS = 672


def _tn(a, b):
    return lax.dot_general(a, b, (((0,), (0,)), ((), ())), preferred_element_type=F32)


def _nt(a, b):
    return lax.dot_general(a, b, (((1,), (1,)), ((), ())), preferred_element_type=F32)


def _nn(a, b):
    return jnp.dot(a, b, preferred_element_type=F32)


def _sigmoid(x):
    return 1.0 / (1.0 + jnp.exp(-x))


def _ln_stats(x):
    mu = jnp.mean(x, axis=-1, keepdims=True)
    xc = x - mu
    var = jnp.mean(xc * xc, axis=-1, keepdims=True)
    rstd = lax.rsqrt(var + LN_EPS)
    return xc * rstd, rstd


def _ln_bwd(dxhat, xhat, rstd):
    m1 = jnp.mean(dxhat, axis=-1, keepdims=True)
    m2 = jnp.mean(dxhat * xhat, axis=-1, keepdims=True)
    return rstd * (dxhat - m1 - xhat * m2)


def _rsum8(x):
    r, n = x.shape
    return x.reshape(r // 8, 8, n).sum(axis=0)


def _rows(i, n=ROWS):
    return pl.ds(i * n, n)


def _loop(n, body):
    for i in range(n):
        body(i)


def _tril_mask():
    r = lax.broadcasted_iota(jnp.int32, (CHUNK, CHUNK), 0)
    c = lax.broadcasted_iota(jnp.int32, (CHUNK, CHUNK), 1)
    return c <= r


def _mixer_a_head(h_ref, r, hd, ga_ref, ba_ref, wsm_ref, bst_ref):
    sl = slice(hd * HEAD_DIM, (hd + 1) * HEAD_DIM)
    hu = h_ref[r, sl]
    hv = h_ref[r, D_A + hd * HEAD_DIM:D_A + (hd + 1) * HEAD_DIM]
    cdf_u = 0.5 * (1.0 + lax.erf(hu * INV_SQRT2))
    cdf_v = 0.5 * (1.0 + lax.erf(hv * INV_SQRT2))
    u = hu * cdf_u
    xhat, rstd = _ln_stats(hv * cdf_v)
    vn = (xhat * ga_ref[hd:hd + 1, :] + ba_ref[hd:hd + 1, :]).astype(BF16)
    sv = _nn(wsm_ref[hd], vn) + bst_ref[:, hd:hd + 1]
    return hu, hv, u, cdf_u, cdf_v, xhat, rstd, vn, sv


def _taps(win, offsets):
    n = win.shape[0]
    for s in range(8):
        ks = [k for k, o in enumerate(offsets) if o % 8 == s]
        if ks:
            moved = win if s == 0 else pltpu.roll(win, n - s, 0)
            for k in ks:
                yield k, moved[offsets[k] - s:offsets[k] - s + ROWS, :]


CONV_B_OFFSETS = [2 + k for k in range(KB)]
CONV_B_T_OFFSETS = [30 - k for k in range(KB)]


def _conv_b_block(ext_ref, base, cw_ref):
    acc = jnp.zeros((ROWS, D_B), F32)
    for k, tap in _taps(ext_ref[pl.ds(base, ROWS + HALO_B), :], CONV_B_OFFSETS):
        acc = acc + tap * cw_ref[k:k + 1, :]
    return acc


def _taps_f(win):
    n = ROWS + HALO_F
    return [pltpu.roll(win, n - 6, 0)[0:ROWS, :], pltpu.roll(win, n - 7, 0)[0:ROWS, :], win[8:n, :]]


def _params(sem, **kw):
    return pltpu.CompilerParams(dimension_semantics=sem, vmem_limit_bytes=VMEM_LIMIT, **kw)


def _resident(shape):
    zeros = (0,) * len(shape)
    return pl.BlockSpec(shape, lambda *_: zeros, pipeline_mode=pl.Buffered(1))


def _full(shape):
    zeros = (0,) * len(shape)
    return pl.BlockSpec(shape, lambda *_: zeros)


def _mesh_pos():
    return lax.axis_index("x"), lax.axis_index("y"), lax.axis_index("c")


def _chip_patterns(x, y):
    return [(x, y), (1 - x, y), (x, 1 - y), (1 - x, 1 - y)]


def _lid(x, y, c):
    return 4 * x + 2 * y + c


def _gather_copy(outs, send_sems, recv_sems, a, k, block, to, src=None):
    blk = outs[a].at[_lid(*block)]
    return pltpu.make_async_remote_copy(
        src_ref=blk if src is None else src, dst_ref=blk,
        send_sem=send_sems.at[a, k], recv_sem=recv_sems.at[a, k], device_id=to, device_id_type=MESH)


def _gather_start(mine, outs, send_sems, recv_sems, local_sems):
    x, y, c = _mesh_pos()
    me, sib = (x, y, c), (x, y, 1 - c)
    for a in range(len(mine)):
        pltpu.make_async_copy(mine[a], outs[a].at[_lid(*me)], local_sems.at[a]).start()
        _gather_copy(outs, send_sems, recv_sems, a, 0, me, sib, src=mine[a]).start()
        for j, chip in enumerate(_chip_patterns(x, y)[1:]):
            _gather_copy(outs, send_sems, recv_sems, a, 1 + j, me, (*chip, c), src=mine[a]).start()


def _gather_finish(mine, outs, send_sems, recv_sems, local_sems):
    x, y, c = _mesh_pos()
    me, sib = (x, y, c), (x, y, 1 - c)
    chips = _chip_patterns(x, y)[1:]
    n = len(mine)
    copy = functools.partial(_gather_copy, outs, send_sems, recv_sems)
    passed = []
    for j, chip in enumerate(chips):
        for a in range(n):
            copy(a, 1 + j, (*chip, c), me).wait_recv()
            cp = copy(a, 4 + j, (*chip, c), sib)
            cp.start()
            passed.append(cp)
    for a in range(n):
        copy(a, 0, sib, me).wait_recv()
        for j, chip in enumerate(chips):
            copy(a, 4 + j, (*chip, 1 - c), me).wait_recv()
        for k in range(4):
            copy(a, k, me, sib, src=mine[a]).wait_send()
        pltpu.make_async_copy(mine[a], outs[a].at[_lid(*me)], local_sems.at[a]).wait()
    for cp in passed:
        cp.wait_send()


def _gather_scratch(n):
    return [pltpu.SemaphoreType.DMA((n, 7)), pltpu.SemaphoreType.DMA((n, 7)), pltpu.SemaphoreType.DMA((n,))]


def prepare_weights(w_in, w_out, w_up_t, w_down, convp):
    def body(win_ref, wout_ref, wup_ref, wdown_ref, convp_ref,
             sin_ref, sout_ref, sup_ref, sdown_ref, gconv_ref, send_sems, recv_sems, local_sems):
        gather = ([convp_ref], [gconv_ref], send_sems, recv_sems, local_sems)
        _gather_start(*gather)
        sin_ref[...] = win_ref[...].astype(BF16)
        sout_ref[...] = wout_ref[...].astype(BF16)
        sup_ref[...] = wup_ref[...].T.astype(BF16)
        sdown_ref[...] = wdown_ref[...].astype(BF16)
        _gather_finish(*gather)

    return pl.pallas_call(
        body, name="prepare_weights",
        out_shape=[jax.ShapeDtypeStruct(w_in.shape, BF16), jax.ShapeDtypeStruct(w_out.shape, BF16),
                   jax.ShapeDtypeStruct(w_up_t.shape[::-1], BF16), jax.ShapeDtypeStruct(w_down.shape, BF16),
                   jax.ShapeDtypeStruct((N_DEV,) + convp.shape, F32)],
        in_specs=[VMEM] * 5, out_specs=[VMEM] * 4 + [ANY],
        scratch_shapes=_gather_scratch(1),
        compiler_params=pltpu.CompilerParams(vmem_limit_bytes=VMEM_LIMIT),
    )(w_in, w_out, w_up_t, w_down, convp)


def _chip_copies(p, land, send_sems, recv_sems):
    x, y, c = _mesh_pos()
    return [pltpu.make_async_remote_copy(
        src_ref=p[a].at[k], dst_ref=land[a].at[k], send_sem=send_sems.at[a, k], recv_sem=recv_sems.at[a, k],
        device_id=(px, py, c), device_id_type=MESH)
        for k, (px, py) in enumerate(_chip_patterns(x, y)[1:]) for a in range(len(p))]


def chip_partials(name, g, land, jidx, rb):
    _, r, c = g.shape

    def body(j_ref, g_ref, l_ref, o_ref):
        o_ref[...] = (g_ref[...] + l_ref[...]).astype(BF16)

    return pl.pallas_call(
        body, name=name,
        out_shape=jax.ShapeDtypeStruct((3, r, c), BF16),
        grid_spec=pltpu.PrefetchScalarGridSpec(
            num_scalar_prefetch=1, grid=(3, r // rb),
            in_specs=[pl.BlockSpec((1, rb, c), lambda k, i, j: (j[1 + k], i, 0)),
                      pl.BlockSpec((1, rb, c), lambda k, i, j: (1 + k, i, 0))],
            out_specs=pl.BlockSpec((1, rb, c), lambda k, i, j: (k, i, 0))),
        compiler_params=_params(("arbitrary", "arbitrary")),
    )(jidx, g, land)


def _adamw(w, g, m, v):
    m2 = ADAM_B1 * m + (1.0 - ADAM_B1) * g
    v2 = ADAM_B2 * v + (1.0 - ADAM_B2) * (g * g)
    m_hat = m2 / (1.0 - ADAM_B1 ** ADAM_STEP)
    v_hat = v2 / (1.0 - ADAM_B2 ** ADAM_STEP)
    delta = -ADAM_LR * (m_hat / (jnp.sqrt(v_hat) + ADAM_EPS) + ADAM_WD * w)
    return delta, m2, v2


def reduce_and_adamw(name, g, land, recv, w, m, v, jidx, rb):
    _, r, c = g.shape

    def body(j_ref, g_ref, l_ref, r_ref, w_ref, m_ref, v_ref, go_ref, do_ref, mo_ref, vo_ref):
        grad = (g_ref[0] + l_ref[0]) + r_ref[0].astype(F32) + r_ref[1].astype(F32) + r_ref[2].astype(F32)
        delta, m2, v2 = _adamw(w_ref[...], grad, m_ref[...], v_ref[...])
        go_ref[...] = grad
        do_ref[...] = delta
        mo_ref[...] = m2
        vo_ref[...] = v2

    blk = pl.BlockSpec((rb, c), lambda i, j: (i, 0))
    return pl.pallas_call(
        body, name=name,
        out_shape=[jax.ShapeDtypeStruct((r, c), F32)] * 4,
        grid_spec=pltpu.PrefetchScalarGridSpec(
            num_scalar_prefetch=1, grid=(r // rb,),
            in_specs=[pl.BlockSpec((1, rb, c), lambda i, j: (j[0], i, 0)),
                      pl.BlockSpec((1, rb, c), lambda i, j: (0, i, 0)),
                      pl.BlockSpec((3, rb, c), lambda i, j: (0, i, 0)),
                      blk, blk, blk],
            out_specs=[blk] * 4),
        compiler_params=_params(("arbitrary",)),
    )(jidx, g, land, recv, w, m, v)


def mixer_reduce_adamw(grads, svec, ws, ms, vs):
    n = len(grads)
    shard = [g.shape[1:] for g in grads]

    def body(*refs):
        g = refs[:n]
        sv_ref = refs[n]
        w, m, v = refs[n + 1:2 * n + 1], refs[2 * n + 1:3 * n + 1], refs[3 * n + 1:4 * n + 1]
        outs = refs[4 * n + 1:8 * n + 1]
        sv_slots = refs[8 * n + 1]
        rest = refs[8 * n + 2:]
        own, land, sendb, recvb = rest[:n], rest[n:2 * n], rest[2 * n:3 * n], rest[3 * n:4 * n]
        sv_land, chip_sv, d2d_send, d2d_recv, ici_send, ici_recv, local_sems, sv_sems = rest[4 * n:]
        x, y, c = _mesh_pos()
        sib = (x, y, 1 - c)
        pats = _chip_patterns(x, y)
        q = 2 * x + y

        d2d, local = [], []
        for a in range(n):
            for k, (px, py) in enumerate(pats):
                d2d.append(pltpu.make_async_remote_copy(
                    src_ref=g[a].at[_lid(px, py, 1 - c)], dst_ref=land[a].at[k],
                    send_sem=d2d_send.at[a, k], recv_sem=d2d_recv.at[a, k], device_id=sib, device_id_type=MESH))
                local.append(pltpu.make_async_copy(g[a].at[_lid(px, py, c)], own[a].at[k], local_sems.at[a, k]))
        d2d.append(pltpu.make_async_remote_copy(
            src_ref=sv_ref, dst_ref=sv_land, send_sem=d2d_send.at[n, 0], recv_sem=d2d_recv.at[n, 0],
            device_id=sib, device_id_type=MESH))
        for cp in d2d + local:
            cp.start()
        for cp in local + d2d:
            cp.wait()

        for a in range(n):
            for k in range(3):
                sendb[a][k] = (own[a][1 + k] + land[a][1 + k]).astype(BF16)
        chip_sv[...] = sv_ref[...] + sv_land[...]
        ici = _chip_copies(sendb, recvb, ici_send, ici_recv)
        sv_local = pltpu.make_async_copy(chip_sv, sv_slots.at[q], sv_sems.at[0])
        sv_out = [pltpu.make_async_remote_copy(
            src_ref=chip_sv, dst_ref=sv_slots.at[q], send_sem=sv_sems.at[1 + k], recv_sem=sv_sems.at[4 + k],
            device_id=(px, py, c), device_id_type=MESH) for k, (px, py) in enumerate(pats[1:])]
        for cp in ici + sv_out + [sv_local]:
            cp.start()
        for cp in ici:
            cp.wait()
        for k, (px, py) in enumerate(pats[1:]):
            sv_out[k].wait_send()
            pltpu.make_async_remote_copy(
                src_ref=chip_sv, dst_ref=sv_slots.at[2 * px + py], send_sem=sv_sems.at[1 + k],
                recv_sem=sv_sems.at[4 + k], device_id=(px, py, c), device_id_type=MESH).wait_recv()
        sv_local.wait()

        for a in range(n):
            grad = ((own[a][0] + land[a][0]) + recvb[a][0].astype(F32) + recvb[a][1].astype(F32)
                    + recvb[a][2].astype(F32))
            delta, m2, v2 = _adamw(w[a][...], grad, m[a][...], v[a][...])
            outs[a][...] = grad
            outs[n + a][...] = delta
            outs[2 * n + a][...] = m2
            outs[3 * n + a][...] = v2

    shard_out = [jax.ShapeDtypeStruct(s, F32) for s in shard]
    return pl.pallas_call(
        body, name="mixer_reduce_adamw",
        out_shape=shard_out * 4 + [jax.ShapeDtypeStruct((4,) + svec.shape, F32)],
        in_specs=[ANY] * n + [VMEM] * (1 + 3 * n), out_specs=[VMEM] * (4 * n) + [ANY],
        scratch_shapes=[pltpu.VMEM((4,) + s, F32) for s in shard] + [pltpu.VMEM((4,) + s, F32) for s in shard]
        + [pltpu.VMEM((3,) + s, BF16) for s in shard] + [pltpu.VMEM((3,) + s, BF16) for s in shard]
        + [pltpu.VMEM(svec.shape, F32), pltpu.VMEM(svec.shape, F32),
           pltpu.SemaphoreType.DMA((n + 1, 4)), pltpu.SemaphoreType.DMA((n + 1, 4)),
           pltpu.SemaphoreType.DMA((n, 3)), pltpu.SemaphoreType.DMA((n, 3)),
           pltpu.SemaphoreType.DMA((n, 4)), pltpu.SemaphoreType.DMA((7,))],
        compiler_params=pltpu.CompilerParams(vmem_limit_bytes=VMEM_LIMIT),
    )(*grads, svec, *ws, *ms, *vs)


SMALL_LAYOUT = [
    ("b_in", S_BIN, 16), ("ln_a_g", S_LNAG, 4), ("ln_a_b", S_LNAB, 4), ("w_spatial", S_WS, 512),
    ("b_spatial", S_BS, 4), ("conv_b_b", S_CBB, 4), ("ln_b_g", S_LNBG, 4), ("ln_b_b", S_LNBB, 4),
    ("b_out", S_BOUT, 8), ("ln1_g", S_LN1G, 8), ("ln1_b", S_LN1B, 8), ("conv_f_b", S_CFB, 44),
    ("ln2_g", S_LN2G, 8), ("ln2_b", S_LN2B, 8),
]


def small_adamw(sv_slots, ws, ms, vs):
    n = len(SMALL_LAYOUT)

    def body(*refs):
        s_ref = refs[0]
        w_refs, m_refs, v_refs = refs[1:1 + n], refs[1 + n:1 + 2 * n], refs[1 + 2 * n:1 + 3 * n]
        outs = refs[1 + 3 * n:]
        for p, (_, row0, rows) in enumerate(SMALL_LAYOUT):
            sl = pl.ds(row0, rows)
            grad = ((s_ref[0, sl, :] + s_ref[1, sl, :]) + s_ref[2, sl, :]) + s_ref[3, sl, :]
            delta, m2, v2 = _adamw(w_refs[p][...], grad, m_refs[p][...], v_refs[p][...])
            outs[p][...] = grad
            outs[n + p][...] = delta
            outs[2 * n + p][...] = m2
            outs[3 * n + p][...] = v2
        sl = pl.ds(S_LOSS, 8)
        outs[4 * n][...] = ((s_ref[0, sl, :] + s_ref[1, sl, :]) + s_ref[2, sl, :]) + s_ref[3, sl, :]

    shapes = [jax.ShapeDtypeStruct((rows, 128), F32) for _, _, rows in SMALL_LAYOUT]
    return pl.pallas_call(
        body, name="small_adamw", out_shape=shapes * 4 + [jax.ShapeDtypeStruct((8, 128), F32)],
        in_specs=[VMEM] * (1 + 3 * n), out_specs=[VMEM] * (4 * n + 1),
    )(sv_slots, *ws, *ms, *vs)


def mix_forward(x, sin, sout, b_in, ln_a_g, ln_a_b, w_spatial, bst, conv_b_w, conv_b_b, ln_b_g, ln_b_b,
                b_out, ln1_g, ln1_b, sup, sdown, tm):
    t = x.shape[0]
    nt = t // tm
    n_chunks = tm // CHUNK

    def body(x_ref, sin_ref, sout_ref, bin_ref, ga_ref, ba_ref, ws_ref, bst_ref, cw_ref, cb_ref, gb_ref,
             bb_ref, bout_ref, g1_ref, b1_ref, sup_ref, sdown_ref,
             h_ref, xhat1_ref, rstd1_ref, yb1_ref, gin_ref, gout_ref, gup_ref, gdown_ref,
             ext_ref, y_ref, wsm_ref, win_ref, wout_ref, load_sems,
             mix_send, mix_recv, mix_local, send_sems, recv_sems, local_sems):
        i = pl.program_id(0)
        mixer = ([sin_ref, sout_ref], [gin_ref, gout_ref], mix_send, mix_recv, mix_local)
        gather = ([sup_ref, sdown_ref], [gup_ref, gdown_ref], send_sems, recv_sems, local_sems)

        @pl.when(i == 0)
        def _():
            _gather_start(*mixer)
            _gather_start(*gather)
            _gather_finish(*mixer)
            loads = [pltpu.make_async_copy(gin_ref, win_ref, load_sems.at[0]),
                     pltpu.make_async_copy(gout_ref, wout_ref, load_sems.at[1])]
            for cp in loads:
                cp.start()
            for cp in loads:
                cp.wait()
            ext_ref[0:HALO_B, :] = jnp.zeros((HALO_B, D_B), F32)
            mask = _tril_mask()
            for hd in range(HEADS):
                wsm_ref[hd] = jnp.where(mask, ws_ref[hd], 0.0).astype(BF16)

        xb = x_ref[...].astype(BF16)
        for j in range(N_DEV):
            cols = slice(j * W_IN_BLK, (j + 1) * W_IN_BLK)
            h_ref[:, cols] = _nn(xb, win_ref[j]) + bin_ref[:, cols]

        def chunk(ci):
            r = _rows(ci, CHUNK)
            for hd in range(HEADS):
                _, _, u, _, _, _, _, _, sv = _mixer_a_head(h_ref, r, hd, ga_ref, ba_ref, wsm_ref, bst_ref)
                y_ref[r, hd * HEAD_DIM:(hd + 1) * HEAD_DIM] = (u * sv).astype(BF16)
            a_b = h_ref[r, 2 * D_A:2 * D_A + D_B]
            g_b = h_ref[r, 2 * D_A + D_B:D_IN]
            ext_ref[pl.ds(HALO_B + ci * CHUNK, CHUNK), :] = a_b * _sigmoid(g_b)

        _loop(n_chunks, chunk)

        def conv_rows(bi):
            base = bi * ROWS
            yb1 = _conv_b_block(ext_ref, base, cw_ref) + cb_ref[...]
            yb1_ref[pl.ds(base, ROWS), :] = yb1
            xhat, _ = _ln_stats(yb1)
            yb2 = xhat * gb_ref[...] + bb_ref[...]
            y_ref[pl.ds(base, ROWS), D_A:D] = (yb2 * _sigmoid(yb2)).astype(BF16)

        _loop(tm // ROWS, conv_rows)
        ext_ref[0:HALO_B, :] = ext_ref[tm:tm + HALO_B, :]

        mix = _nn(y_ref[...], wout_ref[...].reshape(D, D)) + bout_ref[...]
        xhat1, rstd1 = _ln_stats(ALPHA * x_ref[...] + mix)
        xhat1_ref[...] = xhat1
        rstd1_ref[...] = jnp.broadcast_to(rstd1, (tm, 128))

        @pl.when(i == nt - 1)
        def _():
            _gather_finish(*gather)

    row = lambda w: pl.BlockSpec((tm, w), lambda i: (i, 0))
    return pl.pallas_call(
        body, name="mix_forward", grid=(nt,),
        in_specs=[row(D), ANY, ANY, _full(b_in.shape), _full(ln_a_g.shape),
                  _full(ln_a_b.shape), _full(w_spatial.shape), _full(bst.shape),
                  _full(conv_b_w.shape), _full(conv_b_b.shape), _full(ln_b_g.shape),
                  _full(ln_b_b.shape), _full(b_out.shape),
                  _full(ln1_g.shape), _full(ln1_b.shape), ANY, ANY],
        out_specs=[row(D_IN), row(D), row(128), row(D_B), ANY, ANY, ANY, ANY],
        out_shape=[jax.ShapeDtypeStruct((t, D_IN), F32), jax.ShapeDtypeStruct((t, D), F32),
                   jax.ShapeDtypeStruct((t, 128), F32), jax.ShapeDtypeStruct((t, D_B), F32)]
        + [jax.ShapeDtypeStruct((N_DEV,) + sh.shape, BF16) for sh in (sin, sout, sup, sdown)],
        scratch_shapes=[pltpu.VMEM((tm + HALO_B, D_B), F32), pltpu.VMEM((tm, D), BF16),
                        pltpu.VMEM((HEADS, CHUNK, CHUNK), BF16),
                        pltpu.VMEM((N_DEV,) + sin.shape, BF16), pltpu.VMEM((N_DEV,) + sout.shape, BF16),
                        pltpu.SemaphoreType.DMA((2,))] + _gather_scratch(2) + _gather_scratch(2),
        compiler_params=_params(("arbitrary",)),
    )(x, sin, sout, b_in, ln_a_g, ln_a_b, w_spatial, bst, conv_b_w, conv_b_b, ln_b_g, ln_b_b,
      b_out, ln1_g, ln1_b, sup, sdown)


def ffn_forward(xhat1, ln1_g, ln1_b, wup_g, cfw, cfb, wdown, ln2_g, ln2_b, target, tm):
    t = xhat1.shape[0]
    nt = t // tm

    def body(xh_ref, g1_ref, b1_ref, wup_ref, cfw_ref, cfb_ref, wdown_ref, g2_ref, b2_ref, tgt_ref,
             hu_ref, gv_ref, dr2_ref, loss_ref, sln2_ref,
             x1_ref, x1b_ref, hu32_ref, carry_ref, gbuf_ref, ffn_ref, acc_loss, acc_g2, acc_b2):
        i = pl.program_id(0)

        @pl.when(i == 0)
        def _():
            carry_ref[...] = jnp.zeros(carry_ref.shape, F32)
            acc_loss[...] = jnp.zeros(acc_loss.shape, F32)
            acc_g2[...] = jnp.zeros(acc_g2.shape, F32)
            acc_b2[...] = jnp.zeros(acc_b2.shape, F32)

        x1 = xh_ref[...] * g1_ref[...] + b1_ref[...]
        x1_ref[...] = x1
        x1b_ref[...] = x1.astype(BF16)

        def conv(g, j, base):
            if base == 0:
                win = jnp.concatenate([carry_ref[j], hu32_ref[g, 0:ROWS, :]], axis=0)
            else:
                win = hu32_ref[g, base - HALO_F:base + ROWS, :]
            taps = _taps_f(win)
            w = cfw_ref[j]
            return sum(taps[k] * w[k:k + 1, :] for k in range(KF)) + cfb_ref[j:j + 1, :]

        for f in range(N_F):
            hu32_ref[0] = _nn(x1b_ref[...], wup_ref[f])
            hu32_ref[1] = _nn(x1b_ref[...], wup_ref[N_F + f])

            def rows(bi, f=f):
                r = _rows(bi)
                gate = conv(0, f, bi * ROWS)
                val = conv(1, N_F + f, bi * ROWS)
                gbuf_ref[r, :] = (gate * _sigmoid(gate) * val).astype(BF16)
                gv_ref[f, r, :] = gate.astype(BF16)
                gv_ref[N_F + f, r, :] = val.astype(BF16)
                hu_ref[f, r, :] = hu32_ref[0, r, :].astype(BF16)
                hu_ref[N_F + f, r, :] = hu32_ref[1, r, :].astype(BF16)

            _loop(tm // ROWS, rows)
            carry_ref[f] = hu32_ref[0, tm - HALO_F:tm, :]
            carry_ref[N_F + f] = hu32_ref[1, tm - HALO_F:tm, :]
            part = _nn(gbuf_ref[...], wdown_ref[f])
            if f == 0:
                ffn_ref[...] = part
            else:
                ffn_ref[...] += part

        def tail(bi):
            r = _rows(bi, LN_ROWS)
            xhat2, rstd2 = _ln_stats(ALPHA * x1_ref[r, :] + ffn_ref[r, :])
            err = xhat2 * g2_ref[...] + b2_ref[...] - tgt_ref[r, :]
            e2 = _rsum8(err * err)
            acc_loss[...] += sum(e2[:, k * 128:(k + 1) * 128] for k in range(D // 128))
            dy = err * (1.0 / D)
            acc_g2[...] += _rsum8(dy * xhat2)
            acc_b2[...] += _rsum8(dy)
            dr2_ref[r, :] = _ln_bwd(dy * g2_ref[...], xhat2, rstd2)

        _loop(tm // LN_ROWS, tail)
        loss_ref[...] = acc_loss[...]

        @pl.when(i == nt - 1)
        def _():
            dg = jnp.sum(acc_g2[...], axis=0, keepdims=True)
            db = jnp.sum(acc_b2[...], axis=0, keepdims=True)
            for k in range(D // 128):
                sln2_ref[k:k + 1, :] = dg[:, k * 128:(k + 1) * 128]
                sln2_ref[8 + k:9 + k, :] = db[:, k * 128:(k + 1) * 128]

    row = pl.BlockSpec((tm, D), lambda i: (i, 0))
    return pl.pallas_call(
        body, name="ffn_forward", grid=(nt,),
        in_specs=[row, _full(ln1_g.shape), _full(ln1_b.shape), _resident(wup_g.shape),
                  _full(cfw.shape), _full(cfb.shape), _resident(wdown.shape),
                  _full(ln2_g.shape), _full(ln2_b.shape), row],
        out_specs=[pl.BlockSpec((N_DEV, tm, W_UP_BLK), lambda i: (0, i, 0)),
                   pl.BlockSpec((N_DEV, tm, W_UP_BLK), lambda i: (0, i, 0)), row,
                   _full((8, 128)), _full((16, 128))],
        out_shape=[jax.ShapeDtypeStruct((N_DEV, t, W_UP_BLK), BF16),
                   jax.ShapeDtypeStruct((N_DEV, t, W_UP_BLK), BF16), jax.ShapeDtypeStruct((t, D), F32),
                   jax.ShapeDtypeStruct((8, 128), F32), jax.ShapeDtypeStruct((16, 128), F32)],
        scratch_shapes=[pltpu.VMEM((tm, D), F32), pltpu.VMEM((tm, D), BF16),
                        pltpu.VMEM((2, tm, W_UP_BLK), F32),
                        pltpu.VMEM((N_DEV, HALO_F, W_UP_BLK), F32), pltpu.VMEM((tm, W_UP_BLK), BF16),
                        pltpu.VMEM((tm, D), F32), pltpu.VMEM((8, 128), F32),
                        pltpu.VMEM((8, D), F32), pltpu.VMEM((8, D), F32)],
        compiler_params=_params(("arbitrary",)),
    )(xhat1, ln1_g, ln1_b, wup_g, cfw, cfb, wdown, ln2_g, ln2_b, target)


def ffn_backward(order, dr2, xhat1, ln1_g, ln1_b, hu, gv, wup_g, cfw, wdown, tm):
    t = dr2.shape[0]
    nt = t // tm
    sub_rows = tm
    hu4 = hu.reshape(2, N_F, t, W_UP_BLK)
    gv4 = gv.reshape(2, N_F, t, W_UP_BLK)
    wup4 = wup_g.reshape(2, N_F, D, W_UP_BLK)
    cfw4 = cfw.reshape(2, N_F, KF, W_UP_BLK)

    def body(order_ref, dr2_ref, xh_ref, g1_ref, b1_ref, hu_ref, gv_ref, wup_ref, cfw_ref, wdown_ref,
             dwup_ref, dwdown_ref, dcfw_ref, dcfb_ref, dx1_ref, land_up_ref, land_down_ref,
             x1b_ref, drb_ref, dg_ref, dextg_ref, dextv_ref, gbuf_ref,
             dhug_ref, dhuv_ref, acc_wup, acc_wdown, acc_cfw, acc_cfb, sem, send_sems, recv_sems):
        f = order_ref[pl.program_id(0)]
        i = pl.program_id(1)
        x, y, c = _mesh_pos()
        half = D_FF // N_DEV

        def to_sibling(fi, k, src, land_ref, shard_chip):
            d = jnp.bitwise_xor(shard_chip, 2 * x + y)
            slot = jnp.where(d == 1, 2, jnp.where(d == 2, 1, d))
            return pltpu.make_async_remote_copy(
                src_ref=src, dst_ref=land_ref.at[slot], send_sem=send_sems.at[fi, k], recv_sem=recv_sems.at[fi, k],
                device_id=(x, y, 1 - c), device_id_type=MESH)

        def up_copy(fi, g):
            return to_sibling(fi, g, dwup_ref.at[g, fi], land_up_ref, 2 * g + fi // 2)

        def down_copy(fi):
            return to_sibling(fi, 2, dwdown_ref.at[fi, pl.ds((1 - c) * half, half)], land_down_ref, fi)

        @pl.when(i == 0)
        def _():
            acc_wup[...] = jnp.zeros(acc_wup.shape, F32)
            acc_wdown[...] = jnp.zeros(acc_wdown.shape, F32)
            acc_cfw[...] = jnp.zeros(acc_cfw.shape, F32)
            acc_cfb[...] = jnp.zeros(acc_cfb.shape, F32)
            dextg_ref[tm:tm + HALO_F, :] = jnp.zeros((HALO_F, W_UP_BLK), F32)
            dextv_ref[tm:tm + HALO_F, :] = jnp.zeros((HALO_F, W_UP_BLK), F32)

        w = [cfw_ref[0, 0], cfw_ref[1, 0]]
        dext = [dextg_ref, dextv_ref]
        dhu = [dhug_ref, dhuv_ref]

        def rows1(bi):
            r = _rows(bi)
            gate = gv_ref[0, 0, r, :].astype(F32)
            val = gv_ref[1, 0, r, :].astype(F32)
            sg = _sigmoid(gate)
            silu = gate * sg
            gbuf_ref[r, :] = (silu * val).astype(BF16)
            dg = dg_ref[r, :]
            dgate = dg * val * (sg * (1.0 + gate * (1.0 - sg)))
            dval = dg * silu
            dextg_ref[r, :] = dgate
            dextv_ref[r, :] = dval
            acc_cfb[0:8, :] += _rsum8(dgate)
            acc_cfb[8:16, :] += _rsum8(dval)

        def rows2(bi):
            r = _rows(bi)
            for g in range(2):
                win = dext[g][pl.ds(bi * ROWS, ROWS + HALO_F), :]
                n = ROWS + HALO_F
                later = [pltpu.roll(win, n - 2, 0)[0:ROWS, :], pltpu.roll(win, n - 1, 0)[0:ROWS, :],
                         win[0:ROWS, :]]
                d = sum(later[k] * w[g][k:k + 1, :] for k in range(KF))
                dhu[g][r, :] = d.astype(BF16)
                pre = hu_ref[g, 0, r, :].astype(F32)
                for k in range(KF):
                    r0 = 8 * (g * KF + k)
                    acc_cfw[r0:r0 + 8, :] += _rsum8(later[k] * pre)

        for sub in reversed(range(tm // sub_rows)):
            rs = slice(sub * sub_rows, (sub + 1) * sub_rows)
            blocks = range(sub * sub_rows // ROWS, (sub + 1) * sub_rows // ROWS)
            x1b_ref[rs, :] = (xh_ref[rs, :] * g1_ref[...] + b1_ref[...]).astype(BF16)
            drb_ref[rs, :] = dr2_ref[rs, :].astype(BF16)
            dg_ref[rs, :] = _nt(drb_ref[rs, :], wdown_ref[0])
            for bi in blocks:
                rows1(bi)
            for bi in blocks:
                rows2(bi)
            acc_wdown[...] += _tn(gbuf_ref[rs, :], drb_ref[rs, :])
            acc_wup[0] += _tn(dhug_ref[rs, :], x1b_ref[rs, :])
            acc_wup[1] += _tn(dhuv_ref[rs, :], x1b_ref[rs, :])
            dx1_ref[0, rs, :] = (_nt(dhug_ref[rs, :], wup_ref[0, 0])
                                 + _nt(dhuv_ref[rs, :], wup_ref[1, 0])).astype(BF16)
        dextg_ref[tm:tm + HALO_F, :] = dextg_ref[0:HALO_F, :]
        dextv_ref[tm:tm + HALO_F, :] = dextv_ref[0:HALO_F, :]

        @pl.when(i == nt - 1)
        def _():
            for g in range(2):
                dcfb_ref[g, 0] = jnp.sum(acc_cfb[8 * g:8 * g + 8, :], axis=0, keepdims=True)
                for k in range(KF):
                    r0 = 8 * (g * KF + k)
                    dcfw_ref[g, 0, k:k + 1, :] = jnp.sum(acc_cfw[r0:r0 + 8, :], axis=0, keepdims=True)
            cps = [pltpu.make_async_copy(acc_wup.at[0], dwup_ref.at[0, f], sem.at[0]),
                   pltpu.make_async_copy(acc_wup.at[1], dwup_ref.at[1, f], sem.at[1]),
                   pltpu.make_async_copy(acc_wdown, dwdown_ref.at[f], sem.at[2])]
            for cp in cps:
                cp.start()
            for cp in cps:
                cp.wait()
            down_copy(f).start()

            @pl.when(f % 2 != c)
            def _():
                up_copy(f, 0).start()
                up_copy(f, 1).start()

        @pl.when((i == nt - 1) & (pl.program_id(0) == N_F - 1))
        def _():
            for fi in range(N_F):
                down_copy(fi).wait()
                for g in range(2):
                    @pl.when(fi % 2 != c)
                    def _():
                        up_copy(fi, g).wait_send()

                    @pl.when(fi % 2 == c)
                    def _():
                        up_copy(fi, g).wait_recv()

    rev = lambda i: nt - 1 - i
    row = pl.BlockSpec((tm, D), lambda fo, i, o: (rev(i), 0))
    pair = lambda r, c: pl.BlockSpec((2, 1, r, c), lambda fo, i, o: (0, o[fo], 0, 0))
    tile = pl.BlockSpec((2, 1, tm, W_UP_BLK), lambda fo, i, o: (0, o[fo], rev(i), 0))
    return pl.pallas_call(
        body, name="ffn_backward",
        grid_spec=pltpu.PrefetchScalarGridSpec(
            num_scalar_prefetch=1, grid=(N_F, nt),
            in_specs=[row, row, _full(ln1_g.shape), _full(ln1_b.shape), tile, tile,
                      pair(D, W_UP_BLK), pair(KF, W_UP_BLK),
                      pl.BlockSpec((1, W_UP_BLK, D), lambda fo, i, o: (o[fo], 0, 0))],
            out_specs=[ANY, ANY, pair(KF, W_UP_BLK), pair(1, W_UP_BLK),
                       pl.BlockSpec((1, tm, D), lambda fo, i, o: (o[fo], rev(i), 0)), ANY, ANY],
            scratch_shapes=[pltpu.VMEM((tm, D), BF16), pltpu.VMEM((tm, D), BF16),
                            pltpu.VMEM((tm, W_UP_BLK), F32),
                            pltpu.VMEM((tm + HALO_F, W_UP_BLK), F32), pltpu.VMEM((tm + HALO_F, W_UP_BLK), F32),
                            pltpu.VMEM((tm, W_UP_BLK), BF16), pltpu.VMEM((tm, W_UP_BLK), BF16),
                            pltpu.VMEM((tm, W_UP_BLK), BF16),
                            pltpu.VMEM((2, W_UP_BLK, D), F32), pltpu.VMEM((W_UP_BLK, D), F32),
                            pltpu.VMEM((2 * KF * 8, W_UP_BLK), F32), pltpu.VMEM((16, W_UP_BLK), F32),
                            pltpu.SemaphoreType.DMA((3,)),
                            pltpu.SemaphoreType.DMA((N_F, 3)), pltpu.SemaphoreType.DMA((N_F, 3))]),
        out_shape=[jax.ShapeDtypeStruct((2, N_F, W_UP_BLK, D), F32),
                   jax.ShapeDtypeStruct((N_F, W_UP_BLK, D), F32),
                   jax.ShapeDtypeStruct((2, N_F, KF, W_UP_BLK), F32),
                   jax.ShapeDtypeStruct((2, N_F, 1, W_UP_BLK), F32),
                   jax.ShapeDtypeStruct((N_F, t, D), BF16),
                   jax.ShapeDtypeStruct((4, W_UP_BLK, D), F32),
                   jax.ShapeDtypeStruct((4, D_FF // N_DEV, D), F32)],
        compiler_params=_params(("arbitrary", "arbitrary")),
    )(order, dr2, xhat1, ln1_g, ln1_b, hu4, gv4, wup4, cfw4, wdown)


def mix_backward(x, h, yb1, dx1p, dr2, xhat1, rstd1, win_g, ln_a_g, ln_a_b, w_spatial, bst,
                 conv_b_w, ln_b_g, ln_b_b, wout, ln1_g, ffn_partials, tm):
    t = x.shape[0]
    n_p = len(ffn_partials)
    nt = t // tm
    n_chunks = tm // CHUNK
    halo_blocks = tm // HALO_B

    def body(x_ref, h_ref, halo_ref, yb1_ref, dx1p_ref, dr2_ref, xh1_ref, rstd1_ref, win_ref, ga_ref, ba_ref,
             ws_ref, bst_ref, cw_ref, gb_ref, bb_ref, wout_ref, g1_ref, *rest):
        p_refs, rest = rest[:n_p], rest[n_p:]
        gx_ref, dwin_ref, dwout_ref, dcw_ref, small_ref = rest[:5]
        land_refs, rest = rest[5:5 + n_p], rest[5 + n_p:]
        (ext_ref, dext_ref, y_ref, dy_ref, dh_ref, dmb_ref, wsm_ref,
         acc_win, acc_wout, acc_bin, acc_lnag, acc_lnab, acc_ws, acc_bs, acc_cbb, acc_lnbg,
         acc_lnbb, acc_bout, acc_ln1g, acc_ln1b, acc_cw, sem, send_sems, recv_sems) = rest
        i = pl.program_id(0)

        @pl.when(i == 0)
        def _():
            for cp in _chip_copies(p_refs, land_refs, send_sems, recv_sems):
                cp.start()

        first_tile = i == nt - 1
        accs = [acc_win, acc_wout, acc_bin, acc_lnag, acc_lnab, acc_ws, acc_bs, acc_cbb, acc_lnbg,
                acc_lnbb, acc_bout, acc_ln1g, acc_ln1b, acc_cw]

        @pl.when(i == 0)
        def _():
            for acc in accs:
                acc[...] = jnp.zeros(acc.shape, F32)
            dext_ref[tm:tm + HALO_B, :] = jnp.zeros((HALO_B, D_B), F32)
            mask = _tril_mask()
            for hd in range(HEADS):
                wsm_ref[hd] = jnp.where(mask, ws_ref[hd], 0.0).astype(BF16)

        def ln1_rows(bi):
            r = _rows(bi, LN_ROWS)
            part = [dx1p_ref[f, r, :].astype(F32) for f in range(N_F)]
            dx1 = ALPHA * dr2_ref[r, :] + ((part[0] + part[1]) + (part[2] + part[3]))
            xhat = xh1_ref[r, :]
            acc_ln1g[...] += _rsum8(dx1 * xhat)
            acc_ln1b[...] += _rsum8(dx1)
            dr1 = _ln_bwd(dx1 * g1_ref[...], xhat, rstd1_ref[r, 0:1])
            acc_bout[...] += _rsum8(dr1)
            gx_ref[r, :] = ALPHA * dr1
            dmb_ref[r, :] = dr1.astype(BF16)

        _loop(tm // LN_ROWS, ln1_rows)
        dy_ref[...] = _nt(dmb_ref[...], wout_ref[...])

        ha = halo_ref[:, 0:D_B]
        hg = halo_ref[:, D_B:2 * D_B]
        ext_ref[0:HALO_B, :] = jnp.where(first_tile, 0.0, 1.0) * (ha * _sigmoid(hg))

        def chunk(ci):
            r = _rows(ci, CHUNK)
            for hd in range(HEADS):
                sl = slice(hd * HEAD_DIM, (hd + 1) * HEAD_DIM)
                rows8 = slice(8 * hd, 8 * hd + 8)
                hus, hvs, u, cdf_u, cdf_v, xhat, rstd, vn, sv = _mixer_a_head(
                    h_ref, r, hd, ga_ref, ba_ref, wsm_ref, bst_ref)
                dy_a = dy_ref[r, sl]
                y_ref[r, sl] = (u * sv).astype(BF16)
                du = dy_a * sv
                dsv = dy_a * u
                dsvb = dsv.astype(BF16)
                acc_bs[hd] += dsv
                acc_ws[hd] += _nt(dsvb, vn)
                dvn = _tn(wsm_ref[hd], dsvb)
                acc_lnag[rows8, :] += _rsum8(dvn * xhat)
                acc_lnab[rows8, :] += _rsum8(dvn)
                dv = _ln_bwd(dvn * ga_ref[hd:hd + 1, :], xhat, rstd)
                slv = slice(D_A + hd * HEAD_DIM, D_A + (hd + 1) * HEAD_DIM)
                dhu = du * (cdf_u + hus * jnp.exp(-0.5 * hus * hus) * INV_SQRT_2PI)
                dhv = dv * (cdf_v + hvs * jnp.exp(-0.5 * hvs * hvs) * INV_SQRT_2PI)
                acc_bin[:, sl] += _rsum8(dhu)
                acc_bin[:, slv] += _rsum8(dhv)
                dh_ref[r, sl] = dhu.astype(BF16)
                dh_ref[r, slv] = dhv.astype(BF16)
            a_b = h_ref[r, 2 * D_A:2 * D_A + D_B]
            g_b = h_ref[r, 2 * D_A + D_B:D_IN]
            ext_ref[pl.ds(HALO_B + ci * CHUNK, CHUNK), :] = a_b * _sigmoid(g_b)

        _loop(n_chunks, chunk)

        def conv_rows(bi):
            base = bi * ROWS
            r = pl.ds(base, ROWS)
            xhat, rstd = _ln_stats(yb1_ref[r, :])
            yb2 = xhat * gb_ref[...] + bb_ref[...]
            sg = _sigmoid(yb2)
            y_ref[r, D_A:D] = (yb2 * sg).astype(BF16)
            dyb2 = dy_ref[r, D_A:D] * (sg * (1.0 + yb2 * (1.0 - sg)))
            acc_lnbg[...] += _rsum8(dyb2 * xhat)
            acc_lnbb[...] += _rsum8(dyb2)
            dyb1 = _ln_bwd(dyb2 * gb_ref[...], xhat, rstd)
            acc_cbb[...] += _rsum8(dyb1)
            dext_ref[r, :] = dyb1
            for k, tap in _taps(ext_ref[pl.ds(base, ROWS + HALO_B), :], CONV_B_OFFSETS):
                acc_cw[8 * k:8 * k + 8, :] += _rsum8(dyb1 * tap)

        _loop(tm // ROWS, conv_rows)

        def convt_rows(bi):
            base = bi * ROWS
            r = pl.ds(base, ROWS)
            dyb0 = jnp.zeros((ROWS, D_B), F32)
            for k, tap in _taps(dext_ref[pl.ds(base, ROWS + HALO_B), :], CONV_B_T_OFFSETS):
                dyb0 = dyb0 + tap * cw_ref[k:k + 1, :]
            a_b = h_ref[r, 2 * D_A:2 * D_A + D_B]
            sg = _sigmoid(h_ref[r, 2 * D_A + D_B:D_IN])
            da_b = dyb0 * sg
            dg_b = dyb0 * a_b * sg * (1.0 - sg)
            acc_bin[:, 2 * D_A:2 * D_A + D_B] += _rsum8(da_b)
            acc_bin[:, 2 * D_A + D_B:D_IN] += _rsum8(dg_b)
            dh_ref[r, 2 * D_A:2 * D_A + D_B] = da_b.astype(BF16)
            dh_ref[r, 2 * D_A + D_B:D_IN] = dg_b.astype(BF16)

        _loop(tm // ROWS, convt_rows)
        dext_ref[tm:tm + HALO_B, :] = dext_ref[0:HALO_B, :]

        acc_wout[...] += _tn(y_ref[...], dmb_ref[...])
        xt = x_ref[...].T.astype(BF16)
        dh_blocks = [dh_ref[:, j * W_IN_BLK:(j + 1) * W_IN_BLK] for j in range(N_DEV)]
        for j in range(N_DEV):
            acc_win[j] += _nn(xt, dh_blocks[j])
        gx_ref[...] += sum(_nt(dh_blocks[j], win_ref[j]) for j in range(N_DEV))

        @pl.when(i == nt - 1)
        def _():
            cps = [pltpu.make_async_copy(acc_win, dwin_ref, sem.at[0]),
                   pltpu.make_async_copy(acc_wout, dwout_ref, sem.at[1])]
            for cp in cps:
                cp.start()
            small_ref[...] = jnp.zeros(small_ref.shape, F32)

            def put_row_vector(row0, acc):
                vec = jnp.sum(acc[...], axis=0, keepdims=True)
                for k in range(vec.shape[1] // 128):
                    small_ref[row0 + k:row0 + k + 1, :] = vec[:, k * 128:(k + 1) * 128]

            put_row_vector(S_BIN, acc_bin)
            put_row_vector(S_CBB, acc_cbb)
            put_row_vector(S_LNBG, acc_lnbg)
            put_row_vector(S_LNBB, acc_lnbb)
            put_row_vector(S_BOUT, acc_bout)
            put_row_vector(S_LN1G, acc_ln1g)
            put_row_vector(S_LN1B, acc_ln1b)
            mask = _tril_mask()
            for hd in range(HEADS):
                rows8 = slice(8 * hd, 8 * hd + 8)
                small_ref[S_LNAG + hd:S_LNAG + hd + 1, :] = jnp.sum(acc_lnag[rows8, :], axis=0, keepdims=True)
                small_ref[S_LNAB + hd:S_LNAB + hd + 1, :] = jnp.sum(acc_lnab[rows8, :], axis=0, keepdims=True)
                small_ref[S_WS + hd * CHUNK:S_WS + (hd + 1) * CHUNK, :] = jnp.where(mask, acc_ws[hd], 0.0)
                small_ref[S_BS + hd:S_BS + hd + 1, :] = jnp.sum(acc_bs[hd].T, axis=0, keepdims=True)
            for k in range(KB):
                dcw_ref[k:k + 1, :] = jnp.sum(acc_cw[8 * k:8 * k + 8, :], axis=0, keepdims=True)
            for cp in cps:
                cp.wait()
            for cp in _chip_copies(p_refs, land_refs, send_sems, recv_sems):
                cp.wait()

    rev = lambda i: nt - 1 - i
    row = lambda w: pl.BlockSpec((tm, w), lambda i: (rev(i), 0))
    return pl.pallas_call(
        body, name="mix_backward", grid=(nt,),
        in_specs=[row(D), row(D_IN),
                  pl.BlockSpec((HALO_B, 2 * D_B), lambda i: (jnp.maximum(rev(i) * halo_blocks - 1, 0), 1)),
                  row(D_B), pl.BlockSpec((N_F, tm, D), lambda i: (0, rev(i), 0)),
                  row(D), row(D), row(128), _resident(win_g.shape), _full(ln_a_g.shape),
                  _full(ln_a_b.shape), _full(w_spatial.shape), _full(bst.shape), _full(conv_b_w.shape),
                  _full(ln_b_g.shape), _full(ln_b_b.shape),
                  _resident(wout.shape), _full(ln1_g.shape)] + [ANY] * n_p,
        out_specs=[row(D), ANY, ANY, _full((KB, D_B)), _full((S_MIX_ROWS, 128))] + [ANY] * n_p,
        out_shape=[jax.ShapeDtypeStruct((t, D), F32), jax.ShapeDtypeStruct((N_DEV, D, W_IN_BLK), F32),
                   jax.ShapeDtypeStruct((D, D), F32), jax.ShapeDtypeStruct((KB, D_B), F32),
                   jax.ShapeDtypeStruct((S_MIX_ROWS, 128), F32)]
        + [jax.ShapeDtypeStruct(p.shape, BF16) for p in ffn_partials],
        scratch_shapes=[pltpu.VMEM((tm + HALO_B, D_B), F32), pltpu.VMEM((tm + HALO_B, D_B), F32),
                        pltpu.VMEM((tm, D), BF16), pltpu.VMEM((tm, D), F32), pltpu.VMEM((tm, D_IN), BF16),
                        pltpu.VMEM((tm, D), BF16),
                        pltpu.VMEM((HEADS, CHUNK, CHUNK), BF16),
                        pltpu.VMEM((N_DEV, D, W_IN_BLK), F32), pltpu.VMEM((D, D), F32),
                        pltpu.VMEM((8, D_IN), F32), pltpu.VMEM((8 * HEADS, HEAD_DIM), F32),
                        pltpu.VMEM((8 * HEADS, HEAD_DIM), F32), pltpu.VMEM((HEADS, CHUNK, CHUNK), F32),
                        pltpu.VMEM((HEADS, CHUNK, CHUNK), F32), pltpu.VMEM((8, D_B), F32),
                        pltpu.VMEM((8, D_B), F32), pltpu.VMEM((8, D_B), F32), pltpu.VMEM((8, D), F32),
                        pltpu.VMEM((8, D), F32), pltpu.VMEM((8, D), F32), pltpu.VMEM((8 * KB, D_B), F32),
                        pltpu.SemaphoreType.DMA((2,)),
                        pltpu.SemaphoreType.DMA((n_p, 3)), pltpu.SemaphoreType.DMA((n_p, 3))],
        compiler_params=_params(("arbitrary",)),
    )(x, h, h, yb1, dx1p, dr2, xhat1, rstd1, win_g, ln_a_g, ln_a_b, w_spatial, bst, conv_b_w,
      ln_b_g, ln_b_b, wout, ln1_g, *ffn_partials)


def _rows128(a):
    return a.reshape(-1, 128)


def _pack_conv(cb, cf):
    lead = cb.shape[:-2]
    pad = [(0, 0)] * len(lead)
    flat = jnp.pad(cb.reshape(lead + (KB * 64,)), pad + [(0, 3 * W_UP_BLK - KB * 64)])
    rows = jnp.concatenate([cf, flat.reshape(lead + (3, W_UP_BLK))], axis=-2)
    return jnp.pad(rows, pad + [(0, 2), (0, 768 - W_UP_BLK)])


def _unpack_conv(p):
    lead = p.shape[:-2]
    cf = p[..., 0:KF, 0:W_UP_BLK]
    cb = p[..., 3:6, 0:W_UP_BLK].reshape(lead + (3 * W_UP_BLK,))[..., :KB * 64].reshape(lead + (KB, 64))
    return cb, cf


def kernel(x, w_in, b_in, ln_a_g, ln_a_b, w_spatial, b_spatial, conv_b_w, conv_b_b, ln_b_g, ln_b_b, w_out, b_out, ln1_g, ln1_b, w_up, conv_f_w, conv_f_b, w_down, ln2_g, ln2_b, loss_target, m_w_in, m_b_in, m_ln_a_g, m_ln_a_b, m_w_spatial, m_b_spatial, m_conv_b_w, m_conv_b_b, m_ln_b_g, m_ln_b_b, m_w_out, m_b_out, m_ln1_g, m_ln1_b, m_w_up, m_conv_f_w, m_conv_f_b, m_w_down, m_ln2_g, m_ln2_b, v_w_in, v_b_in, v_ln_a_g, v_ln_a_b, v_w_spatial, v_b_spatial, v_conv_b_w, v_conv_b_b, v_ln_b_g, v_ln_b_b, v_w_out, v_b_out, v_ln1_g, v_ln1_b, v_w_up, v_conv_f_w, v_conv_f_b, v_w_down, v_ln2_g, v_ln2_b):
    t = x.shape[1]
    x2 = x.reshape(t, D)
    target = loss_target.reshape(t, D)
    tm_fwd = min(t, 512)
    tm_bwd = min(t, 256)
    tm_ffn_bwd = min(t, 512)

    xi, yi, ci = _mesh_pos()
    jidx = jnp.stack([_lid(px, py, ci) for px, py in _chip_patterns(xi, yi)]).astype(jnp.int32)

    sin, sout, sup, sdown, conv_g = prepare_weights(w_in, w_out, w_up.T, w_down, _pack_conv(conv_b_w, conv_f_w))
    conv_b_all, cfw = _unpack_conv(conv_g)
    conv_b_full = conv_b_all.transpose(1, 0, 2).reshape(KB, D_B)
    cfb = conv_f_b.reshape(N_DEV, W_UP_BLK)
    row = lambda a: a.reshape(1, -1)
    bst = b_spatial.T

    h, xhat1, rstd1, yb1, win_g, wout_g, wup_g, wdown_g = mix_forward(
        x2, sin, sout, row(b_in), ln_a_g, ln_a_b, w_spatial, bst, conv_b_full, row(conv_b_b),
        row(ln_b_g), row(ln_b_b), row(b_out), row(ln1_g), row(ln1_b), sup, sdown, tm_fwd)
    wout_full = wout_g.reshape(D, D)
    wdown4 = wdown_g.reshape(N_F, W_UP_BLK, D)
    hu, gv, dr2, loss_part, s_ln2 = ffn_forward(
        xhat1, row(ln1_g), row(ln1_b), wup_g, cfw, cfb, wdown4, row(ln2_g), row(ln2_b), target, tm_bwd)

    order = jnp.where(ci == 0, jnp.array([1, 3, 0, 2], jnp.int32), jnp.array([0, 2, 1, 3], jnp.int32))
    dwup, dwdown, dcfw, dcfb, dx1p, *ffn_lands = ffn_backward(
        order, dr2, xhat1, row(ln1_g), row(ln1_b), hu, gv, wup_g, cfw, wdown4, tm_ffn_bwd)
    ffn_grads = [dwup.reshape(N_DEV, W_UP_BLK, D), dwdown.reshape(N_DEV, D_FF // N_DEV, D)]
    ffn_partials = [chip_partials("chip_partials_" + nm, g, l, jidx, rb)
                    for nm, g, l, rb in zip(["w_up", "w_down"], ffn_grads, ffn_lands, [352, 352])]
    grad_x, dwin, dwout, dcw, s_mix, *ffn_recvs = mix_backward(
        x2, h, yb1, dx1p, dr2, xhat1, rstd1, win_g, ln_a_g, ln_a_b, w_spatial, bst,
        conv_b_full, row(ln_b_g), row(ln_b_b), wout_full, row(ln1_g), ffn_partials, tm_bwd)

    dcfb_rows = jnp.pad(dcfb.reshape(-1, 128), ((0, 4), (0, 0)))
    svec = jnp.concatenate([s_mix, dcfb_rows, s_ln2, loss_part], axis=0)
    dconv = _pack_conv(dcw.reshape(KB, N_DEV, 64).transpose(1, 0, 2), dcfw.reshape(N_DEV, KF, W_UP_BLK))
    mix_grads = [dwin, dwout.reshape(N_DEV, D // N_DEV, D), dconv]
    mix_w = [w_in, w_out, _pack_conv(conv_b_w, conv_f_w)]
    mix_m = [m_w_in, m_w_out, _pack_conv(m_conv_b_w, m_conv_f_w)]
    mix_v = [v_w_in, v_w_out, _pack_conv(v_conv_b_w, v_conv_f_w)]
    *mix_out, sv_slots = mixer_reduce_adamw(mix_grads, svec, mix_w, mix_m, mix_v)
    big = {nm: [mix_out[k * 3 + p] for k in range(4)] for p, nm in enumerate(["w_in", "w_out", "conv"])}

    ffn_w = [(w_up.T, m_w_up.T, v_w_up.T), (w_down, m_w_down, v_w_down)]
    for nm, g, l, r, (w, m, v) in zip(["w_up", "w_down"], ffn_grads, ffn_lands, ffn_recvs, ffn_w):
        big[nm] = reduce_and_adamw("reduce_adamw_" + nm, g, l, r, w, m, v, jidx, 352)
    big["w_up"] = [o.T for o in big["w_up"]]
    for k in range(4):
        cb_k, cf_k = _unpack_conv(big["conv"][k])
        big.setdefault("conv_b_w", []).append(cb_k)
        big.setdefault("conv_f_w", []).append(cf_k)

    small_w = dict(b_in=b_in, ln_a_g=ln_a_g, ln_a_b=ln_a_b, w_spatial=w_spatial, b_spatial=b_spatial,
                   conv_b_b=conv_b_b, ln_b_g=ln_b_g, ln_b_b=ln_b_b, b_out=b_out, ln1_g=ln1_g,
                   ln1_b=ln1_b, conv_f_b=conv_f_b, ln2_g=ln2_g, ln2_b=ln2_b)
    small_m = dict(b_in=m_b_in, ln_a_g=m_ln_a_g, ln_a_b=m_ln_a_b, w_spatial=m_w_spatial,
                   b_spatial=m_b_spatial, conv_b_b=m_conv_b_b, ln_b_g=m_ln_b_g, ln_b_b=m_ln_b_b,
                   b_out=m_b_out, ln1_g=m_ln1_g, ln1_b=m_ln1_b, conv_f_b=m_conv_f_b, ln2_g=m_ln2_g,
                   ln2_b=m_ln2_b)
    small_v = dict(b_in=v_b_in, ln_a_g=v_ln_a_g, ln_a_b=v_ln_a_b, w_spatial=v_w_spatial,
                   b_spatial=v_b_spatial, conv_b_b=v_conv_b_b, ln_b_g=v_ln_b_g, ln_b_b=v_ln_b_b,
                   b_out=v_b_out, ln1_g=v_ln1_g, ln1_b=v_ln1_b, conv_f_b=v_conv_f_b, ln2_g=v_ln2_g,
                   ln2_b=v_ln2_b)
    order = [nm for nm, _, _ in SMALL_LAYOUT]
    small_out = small_adamw(sv_slots, [_rows128(small_w[nm]) for nm in order],
                            [_rows128(small_m[nm]) for nm in order], [_rows128(small_v[nm]) for nm in order])
    n_small = len(order)
    small = {nm: [small_out[k * n_small + p].reshape(small_w[nm].shape) for k in range(4)]
             for p, nm in enumerate(order)}
    loss = jnp.sum(small_out[4 * n_small]) * (0.5 / D)

    weights = ["w_in", "b_in", "ln_a_g", "ln_a_b", "w_spatial", "b_spatial", "conv_b_w", "conv_b_b",
               "ln_b_g", "ln_b_b", "w_out", "b_out", "ln1_g", "ln1_b", "w_up", "conv_f_w", "conv_f_b",
               "w_down", "ln2_g", "ln2_b"]
    result = lambda nm, k: big[nm][k] if nm in big else small[nm][k]
    return (loss, grad_x.reshape(x.shape), *[result(nm, 0) for nm in weights],
            *[result(nm, 1) for nm in weights], *[result(nm, 2) for nm in weights],
            *[result(nm, 3) for nm in weights])
```

```python
import functools
import math

import jax
import jax.numpy as jnp
from jax import lax
from jax.experimental import pallas as pl
from jax.experimental.pallas import tpu as pltpu

F32 = jnp.float32
BF16 = jnp.bfloat16

D = 1024
D_A = 512
D_B = 512
HEADS = 4
HEAD_DIM = 128
CHUNK = 128
KB = 31
KF = 3
D_FF = 2816
D_IN = 2048
N_DEV = 8
W_IN_BLK = D_IN // N_DEV
W_UP_BLK = 2 * D_FF // N_DEV
N_F = 4
LN_EPS = 1e-5
ALPHA = 2.0 ** 0.25

ADAM_LR = 0.001
ADAM_B1 = 0.9
ADAM_B2 = 0.999
ADAM_EPS = 1e-08
ADAM_WD = 0.01
ADAM_STEP = 10

INV_SQRT2 = 1.0 / math.sqrt(2.0)
INV_SQRT_2PI = 1.0 / math.sqrt(2.0 * math.pi)

HALO_B = 32
HALO_F = 8
ROWS = 64
LN_ROWS = 32
VMEM_LIMIT = 58 * 1024 * 1024

MESH = pl.DeviceIdType.MESH
ANY = pl.BlockSpec(memory_space=pl.ANY)
VMEM = pl.BlockSpec(memory_space=pltpu.VMEM)

S_BIN, S_LNAG, S_LNAB, S_WS, S_BS, S_CBB, S_LNBG, S_LNBB, S_BOUT, S_LN1G, S_LN1B = (
    0, 16, 24, 32, 544, 552, 560, 568, 576, 584, 592)
S_MIX_ROWS = 600
S_CFB = 600
S_LN2G = 648
S_LN2B = 656
S_LOSS = 664
S_ROWS = 672


def _tn(a, b):
    return lax.dot_general(a, b, (((0,), (0,)), ((), ())), preferred_element_type=F32)


def _nt(a, b):
    return lax.dot_general(a, b, (((1,), (1,)), ((), ())), preferred_element_type=F32)


def _nn(a, b):
    return jnp.dot(a, b, preferred_element_type=F32)


def _sigmoid(x):
    return 1.0 / (1.0 + jnp.exp(-x))


def _ln_stats(x):
    mu = jnp.mean(x, axis=-1, keepdims=True)
    xc = x - mu
    var = jnp.mean(xc * xc, axis=-1, keepdims=True)
    rstd = lax.rsqrt(var + LN_EPS)
    return xc * rstd, rstd


def _ln_bwd(dxhat, xhat, rstd):
    m1 = jnp.mean(dxhat, axis=-1, keepdims=True)
    m2 = jnp.mean(dxhat * xhat, axis=-1, keepdims=True)
    return rstd * (dxhat - m1 - xhat * m2)


def _rsum8(x):
    r, n = x.shape
    return x.reshape(r // 8, 8, n).sum(axis=0)


def _rows(i, n=ROWS):
    return pl.ds(i * n, n)


def _loop(n, body):
    for i in range(n):
        body(i)


def _tril_mask():
    r = lax.broadcasted_iota(jnp.int32, (CHUNK, CHUNK), 0)
    c = lax.broadcasted_iota(jnp.int32, (CHUNK, CHUNK), 1)
    return c <= r


def _mixer_a_head(h_ref, r, hd, ga_ref, ba_ref, wsm_ref, bst_ref):
    sl = slice(hd * HEAD_DIM, (hd + 1) * HEAD_DIM)
    hu = h_ref[r, sl]
    hv = h_ref[r, D_A + hd * HEAD_DIM:D_A + (hd + 1) * HEAD_DIM]
    cdf_u = 0.5 * (1.0 + lax.erf(hu * INV_SQRT2))
    cdf_v = 0.5 * (1.0 + lax.erf(hv * INV_SQRT2))
    u = hu * cdf_u
    xhat, rstd = _ln_stats(hv * cdf_v)
    vn = (xhat * ga_ref[hd:hd + 1, :] + ba_ref[hd:hd + 1, :]).astype(BF16)
    sv = _nn(wsm_ref[hd], vn) + bst_ref[:, hd:hd + 1]
    return hu, hv, u, cdf_u, cdf_v, xhat, rstd, vn, sv


def _taps(win, offsets):
    n = win.shape[0]
    for s in range(8):
        ks = [k for k, o in enumerate(offsets) if o % 8 == s]
        if ks:
            moved = win if s == 0 else pltpu.roll(win, n - s, 0)
            for k in ks:
                yield k, moved[offsets[k] - s:offsets[k] - s + ROWS, :]


CONV_B_OFFSETS = [2 + k for k in range(KB)]
CONV_B_T_OFFSETS = [30 - k for k in range(KB)]


def _conv_b_block(ext_ref, base, cw_ref):
    acc = jnp.zeros((ROWS, D_B), F32)
    for k, tap in _taps(ext_ref[pl.ds(base, ROWS + HALO_B), :], CONV_B_OFFSETS):
        acc = acc + tap * cw_ref[k:k + 1, :]
    return acc


def _taps_f(win):
    n = ROWS + HALO_F
    return [pltpu.roll(win, n - 6, 0)[0:ROWS, :], pltpu.roll(win, n - 7, 0)[0:ROWS, :], win[8:n, :]]


def _params(sem, **kw):
    return pltpu.CompilerParams(dimension_semantics=sem, vmem_limit_bytes=VMEM_LIMIT, **kw)


def _resident(shape):
    zeros = (0,) * len(shape)
    return pl.BlockSpec(shape, lambda *_: zeros, pipeline_mode=pl.Buffered(1))


def _full(shape):
    zeros = (0,) * len(shape)
    return pl.BlockSpec(shape, lambda *_: zeros)


def _mesh_pos():
    return lax.axis_index("x"), lax.axis_index("y"), lax.axis_index("c")


def _chip_patterns(x, y):
    return [(x, y), (1 - x, y), (x, 1 - y), (1 - x, 1 - y)]


def _lid(x, y, c):
    return 4 * x + 2 * y + c


def _gather_copy(outs, send_sems, recv_sems, a, k, block, to, src=None):
    blk = outs[a].at[_lid(*block)]
    return pltpu.make_async_remote_copy(
        src_ref=blk if src is None else src, dst_ref=blk,
        send_sem=send_sems.at[a, k], recv_sem=recv_sems.at[a, k], device_id=to, device_id_type=MESH)


def _gather_start(mine, outs, send_sems, recv_sems, local_sems):
    x, y, c = _mesh_pos()
    me = (x, y, c)
    for a in range(len(mine)):
        pltpu.make_async_copy(mine[a], outs[a].at[_lid(*me)], local_sems.at[a]).start()
        for k, to in enumerate([(x, y, 1 - c), (1 - x, y, c), (x, 1 - y, c)]):
            _gather_copy(outs, send_sems, recv_sems, a, k, me, to, src=mine[a]).start()


def _gather_relay(mine, outs, send_sems, recv_sems, local_sems, via):
    x, y, c = _mesh_pos()
    me, sib = (x, y, c), (x, y, 1 - c)
    copy = functools.partial(_gather_copy, outs, send_sems, recv_sems)
    source = {1: (1 - x, y, c), 2: (x, 1 - y, c)}
    for a in range(len(mine)):
        for k in (via[a], 3 - via[a]):
            copy(a, k, source[k], me).wait_recv()
            if k == via[a]:
                copy(a, 3, source[k], source[3 - k]).start()
            copy(a, 3 + k, source[k], sib).start()


def _gather_finish(mine, outs, send_sems, recv_sems, local_sems):
    x, y, c = _mesh_pos()
    me, sib = (x, y, c), (x, y, 1 - c)
    copy = functools.partial(_gather_copy, outs, send_sems, recv_sems)
    diag = (1 - x, 1 - y)
    n = len(mine)
    for a in range(n):
        copy(a, 3, (*diag, c), me).wait_recv()
        copy(a, 6, (*diag, c), sib).start()
    for a in range(n):
        copy(a, 0, sib, me).wait_recv()
        for k, chip in zip((4, 5, 6), [(1 - x, y), (x, 1 - y), diag]):
            copy(a, k, (*chip, 1 - c), me).wait_recv()
        for k in range(7):
            copy(a, k, me, sib, src=mine[a]).wait_send()
        pltpu.make_async_copy(mine[a], outs[a].at[_lid(*me)], local_sems.at[a]).wait()


def _gather_scratch(n):
    return [pltpu.SemaphoreType.DMA((n, 7)), pltpu.SemaphoreType.DMA((n, 7)), pltpu.SemaphoreType.DMA((n,))]


def prepare_weights(w_in, w_out, w_up_t, w_down, convp):
    def body(win_ref, wout_ref, wup_ref, wdown_ref, convp_ref,
             sin_ref, sout_ref, sup_ref, sdown_ref, gconv_ref, send_sems, recv_sems, local_sems):
        gather = ([convp_ref], [gconv_ref], send_sems, recv_sems, local_sems)
        _gather_start(*gather)
        sin_ref[...] = win_ref[...].astype(BF16)
        sout_ref[...] = wout_ref[...].astype(BF16)
        sup_ref[...] = wup_ref[...].T.astype(BF16)
        sdown_ref[...] = wdown_ref[...].astype(BF16)
        _gather_relay(*gather, via=[1])
        _gather_finish(*gather)

    return pl.pallas_call(
        body, name="prepare_weights",
        out_shape=[jax.ShapeDtypeStruct(w_in.shape, BF16), jax.ShapeDtypeStruct(w_out.shape, BF16),
                   jax.ShapeDtypeStruct(w_up_t.shape[::-1], BF16), jax.ShapeDtypeStruct(w_down.shape, BF16),
                   jax.ShapeDtypeStruct((N_DEV,) + convp.shape, F32)],
        in_specs=[VMEM] * 5, out_specs=[VMEM] * 4 + [ANY],
        scratch_shapes=_gather_scratch(1),
        compiler_params=pltpu.CompilerParams(vmem_limit_bytes=VMEM_LIMIT),
    )(w_in, w_out, w_up_t, w_down, convp)


def _chip_copies(p, land, send_sems, recv_sems):
    x, y, c = _mesh_pos()
    return [pltpu.make_async_remote_copy(
        src_ref=p[a].at[k], dst_ref=land[a].at[k], send_sem=send_sems.at[a, k], recv_sem=recv_sems.at[a, k],
        device_id=(px, py, c), device_id_type=MESH)
        for k, (px, py) in enumerate(_chip_patterns(x, y)[1:]) for a in range(len(p))]


def chip_partials(name, g, land, jidx, rb):
    _, r, c = g.shape

    def body(j_ref, g_ref, l_ref, o_ref):
        o_ref[...] = (g_ref[...] + l_ref[...]).astype(BF16)

    return pl.pallas_call(
        body, name=name,
        out_shape=jax.ShapeDtypeStruct((3, r, c), BF16),
        grid_spec=pltpu.PrefetchScalarGridSpec(
            num_scalar_prefetch=1, grid=(3, r // rb),
            in_specs=[pl.BlockSpec((1, rb, c), lambda k, i, j: (j[1 + k], i, 0)),
                      pl.BlockSpec((1, rb, c), lambda k, i, j: (1 + k, i, 0))],
            out_specs=pl.BlockSpec((1, rb, c), lambda k, i, j: (k, i, 0))),
        compiler_params=_params(("arbitrary", "arbitrary")),
    )(jidx, g, land)


def _adamw(w, g, m, v):
    m2 = ADAM_B1 * m + (1.0 - ADAM_B1) * g
    v2 = ADAM_B2 * v + (1.0 - ADAM_B2) * (g * g)
    m_hat = m2 / (1.0 - ADAM_B1 ** ADAM_STEP)
    v_hat = v2 / (1.0 - ADAM_B2 ** ADAM_STEP)
    delta = -ADAM_LR * (m_hat / (jnp.sqrt(v_hat) + ADAM_EPS) + ADAM_WD * w)
    return delta, m2, v2


def reduce_and_adamw(name, g, land, recv, w, m, v, jidx, rb):
    _, r, c = g.shape

    def body(j_ref, g_ref, l_ref, r_ref, w_ref, m_ref, v_ref, go_ref, do_ref, mo_ref, vo_ref):
        grad = (g_ref[0] + l_ref[0]) + r_ref[0].astype(F32) + r_ref[1].astype(F32) + r_ref[2].astype(F32)
        delta, m2, v2 = _adamw(w_ref[...], grad, m_ref[...], v_ref[...])
        go_ref[...] = grad
        do_ref[...] = delta
        mo_ref[...] = m2
        vo_ref[...] = v2

    blk = pl.BlockSpec((rb, c), lambda i, j: (i, 0))
    return pl.pallas_call(
        body, name=name,
        out_shape=[jax.ShapeDtypeStruct((r, c), F32)] * 4,
        grid_spec=pltpu.PrefetchScalarGridSpec(
            num_scalar_prefetch=1, grid=(r // rb,),
            in_specs=[pl.BlockSpec((1, rb, c), lambda i, j: (j[0], i, 0)),
                      pl.BlockSpec((1, rb, c), lambda i, j: (0, i, 0)),
                      pl.BlockSpec((3, rb, c), lambda i, j: (0, i, 0)),
                      blk, blk, blk],
            out_specs=[blk] * 4),
        compiler_params=_params(("arbitrary",)),
    )(jidx, g, land, recv, w, m, v)


def mixer_reduce_adamw(grads, svec, ws, ms, vs):
    n = len(grads)
    shard = [g.shape[1:] for g in grads]

    def body(*refs):
        g = refs[:n]
        sv_ref = refs[n]
        w, m, v = refs[n + 1:2 * n + 1], refs[2 * n + 1:3 * n + 1], refs[3 * n + 1:4 * n + 1]
        outs = refs[4 * n + 1:8 * n + 1]
        sv_slots = refs[8 * n + 1]
        rest = refs[8 * n + 2:]
        own, land, sendb, recvb = rest[:n], rest[n:2 * n], rest[2 * n:3 * n], rest[3 * n:4 * n]
        sv_land, chip_sv, d2d_send, d2d_recv, ici_send, ici_recv, local_sems, sv_sems = rest[4 * n:]
        x, y, c = _mesh_pos()
        sib = (x, y, 1 - c)
        pats = _chip_patterns(x, y)
        q = 2 * x + y

        d2d, local = [], []
        for a in range(n):
            for k, (px, py) in enumerate(pats):
                d2d.append(pltpu.make_async_remote_copy(
                    src_ref=g[a].at[_lid(px, py, 1 - c)], dst_ref=land[a].at[k],
                    send_sem=d2d_send.at[a, k], recv_sem=d2d_recv.at[a, k], device_id=sib, device_id_type=MESH))
                local.append(pltpu.make_async_copy(g[a].at[_lid(px, py, c)], own[a].at[k], local_sems.at[a, k]))
        d2d.append(pltpu.make_async_remote_copy(
            src_ref=sv_ref, dst_ref=sv_land, send_sem=d2d_send.at[n, 0], recv_sem=d2d_recv.at[n, 0],
            device_id=sib, device_id_type=MESH))
        for cp in d2d + local:
            cp.start()
        for cp in local + d2d:
            cp.wait()

        for a in range(n):
            for k in range(3):
                sendb[a][k] = (own[a][1 + k] + land[a][1 + k]).astype(BF16)
        chip_sv[...] = sv_ref[...] + sv_land[...]
        ici = _chip_copies(sendb, recvb, ici_send, ici_recv)
        sv_local = pltpu.make_async_copy(chip_sv, sv_slots.at[q], sv_sems.at[0])
        sv_out = [pltpu.make_async_remote_copy(
            src_ref=chip_sv, dst_ref=sv_slots.at[q], send_sem=sv_sems.at[1 + k], recv_sem=sv_sems.at[4 + k],
            device_id=(px, py, c), device_id_type=MESH) for k, (px, py) in enumerate(pats[1:])]
        for cp in ici + sv_out + [sv_local]:
            cp.start()
        for cp in ici:
            cp.wait()
        for k, (px, py) in enumerate(pats[1:]):
            sv_out[k].wait_send()
            pltpu.make_async_remote_copy(
                src_ref=chip_sv, dst_ref=sv_slots.at[2 * px + py], send_sem=sv_sems.at[1 + k],
                recv_sem=sv_sems.at[4 + k], device_id=(px, py, c), device_id_type=MESH).wait_recv()
        sv_local.wait()

        for a in range(n):
            grad = ((own[a][0] + land[a][0]) + recvb[a][0].astype(F32) + recvb[a][1].astype(F32)
                    + recvb[a][2].astype(F32))
            delta, m2, v2 = _adamw(w[a][...], grad, m[a][...], v[a][...])
            outs[a][...] = grad
            outs[n + a][...] = delta
            outs[2 * n + a][...] = m2
            outs[3 * n + a][...] = v2

    shard_out = [jax.ShapeDtypeStruct(s, F32) for s in shard]
    return pl.pallas_call(
        body, name="mixer_reduce_adamw",
        out_shape=shard_out * 4 + [jax.ShapeDtypeStruct((4,) + svec.shape, F32)],
        in_specs=[ANY] * n + [VMEM] * (1 + 3 * n), out_specs=[VMEM] * (4 * n) + [ANY],
        scratch_shapes=[pltpu.VMEM((4,) + s, F32) for s in shard] + [pltpu.VMEM((4,) + s, F32) for s in shard]
        + [pltpu.VMEM((3,) + s, BF16) for s in shard] + [pltpu.VMEM((3,) + s, BF16) for s in shard]
        + [pltpu.VMEM(svec.shape, F32), pltpu.VMEM(svec.shape, F32),
           pltpu.SemaphoreType.DMA((n + 1, 4)), pltpu.SemaphoreType.DMA((n + 1, 4)),
           pltpu.SemaphoreType.DMA((n, 3)), pltpu.SemaphoreType.DMA((n, 3)),
           pltpu.SemaphoreType.DMA((n, 4)), pltpu.SemaphoreType.DMA((7,))],
        compiler_params=pltpu.CompilerParams(vmem_limit_bytes=VMEM_LIMIT),
    )(*grads, svec, *ws, *ms, *vs)


SMALL_LAYOUT = [
    ("b_in", S_BIN, 16), ("ln_a_g", S_LNAG, 4), ("ln_a_b", S_LNAB, 4), ("w_spatial", S_WS, 512),
    ("b_spatial", S_BS, 4), ("conv_b_b", S_CBB, 4), ("ln_b_g", S_LNBG, 4), ("ln_b_b", S_LNBB, 4),
    ("b_out", S_BOUT, 8), ("ln1_g", S_LN1G, 8), ("ln1_b", S_LN1B, 8), ("conv_f_b", S_CFB, 44),
    ("ln2_g", S_LN2G, 8), ("ln2_b", S_LN2B, 8),
]


def small_adamw(sv_slots, ws, ms, vs):
    n = len(SMALL_LAYOUT)

    def body(*refs):
        s_ref = refs[0]
        w_refs, m_refs, v_refs = refs[1:1 + n], refs[1 + n:1 + 2 * n], refs[1 + 2 * n:1 + 3 * n]
        outs = refs[1 + 3 * n:]
        for p, (_, row0, rows) in enumerate(SMALL_LAYOUT):
            sl = pl.ds(row0, rows)
            grad = ((s_ref[0, sl, :] + s_ref[1, sl, :]) + s_ref[2, sl, :]) + s_ref[3, sl, :]
            delta, m2, v2 = _adamw(w_refs[p][...], grad, m_refs[p][...], v_refs[p][...])
            outs[p][...] = grad
            outs[n + p][...] = delta
            outs[2 * n + p][...] = m2
            outs[3 * n + p][...] = v2
        sl = pl.ds(S_LOSS, 8)
        outs[4 * n][...] = ((s_ref[0, sl, :] + s_ref[1, sl, :]) + s_ref[2, sl, :]) + s_ref[3, sl, :]

    shapes = [jax.ShapeDtypeStruct((rows, 128), F32) for _, _, rows in SMALL_LAYOUT]
    return pl.pallas_call(
        body, name="small_adamw", out_shape=shapes * 4 + [jax.ShapeDtypeStruct((8, 128), F32)],
        in_specs=[VMEM] * (1 + 3 * n), out_specs=[VMEM] * (4 * n + 1),
    )(sv_slots, *ws, *ms, *vs)


def mix_forward(x, sin, sout, b_in, ln_a_g, ln_a_b, w_spatial, bst, conv_b_w, conv_b_b, ln_b_g, ln_b_b,
                b_out, ln1_g, ln1_b, sup, sdown, tm):
    t = x.shape[0]
    nt = t // tm
    n_chunks = tm // CHUNK

    def body(x_ref, sin_ref, sout_ref, bin_ref, ga_ref, ba_ref, ws_ref, bst_ref, cw_ref, cb_ref, gb_ref,
             bb_ref, bout_ref, g1_ref, b1_ref, sup_ref, sdown_ref,
             h_ref, xhat1_ref, rstd1_ref, yb1_ref, gin_ref, gout_ref, gup_ref, gdown_ref,
             ext_ref, y_ref, wsm_ref, win_ref, wout_ref, load_sems,
             mix_send, mix_recv, mix_local, send_sems, recv_sems, local_sems):
        i = pl.program_id(0)
        mixer = ([sin_ref, sout_ref], [gin_ref, gout_ref], mix_send, mix_recv, mix_local)
        gather = ([sup_ref, sdown_ref], [gup_ref, gdown_ref], send_sems, recv_sems, local_sems)

        @pl.when(i == 0)
        def _():
            _gather_start(*mixer)
            _gather_start(*gather)
            _gather_relay(*mixer, via=[2, 2])
            _gather_finish(*mixer)
            loads = [pltpu.make_async_copy(gin_ref, win_ref, load_sems.at[0]),
                     pltpu.make_async_copy(gout_ref, wout_ref, load_sems.at[1])]
            for cp in loads:
                cp.start()
            for cp in loads:
                cp.wait()
            ext_ref[0:HALO_B, :] = jnp.zeros((HALO_B, D_B), F32)
            mask = _tril_mask()
            for hd in range(HEADS):
                wsm_ref[hd] = jnp.where(mask, ws_ref[hd], 0.0).astype(BF16)

        xb = x_ref[...].astype(BF16)
        for j in range(N_DEV):
            cols = slice(j * W_IN_BLK, (j + 1) * W_IN_BLK)
            h_ref[:, cols] = _nn(xb, win_ref[j]) + bin_ref[:, cols]

        def chunk(ci):
            r = _rows(ci, CHUNK)
            for hd in range(HEADS):
                _, _, u, _, _, _, _, _, sv = _mixer_a_head(h_ref, r, hd, ga_ref, ba_ref, wsm_ref, bst_ref)
                y_ref[r, hd * HEAD_DIM:(hd + 1) * HEAD_DIM] = (u * sv).astype(BF16)
            a_b = h_ref[r, 2 * D_A:2 * D_A + D_B]
            g_b = h_ref[r, 2 * D_A + D_B:D_IN]
            ext_ref[pl.ds(HALO_B + ci * CHUNK, CHUNK), :] = a_b * _sigmoid(g_b)

        _loop(n_chunks, chunk)

        def conv_rows(bi):
            base = bi * ROWS
            yb1 = _conv_b_block(ext_ref, base, cw_ref) + cb_ref[...]
            yb1_ref[pl.ds(base, ROWS), :] = yb1
            xhat, _ = _ln_stats(yb1)
            yb2 = xhat * gb_ref[...] + bb_ref[...]
            y_ref[pl.ds(base, ROWS), D_A:D] = (yb2 * _sigmoid(yb2)).astype(BF16)

        _loop(tm // ROWS, conv_rows)
        ext_ref[0:HALO_B, :] = ext_ref[tm:tm + HALO_B, :]

        mix = _nn(y_ref[...], wout_ref[...].reshape(D, D)) + bout_ref[...]
        xhat1, rstd1 = _ln_stats(ALPHA * x_ref[...] + mix)
        xhat1_ref[...] = xhat1
        rstd1_ref[...] = jnp.broadcast_to(rstd1, (tm, 128))

        @pl.when(i == nt // 2)
        def _():
            _gather_relay(*gather, via=[1, 2])

        @pl.when(i == nt - 1)
        def _():
            _gather_finish(*gather)

    row = lambda w: pl.BlockSpec((tm, w), lambda i: (i, 0))
    return pl.pallas_call(
        body, name="mix_forward", grid=(nt,),
        in_specs=[row(D), ANY, ANY, _full(b_in.shape), _full(ln_a_g.shape),
                  _full(ln_a_b.shape), _full(w_spatial.shape), _full(bst.shape),
                  _full(conv_b_w.shape), _full(conv_b_b.shape), _full(ln_b_g.shape),
                  _full(ln_b_b.shape), _full(b_out.shape),
                  _full(ln1_g.shape), _full(ln1_b.shape), ANY, ANY],
        out_specs=[row(D_IN), row(D), row(128), row(D_B), ANY, ANY, ANY, ANY],
        out_shape=[jax.ShapeDtypeStruct((t, D_IN), F32), jax.ShapeDtypeStruct((t, D), F32),
                   jax.ShapeDtypeStruct((t, 128), F32), jax.ShapeDtypeStruct((t, D_B), F32)]
        + [jax.ShapeDtypeStruct((N_DEV,) + sh.shape, BF16) for sh in (sin, sout, sup, sdown)],
        scratch_shapes=[pltpu.VMEM((tm + HALO_B, D_B), F32), pltpu.VMEM((tm, D), BF16),
                        pltpu.VMEM((HEADS, CHUNK, CHUNK), BF16),
                        pltpu.VMEM((N_DEV,) + sin.shape, BF16), pltpu.VMEM((N_DEV,) + sout.shape, BF16),
                        pltpu.SemaphoreType.DMA((2,))] + _gather_scratch(2) + _gather_scratch(2),
        compiler_params=_params(("arbitrary",)),
    )(x, sin, sout, b_in, ln_a_g, ln_a_b, w_spatial, bst, conv_b_w, conv_b_b, ln_b_g, ln_b_b,
      b_out, ln1_g, ln1_b, sup, sdown)


def ffn_forward(xhat1, ln1_g, ln1_b, wup_g, cfw, cfb, wdown, ln2_g, ln2_b, target, tm):
    t = xhat1.shape[0]
    nt = t // tm

    def body(xh_ref, g1_ref, b1_ref, wup_ref, cfw_ref, cfb_ref, wdown_ref, g2_ref, b2_ref, tgt_ref,
             hu_ref, gv_ref, dr2_ref, loss_ref, sln2_ref,
             x1_ref, x1b_ref, hu32_ref, carry_ref, gbuf_ref, ffn_ref, acc_loss, acc_g2, acc_b2):
        i = pl.program_id(0)

        @pl.when(i == 0)
        def _():
            carry_ref[...] = jnp.zeros(carry_ref.shape, F32)
            acc_loss[...] = jnp.zeros(acc_loss.shape, F32)
            acc_g2[...] = jnp.zeros(acc_g2.shape, F32)
            acc_b2[...] = jnp.zeros(acc_b2.shape, F32)

        x1 = xh_ref[...] * g1_ref[...] + b1_ref[...]
        x1_ref[...] = x1
        x1b_ref[...] = x1.astype(BF16)

        def conv(g, j, base):
            if base == 0:
                win = jnp.concatenate([carry_ref[j], hu32_ref[g, 0:ROWS, :]], axis=0)
            else:
                win = hu32_ref[g, base - HALO_F:base + ROWS, :]
            taps = _taps_f(win)
            w = cfw_ref[j]
            return sum(taps[k] * w[k:k + 1, :] for k in range(KF)) + cfb_ref[j:j + 1, :]

        for f in range(N_F):
            hu32_ref[0] = _nn(x1b_ref[...], wup_ref[f])
            hu32_ref[1] = _nn(x1b_ref[...], wup_ref[N_F + f])

            def rows(bi, f=f):
                r = _rows(bi)
                gate = conv(0, f, bi * ROWS)
                val = conv(1, N_F + f, bi * ROWS)
                gbuf_ref[r, :] = (gate * _sigmoid(gate) * val).astype(BF16)
                gv_ref[f, r, :] = gate.astype(BF16)
                gv_ref[N_F + f, r, :] = val.astype(BF16)
                hu_ref[f, r, :] = hu32_ref[0, r, :].astype(BF16)
                hu_ref[N_F + f, r, :] = hu32_ref[1, r, :].astype(BF16)

            _loop(tm // ROWS, rows)
            carry_ref[f] = hu32_ref[0, tm - HALO_F:tm, :]
            carry_ref[N_F + f] = hu32_ref[1, tm - HALO_F:tm, :]
            part = _nn(gbuf_ref[...], wdown_ref[f])
            if f == 0:
                ffn_ref[...] = part
            else:
                ffn_ref[...] += part

        def tail(bi):
            r = _rows(bi, LN_ROWS)
            xhat2, rstd2 = _ln_stats(ALPHA * x1_ref[r, :] + ffn_ref[r, :])
            err = xhat2 * g2_ref[...] + b2_ref[...] - tgt_ref[r, :]
            e2 = _rsum8(err * err)
            acc_loss[...] += sum(e2[:, k * 128:(k + 1) * 128] for k in range(D // 128))
            dy = err * (1.0 / D)
            acc_g2[...] += _rsum8(dy * xhat2)
            acc_b2[...] += _rsum8(dy)
            dr2_ref[r, :] = _ln_bwd(dy * g2_ref[...], xhat2, rstd2)

        _loop(tm // LN_ROWS, tail)
        loss_ref[...] = acc_loss[...]

        @pl.when(i == nt - 1)
        def _():
            dg = jnp.sum(acc_g2[...], axis=0, keepdims=True)
            db = jnp.sum(acc_b2[...], axis=0, keepdims=True)
            for k in range(D // 128):
                sln2_ref[k:k + 1, :] = dg[:, k * 128:(k + 1) * 128]
                sln2_ref[8 + k:9 + k, :] = db[:, k * 128:(k + 1) * 128]

    row = pl.BlockSpec((tm, D), lambda i: (i, 0))
    return pl.pallas_call(
        body, name="ffn_forward", grid=(nt,),
        in_specs=[row, _full(ln1_g.shape), _full(ln1_b.shape), _resident(wup_g.shape),
                  _full(cfw.shape), _full(cfb.shape), _resident(wdown.shape),
                  _full(ln2_g.shape), _full(ln2_b.shape), row],
        out_specs=[pl.BlockSpec((N_DEV, tm, W_UP_BLK), lambda i: (0, i, 0)),
                   pl.BlockSpec((N_DEV, tm, W_UP_BLK), lambda i: (0, i, 0)), row,
                   _full((8, 128)), _full((16, 128))],
        out_shape=[jax.ShapeDtypeStruct((N_DEV, t, W_UP_BLK), BF16),
                   jax.ShapeDtypeStruct((N_DEV, t, W_UP_BLK), BF16), jax.ShapeDtypeStruct((t, D), F32),
                   jax.ShapeDtypeStruct((8, 128), F32), jax.ShapeDtypeStruct((16, 128), F32)],
        scratch_shapes=[pltpu.VMEM((tm, D), F32), pltpu.VMEM((tm, D), BF16),
                        pltpu.VMEM((2, tm, W_UP_BLK), F32),
                        pltpu.VMEM((N_DEV, HALO_F, W_UP_BLK), F32), pltpu.VMEM((tm, W_UP_BLK), BF16),
                        pltpu.VMEM((tm, D), F32), pltpu.VMEM((8, 128), F32),
                        pltpu.VMEM((8, D), F32), pltpu.VMEM((8, D), F32)],
        compiler_params=_params(("arbitrary",)),
    )(xhat1, ln1_g, ln1_b, wup_g, cfw, cfb, wdown, ln2_g, ln2_b, target)


def ffn_backward(order, dr2, xhat1, ln1_g, ln1_b, hu, gv, wup_g, cfw, wdown, tm):
    t = dr2.shape[0]
    nt = t // tm
    sub_rows = tm
    hu4 = hu.reshape(2, N_F, t, W_UP_BLK)
    gv4 = gv.reshape(2, N_F, t, W_UP_BLK)
    wup4 = wup_g.reshape(2, N_F, D, W_UP_BLK)
    cfw4 = cfw.reshape(2, N_F, KF, W_UP_BLK)

    def body(order_ref, dr2_ref, xh_ref, g1_ref, b1_ref, hu_ref, gv_ref, wup_ref, cfw_ref, wdown_ref,
             dwup_ref, dwdown_ref, dcfw_ref, dcfb_ref, dx1_ref, land_up_ref, land_down_ref,
             x1b_ref, drb_ref, dg_ref, dextg_ref, dextv_ref, gbuf_ref,
             dhug_ref, dhuv_ref, acc_wup, acc_wdown, acc_cfw, acc_cfb, sem, send_sems, recv_sems):
        f = order_ref[pl.program_id(0)]
        i = pl.program_id(1)
        x, y, c = _mesh_pos()
        half = D_FF // N_DEV

        def to_sibling(fi, k, src, land_ref, shard_chip):
            d = jnp.bitwise_xor(shard_chip, 2 * x + y)
            slot = jnp.where(d == 1, 2, jnp.where(d == 2, 1, d))
            return pltpu.make_async_remote_copy(
                src_ref=src, dst_ref=land_ref.at[slot], send_sem=send_sems.at[fi, k], recv_sem=recv_sems.at[fi, k],
                device_id=(x, y, 1 - c), device_id_type=MESH)

        def up_copy(fi, g):
            return to_sibling(fi, g, dwup_ref.at[g, fi], land_up_ref, 2 * g + fi // 2)

        def down_copy(fi):
            return to_sibling(fi, 2, dwdown_ref.at[fi, pl.ds((1 - c) * half, half)], land_down_ref, fi)

        @pl.when(i == 0)
        def _():
            acc_wup[...] = jnp.zeros(acc_wup.shape, F32)
            acc_wdown[...] = jnp.zeros(acc_wdown.shape, F32)
            acc_cfw[...] = jnp.zeros(acc_cfw.shape, F32)
            acc_cfb[...] = jnp.zeros(acc_cfb.shape, F32)
            dextg_ref[tm:tm + HALO_F, :] = jnp.zeros((HALO_F, W_UP_BLK), F32)
            dextv_ref[tm:tm + HALO_F, :] = jnp.zeros((HALO_F, W_UP_BLK), F32)

        w = [cfw_ref[0, 0], cfw_ref[1, 0]]
        dext = [dextg_ref, dextv_ref]
        dhu = [dhug_ref, dhuv_ref]

        def rows1(bi):
            r = _rows(bi)
            gate = gv_ref[0, 0, r, :].astype(F32)
            val = gv_ref[1, 0, r, :].astype(F32)
            sg = _sigmoid(gate)
            silu = gate * sg
            gbuf_ref[r, :] = (silu * val).astype(BF16)
            dg = dg_ref[r, :]
            dgate = dg * val * (sg * (1.0 + gate * (1.0 - sg)))
            dval = dg * silu
            dextg_ref[r, :] = dgate
            dextv_ref[r, :] = dval
            acc_cfb[0:8, :] += _rsum8(dgate)
            acc_cfb[8:16, :] += _rsum8(dval)

        def rows2(bi):
            r = _rows(bi)
            for g in range(2):
                win = dext[g][pl.ds(bi * ROWS, ROWS + HALO_F), :]
                n = ROWS + HALO_F
                later = [pltpu.roll(win, n - 2, 0)[0:ROWS, :], pltpu.roll(win, n - 1, 0)[0:ROWS, :],
                         win[0:ROWS, :]]
                d = sum(later[k] * w[g][k:k + 1, :] for k in range(KF))
                dhu[g][r, :] = d.astype(BF16)
                pre = hu_ref[g, 0, r, :].astype(F32)
                for k in range(KF):
                    r0 = 8 * (g * KF + k)
                    acc_cfw[r0:r0 + 8, :] += _rsum8(later[k] * pre)

        for sub in reversed(range(tm // sub_rows)):
            rs = slice(sub * sub_rows, (sub + 1) * sub_rows)
            blocks = range(sub * sub_rows // ROWS, (sub + 1) * sub_rows // ROWS)
            x1b_ref[rs, :] = (xh_ref[rs, :] * g1_ref[...] + b1_ref[...]).astype(BF16)
            drb_ref[rs, :] = dr2_ref[rs, :].astype(BF16)
            dg_ref[rs, :] = _nt(drb_ref[rs, :], wdown_ref[0])
            for bi in blocks:
                rows1(bi)
            for bi in blocks:
                rows2(bi)
            acc_wdown[...] += _tn(gbuf_ref[rs, :], drb_ref[rs, :])
            acc_wup[0] += _tn(dhug_ref[rs, :], x1b_ref[rs, :])
            acc_wup[1] += _tn(dhuv_ref[rs, :], x1b_ref[rs, :])
            dx1_ref[0, rs, :] = (_nt(dhug_ref[rs, :], wup_ref[0, 0])
                                 + _nt(dhuv_ref[rs, :], wup_ref[1, 0])).astype(BF16)
        dextg_ref[tm:tm + HALO_F, :] = dextg_ref[0:HALO_F, :]
        dextv_ref[tm:tm + HALO_F, :] = dextv_ref[0:HALO_F, :]

        @pl.when(i == nt - 1)
        def _():
            for g in range(2):
                dcfb_ref[g, 0] = jnp.sum(acc_cfb[8 * g:8 * g + 8, :], axis=0, keepdims=True)
                for k in range(KF):
                    r0 = 8 * (g * KF + k)
                    dcfw_ref[g, 0, k:k + 1, :] = jnp.sum(acc_cfw[r0:r0 + 8, :], axis=0, keepdims=True)
            cps = [pltpu.make_async_copy(acc_wup.at[0], dwup_ref.at[0, f], sem.at[0]),
                   pltpu.make_async_copy(acc_wup.at[1], dwup_ref.at[1, f], sem.at[1]),
                   pltpu.make_async_copy(acc_wdown, dwdown_ref.at[f], sem.at[2])]
            for cp in cps:
                cp.start()
            for cp in cps:
                cp.wait()
            down_copy(f).start()

            @pl.when(f % 2 != c)
            def _():
                up_copy(f, 0).start()
                up_copy(f, 1).start()

        @pl.when((i == nt - 1) & (pl.program_id(0) == N_F - 1))
        def _():
            for fi in range(N_F):
                down_copy(fi).wait()
                for g in range(2):
                    @pl.when(fi % 2 != c)
                    def _():
                        up_copy(fi, g).wait_send()

                    @pl.when(fi % 2 == c)
                    def _():
                        up_copy(fi, g).wait_recv()

    rev = lambda i: nt - 1 - i
    row = pl.BlockSpec((tm, D), lambda fo, i, o: (rev(i), 0))
    pair = lambda r, c: pl.BlockSpec((2, 1, r, c), lambda fo, i, o: (0, o[fo], 0, 0))
    tile = pl.BlockSpec((2, 1, tm, W_UP_BLK), lambda fo, i, o: (0, o[fo], rev(i), 0))
    return pl.pallas_call(
        body, name="ffn_backward",
        grid_spec=pltpu.PrefetchScalarGridSpec(
            num_scalar_prefetch=1, grid=(N_F, nt),
            in_specs=[row, row, _full(ln1_g.shape), _full(ln1_b.shape), tile, tile,
                      pair(D, W_UP_BLK), pair(KF, W_UP_BLK),
                      pl.BlockSpec((1, W_UP_BLK, D), lambda fo, i, o: (o[fo], 0, 0))],
            out_specs=[ANY, ANY, pair(KF, W_UP_BLK), pair(1, W_UP_BLK),
                       pl.BlockSpec((1, tm, D), lambda fo, i, o: (o[fo], rev(i), 0)), ANY, ANY],
            scratch_shapes=[pltpu.VMEM((tm, D), BF16), pltpu.VMEM((tm, D), BF16),
                            pltpu.VMEM((tm, W_UP_BLK), F32),
                            pltpu.VMEM((tm + HALO_F, W_UP_BLK), F32), pltpu.VMEM((tm + HALO_F, W_UP_BLK), F32),
                            pltpu.VMEM((tm, W_UP_BLK), BF16), pltpu.VMEM((tm, W_UP_BLK), BF16),
                            pltpu.VMEM((tm, W_UP_BLK), BF16),
                            pltpu.VMEM((2, W_UP_BLK, D), F32), pltpu.VMEM((W_UP_BLK, D), F32),
                            pltpu.VMEM((2 * KF * 8, W_UP_BLK), F32), pltpu.VMEM((16, W_UP_BLK), F32),
                            pltpu.SemaphoreType.DMA((3,)),
                            pltpu.SemaphoreType.DMA((N_F, 3)), pltpu.SemaphoreType.DMA((N_F, 3))]),
        out_shape=[jax.ShapeDtypeStruct((2, N_F, W_UP_BLK, D), F32),
                   jax.ShapeDtypeStruct((N_F, W_UP_BLK, D), F32),
                   jax.ShapeDtypeStruct((2, N_F, KF, W_UP_BLK), F32),
                   jax.ShapeDtypeStruct((2, N_F, 1, W_UP_BLK), F32),
                   jax.ShapeDtypeStruct((N_F, t, D), BF16),
                   jax.ShapeDtypeStruct((4, W_UP_BLK, D), F32),
                   jax.ShapeDtypeStruct((4, D_FF // N_DEV, D), F32)],
        compiler_params=_params(("arbitrary", "arbitrary")),
    )(order, dr2, xhat1, ln1_g, ln1_b, hu4, gv4, wup4, cfw4, wdown)


def mix_backward(x, h, yb1, dx1p, dr2, xhat1, rstd1, win_g, ln_a_g, ln_a_b, w_spatial, bst,
                 conv_b_w, ln_b_g, ln_b_b, wout, ln1_g, ffn_partials, tm):
    t = x.shape[0]
    n_p = len(ffn_partials)
    nt = t // tm
    n_chunks = tm // CHUNK
    halo_blocks = tm // HALO_B

    def body(x_ref, h_ref, halo_ref, yb1_ref, dx1p_ref, dr2_ref, xh1_ref, rstd1_ref, win_ref, ga_ref, ba_ref,
             ws_ref, bst_ref, cw_ref, gb_ref, bb_ref, wout_ref, g1_ref, *rest):
        p_refs, rest = rest[:n_p], rest[n_p:]
        gx_ref, dwin_ref, dwout_ref, dcw_ref, small_ref = rest[:5]
        land_refs, rest = rest[5:5 + n_p], rest[5 + n_p:]
        (ext_ref, dext_ref, y_ref, dy_ref, dh_ref, dmb_ref, wsm_ref,
         acc_win, acc_wout, acc_bin, acc_lnag, acc_lnab, acc_ws, acc_bs, acc_cbb, acc_lnbg,
         acc_lnbb, acc_bout, acc_ln1g, acc_ln1b, acc_cw, sem, send_sems, recv_sems) = rest
        i = pl.program_id(0)

        @pl.when(i == 0)
        def _():
            for cp in _chip_copies(p_refs, land_refs, send_sems, recv_sems):
                cp.start()

        first_tile = i == nt - 1
        accs = [acc_win, acc_wout, acc_bin, acc_lnag, acc_lnab, acc_ws, acc_bs, acc_cbb, acc_lnbg,
                acc_lnbb, acc_bout, acc_ln1g, acc_ln1b, acc_cw]

        @pl.when(i == 0)
        def _():
            for acc in accs:
                acc[...] = jnp.zeros(acc.shape, F32)
            dext_ref[tm:tm + HALO_B, :] = jnp.zeros((HALO_B, D_B), F32)
            mask = _tril_mask()
            for hd in range(HEADS):
                wsm_ref[hd] = jnp.where(mask, ws_ref[hd], 0.0).astype(BF16)

        def ln1_rows(bi):
            r = _rows(bi, LN_ROWS)
            part = [dx1p_ref[f, r, :].astype(F32) for f in range(N_F)]
            dx1 = ALPHA * dr2_ref[r, :] + ((part[0] + part[1]) + (part[2] + part[3]))
            xhat = xh1_ref[r, :]
            acc_ln1g[...] += _rsum8(dx1 * xhat)
            acc_ln1b[...] += _rsum8(dx1)
            dr1 = _ln_bwd(dx1 * g1_ref[...], xhat, rstd1_ref[r, 0:1])
            acc_bout[...] += _rsum8(dr1)
            gx_ref[r, :] = ALPHA * dr1
            dmb_ref[r, :] = dr1.astype(BF16)

        _loop(tm // LN_ROWS, ln1_rows)
        dy_ref[...] = _nt(dmb_ref[...], wout_ref[...])

        ha = halo_ref[:, 0:D_B]
        hg = halo_ref[:, D_B:2 * D_B]
        ext_ref[0:HALO_B, :] = jnp.where(first_tile, 0.0, 1.0) * (ha * _sigmoid(hg))

        def chunk(ci):
            r = _rows(ci, CHUNK)
            for hd in range(HEADS):
                sl = slice(hd * HEAD_DIM, (hd + 1) * HEAD_DIM)
                rows8 = slice(8 * hd, 8 * hd + 8)
                hus, hvs, u, cdf_u, cdf_v, xhat, rstd, vn, sv = _mixer_a_head(
                    h_ref, r, hd, ga_ref, ba_ref, wsm_ref, bst_ref)
                dy_a = dy_ref[r, sl]
                y_ref[r, sl] = (u * sv).astype(BF16)
                du = dy_a * sv
                dsv = dy_a * u
                dsvb = dsv.astype(BF16)
                acc_bs[hd] += dsv
                acc_ws[hd] += _nt(dsvb, vn)
                dvn = _tn(wsm_ref[hd], dsvb)
                acc_lnag[rows8, :] += _rsum8(dvn * xhat)
                acc_lnab[rows8, :] += _rsum8(dvn)
                dv = _ln_bwd(dvn * ga_ref[hd:hd + 1, :], xhat, rstd)
                slv = slice(D_A + hd * HEAD_DIM, D_A + (hd + 1) * HEAD_DIM)
                dhu = du * (cdf_u + hus * jnp.exp(-0.5 * hus * hus) * INV_SQRT_2PI)
                dhv = dv * (cdf_v + hvs * jnp.exp(-0.5 * hvs * hvs) * INV_SQRT_2PI)
                acc_bin[:, sl] += _rsum8(dhu)
                acc_bin[:, slv] += _rsum8(dhv)
                dh_ref[r, sl] = dhu.astype(BF16)
                dh_ref[r, slv] = dhv.astype(BF16)
            a_b = h_ref[r, 2 * D_A:2 * D_A + D_B]
            g_b = h_ref[r, 2 * D_A + D_B:D_IN]
            ext_ref[pl.ds(HALO_B + ci * CHUNK, CHUNK), :] = a_b * _sigmoid(g_b)

        _loop(n_chunks, chunk)

        def conv_rows(bi):
            base = bi * ROWS
            r = pl.ds(base, ROWS)
            xhat, rstd = _ln_stats(yb1_ref[r, :])
            yb2 = xhat * gb_ref[...] + bb_ref[...]
            sg = _sigmoid(yb2)
            y_ref[r, D_A:D] = (yb2 * sg).astype(BF16)
            dyb2 = dy_ref[r, D_A:D] * (sg * (1.0 + yb2 * (1.0 - sg)))
            acc_lnbg[...] += _rsum8(dyb2 * xhat)
            acc_lnbb[...] += _rsum8(dyb2)
            dyb1 = _ln_bwd(dyb2 * gb_ref[...], xhat, rstd)
            acc_cbb[...] += _rsum8(dyb1)
            dext_ref[r, :] = dyb1
            for k, tap in _taps(ext_ref[pl.ds(base, ROWS + HALO_B), :], CONV_B_OFFSETS):
                acc_cw[8 * k:8 * k + 8, :] += _rsum8(dyb1 * tap)

        _loop(tm // ROWS, conv_rows)

        def convt_rows(bi):
            base = bi * ROWS
            r = pl.ds(base, ROWS)
            dyb0 = jnp.zeros((ROWS, D_B), F32)
            for k, tap in _taps(dext_ref[pl.ds(base, ROWS + HALO_B), :], CONV_B_T_OFFSETS):
                dyb0 = dyb0 + tap * cw_ref[k:k + 1, :]
            a_b = h_ref[r, 2 * D_A:2 * D_A + D_B]
            sg = _sigmoid(h_ref[r, 2 * D_A + D_B:D_IN])
            da_b = dyb0 * sg
            dg_b = dyb0 * a_b * sg * (1.0 - sg)
            acc_bin[:, 2 * D_A:2 * D_A + D_B] += _rsum8(da_b)
            acc_bin[:, 2 * D_A + D_B:D_IN] += _rsum8(dg_b)
            dh_ref[r, 2 * D_A:2 * D_A + D_B] = da_b.astype(BF16)
            dh_ref[r, 2 * D_A + D_B:D_IN] = dg_b.astype(BF16)

        _loop(tm // ROWS, convt_rows)
        dext_ref[tm:tm + HALO_B, :] = dext_ref[0:HALO_B, :]

        acc_wout[...] += _tn(y_ref[...], dmb_ref[...])
        xt = x_ref[...].T.astype(BF16)
        dh_blocks = [dh_ref[:, j * W_IN_BLK:(j + 1) * W_IN_BLK] for j in range(N_DEV)]
        for j in range(N_DEV):
            acc_win[j] += _nn(xt, dh_blocks[j])
        gx_ref[...] += sum(_nt(dh_blocks[j], win_ref[j]) for j in range(N_DEV))

        @pl.when(i == nt - 1)
        def _():
            cps = [pltpu.make_async_copy(acc_win, dwin_ref, sem.at[0]),
                   pltpu.make_async_copy(acc_wout, dwout_ref, sem.at[1])]
            for cp in cps:
                cp.start()
            small_ref[...] = jnp.zeros(small_ref.shape, F32)

            def put_row_vector(row0, acc):
                vec = jnp.sum(acc[...], axis=0, keepdims=True)
                for k in range(vec.shape[1] // 128):
                    small_ref[row0 + k:row0 + k + 1, :] = vec[:, k * 128:(k + 1) * 128]

            put_row_vector(S_BIN, acc_bin)
            put_row_vector(S_CBB, acc_cbb)
            put_row_vector(S_LNBG, acc_lnbg)
            put_row_vector(S_LNBB, acc_lnbb)
            put_row_vector(S_BOUT, acc_bout)
            put_row_vector(S_LN1G, acc_ln1g)
            put_row_vector(S_LN1B, acc_ln1b)
            mask = _tril_mask()
            for hd in range(HEADS):
                rows8 = slice(8 * hd, 8 * hd + 8)
                small_ref[S_LNAG + hd:S_LNAG + hd + 1, :] = jnp.sum(acc_lnag[rows8, :], axis=0, keepdims=True)
                small_ref[S_LNAB + hd:S_LNAB + hd + 1, :] = jnp.sum(acc_lnab[rows8, :], axis=0, keepdims=True)
                small_ref[S_WS + hd * CHUNK:S_WS + (hd + 1) * CHUNK, :] = jnp.where(mask, acc_ws[hd], 0.0)
                small_ref[S_BS + hd:S_BS + hd + 1, :] = jnp.sum(acc_bs[hd].T, axis=0, keepdims=True)
            for k in range(KB):
                dcw_ref[k:k + 1, :] = jnp.sum(acc_cw[8 * k:8 * k + 8, :], axis=0, keepdims=True)
            for cp in cps:
                cp.wait()
            for cp in _chip_copies(p_refs, land_refs, send_sems, recv_sems):
                cp.wait()

    rev = lambda i: nt - 1 - i
    row = lambda w: pl.BlockSpec((tm, w), lambda i: (rev(i), 0))
    return pl.pallas_call(
        body, name="mix_backward", grid=(nt,),
        in_specs=[row(D), row(D_IN),
                  pl.BlockSpec((HALO_B, 2 * D_B), lambda i: (jnp.maximum(rev(i) * halo_blocks - 1, 0), 1)),
                  row(D_B), pl.BlockSpec((N_F, tm, D), lambda i: (0, rev(i), 0)),
                  row(D), row(D), row(128), _resident(win_g.shape), _full(ln_a_g.shape),
                  _full(ln_a_b.shape), _full(w_spatial.shape), _full(bst.shape), _full(conv_b_w.shape),
                  _full(ln_b_g.shape), _full(ln_b_b.shape),
                  _resident(wout.shape), _full(ln1_g.shape)] + [ANY] * n_p,
        out_specs=[row(D), ANY, ANY, _full((KB, D_B)), _full((S_MIX_ROWS, 128))] + [ANY] * n_p,
        out_shape=[jax.ShapeDtypeStruct((t, D), F32), jax.ShapeDtypeStruct((N_DEV, D, W_IN_BLK), F32),
                   jax.ShapeDtypeStruct((D, D), F32), jax.ShapeDtypeStruct((KB, D_B), F32),
                   jax.ShapeDtypeStruct((S_MIX_ROWS, 128), F32)]
        + [jax.ShapeDtypeStruct(p.shape, BF16) for p in ffn_partials],
        scratch_shapes=[pltpu.VMEM((tm + HALO_B, D_B), F32), pltpu.VMEM((tm + HALO_B, D_B), F32),
                        pltpu.VMEM((tm, D), BF16), pltpu.VMEM((tm, D), F32), pltpu.VMEM((tm, D_IN), BF16),
                        pltpu.VMEM((tm, D), BF16),
                        pltpu.VMEM((HEADS, CHUNK, CHUNK), BF16),
                        pltpu.VMEM((N_DEV, D, W_IN_BLK), F32), pltpu.VMEM((D, D), F32),
                        pltpu.VMEM((8, D_IN), F32), pltpu.VMEM((8 * HEADS, HEAD_DIM), F32),
                        pltpu.VMEM((8 * HEADS, HEAD_DIM), F32), pltpu.VMEM((HEADS, CHUNK, CHUNK), F32),
                        pltpu.VMEM((HEADS, CHUNK, CHUNK), F32), pltpu.VMEM((8, D_B), F32),
                        pltpu.VMEM((8, D_B), F32), pltpu.VMEM((8, D_B), F32), pltpu.VMEM((8, D), F32),
                        pltpu.VMEM((8, D), F32), pltpu.VMEM((8, D), F32), pltpu.VMEM((8 * KB, D_B), F32),
                        pltpu.SemaphoreType.DMA((2,)),
                        pltpu.SemaphoreType.DMA((n_p, 3)), pltpu.SemaphoreType.DMA((n_p, 3))],
        compiler_params=_params(("arbitrary",)),
    )(x, h, h, yb1, dx1p, dr2, xhat1, rstd1, win_g, ln_a_g, ln_a_b, w_spatial, bst, conv_b_w,
      ln_b_g, ln_b_b, wout, ln1_g, *ffn_partials)


def _rows128(a):
    return a.reshape(-1, 128)


def _pack_conv(cb, cf):
    lead = cb.shape[:-2]
    pad = [(0, 0)] * len(lead)
    flat = jnp.pad(cb.reshape(lead + (KB * 64,)), pad + [(0, 3 * W_UP_BLK - KB * 64)])
    rows = jnp.concatenate([cf, flat.reshape(lead + (3, W_UP_BLK))], axis=-2)
    return jnp.pad(rows, pad + [(0, 2), (0, 768 - W_UP_BLK)])


def _unpack_conv(p):
    lead = p.shape[:-2]
    cf = p[..., 0:KF, 0:W_UP_BLK]
    cb = p[..., 3:6, 0:W_UP_BLK].reshape(lead + (3 * W_UP_BLK,))[..., :KB * 64].reshape(lead + (KB, 64))
    return cb, cf


def kernel(x, w_in, b_in, ln_a_g, ln_a_b, w_spatial, b_spatial, conv_b_w, conv_b_b, ln_b_g, ln_b_b, w_out, b_out, ln1_g, ln1_b, w_up, conv_f_w, conv_f_b, w_down, ln2_g, ln2_b, loss_target, m_w_in, m_b_in, m_ln_a_g, m_ln_a_b, m_w_spatial, m_b_spatial, m_conv_b_w, m_conv_b_b, m_ln_b_g, m_ln_b_b, m_w_out, m_b_out, m_ln1_g, m_ln1_b, m_w_up, m_conv_f_w, m_conv_f_b, m_w_down, m_ln2_g, m_ln2_b, v_w_in, v_b_in, v_ln_a_g, v_ln_a_b, v_w_spatial, v_b_spatial, v_conv_b_w, v_conv_b_b, v_ln_b_g, v_ln_b_b, v_w_out, v_b_out, v_ln1_g, v_ln1_b, v_w_up, v_conv_f_w, v_conv_f_b, v_w_down, v_ln2_g, v_ln2_b):
    t = x.shape[1]
    x2 = x.reshape(t, D)
    target = loss_target.reshape(t, D)
    tm_fwd = min(t, 512)
    tm_bwd = min(t, 256)
    tm_ffn_bwd = min(t, 512)

    xi, yi, ci = _mesh_pos()
    jidx = jnp.stack([_lid(px, py, ci) for px, py in _chip_patterns(xi, yi)]).astype(jnp.int32)

    sin, sout, sup, sdown, conv_g = prepare_weights(w_in, w_out, w_up.T, w_down, _pack_conv(conv_b_w, conv_f_w))
    conv_b_all, cfw = _unpack_conv(conv_g)
    conv_b_full = conv_b_all.transpose(1, 0, 2).reshape(KB, D_B)
    cfb = conv_f_b.reshape(N_DEV, W_UP_BLK)
    row = lambda a: a.reshape(1, -1)
    bst = b_spatial.T

    h, xhat1, rstd1, yb1, win_g, wout_g, wup_g, wdown_g = mix_forward(
        x2, sin, sout, row(b_in), ln_a_g, ln_a_b, w_spatial, bst, conv_b_full, row(conv_b_b),
        row(ln_b_g), row(ln_b_b), row(b_out), row(ln1_g), row(ln1_b), sup, sdown, tm_fwd)
    wout_full = wout_g.reshape(D, D)
    wdown4 = wdown_g.reshape(N_F, W_UP_BLK, D)
    hu, gv, dr2, loss_part, s_ln2 = ffn_forward(
        xhat1, row(ln1_g), row(ln1_b), wup_g, cfw, cfb, wdown4, row(ln2_g), row(ln2_b), target, tm_bwd)

    order = jnp.where(ci == 0, jnp.array([1, 3, 0, 2], jnp.int32), jnp.array([0, 2, 1, 3], jnp.int32))
    dwup, dwdown, dcfw, dcfb, dx1p, *ffn_lands = ffn_backward(
        order, dr2, xhat1, row(ln1_g), row(ln1_b), hu, gv, wup_g, cfw, wdown4, tm_ffn_bwd)
    ffn_grads = [dwup.reshape(N_DEV, W_UP_BLK, D), dwdown.reshape(N_DEV, D_FF // N_DEV, D)]
    ffn_partials = [chip_partials("chip_partials_" + nm, g, l, jidx, rb)
                    for nm, g, l, rb in zip(["w_up", "w_down"], ffn_grads, ffn_lands, [352, 352])]
    grad_x, dwin, dwout, dcw, s_mix, *ffn_recvs = mix_backward(
        x2, h, yb1, dx1p, dr2, xhat1, rstd1, win_g, ln_a_g, ln_a_b, w_spatial, bst,
        conv_b_full, row(ln_b_g), row(ln_b_b), wout_full, row(ln1_g), ffn_partials, tm_bwd)

    dcfb_rows = jnp.pad(dcfb.reshape(-1, 128), ((0, 4), (0, 0)))
    svec = jnp.concatenate([s_mix, dcfb_rows, s_ln2, loss_part], axis=0)
    dconv = _pack_conv(dcw.reshape(KB, N_DEV, 64).transpose(1, 0, 2), dcfw.reshape(N_DEV, KF, W_UP_BLK))
    mix_grads = [dwin, dwout.reshape(N_DEV, D // N_DEV, D), dconv]
    mix_w = [w_in, w_out, _pack_conv(conv_b_w, conv_f_w)]
    mix_m = [m_w_in, m_w_out, _pack_conv(m_conv_b_w, m_conv_f_w)]
    mix_v = [v_w_in, v_w_out, _pack_conv(v_conv_b_w, v_conv_f_w)]
    *mix_out, sv_slots = mixer_reduce_adamw(mix_grads, svec, mix_w, mix_m, mix_v)
    big = {nm: [mix_out[k * 3 + p] for k in range(4)] for p, nm in enumerate(["w_in", "w_out", "conv"])}

    ffn_w = [(w_up.T, m_w_up.T, v_w_up.T), (w_down, m_w_down, v_w_down)]
    for nm, g, l, r, (w, m, v) in zip(["w_up", "w_down"], ffn_grads, ffn_lands, ffn_recvs, ffn_w):
        big[nm] = reduce_and_adamw("reduce_adamw_" + nm, g, l, r, w, m, v, jidx, 352)
    big["w_up"] = [o.T for o in big["w_up"]]
    for k in range(4):
        cb_k, cf_k = _unpack_conv(big["conv"][k])
        big.setdefault("conv_b_w", []).append(cb_k)
        big.setdefault("conv_f_w", []).append(cf_k)

    small_w = dict(b_in=b_in, ln_a_g=ln_a_g, ln_a_b=ln_a_b, w_spatial=w_spatial, b_spatial=b_spatial,
                   conv_b_b=conv_b_b, ln_b_g=ln_b_g, ln_b_b=ln_b_b, b_out=b_out, ln1_g=ln1_g,
                   ln1_b=ln1_b, conv_f_b=conv_f_b, ln2_g=ln2_g, ln2_b=ln2_b)
    small_m = dict(b_in=m_b_in, ln_a_g=m_ln_a_g, ln_a_b=m_ln_a_b, w_spatial=m_w_spatial,
                   b_spatial=m_b_spatial, conv_b_b=m_conv_b_b, ln_b_g=m_ln_b_g, ln_b_b=m_ln_b_b,
                   b_out=m_b_out, ln1_g=m_ln1_g, ln1_b=m_ln1_b, conv_f_b=m_conv_f_b, ln2_g=m_ln2_g,
                   ln2_b=m_ln2_b)
    small_v = dict(b_in=v_b_in, ln_a_g=v_ln_a_g, ln_a_b=v_ln_a_b, w_spatial=v_w_spatial,
                   b_spatial=v_b_spatial, conv_b_b=v_conv_b_b, ln_b_g=v_ln_b_g, ln_b_b=v_ln_b_b,
                   b_out=v_b_out, ln1_g=v_ln1_g, ln1_b=v_ln1_b, conv_f_b=v_conv_f_b, ln2_g=v_ln2_g,
                   ln2_b=v_ln2_b)
    order = [nm for nm, _, _ in SMALL_LAYOUT]
    small_out = small_adamw(sv_slots, [_rows128(small_w[nm]) for nm in order],
                            [_rows128(small_m[nm]) for nm in order], [_rows128(small_v[nm]) for nm in order])
    n_small = len(order)
    small = {nm: [small_out[k * n_small + p].reshape(small_w[nm].shape) for k in range(4)]
             for p, nm in enumerate(order)}
    loss = jnp.sum(small_out[4 * n_small]) * (0.5 / D)

    weights = ["w_in", "b_in", "ln_a_g", "ln_a_b", "w_spatial", "b_spatial", "conv_b_w", "conv_b_b",
               "ln_b_g", "ln_b_b", "w_out", "b_out", "ln1_g", "ln1_b", "w_up", "conv_f_w", "conv_f_b",
               "w_down", "ln2_g", "ln2_b"]
    result = lambda nm, k: big[nm][k] if nm in big else small[nm][k]
    return (loss, grad_x.reshape(x.shape), *[result(nm, 0) for nm in weights],
            *[result(nm, 1) for nm in weights], *[result(nm, 2) for nm in weights],
            *[result(nm, 3) for nm in weights])
```

```python
import functools
import math

import jax
import jax.numpy as jnp
from jax import lax
from jax.experimental import pallas as pl
from jax.experimental.pallas import tpu as pltpu

F32 = jnp.float32
BF16 = jnp.bfloat16

D = 1024
D_A = 512
D_B = 512
HEADS = 4
HEAD_DIM = 128
CHUNK = 128
KB = 31
KF = 3
D_FF = 2816
D_IN = 2048
N_DEV = 8
W_IN_BLK = D_IN // N_DEV
W_UP_BLK = 2 * D_FF // N_DEV
N_F = 4
LN_EPS = 1e-5
ALPHA = 2.0 ** 0.25

ADAM_LR = 0.001
ADAM_B1 = 0.9
ADAM_B2 = 0.999
ADAM_EPS = 1e-08
ADAM_WD = 0.01
ADAM_STEP = 10

INV_SQRT2 = 1.0 / math.sqrt(2.0)
INV_SQRT_2PI = 1.0 / math.sqrt(2.0 * math.pi)

HALO_B = 32
HALO_F = 8
ROWS = 64
LN_ROWS = 32
VMEM_LIMIT = 58 * 1024 * 1024

MESH = pl.DeviceIdType.MESH
ANY = pl.BlockSpec(memory_space=pl.ANY)
VMEM = pl.BlockSpec(memory_space=pltpu.VMEM)

S_BIN, S_LNAG, S_LNAB, S_WS, S_BS, S_CBB, S_LNBG, S_LNBB, S_BOUT, S_LN1G, S_LN1B = (
    0, 16, 24, 32, 544, 552, 560, 568, 576, 584, 592)
S_MIX_ROWS = 600
S_CFB = 600
S_LN2G = 648
S_LN2B = 656
S_LOSS = 664
S_ROWS = 672


def _tn(a, b):
    return lax.dot_general(a, b, (((0,), (0,)), ((), ())), preferred_element_type=F32)


def _nt(a, b):
    return lax.dot_general(a, b, (((1,), (1,)), ((), ())), preferred_element_type=F32)


def _nn(a, b):
    return jnp.dot(a, b, preferred_element_type=F32)


def _sigmoid(x):
    return 1.0 / (1.0 + jnp.exp(-x))


def _ln_stats(x):
    mu = jnp.mean(x, axis=-1, keepdims=True)
    xc = x - mu
    var = jnp.mean(xc * xc, axis=-1, keepdims=True)
    rstd = lax.rsqrt(var + LN_EPS)
    return xc * rstd, rstd


def _ln_bwd(dxhat, xhat, rstd):
    m1 = jnp.mean(dxhat, axis=-1, keepdims=True)
    m2 = jnp.mean(dxhat * xhat, axis=-1, keepdims=True)
    return rstd * (dxhat - m1 - xhat * m2)


def _rsum8(x):
    r, n = x.shape
    return x.reshape(r // 8, 8, n).sum(axis=0)


def _rows(i, n=ROWS):
    return pl.ds(i * n, n)


def _loop(n, body):
    for i in range(n):
        body(i)


def _tril_mask():
    r = lax.broadcasted_iota(jnp.int32, (CHUNK, CHUNK), 0)
    c = lax.broadcasted_iota(jnp.int32, (CHUNK, CHUNK), 1)
    return c <= r


def _mixer_a_head(h_ref, r, hd, ga_ref, ba_ref, wsm_ref, bst_ref):
    sl = slice(hd * HEAD_DIM, (hd + 1) * HEAD_DIM)
    hu = h_ref[r, sl]
    hv = h_ref[r, D_A + hd * HEAD_DIM:D_A + (hd + 1) * HEAD_DIM]
    cdf_u = 0.5 * (1.0 + lax.erf(hu * INV_SQRT2))
    cdf_v = 0.5 * (1.0 + lax.erf(hv * INV_SQRT2))
    u = hu * cdf_u
    xhat, rstd = _ln_stats(hv * cdf_v)
    vn = (xhat * ga_ref[hd:hd + 1, :] + ba_ref[hd:hd + 1, :]).astype(BF16)
    sv = _nn(wsm_ref[hd], vn) + bst_ref[:, hd:hd + 1]
    return hu, hv, u, cdf_u, cdf_v, xhat, rstd, vn, sv


def _taps(win, offsets):
    n = win.shape[0]
    for s in range(8):
        ks = [k for k, o in enumerate(offsets) if o % 8 == s]
        if ks:
            moved = win if s == 0 else pltpu.roll(win, n - s, 0)
            for k in ks:
                yield k, moved[offsets[k] - s:offsets[k] - s + ROWS, :]


CONV_B_OFFSETS = [2 + k for k in range(KB)]
CONV_B_T_OFFSETS = [30 - k for k in range(KB)]


def _conv_b_block(ext_ref, base, cw_ref):
    acc = jnp.zeros((ROWS, D_B), F32)
    for k, tap in _taps(ext_ref[pl.ds(base, ROWS + HALO_B), :], CONV_B_OFFSETS):
        acc = acc + tap * cw_ref[k:k + 1, :]
    return acc


def _taps_f(win):
    n = ROWS + HALO_F
    return [pltpu.roll(win, n - 6, 0)[0:ROWS, :], pltpu.roll(win, n - 7, 0)[0:ROWS, :], win[8:n, :]]


def _params(sem, **kw):
    return pltpu.CompilerParams(dimension_semantics=sem, vmem_limit_bytes=VMEM_LIMIT, **kw)


def _resident(shape):
    zeros = (0,) * len(shape)
    return pl.BlockSpec(shape, lambda *_: zeros, pipeline_mode=pl.Buffered(1))


def _full(shape):
    zeros = (0,) * len(shape)
    return pl.BlockSpec(shape, lambda *_: zeros)


def _mesh_pos():
    return lax.axis_index("x"), lax.axis_index("y"), lax.axis_index("c")


def _chip_patterns(x, y):
    return [(x, y), (1 - x, y), (x, 1 - y), (1 - x, 1 - y)]


def _lid(x, y, c):
    return 4 * x + 2 * y + c


def _gather_copy(outs, send_sems, recv_sems, a, k, block, to, src=None):
    blk = outs[a].at[_lid(*block)]
    return pltpu.make_async_remote_copy(
        src_ref=blk if src is None else src, dst_ref=blk,
        send_sem=send_sems.at[a, k], recv_sem=recv_sems.at[a, k], device_id=to, device_id_type=MESH)


def _gather_start(mine, outs, send_sems, recv_sems, local_sems):
    x, y, c = _mesh_pos()
    me = (x, y, c)
    for a in range(len(mine)):
        pltpu.make_async_copy(mine[a], outs[a].at[_lid(*me)], local_sems.at[a]).start()
        for k, to in enumerate([(x, y, 1 - c), (1 - x, y, c), (x, 1 - y, c)]):
            _gather_copy(outs, send_sems, recv_sems, a, k, me, to, src=mine[a]).start()


def _gather_relay(mine, outs, send_sems, recv_sems, local_sems, via):
    x, y, c = _mesh_pos()
    me, sib = (x, y, c), (x, y, 1 - c)
    copy = functools.partial(_gather_copy, outs, send_sems, recv_sems)
    source = {1: (1 - x, y, c), 2: (x, 1 - y, c)}
    for a in range(len(mine)):
        for k in (via[a], 3 - via[a]):
            copy(a, k, source[k], me).wait_recv()
            if k == via[a]:
                copy(a, 3, source[k], source[3 - k]).start()
            copy(a, 3 + k, source[k], sib).start()


def _gather_finish(mine, outs, send_sems, recv_sems, local_sems):
    x, y, c = _mesh_pos()
    me, sib = (x, y, c), (x, y, 1 - c)
    copy = functools.partial(_gather_copy, outs, send_sems, recv_sems)
    diag = (1 - x, 1 - y)
    n = len(mine)
    for a in range(n):
        copy(a, 3, (*diag, c), me).wait_recv()
        copy(a, 6, (*diag, c), sib).start()
    for a in range(n):
        copy(a, 0, sib, me).wait_recv()
        for k, chip in zip((4, 5, 6), [(1 - x, y), (x, 1 - y), diag]):
            copy(a, k, (*chip, 1 - c), me).wait_recv()
        for k in range(7):
            copy(a, k, me, sib, src=mine[a]).wait_send()
        pltpu.make_async_copy(mine[a], outs[a].at[_lid(*me)], local_sems.at[a]).wait()


def _gather_scratch(n):
    return [pltpu.SemaphoreType.DMA((n, 7)), pltpu.SemaphoreType.DMA((n, 7)), pltpu.SemaphoreType.DMA((n,))]


def prepare_weights(w_in, w_out, w_up_t, w_down, convp):
    def body(win_ref, wout_ref, wup_ref, wdown_ref, convp_ref,
             sin_ref, sout_ref, sup_ref, sdown_ref, gconv_ref, send_sems, recv_sems, local_sems):
        gather = ([convp_ref], [gconv_ref], send_sems, recv_sems, local_sems)
        _gather_start(*gather)
        sin_ref[...] = win_ref[...].astype(BF16)
        sout_ref[...] = wout_ref[...].astype(BF16)
        sup_ref[...] = wup_ref[...].T.astype(BF16)
        sdown_ref[...] = wdown_ref[...].astype(BF16)
        _gather_relay(*gather, via=[1])
        _gather_finish(*gather)

    return pl.pallas_call(
        body, name="prepare_weights",
        out_shape=[jax.ShapeDtypeStruct(w_in.shape, BF16), jax.ShapeDtypeStruct(w_out.shape, BF16),
                   jax.ShapeDtypeStruct(w_up_t.shape[::-1], BF16), jax.ShapeDtypeStruct(w_down.shape, BF16),
                   jax.ShapeDtypeStruct((N_DEV,) + convp.shape, F32)],
        in_specs=[VMEM] * 5, out_specs=[VMEM] * 4 + [ANY],
        scratch_shapes=_gather_scratch(1),
        compiler_params=pltpu.CompilerParams(vmem_limit_bytes=VMEM_LIMIT),
    )(w_in, w_out, w_up_t, w_down, convp)


def _chip_copies(p, land, send_sems, recv_sems):
    x, y, c = _mesh_pos()
    return [pltpu.make_async_remote_copy(
        src_ref=p[a].at[k], dst_ref=land[a].at[k], send_sem=send_sems.at[a, k], recv_sem=recv_sems.at[a, k],
        device_id=(px, py, c), device_id_type=MESH)
        for k, (px, py) in enumerate(_chip_patterns(x, y)[1:]) for a in range(len(p))]


def chip_partials(name, g, land, jidx, rb):
    _, r, c = g.shape

    def body(j_ref, g_ref, l_ref, o_ref):
        o_ref[...] = (g_ref[...] + l_ref[...]).astype(BF16)

    return pl.pallas_call(
        body, name=name,
        out_shape=jax.ShapeDtypeStruct((3, r, c), BF16),
        grid_spec=pltpu.PrefetchScalarGridSpec(
            num_scalar_prefetch=1, grid=(3, r // rb),
            in_specs=[pl.BlockSpec((1, rb, c), lambda k, i, j: (j[1 + k], i, 0)),
                      pl.BlockSpec((1, rb, c), lambda k, i, j: (1 + k, i, 0))],
            out_specs=pl.BlockSpec((1, rb, c), lambda k, i, j: (k, i, 0))),
        compiler_params=_params(("arbitrary", "arbitrary")),
    )(jidx, g, land)


def _adamw(w, g, m, v):
    m2 = ADAM_B1 * m + (1.0 - ADAM_B1) * g
    v2 = ADAM_B2 * v + (1.0 - ADAM_B2) * (g * g)
    m_hat = m2 / (1.0 - ADAM_B1 ** ADAM_STEP)
    v_hat = v2 / (1.0 - ADAM_B2 ** ADAM_STEP)
    delta = -ADAM_LR * (m_hat / (jnp.sqrt(v_hat) + ADAM_EPS) + ADAM_WD * w)
    return delta, m2, v2


def reduce_and_adamw(name, g, land, recv, w, m, v, jidx, rb):
    _, r, c = g.shape

    def body(j_ref, g_ref, l_ref, r_ref, w_ref, m_ref, v_ref, go_ref, do_ref, mo_ref, vo_ref):
        grad = (g_ref[0] + l_ref[0]) + r_ref[0].astype(F32) + r_ref[1].astype(F32) + r_ref[2].astype(F32)
        delta, m2, v2 = _adamw(w_ref[...], grad, m_ref[...], v_ref[...])
        go_ref[...] = grad
        do_ref[...] = delta
        mo_ref[...] = m2
        vo_ref[...] = v2

    blk = pl.BlockSpec((rb, c), lambda i, j: (i, 0))
    return pl.pallas_call(
        body, name=name,
        out_shape=[jax.ShapeDtypeStruct((r, c), F32)] * 4,
        grid_spec=pltpu.PrefetchScalarGridSpec(
            num_scalar_prefetch=1, grid=(r // rb,),
            in_specs=[pl.BlockSpec((1, rb, c), lambda i, j: (j[0], i, 0)),
                      pl.BlockSpec((1, rb, c), lambda i, j: (0, i, 0)),
                      pl.BlockSpec((3, rb, c), lambda i, j: (0, i, 0)),
                      blk, blk, blk],
            out_specs=[blk] * 4),
        compiler_params=_params(("arbitrary",)),
    )(jidx, g, land, recv, w, m, v)


def mixer_reduce_adamw(grads, svec, ws, ms, vs):
    n = len(grads)
    shard = [g.shape[1:] for g in grads]

    def body(*refs):
        g = refs[:n]
        sv_ref = refs[n]
        w, m, v = refs[n + 1:2 * n + 1], refs[2 * n + 1:3 * n + 1], refs[3 * n + 1:4 * n + 1]
        outs = refs[4 * n + 1:8 * n + 1]
        sv_slots = refs[8 * n + 1]
        rest = refs[8 * n + 2:]
        own, land, sendb, recvb = rest[:n], rest[n:2 * n], rest[2 * n:3 * n], rest[3 * n:4 * n]
        sv_land, chip_sv, d2d_send, d2d_recv, ici_send, ici_recv, local_sems, sv_sems = rest[4 * n:]
        x, y, c = _mesh_pos()
        sib = (x, y, 1 - c)
        pats = _chip_patterns(x, y)
        q = 2 * x + y

        d2d, local = [], []
        for a in range(n):
            for k, (px, py) in enumerate(pats):
                d2d.append(pltpu.make_async_remote_copy(
                    src_ref=g[a].at[_lid(px, py, 1 - c)], dst_ref=land[a].at[k],
                    send_sem=d2d_send.at[a, k], recv_sem=d2d_recv.at[a, k], device_id=sib, device_id_type=MESH))
                local.append(pltpu.make_async_copy(g[a].at[_lid(px, py, c)], own[a].at[k], local_sems.at[a, k]))
        d2d.append(pltpu.make_async_remote_copy(
            src_ref=sv_ref, dst_ref=sv_land, send_sem=d2d_send.at[n, 0], recv_sem=d2d_recv.at[n, 0],
            device_id=sib, device_id_type=MESH))
        for cp in d2d + local:
            cp.start()
        for cp in local + d2d:
            cp.wait()

        for a in range(n):
            for k in range(3):
                sendb[a][k] = (own[a][1 + k] + land[a][1 + k]).astype(BF16)
        chip_sv[...] = sv_ref[...] + sv_land[...]
        ici = _chip_copies(sendb, recvb, ici_send, ici_recv)
        sv_local = pltpu.make_async_copy(chip_sv, sv_slots.at[q], sv_sems.at[0])
        sv_out = [pltpu.make_async_remote_copy(
            src_ref=chip_sv, dst_ref=sv_slots.at[q], send_sem=sv_sems.at[1 + k], recv_sem=sv_sems.at[4 + k],
            device_id=(px, py, c), device_id_type=MESH) for k, (px, py) in enumerate(pats[1:])]
        for cp in ici + sv_out + [sv_local]:
            cp.start()
        for cp in ici:
            cp.wait()
        for k, (px, py) in enumerate(pats[1:]):
            sv_out[k].wait_send()
            pltpu.make_async_remote_copy(
                src_ref=chip_sv, dst_ref=sv_slots.at[2 * px + py], send_sem=sv_sems.at[1 + k],
                recv_sem=sv_sems.at[4 + k], device_id=(px, py, c), device_id_type=MESH).wait_recv()
        sv_local.wait()

        for a in range(n):
            grad = ((own[a][0] + land[a][0]) + recvb[a][0].astype(F32) + recvb[a][1].astype(F32)
                    + recvb[a][2].astype(F32))
            delta, m2, v2 = _adamw(w[a][...], grad, m[a][...], v[a][...])
            outs[a][...] = grad
            outs[n + a][...] = delta
            outs[2 * n + a][...] = m2
            outs[3 * n + a][...] = v2

    shard_out = [jax.ShapeDtypeStruct(s, F32) for s in shard]
    return pl.pallas_call(
        body, name="mixer_reduce_adamw",
        out_shape=shard_out * 4 + [jax.ShapeDtypeStruct((4,) + svec.shape, F32)],
        in_specs=[ANY] * n + [VMEM] * (1 + 3 * n), out_specs=[VMEM] * (4 * n) + [ANY],
        scratch_shapes=[pltpu.VMEM((4,) + s, F32) for s in shard] + [pltpu.VMEM((4,) + s, F32) for s in shard]
        + [pltpu.VMEM((3,) + s, BF16) for s in shard] + [pltpu.VMEM((3,) + s, BF16) for s in shard]
        + [pltpu.VMEM(svec.shape, F32), pltpu.VMEM(svec.shape, F32),
           pltpu.SemaphoreType.DMA((n + 1, 4)), pltpu.SemaphoreType.DMA((n + 1, 4)),
           pltpu.SemaphoreType.DMA((n, 3)), pltpu.SemaphoreType.DMA((n, 3)),
           pltpu.SemaphoreType.DMA((n, 4)), pltpu.SemaphoreType.DMA((7,))],
        compiler_params=pltpu.CompilerParams(vmem_limit_bytes=VMEM_LIMIT),
    )(*grads, svec, *ws, *ms, *vs)


SMALL_LAYOUT = [
    ("b_in", S_BIN, 16), ("ln_a_g", S_LNAG, 4), ("ln_a_b", S_LNAB, 4), ("w_spatial", S_WS, 512),
    ("b_spatial", S_BS, 4), ("conv_b_b", S_CBB, 4), ("ln_b_g", S_LNBG, 4), ("ln_b_b", S_LNBB, 4),
    ("b_out", S_BOUT, 8), ("ln1_g", S_LN1G, 8), ("ln1_b", S_LN1B, 8), ("conv_f_b", S_CFB, 44),
    ("ln2_g", S_LN2G, 8), ("ln2_b", S_LN2B, 8),
]


def small_adamw(sv_slots, ws, ms, vs):
    n = len(SMALL_LAYOUT)

    def body(*refs):
        s_ref = refs[0]
        w_refs, m_refs, v_refs = refs[1:1 + n], refs[1 + n:1 + 2 * n], refs[1 + 2 * n:1 + 3 * n]
        outs = refs[1 + 3 * n:]
        for p, (_, row0, rows) in enumerate(SMALL_LAYOUT):
            sl = pl.ds(row0, rows)
            grad = ((s_ref[0, sl, :] + s_ref[1, sl, :]) + s_ref[2, sl, :]) + s_ref[3, sl, :]
            delta, m2, v2 = _adamw(w_refs[p][...], grad, m_refs[p][...], v_refs[p][...])
            outs[p][...] = grad
            outs[n + p][...] = delta
            outs[2 * n + p][...] = m2
            outs[3 * n + p][...] = v2
        sl = pl.ds(S_LOSS, 8)
        outs[4 * n][...] = ((s_ref[0, sl, :] + s_ref[1, sl, :]) + s_ref[2, sl, :]) + s_ref[3, sl, :]

    shapes = [jax.ShapeDtypeStruct((rows, 128), F32) for _, _, rows in SMALL_LAYOUT]
    return pl.pallas_call(
        body, name="small_adamw", out_shape=shapes * 4 + [jax.ShapeDtypeStruct((8, 128), F32)],
        in_specs=[VMEM] * (1 + 3 * n), out_specs=[VMEM] * (4 * n + 1),
    )(sv_slots, *ws, *ms, *vs)


def mix_forward(x, sin, sout, b_in, ln_a_g, ln_a_b, w_spatial, bst, conv_b_w, conv_b_b, ln_b_g, ln_b_b,
                b_out, ln1_g, ln1_b, sup, sdown, tm):
    t = x.shape[0]
    nt = t // tm
    n_chunks = tm // CHUNK

    def body(x_ref, sin_ref, sout_ref, bin_ref, ga_ref, ba_ref, ws_ref, bst_ref, cw_ref, cb_ref, gb_ref,
             bb_ref, bout_ref, g1_ref, b1_ref, sup_ref, sdown_ref,
             h_ref, xhat1_ref, rstd1_ref, yb1_ref, gin_ref, gout_ref, gup_ref, gdown_ref,
             ext_ref, y_ref, wsm_ref, win_ref, wout_ref, load_sems,
             mix_send, mix_recv, mix_local, send_sems, recv_sems, local_sems):
        i = pl.program_id(0)
        mixer = ([sin_ref, sout_ref], [gin_ref, gout_ref], mix_send, mix_recv, mix_local)
        gather = ([sup_ref, sdown_ref], [gup_ref, gdown_ref], send_sems, recv_sems, local_sems)

        @pl.when(i == 0)
        def _():
            _gather_start(*mixer)
            _gather_relay(*mixer, via=[2, 2])
            _gather_finish(*mixer)
            _gather_start(*gather)
            loads = [pltpu.make_async_copy(gin_ref, win_ref, load_sems.at[0]),
                     pltpu.make_async_copy(gout_ref, wout_ref, load_sems.at[1])]
            for cp in loads:
                cp.start()
            for cp in loads:
                cp.wait()
            ext_ref[0:HALO_B, :] = jnp.zeros((HALO_B, D_B), F32)
            mask = _tril_mask()
            for hd in range(HEADS):
                wsm_ref[hd] = jnp.where(mask, ws_ref[hd], 0.0).astype(BF16)

        xb = x_ref[...].astype(BF16)
        for j in range(N_DEV):
            cols = slice(j * W_IN_BLK, (j + 1) * W_IN_BLK)
            h_ref[:, cols] = _nn(xb, win_ref[j]) + bin_ref[:, cols]

        def chunk(ci):
            r = _rows(ci, CHUNK)
            for hd in range(HEADS):
                _, _, u, _, _, _, _, _, sv = _mixer_a_head(h_ref, r, hd, ga_ref, ba_ref, wsm_ref, bst_ref)
                y_ref[r, hd * HEAD_DIM:(hd + 1) * HEAD_DIM] = (u * sv).astype(BF16)
            a_b = h_ref[r, 2 * D_A:2 * D_A + D_B]
            g_b = h_ref[r, 2 * D_A + D_B:D_IN]
            ext_ref[pl.ds(HALO_B + ci * CHUNK, CHUNK), :] = a_b * _sigmoid(g_b)

        _loop(n_chunks, chunk)

        def conv_rows(bi):
            base = bi * ROWS
            yb1 = _conv_b_block(ext_ref, base, cw_ref) + cb_ref[...]
            yb1_ref[pl.ds(base, ROWS), :] = yb1
            xhat, _ = _ln_stats(yb1)
            yb2 = xhat * gb_ref[...] + bb_ref[...]
            y_ref[pl.ds(base, ROWS), D_A:D] = (yb2 * _sigmoid(yb2)).astype(BF16)

        _loop(tm // ROWS, conv_rows)
        ext_ref[0:HALO_B, :] = ext_ref[tm:tm + HALO_B, :]

        mix = _nn(y_ref[...], wout_ref[...].reshape(D, D)) + bout_ref[...]
        xhat1, rstd1 = _ln_stats(ALPHA * x_ref[...] + mix)
        xhat1_ref[...] = xhat1
        rstd1_ref[...] = jnp.broadcast_to(rstd1, (tm, 128))

        @pl.when(i == (5 * nt) // 8)
        def _():
            _gather_relay(*gather, via=[1, 2])

        @pl.when(i == nt - 1)
        def _():
            _gather_finish(*gather)

    row = lambda w: pl.BlockSpec((tm, w), lambda i: (i, 0))
    return pl.pallas_call(
        body, name="mix_forward", grid=(nt,),
        in_specs=[row(D), ANY, ANY, _full(b_in.shape), _full(ln_a_g.shape),
                  _full(ln_a_b.shape), _full(w_spatial.shape), _full(bst.shape),
                  _full(conv_b_w.shape), _full(conv_b_b.shape), _full(ln_b_g.shape),
                  _full(ln_b_b.shape), _full(b_out.shape),
                  _full(ln1_g.shape), _full(ln1_b.shape), ANY, ANY],
        out_specs=[row(D_IN), row(D), row(128), row(D_B), ANY, ANY, ANY, ANY],
        out_shape=[jax.ShapeDtypeStruct((t, D_IN), F32), jax.ShapeDtypeStruct((t, D), F32),
                   jax.ShapeDtypeStruct((t, 128), F32), jax.ShapeDtypeStruct((t, D_B), F32)]
        + [jax.ShapeDtypeStruct((N_DEV,) + sh.shape, BF16) for sh in (sin, sout, sup, sdown)],
        scratch_shapes=[pltpu.VMEM((tm + HALO_B, D_B), F32), pltpu.VMEM((tm, D), BF16),
                        pltpu.VMEM((HEADS, CHUNK, CHUNK), BF16),
                        pltpu.VMEM((N_DEV,) + sin.shape, BF16), pltpu.VMEM((N_DEV,) + sout.shape, BF16),
                        pltpu.SemaphoreType.DMA((2,))] + _gather_scratch(2) + _gather_scratch(2),
        compiler_params=_params(("arbitrary",)),
    )(x, sin, sout, b_in, ln_a_g, ln_a_b, w_spatial, bst, conv_b_w, conv_b_b, ln_b_g, ln_b_b,
      b_out, ln1_g, ln1_b, sup, sdown)


def ffn_forward(xhat1, ln1_g, ln1_b, wup_g, cfw, cfb, wdown, ln2_g, ln2_b, target, tm):
    t = xhat1.shape[0]
    nt = t // tm

    def body(xh_ref, g1_ref, b1_ref, wup_ref, cfw_ref, cfb_ref, wdown_ref, g2_ref, b2_ref, tgt_ref,
             hu_ref, gv_ref, dr2_ref, loss_ref, sln2_ref,
             x1_ref, x1b_ref, hu32_ref, carry_ref, gbuf_ref, ffn_ref, acc_loss, acc_g2, acc_b2):
        i = pl.program_id(0)

        @pl.when(i == 0)
        def _():
            carry_ref[...] = jnp.zeros(carry_ref.shape, F32)
            acc_loss[...] = jnp.zeros(acc_loss.shape, F32)
            acc_g2[...] = jnp.zeros(acc_g2.shape, F32)
            acc_b2[...] = jnp.zeros(acc_b2.shape, F32)

        x1 = xh_ref[...] * g1_ref[...] + b1_ref[...]
        x1_ref[...] = x1
        x1b_ref[...] = x1.astype(BF16)

        def conv(g, j, base):
            if base == 0:
                win = jnp.concatenate([carry_ref[j], hu32_ref[g, 0:ROWS, :]], axis=0)
            else:
                win = hu32_ref[g, base - HALO_F:base + ROWS, :]
            taps = _taps_f(win)
            w = cfw_ref[j]
            return sum(taps[k] * w[k:k + 1, :] for k in range(KF)) + cfb_ref[j:j + 1, :]

        for f in range(N_F):
            hu32_ref[0] = _nn(x1b_ref[...], wup_ref[f])
            hu32_ref[1] = _nn(x1b_ref[...], wup_ref[N_F + f])

            def rows(bi, f=f):
                r = _rows(bi)
                gate = conv(0, f, bi * ROWS)
                val = conv(1, N_F + f, bi * ROWS)
                gbuf_ref[r, :] = (gate * _sigmoid(gate) * val).astype(BF16)
                gv_ref[f, r, :] = gate.astype(BF16)
                gv_ref[N_F + f, r, :] = val.astype(BF16)
                hu_ref[f, r, :] = hu32_ref[0, r, :].astype(BF16)
                hu_ref[N_F + f, r, :] = hu32_ref[1, r, :].astype(BF16)

            _loop(tm // ROWS, rows)
            carry_ref[f] = hu32_ref[0, tm - HALO_F:tm, :]
            carry_ref[N_F + f] = hu32_ref[1, tm - HALO_F:tm, :]
            part = _nn(gbuf_ref[...], wdown_ref[f])
            if f == 0:
                ffn_ref[...] = part
            else:
                ffn_ref[...] += part

        def tail(bi):
            r = _rows(bi, LN_ROWS)
            xhat2, rstd2 = _ln_stats(ALPHA * x1_ref[r, :] + ffn_ref[r, :])
            err = xhat2 * g2_ref[...] + b2_ref[...] - tgt_ref[r, :]
            e2 = _rsum8(err * err)
            acc_loss[...] += sum(e2[:, k * 128:(k + 1) * 128] for k in range(D // 128))
            dy = err * (1.0 / D)
            acc_g2[...] += _rsum8(dy * xhat2)
            acc_b2[...] += _rsum8(dy)
            dr2_ref[r, :] = _ln_bwd(dy * g2_ref[...], xhat2, rstd2)

        _loop(tm // LN_ROWS, tail)
        loss_ref[...] = acc_loss[...]

        @pl.when(i == nt - 1)
        def _():
            dg = jnp.sum(acc_g2[...], axis=0, keepdims=True)
            db = jnp.sum(acc_b2[...], axis=0, keepdims=True)
            for k in range(D // 128):
                sln2_ref[k:k + 1, :] = dg[:, k * 128:(k + 1) * 128]
                sln2_ref[8 + k:9 + k, :] = db[:, k * 128:(k + 1) * 128]

    row = pl.BlockSpec((tm, D), lambda i: (i, 0))
    return pl.pallas_call(
        body, name="ffn_forward", grid=(nt,),
        in_specs=[row, _full(ln1_g.shape), _full(ln1_b.shape), _resident(wup_g.shape),
                  _full(cfw.shape), _full(cfb.shape), _resident(wdown.shape),
                  _full(ln2_g.shape), _full(ln2_b.shape), row],
        out_specs=[pl.BlockSpec((N_DEV, tm, W_UP_BLK), lambda i: (0, i, 0)),
                   pl.BlockSpec((N_DEV, tm, W_UP_BLK), lambda i: (0, i, 0)), row,
                   _full((8, 128)), _full((16, 128))],
        out_shape=[jax.ShapeDtypeStruct((N_DEV, t, W_UP_BLK), BF16),
                   jax.ShapeDtypeStruct((N_DEV, t, W_UP_BLK), BF16), jax.ShapeDtypeStruct((t, D), F32),
                   jax.ShapeDtypeStruct((8, 128), F32), jax.ShapeDtypeStruct((16, 128), F32)],
        scratch_shapes=[pltpu.VMEM((tm, D), F32), pltpu.VMEM((tm, D), BF16),
                        pltpu.VMEM((2, tm, W_UP_BLK), F32),
                        pltpu.VMEM((N_DEV, HALO_F, W_UP_BLK), F32), pltpu.VMEM((tm, W_UP_BLK), BF16),
                        pltpu.VMEM((tm, D), F32), pltpu.VMEM((8, 128), F32),
                        pltpu.VMEM((8, D), F32), pltpu.VMEM((8, D), F32)],
        compiler_params=_params(("arbitrary",)),
    )(xhat1, ln1_g, ln1_b, wup_g, cfw, cfb, wdown, ln2_g, ln2_b, target)


def ffn_backward(order, dr2, xhat1, ln1_g, ln1_b, hu, gv, wup_g, cfw, wdown, tm):
    t = dr2.shape[0]
    nt = t // tm
    sub_rows = tm
    hu4 = hu.reshape(2, N_F, t, W_UP_BLK)
    gv4 = gv.reshape(2, N_F, t, W_UP_BLK)
    wup4 = wup_g.reshape(2, N_F, D, W_UP_BLK)
    cfw4 = cfw.reshape(2, N_F, KF, W_UP_BLK)

    def body(order_ref, dr2_ref, xh_ref, g1_ref, b1_ref, hu_ref, gv_ref, wup_ref, cfw_ref, wdown_ref,
             dwup_ref, dwdown_ref, dcfw_ref, dcfb_ref, dx1_ref, land_up_ref, land_down_ref,
             x1b_ref, drb_ref, dg_ref, dextg_ref, dextv_ref, gbuf_ref,
             dhug_ref, dhuv_ref, acc_wup, acc_wdown, acc_cfw, acc_cfb, sem, send_sems, recv_sems):
        f = order_ref[pl.program_id(0)]
        i = pl.program_id(1)
        x, y, c = _mesh_pos()
        half = D_FF // N_DEV

        def to_sibling(fi, k, src, land_ref, shard_chip):
            d = jnp.bitwise_xor(shard_chip, 2 * x + y)
            slot = jnp.where(d == 1, 2, jnp.where(d == 2, 1, d))
            return pltpu.make_async_remote_copy(
                src_ref=src, dst_ref=land_ref.at[slot], send_sem=send_sems.at[fi, k], recv_sem=recv_sems.at[fi, k],
                device_id=(x, y, 1 - c), device_id_type=MESH)

        def up_copy(fi, g):
            return to_sibling(fi, g, dwup_ref.at[g, fi], land_up_ref, 2 * g + fi // 2)

        def down_copy(fi):
            return to_sibling(fi, 2, dwdown_ref.at[fi, pl.ds((1 - c) * half, half)], land_down_ref, fi)

        @pl.when(i == 0)
        def _():
            acc_wup[...] = jnp.zeros(acc_wup.shape, F32)
            acc_wdown[...] = jnp.zeros(acc_wdown.shape, F32)
            acc_cfw[...] = jnp.zeros(acc_cfw.shape, F32)
            acc_cfb[...] = jnp.zeros(acc_cfb.shape, F32)
            dextg_ref[tm:tm + HALO_F, :] = jnp.zeros((HALO_F, W_UP_BLK), F32)
            dextv_ref[tm:tm + HALO_F, :] = jnp.zeros((HALO_F, W_UP_BLK), F32)

        w = [cfw_ref[0, 0], cfw_ref[1, 0]]
        dext = [dextg_ref, dextv_ref]
        dhu = [dhug_ref, dhuv_ref]

        def rows1(bi):
            r = _rows(bi)
            gate = gv_ref[0, 0, r, :].astype(F32)
            val = gv_ref[1, 0, r, :].astype(F32)
            sg = _sigmoid(gate)
            silu = gate * sg
            gbuf_ref[r, :] = (silu * val).astype(BF16)
            dg = dg_ref[r, :]
            dgate = dg * val * (sg * (1.0 + gate * (1.0 - sg)))
            dval = dg * silu
            dextg_ref[r, :] = dgate
            dextv_ref[r, :] = dval
            acc_cfb[0:8, :] += _rsum8(dgate)
            acc_cfb[8:16, :] += _rsum8(dval)

        def rows2(bi):
            r = _rows(bi)
            for g in range(2):
                win = dext[g][pl.ds(bi * ROWS, ROWS + HALO_F), :]
                n = ROWS + HALO_F
                later = [pltpu.roll(win, n - 2, 0)[0:ROWS, :], pltpu.roll(win, n - 1, 0)[0:ROWS, :],
                         win[0:ROWS, :]]
                d = sum(later[k] * w[g][k:k + 1, :] for k in range(KF))
                dhu[g][r, :] = d.astype(BF16)
                pre = hu_ref[g, 0, r, :].astype(F32)
                for k in range(KF):
                    r0 = 8 * (g * KF + k)
                    acc_cfw[r0:r0 + 8, :] += _rsum8(later[k] * pre)

        for sub in reversed(range(tm // sub_rows)):
            rs = slice(sub * sub_rows, (sub + 1) * sub_rows)
            blocks = range(sub * sub_rows // ROWS, (sub + 1) * sub_rows // ROWS)
            x1b_ref[rs, :] = (xh_ref[rs, :] * g1_ref[...] + b1_ref[...]).astype(BF16)
            drb_ref[rs, :] = dr2_ref[rs, :].astype(BF16)
            dg_ref[rs, :] = _nt(drb_ref[rs, :], wdown_ref[0])
            for bi in blocks:
                rows1(bi)
            for bi in blocks:
                rows2(bi)
            acc_wdown[...] += _tn(gbuf_ref[rs, :], drb_ref[rs, :])
            acc_wup[0] += _tn(dhug_ref[rs, :], x1b_ref[rs, :])
            acc_wup[1] += _tn(dhuv_ref[rs, :], x1b_ref[rs, :])
            dx1_ref[0, rs, :] = (_nt(dhug_ref[rs, :], wup_ref[0, 0])
                                 + _nt(dhuv_ref[rs, :], wup_ref[1, 0])).astype(BF16)
        dextg_ref[tm:tm + HALO_F, :] = dextg_ref[0:HALO_F, :]
        dextv_ref[tm:tm + HALO_F, :] = dextv_ref[0:HALO_F, :]

        @pl.when(i == nt - 1)
        def _():
            for g in range(2):
                dcfb_ref[g, 0] = jnp.sum(acc_cfb[8 * g:8 * g + 8, :], axis=0, keepdims=True)
                for k in range(KF):
                    r0 = 8 * (g * KF + k)
                    dcfw_ref[g, 0, k:k + 1, :] = jnp.sum(acc_cfw[r0:r0 + 8, :], axis=0, keepdims=True)
            cps = [pltpu.make_async_copy(acc_wup.at[0], dwup_ref.at[0, f], sem.at[0]),
                   pltpu.make_async_copy(acc_wup.at[1], dwup_ref.at[1, f], sem.at[1]),
                   pltpu.make_async_copy(acc_wdown, dwdown_ref.at[f], sem.at[2])]
            for cp in cps:
                cp.start()
            for cp in cps:
                cp.wait()
            down_copy(f).start()

            @pl.when(f % 2 != c)
            def _():
                up_copy(f, 0).start()
                up_copy(f, 1).start()

        @pl.when((i == nt - 1) & (pl.program_id(0) == N_F - 1))
        def _():
            for fi in range(N_F):
                down_copy(fi).wait()
                for g in range(2):
                    @pl.when(fi % 2 != c)
                    def _():
                        up_copy(fi, g).wait_send()

                    @pl.when(fi % 2 == c)
                    def _():
                        up_copy(fi, g).wait_recv()

    rev = lambda i: nt - 1 - i
    row = pl.BlockSpec((tm, D), lambda fo, i, o: (rev(i), 0))
    pair = lambda r, c: pl.BlockSpec((2, 1, r, c), lambda fo, i, o: (0, o[fo], 0, 0))
    tile = pl.BlockSpec((2, 1, tm, W_UP_BLK), lambda fo, i, o: (0, o[fo], rev(i), 0))
    return pl.pallas_call(
        body, name="ffn_backward",
        grid_spec=pltpu.PrefetchScalarGridSpec(
            num_scalar_prefetch=1, grid=(N_F, nt),
            in_specs=[row, row, _full(ln1_g.shape), _full(ln1_b.shape), tile, tile,
                      pair(D, W_UP_BLK), pair(KF, W_UP_BLK),
                      pl.BlockSpec((1, W_UP_BLK, D), lambda fo, i, o: (o[fo], 0, 0))],
            out_specs=[ANY, ANY, pair(KF, W_UP_BLK), pair(1, W_UP_BLK),
                       pl.BlockSpec((1, tm, D), lambda fo, i, o: (o[fo], rev(i), 0)), ANY, ANY],
            scratch_shapes=[pltpu.VMEM((tm, D), BF16), pltpu.VMEM((tm, D), BF16),
                            pltpu.VMEM((tm, W_UP_BLK), F32),
                            pltpu.VMEM((tm + HALO_F, W_UP_BLK), F32), pltpu.VMEM((tm + HALO_F, W_UP_BLK), F32),
                            pltpu.VMEM((tm, W_UP_BLK), BF16), pltpu.VMEM((tm, W_UP_BLK), BF16),
                            pltpu.VMEM((tm, W_UP_BLK), BF16),
                            pltpu.VMEM((2, W_UP_BLK, D), F32), pltpu.VMEM((W_UP_BLK, D), F32),
                            pltpu.VMEM((2 * KF * 8, W_UP_BLK), F32), pltpu.VMEM((16, W_UP_BLK), F32),
                            pltpu.SemaphoreType.DMA((3,)),
                            pltpu.SemaphoreType.DMA((N_F, 3)), pltpu.SemaphoreType.DMA((N_F, 3))]),
        out_shape=[jax.ShapeDtypeStruct((2, N_F, W_UP_BLK, D), F32),
                   jax.ShapeDtypeStruct((N_F, W_UP_BLK, D), F32),
                   jax.ShapeDtypeStruct((2, N_F, KF, W_UP_BLK), F32),
                   jax.ShapeDtypeStruct((2, N_F, 1, W_UP_BLK), F32),
                   jax.ShapeDtypeStruct((N_F, t, D), BF16),
                   jax.ShapeDtypeStruct((4, W_UP_BLK, D), F32),
                   jax.ShapeDtypeStruct((4, D_FF // N_DEV, D), F32)],
        compiler_params=_params(("arbitrary", "arbitrary")),
    )(order, dr2, xhat1, ln1_g, ln1_b, hu4, gv4, wup4, cfw4, wdown)


def mix_backward(x, h, yb1, dx1p, dr2, xhat1, rstd1, win_g, ln_a_g, ln_a_b, w_spatial, bst,
                 conv_b_w, ln_b_g, ln_b_b, wout, ln1_g, ffn_partials, tm):
    t = x.shape[0]
    n_p = len(ffn_partials)
    nt = t // tm
    n_chunks = tm // CHUNK
    halo_blocks = tm // HALO_B

    def body(x_ref, h_ref, halo_ref, yb1_ref, dx1p_ref, dr2_ref, xh1_ref, rstd1_ref, win_ref, ga_ref, ba_ref,
             ws_ref, bst_ref, cw_ref, gb_ref, bb_ref, wout_ref, g1_ref, *rest):
        p_refs, rest = rest[:n_p], rest[n_p:]
        gx_ref, dwin_ref, dwout_ref, dcw_ref, small_ref = rest[:5]
        land_refs, rest = rest[5:5 + n_p], rest[5 + n_p:]
        (ext_ref, dext_ref, y_ref, dy_ref, dh_ref, dmb_ref, wsm_ref,
         acc_win, acc_wout, acc_bin, acc_lnag, acc_lnab, acc_ws, acc_bs, acc_cbb, acc_lnbg,
         acc_lnbb, acc_bout, acc_ln1g, acc_ln1b, acc_cw, sem, send_sems, recv_sems) = rest
        i = pl.program_id(0)

        @pl.when(i == 0)
        def _():
            for cp in _chip_copies(p_refs, land_refs, send_sems, recv_sems):
                cp.start()

        first_tile = i == nt - 1
        accs = [acc_win, acc_wout, acc_bin, acc_lnag, acc_lnab, acc_ws, acc_bs, acc_cbb, acc_lnbg,
                acc_lnbb, acc_bout, acc_ln1g, acc_ln1b, acc_cw]

        @pl.when(i == 0)
        def _():
            for acc in accs:
                acc[...] = jnp.zeros(acc.shape, F32)
            dext_ref[tm:tm + HALO_B, :] = jnp.zeros((HALO_B, D_B), F32)
            mask = _tril_mask()
            for hd in range(HEADS):
                wsm_ref[hd] = jnp.where(mask, ws_ref[hd], 0.0).astype(BF16)

        def ln1_rows(bi):
            r = _rows(bi, LN_ROWS)
            part = [dx1p_ref[f, r, :].astype(F32) for f in range(N_F)]
            dx1 = ALPHA * dr2_ref[r, :] + ((part[0] + part[1]) + (part[2] + part[3]))
            xhat = xh1_ref[r, :]
            acc_ln1g[...] += _rsum8(dx1 * xhat)
            acc_ln1b[...] += _rsum8(dx1)
            dr1 = _ln_bwd(dx1 * g1_ref[...], xhat, rstd1_ref[r, 0:1])
            acc_bout[...] += _rsum8(dr1)
            gx_ref[r, :] = ALPHA * dr1
            dmb_ref[r, :] = dr1.astype(BF16)

        _loop(tm // LN_ROWS, ln1_rows)
        dy_ref[...] = _nt(dmb_ref[...], wout_ref[...])

        ha = halo_ref[:, 0:D_B]
        hg = halo_ref[:, D_B:2 * D_B]
        ext_ref[0:HALO_B, :] = jnp.where(first_tile, 0.0, 1.0) * (ha * _sigmoid(hg))

        def chunk(ci):
            r = _rows(ci, CHUNK)
            for hd in range(HEADS):
                sl = slice(hd * HEAD_DIM, (hd + 1) * HEAD_DIM)
                rows8 = slice(8 * hd, 8 * hd + 8)
                hus, hvs, u, cdf_u, cdf_v, xhat, rstd, vn, sv = _mixer_a_head(
                    h_ref, r, hd, ga_ref, ba_ref, wsm_ref, bst_ref)
                dy_a = dy_ref[r, sl]
                y_ref[r, sl] = (u * sv).astype(BF16)
                du = dy_a * sv
                dsv = dy_a * u
                dsvb = dsv.astype(BF16)
                acc_bs[hd] += dsv
                acc_ws[hd] += _nt(dsvb, vn)
                dvn = _tn(wsm_ref[hd], dsvb)
                acc_lnag[rows8, :] += _rsum8(dvn * xhat)
                acc_lnab[rows8, :] += _rsum8(dvn)
                dv = _ln_bwd(dvn * ga_ref[hd:hd + 1, :], xhat, rstd)
                slv = slice(D_A + hd * HEAD_DIM, D_A + (hd + 1) * HEAD_DIM)
                dhu = du * (cdf_u + hus * jnp.exp(-0.5 * hus * hus) * INV_SQRT_2PI)
                dhv = dv * (cdf_v + hvs * jnp.exp(-0.5 * hvs * hvs) * INV_SQRT_2PI)
                acc_bin[:, sl] += _rsum8(dhu)
                acc_bin[:, slv] += _rsum8(dhv)
                dh_ref[r, sl] = dhu.astype(BF16)
                dh_ref[r, slv] = dhv.astype(BF16)
            a_b = h_ref[r, 2 * D_A:2 * D_A + D_B]
            g_b = h_ref[r, 2 * D_A + D_B:D_IN]
            ext_ref[pl.ds(HALO_B + ci * CHUNK, CHUNK), :] = a_b * _sigmoid(g_b)

        _loop(n_chunks, chunk)

        def conv_rows(bi):
            base = bi * ROWS
            r = pl.ds(base, ROWS)
            xhat, rstd = _ln_stats(yb1_ref[r, :])
            yb2 = xhat * gb_ref[...] + bb_ref[...]
            sg = _sigmoid(yb2)
            y_ref[r, D_A:D] = (yb2 * sg).astype(BF16)
            dyb2 = dy_ref[r, D_A:D] * (sg * (1.0 + yb2 * (1.0 - sg)))
            acc_lnbg[...] += _rsum8(dyb2 * xhat)
            acc_lnbb[...] += _rsum8(dyb2)
            dyb1 = _ln_bwd(dyb2 * gb_ref[...], xhat, rstd)
            acc_cbb[...] += _rsum8(dyb1)
            dext_ref[r, :] = dyb1
            for k, tap in _taps(ext_ref[pl.ds(base, ROWS + HALO_B), :], CONV_B_OFFSETS):
                acc_cw[8 * k:8 * k + 8, :] += _rsum8(dyb1 * tap)

        _loop(tm // ROWS, conv_rows)

        def convt_rows(bi):
            base = bi * ROWS
            r = pl.ds(base, ROWS)
            dyb0 = jnp.zeros((ROWS, D_B), F32)
            for k, tap in _taps(dext_ref[pl.ds(base, ROWS + HALO_B), :], CONV_B_T_OFFSETS):
                dyb0 = dyb0 + tap * cw_ref[k:k + 1, :]
            a_b = h_ref[r, 2 * D_A:2 * D_A + D_B]
            sg = _sigmoid(h_ref[r, 2 * D_A + D_B:D_IN])
            da_b = dyb0 * sg
            dg_b = dyb0 * a_b * sg * (1.0 - sg)
            acc_bin[:, 2 * D_A:2 * D_A + D_B] += _rsum8(da_b)
            acc_bin[:, 2 * D_A + D_B:D_IN] += _rsum8(dg_b)
            dh_ref[r, 2 * D_A:2 * D_A + D_B] = da_b.astype(BF16)
            dh_ref[r, 2 * D_A + D_B:D_IN] = dg_b.astype(BF16)

        _loop(tm // ROWS, convt_rows)
        dext_ref[tm:tm + HALO_B, :] = dext_ref[0:HALO_B, :]

        acc_wout[...] += _tn(y_ref[...], dmb_ref[...])
        xt = x_ref[...].T.astype(BF16)
        dh_blocks = [dh_ref[:, j * W_IN_BLK:(j + 1) * W_IN_BLK] for j in range(N_DEV)]
        for j in range(N_DEV):
            acc_win[j] += _nn(xt, dh_blocks[j])
        gx_ref[...] += sum(_nt(dh_blocks[j], win_ref[j]) for j in range(N_DEV))

        @pl.when(i == nt - 1)
        def _():
            cps = [pltpu.make_async_copy(acc_win, dwin_ref, sem.at[0]),
                   pltpu.make_async_copy(acc_wout, dwout_ref, sem.at[1])]
            for cp in cps:
                cp.start()
            small_ref[...] = jnp.zeros(small_ref.shape, F32)

            def put_row_vector(row0, acc):
                vec = jnp.sum(acc[...], axis=0, keepdims=True)
                for k in range(vec.shape[1] // 128):
                    small_ref[row0 + k:row0 + k + 1, :] = vec[:, k * 128:(k + 1) * 128]

            put_row_vector(S_BIN, acc_bin)
            put_row_vector(S_CBB, acc_cbb)
            put_row_vector(S_LNBG, acc_lnbg)
            put_row_vector(S_LNBB, acc_lnbb)
            put_row_vector(S_BOUT, acc_bout)
            put_row_vector(S_LN1G, acc_ln1g)
            put_row_vector(S_LN1B, acc_ln1b)
            mask = _tril_mask()
            for hd in range(HEADS):
                rows8 = slice(8 * hd, 8 * hd + 8)
                small_ref[S_LNAG + hd:S_LNAG + hd + 1, :] = jnp.sum(acc_lnag[rows8, :], axis=0, keepdims=True)
                small_ref[S_LNAB + hd:S_LNAB + hd + 1, :] = jnp.sum(acc_lnab[rows8, :], axis=0, keepdims=True)
                small_ref[S_WS + hd * CHUNK:S_WS + (hd + 1) * CHUNK, :] = jnp.where(mask, acc_ws[hd], 0.0)
                small_ref[S_BS + hd:S_BS + hd + 1, :] = jnp.sum(acc_bs[hd].T, axis=0, keepdims=True)
            for k in range(KB):
                dcw_ref[k:k + 1, :] = jnp.sum(acc_cw[8 * k:8 * k + 8, :], axis=0, keepdims=True)
            for cp in cps:
                cp.wait()
            for cp in _chip_copies(p_refs, land_refs, send_sems, recv_sems):
                cp.wait()

    rev = lambda i: nt - 1 - i
    row = lambda w: pl.BlockSpec((tm, w), lambda i: (rev(i), 0))
    return pl.pallas_call(
        body, name="mix_backward", grid=(nt,),
        in_specs=[row(D), row(D_IN),
                  pl.BlockSpec((HALO_B, 2 * D_B), lambda i: (jnp.maximum(rev(i) * halo_blocks - 1, 0), 1)),
                  row(D_B), pl.BlockSpec((N_F, tm, D), lambda i: (0, rev(i), 0)),
                  row(D), row(D), row(128), _resident(win_g.shape), _full(ln_a_g.shape),
                  _full(ln_a_b.shape), _full(w_spatial.shape), _full(bst.shape), _full(conv_b_w.shape),
                  _full(ln_b_g.shape), _full(ln_b_b.shape),
                  _resident(wout.shape), _full(ln1_g.shape)] + [ANY] * n_p,
        out_specs=[row(D), ANY, ANY, _full((KB, D_B)), _full((S_MIX_ROWS, 128))] + [ANY] * n_p,
        out_shape=[jax.ShapeDtypeStruct((t, D), F32), jax.ShapeDtypeStruct((N_DEV, D, W_IN_BLK), F32),
                   jax.ShapeDtypeStruct((D, D), F32), jax.ShapeDtypeStruct((KB, D_B), F32),
                   jax.ShapeDtypeStruct((S_MIX_ROWS, 128), F32)]
        + [jax.ShapeDtypeStruct(p.shape, BF16) for p in ffn_partials],
        scratch_shapes=[pltpu.VMEM((tm + HALO_B, D_B), F32), pltpu.VMEM((tm + HALO_B, D_B), F32),
                        pltpu.VMEM((tm, D), BF16), pltpu.VMEM((tm, D), F32), pltpu.VMEM((tm, D_IN), BF16),
                        pltpu.VMEM((tm, D), BF16),
                        pltpu.VMEM((HEADS, CHUNK, CHUNK), BF16),
                        pltpu.VMEM((N_DEV, D, W_IN_BLK), F32), pltpu.VMEM((D, D), F32),
                        pltpu.VMEM((8, D_IN), F32), pltpu.VMEM((8 * HEADS, HEAD_DIM), F32),
                        pltpu.VMEM((8 * HEADS, HEAD_DIM), F32), pltpu.VMEM((HEADS, CHUNK, CHUNK), F32),
                        pltpu.VMEM((HEADS, CHUNK, CHUNK), F32), pltpu.VMEM((8, D_B), F32),
                        pltpu.VMEM((8, D_B), F32), pltpu.VMEM((8, D_B), F32), pltpu.VMEM((8, D), F32),
                        pltpu.VMEM((8, D), F32), pltpu.VMEM((8, D), F32), pltpu.VMEM((8 * KB, D_B), F32),
                        pltpu.SemaphoreType.DMA((2,)),
                        pltpu.SemaphoreType.DMA((n_p, 3)), pltpu.SemaphoreType.DMA((n_p, 3))],
        compiler_params=_params(("arbitrary",)),
    )(x, h, h, yb1, dx1p, dr2, xhat1, rstd1, win_g, ln_a_g, ln_a_b, w_spatial, bst, conv_b_w,
      ln_b_g, ln_b_b, wout, ln1_g, *ffn_partials)


def _rows128(a):
    return a.reshape(-1, 128)


def _pack_conv(cb, cf):
    lead = cb.shape[:-2]
    pad = [(0, 0)] * len(lead)
    flat = jnp.pad(cb.reshape(lead + (KB * 64,)), pad + [(0, 3 * W_UP_BLK - KB * 64)])
    rows = jnp.concatenate([cf, flat.reshape(lead + (3, W_UP_BLK))], axis=-2)
    return jnp.pad(rows, pad + [(0, 2), (0, 768 - W_UP_BLK)])


def _unpack_conv(p):
    lead = p.shape[:-2]
    cf = p[..., 0:KF, 0:W_UP_BLK]
    cb = p[..., 3:6, 0:W_UP_BLK].reshape(lead + (3 * W_UP_BLK,))[..., :KB * 64].reshape(lead + (KB, 64))
    return cb, cf


def kernel(x, w_in, b_in, ln_a_g, ln_a_b, w_spatial, b_spatial, conv_b_w, conv_b_b, ln_b_g, ln_b_b, w_out, b_out, ln1_g, ln1_b, w_up, conv_f_w, conv_f_b, w_down, ln2_g, ln2_b, loss_target, m_w_in, m_b_in, m_ln_a_g, m_ln_a_b, m_w_spatial, m_b_spatial, m_conv_b_w, m_conv_b_b, m_ln_b_g, m_ln_b_b, m_w_out, m_b_out, m_ln1_g, m_ln1_b, m_w_up, m_conv_f_w, m_conv_f_b, m_w_down, m_ln2_g, m_ln2_b, v_w_in, v_b_in, v_ln_a_g, v_ln_a_b, v_w_spatial, v_b_spatial, v_conv_b_w, v_conv_b_b, v_ln_b_g, v_ln_b_b, v_w_out, v_b_out, v_ln1_g, v_ln1_b, v_w_up, v_conv_f_w, v_conv_f_b, v_w_down, v_ln2_g, v_ln2_b):
    t = x.shape[1]
    x2 = x.reshape(t, D)
    target = loss_target.reshape(t, D)
    tm_fwd = min(t, 512)
    tm_bwd = min(t, 256)
    tm_ffn_bwd = min(t, 512)

    xi, yi, ci = _mesh_pos()
    jidx = jnp.stack([_lid(px, py, ci) for px, py in _chip_patterns(xi, yi)]).astype(jnp.int32)

    sin, sout, sup, sdown, conv_g = prepare_weights(w_in, w_out, w_up.T, w_down, _pack_conv(conv_b_w, conv_f_w))
    conv_b_all, cfw = _unpack_conv(conv_g)
    conv_b_full = conv_b_all.transpose(1, 0, 2).reshape(KB, D_B)
    cfb = conv_f_b.reshape(N_DEV, W_UP_BLK)
    row = lambda a: a.reshape(1, -1)
    bst = b_spatial.T

    h, xhat1, rstd1, yb1, win_g, wout_g, wup_g, wdown_g = mix_forward(
        x2, sin, sout, row(b_in), ln_a_g, ln_a_b, w_spatial, bst, conv_b_full, row(conv_b_b),
        row(ln_b_g), row(ln_b_b), row(b_out), row(ln1_g), row(ln1_b), sup, sdown, tm_fwd)
    wout_full = wout_g.reshape(D, D)
    wdown4 = wdown_g.reshape(N_F, W_UP_BLK, D)
    hu, gv, dr2, loss_part, s_ln2 = ffn_forward(
        xhat1, row(ln1_g), row(ln1_b), wup_g, cfw, cfb, wdown4, row(ln2_g), row(ln2_b), target, tm_bwd)

    order = jnp.where(ci == 0, jnp.array([1, 3, 0, 2], jnp.int32), jnp.array([0, 2, 1, 3], jnp.int32))
    dwup, dwdown, dcfw, dcfb, dx1p, *ffn_lands = ffn_backward(
        order, dr2, xhat1, row(ln1_g), row(ln1_b), hu, gv, wup_g, cfw, wdown4, tm_ffn_bwd)
    ffn_grads = [dwup.reshape(N_DEV, W_UP_BLK, D), dwdown.reshape(N_DEV, D_FF // N_DEV, D)]
    ffn_partials = [chip_partials("chip_partials_" + nm, g, l, jidx, rb)
                    for nm, g, l, rb in zip(["w_up", "w_down"], ffn_grads, ffn_lands, [352, 352])]
    grad_x, dwin, dwout, dcw, s_mix, *ffn_recvs = mix_backward(
        x2, h, yb1, dx1p, dr2, xhat1, rstd1, win_g, ln_a_g, ln_a_b, w_spatial, bst,
        conv_b_full, row(ln_b_g), row(ln_b_b), wout_full, row(ln1_g), ffn_partials, tm_bwd)

    dcfb_rows = jnp.pad(dcfb.reshape(-1, 128), ((0, 4), (0, 0)))
    svec = jnp.concatenate([s_mix, dcfb_rows, s_ln2, loss_part], axis=0)
    dconv = _pack_conv(dcw.reshape(KB, N_DEV, 64).transpose(1, 0, 2), dcfw.reshape(N_DEV, KF, W_UP_BLK))
    mix_grads = [dwin, dwout.reshape(N_DEV, D // N_DEV, D), dconv]
    mix_w = [w_in, w_out, _pack_conv(conv_b_w, conv_f_w)]
    mix_m = [m_w_in, m_w_out, _pack_conv(m_conv_b_w, m_conv_f_w)]
    mix_v = [v_w_in, v_w_out, _pack_conv(v_conv_b_w, v_conv_f_w)]
    *mix_out, sv_slots = mixer_reduce_adamw(mix_grads, svec, mix_w, mix_m, mix_v)
    big = {nm: [mix_out[k * 3 + p] for k in range(4)] for p, nm in enumerate(["w_in", "w_out", "conv"])}

    ffn_w = [(w_up.T, m_w_up.T, v_w_up.T), (w_down, m_w_down, v_w_down)]
    for nm, g, l, r, (w, m, v) in zip(["w_up", "w_down"], ffn_grads, ffn_lands, ffn_recvs, ffn_w):
        big[nm] = reduce_and_adamw("reduce_adamw_" + nm, g, l, r, w, m, v, jidx, 352)
    big["w_up"] = [o.T for o in big["w_up"]]
    for k in range(4):
        cb_k, cf_k = _unpack_conv(big["conv"][k])
        big.setdefault("conv_b_w", []).append(cb_k)
        big.setdefault("conv_f_w", []).append(cf_k)

    small_w = dict(b_in=b_in, ln_a_g=ln_a_g, ln_a_b=ln_a_b, w_spatial=w_spatial, b_spatial=b_spatial,
                   conv_b_b=conv_b_b, ln_b_g=ln_b_g, ln_b_b=ln_b_b, b_out=b_out, ln1_g=ln1_g,
                   ln1_b=ln1_b, conv_f_b=conv_f_b, ln2_g=ln2_g, ln2_b=ln2_b)
    small_m = dict(b_in=m_b_in, ln_a_g=m_ln_a_g, ln_a_b=m_ln_a_b, w_spatial=m_w_spatial,
                   b_spatial=m_b_spatial, conv_b_b=m_conv_b_b, ln_b_g=m_ln_b_g, ln_b_b=m_ln_b_b,
                   b_out=m_b_out, ln1_g=m_ln1_g, ln1_b=m_ln1_b, conv_f_b=m_conv_f_b, ln2_g=m_ln2_g,
                   ln2_b=m_ln2_b)
    small_v = dict(b_in=v_b_in, ln_a_g=v_ln_a_g, ln_a_b=v_ln_a_b, w_spatial=v_w_spatial,
                   b_spatial=v_b_spatial, conv_b_b=v_conv_b_b, ln_b_g=v_ln_b_g, ln_b_b=v_ln_b_b,
                   b_out=v_b_out, ln1_g=v_ln1_g, ln1_b=v_ln1_b, conv_f_b=v_conv_f_b, ln2_g=v_ln2_g,
                   ln2_b=v_ln2_b)
    order = [nm for nm, _, _ in SMALL_LAYOUT]
    small_out = small_adamw(sv_slots, [_rows128(small_w[nm]) for nm in order],
                            [_rows128(small_m[nm]) for nm in order], [_rows128(small_v[nm]) for nm in order])
    n_small = len(order)
    small = {nm: [small_out[k * n_small + p].reshape(small_w[nm].shape) for k in range(4)]
             for p, nm in enumerate(order)}
    loss = jnp.sum(small_out[4 * n_small]) * (0.5 / D)

    weights = ["w_in", "b_in", "ln_a_g", "ln_a_b", "w_spatial", "b_spatial", "conv_b_w", "conv_b_b",
               "ln_b_g", "ln_b_b", "w_out", "b_out", "ln1_g", "ln1_b", "w_up", "conv_f_w", "conv_f_b",
               "w_down", "ln2_g", "ln2_b"]
    result = lambda nm, k: big[nm][k] if nm in big else small[nm][k]
    return (loss, grad_x.reshape(x.shape), *[result(nm, 0) for nm in weights],
            *[result(nm, 1) for nm in weights], *[result(nm, 2) for nm in weights],
            *[result(nm, 3) for nm in weights])
```

```python
import functools
import math

import jax
import jax.numpy as jnp
from jax import lax
from jax.experimental import pallas as pl
from jax.experimental.pallas import tpu as pltpu

F32 = jnp.float32
BF16 = jnp.bfloat16

D = 1024
D_A = 512
D_B = 512
HEADS = 4
HEAD_DIM = 128
CHUNK = 128
KB = 31
KF = 3
D_FF = 2816
D_IN = 2048
N_DEV = 8
W_IN_BLK = D_IN // N_DEV
W_UP_BLK = 2 * D_FF // N_DEV
N_F = 4
LN_EPS = 1e-5
ALPHA = 2.0 ** 0.25

ADAM_LR = 0.001
ADAM_B1 = 0.9
ADAM_B2 = 0.999
ADAM_EPS = 1e-08
ADAM_WD = 0.01
ADAM_STEP = 10

INV_SQRT2 = 1.0 / math.sqrt(2.0)
INV_SQRT_2PI = 1.0 / math.sqrt(2.0 * math.pi)

HALO_B = 32
HALO_F = 8
ROWS = 64
LN_ROWS = 32
VMEM_LIMIT = 58 * 1024 * 1024

MESH = pl.DeviceIdType.MESH
ANY = pl.BlockSpec(memory_space=pl.ANY)
VMEM = pl.BlockSpec(memory_space=pltpu.VMEM)

S_BIN, S_LNAG, S_LNAB, S_WS, S_BS, S_CBB, S_LNBG, S_LNBB, S_BOUT, S_LN1G, S_LN1B = (
    0, 16, 24, 32, 544, 552, 560, 568, 576, 584, 592)
S_MIX_ROWS = 600
S_CFB = 600
S_LN2G = 648
S_LN2B = 656
S_LOSS = 664
S_ROWS = 672


def _tn(a, b):
    return lax.dot_general(a, b, (((0,), (0,)), ((), ())), preferred_element_type=F32)


def _nt(a, b):
    return lax.dot_general(a, b, (((1,), (1,)), ((), ())), preferred_element_type=F32)


def _nn(a, b):
    return jnp.dot(a, b, preferred_element_type=F32)


def _sigmoid(x):
    return 1.0 / (1.0 + jnp.exp(-x))


def _ln_stats(x):
    mu = jnp.mean(x, axis=-1, keepdims=True)
    xc = x - mu
    var = jnp.mean(xc * xc, axis=-1, keepdims=True)
    rstd = lax.rsqrt(var + LN_EPS)
    return xc * rstd, rstd


def _ln_bwd(dxhat, xhat, rstd):
    m1 = jnp.mean(dxhat, axis=-1, keepdims=True)
    m2 = jnp.mean(dxhat * xhat, axis=-1, keepdims=True)
    return rstd * (dxhat - m1 - xhat * m2)


def _rsum8(x):
    r, n = x.shape
    return x.reshape(r // 8, 8, n).sum(axis=0)


def _rows(i, n=ROWS):
    return pl.ds(i * n, n)


def _loop(n, body):
    for i in range(n):
        body(i)


def _tril_mask():
    r = lax.broadcasted_iota(jnp.int32, (CHUNK, CHUNK), 0)
    c = lax.broadcasted_iota(jnp.int32, (CHUNK, CHUNK), 1)
    return c <= r


def _mixer_a_head(h_ref, r, hd, ga_ref, ba_ref, wsm_ref, bst_ref):
    sl = slice(hd * HEAD_DIM, (hd + 1) * HEAD_DIM)
    hu = h_ref[r, sl]
    hv = h_ref[r, D_A + hd * HEAD_DIM:D_A + (hd + 1) * HEAD_DIM]
    cdf_u = 0.5 * (1.0 + lax.erf(hu * INV_SQRT2))
    cdf_v = 0.5 * (1.0 + lax.erf(hv * INV_SQRT2))
    u = hu * cdf_u
    xhat, rstd = _ln_stats(hv * cdf_v)
    vn = (xhat * ga_ref[hd:hd + 1, :] + ba_ref[hd:hd + 1, :]).astype(BF16)
    sv = _nn(wsm_ref[hd], vn) + bst_ref[:, hd:hd + 1]
    return hu, hv, u, cdf_u, cdf_v, xhat, rstd, vn, sv


def _taps(win, offsets):
    n = win.shape[0]
    for s in range(8):
        ks = [k for k, o in enumerate(offsets) if o % 8 == s]
        if ks:
            moved = win if s == 0 else pltpu.roll(win, n - s, 0)
            for k in ks:
                yield k, moved[offsets[k] - s:offsets[k] - s + ROWS, :]


CONV_B_OFFSETS = [2 + k for k in range(KB)]
CONV_B_T_OFFSETS = [30 - k for k in range(KB)]


def _conv_b_block(ext_ref, base, cw_ref):
    acc = jnp.zeros((ROWS, D_B), F32)
    for k, tap in _taps(ext_ref[pl.ds(base, ROWS + HALO_B), :], CONV_B_OFFSETS):
        acc = acc + tap * cw_ref[k:k + 1, :]
    return acc


def _taps_f(win):
    n = ROWS + HALO_F
    return [pltpu.roll(win, n - 6, 0)[0:ROWS, :], pltpu.roll(win, n - 7, 0)[0:ROWS, :], win[8:n, :]]


def _params(sem, **kw):
    return pltpu.CompilerParams(dimension_semantics=sem, vmem_limit_bytes=VMEM_LIMIT, **kw)


def _resident(shape):
    zeros = (0,) * len(shape)
    return pl.BlockSpec(shape, lambda *_: zeros, pipeline_mode=pl.Buffered(1))


def _full(shape):
    zeros = (0,) * len(shape)
    return pl.BlockSpec(shape, lambda *_: zeros)


def _mesh_pos():
    return lax.axis_index("x"), lax.axis_index("y"), lax.axis_index("c")


def _chip_patterns(x, y):
    return [(x, y), (1 - x, y), (x, 1 - y), (1 - x, 1 - y)]


def _lid(x, y, c):
    return 4 * x + 2 * y + c


def _gather_copy(outs, send_sems, recv_sems, a, k, block, to, src=None):
    blk = outs[a].at[_lid(*block)]
    return pltpu.make_async_remote_copy(
        src_ref=blk if src is None else src, dst_ref=blk,
        send_sem=send_sems.at[a, k], recv_sem=recv_sems.at[a, k], device_id=to, device_id_type=MESH)


def _gather_start(mine, outs, send_sems, recv_sems, local_sems):
    x, y, c = _mesh_pos()
    me = (x, y, c)
    for a in range(len(mine)):
        pltpu.make_async_copy(mine[a], outs[a].at[_lid(*me)], local_sems.at[a]).start()
        for k, to in enumerate([(x, y, 1 - c), (1 - x, y, c), (x, 1 - y, c)]):
            _gather_copy(outs, send_sems, recv_sems, a, k, me, to, src=mine[a]).start()


def _gather_relay(mine, outs, send_sems, recv_sems, local_sems, via):
    x, y, c = _mesh_pos()
    me, sib = (x, y, c), (x, y, 1 - c)
    copy = functools.partial(_gather_copy, outs, send_sems, recv_sems)
    source = {1: (1 - x, y, c), 2: (x, 1 - y, c)}
    for a in range(len(mine)):
        for k in (via[a], 3 - via[a]):
            copy(a, k, source[k], me).wait_recv()
            if k == via[a]:
                copy(a, 3, source[k], source[3 - k]).start()
            copy(a, 3 + k, source[k], sib).start()


def _gather_finish(mine, outs, send_sems, recv_sems, local_sems):
    x, y, c = _mesh_pos()
    me, sib = (x, y, c), (x, y, 1 - c)
    copy = functools.partial(_gather_copy, outs, send_sems, recv_sems)
    diag = (1 - x, 1 - y)
    n = len(mine)
    for a in range(n):
        copy(a, 3, (*diag, c), me).wait_recv()
        copy(a, 6, (*diag, c), sib).start()
    for a in range(n):
        copy(a, 0, sib, me).wait_recv()
        for k, chip in zip((4, 5, 6), [(1 - x, y), (x, 1 - y), diag]):
            copy(a, k, (*chip, 1 - c), me).wait_recv()
        for k in range(7):
            copy(a, k, me, sib, src=mine[a]).wait_send()
        pltpu.make_async_copy(mine[a], outs[a].at[_lid(*me)], local_sems.at[a]).wait()


def _gather_scratch(n):
    return [pltpu.SemaphoreType.DMA((n, 7)), pltpu.SemaphoreType.DMA((n, 7)), pltpu.SemaphoreType.DMA((n,))]


def prepare_weights(w_in, w_out, w_up_t, w_down, convp):
    def body(win_ref, wout_ref, wup_ref, wdown_ref, convp_ref,
             sin_ref, sout_ref, sup_ref, sdown_ref, gconv_ref, send_sems, recv_sems, local_sems):
        gather = ([convp_ref], [gconv_ref], send_sems, recv_sems, local_sems)
        _gather_start(*gather)
        sin_ref[...] = win_ref[...].astype(BF16)
        sout_ref[...] = wout_ref[...].astype(BF16)
        sup_ref[...] = wup_ref[...].T.astype(BF16)
        sdown_ref[...] = wdown_ref[...].astype(BF16)
        _gather_relay(*gather, via=[1])
        _gather_finish(*gather)

    return pl.pallas_call(
        body, name="prepare_weights",
        out_shape=[jax.ShapeDtypeStruct(w_in.shape, BF16), jax.ShapeDtypeStruct(w_out.shape, BF16),
                   jax.ShapeDtypeStruct(w_up_t.shape[::-1], BF16), jax.ShapeDtypeStruct(w_down.shape, BF16),
                   jax.ShapeDtypeStruct((N_DEV,) + convp.shape, F32)],
        in_specs=[VMEM] * 5, out_specs=[VMEM] * 4 + [ANY],
        scratch_shapes=_gather_scratch(1),
        compiler_params=pltpu.CompilerParams(vmem_limit_bytes=VMEM_LIMIT),
    )(w_in, w_out, w_up_t, w_down, convp)


def _chip_copies(p, land, send_sems, recv_sems):
    x, y, c = _mesh_pos()
    return [pltpu.make_async_remote_copy(
        src_ref=p[a].at[k], dst_ref=land[a].at[k], send_sem=send_sems.at[a, k], recv_sem=recv_sems.at[a, k],
        device_id=(px, py, c), device_id_type=MESH)
        for k, (px, py) in enumerate(_chip_patterns(x, y)[1:]) for a in range(len(p))]


def chip_partials(name, g, land, jidx, rb):
    _, r, c = g.shape

    def body(j_ref, g_ref, l_ref, o_ref):
        o_ref[...] = (g_ref[...] + l_ref[...]).astype(BF16)

    return pl.pallas_call(
        body, name=name,
        out_shape=jax.ShapeDtypeStruct((3, r, c), BF16),
        grid_spec=pltpu.PrefetchScalarGridSpec(
            num_scalar_prefetch=1, grid=(3, r // rb),
            in_specs=[pl.BlockSpec((1, rb, c), lambda k, i, j: (j[1 + k], i, 0)),
                      pl.BlockSpec((1, rb, c), lambda k, i, j: (1 + k, i, 0))],
            out_specs=pl.BlockSpec((1, rb, c), lambda k, i, j: (k, i, 0))),
        compiler_params=_params(("arbitrary", "arbitrary")),
    )(jidx, g, land)


def _adamw(w, g, m, v):
    m2 = ADAM_B1 * m + (1.0 - ADAM_B1) * g
    v2 = ADAM_B2 * v + (1.0 - ADAM_B2) * (g * g)
    m_hat = m2 / (1.0 - ADAM_B1 ** ADAM_STEP)
    v_hat = v2 / (1.0 - ADAM_B2 ** ADAM_STEP)
    delta = -ADAM_LR * (m_hat / (jnp.sqrt(v_hat) + ADAM_EPS) + ADAM_WD * w)
    return delta, m2, v2


def reduce_and_adamw(name, g, land, recv, w, m, v, jidx, rb):
    _, r, c = g.shape

    def body(j_ref, g_ref, l_ref, r_ref, w_ref, m_ref, v_ref, go_ref, do_ref, mo_ref, vo_ref):
        grad = (g_ref[0] + l_ref[0]) + r_ref[0].astype(F32) + r_ref[1].astype(F32) + r_ref[2].astype(F32)
        delta, m2, v2 = _adamw(w_ref[...], grad, m_ref[...], v_ref[...])
        go_ref[...] = grad
        do_ref[...] = delta
        mo_ref[...] = m2
        vo_ref[...] = v2

    blk = pl.BlockSpec((rb, c), lambda i, j: (i, 0))
    return pl.pallas_call(
        body, name=name,
        out_shape=[jax.ShapeDtypeStruct((r, c), F32)] * 4,
        grid_spec=pltpu.PrefetchScalarGridSpec(
            num_scalar_prefetch=1, grid=(r // rb,),
            in_specs=[pl.BlockSpec((1, rb, c), lambda i, j: (j[0], i, 0)),
                      pl.BlockSpec((1, rb, c), lambda i, j: (0, i, 0)),
                      pl.BlockSpec((3, rb, c), lambda i, j: (0, i, 0)),
                      blk, blk, blk],
            out_specs=[blk] * 4),
        compiler_params=_params(("arbitrary",)),
    )(jidx, g, land, recv, w, m, v)


def mixer_reduce_adamw(grads, svec, ws, ms, vs):
    n = len(grads)
    shard = [g.shape[1:] for g in grads]

    def body(*refs):
        g = refs[:n]
        sv_ref = refs[n]
        w, m, v = refs[n + 1:2 * n + 1], refs[2 * n + 1:3 * n + 1], refs[3 * n + 1:4 * n + 1]
        outs = refs[4 * n + 1:8 * n + 1]
        sv_slots = refs[8 * n + 1]
        rest = refs[8 * n + 2:]
        own, land, sendb, recvb = rest[:n], rest[n:2 * n], rest[2 * n:3 * n], rest[3 * n:4 * n]
        sv_land, chip_sv, d2d_send, d2d_recv, ici_send, ici_recv, local_sems, sv_sems = rest[4 * n:]
        x, y, c = _mesh_pos()
        sib = (x, y, 1 - c)
        pats = _chip_patterns(x, y)
        q = 2 * x + y

        d2d, local = [], []
        for a in range(n):
            for k, (px, py) in enumerate(pats):
                d2d.append(pltpu.make_async_remote_copy(
                    src_ref=g[a].at[_lid(px, py, 1 - c)], dst_ref=land[a].at[k],
                    send_sem=d2d_send.at[a, k], recv_sem=d2d_recv.at[a, k], device_id=sib, device_id_type=MESH))
                local.append(pltpu.make_async_copy(g[a].at[_lid(px, py, c)], own[a].at[k], local_sems.at[a, k]))
        d2d.append(pltpu.make_async_remote_copy(
            src_ref=sv_ref, dst_ref=sv_land, send_sem=d2d_send.at[n, 0], recv_sem=d2d_recv.at[n, 0],
            device_id=sib, device_id_type=MESH))
        for cp in d2d + local:
            cp.start()
        for cp in local + d2d:
            cp.wait()

        for a in range(n):
            for k in range(3):
                sendb[a][k] = (own[a][1 + k] + land[a][1 + k]).astype(BF16)
        chip_sv[...] = sv_ref[...] + sv_land[...]
        ici = _chip_copies(sendb, recvb, ici_send, ici_recv)
        sv_local = pltpu.make_async_copy(chip_sv, sv_slots.at[q], sv_sems.at[0])
        sv_out = [pltpu.make_async_remote_copy(
            src_ref=chip_sv, dst_ref=sv_slots.at[q], send_sem=sv_sems.at[1 + k], recv_sem=sv_sems.at[4 + k],
            device_id=(px, py, c), device_id_type=MESH) for k, (px, py) in enumerate(pats[1:])]
        for cp in ici + sv_out + [sv_local]:
            cp.start()
        for cp in ici:
            cp.wait()
        for k, (px, py) in enumerate(pats[1:]):
            sv_out[k].wait_send()
            pltpu.make_async_remote_copy(
                src_ref=chip_sv, dst_ref=sv_slots.at[2 * px + py], send_sem=sv_sems.at[1 + k],
                recv_sem=sv_sems.at[4 + k], device_id=(px, py, c), device_id_type=MESH).wait_recv()
        sv_local.wait()

        for a in range(n):
            grad = ((own[a][0] + land[a][0]) + recvb[a][0].astype(F32) + recvb[a][1].astype(F32)
                    + recvb[a][2].astype(F32))
            delta, m2, v2 = _adamw(w[a][...], grad, m[a][...], v[a][...])
            outs[a][...] = grad
            outs[n + a][...] = delta
            outs[2 * n + a][...] = m2
            outs[3 * n + a][...] = v2

    shard_out = [jax.ShapeDtypeStruct(s, F32) for s in shard]
    return pl.pallas_call(
        body, name="mixer_reduce_adamw",
        out_shape=shard_out * 4 + [jax.ShapeDtypeStruct((4,) + svec.shape, F32)],
        in_specs=[ANY] * n + [VMEM] * (1 + 3 * n), out_specs=[VMEM] * (4 * n) + [ANY],
        scratch_shapes=[pltpu.VMEM((4,) + s, F32) for s in shard] + [pltpu.VMEM((4,) + s, F32) for s in shard]
        + [pltpu.VMEM((3,) + s, BF16) for s in shard] + [pltpu.VMEM((3,) + s, BF16) for s in shard]
        + [pltpu.VMEM(svec.shape, F32), pltpu.VMEM(svec.shape, F32),
           pltpu.SemaphoreType.DMA((n + 1, 4)), pltpu.SemaphoreType.DMA((n + 1, 4)),
           pltpu.SemaphoreType.DMA((n, 3)), pltpu.SemaphoreType.DMA((n, 3)),
           pltpu.SemaphoreType.DMA((n, 4)), pltpu.SemaphoreType.DMA((7,))],
        compiler_params=pltpu.CompilerParams(vmem_limit_bytes=VMEM_LIMIT),
    )(*grads, svec, *ws, *ms, *vs)


SMALL_LAYOUT = [
    ("b_in", S_BIN, 16), ("ln_a_g", S_LNAG, 4), ("ln_a_b", S_LNAB, 4), ("w_spatial", S_WS, 512),
    ("b_spatial", S_BS, 4), ("conv_b_b", S_CBB, 4), ("ln_b_g", S_LNBG, 4), ("ln_b_b", S_LNBB, 4),
    ("b_out", S_BOUT, 8), ("ln1_g", S_LN1G, 8), ("ln1_b", S_LN1B, 8), ("conv_f_b", S_CFB, 44),
    ("ln2_g", S_LN2G, 8), ("ln2_b", S_LN2B, 8),
]


def small_adamw(sv_slots, ws, ms, vs):
    n = len(SMALL_LAYOUT)

    def body(*refs):
        s_ref = refs[0]
        w_refs, m_refs, v_refs = refs[1:1 + n], refs[1 + n:1 + 2 * n], refs[1 + 2 * n:1 + 3 * n]
        outs = refs[1 + 3 * n:]
        for p, (_, row0, rows) in enumerate(SMALL_LAYOUT):
            sl = pl.ds(row0, rows)
            grad = ((s_ref[0, sl, :] + s_ref[1, sl, :]) + s_ref[2, sl, :]) + s_ref[3, sl, :]
            delta, m2, v2 = _adamw(w_refs[p][...], grad, m_refs[p][...], v_refs[p][...])
            outs[p][...] = grad
            outs[n + p][...] = delta
            outs[2 * n + p][...] = m2
            outs[3 * n + p][...] = v2
        sl = pl.ds(S_LOSS, 8)
        outs[4 * n][...] = ((s_ref[0, sl, :] + s_ref[1, sl, :]) + s_ref[2, sl, :]) + s_ref[3, sl, :]

    shapes = [jax.ShapeDtypeStruct((rows, 128), F32) for _, _, rows in SMALL_LAYOUT]
    return pl.pallas_call(
        body, name="small_adamw", out_shape=shapes * 4 + [jax.ShapeDtypeStruct((8, 128), F32)],
        in_specs=[VMEM] * (1 + 3 * n), out_specs=[VMEM] * (4 * n + 1),
    )(sv_slots, *ws, *ms, *vs)


def mix_forward(x, sin, sout, b_in, ln_a_g, ln_a_b, w_spatial, bst, conv_b_w, conv_b_b, ln_b_g, ln_b_b,
                b_out, ln1_g, ln1_b, sup, sdown, tm):
    t = x.shape[0]
    nt = t // tm
    n_chunks = tm // CHUNK

    def body(x_ref, sin_ref, sout_ref, bin_ref, ga_ref, ba_ref, ws_ref, bst_ref, cw_ref, cb_ref, gb_ref,
             bb_ref, bout_ref, g1_ref, b1_ref, sup_ref, sdown_ref,
             h_ref, xhat1_ref, rstd1_ref, yb1_ref, gin_ref, gout_ref, gup_ref, gdown_ref,
             ext_ref, y_ref, wsm_ref, win_ref, wout_ref, load_sems,
             mix_send, mix_recv, mix_local, send_sems, recv_sems, local_sems):
        i = pl.program_id(0)
        mixer = ([sin_ref, sout_ref], [gin_ref, gout_ref], mix_send, mix_recv, mix_local)
        gather = ([sup_ref, sdown_ref], [gup_ref, gdown_ref], send_sems, recv_sems, local_sems)

        @pl.when(i == 0)
        def _():
            _gather_start(*mixer)
            _gather_relay(*mixer, via=[2, 2])
            _gather_finish(*mixer)
            _gather_start(*gather)
            loads = [pltpu.make_async_copy(gin_ref, win_ref, load_sems.at[0]),
                     pltpu.make_async_copy(gout_ref, wout_ref, load_sems.at[1])]
            for cp in loads:
                cp.start()
            for cp in loads:
                cp.wait()
            ext_ref[0:HALO_B, :] = jnp.zeros((HALO_B, D_B), F32)
            mask = _tril_mask()
            for hd in range(HEADS):
                wsm_ref[hd] = jnp.where(mask, ws_ref[hd], 0.0).astype(BF16)

        xb = x_ref[...].astype(BF16)
        for j in range(N_DEV):
            cols = slice(j * W_IN_BLK, (j + 1) * W_IN_BLK)
            h_ref[:, cols] = _nn(xb, win_ref[j]) + bin_ref[:, cols]

        def chunk(ci):
            r = _rows(ci, CHUNK)
            for hd in range(HEADS):
                _, _, u, _, _, _, _, _, sv = _mixer_a_head(h_ref, r, hd, ga_ref, ba_ref, wsm_ref, bst_ref)
                y_ref[r, hd * HEAD_DIM:(hd + 1) * HEAD_DIM] = (u * sv).astype(BF16)
            a_b = h_ref[r, 2 * D_A:2 * D_A + D_B]
            g_b = h_ref[r, 2 * D_A + D_B:D_IN]
            ext_ref[pl.ds(HALO_B + ci * CHUNK, CHUNK), :] = a_b * _sigmoid(g_b)

        _loop(n_chunks, chunk)

        def conv_rows(bi):
            base = bi * ROWS
            yb1 = _conv_b_block(ext_ref, base, cw_ref) + cb_ref[...]
            yb1_ref[pl.ds(base, ROWS), :] = yb1
            xhat, _ = _ln_stats(yb1)
            yb2 = xhat * gb_ref[...] + bb_ref[...]
            y_ref[pl.ds(base, ROWS), D_A:D] = (yb2 * _sigmoid(yb2)).astype(BF16)

        _loop(tm // ROWS, conv_rows)
        ext_ref[0:HALO_B, :] = ext_ref[tm:tm + HALO_B, :]

        mix = _nn(y_ref[...], wout_ref[...].reshape(D, D)) + bout_ref[...]
        xhat1, rstd1 = _ln_stats(ALPHA * x_ref[...] + mix)
        xhat1_ref[...] = xhat1
        rstd1_ref[...] = jnp.broadcast_to(rstd1, (tm, 128))

        @pl.when(i == (5 * nt) // 8)
        def _():
            _gather_relay(*gather, via=[1, 2])

        @pl.when(i == nt - 1)
        def _():
            _gather_finish(*gather)

    row = lambda w: pl.BlockSpec((tm, w), lambda i: (i, 0))
    return pl.pallas_call(
        body, name="mix_forward", grid=(nt,),
        in_specs=[row(D), ANY, ANY, _full(b_in.shape), _full(ln_a_g.shape),
                  _full(ln_a_b.shape), _full(w_spatial.shape), _full(bst.shape),
                  _full(conv_b_w.shape), _full(conv_b_b.shape), _full(ln_b_g.shape),
                  _full(ln_b_b.shape), _full(b_out.shape),
                  _full(ln1_g.shape), _full(ln1_b.shape), ANY, ANY],
        out_specs=[row(D_IN), row(D), row(128), row(D_B), ANY, ANY, ANY, ANY],
        out_shape=[jax.ShapeDtypeStruct((t, D_IN), F32), jax.ShapeDtypeStruct((t, D), F32),
                   jax.ShapeDtypeStruct((t, 128), F32), jax.ShapeDtypeStruct((t, D_B), F32)]
        + [jax.ShapeDtypeStruct((N_DEV,) + sh.shape, BF16) for sh in (sin, sout, sup, sdown)],
        scratch_shapes=[pltpu.VMEM((tm + HALO_B, D_B), F32), pltpu.VMEM((tm, D), BF16),
                        pltpu.VMEM((HEADS, CHUNK, CHUNK), BF16),
                        pltpu.VMEM((N_DEV,) + sin.shape, BF16), pltpu.VMEM((N_DEV,) + sout.shape, BF16),
                        pltpu.SemaphoreType.DMA((2,))] + _gather_scratch(2) + _gather_scratch(2),
        compiler_params=_params(("arbitrary",)),
    )(x, sin, sout, b_in, ln_a_g, ln_a_b, w_spatial, bst, conv_b_w, conv_b_b, ln_b_g, ln_b_b,
      b_out, ln1_g, ln1_b, sup, sdown)


def ffn_forward(xhat1, ln1_g, ln1_b, wup_g, cfw, cfb, wdown, ln2_g, ln2_b, target, tm):
    t = xhat1.shape[0]
    nt = t // tm

    def body(xh_ref, g1_ref, b1_ref, wup_hbm, cfw_ref, cfb_ref, wdown_hbm, g2_ref, b2_ref, tgt_ref,
             hu_ref, gv_ref, dr2_ref, loss_ref, sln2_ref,
             x1_ref, x1b_ref, hu32_ref, carry_ref, gbuf_ref, ffn_ref, acc_loss, acc_g2, acc_b2,
             wup_ref, wdown_ref, wsem):
        i = pl.program_id(0)

        def weights(f):
            return [pltpu.make_async_copy(wup_hbm.at[f], wup_ref.at[f], wsem.at[f, 0]),
                    pltpu.make_async_copy(wup_hbm.at[N_F + f], wup_ref.at[N_F + f], wsem.at[f, 1]),
                    pltpu.make_async_copy(wdown_hbm.at[f], wdown_ref.at[f], wsem.at[f, 2])]

        @pl.when(i == 0)
        def _():
            for f in range(N_F):
                for cp in weights(f):
                    cp.start()
            carry_ref[...] = jnp.zeros(carry_ref.shape, F32)
            acc_loss[...] = jnp.zeros(acc_loss.shape, F32)
            acc_g2[...] = jnp.zeros(acc_g2.shape, F32)
            acc_b2[...] = jnp.zeros(acc_b2.shape, F32)

        x1 = xh_ref[...] * g1_ref[...] + b1_ref[...]
        x1_ref[...] = x1
        x1b_ref[...] = x1.astype(BF16)

        def conv(g, j, base):
            if base == 0:
                win = jnp.concatenate([carry_ref[j], hu32_ref[g, 0:ROWS, :]], axis=0)
            else:
                win = hu32_ref[g, base - HALO_F:base + ROWS, :]
            taps = _taps_f(win)
            w = cfw_ref[j]
            return sum(taps[k] * w[k:k + 1, :] for k in range(KF)) + cfb_ref[j:j + 1, :]

        for f in range(N_F):
            @pl.when(i == 0)
            def _(f=f):
                for cp in weights(f):
                    cp.wait()

            hu32_ref[0] = _nn(x1b_ref[...], wup_ref[f])
            hu32_ref[1] = _nn(x1b_ref[...], wup_ref[N_F + f])

            def rows(bi, f=f):
                r = _rows(bi)
                gate = conv(0, f, bi * ROWS)
                val = conv(1, N_F + f, bi * ROWS)
                gbuf_ref[r, :] = (gate * _sigmoid(gate) * val).astype(BF16)
                gv_ref[f, r, :] = gate.astype(BF16)
                gv_ref[N_F + f, r, :] = val.astype(BF16)
                hu_ref[f, r, :] = hu32_ref[0, r, :].astype(BF16)
                hu_ref[N_F + f, r, :] = hu32_ref[1, r, :].astype(BF16)

            _loop(tm // ROWS, rows)
            carry_ref[f] = hu32_ref[0, tm - HALO_F:tm, :]
            carry_ref[N_F + f] = hu32_ref[1, tm - HALO_F:tm, :]
            part = _nn(gbuf_ref[...], wdown_ref[f])
            if f == 0:
                ffn_ref[...] = part
            else:
                ffn_ref[...] += part

        def tail(bi):
            r = _rows(bi, LN_ROWS)
            xhat2, rstd2 = _ln_stats(ALPHA * x1_ref[r, :] + ffn_ref[r, :])
            err = xhat2 * g2_ref[...] + b2_ref[...] - tgt_ref[r, :]
            e2 = _rsum8(err * err)
            acc_loss[...] += sum(e2[:, k * 128:(k + 1) * 128] for k in range(D // 128))
            dy = err * (1.0 / D)
            acc_g2[...] += _rsum8(dy * xhat2)
            acc_b2[...] += _rsum8(dy)
            dr2_ref[r, :] = _ln_bwd(dy * g2_ref[...], xhat2, rstd2)

        _loop(tm // LN_ROWS, tail)
        loss_ref[...] = acc_loss[...]

        @pl.when(i == nt - 1)
        def _():
            dg = jnp.sum(acc_g2[...], axis=0, keepdims=True)
            db = jnp.sum(acc_b2[...], axis=0, keepdims=True)
            for k in range(D // 128):
                sln2_ref[k:k + 1, :] = dg[:, k * 128:(k + 1) * 128]
                sln2_ref[8 + k:9 + k, :] = db[:, k * 128:(k + 1) * 128]

    row = pl.BlockSpec((tm, D), lambda i: (i, 0))
    return pl.pallas_call(
        body, name="ffn_forward", grid=(nt,),
        in_specs=[row, _full(ln1_g.shape), _full(ln1_b.shape), ANY,
                  _full(cfw.shape), _full(cfb.shape), ANY,
                  _full(ln2_g.shape), _full(ln2_b.shape), row],
        out_specs=[pl.BlockSpec((N_DEV, tm, W_UP_BLK), lambda i: (0, i, 0)),
                   pl.BlockSpec((N_DEV, tm, W_UP_BLK), lambda i: (0, i, 0)), row,
                   _full((8, 128)), _full((16, 128))],
        out_shape=[jax.ShapeDtypeStruct((N_DEV, t, W_UP_BLK), BF16),
                   jax.ShapeDtypeStruct((N_DEV, t, W_UP_BLK), BF16), jax.ShapeDtypeStruct((t, D), F32),
                   jax.ShapeDtypeStruct((8, 128), F32), jax.ShapeDtypeStruct((16, 128), F32)],
        scratch_shapes=[pltpu.VMEM((tm, D), F32), pltpu.VMEM((tm, D), BF16),
                        pltpu.VMEM((2, tm, W_UP_BLK), F32),
                        pltpu.VMEM((N_DEV, HALO_F, W_UP_BLK), F32), pltpu.VMEM((tm, W_UP_BLK), BF16),
                        pltpu.VMEM((tm, D), F32), pltpu.VMEM((8, 128), F32),
                        pltpu.VMEM((8, D), F32), pltpu.VMEM((8, D), F32),
                        pltpu.VMEM(wup_g.shape, BF16), pltpu.VMEM(wdown.shape, BF16),
                        pltpu.SemaphoreType.DMA((N_F, 3))],
        compiler_params=_params(("arbitrary",)),
    )(xhat1, ln1_g, ln1_b, wup_g, cfw, cfb, wdown, ln2_g, ln2_b, target)


def ffn_backward(order, dr2, xhat1, ln1_g, ln1_b, hu, gv, wup_g, cfw, wdown, tm):
    t = dr2.shape[0]
    nt = t // tm
    sub_rows = tm
    hu4 = hu.reshape(2, N_F, t, W_UP_BLK)
    gv4 = gv.reshape(2, N_F, t, W_UP_BLK)
    wup4 = wup_g.reshape(2, N_F, D, W_UP_BLK)
    cfw4 = cfw.reshape(2, N_F, KF, W_UP_BLK)

    def body(order_ref, dr2_ref, xh_ref, g1_ref, b1_ref, hu_ref, gv_ref, wup_ref, cfw_ref, wdown_ref,
             dwup_ref, dwdown_ref, dcfw_ref, dcfb_ref, dx1_ref, land_up_ref, land_down_ref,
             x1b_ref, drb_ref, dg_ref, dextg_ref, dextv_ref, gbuf_ref,
             dhug_ref, dhuv_ref, acc_wup, acc_wdown, acc_cfw, acc_cfb, sem, send_sems, recv_sems):
        fo = pl.program_id(0)
        f = order_ref[fo]
        f_prev = order_ref[jnp.maximum(fo - 1, 0)]
        i = pl.program_id(1)
        x, y, c = _mesh_pos()
        half = D_FF // N_DEV

        def to_sibling(fi, k, src, land_ref, shard_chip):
            d = jnp.bitwise_xor(shard_chip, 2 * x + y)
            slot = jnp.where(d == 1, 2, jnp.where(d == 2, 1, d))
            return pltpu.make_async_remote_copy(
                src_ref=src, dst_ref=land_ref.at[slot], send_sem=send_sems.at[fi, k], recv_sem=recv_sems.at[fi, k],
                device_id=(x, y, 1 - c), device_id_type=MESH)

        def up_copy(fi, g):
            return to_sibling(fi, g, dwup_ref.at[g, fi], land_up_ref, 2 * g + fi // 2)

        def down_copy(fi):
            return to_sibling(fi, 2, dwdown_ref.at[fi, pl.ds((1 - c) * half, half)], land_down_ref, fi)

        def flush(fi):
            return [pltpu.make_async_copy(acc_wup.at[0], dwup_ref.at[0, fi], sem.at[0]),
                    pltpu.make_async_copy(acc_wup.at[1], dwup_ref.at[1, fi], sem.at[1]),
                    pltpu.make_async_copy(acc_wdown, dwdown_ref.at[fi], sem.at[2])]

        def flushed(fi):
            for cp in flush(fi):
                cp.wait()
            down_copy(fi).start()

            @pl.when(fi % 2 != c)
            def _():
                up_copy(fi, 0).start()
                up_copy(fi, 1).start()

        @pl.when(i == 0)
        def _():
            acc_cfw[...] = jnp.zeros(acc_cfw.shape, F32)
            acc_cfb[...] = jnp.zeros(acc_cfb.shape, F32)
            dextg_ref[tm:tm + HALO_F, :] = jnp.zeros((HALO_F, W_UP_BLK), F32)
            dextv_ref[tm:tm + HALO_F, :] = jnp.zeros((HALO_F, W_UP_BLK), F32)

        w = [cfw_ref[0, 0], cfw_ref[1, 0]]
        dext = [dextg_ref, dextv_ref]
        dhu = [dhug_ref, dhuv_ref]

        def rows1(bi):
            r = _rows(bi)
            gate = gv_ref[0, 0, r, :].astype(F32)
            val = gv_ref[1, 0, r, :].astype(F32)
            sg = _sigmoid(gate)
            silu = gate * sg
            gbuf_ref[r, :] = (silu * val).astype(BF16)
            dg = dg_ref[r, :]
            dgate = dg * val * (sg * (1.0 + gate * (1.0 - sg)))
            dval = dg * silu
            dextg_ref[r, :] = dgate
            dextv_ref[r, :] = dval
            acc_cfb[0:8, :] += _rsum8(dgate)
            acc_cfb[8:16, :] += _rsum8(dval)

        def rows2(bi):
            r = _rows(bi)
            for g in range(2):
                win = dext[g][pl.ds(bi * ROWS, ROWS + HALO_F), :]
                n = ROWS + HALO_F
                later = [pltpu.roll(win, n - 2, 0)[0:ROWS, :], pltpu.roll(win, n - 1, 0)[0:ROWS, :],
                         win[0:ROWS, :]]
                d = sum(later[k] * w[g][k:k + 1, :] for k in range(KF))
                dhu[g][r, :] = d.astype(BF16)
                pre = hu_ref[g, 0, r, :].astype(F32)
                for k in range(KF):
                    r0 = 8 * (g * KF + k)
                    acc_cfw[r0:r0 + 8, :] += _rsum8(later[k] * pre)

        for sub in reversed(range(tm // sub_rows)):
            rs = slice(sub * sub_rows, (sub + 1) * sub_rows)
            blocks = range(sub * sub_rows // ROWS, (sub + 1) * sub_rows // ROWS)
            x1b_ref[rs, :] = (xh_ref[rs, :] * g1_ref[...] + b1_ref[...]).astype(BF16)
            drb_ref[rs, :] = dr2_ref[rs, :].astype(BF16)
            dg_ref[rs, :] = _nt(drb_ref[rs, :], wdown_ref[0])
            for bi in blocks:
                rows1(bi)
            for bi in blocks:
                rows2(bi)

            @pl.when((i == 0) & (fo > 0))
            def _():
                flushed(f_prev)

            @pl.when(i == 0)
            def _():
                acc_wup[...] = jnp.zeros(acc_wup.shape, F32)
                acc_wdown[...] = jnp.zeros(acc_wdown.shape, F32)

            acc_wdown[...] += _tn(gbuf_ref[rs, :], drb_ref[rs, :])
            acc_wup[0] += _tn(dhug_ref[rs, :], x1b_ref[rs, :])
            acc_wup[1] += _tn(dhuv_ref[rs, :], x1b_ref[rs, :])
            dx1_ref[0, rs, :] = (_nt(dhug_ref[rs, :], wup_ref[0, 0])
                                 + _nt(dhuv_ref[rs, :], wup_ref[1, 0])).astype(BF16)
        dextg_ref[tm:tm + HALO_F, :] = dextg_ref[0:HALO_F, :]
        dextv_ref[tm:tm + HALO_F, :] = dextv_ref[0:HALO_F, :]

        @pl.when(i == nt - 1)
        def _():
            for g in range(2):
                dcfb_ref[g, 0] = jnp.sum(acc_cfb[8 * g:8 * g + 8, :], axis=0, keepdims=True)
                for k in range(KF):
                    r0 = 8 * (g * KF + k)
                    dcfw_ref[g, 0, k:k + 1, :] = jnp.sum(acc_cfw[r0:r0 + 8, :], axis=0, keepdims=True)
            for cp in flush(f):
                cp.start()

        @pl.when((i == nt - 1) & (fo == N_F - 1))
        def _():
            flushed(f)
            for fi in range(N_F):
                down_copy(fi).wait()
                for g in range(2):
                    @pl.when(fi % 2 != c)
                    def _():
                        up_copy(fi, g).wait_send()

                    @pl.when(fi % 2 == c)
                    def _():
                        up_copy(fi, g).wait_recv()

    rev = lambda i: nt - 1 - i
    row = pl.BlockSpec((tm, D), lambda fo, i, o: (rev(i), 0))
    pair = lambda r, c: pl.BlockSpec((2, 1, r, c), lambda fo, i, o: (0, o[fo], 0, 0))
    tile = pl.BlockSpec((2, 1, tm, W_UP_BLK), lambda fo, i, o: (0, o[fo], rev(i), 0))
    return pl.pallas_call(
        body, name="ffn_backward",
        grid_spec=pltpu.PrefetchScalarGridSpec(
            num_scalar_prefetch=1, grid=(N_F, nt),
            in_specs=[row, row, _full(ln1_g.shape), _full(ln1_b.shape), tile, tile,
                      pair(D, W_UP_BLK), pair(KF, W_UP_BLK),
                      pl.BlockSpec((1, W_UP_BLK, D), lambda fo, i, o: (o[fo], 0, 0))],
            out_specs=[ANY, ANY, pair(KF, W_UP_BLK), pair(1, W_UP_BLK),
                       pl.BlockSpec((1, tm, D), lambda fo, i, o: (o[fo], rev(i), 0)), ANY, ANY],
            scratch_shapes=[pltpu.VMEM((tm, D), BF16), pltpu.VMEM((tm, D), BF16),
                            pltpu.VMEM((tm, W_UP_BLK), F32),
                            pltpu.VMEM((tm + HALO_F, W_UP_BLK), F32), pltpu.VMEM((tm + HALO_F, W_UP_BLK), F32),
                            pltpu.VMEM((tm, W_UP_BLK), BF16), pltpu.VMEM((tm, W_UP_BLK), BF16),
                            pltpu.VMEM((tm, W_UP_BLK), BF16),
                            pltpu.VMEM((2, W_UP_BLK, D), F32), pltpu.VMEM((W_UP_BLK, D), F32),
                            pltpu.VMEM((2 * KF * 8, W_UP_BLK), F32), pltpu.VMEM((16, W_UP_BLK), F32),
                            pltpu.SemaphoreType.DMA((3,)),
                            pltpu.SemaphoreType.DMA((N_F, 3)), pltpu.SemaphoreType.DMA((N_F, 3))]),
        out_shape=[jax.ShapeDtypeStruct((2, N_F, W_UP_BLK, D), F32),
                   jax.ShapeDtypeStruct((N_F, W_UP_BLK, D), F32),
                   jax.ShapeDtypeStruct((2, N_F, KF, W_UP_BLK), F32),
                   jax.ShapeDtypeStruct((2, N_F, 1, W_UP_BLK), F32),
                   jax.ShapeDtypeStruct((N_F, t, D), BF16),
                   jax.ShapeDtypeStruct((4, W_UP_BLK, D), F32),
                   jax.ShapeDtypeStruct((4, D_FF // N_DEV, D), F32)],
        compiler_params=_params(("arbitrary", "arbitrary")),
    )(order, dr2, xhat1, ln1_g, ln1_b, hu4, gv4, wup4, cfw4, wdown)


def mix_backward(x, h, yb1, dx1p, dr2, xhat1, rstd1, win_g, ln_a_g, ln_a_b, w_spatial, bst,
                 conv_b_w, ln_b_g, ln_b_b, wout, ln1_g, ffn_partials, tm):
    t = x.shape[0]
    n_p = len(ffn_partials)
    nt = t // tm
    n_chunks = tm // CHUNK
    halo_blocks = tm // HALO_B

    def body(x_ref, h_ref, halo_ref, yb1_ref, dx1p_ref, dr2_ref, xh1_ref, rstd1_ref, win_ref, ga_ref, ba_ref,
             ws_ref, bst_ref, cw_ref, gb_ref, bb_ref, wout_ref, g1_ref, *rest):
        p_refs, rest = rest[:n_p], rest[n_p:]
        gx_ref, dwin_ref, dwout_ref, dcw_ref, small_ref = rest[:5]
        land_refs, rest = rest[5:5 + n_p], rest[5 + n_p:]
        (ext_ref, dext_ref, y_ref, dy_ref, dh_ref, dmb_ref, wsm_ref,
         acc_win, acc_wout, acc_bin, acc_lnag, acc_lnab, acc_ws, acc_bs, acc_cbb, acc_lnbg,
         acc_lnbb, acc_bout, acc_ln1g, acc_ln1b, acc_cw, sem, send_sems, recv_sems) = rest
        i = pl.program_id(0)

        @pl.when(i == 0)
        def _():
            for cp in _chip_copies(p_refs, land_refs, send_sems, recv_sems):
                cp.start()

        first_tile = i == nt - 1
        accs = [acc_win, acc_wout, acc_bin, acc_lnag, acc_lnab, acc_ws, acc_bs, acc_cbb, acc_lnbg,
                acc_lnbb, acc_bout, acc_ln1g, acc_ln1b, acc_cw]

        @pl.when(i == 0)
        def _():
            for acc in accs:
                acc[...] = jnp.zeros(acc.shape, F32)
            dext_ref[tm:tm + HALO_B, :] = jnp.zeros((HALO_B, D_B), F32)
            mask = _tril_mask()
            for hd in range(HEADS):
                wsm_ref[hd] = jnp.where(mask, ws_ref[hd], 0.0).astype(BF16)

        def ln1_rows(bi):
            r = _rows(bi, LN_ROWS)
            part = [dx1p_ref[f, r, :].astype(F32) for f in range(N_F)]
            dx1 = ALPHA * dr2_ref[r, :] + ((part[0] + part[1]) + (part[2] + part[3]))
            xhat = xh1_ref[r, :]
            acc_ln1g[...] += _rsum8(dx1 * xhat)
            acc_ln1b[...] += _rsum8(dx1)
            dr1 = _ln_bwd(dx1 * g1_ref[...], xhat, rstd1_ref[r, 0:1])
            acc_bout[...] += _rsum8(dr1)
            gx_ref[r, :] = ALPHA * dr1
            dmb_ref[r, :] = dr1.astype(BF16)

        _loop(tm // LN_ROWS, ln1_rows)
        dy_ref[...] = _nt(dmb_ref[...], wout_ref[...])

        ha = halo_ref[:, 0:D_B]
        hg = halo_ref[:, D_B:2 * D_B]
        ext_ref[0:HALO_B, :] = jnp.where(first_tile, 0.0, 1.0) * (ha * _sigmoid(hg))

        def chunk(ci):
            r = _rows(ci, CHUNK)
            for hd in range(HEADS):
                sl = slice(hd * HEAD_DIM, (hd + 1) * HEAD_DIM)
                rows8 = slice(8 * hd, 8 * hd + 8)
                hus, hvs, u, cdf_u, cdf_v, xhat, rstd, vn, sv = _mixer_a_head(
                    h_ref, r, hd, ga_ref, ba_ref, wsm_ref, bst_ref)
                dy_a = dy_ref[r, sl]
                y_ref[r, sl] = (u * sv).astype(BF16)
                du = dy_a * sv
                dsv = dy_a * u
                dsvb = dsv.astype(BF16)
                acc_bs[hd] += dsv
                acc_ws[hd] += _nt(dsvb, vn)
                dvn = _tn(wsm_ref[hd], dsvb)
                acc_lnag[rows8, :] += _rsum8(dvn * xhat)
                acc_lnab[rows8, :] += _rsum8(dvn)
                dv = _ln_bwd(dvn * ga_ref[hd:hd + 1, :], xhat, rstd)
                slv = slice(D_A + hd * HEAD_DIM, D_A + (hd + 1) * HEAD_DIM)
                dhu = du * (cdf_u + hus * jnp.exp(-0.5 * hus * hus) * INV_SQRT_2PI)
                dhv = dv * (cdf_v + hvs * jnp.exp(-0.5 * hvs * hvs) * INV_SQRT_2PI)
                acc_bin[:, sl] += _rsum8(dhu)
                acc_bin[:, slv] += _rsum8(dhv)
                dh_ref[r, sl] = dhu.astype(BF16)
                dh_ref[r, slv] = dhv.astype(BF16)
            a_b = h_ref[r, 2 * D_A:2 * D_A + D_B]
            g_b = h_ref[r, 2 * D_A + D_B:D_IN]
            ext_ref[pl.ds(HALO_B + ci * CHUNK, CHUNK), :] = a_b * _sigmoid(g_b)

        _loop(n_chunks, chunk)

        def conv_rows(bi):
            base = bi * ROWS
            r = pl.ds(base, ROWS)
            xhat, rstd = _ln_stats(yb1_ref[r, :])
            yb2 = xhat * gb_ref[...] + bb_ref[...]
            sg = _sigmoid(yb2)
            y_ref[r, D_A:D] = (yb2 * sg).astype(BF16)
            dyb2 = dy_ref[r, D_A:D] * (sg * (1.0 + yb2 * (1.0 - sg)))
            acc_lnbg[...] += _rsum8(dyb2 * xhat)
            acc_lnbb[...] += _rsum8(dyb2)
            dyb1 = _ln_bwd(dyb2 * gb_ref[...], xhat, rstd)
            acc_cbb[...] += _rsum8(dyb1)
            dext_ref[r, :] = dyb1
            for k, tap in _taps(ext_ref[pl.ds(base, ROWS + HALO_B), :], CONV_B_OFFSETS):
                acc_cw[8 * k:8 * k + 8, :] += _rsum8(dyb1 * tap)

        _loop(tm // ROWS, conv_rows)

        def convt_rows(bi):
            base = bi * ROWS
            r = pl.ds(base, ROWS)
            dyb0 = jnp.zeros((ROWS, D_B), F32)
            for k, tap in _taps(dext_ref[pl.ds(base, ROWS + HALO_B), :], CONV_B_T_OFFSETS):
                dyb0 = dyb0 + tap * cw_ref[k:k + 1, :]
            a_b = h_ref[r, 2 * D_A:2 * D_A + D_B]
            sg = _sigmoid(h_ref[r, 2 * D_A + D_B:D_IN])
            da_b = dyb0 * sg
            dg_b = dyb0 * a_b * sg * (1.0 - sg)
            acc_bin[:, 2 * D_A:2 * D_A + D_B] += _rsum8(da_b)
            acc_bin[:, 2 * D_A + D_B:D_IN] += _rsum8(dg_b)
            dh_ref[r, 2 * D_A:2 * D_A + D_B] = da_b.astype(BF16)
            dh_ref[r, 2 * D_A + D_B:D_IN] = dg_b.astype(BF16)

        _loop(tm // ROWS, convt_rows)
        dext_ref[tm:tm + HALO_B, :] = dext_ref[0:HALO_B, :]

        acc_wout[...] += _tn(y_ref[...], dmb_ref[...])
        xt = x_ref[...].T.astype(BF16)
        dh_blocks = [dh_ref[:, j * W_IN_BLK:(j + 1) * W_IN_BLK] for j in range(N_DEV)]
        for j in range(N_DEV):
            acc_win[j] += _nn(xt, dh_blocks[j])
        gx_ref[...] += sum(_nt(dh_blocks[j], win_ref[j]) for j in range(N_DEV))

        @pl.when(i == nt - 1)
        def _():
            cps = [pltpu.make_async_copy(acc_win, dwin_ref, sem.at[0]),
                   pltpu.make_async_copy(acc_wout, dwout_ref, sem.at[1])]
            for cp in cps:
                cp.start()
            small_ref[...] = jnp.zeros(small_ref.shape, F32)

            def put_row_vector(row0, acc):
                vec = jnp.sum(acc[...], axis=0, keepdims=True)
                for k in range(vec.shape[1] // 128):
                    small_ref[row0 + k:row0 + k + 1, :] = vec[:, k * 128:(k + 1) * 128]

            put_row_vector(S_BIN, acc_bin)
            put_row_vector(S_CBB, acc_cbb)
            put_row_vector(S_LNBG, acc_lnbg)
            put_row_vector(S_LNBB, acc_lnbb)
            put_row_vector(S_BOUT, acc_bout)
            put_row_vector(S_LN1G, acc_ln1g)
            put_row_vector(S_LN1B, acc_ln1b)
            mask = _tril_mask()
            for hd in range(HEADS):
                rows8 = slice(8 * hd, 8 * hd + 8)
                small_ref[S_LNAG + hd:S_LNAG + hd + 1, :] = jnp.sum(acc_lnag[rows8, :], axis=0, keepdims=True)
                small_ref[S_LNAB + hd:S_LNAB + hd + 1, :] = jnp.sum(acc_lnab[rows8, :], axis=0, keepdims=True)
                small_ref[S_WS + hd * CHUNK:S_WS + (hd + 1) * CHUNK, :] = jnp.where(mask, acc_ws[hd], 0.0)
                small_ref[S_BS + hd:S_BS + hd + 1, :] = jnp.sum(acc_bs[hd].T, axis=0, keepdims=True)
            for k in range(KB):
                dcw_ref[k:k + 1, :] = jnp.sum(acc_cw[8 * k:8 * k + 8, :], axis=0, keepdims=True)
            for cp in cps:
                cp.wait()
            for cp in _chip_copies(p_refs, land_refs, send_sems, recv_sems):
                cp.wait()

    rev = lambda i: nt - 1 - i
    row = lambda w: pl.BlockSpec((tm, w), lambda i: (rev(i), 0))
    return pl.pallas_call(
        body, name="mix_backward", grid=(nt,),
        in_specs=[row(D), row(D_IN),
                  pl.BlockSpec((HALO_B, 2 * D_B), lambda i: (jnp.maximum(rev(i) * halo_blocks - 1, 0), 1)),
                  row(D_B), pl.BlockSpec((N_F, tm, D), lambda i: (0, rev(i), 0)),
                  row(D), row(D), row(128), _resident(win_g.shape), _full(ln_a_g.shape),
                  _full(ln_a_b.shape), _full(w_spatial.shape), _full(bst.shape), _full(conv_b_w.shape),
                  _full(ln_b_g.shape), _full(ln_b_b.shape),
                  _resident(wout.shape), _full(ln1_g.shape)] + [ANY] * n_p,
        out_specs=[row(D), ANY, ANY, _full((KB, D_B)), _full((S_MIX_ROWS, 128))] + [ANY] * n_p,
        out_shape=[jax.ShapeDtypeStruct((t, D), F32), jax.ShapeDtypeStruct((N_DEV, D, W_IN_BLK), F32),
                   jax.ShapeDtypeStruct((D, D), F32), jax.ShapeDtypeStruct((KB, D_B), F32),
                   jax.ShapeDtypeStruct((S_MIX_ROWS, 128), F32)]
        + [jax.ShapeDtypeStruct(p.shape, BF16) for p in ffn_partials],
        scratch_shapes=[pltpu.VMEM((tm + HALO_B, D_B), F32), pltpu.VMEM((tm + HALO_B, D_B), F32),
                        pltpu.VMEM((tm, D), BF16), pltpu.VMEM((tm, D), F32), pltpu.VMEM((tm, D_IN), BF16),
                        pltpu.VMEM((tm, D), BF16),
                        pltpu.VMEM((HEADS, CHUNK, CHUNK), BF16),
                        pltpu.VMEM((N_DEV, D, W_IN_BLK), F32), pltpu.VMEM((D, D), F32),
                        pltpu.VMEM((8, D_IN), F32), pltpu.VMEM((8 * HEADS, HEAD_DIM), F32),
                        pltpu.VMEM((8 * HEADS, HEAD_DIM), F32), pltpu.VMEM((HEADS, CHUNK, CHUNK), F32),
                        pltpu.VMEM((HEADS, CHUNK, CHUNK), F32), pltpu.VMEM((8, D_B), F32),
                        pltpu.VMEM((8, D_B), F32), pltpu.VMEM((8, D_B), F32), pltpu.VMEM((8, D), F32),
                        pltpu.VMEM((8, D), F32), pltpu.VMEM((8, D), F32), pltpu.VMEM((8 * KB, D_B), F32),
                        pltpu.SemaphoreType.DMA((2,)),
                        pltpu.SemaphoreType.DMA((n_p, 3)), pltpu.SemaphoreType.DMA((n_p, 3))],
        compiler_params=_params(("arbitrary",)),
    )(x, h, h, yb1, dx1p, dr2, xhat1, rstd1, win_g, ln_a_g, ln_a_b, w_spatial, bst, conv_b_w,
      ln_b_g, ln_b_b, wout, ln1_g, *ffn_partials)


def _rows128(a):
    return a.reshape(-1, 128)


def _pack_conv(cb, cf):
    lead = cb.shape[:-2]
    pad = [(0, 0)] * len(lead)
    flat = jnp.pad(cb.reshape(lead + (KB * 64,)), pad + [(0, 3 * W_UP_BLK - KB * 64)])
    rows = jnp.concatenate([cf, flat.reshape(lead + (3, W_UP_BLK))], axis=-2)
    return jnp.pad(rows, pad + [(0, 2), (0, 768 - W_UP_BLK)])


def _unpack_conv(p):
    lead = p.shape[:-2]
    cf = p[..., 0:KF, 0:W_UP_BLK]
    cb = p[..., 3:6, 0:W_UP_BLK].reshape(lead + (3 * W_UP_BLK,))[..., :KB * 64].reshape(lead + (KB, 64))
    return cb, cf


def kernel(x, w_in, b_in, ln_a_g, ln_a_b, w_spatial, b_spatial, conv_b_w, conv_b_b, ln_b_g, ln_b_b, w_out, b_out, ln1_g, ln1_b, w_up, conv_f_w, conv_f_b, w_down, ln2_g, ln2_b, loss_target, m_w_in, m_b_in, m_ln_a_g, m_ln_a_b, m_w_spatial, m_b_spatial, m_conv_b_w, m_conv_b_b, m_ln_b_g, m_ln_b_b, m_w_out, m_b_out, m_ln1_g, m_ln1_b, m_w_up, m_conv_f_w, m_conv_f_b, m_w_down, m_ln2_g, m_ln2_b, v_w_in, v_b_in, v_ln_a_g, v_ln_a_b, v_w_spatial, v_b_spatial, v_conv_b_w, v_conv_b_b, v_ln_b_g, v_ln_b_b, v_w_out, v_b_out, v_ln1_g, v_ln1_b, v_w_up, v_conv_f_w, v_conv_f_b, v_w_down, v_ln2_g, v_ln2_b):
    t = x.shape[1]
    x2 = x.reshape(t, D)
    target = loss_target.reshape(t, D)
    tm_fwd = min(t, 512)
    tm_bwd = min(t, 256)
    tm_ffn_bwd = min(t, 512)

    xi, yi, ci = _mesh_pos()
    jidx = jnp.stack([_lid(px, py, ci) for px, py in _chip_patterns(xi, yi)]).astype(jnp.int32)

    sin, sout, sup, sdown, conv_g = prepare_weights(w_in, w_out, w_up.T, w_down, _pack_conv(conv_b_w, conv_f_w))
    conv_b_all, cfw = _unpack_conv(conv_g)
    conv_b_full = conv_b_all.transpose(1, 0, 2).reshape(KB, D_B)
    cfb = conv_f_b.reshape(N_DEV, W_UP_BLK)
    row = lambda a: a.reshape(1, -1)
    bst = b_spatial.T

    h, xhat1, rstd1, yb1, win_g, wout_g, wup_g, wdown_g = mix_forward(
        x2, sin, sout, row(b_in), ln_a_g, ln_a_b, w_spatial, bst, conv_b_full, row(conv_b_b),
        row(ln_b_g), row(ln_b_b), row(b_out), row(ln1_g), row(ln1_b), sup, sdown, tm_fwd)
    wout_full = wout_g.reshape(D, D)
    wdown4 = wdown_g.reshape(N_F, W_UP_BLK, D)
    hu, gv, dr2, loss_part, s_ln2 = ffn_forward(
        xhat1, row(ln1_g), row(ln1_b), wup_g, cfw, cfb, wdown4, row(ln2_g), row(ln2_b), target, tm_bwd)

    order = jnp.where(ci == 0, jnp.array([1, 3, 0, 2], jnp.int32), jnp.array([0, 2, 1, 3], jnp.int32))
    dwup, dwdown, dcfw, dcfb, dx1p, *ffn_lands = ffn_backward(
        order, dr2, xhat1, row(ln1_g), row(ln1_b), hu, gv, wup_g, cfw, wdown4, tm_ffn_bwd)
    ffn_grads = [dwup.reshape(N_DEV, W_UP_BLK, D), dwdown.reshape(N_DEV, D_FF // N_DEV, D)]
    ffn_partials = [chip_partials("chip_partials_" + nm, g, l, jidx, rb)
                    for nm, g, l, rb in zip(["w_up", "w_down"], ffn_grads, ffn_lands, [352, 352])]
    grad_x, dwin, dwout, dcw, s_mix, *ffn_recvs = mix_backward(
        x2, h, yb1, dx1p, dr2, xhat1, rstd1, win_g, ln_a_g, ln_a_b, w_spatial, bst,
        conv_b_full, row(ln_b_g), row(ln_b_b), wout_full, row(ln1_g), ffn_partials, tm_bwd)

    dcfb_rows = jnp.pad(dcfb.reshape(-1, 128), ((0, 4), (0, 0)))
    svec = jnp.concatenate([s_mix, dcfb_rows, s_ln2, loss_part], axis=0)
    dconv = _pack_conv(dcw.reshape(KB, N_DEV, 64).transpose(1, 0, 2), dcfw.reshape(N_DEV, KF, W_UP_BLK))
    mix_grads = [dwin, dwout.reshape(N_DEV, D // N_DEV, D), dconv]
    mix_w = [w_in, w_out, _pack_conv(conv_b_w, conv_f_w)]
    mix_m = [m_w_in, m_w_out, _pack_conv(m_conv_b_w, m_conv_f_w)]
    mix_v = [v_w_in, v_w_out, _pack_conv(v_conv_b_w, v_conv_f_w)]
    *mix_out, sv_slots = mixer_reduce_adamw(mix_grads, svec, mix_w, mix_m, mix_v)
    big = {nm: [mix_out[k * 3 + p] for k in range(4)] for p, nm in enumerate(["w_in", "w_out", "conv"])}

    ffn_w = [(w_up.T, m_w_up.T, v_w_up.T), (w_down, m_w_down, v_w_down)]
    for nm, g, l, r, (w, m, v) in zip(["w_up", "w_down"], ffn_grads, ffn_lands, ffn_recvs, ffn_w):
        big[nm] = reduce_and_adamw("reduce_adamw_" + nm, g, l, r, w, m, v, jidx, 352)
    big["w_up"] = [o.T for o in big["w_up"]]
    for k in range(4):
        cb_k, cf_k = _unpack_conv(big["conv"][k])
        big.setdefault("conv_b_w", []).append(cb_k)
        big.setdefault("conv_f_w", []).append(cf_k)

    small_w = dict(b_in=b_in, ln_a_g=ln_a_g, ln_a_b=ln_a_b, w_spatial=w_spatial, b_spatial=b_spatial,
                   conv_b_b=conv_b_b, ln_b_g=ln_b_g, ln_b_b=ln_b_b, b_out=b_out, ln1_g=ln1_g,
                   ln1_b=ln1_b, conv_f_b=conv_f_b, ln2_g=ln2_g, ln2_b=ln2_b)
    small_m = dict(b_in=m_b_in, ln_a_g=m_ln_a_g, ln_a_b=m_ln_a_b, w_spatial=m_w_spatial,
                   b_spatial=m_b_spatial, conv_b_b=m_conv_b_b, ln_b_g=m_ln_b_g, ln_b_b=m_ln_b_b,
                   b_out=m_b_out, ln1_g=m_ln1_g, ln1_b=m_ln1_b, conv_f_b=m_conv_f_b, ln2_g=m_ln2_g,
                   ln2_b=m_ln2_b)
    small_v = dict(b_in=v_b_in, ln_a_g=v_ln_a_g, ln_a_b=v_ln_a_b, w_spatial=v_w_spatial,
                   b_spatial=v_b_spatial, conv_b_b=v_conv_b_b, ln_b_g=v_ln_b_g, ln_b_b=v_ln_b_b,
                   b_out=v_b_out, ln1_g=v_ln1_g, ln1_b=v_ln1_b, conv_f_b=v_conv_f_b, ln2_g=v_ln2_g,
                   ln2_b=v_ln2_b)
    order = [nm for nm, _, _ in SMALL_LAYOUT]
    small_out = small_adamw(sv_slots, [_rows128(small_w[nm]) for nm in order],
                            [_rows128(small_m[nm]) for nm in order], [_rows128(small_v[nm]) for nm in order])
    n_small = len(order)
    small = {nm: [small_out[k * n_small + p].reshape(small_w[nm].shape) for k in range(4)]
             for p, nm in enumerate(order)}
    loss = jnp.sum(small_out[4 * n_small]) * (0.5 / D)

    weights = ["w_in", "b_in", "ln_a_g", "ln_a_b", "w_spatial", "b_spatial", "conv_b_w", "conv_b_b",
               "ln_b_g", "ln_b_b", "w_out", "b_out", "ln1_g", "ln1_b", "w_up", "conv_f_w", "conv_f_b",
               "w_down", "ln2_g", "ln2_b"]
    result = lambda nm, k: big[nm][k] if nm in big else small[nm][k]
    return (loss, grad_x.reshape(x.shape), *[result(nm, 0) for nm in weights],
            *[result(nm, 1) for nm in weights], *[result(nm, 2) for nm in weights],
            *[result(nm, 3) for nm in weights])
```

```python
import functools
import math

import jax
import jax.numpy as jnp
from jax import lax
from jax.experimental import pallas as pl
from jax.experimental.pallas import tpu as pltpu

F32 = jnp.float32
BF16 = jnp.bfloat16

D = 1024
D_A = 512
D_B = 512
HEADS = 4
HEAD_DIM = 128
CHUNK = 128
KB = 31
KF = 3
D_FF = 2816
D_IN = 2048
N_DEV = 8
W_IN_BLK = D_IN // N_DEV
W_UP_BLK = 2 * D_FF // N_DEV
N_F = 4
LN_EPS = 1e-5
ALPHA = 2.0 ** 0.25

ADAM_LR = 0.001
ADAM_B1 = 0.9
ADAM_B2 = 0.999
ADAM_EPS = 1e-08
ADAM_WD = 0.01
ADAM_STEP = 10

INV_SQRT2 = 1.0 / math.sqrt(2.0)
INV_SQRT_2PI = 1.0 / math.sqrt(2.0 * math.pi)

HALO_B = 32
HALO_F = 8
ROWS = 64
LN_ROWS = 32
VMEM_LIMIT = 58 * 1024 * 1024

MESH = pl.DeviceIdType.MESH
ANY = pl.BlockSpec(memory_space=pl.ANY)
VMEM = pl.BlockSpec(memory_space=pltpu.VMEM)

S_BIN, S_LNAG, S_LNAB, S_WS, S_BS, S_CBB, S_LNBG, S_LNBB, S_BOUT, S_LN1G, S_LN1B = (
    0, 16, 24, 32, 544, 552, 560, 568, 576, 584, 592)
S_MIX_ROWS = 600
S_CFB = 600
S_LN2G = 648
S_LN2B = 656
S_LOSS = 664
S_ROWS = 672


def _tn(a, b):
    return lax.dot_general(a, b, (((0,), (0,)), ((), ())), preferred_element_type=F32)


def _nt(a, b):
    return lax.dot_general(a, b, (((1,), (1,)), ((), ())), preferred_element_type=F32)


def _nn(a, b):
    return jnp.dot(a, b, preferred_element_type=F32)


def _sigmoid(x):
    return 1.0 / (1.0 + jnp.exp(-x))


def _ln_stats(x):
    mu = jnp.mean(x, axis=-1, keepdims=True)
    xc = x - mu
    var = jnp.mean(xc * xc, axis=-1, keepdims=True)
    rstd = lax.rsqrt(var + LN_EPS)
    return xc * rstd, rstd


def _ln_bwd(dxhat, xhat, rstd):
    m1 = jnp.mean(dxhat, axis=-1, keepdims=True)
    m2 = jnp.mean(dxhat * xhat, axis=-1, keepdims=True)
    return rstd * (dxhat - m1 - xhat * m2)


def _rsum8(x):
    r, n = x.shape
    return x.reshape(r // 8, 8, n).sum(axis=0)


def _rows(i, n=ROWS):
    return pl.ds(i * n, n)


def _loop(n, body):
    for i in range(n):
        body(i)


def _tril_mask():
    r = lax.broadcasted_iota(jnp.int32, (CHUNK, CHUNK), 0)
    c = lax.broadcasted_iota(jnp.int32, (CHUNK, CHUNK), 1)
    return c <= r


def _mixer_a_head(h_ref, r, hd, ga_ref, ba_ref, wsm_ref, bst_ref):
    sl = slice(hd * HEAD_DIM, (hd + 1) * HEAD_DIM)
    hu = h_ref[r, sl]
    hv = h_ref[r, D_A + hd * HEAD_DIM:D_A + (hd + 1) * HEAD_DIM]
    cdf_u = 0.5 * (1.0 + lax.erf(hu * INV_SQRT2))
    cdf_v = 0.5 * (1.0 + lax.erf(hv * INV_SQRT2))
    u = hu * cdf_u
    xhat, rstd = _ln_stats(hv * cdf_v)
    vn = (xhat * ga_ref[hd:hd + 1, :] + ba_ref[hd:hd + 1, :]).astype(BF16)
    sv = _nn(wsm_ref[hd], vn) + bst_ref[:, hd:hd + 1]
    return hu, hv, u, cdf_u, cdf_v, xhat, rstd, vn, sv


def _taps(win, offsets):
    n = win.shape[0]
    for s in range(8):
        ks = [k for k, o in enumerate(offsets) if o % 8 == s]
        if ks:
            moved = win if s == 0 else pltpu.roll(win, n - s, 0)
            for k in ks:
                yield k, moved[offsets[k] - s:offsets[k] - s + ROWS, :]


CONV_B_OFFSETS = [2 + k for k in range(KB)]
CONV_B_T_OFFSETS = [30 - k for k in range(KB)]


def _conv_b_block(ext_ref, base, cw_ref):
    acc = jnp.zeros((ROWS, D_B), F32)
    for k, tap in _taps(ext_ref[pl.ds(base, ROWS + HALO_B), :], CONV_B_OFFSETS):
        acc = acc + tap * cw_ref[k:k + 1, :]
    return acc


def _taps_f(win):
    n = ROWS + HALO_F
    return [pltpu.roll(win, n - 6, 0)[0:ROWS, :], pltpu.roll(win, n - 7, 0)[0:ROWS, :], win[8:n, :]]


def _params(sem, **kw):
    return pltpu.CompilerParams(dimension_semantics=sem, vmem_limit_bytes=VMEM_LIMIT, **kw)


def _resident(shape):
    zeros = (0,) * len(shape)
    return pl.BlockSpec(shape, lambda *_: zeros, pipeline_mode=pl.Buffered(1))


def _full(shape):
    zeros = (0,) * len(shape)
    return pl.BlockSpec(shape, lambda *_: zeros)


def _mesh_pos():
    return lax.axis_index("x"), lax.axis_index("y"), lax.axis_index("c")


def _chip_patterns(x, y):
    return [(x, y), (1 - x, y), (x, 1 - y), (1 - x, 1 - y)]


def _lid(x, y, c):
    return 4 * x + 2 * y + c


def _gather_copy(outs, send_sems, recv_sems, a, k, block, to, src=None):
    blk = outs[a].at[_lid(*block)]
    return pltpu.make_async_remote_copy(
        src_ref=blk if src is None else src, dst_ref=blk,
        send_sem=send_sems.at[a, k], recv_sem=recv_sems.at[a, k], device_id=to, device_id_type=MESH)


def _gather_start(mine, outs, send_sems, recv_sems, local_sems):
    x, y, c = _mesh_pos()
    me = (x, y, c)
    for a in range(len(mine)):
        pltpu.make_async_copy(mine[a], outs[a].at[_lid(*me)], local_sems.at[a]).start()
        for k, to in enumerate([(x, y, 1 - c), (1 - x, y, c), (x, 1 - y, c)]):
            _gather_copy(outs, send_sems, recv_sems, a, k, me, to, src=mine[a]).start()


def _gather_relay(mine, outs, send_sems, recv_sems, local_sems, via):
    x, y, c = _mesh_pos()
    me, sib = (x, y, c), (x, y, 1 - c)
    copy = functools.partial(_gather_copy, outs, send_sems, recv_sems)
    source = {1: (1 - x, y, c), 2: (x, 1 - y, c)}
    for a in range(len(mine)):
        for k in (via[a], 3 - via[a]):
            copy(a, k, source[k], me).wait_recv()
            if k == via[a]:
                copy(a, 3, source[k], source[3 - k]).start()
            copy(a, 3 + k, source[k], sib).start()


def _gather_finish(mine, outs, send_sems, recv_sems, local_sems):
    x, y, c = _mesh_pos()
    me, sib = (x, y, c), (x, y, 1 - c)
    copy = functools.partial(_gather_copy, outs, send_sems, recv_sems)
    diag = (1 - x, 1 - y)
    n = len(mine)
    for a in range(n):
        copy(a, 3, (*diag, c), me).wait_recv()
        copy(a, 6, (*diag, c), sib).start()
    for a in range(n):
        copy(a, 0, sib, me).wait_recv()
        for k, chip in zip((4, 5, 6), [(1 - x, y), (x, 1 - y), diag]):
            copy(a, k, (*chip, 1 - c), me).wait_recv()
        for k in range(7):
            copy(a, k, me, sib, src=mine[a]).wait_send()
        pltpu.make_async_copy(mine[a], outs[a].at[_lid(*me)], local_sems.at[a]).wait()


def _gather_scratch(n):
    return [pltpu.SemaphoreType.DMA((n, 7)), pltpu.SemaphoreType.DMA((n, 7)), pltpu.SemaphoreType.DMA((n,))]


def prepare_weights(w_in, w_out, w_up_t, w_down, convp):
    def body(win_ref, wout_ref, wup_ref, wdown_ref, convp_ref,
             sin_ref, sout_ref, sup_ref, sdown_ref, gconv_ref, send_sems, recv_sems, local_sems):
        gather = ([convp_ref], [gconv_ref], send_sems, recv_sems, local_sems)
        _gather_start(*gather)
        sin_ref[...] = win_ref[...].astype(BF16)
        sout_ref[...] = wout_ref[...].astype(BF16)
        sup_ref[...] = wup_ref[...].T.astype(BF16)
        sdown_ref[...] = wdown_ref[...].astype(BF16)
        _gather_relay(*gather, via=[1])
        _gather_finish(*gather)

    return pl.pallas_call(
        body, name="prepare_weights",
        out_shape=[jax.ShapeDtypeStruct(w_in.shape, BF16), jax.ShapeDtypeStruct(w_out.shape, BF16),
                   jax.ShapeDtypeStruct(w_up_t.shape[::-1], BF16), jax.ShapeDtypeStruct(w_down.shape, BF16),
                   jax.ShapeDtypeStruct((N_DEV,) + convp.shape, F32)],
        in_specs=[VMEM] * 5, out_specs=[VMEM] * 4 + [ANY],
        scratch_shapes=_gather_scratch(1),
        compiler_params=pltpu.CompilerParams(vmem_limit_bytes=VMEM_LIMIT),
    )(w_in, w_out, w_up_t, w_down, convp)


def _chip_copies(p, land, send_sems, recv_sems):
    x, y, c = _mesh_pos()
    return [pltpu.make_async_remote_copy(
        src_ref=p[a].at[k], dst_ref=land[a].at[k], send_sem=send_sems.at[a, k], recv_sem=recv_sems.at[a, k],
        device_id=(px, py, c), device_id_type=MESH)
        for k, (px, py) in enumerate(_chip_patterns(x, y)[1:]) for a in range(len(p))]


def chip_partials(name, g, land, jidx, rb):
    _, r, c = g.shape

    def body(j_ref, g_ref, l_ref, o_ref):
        o_ref[...] = (g_ref[...] + l_ref[...]).astype(BF16)

    return pl.pallas_call(
        body, name=name,
        out_shape=jax.ShapeDtypeStruct((3, r, c), BF16),
        grid_spec=pltpu.PrefetchScalarGridSpec(
            num_scalar_prefetch=1, grid=(3, r // rb),
            in_specs=[pl.BlockSpec((1, rb, c), lambda k, i, j: (j[1 + k], i, 0)),
                      pl.BlockSpec((1, rb, c), lambda k, i, j: (1 + k, i, 0))],
            out_specs=pl.BlockSpec((1, rb, c), lambda k, i, j: (k, i, 0))),
        compiler_params=_params(("arbitrary", "arbitrary")),
    )(jidx, g, land)


def _adamw(w, g, m, v):
    m2 = ADAM_B1 * m + (1.0 - ADAM_B1) * g
    v2 = ADAM_B2 * v + (1.0 - ADAM_B2) * (g * g)
    m_hat = m2 / (1.0 - ADAM_B1 ** ADAM_STEP)
    v_hat = v2 / (1.0 - ADAM_B2 ** ADAM_STEP)
    delta = -ADAM_LR * (m_hat / (jnp.sqrt(v_hat) + ADAM_EPS) + ADAM_WD * w)
    return delta, m2, v2


def reduce_and_adamw(name, g, land, recv, w, m, v, jidx, rb):
    _, r, c = g.shape

    def body(j_ref, g_ref, l_ref, r_ref, w_ref, m_ref, v_ref, go_ref, do_ref, mo_ref, vo_ref):
        grad = (g_ref[0] + l_ref[0]) + r_ref[0].astype(F32) + r_ref[1].astype(F32) + r_ref[2].astype(F32)
        delta, m2, v2 = _adamw(w_ref[...], grad, m_ref[...], v_ref[...])
        go_ref[...] = grad
        do_ref[...] = delta
        mo_ref[...] = m2
        vo_ref[...] = v2

    blk = pl.BlockSpec((rb, c), lambda i, j: (i, 0))
    return pl.pallas_call(
        body, name=name,
        out_shape=[jax.ShapeDtypeStruct((r, c), F32)] * 4,
        grid_spec=pltpu.PrefetchScalarGridSpec(
            num_scalar_prefetch=1, grid=(r // rb,),
            in_specs=[pl.BlockSpec((1, rb, c), lambda i, j: (j[0], i, 0)),
                      pl.BlockSpec((1, rb, c), lambda i, j: (0, i, 0)),
                      pl.BlockSpec((3, rb, c), lambda i, j: (0, i, 0)),
                      blk, blk, blk],
            out_specs=[blk] * 4),
        compiler_params=_params(("arbitrary",)),
    )(jidx, g, land, recv, w, m, v)


def mixer_reduce_adamw(grads, svec, ws, ms, vs):
    n = len(grads)
    shard = [g.shape[1:] for g in grads]

    def body(*refs):
        g = refs[:n]
        sv_ref = refs[n]
        w, m, v = refs[n + 1:2 * n + 1], refs[2 * n + 1:3 * n + 1], refs[3 * n + 1:4 * n + 1]
        outs = refs[4 * n + 1:8 * n + 1]
        sv_slots = refs[8 * n + 1]
        rest = refs[8 * n + 2:]
        own, land, sendb, recvb = rest[:n], rest[n:2 * n], rest[2 * n:3 * n], rest[3 * n:4 * n]
        sv_land, chip_sv, d2d_send, d2d_recv, ici_send, ici_recv, local_sems, sv_sems = rest[4 * n:]
        x, y, c = _mesh_pos()
        sib = (x, y, 1 - c)
        pats = _chip_patterns(x, y)
        q = 2 * x + y

        d2d, local = [], []
        for a in range(n):
            for k, (px, py) in enumerate(pats):
                d2d.append(pltpu.make_async_remote_copy(
                    src_ref=g[a].at[_lid(px, py, 1 - c)], dst_ref=land[a].at[k],
                    send_sem=d2d_send.at[a, k], recv_sem=d2d_recv.at[a, k], device_id=sib, device_id_type=MESH))
                local.append(pltpu.make_async_copy(g[a].at[_lid(px, py, c)], own[a].at[k], local_sems.at[a, k]))
        d2d.append(pltpu.make_async_remote_copy(
            src_ref=sv_ref, dst_ref=sv_land, send_sem=d2d_send.at[n, 0], recv_sem=d2d_recv.at[n, 0],
            device_id=sib, device_id_type=MESH))
        for cp in d2d + local:
            cp.start()
        for cp in local + d2d:
            cp.wait()

        for a in range(n):
            for k in range(3):
                sendb[a][k] = (own[a][1 + k] + land[a][1 + k]).astype(BF16)
        chip_sv[...] = sv_ref[...] + sv_land[...]
        ici = _chip_copies(sendb, recvb, ici_send, ici_recv)
        sv_local = pltpu.make_async_copy(chip_sv, sv_slots.at[q], sv_sems.at[0])
        sv_out = [pltpu.make_async_remote_copy(
            src_ref=chip_sv, dst_ref=sv_slots.at[q], send_sem=sv_sems.at[1 + k], recv_sem=sv_sems.at[4 + k],
            device_id=(px, py, c), device_id_type=MESH) for k, (px, py) in enumerate(pats[1:])]
        for cp in ici + sv_out + [sv_local]:
            cp.start()
        for cp in ici:
            cp.wait()
        for k, (px, py) in enumerate(pats[1:]):
            sv_out[k].wait_send()
            pltpu.make_async_remote_copy(
                src_ref=chip_sv, dst_ref=sv_slots.at[2 * px + py], send_sem=sv_sems.at[1 + k],
                recv_sem=sv_sems.at[4 + k], device_id=(px, py, c), device_id_type=MESH).wait_recv()
        sv_local.wait()

        for a in range(n):
            grad = ((own[a][0] + land[a][0]) + recvb[a][0].astype(F32) + recvb[a][1].astype(F32)
                    + recvb[a][2].astype(F32))
            delta, m2, v2 = _adamw(w[a][...], grad, m[a][...], v[a][...])
            outs[a][...] = grad
            outs[n + a][...] = delta
            outs[2 * n + a][...] = m2
            outs[3 * n + a][...] = v2

    shard_out = [jax.ShapeDtypeStruct(s, F32) for s in shard]
    return pl.pallas_call(
        body, name="mixer_reduce_adamw",
        out_shape=shard_out * 4 + [jax.ShapeDtypeStruct((4,) + svec.shape, F32)],
        in_specs=[ANY] * n + [VMEM] * (1 + 3 * n), out_specs=[VMEM] * (4 * n) + [ANY],
        scratch_shapes=[pltpu.VMEM((4,) + s, F32) for s in shard] + [pltpu.VMEM((4,) + s, F32) for s in shard]
        + [pltpu.VMEM((3,) + s, BF16) for s in shard] + [pltpu.VMEM((3,) + s, BF16) for s in shard]
        + [pltpu.VMEM(svec.shape, F32), pltpu.VMEM(svec.shape, F32),
           pltpu.SemaphoreType.DMA((n + 1, 4)), pltpu.SemaphoreType.DMA((n + 1, 4)),
           pltpu.SemaphoreType.DMA((n, 3)), pltpu.SemaphoreType.DMA((n, 3)),
           pltpu.SemaphoreType.DMA((n, 4)), pltpu.SemaphoreType.DMA((7,))],
        compiler_params=pltpu.CompilerParams(vmem_limit_bytes=VMEM_LIMIT),
    )(*grads, svec, *ws, *ms, *vs)


SMALL_LAYOUT = [
    ("b_in", S_BIN, 16), ("ln_a_g", S_LNAG, 4), ("ln_a_b", S_LNAB, 4), ("w_spatial", S_WS, 512),
    ("b_spatial", S_BS, 4), ("conv_b_b", S_CBB, 4), ("ln_b_g", S_LNBG, 4), ("ln_b_b", S_LNBB, 4),
    ("b_out", S_BOUT, 8), ("ln1_g", S_LN1G, 8), ("ln1_b", S_LN1B, 8), ("conv_f_b", S_CFB, 44),
    ("ln2_g", S_LN2G, 8), ("ln2_b", S_LN2B, 8),
]


def small_adamw(sv_slots, ws, ms, vs):
    n = len(SMALL_LAYOUT)

    def body(*refs):
        s_ref = refs[0]
        w_refs, m_refs, v_refs = refs[1:1 + n], refs[1 + n:1 + 2 * n], refs[1 + 2 * n:1 + 3 * n]
        outs = refs[1 + 3 * n:]
        for p, (_, row0, rows) in enumerate(SMALL_LAYOUT):
            sl = pl.ds(row0, rows)
            grad = ((s_ref[0, sl, :] + s_ref[1, sl, :]) + s_ref[2, sl, :]) + s_ref[3, sl, :]
            delta, m2, v2 = _adamw(w_refs[p][...], grad, m_refs[p][...], v_refs[p][...])
            outs[p][...] = grad
            outs[n + p][...] = delta
            outs[2 * n + p][...] = m2
            outs[3 * n + p][...] = v2
        sl = pl.ds(S_LOSS, 8)
        outs[4 * n][...] = ((s_ref[0, sl, :] + s_ref[1, sl, :]) + s_ref[2, sl, :]) + s_ref[3, sl, :]

    shapes = [jax.ShapeDtypeStruct((rows, 128), F32) for _, _, rows in SMALL_LAYOUT]
    return pl.pallas_call(
        body, name="small_adamw", out_shape=shapes * 4 + [jax.ShapeDtypeStruct((8, 128), F32)],
        in_specs=[VMEM] * (1 + 3 * n), out_specs=[VMEM] * (4 * n + 1),
    )(sv_slots, *ws, *ms, *vs)


def mix_forward(x, sin, sout, b_in, ln_a_g, ln_a_b, w_spatial, bst, conv_b_w, conv_b_b, ln_b_g, ln_b_b,
                b_out, ln1_g, ln1_b, sup, sdown, tm):
    t = x.shape[0]
    nt = t // tm
    n_chunks = tm // CHUNK

    def body(x_ref, sin_ref, sout_ref, bin_ref, ga_ref, ba_ref, ws_ref, bst_ref, cw_ref, cb_ref, gb_ref,
             bb_ref, bout_ref, g1_ref, b1_ref, sup_ref, sdown_ref,
             h_ref, xhat1_ref, rstd1_ref, yb1_ref, gin_ref, gout_ref, gup_ref, gdown_ref,
             ext_ref, y_ref, wsm_ref, win_ref, wout_ref, load_sems,
             mix_send, mix_recv, mix_local, send_sems, recv_sems, local_sems):
        i = pl.program_id(0)
        mixer = ([sin_ref, sout_ref], [gin_ref, gout_ref], mix_send, mix_recv, mix_local)
        gather = ([sup_ref, sdown_ref], [gup_ref, gdown_ref], send_sems, recv_sems, local_sems)

        @pl.when(i == 0)
        def _():
            _gather_start(*mixer)
            _gather_relay(*mixer, via=[2, 2])
            _gather_finish(*mixer)
            _gather_start(*gather)
            loads = [pltpu.make_async_copy(gin_ref, win_ref, load_sems.at[0]),
                     pltpu.make_async_copy(gout_ref, wout_ref, load_sems.at[1])]
            for cp in loads:
                cp.start()
            for cp in loads:
                cp.wait()
            ext_ref[0:HALO_B, :] = jnp.zeros((HALO_B, D_B), F32)
            mask = _tril_mask()
            for hd in range(HEADS):
                wsm_ref[hd] = jnp.where(mask, ws_ref[hd], 0.0).astype(BF16)

        xb = x_ref[...].astype(BF16)
        for j in range(N_DEV):
            cols = slice(j * W_IN_BLK, (j + 1) * W_IN_BLK)
            h_ref[:, cols] = _nn(xb, win_ref[j]) + bin_ref[:, cols]

        def chunk(ci):
            r = _rows(ci, CHUNK)
            for hd in range(HEADS):
                _, _, u, _, _, _, _, _, sv = _mixer_a_head(h_ref, r, hd, ga_ref, ba_ref, wsm_ref, bst_ref)
                y_ref[r, hd * HEAD_DIM:(hd + 1) * HEAD_DIM] = (u * sv).astype(BF16)
            a_b = h_ref[r, 2 * D_A:2 * D_A + D_B]
            g_b = h_ref[r, 2 * D_A + D_B:D_IN]
            ext_ref[pl.ds(HALO_B + ci * CHUNK, CHUNK), :] = a_b * _sigmoid(g_b)

        _loop(n_chunks, chunk)

        def conv_rows(bi):
            base = bi * ROWS
            yb1 = _conv_b_block(ext_ref, base, cw_ref) + cb_ref[...]
            yb1_ref[pl.ds(base, ROWS), :] = yb1
            xhat, _ = _ln_stats(yb1)
            yb2 = xhat * gb_ref[...] + bb_ref[...]
            y_ref[pl.ds(base, ROWS), D_A:D] = (yb2 * _sigmoid(yb2)).astype(BF16)

        _loop(tm // ROWS, conv_rows)
        ext_ref[0:HALO_B, :] = ext_ref[tm:tm + HALO_B, :]

        mix = _nn(y_ref[...], wout_ref[...].reshape(D, D)) + bout_ref[...]
        xhat1, rstd1 = _ln_stats(ALPHA * x_ref[...] + mix)
        xhat1_ref[...] = xhat1
        rstd1_ref[...] = jnp.broadcast_to(rstd1, (tm, 128))

        @pl.when(i == (5 * nt) // 8)
        def _():
            _gather_relay(*gather, via=[1, 2])

        @pl.when(i == nt - 1)
        def _():
            _gather_finish(*gather)

    row = lambda w: pl.BlockSpec((tm, w), lambda i: (i, 0))
    return pl.pallas_call(
        body, name="mix_forward", grid=(nt,),
        in_specs=[row(D), ANY, ANY, _full(b_in.shape), _full(ln_a_g.shape),
                  _full(ln_a_b.shape), _full(w_spatial.shape), _full(bst.shape),
                  _full(conv_b_w.shape), _full(conv_b_b.shape), _full(ln_b_g.shape),
                  _full(ln_b_b.shape), _full(b_out.shape),
                  _full(ln1_g.shape), _full(ln1_b.shape), ANY, ANY],
        out_specs=[row(D_IN), row(D), row(128), row(D_B), ANY, ANY, ANY, ANY],
        out_shape=[jax.ShapeDtypeStruct((t, D_IN), F32), jax.ShapeDtypeStruct((t, D), F32),
                   jax.ShapeDtypeStruct((t, 128), F32), jax.ShapeDtypeStruct((t, D_B), F32)]
        + [jax.ShapeDtypeStruct((N_DEV,) + sh.shape, BF16) for sh in (sin, sout, sup, sdown)],
        scratch_shapes=[pltpu.VMEM((tm + HALO_B, D_B), F32), pltpu.VMEM((tm, D), BF16),
                        pltpu.VMEM((HEADS, CHUNK, CHUNK), BF16),
                        pltpu.VMEM((N_DEV,) + sin.shape, BF16), pltpu.VMEM((N_DEV,) + sout.shape, BF16),
                        pltpu.SemaphoreType.DMA((2,))] + _gather_scratch(2) + _gather_scratch(2),
        compiler_params=_params(("arbitrary",)),
    )(x, sin, sout, b_in, ln_a_g, ln_a_b, w_spatial, bst, conv_b_w, conv_b_b, ln_b_g, ln_b_b,
      b_out, ln1_g, ln1_b, sup, sdown)


def ffn_forward(xhat1, ln1_g, ln1_b, wup_g, cfw, cfb, wdown, ln2_g, ln2_b, target, tm):
    t = xhat1.shape[0]
    nt = t // tm

    def body(xh_ref, g1_ref, b1_ref, wup_ref, cfw_ref, cfb_ref, wdown_ref, g2_ref, b2_ref, tgt_ref,
             hu_ref, gv_ref, dr2_ref, loss_ref, sln2_ref,
             x1_ref, x1b_ref, hu32_ref, carry_ref, gbuf_ref, ffn_ref, acc_loss, acc_g2, acc_b2):
        i = pl.program_id(0)

        @pl.when(i == 0)
        def _():
            carry_ref[...] = jnp.zeros(carry_ref.shape, F32)
            acc_loss[...] = jnp.zeros(acc_loss.shape, F32)
            acc_g2[...] = jnp.zeros(acc_g2.shape, F32)
            acc_b2[...] = jnp.zeros(acc_b2.shape, F32)

        x1 = xh_ref[...] * g1_ref[...] + b1_ref[...]
        x1_ref[...] = x1
        x1b_ref[...] = x1.astype(BF16)

        def conv(g, j, base):
            if base == 0:
                win = jnp.concatenate([carry_ref[j], hu32_ref[g, 0:ROWS, :]], axis=0)
            else:
                win = hu32_ref[g, base - HALO_F:base + ROWS, :]
            taps = _taps_f(win)
            w = cfw_ref[j]
            return sum(taps[k] * w[k:k + 1, :] for k in range(KF)) + cfb_ref[j:j + 1, :]

        for f in range(N_F):
            hu32_ref[0] = _nn(x1b_ref[...], wup_ref[f])
            hu32_ref[1] = _nn(x1b_ref[...], wup_ref[N_F + f])

            def rows(bi, f=f):
                r = _rows(bi)
                gate = conv(0, f, bi * ROWS)
                val = conv(1, N_F + f, bi * ROWS)
                gbuf_ref[r, :] = (gate * _sigmoid(gate) * val).astype(BF16)
                gv_ref[f, r, :] = gate.astype(BF16)
                gv_ref[N_F + f, r, :] = val.astype(BF16)
                hu_ref[f, r, :] = hu32_ref[0, r, :].astype(BF16)
                hu_ref[N_F + f, r, :] = hu32_ref[1, r, :].astype(BF16)

            _loop(tm // ROWS, rows)
            carry_ref[f] = hu32_ref[0, tm - HALO_F:tm, :]
            carry_ref[N_F + f] = hu32_ref[1, tm - HALO_F:tm, :]
            part = _nn(gbuf_ref[...], wdown_ref[f])
            if f == 0:
                ffn_ref[...] = part
            else:
                ffn_ref[...] += part

        def tail(bi):
            r = _rows(bi, LN_ROWS)
            xhat2, rstd2 = _ln_stats(ALPHA * x1_ref[r, :] + ffn_ref[r, :])
            err = xhat2 * g2_ref[...] + b2_ref[...] - tgt_ref[r, :]
            e2 = _rsum8(err * err)
            acc_loss[...] += sum(e2[:, k * 128:(k + 1) * 128] for k in range(D // 128))
            dy = err * (1.0 / D)
            acc_g2[...] += _rsum8(dy * xhat2)
            acc_b2[...] += _rsum8(dy)
            dr2_ref[r, :] = _ln_bwd(dy * g2_ref[...], xhat2, rstd2)

        _loop(tm // LN_ROWS, tail)
        loss_ref[...] = acc_loss[...]

        @pl.when(i == nt - 1)
        def _():
            dg = jnp.sum(acc_g2[...], axis=0, keepdims=True)
            db = jnp.sum(acc_b2[...], axis=0, keepdims=True)
            for k in range(D // 128):
                sln2_ref[k:k + 1, :] = dg[:, k * 128:(k + 1) * 128]
                sln2_ref[8 + k:9 + k, :] = db[:, k * 128:(k + 1) * 128]

    row = pl.BlockSpec((tm, D), lambda i: (i, 0))
    return pl.pallas_call(
        body, name="ffn_forward", grid=(nt,),
        in_specs=[row, _full(ln1_g.shape), _full(ln1_b.shape), _resident(wup_g.shape),
                  _full(cfw.shape), _full(cfb.shape), _resident(wdown.shape),
                  _full(ln2_g.shape), _full(ln2_b.shape), row],
        out_specs=[pl.BlockSpec((N_DEV, tm, W_UP_BLK), lambda i: (0, i, 0)),
                   pl.BlockSpec((N_DEV, tm, W_UP_BLK), lambda i: (0, i, 0)), row,
                   _full((8, 128)), _full((16, 128))],
        out_shape=[jax.ShapeDtypeStruct((N_DEV, t, W_UP_BLK), BF16),
                   jax.ShapeDtypeStruct((N_DEV, t, W_UP_BLK), BF16), jax.ShapeDtypeStruct((t, D), F32),
                   jax.ShapeDtypeStruct((8, 128), F32), jax.ShapeDtypeStruct((16, 128), F32)],
        scratch_shapes=[pltpu.VMEM((tm, D), F32), pltpu.VMEM((tm, D), BF16),
                        pltpu.VMEM((2, tm, W_UP_BLK), F32),
                        pltpu.VMEM((N_DEV, HALO_F, W_UP_BLK), F32), pltpu.VMEM((tm, W_UP_BLK), BF16),
                        pltpu.VMEM((tm, D), F32), pltpu.VMEM((8, 128), F32),
                        pltpu.VMEM((8, D), F32), pltpu.VMEM((8, D), F32)],
        compiler_params=_params(("arbitrary",)),
    )(xhat1, ln1_g, ln1_b, wup_g, cfw, cfb, wdown, ln2_g, ln2_b, target)


def ffn_backward(order, dr2, xhat1, ln1_g, ln1_b, hu, gv, wup_g, cfw, wdown, tm):
    t = dr2.shape[0]
    nt = t // tm
    sub_rows = tm
    hu4 = hu.reshape(2, N_F, t, W_UP_BLK)
    gv4 = gv.reshape(2, N_F, t, W_UP_BLK)
    wup4 = wup_g.reshape(2, N_F, D, W_UP_BLK)
    cfw4 = cfw.reshape(2, N_F, KF, W_UP_BLK)

    def body(order_ref, dr2_ref, xh_ref, g1_ref, b1_ref, hu_ref, gv_ref, wup_ref, cfw_ref, wdown_ref,
             dwup_ref, dwdown_ref, dcfw_ref, dcfb_ref, dx1_ref, land_up_ref, land_down_ref,
             x1b_ref, drb_ref, dg_ref, dextg_ref, dextv_ref, gbuf_ref,
             dhug_ref, dhuv_ref, acc_wup, acc_wdown, acc_cfw, acc_cfb, sem, send_sems, recv_sems):
        fo = pl.program_id(0)
        f = order_ref[fo]
        f_prev = order_ref[jnp.maximum(fo - 1, 0)]
        slot = fo % 2
        i = pl.program_id(1)
        x, y, c = _mesh_pos()
        half = D_FF // N_DEV

        def to_sibling(fi, k, src, land_ref, shard_chip):
            d = jnp.bitwise_xor(shard_chip, 2 * x + y)
            slot = jnp.where(d == 1, 2, jnp.where(d == 2, 1, d))
            return pltpu.make_async_remote_copy(
                src_ref=src, dst_ref=land_ref.at[slot], send_sem=send_sems.at[fi, k], recv_sem=recv_sems.at[fi, k],
                device_id=(x, y, 1 - c), device_id_type=MESH)

        def up_copy(fi, g):
            return to_sibling(fi, g, dwup_ref.at[g, fi], land_up_ref, 2 * g + fi // 2)

        def down_copy(fi):
            return to_sibling(fi, 2, dwdown_ref.at[fi, pl.ds((1 - c) * half, half)], land_down_ref, fi)

        def flush(fi, s):
            return [pltpu.make_async_copy(acc_wup.at[s, 0], dwup_ref.at[0, fi], sem.at[s, 0]),
                    pltpu.make_async_copy(acc_wup.at[s, 1], dwup_ref.at[1, fi], sem.at[s, 1]),
                    pltpu.make_async_copy(acc_wdown.at[s], dwdown_ref.at[fi], sem.at[s, 2])]

        def flushed(fi, s):
            for cp in flush(fi, s):
                cp.wait()
            down_copy(fi).start()

            @pl.when(fi % 2 != c)
            def _():
                up_copy(fi, 0).start()
                up_copy(fi, 1).start()

        @pl.when(i == 0)
        def _():
            acc_wup[slot] = jnp.zeros(acc_wup.shape[1:], F32)
            acc_wdown[slot] = jnp.zeros(acc_wdown.shape[1:], F32)
            acc_cfw[...] = jnp.zeros(acc_cfw.shape, F32)
            acc_cfb[...] = jnp.zeros(acc_cfb.shape, F32)
            dextg_ref[tm:tm + HALO_F, :] = jnp.zeros((HALO_F, W_UP_BLK), F32)
            dextv_ref[tm:tm + HALO_F, :] = jnp.zeros((HALO_F, W_UP_BLK), F32)

        w = [cfw_ref[0, 0], cfw_ref[1, 0]]
        dext = [dextg_ref, dextv_ref]
        dhu = [dhug_ref, dhuv_ref]

        def rows1(bi):
            r = _rows(bi)
            gate = gv_ref[0, 0, r, :].astype(F32)
            val = gv_ref[1, 0, r, :].astype(F32)
            sg = _sigmoid(gate)
            silu = gate * sg
            gbuf_ref[r, :] = (silu * val).astype(BF16)
            dg = dg_ref[r, :]
            dgate = dg * val * (sg * (1.0 + gate * (1.0 - sg)))
            dval = dg * silu
            dextg_ref[r, :] = dgate
            dextv_ref[r, :] = dval
            acc_cfb[0:8, :] += _rsum8(dgate)
            acc_cfb[8:16, :] += _rsum8(dval)

        def rows2(bi):
            r = _rows(bi)
            for g in range(2):
                win = dext[g][pl.ds(bi * ROWS, ROWS + HALO_F), :]
                n = ROWS + HALO_F
                later = [pltpu.roll(win, n - 2, 0)[0:ROWS, :], pltpu.roll(win, n - 1, 0)[0:ROWS, :],
                         win[0:ROWS, :]]
                d = sum(later[k] * w[g][k:k + 1, :] for k in range(KF))
                dhu[g][r, :] = d.astype(BF16)
                pre = hu_ref[g, 0, r, :].astype(F32)
                for k in range(KF):
                    r0 = 8 * (g * KF + k)
                    acc_cfw[r0:r0 + 8, :] += _rsum8(later[k] * pre)

        for sub in reversed(range(tm // sub_rows)):
            rs = slice(sub * sub_rows, (sub + 1) * sub_rows)
            blocks = range(sub * sub_rows // ROWS, (sub + 1) * sub_rows // ROWS)
            x1b_ref[rs, :] = (xh_ref[rs, :] * g1_ref[...] + b1_ref[...]).astype(BF16)
            drb_ref[rs, :] = dr2_ref[rs, :].astype(BF16)
            dg_ref[rs, :] = _nt(drb_ref[rs, :], wdown_ref[0])
            for bi in blocks:
                rows1(bi)
            for bi in blocks:
                rows2(bi)
            acc_wdown[slot] += _tn(gbuf_ref[rs, :], drb_ref[rs, :])
            acc_wup[slot, 0] += _tn(dhug_ref[rs, :], x1b_ref[rs, :])
            acc_wup[slot, 1] += _tn(dhuv_ref[rs, :], x1b_ref[rs, :])
            dx1_ref[0, rs, :] = (_nt(dhug_ref[rs, :], wup_ref[0, 0])
                                 + _nt(dhuv_ref[rs, :], wup_ref[1, 0])).astype(BF16)
        dextg_ref[tm:tm + HALO_F, :] = dextg_ref[0:HALO_F, :]
        dextv_ref[tm:tm + HALO_F, :] = dextv_ref[0:HALO_F, :]

        @pl.when(i == nt - 1)
        def _():
            for g in range(2):
                dcfb_ref[g, 0] = jnp.sum(acc_cfb[8 * g:8 * g + 8, :], axis=0, keepdims=True)
                for k in range(KF):
                    r0 = 8 * (g * KF + k)
                    dcfw_ref[g, 0, k:k + 1, :] = jnp.sum(acc_cfw[r0:r0 + 8, :], axis=0, keepdims=True)
            for cp in flush(f, slot):
                cp.start()

        @pl.when((i == 0) & (fo > 0))
        def _():
            flushed(f_prev, 1 - slot)

        @pl.when((i == nt - 1) & (fo == N_F - 1))
        def _():
            flushed(f, slot)
            for fi in range(N_F):
                down_copy(fi).wait()
                for g in range(2):
                    @pl.when(fi % 2 != c)
                    def _():
                        up_copy(fi, g).wait_send()

                    @pl.when(fi % 2 == c)
                    def _():
                        up_copy(fi, g).wait_recv()

    rev = lambda i: nt - 1 - i
    row = pl.BlockSpec((tm, D), lambda fo, i, o: (rev(i), 0))
    pair = lambda r, c: pl.BlockSpec((2, 1, r, c), lambda fo, i, o: (0, o[fo], 0, 0))
    tile = pl.BlockSpec((2, 1, tm, W_UP_BLK), lambda fo, i, o: (0, o[fo], rev(i), 0))
    return pl.pallas_call(
        body, name="ffn_backward",
        grid_spec=pltpu.PrefetchScalarGridSpec(
            num_scalar_prefetch=1, grid=(N_F, nt),
            in_specs=[row, row, _full(ln1_g.shape), _full(ln1_b.shape), tile, tile,
                      pair(D, W_UP_BLK), pair(KF, W_UP_BLK),
                      pl.BlockSpec((1, W_UP_BLK, D), lambda fo, i, o: (o[fo], 0, 0))],
            out_specs=[ANY, ANY, pair(KF, W_UP_BLK), pair(1, W_UP_BLK),
                       pl.BlockSpec((1, tm, D), lambda fo, i, o: (o[fo], rev(i), 0)), ANY, ANY],
            scratch_shapes=[pltpu.VMEM((tm, D), BF16), pltpu.VMEM((tm, D), BF16),
                            pltpu.VMEM((tm, W_UP_BLK), F32),
                            pltpu.VMEM((tm + HALO_F, W_UP_BLK), F32), pltpu.VMEM((tm + HALO_F, W_UP_BLK), F32),
                            pltpu.VMEM((tm, W_UP_BLK), BF16), pltpu.VMEM((tm, W_UP_BLK), BF16),
                            pltpu.VMEM((tm, W_UP_BLK), BF16),
                            pltpu.VMEM((2, 2, W_UP_BLK, D), F32), pltpu.VMEM((2, W_UP_BLK, D), F32),
                            pltpu.VMEM((2 * KF * 8, W_UP_BLK), F32), pltpu.VMEM((16, W_UP_BLK), F32),
                            pltpu.SemaphoreType.DMA((2, 3)),
                            pltpu.SemaphoreType.DMA((N_F, 3)), pltpu.SemaphoreType.DMA((N_F, 3))]),
        out_shape=[jax.ShapeDtypeStruct((2, N_F, W_UP_BLK, D), F32),
                   jax.ShapeDtypeStruct((N_F, W_UP_BLK, D), F32),
                   jax.ShapeDtypeStruct((2, N_F, KF, W_UP_BLK), F32),
                   jax.ShapeDtypeStruct((2, N_F, 1, W_UP_BLK), F32),
                   jax.ShapeDtypeStruct((N_F, t, D), BF16),
                   jax.ShapeDtypeStruct((4, W_UP_BLK, D), F32),
                   jax.ShapeDtypeStruct((4, D_FF // N_DEV, D), F32)],
        compiler_params=_params(("arbitrary", "arbitrary")),
    )(order, dr2, xhat1, ln1_g, ln1_b, hu4, gv4, wup4, cfw4, wdown)


def mix_backward(x, h, yb1, dx1p, dr2, xhat1, rstd1, win_g, ln_a_g, ln_a_b, w_spatial, bst,
                 conv_b_w, ln_b_g, ln_b_b, wout, ln1_g, ffn_partials, tm):
    t = x.shape[0]
    n_p = len(ffn_partials)
    nt = t // tm
    n_chunks = tm // CHUNK
    halo_blocks = tm // HALO_B

    def body(x_ref, h_ref, halo_ref, yb1_ref, dx1p_ref, dr2_ref, xh1_ref, rstd1_ref, win_ref, ga_ref, ba_ref,
             ws_ref, bst_ref, cw_ref, gb_ref, bb_ref, wout_ref, g1_ref, *rest):
        p_refs, rest = rest[:n_p], rest[n_p:]
        gx_ref, dwin_ref, dwout_ref, dcw_ref, small_ref = rest[:5]
        land_refs, rest = rest[5:5 + n_p], rest[5 + n_p:]
        (ext_ref, dext_ref, y_ref, dy_ref, dh_ref, dmb_ref, wsm_ref,
         acc_win, acc_wout, acc_bin, acc_lnag, acc_lnab, acc_ws, acc_bs, acc_cbb, acc_lnbg,
         acc_lnbb, acc_bout, acc_ln1g, acc_ln1b, acc_cw, sem, send_sems, recv_sems) = rest
        i = pl.program_id(0)

        @pl.when(i == 0)
        def _():
            for cp in _chip_copies(p_refs, land_refs, send_sems, recv_sems):
                cp.start()

        first_tile = i == nt - 1
        accs = [acc_win, acc_wout, acc_bin, acc_lnag, acc_lnab, acc_ws, acc_bs, acc_cbb, acc_lnbg,
                acc_lnbb, acc_bout, acc_ln1g, acc_ln1b, acc_cw]

        @pl.when(i == 0)
        def _():
            for acc in accs:
                acc[...] = jnp.zeros(acc.shape, F32)
            dext_ref[tm:tm + HALO_B, :] = jnp.zeros((HALO_B, D_B), F32)
            mask = _tril_mask()
            for hd in range(HEADS):
                wsm_ref[hd] = jnp.where(mask, ws_ref[hd], 0.0).astype(BF16)

        def ln1_rows(bi):
            r = _rows(bi, LN_ROWS)
            part = [dx1p_ref[f, r, :].astype(F32) for f in range(N_F)]
            dx1 = ALPHA * dr2_ref[r, :] + ((part[0] + part[1]) + (part[2] + part[3]))
            xhat = xh1_ref[r, :]
            acc_ln1g[...] += _rsum8(dx1 * xhat)
            acc_ln1b[...] += _rsum8(dx1)
            dr1 = _ln_bwd(dx1 * g1_ref[...], xhat, rstd1_ref[r, 0:1])
            acc_bout[...] += _rsum8(dr1)
            gx_ref[r, :] = ALPHA * dr1
            dmb_ref[r, :] = dr1.astype(BF16)

        _loop(tm // LN_ROWS, ln1_rows)
        dy_ref[...] = _nt(dmb_ref[...], wout_ref[...])

        ha = halo_ref[:, 0:D_B]
        hg = halo_ref[:, D_B:2 * D_B]
        ext_ref[0:HALO_B, :] = jnp.where(first_tile, 0.0, 1.0) * (ha * _sigmoid(hg))

        def chunk(ci):
            r = _rows(ci, CHUNK)
            for hd in range(HEADS):
                sl = slice(hd * HEAD_DIM, (hd + 1) * HEAD_DIM)
                rows8 = slice(8 * hd, 8 * hd + 8)
                hus, hvs, u, cdf_u, cdf_v, xhat, rstd, vn, sv = _mixer_a_head(
                    h_ref, r, hd, ga_ref, ba_ref, wsm_ref, bst_ref)
                dy_a = dy_ref[r, sl]
                y_ref[r, sl] = (u * sv).astype(BF16)
                du = dy_a * sv
                dsv = dy_a * u
                dsvb = dsv.astype(BF16)
                acc_bs[hd] += dsv
                acc_ws[hd] += _nt(dsvb, vn)
                dvn = _tn(wsm_ref[hd], dsvb)
                acc_lnag[rows8, :] += _rsum8(dvn * xhat)
                acc_lnab[rows8, :] += _rsum8(dvn)
                dv = _ln_bwd(dvn * ga_ref[hd:hd + 1, :], xhat, rstd)
                slv = slice(D_A + hd * HEAD_DIM, D_A + (hd + 1) * HEAD_DIM)
                dhu = du * (cdf_u + hus * jnp.exp(-0.5 * hus * hus) * INV_SQRT_2PI)
                dhv = dv * (cdf_v + hvs * jnp.exp(-0.5 * hvs * hvs) * INV_SQRT_2PI)
                acc_bin[:, sl] += _rsum8(dhu)
                acc_bin[:, slv] += _rsum8(dhv)
                dh_ref[r, sl] = dhu.astype(BF16)
                dh_ref[r, slv] = dhv.astype(BF16)
            a_b = h_ref[r, 2 * D_A:2 * D_A + D_B]
            g_b = h_ref[r, 2 * D_A + D_B:D_IN]
            ext_ref[pl.ds(HALO_B + ci * CHUNK, CHUNK), :] = a_b * _sigmoid(g_b)

        _loop(n_chunks, chunk)

        def conv_rows(bi):
            base = bi * ROWS
            r = pl.ds(base, ROWS)
            xhat, rstd = _ln_stats(yb1_ref[r, :])
            yb2 = xhat * gb_ref[...] + bb_ref[...]
            sg = _sigmoid(yb2)
            y_ref[r, D_A:D] = (yb2 * sg).astype(BF16)
            dyb2 = dy_ref[r, D_A:D] * (sg * (1.0 + yb2 * (1.0 - sg)))
            acc_lnbg[...] += _rsum8(dyb2 * xhat)
            acc_lnbb[...] += _rsum8(dyb2)
            dyb1 = _ln_bwd(dyb2 * gb_ref[...], xhat, rstd)
            acc_cbb[...] += _rsum8(dyb1)
            dext_ref[r, :] = dyb1
            for k, tap in _taps(ext_ref[pl.ds(base, ROWS + HALO_B), :], CONV_B_OFFSETS):
                acc_cw[8 * k:8 * k + 8, :] += _rsum8(dyb1 * tap)

        _loop(tm // ROWS, conv_rows)

        def convt_rows(bi):
            base = bi * ROWS
            r = pl.ds(base, ROWS)
            dyb0 = jnp.zeros((ROWS, D_B), F32)
            for k, tap in _taps(dext_ref[pl.ds(base, ROWS + HALO_B), :], CONV_B_T_OFFSETS):
                dyb0 = dyb0 + tap * cw_ref[k:k + 1, :]
            a_b = h_ref[r, 2 * D_A:2 * D_A + D_B]
            sg = _sigmoid(h_ref[r, 2 * D_A + D_B:D_IN])
            da_b = dyb0 * sg
            dg_b = dyb0 * a_b * sg * (1.0 - sg)
            acc_bin[:, 2 * D_A:2 * D_A + D_B] += _rsum8(da_b)
            acc_bin[:, 2 * D_A + D_B:D_IN] += _rsum8(dg_b)
            dh_ref[r, 2 * D_A:2 * D_A + D_B] = da_b.astype(BF16)
            dh_ref[r, 2 * D_A + D_B:D_IN] = dg_b.astype(BF16)

        _loop(tm // ROWS, convt_rows)
        dext_ref[tm:tm + HALO_B, :] = dext_ref[0:HALO_B, :]

        acc_wout[...] += _tn(y_ref[...], dmb_ref[...])
        xt = x_ref[...].T.astype(BF16)
        dh_blocks = [dh_ref[:, j * W_IN_BLK:(j + 1) * W_IN_BLK] for j in range(N_DEV)]
        for j in range(N_DEV):
            acc_win[j] += _nn(xt, dh_blocks[j])
        gx_ref[...] += sum(_nt(dh_blocks[j], win_ref[j]) for j in range(N_DEV))

        @pl.when(i == nt - 1)
        def _():
            cps = [pltpu.make_async_copy(acc_win, dwin_ref, sem.at[0]),
                   pltpu.make_async_copy(acc_wout, dwout_ref, sem.at[1])]
            for cp in cps:
                cp.start()
            small_ref[...] = jnp.zeros(small_ref.shape, F32)

            def put_row_vector(row0, acc):
                vec = jnp.sum(acc[...], axis=0, keepdims=True)
                for k in range(vec.shape[1] // 128):
                    small_ref[row0 + k:row0 + k + 1, :] = vec[:, k * 128:(k + 1) * 128]

            put_row_vector(S_BIN, acc_bin)
            put_row_vector(S_CBB, acc_cbb)
            put_row_vector(S_LNBG, acc_lnbg)
            put_row_vector(S_LNBB, acc_lnbb)
            put_row_vector(S_BOUT, acc_bout)
            put_row_vector(S_LN1G, acc_ln1g)
            put_row_vector(S_LN1B, acc_ln1b)
            mask = _tril_mask()
            for hd in range(HEADS):
                rows8 = slice(8 * hd, 8 * hd + 8)
                small_ref[S_LNAG + hd:S_LNAG + hd + 1, :] = jnp.sum(acc_lnag[rows8, :], axis=0, keepdims=True)
                small_ref[S_LNAB + hd:S_LNAB + hd + 1, :] = jnp.sum(acc_lnab[rows8, :], axis=0, keepdims=True)
                small_ref[S_WS + hd * CHUNK:S_WS + (hd + 1) * CHUNK, :] = jnp.where(mask, acc_ws[hd], 0.0)
                small_ref[S_BS + hd:S_BS + hd + 1, :] = jnp.sum(acc_bs[hd].T, axis=0, keepdims=True)
            for k in range(KB):
                dcw_ref[k:k + 1, :] = jnp.sum(acc_cw[8 * k:8 * k + 8, :], axis=0, keepdims=True)
            for cp in cps:
                cp.wait()
            for cp in _chip_copies(p_refs, land_refs, send_sems, recv_sems):
                cp.wait()

    rev = lambda i: nt - 1 - i
    row = lambda w: pl.BlockSpec((tm, w), lambda i: (rev(i), 0))
    return pl.pallas_call(
        body, name="mix_backward", grid=(nt,),
        in_specs=[row(D), row(D_IN),
                  pl.BlockSpec((HALO_B, 2 * D_B), lambda i: (jnp.maximum(rev(i) * halo_blocks - 1, 0), 1)),
                  row(D_B), pl.BlockSpec((N_F, tm, D), lambda i: (0, rev(i), 0)),
                  row(D), row(D), row(128), _resident(win_g.shape), _full(ln_a_g.shape),
                  _full(ln_a_b.shape), _full(w_spatial.shape), _full(bst.shape), _full(conv_b_w.shape),
                  _full(ln_b_g.shape), _full(ln_b_b.shape),
                  _resident(wout.shape), _full(ln1_g.shape)] + [ANY] * n_p,
        out_specs=[row(D), ANY, ANY, _full((KB, D_B)), _full((S_MIX_ROWS, 128))] + [ANY] * n_p,
        out_shape=[jax.ShapeDtypeStruct((t, D), F32), jax.ShapeDtypeStruct((N_DEV, D, W_IN_BLK), F32),
                   jax.ShapeDtypeStruct((D, D), F32), jax.ShapeDtypeStruct((KB, D_B), F32),
                   jax.ShapeDtypeStruct((S_MIX_ROWS, 128), F32)]
        + [jax.ShapeDtypeStruct(p.shape, BF16) for p in ffn_partials],
        scratch_shapes=[pltpu.VMEM((tm + HALO_B, D_B), F32), pltpu.VMEM((tm + HALO_B, D_B), F32),
                        pltpu.VMEM((tm, D), BF16), pltpu.VMEM((tm, D), F32), pltpu.VMEM((tm, D_IN), BF16),
                        pltpu.VMEM((tm, D), BF16),
                        pltpu.VMEM((HEADS, CHUNK, CHUNK), BF16),
                        pltpu.VMEM((N_DEV, D, W_IN_BLK), F32), pltpu.VMEM((D, D), F32),
                        pltpu.VMEM((8, D_IN), F32), pltpu.VMEM((8 * HEADS, HEAD_DIM), F32),
                        pltpu.VMEM((8 * HEADS, HEAD_DIM), F32), pltpu.VMEM((HEADS, CHUNK, CHUNK), F32),
                        pltpu.VMEM((HEADS, CHUNK, CHUNK), F32), pltpu.VMEM((8, D_B), F32),
                        pltpu.VMEM((8, D_B), F32), pltpu.VMEM((8, D_B), F32), pltpu.VMEM((8, D), F32),
                        pltpu.VMEM((8, D), F32), pltpu.VMEM((8, D), F32), pltpu.VMEM((8 * KB, D_B), F32),
                        pltpu.SemaphoreType.DMA((2,)),
                        pltpu.SemaphoreType.DMA((n_p, 3)), pltpu.SemaphoreType.DMA((n_p, 3))],
        compiler_params=_params(("arbitrary",)),
    )(x, h, h, yb1, dx1p, dr2, xhat1, rstd1, win_g, ln_a_g, ln_a_b, w_spatial, bst, conv_b_w,
      ln_b_g, ln_b_b, wout, ln1_g, *ffn_partials)


def _rows128(a):
    return a.reshape(-1, 128)


def _pack_conv(cb, cf):
    lead = cb.shape[:-2]
    pad = [(0, 0)] * len(lead)
    flat = jnp.pad(cb.reshape(lead + (KB * 64,)), pad + [(0, 3 * W_UP_BLK - KB * 64)])
    rows = jnp.concatenate([cf, flat.reshape(lead + (3, W_UP_BLK))], axis=-2)
    return jnp.pad(rows, pad + [(0, 2), (0, 768 - W_UP_BLK)])


def _unpack_conv(p):
    lead = p.shape[:-2]
    cf = p[..., 0:KF, 0:W_UP_BLK]
    cb = p[..., 3:6, 0:W_UP_BLK].reshape(lead + (3 * W_UP_BLK,))[..., :KB * 64].reshape(lead + (KB, 64))
    return cb, cf


def kernel(x, w_in, b_in, ln_a_g, ln_a_b, w_spatial, b_spatial, conv_b_w, conv_b_b, ln_b_g, ln_b_b, w_out, b_out, ln1_g, ln1_b, w_up, conv_f_w, conv_f_b, w_down, ln2_g, ln2_b, loss_target, m_w_in, m_b_in, m_ln_a_g, m_ln_a_b, m_w_spatial, m_b_spatial, m_conv_b_w, m_conv_b_b, m_ln_b_g, m_ln_b_b, m_w_out, m_b_out, m_ln1_g, m_ln1_b, m_w_up, m_conv_f_w, m_conv_f_b, m_w_down, m_ln2_g, m_ln2_b, v_w_in, v_b_in, v_ln_a_g, v_ln_a_b, v_w_spatial, v_b_spatial, v_conv_b_w, v_conv_b_b, v_ln_b_g, v_ln_b_b, v_w_out, v_b_out, v_ln1_g, v_ln1_b, v_w_up, v_conv_f_w, v_conv_f_b, v_w_down, v_ln2_g, v_ln2_b):
    t = x.shape[1]
    x2 = x.reshape(t, D)
    target = loss_target.reshape(t, D)
    tm_fwd = min(t, 512)
    tm_bwd = min(t, 256)
    tm_ffn_bwd = min(t, 512)

    xi, yi, ci = _mesh_pos()
    jidx = jnp.stack([_lid(px, py, ci) for px, py in _chip_patterns(xi, yi)]).astype(jnp.int32)

    sin, sout, sup, sdown, conv_g = prepare_weights(w_in, w_out, w_up.T, w_down, _pack_conv(conv_b_w, conv_f_w))
    conv_b_all, cfw = _unpack_conv(conv_g)
    conv_b_full = conv_b_all.transpose(1, 0, 2).reshape(KB, D_B)
    cfb = conv_f_b.reshape(N_DEV, W_UP_BLK)
    row = lambda a: a.reshape(1, -1)
    bst = b_spatial.T

    h, xhat1, rstd1, yb1, win_g, wout_g, wup_g, wdown_g = mix_forward(
        x2, sin, sout, row(b_in), ln_a_g, ln_a_b, w_spatial, bst, conv_b_full, row(conv_b_b),
        row(ln_b_g), row(ln_b_b), row(b_out), row(ln1_g), row(ln1_b), sup, sdown, tm_fwd)
    wout_full = wout_g.reshape(D, D)
    wdown4 = wdown_g.reshape(N_F, W_UP_BLK, D)
    hu, gv, dr2, loss_part, s_ln2 = ffn_forward(
        xhat1, row(ln1_g), row(ln1_b), wup_g, cfw, cfb, wdown4, row(ln2_g), row(ln2_b), target, tm_bwd)

    order = jnp.where(ci == 0, jnp.array([1, 3, 0, 2], jnp.int32), jnp.array([0, 2, 1, 3], jnp.int32))
    dwup, dwdown, dcfw, dcfb, dx1p, *ffn_lands = ffn_backward(
        order, dr2, xhat1, row(ln1_g), row(ln1_b), hu, gv, wup_g, cfw, wdown4, tm_ffn_bwd)
    ffn_grads = [dwup.reshape(N_DEV, W_UP_BLK, D), dwdown.reshape(N_DEV, D_FF // N_DEV, D)]
    ffn_partials = [chip_partials("chip_partials_" + nm, g, l, jidx, rb)
                    for nm, g, l, rb in zip(["w_up", "w_down"], ffn_grads, ffn_lands, [352, 352])]
    grad_x, dwin, dwout, dcw, s_mix, *ffn_recvs = mix_backward(
        x2, h, yb1, dx1p, dr2, xhat1, rstd1, win_g, ln_a_g, ln_a_b, w_spatial, bst,
        conv_b_full, row(ln_b_g), row(ln_b_b), wout_full, row(ln1_g), ffn_partials, tm_bwd)

    dcfb_rows = jnp.pad(dcfb.reshape(-1, 128), ((0, 4), (0, 0)))
    svec = jnp.concatenate([s_mix, dcfb_rows, s_ln2, loss_part], axis=0)
    dconv = _pack_conv(dcw.reshape(KB, N_DEV, 64).transpose(1, 0, 2), dcfw.reshape(N_DEV, KF, W_UP_BLK))
    mix_grads = [dwin, dwout.reshape(N_DEV, D // N_DEV, D), dconv]
    mix_w = [w_in, w_out, _pack_conv(conv_b_w, conv_f_w)]
    mix_m = [m_w_in, m_w_out, _pack_conv(m_conv_b_w, m_conv_f_w)]
    mix_v = [v_w_in, v_w_out, _pack_conv(v_conv_b_w, v_conv_f_w)]
    *mix_out, sv_slots = mixer_reduce_adamw(mix_grads, svec, mix_w, mix_m, mix_v)
    big = {nm: [mix_out[k * 3 + p] for k in range(4)] for p, nm in enumerate(["w_in", "w_out", "conv"])}

    ffn_w = [(w_up.T, m_w_up.T, v_w_up.T), (w_down, m_w_down, v_w_down)]
    for nm, g, l, r, (w, m, v) in zip(["w_up", "w_down"], ffn_grads, ffn_lands, ffn_recvs, ffn_w):
        big[nm] = reduce_and_adamw("reduce_adamw_" + nm, g, l, r, w, m, v, jidx, 352)
    big["w_up"] = [o.T for o in big["w_up"]]
    for k in range(4):
        cb_k, cf_k = _unpack_conv(big["conv"][k])
        big.setdefault("conv_b_w", []).append(cb_k)
        big.setdefault("conv_f_w", []).append(cf_k)

    small_w = dict(b_in=b_in, ln_a_g=ln_a_g, ln_a_b=ln_a_b, w_spatial=w_spatial, b_spatial=b_spatial,
                   conv_b_b=conv_b_b, ln_b_g=ln_b_g, ln_b_b=ln_b_b, b_out=b_out, ln1_g=ln1_g,
                   ln1_b=ln1_b, conv_f_b=conv_f_b, ln2_g=ln2_g, ln2_b=ln2_b)
    small_m = dict(b_in=m_b_in, ln_a_g=m_ln_a_g, ln_a_b=m_ln_a_b, w_spatial=m_w_spatial,
                   b_spatial=m_b_spatial, conv_b_b=m_conv_b_b, ln_b_g=m_ln_b_g, ln_b_b=m_ln_b_b,
                   b_out=m_b_out, ln1_g=m_ln1_g, ln1_b=m_ln1_b, conv_f_b=m_conv_f_b, ln2_g=m_ln2_g,
                   ln2_b=m_ln2_b)
    small_v = dict(b_in=v_b_in, ln_a_g=v_ln_a_g, ln_a_b=v_ln_a_b, w_spatial=v_w_spatial,
                   b_spatial=v_b_spatial, conv_b_b=v_conv_b_b, ln_b_g=v_ln_b_g, ln_b_b=v_ln_b_b,
                   b_out=v_b_out, ln1_g=v_ln1_g, ln1_b=v_ln1_b, conv_f_b=v_conv_f_b, ln2_g=v_ln2_g,
                   ln2_b=v_ln2_b)
    order = [nm for nm, _, _ in SMALL_LAYOUT]
    small_out = small_adamw(sv_slots, [_rows128(small_w[nm]) for nm in order],
                            [_rows128(small_m[nm]) for nm in order], [_rows128(small_v[nm]) for nm in order])
    n_small = len(order)
    small = {nm: [small_out[k * n_small + p].reshape(small_w[nm].shape) for k in range(4)]
             for p, nm in enumerate(order)}
    loss = jnp.sum(small_out[4 * n_small]) * (0.5 / D)

    weights = ["w_in", "b_in", "ln_a_g", "ln_a_b", "w_spatial", "b_spatial", "conv_b_w", "conv_b_b",
               "ln_b_g", "ln_b_b", "w_out", "b_out", "ln1_g", "ln1_b", "w_up", "conv_f_w", "conv_f_b",
               "w_down", "ln2_g", "ln2_b"]
    result = lambda nm, k: big[nm][k] if nm in big else small[nm][k]
    return (loss, grad_x.reshape(x.shape), *[result(nm, 0) for nm in weights],
            *[result(nm, 1) for nm in weights], *[result(nm, 2) for nm in weights],
            *[result(nm, 3) for nm in weights])
```

```python
import functools
import math

import jax
import jax.numpy as jnp
from jax import lax
from jax.experimental import pallas as pl
from jax.experimental.pallas import tpu as pltpu

F32 = jnp.float32
BF16 = jnp.bfloat16

D = 1024
D_A = 512
D_B = 512
HEADS = 4
HEAD_DIM = 128
CHUNK = 128
KB = 31
KF = 3
D_FF = 2816
D_IN = 2048
N_DEV = 8
W_IN_BLK = D_IN // N_DEV
W_UP_BLK = 2 * D_FF // N_DEV
N_F = 4
LN_EPS = 1e-5
ALPHA = 2.0 ** 0.25

ADAM_LR = 0.001
ADAM_B1 = 0.9
ADAM_B2 = 0.999
ADAM_EPS = 1e-08
ADAM_WD = 0.01
ADAM_STEP = 10

INV_SQRT2 = 1.0 / math.sqrt(2.0)
INV_SQRT_2PI = 1.0 / math.sqrt(2.0 * math.pi)

HALO_B = 32
HALO_F = 8
ROWS = 64
LN_ROWS = 32
VMEM_LIMIT = 58 * 1024 * 1024

MESH = pl.DeviceIdType.MESH
ANY = pl.BlockSpec(memory_space=pl.ANY)
VMEM = pl.BlockSpec(memory_space=pltpu.VMEM)

S_BIN, S_LNAG, S_LNAB, S_WS, S_BS, S_CBB, S_LNBG, S_LNBB, S_BOUT, S_LN1G, S_LN1B = (
    0, 16, 24, 32, 544, 552, 560, 568, 576, 584, 592)
S_MIX_ROWS = 600
S_CFB = 600
S_LN2G = 648
S_LN2B = 656
S_LOSS = 664
S_ROWS = 672


def _tn(a, b):
    return lax.dot_general(a, b, (((0,), (0,)), ((), ())), preferred_element_type=F32)


def _nt(a, b):
    return lax.dot_general(a, b, (((1,), (1,)), ((), ())), preferred_element_type=F32)


def _nn(a, b):
    return jnp.dot(a, b, preferred_element_type=F32)


def _sigmoid(x):
    return 1.0 / (1.0 + jnp.exp(-x))


def _ln_stats(x):
    mu = jnp.mean(x, axis=-1, keepdims=True)
    xc = x - mu
    var = jnp.mean(xc * xc, axis=-1, keepdims=True)
    rstd = lax.rsqrt(var + LN_EPS)
    return xc * rstd, rstd


def _ln_bwd(dxhat, xhat, rstd):
    m1 = jnp.mean(dxhat, axis=-1, keepdims=True)
    m2 = jnp.mean(dxhat * xhat, axis=-1, keepdims=True)
    return rstd * (dxhat - m1 - xhat * m2)


def _rsum8(x):
    r, n = x.shape
    return x.reshape(r // 8, 8, n).sum(axis=0)


def _rows(i, n=ROWS):
    return pl.ds(i * n, n)


def _loop(n, body):
    for i in range(n):
        body(i)


def _tril_mask():
    r = lax.broadcasted_iota(jnp.int32, (CHUNK, CHUNK), 0)
    c = lax.broadcasted_iota(jnp.int32, (CHUNK, CHUNK), 1)
    return c <= r


def _mixer_a_head(h_ref, r, hd, ga_ref, ba_ref, wsm_ref, bst_ref):
    sl = slice(hd * HEAD_DIM, (hd + 1) * HEAD_DIM)
    hu = h_ref[r, sl]
    hv = h_ref[r, D_A + hd * HEAD_DIM:D_A + (hd + 1) * HEAD_DIM]
    cdf_u = 0.5 * (1.0 + lax.erf(hu * INV_SQRT2))
    cdf_v = 0.5 * (1.0 + lax.erf(hv * INV_SQRT2))
    u = hu * cdf_u
    xhat, rstd = _ln_stats(hv * cdf_v)
    vn = (xhat * ga_ref[hd:hd + 1, :] + ba_ref[hd:hd + 1, :]).astype(BF16)
    sv = _nn(wsm_ref[hd], vn) + bst_ref[:, hd:hd + 1]
    return hu, hv, u, cdf_u, cdf_v, xhat, rstd, vn, sv


def _taps(win, offsets):
    n = win.shape[0]
    for s in range(8):
        ks = [k for k, o in enumerate(offsets) if o % 8 == s]
        if ks:
            moved = win if s == 0 else pltpu.roll(win, n - s, 0)
            for k in ks:
                yield k, moved[offsets[k] - s:offsets[k] - s + ROWS, :]


CONV_B_OFFSETS = [2 + k for k in range(KB)]
CONV_B_T_OFFSETS = [30 - k for k in range(KB)]


def _conv_b_block(ext_ref, base, cw_ref):
    acc = jnp.zeros((ROWS, D_B), F32)
    for k, tap in _taps(ext_ref[pl.ds(base, ROWS + HALO_B), :], CONV_B_OFFSETS):
        acc = acc + tap * cw_ref[k:k + 1, :]
    return acc


def _taps_f(win):
    n = ROWS + HALO_F
    return [pltpu.roll(win, n - 6, 0)[0:ROWS, :], pltpu.roll(win, n - 7, 0)[0:ROWS, :], win[8:n, :]]


def _params(sem, **kw):
    return pltpu.CompilerParams(dimension_semantics=sem, vmem_limit_bytes=VMEM_LIMIT, **kw)


def _resident(shape):
    zeros = (0,) * len(shape)
    return pl.BlockSpec(shape, lambda *_: zeros, pipeline_mode=pl.Buffered(1))


def _full(shape):
    zeros = (0,) * len(shape)
    return pl.BlockSpec(shape, lambda *_: zeros)


def _mesh_pos():
    return lax.axis_index("x"), lax.axis_index("y"), lax.axis_index("c")


def _chip_patterns(x, y):
    return [(x, y), (1 - x, y), (x, 1 - y), (1 - x, 1 - y)]


def _lid(x, y, c):
    return 4 * x + 2 * y + c


def _gather_copy(outs, send_sems, recv_sems, a, k, block, to, src=None):
    blk = outs[a].at[_lid(*block)]
    return pltpu.make_async_remote_copy(
        src_ref=blk if src is None else src, dst_ref=blk,
        send_sem=send_sems.at[a, k], recv_sem=recv_sems.at[a, k], device_id=to, device_id_type=MESH)


def _gather_start(mine, outs, send_sems, recv_sems, local_sems):
    x, y, c = _mesh_pos()
    me = (x, y, c)
    for a in range(len(mine)):
        pltpu.make_async_copy(mine[a], outs[a].at[_lid(*me)], local_sems.at[a]).start()
        for k, to in enumerate([(x, y, 1 - c), (1 - x, y, c), (x, 1 - y, c)]):
            _gather_copy(outs, send_sems, recv_sems, a, k, me, to, src=mine[a]).start()


def _gather_relay(mine, outs, send_sems, recv_sems, local_sems, via):
    x, y, c = _mesh_pos()
    me, sib = (x, y, c), (x, y, 1 - c)
    copy = functools.partial(_gather_copy, outs, send_sems, recv_sems)
    source = {1: (1 - x, y, c), 2: (x, 1 - y, c)}
    for a in range(len(mine)):
        for k in (via[a], 3 - via[a]):
            copy(a, k, source[k], me).wait_recv()
            if k == via[a]:
                copy(a, 3, source[k], source[3 - k]).start()
            copy(a, 3 + k, source[k], sib).start()


def _gather_finish(mine, outs, send_sems, recv_sems, local_sems):
    x, y, c = _mesh_pos()
    me, sib = (x, y, c), (x, y, 1 - c)
    copy = functools.partial(_gather_copy, outs, send_sems, recv_sems)
    diag = (1 - x, 1 - y)
    n = len(mine)
    for a in range(n):
        copy(a, 3, (*diag, c), me).wait_recv()
        copy(a, 6, (*diag, c), sib).start()
    for a in range(n):
        copy(a, 0, sib, me).wait_recv()
        for k, chip in zip((4, 5, 6), [(1 - x, y), (x, 1 - y), diag]):
            copy(a, k, (*chip, 1 - c), me).wait_recv()
        for k in range(7):
            copy(a, k, me, sib, src=mine[a]).wait_send()
        pltpu.make_async_copy(mine[a], outs[a].at[_lid(*me)], local_sems.at[a]).wait()


def _gather_scratch(n):
    return [pltpu.SemaphoreType.DMA((n, 7)), pltpu.SemaphoreType.DMA((n, 7)), pltpu.SemaphoreType.DMA((n,))]


def prepare_weights(w_in, w_out, w_up_t, w_down, convp):
    def body(win_ref, wout_ref, wup_ref, wdown_ref, convp_ref,
             sin_ref, sout_ref, sup_ref, sdown_ref, gconv_ref, send_sems, recv_sems, local_sems):
        gather = ([convp_ref], [gconv_ref], send_sems, recv_sems, local_sems)
        _gather_start(*gather)
        sin_ref[...] = win_ref[...].astype(BF16)
        sout_ref[...] = wout_ref[...].astype(BF16)
        sup_ref[...] = wup_ref[...].T.astype(BF16)
        sdown_ref[...] = wdown_ref[...].astype(BF16)
        _gather_relay(*gather, via=[1])
        _gather_finish(*gather)

    return pl.pallas_call(
        body, name="prepare_weights",
        out_shape=[jax.ShapeDtypeStruct(w_in.shape, BF16), jax.ShapeDtypeStruct(w_out.shape, BF16),
                   jax.ShapeDtypeStruct(w_up_t.shape[::-1], BF16), jax.ShapeDtypeStruct(w_down.shape, BF16),
                   jax.ShapeDtypeStruct((N_DEV,) + convp.shape, F32)],
        in_specs=[VMEM] * 5, out_specs=[VMEM] * 4 + [ANY],
        scratch_shapes=_gather_scratch(1),
        compiler_params=pltpu.CompilerParams(vmem_limit_bytes=VMEM_LIMIT),
    )(w_in, w_out, w_up_t, w_down, convp)


def _chip_copies(p, land, send_sems, recv_sems):
    x, y, c = _mesh_pos()
    return [pltpu.make_async_remote_copy(
        src_ref=p[a].at[k], dst_ref=land[a].at[k], send_sem=send_sems.at[a, k], recv_sem=recv_sems.at[a, k],
        device_id=(px, py, c), device_id_type=MESH)
        for k, (px, py) in enumerate(_chip_patterns(x, y)[1:]) for a in range(len(p))]


def chip_partials(name, g, land, jidx, rb):
    _, r, c = g.shape

    def body(j_ref, g_ref, l_ref, o_ref):
        o_ref[...] = (g_ref[...] + l_ref[...]).astype(BF16)

    return pl.pallas_call(
        body, name=name,
        out_shape=jax.ShapeDtypeStruct((3, r, c), BF16),
        grid_spec=pltpu.PrefetchScalarGridSpec(
            num_scalar_prefetch=1, grid=(3, r // rb),
            in_specs=[pl.BlockSpec((1, rb, c), lambda k, i, j: (j[1 + k], i, 0)),
                      pl.BlockSpec((1, rb, c), lambda k, i, j: (1 + k, i, 0))],
            out_specs=pl.BlockSpec((1, rb, c), lambda k, i, j: (k, i, 0))),
        compiler_params=_params(("arbitrary", "arbitrary")),
    )(jidx, g, land)


def _adamw(w, g, m, v):
    m2 = ADAM_B1 * m + (1.0 - ADAM_B1) * g
    v2 = ADAM_B2 * v + (1.0 - ADAM_B2) * (g * g)
    m_hat = m2 / (1.0 - ADAM_B1 ** ADAM_STEP)
    v_hat = v2 / (1.0 - ADAM_B2 ** ADAM_STEP)
    delta = -ADAM_LR * (m_hat / (jnp.sqrt(v_hat) + ADAM_EPS) + ADAM_WD * w)
    return delta, m2, v2


def reduce_and_adamw(name, g, land, recv, w, m, v, jidx, rb):
    _, r, c = g.shape

    def body(j_ref, g_ref, l_ref, r_ref, w_ref, m_ref, v_ref, go_ref, do_ref, mo_ref, vo_ref):
        grad = (g_ref[0] + l_ref[0]) + r_ref[0].astype(F32) + r_ref[1].astype(F32) + r_ref[2].astype(F32)
        delta, m2, v2 = _adamw(w_ref[...], grad, m_ref[...], v_ref[...])
        go_ref[...] = grad
        do_ref[...] = delta
        mo_ref[...] = m2
        vo_ref[...] = v2

    blk = pl.BlockSpec((rb, c), lambda i, j: (i, 0))
    return pl.pallas_call(
        body, name=name,
        out_shape=[jax.ShapeDtypeStruct((r, c), F32)] * 4,
        grid_spec=pltpu.PrefetchScalarGridSpec(
            num_scalar_prefetch=1, grid=(r // rb,),
            in_specs=[pl.BlockSpec((1, rb, c), lambda i, j: (j[0], i, 0)),
                      pl.BlockSpec((1, rb, c), lambda i, j: (0, i, 0)),
                      pl.BlockSpec((3, rb, c), lambda i, j: (0, i, 0)),
                      blk, blk, blk],
            out_specs=[blk] * 4),
        compiler_params=_params(("arbitrary",)),
    )(jidx, g, land, recv, w, m, v)


def mixer_reduce_adamw(grads, svec, ws, ms, vs):
    n = len(grads)
    shard = [g.shape[1:] for g in grads]

    def body(*refs):
        g = refs[:n]
        sv_ref = refs[n]
        w, m, v = refs[n + 1:2 * n + 1], refs[2 * n + 1:3 * n + 1], refs[3 * n + 1:4 * n + 1]
        outs = refs[4 * n + 1:8 * n + 1]
        sv_slots = refs[8 * n + 1]
        rest = refs[8 * n + 2:]
        own, land, sendb, recvb = rest[:n], rest[n:2 * n], rest[2 * n:3 * n], rest[3 * n:4 * n]
        sv_land, chip_sv, d2d_send, d2d_recv, ici_send, ici_recv, local_sems, sv_sems = rest[4 * n:]
        x, y, c = _mesh_pos()
        sib = (x, y, 1 - c)
        pats = _chip_patterns(x, y)
        q = 2 * x + y

        d2d, local = [], []
        for a in range(n):
            for k, (px, py) in enumerate(pats):
                d2d.append(pltpu.make_async_remote_copy(
                    src_ref=g[a].at[_lid(px, py, 1 - c)], dst_ref=land[a].at[k],
                    send_sem=d2d_send.at[a, k], recv_sem=d2d_recv.at[a, k], device_id=sib, device_id_type=MESH))
                local.append(pltpu.make_async_copy(g[a].at[_lid(px, py, c)], own[a].at[k], local_sems.at[a, k]))
        d2d.append(pltpu.make_async_remote_copy(
            src_ref=sv_ref, dst_ref=sv_land, send_sem=d2d_send.at[n, 0], recv_sem=d2d_recv.at[n, 0],
            device_id=sib, device_id_type=MESH))
        for cp in d2d + local:
            cp.start()
        for cp in local + d2d:
            cp.wait()

        for a in range(n):
            for k in range(3):
                sendb[a][k] = (own[a][1 + k] + land[a][1 + k]).astype(BF16)
        chip_sv[...] = sv_ref[...] + sv_land[...]
        ici = _chip_copies(sendb, recvb, ici_send, ici_recv)
        half_rows = svec.shape[0] // 2
        rows = pl.ds(pl.multiple_of(c * half_rows, 8), half_rows)
        sv_local = pltpu.make_async_copy(chip_sv, sv_slots.at[q], sv_sems.at[0])

        def sv_ici(k, slot, to):
            return pltpu.make_async_remote_copy(
                src_ref=chip_sv.at[rows], dst_ref=sv_slots.at[slot, rows], send_sem=sv_sems.at[1 + k],
                recv_sem=sv_sems.at[4 + k], device_id=to, device_id_type=MESH)

        def sv_pass_on(k, slot):
            return pltpu.make_async_remote_copy(
                src_ref=sv_slots.at[slot, rows], dst_ref=sv_slots.at[slot, rows], send_sem=sv_sems.at[7 + k],
                recv_sem=sv_sems.at[10 + k], device_id=sib, device_id_type=MESH)

        sv_out = [sv_ici(k, q, (px, py, c)) for k, (px, py) in enumerate(pats[1:])]
        for cp in sv_out + [sv_local] + ici:
            cp.start()
        for k, (px, py) in enumerate(pats[1:]):
            sv_out[k].wait_send()
            sv_ici(k, 2 * px + py, (px, py, c)).wait_recv()
            sv_pass_on(k, 2 * px + py).start()
        for cp in ici:
            cp.wait()
        for k, (px, py) in enumerate(pats[1:]):
            sv_pass_on(k, 2 * px + py).wait()
        sv_local.wait()

        for a in range(n):
            grad = ((own[a][0] + land[a][0]) + recvb[a][0].astype(F32) + recvb[a][1].astype(F32)
                    + recvb[a][2].astype(F32))
            delta, m2, v2 = _adamw(w[a][...], grad, m[a][...], v[a][...])
            outs[a][...] = grad
            outs[n + a][...] = delta
            outs[2 * n + a][...] = m2
            outs[3 * n + a][...] = v2

    shard_out = [jax.ShapeDtypeStruct(s, F32) for s in shard]
    return pl.pallas_call(
        body, name="mixer_reduce_adamw",
        out_shape=shard_out * 4 + [jax.ShapeDtypeStruct((4,) + svec.shape, F32)],
        in_specs=[ANY] * n + [VMEM] * (1 + 3 * n), out_specs=[VMEM] * (4 * n) + [ANY],
        scratch_shapes=[pltpu.VMEM((4,) + s, F32) for s in shard] + [pltpu.VMEM((4,) + s, F32) for s in shard]
        + [pltpu.VMEM((3,) + s, BF16) for s in shard] + [pltpu.VMEM((3,) + s, BF16) for s in shard]
        + [pltpu.VMEM(svec.shape, F32), pltpu.VMEM(svec.shape, F32),
           pltpu.SemaphoreType.DMA((n + 1, 4)), pltpu.SemaphoreType.DMA((n + 1, 4)),
           pltpu.SemaphoreType.DMA((n, 3)), pltpu.SemaphoreType.DMA((n, 3)),
           pltpu.SemaphoreType.DMA((n, 4)), pltpu.SemaphoreType.DMA((13,))],
        compiler_params=pltpu.CompilerParams(vmem_limit_bytes=VMEM_LIMIT),
    )(*grads, svec, *ws, *ms, *vs)


SMALL_LAYOUT = [
    ("b_in", S_BIN, 16), ("ln_a_g", S_LNAG, 4), ("ln_a_b", S_LNAB, 4), ("w_spatial", S_WS, 512),
    ("b_spatial", S_BS, 4), ("conv_b_b", S_CBB, 4), ("ln_b_g", S_LNBG, 4), ("ln_b_b", S_LNBB, 4),
    ("b_out", S_BOUT, 8), ("ln1_g", S_LN1G, 8), ("ln1_b", S_LN1B, 8), ("conv_f_b", S_CFB, 44),
    ("ln2_g", S_LN2G, 8), ("ln2_b", S_LN2B, 8),
]


def small_adamw(sv_slots, ws, ms, vs):
    n = len(SMALL_LAYOUT)

    def body(*refs):
        s_ref = refs[0]
        w_refs, m_refs, v_refs = refs[1:1 + n], refs[1 + n:1 + 2 * n], refs[1 + 2 * n:1 + 3 * n]
        outs = refs[1 + 3 * n:]
        for p, (_, row0, rows) in enumerate(SMALL_LAYOUT):
            sl = pl.ds(row0, rows)
            grad = ((s_ref[0, sl, :] + s_ref[1, sl, :]) + s_ref[2, sl, :]) + s_ref[3, sl, :]
            delta, m2, v2 = _adamw(w_refs[p][...], grad, m_refs[p][...], v_refs[p][...])
            outs[p][...] = grad
            outs[n + p][...] = delta
            outs[2 * n + p][...] = m2
            outs[3 * n + p][...] = v2
        sl = pl.ds(S_LOSS, 8)
        outs[4 * n][...] = ((s_ref[0, sl, :] + s_ref[1, sl, :]) + s_ref[2, sl, :]) + s_ref[3, sl, :]

    shapes = [jax.ShapeDtypeStruct((rows, 128), F32) for _, _, rows in SMALL_LAYOUT]
    return pl.pallas_call(
        body, name="small_adamw", out_shape=shapes * 4 + [jax.ShapeDtypeStruct((8, 128), F32)],
        in_specs=[VMEM] * (1 + 3 * n), out_specs=[VMEM] * (4 * n + 1),
    )(sv_slots, *ws, *ms, *vs)


def mix_forward(x, sin, sout, b_in, ln_a_g, ln_a_b, w_spatial, bst, conv_b_w, conv_b_b, ln_b_g, ln_b_b,
                b_out, ln1_g, ln1_b, sup, sdown, tm):
    t = x.shape[0]
    nt = t // tm
    n_chunks = tm // CHUNK

    def body(x_ref, sin_ref, sout_ref, bin_ref, ga_ref, ba_ref, ws_ref, bst_ref, cw_ref, cb_ref, gb_ref,
             bb_ref, bout_ref, g1_ref, b1_ref, sup_ref, sdown_ref,
             h_ref, xhat1_ref, rstd1_ref, yb1_ref, gin_ref, gout_ref, gup_ref, gdown_ref,
             ext_ref, y_ref, wsm_ref, win_ref, wout_ref, load_sems,
             mix_send, mix_recv, mix_local, send_sems, recv_sems, local_sems):
        i = pl.program_id(0)
        mixer = ([sin_ref, sout_ref], [gin_ref, gout_ref], mix_send, mix_recv, mix_local)
        gather = ([sup_ref, sdown_ref], [gup_ref, gdown_ref], send_sems, recv_sems, local_sems)

        @pl.when(i == 0)
        def _():
            _gather_start(*mixer)
            _gather_relay(*mixer, via=[2, 2])
            _gather_finish(*mixer)
            _gather_start(*gather)
            loads = [pltpu.make_async_copy(gin_ref, win_ref, load_sems.at[0]),
                     pltpu.make_async_copy(gout_ref, wout_ref, load_sems.at[1])]
            for cp in loads:
                cp.start()
            for cp in loads:
                cp.wait()
            ext_ref[0:HALO_B, :] = jnp.zeros((HALO_B, D_B), F32)
            mask = _tril_mask()
            for hd in range(HEADS):
                wsm_ref[hd] = jnp.where(mask, ws_ref[hd], 0.0).astype(BF16)

        xb = x_ref[...].astype(BF16)
        for j in range(N_DEV):
            cols = slice(j * W_IN_BLK, (j + 1) * W_IN_BLK)
            h_ref[:, cols] = _nn(xb, win_ref[j]) + bin_ref[:, cols]

        def chunk(ci):
            r = _rows(ci, CHUNK)
            for hd in range(HEADS):
                _, _, u, _, _, _, _, _, sv = _mixer_a_head(h_ref, r, hd, ga_ref, ba_ref, wsm_ref, bst_ref)
                y_ref[r, hd * HEAD_DIM:(hd + 1) * HEAD_DIM] = (u * sv).astype(BF16)
            a_b = h_ref[r, 2 * D_A:2 * D_A + D_B]
            g_b = h_ref[r, 2 * D_A + D_B:D_IN]
            ext_ref[pl.ds(HALO_B + ci * CHUNK, CHUNK), :] = a_b * _sigmoid(g_b)

        _loop(n_chunks, chunk)

        def conv_rows(bi):
            base = bi * ROWS
            yb1 = _conv_b_block(ext_ref, base, cw_ref) + cb_ref[...]
            yb1_ref[pl.ds(base, ROWS), :] = yb1
            xhat, _ = _ln_stats(yb1)
            yb2 = xhat * gb_ref[...] + bb_ref[...]
            y_ref[pl.ds(base, ROWS), D_A:D] = (yb2 * _sigmoid(yb2)).astype(BF16)

        _loop(tm // ROWS, conv_rows)
        ext_ref[0:HALO_B, :] = ext_ref[tm:tm + HALO_B, :]

        mix = _nn(y_ref[...], wout_ref[...].reshape(D, D)) + bout_ref[...]
        xhat1, rstd1 = _ln_stats(ALPHA * x_ref[...] + mix)
        xhat1_ref[...] = xhat1
        rstd1_ref[...] = jnp.broadcast_to(rstd1, (tm, 128))

        @pl.when(i == (5 * nt) // 8)
        def _():
            _gather_relay(*gather, via=[1, 2])

        @pl.when(i == nt - 1)
        def _():
            _gather_finish(*gather)

    row = lambda w: pl.BlockSpec((tm, w), lambda i: (i, 0))
    return pl.pallas_call(
        body, name="mix_forward", grid=(nt,),
        in_specs=[row(D), ANY, ANY, _full(b_in.shape), _full(ln_a_g.shape),
                  _full(ln_a_b.shape), _full(w_spatial.shape), _full(bst.shape),
                  _full(conv_b_w.shape), _full(conv_b_b.shape), _full(ln_b_g.shape),
                  _full(ln_b_b.shape), _full(b_out.shape),
                  _full(ln1_g.shape), _full(ln1_b.shape), ANY, ANY],
        out_specs=[row(D_IN), row(D), row(128), row(D_B), ANY, ANY, ANY, ANY],
        out_shape=[jax.ShapeDtypeStruct((t, D_IN), F32), jax.ShapeDtypeStruct((t, D), F32),
                   jax.ShapeDtypeStruct((t, 128), F32), jax.ShapeDtypeStruct((t, D_B), F32)]
        + [jax.ShapeDtypeStruct((N_DEV,) + sh.shape, BF16) for sh in (sin, sout, sup, sdown)],
        scratch_shapes=[pltpu.VMEM((tm + HALO_B, D_B), F32), pltpu.VMEM((tm, D), BF16),
                        pltpu.VMEM((HEADS, CHUNK, CHUNK), BF16),
                        pltpu.VMEM((N_DEV,) + sin.shape, BF16), pltpu.VMEM((N_DEV,) + sout.shape, BF16),
                        pltpu.SemaphoreType.DMA((2,))] + _gather_scratch(2) + _gather_scratch(2),
        compiler_params=_params(("arbitrary",)),
    )(x, sin, sout, b_in, ln_a_g, ln_a_b, w_spatial, bst, conv_b_w, conv_b_b, ln_b_g, ln_b_b,
      b_out, ln1_g, ln1_b, sup, sdown)


def ffn_forward(xhat1, ln1_g, ln1_b, wup_g, cfw, cfb, wdown, ln2_g, ln2_b, target, tm):
    t = xhat1.shape[0]
    nt = t // tm

    def body(xh_ref, g1_ref, b1_ref, wup_ref, cfw_ref, cfb_ref, wdown_ref, g2_ref, b2_ref, tgt_ref,
             hu_ref, gv_ref, dr2_ref, loss_ref, sln2_ref,
             x1_ref, x1b_ref, hu32_ref, carry_ref, gbuf_ref, ffn_ref, acc_loss, acc_g2, acc_b2):
        i = pl.program_id(0)

        @pl.when(i == 0)
        def _():
            carry_ref[...] = jnp.zeros(carry_ref.shape, F32)
            acc_loss[...] = jnp.zeros(acc_loss.shape, F32)
            acc_g2[...] = jnp.zeros(acc_g2.shape, F32)
            acc_b2[...] = jnp.zeros(acc_b2.shape, F32)

        x1 = xh_ref[...] * g1_ref[...] + b1_ref[...]
        x1_ref[...] = x1
        x1b_ref[...] = x1.astype(BF16)

        def conv(g, j, base):
            if base == 0:
                win = jnp.concatenate([carry_ref[j], hu32_ref[g, 0:ROWS, :]], axis=0)
            else:
                win = hu32_ref[g, base - HALO_F:base + ROWS, :]
            taps = _taps_f(win)
            w = cfw_ref[j]
            return sum(taps[k] * w[k:k + 1, :] for k in range(KF)) + cfb_ref[j:j + 1, :]

        for f in range(N_F):
            hu32_ref[0] = _nn(x1b_ref[...], wup_ref[f])
            hu32_ref[1] = _nn(x1b_ref[...], wup_ref[N_F + f])

            def rows(bi, f=f):
                r = _rows(bi)
                gate = conv(0, f, bi * ROWS)
                val = conv(1, N_F + f, bi * ROWS)
                gbuf_ref[r, :] = (gate * _sigmoid(gate) * val).astype(BF16)
                gv_ref[f, r, :] = gate.astype(BF16)
                gv_ref[N_F + f, r, :] = val.astype(BF16)
                hu_ref[f, r, :] = hu32_ref[0, r, :].astype(BF16)
                hu_ref[N_F + f, r, :] = hu32_ref[1, r, :].astype(BF16)

            _loop(tm // ROWS, rows)
            carry_ref[f] = hu32_ref[0, tm - HALO_F:tm, :]
            carry_ref[N_F + f] = hu32_ref[1, tm - HALO_F:tm, :]
            part = _nn(gbuf_ref[...], wdown_ref[f])
            if f == 0:
                ffn_ref[...] = part
            else:
                ffn_ref[...] += part

        def tail(bi):
            r = _rows(bi, LN_ROWS)
            xhat2, rstd2 = _ln_stats(ALPHA * x1_ref[r, :] + ffn_ref[r, :])
            err = xhat2 * g2_ref[...] + b2_ref[...] - tgt_ref[r, :]
            e2 = _rsum8(err * err)
            acc_loss[...] += sum(e2[:, k * 128:(k + 1) * 128] for k in range(D // 128))
            dy = err * (1.0 / D)
            acc_g2[...] += _rsum8(dy * xhat2)
            acc_b2[...] += _rsum8(dy)
            dr2_ref[r, :] = _ln_bwd(dy * g2_ref[...], xhat2, rstd2)

        _loop(tm // LN_ROWS, tail)
        loss_ref[...] = acc_loss[...]

        @pl.when(i == nt - 1)
        def _():
            dg = jnp.sum(acc_g2[...], axis=0, keepdims=True)
            db = jnp.sum(acc_b2[...], axis=0, keepdims=True)
            for k in range(D // 128):
                sln2_ref[k:k + 1, :] = dg[:, k * 128:(k + 1) * 128]
                sln2_ref[8 + k:9 + k, :] = db[:, k * 128:(k + 1) * 128]

    row = pl.BlockSpec((tm, D), lambda i: (i, 0))
    return pl.pallas_call(
        body, name="ffn_forward", grid=(nt,),
        in_specs=[row, _full(ln1_g.shape), _full(ln1_b.shape), _resident(wup_g.shape),
                  _full(cfw.shape), _full(cfb.shape), _resident(wdown.shape),
                  _full(ln2_g.shape), _full(ln2_b.shape), row],
        out_specs=[pl.BlockSpec((N_DEV, tm, W_UP_BLK), lambda i: (0, i, 0)),
                   pl.BlockSpec((N_DEV, tm, W_UP_BLK), lambda i: (0, i, 0)), row,
                   _full((8, 128)), _full((16, 128))],
        out_shape=[jax.ShapeDtypeStruct((N_DEV, t, W_UP_BLK), BF16),
                   jax.ShapeDtypeStruct((N_DEV, t, W_UP_BLK), BF16), jax.ShapeDtypeStruct((t, D), F32),
                   jax.ShapeDtypeStruct((8, 128), F32), jax.ShapeDtypeStruct((16, 128), F32)],
        scratch_shapes=[pltpu.VMEM((tm, D), F32), pltpu.VMEM((tm, D), BF16),
                        pltpu.VMEM((2, tm, W_UP_BLK), F32),
                        pltpu.VMEM((N_DEV, HALO_F, W_UP_BLK), F32), pltpu.VMEM((tm, W_UP_BLK), BF16),
                        pltpu.VMEM((tm, D), F32), pltpu.VMEM((8, 128), F32),
                        pltpu.VMEM((8, D), F32), pltpu.VMEM((8, D), F32)],
        compiler_params=_params(("arbitrary",)),
    )(xhat1, ln1_g, ln1_b, wup_g, cfw, cfb, wdown, ln2_g, ln2_b, target)


def ffn_backward(order, dr2, xhat1, ln1_g, ln1_b, hu, gv, wup_g, cfw, wdown, tm):
    t = dr2.shape[0]
    nt = t // tm
    sub_rows = tm
    hu4 = hu.reshape(2, N_F, t, W_UP_BLK)
    gv4 = gv.reshape(2, N_F, t, W_UP_BLK)
    wup4 = wup_g.reshape(2, N_F, D, W_UP_BLK)
    cfw4 = cfw.reshape(2, N_F, KF, W_UP_BLK)

    def body(order_ref, dr2_ref, xh_ref, g1_ref, b1_ref, hu_ref, gv_ref, wup_ref, cfw_ref, wdown_ref,
             dwup_ref, dwdown_ref, dcfw_ref, dcfb_ref, dx1_ref, land_up_ref, land_down_ref,
             x1b_ref, drb_ref, dg_ref, dextg_ref, dextv_ref, gbuf_ref,
             dhug_ref, dhuv_ref, acc_wup, acc_wdown, acc_cfw, acc_cfb, sem, send_sems, recv_sems):
        fo = pl.program_id(0)
        f = order_ref[fo]
        f_prev = order_ref[jnp.maximum(fo - 1, 0)]
        slot = fo % 2
        i = pl.program_id(1)
        x, y, c = _mesh_pos()
        half = D_FF // N_DEV

        def to_sibling(fi, k, src, land_ref, shard_chip):
            d = jnp.bitwise_xor(shard_chip, 2 * x + y)
            slot = jnp.where(d == 1, 2, jnp.where(d == 2, 1, d))
            return pltpu.make_async_remote_copy(
                src_ref=src, dst_ref=land_ref.at[slot], send_sem=send_sems.at[fi, k], recv_sem=recv_sems.at[fi, k],
                device_id=(x, y, 1 - c), device_id_type=MESH)

        def up_copy(fi, g):
            return to_sibling(fi, g, dwup_ref.at[g, fi], land_up_ref, 2 * g + fi // 2)

        def down_copy(fi):
            return to_sibling(fi, 2, dwdown_ref.at[fi, pl.ds((1 - c) * half, half)], land_down_ref, fi)

        def flush(fi, s):
            return [pltpu.make_async_copy(acc_wup.at[s, 0], dwup_ref.at[0, fi], sem.at[s, 0]),
                    pltpu.make_async_copy(acc_wup.at[s, 1], dwup_ref.at[1, fi], sem.at[s, 1]),
                    pltpu.make_async_copy(acc_wdown.at[s], dwdown_ref.at[fi], sem.at[s, 2])]

        def flushed(fi, s):
            for cp in flush(fi, s):
                cp.wait()
            down_copy(fi).start()

            @pl.when(fi % 2 != c)
            def _():
                up_copy(fi, 0).start()
                up_copy(fi, 1).start()

        @pl.when(i == 0)
        def _():
            acc_wup[slot] = jnp.zeros(acc_wup.shape[1:], F32)
            acc_wdown[slot] = jnp.zeros(acc_wdown.shape[1:], F32)
            acc_cfw[...] = jnp.zeros(acc_cfw.shape, F32)
            acc_cfb[...] = jnp.zeros(acc_cfb.shape, F32)
            dextg_ref[tm:tm + HALO_F, :] = jnp.zeros((HALO_F, W_UP_BLK), F32)
            dextv_ref[tm:tm + HALO_F, :] = jnp.zeros((HALO_F, W_UP_BLK), F32)

        w = [cfw_ref[0, 0], cfw_ref[1, 0]]
        dext = [dextg_ref, dextv_ref]
        dhu = [dhug_ref, dhuv_ref]

        def rows1(bi):
            r = _rows(bi)
            gate = gv_ref[0, 0, r, :].astype(F32)
            val = gv_ref[1, 0, r, :].astype(F32)
            sg = _sigmoid(gate)
            silu = gate * sg
            gbuf_ref[r, :] = (silu * val).astype(BF16)
            dg = dg_ref[r, :]
            dgate = dg * val * (sg * (1.0 + gate * (1.0 - sg)))
            dval = dg * silu
            dextg_ref[r, :] = dgate
            dextv_ref[r, :] = dval
            acc_cfb[0:8, :] += _rsum8(dgate)
            acc_cfb[8:16, :] += _rsum8(dval)

        def rows2(bi):
            r = _rows(bi)
            for g in range(2):
                win = dext[g][pl.ds(bi * ROWS, ROWS + HALO_F), :]
                n = ROWS + HALO_F
                later = [pltpu.roll(win, n - 2, 0)[0:ROWS, :], pltpu.roll(win, n - 1, 0)[0:ROWS, :],
                         win[0:ROWS, :]]
                d = sum(later[k] * w[g][k:k + 1, :] for k in range(KF))
                dhu[g][r, :] = d.astype(BF16)
                pre = hu_ref[g, 0, r, :].astype(F32)
                for k in range(KF):
                    r0 = 8 * (g * KF + k)
                    acc_cfw[r0:r0 + 8, :] += _rsum8(later[k] * pre)

        for sub in reversed(range(tm // sub_rows)):
            rs = slice(sub * sub_rows, (sub + 1) * sub_rows)
            blocks = range(sub * sub_rows // ROWS, (sub + 1) * sub_rows // ROWS)
            x1b_ref[rs, :] = (xh_ref[rs, :] * g1_ref[...] + b1_ref[...]).astype(BF16)
            drb_ref[rs, :] = dr2_ref[rs, :].astype(BF16)
            dg_ref[rs, :] = _nt(drb_ref[rs, :], wdown_ref[0])
            for bi in blocks:
                rows1(bi)
            for bi in blocks:
                rows2(bi)
            acc_wdown[slot] += _tn(gbuf_ref[rs, :], drb_ref[rs, :])
            acc_wup[slot, 0] += _tn(dhug_ref[rs, :], x1b_ref[rs, :])
            acc_wup[slot, 1] += _tn(dhuv_ref[rs, :], x1b_ref[rs, :])
            dx1_ref[0, rs, :] = (_nt(dhug_ref[rs, :], wup_ref[0, 0])
                                 + _nt(dhuv_ref[rs, :], wup_ref[1, 0])).astype(BF16)
        dextg_ref[tm:tm + HALO_F, :] = dextg_ref[0:HALO_F, :]
        dextv_ref[tm:tm + HALO_F, :] = dextv_ref[0:HALO_F, :]

        @pl.when(i == nt - 1)
        def _():
            for g in range(2):
                dcfb_ref[g, 0] = jnp.sum(acc_cfb[8 * g:8 * g + 8, :], axis=0, keepdims=True)
                for k in range(KF):
                    r0 = 8 * (g * KF + k)
                    dcfw_ref[g, 0, k:k + 1, :] = jnp.sum(acc_cfw[r0:r0 + 8, :], axis=0, keepdims=True)
            for cp in flush(f, slot):
                cp.start()

        @pl.when((i == 0) & (fo > 0))
        def _():
            flushed(f_prev, 1 - slot)

        @pl.when((i == nt - 1) & (fo == N_F - 1))
        def _():
            flushed(f, slot)
            for fi in range(N_F):
                down_copy(fi).wait()
                for g in range(2):
                    @pl.when(fi % 2 != c)
                    def _():
                        up_copy(fi, g).wait_send()

                    @pl.when(fi % 2 == c)
                    def _():
                        up_copy(fi, g).wait_recv()

    rev = lambda i: nt - 1 - i
    row = pl.BlockSpec((tm, D), lambda fo, i, o: (rev(i), 0))
    pair = lambda r, c: pl.BlockSpec((2, 1, r, c), lambda fo, i, o: (0, o[fo], 0, 0))
    tile = pl.BlockSpec((2, 1, tm, W_UP_BLK), lambda fo, i, o: (0, o[fo], rev(i), 0))
    return pl.pallas_call(
        body, name="ffn_backward",
        grid_spec=pltpu.PrefetchScalarGridSpec(
            num_scalar_prefetch=1, grid=(N_F, nt),
            in_specs=[row, row, _full(ln1_g.shape), _full(ln1_b.shape), tile, tile,
                      pair(D, W_UP_BLK), pair(KF, W_UP_BLK),
                      pl.BlockSpec((1, W_UP_BLK, D), lambda fo, i, o: (o[fo], 0, 0))],
            out_specs=[ANY, ANY, pair(KF, W_UP_BLK), pair(1, W_UP_BLK),
                       pl.BlockSpec((1, tm, D), lambda fo, i, o: (o[fo], rev(i), 0)), ANY, ANY],
            scratch_shapes=[pltpu.VMEM((tm, D), BF16), pltpu.VMEM((tm, D), BF16),
                            pltpu.VMEM((tm, W_UP_BLK), F32),
                            pltpu.VMEM((tm + HALO_F, W_UP_BLK), F32), pltpu.VMEM((tm + HALO_F, W_UP_BLK), F32),
                            pltpu.VMEM((tm, W_UP_BLK), BF16), pltpu.VMEM((tm, W_UP_BLK), BF16),
                            pltpu.VMEM((tm, W_UP_BLK), BF16),
                            pltpu.VMEM((2, 2, W_UP_BLK, D), F32), pltpu.VMEM((2, W_UP_BLK, D), F32),
                            pltpu.VMEM((2 * KF * 8, W_UP_BLK), F32), pltpu.VMEM((16, W_UP_BLK), F32),
                            pltpu.SemaphoreType.DMA((2, 3)),
                            pltpu.SemaphoreType.DMA((N_F, 3)), pltpu.SemaphoreType.DMA((N_F, 3))]),
        out_shape=[jax.ShapeDtypeStruct((2, N_F, W_UP_BLK, D), F32),
                   jax.ShapeDtypeStruct((N_F, W_UP_BLK, D), F32),
                   jax.ShapeDtypeStruct((2, N_F, KF, W_UP_BLK), F32),
                   jax.ShapeDtypeStruct((2, N_F, 1, W_UP_BLK), F32),
                   jax.ShapeDtypeStruct((N_F, t, D), BF16),
                   jax.ShapeDtypeStruct((4, W_UP_BLK, D), F32),
                   jax.ShapeDtypeStruct((4, D_FF // N_DEV, D), F32)],
        compiler_params=_params(("arbitrary", "arbitrary")),
    )(order, dr2, xhat1, ln1_g, ln1_b, hu4, gv4, wup4, cfw4, wdown)


def mix_backward(x, h, yb1, dx1p, dr2, xhat1, rstd1, win_g, ln_a_g, ln_a_b, w_spatial, bst,
                 conv_b_w, ln_b_g, ln_b_b, wout, ln1_g, ffn_partials, tm):
    t = x.shape[0]
    n_p = len(ffn_partials)
    nt = t // tm
    n_chunks = tm // CHUNK
    halo_blocks = tm // HALO_B

    def body(x_ref, h_ref, halo_ref, yb1_ref, dx1p_ref, dr2_ref, xh1_ref, rstd1_ref, win_ref, ga_ref, ba_ref,
             ws_ref, bst_ref, cw_ref, gb_ref, bb_ref, wout_ref, g1_ref, *rest):
        p_refs, rest = rest[:n_p], rest[n_p:]
        gx_ref, dwin_ref, dwout_ref, dcw_ref, small_ref = rest[:5]
        land_refs, rest = rest[5:5 + n_p], rest[5 + n_p:]
        (ext_ref, dext_ref, y_ref, dy_ref, dh_ref, dmb_ref, wsm_ref,
         acc_win, acc_wout, acc_bin, acc_lnag, acc_lnab, acc_ws, acc_bs, acc_cbb, acc_lnbg,
         acc_lnbb, acc_bout, acc_ln1g, acc_ln1b, acc_cw, sem, send_sems, recv_sems) = rest
        i = pl.program_id(0)

        @pl.when(i == 0)
        def _():
            for cp in _chip_copies(p_refs, land_refs, send_sems, recv_sems):
                cp.start()

        first_tile = i == nt - 1
        accs = [acc_win, acc_wout, acc_bin, acc_lnag, acc_lnab, acc_ws, acc_bs, acc_cbb, acc_lnbg,
                acc_lnbb, acc_bout, acc_ln1g, acc_ln1b, acc_cw]

        @pl.when(i == 0)
        def _():
            for acc in accs:
                acc[...] = jnp.zeros(acc.shape, F32)
            dext_ref[tm:tm + HALO_B, :] = jnp.zeros((HALO_B, D_B), F32)
            mask = _tril_mask()
            for hd in range(HEADS):
                wsm_ref[hd] = jnp.where(mask, ws_ref[hd], 0.0).astype(BF16)

        def ln1_rows(bi):
            r = _rows(bi, LN_ROWS)
            part = [dx1p_ref[f, r, :].astype(F32) for f in range(N_F)]
            dx1 = ALPHA * dr2_ref[r, :] + ((part[0] + part[1]) + (part[2] + part[3]))
            xhat = xh1_ref[r, :]
            acc_ln1g[...] += _rsum8(dx1 * xhat)
            acc_ln1b[...] += _rsum8(dx1)
            dr1 = _ln_bwd(dx1 * g1_ref[...], xhat, rstd1_ref[r, 0:1])
            acc_bout[...] += _rsum8(dr1)
            gx_ref[r, :] = ALPHA * dr1
            dmb_ref[r, :] = dr1.astype(BF16)

        _loop(tm // LN_ROWS, ln1_rows)
        dy_ref[...] = _nt(dmb_ref[...], wout_ref[...])

        ha = halo_ref[:, 0:D_B]
        hg = halo_ref[:, D_B:2 * D_B]
        ext_ref[0:HALO_B, :] = jnp.where(first_tile, 0.0, 1.0) * (ha * _sigmoid(hg))

        def chunk(ci):
            r = _rows(ci, CHUNK)
            for hd in range(HEADS):
                sl = slice(hd * HEAD_DIM, (hd + 1) * HEAD_DIM)
                rows8 = slice(8 * hd, 8 * hd + 8)
                hus, hvs, u, cdf_u, cdf_v, xhat, rstd, vn, sv = _mixer_a_head(
                    h_ref, r, hd, ga_ref, ba_ref, wsm_ref, bst_ref)
                dy_a = dy_ref[r, sl]
                y_ref[r, sl] = (u * sv).astype(BF16)
                du = dy_a * sv
                dsv = dy_a * u
                dsvb = dsv.astype(BF16)
                acc_bs[hd] += dsv
                acc_ws[hd] += _nt(dsvb, vn)
                dvn = _tn(wsm_ref[hd], dsvb)
                acc_lnag[rows8, :] += _rsum8(dvn * xhat)
                acc_lnab[rows8, :] += _rsum8(dvn)
                dv = _ln_bwd(dvn * ga_ref[hd:hd + 1, :], xhat, rstd)
                slv = slice(D_A + hd * HEAD_DIM, D_A + (hd + 1) * HEAD_DIM)
                dhu = du * (cdf_u + hus * jnp.exp(-0.5 * hus * hus) * INV_SQRT_2PI)
                dhv = dv * (cdf_v + hvs * jnp.exp(-0.5 * hvs * hvs) * INV_SQRT_2PI)
                acc_bin[:, sl] += _rsum8(dhu)
                acc_bin[:, slv] += _rsum8(dhv)
                dh_ref[r, sl] = dhu.astype(BF16)
                dh_ref[r, slv] = dhv.astype(BF16)
            a_b = h_ref[r, 2 * D_A:2 * D_A + D_B]
            g_b = h_ref[r, 2 * D_A + D_B:D_IN]
            ext_ref[pl.ds(HALO_B + ci * CHUNK, CHUNK), :] = a_b * _sigmoid(g_b)

        _loop(n_chunks, chunk)

        def conv_rows(bi):
            base = bi * ROWS
            r = pl.ds(base, ROWS)
            xhat, rstd = _ln_stats(yb1_ref[r, :])
            yb2 = xhat * gb_ref[...] + bb_ref[...]
            sg = _sigmoid(yb2)
            y_ref[r, D_A:D] = (yb2 * sg).astype(BF16)
            dyb2 = dy_ref[r, D_A:D] * (sg * (1.0 + yb2 * (1.0 - sg)))
            acc_lnbg[...] += _rsum8(dyb2 * xhat)
            acc_lnbb[...] += _rsum8(dyb2)
            dyb1 = _ln_bwd(dyb2 * gb_ref[...], xhat, rstd)
            acc_cbb[...] += _rsum8(dyb1)
            dext_ref[r, :] = dyb1
            for k, tap in _taps(ext_ref[pl.ds(base, ROWS + HALO_B), :], CONV_B_OFFSETS):
                acc_cw[8 * k:8 * k + 8, :] += _rsum8(dyb1 * tap)

        _loop(tm // ROWS, conv_rows)

        def convt_rows(bi):
            base = bi * ROWS
            r = pl.ds(base, ROWS)
            dyb0 = jnp.zeros((ROWS, D_B), F32)
            for k, tap in _taps(dext_ref[pl.ds(base, ROWS + HALO_B), :], CONV_B_T_OFFSETS):
                dyb0 = dyb0 + tap * cw_ref[k:k + 1, :]
            a_b = h_ref[r, 2 * D_A:2 * D_A + D_B]
            sg = _sigmoid(h_ref[r, 2 * D_A + D_B:D_IN])
            da_b = dyb0 * sg
            dg_b = dyb0 * a_b * sg * (1.0 - sg)
            acc_bin[:, 2 * D_A:2 * D_A + D_B] += _rsum8(da_b)
            acc_bin[:, 2 * D_A + D_B:D_IN] += _rsum8(dg_b)
            dh_ref[r, 2 * D_A:2 * D_A + D_B] = da_b.astype(BF16)
            dh_ref[r, 2 * D_A + D_B:D_IN] = dg_b.astype(BF16)

        _loop(tm // ROWS, convt_rows)
        dext_ref[tm:tm + HALO_B, :] = dext_ref[0:HALO_B, :]

        acc_wout[...] += _tn(y_ref[...], dmb_ref[...])
        xt = x_ref[...].T.astype(BF16)
        dh_blocks = [dh_ref[:, j * W_IN_BLK:(j + 1) * W_IN_BLK] for j in range(N_DEV)]
        for j in range(N_DEV):
            acc_win[j] += _nn(xt, dh_blocks[j])
        gx_ref[...] += sum(_nt(dh_blocks[j], win_ref[j]) for j in range(N_DEV))

        @pl.when(i == nt - 1)
        def _():
            cps = [pltpu.make_async_copy(acc_win, dwin_ref, sem.at[0]),
                   pltpu.make_async_copy(acc_wout, dwout_ref, sem.at[1])]
            for cp in cps:
                cp.start()
            small_ref[...] = jnp.zeros(small_ref.shape, F32)

            def put_row_vector(row0, acc):
                vec = jnp.sum(acc[...], axis=0, keepdims=True)
                for k in range(vec.shape[1] // 128):
                    small_ref[row0 + k:row0 + k + 1, :] = vec[:, k * 128:(k + 1) * 128]

            put_row_vector(S_BIN, acc_bin)
            put_row_vector(S_CBB, acc_cbb)
            put_row_vector(S_LNBG, acc_lnbg)
            put_row_vector(S_LNBB, acc_lnbb)
            put_row_vector(S_BOUT, acc_bout)
            put_row_vector(S_LN1G, acc_ln1g)
            put_row_vector(S_LN1B, acc_ln1b)
            mask = _tril_mask()
            for hd in range(HEADS):
                rows8 = slice(8 * hd, 8 * hd + 8)
                small_ref[S_LNAG + hd:S_LNAG + hd + 1, :] = jnp.sum(acc_lnag[rows8, :], axis=0, keepdims=True)
                small_ref[S_LNAB + hd:S_LNAB + hd + 1, :] = jnp.sum(acc_lnab[rows8, :], axis=0, keepdims=True)
                small_ref[S_WS + hd * CHUNK:S_WS + (hd + 1) * CHUNK, :] = jnp.where(mask, acc_ws[hd], 0.0)
                small_ref[S_BS + hd:S_BS + hd + 1, :] = jnp.sum(acc_bs[hd].T, axis=0, keepdims=True)
            for k in range(KB):
                dcw_ref[k:k + 1, :] = jnp.sum(acc_cw[8 * k:8 * k + 8, :], axis=0, keepdims=True)
            for cp in cps:
                cp.wait()
            for cp in _chip_copies(p_refs, land_refs, send_sems, recv_sems):
                cp.wait()

    rev = lambda i: nt - 1 - i
    row = lambda w: pl.BlockSpec((tm, w), lambda i: (rev(i), 0))
    return pl.pallas_call(
        body, name="mix_backward", grid=(nt,),
        in_specs=[row(D), row(D_IN),
                  pl.BlockSpec((HALO_B, 2 * D_B), lambda i: (jnp.maximum(rev(i) * halo_blocks - 1, 0), 1)),
                  row(D_B), pl.BlockSpec((N_F, tm, D), lambda i: (0, rev(i), 0)),
                  row(D), row(D), row(128), _resident(win_g.shape), _full(ln_a_g.shape),
                  _full(ln_a_b.shape), _full(w_spatial.shape), _full(bst.shape), _full(conv_b_w.shape),
                  _full(ln_b_g.shape), _full(ln_b_b.shape),
                  _resident(wout.shape), _full(ln1_g.shape)] + [ANY] * n_p,
        out_specs=[row(D), ANY, ANY, _full((KB, D_B)), _full((S_MIX_ROWS, 128))] + [ANY] * n_p,
        out_shape=[jax.ShapeDtypeStruct((t, D), F32), jax.ShapeDtypeStruct((N_DEV, D, W_IN_BLK), F32),
                   jax.ShapeDtypeStruct((D, D), F32), jax.ShapeDtypeStruct((KB, D_B), F32),
                   jax.ShapeDtypeStruct((S_MIX_ROWS, 128), F32)]
        + [jax.ShapeDtypeStruct(p.shape, BF16) for p in ffn_partials],
        scratch_shapes=[pltpu.VMEM((tm + HALO_B, D_B), F32), pltpu.VMEM((tm + HALO_B, D_B), F32),
                        pltpu.VMEM((tm, D), BF16), pltpu.VMEM((tm, D), F32), pltpu.VMEM((tm, D_IN), BF16),
                        pltpu.VMEM((tm, D), BF16),
                        pltpu.VMEM((HEADS, CHUNK, CHUNK), BF16),
                        pltpu.VMEM((N_DEV, D, W_IN_BLK), F32), pltpu.VMEM((D, D), F32),
                        pltpu.VMEM((8, D_IN), F32), pltpu.VMEM((8 * HEADS, HEAD_DIM), F32),
                        pltpu.VMEM((8 * HEADS, HEAD_DIM), F32), pltpu.VMEM((HEADS, CHUNK, CHUNK), F32),
                        pltpu.VMEM((HEADS, CHUNK, CHUNK), F32), pltpu.VMEM((8, D_B), F32),
                        pltpu.VMEM((8, D_B), F32), pltpu.VMEM((8, D_B), F32), pltpu.VMEM((8, D), F32),
                        pltpu.VMEM((8, D), F32), pltpu.VMEM((8, D), F32), pltpu.VMEM((8 * KB, D_B), F32),
                        pltpu.SemaphoreType.DMA((2,)),
                        pltpu.SemaphoreType.DMA((n_p, 3)), pltpu.SemaphoreType.DMA((n_p, 3))],
        compiler_params=_params(("arbitrary",)),
    )(x, h, h, yb1, dx1p, dr2, xhat1, rstd1, win_g, ln_a_g, ln_a_b, w_spatial, bst, conv_b_w,
      ln_b_g, ln_b_b, wout, ln1_g, *ffn_partials)


def _rows128(a):
    return a.reshape(-1, 128)


def _pack_conv(cb, cf):
    lead = cb.shape[:-2]
    pad = [(0, 0)] * len(lead)
    flat = jnp.pad(cb.reshape(lead + (KB * 64,)), pad + [(0, 3 * W_UP_BLK - KB * 64)])
    rows = jnp.concatenate([cf, flat.reshape(lead + (3, W_UP_BLK))], axis=-2)
    return jnp.pad(rows, pad + [(0, 2), (0, 768 - W_UP_BLK)])


def _unpack_conv(p):
    lead = p.shape[:-2]
    cf = p[..., 0:KF, 0:W_UP_BLK]
    cb = p[..., 3:6, 0:W_UP_BLK].reshape(lead + (3 * W_UP_BLK,))[..., :KB * 64].reshape(lead + (KB, 64))
    return cb, cf


def kernel(x, w_in, b_in, ln_a_g, ln_a_b, w_spatial, b_spatial, conv_b_w, conv_b_b, ln_b_g, ln_b_b, w_out, b_out, ln1_g, ln1_b, w_up, conv_f_w, conv_f_b, w_down, ln2_g, ln2_b, loss_target, m_w_in, m_b_in, m_ln_a_g, m_ln_a_b, m_w_spatial, m_b_spatial, m_conv_b_w, m_conv_b_b, m_ln_b_g, m_ln_b_b, m_w_out, m_b_out, m_ln1_g, m_ln1_b, m_w_up, m_conv_f_w, m_conv_f_b, m_w_down, m_ln2_g, m_ln2_b, v_w_in, v_b_in, v_ln_a_g, v_ln_a_b, v_w_spatial, v_b_spatial, v_conv_b_w, v_conv_b_b, v_ln_b_g, v_ln_b_b, v_w_out, v_b_out, v_ln1_g, v_ln1_b, v_w_up, v_conv_f_w, v_conv_f_b, v_w_down, v_ln2_g, v_ln2_b):
    t = x.shape[1]
    x2 = x.reshape(t, D)
    target = loss_target.reshape(t, D)
    tm_fwd = min(t, 512)
    tm_bwd = min(t, 256)
    tm_ffn_bwd = min(t, 512)

    xi, yi, ci = _mesh_pos()
    jidx = jnp.stack([_lid(px, py, ci) for px, py in _chip_patterns(xi, yi)]).astype(jnp.int32)

    sin, sout, sup, sdown, conv_g = prepare_weights(w_in, w_out, w_up.T, w_down, _pack_conv(conv_b_w, conv_f_w))
    conv_b_all, cfw = _unpack_conv(conv_g)
    conv_b_full = conv_b_all.transpose(1, 0, 2).reshape(KB, D_B)
    cfb = conv_f_b.reshape(N_DEV, W_UP_BLK)
    row = lambda a: a.reshape(1, -1)
    bst = b_spatial.T

    h, xhat1, rstd1, yb1, win_g, wout_g, wup_g, wdown_g = mix_forward(
        x2, sin, sout, row(b_in), ln_a_g, ln_a_b, w_spatial, bst, conv_b_full, row(conv_b_b),
        row(ln_b_g), row(ln_b_b), row(b_out), row(ln1_g), row(ln1_b), sup, sdown, tm_fwd)
    wout_full = wout_g.reshape(D, D)
    wdown4 = wdown_g.reshape(N_F, W_UP_BLK, D)
    hu, gv, dr2, loss_part, s_ln2 = ffn_forward(
        xhat1, row(ln1_g), row(ln1_b), wup_g, cfw, cfb, wdown4, row(ln2_g), row(ln2_b), target, tm_bwd)

    order = jnp.where(ci == 0, jnp.array([1, 3, 0, 2], jnp.int32), jnp.array([0, 2, 1, 3], jnp.int32))
    dwup, dwdown, dcfw, dcfb, dx1p, *ffn_lands = ffn_backward(
        order, dr2, xhat1, row(ln1_g), row(ln1_b), hu, gv, wup_g, cfw, wdown4, tm_ffn_bwd)
    ffn_grads = [dwup.reshape(N_DEV, W_UP_BLK, D), dwdown.reshape(N_DEV, D_FF // N_DEV, D)]
    ffn_partials = [chip_partials("chip_partials_" + nm, g, l, jidx, rb)
                    for nm, g, l, rb in zip(["w_up", "w_down"], ffn_grads, ffn_lands, [352, 352])]
    grad_x, dwin, dwout, dcw, s_mix, *ffn_recvs = mix_backward(
        x2, h, yb1, dx1p, dr2, xhat1, rstd1, win_g, ln_a_g, ln_a_b, w_spatial, bst,
        conv_b_full, row(ln_b_g), row(ln_b_b), wout_full, row(ln1_g), ffn_partials, tm_bwd)

    dcfb_rows = jnp.pad(dcfb.reshape(-1, 128), ((0, 4), (0, 0)))
    svec = jnp.concatenate([s_mix, dcfb_rows, s_ln2, loss_part], axis=0)
    dconv = _pack_conv(dcw.reshape(KB, N_DEV, 64).transpose(1, 0, 2), dcfw.reshape(N_DEV, KF, W_UP_BLK))
    mix_grads = [dwin, dwout.reshape(N_DEV, D // N_DEV, D), dconv]
    mix_w = [w_in, w_out, _pack_conv(conv_b_w, conv_f_w)]
    mix_m = [m_w_in, m_w_out, _pack_conv(m_conv_b_w, m_conv_f_w)]
    mix_v = [v_w_in, v_w_out, _pack_conv(v_conv_b_w, v_conv_f_w)]
    *mix_out, sv_slots = mixer_reduce_adamw(mix_grads, svec, mix_w, mix_m, mix_v)
    big = {nm: [mix_out[k * 3 + p] for k in range(4)] for p, nm in enumerate(["w_in", "w_out", "conv"])}

    ffn_w = [(w_up.T, m_w_up.T, v_w_up.T), (w_down, m_w_down, v_w_down)]
    for nm, g, l, r, (w, m, v) in zip(["w_up", "w_down"], ffn_grads, ffn_lands, ffn_recvs, ffn_w):
        big[nm] = reduce_and_adamw("reduce_adamw_" + nm, g, l, r, w, m, v, jidx, 352)
    big["w_up"] = [o.T for o in big["w_up"]]
    for k in range(4):
        cb_k, cf_k = _unpack_conv(big["conv"][k])
        big.setdefault("conv_b_w", []).append(cb_k)
        big.setdefault("conv_f_w", []).append(cf_k)

    small_w = dict(b_in=b_in, ln_a_g=ln_a_g, ln_a_b=ln_a_b, w_spatial=w_spatial, b_spatial=b_spatial,
                   conv_b_b=conv_b_b, ln_b_g=ln_b_g, ln_b_b=ln_b_b, b_out=b_out, ln1_g=ln1_g,
                   ln1_b=ln1_b, conv_f_b=conv_f_b, ln2_g=ln2_g, ln2_b=ln2_b)
    small_m = dict(b_in=m_b_in, ln_a_g=m_ln_a_g, ln_a_b=m_ln_a_b, w_spatial=m_w_spatial,
                   b_spatial=m_b_spatial, conv_b_b=m_conv_b_b, ln_b_g=m_ln_b_g, ln_b_b=m_ln_b_b,
                   b_out=m_b_out, ln1_g=m_ln1_g, ln1_b=m_ln1_b, conv_f_b=m_conv_f_b, ln2_g=m_ln2_g,
                   ln2_b=m_ln2_b)
    small_v = dict(b_in=v_b_in, ln_a_g=v_ln_a_g, ln_a_b=v_ln_a_b, w_spatial=v_w_spatial,
                   b_spatial=v_b_spatial, conv_b_b=v_conv_b_b, ln_b_g=v_ln_b_g, ln_b_b=v_ln_b_b,
                   b_out=v_b_out, ln1_g=v_ln1_g, ln1_b=v_ln1_b, conv_f_b=v_conv_f_b, ln2_g=v_ln2_g,
                   ln2_b=v_ln2_b)
    order = [nm for nm, _, _ in SMALL_LAYOUT]
    small_out = small_adamw(sv_slots, [_rows128(small_w[nm]) for nm in order],
                            [_rows128(small_m[nm]) for nm in order], [_rows128(small_v[nm]) for nm in order])
    n_small = len(order)
    small = {nm: [small_out[k * n_small + p].reshape(small_w[nm].shape) for k in range(4)]
             for p, nm in enumerate(order)}
    loss = jnp.sum(small_out[4 * n_small]) * (0.5 / D)

    weights = ["w_in", "b_in", "ln_a_g", "ln_a_b", "w_spatial", "b_spatial", "conv_b_w", "conv_b_b",
               "ln_b_g", "ln_b_b", "w_out", "b_out", "ln1_g", "ln1_b", "w_up", "conv_f_w", "conv_f_b",
               "w_down", "ln2_g", "ln2_b"]
    result = lambda nm, k: big[nm][k] if nm in big else small[nm][k]
    return (loss, grad_x.reshape(x.shape), *[result(nm, 0) for nm in weights],
            *[result(nm, 1) for nm in weights], *[result(nm, 2) for nm in weights],
            *[result(nm, 3) for nm in weights])
```

```python
import functools
import math

import jax
import jax.numpy as jnp
from jax import lax
from jax.experimental import pallas as pl
from jax.experimental.pallas import tpu as pltpu

F32 = jnp.float32
BF16 = jnp.bfloat16

D = 1024
D_A = 512
D_B = 512
HEADS = 4
HEAD_DIM = 128
CHUNK = 128
KB = 31
KF = 3
D_FF = 2816
D_IN = 2048
N_DEV = 8
W_IN_BLK = D_IN // N_DEV
W_UP_BLK = 2 * D_FF // N_DEV
N_F = 4
LN_EPS = 1e-5
ALPHA = 2.0 ** 0.25

ADAM_LR = 0.001
ADAM_B1 = 0.9
ADAM_B2 = 0.999
ADAM_EPS = 1e-08
ADAM_WD = 0.01
ADAM_STEP = 10

INV_SQRT2 = 1.0 / math.sqrt(2.0)
INV_SQRT_2PI = 1.0 / math.sqrt(2.0 * math.pi)

HALO_B = 32
HALO_F = 8
ROWS = 64
LN_ROWS = 32
VMEM_LIMIT = 58 * 1024 * 1024

MESH = pl.DeviceIdType.MESH
ANY = pl.BlockSpec(memory_space=pl.ANY)
VMEM = pl.BlockSpec(memory_space=pltpu.VMEM)

S_BIN, S_LNAG, S_LNAB, S_WS, S_BS, S_CBB, S_LNBG, S_LNBB, S_BOUT, S_LN1G, S_LN1B = (
    0, 16, 24, 32, 544, 552, 560, 568, 576, 584, 592)
S_MIX_ROWS = 600
S_CFB = 600
S_LN2G = 648
S_LN2B = 656
S_LOSS = 664
S_ROWS = 672


def _tn(a, b):
    return lax.dot_general(a, b, (((0,), (0,)), ((), ())), preferred_element_type=F32)


def _nt(a, b):
    return lax.dot_general(a, b, (((1,), (1,)), ((), ())), preferred_element_type=F32)


def _nn(a, b):
    return jnp.dot(a, b, preferred_element_type=F32)


def _sigmoid(x):
    return 1.0 / (1.0 + jnp.exp(-x))


def _ln_stats(x):
    mu = jnp.mean(x, axis=-1, keepdims=True)
    xc = x - mu
    var = jnp.mean(xc * xc, axis=-1, keepdims=True)
    rstd = lax.rsqrt(var + LN_EPS)
    return xc * rstd, rstd


def _ln_bwd(dxhat, xhat, rstd):
    m1 = jnp.mean(dxhat, axis=-1, keepdims=True)
    m2 = jnp.mean(dxhat * xhat, axis=-1, keepdims=True)
    return rstd * (dxhat - m1 - xhat * m2)


def _rsum8(x):
    r, n = x.shape
    return x.reshape(r // 8, 8, n).sum(axis=0)


def _rows(i, n=ROWS):
    return pl.ds(i * n, n)


def _loop(n, body):
    for i in range(n):
        body(i)


def _tril_mask():
    r = lax.broadcasted_iota(jnp.int32, (CHUNK, CHUNK), 0)
    c = lax.broadcasted_iota(jnp.int32, (CHUNK, CHUNK), 1)
    return c <= r


def _mixer_a_head(h_ref, r, hd, ga_ref, ba_ref, wsm_ref, bst_ref):
    sl = slice(hd * HEAD_DIM, (hd + 1) * HEAD_DIM)
    hu = h_ref[r, sl]
    hv = h_ref[r, D_A + hd * HEAD_DIM:D_A + (hd + 1) * HEAD_DIM]
    cdf_u = 0.5 * (1.0 + lax.erf(hu * INV_SQRT2))
    cdf_v = 0.5 * (1.0 + lax.erf(hv * INV_SQRT2))
    u = hu * cdf_u
    xhat, rstd = _ln_stats(hv * cdf_v)
    vn = (xhat * ga_ref[hd:hd + 1, :] + ba_ref[hd:hd + 1, :]).astype(BF16)
    sv = _nn(wsm_ref[hd], vn) + bst_ref[:, hd:hd + 1]
    return hu, hv, u, cdf_u, cdf_v, xhat, rstd, vn, sv


def _taps(win, offsets):
    n = win.shape[0]
    for s in range(8):
        ks = [k for k, o in enumerate(offsets) if o % 8 == s]
        if ks:
            moved = win if s == 0 else pltpu.roll(win, n - s, 0)
            for k in ks:
                yield k, moved[offsets[k] - s:offsets[k] - s + ROWS, :]


CONV_B_OFFSETS = [2 + k for k in range(KB)]
CONV_B_T_OFFSETS = [30 - k for k in range(KB)]


def _conv_b_block(ext_ref, base, cw_ref):
    acc = jnp.zeros((ROWS, D_B), F32)
    for k, tap in _taps(ext_ref[pl.ds(base, ROWS + HALO_B), :], CONV_B_OFFSETS):
        acc = acc + tap * cw_ref[k:k + 1, :]
    return acc


def _taps_f(win):
    n = ROWS + HALO_F
    return [pltpu.roll(win, n - 6, 0)[0:ROWS, :], pltpu.roll(win, n - 7, 0)[0:ROWS, :], win[8:n, :]]


def _params(sem, **kw):
    return pltpu.CompilerParams(dimension_semantics=sem, vmem_limit_bytes=VMEM_LIMIT, **kw)


def _resident(shape):
    zeros = (0,) * len(shape)
    return pl.BlockSpec(shape, lambda *_: zeros, pipeline_mode=pl.Buffered(1))


def _full(shape):
    zeros = (0,) * len(shape)
    return pl.BlockSpec(shape, lambda *_: zeros)


def _mesh_pos():
    return lax.axis_index("x"), lax.axis_index("y"), lax.axis_index("c")


def _chip_patterns(x, y):
    return [(x, y), (1 - x, y), (x, 1 - y), (1 - x, 1 - y)]


def _lid(x, y, c):
    return 4 * x + 2 * y + c


def _gather_copy(outs, send_sems, recv_sems, a, k, block, to, src=None):
    blk = outs[a].at[_lid(*block)]
    return pltpu.make_async_remote_copy(
        src_ref=blk if src is None else src, dst_ref=blk,
        send_sem=send_sems.at[a, k], recv_sem=recv_sems.at[a, k], device_id=to, device_id_type=MESH)


def _gather_start(mine, outs, send_sems, recv_sems, local_sems):
    x, y, c = _mesh_pos()
    me = (x, y, c)
    for a in range(len(mine)):
        pltpu.make_async_copy(mine[a], outs[a].at[_lid(*me)], local_sems.at[a]).start()
        for k, to in enumerate([(x, y, 1 - c), (1 - x, y, c), (x, 1 - y, c)]):
            _gather_copy(outs, send_sems, recv_sems, a, k, me, to, src=mine[a]).start()


def _gather_relay(mine, outs, send_sems, recv_sems, local_sems, via):
    x, y, c = _mesh_pos()
    me, sib = (x, y, c), (x, y, 1 - c)
    copy = functools.partial(_gather_copy, outs, send_sems, recv_sems)
    source = {1: (1 - x, y, c), 2: (x, 1 - y, c)}
    for a in range(len(mine)):
        for k in (via[a], 3 - via[a]):
            copy(a, k, source[k], me).wait_recv()
            if k == via[a]:
                copy(a, 3, source[k], source[3 - k]).start()
            copy(a, 3 + k, source[k], sib).start()


def _gather_finish(mine, outs, send_sems, recv_sems, local_sems):
    x, y, c = _mesh_pos()
    me, sib = (x, y, c), (x, y, 1 - c)
    copy = functools.partial(_gather_copy, outs, send_sems, recv_sems)
    diag = (1 - x, 1 - y)
    n = len(mine)
    for a in range(n):
        copy(a, 3, (*diag, c), me).wait_recv()
        copy(a, 6, (*diag, c), sib).start()
    for a in range(n):
        copy(a, 0, sib, me).wait_recv()
        for k, chip in zip((4, 5, 6), [(1 - x, y), (x, 1 - y), diag]):
            copy(a, k, (*chip, 1 - c), me).wait_recv()
        for k in range(7):
            copy(a, k, me, sib, src=mine[a]).wait_send()
        pltpu.make_async_copy(mine[a], outs[a].at[_lid(*me)], local_sems.at[a]).wait()


def _gather_scratch(n):
    return [pltpu.SemaphoreType.DMA((n, 7)), pltpu.SemaphoreType.DMA((n, 7)), pltpu.SemaphoreType.DMA((n,))]


def prepare_weights(w_in, w_out, w_up_t, w_down, convp):
    def body(win_ref, wout_ref, wup_ref, wdown_ref, convp_ref,
             sin_ref, sout_ref, sup_ref, sdown_ref, gconv_ref, send_sems, recv_sems, local_sems):
        gather = ([convp_ref], [gconv_ref], send_sems, recv_sems, local_sems)
        _gather_start(*gather)
        sin_ref[...] = win_ref[...].astype(BF16)
        sout_ref[...] = wout_ref[...].astype(BF16)
        sup_ref[...] = wup_ref[...].T.astype(BF16)
        sdown_ref[...] = wdown_ref[...].astype(BF16)
        _gather_relay(*gather, via=[1])
        _gather_finish(*gather)

    return pl.pallas_call(
        body, name="prepare_weights",
        out_shape=[jax.ShapeDtypeStruct(w_in.shape, BF16), jax.ShapeDtypeStruct(w_out.shape, BF16),
                   jax.ShapeDtypeStruct(w_up_t.shape[::-1], BF16), jax.ShapeDtypeStruct(w_down.shape, BF16),
                   jax.ShapeDtypeStruct((N_DEV,) + convp.shape, F32)],
        in_specs=[VMEM] * 5, out_specs=[VMEM] * 4 + [ANY],
        scratch_shapes=_gather_scratch(1),
        compiler_params=pltpu.CompilerParams(vmem_limit_bytes=VMEM_LIMIT),
    )(w_in, w_out, w_up_t, w_down, convp)


def _chip_copies(p, land, send_sems, recv_sems):
    x, y, c = _mesh_pos()
    return [pltpu.make_async_remote_copy(
        src_ref=p[a].at[k], dst_ref=land[a].at[k], send_sem=send_sems.at[a, k], recv_sem=recv_sems.at[a, k],
        device_id=(px, py, c), device_id_type=MESH)
        for k, (px, py) in enumerate(_chip_patterns(x, y)[1:]) for a in range(len(p))]


def chip_partials(name, g, land, jidx, rb):
    _, r, c = g.shape

    def body(j_ref, g_ref, l_ref, o_ref):
        o_ref[...] = (g_ref[...] + l_ref[...]).astype(BF16)

    return pl.pallas_call(
        body, name=name,
        out_shape=jax.ShapeDtypeStruct((3, r, c), BF16),
        grid_spec=pltpu.PrefetchScalarGridSpec(
            num_scalar_prefetch=1, grid=(3, r // rb),
            in_specs=[pl.BlockSpec((1, rb, c), lambda k, i, j: (j[1 + k], i, 0)),
                      pl.BlockSpec((1, rb, c), lambda k, i, j: (1 + k, i, 0))],
            out_specs=pl.BlockSpec((1, rb, c), lambda k, i, j: (k, i, 0))),
        compiler_params=_params(("arbitrary", "arbitrary")),
    )(jidx, g, land)


def _adamw(w, g, m, v):
    m2 = ADAM_B1 * m + (1.0 - ADAM_B1) * g
    v2 = ADAM_B2 * v + (1.0 - ADAM_B2) * (g * g)
    m_hat = m2 / (1.0 - ADAM_B1 ** ADAM_STEP)
    v_hat = v2 / (1.0 - ADAM_B2 ** ADAM_STEP)
    delta = -ADAM_LR * (m_hat / (jnp.sqrt(v_hat) + ADAM_EPS) + ADAM_WD * w)
    return delta, m2, v2


def reduce_and_adamw(name, g, land, recv, w, m, v, jidx, rb):
    _, r, c = g.shape

    def body(j_ref, g_ref, l_ref, r_ref, w_ref, m_ref, v_ref, go_ref, do_ref, mo_ref, vo_ref):
        grad = (g_ref[0] + l_ref[0]) + r_ref[0].astype(F32) + r_ref[1].astype(F32) + r_ref[2].astype(F32)
        delta, m2, v2 = _adamw(w_ref[...], grad, m_ref[...], v_ref[...])
        go_ref[...] = grad
        do_ref[...] = delta
        mo_ref[...] = m2
        vo_ref[...] = v2

    blk = pl.BlockSpec((rb, c), lambda i, j: (i, 0))
    return pl.pallas_call(
        body, name=name,
        out_shape=[jax.ShapeDtypeStruct((r, c), F32)] * 4,
        grid_spec=pltpu.PrefetchScalarGridSpec(
            num_scalar_prefetch=1, grid=(r // rb,),
            in_specs=[pl.BlockSpec((1, rb, c), lambda i, j: (j[0], i, 0)),
                      pl.BlockSpec((1, rb, c), lambda i, j: (0, i, 0)),
                      pl.BlockSpec((3, rb, c), lambda i, j: (0, i, 0)),
                      blk, blk, blk],
            out_specs=[blk] * 4),
        compiler_params=_params(("arbitrary",)),
    )(jidx, g, land, recv, w, m, v)


def mixer_reduce_adamw(grads, svec, ws, ms, vs):
    n = len(grads)
    shard = [g.shape[1:] for g in grads]

    def body(*refs):
        g = refs[:n]
        sv_ref = refs[n]
        w, m, v = refs[n + 1:2 * n + 1], refs[2 * n + 1:3 * n + 1], refs[3 * n + 1:4 * n + 1]
        outs = refs[4 * n + 1:8 * n + 1]
        sv_slots = refs[8 * n + 1]
        rest = refs[8 * n + 2:]
        own, land, sendb, recvb = rest[:n], rest[n:2 * n], rest[2 * n:3 * n], rest[3 * n:4 * n]
        sv_land, chip_sv, d2d_send, d2d_recv, ici_send, ici_recv, local_sems, sv_sems = rest[4 * n:]
        x, y, c = _mesh_pos()
        sib = (x, y, 1 - c)
        pats = _chip_patterns(x, y)
        q = 2 * x + y

        d2d, local = {}, {}
        for a in range(n):
            for k, (px, py) in enumerate(pats):
                d2d[a, k] = pltpu.make_async_remote_copy(
                    src_ref=g[a].at[_lid(px, py, 1 - c)], dst_ref=land[a].at[k],
                    send_sem=d2d_send.at[a, k], recv_sem=d2d_recv.at[a, k], device_id=sib, device_id_type=MESH)
                local[a, k] = pltpu.make_async_copy(g[a].at[_lid(px, py, c)], own[a].at[k], local_sems.at[a, k])
        sv_d2d = pltpu.make_async_remote_copy(
            src_ref=sv_ref, dst_ref=sv_land, send_sem=d2d_send.at[n, 0], recv_sem=d2d_recv.at[n, 0],
            device_id=sib, device_id_type=MESH)
        blocks = [(a, k) for a in range(n) for k in (1, 2, 3)] + [(a, 0) for a in range(n)]
        sv_d2d.start()
        for b in blocks:
            d2d[b].start()
            local[b].start()

        half_rows = svec.shape[0] // 2
        rows = pl.ds(pl.multiple_of(c * half_rows, 8), half_rows)
        sv_local = pltpu.make_async_copy(chip_sv, sv_slots.at[q], sv_sems.at[0])

        def sv_ici(k, slot, to):
            return pltpu.make_async_remote_copy(
                src_ref=chip_sv.at[rows], dst_ref=sv_slots.at[slot, rows], send_sem=sv_sems.at[1 + k],
                recv_sem=sv_sems.at[4 + k], device_id=to, device_id_type=MESH)

        def sv_pass_on(k, slot):
            return pltpu.make_async_remote_copy(
                src_ref=sv_slots.at[slot, rows], dst_ref=sv_slots.at[slot, rows], send_sem=sv_sems.at[7 + k],
                recv_sem=sv_sems.at[10 + k], device_id=sib, device_id_type=MESH)

        sv_d2d.wait()
        chip_sv[...] = sv_ref[...] + sv_land[...]
        sv_out = [sv_ici(k, q, (px, py, c)) for k, (px, py) in enumerate(pats[1:])]
        for cp in sv_out + [sv_local]:
            cp.start()

        ici = _chip_copies(sendb, recvb, ici_send, ici_recv)
        for a, k in blocks:
            local[a, k].wait()
            d2d[a, k].wait()
            if k > 0:
                sendb[a][k - 1] = (own[a][k] + land[a][k]).astype(BF16)
                ici[(k - 1) * n + a].start()
        for k, (px, py) in enumerate(pats[1:]):
            sv_out[k].wait_send()
            sv_ici(k, 2 * px + py, (px, py, c)).wait_recv()
            sv_pass_on(k, 2 * px + py).start()
        for cp in ici:
            cp.wait()
        for k, (px, py) in enumerate(pats[1:]):
            sv_pass_on(k, 2 * px + py).wait()
        sv_local.wait()

        for a in range(n):
            grad = ((own[a][0] + land[a][0]) + recvb[a][0].astype(F32) + recvb[a][1].astype(F32)
                    + recvb[a][2].astype(F32))
            delta, m2, v2 = _adamw(w[a][...], grad, m[a][...], v[a][...])
            outs[a][...] = grad
            outs[n + a][...] = delta
            outs[2 * n + a][...] = m2
            outs[3 * n + a][...] = v2

    shard_out = [jax.ShapeDtypeStruct(s, F32) for s in shard]
    return pl.pallas_call(
        body, name="mixer_reduce_adamw",
        out_shape=shard_out * 4 + [jax.ShapeDtypeStruct((4,) + svec.shape, F32)],
        in_specs=[ANY] * n + [VMEM] * (1 + 3 * n), out_specs=[VMEM] * (4 * n) + [ANY],
        scratch_shapes=[pltpu.VMEM((4,) + s, F32) for s in shard] + [pltpu.VMEM((4,) + s, F32) for s in shard]
        + [pltpu.VMEM((3,) + s, BF16) for s in shard] + [pltpu.VMEM((3,) + s, BF16) for s in shard]
        + [pltpu.VMEM(svec.shape, F32), pltpu.VMEM(svec.shape, F32),
           pltpu.SemaphoreType.DMA((n + 1, 4)), pltpu.SemaphoreType.DMA((n + 1, 4)),
           pltpu.SemaphoreType.DMA((n, 3)), pltpu.SemaphoreType.DMA((n, 3)),
           pltpu.SemaphoreType.DMA((n, 4)), pltpu.SemaphoreType.DMA((13,))],
        compiler_params=pltpu.CompilerParams(vmem_limit_bytes=VMEM_LIMIT),
    )(*grads, svec, *ws, *ms, *vs)


SMALL_LAYOUT = [
    ("b_in", S_BIN, 16), ("ln_a_g", S_LNAG, 4), ("ln_a_b", S_LNAB, 4), ("w_spatial", S_WS, 512),
    ("b_spatial", S_BS, 4), ("conv_b_b", S_CBB, 4), ("ln_b_g", S_LNBG, 4), ("ln_b_b", S_LNBB, 4),
    ("b_out", S_BOUT, 8), ("ln1_g", S_LN1G, 8), ("ln1_b", S_LN1B, 8), ("conv_f_b", S_CFB, 44),
    ("ln2_g", S_LN2G, 8), ("ln2_b", S_LN2B, 8),
]


def small_adamw(sv_slots, ws, ms, vs):
    n = len(SMALL_LAYOUT)

    def body(*refs):
        s_ref = refs[0]
        w_refs, m_refs, v_refs = refs[1:1 + n], refs[1 + n:1 + 2 * n], refs[1 + 2 * n:1 + 3 * n]
        outs = refs[1 + 3 * n:]
        for p, (_, row0, rows) in enumerate(SMALL_LAYOUT):
            sl = pl.ds(row0, rows)
            grad = ((s_ref[0, sl, :] + s_ref[1, sl, :]) + s_ref[2, sl, :]) + s_ref[3, sl, :]
            delta, m2, v2 = _adamw(w_refs[p][...], grad, m_refs[p][...], v_refs[p][...])
            outs[p][...] = grad
            outs[n + p][...] = delta
            outs[2 * n + p][...] = m2
            outs[3 * n + p][...] = v2
        sl = pl.ds(S_LOSS, 8)
        outs[4 * n][...] = ((s_ref[0, sl, :] + s_ref[1, sl, :]) + s_ref[2, sl, :]) + s_ref[3, sl, :]

    shapes = [jax.ShapeDtypeStruct((rows, 128), F32) for _, _, rows in SMALL_LAYOUT]
    return pl.pallas_call(
        body, name="small_adamw", out_shape=shapes * 4 + [jax.ShapeDtypeStruct((8, 128), F32)],
        in_specs=[VMEM] * (1 + 3 * n), out_specs=[VMEM] * (4 * n + 1),
    )(sv_slots, *ws, *ms, *vs)


def mix_forward(x, sin, sout, b_in, ln_a_g, ln_a_b, w_spatial, bst, conv_b_w, conv_b_b, ln_b_g, ln_b_b,
                b_out, ln1_g, ln1_b, sup, sdown, tm):
    t = x.shape[0]
    nt = t // tm
    n_chunks = tm // CHUNK

    def body(x_ref, sin_ref, sout_ref, bin_ref, ga_ref, ba_ref, ws_ref, bst_ref, cw_ref, cb_ref, gb_ref,
             bb_ref, bout_ref, g1_ref, b1_ref, sup_ref, sdown_ref,
             h_ref, xhat1_ref, rstd1_ref, yb1_ref, gin_ref, gout_ref, gup_ref, gdown_ref,
             ext_ref, y_ref, wsm_ref, win_ref, wout_ref, load_sems,
             mix_send, mix_recv, mix_local, send_sems, recv_sems, local_sems):
        i = pl.program_id(0)
        mixer = ([sin_ref, sout_ref], [gin_ref, gout_ref], mix_send, mix_recv, mix_local)
        gather = ([sup_ref, sdown_ref], [gup_ref, gdown_ref], send_sems, recv_sems, local_sems)

        @pl.when(i == 0)
        def _():
            _gather_start(*mixer)
            _gather_relay(*mixer, via=[2, 2])
            _gather_finish(*mixer)
            _gather_start(*gather)
            loads = [pltpu.make_async_copy(gin_ref, win_ref, load_sems.at[0]),
                     pltpu.make_async_copy(gout_ref, wout_ref, load_sems.at[1])]
            for cp in loads:
                cp.start()
            for cp in loads:
                cp.wait()
            ext_ref[0:HALO_B, :] = jnp.zeros((HALO_B, D_B), F32)
            mask = _tril_mask()
            for hd in range(HEADS):
                wsm_ref[hd] = jnp.where(mask, ws_ref[hd], 0.0).astype(BF16)

        xb = x_ref[...].astype(BF16)
        for j in range(N_DEV):
            cols = slice(j * W_IN_BLK, (j + 1) * W_IN_BLK)
            h_ref[:, cols] = _nn(xb, win_ref[j]) + bin_ref[:, cols]

        def chunk(ci):
            r = _rows(ci, CHUNK)
            for hd in range(HEADS):
                _, _, u, _, _, _, _, _, sv = _mixer_a_head(h_ref, r, hd, ga_ref, ba_ref, wsm_ref, bst_ref)
                y_ref[r, hd * HEAD_DIM:(hd + 1) * HEAD_DIM] = (u * sv).astype(BF16)
            a_b = h_ref[r, 2 * D_A:2 * D_A + D_B]
            g_b = h_ref[r, 2 * D_A + D_B:D_IN]
            ext_ref[pl.ds(HALO_B + ci * CHUNK, CHUNK), :] = a_b * _sigmoid(g_b)

        _loop(n_chunks, chunk)

        def conv_rows(bi):
            base = bi * ROWS
            yb1 = _conv_b_block(ext_ref, base, cw_ref) + cb_ref[...]
            yb1_ref[pl.ds(base, ROWS), :] = yb1
            xhat, _ = _ln_stats(yb1)
            yb2 = xhat * gb_ref[...] + bb_ref[...]
            y_ref[pl.ds(base, ROWS), D_A:D] = (yb2 * _sigmoid(yb2)).astype(BF16)

        _loop(tm // ROWS, conv_rows)
        ext_ref[0:HALO_B, :] = ext_ref[tm:tm + HALO_B, :]

        mix = _nn(y_ref[...], wout_ref[...].reshape(D, D)) + bout_ref[...]
        xhat1, rstd1 = _ln_stats(ALPHA * x_ref[...] + mix)
        xhat1_ref[...] = xhat1
        rstd1_ref[...] = jnp.broadcast_to(rstd1, (tm, 128))

        @pl.when(i == (5 * nt) // 8)
        def _():
            _gather_relay(*gather, via=[1, 2])

        @pl.when(i == nt - 1)
        def _():
            _gather_finish(*gather)

    row = lambda w: pl.BlockSpec((tm, w), lambda i: (i, 0))
    return pl.pallas_call(
        body, name="mix_forward", grid=(nt,),
        in_specs=[row(D), ANY, ANY, _full(b_in.shape), _full(ln_a_g.shape),
                  _full(ln_a_b.shape), _full(w_spatial.shape), _full(bst.shape),
                  _full(conv_b_w.shape), _full(conv_b_b.shape), _full(ln_b_g.shape),
                  _full(ln_b_b.shape), _full(b_out.shape),
                  _full(ln1_g.shape), _full(ln1_b.shape), ANY, ANY],
        out_specs=[row(D_IN), row(D), row(128), row(D_B), ANY, ANY, ANY, ANY],
        out_shape=[jax.ShapeDtypeStruct((t, D_IN), F32), jax.ShapeDtypeStruct((t, D), F32),
                   jax.ShapeDtypeStruct((t, 128), F32), jax.ShapeDtypeStruct((t, D_B), F32)]
        + [jax.ShapeDtypeStruct((N_DEV,) + sh.shape, BF16) for sh in (sin, sout, sup, sdown)],
        scratch_shapes=[pltpu.VMEM((tm + HALO_B, D_B), F32), pltpu.VMEM((tm, D), BF16),
                        pltpu.VMEM((HEADS, CHUNK, CHUNK), BF16),
                        pltpu.VMEM((N_DEV,) + sin.shape, BF16), pltpu.VMEM((N_DEV,) + sout.shape, BF16),
                        pltpu.SemaphoreType.DMA((2,))] + _gather_scratch(2) + _gather_scratch(2),
        compiler_params=_params(("arbitrary",)),
    )(x, sin, sout, b_in, ln_a_g, ln_a_b, w_spatial, bst, conv_b_w, conv_b_b, ln_b_g, ln_b_b,
      b_out, ln1_g, ln1_b, sup, sdown)


def ffn_forward(xhat1, ln1_g, ln1_b, wup_g, cfw, cfb, wdown, ln2_g, ln2_b, target, tm):
    t = xhat1.shape[0]
    nt = t // tm

    def body(xh_ref, g1_ref, b1_ref, wup_ref, cfw_ref, cfb_ref, wdown_ref, g2_ref, b2_ref, tgt_ref,
             hu_ref, gv_ref, dr2_ref, loss_ref, sln2_ref,
             x1_ref, x1b_ref, hu32_ref, carry_ref, gbuf_ref, ffn_ref, acc_loss, acc_g2, acc_b2):
        i = pl.program_id(0)

        @pl.when(i == 0)
        def _():
            carry_ref[...] = jnp.zeros(carry_ref.shape, F32)
            acc_loss[...] = jnp.zeros(acc_loss.shape, F32)
            acc_g2[...] = jnp.zeros(acc_g2.shape, F32)
            acc_b2[...] = jnp.zeros(acc_b2.shape, F32)

        x1 = xh_ref[...] * g1_ref[...] + b1_ref[...]
        x1_ref[...] = x1
        x1b_ref[...] = x1.astype(BF16)

        def conv(g, j, base):
            if base == 0:
                win = jnp.concatenate([carry_ref[j], hu32_ref[g, 0:ROWS, :]], axis=0)
            else:
                win = hu32_ref[g, base - HALO_F:base + ROWS, :]
            taps = _taps_f(win)
            w = cfw_ref[j]
            return sum(taps[k] * w[k:k + 1, :] for k in range(KF)) + cfb_ref[j:j + 1, :]

        for f in range(N_F):
            hu32_ref[0] = _nn(x1b_ref[...], wup_ref[f])
            hu32_ref[1] = _nn(x1b_ref[...], wup_ref[N_F + f])

            def rows(bi, f=f):
                r = _rows(bi)
                gate = conv(0, f, bi * ROWS)
                val = conv(1, N_F + f, bi * ROWS)
                gbuf_ref[r, :] = (gate * _sigmoid(gate) * val).astype(BF16)
                gv_ref[f, r, :] = gate.astype(BF16)
                gv_ref[N_F + f, r, :] = val.astype(BF16)
                hu_ref[f, r, :] = hu32_ref[0, r, :].astype(BF16)
                hu_ref[N_F + f, r, :] = hu32_ref[1, r, :].astype(BF16)

            _loop(tm // ROWS, rows)
            carry_ref[f] = hu32_ref[0, tm - HALO_F:tm, :]
            carry_ref[N_F + f] = hu32_ref[1, tm - HALO_F:tm, :]
            part = _nn(gbuf_ref[...], wdown_ref[f])
            if f == 0:
                ffn_ref[...] = part
            else:
                ffn_ref[...] += part

        def tail(bi):
            r = _rows(bi, LN_ROWS)
            xhat2, rstd2 = _ln_stats(ALPHA * x1_ref[r, :] + ffn_ref[r, :])
            err = xhat2 * g2_ref[...] + b2_ref[...] - tgt_ref[r, :]
            e2 = _rsum8(err * err)
            acc_loss[...] += sum(e2[:, k * 128:(k + 1) * 128] for k in range(D // 128))
            dy = err * (1.0 / D)
            acc_g2[...] += _rsum8(dy * xhat2)
            acc_b2[...] += _rsum8(dy)
            dr2_ref[r, :] = _ln_bwd(dy * g2_ref[...], xhat2, rstd2)

        _loop(tm // LN_ROWS, tail)
        loss_ref[...] = acc_loss[...]

        @pl.when(i == nt - 1)
        def _():
            dg = jnp.sum(acc_g2[...], axis=0, keepdims=True)
            db = jnp.sum(acc_b2[...], axis=0, keepdims=True)
            for k in range(D // 128):
                sln2_ref[k:k + 1, :] = dg[:, k * 128:(k + 1) * 128]
                sln2_ref[8 + k:9 + k, :] = db[:, k * 128:(k + 1) * 128]

    row = pl.BlockSpec((tm, D), lambda i: (i, 0))
    return pl.pallas_call(
        body, name="ffn_forward", grid=(nt,),
        in_specs=[row, _full(ln1_g.shape), _full(ln1_b.shape), _resident(wup_g.shape),
                  _full(cfw.shape), _full(cfb.shape), _resident(wdown.shape),
                  _full(ln2_g.shape), _full(ln2_b.shape), row],
        out_specs=[pl.BlockSpec((N_DEV, tm, W_UP_BLK), lambda i: (0, i, 0)),
                   pl.BlockSpec((N_DEV, tm, W_UP_BLK), lambda i: (0, i, 0)), row,
                   _full((8, 128)), _full((16, 128))],
        out_shape=[jax.ShapeDtypeStruct((N_DEV, t, W_UP_BLK), BF16),
                   jax.ShapeDtypeStruct((N_DEV, t, W_UP_BLK), BF16), jax.ShapeDtypeStruct((t, D), F32),
                   jax.ShapeDtypeStruct((8, 128), F32), jax.ShapeDtypeStruct((16, 128), F32)],
        scratch_shapes=[pltpu.VMEM((tm, D), F32), pltpu.VMEM((tm, D), BF16),
                        pltpu.VMEM((2, tm, W_UP_BLK), F32),
                        pltpu.VMEM((N_DEV, HALO_F, W_UP_BLK), F32), pltpu.VMEM((tm, W_UP_BLK), BF16),
                        pltpu.VMEM((tm, D), F32), pltpu.VMEM((8, 128), F32),
                        pltpu.VMEM((8, D), F32), pltpu.VMEM((8, D), F32)],
        compiler_params=_params(("arbitrary",)),
    )(xhat1, ln1_g, ln1_b, wup_g, cfw, cfb, wdown, ln2_g, ln2_b, target)


def ffn_backward(order, dr2, xhat1, ln1_g, ln1_b, hu, gv, wup_g, cfw, wdown, tm):
    t = dr2.shape[0]
    nt = t // tm
    sub_rows = tm
    hu4 = hu.reshape(2, N_F, t, W_UP_BLK)
    gv4 = gv.reshape(2, N_F, t, W_UP_BLK)
    wup4 = wup_g.reshape(2, N_F, D, W_UP_BLK)
    cfw4 = cfw.reshape(2, N_F, KF, W_UP_BLK)

    def body(order_ref, dr2_ref, xh_ref, g1_ref, b1_ref, hu_ref, gv_ref, wup_ref, cfw_ref, wdown_ref,
             dwup_ref, dwdown_ref, dcfw_ref, dcfb_ref, dx1_ref, land_up_ref, land_down_ref,
             x1b_ref, drb_ref, dg_ref, dextg_ref, dextv_ref, gbuf_ref,
             dhug_ref, dhuv_ref, acc_wup, acc_wdown, acc_cfw, acc_cfb, sem, send_sems, recv_sems):
        fo = pl.program_id(0)
        f = order_ref[fo]
        f_prev = order_ref[jnp.maximum(fo - 1, 0)]
        slot = fo % 2
        i = pl.program_id(1)
        x, y, c = _mesh_pos()
        half = D_FF // N_DEV

        def to_sibling(fi, k, src, land_ref, shard_chip):
            d = jnp.bitwise_xor(shard_chip, 2 * x + y)
            slot = jnp.where(d == 1, 2, jnp.where(d == 2, 1, d))
            return pltpu.make_async_remote_copy(
                src_ref=src, dst_ref=land_ref.at[slot], send_sem=send_sems.at[fi, k], recv_sem=recv_sems.at[fi, k],
                device_id=(x, y, 1 - c), device_id_type=MESH)

        def up_copy(fi, g):
            return to_sibling(fi, g, dwup_ref.at[g, fi], land_up_ref, 2 * g + fi // 2)

        def down_copy(fi):
            return to_sibling(fi, 2, dwdown_ref.at[fi, pl.ds((1 - c) * half, half)], land_down_ref, fi)

        def flush(fi, s):
            return [pltpu.make_async_copy(acc_wup.at[s, 0], dwup_ref.at[0, fi], sem.at[s, 0]),
                    pltpu.make_async_copy(acc_wup.at[s, 1], dwup_ref.at[1, fi], sem.at[s, 1]),
                    pltpu.make_async_copy(acc_wdown.at[s], dwdown_ref.at[fi], sem.at[s, 2])]

        def flushed(fi, s):
            for cp in flush(fi, s):
                cp.wait()
            down_copy(fi).start()

            @pl.when(fi % 2 != c)
            def _():
                up_copy(fi, 0).start()
                up_copy(fi, 1).start()

        @pl.when(i == 0)
        def _():
            acc_wup[slot] = jnp.zeros(acc_wup.shape[1:], F32)
            acc_wdown[slot] = jnp.zeros(acc_wdown.shape[1:], F32)
            acc_cfw[...] = jnp.zeros(acc_cfw.shape, F32)
            acc_cfb[...] = jnp.zeros(acc_cfb.shape, F32)
            dextg_ref[tm:tm + HALO_F, :] = jnp.zeros((HALO_F, W_UP_BLK), F32)
            dextv_ref[tm:tm + HALO_F, :] = jnp.zeros((HALO_F, W_UP_BLK), F32)

        w = [cfw_ref[0, 0], cfw_ref[1, 0]]
        dext = [dextg_ref, dextv_ref]
        dhu = [dhug_ref, dhuv_ref]

        def rows1(bi):
            r = _rows(bi)
            gate = gv_ref[0, 0, r, :].astype(F32)
            val = gv_ref[1, 0, r, :].astype(F32)
            sg = _sigmoid(gate)
            silu = gate * sg
            gbuf_ref[r, :] = (silu * val).astype(BF16)
            dg = dg_ref[r, :]
            dgate = dg * val * (sg * (1.0 + gate * (1.0 - sg)))
            dval = dg * silu
            dextg_ref[r, :] = dgate
            dextv_ref[r, :] = dval
            acc_cfb[0:8, :] += _rsum8(dgate)
            acc_cfb[8:16, :] += _rsum8(dval)

        def rows2(bi):
            r = _rows(bi)
            for g in range(2):
                win = dext[g][pl.ds(bi * ROWS, ROWS + HALO_F), :]
                n = ROWS + HALO_F
                later = [pltpu.roll(win, n - 2, 0)[0:ROWS, :], pltpu.roll(win, n - 1, 0)[0:ROWS, :],
                         win[0:ROWS, :]]
                d = sum(later[k] * w[g][k:k + 1, :] for k in range(KF))
                dhu[g][r, :] = d.astype(BF16)
                pre = hu_ref[g, 0, r, :].astype(F32)
                for k in range(KF):
                    r0 = 8 * (g * KF + k)
                    acc_cfw[r0:r0 + 8, :] += _rsum8(later[k] * pre)

        for sub in reversed(range(tm // sub_rows)):
            rs = slice(sub * sub_rows, (sub + 1) * sub_rows)
            blocks = range(sub * sub_rows // ROWS, (sub + 1) * sub_rows // ROWS)
            x1b_ref[rs, :] = (xh_ref[rs, :] * g1_ref[...] + b1_ref[...]).astype(BF16)
            drb_ref[rs, :] = dr2_ref[rs, :].astype(BF16)
            dg_ref[rs, :] = _nt(drb_ref[rs, :], wdown_ref[0])
            for bi in blocks:
                rows1(bi)
            for bi in blocks:
                rows2(bi)
            acc_wdown[slot] += _tn(gbuf_ref[rs, :], drb_ref[rs, :])
            acc_wup[slot, 0] += _tn(dhug_ref[rs, :], x1b_ref[rs, :])
            acc_wup[slot, 1] += _tn(dhuv_ref[rs, :], x1b_ref[rs, :])
            dx1_ref[0, rs, :] = (_nt(dhug_ref[rs, :], wup_ref[0, 0])
                                 + _nt(dhuv_ref[rs, :], wup_ref[1, 0])).astype(BF16)
        dextg_ref[tm:tm + HALO_F, :] = dextg_ref[0:HALO_F, :]
        dextv_ref[tm:tm + HALO_F, :] = dextv_ref[0:HALO_F, :]

        @pl.when(i == nt - 1)
        def _():
            for g in range(2):
                dcfb_ref[g, 0] = jnp.sum(acc_cfb[8 * g:8 * g + 8, :], axis=0, keepdims=True)
                for k in range(KF):
                    r0 = 8 * (g * KF + k)
                    dcfw_ref[g, 0, k:k + 1, :] = jnp.sum(acc_cfw[r0:r0 + 8, :], axis=0, keepdims=True)
            for cp in flush(f, slot):
                cp.start()

        @pl.when((i == 0) & (fo > 0))
        def _():
            flushed(f_prev, 1 - slot)

        @pl.when((i == nt - 1) & (fo == N_F - 1))
        def _():
            flushed(f, slot)
            for fi in range(N_F):
                down_copy(fi).wait()
                for g in range(2):
                    @pl.when(fi % 2 != c)
                    def _():
                        up_copy(fi, g).wait_send()

                    @pl.when(fi % 2 == c)
                    def _():
                        up_copy(fi, g).wait_recv()

    rev = lambda i: nt - 1 - i
    row = pl.BlockSpec((tm, D), lambda fo, i, o: (rev(i), 0))
    pair = lambda r, c: pl.BlockSpec((2, 1, r, c), lambda fo, i, o: (0, o[fo], 0, 0))
    tile = pl.BlockSpec((2, 1, tm, W_UP_BLK), lambda fo, i, o: (0, o[fo], rev(i), 0))
    return pl.pallas_call(
        body, name="ffn_backward",
        grid_spec=pltpu.PrefetchScalarGridSpec(
            num_scalar_prefetch=1, grid=(N_F, nt),
            in_specs=[row, row, _full(ln1_g.shape), _full(ln1_b.shape), tile, tile,
                      pair(D, W_UP_BLK), pair(KF, W_UP_BLK),
                      pl.BlockSpec((1, W_UP_BLK, D), lambda fo, i, o: (o[fo], 0, 0))],
            out_specs=[ANY, ANY, pair(KF, W_UP_BLK), pair(1, W_UP_BLK),
                       pl.BlockSpec((1, tm, D), lambda fo, i, o: (o[fo], rev(i), 0)), ANY, ANY],
            scratch_shapes=[pltpu.VMEM((tm, D), BF16), pltpu.VMEM((tm, D), BF16),
                            pltpu.VMEM((tm, W_UP_BLK), F32),
                            pltpu.VMEM((tm + HALO_F, W_UP_BLK), F32), pltpu.VMEM((tm + HALO_F, W_UP_BLK), F32),
                            pltpu.VMEM((tm, W_UP_BLK), BF16), pltpu.VMEM((tm, W_UP_BLK), BF16),
                            pltpu.VMEM((tm, W_UP_BLK), BF16),
                            pltpu.VMEM((2, 2, W_UP_BLK, D), F32), pltpu.VMEM((2, W_UP_BLK, D), F32),
                            pltpu.VMEM((2 * KF * 8, W_UP_BLK), F32), pltpu.VMEM((16, W_UP_BLK), F32),
                            pltpu.SemaphoreType.DMA((2, 3)),
                            pltpu.SemaphoreType.DMA((N_F, 3)), pltpu.SemaphoreType.DMA((N_F, 3))]),
        out_shape=[jax.ShapeDtypeStruct((2, N_F, W_UP_BLK, D), F32),
                   jax.ShapeDtypeStruct((N_F, W_UP_BLK, D), F32),
                   jax.ShapeDtypeStruct((2, N_F, KF, W_UP_BLK), F32),
                   jax.ShapeDtypeStruct((2, N_F, 1, W_UP_BLK), F32),
                   jax.ShapeDtypeStruct((N_F, t, D), BF16),
                   jax.ShapeDtypeStruct((4, W_UP_BLK, D), F32),
                   jax.ShapeDtypeStruct((4, D_FF // N_DEV, D), F32)],
        compiler_params=_params(("arbitrary", "arbitrary")),
    )(order, dr2, xhat1, ln1_g, ln1_b, hu4, gv4, wup4, cfw4, wdown)


def mix_backward(x, h, yb1, dx1p, dr2, xhat1, rstd1, win_g, ln_a_g, ln_a_b, w_spatial, bst,
                 conv_b_w, ln_b_g, ln_b_b, wout, ln1_g, ffn_partials, tm):
    t = x.shape[0]
    n_p = len(ffn_partials)
    nt = t // tm
    n_chunks = tm // CHUNK
    halo_blocks = tm // HALO_B

    def body(x_ref, h_ref, halo_ref, yb1_ref, dx1p_ref, dr2_ref, xh1_ref, rstd1_ref, win_ref, ga_ref, ba_ref,
             ws_ref, bst_ref, cw_ref, gb_ref, bb_ref, wout_ref, g1_ref, *rest):
        p_refs, rest = rest[:n_p], rest[n_p:]
        gx_ref, dwin_ref, dwout_ref, dcw_ref, small_ref = rest[:5]
        land_refs, rest = rest[5:5 + n_p], rest[5 + n_p:]
        (ext_ref, dext_ref, y_ref, dy_ref, dh_ref, dmb_ref, wsm_ref,
         acc_win, acc_wout, acc_bin, acc_lnag, acc_lnab, acc_ws, acc_bs, acc_cbb, acc_lnbg,
         acc_lnbb, acc_bout, acc_ln1g, acc_ln1b, acc_cw, sem, send_sems, recv_sems) = rest
        i = pl.program_id(0)

        @pl.when(i == 0)
        def _():
            for cp in _chip_copies(p_refs, land_refs, send_sems, recv_sems):
                cp.start()

        first_tile = i == nt - 1
        accs = [acc_win, acc_wout, acc_bin, acc_lnag, acc_lnab, acc_ws, acc_bs, acc_cbb, acc_lnbg,
                acc_lnbb, acc_bout, acc_ln1g, acc_ln1b, acc_cw]

        @pl.when(i == 0)
        def _():
            for acc in accs:
                acc[...] = jnp.zeros(acc.shape, F32)
            dext_ref[tm:tm + HALO_B, :] = jnp.zeros((HALO_B, D_B), F32)
            mask = _tril_mask()
            for hd in range(HEADS):
                wsm_ref[hd] = jnp.where(mask, ws_ref[hd], 0.0).astype(BF16)

        def ln1_rows(bi):
            r = _rows(bi, LN_ROWS)
            part = [dx1p_ref[f, r, :].astype(F32) for f in range(N_F)]
            dx1 = ALPHA * dr2_ref[r, :] + ((part[0] + part[1]) + (part[2] + part[3]))
            xhat = xh1_ref[r, :]
            acc_ln1g[...] += _rsum8(dx1 * xhat)
            acc_ln1b[...] += _rsum8(dx1)
            dr1 = _ln_bwd(dx1 * g1_ref[...], xhat, rstd1_ref[r, 0:1])
            acc_bout[...] += _rsum8(dr1)
            gx_ref[r, :] = ALPHA * dr1
            dmb_ref[r, :] = dr1.astype(BF16)

        _loop(tm // LN_ROWS, ln1_rows)
        dy_ref[...] = _nt(dmb_ref[...], wout_ref[...])

        ha = halo_ref[:, 0:D_B]
        hg = halo_ref[:, D_B:2 * D_B]
        ext_ref[0:HALO_B, :] = jnp.where(first_tile, 0.0, 1.0) * (ha * _sigmoid(hg))

        def chunk(ci):
            r = _rows(ci, CHUNK)
            for hd in range(HEADS):
                sl = slice(hd * HEAD_DIM, (hd + 1) * HEAD_DIM)
                rows8 = slice(8 * hd, 8 * hd + 8)
                hus, hvs, u, cdf_u, cdf_v, xhat, rstd, vn, sv = _mixer_a_head(
                    h_ref, r, hd, ga_ref, ba_ref, wsm_ref, bst_ref)
                dy_a = dy_ref[r, sl]
                y_ref[r, sl] = (u * sv).astype(BF16)
                du = dy_a * sv
                dsv = dy_a * u
                dsvb = dsv.astype(BF16)
                acc_bs[hd] += dsv
                acc_ws[hd] += _nt(dsvb, vn)
                dvn = _tn(wsm_ref[hd], dsvb)
                acc_lnag[rows8, :] += _rsum8(dvn * xhat)
                acc_lnab[rows8, :] += _rsum8(dvn)
                dv = _ln_bwd(dvn * ga_ref[hd:hd + 1, :], xhat, rstd)
                slv = slice(D_A + hd * HEAD_DIM, D_A + (hd + 1) * HEAD_DIM)
                dhu = du * (cdf_u + hus * jnp.exp(-0.5 * hus * hus) * INV_SQRT_2PI)
                dhv = dv * (cdf_v + hvs * jnp.exp(-0.5 * hvs * hvs) * INV_SQRT_2PI)
                acc_bin[:, sl] += _rsum8(dhu)
                acc_bin[:, slv] += _rsum8(dhv)
                dh_ref[r, sl] = dhu.astype(BF16)
                dh_ref[r, slv] = dhv.astype(BF16)
            a_b = h_ref[r, 2 * D_A:2 * D_A + D_B]
            g_b = h_ref[r, 2 * D_A + D_B:D_IN]
            ext_ref[pl.ds(HALO_B + ci * CHUNK, CHUNK), :] = a_b * _sigmoid(g_b)

        _loop(n_chunks, chunk)

        def conv_rows(bi):
            base = bi * ROWS
            r = pl.ds(base, ROWS)
            xhat, rstd = _ln_stats(yb1_ref[r, :])
            yb2 = xhat * gb_ref[...] + bb_ref[...]
            sg = _sigmoid(yb2)
            y_ref[r, D_A:D] = (yb2 * sg).astype(BF16)
            dyb2 = dy_ref[r, D_A:D] * (sg * (1.0 + yb2 * (1.0 - sg)))
            acc_lnbg[...] += _rsum8(dyb2 * xhat)
            acc_lnbb[...] += _rsum8(dyb2)
            dyb1 = _ln_bwd(dyb2 * gb_ref[...], xhat, rstd)
            acc_cbb[...] += _rsum8(dyb1)
            dext_ref[r, :] = dyb1
            for k, tap in _taps(ext_ref[pl.ds(base, ROWS + HALO_B), :], CONV_B_OFFSETS):
                acc_cw[8 * k:8 * k + 8, :] += _rsum8(dyb1 * tap)

        _loop(tm // ROWS, conv_rows)

        def convt_rows(bi):
            base = bi * ROWS
            r = pl.ds(base, ROWS)
            dyb0 = jnp.zeros((ROWS, D_B), F32)
            for k, tap in _taps(dext_ref[pl.ds(base, ROWS + HALO_B), :], CONV_B_T_OFFSETS):
                dyb0 = dyb0 + tap * cw_ref[k:k + 1, :]
            a_b = h_ref[r, 2 * D_A:2 * D_A + D_B]
            sg = _sigmoid(h_ref[r, 2 * D_A + D_B:D_IN])
            da_b = dyb0 * sg
            dg_b = dyb0 * a_b * sg * (1.0 - sg)
            acc_bin[:, 2 * D_A:2 * D_A + D_B] += _rsum8(da_b)
            acc_bin[:, 2 * D_A + D_B:D_IN] += _rsum8(dg_b)
            dh_ref[r, 2 * D_A:2 * D_A + D_B] = da_b.astype(BF16)
            dh_ref[r, 2 * D_A + D_B:D_IN] = dg_b.astype(BF16)

        _loop(tm // ROWS, convt_rows)
        dext_ref[tm:tm + HALO_B, :] = dext_ref[0:HALO_B, :]

        acc_wout[...] += _tn(y_ref[...], dmb_ref[...])
        xt = x_ref[...].T.astype(BF16)
        dh_blocks = [dh_ref[:, j * W_IN_BLK:(j + 1) * W_IN_BLK] for j in range(N_DEV)]
        for j in range(N_DEV):
            acc_win[j] += _nn(xt, dh_blocks[j])
        gx_ref[...] += sum(_nt(dh_blocks[j], win_ref[j]) for j in range(N_DEV))

        @pl.when(i == nt - 1)
        def _():
            cps = [pltpu.make_async_copy(acc_win, dwin_ref, sem.at[0]),
                   pltpu.make_async_copy(acc_wout, dwout_ref, sem.at[1])]
            for cp in cps:
                cp.start()
            small_ref[...] = jnp.zeros(small_ref.shape, F32)

            def put_row_vector(row0, acc):
                vec = jnp.sum(acc[...], axis=0, keepdims=True)
                for k in range(vec.shape[1] // 128):
                    small_ref[row0 + k:row0 + k + 1, :] = vec[:, k * 128:(k + 1) * 128]

            put_row_vector(S_BIN, acc_bin)
            put_row_vector(S_CBB, acc_cbb)
            put_row_vector(S_LNBG, acc_lnbg)
            put_row_vector(S_LNBB, acc_lnbb)
            put_row_vector(S_BOUT, acc_bout)
            put_row_vector(S_LN1G, acc_ln1g)
            put_row_vector(S_LN1B, acc_ln1b)
            mask = _tril_mask()
            for hd in range(HEADS):
                rows8 = slice(8 * hd, 8 * hd + 8)
                small_ref[S_LNAG + hd:S_LNAG + hd + 1, :] = jnp.sum(acc_lnag[rows8, :], axis=0, keepdims=True)
                small_ref[S_LNAB + hd:S_LNAB + hd + 1, :] = jnp.sum(acc_lnab[rows8, :], axis=0, keepdims=True)
                small_ref[S_WS + hd * CHUNK:S_WS + (hd + 1) * CHUNK, :] = jnp.where(mask, acc_ws[hd], 0.0)
                small_ref[S_BS + hd:S_BS + hd + 1, :] = jnp.sum(acc_bs[hd].T, axis=0, keepdims=True)
            for k in range(KB):
                dcw_ref[k:k + 1, :] = jnp.sum(acc_cw[8 * k:8 * k + 8, :], axis=0, keepdims=True)
            for cp in cps:
                cp.wait()
            for cp in _chip_copies(p_refs, land_refs, send_sems, recv_sems):
                cp.wait()

    rev = lambda i: nt - 1 - i
    row = lambda w: pl.BlockSpec((tm, w), lambda i: (rev(i), 0))
    return pl.pallas_call(
        body, name="mix_backward", grid=(nt,),
        in_specs=[row(D), row(D_IN),
                  pl.BlockSpec((HALO_B, 2 * D_B), lambda i: (jnp.maximum(rev(i) * halo_blocks - 1, 0), 1)),
                  row(D_B), pl.BlockSpec((N_F, tm, D), lambda i: (0, rev(i), 0)),
                  row(D), row(D), row(128), _resident(win_g.shape), _full(ln_a_g.shape),
                  _full(ln_a_b.shape), _full(w_spatial.shape), _full(bst.shape), _full(conv_b_w.shape),
                  _full(ln_b_g.shape), _full(ln_b_b.shape),
                  _resident(wout.shape), _full(ln1_g.shape)] + [ANY] * n_p,
        out_specs=[row(D), ANY, ANY, _full((KB, D_B)), _full((S_MIX_ROWS, 128))] + [ANY] * n_p,
        out_shape=[jax.ShapeDtypeStruct((t, D), F32), jax.ShapeDtypeStruct((N_DEV, D, W_IN_BLK), F32),
                   jax.ShapeDtypeStruct((D, D), F32), jax.ShapeDtypeStruct((KB, D_B), F32),
                   jax.ShapeDtypeStruct((S_MIX_ROWS, 128), F32)]
        + [jax.ShapeDtypeStruct(p.shape, BF16) for p in ffn_partials],
        scratch_shapes=[pltpu.VMEM((tm + HALO_B, D_B), F32), pltpu.VMEM((tm + HALO_B, D_B), F32),
                        pltpu.VMEM((tm, D), BF16), pltpu.VMEM((tm, D), F32), pltpu.VMEM((tm, D_IN), BF16),
                        pltpu.VMEM((tm, D), BF16),
                        pltpu.VMEM((HEADS, CHUNK, CHUNK), BF16),
                        pltpu.VMEM((N_DEV, D, W_IN_BLK), F32), pltpu.VMEM((D, D), F32),
                        pltpu.VMEM((8, D_IN), F32), pltpu.VMEM((8 * HEADS, HEAD_DIM), F32),
                        pltpu.VMEM((8 * HEADS, HEAD_DIM), F32), pltpu.VMEM((HEADS, CHUNK, CHUNK), F32),
                        pltpu.VMEM((HEADS, CHUNK, CHUNK), F32), pltpu.VMEM((8, D_B), F32),
                        pltpu.VMEM((8, D_B), F32), pltpu.VMEM((8, D_B), F32), pltpu.VMEM((8, D), F32),
                        pltpu.VMEM((8, D), F32), pltpu.VMEM((8, D), F32), pltpu.VMEM((8 * KB, D_B), F32),
                        pltpu.SemaphoreType.DMA((2,)),
                        pltpu.SemaphoreType.DMA((n_p, 3)), pltpu.SemaphoreType.DMA((n_p, 3))],
        compiler_params=_params(("arbitrary",)),
    )(x, h, h, yb1, dx1p, dr2, xhat1, rstd1, win_g, ln_a_g, ln_a_b, w_spatial, bst, conv_b_w,
      ln_b_g, ln_b_b, wout, ln1_g, *ffn_partials)


def _rows128(a):
    return a.reshape(-1, 128)


def _pack_conv(cb, cf):
    lead = cb.shape[:-2]
    pad = [(0, 0)] * len(lead)
    flat = jnp.pad(cb.reshape(lead + (KB * 64,)), pad + [(0, 3 * W_UP_BLK - KB * 64)])
    rows = jnp.concatenate([cf, flat.reshape(lead + (3, W_UP_BLK))], axis=-2)
    return jnp.pad(rows, pad + [(0, 2), (0, 768 - W_UP_BLK)])


def _unpack_conv(p):
    lead = p.shape[:-2]
    cf = p[..., 0:KF, 0:W_UP_BLK]
    cb = p[..., 3:6, 0:W_UP_BLK].reshape(lead + (3 * W_UP_BLK,))[..., :KB * 64].reshape(lead + (KB, 64))
    return cb, cf


def kernel(x, w_in, b_in, ln_a_g, ln_a_b, w_spatial, b_spatial, conv_b_w, conv_b_b, ln_b_g, ln_b_b, w_out, b_out, ln1_g, ln1_b, w_up, conv_f_w, conv_f_b, w_down, ln2_g, ln2_b, loss_target, m_w_in, m_b_in, m_ln_a_g, m_ln_a_b, m_w_spatial, m_b_spatial, m_conv_b_w, m_conv_b_b, m_ln_b_g, m_ln_b_b, m_w_out, m_b_out, m_ln1_g, m_ln1_b, m_w_up, m_conv_f_w, m_conv_f_b, m_w_down, m_ln2_g, m_ln2_b, v_w_in, v_b_in, v_ln_a_g, v_ln_a_b, v_w_spatial, v_b_spatial, v_conv_b_w, v_conv_b_b, v_ln_b_g, v_ln_b_b, v_w_out, v_b_out, v_ln1_g, v_ln1_b, v_w_up, v_conv_f_w, v_conv_f_b, v_w_down, v_ln2_g, v_ln2_b):
    t = x.shape[1]
    x2 = x.reshape(t, D)
    target = loss_target.reshape(t, D)
    tm_fwd = min(t, 512)
    tm_bwd = min(t, 256)
    tm_ffn_bwd = min(t, 512)

    xi, yi, ci = _mesh_pos()
    jidx = jnp.stack([_lid(px, py, ci) for px, py in _chip_patterns(xi, yi)]).astype(jnp.int32)

    sin, sout, sup, sdown, conv_g = prepare_weights(w_in, w_out, w_up.T, w_down, _pack_conv(conv_b_w, conv_f_w))
    conv_b_all, cfw = _unpack_conv(conv_g)
    conv_b_full = conv_b_all.transpose(1, 0, 2).reshape(KB, D_B)
    cfb = conv_f_b.reshape(N_DEV, W_UP_BLK)
    row = lambda a: a.reshape(1, -1)
    bst = b_spatial.T

    h, xhat1, rstd1, yb1, win_g, wout_g, wup_g, wdown_g = mix_forward(
        x2, sin, sout, row(b_in), ln_a_g, ln_a_b, w_spatial, bst, conv_b_full, row(conv_b_b),
        row(ln_b_g), row(ln_b_b), row(b_out), row(ln1_g), row(ln1_b), sup, sdown, tm_fwd)
    wout_full = wout_g.reshape(D, D)
    wdown4 = wdown_g.reshape(N_F, W_UP_BLK, D)
    hu, gv, dr2, loss_part, s_ln2 = ffn_forward(
        xhat1, row(ln1_g), row(ln1_b), wup_g, cfw, cfb, wdown4, row(ln2_g), row(ln2_b), target, tm_bwd)

    order = jnp.where(ci == 0, jnp.array([1, 3, 0, 2], jnp.int32), jnp.array([0, 2, 1, 3], jnp.int32))
    dwup, dwdown, dcfw, dcfb, dx1p, *ffn_lands = ffn_backward(
        order, dr2, xhat1, row(ln1_g), row(ln1_b), hu, gv, wup_g, cfw, wdown4, tm_ffn_bwd)
    ffn_grads = [dwup.reshape(N_DEV, W_UP_BLK, D), dwdown.reshape(N_DEV, D_FF // N_DEV, D)]
    ffn_partials = [chip_partials("chip_partials_" + nm, g, l, jidx, rb)
                    for nm, g, l, rb in zip(["w_up", "w_down"], ffn_grads, ffn_lands, [352, 352])]
    grad_x, dwin, dwout, dcw, s_mix, *ffn_recvs = mix_backward(
        x2, h, yb1, dx1p, dr2, xhat1, rstd1, win_g, ln_a_g, ln_a_b, w_spatial, bst,
        conv_b_full, row(ln_b_g), row(ln_b_b), wout_full, row(ln1_g), ffn_partials, tm_bwd)

    dcfb_rows = jnp.pad(dcfb.reshape(-1, 128), ((0, 4), (0, 0)))
    svec = jnp.concatenate([s_mix, dcfb_rows, s_ln2, loss_part], axis=0)
    dconv = _pack_conv(dcw.reshape(KB, N_DEV, 64).transpose(1, 0, 2), dcfw.reshape(N_DEV, KF, W_UP_BLK))
    mix_grads = [dwin, dwout.reshape(N_DEV, D // N_DEV, D), dconv]
    mix_w = [w_in, w_out, _pack_conv(conv_b_w, conv_f_w)]
    mix_m = [m_w_in, m_w_out, _pack_conv(m_conv_b_w, m_conv_f_w)]
    mix_v = [v_w_in, v_w_out, _pack_conv(v_conv_b_w, v_conv_f_w)]
    *mix_out, sv_slots = mixer_reduce_adamw(mix_grads, svec, mix_w, mix_m, mix_v)
    big = {nm: [mix_out[k * 3 + p] for k in range(4)] for p, nm in enumerate(["w_in", "w_out", "conv"])}

    ffn_w = [(w_up.T, m_w_up.T, v_w_up.T), (w_down, m_w_down, v_w_down)]
    for nm, g, l, r, (w, m, v) in zip(["w_up", "w_down"], ffn_grads, ffn_lands, ffn_recvs, ffn_w):
        big[nm] = reduce_and_adamw("reduce_adamw_" + nm, g, l, r, w, m, v, jidx, 352)
    big["w_up"] = [o.T for o in big["w_up"]]
    for k in range(4):
        cb_k, cf_k = _unpack_conv(big["conv"][k])
        big.setdefault("conv_b_w", []).append(cb_k)
        big.setdefault("conv_f_w", []).append(cf_k)

    small_w = dict(b_in=b_in, ln_a_g=ln_a_g, ln_a_b=ln_a_b, w_spatial=w_spatial, b_spatial=b_spatial,
                   conv_b_b=conv_b_b, ln_b_g=ln_b_g, ln_b_b=ln_b_b, b_out=b_out, ln1_g=ln1_g,
                   ln1_b=ln1_b, conv_f_b=conv_f_b, ln2_g=ln2_g, ln2_b=ln2_b)
    small_m = dict(b_in=m_b_in, ln_a_g=m_ln_a_g, ln_a_b=m_ln_a_b, w_spatial=m_w_spatial,
                   b_spatial=m_b_spatial, conv_b_b=m_conv_b_b, ln_b_g=m_ln_b_g, ln_b_b=m_ln_b_b,
                   b_out=m_b_out, ln1_g=m_ln1_g, ln1_b=m_ln1_b, conv_f_b=m_conv_f_b, ln2_g=m_ln2_g,
                   ln2_b=m_ln2_b)
    small_v = dict(b_in=v_b_in, ln_a_g=v_ln_a_g, ln_a_b=v_ln_a_b, w_spatial=v_w_spatial,
                   b_spatial=v_b_spatial, conv_b_b=v_conv_b_b, ln_b_g=v_ln_b_g, ln_b_b=v_ln_b_b,
                   b_out=v_b_out, ln1_g=v_ln1_g, ln1_b=v_ln1_b, conv_f_b=v_conv_f_b, ln2_g=v_ln2_g,
                   ln2_b=v_ln2_b)
    order = [nm for nm, _, _ in SMALL_LAYOUT]
    small_out = small_adamw(sv_slots, [_rows128(small_w[nm]) for nm in order],
                            [_rows128(small_m[nm]) for nm in order], [_rows128(small_v[nm]) for nm in order])
    n_small = len(order)
    small = {nm: [small_out[k * n_small + p].reshape(small_w[nm].shape) for k in range(4)]
             for p, nm in enumerate(order)}
    loss = jnp.sum(small_out[4 * n_small]) * (0.5 / D)

    weights = ["w_in", "b_in", "ln_a_g", "ln_a_b", "w_spatial", "b_spatial", "conv_b_w", "conv_b_b",
               "ln_b_g", "ln_b_b", "w_out", "b_out", "ln1_g", "ln1_b", "w_up", "conv_f_w", "conv_f_b",
               "w_down", "ln2_g", "ln2_b"]
    result = lambda nm, k: big[nm][k] if nm in big else small[nm][k]
    return (loss, grad_x.reshape(x.shape), *[result(nm, 0) for nm in weights],
            *[result(nm, 1) for nm in weights], *[result(nm, 2) for nm in weights],
            *[result(nm, 3) for nm in weights])
```

```python
import functools
import math

import jax
import jax.numpy as jnp
from jax import lax
from jax.experimental import pallas as pl
from jax.experimental.pallas import tpu as pltpu

F32 = jnp.float32
BF16 = jnp.bfloat16

D = 1024
D_A = 512
D_B = 512
HEADS = 4
HEAD_DIM = 128
CHUNK = 128
KB = 31
KF = 3
D_FF = 2816
D_IN = 2048
N_DEV = 8
W_IN_BLK = D_IN // N_DEV
W_UP_BLK = 2 * D_FF // N_DEV
N_F = 4
LN_EPS = 1e-5
ALPHA = 2.0 ** 0.25

ADAM_LR = 0.001
ADAM_B1 = 0.9
ADAM_B2 = 0.999
ADAM_EPS = 1e-08
ADAM_WD = 0.01
ADAM_STEP = 10

INV_SQRT2 = 1.0 / math.sqrt(2.0)
INV_SQRT_2PI = 1.0 / math.sqrt(2.0 * math.pi)

HALO_B = 32
HALO_F = 8
ROWS = 64
LN_ROWS = 32
VMEM_LIMIT = 58 * 1024 * 1024

MESH = pl.DeviceIdType.MESH
ANY = pl.BlockSpec(memory_space=pl.ANY)
VMEM = pl.BlockSpec(memory_space=pltpu.VMEM)

S_BIN, S_LNAG, S_LNAB, S_WS, S_BS, S_CBB, S_LNBG, S_LNBB, S_BOUT, S_LN1G, S_LN1B = (
    0, 16, 24, 32, 544, 552, 560, 568, 576, 584, 592)
S_MIX_ROWS = 600
S_CFB = 600
S_LN2G = 648
S_LN2B = 656
S_LOSS = 664
S_ROWS = 672


def _tn(a, b):
    return lax.dot_general(a, b, (((0,), (0,)), ((), ())), preferred_element_type=F32)


def _nt(a, b):
    return lax.dot_general(a, b, (((1,), (1,)), ((), ())), preferred_element_type=F32)


def _nn(a, b):
    return jnp.dot(a, b, preferred_element_type=F32)


def _sigmoid(x):
    return 1.0 / (1.0 + jnp.exp(-x))


def _ln_stats(x):
    mu = jnp.mean(x, axis=-1, keepdims=True)
    xc = x - mu
    var = jnp.mean(xc * xc, axis=-1, keepdims=True)
    rstd = lax.rsqrt(var + LN_EPS)
    return xc * rstd, rstd


def _ln_bwd(dxhat, xhat, rstd):
    m1 = jnp.mean(dxhat, axis=-1, keepdims=True)
    m2 = jnp.mean(dxhat * xhat, axis=-1, keepdims=True)
    return rstd * (dxhat - m1 - xhat * m2)


def _rsum8(x):
    r, n = x.shape
    return x.reshape(r // 8, 8, n).sum(axis=0)


def _rows(i, n=ROWS):
    return pl.ds(i * n, n)


def _loop(n, body):
    for i in range(n):
        body(i)


def _tril_mask():
    r = lax.broadcasted_iota(jnp.int32, (CHUNK, CHUNK), 0)
    c = lax.broadcasted_iota(jnp.int32, (CHUNK, CHUNK), 1)
    return c <= r


def _mixer_a_head(h_ref, r, hd, ga_ref, ba_ref, wsm_ref, bst_ref):
    sl = slice(hd * HEAD_DIM, (hd + 1) * HEAD_DIM)
    hu = h_ref[r, sl]
    hv = h_ref[r, D_A + hd * HEAD_DIM:D_A + (hd + 1) * HEAD_DIM]
    cdf_u = 0.5 * (1.0 + lax.erf(hu * INV_SQRT2))
    cdf_v = 0.5 * (1.0 + lax.erf(hv * INV_SQRT2))
    u = hu * cdf_u
    xhat, rstd = _ln_stats(hv * cdf_v)
    vn = (xhat * ga_ref[hd:hd + 1, :] + ba_ref[hd:hd + 1, :]).astype(BF16)
    sv = _nn(wsm_ref[hd], vn) + bst_ref[:, hd:hd + 1]
    return hu, hv, u, cdf_u, cdf_v, xhat, rstd, vn, sv


def _taps(win, offsets):
    n = win.shape[0]
    for s in range(8):
        ks = [k for k, o in enumerate(offsets) if o % 8 == s]
        if ks:
            moved = win if s == 0 else pltpu.roll(win, n - s, 0)
            for k in ks:
                yield k, moved[offsets[k] - s:offsets[k] - s + ROWS, :]


CONV_B_OFFSETS = [2 + k for k in range(KB)]
CONV_B_T_OFFSETS = [30 - k for k in range(KB)]


def _conv_b_block(ext_ref, base, cw_ref):
    acc = jnp.zeros((ROWS, D_B), F32)
    for k, tap in _taps(ext_ref[pl.ds(base, ROWS + HALO_B), :], CONV_B_OFFSETS):
        acc = acc + tap * cw_ref[k:k + 1, :]
    return acc


def _taps_f(win):
    n = ROWS + HALO_F
    return [pltpu.roll(win, n - 6, 0)[0:ROWS, :], pltpu.roll(win, n - 7, 0)[0:ROWS, :], win[8:n, :]]


def _params(sem, **kw):
    return pltpu.CompilerParams(dimension_semantics=sem, vmem_limit_bytes=VMEM_LIMIT, **kw)


def _resident(shape):
    zeros = (0,) * len(shape)
    return pl.BlockSpec(shape, lambda *_: zeros, pipeline_mode=pl.Buffered(1))


def _full(shape):
    zeros = (0,) * len(shape)
    return pl.BlockSpec(shape, lambda *_: zeros)


def _mesh_pos():
    return lax.axis_index("x"), lax.axis_index("y"), lax.axis_index("c")


def _chip_patterns(x, y):
    return [(x, y), (1 - x, y), (x, 1 - y), (1 - x, 1 - y)]


def _lid(x, y, c):
    return 4 * x + 2 * y + c


def _gather_copy(outs, send_sems, recv_sems, a, k, block, to, src=None):
    blk = outs[a].at[_lid(*block)]
    return pltpu.make_async_remote_copy(
        src_ref=blk if src is None else src, dst_ref=blk,
        send_sem=send_sems.at[a, k], recv_sem=recv_sems.at[a, k], device_id=to, device_id_type=MESH)


def _gather_start(mine, outs, send_sems, recv_sems, local_sems, diagonal=False):
    x, y, c = _mesh_pos()
    me = (x, y, c)
    for a in range(len(mine)):
        pltpu.make_async_copy(mine[a], outs[a].at[_lid(*me)], local_sems.at[a]).start()
        targets = [(x, y, 1 - c), (1 - x, y, c), (x, 1 - y, c), (1 - x, 1 - y, c)]
        for k, to in enumerate(targets if diagonal else targets[:3]):
            _gather_copy(outs, send_sems, recv_sems, a, k, me, to, src=mine[a]).start()


def _gather_relay(mine, outs, send_sems, recv_sems, local_sems, via):
    x, y, c = _mesh_pos()
    me, sib = (x, y, c), (x, y, 1 - c)
    copy = functools.partial(_gather_copy, outs, send_sems, recv_sems)
    source = {1: (1 - x, y, c), 2: (x, 1 - y, c)}
    for a in range(len(mine)):
        for k in ((via[a], 3 - via[a]) if via[a] else (1, 2)):
            copy(a, k, source[k], me).wait_recv()
            if k == via[a]:
                copy(a, 3, source[k], source[3 - k]).start()
            copy(a, 3 + k, source[k], sib).start()


def _gather_finish(mine, outs, send_sems, recv_sems, local_sems):
    x, y, c = _mesh_pos()
    me, sib = (x, y, c), (x, y, 1 - c)
    copy = functools.partial(_gather_copy, outs, send_sems, recv_sems)
    diag = (1 - x, 1 - y)
    n = len(mine)
    for a in range(n):
        copy(a, 3, (*diag, c), me).wait_recv()
        copy(a, 6, (*diag, c), sib).start()
    for a in range(n):
        copy(a, 0, sib, me).wait_recv()
        for k, chip in zip((4, 5, 6), [(1 - x, y), (x, 1 - y), diag]):
            copy(a, k, (*chip, 1 - c), me).wait_recv()
        for k in range(7):
            copy(a, k, me, sib, src=mine[a]).wait_send()
        pltpu.make_async_copy(mine[a], outs[a].at[_lid(*me)], local_sems.at[a]).wait()


def _gather_scratch(n):
    return [pltpu.SemaphoreType.DMA((n, 7)), pltpu.SemaphoreType.DMA((n, 7)), pltpu.SemaphoreType.DMA((n,))]


def prepare_weights(w_in, w_out, w_up_t, w_down, convp):
    def body(win_ref, wout_ref, wup_ref, wdown_ref, convp_ref,
             sin_ref, sout_ref, sup_ref, sdown_ref, gconv_ref, send_sems, recv_sems, local_sems):
        gather = ([convp_ref], [gconv_ref], send_sems, recv_sems, local_sems)
        _gather_start(*gather, diagonal=True)
        sin_ref[...] = win_ref[...].astype(BF16)
        sout_ref[...] = wout_ref[...].astype(BF16)
        sup_ref[...] = wup_ref[...].T.astype(BF16)
        sdown_ref[...] = wdown_ref[...].astype(BF16)
        _gather_relay(*gather, via=[0])
        _gather_finish(*gather)

    return pl.pallas_call(
        body, name="prepare_weights",
        out_shape=[jax.ShapeDtypeStruct(w_in.shape, BF16), jax.ShapeDtypeStruct(w_out.shape, BF16),
                   jax.ShapeDtypeStruct(w_up_t.shape[::-1], BF16), jax.ShapeDtypeStruct(w_down.shape, BF16),
                   jax.ShapeDtypeStruct((N_DEV,) + convp.shape, F32)],
        in_specs=[VMEM] * 5, out_specs=[VMEM] * 4 + [ANY],
        scratch_shapes=_gather_scratch(1),
        compiler_params=pltpu.CompilerParams(vmem_limit_bytes=VMEM_LIMIT),
    )(w_in, w_out, w_up_t, w_down, convp)


def _chip_copies(p, land, send_sems, recv_sems):
    x, y, c = _mesh_pos()
    return [pltpu.make_async_remote_copy(
        src_ref=p[a].at[k], dst_ref=land[a].at[k], send_sem=send_sems.at[a, k], recv_sem=recv_sems.at[a, k],
        device_id=(px, py, c), device_id_type=MESH)
        for k, (px, py) in enumerate(_chip_patterns(x, y)[1:]) for a in range(len(p))]


def chip_partials(name, g, land, jidx, rb):
    _, r, c = g.shape

    def body(j_ref, g_ref, l_ref, o_ref):
        o_ref[...] = (g_ref[...] + l_ref[...]).astype(BF16)

    return pl.pallas_call(
        body, name=name,
        out_shape=jax.ShapeDtypeStruct((3, r, c), BF16),
        grid_spec=pltpu.PrefetchScalarGridSpec(
            num_scalar_prefetch=1, grid=(3, r // rb),
            in_specs=[pl.BlockSpec((1, rb, c), lambda k, i, j: (j[1 + k], i, 0)),
                      pl.BlockSpec((1, rb, c), lambda k, i, j: (1 + k, i, 0))],
            out_specs=pl.BlockSpec((1, rb, c), lambda k, i, j: (k, i, 0))),
        compiler_params=_params(("arbitrary", "arbitrary")),
    )(jidx, g, land)


def _adamw(w, g, m, v):
    m2 = ADAM_B1 * m + (1.0 - ADAM_B1) * g
    v2 = ADAM_B2 * v + (1.0 - ADAM_B2) * (g * g)
    m_hat = m2 / (1.0 - ADAM_B1 ** ADAM_STEP)
    v_hat = v2 / (1.0 - ADAM_B2 ** ADAM_STEP)
    delta = -ADAM_LR * (m_hat / (jnp.sqrt(v_hat) + ADAM_EPS) + ADAM_WD * w)
    return delta, m2, v2


def reduce_and_adamw(name, g, land, recv, w, m, v, jidx, rb):
    _, r, c = g.shape

    def body(j_ref, g_ref, l_ref, r_ref, w_ref, m_ref, v_ref, go_ref, do_ref, mo_ref, vo_ref):
        grad = (g_ref[0] + l_ref[0]) + r_ref[0].astype(F32) + r_ref[1].astype(F32) + r_ref[2].astype(F32)
        delta, m2, v2 = _adamw(w_ref[...], grad, m_ref[...], v_ref[...])
        go_ref[...] = grad
        do_ref[...] = delta
        mo_ref[...] = m2
        vo_ref[...] = v2

    blk = pl.BlockSpec((rb, c), lambda i, j: (i, 0))
    return pl.pallas_call(
        body, name=name,
        out_shape=[jax.ShapeDtypeStruct((r, c), F32)] * 4,
        grid_spec=pltpu.PrefetchScalarGridSpec(
            num_scalar_prefetch=1, grid=(r // rb,),
            in_specs=[pl.BlockSpec((1, rb, c), lambda i, j: (j[0], i, 0)),
                      pl.BlockSpec((1, rb, c), lambda i, j: (0, i, 0)),
                      pl.BlockSpec((3, rb, c), lambda i, j: (0, i, 0)),
                      blk, blk, blk],
            out_specs=[blk] * 4),
        compiler_params=_params(("arbitrary",)),
    )(jidx, g, land, recv, w, m, v)


def mixer_reduce(grads, svec):
    n = len(grads)
    shard = [g.shape[1:] for g in grads]

    def body(*refs):
        g = refs[:n]
        sv_ref = refs[n]
        outs = refs[n + 1:2 * n + 1]
        sv_slots = refs[2 * n + 1]
        rest = refs[2 * n + 2:]
        own, land, sendb, recvb = rest[:n], rest[n:2 * n], rest[2 * n:3 * n], rest[3 * n:4 * n]
        sv_land, chip_sv, d2d_send, d2d_recv, ici_send, ici_recv, local_sems, sv_sems = rest[4 * n:]
        x, y, c = _mesh_pos()
        sib = (x, y, 1 - c)
        pats = _chip_patterns(x, y)
        q = 2 * x + y

        d2d, local = {}, {}
        for a in range(n):
            for k, (px, py) in enumerate(pats):
                d2d[a, k] = pltpu.make_async_remote_copy(
                    src_ref=g[a].at[_lid(px, py, 1 - c)], dst_ref=land[a].at[k],
                    send_sem=d2d_send.at[a, k], recv_sem=d2d_recv.at[a, k], device_id=sib, device_id_type=MESH)
                local[a, k] = pltpu.make_async_copy(g[a].at[_lid(px, py, c)], own[a].at[k], local_sems.at[a, k])
        sv_d2d = pltpu.make_async_remote_copy(
            src_ref=sv_ref, dst_ref=sv_land, send_sem=d2d_send.at[n, 0], recv_sem=d2d_recv.at[n, 0],
            device_id=sib, device_id_type=MESH)
        blocks = [(a, k) for a in range(n) for k in (1, 2, 3)] + [(a, 0) for a in range(n)]
        sv_d2d.start()
        for b in blocks:
            d2d[b].start()
            local[b].start()

        half_rows = svec.shape[0] // 2
        rows = pl.ds(pl.multiple_of(c * half_rows, 8), half_rows)
        sv_local = pltpu.make_async_copy(chip_sv, sv_slots.at[q], sv_sems.at[0])

        def sv_ici(k, slot, to):
            return pltpu.make_async_remote_copy(
                src_ref=chip_sv.at[rows], dst_ref=sv_slots.at[slot, rows], send_sem=sv_sems.at[1 + k],
                recv_sem=sv_sems.at[4 + k], device_id=to, device_id_type=MESH)

        def sv_pass_on(k, slot):
            return pltpu.make_async_remote_copy(
                src_ref=sv_slots.at[slot, rows], dst_ref=sv_slots.at[slot, rows], send_sem=sv_sems.at[7 + k],
                recv_sem=sv_sems.at[10 + k], device_id=sib, device_id_type=MESH)

        sv_d2d.wait()
        chip_sv[...] = sv_ref[...] + sv_land[...]
        sv_out = [sv_ici(k, q, (px, py, c)) for k, (px, py) in enumerate(pats[1:])]
        for cp in sv_out + [sv_local]:
            cp.start()

        ici = _chip_copies(sendb, recvb, ici_send, ici_recv)
        for a, k in blocks:
            local[a, k].wait()
            d2d[a, k].wait()
            if k > 0:
                sendb[a][k - 1] = (own[a][k] + land[a][k]).astype(BF16)
                ici[(k - 1) * n + a].start()
        for k, (px, py) in enumerate(pats[1:]):
            sv_out[k].wait_send()
            sv_ici(k, 2 * px + py, (px, py, c)).wait_recv()
            sv_pass_on(k, 2 * px + py).start()
        for a in range(n):
            for k in range(3):
                ici[k * n + a].wait()
            outs[a][...] = ((own[a][0] + land[a][0]) + recvb[a][0].astype(F32) + recvb[a][1].astype(F32)
                            + recvb[a][2].astype(F32))
        for k, (px, py) in enumerate(pats[1:]):
            sv_pass_on(k, 2 * px + py).wait()
        sv_local.wait()

    shard_out = [jax.ShapeDtypeStruct(s, F32) for s in shard]
    return pl.pallas_call(
        body, name="mixer_reduce",
        out_shape=shard_out + [jax.ShapeDtypeStruct((4,) + svec.shape, F32)],
        in_specs=[ANY] * n + [VMEM], out_specs=[VMEM] * n + [ANY],
        scratch_shapes=[pltpu.VMEM((4,) + s, F32) for s in shard] + [pltpu.VMEM((4,) + s, F32) for s in shard]
        + [pltpu.VMEM((3,) + s, BF16) for s in shard] + [pltpu.VMEM((3,) + s, BF16) for s in shard]
        + [pltpu.VMEM(svec.shape, F32), pltpu.VMEM(svec.shape, F32),
           pltpu.SemaphoreType.DMA((n + 1, 4)), pltpu.SemaphoreType.DMA((n + 1, 4)),
           pltpu.SemaphoreType.DMA((n, 3)), pltpu.SemaphoreType.DMA((n, 3)),
           pltpu.SemaphoreType.DMA((n, 4)), pltpu.SemaphoreType.DMA((13,))],
        compiler_params=pltpu.CompilerParams(vmem_limit_bytes=VMEM_LIMIT),
    )(*grads, svec)


SMALL_LAYOUT = [
    ("b_in", S_BIN, 16), ("ln_a_g", S_LNAG, 4), ("ln_a_b", S_LNAB, 4), ("w_spatial", S_WS, 512),
    ("b_spatial", S_BS, 4), ("conv_b_b", S_CBB, 4), ("ln_b_g", S_LNBG, 4), ("ln_b_b", S_LNBB, 4),
    ("b_out", S_BOUT, 8), ("ln1_g", S_LN1G, 8), ("ln1_b", S_LN1B, 8), ("conv_f_b", S_CFB, 44),
    ("ln2_g", S_LN2G, 8), ("ln2_b", S_LN2B, 8),
]


def small_adamw(sv_slots, ws, ms, vs, shard_grads, shard_ws, shard_ms, shard_vs):
    n = len(SMALL_LAYOUT)
    nb = len(shard_grads)

    def body(*refs):
        s_ref = refs[0]
        w_refs, m_refs, v_refs = refs[1:1 + n], refs[1 + n:1 + 2 * n], refs[1 + 2 * n:1 + 3 * n]
        big_in = refs[1 + 3 * n:1 + 3 * n + 4 * nb]
        outs = refs[1 + 3 * n + 4 * nb:]
        big_out = outs[4 * n + 1:]
        for p in range(nb):
            grad = big_in[p][...]
            delta, m2, v2 = _adamw(big_in[nb + p][...], grad, big_in[2 * nb + p][...], big_in[3 * nb + p][...])
            big_out[p][...] = grad
            big_out[nb + p][...] = delta
            big_out[2 * nb + p][...] = m2
            big_out[3 * nb + p][...] = v2
        for p, (_, row0, rows) in enumerate(SMALL_LAYOUT):
            sl = pl.ds(row0, rows)
            grad = ((s_ref[0, sl, :] + s_ref[1, sl, :]) + s_ref[2, sl, :]) + s_ref[3, sl, :]
            delta, m2, v2 = _adamw(w_refs[p][...], grad, m_refs[p][...], v_refs[p][...])
            outs[p][...] = grad
            outs[n + p][...] = delta
            outs[2 * n + p][...] = m2
            outs[3 * n + p][...] = v2
        sl = pl.ds(S_LOSS, 8)
        outs[4 * n][...] = ((s_ref[0, sl, :] + s_ref[1, sl, :]) + s_ref[2, sl, :]) + s_ref[3, sl, :]

    shapes = [jax.ShapeDtypeStruct((rows, 128), F32) for _, _, rows in SMALL_LAYOUT]
    big_shapes = [jax.ShapeDtypeStruct(g.shape, F32) for g in shard_grads]
    return pl.pallas_call(
        body, name="small_adamw",
        out_shape=shapes * 4 + [jax.ShapeDtypeStruct((8, 128), F32)] + big_shapes * 4,
        in_specs=[VMEM] * (1 + 3 * n + 4 * nb), out_specs=[VMEM] * (4 * n + 1 + 4 * nb),
        compiler_params=pltpu.CompilerParams(vmem_limit_bytes=VMEM_LIMIT),
    )(sv_slots, *ws, *ms, *vs, *shard_grads, *shard_ws, *shard_ms, *shard_vs)


def mix_forward(x, sin, sout, b_in, ln_a_g, ln_a_b, w_spatial, bst, conv_b_w, conv_b_b, ln_b_g, ln_b_b,
                b_out, ln1_g, ln1_b, sup, sdown, tm):
    t = x.shape[0]
    nt = t // tm
    n_chunks = tm // CHUNK

    def body(x_ref, sin_ref, sout_ref, bin_ref, ga_ref, ba_ref, ws_ref, bst_ref, cw_ref, cb_ref, gb_ref,
             bb_ref, bout_ref, g1_ref, b1_ref, sup_ref, sdown_ref,
             h_ref, xhat1_ref, rstd1_ref, yb1_ref, gin_ref, gout_ref, gup_ref, gdown_ref,
             ext_ref, y_ref, wsm_ref, win_ref, wout_ref, load_sems,
             mix_send, mix_recv, mix_local, send_sems, recv_sems, local_sems):
        i = pl.program_id(0)
        mixer = ([sin_ref, sout_ref], [gin_ref, gout_ref], mix_send, mix_recv, mix_local)
        gather = ([sup_ref, sdown_ref], [gup_ref, gdown_ref], send_sems, recv_sems, local_sems)

        @pl.when(i == 0)
        def _():
            _gather_start(*mixer)
            _gather_relay(*mixer, via=[2, 2])
            _gather_finish(*mixer)
            _gather_start(*gather)
            loads = [pltpu.make_async_copy(gin_ref, win_ref, load_sems.at[0]),
                     pltpu.make_async_copy(gout_ref, wout_ref, load_sems.at[1])]
            for cp in loads:
                cp.start()
            for cp in loads:
                cp.wait()
            ext_ref[0:HALO_B, :] = jnp.zeros((HALO_B, D_B), F32)
            mask = _tril_mask()
            for hd in range(HEADS):
                wsm_ref[hd] = jnp.where(mask, ws_ref[hd], 0.0).astype(BF16)

        xb = x_ref[...].astype(BF16)
        for j in range(N_DEV):
            cols = slice(j * W_IN_BLK, (j + 1) * W_IN_BLK)
            h_ref[:, cols] = _nn(xb, win_ref[j]) + bin_ref[:, cols]

        def chunk(ci):
            r = _rows(ci, CHUNK)
            for hd in range(HEADS):
                _, _, u, _, _, _, _, _, sv = _mixer_a_head(h_ref, r, hd, ga_ref, ba_ref, wsm_ref, bst_ref)
                y_ref[r, hd * HEAD_DIM:(hd + 1) * HEAD_DIM] = (u * sv).astype(BF16)
            a_b = h_ref[r, 2 * D_A:2 * D_A + D_B]
            g_b = h_ref[r, 2 * D_A + D_B:D_IN]
            ext_ref[pl.ds(HALO_B + ci * CHUNK, CHUNK), :] = a_b * _sigmoid(g_b)

        _loop(n_chunks, chunk)

        def conv_rows(bi):
            base = bi * ROWS
            yb1 = _conv_b_block(ext_ref, base, cw_ref) + cb_ref[...]
            yb1_ref[pl.ds(base, ROWS), :] = yb1
            xhat, _ = _ln_stats(yb1)
            yb2 = xhat * gb_ref[...] + bb_ref[...]
            y_ref[pl.ds(base, ROWS), D_A:D] = (yb2 * _sigmoid(yb2)).astype(BF16)

        _loop(tm // ROWS, conv_rows)
        ext_ref[0:HALO_B, :] = ext_ref[tm:tm + HALO_B, :]

        mix = _nn(y_ref[...], wout_ref[...].reshape(D, D)) + bout_ref[...]
        xhat1, rstd1 = _ln_stats(ALPHA * x_ref[...] + mix)
        xhat1_ref[...] = xhat1
        rstd1_ref[...] = jnp.broadcast_to(rstd1, (tm, 128))

        @pl.when(i == (5 * nt) // 8)
        def _():
            _gather_relay(*gather, via=[1, 2])

        @pl.when(i == nt - 1)
        def _():
            _gather_finish(*gather)

    row = lambda w: pl.BlockSpec((tm, w), lambda i: (i, 0))
    return pl.pallas_call(
        body, name="mix_forward", grid=(nt,),
        in_specs=[row(D), ANY, ANY, _full(b_in.shape), _full(ln_a_g.shape),
                  _full(ln_a_b.shape), _full(w_spatial.shape), _full(bst.shape),
                  _full(conv_b_w.shape), _full(conv_b_b.shape), _full(ln_b_g.shape),
                  _full(ln_b_b.shape), _full(b_out.shape),
                  _full(ln1_g.shape), _full(ln1_b.shape), ANY, ANY],
        out_specs=[row(D_IN), row(D), row(128), row(D_B), ANY, ANY, ANY, ANY],
        out_shape=[jax.ShapeDtypeStruct((t, D_IN), F32), jax.ShapeDtypeStruct((t, D), F32),
                   jax.ShapeDtypeStruct((t, 128), F32), jax.ShapeDtypeStruct((t, D_B), F32)]
        + [jax.ShapeDtypeStruct((N_DEV,) + sh.shape, BF16) for sh in (sin, sout, sup, sdown)],
        scratch_shapes=[pltpu.VMEM((tm + HALO_B, D_B), F32), pltpu.VMEM((tm, D), BF16),
                        pltpu.VMEM((HEADS, CHUNK, CHUNK), BF16),
                        pltpu.VMEM((N_DEV,) + sin.shape, BF16), pltpu.VMEM((N_DEV,) + sout.shape, BF16),
                        pltpu.SemaphoreType.DMA((2,))] + _gather_scratch(2) + _gather_scratch(2),
        compiler_params=_params(("arbitrary",)),
    )(x, sin, sout, b_in, ln_a_g, ln_a_b, w_spatial, bst, conv_b_w, conv_b_b, ln_b_g, ln_b_b,
      b_out, ln1_g, ln1_b, sup, sdown)


def ffn_forward(xhat1, ln1_g, ln1_b, wup_g, cfw, cfb, wdown, ln2_g, ln2_b, target, tm):
    t = xhat1.shape[0]
    nt = t // tm

    def body(xh_ref, g1_ref, b1_ref, wup_ref, cfw_ref, cfb_ref, wdown_ref, g2_ref, b2_ref, tgt_ref,
             hu_ref, gv_ref, dr2_ref, loss_ref, sln2_ref,
             x1_ref, x1b_ref, hu32_ref, carry_ref, gbuf_ref, ffn_ref, acc_loss, acc_g2, acc_b2):
        i = pl.program_id(0)

        @pl.when(i == 0)
        def _():
            carry_ref[...] = jnp.zeros(carry_ref.shape, F32)
            acc_loss[...] = jnp.zeros(acc_loss.shape, F32)
            acc_g2[...] = jnp.zeros(acc_g2.shape, F32)
            acc_b2[...] = jnp.zeros(acc_b2.shape, F32)

        x1 = xh_ref[...] * g1_ref[...] + b1_ref[...]
        x1_ref[...] = x1
        x1b_ref[...] = x1.astype(BF16)

        def conv(g, j, base):
            if base == 0:
                win = jnp.concatenate([carry_ref[j], hu32_ref[g, 0:ROWS, :]], axis=0)
            else:
                win = hu32_ref[g, base - HALO_F:base + ROWS, :]
            taps = _taps_f(win)
            w = cfw_ref[j]
            return sum(taps[k] * w[k:k + 1, :] for k in range(KF)) + cfb_ref[j:j + 1, :]

        for f in range(N_F):
            hu32_ref[0] = _nn(x1b_ref[...], wup_ref[f])
            hu32_ref[1] = _nn(x1b_ref[...], wup_ref[N_F + f])

            def rows(bi, f=f):
                r = _rows(bi)
                gate = conv(0, f, bi * ROWS)
                val = conv(1, N_F + f, bi * ROWS)
                gbuf_ref[r, :] = (gate * _sigmoid(gate) * val).astype(BF16)
                gv_ref[f, r, :] = gate.astype(BF16)
                gv_ref[N_F + f, r, :] = val.astype(BF16)
                hu_ref[f, r, :] = hu32_ref[0, r, :].astype(BF16)
                hu_ref[N_F + f, r, :] = hu32_ref[1, r, :].astype(BF16)

            _loop(tm // ROWS, rows)
            carry_ref[f] = hu32_ref[0, tm - HALO_F:tm, :]
            carry_ref[N_F + f] = hu32_ref[1, tm - HALO_F:tm, :]
            part = _nn(gbuf_ref[...], wdown_ref[f])
            if f == 0:
                ffn_ref[...] = part
            else:
                ffn_ref[...] += part

        def tail(bi):
            r = _rows(bi, LN_ROWS)
            xhat2, rstd2 = _ln_stats(ALPHA * x1_ref[r, :] + ffn_ref[r, :])
            err = xhat2 * g2_ref[...] + b2_ref[...] - tgt_ref[r, :]
            e2 = _rsum8(err * err)
            acc_loss[...] += sum(e2[:, k * 128:(k + 1) * 128] for k in range(D // 128))
            dy = err * (1.0 / D)
            acc_g2[...] += _rsum8(dy * xhat2)
            acc_b2[...] += _rsum8(dy)
            dr2_ref[r, :] = _ln_bwd(dy * g2_ref[...], xhat2, rstd2)

        _loop(tm // LN_ROWS, tail)
        loss_ref[...] = acc_loss[...]

        @pl.when(i == nt - 1)
        def _():
            dg = jnp.sum(acc_g2[...], axis=0, keepdims=True)
            db = jnp.sum(acc_b2[...], axis=0, keepdims=True)
            for k in range(D // 128):
                sln2_ref[k:k + 1, :] = dg[:, k * 128:(k + 1) * 128]
                sln2_ref[8 + k:9 + k, :] = db[:, k * 128:(k + 1) * 128]

    row = pl.BlockSpec((tm, D), lambda i: (i, 0))
    return pl.pallas_call(
        body, name="ffn_forward", grid=(nt,),
        in_specs=[row, _full(ln1_g.shape), _full(ln1_b.shape), _resident(wup_g.shape),
                  _full(cfw.shape), _full(cfb.shape), _resident(wdown.shape),
                  _full(ln2_g.shape), _full(ln2_b.shape), row],
        out_specs=[pl.BlockSpec((N_DEV, tm, W_UP_BLK), lambda i: (0, i, 0)),
                   pl.BlockSpec((N_DEV, tm, W_UP_BLK), lambda i: (0, i, 0)), row,
                   _full((8, 128)), _full((16, 128))],
        out_shape=[jax.ShapeDtypeStruct((N_DEV, t, W_UP_BLK), BF16),
                   jax.ShapeDtypeStruct((N_DEV, t, W_UP_BLK), BF16), jax.ShapeDtypeStruct((t, D), F32),
                   jax.ShapeDtypeStruct((8, 128), F32), jax.ShapeDtypeStruct((16, 128), F32)],
        scratch_shapes=[pltpu.VMEM((tm, D), F32), pltpu.VMEM((tm, D), BF16),
                        pltpu.VMEM((2, tm, W_UP_BLK), F32),
                        pltpu.VMEM((N_DEV, HALO_F, W_UP_BLK), F32), pltpu.VMEM((tm, W_UP_BLK), BF16),
                        pltpu.VMEM((tm, D), F32), pltpu.VMEM((8, 128), F32),
                        pltpu.VMEM((8, D), F32), pltpu.VMEM((8, D), F32)],
        compiler_params=_params(("arbitrary",)),
    )(xhat1, ln1_g, ln1_b, wup_g, cfw, cfb, wdown, ln2_g, ln2_b, target)


def ffn_backward(order, dr2, xhat1, ln1_g, ln1_b, hu, gv, wup_g, cfw, wdown, tm):
    t = dr2.shape[0]
    nt = t // tm
    sub_rows = tm
    hu4 = hu.reshape(2, N_F, t, W_UP_BLK)
    gv4 = gv.reshape(2, N_F, t, W_UP_BLK)
    wup4 = wup_g.reshape(2, N_F, D, W_UP_BLK)
    cfw4 = cfw.reshape(2, N_F, KF, W_UP_BLK)

    def body(order_ref, dr2_ref, xh_ref, g1_ref, b1_ref, hu_ref, gv_ref, wup_ref, cfw_ref, wdown_ref,
             dwup_ref, dwdown_ref, dcfw_ref, dcfb_ref, dx1_ref, land_up_ref, land_down_ref,
             x1b_ref, drb_ref, dg_ref, dextg_ref, dextv_ref, gbuf_ref,
             dhug_ref, dhuv_ref, acc_wup, acc_wdown, acc_cfw, acc_cfb, sem, send_sems, recv_sems):
        fo = pl.program_id(0)
        f = order_ref[fo]
        f_prev = order_ref[jnp.maximum(fo - 1, 0)]
        slot = fo % 2
        i = pl.program_id(1)
        x, y, c = _mesh_pos()
        half = D_FF // N_DEV

        def to_sibling(fi, k, src, land_ref, shard_chip):
            d = jnp.bitwise_xor(shard_chip, 2 * x + y)
            slot = jnp.where(d == 1, 2, jnp.where(d == 2, 1, d))
            return pltpu.make_async_remote_copy(
                src_ref=src, dst_ref=land_ref.at[slot], send_sem=send_sems.at[fi, k], recv_sem=recv_sems.at[fi, k],
                device_id=(x, y, 1 - c), device_id_type=MESH)

        def up_copy(fi, g):
            return to_sibling(fi, g, dwup_ref.at[g, fi], land_up_ref, 2 * g + fi // 2)

        def down_copy(fi):
            return to_sibling(fi, 2, dwdown_ref.at[fi, pl.ds((1 - c) * half, half)], land_down_ref, fi)

        def flush(fi, s):
            return [pltpu.make_async_copy(acc_wup.at[s, 0], dwup_ref.at[0, fi], sem.at[s, 0]),
                    pltpu.make_async_copy(acc_wup.at[s, 1], dwup_ref.at[1, fi], sem.at[s, 1]),
                    pltpu.make_async_copy(acc_wdown.at[s], dwdown_ref.at[fi], sem.at[s, 2])]

        def flushed(fi, s):
            for cp in flush(fi, s):
                cp.wait()
            down_copy(fi).start()

            @pl.when(fi % 2 != c)
            def _():
                up_copy(fi, 0).start()
                up_copy(fi, 1).start()

        @pl.when(i == 0)
        def _():
            acc_wup[slot] = jnp.zeros(acc_wup.shape[1:], F32)
            acc_wdown[slot] = jnp.zeros(acc_wdown.shape[1:], F32)
            acc_cfw[...] = jnp.zeros(acc_cfw.shape, F32)
            acc_cfb[...] = jnp.zeros(acc_cfb.shape, F32)
            dextg_ref[tm:tm + HALO_F, :] = jnp.zeros((HALO_F, W_UP_BLK), F32)
            dextv_ref[tm:tm + HALO_F, :] = jnp.zeros((HALO_F, W_UP_BLK), F32)

        w = [cfw_ref[0, 0], cfw_ref[1, 0]]
        dext = [dextg_ref, dextv_ref]
        dhu = [dhug_ref, dhuv_ref]

        def rows1(bi):
            r = _rows(bi)
            gate = gv_ref[0, 0, r, :].astype(F32)
            val = gv_ref[1, 0, r, :].astype(F32)
            sg = _sigmoid(gate)
            silu = gate * sg
            gbuf_ref[r, :] = (silu * val).astype(BF16)
            dg = dg_ref[r, :]
            dgate = dg * val * (sg * (1.0 + gate * (1.0 - sg)))
            dval = dg * silu
            dextg_ref[r, :] = dgate
            dextv_ref[r, :] = dval
            acc_cfb[0:8, :] += _rsum8(dgate)
            acc_cfb[8:16, :] += _rsum8(dval)

        def rows2(bi):
            r = _rows(bi)
            for g in range(2):
                win = dext[g][pl.ds(bi * ROWS, ROWS + HALO_F), :]
                n = ROWS + HALO_F
                later = [pltpu.roll(win, n - 2, 0)[0:ROWS, :], pltpu.roll(win, n - 1, 0)[0:ROWS, :],
                         win[0:ROWS, :]]
                d = sum(later[k] * w[g][k:k + 1, :] for k in range(KF))
                dhu[g][r, :] = d.astype(BF16)
                pre = hu_ref[g, 0, r, :].astype(F32)
                for k in range(KF):
                    r0 = 8 * (g * KF + k)
                    acc_cfw[r0:r0 + 8, :] += _rsum8(later[k] * pre)

        for sub in reversed(range(tm // sub_rows)):
            rs = slice(sub * sub_rows, (sub + 1) * sub_rows)
            blocks = range(sub * sub_rows // ROWS, (sub + 1) * sub_rows // ROWS)
            x1b_ref[rs, :] = (xh_ref[rs, :] * g1_ref[...] + b1_ref[...]).astype(BF16)
            drb_ref[rs, :] = dr2_ref[rs, :].astype(BF16)
            dg_ref[rs, :] = _nt(drb_ref[rs, :], wdown_ref[0])
            for bi in blocks:
                rows1(bi)
            for bi in blocks:
                rows2(bi)
            acc_wdown[slot] += _tn(gbuf_ref[rs, :], drb_ref[rs, :])
            acc_wup[slot, 0] += _tn(dhug_ref[rs, :], x1b_ref[rs, :])
            acc_wup[slot, 1] += _tn(dhuv_ref[rs, :], x1b_ref[rs, :])
            dx1_ref[0, rs, :] = (_nt(dhug_ref[rs, :], wup_ref[0, 0])
                                 + _nt(dhuv_ref[rs, :], wup_ref[1, 0])).astype(BF16)
        dextg_ref[tm:tm + HALO_F, :] = dextg_ref[0:HALO_F, :]
        dextv_ref[tm:tm + HALO_F, :] = dextv_ref[0:HALO_F, :]

        @pl.when(i == nt - 1)
        def _():
            for g in range(2):
                dcfb_ref[g, 0] = jnp.sum(acc_cfb[8 * g:8 * g + 8, :], axis=0, keepdims=True)
                for k in range(KF):
                    r0 = 8 * (g * KF + k)
                    dcfw_ref[g, 0, k:k + 1, :] = jnp.sum(acc_cfw[r0:r0 + 8, :], axis=0, keepdims=True)
            for cp in flush(f, slot):
                cp.start()

        @pl.when((i == 0) & (fo > 0))
        def _():
            flushed(f_prev, 1 - slot)

        @pl.when((i == nt - 1) & (fo == N_F - 1))
        def _():
            flushed(f, slot)
            for fi in range(N_F):
                down_copy(fi).wait()
                for g in range(2):
                    @pl.when(fi % 2 != c)
                    def _():
                        up_copy(fi, g).wait_send()

                    @pl.when(fi % 2 == c)
                    def _():
                        up_copy(fi, g).wait_recv()

    rev = lambda i: nt - 1 - i
    row = pl.BlockSpec((tm, D), lambda fo, i, o: (rev(i), 0))
    pair = lambda r, c: pl.BlockSpec((2, 1, r, c), lambda fo, i, o: (0, o[fo], 0, 0))
    tile = pl.BlockSpec((2, 1, tm, W_UP_BLK), lambda fo, i, o: (0, o[fo], rev(i), 0))
    return pl.pallas_call(
        body, name="ffn_backward",
        grid_spec=pltpu.PrefetchScalarGridSpec(
            num_scalar_prefetch=1, grid=(N_F, nt),
            in_specs=[row, row, _full(ln1_g.shape), _full(ln1_b.shape), tile, tile,
                      pair(D, W_UP_BLK), pair(KF, W_UP_BLK),
                      pl.BlockSpec((1, W_UP_BLK, D), lambda fo, i, o: (o[fo], 0, 0))],
            out_specs=[ANY, ANY, pair(KF, W_UP_BLK), pair(1, W_UP_BLK),
                       pl.BlockSpec((1, tm, D), lambda fo, i, o: (o[fo], rev(i), 0)), ANY, ANY],
            scratch_shapes=[pltpu.VMEM((tm, D), BF16), pltpu.VMEM((tm, D), BF16),
                            pltpu.VMEM((tm, W_UP_BLK), F32),
                            pltpu.VMEM((tm + HALO_F, W_UP_BLK), F32), pltpu.VMEM((tm + HALO_F, W_UP_BLK), F32),
                            pltpu.VMEM((tm, W_UP_BLK), BF16), pltpu.VMEM((tm, W_UP_BLK), BF16),
                            pltpu.VMEM((tm, W_UP_BLK), BF16),
                            pltpu.VMEM((2, 2, W_UP_BLK, D), F32), pltpu.VMEM((2, W_UP_BLK, D), F32),
                            pltpu.VMEM((2 * KF * 8, W_UP_BLK), F32), pltpu.VMEM((16, W_UP_BLK), F32),
                            pltpu.SemaphoreType.DMA((2, 3)),
                            pltpu.SemaphoreType.DMA((N_F, 3)), pltpu.SemaphoreType.DMA((N_F, 3))]),
        out_shape=[jax.ShapeDtypeStruct((2, N_F, W_UP_BLK, D), F32),
                   jax.ShapeDtypeStruct((N_F, W_UP_BLK, D), F32),
                   jax.ShapeDtypeStruct((2, N_F, KF, W_UP_BLK), F32),
                   jax.ShapeDtypeStruct((2, N_F, 1, W_UP_BLK), F32),
                   jax.ShapeDtypeStruct((N_F, t, D), BF16),
                   jax.ShapeDtypeStruct((4, W_UP_BLK, D), F32),
                   jax.ShapeDtypeStruct((4, D_FF // N_DEV, D), F32)],
        compiler_params=_params(("arbitrary", "arbitrary")),
    )(order, dr2, xhat1, ln1_g, ln1_b, hu4, gv4, wup4, cfw4, wdown)


def mix_backward(x, h, yb1, dx1p, dr2, xhat1, rstd1, win_g, ln_a_g, ln_a_b, w_spatial, bst,
                 conv_b_w, ln_b_g, ln_b_b, wout, ln1_g, ffn_partials, tm):
    t = x.shape[0]
    n_p = len(ffn_partials)
    nt = t // tm
    n_chunks = tm // CHUNK
    halo_blocks = tm // HALO_B

    def body(x_ref, h_ref, halo_ref, yb1_ref, dx1p_ref, dr2_ref, xh1_ref, rstd1_ref, win_ref, ga_ref, ba_ref,
             ws_ref, bst_ref, cw_ref, gb_ref, bb_ref, wout_ref, g1_ref, *rest):
        p_refs, rest = rest[:n_p], rest[n_p:]
        gx_ref, dwin_ref, dwout_ref, dcw_ref, small_ref = rest[:5]
        land_refs, rest = rest[5:5 + n_p], rest[5 + n_p:]
        (ext_ref, dext_ref, y_ref, dy_ref, dh_ref, dmb_ref, wsm_ref,
         acc_win, acc_wout, acc_bin, acc_lnag, acc_lnab, acc_ws, acc_bs, acc_cbb, acc_lnbg,
         acc_lnbb, acc_bout, acc_ln1g, acc_ln1b, acc_cw, sem, send_sems, recv_sems) = rest
        i = pl.program_id(0)

        @pl.when(i == 0)
        def _():
            for cp in _chip_copies(p_refs, land_refs, send_sems, recv_sems):
                cp.start()

        first_tile = i == nt - 1
        accs = [acc_win, acc_wout, acc_bin, acc_lnag, acc_lnab, acc_ws, acc_bs, acc_cbb, acc_lnbg,
                acc_lnbb, acc_bout, acc_ln1g, acc_ln1b, acc_cw]

        @pl.when(i == 0)
        def _():
            for acc in accs:
                acc[...] = jnp.zeros(acc.shape, F32)
            dext_ref[tm:tm + HALO_B, :] = jnp.zeros((HALO_B, D_B), F32)
            mask = _tril_mask()
            for hd in range(HEADS):
                wsm_ref[hd] = jnp.where(mask, ws_ref[hd], 0.0).astype(BF16)

        def ln1_rows(bi):
            r = _rows(bi, LN_ROWS)
            part = [dx1p_ref[f, r, :].astype(F32) for f in range(N_F)]
            dx1 = ALPHA * dr2_ref[r, :] + ((part[0] + part[1]) + (part[2] + part[3]))
            xhat = xh1_ref[r, :]
            acc_ln1g[...] += _rsum8(dx1 * xhat)
            acc_ln1b[...] += _rsum8(dx1)
            dr1 = _ln_bwd(dx1 * g1_ref[...], xhat, rstd1_ref[r, 0:1])
            acc_bout[...] += _rsum8(dr1)
            gx_ref[r, :] = ALPHA * dr1
            dmb_ref[r, :] = dr1.astype(BF16)

        _loop(tm // LN_ROWS, ln1_rows)
        dy_ref[...] = _nt(dmb_ref[...], wout_ref[...])

        ha = halo_ref[:, 0:D_B]
        hg = halo_ref[:, D_B:2 * D_B]
        ext_ref[0:HALO_B, :] = jnp.where(first_tile, 0.0, 1.0) * (ha * _sigmoid(hg))

        def chunk(ci):
            r = _rows(ci, CHUNK)
            for hd in range(HEADS):
                sl = slice(hd * HEAD_DIM, (hd + 1) * HEAD_DIM)
                rows8 = slice(8 * hd, 8 * hd + 8)
                hus, hvs, u, cdf_u, cdf_v, xhat, rstd, vn, sv = _mixer_a_head(
                    h_ref, r, hd, ga_ref, ba_ref, wsm_ref, bst_ref)
                dy_a = dy_ref[r, sl]
                y_ref[r, sl] = (u * sv).astype(BF16)
                du = dy_a * sv
                dsv = dy_a * u
                dsvb = dsv.astype(BF16)
                acc_bs[hd] += dsv
                acc_ws[hd] += _nt(dsvb, vn)
                dvn = _tn(wsm_ref[hd], dsvb)
                acc_lnag[rows8, :] += _rsum8(dvn * xhat)
                acc_lnab[rows8, :] += _rsum8(dvn)
                dv = _ln_bwd(dvn * ga_ref[hd:hd + 1, :], xhat, rstd)
                slv = slice(D_A + hd * HEAD_DIM, D_A + (hd + 1) * HEAD_DIM)
                dhu = du * (cdf_u + hus * jnp.exp(-0.5 * hus * hus) * INV_SQRT_2PI)
                dhv = dv * (cdf_v + hvs * jnp.exp(-0.5 * hvs * hvs) * INV_SQRT_2PI)
                acc_bin[:, sl] += _rsum8(dhu)
                acc_bin[:, slv] += _rsum8(dhv)
                dh_ref[r, sl] = dhu.astype(BF16)
                dh_ref[r, slv] = dhv.astype(BF16)
            a_b = h_ref[r, 2 * D_A:2 * D_A + D_B]
            g_b = h_ref[r, 2 * D_A + D_B:D_IN]
            ext_ref[pl.ds(HALO_B + ci * CHUNK, CHUNK), :] = a_b * _sigmoid(g_b)

        _loop(n_chunks, chunk)

        def conv_rows(bi):
            base = bi * ROWS
            r = pl.ds(base, ROWS)
            xhat, rstd = _ln_stats(yb1_ref[r, :])
            yb2 = xhat * gb_ref[...] + bb_ref[...]
            sg = _sigmoid(yb2)
            y_ref[r, D_A:D] = (yb2 * sg).astype(BF16)
            dyb2 = dy_ref[r, D_A:D] * (sg * (1.0 + yb2 * (1.0 - sg)))
            acc_lnbg[...] += _rsum8(dyb2 * xhat)
            acc_lnbb[...] += _rsum8(dyb2)
            dyb1 = _ln_bwd(dyb2 * gb_ref[...], xhat, rstd)
            acc_cbb[...] += _rsum8(dyb1)
            dext_ref[r, :] = dyb1
            for k, tap in _taps(ext_ref[pl.ds(base, ROWS + HALO_B), :], CONV_B_OFFSETS):
                acc_cw[8 * k:8 * k + 8, :] += _rsum8(dyb1 * tap)

        _loop(tm // ROWS, conv_rows)

        def convt_rows(bi):
            base = bi * ROWS
            r = pl.ds(base, ROWS)
            dyb0 = jnp.zeros((ROWS, D_B), F32)
            for k, tap in _taps(dext_ref[pl.ds(base, ROWS + HALO_B), :], CONV_B_T_OFFSETS):
                dyb0 = dyb0 + tap * cw_ref[k:k + 1, :]
            a_b = h_ref[r, 2 * D_A:2 * D_A + D_B]
            sg = _sigmoid(h_ref[r, 2 * D_A + D_B:D_IN])
            da_b = dyb0 * sg
            dg_b = dyb0 * a_b * sg * (1.0 - sg)
            acc_bin[:, 2 * D_A:2 * D_A + D_B] += _rsum8(da_b)
            acc_bin[:, 2 * D_A + D_B:D_IN] += _rsum8(dg_b)
            dh_ref[r, 2 * D_A:2 * D_A + D_B] = da_b.astype(BF16)
            dh_ref[r, 2 * D_A + D_B:D_IN] = dg_b.astype(BF16)

        _loop(tm // ROWS, convt_rows)
        dext_ref[tm:tm + HALO_B, :] = dext_ref[0:HALO_B, :]

        acc_wout[...] += _tn(y_ref[...], dmb_ref[...])
        xt = x_ref[...].T.astype(BF16)
        dh_blocks = [dh_ref[:, j * W_IN_BLK:(j + 1) * W_IN_BLK] for j in range(N_DEV)]
        for j in range(N_DEV):
            acc_win[j] += _nn(xt, dh_blocks[j])
        gx_ref[...] += sum(_nt(dh_blocks[j], win_ref[j]) for j in range(N_DEV))

        @pl.when(i == nt - 1)
        def _():
            cps = [pltpu.make_async_copy(acc_win, dwin_ref, sem.at[0]),
                   pltpu.make_async_copy(acc_wout, dwout_ref, sem.at[1])]
            for cp in cps:
                cp.start()
            small_ref[...] = jnp.zeros(small_ref.shape, F32)

            def put_row_vector(row0, acc):
                vec = jnp.sum(acc[...], axis=0, keepdims=True)
                for k in range(vec.shape[1] // 128):
                    small_ref[row0 + k:row0 + k + 1, :] = vec[:, k * 128:(k + 1) * 128]

            put_row_vector(S_BIN, acc_bin)
            put_row_vector(S_CBB, acc_cbb)
            put_row_vector(S_LNBG, acc_lnbg)
            put_row_vector(S_LNBB, acc_lnbb)
            put_row_vector(S_BOUT, acc_bout)
            put_row_vector(S_LN1G, acc_ln1g)
            put_row_vector(S_LN1B, acc_ln1b)
            mask = _tril_mask()
            for hd in range(HEADS):
                rows8 = slice(8 * hd, 8 * hd + 8)
                small_ref[S_LNAG + hd:S_LNAG + hd + 1, :] = jnp.sum(acc_lnag[rows8, :], axis=0, keepdims=True)
                small_ref[S_LNAB + hd:S_LNAB + hd + 1, :] = jnp.sum(acc_lnab[rows8, :], axis=0, keepdims=True)
                small_ref[S_WS + hd * CHUNK:S_WS + (hd + 1) * CHUNK, :] = jnp.where(mask, acc_ws[hd], 0.0)
                small_ref[S_BS + hd:S_BS + hd + 1, :] = jnp.sum(acc_bs[hd].T, axis=0, keepdims=True)
            for k in range(KB):
                dcw_ref[k:k + 1, :] = jnp.sum(acc_cw[8 * k:8 * k + 8, :], axis=0, keepdims=True)
            for cp in cps:
                cp.wait()
            for cp in _chip_copies(p_refs, land_refs, send_sems, recv_sems):
                cp.wait()

    rev = lambda i: nt - 1 - i
    row = lambda w: pl.BlockSpec((tm, w), lambda i: (rev(i), 0))
    return pl.pallas_call(
        body, name="mix_backward", grid=(nt,),
        in_specs=[row(D), row(D_IN),
                  pl.BlockSpec((HALO_B, 2 * D_B), lambda i: (jnp.maximum(rev(i) * halo_blocks - 1, 0), 1)),
                  row(D_B), pl.BlockSpec((N_F, tm, D), lambda i: (0, rev(i), 0)),
                  row(D), row(D), row(128), _resident(win_g.shape), _full(ln_a_g.shape),
                  _full(ln_a_b.shape), _full(w_spatial.shape), _full(bst.shape), _full(conv_b_w.shape),
                  _full(ln_b_g.shape), _full(ln_b_b.shape),
                  _resident(wout.shape), _full(ln1_g.shape)] + [ANY] * n_p,
        out_specs=[row(D), ANY, ANY, _full((KB, D_B)), _full((S_MIX_ROWS, 128))] + [ANY] * n_p,
        out_shape=[jax.ShapeDtypeStruct((t, D), F32), jax.ShapeDtypeStruct((N_DEV, D, W_IN_BLK), F32),
                   jax.ShapeDtypeStruct((D, D), F32), jax.ShapeDtypeStruct((KB, D_B), F32),
                   jax.ShapeDtypeStruct((S_MIX_ROWS, 128), F32)]
        + [jax.ShapeDtypeStruct(p.shape, BF16) for p in ffn_partials],
        scratch_shapes=[pltpu.VMEM((tm + HALO_B, D_B), F32), pltpu.VMEM((tm + HALO_B, D_B), F32),
                        pltpu.VMEM((tm, D), BF16), pltpu.VMEM((tm, D), F32), pltpu.VMEM((tm, D_IN), BF16),
                        pltpu.VMEM((tm, D), BF16),
                        pltpu.VMEM((HEADS, CHUNK, CHUNK), BF16),
                        pltpu.VMEM((N_DEV, D, W_IN_BLK), F32), pltpu.VMEM((D, D), F32),
                        pltpu.VMEM((8, D_IN), F32), pltpu.VMEM((8 * HEADS, HEAD_DIM), F32),
                        pltpu.VMEM((8 * HEADS, HEAD_DIM), F32), pltpu.VMEM((HEADS, CHUNK, CHUNK), F32),
                        pltpu.VMEM((HEADS, CHUNK, CHUNK), F32), pltpu.VMEM((8, D_B), F32),
                        pltpu.VMEM((8, D_B), F32), pltpu.VMEM((8, D_B), F32), pltpu.VMEM((8, D), F32),
                        pltpu.VMEM((8, D), F32), pltpu.VMEM((8, D), F32), pltpu.VMEM((8 * KB, D_B), F32),
                        pltpu.SemaphoreType.DMA((2,)),
                        pltpu.SemaphoreType.DMA((n_p, 3)), pltpu.SemaphoreType.DMA((n_p, 3))],
        compiler_params=_params(("arbitrary",)),
    )(x, h, h, yb1, dx1p, dr2, xhat1, rstd1, win_g, ln_a_g, ln_a_b, w_spatial, bst, conv_b_w,
      ln_b_g, ln_b_b, wout, ln1_g, *ffn_partials)


def _rows128(a):
    return a.reshape(-1, 128)


def _pack_conv(cb, cf):
    lead = cb.shape[:-2]
    pad = [(0, 0)] * len(lead)
    flat = jnp.pad(cb.reshape(lead + (KB * 64,)), pad + [(0, 3 * W_UP_BLK - KB * 64)])
    rows = jnp.concatenate([cf, flat.reshape(lead + (3, W_UP_BLK))], axis=-2)
    return jnp.pad(rows, pad + [(0, 2), (0, 768 - W_UP_BLK)])


def _unpack_conv(p):
    lead = p.shape[:-2]
    cf = p[..., 0:KF, 0:W_UP_BLK]
    cb = p[..., 3:6, 0:W_UP_BLK].reshape(lead + (3 * W_UP_BLK,))[..., :KB * 64].reshape(lead + (KB, 64))
    return cb, cf


def kernel(x, w_in, b_in, ln_a_g, ln_a_b, w_spatial, b_spatial, conv_b_w, conv_b_b, ln_b_g, ln_b_b, w_out, b_out, ln1_g, ln1_b, w_up, conv_f_w, conv_f_b, w_down, ln2_g, ln2_b, loss_target, m_w_in, m_b_in, m_ln_a_g, m_ln_a_b, m_w_spatial, m_b_spatial, m_conv_b_w, m_conv_b_b, m_ln_b_g, m_ln_b_b, m_w_out, m_b_out, m_ln1_g, m_ln1_b, m_w_up, m_conv_f_w, m_conv_f_b, m_w_down, m_ln2_g, m_ln2_b, v_w_in, v_b_in, v_ln_a_g, v_ln_a_b, v_w_spatial, v_b_spatial, v_conv_b_w, v_conv_b_b, v_ln_b_g, v_ln_b_b, v_w_out, v_b_out, v_ln1_g, v_ln1_b, v_w_up, v_conv_f_w, v_conv_f_b, v_w_down, v_ln2_g, v_ln2_b):
    t = x.shape[1]
    x2 = x.reshape(t, D)
    target = loss_target.reshape(t, D)
    tm_fwd = min(t, 512)
    tm_bwd = min(t, 256)
    tm_ffn_bwd = min(t, 512)

    xi, yi, ci = _mesh_pos()
    jidx = jnp.stack([_lid(px, py, ci) for px, py in _chip_patterns(xi, yi)]).astype(jnp.int32)

    sin, sout, sup, sdown, conv_g = prepare_weights(w_in, w_out, w_up.T, w_down, _pack_conv(conv_b_w, conv_f_w))
    conv_b_all, cfw = _unpack_conv(conv_g)
    conv_b_full = conv_b_all.transpose(1, 0, 2).reshape(KB, D_B)
    cfb = conv_f_b.reshape(N_DEV, W_UP_BLK)
    row = lambda a: a.reshape(1, -1)
    bst = b_spatial.T

    h, xhat1, rstd1, yb1, win_g, wout_g, wup_g, wdown_g = mix_forward(
        x2, sin, sout, row(b_in), ln_a_g, ln_a_b, w_spatial, bst, conv_b_full, row(conv_b_b),
        row(ln_b_g), row(ln_b_b), row(b_out), row(ln1_g), row(ln1_b), sup, sdown, tm_fwd)
    wout_full = wout_g.reshape(D, D)
    wdown4 = wdown_g.reshape(N_F, W_UP_BLK, D)
    hu, gv, dr2, loss_part, s_ln2 = ffn_forward(
        xhat1, row(ln1_g), row(ln1_b), wup_g, cfw, cfb, wdown4, row(ln2_g), row(ln2_b), target, tm_bwd)

    order = jnp.where(ci == 0, jnp.array([1, 3, 0, 2], jnp.int32), jnp.array([0, 2, 1, 3], jnp.int32))
    dwup, dwdown, dcfw, dcfb, dx1p, *ffn_lands = ffn_backward(
        order, dr2, xhat1, row(ln1_g), row(ln1_b), hu, gv, wup_g, cfw, wdown4, tm_ffn_bwd)
    ffn_grads = [dwup.reshape(N_DEV, W_UP_BLK, D), dwdown.reshape(N_DEV, D_FF // N_DEV, D)]
    ffn_partials = [chip_partials("chip_partials_" + nm, g, l, jidx, rb)
                    for nm, g, l, rb in zip(["w_up", "w_down"], ffn_grads, ffn_lands, [352, 352])]
    grad_x, dwin, dwout, dcw, s_mix, *ffn_recvs = mix_backward(
        x2, h, yb1, dx1p, dr2, xhat1, rstd1, win_g, ln_a_g, ln_a_b, w_spatial, bst,
        conv_b_full, row(ln_b_g), row(ln_b_b), wout_full, row(ln1_g), ffn_partials, tm_bwd)

    dcfb_rows = jnp.pad(dcfb.reshape(-1, 128), ((0, 4), (0, 0)))
    svec = jnp.concatenate([s_mix, dcfb_rows, s_ln2, loss_part], axis=0)
    dconv = _pack_conv(dcw.reshape(KB, N_DEV, 64).transpose(1, 0, 2), dcfw.reshape(N_DEV, KF, W_UP_BLK))
    mix_grads = [dwin, dwout.reshape(N_DEV, D // N_DEV, D), dconv]
    mix_w = [w_in, w_out, _pack_conv(conv_b_w, conv_f_w)]
    mix_m = [m_w_in, m_w_out, _pack_conv(m_conv_b_w, m_conv_f_w)]
    mix_v = [v_w_in, v_w_out, _pack_conv(v_conv_b_w, v_conv_f_w)]
    *mix_sums, sv_slots = mixer_reduce(mix_grads, svec)
    big = {}

    ffn_w = [(w_up.T, m_w_up.T, v_w_up.T), (w_down, m_w_down, v_w_down)]
    for nm, g, l, r, (w, m, v) in zip(["w_up", "w_down"], ffn_grads, ffn_lands, ffn_recvs, ffn_w):
        big[nm] = reduce_and_adamw("reduce_adamw_" + nm, g, l, r, w, m, v, jidx, 352)
    big["w_up"] = [o.T for o in big["w_up"]]

    small_w = dict(b_in=b_in, ln_a_g=ln_a_g, ln_a_b=ln_a_b, w_spatial=w_spatial, b_spatial=b_spatial,
                   conv_b_b=conv_b_b, ln_b_g=ln_b_g, ln_b_b=ln_b_b, b_out=b_out, ln1_g=ln1_g,
                   ln1_b=ln1_b, conv_f_b=conv_f_b, ln2_g=ln2_g, ln2_b=ln2_b)
    small_m = dict(b_in=m_b_in, ln_a_g=m_ln_a_g, ln_a_b=m_ln_a_b, w_spatial=m_w_spatial,
                   b_spatial=m_b_spatial, conv_b_b=m_conv_b_b, ln_b_g=m_ln_b_g, ln_b_b=m_ln_b_b,
                   b_out=m_b_out, ln1_g=m_ln1_g, ln1_b=m_ln1_b, conv_f_b=m_conv_f_b, ln2_g=m_ln2_g,
                   ln2_b=m_ln2_b)
    small_v = dict(b_in=v_b_in, ln_a_g=v_ln_a_g, ln_a_b=v_ln_a_b, w_spatial=v_w_spatial,
                   b_spatial=v_b_spatial, conv_b_b=v_conv_b_b, ln_b_g=v_ln_b_g, ln_b_b=v_ln_b_b,
                   b_out=v_b_out, ln1_g=v_ln1_g, ln1_b=v_ln1_b, conv_f_b=v_conv_f_b, ln2_g=v_ln2_g,
                   ln2_b=v_ln2_b)
    order = [nm for nm, _, _ in SMALL_LAYOUT]
    small_out = small_adamw(sv_slots, [_rows128(small_w[nm]) for nm in order],
                            [_rows128(small_m[nm]) for nm in order], [_rows128(small_v[nm]) for nm in order],
                            mix_sums, mix_w, mix_m, mix_v)
    n_small = len(order)
    mix_out = small_out[4 * n_small + 1:]
    big.update({nm: [mix_out[k * 3 + p] for k in range(4)] for p, nm in enumerate(["w_in", "w_out", "conv"])})
    for k in range(4):
        cb_k, cf_k = _unpack_conv(big["conv"][k])
        big.setdefault("conv_b_w", []).append(cb_k)
        big.setdefault("conv_f_w", []).append(cf_k)
    small = {nm: [small_out[k * n_small + p].reshape(small_w[nm].shape) for k in range(4)]
             for p, nm in enumerate(order)}
    loss = jnp.sum(small_out[4 * n_small]) * (0.5 / D)

    weights = ["w_in", "b_in", "ln_a_g", "ln_a_b", "w_spatial", "b_spatial", "conv_b_w", "conv_b_b",
               "ln_b_g", "ln_b_b", "w_out", "b_out", "ln1_g", "ln1_b", "w_up", "conv_f_w", "conv_f_b",
               "w_down", "ln2_g", "ln2_b"]
    result = lambda nm, k: big[nm][k] if nm in big else small[nm][k]
    return (loss, grad_x.reshape(x.shape), *[result(nm, 0) for nm in weights],
            *[result(nm, 1) for nm in weights], *[result(nm, 2) for nm in weights],
            *[result(nm, 3) for nm in weights])
```

```python
import functools
import math

import jax
import jax.numpy as jnp
from jax import lax
from jax.experimental import pallas as pl
from jax.experimental.pallas import tpu as pltpu

F32 = jnp.float32
BF16 = jnp.bfloat16

D = 1024
D_A = 512
D_B = 512
HEADS = 4
HEAD_DIM = 128
CHUNK = 128
KB = 31
KF = 3
D_FF = 2816
D_IN = 2048
N_DEV = 8
W_IN_BLK = D_IN // N_DEV
W_UP_BLK = 2 * D_FF // N_DEV
N_F = 4
LN_EPS = 1e-5
ALPHA = 2.0 ** 0.25

ADAM_LR = 0.001
ADAM_B1 = 0.9
ADAM_B2 = 0.999
ADAM_EPS = 1e-08
ADAM_WD = 0.01
ADAM_STEP = 10

INV_SQRT2 = 1.0 / math.sqrt(2.0)
INV_SQRT_2PI = 1.0 / math.sqrt(2.0 * math.pi)

HALO_B = 32
HALO_F = 8
ROWS = 64
LN_ROWS = 32
VMEM_LIMIT = 58 * 1024 * 1024

MESH = pl.DeviceIdType.MESH
ANY = pl.BlockSpec(memory_space=pl.ANY)
VMEM = pl.BlockSpec(memory_space=pltpu.VMEM)

S_BIN, S_LNAG, S_LNAB, S_WS, S_BS, S_CBB, S_LNBG, S_LNBB, S_BOUT, S_LN1G, S_LN1B = (
    0, 16, 24, 32, 544, 552, 560, 568, 576, 584, 592)
S_MIX_ROWS = 600
S_CFB = 600
S_LN2G = 648
S_LN2B = 656
S_LOSS = 664
S_ROWS = 672


def _tn(a, b):
    return lax.dot_general(a, b, (((0,), (0,)), ((), ())), preferred_element_type=F32)


def _nt(a, b):
    return lax.dot_general(a, b, (((1,), (1,)), ((), ())), preferred_element_type=F32)


def _nn(a, b):
    return jnp.dot(a, b, preferred_element_type=F32)


def _sigmoid(x):
    return 1.0 / (1.0 + jnp.exp(-x))


def _ln_stats(x):
    mu = jnp.mean(x, axis=-1, keepdims=True)
    xc = x - mu
    var = jnp.mean(xc * xc, axis=-1, keepdims=True)
    rstd = lax.rsqrt(var + LN_EPS)
    return xc * rstd, rstd


def _ln_bwd(dxhat, xhat, rstd):
    m1 = jnp.mean(dxhat, axis=-1, keepdims=True)
    m2 = jnp.mean(dxhat * xhat, axis=-1, keepdims=True)
    return rstd * (dxhat - m1 - xhat * m2)


def _rsum8(x):
    r, n = x.shape
    return x.reshape(r // 8, 8, n).sum(axis=0)


def _rows(i, n=ROWS):
    return pl.ds(i * n, n)


def _loop(n, body):
    for i in range(n):
        body(i)


def _tril_mask():
    r = lax.broadcasted_iota(jnp.int32, (CHUNK, CHUNK), 0)
    c = lax.broadcasted_iota(jnp.int32, (CHUNK, CHUNK), 1)
    return c <= r


def _mixer_a_head(h_ref, r, hd, ga_ref, ba_ref, wsm_ref, bst_ref):
    sl = slice(hd * HEAD_DIM, (hd + 1) * HEAD_DIM)
    hu = h_ref[r, sl]
    hv = h_ref[r, D_A + hd * HEAD_DIM:D_A + (hd + 1) * HEAD_DIM]
    cdf_u = 0.5 * (1.0 + lax.erf(hu * INV_SQRT2))
    cdf_v = 0.5 * (1.0 + lax.erf(hv * INV_SQRT2))
    u = hu * cdf_u
    xhat, rstd = _ln_stats(hv * cdf_v)
    vn = (xhat * ga_ref[hd:hd + 1, :] + ba_ref[hd:hd + 1, :]).astype(BF16)
    sv = _nn(wsm_ref[hd], vn) + bst_ref[:, hd:hd + 1]
    return hu, hv, u, cdf_u, cdf_v, xhat, rstd, vn, sv


def _taps(win, offsets):
    n = win.shape[0]
    for s in range(8):
        ks = [k for k, o in enumerate(offsets) if o % 8 == s]
        if ks:
            moved = win if s == 0 else pltpu.roll(win, n - s, 0)
            for k in ks:
                yield k, moved[offsets[k] - s:offsets[k] - s + ROWS, :]


CONV_B_OFFSETS = [2 + k for k in range(KB)]
CONV_B_T_OFFSETS = [30 - k for k in range(KB)]


def _conv_b_block(ext_ref, base, cw_ref):
    acc = jnp.zeros((ROWS, D_B), F32)
    for k, tap in _taps(ext_ref[pl.ds(base, ROWS + HALO_B), :], CONV_B_OFFSETS):
        acc = acc + tap * cw_ref[k:k + 1, :]
    return acc


def _taps_f(win):
    n = ROWS + HALO_F
    return [pltpu.roll(win, n - 6, 0)[0:ROWS, :], pltpu.roll(win, n - 7, 0)[0:ROWS, :], win[8:n, :]]


def _params(sem, **kw):
    return pltpu.CompilerParams(dimension_semantics=sem, vmem_limit_bytes=VMEM_LIMIT, **kw)


def _resident(shape):
    zeros = (0,) * len(shape)
    return pl.BlockSpec(shape, lambda *_: zeros, pipeline_mode=pl.Buffered(1))


def _full(shape):
    zeros = (0,) * len(shape)
    return pl.BlockSpec(shape, lambda *_: zeros)


def _mesh_pos():
    return lax.axis_index("x"), lax.axis_index("y"), lax.axis_index("c")


def _chip_patterns(x, y):
    return [(x, y), (1 - x, y), (x, 1 - y), (1 - x, 1 - y)]


def _lid(x, y, c):
    return 4 * x + 2 * y + c


def _gather_copy(outs, send_sems, recv_sems, a, k, block, to, src=None):
    blk = outs[a].at[_lid(*block)]
    return pltpu.make_async_remote_copy(
        src_ref=blk if src is None else src, dst_ref=blk,
        send_sem=send_sems.at[a, k], recv_sem=recv_sems.at[a, k], device_id=to, device_id_type=MESH)


def _gather_start(mine, outs, send_sems, recv_sems, local_sems, diagonal=False, row_only=()):
    x, y, c = _mesh_pos()
    me = (x, y, c)
    for a in range(len(mine)):
        pltpu.make_async_copy(mine[a], outs[a].at[_lid(*me)], local_sems.at[a]).start()
        targets = [(x, y, 1 - c), (1 - x, y, c), (x, 1 - y, c), (1 - x, 1 - y, c)]
        for k, to in enumerate(targets[:2] if a in row_only else targets if diagonal else targets[:3]):
            _gather_copy(outs, send_sems, recv_sems, a, k, me, to, src=mine[a]).start()


def _gather_relay(mine, outs, send_sems, recv_sems, local_sems, via, row_only=()):
    x, y, c = _mesh_pos()
    me, sib = (x, y, c), (x, y, 1 - c)
    copy = functools.partial(_gather_copy, outs, send_sems, recv_sems)
    source = {1: (1 - x, y, c), 2: (x, 1 - y, c)}
    for a in range(len(mine)):
        for k in ((1,) if a in row_only else (via[a], 3 - via[a]) if via[a] else (1, 2)):
            copy(a, k, source[k], me).wait_recv()
            if k == via[a] and a not in row_only:
                copy(a, 3, source[k], source[3 - k]).start()
            copy(a, 3 + k, source[k], sib).start()


def _gather_finish(mine, outs, send_sems, recv_sems, local_sems, row_only=()):
    x, y, c = _mesh_pos()
    me, sib = (x, y, c), (x, y, 1 - c)
    copy = functools.partial(_gather_copy, outs, send_sems, recv_sems)
    diag = (1 - x, 1 - y)
    n = len(mine)
    for a in range(n):
        if a not in row_only:
            copy(a, 3, (*diag, c), me).wait_recv()
            copy(a, 6, (*diag, c), sib).start()
    for a in range(n):
        copy(a, 0, sib, me).wait_recv()
        for k, chip in zip((4, 5, 6), [(1 - x, y), (x, 1 - y), diag]):
            if k == 4 or a not in row_only:
                copy(a, k, (*chip, 1 - c), me).wait_recv()
        for k in ((0, 1, 4) if a in row_only else range(7)):
            copy(a, k, me, sib, src=mine[a]).wait_send()
        pltpu.make_async_copy(mine[a], outs[a].at[_lid(*me)], local_sems.at[a]).wait()


def _gather_scratch(n):
    return [pltpu.SemaphoreType.DMA((n, 7)), pltpu.SemaphoreType.DMA((n, 7)), pltpu.SemaphoreType.DMA((n,))]


def prepare_weights(w_in, w_out, w_up_t, w_down, convp):
    def body(win_ref, wout_ref, wup_ref, wdown_ref, convp_ref,
             sin_ref, sout_ref, sup_ref, sdown_ref, gconv_ref, send_sems, recv_sems, local_sems):
        gather = ([convp_ref], [gconv_ref], send_sems, recv_sems, local_sems)
        _gather_start(*gather, diagonal=True)
        sin_ref[...] = win_ref[...].astype(BF16)
        sout_ref[...] = wout_ref[...].astype(BF16)
        sup_ref[...] = wup_ref[...].T.astype(BF16)
        sdown_ref[...] = wdown_ref[...].astype(BF16)
        _gather_relay(*gather, via=[0])
        _gather_finish(*gather)

    return pl.pallas_call(
        body, name="prepare_weights",
        out_shape=[jax.ShapeDtypeStruct(w_in.shape, BF16), jax.ShapeDtypeStruct(w_out.shape, BF16),
                   jax.ShapeDtypeStruct(w_up_t.shape[::-1], BF16), jax.ShapeDtypeStruct(w_down.shape, BF16),
                   jax.ShapeDtypeStruct((N_DEV,) + convp.shape, F32)],
        in_specs=[VMEM] * 5, out_specs=[VMEM] * 4 + [ANY],
        scratch_shapes=_gather_scratch(1),
        compiler_params=pltpu.CompilerParams(vmem_limit_bytes=VMEM_LIMIT),
    )(w_in, w_out, w_up_t, w_down, convp)


def _chip_copies(p, land, send_sems, recv_sems):
    x, y, c = _mesh_pos()
    return [pltpu.make_async_remote_copy(
        src_ref=p[a].at[k], dst_ref=land[a].at[k], send_sem=send_sems.at[a, k], recv_sem=recv_sems.at[a, k],
        device_id=(px, py, c), device_id_type=MESH)
        for k, (px, py) in enumerate(_chip_patterns(x, y)[1:]) for a in range(len(p))]


def chip_partials(name, g, land, jidx, rb):
    _, r, c = g.shape

    def body(j_ref, g_ref, l_ref, o_ref):
        o_ref[...] = (g_ref[...] + l_ref[...]).astype(BF16)

    return pl.pallas_call(
        body, name=name,
        out_shape=jax.ShapeDtypeStruct((3, r, c), BF16),
        grid_spec=pltpu.PrefetchScalarGridSpec(
            num_scalar_prefetch=1, grid=(3, r // rb),
            in_specs=[pl.BlockSpec((1, rb, c), lambda k, i, j: (j[1 + k], i, 0)),
                      pl.BlockSpec((1, rb, c), lambda k, i, j: (1 + k, i, 0))],
            out_specs=pl.BlockSpec((1, rb, c), lambda k, i, j: (k, i, 0))),
        compiler_params=_params(("arbitrary", "arbitrary")),
    )(jidx, g, land)


def _adamw(w, g, m, v):
    m2 = ADAM_B1 * m + (1.0 - ADAM_B1) * g
    v2 = ADAM_B2 * v + (1.0 - ADAM_B2) * (g * g)
    m_hat = m2 / (1.0 - ADAM_B1 ** ADAM_STEP)
    v_hat = v2 / (1.0 - ADAM_B2 ** ADAM_STEP)
    delta = -ADAM_LR * (m_hat / (jnp.sqrt(v_hat) + ADAM_EPS) + ADAM_WD * w)
    return delta, m2, v2


def reduce_and_adamw(name, g, land, recv, w, m, v, jidx, rb):
    _, r, c = g.shape

    def body(j_ref, g_ref, l_ref, r_ref, w_ref, m_ref, v_ref, go_ref, do_ref, mo_ref, vo_ref):
        grad = (g_ref[0] + l_ref[0]) + r_ref[0].astype(F32) + r_ref[1].astype(F32) + r_ref[2].astype(F32)
        delta, m2, v2 = _adamw(w_ref[...], grad, m_ref[...], v_ref[...])
        go_ref[...] = grad
        do_ref[...] = delta
        mo_ref[...] = m2
        vo_ref[...] = v2

    blk = pl.BlockSpec((rb, c), lambda i, j: (i, 0))
    return pl.pallas_call(
        body, name=name,
        out_shape=[jax.ShapeDtypeStruct((r, c), F32)] * 4,
        grid_spec=pltpu.PrefetchScalarGridSpec(
            num_scalar_prefetch=1, grid=(r // rb,),
            in_specs=[pl.BlockSpec((1, rb, c), lambda i, j: (j[0], i, 0)),
                      pl.BlockSpec((1, rb, c), lambda i, j: (0, i, 0)),
                      pl.BlockSpec((3, rb, c), lambda i, j: (0, i, 0)),
                      blk, blk, blk],
            out_specs=[blk] * 4),
        compiler_params=_params(("arbitrary",)),
    )(jidx, g, land, recv, w, m, v)


def mixer_reduce(grads, svec):
    n = len(grads)
    shard = [g.shape[1:] for g in grads]

    def body(*refs):
        g = refs[:n]
        sv_ref = refs[n]
        outs = refs[n + 1:2 * n + 1]
        sv_slots = refs[2 * n + 1]
        rest = refs[2 * n + 2:]
        own, land, sendb, recvb = rest[:n], rest[n:2 * n], rest[2 * n:3 * n], rest[3 * n:4 * n]
        sv_land, chip_sv, d2d_send, d2d_recv, ici_send, ici_recv, local_sems, sv_sems = rest[4 * n:]
        x, y, c = _mesh_pos()
        sib = (x, y, 1 - c)
        pats = _chip_patterns(x, y)
        q = 2 * x + y

        d2d, local = {}, {}
        for a in range(n):
            for k, (px, py) in enumerate(pats):
                d2d[a, k] = pltpu.make_async_remote_copy(
                    src_ref=g[a].at[_lid(px, py, 1 - c)], dst_ref=land[a].at[k],
                    send_sem=d2d_send.at[a, k], recv_sem=d2d_recv.at[a, k], device_id=sib, device_id_type=MESH)
                local[a, k] = pltpu.make_async_copy(g[a].at[_lid(px, py, c)], own[a].at[k], local_sems.at[a, k])
        sv_d2d = pltpu.make_async_remote_copy(
            src_ref=sv_ref, dst_ref=sv_land, send_sem=d2d_send.at[n, 0], recv_sem=d2d_recv.at[n, 0],
            device_id=sib, device_id_type=MESH)
        blocks = [(a, k) for a in range(n) for k in (1, 2, 3)] + [(a, 0) for a in range(n)]
        sv_d2d.start()
        for b in blocks:
            d2d[b].start()
            local[b].start()

        half_rows = svec.shape[0] // 2
        rows = pl.ds(pl.multiple_of(c * half_rows, 8), half_rows)
        sv_local = pltpu.make_async_copy(chip_sv, sv_slots.at[q], sv_sems.at[0])

        def sv_ici(k, slot, to):
            return pltpu.make_async_remote_copy(
                src_ref=chip_sv.at[rows], dst_ref=sv_slots.at[slot, rows], send_sem=sv_sems.at[1 + k],
                recv_sem=sv_sems.at[4 + k], device_id=to, device_id_type=MESH)

        def sv_pass_on(k, slot):
            return pltpu.make_async_remote_copy(
                src_ref=sv_slots.at[slot, rows], dst_ref=sv_slots.at[slot, rows], send_sem=sv_sems.at[7 + k],
                recv_sem=sv_sems.at[10 + k], device_id=sib, device_id_type=MESH)

        sv_d2d.wait()
        chip_sv[...] = sv_ref[...] + sv_land[...]
        sv_out = [sv_ici(k, q, (px, py, c)) for k, (px, py) in enumerate(pats[1:])]
        for cp in sv_out + [sv_local]:
            cp.start()

        ici = _chip_copies(sendb, recvb, ici_send, ici_recv)
        for a, k in blocks:
            local[a, k].wait()
            d2d[a, k].wait()
            if k > 0:
                sendb[a][k - 1] = (own[a][k] + land[a][k]).astype(BF16)
                ici[(k - 1) * n + a].start()
        for k, (px, py) in enumerate(pats[1:]):
            sv_out[k].wait_send()
            sv_ici(k, 2 * px + py, (px, py, c)).wait_recv()
            sv_pass_on(k, 2 * px + py).start()
        for a in range(n):
            for k in range(3):
                ici[k * n + a].wait()
            outs[a][...] = ((own[a][0] + land[a][0]) + recvb[a][0].astype(F32) + recvb[a][1].astype(F32)
                            + recvb[a][2].astype(F32))
        for k, (px, py) in enumerate(pats[1:]):
            sv_pass_on(k, 2 * px + py).wait()
        sv_local.wait()

    shard_out = [jax.ShapeDtypeStruct(s, F32) for s in shard]
    return pl.pallas_call(
        body, name="mixer_reduce",
        out_shape=shard_out + [jax.ShapeDtypeStruct((4,) + svec.shape, F32)],
        in_specs=[ANY] * n + [VMEM], out_specs=[VMEM] * n + [ANY],
        scratch_shapes=[pltpu.VMEM((4,) + s, F32) for s in shard] + [pltpu.VMEM((4,) + s, F32) for s in shard]
        + [pltpu.VMEM((3,) + s, BF16) for s in shard] + [pltpu.VMEM((3,) + s, BF16) for s in shard]
        + [pltpu.VMEM(svec.shape, F32), pltpu.VMEM(svec.shape, F32),
           pltpu.SemaphoreType.DMA((n + 1, 4)), pltpu.SemaphoreType.DMA((n + 1, 4)),
           pltpu.SemaphoreType.DMA((n, 3)), pltpu.SemaphoreType.DMA((n, 3)),
           pltpu.SemaphoreType.DMA((n, 4)), pltpu.SemaphoreType.DMA((13,))],
        compiler_params=pltpu.CompilerParams(vmem_limit_bytes=VMEM_LIMIT),
    )(*grads, svec)


SMALL_LAYOUT = [
    ("b_in", S_BIN, 16), ("ln_a_g", S_LNAG, 4), ("ln_a_b", S_LNAB, 4), ("w_spatial", S_WS, 512),
    ("b_spatial", S_BS, 4), ("conv_b_b", S_CBB, 4), ("ln_b_g", S_LNBG, 4), ("ln_b_b", S_LNBB, 4),
    ("b_out", S_BOUT, 8), ("ln1_g", S_LN1G, 8), ("ln1_b", S_LN1B, 8), ("conv_f_b", S_CFB, 44),
    ("ln2_g", S_LN2G, 8), ("ln2_b", S_LN2B, 8),
]


def small_adamw(sv_slots, ws, ms, vs, shard_grads, shard_ws, shard_ms, shard_vs):
    n = len(SMALL_LAYOUT)
    nb = len(shard_grads)

    def body(*refs):
        s_ref = refs[0]
        w_refs, m_refs, v_refs = refs[1:1 + n], refs[1 + n:1 + 2 * n], refs[1 + 2 * n:1 + 3 * n]
        big_in = refs[1 + 3 * n:1 + 3 * n + 4 * nb]
        outs = refs[1 + 3 * n + 4 * nb:]
        big_out = outs[4 * n + 1:]
        for p in range(nb):
            grad = big_in[p][...]
            delta, m2, v2 = _adamw(big_in[nb + p][...], grad, big_in[2 * nb + p][...], big_in[3 * nb + p][...])
            big_out[p][...] = grad
            big_out[nb + p][...] = delta
            big_out[2 * nb + p][...] = m2
            big_out[3 * nb + p][...] = v2
        for p, (_, row0, rows) in enumerate(SMALL_LAYOUT):
            sl = pl.ds(row0, rows)
            grad = ((s_ref[0, sl, :] + s_ref[1, sl, :]) + s_ref[2, sl, :]) + s_ref[3, sl, :]
            delta, m2, v2 = _adamw(w_refs[p][...], grad, m_refs[p][...], v_refs[p][...])
            outs[p][...] = grad
            outs[n + p][...] = delta
            outs[2 * n + p][...] = m2
            outs[3 * n + p][...] = v2
        sl = pl.ds(S_LOSS, 8)
        outs[4 * n][...] = ((s_ref[0, sl, :] + s_ref[1, sl, :]) + s_ref[2, sl, :]) + s_ref[3, sl, :]

    shapes = [jax.ShapeDtypeStruct((rows, 128), F32) for _, _, rows in SMALL_LAYOUT]
    big_shapes = [jax.ShapeDtypeStruct(g.shape, F32) for g in shard_grads]
    return pl.pallas_call(
        body, name="small_adamw",
        out_shape=shapes * 4 + [jax.ShapeDtypeStruct((8, 128), F32)] + big_shapes * 4,
        in_specs=[VMEM] * (1 + 3 * n + 4 * nb), out_specs=[VMEM] * (4 * n + 1 + 4 * nb),
        compiler_params=pltpu.CompilerParams(vmem_limit_bytes=VMEM_LIMIT),
    )(sv_slots, *ws, *ms, *vs, *shard_grads, *shard_ws, *shard_ms, *shard_vs)


def mix_forward(x, sin, sout, b_in, ln_a_g, ln_a_b, w_spatial, bst, conv_b_w, conv_b_b, ln_b_g, ln_b_b,
                b_out, ln1_g, ln1_b, sup, sdown, tm):
    t = x.shape[0]
    nt = t // tm
    n_chunks = tm // CHUNK

    def body(x_ref, sin_ref, sout_ref, bin_ref, ga_ref, ba_ref, ws_ref, bst_ref, cw_ref, cb_ref, gb_ref,
             bb_ref, bout_ref, g1_ref, b1_ref, sup_ref, sdown_ref,
             h_ref, xhat1_ref, rstd1_ref, yb1_ref, gin_ref, gout_ref, gup_ref, gdown_ref,
             ext_ref, y_ref, wsm_ref, win_ref, wout_ref, load_sems,
             mix_send, mix_recv, mix_local, send_sems, recv_sems, local_sems):
        i = pl.program_id(0)
        mixer = ([sin_ref, sout_ref], [gin_ref, gout_ref], mix_send, mix_recv, mix_local)
        gather = ([sdown_ref, sup_ref], [gdown_ref, gup_ref], send_sems, recv_sems, local_sems)
        rows = dict(row_only=(1,))

        @pl.when(i == 0)
        def _():
            _gather_start(*mixer)
            _gather_relay(*mixer, via=[2, 2])
            _gather_finish(*mixer)
            _gather_start(*gather, **rows)
            loads = [pltpu.make_async_copy(gin_ref, win_ref, load_sems.at[0]),
                     pltpu.make_async_copy(gout_ref, wout_ref, load_sems.at[1])]
            for cp in loads:
                cp.start()
            for cp in loads:
                cp.wait()
            ext_ref[0:HALO_B, :] = jnp.zeros((HALO_B, D_B), F32)
            mask = _tril_mask()
            for hd in range(HEADS):
                wsm_ref[hd] = jnp.where(mask, ws_ref[hd], 0.0).astype(BF16)

        xb = x_ref[...].astype(BF16)
        for j in range(N_DEV):
            cols = slice(j * W_IN_BLK, (j + 1) * W_IN_BLK)
            h_ref[:, cols] = _nn(xb, win_ref[j]) + bin_ref[:, cols]

        def chunk(ci):
            r = _rows(ci, CHUNK)
            for hd in range(HEADS):
                _, _, u, _, _, _, _, _, sv = _mixer_a_head(h_ref, r, hd, ga_ref, ba_ref, wsm_ref, bst_ref)
                y_ref[r, hd * HEAD_DIM:(hd + 1) * HEAD_DIM] = (u * sv).astype(BF16)
            a_b = h_ref[r, 2 * D_A:2 * D_A + D_B]
            g_b = h_ref[r, 2 * D_A + D_B:D_IN]
            ext_ref[pl.ds(HALO_B + ci * CHUNK, CHUNK), :] = a_b * _sigmoid(g_b)

        _loop(n_chunks, chunk)

        def conv_rows(bi):
            base = bi * ROWS
            yb1 = _conv_b_block(ext_ref, base, cw_ref) + cb_ref[...]
            yb1_ref[pl.ds(base, ROWS), :] = yb1
            xhat, _ = _ln_stats(yb1)
            yb2 = xhat * gb_ref[...] + bb_ref[...]
            y_ref[pl.ds(base, ROWS), D_A:D] = (yb2 * _sigmoid(yb2)).astype(BF16)

        _loop(tm // ROWS, conv_rows)
        ext_ref[0:HALO_B, :] = ext_ref[tm:tm + HALO_B, :]

        mix = _nn(y_ref[...], wout_ref[...].reshape(D, D)) + bout_ref[...]
        xhat1, rstd1 = _ln_stats(ALPHA * x_ref[...] + mix)
        xhat1_ref[...] = xhat1
        rstd1_ref[...] = jnp.broadcast_to(rstd1, (tm, 128))

        @pl.when(i == nt // 2)
        def _():
            _gather_relay(*gather, via=[2, 0], **rows)

        @pl.when(i == nt - 1)
        def _():
            _gather_finish(*gather, **rows)

    row = lambda w: pl.BlockSpec((tm, w), lambda i: (i, 0))
    return pl.pallas_call(
        body, name="mix_forward", grid=(nt,),
        in_specs=[row(D), ANY, ANY, _full(b_in.shape), _full(ln_a_g.shape),
                  _full(ln_a_b.shape), _full(w_spatial.shape), _full(bst.shape),
                  _full(conv_b_w.shape), _full(conv_b_b.shape), _full(ln_b_g.shape),
                  _full(ln_b_b.shape), _full(b_out.shape),
                  _full(ln1_g.shape), _full(ln1_b.shape), ANY, ANY],
        out_specs=[row(D_IN), row(D), row(128), row(D_B), ANY, ANY, ANY, ANY],
        out_shape=[jax.ShapeDtypeStruct((t, D_IN), F32), jax.ShapeDtypeStruct((t, D), F32),
                   jax.ShapeDtypeStruct((t, 128), F32), jax.ShapeDtypeStruct((t, D_B), F32)]
        + [jax.ShapeDtypeStruct((N_DEV,) + sh.shape, BF16) for sh in (sin, sout, sup, sdown)],
        scratch_shapes=[pltpu.VMEM((tm + HALO_B, D_B), F32), pltpu.VMEM((tm, D), BF16),
                        pltpu.VMEM((HEADS, CHUNK, CHUNK), BF16),
                        pltpu.VMEM((N_DEV,) + sin.shape, BF16), pltpu.VMEM((N_DEV,) + sout.shape, BF16),
                        pltpu.SemaphoreType.DMA((2,))] + _gather_scratch(2) + _gather_scratch(2),
        compiler_params=_params(("arbitrary",)),
    )(x, sin, sout, b_in, ln_a_g, ln_a_b, w_spatial, bst, conv_b_w, conv_b_b, ln_b_g, ln_b_b,
      b_out, ln1_g, ln1_b, sup, sdown)


def ffn_forward(order, xhat1, ln1_g, ln1_b, wup_g, cfw, cfb, wdown, ln2_g, ln2_b, target, tm):
    t = xhat1.shape[0]
    nt = t // tm
    last = N_F - 1
    cfw4 = cfw.reshape(2, N_F, KF, W_UP_BLK)
    cfb4 = cfb.reshape(2, N_F, 1, W_UP_BLK)

    def body(order_ref, xh_ref, g1_ref, b1_ref, wup_in, cfw_ref, cfb_ref, wdown_ref, g2_ref, b2_ref, tgt_ref,
             hu_ref, gv_ref, dr2_ref, loss_ref, sln2_ref, wup_ref,
             x1b_ref, y_ref, hu32_ref, carry_ref, gbuf_ref, acc_loss, acc_g2, acc_b2, wbuf, wsem,
             send_sems, recv_sems):
        p = pl.program_id(0)
        i = pl.program_id(1)
        f = order_ref[p]
        slot = p % 2
        x, y, c = _mesh_pos()
        me, sib, ynb = (x, y, c), (x, y, 1 - c), (x, 1 - y, c)

        def remote(k, block, to):
            blk = wup_ref.at[_lid(*block)]
            return pltpu.make_async_remote_copy(
                src_ref=blk, dst_ref=blk, send_sem=send_sems.at[k], recv_sem=recv_sems.at[k],
                device_id=to, device_id_type=MESH)

        def wload(pair, s):
            return [pltpu.make_async_copy(wup_ref.at[g * N_F + pair], wbuf.at[s, g], wsem.at[s, g])
                    for g in range(2)]

        @pl.when((p == 0) & (i == 0))
        def _():
            remote(0, me, ynb).start()
            remote(1, (1 - x, y, c), ynb).start()
            for cp in wload(order_ref[0], 0) + wload(order_ref[1], 1):
                cp.start()
            acc_loss[...] = jnp.zeros(acc_loss.shape, F32)
            acc_g2[...] = jnp.zeros(acc_g2.shape, F32)
            acc_b2[...] = jnp.zeros(acc_b2.shape, F32)

        @pl.when(i == 0)
        def _():
            for cp in wload(f, slot):
                cp.wait()
            carry_ref[...] = jnp.zeros(carry_ref.shape, F32)

        @pl.when(p == 0)
        def _():
            x1b_ref[i] = (xh_ref[...] * g1_ref[...] + b1_ref[...]).astype(BF16)
            y_ref[i] = jnp.zeros((tm, D), F32)

        def conv(g, base):
            if base == 0:
                win = jnp.concatenate([carry_ref[g], hu32_ref[g, 0:ROWS, :]], axis=0)
            else:
                win = hu32_ref[g, base - HALO_F:base + ROWS, :]
            taps = _taps_f(win)
            w = cfw_ref[g, 0]
            return sum(taps[k] * w[k:k + 1, :] for k in range(KF)) + cfb_ref[g, 0]

        hu32_ref[0] = _nn(x1b_ref[i], wbuf[slot, 0])
        hu32_ref[1] = _nn(x1b_ref[i], wbuf[slot, 1])

        def rows(bi):
            r = _rows(bi)
            gate = conv(0, bi * ROWS)
            val = conv(1, bi * ROWS)
            gbuf_ref[r, :] = (gate * _sigmoid(gate) * val).astype(BF16)
            gv_ref[0, 0, r, :] = gate.astype(BF16)
            gv_ref[1, 0, r, :] = val.astype(BF16)
            hu_ref[0, 0, r, :] = hu32_ref[0, r, :].astype(BF16)
            hu_ref[1, 0, r, :] = hu32_ref[1, r, :].astype(BF16)

        _loop(tm // ROWS, rows)
        carry_ref[0] = hu32_ref[0, tm - HALO_F:tm, :]
        carry_ref[1] = hu32_ref[1, tm - HALO_F:tm, :]
        y_ref[i] += _nn(gbuf_ref[...], wdown_ref[f])

        @pl.when(p == last)
        def _():
            def tail(bi):
                r = _rows(bi, LN_ROWS)
                x1 = xh_ref[r, :] * g1_ref[...] + b1_ref[...]
                xhat2, rstd2 = _ln_stats(ALPHA * x1 + y_ref[i, r, :])
                err = xhat2 * g2_ref[...] + b2_ref[...] - tgt_ref[r, :]
                e2 = _rsum8(err * err)
                acc_loss[...] += sum(e2[:, k * 128:(k + 1) * 128] for k in range(D // 128))
                dy = err * (1.0 / D)
                acc_g2[...] += _rsum8(dy * xhat2)
                acc_b2[...] += _rsum8(dy)
                dr2_ref[r, :] = _ln_bwd(dy * g2_ref[...], xhat2, rstd2)

            _loop(tm // LN_ROWS, tail)
            loss_ref[...] = acc_loss[...]

        @pl.when((p == 1) & (i == nt - 1))
        def _():
            for k, block in enumerate([(x, 1 - y, c), (1 - x, 1 - y, c)]):
                remote(k, block, me).wait_recv()
                remote(2 + k, block, sib).start()
            for cp in wload(order_ref[2], 0):
                cp.start()

        @pl.when((p == 2) & (i == nt - 1))
        def _():
            for k, block in enumerate([(x, 1 - y, 1 - c), (1 - x, 1 - y, 1 - c)]):
                remote(2 + k, block, me).wait_recv()
            for cp in wload(order_ref[3], 1):
                cp.start()

        @pl.when((p == last) & (i == nt - 1))
        def _():
            for k in range(4):
                remote(k, me, sib).wait_send()
            dg = jnp.sum(acc_g2[...], axis=0, keepdims=True)
            db = jnp.sum(acc_b2[...], axis=0, keepdims=True)
            for k in range(D // 128):
                sln2_ref[k:k + 1, :] = dg[:, k * 128:(k + 1) * 128]
                sln2_ref[8 + k:9 + k, :] = db[:, k * 128:(k + 1) * 128]

    ends = lambda p, i: jnp.where((p == 0) | (p == last), i, 0)
    pair = lambda r, c: pl.BlockSpec((2, 1, r, c), lambda p, i, o: (0, o[p], 0, 0))
    tile = pl.BlockSpec((2, 1, tm, W_UP_BLK), lambda p, i, o: (0, o[p], i, 0))
    saved = jax.ShapeDtypeStruct((2, N_F, t, W_UP_BLK), BF16)
    return pl.pallas_call(
        body, name="ffn_forward",
        grid_spec=pltpu.PrefetchScalarGridSpec(
            num_scalar_prefetch=1, grid=(N_F, nt),
            in_specs=[pl.BlockSpec((tm, D), lambda p, i, o: (ends(p, i), 0)),
                      _full(ln1_g.shape), _full(ln1_b.shape), ANY,
                      pair(KF, W_UP_BLK), pair(1, W_UP_BLK), _resident(wdown.shape),
                      _full(ln2_g.shape), _full(ln2_b.shape),
                      pl.BlockSpec((tm, D), lambda p, i, o: (jnp.where(p == last, i, 0), 0))],
            out_specs=[tile, tile,
                       pl.BlockSpec((tm, D), lambda p, i, o: (jnp.where(p == last, i, 0), 0)),
                       _full((8, 128)), _full((16, 128)), ANY],
            scratch_shapes=[pltpu.VMEM((nt, tm, D), BF16), pltpu.VMEM((nt, tm, D), F32),
                            pltpu.VMEM((2, tm, W_UP_BLK), F32),
                            pltpu.VMEM((2, HALO_F, W_UP_BLK), F32), pltpu.VMEM((tm, W_UP_BLK), BF16),
                            pltpu.VMEM((8, 128), F32), pltpu.VMEM((8, D), F32), pltpu.VMEM((8, D), F32),
                            pltpu.VMEM((2, 2, D, W_UP_BLK), BF16), pltpu.SemaphoreType.DMA((2, 2)),
                            pltpu.SemaphoreType.DMA((4,)), pltpu.SemaphoreType.DMA((4,))]),
        out_shape=[saved, saved, jax.ShapeDtypeStruct((t, D), F32),
                   jax.ShapeDtypeStruct((8, 128), F32), jax.ShapeDtypeStruct((16, 128), F32),
                   jax.ShapeDtypeStruct(wup_g.shape, BF16)],
        input_output_aliases={4: 5},
        compiler_params=_params(("arbitrary", "arbitrary")),
    )(order, xhat1, ln1_g, ln1_b, wup_g, cfw4, cfb4, wdown, ln2_g, ln2_b, target)


def ffn_backward(order, dr2, xhat1, ln1_g, ln1_b, hu, gv, wup_g, cfw, wdown, tm):
    t = dr2.shape[0]
    nt = t // tm
    sub_rows = tm
    hu4 = hu.reshape(2, N_F, t, W_UP_BLK)
    gv4 = gv.reshape(2, N_F, t, W_UP_BLK)
    wup4 = wup_g.reshape(2, N_F, D, W_UP_BLK)
    cfw4 = cfw.reshape(2, N_F, KF, W_UP_BLK)

    def body(order_ref, dr2_ref, xh_ref, g1_ref, b1_ref, hu_ref, gv_ref, wup_ref, cfw_ref, wdown_ref,
             dwup_ref, dwdown_ref, dcfw_ref, dcfb_ref, dx1_ref, land_up_ref, land_down_ref,
             x1b_ref, drb_ref, dg_ref, dextg_ref, dextv_ref, gbuf_ref,
             dhug_ref, dhuv_ref, acc_wup, acc_wdown, acc_cfw, acc_cfb, sem, send_sems, recv_sems):
        fo = pl.program_id(0)
        f = order_ref[fo]
        f_prev = order_ref[jnp.maximum(fo - 1, 0)]
        slot = fo % 2
        i = pl.program_id(1)
        x, y, c = _mesh_pos()
        half = D_FF // N_DEV

        def to_sibling(fi, k, src, land_ref, shard_chip):
            d = jnp.bitwise_xor(shard_chip, 2 * x + y)
            slot = jnp.where(d == 1, 2, jnp.where(d == 2, 1, d))
            return pltpu.make_async_remote_copy(
                src_ref=src, dst_ref=land_ref.at[slot], send_sem=send_sems.at[fi, k], recv_sem=recv_sems.at[fi, k],
                device_id=(x, y, 1 - c), device_id_type=MESH)

        def up_copy(fi, g):
            return to_sibling(fi, g, dwup_ref.at[g, fi], land_up_ref, 2 * g + fi // 2)

        def down_copy(fi):
            return to_sibling(fi, 2, dwdown_ref.at[fi, pl.ds((1 - c) * half, half)], land_down_ref, fi)

        def flush(fi, s):
            return [pltpu.make_async_copy(acc_wup.at[s, 0], dwup_ref.at[0, fi], sem.at[s, 0]),
                    pltpu.make_async_copy(acc_wup.at[s, 1], dwup_ref.at[1, fi], sem.at[s, 1]),
                    pltpu.make_async_copy(acc_wdown.at[s], dwdown_ref.at[fi], sem.at[s, 2])]

        def flushed(fi, s):
            for cp in flush(fi, s):
                cp.wait()
            down_copy(fi).start()

            @pl.when(fi % 2 != c)
            def _():
                up_copy(fi, 0).start()
                up_copy(fi, 1).start()

        @pl.when(i == 0)
        def _():
            acc_wup[slot] = jnp.zeros(acc_wup.shape[1:], F32)
            acc_wdown[slot] = jnp.zeros(acc_wdown.shape[1:], F32)
            acc_cfw[...] = jnp.zeros(acc_cfw.shape, F32)
            acc_cfb[...] = jnp.zeros(acc_cfb.shape, F32)
            dextg_ref[tm:tm + HALO_F, :] = jnp.zeros((HALO_F, W_UP_BLK), F32)
            dextv_ref[tm:tm + HALO_F, :] = jnp.zeros((HALO_F, W_UP_BLK), F32)

        w = [cfw_ref[0, 0], cfw_ref[1, 0]]
        dext = [dextg_ref, dextv_ref]
        dhu = [dhug_ref, dhuv_ref]

        def rows1(bi):
            r = _rows(bi)
            gate = gv_ref[0, 0, r, :].astype(F32)
            val = gv_ref[1, 0, r, :].astype(F32)
            sg = _sigmoid(gate)
            silu = gate * sg
            gbuf_ref[r, :] = (silu * val).astype(BF16)
            dg = dg_ref[r, :]
            dgate = dg * val * (sg * (1.0 + gate * (1.0 - sg)))
            dval = dg * silu
            dextg_ref[r, :] = dgate
            dextv_ref[r, :] = dval
            acc_cfb[0:8, :] += _rsum8(dgate)
            acc_cfb[8:16, :] += _rsum8(dval)

        def rows2(bi):
            r = _rows(bi)
            for g in range(2):
                win = dext[g][pl.ds(bi * ROWS, ROWS + HALO_F), :]
                n = ROWS + HALO_F
                later = [pltpu.roll(win, n - 2, 0)[0:ROWS, :], pltpu.roll(win, n - 1, 0)[0:ROWS, :],
                         win[0:ROWS, :]]
                d = sum(later[k] * w[g][k:k + 1, :] for k in range(KF))
                dhu[g][r, :] = d.astype(BF16)
                pre = hu_ref[g, 0, r, :].astype(F32)
                for k in range(KF):
                    r0 = 8 * (g * KF + k)
                    acc_cfw[r0:r0 + 8, :] += _rsum8(later[k] * pre)

        for sub in reversed(range(tm // sub_rows)):
            rs = slice(sub * sub_rows, (sub + 1) * sub_rows)
            blocks = range(sub * sub_rows // ROWS, (sub + 1) * sub_rows // ROWS)
            x1b_ref[rs, :] = (xh_ref[rs, :] * g1_ref[...] + b1_ref[...]).astype(BF16)
            drb_ref[rs, :] = dr2_ref[rs, :].astype(BF16)
            dg_ref[rs, :] = _nt(drb_ref[rs, :], wdown_ref[0])
            for bi in blocks:
                rows1(bi)
            for bi in blocks:
                rows2(bi)
            acc_wdown[slot] += _tn(gbuf_ref[rs, :], drb_ref[rs, :])
            acc_wup[slot, 0] += _tn(dhug_ref[rs, :], x1b_ref[rs, :])
            acc_wup[slot, 1] += _tn(dhuv_ref[rs, :], x1b_ref[rs, :])
            dx1_ref[0, rs, :] = (_nt(dhug_ref[rs, :], wup_ref[0, 0])
                                 + _nt(dhuv_ref[rs, :], wup_ref[1, 0])).astype(BF16)
        dextg_ref[tm:tm + HALO_F, :] = dextg_ref[0:HALO_F, :]
        dextv_ref[tm:tm + HALO_F, :] = dextv_ref[0:HALO_F, :]

        @pl.when(i == nt - 1)
        def _():
            for g in range(2):
                dcfb_ref[g, 0] = jnp.sum(acc_cfb[8 * g:8 * g + 8, :], axis=0, keepdims=True)
                for k in range(KF):
                    r0 = 8 * (g * KF + k)
                    dcfw_ref[g, 0, k:k + 1, :] = jnp.sum(acc_cfw[r0:r0 + 8, :], axis=0, keepdims=True)
            for cp in flush(f, slot):
                cp.start()

        @pl.when((i == 0) & (fo > 0))
        def _():
            flushed(f_prev, 1 - slot)

        @pl.when((i == nt - 1) & (fo == N_F - 1))
        def _():
            flushed(f, slot)
            for fi in range(N_F):
                down_copy(fi).wait()
                for g in range(2):
                    @pl.when(fi % 2 != c)
                    def _():
                        up_copy(fi, g).wait_send()

                    @pl.when(fi % 2 == c)
                    def _():
                        up_copy(fi, g).wait_recv()

    rev = lambda i: nt - 1 - i
    row = pl.BlockSpec((tm, D), lambda fo, i, o: (rev(i), 0))
    pair = lambda r, c: pl.BlockSpec((2, 1, r, c), lambda fo, i, o: (0, o[fo], 0, 0))
    tile = pl.BlockSpec((2, 1, tm, W_UP_BLK), lambda fo, i, o: (0, o[fo], rev(i), 0))
    return pl.pallas_call(
        body, name="ffn_backward",
        grid_spec=pltpu.PrefetchScalarGridSpec(
            num_scalar_prefetch=1, grid=(N_F, nt),
            in_specs=[row, row, _full(ln1_g.shape), _full(ln1_b.shape), tile, tile,
                      pair(D, W_UP_BLK), pair(KF, W_UP_BLK),
                      pl.BlockSpec((1, W_UP_BLK, D), lambda fo, i, o: (o[fo], 0, 0))],
            out_specs=[ANY, ANY, pair(KF, W_UP_BLK), pair(1, W_UP_BLK),
                       pl.BlockSpec((1, tm, D), lambda fo, i, o: (o[fo], rev(i), 0)), ANY, ANY],
            scratch_shapes=[pltpu.VMEM((tm, D), BF16), pltpu.VMEM((tm, D), BF16),
                            pltpu.VMEM((tm, W_UP_BLK), F32),
                            pltpu.VMEM((tm + HALO_F, W_UP_BLK), F32), pltpu.VMEM((tm + HALO_F, W_UP_BLK), F32),
                            pltpu.VMEM((tm, W_UP_BLK), BF16), pltpu.VMEM((tm, W_UP_BLK), BF16),
                            pltpu.VMEM((tm, W_UP_BLK), BF16),
                            pltpu.VMEM((2, 2, W_UP_BLK, D), F32), pltpu.VMEM((2, W_UP_BLK, D), F32),
                            pltpu.VMEM((2 * KF * 8, W_UP_BLK), F32), pltpu.VMEM((16, W_UP_BLK), F32),
                            pltpu.SemaphoreType.DMA((2, 3)),
                            pltpu.SemaphoreType.DMA((N_F, 3)), pltpu.SemaphoreType.DMA((N_F, 3))]),
        out_shape=[jax.ShapeDtypeStruct((2, N_F, W_UP_BLK, D), F32),
                   jax.ShapeDtypeStruct((N_F, W_UP_BLK, D), F32),
                   jax.ShapeDtypeStruct((2, N_F, KF, W_UP_BLK), F32),
                   jax.ShapeDtypeStruct((2, N_F, 1, W_UP_BLK), F32),
                   jax.ShapeDtypeStruct((N_F, t, D), BF16),
                   jax.ShapeDtypeStruct((4, W_UP_BLK, D), F32),
                   jax.ShapeDtypeStruct((4, D_FF // N_DEV, D), F32)],
        compiler_params=_params(("arbitrary", "arbitrary")),
    )(order, dr2, xhat1, ln1_g, ln1_b, hu4, gv4, wup4, cfw4, wdown)


def mix_backward(x, h, yb1, dx1p, dr2, xhat1, rstd1, win_g, ln_a_g, ln_a_b, w_spatial, bst,
                 conv_b_w, ln_b_g, ln_b_b, wout, ln1_g, ffn_partials, tm):
    t = x.shape[0]
    n_p = len(ffn_partials)
    nt = t // tm
    n_chunks = tm // CHUNK
    halo_blocks = tm // HALO_B

    def body(x_ref, h_ref, halo_ref, yb1_ref, dx1p_ref, dr2_ref, xh1_ref, rstd1_ref, win_ref, ga_ref, ba_ref,
             ws_ref, bst_ref, cw_ref, gb_ref, bb_ref, wout_ref, g1_ref, *rest):
        p_refs, rest = rest[:n_p], rest[n_p:]
        gx_ref, dwin_ref, dwout_ref, dcw_ref, small_ref = rest[:5]
        land_refs, rest = rest[5:5 + n_p], rest[5 + n_p:]
        (ext_ref, dext_ref, y_ref, dy_ref, dh_ref, dmb_ref, wsm_ref,
         acc_win, acc_wout, acc_bin, acc_lnag, acc_lnab, acc_ws, acc_bs, acc_cbb, acc_lnbg,
         acc_lnbb, acc_bout, acc_ln1g, acc_ln1b, acc_cw, sem, send_sems, recv_sems) = rest
        i = pl.program_id(0)

        @pl.when(i == 0)
        def _():
            for cp in _chip_copies(p_refs, land_refs, send_sems, recv_sems):
                cp.start()

        first_tile = i == nt - 1
        accs = [acc_win, acc_wout, acc_bin, acc_lnag, acc_lnab, acc_ws, acc_bs, acc_cbb, acc_lnbg,
                acc_lnbb, acc_bout, acc_ln1g, acc_ln1b, acc_cw]

        @pl.when(i == 0)
        def _():
            for acc in accs:
                acc[...] = jnp.zeros(acc.shape, F32)
            dext_ref[tm:tm + HALO_B, :] = jnp.zeros((HALO_B, D_B), F32)
            mask = _tril_mask()
            for hd in range(HEADS):
                wsm_ref[hd] = jnp.where(mask, ws_ref[hd], 0.0).astype(BF16)

        def ln1_rows(bi):
            r = _rows(bi, LN_ROWS)
            part = [dx1p_ref[f, r, :].astype(F32) for f in range(N_F)]
            dx1 = ALPHA * dr2_ref[r, :] + ((part[0] + part[1]) + (part[2] + part[3]))
            xhat = xh1_ref[r, :]
            acc_ln1g[...] += _rsum8(dx1 * xhat)
            acc_ln1b[...] += _rsum8(dx1)
            dr1 = _ln_bwd(dx1 * g1_ref[...], xhat, rstd1_ref[r, 0:1])
            acc_bout[...] += _rsum8(dr1)
            gx_ref[r, :] = ALPHA * dr1
            dmb_ref[r, :] = dr1.astype(BF16)

        _loop(tm // LN_ROWS, ln1_rows)
        dy_ref[...] = _nt(dmb_ref[...], wout_ref[...])

        ha = halo_ref[:, 0:D_B]
        hg = halo_ref[:, D_B:2 * D_B]
        ext_ref[0:HALO_B, :] = jnp.where(first_tile, 0.0, 1.0) * (ha * _sigmoid(hg))

        def chunk(ci):
            r = _rows(ci, CHUNK)
            for hd in range(HEADS):
                sl = slice(hd * HEAD_DIM, (hd + 1) * HEAD_DIM)
                rows8 = slice(8 * hd, 8 * hd + 8)
                hus, hvs, u, cdf_u, cdf_v, xhat, rstd, vn, sv = _mixer_a_head(
                    h_ref, r, hd, ga_ref, ba_ref, wsm_ref, bst_ref)
                dy_a = dy_ref[r, sl]
                y_ref[r, sl] = (u * sv).astype(BF16)
                du = dy_a * sv
                dsv = dy_a * u
                dsvb = dsv.astype(BF16)
                acc_bs[hd] += dsv
                acc_ws[hd] += _nt(dsvb, vn)
                dvn = _tn(wsm_ref[hd], dsvb)
                acc_lnag[rows8, :] += _rsum8(dvn * xhat)
                acc_lnab[rows8, :] += _rsum8(dvn)
                dv = _ln_bwd(dvn * ga_ref[hd:hd + 1, :], xhat, rstd)
                slv = slice(D_A + hd * HEAD_DIM, D_A + (hd + 1) * HEAD_DIM)
                dhu = du * (cdf_u + hus * jnp.exp(-0.5 * hus * hus) * INV_SQRT_2PI)
                dhv = dv * (cdf_v + hvs * jnp.exp(-0.5 * hvs * hvs) * INV_SQRT_2PI)
                acc_bin[:, sl] += _rsum8(dhu)
                acc_bin[:, slv] += _rsum8(dhv)
                dh_ref[r, sl] = dhu.astype(BF16)
                dh_ref[r, slv] = dhv.astype(BF16)
            a_b = h_ref[r, 2 * D_A:2 * D_A + D_B]
            g_b = h_ref[r, 2 * D_A + D_B:D_IN]
            ext_ref[pl.ds(HALO_B + ci * CHUNK, CHUNK), :] = a_b * _sigmoid(g_b)

        _loop(n_chunks, chunk)

        def conv_rows(bi):
            base = bi * ROWS
            r = pl.ds(base, ROWS)
            xhat, rstd = _ln_stats(yb1_ref[r, :])
            yb2 = xhat * gb_ref[...] + bb_ref[...]
            sg = _sigmoid(yb2)
            y_ref[r, D_A:D] = (yb2 * sg).astype(BF16)
            dyb2 = dy_ref[r, D_A:D] * (sg * (1.0 + yb2 * (1.0 - sg)))
            acc_lnbg[...] += _rsum8(dyb2 * xhat)
            acc_lnbb[...] += _rsum8(dyb2)
            dyb1 = _ln_bwd(dyb2 * gb_ref[...], xhat, rstd)
            acc_cbb[...] += _rsum8(dyb1)
            dext_ref[r, :] = dyb1
            for k, tap in _taps(ext_ref[pl.ds(base, ROWS + HALO_B), :], CONV_B_OFFSETS):
                acc_cw[8 * k:8 * k + 8, :] += _rsum8(dyb1 * tap)

        _loop(tm // ROWS, conv_rows)

        def convt_rows(bi):
            base = bi * ROWS
            r = pl.ds(base, ROWS)
            dyb0 = jnp.zeros((ROWS, D_B), F32)
            for k, tap in _taps(dext_ref[pl.ds(base, ROWS + HALO_B), :], CONV_B_T_OFFSETS):
                dyb0 = dyb0 + tap * cw_ref[k:k + 1, :]
            a_b = h_ref[r, 2 * D_A:2 * D_A + D_B]
            sg = _sigmoid(h_ref[r, 2 * D_A + D_B:D_IN])
            da_b = dyb0 * sg
            dg_b = dyb0 * a_b * sg * (1.0 - sg)
            acc_bin[:, 2 * D_A:2 * D_A + D_B] += _rsum8(da_b)
            acc_bin[:, 2 * D_A + D_B:D_IN] += _rsum8(dg_b)
            dh_ref[r, 2 * D_A:2 * D_A + D_B] = da_b.astype(BF16)
            dh_ref[r, 2 * D_A + D_B:D_IN] = dg_b.astype(BF16)

        _loop(tm // ROWS, convt_rows)
        dext_ref[tm:tm + HALO_B, :] = dext_ref[0:HALO_B, :]

        acc_wout[...] += _tn(y_ref[...], dmb_ref[...])
        xt = x_ref[...].T.astype(BF16)
        dh_blocks = [dh_ref[:, j * W_IN_BLK:(j + 1) * W_IN_BLK] for j in range(N_DEV)]
        for j in range(N_DEV):
            acc_win[j] += _nn(xt, dh_blocks[j])
        gx_ref[...] += sum(_nt(dh_blocks[j], win_ref[j]) for j in range(N_DEV))

        @pl.when(i == nt - 1)
        def _():
            cps = [pltpu.make_async_copy(acc_win, dwin_ref, sem.at[0]),
                   pltpu.make_async_copy(acc_wout, dwout_ref, sem.at[1])]
            for cp in cps:
                cp.start()
            small_ref[...] = jnp.zeros(small_ref.shape, F32)

            def put_row_vector(row0, acc):
                vec = jnp.sum(acc[...], axis=0, keepdims=True)
                for k in range(vec.shape[1] // 128):
                    small_ref[row0 + k:row0 + k + 1, :] = vec[:, k * 128:(k + 1) * 128]

            put_row_vector(S_BIN, acc_bin)
            put_row_vector(S_CBB, acc_cbb)
            put_row_vector(S_LNBG, acc_lnbg)
            put_row_vector(S_LNBB, acc_lnbb)
            put_row_vector(S_BOUT, acc_bout)
            put_row_vector(S_LN1G, acc_ln1g)
            put_row_vector(S_LN1B, acc_ln1b)
            mask = _tril_mask()
            for hd in range(HEADS):
                rows8 = slice(8 * hd, 8 * hd + 8)
                small_ref[S_LNAG + hd:S_LNAG + hd + 1, :] = jnp.sum(acc_lnag[rows8, :], axis=0, keepdims=True)
                small_ref[S_LNAB + hd:S_LNAB + hd + 1, :] = jnp.sum(acc_lnab[rows8, :], axis=0, keepdims=True)
                small_ref[S_WS + hd * CHUNK:S_WS + (hd + 1) * CHUNK, :] = jnp.where(mask, acc_ws[hd], 0.0)
                small_ref[S_BS + hd:S_BS + hd + 1, :] = jnp.sum(acc_bs[hd].T, axis=0, keepdims=True)
            for k in range(KB):
                dcw_ref[k:k + 1, :] = jnp.sum(acc_cw[8 * k:8 * k + 8, :], axis=0, keepdims=True)
            for cp in cps:
                cp.wait()
            for cp in _chip_copies(p_refs, land_refs, send_sems, recv_sems):
                cp.wait()

    rev = lambda i: nt - 1 - i
    row = lambda w: pl.BlockSpec((tm, w), lambda i: (rev(i), 0))
    return pl.pallas_call(
        body, name="mix_backward", grid=(nt,),
        in_specs=[row(D), row(D_IN),
                  pl.BlockSpec((HALO_B, 2 * D_B), lambda i: (jnp.maximum(rev(i) * halo_blocks - 1, 0), 1)),
                  row(D_B), pl.BlockSpec((N_F, tm, D), lambda i: (0, rev(i), 0)),
                  row(D), row(D), row(128), _resident(win_g.shape), _full(ln_a_g.shape),
                  _full(ln_a_b.shape), _full(w_spatial.shape), _full(bst.shape), _full(conv_b_w.shape),
                  _full(ln_b_g.shape), _full(ln_b_b.shape),
                  _resident(wout.shape), _full(ln1_g.shape)] + [ANY] * n_p,
        out_specs=[row(D), ANY, ANY, _full((KB, D_B)), _full((S_MIX_ROWS, 128))] + [ANY] * n_p,
        out_shape=[jax.ShapeDtypeStruct((t, D), F32), jax.ShapeDtypeStruct((N_DEV, D, W_IN_BLK), F32),
                   jax.ShapeDtypeStruct((D, D), F32), jax.ShapeDtypeStruct((KB, D_B), F32),
                   jax.ShapeDtypeStruct((S_MIX_ROWS, 128), F32)]
        + [jax.ShapeDtypeStruct(p.shape, BF16) for p in ffn_partials],
        scratch_shapes=[pltpu.VMEM((tm + HALO_B, D_B), F32), pltpu.VMEM((tm + HALO_B, D_B), F32),
                        pltpu.VMEM((tm, D), BF16), pltpu.VMEM((tm, D), F32), pltpu.VMEM((tm, D_IN), BF16),
                        pltpu.VMEM((tm, D), BF16),
                        pltpu.VMEM((HEADS, CHUNK, CHUNK), BF16),
                        pltpu.VMEM((N_DEV, D, W_IN_BLK), F32), pltpu.VMEM((D, D), F32),
                        pltpu.VMEM((8, D_IN), F32), pltpu.VMEM((8 * HEADS, HEAD_DIM), F32),
                        pltpu.VMEM((8 * HEADS, HEAD_DIM), F32), pltpu.VMEM((HEADS, CHUNK, CHUNK), F32),
                        pltpu.VMEM((HEADS, CHUNK, CHUNK), F32), pltpu.VMEM((8, D_B), F32),
                        pltpu.VMEM((8, D_B), F32), pltpu.VMEM((8, D_B), F32), pltpu.VMEM((8, D), F32),
                        pltpu.VMEM((8, D), F32), pltpu.VMEM((8, D), F32), pltpu.VMEM((8 * KB, D_B), F32),
                        pltpu.SemaphoreType.DMA((2,)),
                        pltpu.SemaphoreType.DMA((n_p, 3)), pltpu.SemaphoreType.DMA((n_p, 3))],
        compiler_params=_params(("arbitrary",)),
    )(x, h, h, yb1, dx1p, dr2, xhat1, rstd1, win_g, ln_a_g, ln_a_b, w_spatial, bst, conv_b_w,
      ln_b_g, ln_b_b, wout, ln1_g, *ffn_partials)


def _rows128(a):
    return a.reshape(-1, 128)


def _pack_conv(cb, cf):
    lead = cb.shape[:-2]
    pad = [(0, 0)] * len(lead)
    flat = jnp.pad(cb.reshape(lead + (KB * 64,)), pad + [(0, 3 * W_UP_BLK - KB * 64)])
    rows = jnp.concatenate([cf, flat.reshape(lead + (3, W_UP_BLK))], axis=-2)
    return jnp.pad(rows, pad + [(0, 2), (0, 768 - W_UP_BLK)])


def _unpack_conv(p):
    lead = p.shape[:-2]
    cf = p[..., 0:KF, 0:W_UP_BLK]
    cb = p[..., 3:6, 0:W_UP_BLK].reshape(lead + (3 * W_UP_BLK,))[..., :KB * 64].reshape(lead + (KB, 64))
    return cb, cf


def kernel(x, w_in, b_in, ln_a_g, ln_a_b, w_spatial, b_spatial, conv_b_w, conv_b_b, ln_b_g, ln_b_b, w_out, b_out, ln1_g, ln1_b, w_up, conv_f_w, conv_f_b, w_down, ln2_g, ln2_b, loss_target, m_w_in, m_b_in, m_ln_a_g, m_ln_a_b, m_w_spatial, m_b_spatial, m_conv_b_w, m_conv_b_b, m_ln_b_g, m_ln_b_b, m_w_out, m_b_out, m_ln1_g, m_ln1_b, m_w_up, m_conv_f_w, m_conv_f_b, m_w_down, m_ln2_g, m_ln2_b, v_w_in, v_b_in, v_ln_a_g, v_ln_a_b, v_w_spatial, v_b_spatial, v_conv_b_w, v_conv_b_b, v_ln_b_g, v_ln_b_b, v_w_out, v_b_out, v_ln1_g, v_ln1_b, v_w_up, v_conv_f_w, v_conv_f_b, v_w_down, v_ln2_g, v_ln2_b):
    t = x.shape[1]
    x2 = x.reshape(t, D)
    target = loss_target.reshape(t, D)
    tm_fwd = min(t, 512)
    tm_bwd = min(t, 256)
    tm_ffn_bwd = min(t, 512)

    xi, yi, ci = _mesh_pos()
    jidx = jnp.stack([_lid(px, py, ci) for px, py in _chip_patterns(xi, yi)]).astype(jnp.int32)

    sin, sout, sup, sdown, conv_g = prepare_weights(w_in, w_out, w_up.T, w_down, _pack_conv(conv_b_w, conv_f_w))
    conv_b_all, cfw = _unpack_conv(conv_g)
    conv_b_full = conv_b_all.transpose(1, 0, 2).reshape(KB, D_B)
    cfb = conv_f_b.reshape(N_DEV, W_UP_BLK)
    row = lambda a: a.reshape(1, -1)
    bst = b_spatial.T

    h, xhat1, rstd1, yb1, win_g, wout_g, wup_g, wdown_g = mix_forward(
        x2, sin, sout, row(b_in), ln_a_g, ln_a_b, w_spatial, bst, conv_b_full, row(conv_b_b),
        row(ln_b_g), row(ln_b_b), row(b_out), row(ln1_g), row(ln1_b), sup, sdown, tm_fwd)
    wout_full = wout_g.reshape(D, D)
    wdown4 = wdown_g.reshape(N_F, W_UP_BLK, D)
    order_fwd = jnp.stack([2 * yi + ci, 2 * yi + 1 - ci, 2 * (1 - yi) + ci, 2 * (1 - yi) + 1 - ci]).astype(jnp.int32)
    hu, gv, dr2, loss_part, s_ln2, wup_g = ffn_forward(
        order_fwd, xhat1, row(ln1_g), row(ln1_b), wup_g, cfw, cfb, wdown4, row(ln2_g), row(ln2_b), target, tm_bwd)

    order = jnp.where(ci == 0, jnp.array([1, 3, 0, 2], jnp.int32), jnp.array([0, 2, 1, 3], jnp.int32))
    dwup, dwdown, dcfw, dcfb, dx1p, *ffn_lands = ffn_backward(
        order, dr2, xhat1, row(ln1_g), row(ln1_b), hu, gv, wup_g, cfw, wdown4, tm_ffn_bwd)
    ffn_grads = [dwup.reshape(N_DEV, W_UP_BLK, D), dwdown.reshape(N_DEV, D_FF // N_DEV, D)]
    ffn_partials = [chip_partials("chip_partials_" + nm, g, l, jidx, rb)
                    for nm, g, l, rb in zip(["w_up", "w_down"], ffn_grads, ffn_lands, [352, 352])]
    grad_x, dwin, dwout, dcw, s_mix, *ffn_recvs = mix_backward(
        x2, h, yb1, dx1p, dr2, xhat1, rstd1, win_g, ln_a_g, ln_a_b, w_spatial, bst,
        conv_b_full, row(ln_b_g), row(ln_b_b), wout_full, row(ln1_g), ffn_partials, tm_bwd)

    dcfb_rows = jnp.pad(dcfb.reshape(-1, 128), ((0, 4), (0, 0)))
    svec = jnp.concatenate([s_mix, dcfb_rows, s_ln2, loss_part], axis=0)
    dconv = _pack_conv(dcw.reshape(KB, N_DEV, 64).transpose(1, 0, 2), dcfw.reshape(N_DEV, KF, W_UP_BLK))
    mix_grads = [dwin, dwout.reshape(N_DEV, D // N_DEV, D), dconv]
    mix_w = [w_in, w_out, _pack_conv(conv_b_w, conv_f_w)]
    mix_m = [m_w_in, m_w_out, _pack_conv(m_conv_b_w, m_conv_f_w)]
    mix_v = [v_w_in, v_w_out, _pack_conv(v_conv_b_w, v_conv_f_w)]
    *mix_sums, sv_slots = mixer_reduce(mix_grads, svec)
    big = {}

    ffn_w = [(w_up.T, m_w_up.T, v_w_up.T), (w_down, m_w_down, v_w_down)]
    for nm, g, l, r, (w, m, v) in zip(["w_up", "w_down"], ffn_grads, ffn_lands, ffn_recvs, ffn_w):
        big[nm] = reduce_and_adamw("reduce_adamw_" + nm, g, l, r, w, m, v, jidx, 352)
    big["w_up"] = [o.T for o in big["w_up"]]

    small_w = dict(b_in=b_in, ln_a_g=ln_a_g, ln_a_b=ln_a_b, w_spatial=w_spatial, b_spatial=b_spatial,
                   conv_b_b=conv_b_b, ln_b_g=ln_b_g, ln_b_b=ln_b_b, b_out=b_out, ln1_g=ln1_g,
                   ln1_b=ln1_b, conv_f_b=conv_f_b, ln2_g=ln2_g, ln2_b=ln2_b)
    small_m = dict(b_in=m_b_in, ln_a_g=m_ln_a_g, ln_a_b=m_ln_a_b, w_spatial=m_w_spatial,
                   b_spatial=m_b_spatial, conv_b_b=m_conv_b_b, ln_b_g=m_ln_b_g, ln_b_b=m_ln_b_b,
                   b_out=m_b_out, ln1_g=m_ln1_g, ln1_b=m_ln1_b, conv_f_b=m_conv_f_b, ln2_g=m_ln2_g,
                   ln2_b=m_ln2_b)
    small_v = dict(b_in=v_b_in, ln_a_g=v_ln_a_g, ln_a_b=v_ln_a_b, w_spatial=v_w_spatial,
                   b_spatial=v_b_spatial, conv_b_b=v_conv_b_b, ln_b_g=v_ln_b_g, ln_b_b=v_ln_b_b,
                   b_out=v_b_out, ln1_g=v_ln1_g, ln1_b=v_ln1_b, conv_f_b=v_conv_f_b, ln2_g=v_ln2_g,
                   ln2_b=v_ln2_b)
    order = [nm for nm, _, _ in SMALL_LAYOUT]
    small_out = small_adamw(sv_slots, [_rows128(small_w[nm]) for nm in order],
                            [_rows128(small_m[nm]) for nm in order], [_rows128(small_v[nm]) for nm in order],
                            mix_sums, mix_w, mix_m, mix_v)
    n_small = len(order)
    mix_out = small_out[4 * n_small + 1:]
    big.update({nm: [mix_out[k * 3 + p] for k in range(4)] for p, nm in enumerate(["w_in", "w_out", "conv"])})
    for k in range(4):
        cb_k, cf_k = _unpack_conv(big["conv"][k])
        big.setdefault("conv_b_w", []).append(cb_k)
        big.setdefault("conv_f_w", []).append(cf_k)
    small = {nm: [small_out[k * n_small + p].reshape(small_w[nm].shape) for k in range(4)]
             for p, nm in enumerate(order)}
    loss = jnp.sum(small_out[4 * n_small]) * (0.5 / D)

    weights = ["w_in", "b_in", "ln_a_g", "ln_a_b", "w_spatial", "b_spatial", "conv_b_w", "conv_b_b",
               "ln_b_g", "ln_b_b", "w_out", "b_out", "ln1_g", "ln1_b", "w_up", "conv_f_w", "conv_f_b",
               "w_down", "ln2_g", "ln2_b"]
    result = lambda nm, k: big[nm][k] if nm in big else small[nm][k]
    return (loss, grad_x.reshape(x.shape), *[result(nm, 0) for nm in weights],
            *[result(nm, 1) for nm in weights], *[result(nm, 2) for nm in weights],
            *[result(nm, 3) for nm in weights])
```

```python
import functools
import math

import jax
import jax.numpy as jnp
from jax import lax
from jax.experimental import pallas as pl
from jax.experimental.pallas import tpu as pltpu

F32 = jnp.float32
BF16 = jnp.bfloat16

D = 1024
D_A = 512
D_B = 512
HEADS = 4
HEAD_DIM = 128
CHUNK = 128
KB = 31
KF = 3
D_FF = 2816
D_IN = 2048
N_DEV = 8
W_IN_BLK = D_IN // N_DEV
W_UP_BLK = 2 * D_FF // N_DEV
N_F = 4
LN_EPS = 1e-5
ALPHA = 2.0 ** 0.25

ADAM_LR = 0.001
ADAM_B1 = 0.9
ADAM_B2 = 0.999
ADAM_EPS = 1e-08
ADAM_WD = 0.01
ADAM_STEP = 10

INV_SQRT2 = 1.0 / math.sqrt(2.0)
INV_SQRT_2PI = 1.0 / math.sqrt(2.0 * math.pi)

HALO_B = 32
HALO_F = 8
ROWS = 64
LN_ROWS = 32
VMEM_LIMIT = 58 * 1024 * 1024

MESH = pl.DeviceIdType.MESH
ANY = pl.BlockSpec(memory_space=pl.ANY)
VMEM = pl.BlockSpec(memory_space=pltpu.VMEM)

S_BIN, S_LNAG, S_LNAB, S_WS, S_BS, S_CBB, S_LNBG, S_LNBB, S_BOUT, S_LN1G, S_LN1B = (
    0, 16, 24, 32, 544, 552, 560, 568, 576, 584, 592)
S_MIX_ROWS = 600
S_CFB = 600
S_LN2G = 648
S_LN2B = 656
S_LOSS = 664
S_ROWS = 672


def _tn(a, b):
    return lax.dot_general(a, b, (((0,), (0,)), ((), ())), preferred_element_type=F32)


def _nt(a, b):
    return lax.dot_general(a, b, (((1,), (1,)), ((), ())), preferred_element_type=F32)


def _nn(a, b):
    return jnp.dot(a, b, preferred_element_type=F32)


def _sigmoid(x):
    return 1.0 / (1.0 + jnp.exp(-x))


def _ln_stats(x):
    mu = jnp.mean(x, axis=-1, keepdims=True)
    xc = x - mu
    var = jnp.mean(xc * xc, axis=-1, keepdims=True)
    rstd = lax.rsqrt(var + LN_EPS)
    return xc * rstd, rstd


def _ln_bwd(dxhat, xhat, rstd):
    m1 = jnp.mean(dxhat, axis=-1, keepdims=True)
    m2 = jnp.mean(dxhat * xhat, axis=-1, keepdims=True)
    return rstd * (dxhat - m1 - xhat * m2)


def _rsum8(x):
    r, n = x.shape
    return x.reshape(r // 8, 8, n).sum(axis=0)


def _rows(i, n=ROWS):
    return pl.ds(i * n, n)


def _loop(n, body):
    for i in range(n):
        body(i)


def _tril_mask():
    r = lax.broadcasted_iota(jnp.int32, (CHUNK, CHUNK), 0)
    c = lax.broadcasted_iota(jnp.int32, (CHUNK, CHUNK), 1)
    return c <= r


def _mixer_a_head(h_ref, r, hd, ga_ref, ba_ref, wsm_ref, bst_ref):
    sl = slice(hd * HEAD_DIM, (hd + 1) * HEAD_DIM)
    hu = h_ref[r, sl]
    hv = h_ref[r, D_A + hd * HEAD_DIM:D_A + (hd + 1) * HEAD_DIM]
    cdf_u = 0.5 * (1.0 + lax.erf(hu * INV_SQRT2))
    cdf_v = 0.5 * (1.0 + lax.erf(hv * INV_SQRT2))
    u = hu * cdf_u
    xhat, rstd = _ln_stats(hv * cdf_v)
    vn = (xhat * ga_ref[hd:hd + 1, :] + ba_ref[hd:hd + 1, :]).astype(BF16)
    sv = _nn(wsm_ref[hd], vn) + bst_ref[:, hd:hd + 1]
    return hu, hv, u, cdf_u, cdf_v, xhat, rstd, vn, sv


def _taps(win, offsets):
    n = win.shape[0]
    for s in range(8):
        ks = [k for k, o in enumerate(offsets) if o % 8 == s]
        if ks:
            moved = win if s == 0 else pltpu.roll(win, n - s, 0)
            for k in ks:
                yield k, moved[offsets[k] - s:offsets[k] - s + ROWS, :]


CONV_B_OFFSETS = [2 + k for k in range(KB)]
CONV_B_T_OFFSETS = [30 - k for k in range(KB)]


def _conv_b_block(ext_ref, base, cw_ref):
    acc = jnp.zeros((ROWS, D_B), F32)
    for k, tap in _taps(ext_ref[pl.ds(base, ROWS + HALO_B), :], CONV_B_OFFSETS):
        acc = acc + tap * cw_ref[k:k + 1, :]
    return acc


def _taps_f(win):
    n = ROWS + HALO_F
    return [pltpu.roll(win, n - 6, 0)[0:ROWS, :], pltpu.roll(win, n - 7, 0)[0:ROWS, :], win[8:n, :]]


def _params(sem, **kw):
    return pltpu.CompilerParams(dimension_semantics=sem, vmem_limit_bytes=VMEM_LIMIT, **kw)


def _resident(shape):
    zeros = (0,) * len(shape)
    return pl.BlockSpec(shape, lambda *_: zeros, pipeline_mode=pl.Buffered(1))


def _full(shape):
    zeros = (0,) * len(shape)
    return pl.BlockSpec(shape, lambda *_: zeros)


def _mesh_pos():
    return lax.axis_index("x"), lax.axis_index("y"), lax.axis_index("c")


def _chip_patterns(x, y):
    return [(x, y), (1 - x, y), (x, 1 - y), (1 - x, 1 - y)]


def _lid(x, y, c):
    return 4 * x + 2 * y + c


def _gather_copy(outs, send_sems, recv_sems, a, k, block, to, src=None):
    blk = outs[a].at[_lid(*block)]
    return pltpu.make_async_remote_copy(
        src_ref=blk if src is None else src, dst_ref=blk,
        send_sem=send_sems.at[a, k], recv_sem=recv_sems.at[a, k], device_id=to, device_id_type=MESH)


def _gather_start(mine, outs, send_sems, recv_sems, local_sems, diagonal=False, row_only=()):
    x, y, c = _mesh_pos()
    me = (x, y, c)
    for a in range(len(mine)):
        pltpu.make_async_copy(mine[a], outs[a].at[_lid(*me)], local_sems.at[a]).start()
        targets = [(x, y, 1 - c), (1 - x, y, c), (x, 1 - y, c), (1 - x, 1 - y, c)]
        for k, to in enumerate(targets[:2] if a in row_only else targets if diagonal else targets[:3]):
            _gather_copy(outs, send_sems, recv_sems, a, k, me, to, src=mine[a]).start()


def _gather_relay(mine, outs, send_sems, recv_sems, local_sems, via, row_only=()):
    x, y, c = _mesh_pos()
    me, sib = (x, y, c), (x, y, 1 - c)
    copy = functools.partial(_gather_copy, outs, send_sems, recv_sems)
    source = {1: (1 - x, y, c), 2: (x, 1 - y, c)}
    for a in range(len(mine)):
        for k in ((1,) if a in row_only else (via[a], 3 - via[a]) if via[a] else (1, 2)):
            copy(a, k, source[k], me).wait_recv()
            if k == via[a] and a not in row_only:
                copy(a, 3, source[k], source[3 - k]).start()
            copy(a, 3 + k, source[k], sib).start()


def _gather_finish(mine, outs, send_sems, recv_sems, local_sems, row_only=()):
    x, y, c = _mesh_pos()
    me, sib = (x, y, c), (x, y, 1 - c)
    copy = functools.partial(_gather_copy, outs, send_sems, recv_sems)
    diag = (1 - x, 1 - y)
    n = len(mine)
    for a in range(n):
        if a not in row_only:
            copy(a, 3, (*diag, c), me).wait_recv()
            copy(a, 6, (*diag, c), sib).start()
    for a in range(n):
        copy(a, 0, sib, me).wait_recv()
        for k, chip in zip((4, 5, 6), [(1 - x, y), (x, 1 - y), diag]):
            if k == 4 or a not in row_only:
                copy(a, k, (*chip, 1 - c), me).wait_recv()
        for k in ((0, 1, 4) if a in row_only else range(7)):
            copy(a, k, me, sib, src=mine[a]).wait_send()
        pltpu.make_async_copy(mine[a], outs[a].at[_lid(*me)], local_sems.at[a]).wait()


def _gather_scratch(n):
    return [pltpu.SemaphoreType.DMA((n, 7)), pltpu.SemaphoreType.DMA((n, 7)), pltpu.SemaphoreType.DMA((n,))]


def prepare_weights(w_in, w_out, w_up_t, w_down, convp):
    def body(win_ref, wout_ref, wup_ref, wdown_ref, convp_ref,
             sin_ref, sout_ref, sup_ref, sdown_ref, gconv_ref, send_sems, recv_sems, local_sems):
        gather = ([convp_ref], [gconv_ref], send_sems, recv_sems, local_sems)
        _gather_start(*gather, diagonal=True)
        sin_ref[...] = win_ref[...].astype(BF16)
        sout_ref[...] = wout_ref[...].astype(BF16)
        sup_ref[...] = wup_ref[...].T.astype(BF16)
        sdown_ref[...] = wdown_ref[...].astype(BF16)
        _gather_relay(*gather, via=[0])
        _gather_finish(*gather)

    return pl.pallas_call(
        body, name="prepare_weights",
        out_shape=[jax.ShapeDtypeStruct(w_in.shape, BF16), jax.ShapeDtypeStruct(w_out.shape, BF16),
                   jax.ShapeDtypeStruct(w_up_t.shape[::-1], BF16), jax.ShapeDtypeStruct(w_down.shape, BF16),
                   jax.ShapeDtypeStruct((N_DEV,) + convp.shape, F32)],
        in_specs=[VMEM] * 5, out_specs=[VMEM] * 4 + [ANY],
        scratch_shapes=_gather_scratch(1),
        compiler_params=pltpu.CompilerParams(vmem_limit_bytes=VMEM_LIMIT),
    )(w_in, w_out, w_up_t, w_down, convp)


def _chip_copies(p, land, send_sems, recv_sems):
    x, y, c = _mesh_pos()
    return [pltpu.make_async_remote_copy(
        src_ref=p[a].at[k], dst_ref=land[a].at[k], send_sem=send_sems.at[a, k], recv_sem=recv_sems.at[a, k],
        device_id=(px, py, c), device_id_type=MESH)
        for k, (px, py) in enumerate(_chip_patterns(x, y)[1:]) for a in range(len(p))]


def chip_partials(name, g, land, jidx, rb):
    _, r, c = g.shape

    def body(j_ref, g_ref, l_ref, o_ref):
        o_ref[...] = (g_ref[...] + l_ref[...]).astype(BF16)

    return pl.pallas_call(
        body, name=name,
        out_shape=jax.ShapeDtypeStruct((3, r, c), BF16),
        grid_spec=pltpu.PrefetchScalarGridSpec(
            num_scalar_prefetch=1, grid=(3, r // rb),
            in_specs=[pl.BlockSpec((1, rb, c), lambda k, i, j: (j[1 + k], i, 0)),
                      pl.BlockSpec((1, rb, c), lambda k, i, j: (1 + k, i, 0))],
            out_specs=pl.BlockSpec((1, rb, c), lambda k, i, j: (k, i, 0))),
        compiler_params=_params(("arbitrary", "arbitrary")),
    )(jidx, g, land)


def _adamw(w, g, m, v):
    m2 = ADAM_B1 * m + (1.0 - ADAM_B1) * g
    v2 = ADAM_B2 * v + (1.0 - ADAM_B2) * (g * g)
    m_hat = m2 / (1.0 - ADAM_B1 ** ADAM_STEP)
    v_hat = v2 / (1.0 - ADAM_B2 ** ADAM_STEP)
    delta = -ADAM_LR * (m_hat / (jnp.sqrt(v_hat) + ADAM_EPS) + ADAM_WD * w)
    return delta, m2, v2


def reduce_and_adamw(name, g, land, recv, w, m, v, jidx, rb):
    _, r, c = g.shape

    def body(j_ref, g_ref, l_ref, r_ref, w_ref, m_ref, v_ref, go_ref, do_ref, mo_ref, vo_ref):
        grad = (g_ref[0] + l_ref[0]) + r_ref[0].astype(F32) + r_ref[1].astype(F32) + r_ref[2].astype(F32)
        delta, m2, v2 = _adamw(w_ref[...], grad, m_ref[...], v_ref[...])
        go_ref[...] = grad
        do_ref[...] = delta
        mo_ref[...] = m2
        vo_ref[...] = v2

    blk = pl.BlockSpec((rb, c), lambda i, j: (i, 0))
    return pl.pallas_call(
        body, name=name,
        out_shape=[jax.ShapeDtypeStruct((r, c), F32)] * 4,
        grid_spec=pltpu.PrefetchScalarGridSpec(
            num_scalar_prefetch=1, grid=(r // rb,),
            in_specs=[pl.BlockSpec((1, rb, c), lambda i, j: (j[0], i, 0)),
                      pl.BlockSpec((1, rb, c), lambda i, j: (0, i, 0)),
                      pl.BlockSpec((3, rb, c), lambda i, j: (0, i, 0)),
                      blk, blk, blk],
            out_specs=[blk] * 4),
        compiler_params=_params(("arbitrary",)),
    )(jidx, g, land, recv, w, m, v)


def mixer_reduce(grads, svec):
    n = len(grads)
    shard = [g.shape[1:] for g in grads]

    def body(*refs):
        g = refs[:n]
        sv_ref = refs[n]
        outs = refs[n + 1:2 * n + 1]
        sv_slots = refs[2 * n + 1]
        rest = refs[2 * n + 2:]
        own, land, sendb, recvb = rest[:n], rest[n:2 * n], rest[2 * n:3 * n], rest[3 * n:4 * n]
        sv_land, chip_sv, d2d_send, d2d_recv, ici_send, ici_recv, local_sems, sv_sems = rest[4 * n:]
        x, y, c = _mesh_pos()
        sib = (x, y, 1 - c)
        pats = _chip_patterns(x, y)
        q = 2 * x + y

        d2d, local = {}, {}
        for a in range(n):
            for k, (px, py) in enumerate(pats):
                d2d[a, k] = pltpu.make_async_remote_copy(
                    src_ref=g[a].at[_lid(px, py, 1 - c)], dst_ref=land[a].at[k],
                    send_sem=d2d_send.at[a, k], recv_sem=d2d_recv.at[a, k], device_id=sib, device_id_type=MESH)
                local[a, k] = pltpu.make_async_copy(g[a].at[_lid(px, py, c)], own[a].at[k], local_sems.at[a, k])
        sv_d2d = pltpu.make_async_remote_copy(
            src_ref=sv_ref, dst_ref=sv_land, send_sem=d2d_send.at[n, 0], recv_sem=d2d_recv.at[n, 0],
            device_id=sib, device_id_type=MESH)
        blocks = [(a, k) for a in range(n) for k in (1, 2, 3)] + [(a, 0) for a in range(n)]
        sv_d2d.start()
        for b in blocks:
            d2d[b].start()
            local[b].start()

        half_rows = svec.shape[0] // 2
        rows = pl.ds(pl.multiple_of(c * half_rows, 8), half_rows)
        sv_local = pltpu.make_async_copy(chip_sv, sv_slots.at[q], sv_sems.at[0])

        def sv_ici(k, slot, to):
            return pltpu.make_async_remote_copy(
                src_ref=chip_sv.at[rows], dst_ref=sv_slots.at[slot, rows], send_sem=sv_sems.at[1 + k],
                recv_sem=sv_sems.at[4 + k], device_id=to, device_id_type=MESH)

        def sv_pass_on(k, slot):
            return pltpu.make_async_remote_copy(
                src_ref=sv_slots.at[slot, rows], dst_ref=sv_slots.at[slot, rows], send_sem=sv_sems.at[7 + k],
                recv_sem=sv_sems.at[10 + k], device_id=sib, device_id_type=MESH)

        sv_d2d.wait()
        chip_sv[...] = sv_ref[...] + sv_land[...]
        sv_out = [sv_ici(k, q, (px, py, c)) for k, (px, py) in enumerate(pats[1:])]
        for cp in sv_out + [sv_local]:
            cp.start()

        ici = _chip_copies(sendb, recvb, ici_send, ici_recv)
        for a, k in blocks:
            local[a, k].wait()
            d2d[a, k].wait()
            if k > 0:
                sendb[a][k - 1] = (own[a][k] + land[a][k]).astype(BF16)
                ici[(k - 1) * n + a].start()
        for k, (px, py) in enumerate(pats[1:]):
            sv_out[k].wait_send()
            sv_ici(k, 2 * px + py, (px, py, c)).wait_recv()
            sv_pass_on(k, 2 * px + py).start()
        for a in range(n):
            for k in range(3):
                ici[k * n + a].wait()
            outs[a][...] = ((own[a][0] + land[a][0]) + recvb[a][0].astype(F32) + recvb[a][1].astype(F32)
                            + recvb[a][2].astype(F32))
        for k, (px, py) in enumerate(pats[1:]):
            sv_pass_on(k, 2 * px + py).wait()
        sv_local.wait()

    shard_out = [jax.ShapeDtypeStruct(s, F32) for s in shard]
    return pl.pallas_call(
        body, name="mixer_reduce",
        out_shape=shard_out + [jax.ShapeDtypeStruct((4,) + svec.shape, F32)],
        in_specs=[ANY] * n + [VMEM], out_specs=[VMEM] * n + [ANY],
        scratch_shapes=[pltpu.VMEM((4,) + s, F32) for s in shard] + [pltpu.VMEM((4,) + s, F32) for s in shard]
        + [pltpu.VMEM((3,) + s, BF16) for s in shard] + [pltpu.VMEM((3,) + s, BF16) for s in shard]
        + [pltpu.VMEM(svec.shape, F32), pltpu.VMEM(svec.shape, F32),
           pltpu.SemaphoreType.DMA((n + 1, 4)), pltpu.SemaphoreType.DMA((n + 1, 4)),
           pltpu.SemaphoreType.DMA((n, 3)), pltpu.SemaphoreType.DMA((n, 3)),
           pltpu.SemaphoreType.DMA((n, 4)), pltpu.SemaphoreType.DMA((13,))],
        compiler_params=pltpu.CompilerParams(vmem_limit_bytes=VMEM_LIMIT),
    )(*grads, svec)


SMALL_LAYOUT = [
    ("b_in", S_BIN, 16), ("ln_a_g", S_LNAG, 4), ("ln_a_b", S_LNAB, 4), ("w_spatial", S_WS, 512),
    ("b_spatial", S_BS, 4), ("conv_b_b", S_CBB, 4), ("ln_b_g", S_LNBG, 4), ("ln_b_b", S_LNBB, 4),
    ("b_out", S_BOUT, 8), ("ln1_g", S_LN1G, 8), ("ln1_b", S_LN1B, 8), ("conv_f_b", S_CFB, 44),
    ("ln2_g", S_LN2G, 8), ("ln2_b", S_LN2B, 8),
]


def small_adamw(sv_slots, ws, ms, vs, shard_grads, shard_ws, shard_ms, shard_vs):
    n = len(SMALL_LAYOUT)
    nb = len(shard_grads)

    def body(*refs):
        s_ref = refs[0]
        w_refs, m_refs, v_refs = refs[1:1 + n], refs[1 + n:1 + 2 * n], refs[1 + 2 * n:1 + 3 * n]
        big_in = refs[1 + 3 * n:1 + 3 * n + 4 * nb]
        outs = refs[1 + 3 * n + 4 * nb:]
        big_out = outs[4 * n + 1:]
        for p in range(nb):
            grad = big_in[p][...]
            delta, m2, v2 = _adamw(big_in[nb + p][...], grad, big_in[2 * nb + p][...], big_in[3 * nb + p][...])
            big_out[p][...] = grad
            big_out[nb + p][...] = delta
            big_out[2 * nb + p][...] = m2
            big_out[3 * nb + p][...] = v2
        for p, (_, row0, rows) in enumerate(SMALL_LAYOUT):
            sl = pl.ds(row0, rows)
            grad = ((s_ref[0, sl, :] + s_ref[1, sl, :]) + s_ref[2, sl, :]) + s_ref[3, sl, :]
            delta, m2, v2 = _adamw(w_refs[p][...], grad, m_refs[p][...], v_refs[p][...])
            outs[p][...] = grad
            outs[n + p][...] = delta
            outs[2 * n + p][...] = m2
            outs[3 * n + p][...] = v2
        sl = pl.ds(S_LOSS, 8)
        outs[4 * n][...] = ((s_ref[0, sl, :] + s_ref[1, sl, :]) + s_ref[2, sl, :]) + s_ref[3, sl, :]

    shapes = [jax.ShapeDtypeStruct((rows, 128), F32) for _, _, rows in SMALL_LAYOUT]
    big_shapes = [jax.ShapeDtypeStruct(g.shape, F32) for g in shard_grads]
    return pl.pallas_call(
        body, name="small_adamw",
        out_shape=shapes * 4 + [jax.ShapeDtypeStruct((8, 128), F32)] + big_shapes * 4,
        in_specs=[VMEM] * (1 + 3 * n + 4 * nb), out_specs=[VMEM] * (4 * n + 1 + 4 * nb),
        compiler_params=pltpu.CompilerParams(vmem_limit_bytes=VMEM_LIMIT),
    )(sv_slots, *ws, *ms, *vs, *shard_grads, *shard_ws, *shard_ms, *shard_vs)


def mix_forward(x, sin, sout, b_in, ln_a_g, ln_a_b, w_spatial, bst, conv_b_w, conv_b_b, ln_b_g, ln_b_b,
                b_out, ln1_g, ln1_b, sup, sdown, tm):
    t = x.shape[0]
    nt = t // tm
    n_chunks = tm // CHUNK

    def body(x_ref, sin_ref, sout_ref, bin_ref, ga_ref, ba_ref, ws_ref, bst_ref, cw_ref, cb_ref, gb_ref,
             bb_ref, bout_ref, g1_ref, b1_ref, sup_ref, sdown_ref,
             h_ref, xhat1_ref, rstd1_ref, yb1_ref, gin_ref, gout_ref, gup_ref, gdown_ref,
             ext_ref, y_ref, wsm_ref, win_ref, wout_ref, load_sems,
             mix_send, mix_recv, mix_local, send_sems, recv_sems, local_sems):
        i = pl.program_id(0)
        mixer = ([sin_ref, sout_ref], [gin_ref, gout_ref], mix_send, mix_recv, mix_local)
        gather = ([sdown_ref, sup_ref], [gdown_ref, gup_ref], send_sems, recv_sems, local_sems)
        rows = dict(row_only=(1,))

        @pl.when(i == 0)
        def _():
            _gather_start(*mixer)
            _gather_relay(*mixer, via=[2, 2])
            _gather_finish(*mixer)
            _gather_start(*gather, **rows)
            loads = [pltpu.make_async_copy(gin_ref, win_ref, load_sems.at[0]),
                     pltpu.make_async_copy(gout_ref, wout_ref, load_sems.at[1])]
            for cp in loads:
                cp.start()
            for cp in loads:
                cp.wait()
            ext_ref[0:HALO_B, :] = jnp.zeros((HALO_B, D_B), F32)
            mask = _tril_mask()
            for hd in range(HEADS):
                wsm_ref[hd] = jnp.where(mask, ws_ref[hd], 0.0).astype(BF16)

        xb = x_ref[...].astype(BF16)
        for j in range(N_DEV):
            cols = slice(j * W_IN_BLK, (j + 1) * W_IN_BLK)
            h_ref[:, cols] = _nn(xb, win_ref[j]) + bin_ref[:, cols]

        def chunk(ci):
            r = _rows(ci, CHUNK)
            for hd in range(HEADS):
                _, _, u, _, _, _, _, _, sv = _mixer_a_head(h_ref, r, hd, ga_ref, ba_ref, wsm_ref, bst_ref)
                y_ref[r, hd * HEAD_DIM:(hd + 1) * HEAD_DIM] = (u * sv).astype(BF16)
            a_b = h_ref[r, 2 * D_A:2 * D_A + D_B]
            g_b = h_ref[r, 2 * D_A + D_B:D_IN]
            ext_ref[pl.ds(HALO_B + ci * CHUNK, CHUNK), :] = a_b * _sigmoid(g_b)

        _loop(n_chunks, chunk)

        def conv_rows(bi):
            base = bi * ROWS
            yb1 = _conv_b_block(ext_ref, base, cw_ref) + cb_ref[...]
            yb1_ref[pl.ds(base, ROWS), :] = yb1
            xhat, _ = _ln_stats(yb1)
            yb2 = xhat * gb_ref[...] + bb_ref[...]
            y_ref[pl.ds(base, ROWS), D_A:D] = (yb2 * _sigmoid(yb2)).astype(BF16)

        _loop(tm // ROWS, conv_rows)
        ext_ref[0:HALO_B, :] = ext_ref[tm:tm + HALO_B, :]

        mix = _nn(y_ref[...], wout_ref[...].reshape(D, D)) + bout_ref[...]
        xhat1, rstd1 = _ln_stats(ALPHA * x_ref[...] + mix)
        xhat1_ref[...] = xhat1
        rstd1_ref[...] = jnp.broadcast_to(rstd1, (tm, 128))

        @pl.when(i == nt // 2)
        def _():
            _gather_relay(*gather, via=[2, 0], **rows)

        @pl.when(i == nt - 1)
        def _():
            _gather_finish(*gather, **rows)

    row = lambda w: pl.BlockSpec((tm, w), lambda i: (i, 0))
    return pl.pallas_call(
        body, name="mix_forward", grid=(nt,),
        in_specs=[row(D), ANY, ANY, _full(b_in.shape), _full(ln_a_g.shape),
                  _full(ln_a_b.shape), _full(w_spatial.shape), _full(bst.shape),
                  _full(conv_b_w.shape), _full(conv_b_b.shape), _full(ln_b_g.shape),
                  _full(ln_b_b.shape), _full(b_out.shape),
                  _full(ln1_g.shape), _full(ln1_b.shape), ANY, ANY],
        out_specs=[row(D_IN), row(D), row(128), row(D_B), ANY, ANY, ANY, ANY],
        out_shape=[jax.ShapeDtypeStruct((t, D_IN), F32), jax.ShapeDtypeStruct((t, D), F32),
                   jax.ShapeDtypeStruct((t, 128), F32), jax.ShapeDtypeStruct((t, D_B), F32)]
        + [jax.ShapeDtypeStruct((N_DEV,) + sh.shape, BF16) for sh in (sin, sout, sup, sdown)],
        scratch_shapes=[pltpu.VMEM((tm + HALO_B, D_B), F32), pltpu.VMEM((tm, D), BF16),
                        pltpu.VMEM((HEADS, CHUNK, CHUNK), BF16),
                        pltpu.VMEM((N_DEV,) + sin.shape, BF16), pltpu.VMEM((N_DEV,) + sout.shape, BF16),
                        pltpu.SemaphoreType.DMA((2,))] + _gather_scratch(2) + _gather_scratch(2),
        compiler_params=_params(("arbitrary",)),
    )(x, sin, sout, b_in, ln_a_g, ln_a_b, w_spatial, bst, conv_b_w, conv_b_b, ln_b_g, ln_b_b,
      b_out, ln1_g, ln1_b, sup, sdown)


def ffn_forward(order, xhat1, ln1_g, ln1_b, wup_g, cfw, cfb, wdown, ln2_g, ln2_b, target, tm):
    t = xhat1.shape[0]
    nt = t // tm
    last = 1
    cfw4 = cfw.reshape(2, N_F, KF, W_UP_BLK)
    cfb4 = cfb.reshape(2, N_F, 1, W_UP_BLK)

    def body(order_ref, xh_ref, g1_ref, b1_ref, wup_in, cfw_ref, cfb_ref, wdown_ref, g2_ref, b2_ref, tgt_ref,
             hu_ref, gv_ref, dr2_ref, loss_ref, sln2_ref, wup_ref,
             x1b_ref, y_ref, hu32_ref, carry_ref, gbuf_ref, acc_loss, acc_g2, acc_b2, wbuf, wsem,
             send_sems, recv_sems):
        p = pl.program_id(0)
        i = pl.program_id(1)
        row = order_ref[p]
        x, y, c = _mesh_pos()
        me, sib, ynb = (x, y, c), (x, y, 1 - c), (x, 1 - y, c)

        def remote(k, block, to):
            blk = wup_ref.at[_lid(*block)]
            return pltpu.make_async_remote_copy(
                src_ref=blk, dst_ref=blk, send_sem=send_sems.at[k], recv_sem=recv_sems.at[k],
                device_id=to, device_id_type=MESH)

        def wload():
            return [pltpu.make_async_copy(wup_ref.at[g * N_F + 2 * row + q], wbuf.at[q, g], wsem.at[q, g])
                    for q in range(2) for g in range(2)]

        @pl.when((p == 0) & (i == 0))
        def _():
            remote(0, me, ynb).start()
            remote(1, (1 - x, y, c), ynb).start()
            acc_loss[...] = jnp.zeros(acc_loss.shape, F32)
            acc_g2[...] = jnp.zeros(acc_g2.shape, F32)
            acc_b2[...] = jnp.zeros(acc_b2.shape, F32)

        @pl.when((p == 1) & (i == 0))
        def _():
            for k, block in enumerate([(x, 1 - y, 1 - c), (1 - x, 1 - y, 1 - c)]):
                remote(2 + k, block, me).wait_recv()

        @pl.when(i == 0)
        def _():
            for cp in wload():
                cp.start()
            for cp in wload():
                cp.wait()
            carry_ref[...] = jnp.zeros(carry_ref.shape, F32)

        @pl.when(p == 0)
        def _():
            x1b_ref[i] = (xh_ref[...] * g1_ref[...] + b1_ref[...]).astype(BF16)
            y_ref[i] = jnp.zeros((tm, D), F32)

        def conv(q, g, base):
            if base == 0:
                win = jnp.concatenate([carry_ref[q, g], hu32_ref[g, 0:ROWS, :]], axis=0)
            else:
                win = hu32_ref[g, base - HALO_F:base + ROWS, :]
            taps = _taps_f(win)
            w = cfw_ref[g, q]
            return sum(taps[k] * w[k:k + 1, :] for k in range(KF)) + cfb_ref[g, q]

        for q in range(2):
            hu32_ref[0] = _nn(x1b_ref[i], wbuf[q, 0])
            hu32_ref[1] = _nn(x1b_ref[i], wbuf[q, 1])

            def rows(bi, q=q):
                r = _rows(bi)
                gate = conv(q, 0, bi * ROWS)
                val = conv(q, 1, bi * ROWS)
                gbuf_ref[r, :] = (gate * _sigmoid(gate) * val).astype(BF16)
                gv_ref[0, q, r, :] = gate.astype(BF16)
                gv_ref[1, q, r, :] = val.astype(BF16)
                hu_ref[0, q, r, :] = hu32_ref[0, r, :].astype(BF16)
                hu_ref[1, q, r, :] = hu32_ref[1, r, :].astype(BF16)

            _loop(tm // ROWS, rows)
            carry_ref[q, 0] = hu32_ref[0, tm - HALO_F:tm, :]
            carry_ref[q, 1] = hu32_ref[1, tm - HALO_F:tm, :]
            y_ref[i] += _nn(gbuf_ref[...], wdown_ref[2 * row + q])

        @pl.when(p == last)
        def _():
            def tail(bi):
                r = _rows(bi, LN_ROWS)
                x1 = xh_ref[r, :] * g1_ref[...] + b1_ref[...]
                xhat2, rstd2 = _ln_stats(ALPHA * x1 + y_ref[i, r, :])
                err = xhat2 * g2_ref[...] + b2_ref[...] - tgt_ref[r, :]
                e2 = _rsum8(err * err)
                acc_loss[...] += sum(e2[:, k * 128:(k + 1) * 128] for k in range(D // 128))
                dy = err * (1.0 / D)
                acc_g2[...] += _rsum8(dy * xhat2)
                acc_b2[...] += _rsum8(dy)
                dr2_ref[r, :] = _ln_bwd(dy * g2_ref[...], xhat2, rstd2)

            _loop(tm // LN_ROWS, tail)
            loss_ref[...] = acc_loss[...]

        @pl.when((p == 0) & (i == nt - 2))
        def _():
            for k, block in enumerate([(x, 1 - y, c), (1 - x, 1 - y, c)]):
                remote(k, block, me).wait_recv()
                remote(2 + k, block, sib).start()

        @pl.when((p == last) & (i == nt - 1))
        def _():
            for k in range(4):
                remote(k, me, sib).wait_send()
            dg = jnp.sum(acc_g2[...], axis=0, keepdims=True)
            db = jnp.sum(acc_b2[...], axis=0, keepdims=True)
            for k in range(D // 128):
                sln2_ref[k:k + 1, :] = dg[:, k * 128:(k + 1) * 128]
                sln2_ref[8 + k:9 + k, :] = db[:, k * 128:(k + 1) * 128]

    pair = lambda r, c: pl.BlockSpec((2, 2, r, c), lambda p, i, o: (0, o[p], 0, 0))
    tile = pl.BlockSpec((2, 2, tm, W_UP_BLK), lambda p, i, o: (0, o[p], i, 0))
    saved = jax.ShapeDtypeStruct((2, N_F, t, W_UP_BLK), BF16)
    return pl.pallas_call(
        body, name="ffn_forward",
        grid_spec=pltpu.PrefetchScalarGridSpec(
            num_scalar_prefetch=1, grid=(2, nt),
            in_specs=[pl.BlockSpec((tm, D), lambda p, i, o: (i, 0)),
                      _full(ln1_g.shape), _full(ln1_b.shape), ANY,
                      pair(KF, W_UP_BLK), pair(1, W_UP_BLK), _resident(wdown.shape),
                      _full(ln2_g.shape), _full(ln2_b.shape),
                      pl.BlockSpec((tm, D), lambda p, i, o: (jnp.where(p == last, i, 0), 0))],
            out_specs=[tile, tile,
                       pl.BlockSpec((tm, D), lambda p, i, o: (jnp.where(p == last, i, 0), 0)),
                       _full((8, 128)), _full((16, 128)), ANY],
            scratch_shapes=[pltpu.VMEM((nt, tm, D), BF16), pltpu.VMEM((nt, tm, D), F32),
                            pltpu.VMEM((2, tm, W_UP_BLK), F32),
                            pltpu.VMEM((2, 2, HALO_F, W_UP_BLK), F32), pltpu.VMEM((tm, W_UP_BLK), BF16),
                            pltpu.VMEM((8, 128), F32), pltpu.VMEM((8, D), F32), pltpu.VMEM((8, D), F32),
                            pltpu.VMEM((2, 2, D, W_UP_BLK), BF16), pltpu.SemaphoreType.DMA((2, 2)),
                            pltpu.SemaphoreType.DMA((4,)), pltpu.SemaphoreType.DMA((4,))]),
        out_shape=[saved, saved, jax.ShapeDtypeStruct((t, D), F32),
                   jax.ShapeDtypeStruct((8, 128), F32), jax.ShapeDtypeStruct((16, 128), F32),
                   jax.ShapeDtypeStruct(wup_g.shape, BF16)],
        input_output_aliases={4: 5},
        compiler_params=_params(("arbitrary", "arbitrary")),
    )(order, xhat1, ln1_g, ln1_b, wup_g, cfw4, cfb4, wdown, ln2_g, ln2_b, target)


def ffn_backward(order, dr2, xhat1, ln1_g, ln1_b, hu, gv, wup_g, cfw, wdown, tm):
    t = dr2.shape[0]
    nt = t // tm
    sub_rows = tm
    hu4 = hu.reshape(2, N_F, t, W_UP_BLK)
    gv4 = gv.reshape(2, N_F, t, W_UP_BLK)
    wup4 = wup_g.reshape(2, N_F, D, W_UP_BLK)
    cfw4 = cfw.reshape(2, N_F, KF, W_UP_BLK)

    def body(order_ref, dr2_ref, xh_ref, g1_ref, b1_ref, hu_ref, gv_ref, wup_ref, cfw_ref, wdown_ref,
             dwup_ref, dwdown_ref, dcfw_ref, dcfb_ref, dx1_ref, land_up_ref, land_down_ref,
             x1b_ref, drb_ref, dg_ref, dextg_ref, dextv_ref, gbuf_ref,
             dhug_ref, dhuv_ref, acc_wup, acc_wdown, acc_cfw, acc_cfb, sem, send_sems, recv_sems):
        fo = pl.program_id(0)
        f = order_ref[fo]
        f_prev = order_ref[jnp.maximum(fo - 1, 0)]
        slot = fo % 2
        i = pl.program_id(1)
        x, y, c = _mesh_pos()
        half = D_FF // N_DEV

        def to_sibling(fi, k, src, land_ref, shard_chip):
            d = jnp.bitwise_xor(shard_chip, 2 * x + y)
            slot = jnp.where(d == 1, 2, jnp.where(d == 2, 1, d))
            return pltpu.make_async_remote_copy(
                src_ref=src, dst_ref=land_ref.at[slot], send_sem=send_sems.at[fi, k], recv_sem=recv_sems.at[fi, k],
                device_id=(x, y, 1 - c), device_id_type=MESH)

        def up_copy(fi, g):
            return to_sibling(fi, g, dwup_ref.at[g, fi], land_up_ref, 2 * g + fi // 2)

        def down_copy(fi):
            return to_sibling(fi, 2, dwdown_ref.at[fi, pl.ds((1 - c) * half, half)], land_down_ref, fi)

        def flush(fi, s):
            return [pltpu.make_async_copy(acc_wup.at[s, 0], dwup_ref.at[0, fi], sem.at[s, 0]),
                    pltpu.make_async_copy(acc_wup.at[s, 1], dwup_ref.at[1, fi], sem.at[s, 1]),
                    pltpu.make_async_copy(acc_wdown.at[s], dwdown_ref.at[fi], sem.at[s, 2])]

        def flushed(fi, s):
            for cp in flush(fi, s):
                cp.wait()
            down_copy(fi).start()

            @pl.when(fi % 2 != c)
            def _():
                up_copy(fi, 0).start()
                up_copy(fi, 1).start()

        @pl.when(i == 0)
        def _():
            acc_wup[slot] = jnp.zeros(acc_wup.shape[1:], F32)
            acc_wdown[slot] = jnp.zeros(acc_wdown.shape[1:], F32)
            acc_cfw[...] = jnp.zeros(acc_cfw.shape, F32)
            acc_cfb[...] = jnp.zeros(acc_cfb.shape, F32)
            dextg_ref[tm:tm + HALO_F, :] = jnp.zeros((HALO_F, W_UP_BLK), F32)
            dextv_ref[tm:tm + HALO_F, :] = jnp.zeros((HALO_F, W_UP_BLK), F32)

        w = [cfw_ref[0, 0], cfw_ref[1, 0]]
        dext = [dextg_ref, dextv_ref]
        dhu = [dhug_ref, dhuv_ref]

        def rows1(bi):
            r = _rows(bi)
            gate = gv_ref[0, 0, r, :].astype(F32)
            val = gv_ref[1, 0, r, :].astype(F32)
            sg = _sigmoid(gate)
            silu = gate * sg
            gbuf_ref[r, :] = (silu * val).astype(BF16)
            dg = dg_ref[r, :]
            dgate = dg * val * (sg * (1.0 + gate * (1.0 - sg)))
            dval = dg * silu
            dextg_ref[r, :] = dgate
            dextv_ref[r, :] = dval
            acc_cfb[0:8, :] += _rsum8(dgate)
            acc_cfb[8:16, :] += _rsum8(dval)

        def rows2(bi):
            r = _rows(bi)
            for g in range(2):
                win = dext[g][pl.ds(bi * ROWS, ROWS + HALO_F), :]
                n = ROWS + HALO_F
                later = [pltpu.roll(win, n - 2, 0)[0:ROWS, :], pltpu.roll(win, n - 1, 0)[0:ROWS, :],
                         win[0:ROWS, :]]
                d = sum(later[k] * w[g][k:k + 1, :] for k in range(KF))
                dhu[g][r, :] = d.astype(BF16)
                pre = hu_ref[g, 0, r, :].astype(F32)
                for k in range(KF):
                    r0 = 8 * (g * KF + k)
                    acc_cfw[r0:r0 + 8, :] += _rsum8(later[k] * pre)

        for sub in reversed(range(tm // sub_rows)):
            rs = slice(sub * sub_rows, (sub + 1) * sub_rows)
            blocks = range(sub * sub_rows // ROWS, (sub + 1) * sub_rows // ROWS)
            x1b_ref[rs, :] = (xh_ref[rs, :] * g1_ref[...] + b1_ref[...]).astype(BF16)
            drb_ref[rs, :] = dr2_ref[rs, :].astype(BF16)
            dg_ref[rs, :] = _nt(drb_ref[rs, :], wdown_ref[0])
            for bi in blocks:
                rows1(bi)
            for bi in blocks:
                rows2(bi)
            acc_wdown[slot] += _tn(gbuf_ref[rs, :], drb_ref[rs, :])
            acc_wup[slot, 0] += _tn(dhug_ref[rs, :], x1b_ref[rs, :])
            acc_wup[slot, 1] += _tn(dhuv_ref[rs, :], x1b_ref[rs, :])
            dx1_ref[0, rs, :] = (_nt(dhug_ref[rs, :], wup_ref[0, 0])
                                 + _nt(dhuv_ref[rs, :], wup_ref[1, 0])).astype(BF16)
        dextg_ref[tm:tm + HALO_F, :] = dextg_ref[0:HALO_F, :]
        dextv_ref[tm:tm + HALO_F, :] = dextv_ref[0:HALO_F, :]

        @pl.when(i == nt - 1)
        def _():
            for g in range(2):
                dcfb_ref[g, 0] = jnp.sum(acc_cfb[8 * g:8 * g + 8, :], axis=0, keepdims=True)
                for k in range(KF):
                    r0 = 8 * (g * KF + k)
                    dcfw_ref[g, 0, k:k + 1, :] = jnp.sum(acc_cfw[r0:r0 + 8, :], axis=0, keepdims=True)
            for cp in flush(f, slot):
                cp.start()

        @pl.when((i == 0) & (fo > 0))
        def _():
            flushed(f_prev, 1 - slot)

        @pl.when((i == nt - 1) & (fo == N_F - 1))
        def _():
            flushed(f, slot)
            for fi in range(N_F):
                down_copy(fi).wait()
                for g in range(2):
                    @pl.when(fi % 2 != c)
                    def _():
                        up_copy(fi, g).wait_send()

                    @pl.when(fi % 2 == c)
                    def _():
                        up_copy(fi, g).wait_recv()

    rev = lambda i: nt - 1 - i
    row = pl.BlockSpec((tm, D), lambda fo, i, o: (rev(i), 0))
    pair = lambda r, c: pl.BlockSpec((2, 1, r, c), lambda fo, i, o: (0, o[fo], 0, 0))
    tile = pl.BlockSpec((2, 1, tm, W_UP_BLK), lambda fo, i, o: (0, o[fo], rev(i), 0))
    return pl.pallas_call(
        body, name="ffn_backward",
        grid_spec=pltpu.PrefetchScalarGridSpec(
            num_scalar_prefetch=1, grid=(N_F, nt),
            in_specs=[row, row, _full(ln1_g.shape), _full(ln1_b.shape), tile, tile,
                      pair(D, W_UP_BLK), pair(KF, W_UP_BLK),
                      pl.BlockSpec((1, W_UP_BLK, D), lambda fo, i, o: (o[fo], 0, 0))],
            out_specs=[ANY, ANY, pair(KF, W_UP_BLK), pair(1, W_UP_BLK),
                       pl.BlockSpec((1, tm, D), lambda fo, i, o: (o[fo], rev(i), 0)), ANY, ANY],
            scratch_shapes=[pltpu.VMEM((tm, D), BF16), pltpu.VMEM((tm, D), BF16),
                            pltpu.VMEM((tm, W_UP_BLK), F32),
                            pltpu.VMEM((tm + HALO_F, W_UP_BLK), F32), pltpu.VMEM((tm + HALO_F, W_UP_BLK), F32),
                            pltpu.VMEM((tm, W_UP_BLK), BF16), pltpu.VMEM((tm, W_UP_BLK), BF16),
                            pltpu.VMEM((tm, W_UP_BLK), BF16),
                            pltpu.VMEM((2, 2, W_UP_BLK, D), F32), pltpu.VMEM((2, W_UP_BLK, D), F32),
                            pltpu.VMEM((2 * KF * 8, W_UP_BLK), F32), pltpu.VMEM((16, W_UP_BLK), F32),
                            pltpu.SemaphoreType.DMA((2, 3)),
                            pltpu.SemaphoreType.DMA((N_F, 3)), pltpu.SemaphoreType.DMA((N_F, 3))]),
        out_shape=[jax.ShapeDtypeStruct((2, N_F, W_UP_BLK, D), F32),
                   jax.ShapeDtypeStruct((N_F, W_UP_BLK, D), F32),
                   jax.ShapeDtypeStruct((2, N_F, KF, W_UP_BLK), F32),
                   jax.ShapeDtypeStruct((2, N_F, 1, W_UP_BLK), F32),
                   jax.ShapeDtypeStruct((N_F, t, D), BF16),
                   jax.ShapeDtypeStruct((4, W_UP_BLK, D), F32),
                   jax.ShapeDtypeStruct((4, D_FF // N_DEV, D), F32)],
        compiler_params=_params(("arbitrary", "arbitrary")),
    )(order, dr2, xhat1, ln1_g, ln1_b, hu4, gv4, wup4, cfw4, wdown)


def mix_backward(x, h, yb1, dx1p, dr2, xhat1, rstd1, win_g, ln_a_g, ln_a_b, w_spatial, bst,
                 conv_b_w, ln_b_g, ln_b_b, wout, ln1_g, ffn_partials, tm):
    t = x.shape[0]
    n_p = len(ffn_partials)
    nt = t // tm
    n_chunks = tm // CHUNK
    halo_blocks = tm // HALO_B

    def body(x_ref, h_ref, halo_ref, yb1_ref, dx1p_ref, dr2_ref, xh1_ref, rstd1_ref, win_ref, ga_ref, ba_ref,
             ws_ref, bst_ref, cw_ref, gb_ref, bb_ref, wout_ref, g1_ref, *rest):
        p_refs, rest = rest[:n_p], rest[n_p:]
        gx_ref, dwin_ref, dwout_ref, dcw_ref, small_ref = rest[:5]
        land_refs, rest = rest[5:5 + n_p], rest[5 + n_p:]
        (ext_ref, dext_ref, y_ref, dy_ref, dh_ref, dmb_ref, wsm_ref,
         acc_win, acc_wout, acc_bin, acc_lnag, acc_lnab, acc_ws, acc_bs, acc_cbb, acc_lnbg,
         acc_lnbb, acc_bout, acc_ln1g, acc_ln1b, acc_cw, sem, send_sems, recv_sems) = rest
        i = pl.program_id(0)

        @pl.when(i == 0)
        def _():
            for cp in _chip_copies(p_refs, land_refs, send_sems, recv_sems):
                cp.start()

        first_tile = i == nt - 1
        accs = [acc_win, acc_wout, acc_bin, acc_lnag, acc_lnab, acc_ws, acc_bs, acc_cbb, acc_lnbg,
                acc_lnbb, acc_bout, acc_ln1g, acc_ln1b, acc_cw]

        @pl.when(i == 0)
        def _():
            for acc in accs:
                acc[...] = jnp.zeros(acc.shape, F32)
            dext_ref[tm:tm + HALO_B, :] = jnp.zeros((HALO_B, D_B), F32)
            mask = _tril_mask()
            for hd in range(HEADS):
                wsm_ref[hd] = jnp.where(mask, ws_ref[hd], 0.0).astype(BF16)

        def ln1_rows(bi):
            r = _rows(bi, LN_ROWS)
            part = [dx1p_ref[f, r, :].astype(F32) for f in range(N_F)]
            dx1 = ALPHA * dr2_ref[r, :] + ((part[0] + part[1]) + (part[2] + part[3]))
            xhat = xh1_ref[r, :]
            acc_ln1g[...] += _rsum8(dx1 * xhat)
            acc_ln1b[...] += _rsum8(dx1)
            dr1 = _ln_bwd(dx1 * g1_ref[...], xhat, rstd1_ref[r, 0:1])
            acc_bout[...] += _rsum8(dr1)
            gx_ref[r, :] = ALPHA * dr1
            dmb_ref[r, :] = dr1.astype(BF16)

        _loop(tm // LN_ROWS, ln1_rows)
        dy_ref[...] = _nt(dmb_ref[...], wout_ref[...])

        ha = halo_ref[:, 0:D_B]
        hg = halo_ref[:, D_B:2 * D_B]
        ext_ref[0:HALO_B, :] = jnp.where(first_tile, 0.0, 1.0) * (ha * _sigmoid(hg))

        def chunk(ci):
            r = _rows(ci, CHUNK)
            for hd in range(HEADS):
                sl = slice(hd * HEAD_DIM, (hd + 1) * HEAD_DIM)
                rows8 = slice(8 * hd, 8 * hd + 8)
                hus, hvs, u, cdf_u, cdf_v, xhat, rstd, vn, sv = _mixer_a_head(
                    h_ref, r, hd, ga_ref, ba_ref, wsm_ref, bst_ref)
                dy_a = dy_ref[r, sl]
                y_ref[r, sl] = (u * sv).astype(BF16)
                du = dy_a * sv
                dsv = dy_a * u
                dsvb = dsv.astype(BF16)
                acc_bs[hd] += dsv
                acc_ws[hd] += _nt(dsvb, vn)
                dvn = _tn(wsm_ref[hd], dsvb)
                acc_lnag[rows8, :] += _rsum8(dvn * xhat)
                acc_lnab[rows8, :] += _rsum8(dvn)
                dv = _ln_bwd(dvn * ga_ref[hd:hd + 1, :], xhat, rstd)
                slv = slice(D_A + hd * HEAD_DIM, D_A + (hd + 1) * HEAD_DIM)
                dhu = du * (cdf_u + hus * jnp.exp(-0.5 * hus * hus) * INV_SQRT_2PI)
                dhv = dv * (cdf_v + hvs * jnp.exp(-0.5 * hvs * hvs) * INV_SQRT_2PI)
                acc_bin[:, sl] += _rsum8(dhu)
                acc_bin[:, slv] += _rsum8(dhv)
                dh_ref[r, sl] = dhu.astype(BF16)
                dh_ref[r, slv] = dhv.astype(BF16)
            a_b = h_ref[r, 2 * D_A:2 * D_A + D_B]
            g_b = h_ref[r, 2 * D_A + D_B:D_IN]
            ext_ref[pl.ds(HALO_B + ci * CHUNK, CHUNK), :] = a_b * _sigmoid(g_b)

        _loop(n_chunks, chunk)

        def conv_rows(bi):
            base = bi * ROWS
            r = pl.ds(base, ROWS)
            xhat, rstd = _ln_stats(yb1_ref[r, :])
            yb2 = xhat * gb_ref[...] + bb_ref[...]
            sg = _sigmoid(yb2)
            y_ref[r, D_A:D] = (yb2 * sg).astype(BF16)
            dyb2 = dy_ref[r, D_A:D] * (sg * (1.0 + yb2 * (1.0 - sg)))
            acc_lnbg[...] += _rsum8(dyb2 * xhat)
            acc_lnbb[...] += _rsum8(dyb2)
            dyb1 = _ln_bwd(dyb2 * gb_ref[...], xhat, rstd)
            acc_cbb[...] += _rsum8(dyb1)
            dext_ref[r, :] = dyb1
            for k, tap in _taps(ext_ref[pl.ds(base, ROWS + HALO_B), :], CONV_B_OFFSETS):
                acc_cw[8 * k:8 * k + 8, :] += _rsum8(dyb1 * tap)

        _loop(tm // ROWS, conv_rows)

        def convt_rows(bi):
            base = bi * ROWS
            r = pl.ds(base, ROWS)
            dyb0 = jnp.zeros((ROWS, D_B), F32)
            for k, tap in _taps(dext_ref[pl.ds(base, ROWS + HALO_B), :], CONV_B_T_OFFSETS):
                dyb0 = dyb0 + tap * cw_ref[k:k + 1, :]
            a_b = h_ref[r, 2 * D_A:2 * D_A + D_B]
            sg = _sigmoid(h_ref[r, 2 * D_A + D_B:D_IN])
            da_b = dyb0 * sg
            dg_b = dyb0 * a_b * sg * (1.0 - sg)
            acc_bin[:, 2 * D_A:2 * D_A + D_B] += _rsum8(da_b)
            acc_bin[:, 2 * D_A + D_B:D_IN] += _rsum8(dg_b)
            dh_ref[r, 2 * D_A:2 * D_A + D_B] = da_b.astype(BF16)
            dh_ref[r, 2 * D_A + D_B:D_IN] = dg_b.astype(BF16)

        _loop(tm // ROWS, convt_rows)
        dext_ref[tm:tm + HALO_B, :] = dext_ref[0:HALO_B, :]

        acc_wout[...] += _tn(y_ref[...], dmb_ref[...])
        xt = x_ref[...].T.astype(BF16)
        dh_blocks = [dh_ref[:, j * W_IN_BLK:(j + 1) * W_IN_BLK] for j in range(N_DEV)]
        for j in range(N_DEV):
            acc_win[j] += _nn(xt, dh_blocks[j])
        gx_ref[...] += sum(_nt(dh_blocks[j], win_ref[j]) for j in range(N_DEV))

        @pl.when(i == nt - 1)
        def _():
            cps = [pltpu.make_async_copy(acc_win, dwin_ref, sem.at[0]),
                   pltpu.make_async_copy(acc_wout, dwout_ref, sem.at[1])]
            for cp in cps:
                cp.start()
            small_ref[...] = jnp.zeros(small_ref.shape, F32)

            def put_row_vector(row0, acc):
                vec = jnp.sum(acc[...], axis=0, keepdims=True)
                for k in range(vec.shape[1] // 128):
                    small_ref[row0 + k:row0 + k + 1, :] = vec[:, k * 128:(k + 1) * 128]

            put_row_vector(S_BIN, acc_bin)
            put_row_vector(S_CBB, acc_cbb)
            put_row_vector(S_LNBG, acc_lnbg)
            put_row_vector(S_LNBB, acc_lnbb)
            put_row_vector(S_BOUT, acc_bout)
            put_row_vector(S_LN1G, acc_ln1g)
            put_row_vector(S_LN1B, acc_ln1b)
            mask = _tril_mask()
            for hd in range(HEADS):
                rows8 = slice(8 * hd, 8 * hd + 8)
                small_ref[S_LNAG + hd:S_LNAG + hd + 1, :] = jnp.sum(acc_lnag[rows8, :], axis=0, keepdims=True)
                small_ref[S_LNAB + hd:S_LNAB + hd + 1, :] = jnp.sum(acc_lnab[rows8, :], axis=0, keepdims=True)
                small_ref[S_WS + hd * CHUNK:S_WS + (hd + 1) * CHUNK, :] = jnp.where(mask, acc_ws[hd], 0.0)
                small_ref[S_BS + hd:S_BS + hd + 1, :] = jnp.sum(acc_bs[hd].T, axis=0, keepdims=True)
            for k in range(KB):
                dcw_ref[k:k + 1, :] = jnp.sum(acc_cw[8 * k:8 * k + 8, :], axis=0, keepdims=True)
            for cp in cps:
                cp.wait()
            for cp in _chip_copies(p_refs, land_refs, send_sems, recv_sems):
                cp.wait()

    rev = lambda i: nt - 1 - i
    row = lambda w: pl.BlockSpec((tm, w), lambda i: (rev(i), 0))
    return pl.pallas_call(
        body, name="mix_backward", grid=(nt,),
        in_specs=[row(D), row(D_IN),
                  pl.BlockSpec((HALO_B, 2 * D_B), lambda i: (jnp.maximum(rev(i) * halo_blocks - 1, 0), 1)),
                  row(D_B), pl.BlockSpec((N_F, tm, D), lambda i: (0, rev(i), 0)),
                  row(D), row(D), row(128), _resident(win_g.shape), _full(ln_a_g.shape),
                  _full(ln_a_b.shape), _full(w_spatial.shape), _full(bst.shape), _full(conv_b_w.shape),
                  _full(ln_b_g.shape), _full(ln_b_b.shape),
                  _resident(wout.shape), _full(ln1_g.shape)] + [ANY] * n_p,
        out_specs=[row(D), ANY, ANY, _full((KB, D_B)), _full((S_MIX_ROWS, 128))] + [ANY] * n_p,
        out_shape=[jax.ShapeDtypeStruct((t, D), F32), jax.ShapeDtypeStruct((N_DEV, D, W_IN_BLK), F32),
                   jax.ShapeDtypeStruct((D, D), F32), jax.ShapeDtypeStruct((KB, D_B), F32),
                   jax.ShapeDtypeStruct((S_MIX_ROWS, 128), F32)]
        + [jax.ShapeDtypeStruct(p.shape, BF16) for p in ffn_partials],
        scratch_shapes=[pltpu.VMEM((tm + HALO_B, D_B), F32), pltpu.VMEM((tm + HALO_B, D_B), F32),
                        pltpu.VMEM((tm, D), BF16), pltpu.VMEM((tm, D), F32), pltpu.VMEM((tm, D_IN), BF16),
                        pltpu.VMEM((tm, D), BF16),
                        pltpu.VMEM((HEADS, CHUNK, CHUNK), BF16),
                        pltpu.VMEM((N_DEV, D, W_IN_BLK), F32), pltpu.VMEM((D, D), F32),
                        pltpu.VMEM((8, D_IN), F32), pltpu.VMEM((8 * HEADS, HEAD_DIM), F32),
                        pltpu.VMEM((8 * HEADS, HEAD_DIM), F32), pltpu.VMEM((HEADS, CHUNK, CHUNK), F32),
                        pltpu.VMEM((HEADS, CHUNK, CHUNK), F32), pltpu.VMEM((8, D_B), F32),
                        pltpu.VMEM((8, D_B), F32), pltpu.VMEM((8, D_B), F32), pltpu.VMEM((8, D), F32),
                        pltpu.VMEM((8, D), F32), pltpu.VMEM((8, D), F32), pltpu.VMEM((8 * KB, D_B), F32),
                        pltpu.SemaphoreType.DMA((2,)),
                        pltpu.SemaphoreType.DMA((n_p, 3)), pltpu.SemaphoreType.DMA((n_p, 3))],
        compiler_params=_params(("arbitrary",)),
    )(x, h, h, yb1, dx1p, dr2, xhat1, rstd1, win_g, ln_a_g, ln_a_b, w_spatial, bst, conv_b_w,
      ln_b_g, ln_b_b, wout, ln1_g, *ffn_partials)


def _rows128(a):
    return a.reshape(-1, 128)


def _pack_conv(cb, cf):
    lead = cb.shape[:-2]
    pad = [(0, 0)] * len(lead)
    flat = jnp.pad(cb.reshape(lead + (KB * 64,)), pad + [(0, 3 * W_UP_BLK - KB * 64)])
    rows = jnp.concatenate([cf, flat.reshape(lead + (3, W_UP_BLK))], axis=-2)
    return jnp.pad(rows, pad + [(0, 2), (0, 768 - W_UP_BLK)])


def _unpack_conv(p):
    lead = p.shape[:-2]
    cf = p[..., 0:KF, 0:W_UP_BLK]
    cb = p[..., 3:6, 0:W_UP_BLK].reshape(lead + (3 * W_UP_BLK,))[..., :KB * 64].reshape(lead + (KB, 64))
    return cb, cf


def kernel(x, w_in, b_in, ln_a_g, ln_a_b, w_spatial, b_spatial, conv_b_w, conv_b_b, ln_b_g, ln_b_b, w_out, b_out, ln1_g, ln1_b, w_up, conv_f_w, conv_f_b, w_down, ln2_g, ln2_b, loss_target, m_w_in, m_b_in, m_ln_a_g, m_ln_a_b, m_w_spatial, m_b_spatial, m_conv_b_w, m_conv_b_b, m_ln_b_g, m_ln_b_b, m_w_out, m_b_out, m_ln1_g, m_ln1_b, m_w_up, m_conv_f_w, m_conv_f_b, m_w_down, m_ln2_g, m_ln2_b, v_w_in, v_b_in, v_ln_a_g, v_ln_a_b, v_w_spatial, v_b_spatial, v_conv_b_w, v_conv_b_b, v_ln_b_g, v_ln_b_b, v_w_out, v_b_out, v_ln1_g, v_ln1_b, v_w_up, v_conv_f_w, v_conv_f_b, v_w_down, v_ln2_g, v_ln2_b):
    t = x.shape[1]
    x2 = x.reshape(t, D)
    target = loss_target.reshape(t, D)
    tm_fwd = min(t, 512)
    tm_bwd = min(t, 256)
    tm_ffn_bwd = min(t, 512)

    xi, yi, ci = _mesh_pos()
    jidx = jnp.stack([_lid(px, py, ci) for px, py in _chip_patterns(xi, yi)]).astype(jnp.int32)

    sin, sout, sup, sdown, conv_g = prepare_weights(w_in, w_out, w_up.T, w_down, _pack_conv(conv_b_w, conv_f_w))
    conv_b_all, cfw = _unpack_conv(conv_g)
    conv_b_full = conv_b_all.transpose(1, 0, 2).reshape(KB, D_B)
    cfb = conv_f_b.reshape(N_DEV, W_UP_BLK)
    row = lambda a: a.reshape(1, -1)
    bst = b_spatial.T

    h, xhat1, rstd1, yb1, win_g, wout_g, wup_g, wdown_g = mix_forward(
        x2, sin, sout, row(b_in), ln_a_g, ln_a_b, w_spatial, bst, conv_b_full, row(conv_b_b),
        row(ln_b_g), row(ln_b_b), row(b_out), row(ln1_g), row(ln1_b), sup, sdown, tm_fwd)
    wout_full = wout_g.reshape(D, D)
    wdown4 = wdown_g.reshape(N_F, W_UP_BLK, D)
    order_fwd = jnp.stack([yi, 1 - yi]).astype(jnp.int32)
    hu, gv, dr2, loss_part, s_ln2, wup_g = ffn_forward(
        order_fwd, xhat1, row(ln1_g), row(ln1_b), wup_g, cfw, cfb, wdown4, row(ln2_g), row(ln2_b), target, tm_bwd)

    order = jnp.where(ci == 0, jnp.array([1, 3, 0, 2], jnp.int32), jnp.array([0, 2, 1, 3], jnp.int32))
    dwup, dwdown, dcfw, dcfb, dx1p, *ffn_lands = ffn_backward(
        order, dr2, xhat1, row(ln1_g), row(ln1_b), hu, gv, wup_g, cfw, wdown4, tm_ffn_bwd)
    ffn_grads = [dwup.reshape(N_DEV, W_UP_BLK, D), dwdown.reshape(N_DEV, D_FF // N_DEV, D)]
    ffn_partials = [chip_partials("chip_partials_" + nm, g, l, jidx, rb)
                    for nm, g, l, rb in zip(["w_up", "w_down"], ffn_grads, ffn_lands, [352, 352])]
    grad_x, dwin, dwout, dcw, s_mix, *ffn_recvs = mix_backward(
        x2, h, yb1, dx1p, dr2, xhat1, rstd1, win_g, ln_a_g, ln_a_b, w_spatial, bst,
        conv_b_full, row(ln_b_g), row(ln_b_b), wout_full, row(ln1_g), ffn_partials, tm_bwd)

    dcfb_rows = jnp.pad(dcfb.reshape(-1, 128), ((0, 4), (0, 0)))
    svec = jnp.concatenate([s_mix, dcfb_rows, s_ln2, loss_part], axis=0)
    dconv = _pack_conv(dcw.reshape(KB, N_DEV, 64).transpose(1, 0, 2), dcfw.reshape(N_DEV, KF, W_UP_BLK))
    mix_grads = [dwin, dwout.reshape(N_DEV, D // N_DEV, D), dconv]
    mix_w = [w_in, w_out, _pack_conv(conv_b_w, conv_f_w)]
    mix_m = [m_w_in, m_w_out, _pack_conv(m_conv_b_w, m_conv_f_w)]
    mix_v = [v_w_in, v_w_out, _pack_conv(v_conv_b_w, v_conv_f_w)]
    *mix_sums, sv_slots = mixer_reduce(mix_grads, svec)
    big = {}

    ffn_w = [(w_up.T, m_w_up.T, v_w_up.T), (w_down, m_w_down, v_w_down)]
    for nm, g, l, r, (w, m, v) in zip(["w_up", "w_down"], ffn_grads, ffn_lands, ffn_recvs, ffn_w):
        big[nm] = reduce_and_adamw("reduce_adamw_" + nm, g, l, r, w, m, v, jidx, 352)
    big["w_up"] = [o.T for o in big["w_up"]]

    small_w = dict(b_in=b_in, ln_a_g=ln_a_g, ln_a_b=ln_a_b, w_spatial=w_spatial, b_spatial=b_spatial,
                   conv_b_b=conv_b_b, ln_b_g=ln_b_g, ln_b_b=ln_b_b, b_out=b_out, ln1_g=ln1_g,
                   ln1_b=ln1_b, conv_f_b=conv_f_b, ln2_g=ln2_g, ln2_b=ln2_b)
    small_m = dict(b_in=m_b_in, ln_a_g=m_ln_a_g, ln_a_b=m_ln_a_b, w_spatial=m_w_spatial,
                   b_spatial=m_b_spatial, conv_b_b=m_conv_b_b, ln_b_g=m_ln_b_g, ln_b_b=m_ln_b_b,
                   b_out=m_b_out, ln1_g=m_ln1_g, ln1_b=m_ln1_b, conv_f_b=m_conv_f_b, ln2_g=m_ln2_g,
                   ln2_b=m_ln2_b)
    small_v = dict(b_in=v_b_in, ln_a_g=v_ln_a_g, ln_a_b=v_ln_a_b, w_spatial=v_w_spatial,
                   b_spatial=v_b_spatial, conv_b_b=v_conv_b_b, ln_b_g=v_ln_b_g, ln_b_b=v_ln_b_b,
                   b_out=v_b_out, ln1_g=v_ln1_g, ln1_b=v_ln1_b, conv_f_b=v_conv_f_b, ln2_g=v_ln2_g,
                   ln2_b=v_ln2_b)
    order = [nm for nm, _, _ in SMALL_LAYOUT]
    small_out = small_adamw(sv_slots, [_rows128(small_w[nm]) for nm in order],
                            [_rows128(small_m[nm]) for nm in order], [_rows128(small_v[nm]) for nm in order],
                            mix_sums, mix_w, mix_m, mix_v)
    n_small = len(order)
    mix_out = small_out[4 * n_small + 1:]
    big.update({nm: [mix_out[k * 3 + p] for k in range(4)] for p, nm in enumerate(["w_in", "w_out", "conv"])})
    for k in range(4):
        cb_k, cf_k = _unpack_conv(big["conv"][k])
        big.setdefault("conv_b_w", []).append(cb_k)
        big.setdefault("conv_f_w", []).append(cf_k)
    small = {nm: [small_out[k * n_small + p].reshape(small_w[nm].shape) for k in range(4)]
             for p, nm in enumerate(order)}
    loss = jnp.sum(small_out[4 * n_small]) * (0.5 / D)

    weights = ["w_in", "b_in", "ln_a_g", "ln_a_b", "w_spatial", "b_spatial", "conv_b_w", "conv_b_b",
               "ln_b_g", "ln_b_b", "w_out", "b_out", "ln1_g", "ln1_b", "w_up", "conv_f_w", "conv_f_b",
               "w_down", "ln2_g", "ln2_b"]
    result = lambda nm, k: big[nm][k] if nm in big else small[nm][k]
    return (loss, grad_x.reshape(x.shape), *[result(nm, 0) for nm in weights],
            *[result(nm, 1) for nm in weights], *[result(nm, 2) for nm in weights],
            *[result(nm, 3) for nm in weights])
```

```python
import functools
import math

import jax
import jax.numpy as jnp
from jax import lax
from jax.experimental import pallas as pl
from jax.experimental.pallas import tpu as pltpu

F32 = jnp.float32
BF16 = jnp.bfloat16

D = 1024
D_A = 512
D_B = 512
HEADS = 4
HEAD_DIM = 128
CHUNK = 128
KB = 31
KF = 3
D_FF = 2816
D_IN = 2048
N_DEV = 8
W_IN_BLK = D_IN // N_DEV
W_UP_BLK = 2 * D_FF // N_DEV
N_F = 4
LN_EPS = 1e-5
ALPHA = 2.0 ** 0.25

ADAM_LR = 0.001
ADAM_B1 = 0.9
ADAM_B2 = 0.999
ADAM_EPS = 1e-08
ADAM_WD = 0.01
ADAM_STEP = 10

INV_SQRT2 = 1.0 / math.sqrt(2.0)
INV_SQRT_2PI = 1.0 / math.sqrt(2.0 * math.pi)

HALO_B = 32
HALO_F = 8
ROWS = 64
LN_ROWS = 32
VMEM_LIMIT = 58 * 1024 * 1024

MESH = pl.DeviceIdType.MESH
ANY = pl.BlockSpec(memory_space=pl.ANY)
VMEM = pl.BlockSpec(memory_space=pltpu.VMEM)

S_BIN, S_LNAG, S_LNAB, S_WS, S_BS, S_CBB, S_LNBG, S_LNBB, S_BOUT, S_LN1G, S_LN1B = (
    0, 16, 24, 32, 544, 552, 560, 568, 576, 584, 592)
S_MIX_ROWS = 600
S_CFB = 600
S_LN2G = 648
S_LN2B = 656
S_LOSS = 664
S_ROWS = 672


def _tn(a, b):
    return lax.dot_general(a, b, (((0,), (0,)), ((), ())), preferred_element_type=F32)


def _nt(a, b):
    return lax.dot_general(a, b, (((1,), (1,)), ((), ())), preferred_element_type=F32)


def _nn(a, b):
    return jnp.dot(a, b, preferred_element_type=F32)


def _sigmoid(x):
    return 1.0 / (1.0 + jnp.exp(-x))


def _ln_stats(x):
    mu = jnp.mean(x, axis=-1, keepdims=True)
    xc = x - mu
    var = jnp.mean(xc * xc, axis=-1, keepdims=True)
    rstd = lax.rsqrt(var + LN_EPS)
    return xc * rstd, rstd


def _ln_bwd(dxhat, xhat, rstd):
    m1 = jnp.mean(dxhat, axis=-1, keepdims=True)
    m2 = jnp.mean(dxhat * xhat, axis=-1, keepdims=True)
    return rstd * (dxhat - m1 - xhat * m2)


def _rsum8(x):
    r, n = x.shape
    return x.reshape(r // 8, 8, n).sum(axis=0)


def _rows(i, n=ROWS):
    return pl.ds(i * n, n)


def _loop(n, body):
    for i in range(n):
        body(i)


def _tril_mask():
    r = lax.broadcasted_iota(jnp.int32, (CHUNK, CHUNK), 0)
    c = lax.broadcasted_iota(jnp.int32, (CHUNK, CHUNK), 1)
    return c <= r


def _mixer_a_head(h_ref, r, hd, ga_ref, ba_ref, wsm_ref, bst_ref):
    sl = slice(hd * HEAD_DIM, (hd + 1) * HEAD_DIM)
    hu = h_ref[r, sl]
    hv = h_ref[r, D_A + hd * HEAD_DIM:D_A + (hd + 1) * HEAD_DIM]
    cdf_u = 0.5 * (1.0 + lax.erf(hu * INV_SQRT2))
    cdf_v = 0.5 * (1.0 + lax.erf(hv * INV_SQRT2))
    u = hu * cdf_u
    xhat, rstd = _ln_stats(hv * cdf_v)
    vn = (xhat * ga_ref[hd:hd + 1, :] + ba_ref[hd:hd + 1, :]).astype(BF16)
    sv = _nn(wsm_ref[hd], vn) + bst_ref[:, hd:hd + 1]
    return hu, hv, u, cdf_u, cdf_v, xhat, rstd, vn, sv


def _taps(win, offsets):
    n = win.shape[0]
    for s in range(8):
        ks = [k for k, o in enumerate(offsets) if o % 8 == s]
        if ks:
            moved = win if s == 0 else pltpu.roll(win, n - s, 0)
            for k in ks:
                yield k, moved[offsets[k] - s:offsets[k] - s + ROWS, :]


CONV_B_OFFSETS = [2 + k for k in range(KB)]
CONV_B_T_OFFSETS = [30 - k for k in range(KB)]


def _conv_b_block(ext_ref, base, cw_ref):
    acc = jnp.zeros((ROWS, D_B), F32)
    for k, tap in _taps(ext_ref[pl.ds(base, ROWS + HALO_B), :], CONV_B_OFFSETS):
        acc = acc + tap * cw_ref[k:k + 1, :]
    return acc


def _taps_f(win):
    n = ROWS + HALO_F
    return [pltpu.roll(win, n - 6, 0)[0:ROWS, :], pltpu.roll(win, n - 7, 0)[0:ROWS, :], win[8:n, :]]


def _params(sem, **kw):
    return pltpu.CompilerParams(dimension_semantics=sem, vmem_limit_bytes=VMEM_LIMIT, **kw)


def _resident(shape):
    zeros = (0,) * len(shape)
    return pl.BlockSpec(shape, lambda *_: zeros, pipeline_mode=pl.Buffered(1))


def _full(shape):
    zeros = (0,) * len(shape)
    return pl.BlockSpec(shape, lambda *_: zeros)


def _mesh_pos():
    return lax.axis_index("x"), lax.axis_index("y"), lax.axis_index("c")


def _chip_patterns(x, y):
    return [(x, y), (1 - x, y), (x, 1 - y), (1 - x, 1 - y)]


def _lid(x, y, c):
    return 4 * x + 2 * y + c


def _gather_copy(outs, send_sems, recv_sems, a, k, block, to, src=None):
    blk = outs[a].at[_lid(*block)]
    return pltpu.make_async_remote_copy(
        src_ref=blk if src is None else src, dst_ref=blk,
        send_sem=send_sems.at[a, k], recv_sem=recv_sems.at[a, k], device_id=to, device_id_type=MESH)


def _gather_start(mine, outs, send_sems, recv_sems, local_sems, diagonal=False):
    x, y, c = _mesh_pos()
    me = (x, y, c)
    for a in range(len(mine)):
        pltpu.make_async_copy(mine[a], outs[a].at[_lid(*me)], local_sems.at[a]).start()
        targets = [(x, y, 1 - c), (1 - x, y, c), (x, 1 - y, c), (1 - x, 1 - y, c)]
        for k, to in enumerate(targets if diagonal else targets[:3]):
            _gather_copy(outs, send_sems, recv_sems, a, k, me, to, src=mine[a]).start()


def _gather_relay(mine, outs, send_sems, recv_sems, local_sems, via):
    x, y, c = _mesh_pos()
    me, sib = (x, y, c), (x, y, 1 - c)
    copy = functools.partial(_gather_copy, outs, send_sems, recv_sems)
    source = {1: (1 - x, y, c), 2: (x, 1 - y, c)}
    for a in range(len(mine)):
        for k in ((via[a], 3 - via[a]) if via[a] else (1, 2)):
            copy(a, k, source[k], me).wait_recv()
            if k == via[a]:
                copy(a, 3, source[k], source[3 - k]).start()
            copy(a, 3 + k, source[k], sib).start()


def _gather_finish(mine, outs, send_sems, recv_sems, local_sems):
    x, y, c = _mesh_pos()
    me, sib = (x, y, c), (x, y, 1 - c)
    copy = functools.partial(_gather_copy, outs, send_sems, recv_sems)
    diag = (1 - x, 1 - y)
    n = len(mine)
    for a in range(n):
        copy(a, 3, (*diag, c), me).wait_recv()
        copy(a, 6, (*diag, c), sib).start()
    for a in range(n):
        copy(a, 0, sib, me).wait_recv()
        for k, chip in zip((4, 5, 6), [(1 - x, y), (x, 1 - y), diag]):
            copy(a, k, (*chip, 1 - c), me).wait_recv()
        for k in range(7):
            copy(a, k, me, sib, src=mine[a]).wait_send()
        pltpu.make_async_copy(mine[a], outs[a].at[_lid(*me)], local_sems.at[a]).wait()


def _gather_scratch(n):
    return [pltpu.SemaphoreType.DMA((n, 7)), pltpu.SemaphoreType.DMA((n, 7)), pltpu.SemaphoreType.DMA((n,))]


def prepare_weights(w_in, w_out, w_up_t, w_down, convp):
    def body(win_ref, wout_ref, wup_ref, wdown_ref, convp_ref,
             sin_ref, sout_ref, sup_ref, sdown_ref, gconv_ref, send_sems, recv_sems, local_sems):
        gather = ([convp_ref], [gconv_ref], send_sems, recv_sems, local_sems)
        _gather_start(*gather, diagonal=True)
        sin_ref[...] = win_ref[...].astype(BF16)
        sout_ref[...] = wout_ref[...].astype(BF16)
        sup_ref[...] = wup_ref[...].T.astype(BF16)
        sdown_ref[...] = wdown_ref[...].astype(BF16)
        _gather_relay(*gather, via=[0])
        _gather_finish(*gather)

    return pl.pallas_call(
        body, name="prepare_weights",
        out_shape=[jax.ShapeDtypeStruct(w_in.shape, BF16), jax.ShapeDtypeStruct(w_out.shape, BF16),
                   jax.ShapeDtypeStruct(w_up_t.shape[::-1], BF16), jax.ShapeDtypeStruct(w_down.shape, BF16),
                   jax.ShapeDtypeStruct((N_DEV,) + convp.shape, F32)],
        in_specs=[VMEM] * 5, out_specs=[VMEM] * 4 + [ANY],
        scratch_shapes=_gather_scratch(1),
        compiler_params=pltpu.CompilerParams(vmem_limit_bytes=VMEM_LIMIT),
    )(w_in, w_out, w_up_t, w_down, convp)


def _chip_copies(p, land, send_sems, recv_sems):
    x, y, c = _mesh_pos()
    return [pltpu.make_async_remote_copy(
        src_ref=p[a].at[k], dst_ref=land[a].at[k], send_sem=send_sems.at[a, k], recv_sem=recv_sems.at[a, k],
        device_id=(px, py, c), device_id_type=MESH)
        for k, (px, py) in enumerate(_chip_patterns(x, y)[1:]) for a in range(len(p))]


def chip_partials(gs, lands, jidx, steps):
    n = len(gs)
    blocks = [(1, g.shape[1] // steps, g.shape[2]) for g in gs]

    def body(j_ref, *refs):
        for a in range(n):
            refs[2 * n + a][...] = (refs[a][...] + refs[n + a][...]).astype(BF16)

    return pl.pallas_call(
        body, name="chip_partials",
        out_shape=[jax.ShapeDtypeStruct((3,) + g.shape[1:], BF16) for g in gs],
        grid_spec=pltpu.PrefetchScalarGridSpec(
            num_scalar_prefetch=1, grid=(3, steps),
            in_specs=[pl.BlockSpec(b, lambda k, i, j: (j[1 + k], i, 0)) for b in blocks]
            + [pl.BlockSpec(b, lambda k, i, j: (1 + k, i, 0)) for b in blocks],
            out_specs=[pl.BlockSpec(b, lambda k, i, j: (k, i, 0)) for b in blocks]),
        compiler_params=_params(("arbitrary", "arbitrary")),
    )(jidx, *gs, *lands)


def _adamw(w, g, m, v):
    m2 = ADAM_B1 * m + (1.0 - ADAM_B1) * g
    v2 = ADAM_B2 * v + (1.0 - ADAM_B2) * (g * g)
    m_hat = m2 / (1.0 - ADAM_B1 ** ADAM_STEP)
    v_hat = v2 / (1.0 - ADAM_B2 ** ADAM_STEP)
    delta = -ADAM_LR * (m_hat / (jnp.sqrt(v_hat) + ADAM_EPS) + ADAM_WD * w)
    return delta, m2, v2


def reduce_and_adamw(gs, lands, recvs, ws, ms, vs, jidx, steps):
    n = len(gs)
    rbs = [g.shape[1] // steps for g in gs]

    def body(j_ref, *refs):
        g, land, recv, w, m, v = (refs[k * n:(k + 1) * n] for k in range(6))
        outs = refs[6 * n:]
        for a in range(n):
            grad = ((g[a][0] + land[a][0]) + recv[a][0].astype(F32) + recv[a][1].astype(F32)
                    + recv[a][2].astype(F32))
            delta, m2, v2 = _adamw(w[a][...], grad, m[a][...], v[a][...])
            outs[a][...] = grad
            outs[n + a][...] = delta
            outs[2 * n + a][...] = m2
            outs[3 * n + a][...] = v2

    blk = [pl.BlockSpec((rb, g.shape[2]), lambda i, j: (i, 0)) for g, rb in zip(gs, rbs)]
    part = lambda lead, pick: [pl.BlockSpec((lead, rb, g.shape[2]), pick) for g, rb in zip(gs, rbs)]
    return pl.pallas_call(
        body, name="reduce_adamw",
        out_shape=[jax.ShapeDtypeStruct(g.shape[1:], F32) for g in gs] * 4,
        grid_spec=pltpu.PrefetchScalarGridSpec(
            num_scalar_prefetch=1, grid=(steps,),
            in_specs=part(1, lambda i, j: (j[0], i, 0)) + part(1, lambda i, j: (0, i, 0))
            + part(3, lambda i, j: (0, i, 0)) + blk * 3,
            out_specs=blk * 4),
        compiler_params=_params(("arbitrary",)),
    )(jidx, *gs, *lands, *recvs, *ws, *ms, *vs)


def mixer_reduce(grads, sv_parts):
    n = len(grads)
    shard = [g.shape[1:] for g in grads]
    ns = len(sv_parts)
    sv_shape = (sum(p.shape[0] for p in sv_parts), 128)

    def body(*refs):
        g = refs[:n]
        parts = refs[n:n + ns]
        outs = refs[n + ns:2 * n + ns]
        sv_slots = refs[2 * n + ns]
        rest = refs[2 * n + ns + 1:-1]
        sv_ref = refs[-1]
        row0 = 0
        for part in parts:
            sv_ref[row0:row0 + part.shape[0], :] = part[...]
            row0 += part.shape[0]
        own, land, sendb, recvb = rest[:n], rest[n:2 * n], rest[2 * n:3 * n], rest[3 * n:4 * n]
        sv_land, chip_sv, d2d_send, d2d_recv, ici_send, ici_recv, local_sems, sv_sems = rest[4 * n:]
        x, y, c = _mesh_pos()
        sib = (x, y, 1 - c)
        pats = _chip_patterns(x, y)
        q = 2 * x + y

        d2d, local = {}, {}
        for a in range(n):
            for k, (px, py) in enumerate(pats):
                d2d[a, k] = pltpu.make_async_remote_copy(
                    src_ref=g[a].at[_lid(px, py, 1 - c)], dst_ref=land[a].at[k],
                    send_sem=d2d_send.at[a, k], recv_sem=d2d_recv.at[a, k], device_id=sib, device_id_type=MESH)
                local[a, k] = pltpu.make_async_copy(g[a].at[_lid(px, py, c)], own[a].at[k], local_sems.at[a, k])
        sv_d2d = pltpu.make_async_remote_copy(
            src_ref=sv_ref, dst_ref=sv_land, send_sem=d2d_send.at[n, 0], recv_sem=d2d_recv.at[n, 0],
            device_id=sib, device_id_type=MESH)
        blocks = [(a, k) for a in range(n) for k in (1, 2, 3)] + [(a, 0) for a in range(n)]
        sv_d2d.start()
        for b in blocks:
            d2d[b].start()
            local[b].start()

        half_rows = sv_shape[0] // 2
        rows = pl.ds(pl.multiple_of(c * half_rows, 8), half_rows)
        sv_local = pltpu.make_async_copy(chip_sv, sv_slots.at[q], sv_sems.at[0])

        def sv_ici(k, slot, to):
            return pltpu.make_async_remote_copy(
                src_ref=chip_sv.at[rows], dst_ref=sv_slots.at[slot, rows], send_sem=sv_sems.at[1 + k],
                recv_sem=sv_sems.at[4 + k], device_id=to, device_id_type=MESH)

        def sv_pass_on(k, slot):
            return pltpu.make_async_remote_copy(
                src_ref=sv_slots.at[slot, rows], dst_ref=sv_slots.at[slot, rows], send_sem=sv_sems.at[7 + k],
                recv_sem=sv_sems.at[10 + k], device_id=sib, device_id_type=MESH)

        sv_d2d.wait()
        chip_sv[...] = sv_ref[...] + sv_land[...]
        sv_out = [sv_ici(k, q, (px, py, c)) for k, (px, py) in enumerate(pats[1:])]
        for cp in sv_out + [sv_local]:
            cp.start()

        ici = _chip_copies(sendb, recvb, ici_send, ici_recv)
        for a, k in blocks:
            local[a, k].wait()
            d2d[a, k].wait()
            if k > 0:
                sendb[a][k - 1] = (own[a][k] + land[a][k]).astype(BF16)
                ici[(k - 1) * n + a].start()
        for k, (px, py) in enumerate(pats[1:]):
            sv_out[k].wait_send()
            sv_ici(k, 2 * px + py, (px, py, c)).wait_recv()
            sv_pass_on(k, 2 * px + py).start()
        for a in range(n):
            for k in range(3):
                ici[k * n + a].wait()
            outs[a][...] = ((own[a][0] + land[a][0]) + recvb[a][0].astype(F32) + recvb[a][1].astype(F32)
                            + recvb[a][2].astype(F32))
        for k, (px, py) in enumerate(pats[1:]):
            sv_pass_on(k, 2 * px + py).wait()
        sv_local.wait()

    shard_out = [jax.ShapeDtypeStruct(s, F32) for s in shard]
    return pl.pallas_call(
        body, name="mixer_reduce",
        out_shape=shard_out + [jax.ShapeDtypeStruct((4,) + sv_shape, F32)],
        in_specs=[ANY] * n + [VMEM] * ns, out_specs=[VMEM] * n + [ANY],
        scratch_shapes=[pltpu.VMEM((4,) + s, F32) for s in shard] + [pltpu.VMEM((4,) + s, F32) for s in shard]
        + [pltpu.VMEM((3,) + s, BF16) for s in shard] + [pltpu.VMEM((3,) + s, BF16) for s in shard]
        + [pltpu.VMEM(sv_shape, F32), pltpu.VMEM(sv_shape, F32),
           pltpu.SemaphoreType.DMA((n + 1, 4)), pltpu.SemaphoreType.DMA((n + 1, 4)),
           pltpu.SemaphoreType.DMA((n, 3)), pltpu.SemaphoreType.DMA((n, 3)),
           pltpu.SemaphoreType.DMA((n, 4)), pltpu.SemaphoreType.DMA((13,)), pltpu.VMEM(sv_shape, F32)],
        compiler_params=pltpu.CompilerParams(vmem_limit_bytes=VMEM_LIMIT),
    )(*grads, *sv_parts)


SMALL_LAYOUT = [
    ("b_in", S_BIN, 16), ("ln_a_g", S_LNAG, 4), ("ln_a_b", S_LNAB, 4), ("w_spatial", S_WS, 512),
    ("b_spatial", S_BS, 4), ("conv_b_b", S_CBB, 4), ("ln_b_g", S_LNBG, 4), ("ln_b_b", S_LNBB, 4),
    ("b_out", S_BOUT, 8), ("ln1_g", S_LN1G, 8), ("ln1_b", S_LN1B, 8), ("conv_f_b", S_CFB, 44),
    ("ln2_g", S_LN2G, 8), ("ln2_b", S_LN2B, 8),
]


def small_adamw(sv_slots, ws, ms, vs, shard_grads, shard_ws, shard_ms, shard_vs):
    n = len(SMALL_LAYOUT)
    nb = len(shard_grads)

    def body(*refs):
        s_ref = refs[0]
        w_refs, m_refs, v_refs = refs[1:1 + n], refs[1 + n:1 + 2 * n], refs[1 + 2 * n:1 + 3 * n]
        big_in = refs[1 + 3 * n:1 + 3 * n + 4 * nb]
        outs = refs[1 + 3 * n + 4 * nb:]
        big_out = outs[4 * n + 1:]
        for p in range(nb):
            grad = big_in[p][...]
            delta, m2, v2 = _adamw(big_in[nb + p][...], grad, big_in[2 * nb + p][...], big_in[3 * nb + p][...])
            big_out[p][...] = grad
            big_out[nb + p][...] = delta
            big_out[2 * nb + p][...] = m2
            big_out[3 * nb + p][...] = v2
        for p, (_, row0, rows) in enumerate(SMALL_LAYOUT):
            sl = pl.ds(row0, rows)
            grad = ((s_ref[0, sl, :] + s_ref[1, sl, :]) + s_ref[2, sl, :]) + s_ref[3, sl, :]
            delta, m2, v2 = _adamw(w_refs[p][...], grad, m_refs[p][...], v_refs[p][...])
            outs[p][...] = grad
            outs[n + p][...] = delta
            outs[2 * n + p][...] = m2
            outs[3 * n + p][...] = v2
        sl = pl.ds(S_LOSS, 8)
        outs[4 * n][...] = ((s_ref[0, sl, :] + s_ref[1, sl, :]) + s_ref[2, sl, :]) + s_ref[3, sl, :]

    shapes = [jax.ShapeDtypeStruct((rows, 128), F32) for _, _, rows in SMALL_LAYOUT]
    big_shapes = [jax.ShapeDtypeStruct(g.shape, F32) for g in shard_grads]
    return pl.pallas_call(
        body, name="small_adamw",
        out_shape=shapes * 4 + [jax.ShapeDtypeStruct((8, 128), F32)] + big_shapes * 4,
        in_specs=[VMEM] * (1 + 3 * n + 4 * nb), out_specs=[VMEM] * (4 * n + 1 + 4 * nb),
        compiler_params=pltpu.CompilerParams(vmem_limit_bytes=VMEM_LIMIT),
    )(sv_slots, *ws, *ms, *vs, *shard_grads, *shard_ws, *shard_ms, *shard_vs)


def mix_forward(x, sin, sout, b_in, ln_a_g, ln_a_b, w_spatial, bst, conv_b_w, conv_b_b, ln_b_g, ln_b_b,
                b_out, ln1_g, ln1_b, sup, sdown, tm):
    t = x.shape[0]
    nt = t // tm
    n_chunks = tm // CHUNK

    def body(x_ref, sin_ref, sout_ref, bin_ref, ga_ref, ba_ref, ws_ref, bst_ref, cw_ref, cb_ref, gb_ref,
             bb_ref, bout_ref, g1_ref, b1_ref, sup_ref, sdown_ref,
             h_ref, xhat1_ref, rstd1_ref, yb1_ref, gin_ref, gout_ref, gup_ref, gdown_ref,
             ext_ref, y_ref, wsm_ref, win_ref, wout_ref, load_sems,
             mix_send, mix_recv, mix_local, send_sems, recv_sems, local_sems):
        i = pl.program_id(0)
        mixer = ([sin_ref, sout_ref], [gin_ref, gout_ref], mix_send, mix_recv, mix_local)
        gather = ([sup_ref, sdown_ref], [gup_ref, gdown_ref], send_sems, recv_sems, local_sems)

        @pl.when(i == 0)
        def _():
            _gather_start(*mixer)
            _gather_relay(*mixer, via=[2, 2])
            _gather_finish(*mixer)
            _gather_start(*gather)
            loads = [pltpu.make_async_copy(gin_ref, win_ref, load_sems.at[0]),
                     pltpu.make_async_copy(gout_ref, wout_ref, load_sems.at[1])]
            for cp in loads:
                cp.start()
            for cp in loads:
                cp.wait()
            ext_ref[0:HALO_B, :] = jnp.zeros((HALO_B, D_B), F32)
            mask = _tril_mask()
            for hd in range(HEADS):
                wsm_ref[hd] = jnp.where(mask, ws_ref[hd], 0.0).astype(BF16)

        xb = x_ref[...].astype(BF16)
        for j in range(N_DEV):
            cols = slice(j * W_IN_BLK, (j + 1) * W_IN_BLK)
            h_ref[:, cols] = _nn(xb, win_ref[j]) + bin_ref[:, cols]

        def chunk(ci):
            r = _rows(ci, CHUNK)
            for hd in range(HEADS):
                _, _, u, _, _, _, _, _, sv = _mixer_a_head(h_ref, r, hd, ga_ref, ba_ref, wsm_ref, bst_ref)
                y_ref[r, hd * HEAD_DIM:(hd + 1) * HEAD_DIM] = (u * sv).astype(BF16)
            a_b = h_ref[r, 2 * D_A:2 * D_A + D_B]
            g_b = h_ref[r, 2 * D_A + D_B:D_IN]
            ext_ref[pl.ds(HALO_B + ci * CHUNK, CHUNK), :] = a_b * _sigmoid(g_b)

        _loop(n_chunks, chunk)

        def conv_rows(bi):
            base = bi * ROWS
            yb1 = _conv_b_block(ext_ref, base, cw_ref) + cb_ref[...]
            yb1_ref[pl.ds(base, ROWS), :] = yb1
            xhat, _ = _ln_stats(yb1)
            yb2 = xhat * gb_ref[...] + bb_ref[...]
            y_ref[pl.ds(base, ROWS), D_A:D] = (yb2 * _sigmoid(yb2)).astype(BF16)

        _loop(tm // ROWS, conv_rows)
        ext_ref[0:HALO_B, :] = ext_ref[tm:tm + HALO_B, :]

        mix = _nn(y_ref[...], wout_ref[...].reshape(D, D)) + bout_ref[...]
        xhat1, rstd1 = _ln_stats(ALPHA * x_ref[...] + mix)
        xhat1_ref[...] = xhat1
        rstd1_ref[...] = jnp.broadcast_to(rstd1, (tm, 128))

        @pl.when(i == (5 * nt) // 8)
        def _():
            _gather_relay(*gather, via=[1, 2])

        @pl.when(i == nt - 1)
        def _():
            _gather_finish(*gather)

    row = lambda w: pl.BlockSpec((tm, w), lambda i: (i, 0))
    return pl.pallas_call(
        body, name="mix_forward", grid=(nt,),
        in_specs=[row(D), ANY, ANY, _full(b_in.shape), _full(ln_a_g.shape),
                  _full(ln_a_b.shape), _full(w_spatial.shape), _full(bst.shape),
                  _full(conv_b_w.shape), _full(conv_b_b.shape), _full(ln_b_g.shape),
                  _full(ln_b_b.shape), _full(b_out.shape),
                  _full(ln1_g.shape), _full(ln1_b.shape), ANY, ANY],
        out_specs=[row(D_IN), row(D), row(128), row(D_B), ANY, ANY, ANY, ANY],
        out_shape=[jax.ShapeDtypeStruct((t, D_IN), F32), jax.ShapeDtypeStruct((t, D), F32),
                   jax.ShapeDtypeStruct((t, 128), F32), jax.ShapeDtypeStruct((t, D_B), F32)]
        + [jax.ShapeDtypeStruct((N_DEV,) + sh.shape, BF16) for sh in (sin, sout, sup, sdown)],
        scratch_shapes=[pltpu.VMEM((tm + HALO_B, D_B), F32), pltpu.VMEM((tm, D), BF16),
                        pltpu.VMEM((HEADS, CHUNK, CHUNK), BF16),
                        pltpu.VMEM((N_DEV,) + sin.shape, BF16), pltpu.VMEM((N_DEV,) + sout.shape, BF16),
                        pltpu.SemaphoreType.DMA((2,))] + _gather_scratch(2) + _gather_scratch(2),
        compiler_params=_params(("arbitrary",)),
    )(x, sin, sout, b_in, ln_a_g, ln_a_b, w_spatial, bst, conv_b_w, conv_b_b, ln_b_g, ln_b_b,
      b_out, ln1_g, ln1_b, sup, sdown)


def ffn_forward(xhat1, ln1_g, ln1_b, wup_g, cfw, cfb, wdown, ln2_g, ln2_b, target, tm):
    t = xhat1.shape[0]
    nt = t // tm

    def body(xh_ref, g1_ref, b1_ref, wup_ref, cfw_ref, cfb_ref, wdown_ref, g2_ref, b2_ref, tgt_ref,
             hu_ref, gv_ref, dr2_ref, loss_ref, sln2_ref,
             x1_ref, x1b_ref, hu32_ref, carry_ref, gbuf_ref, ffn_ref, acc_loss, acc_g2, acc_b2):
        i = pl.program_id(0)

        @pl.when(i == 0)
        def _():
            carry_ref[...] = jnp.zeros(carry_ref.shape, F32)
            acc_loss[...] = jnp.zeros(acc_loss.shape, F32)
            acc_g2[...] = jnp.zeros(acc_g2.shape, F32)
            acc_b2[...] = jnp.zeros(acc_b2.shape, F32)

        x1 = xh_ref[...] * g1_ref[...] + b1_ref[...]
        x1_ref[...] = x1
        x1b_ref[...] = x1.astype(BF16)

        def conv(g, j, base):
            if base == 0:
                win = jnp.concatenate([carry_ref[j], hu32_ref[g, 0:ROWS, :]], axis=0)
            else:
                win = hu32_ref[g, base - HALO_F:base + ROWS, :]
            taps = _taps_f(win)
            w = cfw_ref[j]
            return sum(taps[k] * w[k:k + 1, :] for k in range(KF)) + cfb_ref[j:j + 1, :]

        for f in range(N_F):
            hu32_ref[0] = _nn(x1b_ref[...], wup_ref[f])
            hu32_ref[1] = _nn(x1b_ref[...], wup_ref[N_F + f])

            def rows(bi, f=f):
                r = _rows(bi)
                gate = conv(0, f, bi * ROWS)
                val = conv(1, N_F + f, bi * ROWS)
                gbuf_ref[r, :] = (gate * _sigmoid(gate) * val).astype(BF16)
                gv_ref[f, r, :] = gate.astype(BF16)
                gv_ref[N_F + f, r, :] = val.astype(BF16)
                hu_ref[f, r, :] = hu32_ref[0, r, :].astype(BF16)
                hu_ref[N_F + f, r, :] = hu32_ref[1, r, :].astype(BF16)

            _loop(tm // ROWS, rows)
            carry_ref[f] = hu32_ref[0, tm - HALO_F:tm, :]
            carry_ref[N_F + f] = hu32_ref[1, tm - HALO_F:tm, :]
            part = _nn(gbuf_ref[...], wdown_ref[f])
            if f == 0:
                ffn_ref[...] = part
            else:
                ffn_ref[...] += part

        def tail(bi):
            r = _rows(bi, LN_ROWS)
            xhat2, rstd2 = _ln_stats(ALPHA * x1_ref[r, :] + ffn_ref[r, :])
            err = xhat2 * g2_ref[...] + b2_ref[...] - tgt_ref[r, :]
            e2 = _rsum8(err * err)
            acc_loss[...] += sum(e2[:, k * 128:(k + 1) * 128] for k in range(D // 128))
            dy = err * (1.0 / D)
            acc_g2[...] += _rsum8(dy * xhat2)
            acc_b2[...] += _rsum8(dy)
            dr2_ref[r, :] = _ln_bwd(dy * g2_ref[...], xhat2, rstd2)

        _loop(tm // LN_ROWS, tail)
        loss_ref[...] = acc_loss[...]

        @pl.when(i == nt - 1)
        def _():
            dg = jnp.sum(acc_g2[...], axis=0, keepdims=True)
            db = jnp.sum(acc_b2[...], axis=0, keepdims=True)
            for k in range(D // 128):
                sln2_ref[k:k + 1, :] = dg[:, k * 128:(k + 1) * 128]
                sln2_ref[8 + k:9 + k, :] = db[:, k * 128:(k + 1) * 128]

    row = pl.BlockSpec((tm, D), lambda i: (i, 0))
    return pl.pallas_call(
        body, name="ffn_forward", grid=(nt,),
        in_specs=[row, _full(ln1_g.shape), _full(ln1_b.shape), _resident(wup_g.shape),
                  _full(cfw.shape), _full(cfb.shape), _resident(wdown.shape),
                  _full(ln2_g.shape), _full(ln2_b.shape), row],
        out_specs=[pl.BlockSpec((N_DEV, tm, W_UP_BLK), lambda i: (0, i, 0)),
                   pl.BlockSpec((N_DEV, tm, W_UP_BLK), lambda i: (0, i, 0)), row,
                   _full((8, 128)), _full((16, 128))],
        out_shape=[jax.ShapeDtypeStruct((N_DEV, t, W_UP_BLK), BF16),
                   jax.ShapeDtypeStruct((N_DEV, t, W_UP_BLK), BF16), jax.ShapeDtypeStruct((t, D), F32),
                   jax.ShapeDtypeStruct((8, 128), F32), jax.ShapeDtypeStruct((16, 128), F32)],
        scratch_shapes=[pltpu.VMEM((tm, D), F32), pltpu.VMEM((tm, D), BF16),
                        pltpu.VMEM((2, tm, W_UP_BLK), F32),
                        pltpu.VMEM((N_DEV, HALO_F, W_UP_BLK), F32), pltpu.VMEM((tm, W_UP_BLK), BF16),
                        pltpu.VMEM((tm, D), F32), pltpu.VMEM((8, 128), F32),
                        pltpu.VMEM((8, D), F32), pltpu.VMEM((8, D), F32)],
        compiler_params=_params(("arbitrary",)),
    )(xhat1, ln1_g, ln1_b, wup_g, cfw, cfb, wdown, ln2_g, ln2_b, target)


def ffn_backward(order, dr2, xhat1, ln1_g, ln1_b, hu, gv, wup_g, cfw, wdown, tm):
    t = dr2.shape[0]
    nt = t // tm
    sub_rows = tm
    hu4 = hu.reshape(2, N_F, t, W_UP_BLK)
    gv4 = gv.reshape(2, N_F, t, W_UP_BLK)
    wup4 = wup_g.reshape(2, N_F, D, W_UP_BLK)
    cfw4 = cfw.reshape(2, N_F, KF, W_UP_BLK)

    def body(order_ref, dr2_ref, xh_ref, g1_ref, b1_ref, hu_ref, gv_ref, wup_ref, cfw_ref, wdown_ref,
             dwup_ref, dwdown_ref, dcfw_ref, dcfb_ref, dx1_ref, land_up_ref, land_down_ref,
             x1b_ref, drb_ref, dg_ref, dextg_ref, dextv_ref, gbuf_ref,
             dhug_ref, dhuv_ref, acc_wup, acc_wdown, acc_cfw, acc_cfb, sem, send_sems, recv_sems):
        fo = pl.program_id(0)
        f = order_ref[fo]
        f_prev = order_ref[jnp.maximum(fo - 1, 0)]
        slot = fo % 2
        i = pl.program_id(1)
        x, y, c = _mesh_pos()
        half = D_FF // N_DEV

        def to_sibling(fi, k, src, land_ref, shard_chip):
            d = jnp.bitwise_xor(shard_chip, 2 * x + y)
            slot = jnp.where(d == 1, 2, jnp.where(d == 2, 1, d))
            return pltpu.make_async_remote_copy(
                src_ref=src, dst_ref=land_ref.at[slot], send_sem=send_sems.at[fi, k], recv_sem=recv_sems.at[fi, k],
                device_id=(x, y, 1 - c), device_id_type=MESH)

        def up_copy(fi, g):
            return to_sibling(fi, g, dwup_ref.at[g, fi], land_up_ref, 2 * g + fi // 2)

        def down_copy(fi):
            return to_sibling(fi, 2, dwdown_ref.at[fi, pl.ds((1 - c) * half, half)], land_down_ref, fi)

        def flush(fi, s):
            return [pltpu.make_async_copy(acc_wup.at[s, 0], dwup_ref.at[0, fi], sem.at[s, 0]),
                    pltpu.make_async_copy(acc_wup.at[s, 1], dwup_ref.at[1, fi], sem.at[s, 1]),
                    pltpu.make_async_copy(acc_wdown.at[s], dwdown_ref.at[fi], sem.at[s, 2])]

        def flushed(fi, s):
            for cp in flush(fi, s):
                cp.wait()
            down_copy(fi).start()

            @pl.when(fi % 2 != c)
            def _():
                up_copy(fi, 0).start()
                up_copy(fi, 1).start()

        @pl.when(i == 0)
        def _():
            acc_wup[slot] = jnp.zeros(acc_wup.shape[1:], F32)
            acc_wdown[slot] = jnp.zeros(acc_wdown.shape[1:], F32)
            acc_cfw[...] = jnp.zeros(acc_cfw.shape, F32)
            acc_cfb[...] = jnp.zeros(acc_cfb.shape, F32)
            dextg_ref[tm:tm + HALO_F, :] = jnp.zeros((HALO_F, W_UP_BLK), F32)
            dextv_ref[tm:tm + HALO_F, :] = jnp.zeros((HALO_F, W_UP_BLK), F32)

        w = [cfw_ref[0, 0], cfw_ref[1, 0]]
        dext = [dextg_ref, dextv_ref]
        dhu = [dhug_ref, dhuv_ref]

        def rows1(bi):
            r = _rows(bi)
            gate = gv_ref[0, 0, r, :].astype(F32)
            val = gv_ref[1, 0, r, :].astype(F32)
            sg = _sigmoid(gate)
            silu = gate * sg
            gbuf_ref[r, :] = (silu * val).astype(BF16)
            dg = dg_ref[r, :]
            dgate = dg * val * (sg * (1.0 + gate * (1.0 - sg)))
            dval = dg * silu
            dextg_ref[r, :] = dgate
            dextv_ref[r, :] = dval
            acc_cfb[0:8, :] += _rsum8(dgate)
            acc_cfb[8:16, :] += _rsum8(dval)

        def rows2(bi):
            r = _rows(bi)
            for g in range(2):
                win = dext[g][pl.ds(bi * ROWS, ROWS + HALO_F), :]
                n = ROWS + HALO_F
                later = [pltpu.roll(win, n - 2, 0)[0:ROWS, :], pltpu.roll(win, n - 1, 0)[0:ROWS, :],
                         win[0:ROWS, :]]
                d = sum(later[k] * w[g][k:k + 1, :] for k in range(KF))
                dhu[g][r, :] = d.astype(BF16)
                pre = hu_ref[g, 0, r, :].astype(F32)
                for k in range(KF):
                    r0 = 8 * (g * KF + k)
                    acc_cfw[r0:r0 + 8, :] += _rsum8(later[k] * pre)

        for sub in reversed(range(tm // sub_rows)):
            rs = slice(sub * sub_rows, (sub + 1) * sub_rows)
            blocks = range(sub * sub_rows // ROWS, (sub + 1) * sub_rows // ROWS)
            x1b_ref[rs, :] = (xh_ref[rs, :] * g1_ref[...] + b1_ref[...]).astype(BF16)
            drb_ref[rs, :] = dr2_ref[rs, :].astype(BF16)
            dg_ref[rs, :] = _nt(drb_ref[rs, :], wdown_ref[0])
            for bi in blocks:
                rows1(bi)
            for bi in blocks:
                rows2(bi)
            acc_wdown[slot] += _tn(gbuf_ref[rs, :], drb_ref[rs, :])
            acc_wup[slot, 0] += _tn(dhug_ref[rs, :], x1b_ref[rs, :])
            acc_wup[slot, 1] += _tn(dhuv_ref[rs, :], x1b_ref[rs, :])
            dx1_ref[0, rs, :] = (_nt(dhug_ref[rs, :], wup_ref[0, 0])
                                 + _nt(dhuv_ref[rs, :], wup_ref[1, 0])).astype(BF16)
        dextg_ref[tm:tm + HALO_F, :] = dextg_ref[0:HALO_F, :]
        dextv_ref[tm:tm + HALO_F, :] = dextv_ref[0:HALO_F, :]

        @pl.when(i == nt - 1)
        def _():
            for g in range(2):
                dcfb_ref[g, 0] = jnp.sum(acc_cfb[8 * g:8 * g + 8, :], axis=0, keepdims=True)
                for k in range(KF):
                    r0 = 8 * (g * KF + k)
                    dcfw_ref[g, 0, k:k + 1, :] = jnp.sum(acc_cfw[r0:r0 + 8, :], axis=0, keepdims=True)
            for cp in flush(f, slot):
                cp.start()

        @pl.when((i == 0) & (fo > 0))
        def _():
            flushed(f_prev, 1 - slot)

        @pl.when((i == nt - 1) & (fo == N_F - 1))
        def _():
            flushed(f, slot)
            for fi in range(N_F):
                down_copy(fi).wait()
                for g in range(2):
                    @pl.when(fi % 2 != c)
                    def _():
                        up_copy(fi, g).wait_send()

                    @pl.when(fi % 2 == c)
                    def _():
                        up_copy(fi, g).wait_recv()

    rev = lambda i: nt - 1 - i
    row = pl.BlockSpec((tm, D), lambda fo, i, o: (rev(i), 0))
    pair = lambda r, c: pl.BlockSpec((2, 1, r, c), lambda fo, i, o: (0, o[fo], 0, 0))
    tile = pl.BlockSpec((2, 1, tm, W_UP_BLK), lambda fo, i, o: (0, o[fo], rev(i), 0))
    return pl.pallas_call(
        body, name="ffn_backward",
        grid_spec=pltpu.PrefetchScalarGridSpec(
            num_scalar_prefetch=1, grid=(N_F, nt),
            in_specs=[row, row, _full(ln1_g.shape), _full(ln1_b.shape), tile, tile,
                      pair(D, W_UP_BLK), pair(KF, W_UP_BLK),
                      pl.BlockSpec((1, W_UP_BLK, D), lambda fo, i, o: (o[fo], 0, 0))],
            out_specs=[ANY, ANY, pair(KF, W_UP_BLK), pair(1, W_UP_BLK),
                       pl.BlockSpec((1, tm, D), lambda fo, i, o: (o[fo], rev(i), 0)), ANY, ANY],
            scratch_shapes=[pltpu.VMEM((tm, D), BF16), pltpu.VMEM((tm, D), BF16),
                            pltpu.VMEM((tm, W_UP_BLK), F32),
                            pltpu.VMEM((tm + HALO_F, W_UP_BLK), F32), pltpu.VMEM((tm + HALO_F, W_UP_BLK), F32),
                            pltpu.VMEM((tm, W_UP_BLK), BF16), pltpu.VMEM((tm, W_UP_BLK), BF16),
                            pltpu.VMEM((tm, W_UP_BLK), BF16),
                            pltpu.VMEM((2, 2, W_UP_BLK, D), F32), pltpu.VMEM((2, W_UP_BLK, D), F32),
                            pltpu.VMEM((2 * KF * 8, W_UP_BLK), F32), pltpu.VMEM((16, W_UP_BLK), F32),
                            pltpu.SemaphoreType.DMA((2, 3)),
                            pltpu.SemaphoreType.DMA((N_F, 3)), pltpu.SemaphoreType.DMA((N_F, 3))]),
        out_shape=[jax.ShapeDtypeStruct((2, N_F, W_UP_BLK, D), F32),
                   jax.ShapeDtypeStruct((N_F, W_UP_BLK, D), F32),
                   jax.ShapeDtypeStruct((2, N_F, KF, W_UP_BLK), F32),
                   jax.ShapeDtypeStruct((2, N_F, 1, W_UP_BLK), F32),
                   jax.ShapeDtypeStruct((N_F, t, D), BF16),
                   jax.ShapeDtypeStruct((4, W_UP_BLK, D), F32),
                   jax.ShapeDtypeStruct((4, D_FF // N_DEV, D), F32)],
        compiler_params=_params(("arbitrary", "arbitrary")),
    )(order, dr2, xhat1, ln1_g, ln1_b, hu4, gv4, wup4, cfw4, wdown)


def mix_backward(x, h, yb1, dx1p, dr2, xhat1, rstd1, win_g, ln_a_g, ln_a_b, w_spatial, bst,
                 conv_b_w, ln_b_g, ln_b_b, wout, ln1_g, ffn_partials, tm):
    t = x.shape[0]
    n_p = len(ffn_partials)
    nt = t // tm
    n_chunks = tm // CHUNK
    halo_blocks = tm // HALO_B

    def body(x_ref, h_ref, halo_ref, yb1_ref, dx1p_ref, dr2_ref, xh1_ref, rstd1_ref, win_ref, ga_ref, ba_ref,
             ws_ref, bst_ref, cw_ref, gb_ref, bb_ref, wout_ref, g1_ref, *rest):
        p_refs, rest = rest[:n_p], rest[n_p:]
        gx_ref, dwin_ref, dwout_ref, dcw_ref, small_ref = rest[:5]
        land_refs, rest = rest[5:5 + n_p], rest[5 + n_p:]
        (ext_ref, dext_ref, y_ref, dy_ref, dh_ref, dmb_ref, wsm_ref,
         acc_win, acc_wout, acc_bin, acc_lnag, acc_lnab, acc_ws, acc_bs, acc_cbb, acc_lnbg,
         acc_lnbb, acc_bout, acc_ln1g, acc_ln1b, acc_cw, sem, send_sems, recv_sems) = rest
        i = pl.program_id(0)

        @pl.when(i == 0)
        def _():
            for cp in _chip_copies(p_refs, land_refs, send_sems, recv_sems):
                cp.start()

        first_tile = i == nt - 1
        accs = [acc_win, acc_wout, acc_bin, acc_lnag, acc_lnab, acc_ws, acc_bs, acc_cbb, acc_lnbg,
                acc_lnbb, acc_bout, acc_ln1g, acc_ln1b, acc_cw]

        @pl.when(i == 0)
        def _():
            for acc in accs:
                acc[...] = jnp.zeros(acc.shape, F32)
            dext_ref[tm:tm + HALO_B, :] = jnp.zeros((HALO_B, D_B), F32)
            mask = _tril_mask()
            for hd in range(HEADS):
                wsm_ref[hd] = jnp.where(mask, ws_ref[hd], 0.0).astype(BF16)

        def ln1_rows(bi):
            r = _rows(bi, LN_ROWS)
            part = [dx1p_ref[f, r, :].astype(F32) for f in range(N_F)]
            dx1 = ALPHA * dr2_ref[r, :] + ((part[0] + part[1]) + (part[2] + part[3]))
            xhat = xh1_ref[r, :]
            acc_ln1g[...] += _rsum8(dx1 * xhat)
            acc_ln1b[...] += _rsum8(dx1)
            dr1 = _ln_bwd(dx1 * g1_ref[...], xhat, rstd1_ref[r, 0:1])
            acc_bout[...] += _rsum8(dr1)
            gx_ref[r, :] = ALPHA * dr1
            dmb_ref[r, :] = dr1.astype(BF16)

        _loop(tm // LN_ROWS, ln1_rows)
        dy_ref[...] = _nt(dmb_ref[...], wout_ref[...])

        ha = halo_ref[:, 0:D_B]
        hg = halo_ref[:, D_B:2 * D_B]
        ext_ref[0:HALO_B, :] = jnp.where(first_tile, 0.0, 1.0) * (ha * _sigmoid(hg))

        def chunk(ci):
            r = _rows(ci, CHUNK)
            for hd in range(HEADS):
                sl = slice(hd * HEAD_DIM, (hd + 1) * HEAD_DIM)
                rows8 = slice(8 * hd, 8 * hd + 8)
                hus, hvs, u, cdf_u, cdf_v, xhat, rstd, vn, sv = _mixer_a_head(
                    h_ref, r, hd, ga_ref, ba_ref, wsm_ref, bst_ref)
                dy_a = dy_ref[r, sl]
                y_ref[r, sl] = (u * sv).astype(BF16)
                du = dy_a * sv
                dsv = dy_a * u
                dsvb = dsv.astype(BF16)
                acc_bs[hd] += dsv
                acc_ws[hd] += _nt(dsvb, vn)
                dvn = _tn(wsm_ref[hd], dsvb)
                acc_lnag[rows8, :] += _rsum8(dvn * xhat)
                acc_lnab[rows8, :] += _rsum8(dvn)
                dv = _ln_bwd(dvn * ga_ref[hd:hd + 1, :], xhat, rstd)
                slv = slice(D_A + hd * HEAD_DIM, D_A + (hd + 1) * HEAD_DIM)
                dhu = du * (cdf_u + hus * jnp.exp(-0.5 * hus * hus) * INV_SQRT_2PI)
                dhv = dv * (cdf_v + hvs * jnp.exp(-0.5 * hvs * hvs) * INV_SQRT_2PI)
                acc_bin[:, sl] += _rsum8(dhu)
                acc_bin[:, slv] += _rsum8(dhv)
                dh_ref[r, sl] = dhu.astype(BF16)
                dh_ref[r, slv] = dhv.astype(BF16)
            a_b = h_ref[r, 2 * D_A:2 * D_A + D_B]
            g_b = h_ref[r, 2 * D_A + D_B:D_IN]
            ext_ref[pl.ds(HALO_B + ci * CHUNK, CHUNK), :] = a_b * _sigmoid(g_b)

        _loop(n_chunks, chunk)

        def conv_rows(bi):
            base = bi * ROWS
            r = pl.ds(base, ROWS)
            xhat, rstd = _ln_stats(yb1_ref[r, :])
            yb2 = xhat * gb_ref[...] + bb_ref[...]
            sg = _sigmoid(yb2)
            y_ref[r, D_A:D] = (yb2 * sg).astype(BF16)
            dyb2 = dy_ref[r, D_A:D] * (sg * (1.0 + yb2 * (1.0 - sg)))
            acc_lnbg[...] += _rsum8(dyb2 * xhat)
            acc_lnbb[...] += _rsum8(dyb2)
            dyb1 = _ln_bwd(dyb2 * gb_ref[...], xhat, rstd)
            acc_cbb[...] += _rsum8(dyb1)
            dext_ref[r, :] = dyb1
            for k, tap in _taps(ext_ref[pl.ds(base, ROWS + HALO_B), :], CONV_B_OFFSETS):
                acc_cw[8 * k:8 * k + 8, :] += _rsum8(dyb1 * tap)

        _loop(tm // ROWS, conv_rows)

        def convt_rows(bi):
            base = bi * ROWS
            r = pl.ds(base, ROWS)
            dyb0 = jnp.zeros((ROWS, D_B), F32)
            for k, tap in _taps(dext_ref[pl.ds(base, ROWS + HALO_B), :], CONV_B_T_OFFSETS):
                dyb0 = dyb0 + tap * cw_ref[k:k + 1, :]
            a_b = h_ref[r, 2 * D_A:2 * D_A + D_B]
            sg = _sigmoid(h_ref[r, 2 * D_A + D_B:D_IN])
            da_b = dyb0 * sg
            dg_b = dyb0 * a_b * sg * (1.0 - sg)
            acc_bin[:, 2 * D_A:2 * D_A + D_B] += _rsum8(da_b)
            acc_bin[:, 2 * D_A + D_B:D_IN] += _rsum8(dg_b)
            dh_ref[r, 2 * D_A:2 * D_A + D_B] = da_b.astype(BF16)
            dh_ref[r, 2 * D_A + D_B:D_IN] = dg_b.astype(BF16)

        _loop(tm // ROWS, convt_rows)
        dext_ref[tm:tm + HALO_B, :] = dext_ref[0:HALO_B, :]

        acc_wout[...] += _tn(y_ref[...], dmb_ref[...])
        xt = x_ref[...].T.astype(BF16)
        dh_blocks = [dh_ref[:, j * W_IN_BLK:(j + 1) * W_IN_BLK] for j in range(N_DEV)]
        for j in range(N_DEV):
            acc_win[j] += _nn(xt, dh_blocks[j])
        gx_ref[...] += sum(_nt(dh_blocks[j], win_ref[j]) for j in range(N_DEV))

        @pl.when(i == nt - 1)
        def _():
            cps = [pltpu.make_async_copy(acc_win, dwin_ref, sem.at[0]),
                   pltpu.make_async_copy(acc_wout, dwout_ref, sem.at[1])]
            for cp in cps:
                cp.start()
            small_ref[...] = jnp.zeros(small_ref.shape, F32)

            def put_row_vector(row0, acc):
                vec = jnp.sum(acc[...], axis=0, keepdims=True)
                for k in range(vec.shape[1] // 128):
                    small_ref[row0 + k:row0 + k + 1, :] = vec[:, k * 128:(k + 1) * 128]

            put_row_vector(S_BIN, acc_bin)
            put_row_vector(S_CBB, acc_cbb)
            put_row_vector(S_LNBG, acc_lnbg)
            put_row_vector(S_LNBB, acc_lnbb)
            put_row_vector(S_BOUT, acc_bout)
            put_row_vector(S_LN1G, acc_ln1g)
            put_row_vector(S_LN1B, acc_ln1b)
            mask = _tril_mask()
            for hd in range(HEADS):
                rows8 = slice(8 * hd, 8 * hd + 8)
                small_ref[S_LNAG + hd:S_LNAG + hd + 1, :] = jnp.sum(acc_lnag[rows8, :], axis=0, keepdims=True)
                small_ref[S_LNAB + hd:S_LNAB + hd + 1, :] = jnp.sum(acc_lnab[rows8, :], axis=0, keepdims=True)
                small_ref[S_WS + hd * CHUNK:S_WS + (hd + 1) * CHUNK, :] = jnp.where(mask, acc_ws[hd], 0.0)
                small_ref[S_BS + hd:S_BS + hd + 1, :] = jnp.sum(acc_bs[hd].T, axis=0, keepdims=True)
            for k in range(KB):
                dcw_ref[k:k + 1, :] = jnp.sum(acc_cw[8 * k:8 * k + 8, :], axis=0, keepdims=True)
            for cp in cps:
                cp.wait()
            for cp in _chip_copies(p_refs, land_refs, send_sems, recv_sems):
                cp.wait()

    rev = lambda i: nt - 1 - i
    row = lambda w: pl.BlockSpec((tm, w), lambda i: (rev(i), 0))
    return pl.pallas_call(
        body, name="mix_backward", grid=(nt,),
        in_specs=[row(D), row(D_IN),
                  pl.BlockSpec((HALO_B, 2 * D_B), lambda i: (jnp.maximum(rev(i) * halo_blocks - 1, 0), 1)),
                  row(D_B), pl.BlockSpec((N_F, tm, D), lambda i: (0, rev(i), 0)),
                  row(D), row(D), row(128), _resident(win_g.shape), _full(ln_a_g.shape),
                  _full(ln_a_b.shape), _full(w_spatial.shape), _full(bst.shape), _full(conv_b_w.shape),
                  _full(ln_b_g.shape), _full(ln_b_b.shape),
                  _resident(wout.shape), _full(ln1_g.shape)] + [ANY] * n_p,
        out_specs=[row(D), ANY, ANY, _full((KB, D_B)), _full((S_MIX_ROWS, 128))] + [ANY] * n_p,
        out_shape=[jax.ShapeDtypeStruct((t, D), F32), jax.ShapeDtypeStruct((N_DEV, D, W_IN_BLK), F32),
                   jax.ShapeDtypeStruct((D, D), F32), jax.ShapeDtypeStruct((KB, D_B), F32),
                   jax.ShapeDtypeStruct((S_MIX_ROWS, 128), F32)]
        + [jax.ShapeDtypeStruct(p.shape, BF16) for p in ffn_partials],
        scratch_shapes=[pltpu.VMEM((tm + HALO_B, D_B), F32), pltpu.VMEM((tm + HALO_B, D_B), F32),
                        pltpu.VMEM((tm, D), BF16), pltpu.VMEM((tm, D), F32), pltpu.VMEM((tm, D_IN), BF16),
                        pltpu.VMEM((tm, D), BF16),
                        pltpu.VMEM((HEADS, CHUNK, CHUNK), BF16),
                        pltpu.VMEM((N_DEV, D, W_IN_BLK), F32), pltpu.VMEM((D, D), F32),
                        pltpu.VMEM((8, D_IN), F32), pltpu.VMEM((8 * HEADS, HEAD_DIM), F32),
                        pltpu.VMEM((8 * HEADS, HEAD_DIM), F32), pltpu.VMEM((HEADS, CHUNK, CHUNK), F32),
                        pltpu.VMEM((HEADS, CHUNK, CHUNK), F32), pltpu.VMEM((8, D_B), F32),
                        pltpu.VMEM((8, D_B), F32), pltpu.VMEM((8, D_B), F32), pltpu.VMEM((8, D), F32),
                        pltpu.VMEM((8, D), F32), pltpu.VMEM((8, D), F32), pltpu.VMEM((8 * KB, D_B), F32),
                        pltpu.SemaphoreType.DMA((2,)),
                        pltpu.SemaphoreType.DMA((n_p, 3)), pltpu.SemaphoreType.DMA((n_p, 3))],
        compiler_params=_params(("arbitrary",)),
    )(x, h, h, yb1, dx1p, dr2, xhat1, rstd1, win_g, ln_a_g, ln_a_b, w_spatial, bst, conv_b_w,
      ln_b_g, ln_b_b, wout, ln1_g, *ffn_partials)


def _rows128(a):
    return a.reshape(-1, 128)


def _pack_conv(cb, cf):
    lead = cb.shape[:-2]
    pad = [(0, 0)] * len(lead)
    flat = jnp.pad(cb.reshape(lead + (KB * 64,)), pad + [(0, 3 * W_UP_BLK - KB * 64)])
    rows = jnp.concatenate([cf, flat.reshape(lead + (3, W_UP_BLK))], axis=-2)
    return jnp.pad(rows, pad + [(0, 2), (0, 768 - W_UP_BLK)])


def _unpack_conv(p):
    lead = p.shape[:-2]
    cf = p[..., 0:KF, 0:W_UP_BLK]
    cb = p[..., 3:6, 0:W_UP_BLK].reshape(lead + (3 * W_UP_BLK,))[..., :KB * 64].reshape(lead + (KB, 64))
    return cb, cf


def kernel(x, w_in, b_in, ln_a_g, ln_a_b, w_spatial, b_spatial, conv_b_w, conv_b_b, ln_b_g, ln_b_b, w_out, b_out, ln1_g, ln1_b, w_up, conv_f_w, conv_f_b, w_down, ln2_g, ln2_b, loss_target, m_w_in, m_b_in, m_ln_a_g, m_ln_a_b, m_w_spatial, m_b_spatial, m_conv_b_w, m_conv_b_b, m_ln_b_g, m_ln_b_b, m_w_out, m_b_out, m_ln1_g, m_ln1_b, m_w_up, m_conv_f_w, m_conv_f_b, m_w_down, m_ln2_g, m_ln2_b, v_w_in, v_b_in, v_ln_a_g, v_ln_a_b, v_w_spatial, v_b_spatial, v_conv_b_w, v_conv_b_b, v_ln_b_g, v_ln_b_b, v_w_out, v_b_out, v_ln1_g, v_ln1_b, v_w_up, v_conv_f_w, v_conv_f_b, v_w_down, v_ln2_g, v_ln2_b):
    t = x.shape[1]
    x2 = x.reshape(t, D)
    target = loss_target.reshape(t, D)
    tm_fwd = min(t, 512)
    tm_bwd = min(t, 256)
    tm_ffn_bwd = min(t, 512)

    xi, yi, ci = _mesh_pos()
    jidx = jnp.stack([_lid(px, py, ci) for px, py in _chip_patterns(xi, yi)]).astype(jnp.int32)

    sin, sout, sup, sdown, conv_g = prepare_weights(w_in, w_out, w_up.T, w_down, _pack_conv(conv_b_w, conv_f_w))
    conv_b_all, cfw = _unpack_conv(conv_g)
    conv_b_full = conv_b_all.transpose(1, 0, 2).reshape(KB, D_B)
    cfb = conv_f_b.reshape(N_DEV, W_UP_BLK)
    row = lambda a: a.reshape(1, -1)
    bst = b_spatial.T

    h, xhat1, rstd1, yb1, win_g, wout_g, wup_g, wdown_g = mix_forward(
        x2, sin, sout, row(b_in), ln_a_g, ln_a_b, w_spatial, bst, conv_b_full, row(conv_b_b),
        row(ln_b_g), row(ln_b_b), row(b_out), row(ln1_g), row(ln1_b), sup, sdown, tm_fwd)
    wout_full = wout_g.reshape(D, D)
    wdown4 = wdown_g.reshape(N_F, W_UP_BLK, D)
    hu, gv, dr2, loss_part, s_ln2 = ffn_forward(
        xhat1, row(ln1_g), row(ln1_b), wup_g, cfw, cfb, wdown4, row(ln2_g), row(ln2_b), target, tm_bwd)

    order = jnp.where(ci == 0, jnp.array([1, 3, 0, 2], jnp.int32), jnp.array([0, 2, 1, 3], jnp.int32))
    dwup, dwdown, dcfw, dcfb, dx1p, *ffn_lands = ffn_backward(
        order, dr2, xhat1, row(ln1_g), row(ln1_b), hu, gv, wup_g, cfw, wdown4, tm_ffn_bwd)
    ffn_grads = [dwup.reshape(N_DEV, W_UP_BLK, D), dwdown.reshape(N_DEV, D_FF // N_DEV, D)]
    ffn_partials = chip_partials(ffn_grads, ffn_lands, jidx, 2)
    grad_x, dwin, dwout, dcw, s_mix, *ffn_recvs = mix_backward(
        x2, h, yb1, dx1p, dr2, xhat1, rstd1, win_g, ln_a_g, ln_a_b, w_spatial, bst,
        conv_b_full, row(ln_b_g), row(ln_b_b), wout_full, row(ln1_g), ffn_partials, tm_bwd)

    dcfb_rows = jnp.pad(dcfb.reshape(-1, 128), ((0, 4), (0, 0)))
    dconv = _pack_conv(dcw.reshape(KB, N_DEV, 64).transpose(1, 0, 2), dcfw.reshape(N_DEV, KF, W_UP_BLK))
    mix_grads = [dwin, dwout.reshape(N_DEV, D // N_DEV, D), dconv]
    mix_w = [w_in, w_out, _pack_conv(conv_b_w, conv_f_w)]
    mix_m = [m_w_in, m_w_out, _pack_conv(m_conv_b_w, m_conv_f_w)]
    mix_v = [v_w_in, v_w_out, _pack_conv(v_conv_b_w, v_conv_f_w)]
    *mix_sums, sv_slots = mixer_reduce(mix_grads, [s_mix, dcfb_rows, s_ln2, loss_part])
    big = {}

    ffn_out = reduce_and_adamw(ffn_grads, ffn_lands, ffn_recvs, [w_up.T, w_down], [m_w_up.T, m_w_down],
                               [v_w_up.T, v_w_down], jidx, 2)
    big["w_up"] = [ffn_out[2 * k].T for k in range(4)]
    big["w_down"] = [ffn_out[2 * k + 1] for k in range(4)]

    small_w = dict(b_in=b_in, ln_a_g=ln_a_g, ln_a_b=ln_a_b, w_spatial=w_spatial, b_spatial=b_spatial,
                   conv_b_b=conv_b_b, ln_b_g=ln_b_g, ln_b_b=ln_b_b, b_out=b_out, ln1_g=ln1_g,
                   ln1_b=ln1_b, conv_f_b=conv_f_b, ln2_g=ln2_g, ln2_b=ln2_b)
    small_m = dict(b_in=m_b_in, ln_a_g=m_ln_a_g, ln_a_b=m_ln_a_b, w_spatial=m_w_spatial,
                   b_spatial=m_b_spatial, conv_b_b=m_conv_b_b, ln_b_g=m_ln_b_g, ln_b_b=m_ln_b_b,
                   b_out=m_b_out, ln1_g=m_ln1_g, ln1_b=m_ln1_b, conv_f_b=m_conv_f_b, ln2_g=m_ln2_g,
                   ln2_b=m_ln2_b)
    small_v = dict(b_in=v_b_in, ln_a_g=v_ln_a_g, ln_a_b=v_ln_a_b, w_spatial=v_w_spatial,
                   b_spatial=v_b_spatial, conv_b_b=v_conv_b_b, ln_b_g=v_ln_b_g, ln_b_b=v_ln_b_b,
                   b_out=v_b_out, ln1_g=v_ln1_g, ln1_b=v_ln1_b, conv_f_b=v_conv_f_b, ln2_g=v_ln2_g,
                   ln2_b=v_ln2_b)
    order = [nm for nm, _, _ in SMALL_LAYOUT]
    small_out = small_adamw(sv_slots, [_rows128(small_w[nm]) for nm in order],
                            [_rows128(small_m[nm]) for nm in order], [_rows128(small_v[nm]) for nm in order],
                            mix_sums, mix_w, mix_m, mix_v)
    n_small = len(order)
    mix_out = small_out[4 * n_small + 1:]
    big.update({nm: [mix_out[k * 3 + p] for k in range(4)] for p, nm in enumerate(["w_in", "w_out", "conv"])})
    for k in range(4):
        cb_k, cf_k = _unpack_conv(big["conv"][k])
        big.setdefault("conv_b_w", []).append(cb_k)
        big.setdefault("conv_f_w", []).append(cf_k)
    small = {nm: [small_out[k * n_small + p].reshape(small_w[nm].shape) for k in range(4)]
             for p, nm in enumerate(order)}
    loss = jnp.sum(small_out[4 * n_small]) * (0.5 / D)

    weights = ["w_in", "b_in", "ln_a_g", "ln_a_b", "w_spatial", "b_spatial", "conv_b_w", "conv_b_b",
               "ln_b_g", "ln_b_b", "w_out", "b_out", "ln1_g", "ln1_b", "w_up", "conv_f_w", "conv_f_b",
               "w_down", "ln2_g", "ln2_b"]
    result = lambda nm, k: big[nm][k] if nm in big else small[nm][k]
    return (loss, grad_x.reshape(x.shape), *[result(nm, 0) for nm in weights],
            *[result(nm, 1) for nm in weights], *[result(nm, 2) for nm in weights],
            *[result(nm, 3) for nm in weights])
```

```python
import functools
import math

import jax
import jax.numpy as jnp
from jax import lax
from jax.experimental import pallas as pl
from jax.experimental.pallas import tpu as pltpu

F32 = jnp.float32
BF16 = jnp.bfloat16

D = 1024
D_A = 512
D_B = 512
HEADS = 4
HEAD_DIM = 128
CHUNK = 128
KB = 31
KF = 3
D_FF = 2816
D_IN = 2048
N_DEV = 8
W_IN_BLK = D_IN // N_DEV
W_UP_BLK = 2 * D_FF // N_DEV
N_F = 4
LN_EPS = 1e-5
ALPHA = 2.0 ** 0.25

ADAM_LR = 0.001
ADAM_B1 = 0.9
ADAM_B2 = 0.999
ADAM_EPS = 1e-08
ADAM_WD = 0.01
ADAM_STEP = 10

INV_SQRT2 = 1.0 / math.sqrt(2.0)
INV_SQRT_2PI = 1.0 / math.sqrt(2.0 * math.pi)

HALO_B = 32
HALO_F = 8
ROWS = 64
LN_ROWS = 32
VMEM_LIMIT = 58 * 1024 * 1024

MESH = pl.DeviceIdType.MESH
ANY = pl.BlockSpec(memory_space=pl.ANY)
VMEM = pl.BlockSpec(memory_space=pltpu.VMEM)

S_BIN, S_LNAG, S_LNAB, S_WS, S_BS, S_CBB, S_LNBG, S_LNBB, S_BOUT, S_LN1G, S_LN1B = (
    0, 16, 24, 32, 544, 552, 560, 568, 576, 584, 592)
S_MIX_ROWS = 600
S_CFB = 600
S_LN2G = 648
S_LN2B = 656
S_LOSS = 664
S_ROWS = 672


def _tn(a, b):
    return lax.dot_general(a, b, (((0,), (0,)), ((), ())), preferred_element_type=F32)


def _nt(a, b):
    return lax.dot_general(a, b, (((1,), (1,)), ((), ())), preferred_element_type=F32)


def _nn(a, b):
    return jnp.dot(a, b, preferred_element_type=F32)


def _sigmoid(x):
    return 1.0 / (1.0 + jnp.exp(-x))


def _ln_stats(x):
    mu = jnp.mean(x, axis=-1, keepdims=True)
    xc = x - mu
    var = jnp.mean(xc * xc, axis=-1, keepdims=True)
    rstd = lax.rsqrt(var + LN_EPS)
    return xc * rstd, rstd


def _ln_bwd(dxhat, xhat, rstd):
    m1 = jnp.mean(dxhat, axis=-1, keepdims=True)
    m2 = jnp.mean(dxhat * xhat, axis=-1, keepdims=True)
    return rstd * (dxhat - m1 - xhat * m2)


def _rsum8(x):
    r, n = x.shape
    return x.reshape(r // 8, 8, n).sum(axis=0)


def _rows(i, n=ROWS):
    return pl.ds(i * n, n)


def _loop(n, body):
    for i in range(n):
        body(i)


def _tril_mask():
    r = lax.broadcasted_iota(jnp.int32, (CHUNK, CHUNK), 0)
    c = lax.broadcasted_iota(jnp.int32, (CHUNK, CHUNK), 1)
    return c <= r


def _mixer_a_head(h_ref, r, hd, ga_ref, ba_ref, wsm_ref, bst_ref):
    sl = slice(hd * HEAD_DIM, (hd + 1) * HEAD_DIM)
    hu = h_ref[r, sl]
    hv = h_ref[r, D_A + hd * HEAD_DIM:D_A + (hd + 1) * HEAD_DIM]
    cdf_u = 0.5 * (1.0 + lax.erf(hu * INV_SQRT2))
    cdf_v = 0.5 * (1.0 + lax.erf(hv * INV_SQRT2))
    u = hu * cdf_u
    xhat, rstd = _ln_stats(hv * cdf_v)
    vn = (xhat * ga_ref[hd:hd + 1, :] + ba_ref[hd:hd + 1, :]).astype(BF16)
    sv = _nn(wsm_ref[hd], vn) + bst_ref[:, hd:hd + 1]
    return hu, hv, u, cdf_u, cdf_v, xhat, rstd, vn, sv


def _taps(win, offsets):
    n = win.shape[0]
    for s in range(8):
        ks = [k for k, o in enumerate(offsets) if o % 8 == s]
        if ks:
            moved = win if s == 0 else pltpu.roll(win, n - s, 0)
            for k in ks:
                yield k, moved[offsets[k] - s:offsets[k] - s + ROWS, :]


CONV_B_OFFSETS = [2 + k for k in range(KB)]
CONV_B_T_OFFSETS = [30 - k for k in range(KB)]


def _conv_b_block(ext_ref, base, cw_ref):
    acc = jnp.zeros((ROWS, D_B), F32)
    for k, tap in _taps(ext_ref[pl.ds(base, ROWS + HALO_B), :], CONV_B_OFFSETS):
        acc = acc + tap * cw_ref[k:k + 1, :]
    return acc


def _taps_f(win):
    n = ROWS + HALO_F
    return [pltpu.roll(win, n - 6, 0)[0:ROWS, :], pltpu.roll(win, n - 7, 0)[0:ROWS, :], win[8:n, :]]


def _params(sem, **kw):
    return pltpu.CompilerParams(dimension_semantics=sem, vmem_limit_bytes=VMEM_LIMIT, **kw)


def _resident(shape):
    zeros = (0,) * len(shape)
    return pl.BlockSpec(shape, lambda *_: zeros, pipeline_mode=pl.Buffered(1))


def _full(shape):
    zeros = (0,) * len(shape)
    return pl.BlockSpec(shape, lambda *_: zeros)


def _mesh_pos():
    return lax.axis_index("x"), lax.axis_index("y"), lax.axis_index("c")


def _chip_patterns(x, y):
    return [(x, y), (1 - x, y), (x, 1 - y), (1 - x, 1 - y)]


def _lid(x, y, c):
    return 4 * x + 2 * y + c


def _gather_copy(outs, send_sems, recv_sems, a, k, block, to, src=None):
    blk = outs[a].at[_lid(*block)]
    return pltpu.make_async_remote_copy(
        src_ref=blk if src is None else src, dst_ref=blk,
        send_sem=send_sems.at[a, k], recv_sem=recv_sems.at[a, k], device_id=to, device_id_type=MESH)


def _gather_start(mine, outs, send_sems, recv_sems, local_sems, diagonal=False):
    x, y, c = _mesh_pos()
    me = (x, y, c)
    for a in range(len(mine)):
        pltpu.make_async_copy(mine[a], outs[a].at[_lid(*me)], local_sems.at[a]).start()
        targets = [(x, y, 1 - c), (1 - x, y, c), (x, 1 - y, c), (1 - x, 1 - y, c)]
        for k, to in enumerate(targets if diagonal else targets[:3]):
            _gather_copy(outs, send_sems, recv_sems, a, k, me, to, src=mine[a]).start()


def _gather_relay(mine, outs, send_sems, recv_sems, local_sems, via):
    x, y, c = _mesh_pos()
    me, sib = (x, y, c), (x, y, 1 - c)
    copy = functools.partial(_gather_copy, outs, send_sems, recv_sems)
    source = {1: (1 - x, y, c), 2: (x, 1 - y, c)}
    for a in range(len(mine)):
        for k in ((via[a], 3 - via[a]) if via[a] else (1, 2)):
            copy(a, k, source[k], me).wait_recv()
            if k == via[a]:
                copy(a, 3, source[k], source[3 - k]).start()
            copy(a, 3 + k, source[k], sib).start()


def _gather_finish(mine, outs, send_sems, recv_sems, local_sems):
    x, y, c = _mesh_pos()
    me, sib = (x, y, c), (x, y, 1 - c)
    copy = functools.partial(_gather_copy, outs, send_sems, recv_sems)
    diag = (1 - x, 1 - y)
    n = len(mine)
    for a in range(n):
        copy(a, 3, (*diag, c), me).wait_recv()
        copy(a, 6, (*diag, c), sib).start()
    for a in range(n):
        copy(a, 0, sib, me).wait_recv()
        for k, chip in zip((4, 5, 6), [(1 - x, y), (x, 1 - y), diag]):
            copy(a, k, (*chip, 1 - c), me).wait_recv()
        for k in range(7):
            copy(a, k, me, sib, src=mine[a]).wait_send()
        pltpu.make_async_copy(mine[a], outs[a].at[_lid(*me)], local_sems.at[a]).wait()


def _gather_scratch(n):
    return [pltpu.SemaphoreType.DMA((n, 7)), pltpu.SemaphoreType.DMA((n, 7)), pltpu.SemaphoreType.DMA((n,))]


def prepare_weights(w_in, w_out, w_up_t, w_down, convp):
    def body(win_ref, wout_ref, wup_ref, wdown_ref, convp_ref,
             sin_ref, sout_ref, sup_ref, sdown_ref, gconv_ref, send_sems, recv_sems, local_sems):
        gather = ([convp_ref], [gconv_ref], send_sems, recv_sems, local_sems)
        _gather_start(*gather, diagonal=True)
        sin_ref[...] = win_ref[...].astype(BF16)
        sout_ref[...] = wout_ref[...].astype(BF16)
        sup_ref[...] = wup_ref[...].T.astype(BF16)
        sdown_ref[...] = wdown_ref[...].astype(BF16)
        _gather_relay(*gather, via=[0])
        _gather_finish(*gather)

    return pl.pallas_call(
        body, name="prepare_weights",
        out_shape=[jax.ShapeDtypeStruct(w_in.shape, BF16), jax.ShapeDtypeStruct(w_out.shape, BF16),
                   jax.ShapeDtypeStruct(w_up_t.shape[::-1], BF16), jax.ShapeDtypeStruct(w_down.shape, BF16),
                   jax.ShapeDtypeStruct((N_DEV,) + convp.shape, F32)],
        in_specs=[VMEM] * 5, out_specs=[VMEM] * 4 + [ANY],
        scratch_shapes=_gather_scratch(1),
        compiler_params=pltpu.CompilerParams(vmem_limit_bytes=VMEM_LIMIT),
    )(w_in, w_out, w_up_t, w_down, convp)


def _chip_copies(p, land, send_sems, recv_sems):
    x, y, c = _mesh_pos()
    return [pltpu.make_async_remote_copy(
        src_ref=p[a].at[k], dst_ref=land[a].at[k], send_sem=send_sems.at[a, k], recv_sem=recv_sems.at[a, k],
        device_id=(px, py, c), device_id_type=MESH)
        for k, (px, py) in enumerate(_chip_patterns(x, y)[1:]) for a in range(len(p))]


def chip_partials(gs, lands, jidx, steps):
    n = len(gs)
    blocks = [(1, g.shape[1] // steps, g.shape[2]) for g in gs]

    def body(j_ref, *refs):
        for a in range(n):
            refs[2 * n + a][...] = (refs[a][...] + refs[n + a][...]).astype(BF16)

    return pl.pallas_call(
        body, name="chip_partials",
        out_shape=[jax.ShapeDtypeStruct((3,) + g.shape[1:], BF16) for g in gs],
        grid_spec=pltpu.PrefetchScalarGridSpec(
            num_scalar_prefetch=1, grid=(3, steps),
            in_specs=[pl.BlockSpec(b, lambda k, i, j: (j[1 + k], i, 0)) for b in blocks]
            + [pl.BlockSpec(b, lambda k, i, j: (1 + k, i, 0)) for b in blocks],
            out_specs=[pl.BlockSpec(b, lambda k, i, j: (k, i, 0)) for b in blocks]),
        compiler_params=_params(("arbitrary", "arbitrary")),
    )(jidx, *gs, *lands)


def _adamw(w, g, m, v):
    m2 = ADAM_B1 * m + (1.0 - ADAM_B1) * g
    v2 = ADAM_B2 * v + (1.0 - ADAM_B2) * (g * g)
    m_hat = m2 / (1.0 - ADAM_B1 ** ADAM_STEP)
    v_hat = v2 / (1.0 - ADAM_B2 ** ADAM_STEP)
    delta = -ADAM_LR * (m_hat / (jnp.sqrt(v_hat) + ADAM_EPS) + ADAM_WD * w)
    return delta, m2, v2


def reduce_and_adamw(gs, lands, recvs, ws, ms, vs, jidx, steps):
    n = len(gs)
    rbs = [g.shape[1] // steps for g in gs]

    def body(j_ref, *refs):
        g, land, recv, w, m, v = (refs[k * n:(k + 1) * n] for k in range(6))
        outs = refs[6 * n:]
        for a in range(n):
            grad = ((g[a][0] + land[a][0]) + recv[a][0].astype(F32) + recv[a][1].astype(F32)
                    + recv[a][2].astype(F32))
            delta, m2, v2 = _adamw(w[a][...], grad, m[a][...], v[a][...])
            outs[a][...] = grad
            outs[n + a][...] = delta
            outs[2 * n + a][...] = m2
            outs[3 * n + a][...] = v2

    blk = [pl.BlockSpec((rb, g.shape[2]), lambda i, j: (i, 0)) for g, rb in zip(gs, rbs)]
    part = lambda lead, pick: [pl.BlockSpec((lead, rb, g.shape[2]), pick) for g, rb in zip(gs, rbs)]
    return pl.pallas_call(
        body, name="reduce_adamw",
        out_shape=[jax.ShapeDtypeStruct(g.shape[1:], F32) for g in gs] * 4,
        grid_spec=pltpu.PrefetchScalarGridSpec(
            num_scalar_prefetch=1, grid=(steps,),
            in_specs=part(1, lambda i, j: (j[0], i, 0)) + part(1, lambda i, j: (0, i, 0))
            + part(3, lambda i, j: (0, i, 0)) + blk * 3,
            out_specs=blk * 4),
        compiler_params=_params(("arbitrary",)),
    )(jidx, *gs, *lands, *recvs, *ws, *ms, *vs)


def mixer_reduce(grads, sv_parts):
    n = len(grads)
    shard = [g.shape[1:] for g in grads]
    ns = len(sv_parts)
    sv_shape = (sum(p.shape[0] for p in sv_parts), 128)

    def body(*refs):
        g = refs[:n]
        parts = refs[n:n + ns]
        outs = refs[n + ns:2 * n + ns]
        sv_slots = refs[2 * n + ns]
        rest = refs[2 * n + ns + 1:-1]
        sv_ref = refs[-1]
        row0 = 0
        for part in parts:
            sv_ref[row0:row0 + part.shape[0], :] = part[...]
            row0 += part.shape[0]
        own, land, sendb, recvb = rest[:n], rest[n:2 * n], rest[2 * n:3 * n], rest[3 * n:4 * n]
        sv_land, chip_sv, d2d_send, d2d_recv, ici_send, ici_recv, local_sems, sv_sems = rest[4 * n:]
        x, y, c = _mesh_pos()
        sib = (x, y, 1 - c)
        pats = _chip_patterns(x, y)
        q = 2 * x + y

        d2d, local = {}, {}
        for a in range(n):
            for k, (px, py) in enumerate(pats):
                d2d[a, k] = pltpu.make_async_remote_copy(
                    src_ref=g[a].at[_lid(px, py, 1 - c)], dst_ref=land[a].at[k],
                    send_sem=d2d_send.at[a, k], recv_sem=d2d_recv.at[a, k], device_id=sib, device_id_type=MESH)
                local[a, k] = pltpu.make_async_copy(g[a].at[_lid(px, py, c)], own[a].at[k], local_sems.at[a, k])
        sv_d2d = pltpu.make_async_remote_copy(
            src_ref=sv_ref, dst_ref=sv_land, send_sem=d2d_send.at[n, 0], recv_sem=d2d_recv.at[n, 0],
            device_id=sib, device_id_type=MESH)
        blocks = [(a, k) for a in range(n) for k in (1, 2, 3)] + [(a, 0) for a in range(n)]
        sv_d2d.start()
        for b in blocks:
            d2d[b].start()
            local[b].start()

        half_rows = sv_shape[0] // 2
        rows = pl.ds(pl.multiple_of(c * half_rows, 8), half_rows)
        sv_local = pltpu.make_async_copy(chip_sv, sv_slots.at[q], sv_sems.at[0])

        def sv_ici(k, slot, to):
            return pltpu.make_async_remote_copy(
                src_ref=chip_sv.at[rows], dst_ref=sv_slots.at[slot, rows], send_sem=sv_sems.at[1 + k],
                recv_sem=sv_sems.at[4 + k], device_id=to, device_id_type=MESH)

        def sv_pass_on(k, slot):
            return pltpu.make_async_remote_copy(
                src_ref=sv_slots.at[slot, rows], dst_ref=sv_slots.at[slot, rows], send_sem=sv_sems.at[7 + k],
                recv_sem=sv_sems.at[10 + k], device_id=sib, device_id_type=MESH)

        sv_d2d.wait()
        chip_sv[...] = sv_ref[...] + sv_land[...]
        sv_out = [sv_ici(k, q, (px, py, c)) for k, (px, py) in enumerate(pats[1:])]
        for cp in sv_out + [sv_local]:
            cp.start()

        ici = _chip_copies(sendb, recvb, ici_send, ici_recv)
        for a, k in blocks:
            local[a, k].wait()
            d2d[a, k].wait()
            if k > 0:
                sendb[a][k - 1] = (own[a][k] + land[a][k]).astype(BF16)
                ici[(k - 1) * n + a].start()
        for k, (px, py) in enumerate(pats[1:]):
            sv_out[k].wait_send()
            sv_ici(k, 2 * px + py, (px, py, c)).wait_recv()
            sv_pass_on(k, 2 * px + py).start()
        for a in range(n):
            for k in range(3):
                ici[k * n + a].wait()
            outs[a][...] = ((own[a][0] + land[a][0]) + recvb[a][0].astype(F32) + recvb[a][1].astype(F32)
                            + recvb[a][2].astype(F32))
        for k, (px, py) in enumerate(pats[1:]):
            sv_pass_on(k, 2 * px + py).wait()
        sv_local.wait()

    shard_out = [jax.ShapeDtypeStruct(s, F32) for s in shard]
    return pl.pallas_call(
        body, name="mixer_reduce",
        out_shape=shard_out + [jax.ShapeDtypeStruct((4,) + sv_shape, F32)],
        in_specs=[ANY] * n + [VMEM] * ns, out_specs=[VMEM] * n + [ANY],
        scratch_shapes=[pltpu.VMEM((4,) + s, F32) for s in shard] + [pltpu.VMEM((4,) + s, F32) for s in shard]
        + [pltpu.VMEM((3,) + s, BF16) for s in shard] + [pltpu.VMEM((3,) + s, BF16) for s in shard]
        + [pltpu.VMEM(sv_shape, F32), pltpu.VMEM(sv_shape, F32),
           pltpu.SemaphoreType.DMA((n + 1, 4)), pltpu.SemaphoreType.DMA((n + 1, 4)),
           pltpu.SemaphoreType.DMA((n, 3)), pltpu.SemaphoreType.DMA((n, 3)),
           pltpu.SemaphoreType.DMA((n, 4)), pltpu.SemaphoreType.DMA((13,)), pltpu.VMEM(sv_shape, F32)],
        compiler_params=pltpu.CompilerParams(vmem_limit_bytes=VMEM_LIMIT),
    )(*grads, *sv_parts)


SMALL_LAYOUT = [
    ("b_in", S_BIN, 16), ("ln_a_g", S_LNAG, 4), ("ln_a_b", S_LNAB, 4), ("w_spatial", S_WS, 512),
    ("b_spatial", S_BS, 4), ("conv_b_b", S_CBB, 4), ("ln_b_g", S_LNBG, 4), ("ln_b_b", S_LNBB, 4),
    ("b_out", S_BOUT, 8), ("ln1_g", S_LN1G, 8), ("ln1_b", S_LN1B, 8), ("conv_f_b", S_CFB, 44),
    ("ln2_g", S_LN2G, 8), ("ln2_b", S_LN2B, 8),
]


def small_adamw(sv_slots, ws, ms, vs, shard_grads, shard_ws, shard_ms, shard_vs):
    n = len(SMALL_LAYOUT)
    nb = len(shard_grads)

    def body(*refs):
        s_ref = refs[0]
        w_refs, m_refs, v_refs = refs[1:1 + n], refs[1 + n:1 + 2 * n], refs[1 + 2 * n:1 + 3 * n]
        big_in = refs[1 + 3 * n:1 + 3 * n + 4 * nb]
        outs = refs[1 + 3 * n + 4 * nb:]
        big_out = outs[4 * n + 1:]
        for p in range(nb):
            grad = big_in[p][...]
            delta, m2, v2 = _adamw(big_in[nb + p][...], grad, big_in[2 * nb + p][...], big_in[3 * nb + p][...])
            big_out[p][...] = grad
            big_out[nb + p][...] = delta
            big_out[2 * nb + p][...] = m2
            big_out[3 * nb + p][...] = v2
        for p, (_, row0, rows) in enumerate(SMALL_LAYOUT):
            sl = pl.ds(row0, rows)
            grad = ((s_ref[0, sl, :] + s_ref[1, sl, :]) + s_ref[2, sl, :]) + s_ref[3, sl, :]
            delta, m2, v2 = _adamw(w_refs[p][...], grad, m_refs[p][...], v_refs[p][...])
            outs[p][...] = grad
            outs[n + p][...] = delta
            outs[2 * n + p][...] = m2
            outs[3 * n + p][...] = v2
        sl = pl.ds(S_LOSS, 8)
        outs[4 * n][...] = ((s_ref[0, sl, :] + s_ref[1, sl, :]) + s_ref[2, sl, :]) + s_ref[3, sl, :]

    shapes = [jax.ShapeDtypeStruct((rows, 128), F32) for _, _, rows in SMALL_LAYOUT]
    big_shapes = [jax.ShapeDtypeStruct(g.shape, F32) for g in shard_grads]
    return pl.pallas_call(
        body, name="small_adamw",
        out_shape=shapes * 4 + [jax.ShapeDtypeStruct((8, 128), F32)] + big_shapes * 4,
        in_specs=[VMEM] * (1 + 3 * n + 4 * nb), out_specs=[VMEM] * (4 * n + 1 + 4 * nb),
        compiler_params=pltpu.CompilerParams(vmem_limit_bytes=VMEM_LIMIT),
    )(sv_slots, *ws, *ms, *vs, *shard_grads, *shard_ws, *shard_ms, *shard_vs)


def mix_forward(x, sin, sout, b_in, ln_a_g, ln_a_b, w_spatial, bst, conv_b_w, conv_b_b, ln_b_g, ln_b_b,
                b_out, ln1_g, ln1_b, sup, sdown, tm):
    t = x.shape[0]
    nt = t // tm
    n_chunks = tm // CHUNK

    def body(x_ref, sin_ref, sout_ref, bin_ref, ga_ref, ba_ref, ws_ref, bst_ref, cw_ref, cb_ref, gb_ref,
             bb_ref, bout_ref, g1_ref, b1_ref, sup_ref, sdown_ref,
             h_ref, xhat1_ref, rstd1_ref, yb1_ref, gin_ref, gout_ref, gup_ref, gdown_ref,
             ext_ref, y_ref, wsm_ref, win_ref, wout_ref, load_sems,
             mix_send, mix_recv, mix_local, send_sems, recv_sems, local_sems):
        i = pl.program_id(0)
        mixer = ([sin_ref, sout_ref], [gin_ref, gout_ref], mix_send, mix_recv, mix_local)
        gather = ([sup_ref, sdown_ref], [gup_ref, gdown_ref], send_sems, recv_sems, local_sems)

        @pl.when(i == 0)
        def _():
            _gather_start(*mixer)
            _gather_relay(*mixer, via=[2, 2])
            _gather_finish(*mixer)
            _gather_start(*gather)
            loads = [pltpu.make_async_copy(gin_ref, win_ref, load_sems.at[0]),
                     pltpu.make_async_copy(gout_ref, wout_ref, load_sems.at[1])]
            for cp in loads:
                cp.start()
            for cp in loads:
                cp.wait()
            ext_ref[0:HALO_B, :] = jnp.zeros((HALO_B, D_B), F32)
            mask = _tril_mask()
            for hd in range(HEADS):
                wsm_ref[hd] = jnp.where(mask, ws_ref[hd], 0.0).astype(BF16)

        xb = x_ref[...].astype(BF16)
        for j in range(N_DEV):
            cols = slice(j * W_IN_BLK, (j + 1) * W_IN_BLK)
            h_ref[:, cols] = _nn(xb, win_ref[j]) + bin_ref[:, cols]

        def chunk(ci):
            r = _rows(ci, CHUNK)
            for hd in range(HEADS):
                _, _, u, _, _, _, _, _, sv = _mixer_a_head(h_ref, r, hd, ga_ref, ba_ref, wsm_ref, bst_ref)
                y_ref[r, hd * HEAD_DIM:(hd + 1) * HEAD_DIM] = (u * sv).astype(BF16)
            a_b = h_ref[r, 2 * D_A:2 * D_A + D_B]
            g_b = h_ref[r, 2 * D_A + D_B:D_IN]
            ext_ref[pl.ds(HALO_B + ci * CHUNK, CHUNK), :] = a_b * _sigmoid(g_b)

        _loop(n_chunks, chunk)

        def conv_rows(bi):
            base = bi * ROWS
            yb1 = _conv_b_block(ext_ref, base, cw_ref) + cb_ref[...]
            yb1_ref[pl.ds(base, ROWS), :] = yb1
            xhat, _ = _ln_stats(yb1)
            yb2 = xhat * gb_ref[...] + bb_ref[...]
            y_ref[pl.ds(base, ROWS), D_A:D] = (yb2 * _sigmoid(yb2)).astype(BF16)

        _loop(tm // ROWS, conv_rows)
        ext_ref[0:HALO_B, :] = ext_ref[tm:tm + HALO_B, :]

        mix = _nn(y_ref[...], wout_ref[...].reshape(D, D)) + bout_ref[...]
        xhat1, rstd1 = _ln_stats(ALPHA * x_ref[...] + mix)
        xhat1_ref[...] = xhat1
        rstd1_ref[...] = jnp.broadcast_to(rstd1, (tm, 128))

        @pl.when(i == (5 * nt) // 8)
        def _():
            _gather_relay(*gather, via=[1, 2])

        @pl.when(i == nt - 1)
        def _():
            _gather_finish(*gather)

    row = lambda w: pl.BlockSpec((tm, w), lambda i: (i, 0))
    return pl.pallas_call(
        body, name="mix_forward", grid=(nt,),
        in_specs=[row(D), ANY, ANY, _full(b_in.shape), _full(ln_a_g.shape),
                  _full(ln_a_b.shape), _full(w_spatial.shape), _full(bst.shape),
                  _full(conv_b_w.shape), _full(conv_b_b.shape), _full(ln_b_g.shape),
                  _full(ln_b_b.shape), _full(b_out.shape),
                  _full(ln1_g.shape), _full(ln1_b.shape), ANY, ANY],
        out_specs=[row(D_IN), row(D), row(128), row(D_B), ANY, ANY, ANY, ANY],
        out_shape=[jax.ShapeDtypeStruct((t, D_IN), F32), jax.ShapeDtypeStruct((t, D), F32),
                   jax.ShapeDtypeStruct((t, 128), F32), jax.ShapeDtypeStruct((t, D_B), F32)]
        + [jax.ShapeDtypeStruct((N_DEV,) + sh.shape, BF16) for sh in (sin, sout, sup, sdown)],
        scratch_shapes=[pltpu.VMEM((tm + HALO_B, D_B), F32), pltpu.VMEM((tm, D), BF16),
                        pltpu.VMEM((HEADS, CHUNK, CHUNK), BF16),
                        pltpu.VMEM((N_DEV,) + sin.shape, BF16), pltpu.VMEM((N_DEV,) + sout.shape, BF16),
                        pltpu.SemaphoreType.DMA((2,))] + _gather_scratch(2) + _gather_scratch(2),
        compiler_params=_params(("arbitrary",)),
    )(x, sin, sout, b_in, ln_a_g, ln_a_b, w_spatial, bst, conv_b_w, conv_b_b, ln_b_g, ln_b_b,
      b_out, ln1_g, ln1_b, sup, sdown)


def ffn_forward(xhat1, ln1_g, ln1_b, wup_g, cfw, cfb, wdown, ln2_g, ln2_b, target, tm):
    t = xhat1.shape[0]
    nt = t // tm

    def body(xh_ref, g1_ref, b1_ref, wup_ref, cfw_ref, cfb_ref, wdown_ref, g2_ref, b2_ref, tgt_ref,
             hu_ref, gv_ref, dr2_ref, loss_ref, sln2_ref,
             x1_ref, x1b_ref, hu32_ref, carry_ref, gbuf_ref, ffn_ref, acc_loss, acc_g2, acc_b2):
        i = pl.program_id(0)

        @pl.when(i == 0)
        def _():
            carry_ref[...] = jnp.zeros(carry_ref.shape, F32)
            acc_loss[...] = jnp.zeros(acc_loss.shape, F32)
            acc_g2[...] = jnp.zeros(acc_g2.shape, F32)
            acc_b2[...] = jnp.zeros(acc_b2.shape, F32)

        x1 = xh_ref[...] * g1_ref[...] + b1_ref[...]
        x1_ref[...] = x1
        x1b_ref[...] = x1.astype(BF16)

        def conv(g, j, base):
            if base == 0:
                win = jnp.concatenate([carry_ref[j], hu32_ref[g, 0:ROWS, :]], axis=0)
            else:
                win = hu32_ref[g, base - HALO_F:base + ROWS, :]
            taps = _taps_f(win)
            w = cfw_ref[j]
            return sum(taps[k] * w[k:k + 1, :] for k in range(KF)) + cfb_ref[j:j + 1, :]

        for f in range(N_F):
            hu32_ref[0] = _nn(x1b_ref[...], wup_ref[f])
            hu32_ref[1] = _nn(x1b_ref[...], wup_ref[N_F + f])

            def rows(bi, f=f):
                r = _rows(bi)
                gate = conv(0, f, bi * ROWS)
                val = conv(1, N_F + f, bi * ROWS)
                gbuf_ref[r, :] = (gate * _sigmoid(gate) * val).astype(BF16)
                gv_ref[f, r, :] = gate.astype(BF16)
                gv_ref[N_F + f, r, :] = val.astype(BF16)
                hu_ref[f, r, :] = hu32_ref[0, r, :].astype(BF16)
                hu_ref[N_F + f, r, :] = hu32_ref[1, r, :].astype(BF16)

            _loop(tm // ROWS, rows)
            carry_ref[f] = hu32_ref[0, tm - HALO_F:tm, :]
            carry_ref[N_F + f] = hu32_ref[1, tm - HALO_F:tm, :]
            part = _nn(gbuf_ref[...], wdown_ref[f])
            if f == 0:
                ffn_ref[...] = part
            else:
                ffn_ref[...] += part

        def tail(bi):
            r = _rows(bi, LN_ROWS)
            xhat2, rstd2 = _ln_stats(ALPHA * x1_ref[r, :] + ffn_ref[r, :])
            err = xhat2 * g2_ref[...] + b2_ref[...] - tgt_ref[r, :]
            e2 = _rsum8(err * err)
            acc_loss[...] += sum(e2[:, k * 128:(k + 1) * 128] for k in range(D // 128))
            dy = err * (1.0 / D)
            acc_g2[...] += _rsum8(dy * xhat2)
            acc_b2[...] += _rsum8(dy)
            dr2_ref[r, :] = _ln_bwd(dy * g2_ref[...], xhat2, rstd2)

        _loop(tm // LN_ROWS, tail)
        loss_ref[...] = acc_loss[...]

        @pl.when(i == nt - 1)
        def _():
            dg = jnp.sum(acc_g2[...], axis=0, keepdims=True)
            db = jnp.sum(acc_b2[...], axis=0, keepdims=True)
            for k in range(D // 128):
                sln2_ref[k:k + 1, :] = dg[:, k * 128:(k + 1) * 128]
                sln2_ref[8 + k:9 + k, :] = db[:, k * 128:(k + 1) * 128]

    row = pl.BlockSpec((tm, D), lambda i: (i, 0))
    return pl.pallas_call(
        body, name="ffn_forward", grid=(nt,),
        in_specs=[row, _full(ln1_g.shape), _full(ln1_b.shape), _resident(wup_g.shape),
                  _full(cfw.shape), _full(cfb.shape), _resident(wdown.shape),
                  _full(ln2_g.shape), _full(ln2_b.shape), row],
        out_specs=[pl.BlockSpec((N_DEV, tm, W_UP_BLK), lambda i: (0, i, 0)),
                   pl.BlockSpec((N_DEV, tm, W_UP_BLK), lambda i: (0, i, 0)), row,
                   _full((8, 128)), _full((16, 128))],
        out_shape=[jax.ShapeDtypeStruct((N_DEV, t, W_UP_BLK), BF16),
                   jax.ShapeDtypeStruct((N_DEV, t, W_UP_BLK), BF16), jax.ShapeDtypeStruct((t, D), F32),
                   jax.ShapeDtypeStruct((8, 128), F32), jax.ShapeDtypeStruct((16, 128), F32)],
        scratch_shapes=[pltpu.VMEM((tm, D), F32), pltpu.VMEM((tm, D), BF16),
                        pltpu.VMEM((2, tm, W_UP_BLK), F32),
                        pltpu.VMEM((N_DEV, HALO_F, W_UP_BLK), F32), pltpu.VMEM((tm, W_UP_BLK), BF16),
                        pltpu.VMEM((tm, D), F32), pltpu.VMEM((8, 128), F32),
                        pltpu.VMEM((8, D), F32), pltpu.VMEM((8, D), F32)],
        compiler_params=_params(("arbitrary",)),
    )(xhat1, ln1_g, ln1_b, wup_g, cfw, cfb, wdown, ln2_g, ln2_b, target)


def ffn_backward(order, dr2, xhat1, ln1_g, ln1_b, hu, gv, wup_g, cfw, wdown, tm):
    t = dr2.shape[0]
    nt = t // tm
    sub_rows = tm
    hu4 = hu.reshape(2, N_F, t, W_UP_BLK)
    gv4 = gv.reshape(2, N_F, t, W_UP_BLK)
    wup4 = wup_g.reshape(2, N_F, D, W_UP_BLK)
    cfw4 = cfw.reshape(2, N_F, KF, W_UP_BLK)

    def body(order_ref, dr2_ref, xh_ref, g1_ref, b1_ref, hu_ref, gv_ref, wup_ref, cfw_ref, wdown_ref,
             dwup_ref, dwdown_ref, dcfw_ref, dcfb_ref, dx1_ref, land_up_ref, land_down_ref,
             x1b_ref, drb_ref, dg_ref, dextg_ref, dextv_ref, gbuf_ref,
             dhug_ref, dhuv_ref, acc_wup, acc_wdown, acc_cfw, acc_cfb, sem, send_sems, recv_sems):
        fo = pl.program_id(0)
        f = order_ref[fo]
        f_prev = order_ref[jnp.maximum(fo - 1, 0)]
        slot = fo % 2
        i = pl.program_id(1)
        x, y, c = _mesh_pos()
        half = D_FF // N_DEV

        def to_sibling(fi, k, src, land_ref, shard_chip):
            d = jnp.bitwise_xor(shard_chip, 2 * x + y)
            slot = jnp.where(d == 1, 2, jnp.where(d == 2, 1, d))
            return pltpu.make_async_remote_copy(
                src_ref=src, dst_ref=land_ref.at[slot], send_sem=send_sems.at[fi, k], recv_sem=recv_sems.at[fi, k],
                device_id=(x, y, 1 - c), device_id_type=MESH)

        def up_copy(fi, g):
            return to_sibling(fi, g, dwup_ref.at[g, fi], land_up_ref, 2 * g + fi // 2)

        def down_copy(fi):
            return to_sibling(fi, 2, dwdown_ref.at[fi, pl.ds((1 - c) * half, half)], land_down_ref, fi)

        def flush(fi, s):
            return [pltpu.make_async_copy(acc_wup.at[s, 0], dwup_ref.at[0, fi], sem.at[s, 0]),
                    pltpu.make_async_copy(acc_wup.at[s, 1], dwup_ref.at[1, fi], sem.at[s, 1]),
                    pltpu.make_async_copy(acc_wdown.at[s], dwdown_ref.at[fi], sem.at[s, 2])]

        def flushed(fi, s):
            for cp in flush(fi, s):
                cp.wait()
            down_copy(fi).start()

            @pl.when(fi % 2 != c)
            def _():
                up_copy(fi, 0).start()
                up_copy(fi, 1).start()

        @pl.when(i == 0)
        def _():
            acc_wup[slot] = jnp.zeros(acc_wup.shape[1:], F32)
            acc_wdown[slot] = jnp.zeros(acc_wdown.shape[1:], F32)
            acc_cfw[...] = jnp.zeros(acc_cfw.shape, F32)
            acc_cfb[...] = jnp.zeros(acc_cfb.shape, F32)
            dextg_ref[tm:tm + HALO_F, :] = jnp.zeros((HALO_F, W_UP_BLK), F32)
            dextv_ref[tm:tm + HALO_F, :] = jnp.zeros((HALO_F, W_UP_BLK), F32)

        w = [cfw_ref[0, 0], cfw_ref[1, 0]]
        dext = [dextg_ref, dextv_ref]
        dhu = [dhug_ref, dhuv_ref]

        def rows1(bi):
            r = _rows(bi)
            gate = gv_ref[0, 0, r, :].astype(F32)
            val = gv_ref[1, 0, r, :].astype(F32)
            sg = _sigmoid(gate)
            silu = gate * sg
            gbuf_ref[r, :] = (silu * val).astype(BF16)
            dg = dg_ref[r, :]
            dgate = dg * val * (sg * (1.0 + gate * (1.0 - sg)))
            dval = dg * silu
            dextg_ref[r, :] = dgate
            dextv_ref[r, :] = dval
            acc_cfb[0:8, :] += _rsum8(dgate)
            acc_cfb[8:16, :] += _rsum8(dval)

        def rows2(bi):
            r = _rows(bi)
            for g in range(2):
                win = dext[g][pl.ds(bi * ROWS, ROWS + HALO_F), :]
                n = ROWS + HALO_F
                later = [pltpu.roll(win, n - 2, 0)[0:ROWS, :], pltpu.roll(win, n - 1, 0)[0:ROWS, :],
                         win[0:ROWS, :]]
                d = sum(later[k] * w[g][k:k + 1, :] for k in range(KF))
                dhu[g][r, :] = d.astype(BF16)
                pre = hu_ref[g, 0, r, :].astype(F32)
                for k in range(KF):
                    r0 = 8 * (g * KF + k)
                    acc_cfw[r0:r0 + 8, :] += _rsum8(later[k] * pre)

        for sub in reversed(range(tm // sub_rows)):
            rs = slice(sub * sub_rows, (sub + 1) * sub_rows)
            blocks = range(sub * sub_rows // ROWS, (sub + 1) * sub_rows // ROWS)
            x1b_ref[rs, :] = (xh_ref[rs, :] * g1_ref[...] + b1_ref[...]).astype(BF16)
            drb_ref[rs, :] = dr2_ref[rs, :].astype(BF16)
            dg_ref[rs, :] = _nt(drb_ref[rs, :], wdown_ref[0])
            for bi in blocks:
                rows1(bi)
            for bi in blocks:
                rows2(bi)
            acc_wdown[slot] += _tn(gbuf_ref[rs, :], drb_ref[rs, :])
            acc_wup[slot, 0] += _tn(dhug_ref[rs, :], x1b_ref[rs, :])
            acc_wup[slot, 1] += _tn(dhuv_ref[rs, :], x1b_ref[rs, :])
            dx1_ref[0, rs, :] = (_nt(dhug_ref[rs, :], wup_ref[0, 0])
                                 + _nt(dhuv_ref[rs, :], wup_ref[1, 0])).astype(BF16)
        dextg_ref[tm:tm + HALO_F, :] = dextg_ref[0:HALO_F, :]
        dextv_ref[tm:tm + HALO_F, :] = dextv_ref[0:HALO_F, :]

        @pl.when(i == nt - 1)
        def _():
            for g in range(2):
                dcfb_ref[g, 0] = jnp.sum(acc_cfb[8 * g:8 * g + 8, :], axis=0, keepdims=True)
                for k in range(KF):
                    r0 = 8 * (g * KF + k)
                    dcfw_ref[g, 0, k:k + 1, :] = jnp.sum(acc_cfw[r0:r0 + 8, :], axis=0, keepdims=True)
            for cp in flush(f, slot):
                cp.start()

        @pl.when((i == 0) & (fo > 0))
        def _():
            flushed(f_prev, 1 - slot)

        @pl.when((i == nt - 1) & (fo == N_F - 1))
        def _():
            rows = pl.ds(pl.multiple_of((1 - c) * half, 8), half)
            to_sibling(f, 2, acc_wdown.at[slot, rows], land_down_ref, f).start()
            for cp in flush(f, slot):
                cp.wait()

            @pl.when(f % 2 != c)
            def _():
                up_copy(f, 0).start()
                up_copy(f, 1).start()

            for fi in range(N_F):
                down_copy(fi).wait()
                for g in range(2):
                    @pl.when(fi % 2 != c)
                    def _():
                        up_copy(fi, g).wait_send()

                    @pl.when(fi % 2 == c)
                    def _():
                        up_copy(fi, g).wait_recv()

    rev = lambda i: nt - 1 - i
    row = pl.BlockSpec((tm, D), lambda fo, i, o: (rev(i), 0))
    pair = lambda r, c: pl.BlockSpec((2, 1, r, c), lambda fo, i, o: (0, o[fo], 0, 0))
    tile = pl.BlockSpec((2, 1, tm, W_UP_BLK), lambda fo, i, o: (0, o[fo], rev(i), 0))
    return pl.pallas_call(
        body, name="ffn_backward",
        grid_spec=pltpu.PrefetchScalarGridSpec(
            num_scalar_prefetch=1, grid=(N_F, nt),
            in_specs=[row, row, _full(ln1_g.shape), _full(ln1_b.shape), tile, tile,
                      pair(D, W_UP_BLK), pair(KF, W_UP_BLK),
                      pl.BlockSpec((1, W_UP_BLK, D), lambda fo, i, o: (o[fo], 0, 0))],
            out_specs=[ANY, ANY, pair(KF, W_UP_BLK), pair(1, W_UP_BLK),
                       pl.BlockSpec((1, tm, D), lambda fo, i, o: (o[fo], rev(i), 0)), ANY, ANY],
            scratch_shapes=[pltpu.VMEM((tm, D), BF16), pltpu.VMEM((tm, D), BF16),
                            pltpu.VMEM((tm, W_UP_BLK), F32),
                            pltpu.VMEM((tm + HALO_F, W_UP_BLK), F32), pltpu.VMEM((tm + HALO_F, W_UP_BLK), F32),
                            pltpu.VMEM((tm, W_UP_BLK), BF16), pltpu.VMEM((tm, W_UP_BLK), BF16),
                            pltpu.VMEM((tm, W_UP_BLK), BF16),
                            pltpu.VMEM((2, 2, W_UP_BLK, D), F32), pltpu.VMEM((2, W_UP_BLK, D), F32),
                            pltpu.VMEM((2 * KF * 8, W_UP_BLK), F32), pltpu.VMEM((16, W_UP_BLK), F32),
                            pltpu.SemaphoreType.DMA((2, 3)),
                            pltpu.SemaphoreType.DMA((N_F, 3)), pltpu.SemaphoreType.DMA((N_F, 3))]),
        out_shape=[jax.ShapeDtypeStruct((2, N_F, W_UP_BLK, D), F32),
                   jax.ShapeDtypeStruct((N_F, W_UP_BLK, D), F32),
                   jax.ShapeDtypeStruct((2, N_F, KF, W_UP_BLK), F32),
                   jax.ShapeDtypeStruct((2, N_F, 1, W_UP_BLK), F32),
                   jax.ShapeDtypeStruct((N_F, t, D), BF16),
                   jax.ShapeDtypeStruct((4, W_UP_BLK, D), F32),
                   jax.ShapeDtypeStruct((4, D_FF // N_DEV, D), F32)],
        compiler_params=_params(("arbitrary", "arbitrary")),
    )(order, dr2, xhat1, ln1_g, ln1_b, hu4, gv4, wup4, cfw4, wdown)


def mix_backward(x, h, yb1, dx1p, dr2, xhat1, rstd1, win_g, ln_a_g, ln_a_b, w_spatial, bst,
                 conv_b_w, ln_b_g, ln_b_b, wout, ln1_g, ffn_partials, tm):
    t = x.shape[0]
    n_p = len(ffn_partials)
    nt = t // tm
    n_chunks = tm // CHUNK
    halo_blocks = tm // HALO_B

    def body(x_ref, h_ref, halo_ref, yb1_ref, dx1p_ref, dr2_ref, xh1_ref, rstd1_ref, win_ref, ga_ref, ba_ref,
             ws_ref, bst_ref, cw_ref, gb_ref, bb_ref, wout_ref, g1_ref, *rest):
        p_refs, rest = rest[:n_p], rest[n_p:]
        gx_ref, dwin_ref, dwout_ref, dcw_ref, small_ref = rest[:5]
        land_refs, rest = rest[5:5 + n_p], rest[5 + n_p:]
        (ext_ref, dext_ref, y_ref, dy_ref, dh_ref, dmb_ref, wsm_ref,
         acc_win, acc_wout, acc_bin, acc_lnag, acc_lnab, acc_ws, acc_bs, acc_cbb, acc_lnbg,
         acc_lnbb, acc_bout, acc_ln1g, acc_ln1b, acc_cw, sem, send_sems, recv_sems) = rest
        i = pl.program_id(0)

        @pl.when(i == 0)
        def _():
            for cp in _chip_copies(p_refs, land_refs, send_sems, recv_sems):
                cp.start()

        first_tile = i == nt - 1
        accs = [acc_win, acc_wout, acc_bin, acc_lnag, acc_lnab, acc_ws, acc_bs, acc_cbb, acc_lnbg,
                acc_lnbb, acc_bout, acc_ln1g, acc_ln1b, acc_cw]

        @pl.when(i == 0)
        def _():
            for acc in accs:
                acc[...] = jnp.zeros(acc.shape, F32)
            dext_ref[tm:tm + HALO_B, :] = jnp.zeros((HALO_B, D_B), F32)
            mask = _tril_mask()
            for hd in range(HEADS):
                wsm_ref[hd] = jnp.where(mask, ws_ref[hd], 0.0).astype(BF16)

        def ln1_rows(bi):
            r = _rows(bi, LN_ROWS)
            part = [dx1p_ref[f, r, :].astype(F32) for f in range(N_F)]
            dx1 = ALPHA * dr2_ref[r, :] + ((part[0] + part[1]) + (part[2] + part[3]))
            xhat = xh1_ref[r, :]
            acc_ln1g[...] += _rsum8(dx1 * xhat)
            acc_ln1b[...] += _rsum8(dx1)
            dr1 = _ln_bwd(dx1 * g1_ref[...], xhat, rstd1_ref[r, 0:1])
            acc_bout[...] += _rsum8(dr1)
            gx_ref[r, :] = ALPHA * dr1
            dmb_ref[r, :] = dr1.astype(BF16)

        _loop(tm // LN_ROWS, ln1_rows)
        dy_ref[...] = _nt(dmb_ref[...], wout_ref[...])

        ha = halo_ref[:, 0:D_B]
        hg = halo_ref[:, D_B:2 * D_B]
        ext_ref[0:HALO_B, :] = jnp.where(first_tile, 0.0, 1.0) * (ha * _sigmoid(hg))

        def chunk(ci):
            r = _rows(ci, CHUNK)
            for hd in range(HEADS):
                sl = slice(hd * HEAD_DIM, (hd + 1) * HEAD_DIM)
                rows8 = slice(8 * hd, 8 * hd + 8)
                hus, hvs, u, cdf_u, cdf_v, xhat, rstd, vn, sv = _mixer_a_head(
                    h_ref, r, hd, ga_ref, ba_ref, wsm_ref, bst_ref)
                dy_a = dy_ref[r, sl]
                y_ref[r, sl] = (u * sv).astype(BF16)
                du = dy_a * sv
                dsv = dy_a * u
                dsvb = dsv.astype(BF16)
                acc_bs[hd] += dsv
                acc_ws[hd] += _nt(dsvb, vn)
                dvn = _tn(wsm_ref[hd], dsvb)
                acc_lnag[rows8, :] += _rsum8(dvn * xhat)
                acc_lnab[rows8, :] += _rsum8(dvn)
                dv = _ln_bwd(dvn * ga_ref[hd:hd + 1, :], xhat, rstd)
                slv = slice(D_A + hd * HEAD_DIM, D_A + (hd + 1) * HEAD_DIM)
                dhu = du * (cdf_u + hus * jnp.exp(-0.5 * hus * hus) * INV_SQRT_2PI)
                dhv = dv * (cdf_v + hvs * jnp.exp(-0.5 * hvs * hvs) * INV_SQRT_2PI)
                acc_bin[:, sl] += _rsum8(dhu)
                acc_bin[:, slv] += _rsum8(dhv)
                dh_ref[r, sl] = dhu.astype(BF16)
                dh_ref[r, slv] = dhv.astype(BF16)
            a_b = h_ref[r, 2 * D_A:2 * D_A + D_B]
            g_b = h_ref[r, 2 * D_A + D_B:D_IN]
            ext_ref[pl.ds(HALO_B + ci * CHUNK, CHUNK), :] = a_b * _sigmoid(g_b)

        _loop(n_chunks, chunk)

        def conv_rows(bi):
            base = bi * ROWS
            r = pl.ds(base, ROWS)
            xhat, rstd = _ln_stats(yb1_ref[r, :])
            yb2 = xhat * gb_ref[...] + bb_ref[...]
            sg = _sigmoid(yb2)
            y_ref[r, D_A:D] = (yb2 * sg).astype(BF16)
            dyb2 = dy_ref[r, D_A:D] * (sg * (1.0 + yb2 * (1.0 - sg)))
            acc_lnbg[...] += _rsum8(dyb2 * xhat)
            acc_lnbb[...] += _rsum8(dyb2)
            dyb1 = _ln_bwd(dyb2 * gb_ref[...], xhat, rstd)
            acc_cbb[...] += _rsum8(dyb1)
            dext_ref[r, :] = dyb1
            for k, tap in _taps(ext_ref[pl.ds(base, ROWS + HALO_B), :], CONV_B_OFFSETS):
                acc_cw[8 * k:8 * k + 8, :] += _rsum8(dyb1 * tap)

        _loop(tm // ROWS, conv_rows)

        def convt_rows(bi):
            base = bi * ROWS
            r = pl.ds(base, ROWS)
            dyb0 = jnp.zeros((ROWS, D_B), F32)
            for k, tap in _taps(dext_ref[pl.ds(base, ROWS + HALO_B), :], CONV_B_T_OFFSETS):
                dyb0 = dyb0 + tap * cw_ref[k:k + 1, :]
            a_b = h_ref[r, 2 * D_A:2 * D_A + D_B]
            sg = _sigmoid(h_ref[r, 2 * D_A + D_B:D_IN])
            da_b = dyb0 * sg
            dg_b = dyb0 * a_b * sg * (1.0 - sg)
            acc_bin[:, 2 * D_A:2 * D_A + D_B] += _rsum8(da_b)
            acc_bin[:, 2 * D_A + D_B:D_IN] += _rsum8(dg_b)
            dh_ref[r, 2 * D_A:2 * D_A + D_B] = da_b.astype(BF16)
            dh_ref[r, 2 * D_A + D_B:D_IN] = dg_b.astype(BF16)

        _loop(tm // ROWS, convt_rows)
        dext_ref[tm:tm + HALO_B, :] = dext_ref[0:HALO_B, :]

        acc_wout[...] += _tn(y_ref[...], dmb_ref[...])
        xt = x_ref[...].T.astype(BF16)
        dh_blocks = [dh_ref[:, j * W_IN_BLK:(j + 1) * W_IN_BLK] for j in range(N_DEV)]
        for j in range(N_DEV):
            acc_win[j] += _nn(xt, dh_blocks[j])
        gx_ref[...] += sum(_nt(dh_blocks[j], win_ref[j]) for j in range(N_DEV))

        @pl.when(i == nt - 1)
        def _():
            cps = [pltpu.make_async_copy(acc_win, dwin_ref, sem.at[0]),
                   pltpu.make_async_copy(acc_wout, dwout_ref, sem.at[1])]
            for cp in cps:
                cp.start()
            small_ref[...] = jnp.zeros(small_ref.shape, F32)

            def put_row_vector(row0, acc):
                vec = jnp.sum(acc[...], axis=0, keepdims=True)
                for k in range(vec.shape[1] // 128):
                    small_ref[row0 + k:row0 + k + 1, :] = vec[:, k * 128:(k + 1) * 128]

            put_row_vector(S_BIN, acc_bin)
            put_row_vector(S_CBB, acc_cbb)
            put_row_vector(S_LNBG, acc_lnbg)
            put_row_vector(S_LNBB, acc_lnbb)
            put_row_vector(S_BOUT, acc_bout)
            put_row_vector(S_LN1G, acc_ln1g)
            put_row_vector(S_LN1B, acc_ln1b)
            mask = _tril_mask()
            for hd in range(HEADS):
                rows8 = slice(8 * hd, 8 * hd + 8)
                small_ref[S_LNAG + hd:S_LNAG + hd + 1, :] = jnp.sum(acc_lnag[rows8, :], axis=0, keepdims=True)
                small_ref[S_LNAB + hd:S_LNAB + hd + 1, :] = jnp.sum(acc_lnab[rows8, :], axis=0, keepdims=True)
                small_ref[S_WS + hd * CHUNK:S_WS + (hd + 1) * CHUNK, :] = jnp.where(mask, acc_ws[hd], 0.0)
                small_ref[S_BS + hd:S_BS + hd + 1, :] = jnp.sum(acc_bs[hd].T, axis=0, keepdims=True)
            for k in range(KB):
                dcw_ref[k:k + 1, :] = jnp.sum(acc_cw[8 * k:8 * k + 8, :], axis=0, keepdims=True)
            for cp in cps:
                cp.wait()
            for cp in _chip_copies(p_refs, land_refs, send_sems, recv_sems):
                cp.wait()

    rev = lambda i: nt - 1 - i
    row = lambda w: pl.BlockSpec((tm, w), lambda i: (rev(i), 0))
    return pl.pallas_call(
        body, name="mix_backward", grid=(nt,),
        in_specs=[row(D), row(D_IN),
                  pl.BlockSpec((HALO_B, 2 * D_B), lambda i: (jnp.maximum(rev(i) * halo_blocks - 1, 0), 1)),
                  row(D_B), pl.BlockSpec((N_F, tm, D), lambda i: (0, rev(i), 0)),
                  row(D), row(D), row(128), _resident(win_g.shape), _full(ln_a_g.shape),
                  _full(ln_a_b.shape), _full(w_spatial.shape), _full(bst.shape), _full(conv_b_w.shape),
                  _full(ln_b_g.shape), _full(ln_b_b.shape),
                  _resident(wout.shape), _full(ln1_g.shape)] + [ANY] * n_p,
        out_specs=[row(D), ANY, ANY, _full((KB, D_B)), _full((S_MIX_ROWS, 128))] + [ANY] * n_p,
        out_shape=[jax.ShapeDtypeStruct((t, D), F32), jax.ShapeDtypeStruct((N_DEV, D, W_IN_BLK), F32),
                   jax.ShapeDtypeStruct((D, D), F32), jax.ShapeDtypeStruct((KB, D_B), F32),
                   jax.ShapeDtypeStruct((S_MIX_ROWS, 128), F32)]
        + [jax.ShapeDtypeStruct(p.shape, BF16) for p in ffn_partials],
        scratch_shapes=[pltpu.VMEM((tm + HALO_B, D_B), F32), pltpu.VMEM((tm + HALO_B, D_B), F32),
                        pltpu.VMEM((tm, D), BF16), pltpu.VMEM((tm, D), F32), pltpu.VMEM((tm, D_IN), BF16),
                        pltpu.VMEM((tm, D), BF16),
                        pltpu.VMEM((HEADS, CHUNK, CHUNK), BF16),
                        pltpu.VMEM((N_DEV, D, W_IN_BLK), F32), pltpu.VMEM((D, D), F32),
                        pltpu.VMEM((8, D_IN), F32), pltpu.VMEM((8 * HEADS, HEAD_DIM), F32),
                        pltpu.VMEM((8 * HEADS, HEAD_DIM), F32), pltpu.VMEM((HEADS, CHUNK, CHUNK), F32),
                        pltpu.VMEM((HEADS, CHUNK, CHUNK), F32), pltpu.VMEM((8, D_B), F32),
                        pltpu.VMEM((8, D_B), F32), pltpu.VMEM((8, D_B), F32), pltpu.VMEM((8, D), F32),
                        pltpu.VMEM((8, D), F32), pltpu.VMEM((8, D), F32), pltpu.VMEM((8 * KB, D_B), F32),
                        pltpu.SemaphoreType.DMA((2,)),
                        pltpu.SemaphoreType.DMA((n_p, 3)), pltpu.SemaphoreType.DMA((n_p, 3))],
        compiler_params=_params(("arbitrary",)),
    )(x, h, h, yb1, dx1p, dr2, xhat1, rstd1, win_g, ln_a_g, ln_a_b, w_spatial, bst, conv_b_w,
      ln_b_g, ln_b_b, wout, ln1_g, *ffn_partials)


def _rows128(a):
    return a.reshape(-1, 128)


def _pack_conv(cb, cf):
    lead = cb.shape[:-2]
    pad = [(0, 0)] * len(lead)
    flat = jnp.pad(cb.reshape(lead + (KB * 64,)), pad + [(0, 3 * W_UP_BLK - KB * 64)])
    rows = jnp.concatenate([cf, flat.reshape(lead + (3, W_UP_BLK))], axis=-2)
    return jnp.pad(rows, pad + [(0, 2), (0, 768 - W_UP_BLK)])


def _unpack_conv(p):
    lead = p.shape[:-2]
    cf = p[..., 0:KF, 0:W_UP_BLK]
    cb = p[..., 3:6, 0:W_UP_BLK].reshape(lead + (3 * W_UP_BLK,))[..., :KB * 64].reshape(lead + (KB, 64))
    return cb, cf


def kernel(x, w_in, b_in, ln_a_g, ln_a_b, w_spatial, b_spatial, conv_b_w, conv_b_b, ln_b_g, ln_b_b, w_out, b_out, ln1_g, ln1_b, w_up, conv_f_w, conv_f_b, w_down, ln2_g, ln2_b, loss_target, m_w_in, m_b_in, m_ln_a_g, m_ln_a_b, m_w_spatial, m_b_spatial, m_conv_b_w, m_conv_b_b, m_ln_b_g, m_ln_b_b, m_w_out, m_b_out, m_ln1_g, m_ln1_b, m_w_up, m_conv_f_w, m_conv_f_b, m_w_down, m_ln2_g, m_ln2_b, v_w_in, v_b_in, v_ln_a_g, v_ln_a_b, v_w_spatial, v_b_spatial, v_conv_b_w, v_conv_b_b, v_ln_b_g, v_ln_b_b, v_w_out, v_b_out, v_ln1_g, v_ln1_b, v_w_up, v_conv_f_w, v_conv_f_b, v_w_down, v_ln2_g, v_ln2_b):
    t = x.shape[1]
    x2 = x.reshape(t, D)
    target = loss_target.reshape(t, D)
    tm_fwd = min(t, 512)
    tm_bwd = min(t, 256)
    tm_ffn_bwd = min(t, 512)

    xi, yi, ci = _mesh_pos()
    jidx = jnp.stack([_lid(px, py, ci) for px, py in _chip_patterns(xi, yi)]).astype(jnp.int32)

    sin, sout, sup, sdown, conv_g = prepare_weights(w_in, w_out, w_up.T, w_down, _pack_conv(conv_b_w, conv_f_w))
    conv_b_all, cfw = _unpack_conv(conv_g)
    conv_b_full = conv_b_all.transpose(1, 0, 2).reshape(KB, D_B)
    cfb = conv_f_b.reshape(N_DEV, W_UP_BLK)
    row = lambda a: a.reshape(1, -1)
    bst = b_spatial.T

    h, xhat1, rstd1, yb1, win_g, wout_g, wup_g, wdown_g = mix_forward(
        x2, sin, sout, row(b_in), ln_a_g, ln_a_b, w_spatial, bst, conv_b_full, row(conv_b_b),
        row(ln_b_g), row(ln_b_b), row(b_out), row(ln1_g), row(ln1_b), sup, sdown, tm_fwd)
    wout_full = wout_g.reshape(D, D)
    wdown4 = wdown_g.reshape(N_F, W_UP_BLK, D)
    hu, gv, dr2, loss_part, s_ln2 = ffn_forward(
        xhat1, row(ln1_g), row(ln1_b), wup_g, cfw, cfb, wdown4, row(ln2_g), row(ln2_b), target, tm_bwd)

    order = jnp.where(ci == 0, jnp.array([1, 3, 0, 2], jnp.int32), jnp.array([0, 2, 1, 3], jnp.int32))
    dwup, dwdown, dcfw, dcfb, dx1p, *ffn_lands = ffn_backward(
        order, dr2, xhat1, row(ln1_g), row(ln1_b), hu, gv, wup_g, cfw, wdown4, tm_ffn_bwd)
    ffn_grads = [dwup.reshape(N_DEV, W_UP_BLK, D), dwdown.reshape(N_DEV, D_FF // N_DEV, D)]
    ffn_partials = chip_partials(ffn_grads, ffn_lands, jidx, 2)
    grad_x, dwin, dwout, dcw, s_mix, *ffn_recvs = mix_backward(
        x2, h, yb1, dx1p, dr2, xhat1, rstd1, win_g, ln_a_g, ln_a_b, w_spatial, bst,
        conv_b_full, row(ln_b_g), row(ln_b_b), wout_full, row(ln1_g), ffn_partials, tm_bwd)

    dcfb_rows = jnp.pad(dcfb.reshape(-1, 128), ((0, 4), (0, 0)))
    dconv = _pack_conv(dcw.reshape(KB, N_DEV, 64).transpose(1, 0, 2), dcfw.reshape(N_DEV, KF, W_UP_BLK))
    mix_grads = [dwin, dwout.reshape(N_DEV, D // N_DEV, D), dconv]
    mix_w = [w_in, w_out, _pack_conv(conv_b_w, conv_f_w)]
    mix_m = [m_w_in, m_w_out, _pack_conv(m_conv_b_w, m_conv_f_w)]
    mix_v = [v_w_in, v_w_out, _pack_conv(v_conv_b_w, v_conv_f_w)]
    *mix_sums, sv_slots = mixer_reduce(mix_grads, [s_mix, dcfb_rows, s_ln2, loss_part])
    big = {}

    ffn_out = reduce_and_adamw(ffn_grads, ffn_lands, ffn_recvs, [w_up.T, w_down], [m_w_up.T, m_w_down],
                               [v_w_up.T, v_w_down], jidx, 2)
    big["w_up"] = [ffn_out[2 * k].T for k in range(4)]
    big["w_down"] = [ffn_out[2 * k + 1] for k in range(4)]

    small_w = dict(b_in=b_in, ln_a_g=ln_a_g, ln_a_b=ln_a_b, w_spatial=w_spatial, b_spatial=b_spatial,
                   conv_b_b=conv_b_b, ln_b_g=ln_b_g, ln_b_b=ln_b_b, b_out=b_out, ln1_g=ln1_g,
                   ln1_b=ln1_b, conv_f_b=conv_f_b, ln2_g=ln2_g, ln2_b=ln2_b)
    small_m = dict(b_in=m_b_in, ln_a_g=m_ln_a_g, ln_a_b=m_ln_a_b, w_spatial=m_w_spatial,
                   b_spatial=m_b_spatial, conv_b_b=m_conv_b_b, ln_b_g=m_ln_b_g, ln_b_b=m_ln_b_b,
                   b_out=m_b_out, ln1_g=m_ln1_g, ln1_b=m_ln1_b, conv_f_b=m_conv_f_b, ln2_g=m_ln2_g,
                   ln2_b=m_ln2_b)
    small_v = dict(b_in=v_b_in, ln_a_g=v_ln_a_g, ln_a_b=v_ln_a_b, w_spatial=v_w_spatial,
                   b_spatial=v_b_spatial, conv_b_b=v_conv_b_b, ln_b_g=v_ln_b_g, ln_b_b=v_ln_b_b,
                   b_out=v_b_out, ln1_g=v_ln1_g, ln1_b=v_ln1_b, conv_f_b=v_conv_f_b, ln2_g=v_ln2_g,
                   ln2_b=v_ln2_b)
    order = [nm for nm, _, _ in SMALL_LAYOUT]
    small_out = small_adamw(sv_slots, [_rows128(small_w[nm]) for nm in order],
                            [_rows128(small_m[nm]) for nm in order], [_rows128(small_v[nm]) for nm in order],
                            mix_sums, mix_w, mix_m, mix_v)
    n_small = len(order)
    mix_out = small_out[4 * n_small + 1:]
    big.update({nm: [mix_out[k * 3 + p] for k in range(4)] for p, nm in enumerate(["w_in", "w_out", "conv"])})
    for k in range(4):
        cb_k, cf_k = _unpack_conv(big["conv"][k])
        big.setdefault("conv_b_w", []).append(cb_k)
        big.setdefault("conv_f_w", []).append(cf_k)
    small = {nm: [small_out[k * n_small + p].reshape(small_w[nm].shape) for k in range(4)]
             for p, nm in enumerate(order)}
    loss = jnp.sum(small_out[4 * n_small]) * (0.5 / D)

    weights = ["w_in", "b_in", "ln_a_g", "ln_a_b", "w_spatial", "b_spatial", "conv_b_w", "conv_b_b",
               "ln_b_g", "ln_b_b", "w_out", "b_out", "ln1_g", "ln1_b", "w_up", "conv_f_w", "conv_f_b",
               "w_down", "ln2_g", "ln2_b"]
    result = lambda nm, k: big[nm][k] if nm in big else small[nm][k]
    return (loss, grad_x.reshape(x.shape), *[result(nm, 0) for nm in weights],
            *[result(nm, 1) for nm in weights], *[result(nm, 2) for nm in weights],
            *[result(nm, 3) for nm in weights])
```

```python
import functools
import math

import jax
import jax.numpy as jnp
from jax import lax
from jax.experimental import pallas as pl
from jax.experimental.pallas import tpu as pltpu

F32 = jnp.float32
BF16 = jnp.bfloat16

D = 1024
D_A = 512
D_B = 512
HEADS = 4
HEAD_DIM = 128
CHUNK = 128
KB = 31
KF = 3
D_FF = 2816
D_IN = 2048
N_DEV = 8
W_IN_BLK = D_IN // N_DEV
W_UP_BLK = 2 * D_FF // N_DEV
N_F = 4
LN_EPS = 1e-5
ALPHA = 2.0 ** 0.25

ADAM_LR = 0.001
ADAM_B1 = 0.9
ADAM_B2 = 0.999
ADAM_EPS = 1e-08
ADAM_WD = 0.01
ADAM_STEP = 10

INV_SQRT2 = 1.0 / math.sqrt(2.0)
INV_SQRT_2PI = 1.0 / math.sqrt(2.0 * math.pi)

HALO_B = 32
HALO_F = 8
ROWS = 64
LN_ROWS = 32
VMEM_LIMIT = 58 * 1024 * 1024

MESH = pl.DeviceIdType.MESH
ANY = pl.BlockSpec(memory_space=pl.ANY)
VMEM = pl.BlockSpec(memory_space=pltpu.VMEM)

S_BIN, S_LNAG, S_LNAB, S_WS, S_BS, S_CBB, S_LNBG, S_LNBB, S_BOUT, S_LN1G, S_LN1B = (
    0, 16, 24, 32, 544, 552, 560, 568, 576, 584, 592)
S_MIX_ROWS = 600
S_CFB = 600
S_LN2G = 648
S_LN2B = 656
S_LOSS = 664
S_ROWS = 672


def _tn(a, b):
    return lax.dot_general(a, b, (((0,), (0,)), ((), ())), preferred_element_type=F32)


def _nt(a, b):
    return lax.dot_general(a, b, (((1,), (1,)), ((), ())), preferred_element_type=F32)


def _nn(a, b):
    return jnp.dot(a, b, preferred_element_type=F32)


def _sigmoid(x):
    return 1.0 / (1.0 + jnp.exp(-x))


def _ln_stats(x):
    mu = jnp.mean(x, axis=-1, keepdims=True)
    xc = x - mu
    var = jnp.mean(xc * xc, axis=-1, keepdims=True)
    rstd = lax.rsqrt(var + LN_EPS)
    return xc * rstd, rstd


def _ln_bwd(dxhat, xhat, rstd):
    m1 = jnp.mean(dxhat, axis=-1, keepdims=True)
    m2 = jnp.mean(dxhat * xhat, axis=-1, keepdims=True)
    return rstd * (dxhat - m1 - xhat * m2)


def _rsum8(x):
    r, n = x.shape
    return x.reshape(r // 8, 8, n).sum(axis=0)


def _rows(i, n=ROWS):
    return pl.ds(i * n, n)


def _loop(n, body):
    for i in range(n):
        body(i)


def _tril_mask():
    r = lax.broadcasted_iota(jnp.int32, (CHUNK, CHUNK), 0)
    c = lax.broadcasted_iota(jnp.int32, (CHUNK, CHUNK), 1)
    return c <= r


def _mixer_a_head(h_ref, r, hd, ga_ref, ba_ref, wsm_ref, bst_ref):
    sl = slice(hd * HEAD_DIM, (hd + 1) * HEAD_DIM)
    hu = h_ref[r, sl]
    hv = h_ref[r, D_A + hd * HEAD_DIM:D_A + (hd + 1) * HEAD_DIM]
    cdf_u = 0.5 * (1.0 + lax.erf(hu * INV_SQRT2))
    cdf_v = 0.5 * (1.0 + lax.erf(hv * INV_SQRT2))
    u = hu * cdf_u
    xhat, rstd = _ln_stats(hv * cdf_v)
    vn = (xhat * ga_ref[hd:hd + 1, :] + ba_ref[hd:hd + 1, :]).astype(BF16)
    sv = _nn(wsm_ref[hd], vn) + bst_ref[:, hd:hd + 1]
    return hu, hv, u, cdf_u, cdf_v, xhat, rstd, vn, sv


def _taps(win, offsets):
    n = win.shape[0]
    for s in range(8):
        ks = [k for k, o in enumerate(offsets) if o % 8 == s]
        if ks:
            moved = win if s == 0 else pltpu.roll(win, n - s, 0)
            for k in ks:
                yield k, moved[offsets[k] - s:offsets[k] - s + ROWS, :]


CONV_B_OFFSETS = [2 + k for k in range(KB)]
CONV_B_T_OFFSETS = [30 - k for k in range(KB)]


def _conv_b_block(ext_ref, base, cw_ref):
    acc = jnp.zeros((ROWS, D_B), F32)
    for k, tap in _taps(ext_ref[pl.ds(base, ROWS + HALO_B), :], CONV_B_OFFSETS):
        acc = acc + tap * cw_ref[k:k + 1, :]
    return acc


def _taps_f(win):
    n = ROWS + HALO_F
    return [pltpu.roll(win, n - 6, 0)[0:ROWS, :], pltpu.roll(win, n - 7, 0)[0:ROWS, :], win[8:n, :]]


def _params(sem, **kw):
    return pltpu.CompilerParams(dimension_semantics=sem, vmem_limit_bytes=VMEM_LIMIT, **kw)


def _resident(shape):
    zeros = (0,) * len(shape)
    return pl.BlockSpec(shape, lambda *_: zeros, pipeline_mode=pl.Buffered(1))


def _full(shape):
    zeros = (0,) * len(shape)
    return pl.BlockSpec(shape, lambda *_: zeros)


def _mesh_pos():
    return lax.axis_index("x"), lax.axis_index("y"), lax.axis_index("c")


def _chip_patterns(x, y):
    return [(x, y), (1 - x, y), (x, 1 - y), (1 - x, 1 - y)]


def _lid(x, y, c):
    return 4 * x + 2 * y + c


def _gather_copy(outs, send_sems, recv_sems, a, k, block, to, src=None):
    blk = outs[a].at[_lid(*block)]
    return pltpu.make_async_remote_copy(
        src_ref=blk if src is None else src, dst_ref=blk,
        send_sem=send_sems.at[a, k], recv_sem=recv_sems.at[a, k], device_id=to, device_id_type=MESH)


def _gather_start(mine, outs, send_sems, recv_sems, local_sems, diagonal=False):
    x, y, c = _mesh_pos()
    me = (x, y, c)
    for a in range(len(mine)):
        pltpu.make_async_copy(mine[a], outs[a].at[_lid(*me)], local_sems.at[a]).start()
        targets = [(x, y, 1 - c), (1 - x, y, c), (x, 1 - y, c), (1 - x, 1 - y, c)]
        for k, to in enumerate(targets if diagonal else targets[:3]):
            _gather_copy(outs, send_sems, recv_sems, a, k, me, to, src=mine[a]).start()


def _gather_relay(mine, outs, send_sems, recv_sems, local_sems, via):
    x, y, c = _mesh_pos()
    me, sib = (x, y, c), (x, y, 1 - c)
    copy = functools.partial(_gather_copy, outs, send_sems, recv_sems)
    source = {1: (1 - x, y, c), 2: (x, 1 - y, c)}
    for a in range(len(mine)):
        for k in ((via[a], 3 - via[a]) if via[a] else (1, 2)):
            copy(a, k, source[k], me).wait_recv()
            if k == via[a]:
                copy(a, 3, source[k], source[3 - k]).start()
            copy(a, 3 + k, source[k], sib).start()


def _gather_finish(mine, outs, send_sems, recv_sems, local_sems):
    x, y, c = _mesh_pos()
    me, sib = (x, y, c), (x, y, 1 - c)
    copy = functools.partial(_gather_copy, outs, send_sems, recv_sems)
    diag = (1 - x, 1 - y)
    n = len(mine)
    for a in range(n):
        copy(a, 3, (*diag, c), me).wait_recv()
        copy(a, 6, (*diag, c), sib).start()
    for a in range(n):
        copy(a, 0, sib, me).wait_recv()
        for k, chip in zip((4, 5, 6), [(1 - x, y), (x, 1 - y), diag]):
            copy(a, k, (*chip, 1 - c), me).wait_recv()
        for k in range(7):
            copy(a, k, me, sib, src=mine[a]).wait_send()
        pltpu.make_async_copy(mine[a], outs[a].at[_lid(*me)], local_sems.at[a]).wait()


def _gather_scratch(n):
    return [pltpu.SemaphoreType.DMA((n, 7)), pltpu.SemaphoreType.DMA((n, 7)), pltpu.SemaphoreType.DMA((n,))]


def prepare_weights(w_in, w_out, w_up_t, w_down, convp):
    def body(win_ref, wout_ref, wup_ref, wdown_ref, convp_ref,
             sin_ref, sout_ref, sup_ref, sdown_ref, gconv_ref, send_sems, recv_sems, local_sems):
        gather = ([convp_ref], [gconv_ref], send_sems, recv_sems, local_sems)
        _gather_start(*gather, diagonal=True)
        sin_ref[...] = win_ref[...].astype(BF16)
        sout_ref[...] = wout_ref[...].astype(BF16)
        sup_ref[...] = wup_ref[...].T.astype(BF16)
        sdown_ref[...] = wdown_ref[...].astype(BF16)
        _gather_relay(*gather, via=[0])
        _gather_finish(*gather)

    return pl.pallas_call(
        body, name="prepare_weights",
        out_shape=[jax.ShapeDtypeStruct(w_in.shape, BF16), jax.ShapeDtypeStruct(w_out.shape, BF16),
                   jax.ShapeDtypeStruct(w_up_t.shape[::-1], BF16), jax.ShapeDtypeStruct(w_down.shape, BF16),
                   jax.ShapeDtypeStruct((N_DEV,) + convp.shape, F32)],
        in_specs=[VMEM] * 5, out_specs=[VMEM] * 4 + [ANY],
        scratch_shapes=_gather_scratch(1),
        compiler_params=pltpu.CompilerParams(vmem_limit_bytes=VMEM_LIMIT),
    )(w_in, w_out, w_up_t, w_down, convp)


def _chip_copies(p, land, send_sems, recv_sems):
    x, y, c = _mesh_pos()
    return [pltpu.make_async_remote_copy(
        src_ref=p[a].at[k], dst_ref=land[a].at[k], send_sem=send_sems.at[a, k], recv_sem=recv_sems.at[a, k],
        device_id=(px, py, c), device_id_type=MESH)
        for k, (px, py) in enumerate(_chip_patterns(x, y)[1:]) for a in range(len(p))]


def chip_partials(gs, lands, jidx, steps):
    n = len(gs)
    blocks = [(1, g.shape[1] // steps, g.shape[2]) for g in gs]

    def body(j_ref, *refs):
        for a in range(n):
            refs[2 * n + a][...] = (refs[a][...] + refs[n + a][...]).astype(BF16)

    return pl.pallas_call(
        body, name="chip_partials",
        out_shape=[jax.ShapeDtypeStruct((3,) + g.shape[1:], BF16) for g in gs],
        grid_spec=pltpu.PrefetchScalarGridSpec(
            num_scalar_prefetch=1, grid=(3, steps),
            in_specs=[pl.BlockSpec(b, lambda k, i, j: (j[1 + k], i, 0)) for b in blocks]
            + [pl.BlockSpec(b, lambda k, i, j: (1 + k, i, 0)) for b in blocks],
            out_specs=[pl.BlockSpec(b, lambda k, i, j: (k, i, 0)) for b in blocks]),
        compiler_params=_params(("arbitrary", "arbitrary")),
    )(jidx, *gs, *lands)


def _adamw(w, g, m, v):
    m2 = ADAM_B1 * m + (1.0 - ADAM_B1) * g
    v2 = ADAM_B2 * v + (1.0 - ADAM_B2) * (g * g)
    m_hat = m2 / (1.0 - ADAM_B1 ** ADAM_STEP)
    v_hat = v2 / (1.0 - ADAM_B2 ** ADAM_STEP)
    delta = -ADAM_LR * (m_hat / (jnp.sqrt(v_hat) + ADAM_EPS) + ADAM_WD * w)
    return delta, m2, v2


def reduce_and_adamw(gs, lands, recvs, ws, ms, vs, jidx, steps):
    n = len(gs)
    rbs = [g.shape[1] // steps for g in gs]

    def body(j_ref, *refs):
        g, land, recv, w, m, v = (refs[k * n:(k + 1) * n] for k in range(6))
        outs = refs[6 * n:]
        for a in range(n):
            grad = ((g[a][0] + land[a][0]) + recv[a][0].astype(F32) + recv[a][1].astype(F32)
                    + recv[a][2].astype(F32))
            delta, m2, v2 = _adamw(w[a][...], grad, m[a][...], v[a][...])
            outs[a][...] = grad
            outs[n + a][...] = delta
            outs[2 * n + a][...] = m2
            outs[3 * n + a][...] = v2

    blk = [pl.BlockSpec((rb, g.shape[2]), lambda i, j: (i, 0)) for g, rb in zip(gs, rbs)]
    part = lambda lead, pick: [pl.BlockSpec((lead, rb, g.shape[2]), pick) for g, rb in zip(gs, rbs)]
    return pl.pallas_call(
        body, name="reduce_adamw",
        out_shape=[jax.ShapeDtypeStruct(g.shape[1:], F32) for g in gs] * 4,
        grid_spec=pltpu.PrefetchScalarGridSpec(
            num_scalar_prefetch=1, grid=(steps,),
            in_specs=part(1, lambda i, j: (j[0], i, 0)) + part(1, lambda i, j: (0, i, 0))
            + part(3, lambda i, j: (0, i, 0)) + blk * 3,
            out_specs=blk * 4),
        compiler_params=_params(("arbitrary",)),
    )(jidx, *gs, *lands, *recvs, *ws, *ms, *vs)


def mixer_reduce(grads, sv_parts):
    n = len(grads)
    shard = [g.shape[1:] for g in grads]
    ns = len(sv_parts)
    sv_shape = (sum(p.shape[0] for p in sv_parts), 128)

    def body(*refs):
        g = refs[:n]
        parts = refs[n:n + ns]
        outs = refs[n + ns:2 * n + ns]
        sv_slots = refs[2 * n + ns]
        rest = refs[2 * n + ns + 1:-1]
        sv_ref = refs[-1]
        row0 = 0
        for part in parts:
            sv_ref[row0:row0 + part.shape[0], :] = part[...]
            row0 += part.shape[0]
        own, land, sendb, recvb = rest[:n], rest[n:2 * n], rest[2 * n:3 * n], rest[3 * n:4 * n]
        sv_land, chip_sv, d2d_send, d2d_recv, ici_send, ici_recv, local_sems, sv_sems = rest[4 * n:]
        x, y, c = _mesh_pos()
        sib = (x, y, 1 - c)
        pats = _chip_patterns(x, y)
        q = 2 * x + y

        d2d, local = {}, {}
        for a in range(n):
            for k, (px, py) in enumerate(pats):
                d2d[a, k] = pltpu.make_async_remote_copy(
                    src_ref=g[a].at[_lid(px, py, 1 - c)], dst_ref=land[a].at[k],
                    send_sem=d2d_send.at[a, k], recv_sem=d2d_recv.at[a, k], device_id=sib, device_id_type=MESH)
                local[a, k] = pltpu.make_async_copy(g[a].at[_lid(px, py, c)], own[a].at[k], local_sems.at[a, k])
        sv_d2d = pltpu.make_async_remote_copy(
            src_ref=sv_ref, dst_ref=sv_land, send_sem=d2d_send.at[n, 0], recv_sem=d2d_recv.at[n, 0],
            device_id=sib, device_id_type=MESH)
        blocks = [(a, k) for a in range(n) for k in (1, 2, 3)] + [(a, 0) for a in range(n)]
        sv_d2d.start()
        for b in blocks:
            d2d[b].start()
            local[b].start()

        half_rows = sv_shape[0] // 2
        rows = pl.ds(pl.multiple_of(c * half_rows, 8), half_rows)
        sv_local = pltpu.make_async_copy(chip_sv, sv_slots.at[q], sv_sems.at[0])

        def sv_ici(k, slot, to):
            return pltpu.make_async_remote_copy(
                src_ref=chip_sv.at[rows], dst_ref=sv_slots.at[slot, rows], send_sem=sv_sems.at[1 + k],
                recv_sem=sv_sems.at[4 + k], device_id=to, device_id_type=MESH)

        def sv_pass_on(k, slot):
            return pltpu.make_async_remote_copy(
                src_ref=sv_slots.at[slot, rows], dst_ref=sv_slots.at[slot, rows], send_sem=sv_sems.at[7 + k],
                recv_sem=sv_sems.at[10 + k], device_id=sib, device_id_type=MESH)

        sv_d2d.wait()
        chip_sv[...] = sv_ref[...] + sv_land[...]
        sv_out = [sv_ici(k, q, (px, py, c)) for k, (px, py) in enumerate(pats[1:])]
        for cp in sv_out + [sv_local]:
            cp.start()

        ici = _chip_copies(sendb, recvb, ici_send, ici_recv)
        for a, k in blocks:
            local[a, k].wait()
            d2d[a, k].wait()
            if k > 0:
                sendb[a][k - 1] = (own[a][k] + land[a][k]).astype(BF16)
                ici[(k - 1) * n + a].start()
        for k, (px, py) in enumerate(pats[1:]):
            sv_out[k].wait_send()
            sv_ici(k, 2 * px + py, (px, py, c)).wait_recv()
            sv_pass_on(k, 2 * px + py).start()
        for a in range(n):
            for k in range(3):
                ici[k * n + a].wait()
            outs[a][...] = ((own[a][0] + land[a][0]) + recvb[a][0].astype(F32) + recvb[a][1].astype(F32)
                            + recvb[a][2].astype(F32))
        for k, (px, py) in enumerate(pats[1:]):
            sv_pass_on(k, 2 * px + py).wait()
        sv_local.wait()

    shard_out = [jax.ShapeDtypeStruct(s, F32) for s in shard]
    return pl.pallas_call(
        body, name="mixer_reduce",
        out_shape=shard_out + [jax.ShapeDtypeStruct((4,) + sv_shape, F32)],
        in_specs=[ANY] * n + [VMEM] * ns, out_specs=[VMEM] * n + [ANY],
        scratch_shapes=[pltpu.VMEM((4,) + s, F32) for s in shard] + [pltpu.VMEM((4,) + s, F32) for s in shard]
        + [pltpu.VMEM((3,) + s, BF16) for s in shard] + [pltpu.VMEM((3,) + s, BF16) for s in shard]
        + [pltpu.VMEM(sv_shape, F32), pltpu.VMEM(sv_shape, F32),
           pltpu.SemaphoreType.DMA((n + 1, 4)), pltpu.SemaphoreType.DMA((n + 1, 4)),
           pltpu.SemaphoreType.DMA((n, 3)), pltpu.SemaphoreType.DMA((n, 3)),
           pltpu.SemaphoreType.DMA((n, 4)), pltpu.SemaphoreType.DMA((13,)), pltpu.VMEM(sv_shape, F32)],
        compiler_params=pltpu.CompilerParams(vmem_limit_bytes=VMEM_LIMIT),
    )(*grads, *sv_parts)


SMALL_LAYOUT = [
    ("b_in", S_BIN, 16), ("ln_a_g", S_LNAG, 4), ("ln_a_b", S_LNAB, 4), ("w_spatial", S_WS, 512),
    ("b_spatial", S_BS, 4), ("conv_b_b", S_CBB, 4), ("ln_b_g", S_LNBG, 4), ("ln_b_b", S_LNBB, 4),
    ("b_out", S_BOUT, 8), ("ln1_g", S_LN1G, 8), ("ln1_b", S_LN1B, 8), ("conv_f_b", S_CFB, 44),
    ("ln2_g", S_LN2G, 8), ("ln2_b", S_LN2B, 8),
]


def small_adamw(sv_slots, ws, ms, vs, shard_grads, shard_ws, shard_ms, shard_vs):
    n = len(SMALL_LAYOUT)
    nb = len(shard_grads)

    def body(*refs):
        s_ref = refs[0]
        w_refs, m_refs, v_refs = refs[1:1 + n], refs[1 + n:1 + 2 * n], refs[1 + 2 * n:1 + 3 * n]
        big_in = refs[1 + 3 * n:1 + 3 * n + 4 * nb]
        outs = refs[1 + 3 * n + 4 * nb:]
        big_out = outs[4 * n + 1:]

        def shard_step(p):
            grad = big_in[p][...]
            delta, m2, v2 = _adamw(big_in[nb + p][...], grad, big_in[2 * nb + p][...], big_in[3 * nb + p][...])
            big_out[p][...] = grad
            big_out[nb + p][...] = delta
            big_out[2 * nb + p][...] = m2
            big_out[3 * nb + p][...] = v2

        shard_step(0)

        @pl.when(pl.program_id(0) == 0)
        def _():
            for p in range(1, nb):
                shard_step(p)
            for p, (_, row0, rows) in enumerate(SMALL_LAYOUT):
                sl = pl.ds(row0, rows)
                grad = ((s_ref[0, sl, :] + s_ref[1, sl, :]) + s_ref[2, sl, :]) + s_ref[3, sl, :]
                delta, m2, v2 = _adamw(w_refs[p][...], grad, m_refs[p][...], v_refs[p][...])
                outs[p][...] = grad
                outs[n + p][...] = delta
                outs[2 * n + p][...] = m2
                outs[3 * n + p][...] = v2
            sl = pl.ds(S_LOSS, 8)
            outs[4 * n][...] = ((s_ref[0, sl, :] + s_ref[1, sl, :]) + s_ref[2, sl, :]) + s_ref[3, sl, :]

    steps = 4
    shapes = [(rows, 128) for _, _, rows in SMALL_LAYOUT]
    first = shard_grads[0].shape
    big_specs = [pl.BlockSpec((first[0] // steps,) + first[1:], lambda i: (i,) + (0,) * (len(first) - 1))]
    big_specs += [_full(g.shape) for g in shard_grads[1:]]
    return pl.pallas_call(
        body, name="small_adamw", grid=(steps,),
        out_shape=[jax.ShapeDtypeStruct(s, F32) for s in shapes] * 4 + [jax.ShapeDtypeStruct((8, 128), F32)]
        + [jax.ShapeDtypeStruct(g.shape, F32) for g in shard_grads] * 4,
        in_specs=[_full(sv_slots.shape)] + [_full(s) for s in shapes] * 3 + big_specs * 4,
        out_specs=[_full(s) for s in shapes] * 4 + [_full((8, 128))] + big_specs * 4,
        compiler_params=_params(("arbitrary",)),
    )(sv_slots, *ws, *ms, *vs, *shard_grads, *shard_ws, *shard_ms, *shard_vs)


def mix_forward(x, sin, sout, b_in, ln_a_g, ln_a_b, w_spatial, bst, conv_b_w, conv_b_b, ln_b_g, ln_b_b,
                b_out, ln1_g, ln1_b, sup, sdown, tm):
    t = x.shape[0]
    nt = t // tm
    n_chunks = tm // CHUNK

    def body(x_ref, sin_ref, sout_ref, bin_ref, ga_ref, ba_ref, ws_ref, bst_ref, cw_ref, cb_ref, gb_ref,
             bb_ref, bout_ref, g1_ref, b1_ref, sup_ref, sdown_ref,
             h_ref, xhat1_ref, rstd1_ref, yb1_ref, gin_ref, gout_ref, gup_ref, gdown_ref,
             ext_ref, y_ref, wsm_ref, win_ref, wout_ref, load_sems,
             mix_send, mix_recv, mix_local, send_sems, recv_sems, local_sems):
        i = pl.program_id(0)
        mixer = ([sin_ref, sout_ref], [gin_ref, gout_ref], mix_send, mix_recv, mix_local)
        gather = ([sup_ref, sdown_ref], [gup_ref, gdown_ref], send_sems, recv_sems, local_sems)

        @pl.when(i == 0)
        def _():
            _gather_start(*mixer)
            _gather_relay(*mixer, via=[2, 2])
            _gather_finish(*mixer)
            _gather_start(*gather)
            loads = [pltpu.make_async_copy(gin_ref, win_ref, load_sems.at[0]),
                     pltpu.make_async_copy(gout_ref, wout_ref, load_sems.at[1])]
            for cp in loads:
                cp.start()
            for cp in loads:
                cp.wait()
            ext_ref[0:HALO_B, :] = jnp.zeros((HALO_B, D_B), F32)
            mask = _tril_mask()
            for hd in range(HEADS):
                wsm_ref[hd] = jnp.where(mask, ws_ref[hd], 0.0).astype(BF16)

        xb = x_ref[...].astype(BF16)
        for j in range(N_DEV):
            cols = slice(j * W_IN_BLK, (j + 1) * W_IN_BLK)
            h_ref[:, cols] = _nn(xb, win_ref[j]) + bin_ref[:, cols]

        def chunk(ci):
            r = _rows(ci, CHUNK)
            for hd in range(HEADS):
                _, _, u, _, _, _, _, _, sv = _mixer_a_head(h_ref, r, hd, ga_ref, ba_ref, wsm_ref, bst_ref)
                y_ref[r, hd * HEAD_DIM:(hd + 1) * HEAD_DIM] = (u * sv).astype(BF16)
            a_b = h_ref[r, 2 * D_A:2 * D_A + D_B]
            g_b = h_ref[r, 2 * D_A + D_B:D_IN]
            ext_ref[pl.ds(HALO_B + ci * CHUNK, CHUNK), :] = a_b * _sigmoid(g_b)

        _loop(n_chunks, chunk)

        def conv_rows(bi):
            base = bi * ROWS
            yb1 = _conv_b_block(ext_ref, base, cw_ref) + cb_ref[...]
            yb1_ref[pl.ds(base, ROWS), :] = yb1
            xhat, _ = _ln_stats(yb1)
            yb2 = xhat * gb_ref[...] + bb_ref[...]
            y_ref[pl.ds(base, ROWS), D_A:D] = (yb2 * _sigmoid(yb2)).astype(BF16)

        _loop(tm // ROWS, conv_rows)
        ext_ref[0:HALO_B, :] = ext_ref[tm:tm + HALO_B, :]

        mix = _nn(y_ref[...], wout_ref[...].reshape(D, D)) + bout_ref[...]
        xhat1, rstd1 = _ln_stats(ALPHA * x_ref[...] + mix)
        xhat1_ref[...] = xhat1
        rstd1_ref[...] = jnp.broadcast_to(rstd1, (tm, 128))

        @pl.when(i == (5 * nt) // 8)
        def _():
            _gather_relay(*gather, via=[1, 2])

        @pl.when(i == nt - 1)
        def _():
            _gather_finish(*gather)

    row = lambda w: pl.BlockSpec((tm, w), lambda i: (i, 0))
    return pl.pallas_call(
        body, name="mix_forward", grid=(nt,),
        in_specs=[row(D), ANY, ANY, _full(b_in.shape), _full(ln_a_g.shape),
                  _full(ln_a_b.shape), _full(w_spatial.shape), _full(bst.shape),
                  _full(conv_b_w.shape), _full(conv_b_b.shape), _full(ln_b_g.shape),
                  _full(ln_b_b.shape), _full(b_out.shape),
                  _full(ln1_g.shape), _full(ln1_b.shape), ANY, ANY],
        out_specs=[row(D_IN), row(D), row(128), row(D_B), ANY, ANY, ANY, ANY],
        out_shape=[jax.ShapeDtypeStruct((t, D_IN), F32), jax.ShapeDtypeStruct((t, D), F32),
                   jax.ShapeDtypeStruct((t, 128), F32), jax.ShapeDtypeStruct((t, D_B), F32)]
        + [jax.ShapeDtypeStruct((N_DEV,) + sh.shape, BF16) for sh in (sin, sout, sup, sdown)],
        scratch_shapes=[pltpu.VMEM((tm + HALO_B, D_B), F32), pltpu.VMEM((tm, D), BF16),
                        pltpu.VMEM((HEADS, CHUNK, CHUNK), BF16),
                        pltpu.VMEM((N_DEV,) + sin.shape, BF16), pltpu.VMEM((N_DEV,) + sout.shape, BF16),
                        pltpu.SemaphoreType.DMA((2,))] + _gather_scratch(2) + _gather_scratch(2),
        compiler_params=_params(("arbitrary",)),
    )(x, sin, sout, b_in, ln_a_g, ln_a_b, w_spatial, bst, conv_b_w, conv_b_b, ln_b_g, ln_b_b,
      b_out, ln1_g, ln1_b, sup, sdown)


def ffn_forward(xhat1, ln1_g, ln1_b, wup_g, cfw, cfb, wdown, ln2_g, ln2_b, target, tm):
    t = xhat1.shape[0]
    nt = t // tm

    def body(xh_ref, g1_ref, b1_ref, wup_ref, cfw_ref, cfb_ref, wdown_ref, g2_ref, b2_ref, tgt_ref,
             hu_ref, gv_ref, dr2_ref, loss_ref, sln2_ref,
             x1_ref, x1b_ref, hu32_ref, carry_ref, gbuf_ref, ffn_ref, acc_loss, acc_g2, acc_b2):
        i = pl.program_id(0)

        @pl.when(i == 0)
        def _():
            carry_ref[...] = jnp.zeros(carry_ref.shape, F32)
            acc_loss[...] = jnp.zeros(acc_loss.shape, F32)
            acc_g2[...] = jnp.zeros(acc_g2.shape, F32)
            acc_b2[...] = jnp.zeros(acc_b2.shape, F32)

        x1 = xh_ref[...] * g1_ref[...] + b1_ref[...]
        x1_ref[...] = x1
        x1b_ref[...] = x1.astype(BF16)

        def conv(g, j, base):
            if base == 0:
                win = jnp.concatenate([carry_ref[j], hu32_ref[g, 0:ROWS, :]], axis=0)
            else:
                win = hu32_ref[g, base - HALO_F:base + ROWS, :]
            taps = _taps_f(win)
            w = cfw_ref[j]
            return sum(taps[k] * w[k:k + 1, :] for k in range(KF)) + cfb_ref[j:j + 1, :]

        for f in range(N_F):
            hu32_ref[0] = _nn(x1b_ref[...], wup_ref[f])
            hu32_ref[1] = _nn(x1b_ref[...], wup_ref[N_F + f])

            def rows(bi, f=f):
                r = _rows(bi)
                gate = conv(0, f, bi * ROWS)
                val = conv(1, N_F + f, bi * ROWS)
                gbuf_ref[r, :] = (gate * _sigmoid(gate) * val).astype(BF16)
                gv_ref[f, r, :] = gate.astype(BF16)
                gv_ref[N_F + f, r, :] = val.astype(BF16)
                hu_ref[f, r, :] = hu32_ref[0, r, :].astype(BF16)
                hu_ref[N_F + f, r, :] = hu32_ref[1, r, :].astype(BF16)

            _loop(tm // ROWS, rows)
            carry_ref[f] = hu32_ref[0, tm - HALO_F:tm, :]
            carry_ref[N_F + f] = hu32_ref[1, tm - HALO_F:tm, :]
            part = _nn(gbuf_ref[...], wdown_ref[f])
            if f == 0:
                ffn_ref[...] = part
            else:
                ffn_ref[...] += part

        def tail(bi):
            r = _rows(bi, LN_ROWS)
            xhat2, rstd2 = _ln_stats(ALPHA * x1_ref[r, :] + ffn_ref[r, :])
            err = xhat2 * g2_ref[...] + b2_ref[...] - tgt_ref[r, :]
            e2 = _rsum8(err * err)
            acc_loss[...] += sum(e2[:, k * 128:(k + 1) * 128] for k in range(D // 128))
            dy = err * (1.0 / D)
            acc_g2[...] += _rsum8(dy * xhat2)
            acc_b2[...] += _rsum8(dy)
            dr2_ref[r, :] = _ln_bwd(dy * g2_ref[...], xhat2, rstd2)

        _loop(tm // LN_ROWS, tail)
        loss_ref[...] = acc_loss[...]

        @pl.when(i == nt - 1)
        def _():
            dg = jnp.sum(acc_g2[...], axis=0, keepdims=True)
            db = jnp.sum(acc_b2[...], axis=0, keepdims=True)
            for k in range(D // 128):
                sln2_ref[k:k + 1, :] = dg[:, k * 128:(k + 1) * 128]
                sln2_ref[8 + k:9 + k, :] = db[:, k * 128:(k + 1) * 128]

    row = pl.BlockSpec((tm, D), lambda i: (i, 0))
    return pl.pallas_call(
        body, name="ffn_forward", grid=(nt,),
        in_specs=[row, _full(ln1_g.shape), _full(ln1_b.shape), _resident(wup_g.shape),
                  _full(cfw.shape), _full(cfb.shape), _resident(wdown.shape),
                  _full(ln2_g.shape), _full(ln2_b.shape), row],
        out_specs=[pl.BlockSpec((N_DEV, tm, W_UP_BLK), lambda i: (0, i, 0)),
                   pl.BlockSpec((N_DEV, tm, W_UP_BLK), lambda i: (0, i, 0)), row,
                   _full((8, 128)), _full((16, 128))],
        out_shape=[jax.ShapeDtypeStruct((N_DEV, t, W_UP_BLK), BF16),
                   jax.ShapeDtypeStruct((N_DEV, t, W_UP_BLK), BF16), jax.ShapeDtypeStruct((t, D), F32),
                   jax.ShapeDtypeStruct((8, 128), F32), jax.ShapeDtypeStruct((16, 128), F32)],
        scratch_shapes=[pltpu.VMEM((tm, D), F32), pltpu.VMEM((tm, D), BF16),
                        pltpu.VMEM((2, tm, W_UP_BLK), F32),
                        pltpu.VMEM((N_DEV, HALO_F, W_UP_BLK), F32), pltpu.VMEM((tm, W_UP_BLK), BF16),
                        pltpu.VMEM((tm, D), F32), pltpu.VMEM((8, 128), F32),
                        pltpu.VMEM((8, D), F32), pltpu.VMEM((8, D), F32)],
        compiler_params=_params(("arbitrary",)),
    )(xhat1, ln1_g, ln1_b, wup_g, cfw, cfb, wdown, ln2_g, ln2_b, target)


def ffn_backward(order, dr2, xhat1, ln1_g, ln1_b, hu, gv, wup_g, cfw, wdown, tm):
    t = dr2.shape[0]
    nt = t // tm
    sub_rows = tm
    hu4 = hu.reshape(2, N_F, t, W_UP_BLK)
    gv4 = gv.reshape(2, N_F, t, W_UP_BLK)
    wup4 = wup_g.reshape(2, N_F, D, W_UP_BLK)
    cfw4 = cfw.reshape(2, N_F, KF, W_UP_BLK)

    def body(order_ref, dr2_ref, xh_ref, g1_ref, b1_ref, hu_ref, gv_ref, wup_ref, cfw_ref, wdown_ref,
             dwup_ref, dwdown_ref, dcfw_ref, dcfb_ref, dx1_ref, land_up_ref, land_down_ref,
             x1b_ref, drb_ref, dg_ref, dextg_ref, dextv_ref, gbuf_ref,
             dhug_ref, dhuv_ref, acc_wup, acc_wdown, acc_cfw, acc_cfb, sem, send_sems, recv_sems):
        fo = pl.program_id(0)
        f = order_ref[fo]
        f_prev = order_ref[jnp.maximum(fo - 1, 0)]
        slot = fo % 2
        i = pl.program_id(1)
        x, y, c = _mesh_pos()
        half = D_FF // N_DEV

        def to_sibling(fi, k, src, land_ref, shard_chip):
            d = jnp.bitwise_xor(shard_chip, 2 * x + y)
            slot = jnp.where(d == 1, 2, jnp.where(d == 2, 1, d))
            return pltpu.make_async_remote_copy(
                src_ref=src, dst_ref=land_ref.at[slot], send_sem=send_sems.at[fi, k], recv_sem=recv_sems.at[fi, k],
                device_id=(x, y, 1 - c), device_id_type=MESH)

        def up_copy(fi, g):
            return to_sibling(fi, g, dwup_ref.at[g, fi], land_up_ref, 2 * g + fi // 2)

        def down_copy(fi):
            return to_sibling(fi, 2, dwdown_ref.at[fi, pl.ds((1 - c) * half, half)], land_down_ref, fi)

        def flush(fi, s):
            return [pltpu.make_async_copy(acc_wup.at[s, 0], dwup_ref.at[0, fi], sem.at[s, 0]),
                    pltpu.make_async_copy(acc_wup.at[s, 1], dwup_ref.at[1, fi], sem.at[s, 1]),
                    pltpu.make_async_copy(acc_wdown.at[s], dwdown_ref.at[fi], sem.at[s, 2])]

        def flushed(fi, s):
            for cp in flush(fi, s):
                cp.wait()
            down_copy(fi).start()

            @pl.when(fi % 2 != c)
            def _():
                up_copy(fi, 0).start()
                up_copy(fi, 1).start()

        @pl.when(i == 0)
        def _():
            acc_wup[slot] = jnp.zeros(acc_wup.shape[1:], F32)
            acc_wdown[slot] = jnp.zeros(acc_wdown.shape[1:], F32)
            acc_cfw[...] = jnp.zeros(acc_cfw.shape, F32)
            acc_cfb[...] = jnp.zeros(acc_cfb.shape, F32)
            dextg_ref[tm:tm + HALO_F, :] = jnp.zeros((HALO_F, W_UP_BLK), F32)
            dextv_ref[tm:tm + HALO_F, :] = jnp.zeros((HALO_F, W_UP_BLK), F32)

        w = [cfw_ref[0, 0], cfw_ref[1, 0]]
        dext = [dextg_ref, dextv_ref]
        dhu = [dhug_ref, dhuv_ref]

        def rows1(bi):
            r = _rows(bi)
            gate = gv_ref[0, 0, r, :].astype(F32)
            val = gv_ref[1, 0, r, :].astype(F32)
            sg = _sigmoid(gate)
            silu = gate * sg
            gbuf_ref[r, :] = (silu * val).astype(BF16)
            dg = dg_ref[r, :]
            dgate = dg * val * (sg * (1.0 + gate * (1.0 - sg)))
            dval = dg * silu
            dextg_ref[r, :] = dgate
            dextv_ref[r, :] = dval
            acc_cfb[0:8, :] += _rsum8(dgate)
            acc_cfb[8:16, :] += _rsum8(dval)

        def rows2(bi):
            r = _rows(bi)
            for g in range(2):
                win = dext[g][pl.ds(bi * ROWS, ROWS + HALO_F), :]
                n = ROWS + HALO_F
                later = [pltpu.roll(win, n - 2, 0)[0:ROWS, :], pltpu.roll(win, n - 1, 0)[0:ROWS, :],
                         win[0:ROWS, :]]
                d = sum(later[k] * w[g][k:k + 1, :] for k in range(KF))
                dhu[g][r, :] = d.astype(BF16)
                pre = hu_ref[g, 0, r, :].astype(F32)
                for k in range(KF):
                    r0 = 8 * (g * KF + k)
                    acc_cfw[r0:r0 + 8, :] += _rsum8(later[k] * pre)

        for sub in reversed(range(tm // sub_rows)):
            rs = slice(sub * sub_rows, (sub + 1) * sub_rows)
            blocks = range(sub * sub_rows // ROWS, (sub + 1) * sub_rows // ROWS)
            x1b_ref[rs, :] = (xh_ref[rs, :] * g1_ref[...] + b1_ref[...]).astype(BF16)
            drb_ref[rs, :] = dr2_ref[rs, :].astype(BF16)
            dg_ref[rs, :] = _nt(drb_ref[rs, :], wdown_ref[0])
            for bi in blocks:
                rows1(bi)
            for bi in blocks:
                rows2(bi)
            acc_wdown[slot] += _tn(gbuf_ref[rs, :], drb_ref[rs, :])
            acc_wup[slot, 0] += _tn(dhug_ref[rs, :], x1b_ref[rs, :])
            acc_wup[slot, 1] += _tn(dhuv_ref[rs, :], x1b_ref[rs, :])
            dx1_ref[0, rs, :] = (_nt(dhug_ref[rs, :], wup_ref[0, 0])
                                 + _nt(dhuv_ref[rs, :], wup_ref[1, 0])).astype(BF16)
        dextg_ref[tm:tm + HALO_F, :] = dextg_ref[0:HALO_F, :]
        dextv_ref[tm:tm + HALO_F, :] = dextv_ref[0:HALO_F, :]

        @pl.when(i == nt - 1)
        def _():
            for g in range(2):
                dcfb_ref[g, 0] = jnp.sum(acc_cfb[8 * g:8 * g + 8, :], axis=0, keepdims=True)
                for k in range(KF):
                    r0 = 8 * (g * KF + k)
                    dcfw_ref[g, 0, k:k + 1, :] = jnp.sum(acc_cfw[r0:r0 + 8, :], axis=0, keepdims=True)
            for cp in flush(f, slot):
                cp.start()

        @pl.when((i == 0) & (fo > 0))
        def _():
            flushed(f_prev, 1 - slot)

        @pl.when((i == nt - 1) & (fo == N_F - 1))
        def _():
            rows = pl.ds(pl.multiple_of((1 - c) * half, 8), half)
            to_sibling(f, 2, acc_wdown.at[slot, rows], land_down_ref, f).start()
            for cp in flush(f, slot):
                cp.wait()

            @pl.when(f % 2 != c)
            def _():
                up_copy(f, 0).start()
                up_copy(f, 1).start()

            for fi in range(N_F):
                down_copy(fi).wait()
                for g in range(2):
                    @pl.when(fi % 2 != c)
                    def _():
                        up_copy(fi, g).wait_send()

                    @pl.when(fi % 2 == c)
                    def _():
                        up_copy(fi, g).wait_recv()

    rev = lambda i: nt - 1 - i
    row = pl.BlockSpec((tm, D), lambda fo, i, o: (rev(i), 0))
    pair = lambda r, c: pl.BlockSpec((2, 1, r, c), lambda fo, i, o: (0, o[fo], 0, 0))
    tile = pl.BlockSpec((2, 1, tm, W_UP_BLK), lambda fo, i, o: (0, o[fo], rev(i), 0))
    return pl.pallas_call(
        body, name="ffn_backward",
        grid_spec=pltpu.PrefetchScalarGridSpec(
            num_scalar_prefetch=1, grid=(N_F, nt),
            in_specs=[row, row, _full(ln1_g.shape), _full(ln1_b.shape), tile, tile,
                      pair(D, W_UP_BLK), pair(KF, W_UP_BLK),
                      pl.BlockSpec((1, W_UP_BLK, D), lambda fo, i, o: (o[fo], 0, 0))],
            out_specs=[ANY, ANY, pair(KF, W_UP_BLK), pair(1, W_UP_BLK),
                       pl.BlockSpec((1, tm, D), lambda fo, i, o: (o[fo], rev(i), 0)), ANY, ANY],
            scratch_shapes=[pltpu.VMEM((tm, D), BF16), pltpu.VMEM((tm, D), BF16),
                            pltpu.VMEM((tm, W_UP_BLK), F32),
                            pltpu.VMEM((tm + HALO_F, W_UP_BLK), F32), pltpu.VMEM((tm + HALO_F, W_UP_BLK), F32),
                            pltpu.VMEM((tm, W_UP_BLK), BF16), pltpu.VMEM((tm, W_UP_BLK), BF16),
                            pltpu.VMEM((tm, W_UP_BLK), BF16),
                            pltpu.VMEM((2, 2, W_UP_BLK, D), F32), pltpu.VMEM((2, W_UP_BLK, D), F32),
                            pltpu.VMEM((2 * KF * 8, W_UP_BLK), F32), pltpu.VMEM((16, W_UP_BLK), F32),
                            pltpu.SemaphoreType.DMA((2, 3)),
                            pltpu.SemaphoreType.DMA((N_F, 3)), pltpu.SemaphoreType.DMA((N_F, 3))]),
        out_shape=[jax.ShapeDtypeStruct((2, N_F, W_UP_BLK, D), F32),
                   jax.ShapeDtypeStruct((N_F, W_UP_BLK, D), F32),
                   jax.ShapeDtypeStruct((2, N_F, KF, W_UP_BLK), F32),
                   jax.ShapeDtypeStruct((2, N_F, 1, W_UP_BLK), F32),
                   jax.ShapeDtypeStruct((N_F, t, D), BF16),
                   jax.ShapeDtypeStruct((4, W_UP_BLK, D), F32),
                   jax.ShapeDtypeStruct((4, D_FF // N_DEV, D), F32)],
        compiler_params=_params(("arbitrary", "arbitrary")),
    )(order, dr2, xhat1, ln1_g, ln1_b, hu4, gv4, wup4, cfw4, wdown)


def mix_backward(x, h, yb1, dx1p, dr2, xhat1, rstd1, win_g, ln_a_g, ln_a_b, w_spatial, bst,
                 conv_b_w, ln_b_g, ln_b_b, wout, ln1_g, ffn_partials, tm):
    t = x.shape[0]
    n_p = len(ffn_partials)
    nt = t // tm
    n_chunks = tm // CHUNK
    halo_blocks = tm // HALO_B

    def body(x_ref, h_ref, halo_ref, yb1_ref, dx1p_ref, dr2_ref, xh1_ref, rstd1_ref, win_ref, ga_ref, ba_ref,
             ws_ref, bst_ref, cw_ref, gb_ref, bb_ref, wout_ref, g1_ref, *rest):
        p_refs, rest = rest[:n_p], rest[n_p:]
        gx_ref, dwin_ref, dwout_ref, dcw_ref, small_ref = rest[:5]
        land_refs, rest = rest[5:5 + n_p], rest[5 + n_p:]
        (ext_ref, dext_ref, y_ref, dy_ref, dh_ref, dmb_ref, wsm_ref,
         acc_win, acc_wout, acc_bin, acc_lnag, acc_lnab, acc_ws, acc_bs, acc_cbb, acc_lnbg,
         acc_lnbb, acc_bout, acc_ln1g, acc_ln1b, acc_cw, sem, send_sems, recv_sems) = rest
        i = pl.program_id(0)

        @pl.when(i == 0)
        def _():
            for cp in _chip_copies(p_refs, land_refs, send_sems, recv_sems):
                cp.start()

        first_tile = i == nt - 1
        accs = [acc_win, acc_wout, acc_bin, acc_lnag, acc_lnab, acc_ws, acc_bs, acc_cbb, acc_lnbg,
                acc_lnbb, acc_bout, acc_ln1g, acc_ln1b, acc_cw]

        @pl.when(i == 0)
        def _():
            for acc in accs:
                acc[...] = jnp.zeros(acc.shape, F32)
            dext_ref[tm:tm + HALO_B, :] = jnp.zeros((HALO_B, D_B), F32)
            mask = _tril_mask()
            for hd in range(HEADS):
                wsm_ref[hd] = jnp.where(mask, ws_ref[hd], 0.0).astype(BF16)

        def ln1_rows(bi):
            r = _rows(bi, LN_ROWS)
            part = [dx1p_ref[f, r, :].astype(F32) for f in range(N_F)]
            dx1 = ALPHA * dr2_ref[r, :] + ((part[0] + part[1]) + (part[2] + part[3]))
            xhat = xh1_ref[r, :]
            acc_ln1g[...] += _rsum8(dx1 * xhat)
            acc_ln1b[...] += _rsum8(dx1)
            dr1 = _ln_bwd(dx1 * g1_ref[...], xhat, rstd1_ref[r, 0:1])
            acc_bout[...] += _rsum8(dr1)
            gx_ref[r, :] = ALPHA * dr1
            dmb_ref[r, :] = dr1.astype(BF16)

        _loop(tm // LN_ROWS, ln1_rows)
        dy_ref[...] = _nt(dmb_ref[...], wout_ref[...])

        ha = halo_ref[:, 0:D_B]
        hg = halo_ref[:, D_B:2 * D_B]
        ext_ref[0:HALO_B, :] = jnp.where(first_tile, 0.0, 1.0) * (ha * _sigmoid(hg))

        def chunk(ci):
            r = _rows(ci, CHUNK)
            for hd in range(HEADS):
                sl = slice(hd * HEAD_DIM, (hd + 1) * HEAD_DIM)
                rows8 = slice(8 * hd, 8 * hd + 8)
                hus, hvs, u, cdf_u, cdf_v, xhat, rstd, vn, sv = _mixer_a_head(
                    h_ref, r, hd, ga_ref, ba_ref, wsm_ref, bst_ref)
                dy_a = dy_ref[r, sl]
                y_ref[r, sl] = (u * sv).astype(BF16)
                du = dy_a * sv
                dsv = dy_a * u
                dsvb = dsv.astype(BF16)
                acc_bs[hd] += dsv
                acc_ws[hd] += _nt(dsvb, vn)
                dvn = _tn(wsm_ref[hd], dsvb)
                acc_lnag[rows8, :] += _rsum8(dvn * xhat)
                acc_lnab[rows8, :] += _rsum8(dvn)
                dv = _ln_bwd(dvn * ga_ref[hd:hd + 1, :], xhat, rstd)
                slv = slice(D_A + hd * HEAD_DIM, D_A + (hd + 1) * HEAD_DIM)
                dhu = du * (cdf_u + hus * jnp.exp(-0.5 * hus * hus) * INV_SQRT_2PI)
                dhv = dv * (cdf_v + hvs * jnp.exp(-0.5 * hvs * hvs) * INV_SQRT_2PI)
                acc_bin[:, sl] += _rsum8(dhu)
                acc_bin[:, slv] += _rsum8(dhv)
                dh_ref[r, sl] = dhu.astype(BF16)
                dh_ref[r, slv] = dhv.astype(BF16)
            a_b = h_ref[r, 2 * D_A:2 * D_A + D_B]
            g_b = h_ref[r, 2 * D_A + D_B:D_IN]
            ext_ref[pl.ds(HALO_B + ci * CHUNK, CHUNK), :] = a_b * _sigmoid(g_b)

        _loop(n_chunks, chunk)

        def conv_rows(bi):
            base = bi * ROWS
            r = pl.ds(base, ROWS)
            xhat, rstd = _ln_stats(yb1_ref[r, :])
            yb2 = xhat * gb_ref[...] + bb_ref[...]
            sg = _sigmoid(yb2)
            y_ref[r, D_A:D] = (yb2 * sg).astype(BF16)
            dyb2 = dy_ref[r, D_A:D] * (sg * (1.0 + yb2 * (1.0 - sg)))
            acc_lnbg[...] += _rsum8(dyb2 * xhat)
            acc_lnbb[...] += _rsum8(dyb2)
            dyb1 = _ln_bwd(dyb2 * gb_ref[...], xhat, rstd)
            acc_cbb[...] += _rsum8(dyb1)
            dext_ref[r, :] = dyb1
            for k, tap in _taps(ext_ref[pl.ds(base, ROWS + HALO_B), :], CONV_B_OFFSETS):
                acc_cw[8 * k:8 * k + 8, :] += _rsum8(dyb1 * tap)

        _loop(tm // ROWS, conv_rows)

        def convt_rows(bi):
            base = bi * ROWS
            r = pl.ds(base, ROWS)
            dyb0 = jnp.zeros((ROWS, D_B), F32)
            for k, tap in _taps(dext_ref[pl.ds(base, ROWS + HALO_B), :], CONV_B_T_OFFSETS):
                dyb0 = dyb0 + tap * cw_ref[k:k + 1, :]
            a_b = h_ref[r, 2 * D_A:2 * D_A + D_B]
            sg = _sigmoid(h_ref[r, 2 * D_A + D_B:D_IN])
            da_b = dyb0 * sg
            dg_b = dyb0 * a_b * sg * (1.0 - sg)
            acc_bin[:, 2 * D_A:2 * D_A + D_B] += _rsum8(da_b)
            acc_bin[:, 2 * D_A + D_B:D_IN] += _rsum8(dg_b)
            dh_ref[r, 2 * D_A:2 * D_A + D_B] = da_b.astype(BF16)
            dh_ref[r, 2 * D_A + D_B:D_IN] = dg_b.astype(BF16)

        _loop(tm // ROWS, convt_rows)
        dext_ref[tm:tm + HALO_B, :] = dext_ref[0:HALO_B, :]

        acc_wout[...] += _tn(y_ref[...], dmb_ref[...])
        xt = x_ref[...].T.astype(BF16)
        dh_blocks = [dh_ref[:, j * W_IN_BLK:(j + 1) * W_IN_BLK] for j in range(N_DEV)]
        for j in range(N_DEV):
            acc_win[j] += _nn(xt, dh_blocks[j])
        gx_ref[...] += sum(_nt(dh_blocks[j], win_ref[j]) for j in range(N_DEV))

        @pl.when(i == nt - 1)
        def _():
            cps = [pltpu.make_async_copy(acc_win, dwin_ref, sem.at[0]),
                   pltpu.make_async_copy(acc_wout, dwout_ref, sem.at[1])]
            for cp in cps:
                cp.start()
            small_ref[...] = jnp.zeros(small_ref.shape, F32)

            def put_row_vector(row0, acc):
                vec = jnp.sum(acc[...], axis=0, keepdims=True)
                for k in range(vec.shape[1] // 128):
                    small_ref[row0 + k:row0 + k + 1, :] = vec[:, k * 128:(k + 1) * 128]

            put_row_vector(S_BIN, acc_bin)
            put_row_vector(S_CBB, acc_cbb)
            put_row_vector(S_LNBG, acc_lnbg)
            put_row_vector(S_LNBB, acc_lnbb)
            put_row_vector(S_BOUT, acc_bout)
            put_row_vector(S_LN1G, acc_ln1g)
            put_row_vector(S_LN1B, acc_ln1b)
            mask = _tril_mask()
            for hd in range(HEADS):
                rows8 = slice(8 * hd, 8 * hd + 8)
                small_ref[S_LNAG + hd:S_LNAG + hd + 1, :] = jnp.sum(acc_lnag[rows8, :], axis=0, keepdims=True)
                small_ref[S_LNAB + hd:S_LNAB + hd + 1, :] = jnp.sum(acc_lnab[rows8, :], axis=0, keepdims=True)
                small_ref[S_WS + hd * CHUNK:S_WS + (hd + 1) * CHUNK, :] = jnp.where(mask, acc_ws[hd], 0.0)
                small_ref[S_BS + hd:S_BS + hd + 1, :] = jnp.sum(acc_bs[hd].T, axis=0, keepdims=True)
            for k in range(KB):
                dcw_ref[k:k + 1, :] = jnp.sum(acc_cw[8 * k:8 * k + 8, :], axis=0, keepdims=True)
            for cp in cps:
                cp.wait()
            for cp in _chip_copies(p_refs, land_refs, send_sems, recv_sems):
                cp.wait()

    rev = lambda i: nt - 1 - i
    row = lambda w: pl.BlockSpec((tm, w), lambda i: (rev(i), 0))
    return pl.pallas_call(
        body, name="mix_backward", grid=(nt,),
        in_specs=[row(D), row(D_IN),
                  pl.BlockSpec((HALO_B, 2 * D_B), lambda i: (jnp.maximum(rev(i) * halo_blocks - 1, 0), 1)),
                  row(D_B), pl.BlockSpec((N_F, tm, D), lambda i: (0, rev(i), 0)),
                  row(D), row(D), row(128), _resident(win_g.shape), _full(ln_a_g.shape),
                  _full(ln_a_b.shape), _full(w_spatial.shape), _full(bst.shape), _full(conv_b_w.shape),
                  _full(ln_b_g.shape), _full(ln_b_b.shape),
                  _resident(wout.shape), _full(ln1_g.shape)] + [ANY] * n_p,
        out_specs=[row(D), ANY, ANY, _full((KB, D_B)), _full((S_MIX_ROWS, 128))] + [ANY] * n_p,
        out_shape=[jax.ShapeDtypeStruct((t, D), F32), jax.ShapeDtypeStruct((N_DEV, D, W_IN_BLK), F32),
                   jax.ShapeDtypeStruct((D, D), F32), jax.ShapeDtypeStruct((KB, D_B), F32),
                   jax.ShapeDtypeStruct((S_MIX_ROWS, 128), F32)]
        + [jax.ShapeDtypeStruct(p.shape, BF16) for p in ffn_partials],
        scratch_shapes=[pltpu.VMEM((tm + HALO_B, D_B), F32), pltpu.VMEM((tm + HALO_B, D_B), F32),
                        pltpu.VMEM((tm, D), BF16), pltpu.VMEM((tm, D), F32), pltpu.VMEM((tm, D_IN), BF16),
                        pltpu.VMEM((tm, D), BF16),
                        pltpu.VMEM((HEADS, CHUNK, CHUNK), BF16),
                        pltpu.VMEM((N_DEV, D, W_IN_BLK), F32), pltpu.VMEM((D, D), F32),
                        pltpu.VMEM((8, D_IN), F32), pltpu.VMEM((8 * HEADS, HEAD_DIM), F32),
                        pltpu.VMEM((8 * HEADS, HEAD_DIM), F32), pltpu.VMEM((HEADS, CHUNK, CHUNK), F32),
                        pltpu.VMEM((HEADS, CHUNK, CHUNK), F32), pltpu.VMEM((8, D_B), F32),
                        pltpu.VMEM((8, D_B), F32), pltpu.VMEM((8, D_B), F32), pltpu.VMEM((8, D), F32),
                        pltpu.VMEM((8, D), F32), pltpu.VMEM((8, D), F32), pltpu.VMEM((8 * KB, D_B), F32),
                        pltpu.SemaphoreType.DMA((2,)),
                        pltpu.SemaphoreType.DMA((n_p, 3)), pltpu.SemaphoreType.DMA((n_p, 3))],
        compiler_params=_params(("arbitrary",)),
    )(x, h, h, yb1, dx1p, dr2, xhat1, rstd1, win_g, ln_a_g, ln_a_b, w_spatial, bst, conv_b_w,
      ln_b_g, ln_b_b, wout, ln1_g, *ffn_partials)


def _rows128(a):
    return a.reshape(-1, 128)


def _pack_conv(cb, cf):
    lead = cb.shape[:-2]
    pad = [(0, 0)] * len(lead)
    flat = jnp.pad(cb.reshape(lead + (KB * 64,)), pad + [(0, 3 * W_UP_BLK - KB * 64)])
    rows = jnp.concatenate([cf, flat.reshape(lead + (3, W_UP_BLK))], axis=-2)
    return jnp.pad(rows, pad + [(0, 2), (0, 768 - W_UP_BLK)])


def _unpack_conv(p):
    lead = p.shape[:-2]
    cf = p[..., 0:KF, 0:W_UP_BLK]
    cb = p[..., 3:6, 0:W_UP_BLK].reshape(lead + (3 * W_UP_BLK,))[..., :KB * 64].reshape(lead + (KB, 64))
    return cb, cf


def kernel(x, w_in, b_in, ln_a_g, ln_a_b, w_spatial, b_spatial, conv_b_w, conv_b_b, ln_b_g, ln_b_b, w_out, b_out, ln1_g, ln1_b, w_up, conv_f_w, conv_f_b, w_down, ln2_g, ln2_b, loss_target, m_w_in, m_b_in, m_ln_a_g, m_ln_a_b, m_w_spatial, m_b_spatial, m_conv_b_w, m_conv_b_b, m_ln_b_g, m_ln_b_b, m_w_out, m_b_out, m_ln1_g, m_ln1_b, m_w_up, m_conv_f_w, m_conv_f_b, m_w_down, m_ln2_g, m_ln2_b, v_w_in, v_b_in, v_ln_a_g, v_ln_a_b, v_w_spatial, v_b_spatial, v_conv_b_w, v_conv_b_b, v_ln_b_g, v_ln_b_b, v_w_out, v_b_out, v_ln1_g, v_ln1_b, v_w_up, v_conv_f_w, v_conv_f_b, v_w_down, v_ln2_g, v_ln2_b):
    t = x.shape[1]
    x2 = x.reshape(t, D)
    target = loss_target.reshape(t, D)
    tm_fwd = min(t, 512)
    tm_bwd = min(t, 256)
    tm_ffn_bwd = min(t, 512)

    xi, yi, ci = _mesh_pos()
    jidx = jnp.stack([_lid(px, py, ci) for px, py in _chip_patterns(xi, yi)]).astype(jnp.int32)

    sin, sout, sup, sdown, conv_g = prepare_weights(w_in, w_out, w_up.T, w_down, _pack_conv(conv_b_w, conv_f_w))
    conv_b_all, cfw = _unpack_conv(conv_g)
    conv_b_full = conv_b_all.transpose(1, 0, 2).reshape(KB, D_B)
    cfb = conv_f_b.reshape(N_DEV, W_UP_BLK)
    row = lambda a: a.reshape(1, -1)
    bst = b_spatial.T

    h, xhat1, rstd1, yb1, win_g, wout_g, wup_g, wdown_g = mix_forward(
        x2, sin, sout, row(b_in), ln_a_g, ln_a_b, w_spatial, bst, conv_b_full, row(conv_b_b),
        row(ln_b_g), row(ln_b_b), row(b_out), row(ln1_g), row(ln1_b), sup, sdown, tm_fwd)
    wout_full = wout_g.reshape(D, D)
    wdown4 = wdown_g.reshape(N_F, W_UP_BLK, D)
    hu, gv, dr2, loss_part, s_ln2 = ffn_forward(
        xhat1, row(ln1_g), row(ln1_b), wup_g, cfw, cfb, wdown4, row(ln2_g), row(ln2_b), target, tm_bwd)

    order = jnp.where(ci == 0, jnp.array([1, 3, 0, 2], jnp.int32), jnp.array([0, 2, 1, 3], jnp.int32))
    dwup, dwdown, dcfw, dcfb, dx1p, *ffn_lands = ffn_backward(
        order, dr2, xhat1, row(ln1_g), row(ln1_b), hu, gv, wup_g, cfw, wdown4, tm_ffn_bwd)
    ffn_grads = [dwup.reshape(N_DEV, W_UP_BLK, D), dwdown.reshape(N_DEV, D_FF // N_DEV, D)]
    ffn_partials = chip_partials(ffn_grads, ffn_lands, jidx, 2)
    grad_x, dwin, dwout, dcw, s_mix, *ffn_recvs = mix_backward(
        x2, h, yb1, dx1p, dr2, xhat1, rstd1, win_g, ln_a_g, ln_a_b, w_spatial, bst,
        conv_b_full, row(ln_b_g), row(ln_b_b), wout_full, row(ln1_g), ffn_partials, tm_bwd)

    dcfb_rows = jnp.pad(dcfb.reshape(-1, 128), ((0, 4), (0, 0)))
    dconv = _pack_conv(dcw.reshape(KB, N_DEV, 64).transpose(1, 0, 2), dcfw.reshape(N_DEV, KF, W_UP_BLK))
    mix_grads = [dwin, dwout.reshape(N_DEV, D // N_DEV, D), dconv]
    mix_w = [w_in, w_out, _pack_conv(conv_b_w, conv_f_w)]
    mix_m = [m_w_in, m_w_out, _pack_conv(m_conv_b_w, m_conv_f_w)]
    mix_v = [v_w_in, v_w_out, _pack_conv(v_conv_b_w, v_conv_f_w)]
    *mix_sums, sv_slots = mixer_reduce(mix_grads, [s_mix, dcfb_rows, s_ln2, loss_part])
    big = {}

    ffn_out = reduce_and_adamw(ffn_grads, ffn_lands, ffn_recvs, [w_up.T, w_down], [m_w_up.T, m_w_down],
                               [v_w_up.T, v_w_down], jidx, 2)
    big["w_up"] = [ffn_out[2 * k].T for k in range(4)]
    big["w_down"] = [ffn_out[2 * k + 1] for k in range(4)]

    small_w = dict(b_in=b_in, ln_a_g=ln_a_g, ln_a_b=ln_a_b, w_spatial=w_spatial, b_spatial=b_spatial,
                   conv_b_b=conv_b_b, ln_b_g=ln_b_g, ln_b_b=ln_b_b, b_out=b_out, ln1_g=ln1_g,
                   ln1_b=ln1_b, conv_f_b=conv_f_b, ln2_g=ln2_g, ln2_b=ln2_b)
    small_m = dict(b_in=m_b_in, ln_a_g=m_ln_a_g, ln_a_b=m_ln_a_b, w_spatial=m_w_spatial,
                   b_spatial=m_b_spatial, conv_b_b=m_conv_b_b, ln_b_g=m_ln_b_g, ln_b_b=m_ln_b_b,
                   b_out=m_b_out, ln1_g=m_ln1_g, ln1_b=m_ln1_b, conv_f_b=m_conv_f_b, ln2_g=m_ln2_g,
                   ln2_b=m_ln2_b)
    small_v = dict(b_in=v_b_in, ln_a_g=v_ln_a_g, ln_a_b=v_ln_a_b, w_spatial=v_w_spatial,
                   b_spatial=v_b_spatial, conv_b_b=v_conv_b_b, ln_b_g=v_ln_b_g, ln_b_b=v_ln_b_b,
                   b_out=v_b_out, ln1_g=v_ln1_g, ln1_b=v_ln1_b, conv_f_b=v_conv_f_b, ln2_g=v_ln2_g,
                   ln2_b=v_ln2_b)
    order = [nm for nm, _, _ in SMALL_LAYOUT]
    small_out = small_adamw(sv_slots, [_rows128(small_w[nm]) for nm in order],
                            [_rows128(small_m[nm]) for nm in order], [_rows128(small_v[nm]) for nm in order],
                            mix_sums, mix_w, mix_m, mix_v)
    n_small = len(order)
    mix_out = small_out[4 * n_small + 1:]
    big.update({nm: [mix_out[k * 3 + p] for k in range(4)] for p, nm in enumerate(["w_in", "w_out", "conv"])})
    for k in range(4):
        cb_k, cf_k = _unpack_conv(big["conv"][k])
        big.setdefault("conv_b_w", []).append(cb_k)
        big.setdefault("conv_f_w", []).append(cf_k)
    small = {nm: [small_out[k * n_small + p].reshape(small_w[nm].shape) for k in range(4)]
             for p, nm in enumerate(order)}
    loss = jnp.sum(small_out[4 * n_small]) * (0.5 / D)

    weights = ["w_in", "b_in", "ln_a_g", "ln_a_b", "w_spatial", "b_spatial", "conv_b_w", "conv_b_b",
               "ln_b_g", "ln_b_b", "w_out", "b_out", "ln1_g", "ln1_b", "w_up", "conv_f_w", "conv_f_b",
               "w_down", "ln2_g", "ln2_b"]
    result = lambda nm, k: big[nm][k] if nm in big else small[nm][k]
    return (loss, grad_x.reshape(x.shape), *[result(nm, 0) for nm in weights],
            *[result(nm, 1) for nm in weights], *[result(nm, 2) for nm in weights],
            *[result(nm, 3) for nm in weights])
```

```python
import functools
import math

import jax
import jax.numpy as jnp
from jax import lax
from jax.experimental import pallas as pl
from jax.experimental.pallas import tpu as pltpu

F32 = jnp.float32
BF16 = jnp.bfloat16

D = 1024
D_A = 512
D_B = 512
HEADS = 4
HEAD_DIM = 128
CHUNK = 128
KB = 31
KF = 3
D_FF = 2816
D_IN = 2048
N_DEV = 8
W_IN_BLK = D_IN // N_DEV
W_UP_BLK = 2 * D_FF // N_DEV
N_F = 4
LN_EPS = 1e-5
ALPHA = 2.0 ** 0.25

ADAM_LR = 0.001
ADAM_B1 = 0.9
ADAM_B2 = 0.999
ADAM_EPS = 1e-08
ADAM_WD = 0.01
ADAM_STEP = 10

INV_SQRT2 = 1.0 / math.sqrt(2.0)
INV_SQRT_2PI = 1.0 / math.sqrt(2.0 * math.pi)

HALO_B = 32
HALO_F = 8
ROWS = 64
LN_ROWS = 32
VMEM_LIMIT = 58 * 1024 * 1024

MESH = pl.DeviceIdType.MESH
ANY = pl.BlockSpec(memory_space=pl.ANY)
VMEM = pl.BlockSpec(memory_space=pltpu.VMEM)

S_BIN, S_LNAG, S_LNAB, S_WS, S_BS, S_CBB, S_LNBG, S_LNBB, S_BOUT, S_LN1G, S_LN1B = (
    0, 16, 24, 32, 544, 552, 560, 568, 576, 584, 592)
S_MIX_ROWS = 600
S_CFB = 600
S_LN2G = 648
S_LN2B = 656
S_LOSS = 664
S_ROWS = 672


def _tn(a, b):
    return lax.dot_general(a, b, (((0,), (0,)), ((), ())), preferred_element_type=F32)


def _nt(a, b):
    return lax.dot_general(a, b, (((1,), (1,)), ((), ())), preferred_element_type=F32)


def _nn(a, b):
    return jnp.dot(a, b, preferred_element_type=F32)


def _sigmoid(x):
    return 1.0 / (1.0 + jnp.exp(-x))


def _ln_stats(x):
    mu = jnp.mean(x, axis=-1, keepdims=True)
    xc = x - mu
    var = jnp.mean(xc * xc, axis=-1, keepdims=True)
    rstd = lax.rsqrt(var + LN_EPS)
    return xc * rstd, rstd


def _ln_bwd(dxhat, xhat, rstd):
    m1 = jnp.mean(dxhat, axis=-1, keepdims=True)
    m2 = jnp.mean(dxhat * xhat, axis=-1, keepdims=True)
    return rstd * (dxhat - m1 - xhat * m2)


def _rsum8(x):
    r, n = x.shape
    return x.reshape(r // 8, 8, n).sum(axis=0)


def _rows(i, n=ROWS):
    return pl.ds(i * n, n)


def _loop(n, body):
    for i in range(n):
        body(i)


def _tril_mask():
    r = lax.broadcasted_iota(jnp.int32, (CHUNK, CHUNK), 0)
    c = lax.broadcasted_iota(jnp.int32, (CHUNK, CHUNK), 1)
    return c <= r


def _mixer_a_head(h_ref, r, hd, ga_ref, ba_ref, wsm_ref, bst_ref):
    sl = slice(hd * HEAD_DIM, (hd + 1) * HEAD_DIM)
    hu = h_ref[r, sl]
    hv = h_ref[r, D_A + hd * HEAD_DIM:D_A + (hd + 1) * HEAD_DIM]
    cdf_u = 0.5 * (1.0 + lax.erf(hu * INV_SQRT2))
    cdf_v = 0.5 * (1.0 + lax.erf(hv * INV_SQRT2))
    u = hu * cdf_u
    xhat, rstd = _ln_stats(hv * cdf_v)
    vn = (xhat * ga_ref[hd:hd + 1, :] + ba_ref[hd:hd + 1, :]).astype(BF16)
    sv = _nn(wsm_ref[hd], vn) + bst_ref[:, hd:hd + 1]
    return hu, hv, u, cdf_u, cdf_v, xhat, rstd, vn, sv


def _taps(win, offsets):
    n = win.shape[0]
    for s in range(8):
        ks = [k for k, o in enumerate(offsets) if o % 8 == s]
        if ks:
            moved = win if s == 0 else pltpu.roll(win, n - s, 0)
            for k in ks:
                yield k, moved[offsets[k] - s:offsets[k] - s + ROWS, :]


CONV_B_OFFSETS = [2 + k for k in range(KB)]
CONV_B_T_OFFSETS = [30 - k for k in range(KB)]


def _conv_b_block(ext_ref, base, cw_ref):
    acc = jnp.zeros((ROWS, D_B), F32)
    for k, tap in _taps(ext_ref[pl.ds(base, ROWS + HALO_B), :], CONV_B_OFFSETS):
        acc = acc + tap * cw_ref[k:k + 1, :]
    return acc


def _taps_f(win):
    n = ROWS + HALO_F
    return [pltpu.roll(win, n - 6, 0)[0:ROWS, :], pltpu.roll(win, n - 7, 0)[0:ROWS, :], win[8:n, :]]


def _params(sem, **kw):
    return pltpu.CompilerParams(dimension_semantics=sem, vmem_limit_bytes=VMEM_LIMIT, **kw)


def _resident(shape):
    zeros = (0,) * len(shape)
    return pl.BlockSpec(shape, lambda *_: zeros, pipeline_mode=pl.Buffered(1))


def _full(shape):
    zeros = (0,) * len(shape)
    return pl.BlockSpec(shape, lambda *_: zeros)


def _mesh_pos():
    return lax.axis_index("x"), lax.axis_index("y"), lax.axis_index("c")


def _chip_patterns(x, y):
    return [(x, y), (1 - x, y), (x, 1 - y), (1 - x, 1 - y)]


def _lid(x, y, c):
    return 4 * x + 2 * y + c


def _gather_copy(outs, send_sems, recv_sems, a, k, block, to, src=None):
    blk = outs[a].at[_lid(*block)]
    return pltpu.make_async_remote_copy(
        src_ref=blk if src is None else src, dst_ref=blk,
        send_sem=send_sems.at[a, k], recv_sem=recv_sems.at[a, k], device_id=to, device_id_type=MESH)


def _gather_start(mine, outs, send_sems, recv_sems, local_sems, diagonal=False):
    x, y, c = _mesh_pos()
    me = (x, y, c)
    for a in range(len(mine)):
        pltpu.make_async_copy(mine[a], outs[a].at[_lid(*me)], local_sems.at[a]).start()
        targets = [(x, y, 1 - c), (1 - x, y, c), (x, 1 - y, c), (1 - x, 1 - y, c)]
        for k, to in enumerate(targets if diagonal else targets[:3]):
            _gather_copy(outs, send_sems, recv_sems, a, k, me, to, src=mine[a]).start()


def _gather_relay(mine, outs, send_sems, recv_sems, local_sems, via):
    x, y, c = _mesh_pos()
    me, sib = (x, y, c), (x, y, 1 - c)
    copy = functools.partial(_gather_copy, outs, send_sems, recv_sems)
    source = {1: (1 - x, y, c), 2: (x, 1 - y, c)}
    for a in range(len(mine)):
        for k in ((via[a], 3 - via[a]) if via[a] else (1, 2)):
            copy(a, k, source[k], me).wait_recv()
            if k == via[a]:
                copy(a, 3, source[k], source[3 - k]).start()
            copy(a, 3 + k, source[k], sib).start()


def _gather_finish(mine, outs, send_sems, recv_sems, local_sems):
    x, y, c = _mesh_pos()
    me, sib = (x, y, c), (x, y, 1 - c)
    copy = functools.partial(_gather_copy, outs, send_sems, recv_sems)
    diag = (1 - x, 1 - y)
    n = len(mine)
    for a in range(n):
        copy(a, 3, (*diag, c), me).wait_recv()
        copy(a, 6, (*diag, c), sib).start()
    for a in range(n):
        copy(a, 0, sib, me).wait_recv()
        for k, chip in zip((4, 5, 6), [(1 - x, y), (x, 1 - y), diag]):
            copy(a, k, (*chip, 1 - c), me).wait_recv()
        for k in range(7):
            copy(a, k, me, sib, src=mine[a]).wait_send()
        pltpu.make_async_copy(mine[a], outs[a].at[_lid(*me)], local_sems.at[a]).wait()


def _gather_scratch(n):
    return [pltpu.SemaphoreType.DMA((n, 7)), pltpu.SemaphoreType.DMA((n, 7)), pltpu.SemaphoreType.DMA((n,))]


def prepare_weights(w_in, w_out, w_up_t, w_down, convp):
    def body(win_ref, wout_ref, wup_ref, wdown_ref, convp_ref,
             sin_ref, sout_ref, sup_ref, sdown_ref, gconv_ref, send_sems, recv_sems, local_sems):
        gather = ([convp_ref], [gconv_ref], send_sems, recv_sems, local_sems)
        _gather_start(*gather, diagonal=True)
        sin_ref[...] = win_ref[...].astype(BF16)
        sout_ref[...] = wout_ref[...].astype(BF16)
        sup_ref[...] = wup_ref[...].T.astype(BF16)
        sdown_ref[...] = wdown_ref[...].astype(BF16)
        _gather_relay(*gather, via=[0])
        _gather_finish(*gather)

    return pl.pallas_call(
        body, name="prepare_weights",
        out_shape=[jax.ShapeDtypeStruct(w_in.shape, BF16), jax.ShapeDtypeStruct(w_out.shape, BF16),
                   jax.ShapeDtypeStruct(w_up_t.shape[::-1], BF16), jax.ShapeDtypeStruct(w_down.shape, BF16),
                   jax.ShapeDtypeStruct((N_DEV,) + convp.shape, F32)],
        in_specs=[VMEM] * 5, out_specs=[VMEM] * 4 + [ANY],
        scratch_shapes=_gather_scratch(1),
        compiler_params=pltpu.CompilerParams(vmem_limit_bytes=VMEM_LIMIT),
    )(w_in, w_out, w_up_t, w_down, convp)


def _chip_copies(p, land, send_sems, recv_sems):
    x, y, c = _mesh_pos()
    return [pltpu.make_async_remote_copy(
        src_ref=p[a].at[k], dst_ref=land[a].at[k], send_sem=send_sems.at[a, k], recv_sem=recv_sems.at[a, k],
        device_id=(px, py, c), device_id_type=MESH)
        for k, (px, py) in enumerate(_chip_patterns(x, y)[1:]) for a in range(len(p))]


def chip_partials(gs, lands, jidx, steps):
    n = len(gs)
    blocks = [(1, g.shape[1] // steps, g.shape[2]) for g in gs]

    def body(j_ref, *refs):
        for a in range(n):
            refs[2 * n + a][...] = (refs[a][...] + refs[n + a][...]).astype(BF16)

    return pl.pallas_call(
        body, name="chip_partials",
        out_shape=[jax.ShapeDtypeStruct((3,) + g.shape[1:], BF16) for g in gs],
        grid_spec=pltpu.PrefetchScalarGridSpec(
            num_scalar_prefetch=1, grid=(3, steps),
            in_specs=[pl.BlockSpec(b, lambda k, i, j: (j[1 + k], i, 0)) for b in blocks]
            + [pl.BlockSpec(b, lambda k, i, j: (1 + k, i, 0)) for b in blocks],
            out_specs=[pl.BlockSpec(b, lambda k, i, j: (k, i, 0)) for b in blocks]),
        compiler_params=_params(("arbitrary", "arbitrary")),
    )(jidx, *gs, *lands)


def _adamw(w, g, m, v):
    m2 = ADAM_B1 * m + (1.0 - ADAM_B1) * g
    v2 = ADAM_B2 * v + (1.0 - ADAM_B2) * (g * g)
    m_hat = m2 / (1.0 - ADAM_B1 ** ADAM_STEP)
    v_hat = v2 / (1.0 - ADAM_B2 ** ADAM_STEP)
    delta = -ADAM_LR * (m_hat / (jnp.sqrt(v_hat) + ADAM_EPS) + ADAM_WD * w)
    return delta, m2, v2


def reduce_and_adamw(gs, lands, recvs, ws, ms, vs, jidx, steps):
    n = len(gs)
    rbs = [g.shape[1] // steps for g in gs]

    def body(j_ref, *refs):
        g, land, recv, w, m, v = (refs[k * n:(k + 1) * n] for k in range(6))
        outs = refs[6 * n:]
        for a in range(n):
            grad = ((g[a][0] + land[a][0]) + recv[a][0].astype(F32) + recv[a][1].astype(F32)
                    + recv[a][2].astype(F32))
            delta, m2, v2 = _adamw(w[a][...], grad, m[a][...], v[a][...])
            outs[a][...] = grad
            outs[n + a][...] = delta
            outs[2 * n + a][...] = m2
            outs[3 * n + a][...] = v2

    blk = [pl.BlockSpec((rb, g.shape[2]), lambda i, j: (i, 0)) for g, rb in zip(gs, rbs)]
    part = lambda lead, pick: [pl.BlockSpec((lead, rb, g.shape[2]), pick) for g, rb in zip(gs, rbs)]
    return pl.pallas_call(
        body, name="reduce_adamw",
        out_shape=[jax.ShapeDtypeStruct(g.shape[1:], F32) for g in gs] * 4,
        grid_spec=pltpu.PrefetchScalarGridSpec(
            num_scalar_prefetch=1, grid=(steps,),
            in_specs=part(1, lambda i, j: (j[0], i, 0)) + part(1, lambda i, j: (0, i, 0))
            + part(3, lambda i, j: (0, i, 0)) + blk * 3,
            out_specs=blk * 4),
        compiler_params=_params(("arbitrary",)),
    )(jidx, *gs, *lands, *recvs, *ws, *ms, *vs)


def mixer_reduce(grads, sv_parts):
    n = len(grads)
    shard = [g.shape[1:] for g in grads]
    ns = len(sv_parts)
    sv_shape = (sum(p.shape[0] for p in sv_parts), 128)

    def body(*refs):
        g = refs[:n]
        parts = refs[n:n + ns]
        outs = refs[n + ns:2 * n + ns]
        sv_slots = refs[2 * n + ns]
        rest = refs[2 * n + ns + 1:-1]
        sv_ref = refs[-1]
        row0 = 0
        for part in parts:
            sv_ref[row0:row0 + part.shape[0], :] = part[...]
            row0 += part.shape[0]
        own, land, sendb, recvb = rest[:n], rest[n:2 * n], rest[2 * n:3 * n], rest[3 * n:4 * n]
        sv_land, chip_sv, d2d_send, d2d_recv, ici_send, ici_recv, local_sems, sv_sems = rest[4 * n:]
        x, y, c = _mesh_pos()
        sib = (x, y, 1 - c)
        pats = _chip_patterns(x, y)
        q = 2 * x + y

        d2d, local = {}, {}
        for a in range(n):
            for k, (px, py) in enumerate(pats):
                d2d[a, k] = pltpu.make_async_remote_copy(
                    src_ref=g[a].at[_lid(px, py, 1 - c)], dst_ref=land[a].at[k],
                    send_sem=d2d_send.at[a, k], recv_sem=d2d_recv.at[a, k], device_id=sib, device_id_type=MESH)
                local[a, k] = pltpu.make_async_copy(g[a].at[_lid(px, py, c)], own[a].at[k], local_sems.at[a, k])
        sv_d2d = pltpu.make_async_remote_copy(
            src_ref=sv_ref, dst_ref=sv_land, send_sem=d2d_send.at[n, 0], recv_sem=d2d_recv.at[n, 0],
            device_id=sib, device_id_type=MESH)
        blocks = [(a, k) for a in range(n) for k in (1, 2, 3)] + [(a, 0) for a in range(n)]
        sv_d2d.start()
        for b in blocks:
            d2d[b].start()
            local[b].start()

        half_rows = sv_shape[0] // 2
        rows = pl.ds(pl.multiple_of(c * half_rows, 8), half_rows)
        sv_local = pltpu.make_async_copy(chip_sv, sv_slots.at[q], sv_sems.at[0])

        def sv_ici(k, slot, to):
            return pltpu.make_async_remote_copy(
                src_ref=chip_sv.at[rows], dst_ref=sv_slots.at[slot, rows], send_sem=sv_sems.at[1 + k],
                recv_sem=sv_sems.at[4 + k], device_id=to, device_id_type=MESH)

        def sv_pass_on(k, slot):
            return pltpu.make_async_remote_copy(
                src_ref=sv_slots.at[slot, rows], dst_ref=sv_slots.at[slot, rows], send_sem=sv_sems.at[7 + k],
                recv_sem=sv_sems.at[10 + k], device_id=sib, device_id_type=MESH)

        sv_d2d.wait()
        chip_sv[...] = sv_ref[...] + sv_land[...]
        sv_out = [sv_ici(k, q, (px, py, c)) for k, (px, py) in enumerate(pats[1:])]
        for cp in sv_out + [sv_local]:
            cp.start()

        ici = _chip_copies(sendb, recvb, ici_send, ici_recv)
        for a, k in blocks:
            local[a, k].wait()
            d2d[a, k].wait()
            if k > 0:
                sendb[a][k - 1] = (own[a][k] + land[a][k]).astype(BF16)
                ici[(k - 1) * n + a].start()
        for k, (px, py) in enumerate(pats[1:]):
            sv_out[k].wait_send()
            sv_ici(k, 2 * px + py, (px, py, c)).wait_recv()
            sv_pass_on(k, 2 * px + py).start()
        for a in range(n):
            for k in range(3):
                ici[k * n + a].wait()
            outs[a][...] = ((own[a][0] + land[a][0]) + recvb[a][0].astype(F32) + recvb[a][1].astype(F32)
                            + recvb[a][2].astype(F32))
        for k, (px, py) in enumerate(pats[1:]):
            sv_pass_on(k, 2 * px + py).wait()
        sv_local.wait()

    shard_out = [jax.ShapeDtypeStruct(s, F32) for s in shard]
    return pl.pallas_call(
        body, name="mixer_reduce",
        out_shape=shard_out + [jax.ShapeDtypeStruct((4,) + sv_shape, F32)],
        in_specs=[ANY] * n + [VMEM] * ns, out_specs=[VMEM] * n + [ANY],
        scratch_shapes=[pltpu.VMEM((4,) + s, F32) for s in shard] + [pltpu.VMEM((4,) + s, F32) for s in shard]
        + [pltpu.VMEM((3,) + s, BF16) for s in shard] + [pltpu.VMEM((3,) + s, BF16) for s in shard]
        + [pltpu.VMEM(sv_shape, F32), pltpu.VMEM(sv_shape, F32),
           pltpu.SemaphoreType.DMA((n + 1, 4)), pltpu.SemaphoreType.DMA((n + 1, 4)),
           pltpu.SemaphoreType.DMA((n, 3)), pltpu.SemaphoreType.DMA((n, 3)),
           pltpu.SemaphoreType.DMA((n, 4)), pltpu.SemaphoreType.DMA((13,)), pltpu.VMEM(sv_shape, F32)],
        compiler_params=pltpu.CompilerParams(vmem_limit_bytes=VMEM_LIMIT),
    )(*grads, *sv_parts)


SMALL_LAYOUT = [
    ("b_in", S_BIN, 16), ("ln_a_g", S_LNAG, 4), ("ln_a_b", S_LNAB, 4), ("w_spatial", S_WS, 512),
    ("b_spatial", S_BS, 4), ("conv_b_b", S_CBB, 4), ("ln_b_g", S_LNBG, 4), ("ln_b_b", S_LNBB, 4),
    ("b_out", S_BOUT, 8), ("ln1_g", S_LN1G, 8), ("ln1_b", S_LN1B, 8), ("conv_f_b", S_CFB, 44),
    ("ln2_g", S_LN2G, 8), ("ln2_b", S_LN2B, 8),
]


def small_adamw(sv_slots, ws, ms, vs, shard_grads, shard_ws, shard_ms, shard_vs, handed_on):
    n = len(SMALL_LAYOUT)
    nb = len(shard_grads)

    def body(*refs):
        s_ref = refs[0]
        w_refs, m_refs, v_refs = refs[1:1 + n], refs[1 + n:1 + 2 * n], refs[1 + 2 * n:1 + 3 * n]
        big_in = refs[1 + 3 * n:1 + 3 * n + 4 * nb]
        outs = refs[2 + 3 * n + 4 * nb:]
        big_out = outs[4 * n + 1:4 * n + 1 + 4 * nb]
        for p in range(nb):
            grad = big_in[p][...]
            delta, m2, v2 = _adamw(big_in[nb + p][...], grad, big_in[2 * nb + p][...], big_in[3 * nb + p][...])
            big_out[p][...] = grad
            big_out[nb + p][...] = delta
            big_out[2 * nb + p][...] = m2
            big_out[3 * nb + p][...] = v2
        for p, (_, row0, rows) in enumerate(SMALL_LAYOUT):
            sl = pl.ds(row0, rows)
            grad = ((s_ref[0, sl, :] + s_ref[1, sl, :]) + s_ref[2, sl, :]) + s_ref[3, sl, :]
            delta, m2, v2 = _adamw(w_refs[p][...], grad, m_refs[p][...], v_refs[p][...])
            outs[p][...] = grad
            outs[n + p][...] = delta
            outs[2 * n + p][...] = m2
            outs[3 * n + p][...] = v2
        sl = pl.ds(S_LOSS, 8)
        outs[4 * n][...] = ((s_ref[0, sl, :] + s_ref[1, sl, :]) + s_ref[2, sl, :]) + s_ref[3, sl, :]

    shapes = [jax.ShapeDtypeStruct((rows, 128), F32) for _, _, rows in SMALL_LAYOUT]
    big_shapes = [jax.ShapeDtypeStruct(g.shape, F32) for g in shard_grads]
    return pl.pallas_call(
        body, name="small_adamw",
        out_shape=shapes * 4 + [jax.ShapeDtypeStruct((8, 128), F32)] + big_shapes * 4
        + [jax.ShapeDtypeStruct(handed_on.shape, handed_on.dtype)],
        in_specs=[VMEM] * (1 + 3 * n + 4 * nb) + [ANY], out_specs=[VMEM] * (4 * n + 1 + 4 * nb) + [ANY],
        input_output_aliases={1 + 3 * n + 4 * nb: 4 * n + 1 + 4 * nb},
        compiler_params=pltpu.CompilerParams(vmem_limit_bytes=VMEM_LIMIT),
    )(sv_slots, *ws, *ms, *vs, *shard_grads, *shard_ws, *shard_ms, *shard_vs, handed_on)


def mix_forward(x, sin, sout, b_in, ln_a_g, ln_a_b, w_spatial, bst, conv_b_w, conv_b_b, ln_b_g, ln_b_b,
                b_out, ln1_g, ln1_b, sup, sdown, tm):
    t = x.shape[0]
    nt = t // tm
    n_chunks = tm // CHUNK

    def body(x_ref, sin_ref, sout_ref, bin_ref, ga_ref, ba_ref, ws_ref, bst_ref, cw_ref, cb_ref, gb_ref,
             bb_ref, bout_ref, g1_ref, b1_ref, sup_ref, sdown_ref,
             h_ref, xhat1_ref, rstd1_ref, yb1_ref, gin_ref, gout_ref, gup_ref, gdown_ref,
             ext_ref, y_ref, wsm_ref, win_ref, wout_ref, load_sems,
             mix_send, mix_recv, mix_local, send_sems, recv_sems, local_sems):
        i = pl.program_id(0)
        mixer = ([sin_ref, sout_ref], [gin_ref, gout_ref], mix_send, mix_recv, mix_local)
        gather = ([sup_ref, sdown_ref], [gup_ref, gdown_ref], send_sems, recv_sems, local_sems)

        @pl.when(i == 0)
        def _():
            _gather_start(*mixer)
            _gather_relay(*mixer, via=[2, 2])
            _gather_finish(*mixer)
            _gather_start(*gather)
            loads = [pltpu.make_async_copy(gin_ref, win_ref, load_sems.at[0]),
                     pltpu.make_async_copy(gout_ref, wout_ref, load_sems.at[1])]
            for cp in loads:
                cp.start()
            for cp in loads:
                cp.wait()
            ext_ref[0:HALO_B, :] = jnp.zeros((HALO_B, D_B), F32)
            mask = _tril_mask()
            for hd in range(HEADS):
                wsm_ref[hd] = jnp.where(mask, ws_ref[hd], 0.0).astype(BF16)

        xb = x_ref[...].astype(BF16)
        for j in range(N_DEV):
            cols = slice(j * W_IN_BLK, (j + 1) * W_IN_BLK)
            h_ref[:, cols] = _nn(xb, win_ref[j]) + bin_ref[:, cols]

        def chunk(ci):
            r = _rows(ci, CHUNK)
            for hd in range(HEADS):
                _, _, u, _, _, _, _, _, sv = _mixer_a_head(h_ref, r, hd, ga_ref, ba_ref, wsm_ref, bst_ref)
                y_ref[r, hd * HEAD_DIM:(hd + 1) * HEAD_DIM] = (u * sv).astype(BF16)
            a_b = h_ref[r, 2 * D_A:2 * D_A + D_B]
            g_b = h_ref[r, 2 * D_A + D_B:D_IN]
            ext_ref[pl.ds(HALO_B + ci * CHUNK, CHUNK), :] = a_b * _sigmoid(g_b)

        _loop(n_chunks, chunk)

        def conv_rows(bi):
            base = bi * ROWS
            yb1 = _conv_b_block(ext_ref, base, cw_ref) + cb_ref[...]
            yb1_ref[pl.ds(base, ROWS), :] = yb1
            xhat, _ = _ln_stats(yb1)
            yb2 = xhat * gb_ref[...] + bb_ref[...]
            y_ref[pl.ds(base, ROWS), D_A:D] = (yb2 * _sigmoid(yb2)).astype(BF16)

        _loop(tm // ROWS, conv_rows)
        ext_ref[0:HALO_B, :] = ext_ref[tm:tm + HALO_B, :]

        mix = _nn(y_ref[...], wout_ref[...].reshape(D, D)) + bout_ref[...]
        xhat1, rstd1 = _ln_stats(ALPHA * x_ref[...] + mix)
        xhat1_ref[...] = xhat1
        rstd1_ref[...] = jnp.broadcast_to(rstd1, (tm, 128))

        @pl.when(i == (5 * nt) // 8)
        def _():
            _gather_relay(*gather, via=[1, 2])

        @pl.when(i == nt - 1)
        def _():
            _gather_finish(*gather)

    row = lambda w: pl.BlockSpec((tm, w), lambda i: (i, 0))
    return pl.pallas_call(
        body, name="mix_forward", grid=(nt,),
        in_specs=[row(D), ANY, ANY, _full(b_in.shape), _full(ln_a_g.shape),
                  _full(ln_a_b.shape), _full(w_spatial.shape), _full(bst.shape),
                  _full(conv_b_w.shape), _full(conv_b_b.shape), _full(ln_b_g.shape),
                  _full(ln_b_b.shape), _full(b_out.shape),
                  _full(ln1_g.shape), _full(ln1_b.shape), ANY, ANY],
        out_specs=[row(D_IN), row(D), row(128), row(D_B), ANY, ANY, ANY, ANY],
        out_shape=[jax.ShapeDtypeStruct((t, D_IN), F32), jax.ShapeDtypeStruct((t, D), F32),
                   jax.ShapeDtypeStruct((t, 128), F32), jax.ShapeDtypeStruct((t, D_B), F32)]
        + [jax.ShapeDtypeStruct((N_DEV,) + sh.shape, BF16) for sh in (sin, sout, sup, sdown)],
        scratch_shapes=[pltpu.VMEM((tm + HALO_B, D_B), F32), pltpu.VMEM((tm, D), BF16),
                        pltpu.VMEM((HEADS, CHUNK, CHUNK), BF16),
                        pltpu.VMEM((N_DEV,) + sin.shape, BF16), pltpu.VMEM((N_DEV,) + sout.shape, BF16),
                        pltpu.SemaphoreType.DMA((2,))] + _gather_scratch(2) + _gather_scratch(2),
        compiler_params=_params(("arbitrary",)),
    )(x, sin, sout, b_in, ln_a_g, ln_a_b, w_spatial, bst, conv_b_w, conv_b_b, ln_b_g, ln_b_b,
      b_out, ln1_g, ln1_b, sup, sdown)


def ffn_forward(xhat1, ln1_g, ln1_b, wup_g, cfw, cfb, wdown, ln2_g, ln2_b, target, tm):
    t = xhat1.shape[0]
    nt = t // tm

    def body(xh_ref, g1_ref, b1_ref, wup_ref, cfw_ref, cfb_ref, wdown_ref, g2_ref, b2_ref, tgt_ref,
             hu_ref, gv_ref, dr2_ref, loss_ref, sln2_ref,
             x1_ref, x1b_ref, hu32_ref, carry_ref, gbuf_ref, ffn_ref, acc_loss, acc_g2, acc_b2):
        i = pl.program_id(0)

        @pl.when(i == 0)
        def _():
            carry_ref[...] = jnp.zeros(carry_ref.shape, F32)
            acc_loss[...] = jnp.zeros(acc_loss.shape, F32)
            acc_g2[...] = jnp.zeros(acc_g2.shape, F32)
            acc_b2[...] = jnp.zeros(acc_b2.shape, F32)

        x1 = xh_ref[...] * g1_ref[...] + b1_ref[...]
        x1_ref[...] = x1
        x1b_ref[...] = x1.astype(BF16)

        def conv(g, j, base):
            if base == 0:
                win = jnp.concatenate([carry_ref[j], hu32_ref[g, 0:ROWS, :]], axis=0)
            else:
                win = hu32_ref[g, base - HALO_F:base + ROWS, :]
            taps = _taps_f(win)
            w = cfw_ref[j]
            return sum(taps[k] * w[k:k + 1, :] for k in range(KF)) + cfb_ref[j:j + 1, :]

        for f in range(N_F):
            hu32_ref[0] = _nn(x1b_ref[...], wup_ref[f])
            hu32_ref[1] = _nn(x1b_ref[...], wup_ref[N_F + f])

            def rows(bi, f=f):
                r = _rows(bi)
                gate = conv(0, f, bi * ROWS)
                val = conv(1, N_F + f, bi * ROWS)
                gbuf_ref[r, :] = (gate * _sigmoid(gate) * val).astype(BF16)
                gv_ref[f, r, :] = gate.astype(BF16)
                gv_ref[N_F + f, r, :] = val.astype(BF16)
                hu_ref[f, r, :] = hu32_ref[0, r, :].astype(BF16)
                hu_ref[N_F + f, r, :] = hu32_ref[1, r, :].astype(BF16)

            _loop(tm // ROWS, rows)
            carry_ref[f] = hu32_ref[0, tm - HALO_F:tm, :]
            carry_ref[N_F + f] = hu32_ref[1, tm - HALO_F:tm, :]
            part = _nn(gbuf_ref[...], wdown_ref[f])
            if f == 0:
                ffn_ref[...] = part
            else:
                ffn_ref[...] += part

        def tail(bi):
            r = _rows(bi, LN_ROWS)
            xhat2, rstd2 = _ln_stats(ALPHA * x1_ref[r, :] + ffn_ref[r, :])
            err = xhat2 * g2_ref[...] + b2_ref[...] - tgt_ref[r, :]
            e2 = _rsum8(err * err)
            acc_loss[...] += sum(e2[:, k * 128:(k + 1) * 128] for k in range(D // 128))
            dy = err * (1.0 / D)
            acc_g2[...] += _rsum8(dy * xhat2)
            acc_b2[...] += _rsum8(dy)
            dr2_ref[r, :] = _ln_bwd(dy * g2_ref[...], xhat2, rstd2)

        _loop(tm // LN_ROWS, tail)
        loss_ref[...] = acc_loss[...]

        @pl.when(i == nt - 1)
        def _():
            dg = jnp.sum(acc_g2[...], axis=0, keepdims=True)
            db = jnp.sum(acc_b2[...], axis=0, keepdims=True)
            for k in range(D // 128):
                sln2_ref[k:k + 1, :] = dg[:, k * 128:(k + 1) * 128]
                sln2_ref[8 + k:9 + k, :] = db[:, k * 128:(k + 1) * 128]

    row = pl.BlockSpec((tm, D), lambda i: (i, 0))
    return pl.pallas_call(
        body, name="ffn_forward", grid=(nt,),
        in_specs=[row, _full(ln1_g.shape), _full(ln1_b.shape), _resident(wup_g.shape),
                  _full(cfw.shape), _full(cfb.shape), _resident(wdown.shape),
                  _full(ln2_g.shape), _full(ln2_b.shape), row],
        out_specs=[pl.BlockSpec((N_DEV, tm, W_UP_BLK), lambda i: (0, i, 0)),
                   pl.BlockSpec((N_DEV, tm, W_UP_BLK), lambda i: (0, i, 0)), row,
                   _full((8, 128)), _full((16, 128))],
        out_shape=[jax.ShapeDtypeStruct((N_DEV, t, W_UP_BLK), BF16),
                   jax.ShapeDtypeStruct((N_DEV, t, W_UP_BLK), BF16), jax.ShapeDtypeStruct((t, D), F32),
                   jax.ShapeDtypeStruct((8, 128), F32), jax.ShapeDtypeStruct((16, 128), F32)],
        scratch_shapes=[pltpu.VMEM((tm, D), F32), pltpu.VMEM((tm, D), BF16),
                        pltpu.VMEM((2, tm, W_UP_BLK), F32),
                        pltpu.VMEM((N_DEV, HALO_F, W_UP_BLK), F32), pltpu.VMEM((tm, W_UP_BLK), BF16),
                        pltpu.VMEM((tm, D), F32), pltpu.VMEM((8, 128), F32),
                        pltpu.VMEM((8, D), F32), pltpu.VMEM((8, D), F32)],
        compiler_params=_params(("arbitrary",)),
    )(xhat1, ln1_g, ln1_b, wup_g, cfw, cfb, wdown, ln2_g, ln2_b, target)


def ffn_backward(order, dr2, xhat1, ln1_g, ln1_b, hu, gv, wup_g, cfw, wdown, tm):
    t = dr2.shape[0]
    nt = t // tm
    sub_rows = tm
    hu4 = hu.reshape(2, N_F, t, W_UP_BLK)
    gv4 = gv.reshape(2, N_F, t, W_UP_BLK)
    wup4 = wup_g.reshape(2, N_F, D, W_UP_BLK)
    cfw4 = cfw.reshape(2, N_F, KF, W_UP_BLK)

    def body(order_ref, dr2_ref, xh_ref, g1_ref, b1_ref, hu_ref, gv_ref, wup_ref, cfw_ref, wdown_ref,
             dwup_ref, dwdown_ref, dcfw_ref, dcfb_ref, dx1_ref, land_up_ref, land_down_ref,
             x1b_ref, drb_ref, dg_ref, dextg_ref, dextv_ref, gbuf_ref,
             dhug_ref, dhuv_ref, acc_wup, acc_wdown, acc_cfw, acc_cfb, sem, send_sems, recv_sems):
        fo = pl.program_id(0)
        f = order_ref[fo]
        f_prev = order_ref[jnp.maximum(fo - 1, 0)]
        slot = fo % 2
        i = pl.program_id(1)
        x, y, c = _mesh_pos()
        half = D_FF // N_DEV

        def to_sibling(fi, k, src, land_ref, shard_chip):
            d = jnp.bitwise_xor(shard_chip, 2 * x + y)
            slot = jnp.where(d == 1, 2, jnp.where(d == 2, 1, d))
            return pltpu.make_async_remote_copy(
                src_ref=src, dst_ref=land_ref.at[slot], send_sem=send_sems.at[fi, k], recv_sem=recv_sems.at[fi, k],
                device_id=(x, y, 1 - c), device_id_type=MESH)

        def up_copy(fi, g):
            return to_sibling(fi, g, dwup_ref.at[g, fi], land_up_ref, 2 * g + fi // 2)

        def down_copy(fi):
            return to_sibling(fi, 2, dwdown_ref.at[fi, pl.ds((1 - c) * half, half)], land_down_ref, fi)

        def flush(fi, s):
            return [pltpu.make_async_copy(acc_wup.at[s, 0], dwup_ref.at[0, fi], sem.at[s, 0]),
                    pltpu.make_async_copy(acc_wup.at[s, 1], dwup_ref.at[1, fi], sem.at[s, 1]),
                    pltpu.make_async_copy(acc_wdown.at[s], dwdown_ref.at[fi], sem.at[s, 2])]

        def flushed(fi, s):
            for cp in flush(fi, s):
                cp.wait()
            down_copy(fi).start()

            @pl.when(fi % 2 != c)
            def _():
                up_copy(fi, 0).start()
                up_copy(fi, 1).start()

        @pl.when(i == 0)
        def _():
            acc_wup[slot] = jnp.zeros(acc_wup.shape[1:], F32)
            acc_wdown[slot] = jnp.zeros(acc_wdown.shape[1:], F32)
            acc_cfw[...] = jnp.zeros(acc_cfw.shape, F32)
            acc_cfb[...] = jnp.zeros(acc_cfb.shape, F32)
            dextg_ref[tm:tm + HALO_F, :] = jnp.zeros((HALO_F, W_UP_BLK), F32)
            dextv_ref[tm:tm + HALO_F, :] = jnp.zeros((HALO_F, W_UP_BLK), F32)

        w = [cfw_ref[0, 0], cfw_ref[1, 0]]
        dext = [dextg_ref, dextv_ref]
        dhu = [dhug_ref, dhuv_ref]

        def rows1(bi):
            r = _rows(bi)
            gate = gv_ref[0, 0, r, :].astype(F32)
            val = gv_ref[1, 0, r, :].astype(F32)
            sg = _sigmoid(gate)
            silu = gate * sg
            gbuf_ref[r, :] = (silu * val).astype(BF16)
            dg = dg_ref[r, :]
            dgate = dg * val * (sg * (1.0 + gate * (1.0 - sg)))
            dval = dg * silu
            dextg_ref[r, :] = dgate
            dextv_ref[r, :] = dval
            acc_cfb[0:8, :] += _rsum8(dgate)
            acc_cfb[8:16, :] += _rsum8(dval)

        def rows2(bi):
            r = _rows(bi)
            for g in range(2):
                win = dext[g][pl.ds(bi * ROWS, ROWS + HALO_F), :]
                n = ROWS + HALO_F
                later = [pltpu.roll(win, n - 2, 0)[0:ROWS, :], pltpu.roll(win, n - 1, 0)[0:ROWS, :],
                         win[0:ROWS, :]]
                d = sum(later[k] * w[g][k:k + 1, :] for k in range(KF))
                dhu[g][r, :] = d.astype(BF16)
                pre = hu_ref[g, 0, r, :].astype(F32)
                for k in range(KF):
                    r0 = 8 * (g * KF + k)
                    acc_cfw[r0:r0 + 8, :] += _rsum8(later[k] * pre)

        for sub in reversed(range(tm // sub_rows)):
            rs = slice(sub * sub_rows, (sub + 1) * sub_rows)
            blocks = range(sub * sub_rows // ROWS, (sub + 1) * sub_rows // ROWS)
            x1b_ref[rs, :] = (xh_ref[rs, :] * g1_ref[...] + b1_ref[...]).astype(BF16)
            drb_ref[rs, :] = dr2_ref[rs, :].astype(BF16)
            dg_ref[rs, :] = _nt(drb_ref[rs, :], wdown_ref[0])
            for bi in blocks:
                rows1(bi)
            for bi in blocks:
                rows2(bi)
            acc_wdown[slot] += _tn(gbuf_ref[rs, :], drb_ref[rs, :])
            acc_wup[slot, 0] += _tn(dhug_ref[rs, :], x1b_ref[rs, :])
            acc_wup[slot, 1] += _tn(dhuv_ref[rs, :], x1b_ref[rs, :])
            dx1_ref[0, rs, :] = (_nt(dhug_ref[rs, :], wup_ref[0, 0])
                                 + _nt(dhuv_ref[rs, :], wup_ref[1, 0])).astype(BF16)
        dextg_ref[tm:tm + HALO_F, :] = dextg_ref[0:HALO_F, :]
        dextv_ref[tm:tm + HALO_F, :] = dextv_ref[0:HALO_F, :]

        @pl.when(i == nt - 1)
        def _():
            for g in range(2):
                dcfb_ref[g, 0] = jnp.sum(acc_cfb[8 * g:8 * g + 8, :], axis=0, keepdims=True)
                for k in range(KF):
                    r0 = 8 * (g * KF + k)
                    dcfw_ref[g, 0, k:k + 1, :] = jnp.sum(acc_cfw[r0:r0 + 8, :], axis=0, keepdims=True)
            for cp in flush(f, slot):
                cp.start()

        @pl.when((i == 0) & (fo > 0))
        def _():
            flushed(f_prev, 1 - slot)

        @pl.when((i == nt - 1) & (fo == N_F - 1))
        def _():
            rows = pl.ds(pl.multiple_of((1 - c) * half, 8), half)
            to_sibling(f, 2, acc_wdown.at[slot, rows], land_down_ref, f).start()
            for cp in flush(f, slot):
                cp.wait()

            @pl.when(f % 2 != c)
            def _():
                up_copy(f, 0).start()
                up_copy(f, 1).start()

            for fi in range(N_F):
                down_copy(fi).wait()
                for g in range(2):
                    @pl.when(fi % 2 != c)
                    def _():
                        up_copy(fi, g).wait_send()

                    @pl.when(fi % 2 == c)
                    def _():
                        up_copy(fi, g).wait_recv()

    rev = lambda i: nt - 1 - i
    row = pl.BlockSpec((tm, D), lambda fo, i, o: (rev(i), 0))
    pair = lambda r, c: pl.BlockSpec((2, 1, r, c), lambda fo, i, o: (0, o[fo], 0, 0))
    tile = pl.BlockSpec((2, 1, tm, W_UP_BLK), lambda fo, i, o: (0, o[fo], rev(i), 0))
    return pl.pallas_call(
        body, name="ffn_backward",
        grid_spec=pltpu.PrefetchScalarGridSpec(
            num_scalar_prefetch=1, grid=(N_F, nt),
            in_specs=[row, row, _full(ln1_g.shape), _full(ln1_b.shape), tile, tile,
                      pair(D, W_UP_BLK), pair(KF, W_UP_BLK),
                      pl.BlockSpec((1, W_UP_BLK, D), lambda fo, i, o: (o[fo], 0, 0))],
            out_specs=[ANY, ANY, pair(KF, W_UP_BLK), pair(1, W_UP_BLK),
                       pl.BlockSpec((1, tm, D), lambda fo, i, o: (o[fo], rev(i), 0)), ANY, ANY],
            scratch_shapes=[pltpu.VMEM((tm, D), BF16), pltpu.VMEM((tm, D), BF16),
                            pltpu.VMEM((tm, W_UP_BLK), F32),
                            pltpu.VMEM((tm + HALO_F, W_UP_BLK), F32), pltpu.VMEM((tm + HALO_F, W_UP_BLK), F32),
                            pltpu.VMEM((tm, W_UP_BLK), BF16), pltpu.VMEM((tm, W_UP_BLK), BF16),
                            pltpu.VMEM((tm, W_UP_BLK), BF16),
                            pltpu.VMEM((2, 2, W_UP_BLK, D), F32), pltpu.VMEM((2, W_UP_BLK, D), F32),
                            pltpu.VMEM((2 * KF * 8, W_UP_BLK), F32), pltpu.VMEM((16, W_UP_BLK), F32),
                            pltpu.SemaphoreType.DMA((2, 3)),
                            pltpu.SemaphoreType.DMA((N_F, 3)), pltpu.SemaphoreType.DMA((N_F, 3))]),
        out_shape=[jax.ShapeDtypeStruct((2, N_F, W_UP_BLK, D), F32),
                   jax.ShapeDtypeStruct((N_F, W_UP_BLK, D), F32),
                   jax.ShapeDtypeStruct((2, N_F, KF, W_UP_BLK), F32),
                   jax.ShapeDtypeStruct((2, N_F, 1, W_UP_BLK), F32),
                   jax.ShapeDtypeStruct((N_F, t, D), BF16),
                   jax.ShapeDtypeStruct((4, W_UP_BLK, D), F32),
                   jax.ShapeDtypeStruct((4, D_FF // N_DEV, D), F32)],
        compiler_params=_params(("arbitrary", "arbitrary")),
    )(order, dr2, xhat1, ln1_g, ln1_b, hu4, gv4, wup4, cfw4, wdown)


def mix_backward(x, h, yb1, dx1p, dr2, xhat1, rstd1, win_g, ln_a_g, ln_a_b, w_spatial, bst,
                 conv_b_w, ln_b_g, ln_b_b, wout, ln1_g, ffn_partials, tm):
    t = x.shape[0]
    n_p = len(ffn_partials)
    nt = t // tm
    n_chunks = tm // CHUNK
    halo_blocks = tm // HALO_B

    def body(x_ref, h_ref, halo_ref, yb1_ref, dx1p_ref, dr2_ref, xh1_ref, rstd1_ref, win_ref, ga_ref, ba_ref,
             ws_ref, bst_ref, cw_ref, gb_ref, bb_ref, wout_ref, g1_ref, *rest):
        p_refs, rest = rest[:n_p], rest[n_p:]
        gx_ref, dwin_ref, dwout_ref, dcw_ref, small_ref = rest[:5]
        land_refs, rest = rest[5:5 + n_p], rest[5 + n_p:]
        (ext_ref, dext_ref, y_ref, dy_ref, dh_ref, dmb_ref, wsm_ref,
         acc_win, acc_wout, acc_bin, acc_lnag, acc_lnab, acc_ws, acc_bs, acc_cbb, acc_lnbg,
         acc_lnbb, acc_bout, acc_ln1g, acc_ln1b, acc_cw, sem, send_sems, recv_sems) = rest
        i = pl.program_id(0)

        @pl.when(i == 0)
        def _():
            for cp in _chip_copies(p_refs, land_refs, send_sems, recv_sems):
                cp.start()

        first_tile = i == nt - 1
        accs = [acc_win, acc_wout, acc_bin, acc_lnag, acc_lnab, acc_ws, acc_bs, acc_cbb, acc_lnbg,
                acc_lnbb, acc_bout, acc_ln1g, acc_ln1b, acc_cw]

        @pl.when(i == 0)
        def _():
            for acc in accs:
                acc[...] = jnp.zeros(acc.shape, F32)
            dext_ref[tm:tm + HALO_B, :] = jnp.zeros((HALO_B, D_B), F32)
            mask = _tril_mask()
            for hd in range(HEADS):
                wsm_ref[hd] = jnp.where(mask, ws_ref[hd], 0.0).astype(BF16)

        def ln1_rows(bi):
            r = _rows(bi, LN_ROWS)
            part = [dx1p_ref[f, r, :].astype(F32) for f in range(N_F)]
            dx1 = ALPHA * dr2_ref[r, :] + ((part[0] + part[1]) + (part[2] + part[3]))
            xhat = xh1_ref[r, :]
            acc_ln1g[...] += _rsum8(dx1 * xhat)
            acc_ln1b[...] += _rsum8(dx1)
            dr1 = _ln_bwd(dx1 * g1_ref[...], xhat, rstd1_ref[r, 0:1])
            acc_bout[...] += _rsum8(dr1)
            gx_ref[r, :] = ALPHA * dr1
            dmb_ref[r, :] = dr1.astype(BF16)

        _loop(tm // LN_ROWS, ln1_rows)
        dy_ref[...] = _nt(dmb_ref[...], wout_ref[...])

        ha = halo_ref[:, 0:D_B]
        hg = halo_ref[:, D_B:2 * D_B]
        ext_ref[0:HALO_B, :] = jnp.where(first_tile, 0.0, 1.0) * (ha * _sigmoid(hg))

        def chunk(ci):
            r = _rows(ci, CHUNK)
            for hd in range(HEADS):
                sl = slice(hd * HEAD_DIM, (hd + 1) * HEAD_DIM)
                rows8 = slice(8 * hd, 8 * hd + 8)
                hus, hvs, u, cdf_u, cdf_v, xhat, rstd, vn, sv = _mixer_a_head(
                    h_ref, r, hd, ga_ref, ba_ref, wsm_ref, bst_ref)
                dy_a = dy_ref[r, sl]
                y_ref[r, sl] = (u * sv).astype(BF16)
                du = dy_a * sv
                dsv = dy_a * u
                dsvb = dsv.astype(BF16)
                acc_bs[hd] += dsv
                acc_ws[hd] += _nt(dsvb, vn)
                dvn = _tn(wsm_ref[hd], dsvb)
                acc_lnag[rows8, :] += _rsum8(dvn * xhat)
                acc_lnab[rows8, :] += _rsum8(dvn)
                dv = _ln_bwd(dvn * ga_ref[hd:hd + 1, :], xhat, rstd)
                slv = slice(D_A + hd * HEAD_DIM, D_A + (hd + 1) * HEAD_DIM)
                dhu = du * (cdf_u + hus * jnp.exp(-0.5 * hus * hus) * INV_SQRT_2PI)
                dhv = dv * (cdf_v + hvs * jnp.exp(-0.5 * hvs * hvs) * INV_SQRT_2PI)
                acc_bin[:, sl] += _rsum8(dhu)
                acc_bin[:, slv] += _rsum8(dhv)
                dh_ref[r, sl] = dhu.astype(BF16)
                dh_ref[r, slv] = dhv.astype(BF16)
            a_b = h_ref[r, 2 * D_A:2 * D_A + D_B]
            g_b = h_ref[r, 2 * D_A + D_B:D_IN]
            ext_ref[pl.ds(HALO_B + ci * CHUNK, CHUNK), :] = a_b * _sigmoid(g_b)

        _loop(n_chunks, chunk)

        def conv_rows(bi):
            base = bi * ROWS
            r = pl.ds(base, ROWS)
            xhat, rstd = _ln_stats(yb1_ref[r, :])
            yb2 = xhat * gb_ref[...] + bb_ref[...]
            sg = _sigmoid(yb2)
            y_ref[r, D_A:D] = (yb2 * sg).astype(BF16)
            dyb2 = dy_ref[r, D_A:D] * (sg * (1.0 + yb2 * (1.0 - sg)))
            acc_lnbg[...] += _rsum8(dyb2 * xhat)
            acc_lnbb[...] += _rsum8(dyb2)
            dyb1 = _ln_bwd(dyb2 * gb_ref[...], xhat, rstd)
            acc_cbb[...] += _rsum8(dyb1)
            dext_ref[r, :] = dyb1
            for k, tap in _taps(ext_ref[pl.ds(base, ROWS + HALO_B), :], CONV_B_OFFSETS):
                acc_cw[8 * k:8 * k + 8, :] += _rsum8(dyb1 * tap)

        _loop(tm // ROWS, conv_rows)

        def convt_rows(bi):
            base = bi * ROWS
            r = pl.ds(base, ROWS)
            dyb0 = jnp.zeros((ROWS, D_B), F32)
            for k, tap in _taps(dext_ref[pl.ds(base, ROWS + HALO_B), :], CONV_B_T_OFFSETS):
                dyb0 = dyb0 + tap * cw_ref[k:k + 1, :]
            a_b = h_ref[r, 2 * D_A:2 * D_A + D_B]
            sg = _sigmoid(h_ref[r, 2 * D_A + D_B:D_IN])
            da_b = dyb0 * sg
            dg_b = dyb0 * a_b * sg * (1.0 - sg)
            acc_bin[:, 2 * D_A:2 * D_A + D_B] += _rsum8(da_b)
            acc_bin[:, 2 * D_A + D_B:D_IN] += _rsum8(dg_b)
            dh_ref[r, 2 * D_A:2 * D_A + D_B] = da_b.astype(BF16)
            dh_ref[r, 2 * D_A + D_B:D_IN] = dg_b.astype(BF16)

        _loop(tm // ROWS, convt_rows)
        dext_ref[tm:tm + HALO_B, :] = dext_ref[0:HALO_B, :]

        acc_wout[...] += _tn(y_ref[...], dmb_ref[...])
        xt = x_ref[...].T.astype(BF16)
        dh_blocks = [dh_ref[:, j * W_IN_BLK:(j + 1) * W_IN_BLK] for j in range(N_DEV)]
        for j in range(N_DEV):
            acc_win[j] += _nn(xt, dh_blocks[j])
        gx_ref[...] += sum(_nt(dh_blocks[j], win_ref[j]) for j in range(N_DEV))

        @pl.when(i == nt - 1)
        def _():
            cps = [pltpu.make_async_copy(acc_win, dwin_ref, sem.at[0]),
                   pltpu.make_async_copy(acc_wout, dwout_ref, sem.at[1])]
            for cp in cps:
                cp.start()
            small_ref[...] = jnp.zeros(small_ref.shape, F32)

            def put_row_vector(row0, acc):
                vec = jnp.sum(acc[...], axis=0, keepdims=True)
                for k in range(vec.shape[1] // 128):
                    small_ref[row0 + k:row0 + k + 1, :] = vec[:, k * 128:(k + 1) * 128]

            put_row_vector(S_BIN, acc_bin)
            put_row_vector(S_CBB, acc_cbb)
            put_row_vector(S_LNBG, acc_lnbg)
            put_row_vector(S_LNBB, acc_lnbb)
            put_row_vector(S_BOUT, acc_bout)
            put_row_vector(S_LN1G, acc_ln1g)
            put_row_vector(S_LN1B, acc_ln1b)
            mask = _tril_mask()
            for hd in range(HEADS):
                rows8 = slice(8 * hd, 8 * hd + 8)
                small_ref[S_LNAG + hd:S_LNAG + hd + 1, :] = jnp.sum(acc_lnag[rows8, :], axis=0, keepdims=True)
                small_ref[S_LNAB + hd:S_LNAB + hd + 1, :] = jnp.sum(acc_lnab[rows8, :], axis=0, keepdims=True)
                small_ref[S_WS + hd * CHUNK:S_WS + (hd + 1) * CHUNK, :] = jnp.where(mask, acc_ws[hd], 0.0)
                small_ref[S_BS + hd:S_BS + hd + 1, :] = jnp.sum(acc_bs[hd].T, axis=0, keepdims=True)
            for k in range(KB):
                dcw_ref[k:k + 1, :] = jnp.sum(acc_cw[8 * k:8 * k + 8, :], axis=0, keepdims=True)
            for cp in cps:
                cp.wait()
            for cp in _chip_copies(p_refs, land_refs, send_sems, recv_sems):
                cp.wait()

    rev = lambda i: nt - 1 - i
    row = lambda w: pl.BlockSpec((tm, w), lambda i: (rev(i), 0))
    return pl.pallas_call(
        body, name="mix_backward", grid=(nt,),
        in_specs=[row(D), row(D_IN),
                  pl.BlockSpec((HALO_B, 2 * D_B), lambda i: (jnp.maximum(rev(i) * halo_blocks - 1, 0), 1)),
                  row(D_B), pl.BlockSpec((N_F, tm, D), lambda i: (0, rev(i), 0)),
                  row(D), row(D), row(128), _resident(win_g.shape), _full(ln_a_g.shape),
                  _full(ln_a_b.shape), _full(w_spatial.shape), _full(bst.shape), _full(conv_b_w.shape),
                  _full(ln_b_g.shape), _full(ln_b_b.shape),
                  _resident(wout.shape), _full(ln1_g.shape)] + [ANY] * n_p,
        out_specs=[row(D), ANY, ANY, _full((KB, D_B)), _full((S_MIX_ROWS, 128))] + [ANY] * n_p,
        out_shape=[jax.ShapeDtypeStruct((t, D), F32), jax.ShapeDtypeStruct((N_DEV, D, W_IN_BLK), F32),
                   jax.ShapeDtypeStruct((D, D), F32), jax.ShapeDtypeStruct((KB, D_B), F32),
                   jax.ShapeDtypeStruct((S_MIX_ROWS, 128), F32)]
        + [jax.ShapeDtypeStruct(p.shape, BF16) for p in ffn_partials],
        scratch_shapes=[pltpu.VMEM((tm + HALO_B, D_B), F32), pltpu.VMEM((tm + HALO_B, D_B), F32),
                        pltpu.VMEM((tm, D), BF16), pltpu.VMEM((tm, D), F32), pltpu.VMEM((tm, D_IN), BF16),
                        pltpu.VMEM((tm, D), BF16),
                        pltpu.VMEM((HEADS, CHUNK, CHUNK), BF16),
                        pltpu.VMEM((N_DEV, D, W_IN_BLK), F32), pltpu.VMEM((D, D), F32),
                        pltpu.VMEM((8, D_IN), F32), pltpu.VMEM((8 * HEADS, HEAD_DIM), F32),
                        pltpu.VMEM((8 * HEADS, HEAD_DIM), F32), pltpu.VMEM((HEADS, CHUNK, CHUNK), F32),
                        pltpu.VMEM((HEADS, CHUNK, CHUNK), F32), pltpu.VMEM((8, D_B), F32),
                        pltpu.VMEM((8, D_B), F32), pltpu.VMEM((8, D_B), F32), pltpu.VMEM((8, D), F32),
                        pltpu.VMEM((8, D), F32), pltpu.VMEM((8, D), F32), pltpu.VMEM((8 * KB, D_B), F32),
                        pltpu.SemaphoreType.DMA((2,)),
                        pltpu.SemaphoreType.DMA((n_p, 3)), pltpu.SemaphoreType.DMA((n_p, 3))],
        compiler_params=_params(("arbitrary",)),
    )(x, h, h, yb1, dx1p, dr2, xhat1, rstd1, win_g, ln_a_g, ln_a_b, w_spatial, bst, conv_b_w,
      ln_b_g, ln_b_b, wout, ln1_g, *ffn_partials)


def _rows128(a):
    return a.reshape(-1, 128)


def _pack_conv(cb, cf):
    lead = cb.shape[:-2]
    pad = [(0, 0)] * len(lead)
    flat = jnp.pad(cb.reshape(lead + (KB * 64,)), pad + [(0, 3 * W_UP_BLK - KB * 64)])
    rows = jnp.concatenate([cf, flat.reshape(lead + (3, W_UP_BLK))], axis=-2)
    return jnp.pad(rows, pad + [(0, 2), (0, 768 - W_UP_BLK)])


def _unpack_conv(p):
    lead = p.shape[:-2]
    cf = p[..., 0:KF, 0:W_UP_BLK]
    cb = p[..., 3:6, 0:W_UP_BLK].reshape(lead + (3 * W_UP_BLK,))[..., :KB * 64].reshape(lead + (KB, 64))
    return cb, cf


def kernel(x, w_in, b_in, ln_a_g, ln_a_b, w_spatial, b_spatial, conv_b_w, conv_b_b, ln_b_g, ln_b_b, w_out, b_out, ln1_g, ln1_b, w_up, conv_f_w, conv_f_b, w_down, ln2_g, ln2_b, loss_target, m_w_in, m_b_in, m_ln_a_g, m_ln_a_b, m_w_spatial, m_b_spatial, m_conv_b_w, m_conv_b_b, m_ln_b_g, m_ln_b_b, m_w_out, m_b_out, m_ln1_g, m_ln1_b, m_w_up, m_conv_f_w, m_conv_f_b, m_w_down, m_ln2_g, m_ln2_b, v_w_in, v_b_in, v_ln_a_g, v_ln_a_b, v_w_spatial, v_b_spatial, v_conv_b_w, v_conv_b_b, v_ln_b_g, v_ln_b_b, v_w_out, v_b_out, v_ln1_g, v_ln1_b, v_w_up, v_conv_f_w, v_conv_f_b, v_w_down, v_ln2_g, v_ln2_b):
    t = x.shape[1]
    x2 = x.reshape(t, D)
    target = loss_target.reshape(t, D)
    tm_fwd = min(t, 512)
    tm_bwd = min(t, 256)
    tm_ffn_bwd = min(t, 512)

    xi, yi, ci = _mesh_pos()
    jidx = jnp.stack([_lid(px, py, ci) for px, py in _chip_patterns(xi, yi)]).astype(jnp.int32)

    sin, sout, sup, sdown, conv_g = prepare_weights(w_in, w_out, w_up.T, w_down, _pack_conv(conv_b_w, conv_f_w))
    conv_b_all, cfw = _unpack_conv(conv_g)
    conv_b_full = conv_b_all.transpose(1, 0, 2).reshape(KB, D_B)
    cfb = conv_f_b.reshape(N_DEV, W_UP_BLK)
    row = lambda a: a.reshape(1, -1)
    bst = b_spatial.T

    h, xhat1, rstd1, yb1, win_g, wout_g, wup_g, wdown_g = mix_forward(
        x2, sin, sout, row(b_in), ln_a_g, ln_a_b, w_spatial, bst, conv_b_full, row(conv_b_b),
        row(ln_b_g), row(ln_b_b), row(b_out), row(ln1_g), row(ln1_b), sup, sdown, tm_fwd)
    wout_full = wout_g.reshape(D, D)
    wdown4 = wdown_g.reshape(N_F, W_UP_BLK, D)
    hu, gv, dr2, loss_part, s_ln2 = ffn_forward(
        xhat1, row(ln1_g), row(ln1_b), wup_g, cfw, cfb, wdown4, row(ln2_g), row(ln2_b), target, tm_bwd)

    order = jnp.where(ci == 0, jnp.array([1, 3, 0, 2], jnp.int32), jnp.array([0, 2, 1, 3], jnp.int32))
    dwup, dwdown, dcfw, dcfb, dx1p, *ffn_lands = ffn_backward(
        order, dr2, xhat1, row(ln1_g), row(ln1_b), hu, gv, wup_g, cfw, wdown4, tm_ffn_bwd)
    ffn_grads = [dwup.reshape(N_DEV, W_UP_BLK, D), dwdown.reshape(N_DEV, D_FF // N_DEV, D)]
    ffn_partials = chip_partials(ffn_grads, ffn_lands, jidx, 2)
    grad_x, dwin, dwout, dcw, s_mix, *ffn_recvs = mix_backward(
        x2, h, yb1, dx1p, dr2, xhat1, rstd1, win_g, ln_a_g, ln_a_b, w_spatial, bst,
        conv_b_full, row(ln_b_g), row(ln_b_b), wout_full, row(ln1_g), ffn_partials, tm_bwd)

    dcfb_rows = jnp.pad(dcfb.reshape(-1, 128), ((0, 4), (0, 0)))
    dconv = _pack_conv(dcw.reshape(KB, N_DEV, 64).transpose(1, 0, 2), dcfw.reshape(N_DEV, KF, W_UP_BLK))
    mix_grads = [dwin, dwout.reshape(N_DEV, D // N_DEV, D), dconv]
    mix_w = [w_in, w_out, _pack_conv(conv_b_w, conv_f_w)]
    mix_m = [m_w_in, m_w_out, _pack_conv(m_conv_b_w, m_conv_f_w)]
    mix_v = [v_w_in, v_w_out, _pack_conv(v_conv_b_w, v_conv_f_w)]
    *mix_sums, sv_slots = mixer_reduce(mix_grads, [s_mix, dcfb_rows, s_ln2, loss_part])
    big = {}

    ffn_out = reduce_and_adamw(ffn_grads, ffn_lands, ffn_recvs, [w_up.T, w_down], [m_w_up.T, m_w_down],
                               [v_w_up.T, v_w_down], jidx, 2)
    big["w_up"] = [ffn_out[2 * k].T for k in range(4)]
    big["w_down"] = [ffn_out[2 * k + 1] for k in range(4)]

    small_w = dict(b_in=b_in, ln_a_g=ln_a_g, ln_a_b=ln_a_b, w_spatial=w_spatial, b_spatial=b_spatial,
                   conv_b_b=conv_b_b, ln_b_g=ln_b_g, ln_b_b=ln_b_b, b_out=b_out, ln1_g=ln1_g,
                   ln1_b=ln1_b, conv_f_b=conv_f_b, ln2_g=ln2_g, ln2_b=ln2_b)
    small_m = dict(b_in=m_b_in, ln_a_g=m_ln_a_g, ln_a_b=m_ln_a_b, w_spatial=m_w_spatial,
                   b_spatial=m_b_spatial, conv_b_b=m_conv_b_b, ln_b_g=m_ln_b_g, ln_b_b=m_ln_b_b,
                   b_out=m_b_out, ln1_g=m_ln1_g, ln1_b=m_ln1_b, conv_f_b=m_conv_f_b, ln2_g=m_ln2_g,
                   ln2_b=m_ln2_b)
    small_v = dict(b_in=v_b_in, ln_a_g=v_ln_a_g, ln_a_b=v_ln_a_b, w_spatial=v_w_spatial,
                   b_spatial=v_b_spatial, conv_b_b=v_conv_b_b, ln_b_g=v_ln_b_g, ln_b_b=v_ln_b_b,
                   b_out=v_b_out, ln1_g=v_ln1_g, ln1_b=v_ln1_b, conv_f_b=v_conv_f_b, ln2_g=v_ln2_g,
                   ln2_b=v_ln2_b)
    order = [nm for nm, _, _ in SMALL_LAYOUT]
    small_out = small_adamw(sv_slots, [_rows128(small_w[nm]) for nm in order],
                            [_rows128(small_m[nm]) for nm in order], [_rows128(small_v[nm]) for nm in order],
                            mix_sums, mix_w, mix_m, mix_v, grad_x)
    n_small = len(order)
    mix_out = small_out[4 * n_small + 1:-1]
    grad_x = small_out[-1]
    big.update({nm: [mix_out[k * 3 + p] for k in range(4)] for p, nm in enumerate(["w_in", "w_out", "conv"])})
    for k in range(4):
        cb_k, cf_k = _unpack_conv(big["conv"][k])
        big.setdefault("conv_b_w", []).append(cb_k)
        big.setdefault("conv_f_w", []).append(cf_k)
    small = {nm: [small_out[k * n_small + p].reshape(small_w[nm].shape) for k in range(4)]
             for p, nm in enumerate(order)}
    loss = jnp.sum(small_out[4 * n_small]) * (0.5 / D)

    weights = ["w_in", "b_in", "ln_a_g", "ln_a_b", "w_spatial", "b_spatial", "conv_b_w", "conv_b_b",
               "ln_b_g", "ln_b_b", "w_out", "b_out", "ln1_g", "ln1_b", "w_up", "conv_f_w", "conv_f_b",
               "w_down", "ln2_g", "ln2_b"]
    result = lambda nm, k: big[nm][k] if nm in big else small[nm][k]
    return (loss, grad_x.reshape(x.shape), *[result(nm, 0) for nm in weights],
            *[result(nm, 1) for nm in weights], *[result(nm, 2) for nm in weights],
            *[result(nm, 3) for nm in weights])
```

```python
import functools
import math

import jax
import jax.numpy as jnp
from jax import lax
from jax.experimental import pallas as pl
from jax.experimental.pallas import tpu as pltpu

F32 = jnp.float32
BF16 = jnp.bfloat16

D = 1024
D_A = 512
D_B = 512
HEADS = 4
HEAD_DIM = 128
CHUNK = 128
KB = 31
KF = 3
D_FF = 2816
D_IN = 2048
N_DEV = 8
W_IN_BLK = D_IN // N_DEV
W_UP_BLK = 2 * D_FF // N_DEV
N_F = 4
LN_EPS = 1e-5
ALPHA = 2.0 ** 0.25

ADAM_LR = 0.001
ADAM_B1 = 0.9
ADAM_B2 = 0.999
ADAM_EPS = 1e-08
ADAM_WD = 0.01
ADAM_STEP = 10

INV_SQRT2 = 1.0 / math.sqrt(2.0)
INV_SQRT_2PI = 1.0 / math.sqrt(2.0 * math.pi)

HALO_B = 32
HALO_F = 8
ROWS = 64
LN_ROWS = 32
VMEM_LIMIT = 58 * 1024 * 1024

MESH = pl.DeviceIdType.MESH
ANY = pl.BlockSpec(memory_space=pl.ANY)
VMEM = pl.BlockSpec(memory_space=pltpu.VMEM)

S_BIN, S_LNAG, S_LNAB, S_WS, S_BS, S_CBB, S_LNBG, S_LNBB, S_BOUT, S_LN1G, S_LN1B = (
    0, 16, 24, 32, 544, 552, 560, 568, 576, 584, 592)
S_MIX_ROWS = 600
S_CFB = 600
S_LN2G = 648
S_LN2B = 656
S_LOSS = 664
S_ROWS = 672


def _tn(a, b):
    return lax.dot_general(a, b, (((0,), (0,)), ((), ())), preferred_element_type=F32)


def _nt(a, b):
    return lax.dot_general(a, b, (((1,), (1,)), ((), ())), preferred_element_type=F32)


def _nn(a, b):
    return jnp.dot(a, b, preferred_element_type=F32)


def _sigmoid(x):
    return 1.0 / (1.0 + jnp.exp(-x))


def _ln_stats(x):
    mu = jnp.mean(x, axis=-1, keepdims=True)
    xc = x - mu
    var = jnp.mean(xc * xc, axis=-1, keepdims=True)
    rstd = lax.rsqrt(var + LN_EPS)
    return xc * rstd, rstd


def _ln_bwd(dxhat, xhat, rstd):
    m1 = jnp.mean(dxhat, axis=-1, keepdims=True)
    m2 = jnp.mean(dxhat * xhat, axis=-1, keepdims=True)
    return rstd * (dxhat - m1 - xhat * m2)


def _rsum8(x):
    r, n = x.shape
    return x.reshape(r // 8, 8, n).sum(axis=0)


def _rows(i, n=ROWS):
    return pl.ds(i * n, n)


def _loop(n, body):
    for i in range(n):
        body(i)


def _tril_mask():
    r = lax.broadcasted_iota(jnp.int32, (CHUNK, CHUNK), 0)
    c = lax.broadcasted_iota(jnp.int32, (CHUNK, CHUNK), 1)
    return c <= r


def _mixer_a_head(h_ref, r, hd, ga_ref, ba_ref, wsm_ref, bst_ref):
    sl = slice(hd * HEAD_DIM, (hd + 1) * HEAD_DIM)
    hu = h_ref[r, sl]
    hv = h_ref[r, D_A + hd * HEAD_DIM:D_A + (hd + 1) * HEAD_DIM]
    cdf_u = 0.5 * (1.0 + lax.erf(hu * INV_SQRT2))
    cdf_v = 0.5 * (1.0 + lax.erf(hv * INV_SQRT2))
    u = hu * cdf_u
    xhat, rstd = _ln_stats(hv * cdf_v)
    vn = (xhat * ga_ref[hd:hd + 1, :] + ba_ref[hd:hd + 1, :]).astype(BF16)
    sv = _nn(wsm_ref[hd], vn) + bst_ref[:, hd:hd + 1]
    return hu, hv, u, cdf_u, cdf_v, xhat, rstd, vn, sv


def _taps(win, offsets):
    n = win.shape[0]
    for s in range(8):
        ks = [k for k, o in enumerate(offsets) if o % 8 == s]
        if ks:
            moved = win if s == 0 else pltpu.roll(win, n - s, 0)
            for k in ks:
                yield k, moved[offsets[k] - s:offsets[k] - s + ROWS, :]


CONV_B_OFFSETS = [2 + k for k in range(KB)]
CONV_B_T_OFFSETS = [30 - k for k in range(KB)]


def _conv_b_block(ext_ref, base, cw_ref):
    acc = jnp.zeros((ROWS, D_B), F32)
    for k, tap in _taps(ext_ref[pl.ds(base, ROWS + HALO_B), :], CONV_B_OFFSETS):
        acc = acc + tap * cw_ref[k:k + 1, :]
    return acc


def _taps_f(win):
    n = ROWS + HALO_F
    return [pltpu.roll(win, n - 6, 0)[0:ROWS, :], pltpu.roll(win, n - 7, 0)[0:ROWS, :], win[8:n, :]]


def _params(sem, **kw):
    return pltpu.CompilerParams(dimension_semantics=sem, vmem_limit_bytes=VMEM_LIMIT, **kw)


def _resident(shape):
    zeros = (0,) * len(shape)
    return pl.BlockSpec(shape, lambda *_: zeros, pipeline_mode=pl.Buffered(1))


def _full(shape):
    zeros = (0,) * len(shape)
    return pl.BlockSpec(shape, lambda *_: zeros)


def _mesh_pos():
    return lax.axis_index("x"), lax.axis_index("y"), lax.axis_index("c")


def _chip_patterns(x, y):
    return [(x, y), (1 - x, y), (x, 1 - y), (1 - x, 1 - y)]


def _lid(x, y, c):
    return 4 * x + 2 * y + c


def _gather_copy(outs, send_sems, recv_sems, a, k, block, to, src=None):
    blk = outs[a].at[_lid(*block)]
    return pltpu.make_async_remote_copy(
        src_ref=blk if src is None else src, dst_ref=blk,
        send_sem=send_sems.at[a, k], recv_sem=recv_sems.at[a, k], device_id=to, device_id_type=MESH)


def _gather_start(mine, outs, send_sems, recv_sems, local_sems, diagonal=False):
    x, y, c = _mesh_pos()
    me = (x, y, c)
    for a in range(len(mine)):
        pltpu.make_async_copy(mine[a], outs[a].at[_lid(*me)], local_sems.at[a]).start()
        targets = [(x, y, 1 - c), (1 - x, y, c), (x, 1 - y, c), (1 - x, 1 - y, c)]
        for k, to in enumerate(targets if diagonal else targets[:3]):
            _gather_copy(outs, send_sems, recv_sems, a, k, me, to, src=mine[a]).start()


def _gather_relay(mine, outs, send_sems, recv_sems, local_sems, via):
    x, y, c = _mesh_pos()
    me, sib = (x, y, c), (x, y, 1 - c)
    copy = functools.partial(_gather_copy, outs, send_sems, recv_sems)
    source = {1: (1 - x, y, c), 2: (x, 1 - y, c)}
    for a in range(len(mine)):
        for k in ((via[a], 3 - via[a]) if via[a] else (1, 2)):
            copy(a, k, source[k], me).wait_recv()
            if k == via[a]:
                copy(a, 3, source[k], source[3 - k]).start()
            copy(a, 3 + k, source[k], sib).start()


def _gather_finish(mine, outs, send_sems, recv_sems, local_sems):
    x, y, c = _mesh_pos()
    me, sib = (x, y, c), (x, y, 1 - c)
    copy = functools.partial(_gather_copy, outs, send_sems, recv_sems)
    diag = (1 - x, 1 - y)
    n = len(mine)
    for a in range(n):
        copy(a, 3, (*diag, c), me).wait_recv()
        copy(a, 6, (*diag, c), sib).start()
    for a in range(n):
        copy(a, 0, sib, me).wait_recv()
        for k, chip in zip((4, 5, 6), [(1 - x, y), (x, 1 - y), diag]):
            copy(a, k, (*chip, 1 - c), me).wait_recv()
        for k in range(7):
            copy(a, k, me, sib, src=mine[a]).wait_send()
        pltpu.make_async_copy(mine[a], outs[a].at[_lid(*me)], local_sems.at[a]).wait()


def _gather_scratch(n):
    return [pltpu.SemaphoreType.DMA((n, 7)), pltpu.SemaphoreType.DMA((n, 7)), pltpu.SemaphoreType.DMA((n,))]


def prepare_weights(w_in, w_out, w_up_t, w_down, convp):
    def body(win_ref, wout_ref, wup_ref, wdown_ref, convp_ref,
             sin_ref, sout_ref, sup_ref, sdown_ref, gconv_ref, send_sems, recv_sems, local_sems):
        gather = ([convp_ref], [gconv_ref], send_sems, recv_sems, local_sems)
        _gather_start(*gather, diagonal=True)
        sin_ref[...] = win_ref[...].astype(BF16)
        sout_ref[...] = wout_ref[...].astype(BF16)
        sup_ref[...] = wup_ref[...].T.astype(BF16)
        sdown_ref[...] = wdown_ref[...].astype(BF16)
        _gather_relay(*gather, via=[0])
        _gather_finish(*gather)

    return pl.pallas_call(
        body, name="prepare_weights",
        out_shape=[jax.ShapeDtypeStruct(w_in.shape, BF16), jax.ShapeDtypeStruct(w_out.shape, BF16),
                   jax.ShapeDtypeStruct(w_up_t.shape[::-1], BF16), jax.ShapeDtypeStruct(w_down.shape, BF16),
                   jax.ShapeDtypeStruct((N_DEV,) + convp.shape, F32)],
        in_specs=[VMEM] * 5, out_specs=[VMEM] * 4 + [ANY],
        scratch_shapes=_gather_scratch(1),
        compiler_params=pltpu.CompilerParams(vmem_limit_bytes=VMEM_LIMIT),
    )(w_in, w_out, w_up_t, w_down, convp)


def _chip_copies(p, land, send_sems, recv_sems):
    x, y, c = _mesh_pos()
    return [pltpu.make_async_remote_copy(
        src_ref=p[a].at[k], dst_ref=land[a].at[k], send_sem=send_sems.at[a, k], recv_sem=recv_sems.at[a, k],
        device_id=(px, py, c), device_id_type=MESH)
        for k, (px, py) in enumerate(_chip_patterns(x, y)[1:]) for a in range(len(p))]


def chip_partials(gs, lands, jidx, steps):
    n = len(gs)
    blocks = [(1, g.shape[1] // steps, g.shape[2]) for g in gs]

    def body(j_ref, *refs):
        for a in range(n):
            refs[2 * n + a][...] = (refs[a][...] + refs[n + a][...]).astype(BF16)

    return pl.pallas_call(
        body, name="chip_partials",
        out_shape=[jax.ShapeDtypeStruct((3,) + g.shape[1:], BF16) for g in gs],
        grid_spec=pltpu.PrefetchScalarGridSpec(
            num_scalar_prefetch=1, grid=(3, steps),
            in_specs=[pl.BlockSpec(b, lambda k, i, j: (j[1 + k], i, 0)) for b in blocks]
            + [pl.BlockSpec(b, lambda k, i, j: (1 + k, i, 0)) for b in blocks],
            out_specs=[pl.BlockSpec(b, lambda k, i, j: (k, i, 0)) for b in blocks]),
        compiler_params=_params(("arbitrary", "arbitrary")),
    )(jidx, *gs, *lands)


def _adamw(w, g, m, v):
    m2 = ADAM_B1 * m + (1.0 - ADAM_B1) * g
    v2 = ADAM_B2 * v + (1.0 - ADAM_B2) * (g * g)
    m_hat = m2 / (1.0 - ADAM_B1 ** ADAM_STEP)
    v_hat = v2 / (1.0 - ADAM_B2 ** ADAM_STEP)
    delta = -ADAM_LR * (m_hat / (jnp.sqrt(v_hat) + ADAM_EPS) + ADAM_WD * w)
    return delta, m2, v2


def reduce_and_adamw(gs, lands, recvs, ws, ms, vs, jidx, steps):
    n = len(gs)
    rbs = [g.shape[1] // steps for g in gs]

    def body(j_ref, *refs):
        g, land, recv, w, m, v = (refs[k * n:(k + 1) * n] for k in range(6))
        outs = refs[6 * n:]
        for a in range(n):
            grad = ((g[a][0] + land[a][0]) + recv[a][0].astype(F32) + recv[a][1].astype(F32)
                    + recv[a][2].astype(F32))
            delta, m2, v2 = _adamw(w[a][...], grad, m[a][...], v[a][...])
            outs[a][...] = grad
            outs[n + a][...] = delta
            outs[2 * n + a][...] = m2
            outs[3 * n + a][...] = v2

    blk = [pl.BlockSpec((rb, g.shape[2]), lambda i, j: (i, 0)) for g, rb in zip(gs, rbs)]
    part = lambda lead, pick: [pl.BlockSpec((lead, rb, g.shape[2]), pick) for g, rb in zip(gs, rbs)]
    return pl.pallas_call(
        body, name="reduce_adamw",
        out_shape=[jax.ShapeDtypeStruct(g.shape[1:], F32) for g in gs] * 4,
        grid_spec=pltpu.PrefetchScalarGridSpec(
            num_scalar_prefetch=1, grid=(steps,),
            in_specs=part(1, lambda i, j: (j[0], i, 0)) + part(1, lambda i, j: (0, i, 0))
            + part(3, lambda i, j: (0, i, 0)) + blk * 3,
            out_specs=blk * 4),
        compiler_params=_params(("arbitrary",)),
    )(jidx, *gs, *lands, *recvs, *ws, *ms, *vs)


def mixer_reduce(grads, sv_parts):
    n = len(grads)
    shard = [g.shape[1:] for g in grads]
    ns = len(sv_parts)
    sv_shape = (sum(p.shape[0] for p in sv_parts), 128)

    def body(*refs):
        g = refs[:n]
        parts = refs[n:n + ns]
        outs = refs[n + ns:2 * n + ns]
        sv_slots = refs[2 * n + ns]
        rest = refs[2 * n + ns + 1:-1]
        sv_ref = refs[-1]
        row0 = 0
        for part in parts:
            sv_ref[row0:row0 + part.shape[0], :] = part[...]
            row0 += part.shape[0]
        own, land, sendb, recvb = rest[:n], rest[n:2 * n], rest[2 * n:3 * n], rest[3 * n:4 * n]
        sv_land, chip_sv, d2d_send, d2d_recv, ici_send, ici_recv, local_sems, sv_sems = rest[4 * n:]
        x, y, c = _mesh_pos()
        sib = (x, y, 1 - c)
        pats = _chip_patterns(x, y)
        q = 2 * x + y

        d2d, local = {}, {}
        for a in range(n):
            for k, (px, py) in enumerate(pats):
                d2d[a, k] = pltpu.make_async_remote_copy(
                    src_ref=g[a].at[_lid(px, py, 1 - c)], dst_ref=land[a].at[k],
                    send_sem=d2d_send.at[a, k], recv_sem=d2d_recv.at[a, k], device_id=sib, device_id_type=MESH)
                local[a, k] = pltpu.make_async_copy(g[a].at[_lid(px, py, c)], own[a].at[k], local_sems.at[a, k])
        sv_d2d = pltpu.make_async_remote_copy(
            src_ref=sv_ref, dst_ref=sv_land, send_sem=d2d_send.at[n, 0], recv_sem=d2d_recv.at[n, 0],
            device_id=sib, device_id_type=MESH)
        blocks = [(a, k) for a in range(n) for k in (1, 2, 3)] + [(a, 0) for a in range(n)]
        sv_d2d.start()
        for b in blocks:
            d2d[b].start()
            local[b].start()

        half_rows = sv_shape[0] // 2
        rows = pl.ds(pl.multiple_of(c * half_rows, 8), half_rows)
        sv_local = pltpu.make_async_copy(chip_sv, sv_slots.at[q], sv_sems.at[0])

        def sv_ici(k, slot, to):
            return pltpu.make_async_remote_copy(
                src_ref=chip_sv.at[rows], dst_ref=sv_slots.at[slot, rows], send_sem=sv_sems.at[1 + k],
                recv_sem=sv_sems.at[4 + k], device_id=to, device_id_type=MESH)

        def sv_pass_on(k, slot):
            return pltpu.make_async_remote_copy(
                src_ref=sv_slots.at[slot, rows], dst_ref=sv_slots.at[slot, rows], send_sem=sv_sems.at[7 + k],
                recv_sem=sv_sems.at[10 + k], device_id=sib, device_id_type=MESH)

        sv_d2d.wait()
        chip_sv[...] = sv_ref[...] + sv_land[...]
        sv_out = [sv_ici(k, q, (px, py, c)) for k, (px, py) in enumerate(pats[1:])]
        for cp in sv_out + [sv_local]:
            cp.start()

        ici = _chip_copies(sendb, recvb, ici_send, ici_recv)
        for a, k in blocks:
            local[a, k].wait()
            d2d[a, k].wait()
            if k > 0:
                sendb[a][k - 1] = (own[a][k] + land[a][k]).astype(BF16)
                ici[(k - 1) * n + a].start()
        for k, (px, py) in enumerate(pats[1:]):
            sv_out[k].wait_send()
            sv_ici(k, 2 * px + py, (px, py, c)).wait_recv()
            sv_pass_on(k, 2 * px + py).start()
        for a in range(n):
            for k in range(3):
                ici[k * n + a].wait()
            outs[a][...] = ((own[a][0] + land[a][0]) + recvb[a][0].astype(F32) + recvb[a][1].astype(F32)
                            + recvb[a][2].astype(F32))
        for k, (px, py) in enumerate(pats[1:]):
            sv_pass_on(k, 2 * px + py).wait()
        sv_local.wait()

    shard_out = [jax.ShapeDtypeStruct(s, F32) for s in shard]
    return pl.pallas_call(
        body, name="mixer_reduce",
        out_shape=shard_out + [jax.ShapeDtypeStruct((4,) + sv_shape, F32)],
        in_specs=[ANY] * n + [VMEM] * ns, out_specs=[VMEM] * n + [ANY],
        scratch_shapes=[pltpu.VMEM((4,) + s, F32) for s in shard] + [pltpu.VMEM((4,) + s, F32) for s in shard]
        + [pltpu.VMEM((3,) + s, BF16) for s in shard] + [pltpu.VMEM((3,) + s, BF16) for s in shard]
        + [pltpu.VMEM(sv_shape, F32), pltpu.VMEM(sv_shape, F32),
           pltpu.SemaphoreType.DMA((n + 1, 4)), pltpu.SemaphoreType.DMA((n + 1, 4)),
           pltpu.SemaphoreType.DMA((n, 3)), pltpu.SemaphoreType.DMA((n, 3)),
           pltpu.SemaphoreType.DMA((n, 4)), pltpu.SemaphoreType.DMA((13,)), pltpu.VMEM(sv_shape, F32)],
        compiler_params=pltpu.CompilerParams(vmem_limit_bytes=VMEM_LIMIT),
    )(*grads, *sv_parts)


SMALL_LAYOUT = [
    ("b_in", S_BIN, 16), ("ln_a_g", S_LNAG, 4), ("ln_a_b", S_LNAB, 4), ("w_spatial", S_WS, 512),
    ("b_spatial", S_BS, 4), ("conv_b_b", S_CBB, 4), ("ln_b_g", S_LNBG, 4), ("ln_b_b", S_LNBB, 4),
    ("b_out", S_BOUT, 8), ("ln1_g", S_LN1G, 8), ("ln1_b", S_LN1B, 8), ("conv_f_b", S_CFB, 44),
    ("ln2_g", S_LN2G, 8), ("ln2_b", S_LN2B, 8),
]


def small_adamw(sv_slots, ws, ms, vs, shard_grads, shard_ws, shard_ms, shard_vs):
    n = len(SMALL_LAYOUT)
    nb = len(shard_grads)

    def body(*refs):
        s_ref = refs[0]
        w_refs, m_refs, v_refs = refs[1:1 + n], refs[1 + n:1 + 2 * n], refs[1 + 2 * n:1 + 3 * n]
        big_in = refs[1 + 3 * n:1 + 3 * n + 4 * nb]
        outs = refs[1 + 3 * n + 4 * nb:]
        big_out = outs[4 * n + 1:]
        for p in range(nb):
            grad = big_in[p][...]
            delta, m2, v2 = _adamw(big_in[nb + p][...], grad, big_in[2 * nb + p][...], big_in[3 * nb + p][...])
            big_out[p][...] = grad
            big_out[nb + p][...] = delta
            big_out[2 * nb + p][...] = m2
            big_out[3 * nb + p][...] = v2
        for p, (_, row0, rows) in enumerate(SMALL_LAYOUT):
            sl = pl.ds(row0, rows)
            grad = ((s_ref[0, sl, :] + s_ref[1, sl, :]) + s_ref[2, sl, :]) + s_ref[3, sl, :]
            delta, m2, v2 = _adamw(w_refs[p][...], grad, m_refs[p][...], v_refs[p][...])
            outs[p][...] = grad
            outs[n + p][...] = delta
            outs[2 * n + p][...] = m2
            outs[3 * n + p][...] = v2
        sl = pl.ds(S_LOSS, 8)
        outs[4 * n][...] = ((s_ref[0, sl, :] + s_ref[1, sl, :]) + s_ref[2, sl, :]) + s_ref[3, sl, :]

    shapes = [jax.ShapeDtypeStruct((rows, 128), F32) for _, _, rows in SMALL_LAYOUT]
    big_shapes = [jax.ShapeDtypeStruct(g.shape, F32) for g in shard_grads]
    return pl.pallas_call(
        body, name="small_adamw",
        out_shape=shapes * 4 + [jax.ShapeDtypeStruct((8, 128), F32)] + big_shapes * 4,
        in_specs=[VMEM] * (1 + 3 * n + 4 * nb), out_specs=[VMEM] * (4 * n + 1 + 4 * nb),
        compiler_params=pltpu.CompilerParams(vmem_limit_bytes=VMEM_LIMIT),
    )(sv_slots, *ws, *ms, *vs, *shard_grads, *shard_ws, *shard_ms, *shard_vs)


def mix_forward(x, sin, sout, b_in, ln_a_g, ln_a_b, w_spatial, bst, conv_b_w, conv_b_b, ln_b_g, ln_b_b,
                b_out, ln1_g, ln1_b, sup, sdown, tm):
    t = x.shape[0]
    nt = t // tm
    n_chunks = tm // CHUNK

    def body(x_ref, sin_ref, sout_ref, bin_ref, ga_ref, ba_ref, ws_ref, bst_ref, cw_ref, cb_ref, gb_ref,
             bb_ref, bout_ref, g1_ref, b1_ref, sup_ref, sdown_ref,
             h_ref, xhat1_ref, rstd1_ref, yb1_ref, gin_ref, gout_ref, gup_ref, gdown_ref,
             ext_ref, y_ref, wsm_ref, win_ref, wout_ref, load_sems,
             mix_send, mix_recv, mix_local, send_sems, recv_sems, local_sems):
        i = pl.program_id(0)
        mixer = ([sin_ref, sout_ref], [gin_ref, gout_ref], mix_send, mix_recv, mix_local)
        gather = ([sup_ref, sdown_ref], [gup_ref, gdown_ref], send_sems, recv_sems, local_sems)

        @pl.when(i == 0)
        def _():
            _gather_start(*mixer)
            _gather_relay(*mixer, via=[2, 2])
            _gather_finish(*mixer)
            _gather_start(*gather)
            loads = [pltpu.make_async_copy(gin_ref, win_ref, load_sems.at[0]),
                     pltpu.make_async_copy(gout_ref, wout_ref, load_sems.at[1])]
            for cp in loads:
                cp.start()
            for cp in loads:
                cp.wait()
            ext_ref[0:HALO_B, :] = jnp.zeros((HALO_B, D_B), F32)
            mask = _tril_mask()
            for hd in range(HEADS):
                wsm_ref[hd] = jnp.where(mask, ws_ref[hd], 0.0).astype(BF16)

        xb = x_ref[...].astype(BF16)
        for j in range(N_DEV):
            cols = slice(j * W_IN_BLK, (j + 1) * W_IN_BLK)
            h_ref[:, cols] = _nn(xb, win_ref[j]) + bin_ref[:, cols]

        def chunk(ci):
            r = _rows(ci, CHUNK)
            for hd in range(HEADS):
                _, _, u, _, _, _, _, _, sv = _mixer_a_head(h_ref, r, hd, ga_ref, ba_ref, wsm_ref, bst_ref)
                y_ref[r, hd * HEAD_DIM:(hd + 1) * HEAD_DIM] = (u * sv).astype(BF16)
            a_b = h_ref[r, 2 * D_A:2 * D_A + D_B]
            g_b = h_ref[r, 2 * D_A + D_B:D_IN]
            ext_ref[pl.ds(HALO_B + ci * CHUNK, CHUNK), :] = a_b * _sigmoid(g_b)

        _loop(n_chunks, chunk)

        def conv_rows(bi):
            base = bi * ROWS
            yb1 = _conv_b_block(ext_ref, base, cw_ref) + cb_ref[...]
            yb1_ref[pl.ds(base, ROWS), :] = yb1
            xhat, _ = _ln_stats(yb1)
            yb2 = xhat * gb_ref[...] + bb_ref[...]
            y_ref[pl.ds(base, ROWS), D_A:D] = (yb2 * _sigmoid(yb2)).astype(BF16)

        _loop(tm // ROWS, conv_rows)
        ext_ref[0:HALO_B, :] = ext_ref[tm:tm + HALO_B, :]

        mix = _nn(y_ref[...], wout_ref[...].reshape(D, D)) + bout_ref[...]
        xhat1, rstd1 = _ln_stats(ALPHA * x_ref[...] + mix)
        xhat1_ref[...] = xhat1
        rstd1_ref[...] = jnp.broadcast_to(rstd1, (tm, 128))

        @pl.when(i == (5 * nt) // 8)
        def _():
            _gather_relay(*gather, via=[1, 2])

        @pl.when(i == nt - 1)
        def _():
            _gather_finish(*gather)

    row = lambda w: pl.BlockSpec((tm, w), lambda i: (i, 0))
    return pl.pallas_call(
        body, name="mix_forward", grid=(nt,),
        in_specs=[row(D), ANY, ANY, _full(b_in.shape), _full(ln_a_g.shape),
                  _full(ln_a_b.shape), _full(w_spatial.shape), _full(bst.shape),
                  _full(conv_b_w.shape), _full(conv_b_b.shape), _full(ln_b_g.shape),
                  _full(ln_b_b.shape), _full(b_out.shape),
                  _full(ln1_g.shape), _full(ln1_b.shape), ANY, ANY],
        out_specs=[row(D_IN), row(D), row(128), row(D_B), ANY, ANY, ANY, ANY],
        out_shape=[jax.ShapeDtypeStruct((t, D_IN), F32), jax.ShapeDtypeStruct((t, D), F32),
                   jax.ShapeDtypeStruct((t, 128), F32), jax.ShapeDtypeStruct((t, D_B), F32)]
        + [jax.ShapeDtypeStruct((N_DEV,) + sh.shape, BF16) for sh in (sin, sout, sup, sdown)],
        scratch_shapes=[pltpu.VMEM((tm + HALO_B, D_B), F32), pltpu.VMEM((tm, D), BF16),
                        pltpu.VMEM((HEADS, CHUNK, CHUNK), BF16),
                        pltpu.VMEM((N_DEV,) + sin.shape, BF16), pltpu.VMEM((N_DEV,) + sout.shape, BF16),
                        pltpu.SemaphoreType.DMA((2,))] + _gather_scratch(2) + _gather_scratch(2),
        compiler_params=_params(("arbitrary",)),
    )(x, sin, sout, b_in, ln_a_g, ln_a_b, w_spatial, bst, conv_b_w, conv_b_b, ln_b_g, ln_b_b,
      b_out, ln1_g, ln1_b, sup, sdown)


def ffn_forward(xhat1, ln1_g, ln1_b, wup_g, cfw, cfb, wdown, ln2_g, ln2_b, target, tm):
    t = xhat1.shape[0]
    nt = t // tm

    def body(xh_hbm, g1_ref, b1_ref, wup_ref, cfw_ref, cfb_ref, wdown_ref, g2_ref, b2_ref, tgt_hbm,
             hu_ref, gv_ref, dr2_ref, loss_ref, sln2_ref,
             x1_ref, x1b_ref, hu32_ref, carry_ref, gbuf_ref, ffn_ref, acc_loss, acc_g2, acc_b2,
             xh_ring, tgt_ring, ring_sems):
        i = pl.program_id(0)

        def fetch(s, slot):
            rows = pl.ds(pl.multiple_of(s * tm, tm), tm)
            return [pltpu.make_async_copy(xh_hbm.at[rows], xh_ring.at[slot], ring_sems.at[0, slot]),
                    pltpu.make_async_copy(tgt_hbm.at[rows], tgt_ring.at[slot], ring_sems.at[1, slot])]

        @pl.when(i == 0)
        def _():
            for s in range(min(2, nt)):
                for cp in fetch(s, s):
                    cp.start()
            carry_ref[...] = jnp.zeros(carry_ref.shape, F32)
            acc_loss[...] = jnp.zeros(acc_loss.shape, F32)
            acc_g2[...] = jnp.zeros(acc_g2.shape, F32)
            acc_b2[...] = jnp.zeros(acc_b2.shape, F32)

        @pl.when(i + 2 < nt)
        def _():
            for cp in fetch(i + 2, (i + 2) % 3):
                cp.start()

        for cp in fetch(i, i % 3):
            cp.wait()
        xh_ref = xh_ring.at[i % 3]
        tgt_ref = tgt_ring.at[i % 3]

        x1 = xh_ref[...] * g1_ref[...] + b1_ref[...]
        x1_ref[...] = x1
        x1b_ref[...] = x1.astype(BF16)

        def conv(g, j, base):
            if base == 0:
                win = jnp.concatenate([carry_ref[j], hu32_ref[g, 0:ROWS, :]], axis=0)
            else:
                win = hu32_ref[g, base - HALO_F:base + ROWS, :]
            taps = _taps_f(win)
            w = cfw_ref[j]
            return sum(taps[k] * w[k:k + 1, :] for k in range(KF)) + cfb_ref[j:j + 1, :]

        for f in range(N_F):
            hu32_ref[0] = _nn(x1b_ref[...], wup_ref[f])
            hu32_ref[1] = _nn(x1b_ref[...], wup_ref[N_F + f])

            def rows(bi, f=f):
                r = _rows(bi)
                gate = conv(0, f, bi * ROWS)
                val = conv(1, N_F + f, bi * ROWS)
                gbuf_ref[r, :] = (gate * _sigmoid(gate) * val).astype(BF16)
                gv_ref[f, r, :] = gate.astype(BF16)
                gv_ref[N_F + f, r, :] = val.astype(BF16)
                hu_ref[f, r, :] = hu32_ref[0, r, :].astype(BF16)
                hu_ref[N_F + f, r, :] = hu32_ref[1, r, :].astype(BF16)

            _loop(tm // ROWS, rows)
            carry_ref[f] = hu32_ref[0, tm - HALO_F:tm, :]
            carry_ref[N_F + f] = hu32_ref[1, tm - HALO_F:tm, :]
            part = _nn(gbuf_ref[...], wdown_ref[f])
            if f == 0:
                ffn_ref[...] = part
            else:
                ffn_ref[...] += part

        def tail(bi):
            r = _rows(bi, LN_ROWS)
            xhat2, rstd2 = _ln_stats(ALPHA * x1_ref[r, :] + ffn_ref[r, :])
            err = xhat2 * g2_ref[...] + b2_ref[...] - tgt_ref[r, :]
            e2 = _rsum8(err * err)
            acc_loss[...] += sum(e2[:, k * 128:(k + 1) * 128] for k in range(D // 128))
            dy = err * (1.0 / D)
            acc_g2[...] += _rsum8(dy * xhat2)
            acc_b2[...] += _rsum8(dy)
            dr2_ref[r, :] = _ln_bwd(dy * g2_ref[...], xhat2, rstd2)

        _loop(tm // LN_ROWS, tail)
        loss_ref[...] = acc_loss[...]

        @pl.when(i == nt - 1)
        def _():
            dg = jnp.sum(acc_g2[...], axis=0, keepdims=True)
            db = jnp.sum(acc_b2[...], axis=0, keepdims=True)
            for k in range(D // 128):
                sln2_ref[k:k + 1, :] = dg[:, k * 128:(k + 1) * 128]
                sln2_ref[8 + k:9 + k, :] = db[:, k * 128:(k + 1) * 128]

    row = pl.BlockSpec((tm, D), lambda i: (i, 0))
    return pl.pallas_call(
        body, name="ffn_forward", grid=(nt,),
        in_specs=[ANY, _full(ln1_g.shape), _full(ln1_b.shape), _resident(wup_g.shape),
                  _full(cfw.shape), _full(cfb.shape), _resident(wdown.shape),
                  _full(ln2_g.shape), _full(ln2_b.shape), ANY],
        out_specs=[pl.BlockSpec((N_DEV, tm, W_UP_BLK), lambda i: (0, i, 0)),
                   pl.BlockSpec((N_DEV, tm, W_UP_BLK), lambda i: (0, i, 0)), row,
                   _full((8, 128)), _full((16, 128))],
        out_shape=[jax.ShapeDtypeStruct((N_DEV, t, W_UP_BLK), BF16),
                   jax.ShapeDtypeStruct((N_DEV, t, W_UP_BLK), BF16), jax.ShapeDtypeStruct((t, D), F32),
                   jax.ShapeDtypeStruct((8, 128), F32), jax.ShapeDtypeStruct((16, 128), F32)],
        scratch_shapes=[pltpu.VMEM((tm, D), F32), pltpu.VMEM((tm, D), BF16),
                        pltpu.VMEM((2, tm, W_UP_BLK), F32),
                        pltpu.VMEM((N_DEV, HALO_F, W_UP_BLK), F32), pltpu.VMEM((tm, W_UP_BLK), BF16),
                        pltpu.VMEM((tm, D), F32), pltpu.VMEM((8, 128), F32),
                        pltpu.VMEM((8, D), F32), pltpu.VMEM((8, D), F32),
                        pltpu.VMEM((3, tm, D), F32), pltpu.VMEM((3, tm, D), F32),
                        pltpu.SemaphoreType.DMA((2, 3))],
        compiler_params=_params(("arbitrary",)),
    )(xhat1, ln1_g, ln1_b, wup_g, cfw, cfb, wdown, ln2_g, ln2_b, target)


def ffn_backward(order, dr2, xhat1, ln1_g, ln1_b, hu, gv, wup_g, cfw, wdown, tm):
    t = dr2.shape[0]
    nt = t // tm
    sub_rows = tm
    hu4 = hu.reshape(2, N_F, t, W_UP_BLK)
    gv4 = gv.reshape(2, N_F, t, W_UP_BLK)
    wup4 = wup_g.reshape(2, N_F, D, W_UP_BLK)
    cfw4 = cfw.reshape(2, N_F, KF, W_UP_BLK)

    def body(order_ref, dr2_ref, xh_ref, g1_ref, b1_ref, hu_ref, gv_ref, wup_ref, cfw_ref, wdown_ref,
             dwup_ref, dwdown_ref, dcfw_ref, dcfb_ref, dx1_ref, land_up_ref, land_down_ref,
             x1b_ref, drb_ref, dg_ref, dextg_ref, dextv_ref, gbuf_ref,
             dhug_ref, dhuv_ref, acc_wup, acc_wdown, acc_cfw, acc_cfb, sem, send_sems, recv_sems):
        fo = pl.program_id(0)
        f = order_ref[fo]
        f_prev = order_ref[jnp.maximum(fo - 1, 0)]
        slot = fo % 2
        i = pl.program_id(1)
        x, y, c = _mesh_pos()
        half = D_FF // N_DEV

        def to_sibling(fi, k, src, land_ref, shard_chip):
            d = jnp.bitwise_xor(shard_chip, 2 * x + y)
            slot = jnp.where(d == 1, 2, jnp.where(d == 2, 1, d))
            return pltpu.make_async_remote_copy(
                src_ref=src, dst_ref=land_ref.at[slot], send_sem=send_sems.at[fi, k], recv_sem=recv_sems.at[fi, k],
                device_id=(x, y, 1 - c), device_id_type=MESH)

        def up_copy(fi, g):
            return to_sibling(fi, g, dwup_ref.at[g, fi], land_up_ref, 2 * g + fi // 2)

        def down_copy(fi):
            return to_sibling(fi, 2, dwdown_ref.at[fi, pl.ds((1 - c) * half, half)], land_down_ref, fi)

        def flush(fi, s):
            return [pltpu.make_async_copy(acc_wup.at[s, 0], dwup_ref.at[0, fi], sem.at[s, 0]),
                    pltpu.make_async_copy(acc_wup.at[s, 1], dwup_ref.at[1, fi], sem.at[s, 1]),
                    pltpu.make_async_copy(acc_wdown.at[s], dwdown_ref.at[fi], sem.at[s, 2])]

        def flushed(fi, s):
            for cp in flush(fi, s):
                cp.wait()
            down_copy(fi).start()

            @pl.when(fi % 2 != c)
            def _():
                up_copy(fi, 0).start()
                up_copy(fi, 1).start()

        @pl.when(i == 0)
        def _():
            acc_wup[slot] = jnp.zeros(acc_wup.shape[1:], F32)
            acc_wdown[slot] = jnp.zeros(acc_wdown.shape[1:], F32)
            acc_cfw[...] = jnp.zeros(acc_cfw.shape, F32)
            acc_cfb[...] = jnp.zeros(acc_cfb.shape, F32)
            dextg_ref[tm:tm + HALO_F, :] = jnp.zeros((HALO_F, W_UP_BLK), F32)
            dextv_ref[tm:tm + HALO_F, :] = jnp.zeros((HALO_F, W_UP_BLK), F32)

        w = [cfw_ref[0, 0], cfw_ref[1, 0]]
        dext = [dextg_ref, dextv_ref]
        dhu = [dhug_ref, dhuv_ref]

        def rows1(bi):
            r = _rows(bi)
            gate = gv_ref[0, 0, r, :].astype(F32)
            val = gv_ref[1, 0, r, :].astype(F32)
            sg = _sigmoid(gate)
            silu = gate * sg
            gbuf_ref[r, :] = (silu * val).astype(BF16)
            dg = dg_ref[r, :]
            dgate = dg * val * (sg * (1.0 + gate * (1.0 - sg)))
            dval = dg * silu
            dextg_ref[r, :] = dgate
            dextv_ref[r, :] = dval
            acc_cfb[0:8, :] += _rsum8(dgate)
            acc_cfb[8:16, :] += _rsum8(dval)

        def rows2(bi):
            r = _rows(bi)
            for g in range(2):
                win = dext[g][pl.ds(bi * ROWS, ROWS + HALO_F), :]
                n = ROWS + HALO_F
                later = [pltpu.roll(win, n - 2, 0)[0:ROWS, :], pltpu.roll(win, n - 1, 0)[0:ROWS, :],
                         win[0:ROWS, :]]
                d = sum(later[k] * w[g][k:k + 1, :] for k in range(KF))
                dhu[g][r, :] = d.astype(BF16)
                pre = hu_ref[g, 0, r, :].astype(F32)
                for k in range(KF):
                    r0 = 8 * (g * KF + k)
                    acc_cfw[r0:r0 + 8, :] += _rsum8(later[k] * pre)

        for sub in reversed(range(tm // sub_rows)):
            rs = slice(sub * sub_rows, (sub + 1) * sub_rows)
            blocks = range(sub * sub_rows // ROWS, (sub + 1) * sub_rows // ROWS)
            x1b_ref[rs, :] = (xh_ref[rs, :] * g1_ref[...] + b1_ref[...]).astype(BF16)
            drb_ref[rs, :] = dr2_ref[rs, :].astype(BF16)
            dg_ref[rs, :] = _nt(drb_ref[rs, :], wdown_ref[0])
            for bi in blocks:
                rows1(bi)
            for bi in blocks:
                rows2(bi)
            acc_wdown[slot] += _tn(gbuf_ref[rs, :], drb_ref[rs, :])
            acc_wup[slot, 0] += _tn(dhug_ref[rs, :], x1b_ref[rs, :])
            acc_wup[slot, 1] += _tn(dhuv_ref[rs, :], x1b_ref[rs, :])
            dx1_ref[0, rs, :] = (_nt(dhug_ref[rs, :], wup_ref[0, 0])
                                 + _nt(dhuv_ref[rs, :], wup_ref[1, 0])).astype(BF16)
        dextg_ref[tm:tm + HALO_F, :] = dextg_ref[0:HALO_F, :]
        dextv_ref[tm:tm + HALO_F, :] = dextv_ref[0:HALO_F, :]

        @pl.when(i == nt - 1)
        def _():
            for g in range(2):
                dcfb_ref[g, 0] = jnp.sum(acc_cfb[8 * g:8 * g + 8, :], axis=0, keepdims=True)
                for k in range(KF):
                    r0 = 8 * (g * KF + k)
                    dcfw_ref[g, 0, k:k + 1, :] = jnp.sum(acc_cfw[r0:r0 + 8, :], axis=0, keepdims=True)
            for cp in flush(f, slot):
                cp.start()

        @pl.when((i == 0) & (fo > 0))
        def _():
            flushed(f_prev, 1 - slot)

        @pl.when((i == nt - 1) & (fo == N_F - 1))
        def _():
            rows = pl.ds(pl.multiple_of((1 - c) * half, 8), half)
            to_sibling(f, 2, acc_wdown.at[slot, rows], land_down_ref, f).start()
            for cp in flush(f, slot):
                cp.wait()

            @pl.when(f % 2 != c)
            def _():
                up_copy(f, 0).start()
                up_copy(f, 1).start()

            for fi in range(N_F):
                down_copy(fi).wait()
                for g in range(2):
                    @pl.when(fi % 2 != c)
                    def _():
                        up_copy(fi, g).wait_send()

                    @pl.when(fi % 2 == c)
                    def _():
                        up_copy(fi, g).wait_recv()

    rev = lambda i: nt - 1 - i
    row = pl.BlockSpec((tm, D), lambda fo, i, o: (rev(i), 0))
    pair = lambda r, c: pl.BlockSpec((2, 1, r, c), lambda fo, i, o: (0, o[fo], 0, 0))
    tile = pl.BlockSpec((2, 1, tm, W_UP_BLK), lambda fo, i, o: (0, o[fo], rev(i), 0))
    return pl.pallas_call(
        body, name="ffn_backward",
        grid_spec=pltpu.PrefetchScalarGridSpec(
            num_scalar_prefetch=1, grid=(N_F, nt),
            in_specs=[row, row, _full(ln1_g.shape), _full(ln1_b.shape), tile, tile,
                      pair(D, W_UP_BLK), pair(KF, W_UP_BLK),
                      pl.BlockSpec((1, W_UP_BLK, D), lambda fo, i, o: (o[fo], 0, 0))],
            out_specs=[ANY, ANY, pair(KF, W_UP_BLK), pair(1, W_UP_BLK),
                       pl.BlockSpec((1, tm, D), lambda fo, i, o: (o[fo], rev(i), 0)), ANY, ANY],
            scratch_shapes=[pltpu.VMEM((tm, D), BF16), pltpu.VMEM((tm, D), BF16),
                            pltpu.VMEM((tm, W_UP_BLK), F32),
                            pltpu.VMEM((tm + HALO_F, W_UP_BLK), F32), pltpu.VMEM((tm + HALO_F, W_UP_BLK), F32),
                            pltpu.VMEM((tm, W_UP_BLK), BF16), pltpu.VMEM((tm, W_UP_BLK), BF16),
                            pltpu.VMEM((tm, W_UP_BLK), BF16),
                            pltpu.VMEM((2, 2, W_UP_BLK, D), F32), pltpu.VMEM((2, W_UP_BLK, D), F32),
                            pltpu.VMEM((2 * KF * 8, W_UP_BLK), F32), pltpu.VMEM((16, W_UP_BLK), F32),
                            pltpu.SemaphoreType.DMA((2, 3)),
                            pltpu.SemaphoreType.DMA((N_F, 3)), pltpu.SemaphoreType.DMA((N_F, 3))]),
        out_shape=[jax.ShapeDtypeStruct((2, N_F, W_UP_BLK, D), F32),
                   jax.ShapeDtypeStruct((N_F, W_UP_BLK, D), F32),
                   jax.ShapeDtypeStruct((2, N_F, KF, W_UP_BLK), F32),
                   jax.ShapeDtypeStruct((2, N_F, 1, W_UP_BLK), F32),
                   jax.ShapeDtypeStruct((N_F, t, D), BF16),
                   jax.ShapeDtypeStruct((4, W_UP_BLK, D), F32),
                   jax.ShapeDtypeStruct((4, D_FF // N_DEV, D), F32)],
        compiler_params=_params(("arbitrary", "arbitrary")),
    )(order, dr2, xhat1, ln1_g, ln1_b, hu4, gv4, wup4, cfw4, wdown)


def mix_backward(x, h, yb1, dx1p, dr2, xhat1, rstd1, win_g, ln_a_g, ln_a_b, w_spatial, bst,
                 conv_b_w, ln_b_g, ln_b_b, wout, ln1_g, ffn_partials, tm):
    t = x.shape[0]
    n_p = len(ffn_partials)
    nt = t // tm
    n_chunks = tm // CHUNK
    halo_blocks = tm // HALO_B

    def body(x_ref, h_ref, halo_ref, yb1_ref, dx1p_ref, dr2_ref, xh1_ref, rstd1_ref, win_ref, ga_ref, ba_ref,
             ws_ref, bst_ref, cw_ref, gb_ref, bb_ref, wout_ref, g1_ref, *rest):
        p_refs, rest = rest[:n_p], rest[n_p:]
        gx_ref, dwin_ref, dwout_ref, dcw_ref, small_ref = rest[:5]
        land_refs, rest = rest[5:5 + n_p], rest[5 + n_p:]
        (ext_ref, dext_ref, y_ref, dy_ref, dh_ref, dmb_ref, wsm_ref,
         acc_win, acc_wout, acc_bin, acc_lnag, acc_lnab, acc_ws, acc_bs, acc_cbb, acc_lnbg,
         acc_lnbb, acc_bout, acc_ln1g, acc_ln1b, acc_cw, sem, send_sems, recv_sems) = rest
        i = pl.program_id(0)

        @pl.when(i == 0)
        def _():
            for cp in _chip_copies(p_refs, land_refs, send_sems, recv_sems):
                cp.start()

        first_tile = i == nt - 1
        accs = [acc_win, acc_wout, acc_bin, acc_lnag, acc_lnab, acc_ws, acc_bs, acc_cbb, acc_lnbg,
                acc_lnbb, acc_bout, acc_ln1g, acc_ln1b, acc_cw]

        @pl.when(i == 0)
        def _():
            for acc in accs:
                acc[...] = jnp.zeros(acc.shape, F32)
            dext_ref[tm:tm + HALO_B, :] = jnp.zeros((HALO_B, D_B), F32)
            mask = _tril_mask()
            for hd in range(HEADS):
                wsm_ref[hd] = jnp.where(mask, ws_ref[hd], 0.0).astype(BF16)

        def ln1_rows(bi):
            r = _rows(bi, LN_ROWS)
            part = [dx1p_ref[f, r, :].astype(F32) for f in range(N_F)]
            dx1 = ALPHA * dr2_ref[r, :] + ((part[0] + part[1]) + (part[2] + part[3]))
            xhat = xh1_ref[r, :]
            acc_ln1g[...] += _rsum8(dx1 * xhat)
            acc_ln1b[...] += _rsum8(dx1)
            dr1 = _ln_bwd(dx1 * g1_ref[...], xhat, rstd1_ref[r, 0:1])
            acc_bout[...] += _rsum8(dr1)
            gx_ref[r, :] = ALPHA * dr1
            dmb_ref[r, :] = dr1.astype(BF16)

        _loop(tm // LN_ROWS, ln1_rows)
        dy_ref[...] = _nt(dmb_ref[...], wout_ref[...])

        ha = halo_ref[:, 0:D_B]
        hg = halo_ref[:, D_B:2 * D_B]
        ext_ref[0:HALO_B, :] = jnp.where(first_tile, 0.0, 1.0) * (ha * _sigmoid(hg))

        def chunk(ci):
            r = _rows(ci, CHUNK)
            for hd in range(HEADS):
                sl = slice(hd * HEAD_DIM, (hd + 1) * HEAD_DIM)
                rows8 = slice(8 * hd, 8 * hd + 8)
                hus, hvs, u, cdf_u, cdf_v, xhat, rstd, vn, sv = _mixer_a_head(
                    h_ref, r, hd, ga_ref, ba_ref, wsm_ref, bst_ref)
                dy_a = dy_ref[r, sl]
                y_ref[r, sl] = (u * sv).astype(BF16)
                du = dy_a * sv
                dsv = dy_a * u
                dsvb = dsv.astype(BF16)
                acc_bs[hd] += dsv
                acc_ws[hd] += _nt(dsvb, vn)
                dvn = _tn(wsm_ref[hd], dsvb)
                acc_lnag[rows8, :] += _rsum8(dvn * xhat)
                acc_lnab[rows8, :] += _rsum8(dvn)
                dv = _ln_bwd(dvn * ga_ref[hd:hd + 1, :], xhat, rstd)
                slv = slice(D_A + hd * HEAD_DIM, D_A + (hd + 1) * HEAD_DIM)
                dhu = du * (cdf_u + hus * jnp.exp(-0.5 * hus * hus) * INV_SQRT_2PI)
                dhv = dv * (cdf_v + hvs * jnp.exp(-0.5 * hvs * hvs) * INV_SQRT_2PI)
                acc_bin[:, sl] += _rsum8(dhu)
                acc_bin[:, slv] += _rsum8(dhv)
                dh_ref[r, sl] = dhu.astype(BF16)
                dh_ref[r, slv] = dhv.astype(BF16)
            a_b = h_ref[r, 2 * D_A:2 * D_A + D_B]
            g_b = h_ref[r, 2 * D_A + D_B:D_IN]
            ext_ref[pl.ds(HALO_B + ci * CHUNK, CHUNK), :] = a_b * _sigmoid(g_b)

        _loop(n_chunks, chunk)

        def conv_rows(bi):
            base = bi * ROWS
            r = pl.ds(base, ROWS)
            xhat, rstd = _ln_stats(yb1_ref[r, :])
            yb2 = xhat * gb_ref[...] + bb_ref[...]
            sg = _sigmoid(yb2)
            y_ref[r, D_A:D] = (yb2 * sg).astype(BF16)
            dyb2 = dy_ref[r, D_A:D] * (sg * (1.0 + yb2 * (1.0 - sg)))
            acc_lnbg[...] += _rsum8(dyb2 * xhat)
            acc_lnbb[...] += _rsum8(dyb2)
            dyb1 = _ln_bwd(dyb2 * gb_ref[...], xhat, rstd)
            acc_cbb[...] += _rsum8(dyb1)
            dext_ref[r, :] = dyb1
            for k, tap in _taps(ext_ref[pl.ds(base, ROWS + HALO_B), :], CONV_B_OFFSETS):
                acc_cw[8 * k:8 * k + 8, :] += _rsum8(dyb1 * tap)

        _loop(tm // ROWS, conv_rows)

        def convt_rows(bi):
            base = bi * ROWS
            r = pl.ds(base, ROWS)
            dyb0 = jnp.zeros((ROWS, D_B), F32)
            for k, tap in _taps(dext_ref[pl.ds(base, ROWS + HALO_B), :], CONV_B_T_OFFSETS):
                dyb0 = dyb0 + tap * cw_ref[k:k + 1, :]
            a_b = h_ref[r, 2 * D_A:2 * D_A + D_B]
            sg = _sigmoid(h_ref[r, 2 * D_A + D_B:D_IN])
            da_b = dyb0 * sg
            dg_b = dyb0 * a_b * sg * (1.0 - sg)
            acc_bin[:, 2 * D_A:2 * D_A + D_B] += _rsum8(da_b)
            acc_bin[:, 2 * D_A + D_B:D_IN] += _rsum8(dg_b)
            dh_ref[r, 2 * D_A:2 * D_A + D_B] = da_b.astype(BF16)
            dh_ref[r, 2 * D_A + D_B:D_IN] = dg_b.astype(BF16)

        _loop(tm // ROWS, convt_rows)
        dext_ref[tm:tm + HALO_B, :] = dext_ref[0:HALO_B, :]

        acc_wout[...] += _tn(y_ref[...], dmb_ref[...])
        xt = x_ref[...].T.astype(BF16)
        dh_blocks = [dh_ref[:, j * W_IN_BLK:(j + 1) * W_IN_BLK] for j in range(N_DEV)]
        for j in range(N_DEV):
            acc_win[j] += _nn(xt, dh_blocks[j])
        gx_ref[...] += sum(_nt(dh_blocks[j], win_ref[j]) for j in range(N_DEV))

        @pl.when(i == nt - 1)
        def _():
            cps = [pltpu.make_async_copy(acc_win, dwin_ref, sem.at[0]),
                   pltpu.make_async_copy(acc_wout, dwout_ref, sem.at[1])]
            for cp in cps:
                cp.start()
            small_ref[...] = jnp.zeros(small_ref.shape, F32)

            def put_row_vector(row0, acc):
                vec = jnp.sum(acc[...], axis=0, keepdims=True)
                for k in range(vec.shape[1] // 128):
                    small_ref[row0 + k:row0 + k + 1, :] = vec[:, k * 128:(k + 1) * 128]

            put_row_vector(S_BIN, acc_bin)
            put_row_vector(S_CBB, acc_cbb)
            put_row_vector(S_LNBG, acc_lnbg)
            put_row_vector(S_LNBB, acc_lnbb)
            put_row_vector(S_BOUT, acc_bout)
            put_row_vector(S_LN1G, acc_ln1g)
            put_row_vector(S_LN1B, acc_ln1b)
            mask = _tril_mask()
            for hd in range(HEADS):
                rows8 = slice(8 * hd, 8 * hd + 8)
                small_ref[S_LNAG + hd:S_LNAG + hd + 1, :] = jnp.sum(acc_lnag[rows8, :], axis=0, keepdims=True)
                small_ref[S_LNAB + hd:S_LNAB + hd + 1, :] = jnp.sum(acc_lnab[rows8, :], axis=0, keepdims=True)
                small_ref[S_WS + hd * CHUNK:S_WS + (hd + 1) * CHUNK, :] = jnp.where(mask, acc_ws[hd], 0.0)
                small_ref[S_BS + hd:S_BS + hd + 1, :] = jnp.sum(acc_bs[hd].T, axis=0, keepdims=True)
            for k in range(KB):
                dcw_ref[k:k + 1, :] = jnp.sum(acc_cw[8 * k:8 * k + 8, :], axis=0, keepdims=True)
            for cp in cps:
                cp.wait()
            for cp in _chip_copies(p_refs, land_refs, send_sems, recv_sems):
                cp.wait()

    rev = lambda i: nt - 1 - i
    row = lambda w: pl.BlockSpec((tm, w), lambda i: (rev(i), 0))
    return pl.pallas_call(
        body, name="mix_backward", grid=(nt,),
        in_specs=[row(D), row(D_IN),
                  pl.BlockSpec((HALO_B, 2 * D_B), lambda i: (jnp.maximum(rev(i) * halo_blocks - 1, 0), 1)),
                  row(D_B), pl.BlockSpec((N_F, tm, D), lambda i: (0, rev(i), 0)),
                  row(D), row(D), row(128), _resident(win_g.shape), _full(ln_a_g.shape),
                  _full(ln_a_b.shape), _full(w_spatial.shape), _full(bst.shape), _full(conv_b_w.shape),
                  _full(ln_b_g.shape), _full(ln_b_b.shape),
                  _resident(wout.shape), _full(ln1_g.shape)] + [ANY] * n_p,
        out_specs=[row(D), ANY, ANY, _full((KB, D_B)), _full((S_MIX_ROWS, 128))] + [ANY] * n_p,
        out_shape=[jax.ShapeDtypeStruct((t, D), F32), jax.ShapeDtypeStruct((N_DEV, D, W_IN_BLK), F32),
                   jax.ShapeDtypeStruct((D, D), F32), jax.ShapeDtypeStruct((KB, D_B), F32),
                   jax.ShapeDtypeStruct((S_MIX_ROWS, 128), F32)]
        + [jax.ShapeDtypeStruct(p.shape, BF16) for p in ffn_partials],
        scratch_shapes=[pltpu.VMEM((tm + HALO_B, D_B), F32), pltpu.VMEM((tm + HALO_B, D_B), F32),
                        pltpu.VMEM((tm, D), BF16), pltpu.VMEM((tm, D), F32), pltpu.VMEM((tm, D_IN), BF16),
                        pltpu.VMEM((tm, D), BF16),
                        pltpu.VMEM((HEADS, CHUNK, CHUNK), BF16),
                        pltpu.VMEM((N_DEV, D, W_IN_BLK), F32), pltpu.VMEM((D, D), F32),
                        pltpu.VMEM((8, D_IN), F32), pltpu.VMEM((8 * HEADS, HEAD_DIM), F32),
                        pltpu.VMEM((8 * HEADS, HEAD_DIM), F32), pltpu.VMEM((HEADS, CHUNK, CHUNK), F32),
                        pltpu.VMEM((HEADS, CHUNK, CHUNK), F32), pltpu.VMEM((8, D_B), F32),
                        pltpu.VMEM((8, D_B), F32), pltpu.VMEM((8, D_B), F32), pltpu.VMEM((8, D), F32),
                        pltpu.VMEM((8, D), F32), pltpu.VMEM((8, D), F32), pltpu.VMEM((8 * KB, D_B), F32),
                        pltpu.SemaphoreType.DMA((2,)),
                        pltpu.SemaphoreType.DMA((n_p, 3)), pltpu.SemaphoreType.DMA((n_p, 3))],
        compiler_params=_params(("arbitrary",)),
    )(x, h, h, yb1, dx1p, dr2, xhat1, rstd1, win_g, ln_a_g, ln_a_b, w_spatial, bst, conv_b_w,
      ln_b_g, ln_b_b, wout, ln1_g, *ffn_partials)


def _rows128(a):
    return a.reshape(-1, 128)


def _pack_conv(cb, cf):
    lead = cb.shape[:-2]
    pad = [(0, 0)] * len(lead)
    flat = jnp.pad(cb.reshape(lead + (KB * 64,)), pad + [(0, 3 * W_UP_BLK - KB * 64)])
    rows = jnp.concatenate([cf, flat.reshape(lead + (3, W_UP_BLK))], axis=-2)
    return jnp.pad(rows, pad + [(0, 2), (0, 768 - W_UP_BLK)])


def _unpack_conv(p):
    lead = p.shape[:-2]
    cf = p[..., 0:KF, 0:W_UP_BLK]
    cb = p[..., 3:6, 0:W_UP_BLK].reshape(lead + (3 * W_UP_BLK,))[..., :KB * 64].reshape(lead + (KB, 64))
    return cb, cf


def kernel(x, w_in, b_in, ln_a_g, ln_a_b, w_spatial, b_spatial, conv_b_w, conv_b_b, ln_b_g, ln_b_b, w_out, b_out, ln1_g, ln1_b, w_up, conv_f_w, conv_f_b, w_down, ln2_g, ln2_b, loss_target, m_w_in, m_b_in, m_ln_a_g, m_ln_a_b, m_w_spatial, m_b_spatial, m_conv_b_w, m_conv_b_b, m_ln_b_g, m_ln_b_b, m_w_out, m_b_out, m_ln1_g, m_ln1_b, m_w_up, m_conv_f_w, m_conv_f_b, m_w_down, m_ln2_g, m_ln2_b, v_w_in, v_b_in, v_ln_a_g, v_ln_a_b, v_w_spatial, v_b_spatial, v_conv_b_w, v_conv_b_b, v_ln_b_g, v_ln_b_b, v_w_out, v_b_out, v_ln1_g, v_ln1_b, v_w_up, v_conv_f_w, v_conv_f_b, v_w_down, v_ln2_g, v_ln2_b):
    t = x.shape[1]
    x2 = x.reshape(t, D)
    target = loss_target.reshape(t, D)
    tm_fwd = min(t, 512)
    tm_bwd = min(t, 256)
    tm_ffn_bwd = min(t, 512)

    xi, yi, ci = _mesh_pos()
    jidx = jnp.stack([_lid(px, py, ci) for px, py in _chip_patterns(xi, yi)]).astype(jnp.int32)

    sin, sout, sup, sdown, conv_g = prepare_weights(w_in, w_out, w_up.T, w_down, _pack_conv(conv_b_w, conv_f_w))
    conv_b_all, cfw = _unpack_conv(conv_g)
    conv_b_full = conv_b_all.transpose(1, 0, 2).reshape(KB, D_B)
    cfb = conv_f_b.reshape(N_DEV, W_UP_BLK)
    row = lambda a: a.reshape(1, -1)
    bst = b_spatial.T

    h, xhat1, rstd1, yb1, win_g, wout_g, wup_g, wdown_g = mix_forward(
        x2, sin, sout, row(b_in), ln_a_g, ln_a_b, w_spatial, bst, conv_b_full, row(conv_b_b),
        row(ln_b_g), row(ln_b_b), row(b_out), row(ln1_g), row(ln1_b), sup, sdown, tm_fwd)
    wout_full = wout_g.reshape(D, D)
    wdown4 = wdown_g.reshape(N_F, W_UP_BLK, D)
    hu, gv, dr2, loss_part, s_ln2 = ffn_forward(
        xhat1, row(ln1_g), row(ln1_b), wup_g, cfw, cfb, wdown4, row(ln2_g), row(ln2_b), target, tm_bwd)

    order = jnp.where(ci == 0, jnp.array([1, 3, 0, 2], jnp.int32), jnp.array([0, 2, 1, 3], jnp.int32))
    dwup, dwdown, dcfw, dcfb, dx1p, *ffn_lands = ffn_backward(
        order, dr2, xhat1, row(ln1_g), row(ln1_b), hu, gv, wup_g, cfw, wdown4, tm_ffn_bwd)
    ffn_grads = [dwup.reshape(N_DEV, W_UP_BLK, D), dwdown.reshape(N_DEV, D_FF // N_DEV, D)]
    ffn_partials = chip_partials(ffn_grads, ffn_lands, jidx, 2)
    grad_x, dwin, dwout, dcw, s_mix, *ffn_recvs = mix_backward(
        x2, h, yb1, dx1p, dr2, xhat1, rstd1, win_g, ln_a_g, ln_a_b, w_spatial, bst,
        conv_b_full, row(ln_b_g), row(ln_b_b), wout_full, row(ln1_g), ffn_partials, tm_bwd)

    dcfb_rows = jnp.pad(dcfb.reshape(-1, 128), ((0, 4), (0, 0)))
    dconv = _pack_conv(dcw.reshape(KB, N_DEV, 64).transpose(1, 0, 2), dcfw.reshape(N_DEV, KF, W_UP_BLK))
    mix_grads = [dwin, dwout.reshape(N_DEV, D // N_DEV, D), dconv]
    mix_w = [w_in, w_out, _pack_conv(conv_b_w, conv_f_w)]
    mix_m = [m_w_in, m_w_out, _pack_conv(m_conv_b_w, m_conv_f_w)]
    mix_v = [v_w_in, v_w_out, _pack_conv(v_conv_b_w, v_conv_f_w)]
    *mix_sums, sv_slots = mixer_reduce(mix_grads, [s_mix, dcfb_rows, s_ln2, loss_part])
    big = {}

    ffn_out = reduce_and_adamw(ffn_grads, ffn_lands, ffn_recvs, [w_up.T, w_down], [m_w_up.T, m_w_down],
                               [v_w_up.T, v_w_down], jidx, 2)
    big["w_up"] = [ffn_out[2 * k].T for k in range(4)]
    big["w_down"] = [ffn_out[2 * k + 1] for k in range(4)]

    small_w = dict(b_in=b_in, ln_a_g=ln_a_g, ln_a_b=ln_a_b, w_spatial=w_spatial, b_spatial=b_spatial,
                   conv_b_b=conv_b_b, ln_b_g=ln_b_g, ln_b_b=ln_b_b, b_out=b_out, ln1_g=ln1_g,
                   ln1_b=ln1_b, conv_f_b=conv_f_b, ln2_g=ln2_g, ln2_b=ln2_b)
    small_m = dict(b_in=m_b_in, ln_a_g=m_ln_a_g, ln_a_b=m_ln_a_b, w_spatial=m_w_spatial,
                   b_spatial=m_b_spatial, conv_b_b=m_conv_b_b, ln_b_g=m_ln_b_g, ln_b_b=m_ln_b_b,
                   b_out=m_b_out, ln1_g=m_ln1_g, ln1_b=m_ln1_b, conv_f_b=m_conv_f_b, ln2_g=m_ln2_g,
                   ln2_b=m_ln2_b)
    small_v = dict(b_in=v_b_in, ln_a_g=v_ln_a_g, ln_a_b=v_ln_a_b, w_spatial=v_w_spatial,
                   b_spatial=v_b_spatial, conv_b_b=v_conv_b_b, ln_b_g=v_ln_b_g, ln_b_b=v_ln_b_b,
                   b_out=v_b_out, ln1_g=v_ln1_g, ln1_b=v_ln1_b, conv_f_b=v_conv_f_b, ln2_g=v_ln2_g,
                   ln2_b=v_ln2_b)
    order = [nm for nm, _, _ in SMALL_LAYOUT]
    small_out = small_adamw(sv_slots, [_rows128(small_w[nm]) for nm in order],
                            [_rows128(small_m[nm]) for nm in order], [_rows128(small_v[nm]) for nm in order],
                            mix_sums, mix_w, mix_m, mix_v)
    n_small = len(order)
    mix_out = small_out[4 * n_small + 1:]
    big.update({nm: [mix_out[k * 3 + p] for k in range(4)] for p, nm in enumerate(["w_in", "w_out", "conv"])})
    for k in range(4):
        cb_k, cf_k = _unpack_conv(big["conv"][k])
        big.setdefault("conv_b_w", []).append(cb_k)
        big.setdefault("conv_f_w", []).append(cf_k)
    small = {nm: [small_out[k * n_small + p].reshape(small_w[nm].shape) for k in range(4)]
             for p, nm in enumerate(order)}
    loss = jnp.sum(small_out[4 * n_small]) * (0.5 / D)

    weights = ["w_in", "b_in", "ln_a_g", "ln_a_b", "w_spatial", "b_spatial", "conv_b_w", "conv_b_b",
               "ln_b_g", "ln_b_b", "w_out", "b_out", "ln1_g", "ln1_b", "w_up", "conv_f_w", "conv_f_b",
               "w_down", "ln2_g", "ln2_b"]
    result = lambda nm, k: big[nm][k] if nm in big else small[nm][k]
    return (loss, grad_x.reshape(x.shape), *[result(nm, 0) for nm in weights],
            *[result(nm, 1) for nm in weights], *[result(nm, 2) for nm in weights],
            *[result(nm, 3) for nm in weights])
```
